```python
import jax, jax.numpy as jnp
from jax import lax
import numpy as np

D_MODEL = 2048
BATCH = 8
SEQ = 2048
DEPTH = 2

MEM_LEN = 256
HEAD_DIM = 128
CHUNK = 128
GMLP_WIDTH = D_MODEL // 2
POOL_WIDTH = D_MODEL // 4
CONV_WIDTH = D_MODEL - GMLP_WIDTH - POOL_WIDTH
MIX_WIDTH = GMLP_WIDTH + POOL_WIDTH + CONV_WIDTH
GMLP_HEADS = GMLP_WIDTH // HEAD_DIM
POOL_WINDOWS = (2, 4, 8, 16)
POOL_GROUPS = len(POOL_WINDOWS)
POOL_GROUP_WIDTH = POOL_WIDTH // POOL_GROUPS
MAX_WINDOW = max(POOL_WINDOWS)
CONV_K = 31
IN_COLS = 2 * GMLP_WIDTH + POOL_WIDTH + 2 * CONV_WIDTH
XATTN_HEADS = 4
XATTN_HEAD_DIM = D_MODEL // XATTN_HEADS
D_FF = 4 * D_MODEL
RMS_EPS = 1e-6
LN_EPS = 1e-5

kernel_name = "hybrid_gmlp_pool_conformer_block"


def rms_norm(x, g):
    xf = x.astype(jnp.float32)
    y = xf * lax.rsqrt(jnp.mean(xf * xf, axis=-1, keepdims=True) + RMS_EPS)
    return (y * g.astype(jnp.float32)).astype(x.dtype)


def layer_norm(x, g, b=None):
    xf = x.astype(jnp.float32)
    mu = jnp.mean(xf, axis=-1, keepdims=True)
    var = jnp.mean(jnp.square(xf - mu), axis=-1, keepdims=True)
    y = (xf - mu) * lax.rsqrt(var + LN_EPS) * g.astype(jnp.float32)
    if b is not None:
        y = y + b.astype(jnp.float32)
    return y.astype(x.dtype)


def spatial_gating(u, v, g_v, w_s, b_s):
    B, S, _ = u.shape
    u = u.reshape(B, S, GMLP_HEADS, HEAD_DIM)
    v = layer_norm(v.reshape(B, S, GMLP_HEADS, HEAD_DIM), g_v)
    mask = jnp.tril(jnp.ones((CHUNK, CHUNK), w_s.dtype))
    vc = v.reshape(B, S // CHUNK, CHUNK, GMLP_HEADS, HEAD_DIM)
    mixed = jnp.einsum('hts,bcshd->bcthd', w_s * mask, vc) + b_s.T[None, None, :, :, None]
    return (u * mixed.reshape(B, S, GMLP_HEADS, HEAD_DIM)).reshape(B, S, GMLP_WIDTH)


def multiscale_pool(p, w_pool, s_pool):
    B, S, _ = p.shape
    pf = p.astype(jnp.float32).reshape(B, S, POOL_GROUPS, POOL_GROUP_WIDTH)
    cs = jnp.cumsum(pf, axis=1)
    cs = jnp.pad(cs, ((0, 0), (MAX_WINDOW, 0), (0, 0), (0, 0)))
    pos = jnp.arange(S, dtype=jnp.float32)
    means = []
    for g, w in enumerate(POOL_WINDOWS):
        win = cs[:, MAX_WINDOW:, g] - cs[:, MAX_WINDOW - w:MAX_WINDOW - w + S, g]
        cnt = jnp.minimum(pos + 1.0, float(w))[None, :, None]
        means.append(win / cnt)
    pooled = jnp.stack(means, axis=2) - pf
    out = jnp.einsum('bsgc,gcd->bsgd', pooled, w_pool.astype(jnp.float32))
    out = out * s_pool.astype(jnp.float32).reshape(POOL_GROUPS, POOL_GROUP_WIDTH)
    return out.reshape(B, S, POOL_WIDTH).astype(p.dtype)


def conformer_conv(c_val, c_gate, w_dw, b_dw, ln_g, ln_b):
    h = c_val * jax.nn.sigmoid(c_gate)
    h = lax.conv_general_dilated(
        h, w_dw[:, None, :], window_strides=(1,), padding=[(CONV_K - 1, 0)],
        dimension_numbers=('NWC', 'WIO', 'NWC'), feature_group_count=CONV_WIDTH) + b_dw
    return jax.nn.silu(layer_norm(h, ln_g, ln_b))


def hybrid_mixer(h, w_in, w_out, g_v, w_s, b_s, w_pool, s_pool, w_dw, b_dw, ln_g, ln_b):
    z = h @ w_in
    cuts = (GMLP_WIDTH, 2 * GMLP_WIDTH, 2 * GMLP_WIDTH + POOL_WIDTH,
            2 * GMLP_WIDTH + POOL_WIDTH + CONV_WIDTH)
    z_a, p_b, c_val, c_gate = (z[..., :cuts[1]], z[..., cuts[1]:cuts[2]],
                               z[..., cuts[2]:cuts[3]], z[..., cuts[3]:])
    z_a = jax.nn.gelu(z_a)
    y_a = spatial_gating(z_a[..., :GMLP_WIDTH], z_a[..., GMLP_WIDTH:], g_v, w_s, b_s)
    y_b = multiscale_pool(p_b, w_pool, s_pool)
    y_c = conformer_conv(c_val, c_gate, w_dw, b_dw, ln_g, ln_b)
    return jnp.concatenate([y_a, y_b, y_c], axis=-1) @ w_out


def cross_attention(h, m, w_q, w_k, w_v, w_o):
    B, S, _ = h.shape
    q = (h @ w_q).reshape(B, S, XATTN_HEADS, XATTN_HEAD_DIM)
    k = (m @ w_k).reshape(B, MEM_LEN, XATTN_HEADS, XATTN_HEAD_DIM)
    v = (m @ w_v).reshape(B, MEM_LEN, XATTN_HEADS, XATTN_HEAD_DIM)
    scores = jnp.einsum('bshd,bmhd->bhsm', q.astype(jnp.float32), k.astype(jnp.float32))
    probs = jax.nn.softmax(scores * (XATTN_HEAD_DIM ** -0.5), axis=-1).astype(v.dtype)
    out = jnp.einsum('bhsm,bmhd->bshd', probs, v).reshape(B, S, D_MODEL)
    return out @ w_o


def _fwd_setup_inputs(seed: int = 0) -> dict:
    key = jax.random.key(seed)
    ks = iter(jax.random.split(key, 32))
    f32 = jnp.float32

    def nrm(shape, scale):
        return jax.random.normal(next(ks), shape, f32) * scale

    def gain(shape):
        return 1.0 + 0.05 * jax.random.normal(next(ks), shape, f32)

    L = DEPTH
    return {
        "x": jax.random.normal(next(ks), (BATCH, SEQ, D_MODEL), f32),
        "mem": jax.random.normal(next(ks), (BATCH, MEM_LEN, D_MODEL), f32),
        "norm_mix_pre": gain((L, D_MODEL)),
        "norm_mix_post": gain((L, D_MODEL)),
        "w_in": nrm((L, D_MODEL, IN_COLS), D_MODEL ** -0.5),
        "w_out": nrm((L, MIX_WIDTH, D_MODEL), MIX_WIDTH ** -0.5),
        "gmlp_v_gain": gain((L, GMLP_HEADS, HEAD_DIM)),
        "w_spatial": nrm((L, GMLP_HEADS, CHUNK, CHUNK), CHUNK ** -0.5),
        "b_spatial": gain((L, GMLP_HEADS, CHUNK)),
        "w_pool": nrm((L, POOL_GROUPS, POOL_GROUP_WIDTH, POOL_GROUP_WIDTH), POOL_GROUP_WIDTH ** -0.5),
        "s_pool": gain((L, POOL_WIDTH)),
        "w_dw": nrm((L, CONV_K, CONV_WIDTH), CONV_K ** -0.5),
        "b_dw": nrm((L, CONV_WIDTH), 0.02),
        "conv_ln_g": gain((L, CONV_WIDTH)),
        "conv_ln_b": nrm((L, CONV_WIDTH), 0.02),
        "norm_xattn_pre": gain((L, D_MODEL)),
        "norm_mem": gain((L, D_MODEL)),
        "norm_xattn_post": gain((L, D_MODEL)),
        "w_q": nrm((L, D_MODEL, D_MODEL), D_MODEL ** -0.5),
        "w_k": nrm((L, D_MODEL, D_MODEL), D_MODEL ** -0.5),
        "w_v": nrm((L, D_MODEL, D_MODEL), D_MODEL ** -0.5),
        "w_o": nrm((L, D_MODEL, D_MODEL), D_MODEL ** -0.5),
        "norm_ffn_pre": gain((L, D_MODEL)),
        "norm_ffn_post": gain((L, D_MODEL)),
        "w_up": nrm((L, D_MODEL, D_FF), D_MODEL ** -0.5),
        "w_down": nrm((L, D_FF, D_MODEL), D_FF ** -0.5),
    }


def _fwd_reference(x, mem, norm_mix_pre, norm_mix_post, w_in, w_out, gmlp_v_gain, w_spatial,
              b_spatial, w_pool, s_pool, w_dw, b_dw, conv_ln_g, conv_ln_b, norm_xattn_pre,
              norm_mem, norm_xattn_post, w_q, w_k, w_v, w_o, norm_ffn_pre, norm_ffn_post,
              w_up, w_down):
    for l in range(DEPTH):
        h = rms_norm(x, norm_mix_pre[l])
        h = hybrid_mixer(h, w_in[l], w_out[l], gmlp_v_gain[l], w_spatial[l], b_spatial[l],
                         w_pool[l], s_pool[l], w_dw[l], b_dw[l], conv_ln_g[l], conv_ln_b[l])
        x = x + rms_norm(h, norm_mix_post[l])
        h = rms_norm(x, norm_xattn_pre[l])
        m = rms_norm(mem, norm_mem[l])
        h = cross_attention(h, m, w_q[l], w_k[l], w_v[l], w_o[l])
        x = x + rms_norm(h, norm_xattn_post[l])
        h = rms_norm(x, norm_ffn_pre[l])
        h = jnp.square(jax.nn.relu(h @ w_up[l])) @ w_down[l]
        x = x + rms_norm(h, norm_ffn_post[l])
    return x


import jax as _jax
import jax.numpy as _jnp

TWIN_FORMAT = 'train_step'
FWD_PARAMS = ['x', 'mem', 'norm_mix_pre', 'norm_mix_post', 'w_in', 'w_out', 'gmlp_v_gain', 'w_spatial', 'b_spatial', 'w_pool', 's_pool', 'w_dw', 'b_dw', 'conv_ln_g', 'conv_ln_b', 'norm_xattn_pre', 'norm_mem', 'norm_xattn_post', 'w_q', 'w_k', 'w_v', 'w_o', 'norm_ffn_pre', 'norm_ffn_post', 'w_up', 'w_down']
TWIN_WEIGHTS = ['norm_mix_pre', 'norm_mix_post', 'w_in', 'w_out', 'gmlp_v_gain', 'w_spatial', 'b_spatial', 'w_pool', 's_pool', 'w_dw', 'b_dw', 'conv_ln_g', 'conv_ln_b', 'norm_xattn_pre', 'norm_mem', 'norm_xattn_post', 'w_q', 'w_k', 'w_v', 'w_o', 'norm_ffn_pre', 'norm_ffn_post', 'w_up', 'w_down']
TWIN_DIFF_INPUT = 'x'
TWIN_INPUTS = ['x', 'mem', 'norm_mix_pre', 'norm_mix_post', 'w_in', 'w_out', 'gmlp_v_gain', 'w_spatial', 'b_spatial', 'w_pool', 's_pool', 'w_dw', 'b_dw', 'conv_ln_g', 'conv_ln_b', 'norm_xattn_pre', 'norm_mem', 'norm_xattn_post', 'w_q', 'w_k', 'w_v', 'w_o', 'norm_ffn_pre', 'norm_ffn_post', 'w_up', 'w_down', 'loss_target', 'm_norm_mix_pre', 'm_norm_mix_post', 'm_w_in', 'm_w_out', 'm_gmlp_v_gain', 'm_w_spatial', 'm_b_spatial', 'm_w_pool', 'm_s_pool', 'm_w_dw', 'm_b_dw', 'm_conv_ln_g', 'm_conv_ln_b', 'm_norm_xattn_pre', 'm_norm_mem', 'm_norm_xattn_post', 'm_w_q', 'm_w_k', 'm_w_v', 'm_w_o', 'm_norm_ffn_pre', 'm_norm_ffn_post', 'm_w_up', 'm_w_down', 'v_norm_mix_pre', 'v_norm_mix_post', 'v_w_in', 'v_w_out', 'v_gmlp_v_gain', 'v_w_spatial', 'v_b_spatial', 'v_w_pool', 'v_s_pool', 'v_w_dw', 'v_b_dw', 'v_conv_ln_g', 'v_conv_ln_b', 'v_norm_xattn_pre', 'v_norm_mem', 'v_norm_xattn_post', 'v_w_q', 'v_w_k', 'v_w_v', 'v_w_o', 'v_norm_ffn_pre', 'v_norm_ffn_post', 'v_w_up', 'v_w_down']
TWIN_OUTPUTS = ['loss', 'grad_x', 'grad_norm_mix_pre', 'grad_norm_mix_post', 'grad_w_in', 'grad_w_out', 'grad_gmlp_v_gain', 'grad_w_spatial', 'grad_b_spatial', 'grad_w_pool', 'grad_s_pool', 'grad_w_dw', 'grad_b_dw', 'grad_conv_ln_g', 'grad_conv_ln_b', 'grad_norm_xattn_pre', 'grad_norm_mem', 'grad_norm_xattn_post', 'grad_w_q', 'grad_w_k', 'grad_w_v', 'grad_w_o', 'grad_norm_ffn_pre', 'grad_norm_ffn_post', 'grad_w_up', 'grad_w_down', 'delta_norm_mix_pre', 'delta_norm_mix_post', 'delta_w_in', 'delta_w_out', 'delta_gmlp_v_gain', 'delta_w_spatial', 'delta_b_spatial', 'delta_w_pool', 'delta_s_pool', 'delta_w_dw', 'delta_b_dw', 'delta_conv_ln_g', 'delta_conv_ln_b', 'delta_norm_xattn_pre', 'delta_norm_mem', 'delta_norm_xattn_post', 'delta_w_q', 'delta_w_k', 'delta_w_v', 'delta_w_o', 'delta_norm_ffn_pre', 'delta_norm_ffn_post', 'delta_w_up', 'delta_w_down', 'new_m_norm_mix_pre', 'new_m_norm_mix_post', 'new_m_w_in', 'new_m_w_out', 'new_m_gmlp_v_gain', 'new_m_w_spatial', 'new_m_b_spatial', 'new_m_w_pool', 'new_m_s_pool', 'new_m_w_dw', 'new_m_b_dw', 'new_m_conv_ln_g', 'new_m_conv_ln_b', 'new_m_norm_xattn_pre', 'new_m_norm_mem', 'new_m_norm_xattn_post', 'new_m_w_q', 'new_m_w_k', 'new_m_w_v', 'new_m_w_o', 'new_m_norm_ffn_pre', 'new_m_norm_ffn_post', 'new_m_w_up', 'new_m_w_down', 'new_v_norm_mix_pre', 'new_v_norm_mix_post', 'new_v_w_in', 'new_v_w_out', 'new_v_gmlp_v_gain', 'new_v_w_spatial', 'new_v_b_spatial', 'new_v_w_pool', 'new_v_s_pool', 'new_v_w_dw', 'new_v_b_dw', 'new_v_conv_ln_g', 'new_v_conv_ln_b', 'new_v_norm_xattn_pre', 'new_v_norm_mem', 'new_v_norm_xattn_post', 'new_v_w_q', 'new_v_w_k', 'new_v_w_v', 'new_v_w_o', 'new_v_norm_ffn_pre', 'new_v_norm_ffn_post', 'new_v_w_up', 'new_v_w_down']
TWIN_LEAF_KINDS = {'loss': 'loss', 'grad_x': 'grad_x', 'grad_norm_mix_pre': 'grad_w', 'grad_norm_mix_post': 'grad_w', 'grad_w_in': 'grad_w', 'grad_w_out': 'grad_w', 'grad_gmlp_v_gain': 'grad_w', 'grad_w_spatial': 'grad_w', 'grad_b_spatial': 'grad_w', 'grad_w_pool': 'grad_w', 'grad_s_pool': 'grad_w', 'grad_w_dw': 'grad_w', 'grad_b_dw': 'grad_w', 'grad_conv_ln_g': 'grad_w', 'grad_conv_ln_b': 'grad_w', 'grad_norm_xattn_pre': 'grad_w', 'grad_norm_mem': 'grad_w', 'grad_norm_xattn_post': 'grad_w', 'grad_w_q': 'grad_w', 'grad_w_k': 'grad_w', 'grad_w_v': 'grad_w', 'grad_w_o': 'grad_w', 'grad_norm_ffn_pre': 'grad_w', 'grad_norm_ffn_post': 'grad_w', 'grad_w_up': 'grad_w', 'grad_w_down': 'grad_w', 'delta_norm_mix_pre': 'delta_w', 'delta_norm_mix_post': 'delta_w', 'delta_w_in': 'delta_w', 'delta_w_out': 'delta_w', 'delta_gmlp_v_gain': 'delta_w', 'delta_w_spatial': 'delta_w', 'delta_b_spatial': 'delta_w', 'delta_w_pool': 'delta_w', 'delta_s_pool': 'delta_w', 'delta_w_dw': 'delta_w', 'delta_b_dw': 'delta_w', 'delta_conv_ln_g': 'delta_w', 'delta_conv_ln_b': 'delta_w', 'delta_norm_xattn_pre': 'delta_w', 'delta_norm_mem': 'delta_w', 'delta_norm_xattn_post': 'delta_w', 'delta_w_q': 'delta_w', 'delta_w_k': 'delta_w', 'delta_w_v': 'delta_w', 'delta_w_o': 'delta_w', 'delta_norm_ffn_pre': 'delta_w', 'delta_norm_ffn_post': 'delta_w', 'delta_w_up': 'delta_w', 'delta_w_down': 'delta_w', 'new_m_norm_mix_pre': 'new_m', 'new_m_norm_mix_post': 'new_m', 'new_m_w_in': 'new_m', 'new_m_w_out': 'new_m', 'new_m_gmlp_v_gain': 'new_m', 'new_m_w_spatial': 'new_m', 'new_m_b_spatial': 'new_m', 'new_m_w_pool': 'new_m', 'new_m_s_pool': 'new_m', 'new_m_w_dw': 'new_m', 'new_m_b_dw': 'new_m', 'new_m_conv_ln_g': 'new_m', 'new_m_conv_ln_b': 'new_m', 'new_m_norm_xattn_pre': 'new_m', 'new_m_norm_mem': 'new_m', 'new_m_norm_xattn_post': 'new_m', 'new_m_w_q': 'new_m', 'new_m_w_k': 'new_m', 'new_m_w_v': 'new_m', 'new_m_w_o': 'new_m', 'new_m_norm_ffn_pre': 'new_m', 'new_m_norm_ffn_post': 'new_m', 'new_m_w_up': 'new_m', 'new_m_w_down': 'new_m', 'new_v_norm_mix_pre': 'new_v', 'new_v_norm_mix_post': 'new_v', 'new_v_w_in': 'new_v', 'new_v_w_out': 'new_v', 'new_v_gmlp_v_gain': 'new_v', 'new_v_w_spatial': 'new_v', 'new_v_b_spatial': 'new_v', 'new_v_w_pool': 'new_v', 'new_v_s_pool': 'new_v', 'new_v_w_dw': 'new_v', 'new_v_b_dw': 'new_v', 'new_v_conv_ln_g': 'new_v', 'new_v_conv_ln_b': 'new_v', 'new_v_norm_xattn_pre': 'new_v', 'new_v_norm_mem': 'new_v', 'new_v_norm_xattn_post': 'new_v', 'new_v_w_q': 'new_v', 'new_v_w_k': 'new_v', 'new_v_w_v': 'new_v', 'new_v_w_o': 'new_v', 'new_v_norm_ffn_pre': 'new_v', 'new_v_norm_ffn_post': 'new_v', 'new_v_w_up': 'new_v', 'new_v_w_down': 'new_v'}


def _forward(args):
    return _fwd_reference(*[args[k] for k in FWD_PARAMS])


def _output_shape():
    out = _jax.eval_shape(lambda: _forward(_fwd_setup_inputs(0)))
    return out.shape, out.dtype

N_MICROBATCH = 1
ADAM_LR = 0.001
ADAM_B1 = 0.9
ADAM_B2 = 0.999
ADAM_EPS = 1e-08
ADAM_WD = 0.01
ADAM_STEP = 10
PER_EXAMPLE_BATCH_AXIS = {'x': 0, 'mem': 0, 'loss_target': 0}
SHARED_INPUTS = []
_WEIGHT_DTYPES = {'norm_mix_pre': _jnp.float32, 'norm_mix_post': _jnp.float32, 'w_in': _jnp.float32, 'w_out': _jnp.float32, 'gmlp_v_gain': _jnp.float32, 'w_spatial': _jnp.float32, 'b_spatial': _jnp.float32, 'w_pool': _jnp.float32, 's_pool': _jnp.float32, 'w_dw': _jnp.float32, 'b_dw': _jnp.float32, 'conv_ln_g': _jnp.float32, 'conv_ln_b': _jnp.float32, 'norm_xattn_pre': _jnp.float32, 'norm_mem': _jnp.float32, 'norm_xattn_post': _jnp.float32, 'w_q': _jnp.float32, 'w_k': _jnp.float32, 'w_v': _jnp.float32, 'w_o': _jnp.float32, 'norm_ffn_pre': _jnp.float32, 'norm_ffn_post': _jnp.float32, 'w_up': _jnp.float32, 'w_down': _jnp.float32}
MOMENT_SCALE = {'norm_mix_pre': 1.062830e+00, 'norm_mix_post': 8.516674e+00, 'w_in': 7.652971e-01, 'w_out': 2.498980e+00, 'gmlp_v_gain': 1.783112e-01, 'w_spatial': 1.631704e-01, 'b_spatial': 3.060885e-01, 'w_pool': 4.439344e-01, 's_pool': 5.651698e-01, 'w_dw': 1.324117e+00, 'b_dw': 9.018982e+00, 'conv_ln_g': 3.812552e+00, 'conv_ln_b': 5.239860e+00, 'norm_xattn_pre': 9.577755e-01, 'norm_mem': 3.354517e+00, 'norm_xattn_post': 8.962357e+00, 'w_q': 9.923443e-01, 'w_k': 1.015649e+00, 'w_v': 3.150406e+00, 'w_o': 3.231226e+00, 'norm_ffn_pre': 1.719175e+00, 'norm_ffn_post': 8.881772e+00, 'w_up': 8.435689e-01, 'w_down': 3.221354e+00}


def _to_microbatches(a, axis):
    t = _jnp.moveaxis(a, axis, 0)
    t = t.reshape((N_MICROBATCH, t.shape[0] // N_MICROBATCH) + t.shape[1:])
    return _jnp.moveaxis(t, 1, axis + 1)


def setup_inputs(seed: int = 0) -> dict:
    inp = _fwd_setup_inputs(seed)
    key = _jax.random.fold_in(_jax.random.key(seed), 7919)
    shape, _ = _output_shape()
    out = dict(inp)
    out["loss_target"] = _jax.random.normal(_jax.random.fold_in(key, 0), shape, _jnp.float32)
    for i, name in enumerate(TWIN_WEIGHTS):
        w = inp[name].astype(_jnp.float32)
        if MOMENT_SCALE is None:
            s = _jnp.sqrt(_jnp.mean(_jnp.square(w)) + 1e-30)
        else:
            s = MOMENT_SCALE[name]
        km, kv = _jax.random.split(_jax.random.fold_in(key, i + 1))
        out[name] = w
        out["m_" + name] = s * _jax.random.normal(km, w.shape, _jnp.float32)
        out["v_" + name] = (s * s) * _jax.random.uniform(kv, w.shape, _jnp.float32, 0.5, 1.5)
    if N_MICROBATCH > 1:
        for name, axis in PER_EXAMPLE_BATCH_AXIS.items():
            out[name] = _to_microbatches(out[name], axis)
    return {'x': out['x'], 'mem': out['mem'], 'norm_mix_pre': out['norm_mix_pre'], 'norm_mix_post': out['norm_mix_post'], 'w_in': out['w_in'], 'w_out': out['w_out'], 'gmlp_v_gain': out['gmlp_v_gain'], 'w_spatial': out['w_spatial'], 'b_spatial': out['b_spatial'], 'w_pool': out['w_pool'], 's_pool': out['s_pool'], 'w_dw': out['w_dw'], 'b_dw': out['b_dw'], 'conv_ln_g': out['conv_ln_g'], 'conv_ln_b': out['conv_ln_b'], 'norm_xattn_pre': out['norm_xattn_pre'], 'norm_mem': out['norm_mem'], 'norm_xattn_post': out['norm_xattn_post'], 'w_q': out['w_q'], 'w_k': out['w_k'], 'w_v': out['w_v'], 'w_o': out['w_o'], 'norm_ffn_pre': out['norm_ffn_pre'], 'norm_ffn_post': out['norm_ffn_post'], 'w_up': out['w_up'], 'w_down': out['w_down'], 'loss_target': out['loss_target'], 'm_norm_mix_pre': out['m_norm_mix_pre'], 'm_norm_mix_post': out['m_norm_mix_post'], 'm_w_in': out['m_w_in'], 'm_w_out': out['m_w_out'], 'm_gmlp_v_gain': out['m_gmlp_v_gain'], 'm_w_spatial': out['m_w_spatial'], 'm_b_spatial': out['m_b_spatial'], 'm_w_pool': out['m_w_pool'], 'm_s_pool': out['m_s_pool'], 'm_w_dw': out['m_w_dw'], 'm_b_dw': out['m_b_dw'], 'm_conv_ln_g': out['m_conv_ln_g'], 'm_conv_ln_b': out['m_conv_ln_b'], 'm_norm_xattn_pre': out['m_norm_xattn_pre'], 'm_norm_mem': out['m_norm_mem'], 'm_norm_xattn_post': out['m_norm_xattn_post'], 'm_w_q': out['m_w_q'], 'm_w_k': out['m_w_k'], 'm_w_v': out['m_w_v'], 'm_w_o': out['m_w_o'], 'm_norm_ffn_pre': out['m_norm_ffn_pre'], 'm_norm_ffn_post': out['m_norm_ffn_post'], 'm_w_up': out['m_w_up'], 'm_w_down': out['m_w_down'], 'v_norm_mix_pre': out['v_norm_mix_pre'], 'v_norm_mix_post': out['v_norm_mix_post'], 'v_w_in': out['v_w_in'], 'v_w_out': out['v_w_out'], 'v_gmlp_v_gain': out['v_gmlp_v_gain'], 'v_w_spatial': out['v_w_spatial'], 'v_b_spatial': out['v_b_spatial'], 'v_w_pool': out['v_w_pool'], 'v_s_pool': out['v_s_pool'], 'v_w_dw': out['v_w_dw'], 'v_b_dw': out['v_b_dw'], 'v_conv_ln_g': out['v_conv_ln_g'], 'v_conv_ln_b': out['v_conv_ln_b'], 'v_norm_xattn_pre': out['v_norm_xattn_pre'], 'v_norm_mem': out['v_norm_mem'], 'v_norm_xattn_post': out['v_norm_xattn_post'], 'v_w_q': out['v_w_q'], 'v_w_k': out['v_w_k'], 'v_w_v': out['v_w_v'], 'v_w_o': out['v_w_o'], 'v_norm_ffn_pre': out['v_norm_ffn_pre'], 'v_norm_ffn_post': out['v_norm_ffn_post'], 'v_w_up': out['v_w_up'], 'v_w_down': out['v_w_down']}


def _loss(weights, diff, rest, loss_target):
    with _jax.named_scope("forward"):
        args = {**rest, TWIN_DIFF_INPUT: diff, **{k: w.astype(_WEIGHT_DTYPES[k]) for k, w in weights.items()}}
        y = _forward(args)
    with _jax.named_scope("loss_head"):
        err = _jnp.square(y.astype(_jnp.float32) - loss_target)
        return 0.5 * _jnp.sum(_jnp.mean(err, axis=-1)) if err.ndim else 0.5 * err


def _adamw(w, g, m, v):
    m = ADAM_B1 * m + (1.0 - ADAM_B1) * g
    v = ADAM_B2 * v + (1.0 - ADAM_B2) * _jnp.square(g)
    m_hat = m / (1.0 - ADAM_B1 ** ADAM_STEP)
    v_hat = v / (1.0 - ADAM_B2 ** ADAM_STEP)
    delta = -ADAM_LR * (m_hat / (_jnp.sqrt(v_hat) + ADAM_EPS) + ADAM_WD * w)
    return delta, m, v


def reference(x, mem, norm_mix_pre, norm_mix_post, w_in, w_out, gmlp_v_gain, w_spatial, b_spatial, w_pool, s_pool, w_dw, b_dw, conv_ln_g, conv_ln_b, norm_xattn_pre, norm_mem, norm_xattn_post, w_q, w_k, w_v, w_o, norm_ffn_pre, norm_ffn_post, w_up, w_down, loss_target, m_norm_mix_pre, m_norm_mix_post, m_w_in, m_w_out, m_gmlp_v_gain, m_w_spatial, m_b_spatial, m_w_pool, m_s_pool, m_w_dw, m_b_dw, m_conv_ln_g, m_conv_ln_b, m_norm_xattn_pre, m_norm_mem, m_norm_xattn_post, m_w_q, m_w_k, m_w_v, m_w_o, m_norm_ffn_pre, m_norm_ffn_post, m_w_up, m_w_down, v_norm_mix_pre, v_norm_mix_post, v_w_in, v_w_out, v_gmlp_v_gain, v_w_spatial, v_b_spatial, v_w_pool, v_s_pool, v_w_dw, v_b_dw, v_conv_ln_g, v_conv_ln_b, v_norm_xattn_pre, v_norm_mem, v_norm_xattn_post, v_w_q, v_w_k, v_w_v, v_w_o, v_norm_ffn_pre, v_norm_ffn_post, v_w_up, v_w_down):
    given = dict(x=x, mem=mem, norm_mix_pre=norm_mix_pre, norm_mix_post=norm_mix_post, w_in=w_in, w_out=w_out, gmlp_v_gain=gmlp_v_gain, w_spatial=w_spatial, b_spatial=b_spatial, w_pool=w_pool, s_pool=s_pool, w_dw=w_dw, b_dw=b_dw, conv_ln_g=conv_ln_g, conv_ln_b=conv_ln_b, norm_xattn_pre=norm_xattn_pre, norm_mem=norm_mem, norm_xattn_post=norm_xattn_post, w_q=w_q, w_k=w_k, w_v=w_v, w_o=w_o, norm_ffn_pre=norm_ffn_pre, norm_ffn_post=norm_ffn_post, w_up=w_up, w_down=w_down, loss_target=loss_target, m_norm_mix_pre=m_norm_mix_pre, m_norm_mix_post=m_norm_mix_post, m_w_in=m_w_in, m_w_out=m_w_out, m_gmlp_v_gain=m_gmlp_v_gain, m_w_spatial=m_w_spatial, m_b_spatial=m_b_spatial, m_w_pool=m_w_pool, m_s_pool=m_s_pool, m_w_dw=m_w_dw, m_b_dw=m_b_dw, m_conv_ln_g=m_conv_ln_g, m_conv_ln_b=m_conv_ln_b, m_norm_xattn_pre=m_norm_xattn_pre, m_norm_mem=m_norm_mem, m_norm_xattn_post=m_norm_xattn_post, m_w_q=m_w_q, m_w_k=m_w_k, m_w_v=m_w_v, m_w_o=m_w_o, m_norm_ffn_pre=m_norm_ffn_pre, m_norm_ffn_post=m_norm_ffn_post, m_w_up=m_w_up, m_w_down=m_w_down, v_norm_mix_pre=v_norm_mix_pre, v_norm_mix_post=v_norm_mix_post, v_w_in=v_w_in, v_w_out=v_w_out, v_gmlp_v_gain=v_gmlp_v_gain, v_w_spatial=v_w_spatial, v_b_spatial=v_b_spatial, v_w_pool=v_w_pool, v_s_pool=v_s_pool, v_w_dw=v_w_dw, v_b_dw=v_b_dw, v_conv_ln_g=v_conv_ln_g, v_conv_ln_b=v_conv_ln_b, v_norm_xattn_pre=v_norm_xattn_pre, v_norm_mem=v_norm_mem, v_norm_xattn_post=v_norm_xattn_post, v_w_q=v_w_q, v_w_k=v_w_k, v_w_v=v_w_v, v_w_o=v_w_o, v_norm_ffn_pre=v_norm_ffn_pre, v_norm_ffn_post=v_norm_ffn_post, v_w_up=v_w_up, v_w_down=v_w_down)
    weights = {n: given[n] for n in TWIN_WEIGHTS}
    shared = {n: given[n] for n in SHARED_INPUTS}
    per_example = {n: given[n] for n in ['x', 'mem']}
    grad_fn = _jax.value_and_grad(_loss, argnums=(0, 1))

    def one_microbatch(ex, loss_target):
        ex = dict(ex)
        diff = ex.pop(TWIN_DIFF_INPUT)
        return grad_fn(weights, diff, {**shared, **ex}, loss_target)

    if N_MICROBATCH == 1:
        loss, (grad_w, grad_x) = one_microbatch(per_example, given["loss_target"])
    else:
        def body(carry, xs):
            loss_sum, grad_sum = carry
            l_k, (gw_k, gx_k) = one_microbatch(xs[0], xs[1])
            with _jax.named_scope("update"):
                return (loss_sum + l_k, _jax.tree.map(_jnp.add, grad_sum, gw_k)), gx_k

        init = (_jnp.zeros((), _jnp.float32), _jax.tree.map(_jnp.zeros_like, weights))
        (loss, grad_w), grad_x = _jax.lax.scan(body, init, (per_example, given["loss_target"]))
    with _jax.named_scope("update"):
        delta_w, new_m, new_v = {}, {}, {}
        for n in TWIN_WEIGHTS:
            delta_w[n], new_m[n], new_v[n] = _adamw(weights[n], grad_w[n], given["m_" + n], given["v_" + n])
    return (loss, grad_x, *[grad_w[n] for n in TWIN_WEIGHTS], *[delta_w[n] for n in TWIN_WEIGHTS],
            *[new_m[n] for n in TWIN_WEIGHTS], *[new_v[n] for n in TWIN_WEIGHTS])
```

```python
import functools

import jax
import jax.numpy as jnp
from jax import lax
from jax.experimental import pallas as pl
from jax.experimental.pallas import tpu as pltpu

F32 = jnp.float32
BF16 = jnp.bfloat16

D = 2048
GW = 1024
PW = 512
CW = 512
HD = 128
NH = 8
NG = 4
POOL_WINDOWS = (2, 4, 8, 16)
CONV_K = 31
IN_COLS = 2 * GW + PW + 2 * CW
DFF = 4 * D
XH = 4
XHD = D // XH
ATT_SCALE = XHD ** -0.5
RMS_EPS = 1e-6
LN_EPS = 1e-5
DEPTH = 2
N_DEV = 8

ADAM_LR = 0.001
ADAM_B1 = 0.9
ADAM_B2 = 0.999
ADAM_EPS = 1e-08
ADAM_WD = 0.01
ADAM_STEP = 10

LANES = 128
CONV_HALO = 32
POOL_HALO = 16
ROW_TILE = 128
VMEM_LIMIT = 60 * 1024 * 1024

MESH = pl.DeviceIdType.MESH
NT = (((1,), (1,)), ((), ()))
NN = (((1,), (0,)), ((), ()))
TN = (((0,), (0,)), ((), ()))

BIG = ("w_out", "w_q", "w_k", "w_v", "w_o", "w_up", "w_down", "w_in")
SMALL = ("norm_mix_pre", "norm_mix_post", "gmlp_v_gain", "w_spatial", "b_spatial", "w_pool", "s_pool",
         "w_dw", "b_dw", "conv_ln_g", "conv_ln_b", "norm_xattn_pre", "norm_mem", "norm_xattn_post",
         "norm_ffn_pre", "norm_ffn_post")
WEIGHTS = ("norm_mix_pre", "norm_mix_post", "w_in", "w_out", "gmlp_v_gain", "w_spatial", "b_spatial", "w_pool",
           "s_pool", "w_dw", "b_dw", "conv_ln_g", "conv_ln_b", "norm_xattn_pre", "norm_mem", "norm_xattn_post",
           "w_q", "w_k", "w_v", "w_o", "norm_ffn_pre", "norm_ffn_post", "w_up", "w_down")


def _cparams():
    return pltpu.CompilerParams(vmem_limit_bytes=VMEM_LIMIT)


def _dot(a, b, dims):
    return lax.dot_general(a, b, dims, preferred_element_type=F32)


def _rms(x, g):
    y = x * lax.rsqrt(jnp.mean(x * x, axis=-1, keepdims=True) + RMS_EPS)
    return y * g


def _gelu(x):
    cdf = 0.5 * (1.0 + jnp.tanh(0.7978845608028654 * (x + 0.044715 * (x * x * x))))
    return x * cdf


def _layer_norm(x, g, b=None):
    mu = jnp.mean(x, axis=-1, keepdims=True)
    xc = x - mu
    var = jnp.mean(xc * xc, axis=-1, keepdims=True)
    y = xc * lax.rsqrt(var + LN_EPS) * g
    return y if b is None else y + b


def _sigmoid(x):
    return 1.0 / (1.0 + jnp.exp(-x))


def _gmlp_rows(zu, zv, gv):
    return _gelu(zu), _layer_norm(_gelu(zv), gv)


def _glu(cv, cg):
    return cv * _sigmoid(cg)


def _ln_silu(h, g, b):
    y = _layer_norm(h, g, b)
    return y * _sigmoid(y)


def _rowop_mm(name, kind, rows, g, w, dims, out_dtype, tm, tn, u=None):
    s = rows[0].shape[0]
    n = w.shape[0] if dims == NT else w.shape[1]
    tm, tn = min(tm, s), min(tn, n)
    ni, nj = s // tm, n // tn
    bwd = kind == "rms_bwd"

    def body(*refs):
        refs = list(refs)
        row_refs = [refs.pop(0) for _ in rows]
        g_ref, w_ref = refs.pop(0), refs.pop(0)
        u_ref = refs.pop(0) if u is not None else None
        out_ref, a_ref = refs.pop(0), refs.pop(0)
        dg_ref = refs.pop(0) if bwd else None
        a_s = refs.pop(0)

        @pl.when(pl.program_id(1) == 0)
        def _():
            if bwd:
                _, vjp = jax.vjp(_rms, row_refs[0][...], g_ref[...])
                a, dg = vjp(row_refs[1][...])
                dg_ref[0] = dg
            else:
                a = _rms(row_refs[0][...], g_ref[...])
            a_s[...] = a.astype(BF16)
            a_ref[...] = a_s[...]

        acc = _dot(a_s[...], w_ref[...], dims)
        if u_ref is not None:
            acc = acc * (2.0 * jnp.maximum(u_ref[...], 0.0))
        out_ref[...] = acc.astype(out_dtype)

    row_spec = pl.BlockSpec((tm, D), lambda i, j: (i, 0))
    w_spec = pl.BlockSpec((tn, D), lambda i, j: (j, 0)) if dims == NT else pl.BlockSpec((D, tn), lambda i, j: (0, j))
    in_specs = [row_spec] * len(rows) + [pl.BlockSpec((1, D), lambda i, j: (0, 0)), w_spec]
    args = list(rows) + [g, w]
    if u is not None:
        in_specs.append(pl.BlockSpec((tm, tn), lambda i, j: (i, j)))
        args.append(u)
    out_shape = [jax.ShapeDtypeStruct((s, n), out_dtype), jax.ShapeDtypeStruct((s, D), BF16)]
    out_specs = [pl.BlockSpec((tm, tn), lambda i, j: (i, j)), row_spec]
    if bwd:
        out_shape.append(jax.ShapeDtypeStruct((ni, 1, D), F32))
        out_specs.append(pl.BlockSpec((1, 1, D), lambda i, j: (i, 0, 0)))
    return pl.pallas_call(
        body, name=name, grid=(ni, nj), in_specs=in_specs, out_specs=out_specs, out_shape=out_shape,
        scratch_shapes=[pltpu.VMEM((tm, D), BF16)], compiler_params=_cparams(),
    )(*args)


def _mm_rowop(name, kind, pairs, rows, g, tm, tk, relu2=False):
    s, kdim = pairs[0][0].shape
    tm, tk = min(tm, s), min(tk, kdim)
    ni, nk = s // tm, kdim // tk
    npair = len(pairs)

    def body(*refs):
        refs = list(refs)
        a_refs = [refs.pop(0) for _ in range(npair)]
        w_refs = [refs.pop(0) for _ in range(npair)]
        row_refs = [refs.pop(0) for _ in rows]
        g_ref = refs.pop(0)
        acc = refs.pop()
        outs = refs
        k = pl.program_id(1)

        @pl.when(k == 0)
        def _():
            acc[...] = jnp.zeros_like(acc)

        for a_ref, w_ref, (_, _, dims) in zip(a_refs, w_refs, pairs):
            a = a_ref[...]
            if relu2:
                a = jnp.square(jnp.maximum(a, 0.0))
            acc[...] += _dot(a.astype(BF16), w_ref[...], dims)

        @pl.when(k == nk - 1)
        def _():
            h = acc[...]
            if kind == "rms_res":
                outs[0][...] = row_refs[0][...] + _rms(h, g_ref[...])
                outs[1][...] = h
            else:
                _, vjp = jax.vjp(_rms, row_refs[0][...], g_ref[...])
                dx, dg = vjp(h)
                if kind == "rms_bwd_res":
                    outs[0][...] = row_refs[1][...] + dx
                    outs[1][0] = dg
                else:
                    outs[0][0] = dg

    row_spec = pl.BlockSpec((tm, D), lambda i, k: (i, 0))
    dg_shape = jax.ShapeDtypeStruct((ni, 1, D), F32)
    dg_spec = pl.BlockSpec((1, 1, D), lambda i, k: (i, 0, 0))
    in_specs = [pl.BlockSpec((tm, tk), lambda i, k: (i, k))] * npair
    for _, _, dims in pairs:
        in_specs.append(pl.BlockSpec((tk, D), lambda i, k: (k, 0)) if dims == NN
                        else pl.BlockSpec((D, tk), lambda i, k: (0, k)))
    in_specs += [row_spec] * len(rows) + [pl.BlockSpec((1, D), lambda i, k: (0, 0))]
    if kind == "rms_res":
        out_shape = [jax.ShapeDtypeStruct((s, D), F32)] * 2
        out_specs = [row_spec, row_spec]
    elif kind == "rms_bwd_res":
        out_shape = [jax.ShapeDtypeStruct((s, D), F32), dg_shape]
        out_specs = [row_spec, dg_spec]
    else:
        out_shape = [dg_shape]
        out_specs = [dg_spec]
    return pl.pallas_call(
        body, name=name, grid=(ni, nk), in_specs=in_specs, out_specs=out_specs, out_shape=out_shape,
        scratch_shapes=[pltpu.VMEM((tm, D), F32)], compiler_params=_cparams(),
    )(*[p[0] for p in pairs], *[p[1] for p in pairs], *rows, g)


def _mm_tn(name, a, gmat, tm, ts, relu2=False):
    s, m = a.shape
    tm, ts = min(tm, m), min(ts, s)
    ni, ns = m // tm, s // ts

    def body(a_ref, g_ref, o_ref, acc):
        k = pl.program_id(1)

        @pl.when(k == 0)
        def _():
            acc[...] = jnp.zeros_like(acc)

        av = a_ref[...]
        if relu2:
            av = jnp.square(jnp.maximum(av, 0.0))
        acc[...] += _dot(av.astype(BF16), g_ref[...], TN)

        @pl.when(k == ns - 1)
        def _():
            o_ref[...] = acc[...].astype(BF16)

    return pl.pallas_call(
        body, name=name, grid=(ni, ns),
        in_specs=[pl.BlockSpec((ts, tm), lambda i, k: (k, i)), pl.BlockSpec((ts, D), lambda i, k: (k, 0))],
        out_specs=pl.BlockSpec((tm, D), lambda i, k: (i, 0)),
        out_shape=jax.ShapeDtypeStruct((m, D), BF16),
        scratch_shapes=[pltpu.VMEM((tm, D), F32)], compiler_params=_cparams(),
    )(a, gmat)


def _tril():
    r = lax.broadcasted_iota(jnp.int32, (HD, HD), 0)
    c = lax.broadcasted_iota(jnp.int32, (HD, HD), 1)
    return (c <= r).astype(F32)


def _gmlp_fwd(z, gv, ws, bst, tb):
    s = z.shape[0]
    tb = min(tb, s)

    def body(zu_ref, zv_ref, gv_ref, ws_ref, bst_ref, y_ref):
        tril = _tril()
        for h in range(NH):
            cols = slice(h * HD, (h + 1) * HD)
            u, vln = _gmlp_rows(zu_ref[:, cols], zv_ref[:, cols], gv_ref[h:h + 1, :])
            wm = (ws_ref[h] * tril).astype(BF16)
            vb = vln.astype(BF16)
            for c in range(tb // HD):
                rws = slice(c * HD, (c + 1) * HD)
                mixed = _dot(wm, vb[rws], NN) + bst_ref[:, h:h + 1]
                y_ref[rws, cols] = (u[rws] * mixed).astype(BF16)

    return pl.pallas_call(
        body, name="gmlp_fwd", grid=(s // tb,),
        in_specs=[pl.BlockSpec((tb, GW), lambda i: (i, 0)), pl.BlockSpec((tb, GW), lambda i: (i, 1)),
                  pl.BlockSpec((NH, HD), lambda i: (0, 0)), pl.BlockSpec((NH, HD, HD), lambda i: (0, 0, 0)),
                  pl.BlockSpec((HD, NH), lambda i: (0, 0))],
        out_specs=pl.BlockSpec((tb, GW), lambda i: (i, 0)),
        out_shape=jax.ShapeDtypeStruct((s, GW), BF16), compiler_params=_cparams(),
    )(z, z, gv, ws, bst)


def _gmlp_bwd(z, dy, gv, ws, bst, tb):
    s = z.shape[0]
    tb = min(tb, s)
    nb = s // tb

    def body(zu_ref, zv_ref, dy_ref, gv_ref, ws_ref, bst_ref, dzu_ref, dzv_ref, dgv_ref, dws_ref, db_ref):
        tril = _tril()
        for h in range(NH):
            cols = slice(h * HD, (h + 1) * HD)
            (u, vln), vjp = jax.vjp(_gmlp_rows, zu_ref[:, cols], zv_ref[:, cols], gv_ref[h:h + 1, :])
            wmf = ws_ref[h] * tril
            wm = wmf.astype(BF16)
            wmt = wmf.T.astype(BF16)
            vb = vln.astype(BF16)
            dws = jnp.zeros((HD, HD), F32)
            db = jnp.zeros((HD, 1), F32)
            du_parts, dvln_parts = [], []
            for c in range(tb // HD):
                rws = slice(c * HD, (c + 1) * HD)
                mixed = _dot(wm, vb[rws], NN) + bst_ref[:, h:h + 1]
                dyc = dy_ref[rws, cols]
                du_parts.append(dyc * mixed)
                dmixed = dyc * u[rws]
                dmb = dmixed.astype(BF16)
                dws = dws + _dot(dmb, vb[rws], NT)
                db = db + jnp.sum(dmixed, axis=1, keepdims=True)
                dvln_parts.append(_dot(wmt, dmb, NN))
            du = jnp.concatenate(du_parts, axis=0)
            dvln = jnp.concatenate(dvln_parts, axis=0)
            dzu, dzv, dgv = vjp((du, dvln))
            dzu_ref[:, cols] = dzu.astype(BF16)
            dzv_ref[:, cols] = dzv.astype(BF16)
            dgv_ref[0, h:h + 1, :] = dgv
            dws_ref[0, h] = dws * tril
            db_ref[0, h] = jnp.broadcast_to(db, (HD, LANES))

    blk = pl.BlockSpec((tb, GW), lambda i: (i, 0))
    return pl.pallas_call(
        body, name="gmlp_bwd", grid=(nb,),
        in_specs=[blk, pl.BlockSpec((tb, GW), lambda i: (i, 1)), blk,
                  pl.BlockSpec((NH, HD), lambda i: (0, 0)), pl.BlockSpec((NH, HD, HD), lambda i: (0, 0, 0)),
                  pl.BlockSpec((HD, NH), lambda i: (0, 0))],
        out_specs=[blk, blk, pl.BlockSpec((1, NH, HD), lambda i: (i, 0, 0)),
                   pl.BlockSpec((1, NH, HD, HD), lambda i: (i, 0, 0, 0)),
                   pl.BlockSpec((1, NH, HD, LANES), lambda i: (i, 0, 0, 0))],
        out_shape=[jax.ShapeDtypeStruct((s, GW), BF16), jax.ShapeDtypeStruct((s, GW), BF16),
                   jax.ShapeDtypeStruct((nb, NH, HD), F32), jax.ShapeDtypeStruct((nb, NH, HD, HD), F32),
                   jax.ShapeDtypeStruct((nb, NH, HD, LANES), F32)],
        compiler_params=_cparams(),
    )(z, z, dy, gv, ws, bst)


def _pool_count(t0, window):
    pos = (t0 + lax.broadcasted_iota(jnp.int32, (ROW_TILE, LANES), 0)).astype(F32)
    return jnp.minimum(pos + 1.0, float(window))


def _window_sum(win, levels, back):
    n = win.shape[0]
    for lv in range(levels):
        step = 1 << lv
        win = win + pltpu.roll(win, n - step if back else step, 0)
    return win


def _pool_pooled(ppad_ref, t0, g):
    win = ppad_ref[pl.ds(t0, ROW_TILE + POOL_HALO), :]
    wsum = _window_sum(win, g + 1, False)[POOL_HALO:]
    return wsum / _pool_count(t0, POOL_WINDOWS[g]) - win[POOL_HALO:]


def _pool_fwd(z, wp, sp):
    s = z.shape[0]
    nt = s // ROW_TILE

    def body(p_ref, wp_ref, sp_ref, y_ref, ppad):
        for g in range(NG):
            cols = slice(g * LANES, (g + 1) * LANES)
            ppad[pl.ds(0, POOL_HALO), :] = jnp.zeros((POOL_HALO, LANES), F32)
            ppad[pl.ds(POOL_HALO, s), :] = p_ref[:, cols]
            wpb = wp_ref[g].astype(BF16)
            scale = sp_ref[:, cols]

            def tile(t, carry):
                t0 = pl.multiple_of(t * ROW_TILE, ROW_TILE)
                pooled = _pool_pooled(ppad, t0, g)
                y_ref[pl.ds(t0, ROW_TILE), cols] = (_dot(pooled.astype(BF16), wpb, NN) * scale).astype(BF16)
                return carry

            lax.fori_loop(0, nt, tile, 0)

    return pl.pallas_call(
        body, name="pool_fwd", grid=(1,),
        in_specs=[pl.BlockSpec((s, PW), lambda i: (0, 2 * GW // PW)),
                  pl.BlockSpec((NG, LANES, LANES), lambda i: (0, 0, 0)), pl.BlockSpec((1, PW), lambda i: (0, 0))],
        out_specs=pl.BlockSpec((s, PW), lambda i: (0, 0)),
        out_shape=jax.ShapeDtypeStruct((s, PW), BF16),
        scratch_shapes=[pltpu.VMEM((s + POOL_HALO, LANES), F32)], compiler_params=_cparams(),
    )(z, wp, sp)


def _pool_bwd(z, dy, wp, sp):
    s = z.shape[0]
    nt = s // ROW_TILE

    def body(p_ref, dy_ref, wp_ref, sp_ref, dp_ref, dwp_ref, dsp_ref, ppad, rpad, dpool):
        for g in range(NG):
            cols = slice(g * LANES, (g + 1) * LANES)
            ppad[pl.ds(0, POOL_HALO), :] = jnp.zeros((POOL_HALO, LANES), F32)
            ppad[pl.ds(POOL_HALO, s), :] = p_ref[:, cols]
            rpad[pl.ds(s, POOL_HALO), :] = jnp.zeros((POOL_HALO, LANES), F32)
            wpb = wp_ref[g].astype(BF16)
            scale = sp_ref[:, cols]

            def tile(t, carry):
                dwp, dsp = carry
                t0 = pl.multiple_of(t * ROW_TILE, ROW_TILE)
                pooled = _pool_pooled(ppad, t0, g)
                pb = pooled.astype(BF16)
                dyt = dy_ref[pl.ds(t0, ROW_TILE), cols]
                dsp = dsp + jnp.sum(dyt * _dot(pb, wpb, NN), axis=0, keepdims=True)
                dmm = (dyt * scale).astype(BF16)
                dwp = dwp + _dot(pb, dmm, TN)
                dpooled = _dot(dmm, wpb, NT)
                rpad[pl.ds(t0, ROW_TILE), :] = dpooled / _pool_count(t0, POOL_WINDOWS[g])
                dpool[pl.ds(t0, ROW_TILE), :] = dpooled
                return dwp, dsp

            dwp, dsp = lax.fori_loop(0, nt, tile, (jnp.zeros((LANES, LANES), F32), jnp.zeros((1, LANES), F32)))
            dwp_ref[g] = dwp
            dsp_ref[:, cols] = dsp

            def tile2(t, carry):
                t0 = pl.multiple_of(t * ROW_TILE, ROW_TILE)
                win = rpad[pl.ds(t0, ROW_TILE + POOL_HALO), :]
                back = _window_sum(win, g + 1, True)[:ROW_TILE]
                rows = pl.ds(t0, ROW_TILE)
                dp_ref[rows, cols] = (back - dpool[rows, :]).astype(BF16)
                return carry

            lax.fori_loop(0, nt, tile2, 0)

    return pl.pallas_call(
        body, name="pool_bwd", grid=(1,),
        in_specs=[pl.BlockSpec((s, PW), lambda i: (0, 2 * GW // PW)), pl.BlockSpec((s, PW), lambda i: (0, GW // PW)),
                  pl.BlockSpec((NG, LANES, LANES), lambda i: (0, 0, 0)), pl.BlockSpec((1, PW), lambda i: (0, 0))],
        out_specs=[pl.BlockSpec((s, PW), lambda i: (0, 0)), pl.BlockSpec((NG, LANES, LANES), lambda i: (0, 0, 0)),
                   pl.BlockSpec((1, PW), lambda i: (0, 0))],
        out_shape=[jax.ShapeDtypeStruct((s, PW), BF16), jax.ShapeDtypeStruct((NG, LANES, LANES), F32),
                   jax.ShapeDtypeStruct((1, PW), F32)],
        scratch_shapes=[pltpu.VMEM((s + POOL_HALO, LANES), F32), pltpu.VMEM((s + POOL_HALO, LANES), F32),
                        pltpu.VMEM((s, LANES), F32)],
        compiler_params=_cparams(),
    )(z, dy, wp, sp)


CONV_LEAD = CONV_HALO - (CONV_K - 1)


def _conv_taps(win, wdw_ref, lead, reverse):
    n = win.shape[0]
    acc = jnp.zeros((ROW_TILE, CW), F32)
    for j in range(CONV_K):
        tap = (CONV_K - 1 - j) if reverse else j
        acc = acc + wdw_ref[tap:tap + 1, :] * pltpu.roll(win, (n - (lead + j)) % n, 0)[:ROW_TILE]
    return acc


def _conv_fill_glu(cv_ref, cg_ref, xpad, s):
    xpad[pl.ds(0, CONV_HALO), :] = jnp.zeros((CONV_HALO, CW), F32)

    def fill(t, carry):
        t0 = pl.multiple_of(t * ROW_TILE, ROW_TILE)
        rows = pl.ds(t0, ROW_TILE)
        xpad[pl.ds(t0 + CONV_HALO, ROW_TILE), :] = _glu(cv_ref[rows, :], cg_ref[rows, :])
        return carry

    lax.fori_loop(0, s // ROW_TILE, fill, 0)


def _conv_fwd(z, wdw, bdw, lng, lnb):
    s = z.shape[0]

    def body(cv_ref, cg_ref, wdw_ref, bdw_ref, lng_ref, lnb_ref, y_ref, xpad):
        _conv_fill_glu(cv_ref, cg_ref, xpad, s)

        def tile(t, carry):
            t0 = pl.multiple_of(t * ROW_TILE, ROW_TILE)
            win = xpad[pl.ds(t0, ROW_TILE + CONV_HALO), :]
            hc = _conv_taps(win, wdw_ref, CONV_LEAD, False) + bdw_ref[...]
            y_ref[pl.ds(t0, ROW_TILE), :] = _ln_silu(hc, lng_ref[...], lnb_ref[...]).astype(BF16)
            return carry

        lax.fori_loop(0, s // ROW_TILE, tile, 0)

    vec = pl.BlockSpec((1, CW), lambda i: (0, 0))
    return pl.pallas_call(
        body, name="conv_fwd", grid=(1,),
        in_specs=[pl.BlockSpec((s, CW), lambda i: (0, (2 * GW + PW) // CW)),
                  pl.BlockSpec((s, CW), lambda i: (0, (2 * GW + PW) // CW + 1)),
                  pl.BlockSpec((CONV_K + 1, CW), lambda i: (0, 0)), vec, vec, vec],
        out_specs=pl.BlockSpec((s, CW), lambda i: (0, 0)),
        out_shape=jax.ShapeDtypeStruct((s, CW), BF16),
        scratch_shapes=[pltpu.VMEM((s + CONV_HALO, CW), F32)], compiler_params=_cparams(),
    )(z, z, wdw, bdw, lng, lnb)


def _conv_bwd(z, dy, wdw, bdw, lng, lnb):
    s = z.shape[0]

    def body(cv_ref, cg_ref, dy_ref, wdw_ref, bdw_ref, lng_ref, lnb_ref,
             dcv_ref, dcg_ref, dwdw_ref, dbdw_ref, dlng_ref, dlnb_ref, xpad, dpad):
        _conv_fill_glu(cv_ref, cg_ref, xpad, s)
        dpad[pl.ds(s, CONV_HALO), :] = jnp.zeros((CONV_HALO, CW), F32)
        dwdw_ref[...] = jnp.zeros((CONV_K + 1, CW), F32)

        def tile(t, carry):
            db, dg, dbeta = carry
            t0 = pl.multiple_of(t * ROW_TILE, ROW_TILE)
            win = xpad[pl.ds(t0, ROW_TILE + CONV_HALO), :]
            hc = _conv_taps(win, wdw_ref, CONV_LEAD, False) + bdw_ref[...]
            _, vjp = jax.vjp(_ln_silu, hc, lng_ref[...], lnb_ref[...])
            dhc, dg_t, dbeta_t = vjp(dy_ref[pl.ds(t0, ROW_TILE), :])
            dpad[pl.ds(t0, ROW_TILE), :] = dhc
            n = win.shape[0]
            for j in range(CONV_K):
                shifted = pltpu.roll(win, (n - (CONV_LEAD + j)) % n, 0)[:ROW_TILE]
                dwdw_ref[j:j + 1, :] += jnp.sum(dhc * shifted, axis=0, keepdims=True)
            return db + jnp.sum(dhc, axis=0, keepdims=True), dg + dg_t, dbeta + dbeta_t

        zero = jnp.zeros((1, CW), F32)
        db, dg, dbeta = lax.fori_loop(0, s // ROW_TILE, tile, (zero, zero, zero))
        dbdw_ref[...] = db
        dlng_ref[...] = dg
        dlnb_ref[...] = dbeta

        def tile2(t, carry):
            t0 = pl.multiple_of(t * ROW_TILE, ROW_TILE)
            rows = pl.ds(t0, ROW_TILE)
            win = dpad[pl.ds(t0, ROW_TILE + CONV_HALO), :]
            dglu = _conv_taps(win, wdw_ref, 0, True)
            _, vjp = jax.vjp(_glu, cv_ref[rows, :], cg_ref[rows, :])
            dcv, dcg = vjp(dglu)
            dcv_ref[rows, :] = dcv.astype(BF16)
            dcg_ref[rows, :] = dcg.astype(BF16)
            return carry

        lax.fori_loop(0, s // ROW_TILE, tile2, 0)

    vec = pl.BlockSpec((1, CW), lambda i: (0, 0))
    full = pl.BlockSpec((s, CW), lambda i: (0, 0))
    wspec = pl.BlockSpec((CONV_K + 1, CW), lambda i: (0, 0))
    vshape = jax.ShapeDtypeStruct((1, CW), F32)
    return pl.pallas_call(
        body, name="conv_bwd", grid=(1,),
        in_specs=[pl.BlockSpec((s, CW), lambda i: (0, (2 * GW + PW) // CW)),
                  pl.BlockSpec((s, CW), lambda i: (0, (2 * GW + PW) // CW + 1)),
                  pl.BlockSpec((s, CW), lambda i: (0, (GW + PW) // CW)), wspec, vec, vec, vec],
        out_specs=[full, full, wspec, vec, vec, vec],
        out_shape=[jax.ShapeDtypeStruct((s, CW), BF16), jax.ShapeDtypeStruct((s, CW), BF16),
                   jax.ShapeDtypeStruct((CONV_K + 1, CW), F32), vshape, vshape, vshape],
        scratch_shapes=[pltpu.VMEM((s + CONV_HALO, CW), F32), pltpu.VMEM((s + CONV_HALO, CW), F32)],
        compiler_params=_cparams(),
    )(z, z, dy, wdw, bdw, lng, lnb)


def _softmax_rows(sc):
    e = jnp.exp(sc - jnp.max(sc, axis=-1, keepdims=True))
    return e / jnp.sum(e, axis=-1, keepdims=True)


def _attn_fwd(q, k, v, tq):
    s, m = q.shape[0], k.shape[0]
    tq = min(tq, s)

    def body(q_ref, k_ref, v_ref, o_ref):
        for h in range(XH):
            cols = slice(h * XHD, (h + 1) * XHD)
            p = _softmax_rows(_dot(q_ref[:, cols], k_ref[:, cols], NT) * ATT_SCALE)
            o_ref[:, cols] = _dot(p.astype(BF16), v_ref[:, cols], NN).astype(BF16)

    kv = pl.BlockSpec((m, D), lambda i: (0, 0))
    return pl.pallas_call(
        body, name="attn_fwd", grid=(s // tq,),
        in_specs=[pl.BlockSpec((tq, D), lambda i: (i, 0)), kv, kv],
        out_specs=pl.BlockSpec((tq, D), lambda i: (i, 0)),
        out_shape=jax.ShapeDtypeStruct((s, D), BF16), compiler_params=_cparams(),
    )(q, k, v)


def _attn_bwd(q, k, v, do, tq):
    s, m = q.shape[0], k.shape[0]
    tq = min(tq, s)

    def body(q_ref, k_ref, v_ref, do_ref, dq_ref, dk_ref, dv_ref):
        @pl.when(pl.program_id(0) == 0)
        def _():
            dk_ref[...] = jnp.zeros_like(dk_ref)
            dv_ref[...] = jnp.zeros_like(dv_ref)

        for h in range(XH):
            cols = slice(h * XHD, (h + 1) * XHD)
            qh, kh, vh, doh = q_ref[:, cols], k_ref[:, cols], v_ref[:, cols], do_ref[:, cols]
            p = _softmax_rows(_dot(qh, kh, NT) * ATT_SCALE)
            dp = _dot(doh, vh, NT)
            dv_ref[:, cols] += _dot(p.astype(BF16), doh, TN)
            ds = (p * (dp - jnp.sum(p * dp, axis=-1, keepdims=True)) * ATT_SCALE).astype(BF16)
            dq_ref[:, cols] = _dot(ds, kh, NN).astype(BF16)
            dk_ref[:, cols] += _dot(ds, qh, TN)

    kv = pl.BlockSpec((m, D), lambda i: (0, 0))
    qs = pl.BlockSpec((tq, D), lambda i: (i, 0))
    return pl.pallas_call(
        body, name="attn_bwd", grid=(s // tq,),
        in_specs=[qs, kv, kv, qs], out_specs=[qs, kv, kv],
        out_shape=[jax.ShapeDtypeStruct((s, D), BF16), jax.ShapeDtypeStruct((m, D), F32),
                   jax.ShapeDtypeStruct((m, D), F32)],
        compiler_params=_cparams(),
    )(q, k, v, do)


def _loss_head(y, target, tm):
    s = y.shape[0]
    tm = min(tm, s)

    def body(y_ref, t_ref, dy_ref, part_ref):
        err = y_ref[...] - t_ref[...]
        dy_ref[...] = err * (1.0 / D)
        part_ref[...] = jnp.full((1, 8, LANES), 0.5 * jnp.sum(err * err) * (1.0 / D), F32)

    blk = pl.BlockSpec((tm, D), lambda i: (i, 0))
    dy, part = pl.pallas_call(
        body, name="loss_head", grid=(s // tm,), in_specs=[blk, blk],
        out_specs=[blk, pl.BlockSpec((1, 8, LANES), lambda i: (i, 0, 0))],
        out_shape=[jax.ShapeDtypeStruct((s, D), F32), jax.ShapeDtypeStruct((s // tm, 8, LANES), F32)],
        compiler_params=_cparams(),
    )(y, target)
    return dy, jnp.sum(part[:, 0, 0])


def _layer_fwd(x0, mem, w, p):
    z, hn0 = _rowop_mm("mix_in", "rms", (x0,), p["norm_mix_pre"], w["w_in"], NT, F32, 512, 512)
    ya = _gmlp_fwd(z, p["gmlp_v_gain"], p["w_spatial"], p["b_spatial_t"], 512)
    yb = _pool_fwd(z, p["w_pool"], p["s_pool"])
    yc = _conv_fwd(z, p["w_dw"], p["b_dw"], p["conv_ln_g"], p["conv_ln_b"])
    y = jnp.concatenate([ya, yb, yc], axis=1)
    x1, h0 = _mm_rowop("mix_out", "rms_res", [(y, w["w_out"], NN)], (x0,), p["norm_mix_post"], 512, 512)
    q, hn1 = _rowop_mm("att_q", "rms", (x1,), p["norm_xattn_pre"], w["w_q"], NN, BF16, 512, 512)
    k, mn = _rowop_mm("att_k", "rms", (mem,), p["norm_mem"], w["w_k"], NN, BF16, 256, 512)
    v, _ = _rowop_mm("att_v", "rms", (mem,), p["norm_mem"], w["w_v"], NN, BF16, 256, 512)
    o = _attn_fwd(q, k, v, 256)
    x2, h1 = _mm_rowop("att_o", "rms_res", [(o, w["w_o"], NN)], (x1,), p["norm_xattn_post"], 512, 512)
    u, hn2 = _rowop_mm("ffn_up", "rms", (x2,), p["norm_ffn_pre"], w["w_up"], NT, F32, 512, 512)
    x3, h2 = _mm_rowop("ffn_down", "rms_res", [(u, w["w_down"], NN)], (x2,), p["norm_ffn_post"], 512, 512, relu2=True)
    saved = dict(x0=x0, z=z, hn0=hn0, y=y, h0=h0, x1=x1, q=q, hn1=hn1, k=k, v=v, mn=mn, o=o, h1=h1, x2=x2, u=u,
                 hn2=hn2, h2=h2)
    return x3, saved


def _layer_bwd(dx3, mem, w, p, sv):
    gb, gs = {}, {}
    du, dh2, dg = _rowop_mm("ffn_down_bwd", "rms_bwd", (sv["h2"], dx3), p["norm_ffn_post"], w["w_down"], NT, BF16,
                            512, 512, u=sv["u"])
    gs["norm_ffn_post"] = jnp.sum(dg, axis=0)
    gb["w_down"] = _mm_tn("ffn_down_dw", sv["u"], dh2, 512, 512, relu2=True)
    dx2, dg = _mm_rowop("ffn_up_bwd", "rms_bwd_res", [(du, w["w_up"], NN)], (sv["x2"], dx3), p["norm_ffn_pre"], 512, 512)
    gs["norm_ffn_pre"] = jnp.sum(dg, axis=0)
    gb["w_up"] = _mm_tn("ffn_up_dw", du, sv["hn2"], 512, 512)
    do, dh1, dg = _rowop_mm("att_o_bwd", "rms_bwd", (sv["h1"], dx2), p["norm_xattn_post"], w["w_o"], NT, BF16, 512, 512)
    gs["norm_xattn_post"] = jnp.sum(dg, axis=0)
    gb["w_o"] = _mm_tn("att_o_dw", sv["o"], dh1, 512, 512)
    dq, dk, dv = _attn_bwd(sv["q"], sv["k"], sv["v"], do, 256)
    dk, dv = dk.astype(BF16), dv.astype(BF16)
    dx1, dg = _mm_rowop("att_q_bwd", "rms_bwd_res", [(dq, w["w_q"], NT)], (sv["x1"], dx2), p["norm_xattn_pre"], 512, 512)
    gs["norm_xattn_pre"] = jnp.sum(dg, axis=0)
    gb["w_q"] = _mm_tn("att_q_dw", sv["hn1"], dq, 512, 512)
    gb["w_k"] = _mm_tn("att_k_dw", sv["mn"], dk, 512, 256)
    gb["w_v"] = _mm_tn("att_v_dw", sv["mn"], dv, 512, 256)
    (dg,) = _mm_rowop("att_kv_bwd", "rms_bwd_gain", [(dk, w["w_k"], NT), (dv, w["w_v"], NT)], (mem,), p["norm_mem"],
                      256, 512)
    gs["norm_mem"] = jnp.sum(dg, axis=0)
    dy, dh0, dg = _rowop_mm("mix_out_bwd", "rms_bwd", (sv["h0"], dx1), p["norm_mix_post"], w["w_out"], NT, F32, 512, 512)
    gs["norm_mix_post"] = jnp.sum(dg, axis=0)
    gb["w_out"] = _mm_tn("mix_out_dw", sv["y"], dh0, 512, 512)
    z = sv["z"]
    dzu, dzv, dgv, dws, dbs = _gmlp_bwd(z, dy, p["gmlp_v_gain"], p["w_spatial"], p["b_spatial_t"], 512)
    gs["gmlp_v_gain"] = jnp.sum(dgv, axis=0)
    gs["w_spatial"] = jnp.sum(dws, axis=0)
    gs["b_spatial"] = jnp.sum(dbs[..., 0], axis=0)
    dp, gs["w_pool"], gs["s_pool"] = _pool_bwd(z, dy, p["w_pool"], p["s_pool"])
    dcv, dcg, dwdw, gs["b_dw"], gs["conv_ln_g"], gs["conv_ln_b"] = _conv_bwd(
        z, dy, p["w_dw"], p["b_dw"], p["conv_ln_g"], p["conv_ln_b"])
    gs["w_dw"] = dwdw[:CONV_K]
    dz = jnp.concatenate([dzu, dzv, dp, dcv, dcg], axis=1)
    dx0, dg = _mm_rowop("mix_in_bwd", "rms_bwd_res", [(dz, w["w_in"], NN)], (sv["x0"], dx1), p["norm_mix_pre"], 512, 512)
    gs["norm_mix_pre"] = jnp.sum(dg, axis=0)
    gb["w_in"] = _mm_tn("mix_in_dw", dz, sv["hn0"], 512, 512)
    return dx0, gb, gs


def _layer_params(small, l):
    p = {n: small[n][l].reshape(1, -1) for n in ("norm_mix_pre", "norm_mix_post", "s_pool", "b_dw", "conv_ln_g",
                                                   "conv_ln_b", "norm_xattn_pre", "norm_mem", "norm_xattn_post",
                                                   "norm_ffn_pre", "norm_ffn_post")}
    p["gmlp_v_gain"] = small["gmlp_v_gain"][l]
    p["w_spatial"] = small["w_spatial"][l]
    p["b_spatial_t"] = small["b_spatial"][l].T
    p["w_pool"] = small["w_pool"][l]
    p["w_dw"] = jnp.pad(small["w_dw"][l], ((0, 1), (0, 0)))
    return p


def _local_step(x, mem, target, wfull, small):
    params = [_layer_params(small, l) for l in range(DEPTH)]
    saved = []
    h = x
    for l in range(DEPTH):
        h, sv = _layer_fwd(h, mem, wfull[l], params[l])
        saved.append(sv)
    dh, loss = _loss_head(h, target, 512)
    gbig, gsmall = [None] * DEPTH, [None] * DEPTH
    for l in reversed(range(DEPTH)):
        dh, gbig[l], gs = _layer_bwd(dh, mem, wfull[l], params[l], saved[l])
        gsmall[l] = {n: gs[n].reshape(small[n].shape[1:]) for n in SMALL}
    return loss, dh, gbig, gsmall


HBM = pl.BlockSpec(memory_space=pltpu.HBM)


def _position():
    return lax.axis_index("x"), lax.axis_index("y"), lax.axis_index("c")


def _all_gather(name, shards):
    n = len(shards)

    def body(*refs):
        ins, outs = refs[:n], refs[n:2 * n]
        send_sems, recv_sems, local_sems = refs[2 * n:]
        x, y, c = _position()
        me, sibling = (x, y, c), (x, y, 1 - c)
        chips = [(1 - x, y), (x, 1 - y), (1 - x, 1 - y)]

        def rows(a, dev):
            r = shards[a].shape[0]
            return outs[a].at[pl.ds((4 * dev[0] + 2 * dev[1] + dev[2]) * r, r), :]

        def copy(a, k, block, to, src=None):
            return pltpu.make_async_remote_copy(
                src_ref=rows(a, block) if src is None else src, dst_ref=rows(a, block),
                send_sem=send_sems.at[a, k], recv_sem=recv_sems.at[a, k], device_id=to, device_id_type=MESH)

        started = []
        for a in range(n):
            mine = pltpu.make_async_copy(ins[a], rows(a, me), local_sems.at[a])
            mine.start()
            started.append(mine)
        first = []
        for a in range(n):
            for j, chip in enumerate(chips):
                first.append(copy(a, 1 + j, me, (*chip, c), src=ins[a]))
        for a in range(n):
            first.append(copy(a, 0, me, sibling, src=ins[a]))
        for cp in first:
            cp.start()
        passed = []
        for a in range(n):
            for j, chip in enumerate(chips):
                copy(a, 1 + j, (*chip, c), me).wait_recv()
                fwd = copy(a, 4 + j, (*chip, c), sibling)
                fwd.start()
                passed.append(fwd)
        for a in range(n):
            copy(a, 0, sibling, me).wait_recv()
            for j, chip in enumerate(chips):
                copy(a, 4 + j, (*chip, 1 - c), me).wait_recv()
        for cp in first + passed:
            cp.wait_send()
        for mine in started:
            mine.wait()

    return pl.pallas_call(
        body, name=name, in_specs=[HBM] * n, out_specs=[HBM] * n,
        out_shape=[jax.ShapeDtypeStruct((N_DEV * s.shape[0], s.shape[1]), s.dtype) for s in shards],
        scratch_shapes=[pltpu.SemaphoreType.DMA((n, 7)), pltpu.SemaphoreType.DMA((n, 7)), pltpu.SemaphoreType.DMA((n,))],
    )(*shards)


def _sibling_exchange(name, grads):
    n = len(grads)

    def body(*refs):
        ins, outs = refs[:n], refs[n:2 * n]
        send_sems, recv_sems = refs[2 * n:]
        x, y, c = _position()
        copies = []
        for a in range(n):
            r = grads[a].shape[0] // N_DEV
            for q in range(4):
                copies.append(pltpu.make_async_remote_copy(
                    src_ref=ins[a].at[pl.ds((2 * q + 1 - c) * r, r), :], dst_ref=outs[a].at[q],
                    send_sem=send_sems.at[a, q], recv_sem=recv_sems.at[a, q], device_id=(x, y, 1 - c),
                    device_id_type=MESH))
        for cp in copies:
            cp.start()
        for cp in copies:
            cp.wait()

    return pl.pallas_call(
        body, name=name, in_specs=[HBM] * n, out_specs=[HBM] * n,
        out_shape=[jax.ShapeDtypeStruct((4, g.shape[0] // N_DEV, g.shape[1]), g.dtype) for g in grads],
        scratch_shapes=[pltpu.SemaphoreType.DMA((n, 4)), pltpu.SemaphoreType.DMA((n, 4))],
    )(*grads)


def _chip_exchange(name, parts):
    n = len(parts)

    def body(*refs):
        ins, outs = refs[:n], refs[n:2 * n]
        send_sems, recv_sems = refs[2 * n:]
        x, y, c = _position()
        chips = [(1 - x, y), (x, 1 - y), (1 - x, 1 - y)]
        copies = []
        for a in range(n):
            for j, chip in enumerate(chips):
                copies.append(pltpu.make_async_remote_copy(
                    src_ref=ins[a].at[2 * chip[0] + chip[1]], dst_ref=outs[a].at[j],
                    send_sem=send_sems.at[a, j], recv_sem=recv_sems.at[a, j], device_id=(*chip, c),
                    device_id_type=MESH))
        for cp in copies:
            cp.start()
        for cp in copies:
            cp.wait()

    return pl.pallas_call(
        body, name=name, in_specs=[HBM] * n, out_specs=[HBM] * n,
        out_shape=[jax.ShapeDtypeStruct((3,) + p.shape[1:], p.dtype) for p in parts],
        scratch_shapes=[pltpu.SemaphoreType.DMA((n, 3)), pltpu.SemaphoreType.DMA((n, 3))],
    )(*parts)


def _row_tile(r, target):
    return max(t for t in range(16, min(r, target) + 1, 16) if r % t == 0)


def _chip_partial(name, grad, got, c, tr):
    r = grad.shape[0] // N_DEV
    tr = _row_tile(r, tr)
    g4 = grad.reshape(4, 2, r, D)

    def body(c_ref, g_ref, s_ref, o_ref):
        o_ref[...] = (g_ref[...].astype(F32) + s_ref[...].astype(F32)).astype(BF16)

    return pl.pallas_call(
        body, name=name,
        grid_spec=pltpu.PrefetchScalarGridSpec(
            num_scalar_prefetch=1, grid=(4, r // tr),
            in_specs=[pl.BlockSpec((None, None, tr, D), lambda q, i, c_ref: (q, c_ref[0], i, 0)),
                      pl.BlockSpec((None, tr, D), lambda q, i, c_ref: (q, i, 0))],
            out_specs=pl.BlockSpec((None, tr, D), lambda q, i, c_ref: (q, i, 0))),
        out_shape=jax.ShapeDtypeStruct((4, r, D), BF16), compiler_params=_cparams(),
    )(c, g4, got)


def _chip_sum(name, part, got, chip, tr):
    r = part.shape[1]
    tr = _row_tile(r, tr)

    def body(q_ref, p_ref, g_ref, o_ref):
        acc = p_ref[...].astype(F32)
        for j in range(3):
            acc = acc + g_ref[j].astype(F32)
        o_ref[...] = acc

    return pl.pallas_call(
        body, name=name,
        grid_spec=pltpu.PrefetchScalarGridSpec(
            num_scalar_prefetch=1, grid=(r // tr,),
            in_specs=[pl.BlockSpec((None, tr, D), lambda i, q_ref: (q_ref[0], i, 0)),
                      pl.BlockSpec((3, tr, D), lambda i, q_ref: (0, i, 0))],
            out_specs=pl.BlockSpec((tr, D), lambda i, q_ref: (i, 0))),
        out_shape=jax.ShapeDtypeStruct((r, D), F32), compiler_params=_cparams(),
    )(chip, part, got)


def _sum_blocks(name, gathered, r):
    def body(g_ref, o_ref):
        acc = g_ref[0]
        for d in range(1, N_DEV):
            acc = acc + g_ref[d]
        o_ref[...] = acc

    tr = 8
    return pl.pallas_call(
        body, name=name, grid=(r // tr,),
        in_specs=[pl.BlockSpec((N_DEV, tr, D), lambda i: (0, i, 0))],
        out_specs=pl.BlockSpec((tr, D), lambda i: (i, 0)),
        out_shape=jax.ShapeDtypeStruct((r, D), F32), compiler_params=_cparams(),
    )(gathered.reshape(N_DEV, r, D))


def _adamw_math(w, g, m, v):
    m = ADAM_B1 * m + (1.0 - ADAM_B1) * g
    v = ADAM_B2 * v + (1.0 - ADAM_B2) * jnp.square(g)
    m_hat = m / (1.0 - ADAM_B1 ** ADAM_STEP)
    v_hat = v / (1.0 - ADAM_B2 ** ADAM_STEP)
    delta = -ADAM_LR * (m_hat / (jnp.sqrt(v_hat) + ADAM_EPS) + ADAM_WD * w)
    return delta, m, v


def _adamw(name, w, g, m, v, tr):
    nl, r, cdim = w.shape
    tr = min(tr, r)

    def body(w_ref, g_ref, m_ref, v_ref, d_ref, nm_ref, nv_ref):
        d_ref[...], nm_ref[...], nv_ref[...] = _adamw_math(w_ref[...], g_ref[...], m_ref[...], v_ref[...])

    blk = pl.BlockSpec((None, tr, cdim), lambda l, i: (l, i, 0))
    shape = jax.ShapeDtypeStruct((nl, r, cdim), F32)
    return pl.pallas_call(
        body, name=name, grid=(nl, r // tr), in_specs=[blk] * 4, out_specs=[blk] * 3, out_shape=[shape] * 3,
        compiler_params=_cparams(),
    )(w, g, m, v)


def _to_gather_layout(name, w):
    if name in ("w_in", "w_up"):
        w = w.T
    return w.astype(BF16)


def _from_gather_layout(name, g):
    return g.T if name in ("w_in", "w_up") else g


def _pack(arrays, rows):
    flat = jnp.concatenate([a.reshape(-1) for a in arrays])
    return jnp.pad(flat, (0, rows * D - flat.shape[0])).reshape(rows, D)


def _unpack(buf, shapes):
    flat = buf.reshape(-1)
    out, off = [], 0
    for shp in shapes:
        size = 1
        for dim in shp:
            size *= dim
        out.append(flat[off:off + size].reshape(shp))
        off += size
    return out


def _rows_for(shapes, mult=8):
    total = 0
    for shp in shapes:
        size = 1
        for dim in shp:
            size *= dim
        total += size
    return -(-total // (mult * D)) * mult


def kernel(x, mem, norm_mix_pre, norm_mix_post, w_in, w_out, gmlp_v_gain, w_spatial, b_spatial, w_pool, s_pool, w_dw, b_dw, conv_ln_g, conv_ln_b, norm_xattn_pre, norm_mem, norm_xattn_post, w_q, w_k, w_v, w_o, norm_ffn_pre, norm_ffn_post, w_up, w_down, loss_target, m_norm_mix_pre, m_norm_mix_post, m_w_in, m_w_out, m_gmlp_v_gain, m_w_spatial, m_b_spatial, m_w_pool, m_s_pool, m_w_dw, m_b_dw, m_conv_ln_g, m_conv_ln_b, m_norm_xattn_pre, m_norm_mem, m_norm_xattn_post, m_w_q, m_w_k, m_w_v, m_w_o, m_norm_ffn_pre, m_norm_ffn_post, m_w_up, m_w_down, v_norm_mix_pre, v_norm_mix_post, v_w_in, v_w_out, v_gmlp_v_gain, v_w_spatial, v_b_spatial, v_w_pool, v_s_pool, v_w_dw, v_b_dw, v_conv_ln_g, v_conv_ln_b, v_norm_xattn_pre, v_norm_mem, v_norm_xattn_post, v_w_q, v_w_k, v_w_v, v_w_o, v_norm_ffn_pre, v_norm_ffn_post, v_w_up, v_w_down):
    args = dict(locals())
    wts = {n: args[n] for n in WEIGHTS}
    mom_m = {n: args["m_" + n] for n in WEIGHTS}
    mom_v = {n: args["v_" + n] for n in WEIGHTS}
    xi, yi, ci = _position()
    me = 4 * xi + 2 * yi + ci

    dw_rows = _rows_for([w_dw.shape])
    shards = [_to_gather_layout(n, wts[n][l]) for l in range(DEPTH) for n in BIG] + [_pack([w_dw], dw_rows)]
    gathered = _all_gather("gather_weights", shards)
    wfull = [dict(zip(BIG, gathered[l * len(BIG):(l + 1) * len(BIG)])) for l in range(DEPTH)]
    dw_blocks = gathered[-1].reshape(N_DEV, -1)[:, :w_dw.size].reshape((N_DEV,) + w_dw.shape)
    small = {n: wts[n] for n in SMALL}
    small["w_dw"] = jnp.moveaxis(dw_blocks, 0, 2).reshape(DEPTH, CONV_K, CW)

    loss, dx, gbig, gsmall = _local_step(x[0], mem[0], loss_target[0], wfull, small)

    cvec = jnp.reshape(ci, (1,)).astype(jnp.int32)
    qvec = jnp.reshape(2 * xi + yi, (1,)).astype(jnp.int32)
    keys = [(l, n) for l in range(DEPTH) for n in BIG]
    partial_list = [gbig[l][n] for l, n in keys]
    from_sibling = _sibling_exchange("grad_sibling_exchange", partial_list)
    chip_parts = [_chip_partial("chip_partial_%d_%s" % (l, n), g, r, cvec, 256)
                  for (l, n), g, r in zip(keys, partial_list, from_sibling)]
    from_chips = _chip_exchange("grad_chip_exchange", chip_parts)
    grads = {}
    for (l, n), part, got in zip(keys, chip_parts, from_chips):
        grads[(l, n)] = _from_gather_layout(n, _chip_sum("chip_sum_%d_%s" % (l, n), part, got, qvec, 256))

    small_shapes = [wts[n].shape[1:] for n in SMALL]
    small_shapes[SMALL.index("w_dw")] = (CONV_K, CW)
    srows = _rows_for(small_shapes * DEPTH)
    packed = _pack([gsmall[l][n] for l in range(DEPTH) for n in SMALL], srows)
    (all_small,) = _all_gather("gather_small_grads", [packed])
    summed = _unpack(_sum_blocks("sum_small_grads", all_small, srows), small_shapes * DEPTH)
    for idx, (l, n) in enumerate([(l, n) for l in range(DEPTH) for n in SMALL]):
        grads[(l, n)] = summed[idx]
    shard_cols = CW // N_DEV
    for l in range(DEPTH):
        grads[(l, "w_dw")] = lax.dynamic_slice_in_dim(grads[(l, "w_dw")], me * shard_cols, shard_cols, axis=1)

    grad_w = {n: jnp.stack([grads[(l, n)] for l in range(DEPTH)]) for n in SMALL}
    delta, new_m, new_v = {}, {}, {}
    small_all_shapes = [wts[n].shape for n in SMALL]
    arows = _rows_for(small_all_shapes, 64)
    packs = [_pack([src[n] for n in SMALL], arows)[None] for src in (wts, grad_w, mom_m, mom_v)]
    outs = _adamw("adamw_small", *packs, 64)
    for dst, buf in zip((delta, new_m, new_v), outs):
        dst.update(zip(SMALL, _unpack(buf, small_all_shapes)))
    for n in BIG:
        grad_w[n] = jnp.stack([grads[(l, n)] for l in range(DEPTH)])
        delta[n], new_m[n], new_v[n] = _adamw("adamw_" + n, wts[n], grad_w[n], mom_m[n], mom_v[n], 256)

    loss = lax.psum(loss, ("x", "y", "c"))
    return (loss, dx[None], *[grad_w[n] for n in WEIGHTS], *[delta[n] for n in WEIGHTS],
            *[new_m[n] for n in WEIGHTS], *[new_v[n] for n in WEIGHTS])
```

```python
import functools

import jax
import jax.numpy as jnp
from jax import lax
from jax.experimental import pallas as pl
from jax.experimental.pallas import tpu as pltpu

F32 = jnp.float32
BF16 = jnp.bfloat16

D = 2048
GW = 1024
PW = 512
CW = 512
HD = 128
NH = 8
NG = 4
POOL_WINDOWS = (2, 4, 8, 16)
CONV_K = 31
IN_COLS = 2 * GW + PW + 2 * CW
DFF = 4 * D
XH = 4
XHD = D // XH
ATT_SCALE = XHD ** -0.5
RMS_EPS = 1e-6
LN_EPS = 1e-5
DEPTH = 2
N_DEV = 8

ADAM_LR = 0.001
ADAM_B1 = 0.9
ADAM_B2 = 0.999
ADAM_EPS = 1e-08
ADAM_WD = 0.01
ADAM_STEP = 10

LANES = 128
CONV_HALO = 32
POOL_HALO = 16
ROW_TILE = 128
VMEM_LIMIT = 60 * 1024 * 1024

MESH = pl.DeviceIdType.MESH
NT = (((1,), (1,)), ((), ()))
NN = (((1,), (0,)), ((), ()))
TN = (((0,), (0,)), ((), ()))

BIG = ("w_out", "w_q", "w_k", "w_v", "w_o", "w_up", "w_down", "w_in")
GATHER_GROUPS = (("in", ("w_in",)), ("att", ("w_out", "w_q", "w_k", "w_v", "w_o")), ("ffn", ("w_up", "w_down")))
SMALL = ("norm_mix_pre", "norm_mix_post", "gmlp_v_gain", "w_spatial", "b_spatial", "w_pool", "s_pool",
         "w_dw", "b_dw", "conv_ln_g", "conv_ln_b", "norm_xattn_pre", "norm_mem", "norm_xattn_post",
         "norm_ffn_pre", "norm_ffn_post")
WEIGHTS = ("norm_mix_pre", "norm_mix_post", "w_in", "w_out", "gmlp_v_gain", "w_spatial", "b_spatial", "w_pool",
           "s_pool", "w_dw", "b_dw", "conv_ln_g", "conv_ln_b", "norm_xattn_pre", "norm_mem", "norm_xattn_post",
           "w_q", "w_k", "w_v", "w_o", "norm_ffn_pre", "norm_ffn_post", "w_up", "w_down")


def _cparams():
    return pltpu.CompilerParams(vmem_limit_bytes=VMEM_LIMIT)


def _dot(a, b, dims):
    return lax.dot_general(a, b, dims, preferred_element_type=F32)


def _rms(x, g):
    y = x * lax.rsqrt(jnp.mean(x * x, axis=-1, keepdims=True) + RMS_EPS)
    return y * g


def _gelu(x):
    cdf = 0.5 * (1.0 + jnp.tanh(0.7978845608028654 * (x + 0.044715 * (x * x * x))))
    return x * cdf


def _layer_norm(x, g, b=None):
    mu = jnp.mean(x, axis=-1, keepdims=True)
    xc = x - mu
    var = jnp.mean(xc * xc, axis=-1, keepdims=True)
    y = xc * lax.rsqrt(var + LN_EPS) * g
    return y if b is None else y + b


def _sigmoid(x):
    return 1.0 / (1.0 + jnp.exp(-x))


def _gmlp_rows(zu, zv, gv):
    return _gelu(zu), _layer_norm(_gelu(zv), gv)


def _glu(cv, cg):
    return cv * _sigmoid(cg)


def _ln_silu(h, g, b):
    y = _layer_norm(h, g, b)
    return y * _sigmoid(y)


ANY = pl.BlockSpec(memory_space=pl.ANY)


def _rowop_mm(name, kind, rows, g, w, dims, out_dtype, tm, tn, u=None, after=()):
    s = rows[0].shape[0]
    n = w.shape[0] if dims == NT else w.shape[1]
    tm, tn = min(tm, s), min(tn, n)
    ni, nj = s // tm, n // tn
    bwd = kind == "rms_bwd"

    def body(*refs):
        refs = list(refs)
        row_refs = [refs.pop(0) for _ in rows]
        g_ref, w_ref = refs.pop(0), refs.pop(0)
        u_ref = refs.pop(0) if u is not None else None
        del refs[:len(after)]
        out_ref, a_ref = refs.pop(0), refs.pop(0)
        dg_ref = refs.pop(0) if bwd else None
        a_s = refs.pop(0)

        @pl.when(pl.program_id(1) == 0)
        def _():
            if bwd:
                _, vjp = jax.vjp(_rms, row_refs[0][...], g_ref[...])
                a, dg = vjp(row_refs[1][...])
                dg_ref[0] = dg
            else:
                a = _rms(row_refs[0][...], g_ref[...])
            a_s[...] = a.astype(BF16)
            a_ref[...] = a_s[...]

        acc = _dot(a_s[...], w_ref[...], dims)
        if u_ref is not None:
            acc = acc * (2.0 * jnp.maximum(u_ref[...], 0.0))
        out_ref[...] = acc.astype(out_dtype)

    row_spec = pl.BlockSpec((tm, D), lambda i, j: (i, 0))
    w_spec = pl.BlockSpec((tn, D), lambda i, j: (j, 0)) if dims == NT else pl.BlockSpec((D, tn), lambda i, j: (0, j))
    in_specs = [row_spec] * len(rows) + [pl.BlockSpec((1, D), lambda i, j: (0, 0)), w_spec]
    args = list(rows) + [g, w]
    if u is not None:
        in_specs.append(pl.BlockSpec((tm, tn), lambda i, j: (i, j)))
        args.append(u)
    in_specs += [ANY] * len(after)
    args += list(after)
    out_shape = [jax.ShapeDtypeStruct((s, n), out_dtype), jax.ShapeDtypeStruct((s, D), BF16)]
    out_specs = [pl.BlockSpec((tm, tn), lambda i, j: (i, j)), row_spec]
    if bwd:
        out_shape.append(jax.ShapeDtypeStruct((ni, 1, D), F32))
        out_specs.append(pl.BlockSpec((1, 1, D), lambda i, j: (i, 0, 0)))
    return pl.pallas_call(
        body, name=name, grid=(ni, nj), in_specs=in_specs, out_specs=out_specs, out_shape=out_shape,
        scratch_shapes=[pltpu.VMEM((tm, D), BF16)], compiler_params=_cparams(),
    )(*args)


def _mm_rowop(name, kind, pairs, rows, g, tm, tk, relu2=False, after=()):
    s, kdim = pairs[0][0].shape
    tm, tk = min(tm, s), min(tk, kdim)
    ni, nk = s // tm, kdim // tk
    npair = len(pairs)

    def body(*refs):
        refs = list(refs)
        a_refs = [refs.pop(0) for _ in range(npair)]
        w_refs = [refs.pop(0) for _ in range(npair)]
        row_refs = [refs.pop(0) for _ in rows]
        g_ref = refs.pop(0)
        del refs[:len(after)]
        acc = refs.pop()
        outs = refs
        k = pl.program_id(1)

        @pl.when(k == 0)
        def _():
            acc[...] = jnp.zeros_like(acc)

        for a_ref, w_ref, (_, _, dims) in zip(a_refs, w_refs, pairs):
            a = a_ref[...]
            if relu2:
                a = jnp.square(jnp.maximum(a, 0.0))
            acc[...] += _dot(a.astype(BF16), w_ref[...], dims)

        @pl.when(k == nk - 1)
        def _():
            h = acc[...]
            if kind == "rms_res":
                outs[0][...] = row_refs[0][...] + _rms(h, g_ref[...])
                outs[1][...] = h
            else:
                _, vjp = jax.vjp(_rms, row_refs[0][...], g_ref[...])
                dx, dg = vjp(h)
                if kind == "rms_bwd_res":
                    outs[0][...] = row_refs[1][...] + dx
                    outs[1][0] = dg
                else:
                    outs[0][0] = dg

    row_spec = pl.BlockSpec((tm, D), lambda i, k: (i, 0))
    dg_shape = jax.ShapeDtypeStruct((ni, 1, D), F32)
    dg_spec = pl.BlockSpec((1, 1, D), lambda i, k: (i, 0, 0))
    in_specs = [pl.BlockSpec((tm, tk), lambda i, k: (i, k))] * npair
    for _, _, dims in pairs:
        in_specs.append(pl.BlockSpec((tk, D), lambda i, k: (k, 0)) if dims == NN
                        else pl.BlockSpec((D, tk), lambda i, k: (0, k)))
    in_specs += [row_spec] * len(rows) + [pl.BlockSpec((1, D), lambda i, k: (0, 0))] + [ANY] * len(after)
    if kind == "rms_res":
        out_shape = [jax.ShapeDtypeStruct((s, D), F32)] * 2
        out_specs = [row_spec, row_spec]
    elif kind == "rms_bwd_res":
        out_shape = [jax.ShapeDtypeStruct((s, D), F32), dg_shape]
        out_specs = [row_spec, dg_spec]
    else:
        out_shape = [dg_shape]
        out_specs = [dg_spec]
    return pl.pallas_call(
        body, name=name, grid=(ni, nk), in_specs=in_specs, out_specs=out_specs, out_shape=out_shape,
        scratch_shapes=[pltpu.VMEM((tm, D), F32)], compiler_params=_cparams(),
    )(*[p[0] for p in pairs], *[p[1] for p in pairs], *rows, g, *after)


def _mm_tn(name, a, gmat, tm, ts, relu2=False, after=()):
    s, m = a.shape
    tm, ts = min(tm, m), min(ts, s)
    ni, ns = m // tm, s // ts

    def body(a_ref, g_ref, *rest):
        o_ref, acc = rest[len(after):]
        k = pl.program_id(1)

        @pl.when(k == 0)
        def _():
            acc[...] = jnp.zeros_like(acc)

        av = a_ref[...]
        if relu2:
            av = jnp.square(jnp.maximum(av, 0.0))
        acc[...] += _dot(av.astype(BF16), g_ref[...], TN)

        @pl.when(k == ns - 1)
        def _():
            o_ref[...] = acc[...].astype(BF16)

    return pl.pallas_call(
        body, name=name, grid=(ni, ns),
        in_specs=[pl.BlockSpec((ts, tm), lambda i, k: (k, i)), pl.BlockSpec((ts, D), lambda i, k: (k, 0))]
        + [ANY] * len(after),
        out_specs=pl.BlockSpec((tm, D), lambda i, k: (i, 0)),
        out_shape=jax.ShapeDtypeStruct((m, D), BF16),
        scratch_shapes=[pltpu.VMEM((tm, D), F32)], compiler_params=_cparams(),
    )(a, gmat, *after)


def _tril():
    r = lax.broadcasted_iota(jnp.int32, (HD, HD), 0)
    c = lax.broadcasted_iota(jnp.int32, (HD, HD), 1)
    return (c <= r).astype(F32)


def _gmlp_fwd(z, gv, ws, bst, tb):
    s = z.shape[0]
    tb = min(tb, s)

    def body(zu_ref, zv_ref, gv_ref, ws_ref, bst_ref, y_ref):
        tril = _tril()
        for h in range(NH):
            cols = slice(h * HD, (h + 1) * HD)
            u, vln = _gmlp_rows(zu_ref[:, cols], zv_ref[:, cols], gv_ref[h:h + 1, :])
            wm = (ws_ref[h] * tril).astype(BF16)
            vb = vln.astype(BF16)
            for c in range(tb // HD):
                rws = slice(c * HD, (c + 1) * HD)
                mixed = _dot(wm, vb[rws], NN) + bst_ref[:, h:h + 1]
                y_ref[rws, cols] = (u[rws] * mixed).astype(BF16)

    return pl.pallas_call(
        body, name="gmlp_fwd", grid=(s // tb,),
        in_specs=[pl.BlockSpec((tb, GW), lambda i: (i, 0)), pl.BlockSpec((tb, GW), lambda i: (i, 1)),
                  pl.BlockSpec((NH, HD), lambda i: (0, 0)), pl.BlockSpec((NH, HD, HD), lambda i: (0, 0, 0)),
                  pl.BlockSpec((HD, NH), lambda i: (0, 0))],
        out_specs=pl.BlockSpec((tb, GW), lambda i: (i, 0)),
        out_shape=jax.ShapeDtypeStruct((s, GW), BF16), compiler_params=_cparams(),
    )(z, z, gv, ws, bst)


def _gmlp_bwd(z, dy, gv, ws, bst, tb):
    s = z.shape[0]
    tb = min(tb, s)
    nb = s // tb

    def body(zu_ref, zv_ref, dy_ref, gv_ref, ws_ref, bst_ref, dzu_ref, dzv_ref, dgv_ref, dws_ref, db_ref):
        tril = _tril()
        for h in range(NH):
            cols = slice(h * HD, (h + 1) * HD)
            (u, vln), vjp = jax.vjp(_gmlp_rows, zu_ref[:, cols], zv_ref[:, cols], gv_ref[h:h + 1, :])
            wmf = ws_ref[h] * tril
            wm = wmf.astype(BF16)
            wmt = wmf.T.astype(BF16)
            vb = vln.astype(BF16)
            dws = jnp.zeros((HD, HD), F32)
            db = jnp.zeros((HD, 1), F32)
            du_parts, dvln_parts = [], []
            for c in range(tb // HD):
                rws = slice(c * HD, (c + 1) * HD)
                mixed = _dot(wm, vb[rws], NN) + bst_ref[:, h:h + 1]
                dyc = dy_ref[rws, cols]
                du_parts.append(dyc * mixed)
                dmixed = dyc * u[rws]
                dmb = dmixed.astype(BF16)
                dws = dws + _dot(dmb, vb[rws], NT)
                db = db + jnp.sum(dmixed, axis=1, keepdims=True)
                dvln_parts.append(_dot(wmt, dmb, NN))
            du = jnp.concatenate(du_parts, axis=0)
            dvln = jnp.concatenate(dvln_parts, axis=0)
            dzu, dzv, dgv = vjp((du, dvln))
            dzu_ref[:, cols] = dzu.astype(BF16)
            dzv_ref[:, cols] = dzv.astype(BF16)
            dgv_ref[0, h:h + 1, :] = dgv
            dws_ref[0, h] = dws * tril
            db_ref[0, h] = jnp.broadcast_to(db, (HD, LANES))

    blk = pl.BlockSpec((tb, GW), lambda i: (i, 0))
    return pl.pallas_call(
        body, name="gmlp_bwd", grid=(nb,),
        in_specs=[blk, pl.BlockSpec((tb, GW), lambda i: (i, 1)), blk,
                  pl.BlockSpec((NH, HD), lambda i: (0, 0)), pl.BlockSpec((NH, HD, HD), lambda i: (0, 0, 0)),
                  pl.BlockSpec((HD, NH), lambda i: (0, 0))],
        out_specs=[blk, blk, pl.BlockSpec((1, NH, HD), lambda i: (i, 0, 0)),
                   pl.BlockSpec((1, NH, HD, HD), lambda i: (i, 0, 0, 0)),
                   pl.BlockSpec((1, NH, HD, LANES), lambda i: (i, 0, 0, 0))],
        out_shape=[jax.ShapeDtypeStruct((s, GW), BF16), jax.ShapeDtypeStruct((s, GW), BF16),
                   jax.ShapeDtypeStruct((nb, NH, HD), F32), jax.ShapeDtypeStruct((nb, NH, HD, HD), F32),
                   jax.ShapeDtypeStruct((nb, NH, HD, LANES), F32)],
        compiler_params=_cparams(),
    )(z, z, dy, gv, ws, bst)


def _pool_count(t0, window):
    pos = (t0 + lax.broadcasted_iota(jnp.int32, (ROW_TILE, LANES), 0)).astype(F32)
    return jnp.minimum(pos + 1.0, float(window))


def _window_sum(win, levels, back):
    n = win.shape[0]
    for lv in range(levels):
        step = 1 << lv
        win = win + pltpu.roll(win, n - step if back else step, 0)
    return win


def _pool_pooled(ppad_ref, t0, g):
    win = ppad_ref[pl.ds(t0, ROW_TILE + POOL_HALO), :]
    wsum = _window_sum(win, g + 1, False)[POOL_HALO:]
    return wsum / _pool_count(t0, POOL_WINDOWS[g]) - win[POOL_HALO:]


def _pool_fwd(z, wp, sp):
    s = z.shape[0]
    nt = s // ROW_TILE

    def body(p_ref, wp_ref, sp_ref, y_ref, ppad):
        for g in range(NG):
            cols = slice(g * LANES, (g + 1) * LANES)
            ppad[pl.ds(0, POOL_HALO), :] = jnp.zeros((POOL_HALO, LANES), F32)
            ppad[pl.ds(POOL_HALO, s), :] = p_ref[:, cols]
            wpb = wp_ref[g].astype(BF16)
            scale = sp_ref[:, cols]

            def tile(t, carry):
                t0 = pl.multiple_of(t * ROW_TILE, ROW_TILE)
                pooled = _pool_pooled(ppad, t0, g)
                y_ref[pl.ds(t0, ROW_TILE), cols] = (_dot(pooled.astype(BF16), wpb, NN) * scale).astype(BF16)
                return carry

            lax.fori_loop(0, nt, tile, 0)

    return pl.pallas_call(
        body, name="pool_fwd", grid=(1,),
        in_specs=[pl.BlockSpec((s, PW), lambda i: (0, 2 * GW // PW)),
                  pl.BlockSpec((NG, LANES, LANES), lambda i: (0, 0, 0)), pl.BlockSpec((1, PW), lambda i: (0, 0))],
        out_specs=pl.BlockSpec((s, PW), lambda i: (0, 0)),
        out_shape=jax.ShapeDtypeStruct((s, PW), BF16),
        scratch_shapes=[pltpu.VMEM((s + POOL_HALO, LANES), F32)], compiler_params=_cparams(),
    )(z, wp, sp)


def _pool_bwd(z, dy, wp, sp):
    s = z.shape[0]
    nt = s // ROW_TILE

    def body(p_ref, dy_ref, wp_ref, sp_ref, dp_ref, dwp_ref, dsp_ref, ppad, rpad, dpool):
        for g in range(NG):
            cols = slice(g * LANES, (g + 1) * LANES)
            ppad[pl.ds(0, POOL_HALO), :] = jnp.zeros((POOL_HALO, LANES), F32)
            ppad[pl.ds(POOL_HALO, s), :] = p_ref[:, cols]
            rpad[pl.ds(s, POOL_HALO), :] = jnp.zeros((POOL_HALO, LANES), F32)
            wpb = wp_ref[g].astype(BF16)
            scale = sp_ref[:, cols]

            def tile(t, carry):
                dwp, dsp = carry
                t0 = pl.multiple_of(t * ROW_TILE, ROW_TILE)
                pooled = _pool_pooled(ppad, t0, g)
                pb = pooled.astype(BF16)
                dyt = dy_ref[pl.ds(t0, ROW_TILE), cols]
                dsp = dsp + jnp.sum(dyt * _dot(pb, wpb, NN), axis=0, keepdims=True)
                dmm = (dyt * scale).astype(BF16)
                dwp = dwp + _dot(pb, dmm, TN)
                dpooled = _dot(dmm, wpb, NT)
                rpad[pl.ds(t0, ROW_TILE), :] = dpooled / _pool_count(t0, POOL_WINDOWS[g])
                dpool[pl.ds(t0, ROW_TILE), :] = dpooled
                return dwp, dsp

            dwp, dsp = lax.fori_loop(0, nt, tile, (jnp.zeros((LANES, LANES), F32), jnp.zeros((1, LANES), F32)))
            dwp_ref[g] = dwp
            dsp_ref[:, cols] = dsp

            def tile2(t, carry):
                t0 = pl.multiple_of(t * ROW_TILE, ROW_TILE)
                win = rpad[pl.ds(t0, ROW_TILE + POOL_HALO), :]
                back = _window_sum(win, g + 1, True)[:ROW_TILE]
                rows = pl.ds(t0, ROW_TILE)
                dp_ref[rows, cols] = (back - dpool[rows, :]).astype(BF16)
                return carry

            lax.fori_loop(0, nt, tile2, 0)

    return pl.pallas_call(
        body, name="pool_bwd", grid=(1,),
        in_specs=[pl.BlockSpec((s, PW), lambda i: (0, 2 * GW // PW)), pl.BlockSpec((s, PW), lambda i: (0, GW // PW)),
                  pl.BlockSpec((NG, LANES, LANES), lambda i: (0, 0, 0)), pl.BlockSpec((1, PW), lambda i: (0, 0))],
        out_specs=[pl.BlockSpec((s, PW), lambda i: (0, 0)), pl.BlockSpec((NG, LANES, LANES), lambda i: (0, 0, 0)),
                   pl.BlockSpec((1, PW), lambda i: (0, 0))],
        out_shape=[jax.ShapeDtypeStruct((s, PW), BF16), jax.ShapeDtypeStruct((NG, LANES, LANES), F32),
                   jax.ShapeDtypeStruct((1, PW), F32)],
        scratch_shapes=[pltpu.VMEM((s + POOL_HALO, LANES), F32), pltpu.VMEM((s + POOL_HALO, LANES), F32),
                        pltpu.VMEM((s, LANES), F32)],
        compiler_params=_cparams(),
    )(z, dy, wp, sp)


CONV_LEAD = CONV_HALO - (CONV_K - 1)


def _conv_taps(win, wdw_ref, lead, reverse):
    n = win.shape[0]
    acc = jnp.zeros((ROW_TILE, CW), F32)
    for j in range(CONV_K):
        tap = (CONV_K - 1 - j) if reverse else j
        acc = acc + wdw_ref[tap:tap + 1, :] * pltpu.roll(win, (n - (lead + j)) % n, 0)[:ROW_TILE]
    return acc


def _conv_fill_glu(cv_ref, cg_ref, xpad, s):
    xpad[pl.ds(0, CONV_HALO), :] = jnp.zeros((CONV_HALO, CW), F32)

    def fill(t, carry):
        t0 = pl.multiple_of(t * ROW_TILE, ROW_TILE)
        rows = pl.ds(t0, ROW_TILE)
        xpad[pl.ds(t0 + CONV_HALO, ROW_TILE), :] = _glu(cv_ref[rows, :], cg_ref[rows, :])
        return carry

    lax.fori_loop(0, s // ROW_TILE, fill, 0)


def _conv_fwd(z, wdw, bdw, lng, lnb):
    s = z.shape[0]

    def body(cv_ref, cg_ref, wdw_ref, bdw_ref, lng_ref, lnb_ref, y_ref, xpad):
        _conv_fill_glu(cv_ref, cg_ref, xpad, s)

        def tile(t, carry):
            t0 = pl.multiple_of(t * ROW_TILE, ROW_TILE)
            win = xpad[pl.ds(t0, ROW_TILE + CONV_HALO), :]
            hc = _conv_taps(win, wdw_ref, CONV_LEAD, False) + bdw_ref[...]
            y_ref[pl.ds(t0, ROW_TILE), :] = _ln_silu(hc, lng_ref[...], lnb_ref[...]).astype(BF16)
            return carry

        lax.fori_loop(0, s // ROW_TILE, tile, 0)

    vec = pl.BlockSpec((1, CW), lambda i: (0, 0))
    return pl.pallas_call(
        body, name="conv_fwd", grid=(1,),
        in_specs=[pl.BlockSpec((s, CW), lambda i: (0, (2 * GW + PW) // CW)),
                  pl.BlockSpec((s, CW), lambda i: (0, (2 * GW + PW) // CW + 1)),
                  pl.BlockSpec((CONV_K + 1, CW), lambda i: (0, 0)), vec, vec, vec],
        out_specs=pl.BlockSpec((s, CW), lambda i: (0, 0)),
        out_shape=jax.ShapeDtypeStruct((s, CW), BF16),
        scratch_shapes=[pltpu.VMEM((s + CONV_HALO, CW), F32)], compiler_params=_cparams(),
    )(z, z, wdw, bdw, lng, lnb)


def _conv_bwd(z, dy, wdw, bdw, lng, lnb):
    s = z.shape[0]

    def body(cv_ref, cg_ref, dy_ref, wdw_ref, bdw_ref, lng_ref, lnb_ref,
             dcv_ref, dcg_ref, dwdw_ref, dbdw_ref, dlng_ref, dlnb_ref, xpad, dpad):
        _conv_fill_glu(cv_ref, cg_ref, xpad, s)
        dpad[pl.ds(s, CONV_HALO), :] = jnp.zeros((CONV_HALO, CW), F32)
        dwdw_ref[...] = jnp.zeros((CONV_K + 1, CW), F32)

        def tile(t, carry):
            db, dg, dbeta = carry
            t0 = pl.multiple_of(t * ROW_TILE, ROW_TILE)
            win = xpad[pl.ds(t0, ROW_TILE + CONV_HALO), :]
            hc = _conv_taps(win, wdw_ref, CONV_LEAD, False) + bdw_ref[...]
            _, vjp = jax.vjp(_ln_silu, hc, lng_ref[...], lnb_ref[...])
            dhc, dg_t, dbeta_t = vjp(dy_ref[pl.ds(t0, ROW_TILE), :])
            dpad[pl.ds(t0, ROW_TILE), :] = dhc
            n = win.shape[0]
            for j in range(CONV_K):
                shifted = pltpu.roll(win, (n - (CONV_LEAD + j)) % n, 0)[:ROW_TILE]
                dwdw_ref[j:j + 1, :] += jnp.sum(dhc * shifted, axis=0, keepdims=True)
            return db + jnp.sum(dhc, axis=0, keepdims=True), dg + dg_t, dbeta + dbeta_t

        zero = jnp.zeros((1, CW), F32)
        db, dg, dbeta = lax.fori_loop(0, s // ROW_TILE, tile, (zero, zero, zero))
        dbdw_ref[...] = db
        dlng_ref[...] = dg
        dlnb_ref[...] = dbeta

        def tile2(t, carry):
            t0 = pl.multiple_of(t * ROW_TILE, ROW_TILE)
            rows = pl.ds(t0, ROW_TILE)
            win = dpad[pl.ds(t0, ROW_TILE + CONV_HALO), :]
            dglu = _conv_taps(win, wdw_ref, 0, True)
            _, vjp = jax.vjp(_glu, cv_ref[rows, :], cg_ref[rows, :])
            dcv, dcg = vjp(dglu)
            dcv_ref[rows, :] = dcv.astype(BF16)
            dcg_ref[rows, :] = dcg.astype(BF16)
            return carry

        lax.fori_loop(0, s // ROW_TILE, tile2, 0)

    vec = pl.BlockSpec((1, CW), lambda i: (0, 0))
    full = pl.BlockSpec((s, CW), lambda i: (0, 0))
    wspec = pl.BlockSpec((CONV_K + 1, CW), lambda i: (0, 0))
    vshape = jax.ShapeDtypeStruct((1, CW), F32)
    return pl.pallas_call(
        body, name="conv_bwd", grid=(1,),
        in_specs=[pl.BlockSpec((s, CW), lambda i: (0, (2 * GW + PW) // CW)),
                  pl.BlockSpec((s, CW), lambda i: (0, (2 * GW + PW) // CW + 1)),
                  pl.BlockSpec((s, CW), lambda i: (0, (GW + PW) // CW)), wspec, vec, vec, vec],
        out_specs=[full, full, wspec, vec, vec, vec],
        out_shape=[jax.ShapeDtypeStruct((s, CW), BF16), jax.ShapeDtypeStruct((s, CW), BF16),
                   jax.ShapeDtypeStruct((CONV_K + 1, CW), F32), vshape, vshape, vshape],
        scratch_shapes=[pltpu.VMEM((s + CONV_HALO, CW), F32), pltpu.VMEM((s + CONV_HALO, CW), F32)],
        compiler_params=_cparams(),
    )(z, z, dy, wdw, bdw, lng, lnb)


def _softmax_rows(sc):
    e = jnp.exp(sc - jnp.max(sc, axis=-1, keepdims=True))
    return e / jnp.sum(e, axis=-1, keepdims=True)


def _attn_fwd(q, k, v, tq):
    s, m = q.shape[0], k.shape[0]
    tq = min(tq, s)

    def body(q_ref, k_ref, v_ref, o_ref):
        for h in range(XH):
            cols = slice(h * XHD, (h + 1) * XHD)
            p = _softmax_rows(_dot(q_ref[:, cols], k_ref[:, cols], NT) * ATT_SCALE)
            o_ref[:, cols] = _dot(p.astype(BF16), v_ref[:, cols], NN).astype(BF16)

    kv = pl.BlockSpec((m, D), lambda i: (0, 0))
    return pl.pallas_call(
        body, name="attn_fwd", grid=(s // tq,),
        in_specs=[pl.BlockSpec((tq, D), lambda i: (i, 0)), kv, kv],
        out_specs=pl.BlockSpec((tq, D), lambda i: (i, 0)),
        out_shape=jax.ShapeDtypeStruct((s, D), BF16), compiler_params=_cparams(),
    )(q, k, v)


def _attn_bwd(q, k, v, do, tq):
    s, m = q.shape[0], k.shape[0]
    tq = min(tq, s)

    def body(q_ref, k_ref, v_ref, do_ref, dq_ref, dk_ref, dv_ref):
        @pl.when(pl.program_id(0) == 0)
        def _():
            dk_ref[...] = jnp.zeros_like(dk_ref)
            dv_ref[...] = jnp.zeros_like(dv_ref)

        for h in range(XH):
            cols = slice(h * XHD, (h + 1) * XHD)
            qh, kh, vh, doh = q_ref[:, cols], k_ref[:, cols], v_ref[:, cols], do_ref[:, cols]
            p = _softmax_rows(_dot(qh, kh, NT) * ATT_SCALE)
            dp = _dot(doh, vh, NT)
            dv_ref[:, cols] += _dot(p.astype(BF16), doh, TN)
            ds = (p * (dp - jnp.sum(p * dp, axis=-1, keepdims=True)) * ATT_SCALE).astype(BF16)
            dq_ref[:, cols] = _dot(ds, kh, NN).astype(BF16)
            dk_ref[:, cols] += _dot(ds, qh, TN)

    kv = pl.BlockSpec((m, D), lambda i: (0, 0))
    qs = pl.BlockSpec((tq, D), lambda i: (i, 0))
    return pl.pallas_call(
        body, name="attn_bwd", grid=(s // tq,),
        in_specs=[qs, kv, kv, qs], out_specs=[qs, kv, kv],
        out_shape=[jax.ShapeDtypeStruct((s, D), BF16), jax.ShapeDtypeStruct((m, D), F32),
                   jax.ShapeDtypeStruct((m, D), F32)],
        compiler_params=_cparams(),
    )(q, k, v, do)


def _loss_head(y, target, tm):
    s = y.shape[0]
    tm = min(tm, s)

    def body(y_ref, t_ref, dy_ref, part_ref):
        err = y_ref[...] - t_ref[...]
        dy_ref[...] = err * (1.0 / D)
        part_ref[...] = jnp.full((1, 8, LANES), 0.5 * jnp.sum(err * err) * (1.0 / D), F32)

    blk = pl.BlockSpec((tm, D), lambda i: (i, 0))
    dy, part = pl.pallas_call(
        body, name="loss_head", grid=(s // tm,), in_specs=[blk, blk],
        out_specs=[blk, pl.BlockSpec((1, 8, LANES), lambda i: (i, 0, 0))],
        out_shape=[jax.ShapeDtypeStruct((s, D), F32), jax.ShapeDtypeStruct((s // tm, 8, LANES), F32)],
        compiler_params=_cparams(),
    )(y, target)
    return dy, jnp.sum(part[:, 0, 0])


def _layer_fwd(x0, mem, w, p, fetch):
    z, hn0 = _rowop_mm("mix_in", "rms", (x0,), p["norm_mix_pre"], w["w_in"], NT, F32, 512, 512)
    ya = _gmlp_fwd(z, p["gmlp_v_gain"], p["w_spatial"], p["b_spatial_t"], 512)
    yb = _pool_fwd(z, p["w_pool"], p["s_pool"])
    yc = _conv_fwd(z, p["w_dw"], p["b_dw"], p["conv_ln_g"], p["conv_ln_b"])
    y = jnp.concatenate([ya, yb, yc], axis=1)
    w.update(fetch("att", (y,)))
    x1, h0 = _mm_rowop("mix_out", "rms_res", [(y, w["w_out"], NN)], (x0,), p["norm_mix_post"], 512, 512)
    q, hn1 = _rowop_mm("att_q", "rms", (x1,), p["norm_xattn_pre"], w["w_q"], NN, BF16, 512, 512)
    k, mn = _rowop_mm("att_k", "rms", (mem,), p["norm_mem"], w["w_k"], NN, BF16, 256, 512)
    v, _ = _rowop_mm("att_v", "rms", (mem,), p["norm_mem"], w["w_v"], NN, BF16, 256, 512)
    o = _attn_fwd(q, k, v, 256)
    x2, h1 = _mm_rowop("att_o", "rms_res", [(o, w["w_o"], NN)], (x1,), p["norm_xattn_post"], 512, 512)
    w.update(fetch("ffn", (x2,)))
    u, hn2 = _rowop_mm("ffn_up", "rms", (x2,), p["norm_ffn_pre"], w["w_up"], NT, F32, 512, 512)
    x3, h2 = _mm_rowop("ffn_down", "rms_res", [(u, w["w_down"], NN)], (x2,), p["norm_ffn_post"], 512, 512, relu2=True)
    saved = dict(x0=x0, z=z, hn0=hn0, y=y, h0=h0, x1=x1, q=q, hn1=hn1, k=k, v=v, mn=mn, o=o, h1=h1, x2=x2, u=u,
                 hn2=hn2, h2=h2)
    return x3, saved


def _layer_bwd(dx3, mem, w, p, sv, red):
    gs = {}
    du, dh2, dg = _rowop_mm("ffn_down_bwd", "rms_bwd", (sv["h2"], dx3), p["norm_ffn_post"], w["w_down"], NT, BF16,
                            512, 512, u=sv["u"], after=red.after())
    gs["norm_ffn_post"] = jnp.sum(dg, axis=0)
    g_down = _mm_tn("ffn_down_dw", sv["u"], dh2, 512, 512, relu2=True)
    dx2, dg = _mm_rowop("ffn_up_bwd", "rms_bwd_res", [(du, w["w_up"], NN)], (sv["x2"], dx3), p["norm_ffn_pre"], 512, 512)
    gs["norm_ffn_pre"] = jnp.sum(dg, axis=0)
    g_up = _mm_tn("ffn_up_dw", du, sv["hn2"], 512, 512)
    red.add("ffn", ("w_down", "w_up"), [g_down, g_up])
    do, dh1, dg = _rowop_mm("att_o_bwd", "rms_bwd", (sv["h1"], dx2), p["norm_xattn_post"], w["w_o"], NT, BF16, 512, 512,
                            after=red.after())
    gs["norm_xattn_post"] = jnp.sum(dg, axis=0)
    g_o = _mm_tn("att_o_dw", sv["o"], dh1, 512, 512)
    red.advance((g_o,))
    dq, dk, dv = _attn_bwd(sv["q"], sv["k"], sv["v"], do, 256)
    dk, dv = dk.astype(BF16), dv.astype(BF16)
    dx1, dg = _mm_rowop("att_q_bwd", "rms_bwd_res", [(dq, w["w_q"], NT)], (sv["x1"], dx2), p["norm_xattn_pre"], 512, 512,
                        after=red.after())
    gs["norm_xattn_pre"] = jnp.sum(dg, axis=0)
    g_q = _mm_tn("att_q_dw", sv["hn1"], dq, 512, 512)
    g_k = _mm_tn("att_k_dw", sv["mn"], dk, 512, 256)
    g_v = _mm_tn("att_v_dw", sv["mn"], dv, 512, 256)
    (dg,) = _mm_rowop("att_kv_bwd", "rms_bwd_gain", [(dk, w["w_k"], NT), (dv, w["w_v"], NT)], (mem,), p["norm_mem"],
                      256, 512)
    gs["norm_mem"] = jnp.sum(dg, axis=0)
    red.add("att", ("w_o", "w_q", "w_k", "w_v"), [g_o, g_q, g_k, g_v])
    dy, dh0, dg = _rowop_mm("mix_out_bwd", "rms_bwd", (sv["h0"], dx1), p["norm_mix_post"], w["w_out"], NT, F32, 512, 512,
                            after=red.after())
    gs["norm_mix_post"] = jnp.sum(dg, axis=0)
    g_out = _mm_tn("mix_out_dw", sv["y"], dh0, 512, 512)
    red.advance((g_out,))
    z = sv["z"]
    dzu, dzv, dgv, dws, dbs = _gmlp_bwd(z, dy, p["gmlp_v_gain"], p["w_spatial"], p["b_spatial_t"], 512)
    gs["gmlp_v_gain"] = jnp.sum(dgv, axis=0)
    gs["w_spatial"] = jnp.sum(dws, axis=0)
    gs["b_spatial"] = jnp.sum(dbs[..., 0], axis=0)
    dp, gs["w_pool"], gs["s_pool"] = _pool_bwd(z, dy, p["w_pool"], p["s_pool"])
    dcv, dcg, dwdw, gs["b_dw"], gs["conv_ln_g"], gs["conv_ln_b"] = _conv_bwd(
        z, dy, p["w_dw"], p["b_dw"], p["conv_ln_g"], p["conv_ln_b"])
    gs["w_dw"] = dwdw[:CONV_K]
    dz = jnp.concatenate([dzu, dzv, dp, dcv, dcg], axis=1)
    dx0, dg = _mm_rowop("mix_in_bwd", "rms_bwd_res", [(dz, w["w_in"], NN)], (sv["x0"], dx1), p["norm_mix_pre"], 512, 512,
                        after=red.after())
    gs["norm_mix_pre"] = jnp.sum(dg, axis=0)
    g_in = _mm_tn("mix_in_dw", dz, sv["hn0"], 512, 512)
    red.add("mix", ("w_out", "w_in"), [g_out, g_in])
    return dx0, gs


def _layer_params(small, l):
    p = {n: small[n][l].reshape(1, -1) for n in ("norm_mix_pre", "norm_mix_post", "s_pool", "b_dw", "conv_ln_g",
                                                   "conv_ln_b", "norm_xattn_pre", "norm_mem", "norm_xattn_post",
                                                   "norm_ffn_pre", "norm_ffn_post")}
    p["gmlp_v_gain"] = small["gmlp_v_gain"][l]
    p["w_spatial"] = small["w_spatial"][l]
    p["b_spatial_t"] = small["b_spatial"][l].T
    p["w_pool"] = small["w_pool"][l]
    p["w_dw"] = jnp.pad(small["w_dw"][l], ((0, 1), (0, 0)))
    return p


def _local_step(x, mem, target, fetch, small, red):
    small = dict(small)
    saved, weights, params = [], [], []
    h = x
    marker = ()
    for l in range(DEPTH):
        w = fetch(l, "in", marker)
        if "taps" in w:
            small["w_dw"] = w.pop("taps")
        p = _layer_params(small, l)
        h, sv = _layer_fwd(h, mem, w, p, functools.partial(fetch, l))
        marker = (h,)
        saved.append(sv)
        weights.append(w)
        params.append(p)
    dh, loss = _loss_head(h, target, 512)
    gsmall = [None] * DEPTH
    for l in reversed(range(DEPTH)):
        red.layer = l
        dh, gs = _layer_bwd(dh, mem, weights[l], params[l], saved[l], red)
        gsmall[l] = {n: gs[n].reshape(small[n].shape[1:]) for n in SMALL}
    return loss, dh, gsmall


HBM = pl.BlockSpec(memory_space=pltpu.HBM)


def _position():
    return lax.axis_index("x"), lax.axis_index("y"), lax.axis_index("c")


def _all_gather(name, shards):
    n = len(shards)

    def body(*refs):
        ins, outs = refs[:n], refs[n:2 * n]
        send_sems, recv_sems, local_sems = refs[2 * n:]
        x, y, c = _position()
        me, sibling = (x, y, c), (x, y, 1 - c)
        chips = [(1 - x, y), (x, 1 - y), (1 - x, 1 - y)]

        def rows(a, dev):
            r = shards[a].shape[0]
            return outs[a].at[pl.ds((4 * dev[0] + 2 * dev[1] + dev[2]) * r, r), :]

        def copy(a, k, block, to, src=None):
            return pltpu.make_async_remote_copy(
                src_ref=rows(a, block) if src is None else src, dst_ref=rows(a, block),
                send_sem=send_sems.at[a, k], recv_sem=recv_sems.at[a, k], device_id=to, device_id_type=MESH)

        started = []
        for a in range(n):
            mine = pltpu.make_async_copy(ins[a], rows(a, me), local_sems.at[a])
            mine.start()
            started.append(mine)
        first = []
        for a in range(n):
            for j, chip in enumerate(chips):
                first.append(copy(a, 1 + j, me, (*chip, c), src=ins[a]))
        for a in range(n):
            first.append(copy(a, 0, me, sibling, src=ins[a]))
        for cp in first:
            cp.start()
        passed = []
        for a in range(n):
            for j, chip in enumerate(chips):
                copy(a, 1 + j, (*chip, c), me).wait_recv()
                fwd = copy(a, 4 + j, (*chip, c), sibling)
                fwd.start()
                passed.append(fwd)
        for a in range(n):
            copy(a, 0, sibling, me).wait_recv()
            for j, chip in enumerate(chips):
                copy(a, 4 + j, (*chip, 1 - c), me).wait_recv()
        for cp in first + passed:
            cp.wait_send()
        for mine in started:
            mine.wait()

    return pl.pallas_call(
        body, name=name, in_specs=[HBM] * n, out_specs=[HBM] * n,
        out_shape=[jax.ShapeDtypeStruct((N_DEV * s.shape[0], s.shape[1]), s.dtype) for s in shards],
        scratch_shapes=[pltpu.SemaphoreType.DMA((n, 7)), pltpu.SemaphoreType.DMA((n, 7)), pltpu.SemaphoreType.DMA((n,))],
    )(*shards)


SEM = pl.BlockSpec(memory_space=pltpu.SEMAPHORE)
EFFECT = pltpu.SideEffectType.DATAFLOW_SIDE_EFFECTING
TOKEN = jax.ShapeDtypeStruct((8, LANES), F32)
TOKEN_SPEC = pl.BlockSpec(memory_space=pltpu.VMEM)


def _landing(shape, dtype):
    return pltpu.with_memory_space_constraint(lax.empty(shape, dtype), pltpu.HBM)


def _hbm_shapes(arrays):
    return [pltpu.HBM(a.shape, a.dtype) for a in arrays]


def _block(ref, r, dev):
    return ref.at[pl.ds((4 * dev[0] + 2 * dev[1] + dev[2]) * r, r), :]


def _split_call(name, body, thru, sems_in, after, sems_out, token):
    n = len(thru)
    out_shape = [pltpu.SemaphoreType.DMA(s) for s in sems_out] + _hbm_shapes(thru) + ([TOKEN] if token else [])
    out_specs = [SEM] * len(sems_out) + [HBM] * n + ([TOKEN_SPEC] if token else [])
    return pl.pallas_call(
        body, name=name, in_specs=[HBM] * n + [SEM] * len(sems_in) + [ANY] * len(after),
        out_specs=out_specs, out_shape=out_shape,
        input_output_aliases={i: len(sems_out) + i for i in range(n)},
        compiler_params=pltpu.CompilerParams(has_side_effects=EFFECT),
    )(*thru, *sems_in, *after)


def _place_own(name, shards):
    n = len(shards)

    def body(*refs):
        ins, outs, sems = refs[:n], refs[n:2 * n], refs[2 * n]
        me = _position()
        copies = [pltpu.make_async_copy(ins[a], _block(outs[a], shards[a].shape[0], me), sems.at[a]) for a in range(n)]
        for cp in copies:
            cp.start()
        for cp in copies:
            cp.wait()

    return pl.pallas_call(
        body, name=name, in_specs=[HBM] * n, out_specs=[HBM] * n,
        out_shape=[jax.ShapeDtypeStruct((N_DEV * s.shape[0], s.shape[1]), s.dtype) for s in shards],
        scratch_shapes=[pltpu.SemaphoreType.DMA((n,))],
    )(*shards)


def _gather_peers(x, y, c):
    return [(1 - x, y, c), (x, 1 - y, c), (1 - x, 1 - y, c), (x, y, 1 - c)]


def _gather_start(name, shards, lands, after):
    n = len(shards)

    def body(*refs):
        ins, lz = refs[:n], refs[n:2 * n]
        send_sems, recv_sems = refs[2 * n + len(after)], refs[2 * n + len(after) + 1]
        token = refs[-1]
        x, y, c = _position()
        for a in range(n):
            for k, to in enumerate(_gather_peers(x, y, c)):
                pltpu.make_async_remote_copy(
                    src_ref=ins[a], dst_ref=_block(lz[a], shards[a].shape[0], (x, y, c)), send_sem=send_sems.at[k],
                    recv_sem=recv_sems.at[k], device_id=to, device_id_type=MESH).start()
        token[...] = jnp.zeros_like(token)

    out = _split_call(name, body, list(shards) + list(lands), [], after, [(4,), (4,)], True)
    return out[0], out[1], out[2:2 + n], out[2 + n:2 + 2 * n], out[-1]


def _gather_forward(name, shards, lands, recv_sems, after):
    n = len(shards)

    def body(*refs):
        lz = refs[n:2 * n]
        recv0 = refs[2 * n]
        fsend, frecv = refs[2 * n + 1 + len(after)], refs[2 * n + 2 + len(after)]
        token = refs[-1]
        x, y, c = _position()
        chips = _gather_peers(x, y, c)[:3]
        for a in range(n):
            for j, chip in enumerate(chips):
                blk = _block(lz[a], shards[a].shape[0], chip)
                pltpu.make_async_remote_copy(src_ref=blk, dst_ref=blk, send_sem=fsend.at[j], recv_sem=recv0.at[j],
                                             device_id=(x, y, c), device_id_type=MESH).wait_recv()
        for a in range(n):
            for j, chip in enumerate(chips):
                blk = _block(lz[a], shards[a].shape[0], chip)
                pltpu.make_async_remote_copy(src_ref=blk, dst_ref=blk, send_sem=fsend.at[j], recv_sem=frecv.at[j],
                                             device_id=(x, y, 1 - c), device_id_type=MESH).start()
        token[...] = jnp.zeros_like(token)

    out = _split_call(name, body, list(shards) + list(lands), [recv_sems], after, [(3,), (3,)], True)
    return out[0], out[1], out[2:2 + n], out[2 + n:2 + 2 * n], out[-1]


def _gather_finish(name, shards, lands, send_sems, recv_sems, fsend, frecv, after):
    n = len(shards)

    def body(*refs):
        ins, lz = refs[:n], refs[n:2 * n]
        send0, recv0, fsend_ref, frecv_ref = refs[2 * n:2 * n + 4]
        x, y, c = _position()
        me = (x, y, c)
        chips = _gather_peers(x, y, c)[:3]
        for a in range(n):
            r = shards[a].shape[0]
            own = _block(lz[a], r, (x, y, 1 - c))
            pltpu.make_async_remote_copy(src_ref=own, dst_ref=own, send_sem=send0.at[3], recv_sem=recv0.at[3],
                                         device_id=me, device_id_type=MESH).wait_recv()
            for j, chip in enumerate(chips):
                blk = _block(lz[a], r, (chip[0], chip[1], 1 - c))
                pltpu.make_async_remote_copy(src_ref=blk, dst_ref=blk, send_sem=fsend_ref.at[j], recv_sem=frecv_ref.at[j],
                                             device_id=me, device_id_type=MESH).wait_recv()
            for k in range(4):
                pltpu.make_async_remote_copy(src_ref=ins[a], dst_ref=_block(lz[a], r, me), send_sem=send0.at[k],
                                             recv_sem=recv0.at[k], device_id=me, device_id_type=MESH).wait_send()
            for j, chip in enumerate(chips):
                blk = _block(lz[a], r, chip)
                pltpu.make_async_remote_copy(src_ref=blk, dst_ref=blk, send_sem=fsend_ref.at[j], recv_sem=frecv_ref.at[j],
                                             device_id=me, device_id_type=MESH).wait_send()

    out = _split_call(name, body, list(shards) + list(lands), [send_sems, recv_sems, fsend, frecv], after, [], False)
    return out[n:2 * n]


def _sibling_start(name, grads, after):
    n = len(grads)
    lands = [_landing((4, g.shape[0] // N_DEV, D), g.dtype) for g in grads]

    def body(*refs):
        ins, lz = refs[:n], refs[n:2 * n]
        send_sem, recv_sem = refs[2 * n + len(after)], refs[2 * n + len(after) + 1]
        token = refs[-1]
        x, y, c = _position()
        for a in range(n):
            r = grads[a].shape[0] // N_DEV
            for q in range(4):
                pltpu.make_async_remote_copy(
                    src_ref=ins[a].at[pl.ds((2 * q + 1 - c) * r, r), :], dst_ref=lz[a].at[q], send_sem=send_sem.at[0],
                    recv_sem=recv_sem.at[0], device_id=(x, y, 1 - c), device_id_type=MESH).start()
        token[...] = jnp.zeros_like(token)

    out = _split_call(name, body, list(grads) + lands, [], after, [(1,), (1,)], True)
    return out[0], out[1], out[2:2 + n], out[2 + n:2 + 2 * n], out[-1]


def _sibling_finish(name, grads, lands, send_sem, recv_sem, after):
    n = len(grads)

    def body(*refs):
        ins, lz = refs[:n], refs[n:2 * n]
        send_ref, recv_ref = refs[2 * n], refs[2 * n + 1]
        x, y, c = _position()
        for a in range(n):
            r = grads[a].shape[0] // N_DEV
            for q in range(4):
                cp = pltpu.make_async_remote_copy(
                    src_ref=ins[a].at[pl.ds((2 * q + 1 - c) * r, r), :], dst_ref=lz[a].at[q], send_sem=send_ref.at[0],
                    recv_sem=recv_ref.at[0], device_id=(x, y, c), device_id_type=MESH)
                cp.wait_send()
                cp.wait_recv()

    out = _split_call(name, body, list(grads) + list(lands), [send_sem, recv_sem], after, [], False)
    return out[:n], out[n:2 * n]


def _chip_start(name, parts, after):
    n = len(parts)
    lands = [_landing((3,) + p.shape[1:], p.dtype) for p in parts]

    def body(*refs):
        ins, lz = refs[:n], refs[n:2 * n]
        send_sems, recv_sems = refs[2 * n + len(after)], refs[2 * n + len(after) + 1]
        token = refs[-1]
        x, y, c = _position()
        for a in range(n):
            for j, chip in enumerate(_gather_peers(x, y, c)[:3]):
                pltpu.make_async_remote_copy(
                    src_ref=ins[a].at[2 * chip[0] + chip[1]], dst_ref=lz[a].at[j], send_sem=send_sems.at[j],
                    recv_sem=recv_sems.at[j], device_id=chip, device_id_type=MESH).start()
        token[...] = jnp.zeros_like(token)

    out = _split_call(name, body, list(parts) + lands, [], after, [(3,), (3,)], True)
    return out[0], out[1], out[2:2 + n], out[2 + n:2 + 2 * n], out[-1]


def _chip_finish(name, parts, lands, send_sems, recv_sems, after):
    n = len(parts)

    def body(*refs):
        ins, lz = refs[:n], refs[n:2 * n]
        send_ref, recv_ref = refs[2 * n], refs[2 * n + 1]
        me = _position()
        for a in range(n):
            for j in range(3):
                cp = pltpu.make_async_remote_copy(
                    src_ref=ins[a].at[j], dst_ref=lz[a].at[j], send_sem=send_ref.at[j], recv_sem=recv_ref.at[j],
                    device_id=me, device_id_type=MESH)
                cp.wait_send()
                cp.wait_recv()

    out = _split_call(name, body, list(parts) + list(lands), [send_sems, recv_sems], after, [], False)
    return out[:n], out[n:2 * n]


def _row_tile(r, target):
    return max(t for t in range(16, min(r, target) + 1, 16) if r % t == 0)


def _chip_partial(name, grad, got, c, tr):
    r = grad.shape[0] // N_DEV
    tr = _row_tile(r, tr)
    g4 = grad.reshape(4, 2, r, D)

    def body(c_ref, g_ref, s_ref, o_ref):
        o_ref[...] = (g_ref[...].astype(F32) + s_ref[...].astype(F32)).astype(BF16)

    return pl.pallas_call(
        body, name=name,
        grid_spec=pltpu.PrefetchScalarGridSpec(
            num_scalar_prefetch=1, grid=(4, r // tr),
            in_specs=[pl.BlockSpec((None, None, tr, D), lambda q, i, c_ref: (q, c_ref[0], i, 0)),
                      pl.BlockSpec((None, tr, D), lambda q, i, c_ref: (q, i, 0))],
            out_specs=pl.BlockSpec((None, tr, D), lambda q, i, c_ref: (q, i, 0))),
        out_shape=jax.ShapeDtypeStruct((4, r, D), BF16), compiler_params=_cparams(),
    )(c, g4, got)


def _chip_sum(name, part, got, chip, tr):
    r = part.shape[1]
    tr = _row_tile(r, tr)

    def body(q_ref, p_ref, g_ref, o_ref):
        acc = p_ref[...].astype(F32)
        for j in range(3):
            acc = acc + g_ref[j].astype(F32)
        o_ref[...] = acc

    return pl.pallas_call(
        body, name=name,
        grid_spec=pltpu.PrefetchScalarGridSpec(
            num_scalar_prefetch=1, grid=(r // tr,),
            in_specs=[pl.BlockSpec((None, tr, D), lambda i, q_ref: (q_ref[0], i, 0)),
                      pl.BlockSpec((3, tr, D), lambda i, q_ref: (0, i, 0))],
            out_specs=pl.BlockSpec((tr, D), lambda i, q_ref: (i, 0))),
        out_shape=jax.ShapeDtypeStruct((r, D), F32), compiler_params=_cparams(),
    )(chip, part, got)


def _sum_blocks(name, gathered, r):
    def body(g_ref, o_ref):
        acc = g_ref[0]
        for d in range(1, N_DEV):
            acc = acc + g_ref[d]
        o_ref[...] = acc

    tr = 8
    return pl.pallas_call(
        body, name=name, grid=(r // tr,),
        in_specs=[pl.BlockSpec((N_DEV, tr, D), lambda i: (0, i, 0))],
        out_specs=pl.BlockSpec((tr, D), lambda i: (i, 0)),
        out_shape=jax.ShapeDtypeStruct((r, D), F32), compiler_params=_cparams(),
    )(gathered.reshape(N_DEV, r, D))


class _WeightGather:
    def __init__(self, groups):
        lands = _place_own("gather_place_own", [s for _, _, shards in groups for s in shards])
        self.state, token, off = {}, (), 0
        for key, names, shards in groups:
            n = len(shards)
            send, recv, shards, lz, tok = _gather_start("gather_start_%s_%d" % key[::-1], shards, lands[off:off + n], token)
            self.state[key] = (names, send, recv, shards, lz)
            token, off = (tok,), off + n
        self.started = token

    def fetch(self, layer, group, marker):
        names, send, recv, shards, lz = self.state.pop((layer, group))
        tag = "%s_%d" % (group, layer)
        fsend, frecv, shards, lz, tok = _gather_forward("gather_forward_" + tag, shards, lz, recv, marker or self.started)
        lz = _gather_finish("gather_finish_" + tag, shards, lz, send, recv, fsend, frecv, (tok,))
        return dict(zip(names, lz))


class _GradReduce:
    def __init__(self, core, chip):
        self.core, self.chip = core, chip
        self.layer = None
        self.token = ()
        self.at_sibling, self.at_chips = [], []

    def after(self):
        return self.token

    def add(self, group, names, grads):
        tag = "%s_%d" % (group, self.layer)
        send, recv, grads, lands, tok = _sibling_start("grad_sibling_start_" + tag, grads, self.token)
        self.at_sibling.append((tag, [(self.layer, n) for n in names], send, recv, grads, lands))
        self.token = (tok,)

    def advance(self, marker):
        for tag, keys, send, recv, grads, lands in self.at_sibling:
            grads, lands = _sibling_finish("grad_sibling_finish_" + tag, grads, lands, send, recv, marker)
            parts = [_chip_partial("chip_partial_%d_%s" % key, g, got, self.core, 256)
                     for key, g, got in zip(keys, grads, lands)]
            send, recv, parts, lands, tok = _chip_start("grad_chip_start_" + tag, parts, ())
            self.at_chips.append((tag, keys, send, recv, parts, lands))
            self.token = (tok,)
        self.at_sibling = []

    def finish(self, marker):
        self.advance(marker)
        grads = {}
        for tag, keys, send, recv, parts, lands in self.at_chips:
            parts, lands = _chip_finish("grad_chip_finish_" + tag, parts, lands, send, recv, marker)
            for key, part, got in zip(keys, parts, lands):
                grads[key] = _chip_sum("chip_sum_%d_%s" % key, part, got, self.chip, 256)
        return grads


def _adamw_math(w, g, m, v):
    m = ADAM_B1 * m + (1.0 - ADAM_B1) * g
    v = ADAM_B2 * v + (1.0 - ADAM_B2) * jnp.square(g)
    m_hat = m / (1.0 - ADAM_B1 ** ADAM_STEP)
    v_hat = v / (1.0 - ADAM_B2 ** ADAM_STEP)
    delta = -ADAM_LR * (m_hat / (jnp.sqrt(v_hat) + ADAM_EPS) + ADAM_WD * w)
    return delta, m, v


def _adamw(name, w, g, m, v, tr):
    nl, r, cdim = w.shape
    tr = min(tr, r)

    def body(w_ref, g_ref, m_ref, v_ref, d_ref, nm_ref, nv_ref):
        d_ref[...], nm_ref[...], nv_ref[...] = _adamw_math(w_ref[...], g_ref[...], m_ref[...], v_ref[...])

    blk = pl.BlockSpec((None, tr, cdim), lambda l, i: (l, i, 0))
    shape = jax.ShapeDtypeStruct((nl, r, cdim), F32)
    return pl.pallas_call(
        body, name=name, grid=(nl, r // tr), in_specs=[blk] * 4, out_specs=[blk] * 3, out_shape=[shape] * 3,
        compiler_params=_cparams(),
    )(w, g, m, v)


def _to_gather_layout(name, w):
    if name in ("w_in", "w_up"):
        w = w.T
    return w.astype(BF16)


def _from_gather_layout(name, g):
    return g.T if name in ("w_in", "w_up") else g


def _pack(arrays, rows):
    flat = jnp.concatenate([a.reshape(-1) for a in arrays])
    return jnp.pad(flat, (0, rows * D - flat.shape[0])).reshape(rows, D)


def _unpack(buf, shapes):
    flat = buf.reshape(-1)
    out, off = [], 0
    for shp in shapes:
        size = 1
        for dim in shp:
            size *= dim
        out.append(flat[off:off + size].reshape(shp))
        off += size
    return out


def _rows_for(shapes, mult=8):
    total = 0
    for shp in shapes:
        size = 1
        for dim in shp:
            size *= dim
        total += size
    return -(-total // (mult * D)) * mult


def kernel(x, mem, norm_mix_pre, norm_mix_post, w_in, w_out, gmlp_v_gain, w_spatial, b_spatial, w_pool, s_pool, w_dw, b_dw, conv_ln_g, conv_ln_b, norm_xattn_pre, norm_mem, norm_xattn_post, w_q, w_k, w_v, w_o, norm_ffn_pre, norm_ffn_post, w_up, w_down, loss_target, m_norm_mix_pre, m_norm_mix_post, m_w_in, m_w_out, m_gmlp_v_gain, m_w_spatial, m_b_spatial, m_w_pool, m_s_pool, m_w_dw, m_b_dw, m_conv_ln_g, m_conv_ln_b, m_norm_xattn_pre, m_norm_mem, m_norm_xattn_post, m_w_q, m_w_k, m_w_v, m_w_o, m_norm_ffn_pre, m_norm_ffn_post, m_w_up, m_w_down, v_norm_mix_pre, v_norm_mix_post, v_w_in, v_w_out, v_gmlp_v_gain, v_w_spatial, v_b_spatial, v_w_pool, v_s_pool, v_w_dw, v_b_dw, v_conv_ln_g, v_conv_ln_b, v_norm_xattn_pre, v_norm_mem, v_norm_xattn_post, v_w_q, v_w_k, v_w_v, v_w_o, v_norm_ffn_pre, v_norm_ffn_post, v_w_up, v_w_down):
    args = dict(locals())
    wts = {n: args[n] for n in WEIGHTS}
    mom_m = {n: args["m_" + n] for n in WEIGHTS}
    mom_v = {n: args["v_" + n] for n in WEIGHTS}
    xi, yi, ci = _position()
    me = 4 * xi + 2 * yi + ci

    groups = []
    for l in range(DEPTH):
        for group, names in GATHER_GROUPS:
            shards = [_to_gather_layout(n, wts[n][l]) for n in names]
            if (l, group) == (0, "in"):
                names = names + ("taps",)
                shards.append(_pack([w_dw], _rows_for([w_dw.shape])))
            groups.append(((l, group), names, shards))
    gather = _WeightGather(groups)

    def fetch(layer, group, marker):
        w = gather.fetch(layer, group, marker)
        if "taps" in w:
            blocks = w["taps"].reshape(N_DEV, -1)[:, :w_dw.size].reshape((N_DEV,) + w_dw.shape)
            w["taps"] = jnp.moveaxis(blocks, 0, 2).reshape(DEPTH, CONV_K, CW)
        return w

    reduce = _GradReduce(jnp.reshape(ci, (1,)).astype(jnp.int32), jnp.reshape(2 * xi + yi, (1,)).astype(jnp.int32))
    small = {n: wts[n] for n in SMALL if n != "w_dw"}
    loss, dx, gsmall = _local_step(x[0], mem[0], loss_target[0], fetch, small, reduce)
    grads = {key: _from_gather_layout(key[1], g) for key, g in reduce.finish((dx,)).items()}

    small_shapes = [wts[n].shape[1:] for n in SMALL]
    small_shapes[SMALL.index("w_dw")] = (CONV_K, CW)
    srows = _rows_for(small_shapes * DEPTH)
    packed = _pack([gsmall[l][n] for l in range(DEPTH) for n in SMALL], srows)
    (all_small,) = _all_gather("gather_small_grads", [packed])
    summed = _unpack(_sum_blocks("sum_small_grads", all_small, srows), small_shapes * DEPTH)
    for idx, (l, n) in enumerate([(l, n) for l in range(DEPTH) for n in SMALL]):
        grads[(l, n)] = summed[idx]
    shard_cols = CW // N_DEV
    for l in range(DEPTH):
        grads[(l, "w_dw")] = lax.dynamic_slice_in_dim(grads[(l, "w_dw")], me * shard_cols, shard_cols, axis=1)

    grad_w = {n: jnp.stack([grads[(l, n)] for l in range(DEPTH)]) for n in SMALL}
    delta, new_m, new_v = {}, {}, {}
    small_all_shapes = [wts[n].shape for n in SMALL]
    arows = _rows_for(small_all_shapes, 64)
    packs = [_pack([src[n] for n in SMALL], arows)[None] for src in (wts, grad_w, mom_m, mom_v)]
    outs = _adamw("adamw_small", *packs, 64)
    for dst, buf in zip((delta, new_m, new_v), outs):
        dst.update(zip(SMALL, _unpack(buf, small_all_shapes)))
    for n in BIG:
        grad_w[n] = jnp.stack([grads[(l, n)] for l in range(DEPTH)])
        delta[n], new_m[n], new_v[n] = _adamw("adamw_" + n, wts[n], grad_w[n], mom_m[n], mom_v[n], 256)

    loss = lax.psum(loss, ("x", "y", "c"))
    return (loss, dx[None], *[grad_w[n] for n in WEIGHTS], *[delta[n] for n in WEIGHTS],
            *[new_m[n] for n in WEIGHTS], *[new_v[n] for n in WEIGHTS])
```

```python
import functools

import jax
import jax.numpy as jnp
from jax import lax
from jax.experimental import pallas as pl
from jax.experimental.pallas import tpu as pltpu

F32 = jnp.float32
BF16 = jnp.bfloat16

D = 2048
GW = 1024
PW = 512
CW = 512
HD = 128
NH = 8
NG = 4
POOL_WINDOWS = (2, 4, 8, 16)
CONV_K = 31
IN_COLS = 2 * GW + PW + 2 * CW
DFF = 4 * D
XH = 4
XHD = D // XH
ATT_SCALE = XHD ** -0.5
RMS_EPS = 1e-6
LN_EPS = 1e-5
DEPTH = 2
N_DEV = 8

ADAM_LR = 0.001
ADAM_B1 = 0.9
ADAM_B2 = 0.999
ADAM_EPS = 1e-08
ADAM_WD = 0.01
ADAM_STEP = 10

LANES = 128
CONV_HALO = 32
POOL_HALO = 16
ROW_TILE = 128
VMEM_LIMIT = 60 * 1024 * 1024

MESH = pl.DeviceIdType.MESH
NT = (((1,), (1,)), ((), ()))
NN = (((1,), (0,)), ((), ()))
TN = (((0,), (0,)), ((), ()))

BIG = ("w_out", "w_q", "w_k", "w_v", "w_o", "w_up", "w_down", "w_in")
GATHER_GROUPS = (("in", ("w_in",)), ("att", ("w_out", "w_q", "w_k", "w_v", "w_o")), ("ffn", ("w_up", "w_down")))
SMALL = ("norm_mix_pre", "norm_mix_post", "gmlp_v_gain", "w_spatial", "b_spatial", "w_pool", "s_pool",
         "w_dw", "b_dw", "conv_ln_g", "conv_ln_b", "norm_xattn_pre", "norm_mem", "norm_xattn_post",
         "norm_ffn_pre", "norm_ffn_post")
WEIGHTS = ("norm_mix_pre", "norm_mix_post", "w_in", "w_out", "gmlp_v_gain", "w_spatial", "b_spatial", "w_pool",
           "s_pool", "w_dw", "b_dw", "conv_ln_g", "conv_ln_b", "norm_xattn_pre", "norm_mem", "norm_xattn_post",
           "w_q", "w_k", "w_v", "w_o", "norm_ffn_pre", "norm_ffn_post", "w_up", "w_down")


def _cparams():
    return pltpu.CompilerParams(vmem_limit_bytes=VMEM_LIMIT)


def _dot(a, b, dims):
    return lax.dot_general(a, b, dims, preferred_element_type=F32)


def _rms(x, g):
    y = x * lax.rsqrt(jnp.mean(x * x, axis=-1, keepdims=True) + RMS_EPS)
    return y * g


def _gelu(x):
    cdf = 0.5 * (1.0 + jnp.tanh(0.7978845608028654 * (x + 0.044715 * (x * x * x))))
    return x * cdf


def _layer_norm(x, g, b=None):
    mu = jnp.mean(x, axis=-1, keepdims=True)
    xc = x - mu
    var = jnp.mean(xc * xc, axis=-1, keepdims=True)
    y = xc * lax.rsqrt(var + LN_EPS) * g
    return y if b is None else y + b


def _sigmoid(x):
    return 1.0 / (1.0 + jnp.exp(-x))


def _gmlp_rows(zu, zv, gv):
    return _gelu(zu), _layer_norm(_gelu(zv), gv)


def _glu(cv, cg):
    return cv * _sigmoid(cg)


def _ln_silu(h, g, b):
    y = _layer_norm(h, g, b)
    return y * _sigmoid(y)


ANY = pl.BlockSpec(memory_space=pl.ANY)


def _rowop_mm(name, kind, rows, g, w, dims, out_dtype, tm, tn, u=None, after=()):
    s = rows[0].shape[0]
    n = w.shape[0] if dims == NT else w.shape[1]
    tm, tn = min(tm, s), min(tn, n)
    ni, nj = s // tm, n // tn
    bwd = kind == "rms_bwd"

    def body(*refs):
        refs = list(refs)
        row_refs = [refs.pop(0) for _ in rows]
        g_ref, w_ref = refs.pop(0), refs.pop(0)
        u_ref = refs.pop(0) if u is not None else None
        del refs[:len(after)]
        out_ref, a_ref = refs.pop(0), refs.pop(0)
        dg_ref = refs.pop(0) if bwd else None
        a_s = refs.pop(0)

        @pl.when(pl.program_id(1) == 0)
        def _():
            if bwd:
                _, vjp = jax.vjp(_rms, row_refs[0][...], g_ref[...])
                a, dg = vjp(row_refs[1][...])
                dg_ref[0] = dg
            else:
                a = _rms(row_refs[0][...], g_ref[...])
            a_s[...] = a.astype(BF16)
            a_ref[...] = a_s[...]

        acc = _dot(a_s[...], w_ref[...], dims)
        if u_ref is not None:
            acc = acc * (2.0 * jnp.maximum(u_ref[...], 0.0))
        out_ref[...] = acc.astype(out_dtype)

    row_spec = pl.BlockSpec((tm, D), lambda i, j: (i, 0))
    w_spec = pl.BlockSpec((tn, D), lambda i, j: (j, 0)) if dims == NT else pl.BlockSpec((D, tn), lambda i, j: (0, j))
    in_specs = [row_spec] * len(rows) + [pl.BlockSpec((1, D), lambda i, j: (0, 0)), w_spec]
    args = list(rows) + [g, w]
    if u is not None:
        in_specs.append(pl.BlockSpec((tm, tn), lambda i, j: (i, j)))
        args.append(u)
    in_specs += [ANY] * len(after)
    args += list(after)
    out_shape = [jax.ShapeDtypeStruct((s, n), out_dtype), jax.ShapeDtypeStruct((s, D), BF16)]
    out_specs = [pl.BlockSpec((tm, tn), lambda i, j: (i, j)), row_spec]
    if bwd:
        out_shape.append(jax.ShapeDtypeStruct((ni, 1, D), F32))
        out_specs.append(pl.BlockSpec((1, 1, D), lambda i, j: (i, 0, 0)))
    return pl.pallas_call(
        body, name=name, grid=(ni, nj), in_specs=in_specs, out_specs=out_specs, out_shape=out_shape,
        scratch_shapes=[pltpu.VMEM((tm, D), BF16)], compiler_params=_cparams(),
    )(*args)


def _mm_rowop(name, kind, pairs, rows, g, tm, tk, relu2=False, after=()):
    s, kdim = pairs[0][0].shape
    tm, tk = min(tm, s), min(tk, kdim)
    ni, nk = s // tm, kdim // tk
    npair = len(pairs)

    def body(*refs):
        refs = list(refs)
        a_refs = [refs.pop(0) for _ in range(npair)]
        w_refs = [refs.pop(0) for _ in range(npair)]
        row_refs = [refs.pop(0) for _ in rows]
        g_ref = refs.pop(0)
        del refs[:len(after)]
        acc = refs.pop()
        outs = refs
        k = pl.program_id(1)

        @pl.when(k == 0)
        def _():
            acc[...] = jnp.zeros_like(acc)

        for a_ref, w_ref, (_, _, dims) in zip(a_refs, w_refs, pairs):
            a = a_ref[...]
            if relu2:
                a = jnp.square(jnp.maximum(a, 0.0))
            acc[...] += _dot(a.astype(BF16), w_ref[...], dims)

        @pl.when(k == nk - 1)
        def _():
            h = acc[...]
            if kind == "rms_res":
                outs[0][...] = row_refs[0][...] + _rms(h, g_ref[...])
                outs[1][...] = h
            else:
                _, vjp = jax.vjp(_rms, row_refs[0][...], g_ref[...])
                dx, dg = vjp(h)
                if kind == "rms_bwd_res":
                    outs[0][...] = row_refs[1][...] + dx
                    outs[1][0] = dg
                else:
                    outs[0][0] = dg

    row_spec = pl.BlockSpec((tm, D), lambda i, k: (i, 0))
    dg_shape = jax.ShapeDtypeStruct((ni, 1, D), F32)
    dg_spec = pl.BlockSpec((1, 1, D), lambda i, k: (i, 0, 0))
    in_specs = [pl.BlockSpec((tm, tk), lambda i, k: (i, k))] * npair
    for _, _, dims in pairs:
        in_specs.append(pl.BlockSpec((tk, D), lambda i, k: (k, 0)) if dims == NN
                        else pl.BlockSpec((D, tk), lambda i, k: (0, k)))
    in_specs += [row_spec] * len(rows) + [pl.BlockSpec((1, D), lambda i, k: (0, 0))] + [ANY] * len(after)
    if kind == "rms_res":
        out_shape = [jax.ShapeDtypeStruct((s, D), F32)] * 2
        out_specs = [row_spec, row_spec]
    elif kind == "rms_bwd_res":
        out_shape = [jax.ShapeDtypeStruct((s, D), F32), dg_shape]
        out_specs = [row_spec, dg_spec]
    else:
        out_shape = [dg_shape]
        out_specs = [dg_spec]
    return pl.pallas_call(
        body, name=name, grid=(ni, nk), in_specs=in_specs, out_specs=out_specs, out_shape=out_shape,
        scratch_shapes=[pltpu.VMEM((tm, D), F32)], compiler_params=_cparams(),
    )(*[p[0] for p in pairs], *[p[1] for p in pairs], *rows, g, *after)


def _mm_tn(name, a, gmat, tm, ts, relu2=False, after=()):
    s, m = a.shape
    tm, ts = min(tm, m), min(ts, s)
    ni, ns = m // tm, s // ts

    def body(a_ref, g_ref, *rest):
        o_ref, acc = rest[len(after):]
        k = pl.program_id(1)

        @pl.when(k == 0)
        def _():
            acc[...] = jnp.zeros_like(acc)

        av = a_ref[...]
        if relu2:
            av = jnp.square(jnp.maximum(av, 0.0))
        acc[...] += _dot(av.astype(BF16), g_ref[...], TN)

        @pl.when(k == ns - 1)
        def _():
            o_ref[...] = acc[...].astype(BF16)

    return pl.pallas_call(
        body, name=name, grid=(ni, ns),
        in_specs=[pl.BlockSpec((ts, tm), lambda i, k: (k, i)), pl.BlockSpec((ts, D), lambda i, k: (k, 0))]
        + [ANY] * len(after),
        out_specs=pl.BlockSpec((tm, D), lambda i, k: (i, 0)),
        out_shape=jax.ShapeDtypeStruct((m, D), BF16),
        scratch_shapes=[pltpu.VMEM((tm, D), F32)], compiler_params=_cparams(),
    )(a, gmat, *after)


def _tril():
    r = lax.broadcasted_iota(jnp.int32, (HD, HD), 0)
    c = lax.broadcasted_iota(jnp.int32, (HD, HD), 1)
    return (c <= r).astype(F32)


def _gmlp_fwd(z, gv, ws, bst, tb):
    s = z.shape[0]
    tb = min(tb, s)

    def body(zu_ref, zv_ref, gv_ref, ws_ref, bst_ref, y_ref):
        tril = _tril()
        for h in range(NH):
            cols = slice(h * HD, (h + 1) * HD)
            u, vln = _gmlp_rows(zu_ref[:, cols], zv_ref[:, cols], gv_ref[h:h + 1, :])
            wm = (ws_ref[h] * tril).astype(BF16)
            vb = vln.astype(BF16)
            for c in range(tb // HD):
                rws = slice(c * HD, (c + 1) * HD)
                mixed = _dot(wm, vb[rws], NN) + bst_ref[:, h:h + 1]
                y_ref[rws, cols] = (u[rws] * mixed).astype(BF16)

    return pl.pallas_call(
        body, name="gmlp_fwd", grid=(s // tb,),
        in_specs=[pl.BlockSpec((tb, GW), lambda i: (i, 0)), pl.BlockSpec((tb, GW), lambda i: (i, 1)),
                  pl.BlockSpec((NH, HD), lambda i: (0, 0)), pl.BlockSpec((NH, HD, HD), lambda i: (0, 0, 0)),
                  pl.BlockSpec((HD, NH), lambda i: (0, 0))],
        out_specs=pl.BlockSpec((tb, GW), lambda i: (i, 0)),
        out_shape=jax.ShapeDtypeStruct((s, GW), BF16), compiler_params=_cparams(),
    )(z, z, gv, ws, bst)


def _gmlp_bwd(z, dy, gv, ws, bst, tb):
    s = z.shape[0]
    tb = min(tb, s)
    nb = s // tb

    def body(zu_ref, zv_ref, dy_ref, gv_ref, ws_ref, bst_ref, dzu_ref, dzv_ref, dgv_ref, dws_ref, db_ref):
        tril = _tril()
        for h in range(NH):
            cols = slice(h * HD, (h + 1) * HD)
            (u, vln), vjp = jax.vjp(_gmlp_rows, zu_ref[:, cols], zv_ref[:, cols], gv_ref[h:h + 1, :])
            wmf = ws_ref[h] * tril
            wm = wmf.astype(BF16)
            wmt = wmf.T.astype(BF16)
            vb = vln.astype(BF16)
            dws = jnp.zeros((HD, HD), F32)
            db = jnp.zeros((HD, 1), F32)
            du_parts, dvln_parts = [], []
            for c in range(tb // HD):
                rws = slice(c * HD, (c + 1) * HD)
                mixed = _dot(wm, vb[rws], NN) + bst_ref[:, h:h + 1]
                dyc = dy_ref[rws, cols]
                du_parts.append(dyc * mixed)
                dmixed = dyc * u[rws]
                dmb = dmixed.astype(BF16)
                dws = dws + _dot(dmb, vb[rws], NT)
                db = db + jnp.sum(dmixed, axis=1, keepdims=True)
                dvln_parts.append(_dot(wmt, dmb, NN))
            du = jnp.concatenate(du_parts, axis=0)
            dvln = jnp.concatenate(dvln_parts, axis=0)
            dzu, dzv, dgv = vjp((du, dvln))
            dzu_ref[:, cols] = dzu.astype(BF16)
            dzv_ref[:, cols] = dzv.astype(BF16)
            dgv_ref[0, h:h + 1, :] = dgv
            dws_ref[0, h] = dws * tril
            db_ref[0, h] = jnp.broadcast_to(db, (HD, LANES))

    blk = pl.BlockSpec((tb, GW), lambda i: (i, 0))
    return pl.pallas_call(
        body, name="gmlp_bwd", grid=(nb,),
        in_specs=[blk, pl.BlockSpec((tb, GW), lambda i: (i, 1)), blk,
                  pl.BlockSpec((NH, HD), lambda i: (0, 0)), pl.BlockSpec((NH, HD, HD), lambda i: (0, 0, 0)),
                  pl.BlockSpec((HD, NH), lambda i: (0, 0))],
        out_specs=[blk, blk, pl.BlockSpec((1, NH, HD), lambda i: (i, 0, 0)),
                   pl.BlockSpec((1, NH, HD, HD), lambda i: (i, 0, 0, 0)),
                   pl.BlockSpec((1, NH, HD, LANES), lambda i: (i, 0, 0, 0))],
        out_shape=[jax.ShapeDtypeStruct((s, GW), BF16), jax.ShapeDtypeStruct((s, GW), BF16),
                   jax.ShapeDtypeStruct((nb, NH, HD), F32), jax.ShapeDtypeStruct((nb, NH, HD, HD), F32),
                   jax.ShapeDtypeStruct((nb, NH, HD, LANES), F32)],
        compiler_params=_cparams(),
    )(z, z, dy, gv, ws, bst)


def _pool_count(t0, window):
    pos = (t0 + lax.broadcasted_iota(jnp.int32, (ROW_TILE, LANES), 0)).astype(F32)
    return jnp.minimum(pos + 1.0, float(window))


def _window_sum(win, levels, back):
    n = win.shape[0]
    for lv in range(levels):
        step = 1 << lv
        win = win + pltpu.roll(win, n - step if back else step, 0)
    return win


def _pool_pooled(ppad_ref, t0, g):
    win = ppad_ref[pl.ds(t0, ROW_TILE + POOL_HALO), :]
    wsum = _window_sum(win, g + 1, False)[POOL_HALO:]
    return wsum / _pool_count(t0, POOL_WINDOWS[g]) - win[POOL_HALO:]


def _pool_fwd(z, wp, sp):
    s = z.shape[0]
    nt = s // ROW_TILE

    def body(p_ref, wp_ref, sp_ref, y_ref, ppad):
        for g in range(NG):
            cols = slice(g * LANES, (g + 1) * LANES)
            ppad[pl.ds(0, POOL_HALO), :] = jnp.zeros((POOL_HALO, LANES), F32)
            ppad[pl.ds(POOL_HALO, s), :] = p_ref[:, cols]
            wpb = wp_ref[g].astype(BF16)
            scale = sp_ref[:, cols]

            def tile(t, carry):
                t0 = pl.multiple_of(t * ROW_TILE, ROW_TILE)
                pooled = _pool_pooled(ppad, t0, g)
                y_ref[pl.ds(t0, ROW_TILE), cols] = (_dot(pooled.astype(BF16), wpb, NN) * scale).astype(BF16)
                return carry

            lax.fori_loop(0, nt, tile, 0)

    return pl.pallas_call(
        body, name="pool_fwd", grid=(1,),
        in_specs=[pl.BlockSpec((s, PW), lambda i: (0, 2 * GW // PW)),
                  pl.BlockSpec((NG, LANES, LANES), lambda i: (0, 0, 0)), pl.BlockSpec((1, PW), lambda i: (0, 0))],
        out_specs=pl.BlockSpec((s, PW), lambda i: (0, 0)),
        out_shape=jax.ShapeDtypeStruct((s, PW), BF16),
        scratch_shapes=[pltpu.VMEM((s + POOL_HALO, LANES), F32)], compiler_params=_cparams(),
    )(z, wp, sp)


def _pool_bwd(z, dy, wp, sp):
    s = z.shape[0]
    nt = s // ROW_TILE

    def body(p_ref, dy_ref, wp_ref, sp_ref, dp_ref, dwp_ref, dsp_ref, ppad, rpad, dpool):
        for g in range(NG):
            cols = slice(g * LANES, (g + 1) * LANES)
            ppad[pl.ds(0, POOL_HALO), :] = jnp.zeros((POOL_HALO, LANES), F32)
            ppad[pl.ds(POOL_HALO, s), :] = p_ref[:, cols]
            rpad[pl.ds(s, POOL_HALO), :] = jnp.zeros((POOL_HALO, LANES), F32)
            wpb = wp_ref[g].astype(BF16)
            scale = sp_ref[:, cols]

            def tile(t, carry):
                dwp, dsp = carry
                t0 = pl.multiple_of(t * ROW_TILE, ROW_TILE)
                pooled = _pool_pooled(ppad, t0, g)
                pb = pooled.astype(BF16)
                dyt = dy_ref[pl.ds(t0, ROW_TILE), cols]
                dsp = dsp + jnp.sum(dyt * _dot(pb, wpb, NN), axis=0, keepdims=True)
                dmm = (dyt * scale).astype(BF16)
                dwp = dwp + _dot(pb, dmm, TN)
                dpooled = _dot(dmm, wpb, NT)
                rpad[pl.ds(t0, ROW_TILE), :] = dpooled / _pool_count(t0, POOL_WINDOWS[g])
                dpool[pl.ds(t0, ROW_TILE), :] = dpooled
                return dwp, dsp

            dwp, dsp = lax.fori_loop(0, nt, tile, (jnp.zeros((LANES, LANES), F32), jnp.zeros((1, LANES), F32)))
            dwp_ref[g] = dwp
            dsp_ref[:, cols] = dsp

            def tile2(t, carry):
                t0 = pl.multiple_of(t * ROW_TILE, ROW_TILE)
                win = rpad[pl.ds(t0, ROW_TILE + POOL_HALO), :]
                back = _window_sum(win, g + 1, True)[:ROW_TILE]
                rows = pl.ds(t0, ROW_TILE)
                dp_ref[rows, cols] = (back - dpool[rows, :]).astype(BF16)
                return carry

            lax.fori_loop(0, nt, tile2, 0)

    return pl.pallas_call(
        body, name="pool_bwd", grid=(1,),
        in_specs=[pl.BlockSpec((s, PW), lambda i: (0, 2 * GW // PW)), pl.BlockSpec((s, PW), lambda i: (0, GW // PW)),
                  pl.BlockSpec((NG, LANES, LANES), lambda i: (0, 0, 0)), pl.BlockSpec((1, PW), lambda i: (0, 0))],
        out_specs=[pl.BlockSpec((s, PW), lambda i: (0, 0)), pl.BlockSpec((NG, LANES, LANES), lambda i: (0, 0, 0)),
                   pl.BlockSpec((1, PW), lambda i: (0, 0))],
        out_shape=[jax.ShapeDtypeStruct((s, PW), BF16), jax.ShapeDtypeStruct((NG, LANES, LANES), F32),
                   jax.ShapeDtypeStruct((1, PW), F32)],
        scratch_shapes=[pltpu.VMEM((s + POOL_HALO, LANES), F32), pltpu.VMEM((s + POOL_HALO, LANES), F32),
                        pltpu.VMEM((s, LANES), F32)],
        compiler_params=_cparams(),
    )(z, dy, wp, sp)


CONV_LEAD = CONV_HALO - (CONV_K - 1)


def _conv_taps(win, wdw_ref, lead, reverse):
    n = win.shape[0]
    acc = jnp.zeros((ROW_TILE, CW), F32)
    for j in range(CONV_K):
        tap = (CONV_K - 1 - j) if reverse else j
        acc = acc + wdw_ref[tap:tap + 1, :] * pltpu.roll(win, (n - (lead + j)) % n, 0)[:ROW_TILE]
    return acc


def _conv_fill_glu(cv_ref, cg_ref, xpad, s):
    xpad[pl.ds(0, CONV_HALO), :] = jnp.zeros((CONV_HALO, CW), F32)

    def fill(t, carry):
        t0 = pl.multiple_of(t * ROW_TILE, ROW_TILE)
        rows = pl.ds(t0, ROW_TILE)
        xpad[pl.ds(t0 + CONV_HALO, ROW_TILE), :] = _glu(cv_ref[rows, :], cg_ref[rows, :])
        return carry

    lax.fori_loop(0, s // ROW_TILE, fill, 0)


def _conv_fwd(z, wdw, bdw, lng, lnb):
    s = z.shape[0]

    def body(cv_ref, cg_ref, wdw_ref, bdw_ref, lng_ref, lnb_ref, y_ref, xpad):
        _conv_fill_glu(cv_ref, cg_ref, xpad, s)

        def tile(t, carry):
            t0 = pl.multiple_of(t * ROW_TILE, ROW_TILE)
            win = xpad[pl.ds(t0, ROW_TILE + CONV_HALO), :]
            hc = _conv_taps(win, wdw_ref, CONV_LEAD, False) + bdw_ref[...]
            y_ref[pl.ds(t0, ROW_TILE), :] = _ln_silu(hc, lng_ref[...], lnb_ref[...]).astype(BF16)
            return carry

        lax.fori_loop(0, s // ROW_TILE, tile, 0)

    vec = pl.BlockSpec((1, CW), lambda i: (0, 0))
    return pl.pallas_call(
        body, name="conv_fwd", grid=(1,),
        in_specs=[pl.BlockSpec((s, CW), lambda i: (0, (2 * GW + PW) // CW)),
                  pl.BlockSpec((s, CW), lambda i: (0, (2 * GW + PW) // CW + 1)),
                  pl.BlockSpec((CONV_K + 1, CW), lambda i: (0, 0)), vec, vec, vec],
        out_specs=pl.BlockSpec((s, CW), lambda i: (0, 0)),
        out_shape=jax.ShapeDtypeStruct((s, CW), BF16),
        scratch_shapes=[pltpu.VMEM((s + CONV_HALO, CW), F32)], compiler_params=_cparams(),
    )(z, z, wdw, bdw, lng, lnb)


def _conv_bwd(z, dy, wdw, bdw, lng, lnb):
    s = z.shape[0]

    def body(cv_ref, cg_ref, dy_ref, wdw_ref, bdw_ref, lng_ref, lnb_ref,
             dcv_ref, dcg_ref, dwdw_ref, dbdw_ref, dlng_ref, dlnb_ref, xpad, dpad):
        _conv_fill_glu(cv_ref, cg_ref, xpad, s)
        dpad[pl.ds(s, CONV_HALO), :] = jnp.zeros((CONV_HALO, CW), F32)
        dwdw_ref[...] = jnp.zeros((CONV_K + 1, CW), F32)

        def tile(t, carry):
            db, dg, dbeta = carry
            t0 = pl.multiple_of(t * ROW_TILE, ROW_TILE)
            win = xpad[pl.ds(t0, ROW_TILE + CONV_HALO), :]
            hc = _conv_taps(win, wdw_ref, CONV_LEAD, False) + bdw_ref[...]
            _, vjp = jax.vjp(_ln_silu, hc, lng_ref[...], lnb_ref[...])
            dhc, dg_t, dbeta_t = vjp(dy_ref[pl.ds(t0, ROW_TILE), :])
            dpad[pl.ds(t0, ROW_TILE), :] = dhc
            n = win.shape[0]
            for j in range(CONV_K):
                shifted = pltpu.roll(win, (n - (CONV_LEAD + j)) % n, 0)[:ROW_TILE]
                dwdw_ref[j:j + 1, :] += jnp.sum(dhc * shifted, axis=0, keepdims=True)
            return db + jnp.sum(dhc, axis=0, keepdims=True), dg + dg_t, dbeta + dbeta_t

        zero = jnp.zeros((1, CW), F32)
        db, dg, dbeta = lax.fori_loop(0, s // ROW_TILE, tile, (zero, zero, zero))
        dbdw_ref[...] = db
        dlng_ref[...] = dg
        dlnb_ref[...] = dbeta

        def tile2(t, carry):
            t0 = pl.multiple_of(t * ROW_TILE, ROW_TILE)
            rows = pl.ds(t0, ROW_TILE)
            win = dpad[pl.ds(t0, ROW_TILE + CONV_HALO), :]
            dglu = _conv_taps(win, wdw_ref, 0, True)
            _, vjp = jax.vjp(_glu, cv_ref[rows, :], cg_ref[rows, :])
            dcv, dcg = vjp(dglu)
            dcv_ref[rows, :] = dcv.astype(BF16)
            dcg_ref[rows, :] = dcg.astype(BF16)
            return carry

        lax.fori_loop(0, s // ROW_TILE, tile2, 0)

    vec = pl.BlockSpec((1, CW), lambda i: (0, 0))
    full = pl.BlockSpec((s, CW), lambda i: (0, 0))
    wspec = pl.BlockSpec((CONV_K + 1, CW), lambda i: (0, 0))
    vshape = jax.ShapeDtypeStruct((1, CW), F32)
    return pl.pallas_call(
        body, name="conv_bwd", grid=(1,),
        in_specs=[pl.BlockSpec((s, CW), lambda i: (0, (2 * GW + PW) // CW)),
                  pl.BlockSpec((s, CW), lambda i: (0, (2 * GW + PW) // CW + 1)),
                  pl.BlockSpec((s, CW), lambda i: (0, (GW + PW) // CW)), wspec, vec, vec, vec],
        out_specs=[full, full, wspec, vec, vec, vec],
        out_shape=[jax.ShapeDtypeStruct((s, CW), BF16), jax.ShapeDtypeStruct((s, CW), BF16),
                   jax.ShapeDtypeStruct((CONV_K + 1, CW), F32), vshape, vshape, vshape],
        scratch_shapes=[pltpu.VMEM((s + CONV_HALO, CW), F32), pltpu.VMEM((s + CONV_HALO, CW), F32)],
        compiler_params=_cparams(),
    )(z, z, dy, wdw, bdw, lng, lnb)


def _softmax_rows(sc):
    e = jnp.exp(sc - jnp.max(sc, axis=-1, keepdims=True))
    return e / jnp.sum(e, axis=-1, keepdims=True)


def _attn_fwd(q, k, v, tq):
    s, m = q.shape[0], k.shape[0]
    tq = min(tq, s)

    def body(q_ref, k_ref, v_ref, o_ref):
        for h in range(XH):
            cols = slice(h * XHD, (h + 1) * XHD)
            p = _softmax_rows(_dot(q_ref[:, cols], k_ref[:, cols], NT) * ATT_SCALE)
            o_ref[:, cols] = _dot(p.astype(BF16), v_ref[:, cols], NN).astype(BF16)

    kv = pl.BlockSpec((m, D), lambda i: (0, 0))
    return pl.pallas_call(
        body, name="attn_fwd", grid=(s // tq,),
        in_specs=[pl.BlockSpec((tq, D), lambda i: (i, 0)), kv, kv],
        out_specs=pl.BlockSpec((tq, D), lambda i: (i, 0)),
        out_shape=jax.ShapeDtypeStruct((s, D), BF16), compiler_params=_cparams(),
    )(q, k, v)


def _attn_bwd(q, k, v, do, tq):
    s, m = q.shape[0], k.shape[0]
    tq = min(tq, s)

    def body(q_ref, k_ref, v_ref, do_ref, dq_ref, dk_ref, dv_ref):
        @pl.when(pl.program_id(0) == 0)
        def _():
            dk_ref[...] = jnp.zeros_like(dk_ref)
            dv_ref[...] = jnp.zeros_like(dv_ref)

        for h in range(XH):
            cols = slice(h * XHD, (h + 1) * XHD)
            qh, kh, vh, doh = q_ref[:, cols], k_ref[:, cols], v_ref[:, cols], do_ref[:, cols]
            p = _softmax_rows(_dot(qh, kh, NT) * ATT_SCALE)
            dp = _dot(doh, vh, NT)
            dv_ref[:, cols] += _dot(p.astype(BF16), doh, TN)
            ds = (p * (dp - jnp.sum(p * dp, axis=-1, keepdims=True)) * ATT_SCALE).astype(BF16)
            dq_ref[:, cols] = _dot(ds, kh, NN).astype(BF16)
            dk_ref[:, cols] += _dot(ds, qh, TN)

    kv = pl.BlockSpec((m, D), lambda i: (0, 0))
    qs = pl.BlockSpec((tq, D), lambda i: (i, 0))
    return pl.pallas_call(
        body, name="attn_bwd", grid=(s // tq,),
        in_specs=[qs, kv, kv, qs], out_specs=[qs, kv, kv],
        out_shape=[jax.ShapeDtypeStruct((s, D), BF16), jax.ShapeDtypeStruct((m, D), F32),
                   jax.ShapeDtypeStruct((m, D), F32)],
        compiler_params=_cparams(),
    )(q, k, v, do)


def _loss_head(y, target, tm):
    s = y.shape[0]
    tm = min(tm, s)

    def body(y_ref, t_ref, dy_ref, part_ref):
        err = y_ref[...] - t_ref[...]
        dy_ref[...] = err * (1.0 / D)
        part_ref[...] = jnp.full((1, 8, LANES), 0.5 * jnp.sum(err * err) * (1.0 / D), F32)

    blk = pl.BlockSpec((tm, D), lambda i: (i, 0))
    dy, part = pl.pallas_call(
        body, name="loss_head", grid=(s // tm,), in_specs=[blk, blk],
        out_specs=[blk, pl.BlockSpec((1, 8, LANES), lambda i: (i, 0, 0))],
        out_shape=[jax.ShapeDtypeStruct((s, D), F32), jax.ShapeDtypeStruct((s // tm, 8, LANES), F32)],
        compiler_params=_cparams(),
    )(y, target)
    return dy, jnp.sum(part[:, 0, 0])


def _layer_fwd(x0, mem, w, p, fetch):
    z, hn0 = _rowop_mm("mix_in", "rms", (x0,), p["norm_mix_pre"], w["w_in"], NT, F32, 512, 512)
    ya = _gmlp_fwd(z, p["gmlp_v_gain"], p["w_spatial"], p["b_spatial_t"], 512)
    yb = _pool_fwd(z, p["w_pool"], p["s_pool"])
    yc = _conv_fwd(z, p["w_dw"], p["b_dw"], p["conv_ln_g"], p["conv_ln_b"])
    y = jnp.concatenate([ya, yb, yc], axis=1)
    w.update(fetch("att", (y,)))
    x1, h0 = _mm_rowop("mix_out", "rms_res", [(y, w["w_out"], NN)], (x0,), p["norm_mix_post"], 512, 512)
    q, hn1 = _rowop_mm("att_q", "rms", (x1,), p["norm_xattn_pre"], w["w_q"], NN, BF16, 512, 512)
    k, mn = _rowop_mm("att_k", "rms", (mem,), p["norm_mem"], w["w_k"], NN, BF16, 256, 512)
    v, _ = _rowop_mm("att_v", "rms", (mem,), p["norm_mem"], w["w_v"], NN, BF16, 256, 512)
    o = _attn_fwd(q, k, v, 256)
    x2, h1 = _mm_rowop("att_o", "rms_res", [(o, w["w_o"], NN)], (x1,), p["norm_xattn_post"], 512, 512)
    w.update(fetch("ffn", (x2,)))
    u, hn2 = _rowop_mm("ffn_up", "rms", (x2,), p["norm_ffn_pre"], w["w_up"], NT, F32, 512, 512)
    x3, h2 = _mm_rowop("ffn_down", "rms_res", [(u, w["w_down"], NN)], (x2,), p["norm_ffn_post"], 512, 512, relu2=True)
    saved = dict(x0=x0, z=z, hn0=hn0, y=y, h0=h0, x1=x1, q=q, hn1=hn1, k=k, v=v, mn=mn, o=o, h1=h1, x2=x2, u=u,
                 hn2=hn2, h2=h2)
    return x3, saved


def _layer_bwd(dx3, mem, w, p, sv, red):
    gs = {}
    du, dh2, dg = _rowop_mm("ffn_down_bwd", "rms_bwd", (sv["h2"], dx3), p["norm_ffn_post"], w["w_down"], NT, BF16,
                            512, 512, u=sv["u"], after=red.after())
    gs["norm_ffn_post"] = jnp.sum(dg, axis=0)
    g_down = _mm_tn("ffn_down_dw", sv["u"], dh2, 512, 512, relu2=True)
    red.advance((g_down,))
    dx2, dg = _mm_rowop("ffn_up_bwd", "rms_bwd_res", [(du, w["w_up"], NN)], (sv["x2"], dx3), p["norm_ffn_pre"], 512, 512,
                        after=red.after())
    gs["norm_ffn_pre"] = jnp.sum(dg, axis=0)
    g_up = _mm_tn("ffn_up_dw", du, sv["hn2"], 512, 512)
    red.add("ffn", ("w_down", "w_up"), [g_down, g_up])
    do, dh1, dg = _rowop_mm("att_o_bwd", "rms_bwd", (sv["h1"], dx2), p["norm_xattn_post"], w["w_o"], NT, BF16, 512, 512,
                            after=red.after())
    gs["norm_xattn_post"] = jnp.sum(dg, axis=0)
    g_o = _mm_tn("att_o_dw", sv["o"], dh1, 512, 512)
    red.advance((g_o,))
    dq, dk, dv = _attn_bwd(sv["q"], sv["k"], sv["v"], do, 256)
    dk, dv = dk.astype(BF16), dv.astype(BF16)
    dx1, dg = _mm_rowop("att_q_bwd", "rms_bwd_res", [(dq, w["w_q"], NT)], (sv["x1"], dx2), p["norm_xattn_pre"], 512, 512,
                        after=red.after())
    gs["norm_xattn_pre"] = jnp.sum(dg, axis=0)
    g_q = _mm_tn("att_q_dw", sv["hn1"], dq, 512, 512)
    g_k = _mm_tn("att_k_dw", sv["mn"], dk, 512, 256)
    g_v = _mm_tn("att_v_dw", sv["mn"], dv, 512, 256)
    (dg,) = _mm_rowop("att_kv_bwd", "rms_bwd_gain", [(dk, w["w_k"], NT), (dv, w["w_v"], NT)], (mem,), p["norm_mem"],
                      256, 512)
    gs["norm_mem"] = jnp.sum(dg, axis=0)
    red.add("att", ("w_o", "w_q", "w_k", "w_v"), [g_o, g_q, g_k, g_v])
    dy, dh0, dg = _rowop_mm("mix_out_bwd", "rms_bwd", (sv["h0"], dx1), p["norm_mix_post"], w["w_out"], NT, F32, 512, 512,
                            after=red.after())
    gs["norm_mix_post"] = jnp.sum(dg, axis=0)
    g_out = _mm_tn("mix_out_dw", sv["y"], dh0, 512, 512)
    red.advance((g_out,))
    red.add("out", ("w_out",), [g_out])
    z = sv["z"]
    dzu, dzv, dgv, dws, dbs = _gmlp_bwd(z, dy, p["gmlp_v_gain"], p["w_spatial"], p["b_spatial_t"], 512)
    gs["gmlp_v_gain"] = jnp.sum(dgv, axis=0)
    gs["w_spatial"] = jnp.sum(dws, axis=0)
    gs["b_spatial"] = jnp.sum(dbs[..., 0], axis=0)
    dp, gs["w_pool"], gs["s_pool"] = _pool_bwd(z, dy, p["w_pool"], p["s_pool"])
    dcv, dcg, dwdw, gs["b_dw"], gs["conv_ln_g"], gs["conv_ln_b"] = _conv_bwd(
        z, dy, p["w_dw"], p["b_dw"], p["conv_ln_g"], p["conv_ln_b"])
    gs["w_dw"] = dwdw[:CONV_K]
    dz = jnp.concatenate([dzu, dzv, dp, dcv, dcg], axis=1)
    red.advance((dz,))
    g_in = _mm_tn("mix_in_dw", dz, sv["hn0"], 512, 512, after=red.after())
    red.add("in", ("w_in",), [g_in])
    dx0, dg = _mm_rowop("mix_in_bwd", "rms_bwd_res", [(dz, w["w_in"], NN)], (sv["x0"], dx1), p["norm_mix_pre"], 512, 512,
                        after=red.after())
    gs["norm_mix_pre"] = jnp.sum(dg, axis=0)
    return dx0, gs


def _layer_params(small, l):
    p = {n: small[n][l].reshape(1, -1) for n in ("norm_mix_pre", "norm_mix_post", "s_pool", "b_dw", "conv_ln_g",
                                                   "conv_ln_b", "norm_xattn_pre", "norm_mem", "norm_xattn_post",
                                                   "norm_ffn_pre", "norm_ffn_post")}
    p["gmlp_v_gain"] = small["gmlp_v_gain"][l]
    p["w_spatial"] = small["w_spatial"][l]
    p["b_spatial_t"] = small["b_spatial"][l].T
    p["w_pool"] = small["w_pool"][l]
    p["w_dw"] = jnp.pad(small["w_dw"][l], ((0, 1), (0, 0)))
    return p


def _local_step(x, mem, target, fetch, small, red):
    small = dict(small)
    saved, weights, params = [], [], []
    h = x
    marker = ()
    for l in range(DEPTH):
        w = fetch(l, "in", marker)
        if "taps" in w:
            small["w_dw"] = w.pop("taps")
        p = _layer_params(small, l)
        h, sv = _layer_fwd(h, mem, w, p, functools.partial(fetch, l))
        marker = (h,)
        saved.append(sv)
        weights.append(w)
        params.append(p)
    dh, loss = _loss_head(h, target, 512)
    gsmall = [None] * DEPTH
    for l in reversed(range(DEPTH)):
        red.layer = l
        dh, gs = _layer_bwd(dh, mem, weights[l], params[l], saved[l], red)
        gsmall[l] = {n: gs[n].reshape(small[n].shape[1:]) for n in SMALL}
    return loss, dh, gsmall


HBM = pl.BlockSpec(memory_space=pltpu.HBM)


def _position():
    return lax.axis_index("x"), lax.axis_index("y"), lax.axis_index("c")


def _all_gather(name, shards):
    n = len(shards)

    def body(*refs):
        ins, outs = refs[:n], refs[n:2 * n]
        send_sems, recv_sems, local_sems = refs[2 * n:]
        x, y, c = _position()
        me, sibling = (x, y, c), (x, y, 1 - c)
        chips = [(1 - x, y), (x, 1 - y), (1 - x, 1 - y)]

        def rows(a, dev):
            r = shards[a].shape[0]
            return outs[a].at[pl.ds((4 * dev[0] + 2 * dev[1] + dev[2]) * r, r), :]

        def copy(a, k, block, to, src=None):
            return pltpu.make_async_remote_copy(
                src_ref=rows(a, block) if src is None else src, dst_ref=rows(a, block),
                send_sem=send_sems.at[a, k], recv_sem=recv_sems.at[a, k], device_id=to, device_id_type=MESH)

        started = []
        for a in range(n):
            mine = pltpu.make_async_copy(ins[a], rows(a, me), local_sems.at[a])
            mine.start()
            started.append(mine)
        first = []
        for a in range(n):
            for j, chip in enumerate(chips):
                first.append(copy(a, 1 + j, me, (*chip, c), src=ins[a]))
        for a in range(n):
            first.append(copy(a, 0, me, sibling, src=ins[a]))
        for cp in first:
            cp.start()
        passed = []
        for a in range(n):
            for j, chip in enumerate(chips):
                copy(a, 1 + j, (*chip, c), me).wait_recv()
                fwd = copy(a, 4 + j, (*chip, c), sibling)
                fwd.start()
                passed.append(fwd)
        for a in range(n):
            copy(a, 0, sibling, me).wait_recv()
            for j, chip in enumerate(chips):
                copy(a, 4 + j, (*chip, 1 - c), me).wait_recv()
        for cp in first + passed:
            cp.wait_send()
        for mine in started:
            mine.wait()

    return pl.pallas_call(
        body, name=name, in_specs=[HBM] * n, out_specs=[HBM] * n,
        out_shape=[jax.ShapeDtypeStruct((N_DEV * s.shape[0], s.shape[1]), s.dtype) for s in shards],
        scratch_shapes=[pltpu.SemaphoreType.DMA((n, 7)), pltpu.SemaphoreType.DMA((n, 7)), pltpu.SemaphoreType.DMA((n,))],
    )(*shards)


SEM = pl.BlockSpec(memory_space=pltpu.SEMAPHORE)
EFFECT = pltpu.SideEffectType.DATAFLOW_SIDE_EFFECTING
TOKEN = jax.ShapeDtypeStruct((8, LANES), F32)
TOKEN_SPEC = pl.BlockSpec(memory_space=pltpu.VMEM)


def _landing(shape, dtype):
    return pltpu.with_memory_space_constraint(lax.empty(shape, dtype), pltpu.HBM)


def _hbm_shapes(arrays):
    return [pltpu.HBM(a.shape, a.dtype) for a in arrays]


def _block(ref, r, dev):
    return ref.at[pl.ds((4 * dev[0] + 2 * dev[1] + dev[2]) * r, r), :]


def _split_call(name, body, thru, sems_in, after, sems_out, token):
    n = len(thru)
    out_shape = [pltpu.SemaphoreType.DMA(s) for s in sems_out] + _hbm_shapes(thru) + ([TOKEN] if token else [])
    out_specs = [SEM] * len(sems_out) + [HBM] * n + ([TOKEN_SPEC] if token else [])
    return pl.pallas_call(
        body, name=name, in_specs=[HBM] * n + [SEM] * len(sems_in) + [ANY] * len(after),
        out_specs=out_specs, out_shape=out_shape,
        input_output_aliases={i: len(sems_out) + i for i in range(n)},
        compiler_params=pltpu.CompilerParams(has_side_effects=EFFECT),
    )(*thru, *sems_in, *after)


def _place_own(name, srcs, dev, out_dtype, tr):
    n = len(srcs)
    r, cols = srcs[0][0].shape[-2:]
    tr = r if r < 16 else _row_tile(r, tr)
    nb = r // tr

    def body(dev_ref, *refs):
        for a in range(n):
            refs[n + a][...] = refs[a][...].astype(out_dtype)

    in_specs = [pl.BlockSpec((tr, cols), lambda i, d: (i, 0)) if l is None
                else pl.BlockSpec((None, tr, cols), lambda i, d, l=l: (l, i, 0)) for _, l in srcs]
    return pl.pallas_call(
        body, name=name,
        grid_spec=pltpu.PrefetchScalarGridSpec(
            num_scalar_prefetch=1, grid=(nb,), in_specs=in_specs,
            out_specs=[pl.BlockSpec((tr, cols), lambda i, d: (d[0] * nb + i, 0))] * n),
        out_shape=[jax.ShapeDtypeStruct((N_DEV * r, cols), out_dtype)] * n, compiler_params=_cparams(),
    )(dev, *[a for a, _ in srcs])


def _gather_peers(x, y, c):
    return [(1 - x, y, c), (x, 1 - y, c), (1 - x, 1 - y, c), (x, y, 1 - c)]


def _block_rows(land):
    return land.shape[0] // N_DEV


def _gather_start(name, lands, after):
    n = len(lands)

    def body(*refs):
        lz = refs[:n]
        send_sems, recv_sems = refs[n + len(after)], refs[n + len(after) + 1]
        token = refs[-1]
        x, y, c = _position()
        for a in range(n):
            own = _block(lz[a], _block_rows(lands[a]), (x, y, c))
            for k, to in enumerate(_gather_peers(x, y, c)):
                pltpu.make_async_remote_copy(src_ref=own, dst_ref=own, send_sem=send_sems.at[k], recv_sem=recv_sems.at[k],
                                             device_id=to, device_id_type=MESH).start()
        token[...] = jnp.zeros_like(token)

    out = _split_call(name, body, list(lands), [], after, [(4,), (4,)], True)
    return out[0], out[1], out[2:2 + n], out[-1]


def _gather_forward(name, lands, recv_sems, after):
    n = len(lands)

    def body(*refs):
        lz = refs[:n]
        recv0 = refs[n]
        fsend, frecv = refs[n + 1 + len(after)], refs[n + 2 + len(after)]
        token = refs[-1]
        x, y, c = _position()
        chips = _gather_peers(x, y, c)[:3]
        for a in range(n):
            for j, chip in enumerate(chips):
                blk = _block(lz[a], _block_rows(lands[a]), chip)
                pltpu.make_async_remote_copy(src_ref=blk, dst_ref=blk, send_sem=fsend.at[j], recv_sem=recv0.at[j],
                                             device_id=(x, y, c), device_id_type=MESH).wait_recv()
        for a in range(n):
            for j, chip in enumerate(chips):
                blk = _block(lz[a], _block_rows(lands[a]), chip)
                pltpu.make_async_remote_copy(src_ref=blk, dst_ref=blk, send_sem=fsend.at[j], recv_sem=frecv.at[j],
                                             device_id=(x, y, 1 - c), device_id_type=MESH).start()
        token[...] = jnp.zeros_like(token)

    out = _split_call(name, body, list(lands), [recv_sems], after, [(3,), (3,)], True)
    return out[0], out[1], out[2:2 + n], out[-1]


def _gather_finish(name, lands, send_sems, recv_sems, fsend, frecv, after):
    n = len(lands)

    def body(*refs):
        lz = refs[:n]
        send0, recv0, fsend_ref, frecv_ref = refs[n:n + 4]
        x, y, c = _position()
        me = (x, y, c)
        chips = _gather_peers(x, y, c)[:3]
        for a in range(n):
            r = _block_rows(lands[a])
            sib = _block(lz[a], r, (x, y, 1 - c))
            pltpu.make_async_remote_copy(src_ref=sib, dst_ref=sib, send_sem=send0.at[3], recv_sem=recv0.at[3],
                                         device_id=me, device_id_type=MESH).wait_recv()
            for j, chip in enumerate(chips):
                blk = _block(lz[a], r, (chip[0], chip[1], 1 - c))
                pltpu.make_async_remote_copy(src_ref=blk, dst_ref=blk, send_sem=fsend_ref.at[j], recv_sem=frecv_ref.at[j],
                                             device_id=me, device_id_type=MESH).wait_recv()
            own = _block(lz[a], r, me)
            for k in range(4):
                pltpu.make_async_remote_copy(src_ref=own, dst_ref=own, send_sem=send0.at[k], recv_sem=recv0.at[k],
                                             device_id=me, device_id_type=MESH).wait_send()
            for j, chip in enumerate(chips):
                blk = _block(lz[a], r, chip)
                pltpu.make_async_remote_copy(src_ref=blk, dst_ref=blk, send_sem=fsend_ref.at[j], recv_sem=frecv_ref.at[j],
                                             device_id=me, device_id_type=MESH).wait_send()

    return _split_call(name, body, list(lands), [send_sems, recv_sems, fsend, frecv], after, [], False)


def _sibling_start(name, grads, after):
    n = len(grads)
    lands = [_landing((4, g.shape[0] // N_DEV, D), g.dtype) for g in grads]

    def body(*refs):
        ins, lz = refs[:n], refs[n:2 * n]
        send_sem, recv_sem = refs[2 * n + len(after)], refs[2 * n + len(after) + 1]
        token = refs[-1]
        x, y, c = _position()
        for a in range(n):
            r = grads[a].shape[0] // N_DEV
            for q in range(4):
                pltpu.make_async_remote_copy(
                    src_ref=ins[a].at[pl.ds((2 * q + 1 - c) * r, r), :], dst_ref=lz[a].at[q], send_sem=send_sem.at[0],
                    recv_sem=recv_sem.at[0], device_id=(x, y, 1 - c), device_id_type=MESH).start()
        token[...] = jnp.zeros_like(token)

    out = _split_call(name, body, list(grads) + lands, [], after, [(1,), (1,)], True)
    return out[0], out[1], out[2:2 + n], out[2 + n:2 + 2 * n], out[-1]


def _sibling_finish(name, grads, lands, send_sem, recv_sem, after):
    n = len(grads)

    def body(*refs):
        ins, lz = refs[:n], refs[n:2 * n]
        send_ref, recv_ref = refs[2 * n], refs[2 * n + 1]
        x, y, c = _position()
        for a in range(n):
            r = grads[a].shape[0] // N_DEV
            for q in range(4):
                cp = pltpu.make_async_remote_copy(
                    src_ref=ins[a].at[pl.ds((2 * q + 1 - c) * r, r), :], dst_ref=lz[a].at[q], send_sem=send_ref.at[0],
                    recv_sem=recv_ref.at[0], device_id=(x, y, c), device_id_type=MESH)
                cp.wait_send()
                cp.wait_recv()

    out = _split_call(name, body, list(grads) + list(lands), [send_sem, recv_sem], after, [], False)
    return out[:n], out[n:2 * n]


def _chip_start(name, parts, after):
    n = len(parts)
    lands = [_landing((3,) + p.shape[1:], p.dtype) for p in parts]

    def body(*refs):
        ins, lz = refs[:n], refs[n:2 * n]
        send_sems, recv_sems = refs[2 * n + len(after)], refs[2 * n + len(after) + 1]
        token = refs[-1]
        x, y, c = _position()
        for a in range(n):
            for j, chip in enumerate(_gather_peers(x, y, c)[:3]):
                pltpu.make_async_remote_copy(
                    src_ref=ins[a].at[2 * chip[0] + chip[1]], dst_ref=lz[a].at[j], send_sem=send_sems.at[j],
                    recv_sem=recv_sems.at[j], device_id=chip, device_id_type=MESH).start()
        token[...] = jnp.zeros_like(token)

    out = _split_call(name, body, list(parts) + lands, [], after, [(3,), (3,)], True)
    return out[0], out[1], out[2:2 + n], out[2 + n:2 + 2 * n], out[-1]


def _chip_finish(name, parts, lands, send_sems, recv_sems, after):
    n = len(parts)

    def body(*refs):
        ins, lz = refs[:n], refs[n:2 * n]
        send_ref, recv_ref = refs[2 * n], refs[2 * n + 1]
        me = _position()
        for a in range(n):
            for j in range(3):
                cp = pltpu.make_async_remote_copy(
                    src_ref=ins[a].at[j], dst_ref=lz[a].at[j], send_sem=send_ref.at[j], recv_sem=recv_ref.at[j],
                    device_id=me, device_id_type=MESH)
                cp.wait_send()
                cp.wait_recv()

    out = _split_call(name, body, list(parts) + list(lands), [send_sems, recv_sems], after, [], False)
    return out[:n], out[n:2 * n]


def _row_tile(r, target):
    return max(t for t in range(16, min(r, target) + 1, 16) if r % t == 0)


def _chip_partial(name, grad, got, c, tr):
    r = grad.shape[0] // N_DEV
    tr = _row_tile(r, tr)
    g4 = grad.reshape(4, 2, r, D)

    def body(c_ref, g_ref, s_ref, o_ref):
        o_ref[...] = (g_ref[...].astype(F32) + s_ref[...].astype(F32)).astype(BF16)

    return pl.pallas_call(
        body, name=name,
        grid_spec=pltpu.PrefetchScalarGridSpec(
            num_scalar_prefetch=1, grid=(4, r // tr),
            in_specs=[pl.BlockSpec((None, None, tr, D), lambda q, i, c_ref: (q, c_ref[0], i, 0)),
                      pl.BlockSpec((None, tr, D), lambda q, i, c_ref: (q, i, 0))],
            out_specs=pl.BlockSpec((None, tr, D), lambda q, i, c_ref: (q, i, 0))),
        out_shape=jax.ShapeDtypeStruct((4, r, D), BF16), compiler_params=_cparams(),
    )(c, g4, got)


def _chip_sum(name, part, got, chip, tr):
    r = part.shape[1]
    tr = _row_tile(r, tr)

    def body(q_ref, p_ref, g_ref, o_ref):
        acc = p_ref[...].astype(F32)
        for j in range(3):
            acc = acc + g_ref[j].astype(F32)
        o_ref[...] = acc

    return pl.pallas_call(
        body, name=name,
        grid_spec=pltpu.PrefetchScalarGridSpec(
            num_scalar_prefetch=1, grid=(r // tr,),
            in_specs=[pl.BlockSpec((None, tr, D), lambda i, q_ref: (q_ref[0], i, 0)),
                      pl.BlockSpec((3, tr, D), lambda i, q_ref: (0, i, 0))],
            out_specs=pl.BlockSpec((tr, D), lambda i, q_ref: (i, 0))),
        out_shape=jax.ShapeDtypeStruct((r, D), F32), compiler_params=_cparams(),
    )(chip, part, got)


def _sum_blocks(name, gathered, r):
    def body(g_ref, o_ref):
        acc = g_ref[0]
        for d in range(1, N_DEV):
            acc = acc + g_ref[d]
        o_ref[...] = acc

    tr = 8
    return pl.pallas_call(
        body, name=name, grid=(r // tr,),
        in_specs=[pl.BlockSpec((N_DEV, tr, D), lambda i: (0, i, 0))],
        out_specs=pl.BlockSpec((tr, D), lambda i: (i, 0)),
        out_shape=jax.ShapeDtypeStruct((r, D), F32), compiler_params=_cparams(),
    )(gathered.reshape(N_DEV, r, D))


class _WeightGather:
    def __init__(self, groups):
        self.state, token = {}, ()
        for key, names, lands in groups:
            send, recv, lz, tok = _gather_start("gather_start_%s_%d" % key[::-1], lands, token)
            self.state[key] = (names, send, recv, lz)
            token = (tok,)
        self.started = token

    def fetch(self, layer, group, marker):
        names, send, recv, lz = self.state.pop((layer, group))
        tag = "%s_%d" % (group, layer)
        fsend, frecv, lz, tok = _gather_forward("gather_forward_" + tag, lz, recv, marker or self.started)
        lz = _gather_finish("gather_finish_" + tag, lz, send, recv, fsend, frecv, (tok,))
        return dict(zip(names, lz))


class _GradReduce:
    def __init__(self, core, chip):
        self.core, self.chip = core, chip
        self.layer = None
        self.token = ()
        self.at_sibling, self.at_chips = [], []

    def after(self):
        return self.token

    def add(self, group, names, grads):
        tag = "%s_%d" % (group, self.layer)
        send, recv, grads, lands, tok = _sibling_start("grad_sibling_start_" + tag, grads, self.token)
        self.at_sibling.append((tag, [(self.layer, n) for n in names], send, recv, grads, lands))
        self.token = (tok,)

    def advance(self, marker):
        for tag, keys, send, recv, grads, lands in self.at_sibling:
            grads, lands = _sibling_finish("grad_sibling_finish_" + tag, grads, lands, send, recv, marker)
            parts = [_chip_partial("chip_partial_%d_%s" % key, g, got, self.core, 256)
                     for key, g, got in zip(keys, grads, lands)]
            send, recv, parts, lands, tok = _chip_start("grad_chip_start_" + tag, parts, ())
            self.at_chips.append((tag, keys, send, recv, parts, lands))
            self.token = (tok,)
        self.at_sibling = []

    def finish(self, marker):
        self.advance(marker)
        grads = {}
        for tag, keys, send, recv, parts, lands in self.at_chips:
            parts, lands = _chip_finish("grad_chip_finish_" + tag, parts, lands, send, recv, marker)
            for key, part, got in zip(keys, parts, lands):
                grads[key] = _chip_sum("chip_sum_%d_%s" % key, part, got, self.chip, 256)
        return grads


def _adamw_math(w, g, m, v):
    m = ADAM_B1 * m + (1.0 - ADAM_B1) * g
    v = ADAM_B2 * v + (1.0 - ADAM_B2) * jnp.square(g)
    m_hat = m / (1.0 - ADAM_B1 ** ADAM_STEP)
    v_hat = v / (1.0 - ADAM_B2 ** ADAM_STEP)
    delta = -ADAM_LR * (m_hat / (jnp.sqrt(v_hat) + ADAM_EPS) + ADAM_WD * w)
    return delta, m, v


def _adamw(name, w, g, m, v, tr):
    nl, r, cdim = w.shape
    tr = min(tr, r)

    def body(w_ref, g_ref, m_ref, v_ref, d_ref, nm_ref, nv_ref):
        d_ref[...], nm_ref[...], nv_ref[...] = _adamw_math(w_ref[...], g_ref[...], m_ref[...], v_ref[...])

    blk = pl.BlockSpec((None, tr, cdim), lambda l, i: (l, i, 0))
    shape = jax.ShapeDtypeStruct((nl, r, cdim), F32)
    return pl.pallas_call(
        body, name=name, grid=(nl, r // tr), in_specs=[blk] * 4, out_specs=[blk] * 3, out_shape=[shape] * 3,
        compiler_params=_cparams(),
    )(w, g, m, v)


def _adamw_layers(name, w, g0, g1, m, v, tr):
    nl, r, cdim = w.shape
    tr = min(tr, r)
    nb = r // tr

    def body(w_ref, g0_ref, g1_ref, m_ref, v_ref, g_ref, d_ref, nm_ref, nv_ref):
        g = jnp.where(pl.program_id(0) == 0, g0_ref[...], g1_ref[...])
        g_ref[...] = g
        d_ref[...], nm_ref[...], nv_ref[...] = _adamw_math(w_ref[...], g, m_ref[...], v_ref[...])

    blk = pl.BlockSpec((None, tr, cdim), lambda l, i: (l, i, 0))
    g0_spec = pl.BlockSpec((tr, cdim), lambda l, i: (i * (1 - l) + (nb - 1) * l, 0))
    g1_spec = pl.BlockSpec((tr, cdim), lambda l, i: (i * l, 0))
    shape = jax.ShapeDtypeStruct((nl, r, cdim), F32)
    return pl.pallas_call(
        body, name=name, grid=(nl, nb), in_specs=[blk, g0_spec, g1_spec, blk, blk], out_specs=[blk] * 4,
        out_shape=[shape] * 4, compiler_params=_cparams(),
    )(w, g0, g1, m, v)


def _to_gather_layout(name, w):
    if name in ("w_in", "w_up"):
        w = w.T
    return w.astype(BF16)


def _from_gather_layout(name, g):
    return g.T if name in ("w_in", "w_up") else g


def _pack(arrays, rows):
    flat = jnp.concatenate([a.reshape(-1) for a in arrays])
    return jnp.pad(flat, (0, rows * D - flat.shape[0])).reshape(rows, D)


def _unpack(buf, shapes):
    flat = buf.reshape(-1)
    out, off = [], 0
    for shp in shapes:
        size = 1
        for dim in shp:
            size *= dim
        out.append(flat[off:off + size].reshape(shp))
        off += size
    return out


def _rows_for(shapes, mult=8):
    total = 0
    for shp in shapes:
        size = 1
        for dim in shp:
            size *= dim
        total += size
    return -(-total // (mult * D)) * mult


def kernel(x, mem, norm_mix_pre, norm_mix_post, w_in, w_out, gmlp_v_gain, w_spatial, b_spatial, w_pool, s_pool, w_dw, b_dw, conv_ln_g, conv_ln_b, norm_xattn_pre, norm_mem, norm_xattn_post, w_q, w_k, w_v, w_o, norm_ffn_pre, norm_ffn_post, w_up, w_down, loss_target, m_norm_mix_pre, m_norm_mix_post, m_w_in, m_w_out, m_gmlp_v_gain, m_w_spatial, m_b_spatial, m_w_pool, m_s_pool, m_w_dw, m_b_dw, m_conv_ln_g, m_conv_ln_b, m_norm_xattn_pre, m_norm_mem, m_norm_xattn_post, m_w_q, m_w_k, m_w_v, m_w_o, m_norm_ffn_pre, m_norm_ffn_post, m_w_up, m_w_down, v_norm_mix_pre, v_norm_mix_post, v_w_in, v_w_out, v_gmlp_v_gain, v_w_spatial, v_b_spatial, v_w_pool, v_s_pool, v_w_dw, v_b_dw, v_conv_ln_g, v_conv_ln_b, v_norm_xattn_pre, v_norm_mem, v_norm_xattn_post, v_w_q, v_w_k, v_w_v, v_w_o, v_norm_ffn_pre, v_norm_ffn_post, v_w_up, v_w_down):
    args = dict(locals())
    wts = {n: args[n] for n in WEIGHTS}
    mom_m = {n: args["m_" + n] for n in WEIGHTS}
    mom_v = {n: args["v_" + n] for n in WEIGHTS}
    xi, yi, ci = _position()
    me = 4 * xi + 2 * yi + ci

    dev = jnp.reshape(me, (1,)).astype(jnp.int32)
    lands = {}
    for call, names, tr in (("place_att", ("w_out", "w_q", "w_k", "w_v", "w_o"), 64), ("place_down", ("w_down",), 256),
                            ("place_up", ("w_up",), 256), ("place_in", ("w_in",), 256)):
        srcs = [(jnp.swapaxes(wts[n], 1, 2) if n in ("w_in", "w_up") else wts[n], l) for l in range(DEPTH) for n in names]
        placed = _place_own(call, srcs, dev, BF16, tr)
        lands.update(zip([(l, n) for l in range(DEPTH) for n in names], placed))
    (lands[(0, "taps")],) = _place_own("place_taps", [(_pack([w_dw], _rows_for([w_dw.shape])), None)], dev, F32, 8)
    groups = []
    for l in range(DEPTH):
        for group, names in GATHER_GROUPS:
            if (l, group) == (0, "in"):
                names = names + ("taps",)
            groups.append(((l, group), names, [lands[(l, n)] for n in names]))
    gather = _WeightGather(groups)

    def fetch(layer, group, marker):
        w = gather.fetch(layer, group, marker)
        if "taps" in w:
            blocks = w["taps"].reshape(N_DEV, -1)[:, :w_dw.size].reshape((N_DEV,) + w_dw.shape)
            w["taps"] = jnp.moveaxis(blocks, 0, 2).reshape(DEPTH, CONV_K, CW)
        return w

    reduce = _GradReduce(jnp.reshape(ci, (1,)).astype(jnp.int32), jnp.reshape(2 * xi + yi, (1,)).astype(jnp.int32))
    small = {n: wts[n] for n in SMALL if n != "w_dw"}
    loss, dx, gsmall = _local_step(x[0], mem[0], loss_target[0], fetch, small, reduce)
    grads = {key: _from_gather_layout(key[1], g) for key, g in reduce.finish((dx,)).items()}

    small_shapes = [wts[n].shape[1:] for n in SMALL]
    small_shapes[SMALL.index("w_dw")] = (CONV_K, CW)
    srows = _rows_for(small_shapes * DEPTH)
    packed = _pack([gsmall[l][n] for l in range(DEPTH) for n in SMALL], srows)
    (all_small,) = _all_gather("gather_small_grads", [packed])
    summed = _unpack(_sum_blocks("sum_small_grads", all_small, srows), small_shapes * DEPTH)
    for idx, (l, n) in enumerate([(l, n) for l in range(DEPTH) for n in SMALL]):
        grads[(l, n)] = summed[idx]
    shard_cols = CW // N_DEV
    for l in range(DEPTH):
        grads[(l, "w_dw")] = lax.dynamic_slice_in_dim(grads[(l, "w_dw")], me * shard_cols, shard_cols, axis=1)

    grad_w = {n: jnp.stack([grads[(l, n)] for l in range(DEPTH)]) for n in SMALL}
    delta, new_m, new_v = {}, {}, {}
    small_all_shapes = [wts[n].shape for n in SMALL]
    arows = _rows_for(small_all_shapes, 64)
    packs = [_pack([src[n] for n in SMALL], arows)[None] for src in (wts, grad_w, mom_m, mom_v)]
    outs = _adamw("adamw_small", *packs, 64)
    for dst, buf in zip((delta, new_m, new_v), outs):
        dst.update(zip(SMALL, _unpack(buf, small_all_shapes)))
    for n in BIG:
        grad_w[n], delta[n], new_m[n], new_v[n] = _adamw_layers(
            "adamw_" + n, wts[n], grads[(0, n)], grads[(1, n)], mom_m[n], mom_v[n], 256)

    loss = lax.psum(loss, ("x", "y", "c"))
    return (loss, dx[None], *[grad_w[n] for n in WEIGHTS], *[delta[n] for n in WEIGHTS],
            *[new_m[n] for n in WEIGHTS], *[new_v[n] for n in WEIGHTS])
```

```python
import functools

import jax
import jax.numpy as jnp
from jax import lax
from jax.experimental import pallas as pl
from jax.experimental.pallas import tpu as pltpu

F32 = jnp.float32
BF16 = jnp.bfloat16

D = 2048
GW = 1024
PW = 512
CW = 512
HD = 128
NH = 8
NG = 4
POOL_WINDOWS = (2, 4, 8, 16)
CONV_K = 31
IN_COLS = 2 * GW + PW + 2 * CW
DFF = 4 * D
XH = 4
XHD = D // XH
ATT_SCALE = XHD ** -0.5
RMS_EPS = 1e-6
LN_EPS = 1e-5
DEPTH = 2
N_DEV = 8

ADAM_LR = 0.001
ADAM_B1 = 0.9
ADAM_B2 = 0.999
ADAM_EPS = 1e-08
ADAM_WD = 0.01
ADAM_STEP = 10

LANES = 128
CONV_HALO = 32
POOL_HALO = 16
ROW_TILE = 128
VMEM_LIMIT = 60 * 1024 * 1024

MESH = pl.DeviceIdType.MESH
NT = (((1,), (1,)), ((), ()))
NN = (((1,), (0,)), ((), ()))
TN = (((0,), (0,)), ((), ()))

BIG = ("w_out", "w_q", "w_k", "w_v", "w_o", "w_up", "w_down", "w_in")
GATHER_GROUPS = (("in", ("w_in",)), ("att", ("w_out", "w_q", "w_k", "w_v", "w_o")), ("ffn", ("w_up", "w_down")))
SMALL = ("norm_mix_pre", "norm_mix_post", "gmlp_v_gain", "w_spatial", "b_spatial", "w_pool", "s_pool",
         "w_dw", "b_dw", "conv_ln_g", "conv_ln_b", "norm_xattn_pre", "norm_mem", "norm_xattn_post",
         "norm_ffn_pre", "norm_ffn_post")
WEIGHTS = ("norm_mix_pre", "norm_mix_post", "w_in", "w_out", "gmlp_v_gain", "w_spatial", "b_spatial", "w_pool",
           "s_pool", "w_dw", "b_dw", "conv_ln_g", "conv_ln_b", "norm_xattn_pre", "norm_mem", "norm_xattn_post",
           "w_q", "w_k", "w_v", "w_o", "norm_ffn_pre", "norm_ffn_post", "w_up", "w_down")


def _cparams():
    return pltpu.CompilerParams(vmem_limit_bytes=VMEM_LIMIT)


def _dot(a, b, dims):
    return lax.dot_general(a, b, dims, preferred_element_type=F32)


def _rms(x, g):
    y = x * lax.rsqrt(jnp.mean(x * x, axis=-1, keepdims=True) + RMS_EPS)
    return y * g


def _gelu(x):
    cdf = 0.5 * (1.0 + jnp.tanh(0.7978845608028654 * (x + 0.044715 * (x * x * x))))
    return x * cdf


def _layer_norm(x, g, b=None):
    mu = jnp.mean(x, axis=-1, keepdims=True)
    xc = x - mu
    var = jnp.mean(xc * xc, axis=-1, keepdims=True)
    y = xc * lax.rsqrt(var + LN_EPS) * g
    return y if b is None else y + b


def _sigmoid(x):
    return 1.0 / (1.0 + jnp.exp(-x))


def _gmlp_rows(zu, zv, gv):
    return _gelu(zu), _layer_norm(_gelu(zv), gv)


def _glu(cv, cg):
    return cv * _sigmoid(cg)


def _ln_silu(h, g, b):
    y = _layer_norm(h, g, b)
    return y * _sigmoid(y)


ANY = pl.BlockSpec(memory_space=pl.ANY)


ROWS_TILE = 256
COLS_TILE = 512
DW_TILE = 512
RESIDENT_K = 2048
STREAM_K_TILE = 1024
STREAM_ROWS = 512


def _k_tiles(kdim):
    if kdim <= RESIDENT_K:
        return ROWS_TILE, kdim
    return STREAM_ROWS, max(t for t in range(LANES, STREAM_K_TILE + 1, LANES) if kdim % t == 0)


def _rowop_mm(name, kind, rows, g, w, dims, out_dtype, u=None, after=()):
    s = rows[0].shape[0]
    n = w.shape[0] if dims == NT else w.shape[1]
    tm, tn = min(ROWS_TILE, s), min(COLS_TILE, n)
    ni = s // tm
    bwd = kind == "rms_bwd"

    def rows_body(*refs):
        refs = list(refs)
        row_refs = [refs.pop(0) for _ in rows]
        g_ref = refs.pop(0)
        del refs[:len(after)]
        if bwd:
            _, vjp = jax.vjp(_rms, row_refs[0][...], g_ref[...])
            a, dg = vjp(row_refs[1][...])
            refs[1][0] = dg
        else:
            a = _rms(row_refs[0][...], g_ref[...])
        refs[0][...] = a.astype(BF16)

    row_spec = pl.BlockSpec((tm, D), lambda i: (i, 0))
    res = pl.pallas_call(
        rows_body, name=name + "_rows", grid=(ni,),
        in_specs=[row_spec] * len(rows) + [pl.BlockSpec((1, D), lambda i: (0, 0))] + [ANY] * len(after),
        out_specs=[row_spec] + ([pl.BlockSpec((1, 1, D), lambda i: (i, 0, 0))] if bwd else []),
        out_shape=[jax.ShapeDtypeStruct((s, D), BF16)] + ([jax.ShapeDtypeStruct((ni, 1, D), F32)] if bwd else []),
        compiler_params=_cparams(),
    )(*rows, g, *after)
    a = res[0]

    def body(a_ref, w_ref, *rest):
        acc = _dot(a_ref[...], w_ref[...], dims)
        if u is not None:
            acc = acc * (2.0 * jnp.maximum(rest[0][...], 0.0))
        rest[-1][...] = acc.astype(out_dtype)

    w_spec = pl.BlockSpec((tn, D), lambda j: (j, 0)) if dims == NT else pl.BlockSpec((D, tn), lambda j: (0, j))
    tile = pl.BlockSpec((s, tn), lambda j: (0, j))
    out = pl.pallas_call(
        body, name=name, grid=(n // tn,),
        in_specs=[pl.BlockSpec((s, D), lambda j: (0, 0)), w_spec] + ([tile] if u is not None else []),
        out_specs=tile, out_shape=jax.ShapeDtypeStruct((s, n), out_dtype), compiler_params=_cparams(),
    )(a, w, *([u] if u is not None else []))
    return (out, *res)


def _mm_rowop(name, kind, pairs, rows, g, relu2=False, after=()):
    s, kdim = pairs[0][0].shape
    tm, tk = _k_tiles(kdim)
    tm = min(tm, s)
    ni, nk = s // tm, kdim // tk
    npair = len(pairs)

    def body(*refs):
        refs = list(refs)
        a_refs = [refs.pop(0) for _ in range(npair)]
        w_refs = [refs.pop(0) for _ in range(npair)]
        row_refs = [refs.pop(0) for _ in rows]
        g_ref = refs.pop(0)
        del refs[:len(after)]
        acc = refs.pop()
        outs = refs
        k = pl.program_id(1)

        @pl.when(k == 0)
        def _():
            acc[...] = jnp.zeros_like(acc)

        for a_ref, w_ref, (_, _, dims) in zip(a_refs, w_refs, pairs):
            a = a_ref[...]
            if relu2:
                a = jnp.square(jnp.maximum(a, 0.0))
            acc[...] += _dot(a.astype(BF16), w_ref[...], dims)

        @pl.when(k == nk - 1)
        def _():
            h = acc[...]
            if kind == "rms_res":
                outs[0][...] = row_refs[0][...] + _rms(h, g_ref[...])
                outs[1][...] = h
            else:
                _, vjp = jax.vjp(_rms, row_refs[0][...], g_ref[...])
                dx, dg = vjp(h)
                if kind == "rms_bwd_res":
                    outs[0][...] = row_refs[1][...] + dx
                    outs[1][0] = dg
                else:
                    outs[0][0] = dg

    row_spec = pl.BlockSpec((tm, D), lambda i, k: (i, 0))
    dg_shape = jax.ShapeDtypeStruct((ni, 1, D), F32)
    dg_spec = pl.BlockSpec((1, 1, D), lambda i, k: (i, 0, 0))
    in_specs = [pl.BlockSpec((tm, tk), lambda i, k: (i, k))] * npair
    for _, _, dims in pairs:
        in_specs.append(pl.BlockSpec((tk, D), lambda i, k: (k, 0)) if dims == NN
                        else pl.BlockSpec((D, tk), lambda i, k: (0, k)))
    in_specs += [row_spec] * len(rows) + [pl.BlockSpec((1, D), lambda i, k: (0, 0))] + [ANY] * len(after)
    if kind == "rms_res":
        out_shape = [jax.ShapeDtypeStruct((s, D), F32)] * 2
        out_specs = [row_spec, row_spec]
    elif kind == "rms_bwd_res":
        out_shape = [jax.ShapeDtypeStruct((s, D), F32), dg_shape]
        out_specs = [row_spec, dg_spec]
    else:
        out_shape = [dg_shape]
        out_specs = [dg_spec]
    return pl.pallas_call(
        body, name=name, grid=(ni, nk), in_specs=in_specs, out_specs=out_specs, out_shape=out_shape,
        scratch_shapes=[pltpu.VMEM((tm, D), F32)], compiler_params=_cparams(),
    )(*[p[0] for p in pairs], *[p[1] for p in pairs], *rows, g, *after)


def _mm_tn(name, a, gmat, relu2=False, after=()):
    s, m = a.shape
    tm, ts = min(DW_TILE, m), s
    ni, ns = m // tm, s // ts

    def body(a_ref, g_ref, *rest):
        o_ref, acc = rest[len(after):]
        k = pl.program_id(1)

        @pl.when(k == 0)
        def _():
            acc[...] = jnp.zeros_like(acc)

        av = a_ref[...]
        if relu2:
            av = jnp.square(jnp.maximum(av, 0.0))
        acc[...] += _dot(av.astype(BF16), g_ref[...], TN)

        @pl.when(k == ns - 1)
        def _():
            o_ref[...] = acc[...].astype(BF16)

    return pl.pallas_call(
        body, name=name, grid=(ni, ns),
        in_specs=[pl.BlockSpec((ts, tm), lambda i, k: (k, i)), pl.BlockSpec((ts, D), lambda i, k: (k, 0))]
        + [ANY] * len(after),
        out_specs=pl.BlockSpec((tm, D), lambda i, k: (i, 0)),
        out_shape=jax.ShapeDtypeStruct((m, D), BF16),
        scratch_shapes=[pltpu.VMEM((tm, D), F32)], compiler_params=_cparams(),
    )(a, gmat, *after)


def _tril():
    r = lax.broadcasted_iota(jnp.int32, (HD, HD), 0)
    c = lax.broadcasted_iota(jnp.int32, (HD, HD), 1)
    return (c <= r).astype(F32)


def _gmlp_fwd(z, gv, ws, bst, tb):
    s = z.shape[0]
    tb = min(tb, s)

    def body(zu_ref, zv_ref, gv_ref, ws_ref, bst_ref, y_ref):
        tril = _tril()
        for h in range(NH):
            cols = slice(h * HD, (h + 1) * HD)
            u, vln = _gmlp_rows(zu_ref[:, cols], zv_ref[:, cols], gv_ref[h:h + 1, :])
            wm = (ws_ref[h] * tril).astype(BF16)
            vb = vln.astype(BF16)
            for c in range(tb // HD):
                rws = slice(c * HD, (c + 1) * HD)
                mixed = _dot(wm, vb[rws], NN) + bst_ref[:, h:h + 1]
                y_ref[rws, cols] = (u[rws] * mixed).astype(BF16)

    return pl.pallas_call(
        body, name="gmlp_fwd", grid=(s // tb,),
        in_specs=[pl.BlockSpec((tb, GW), lambda i: (i, 0)), pl.BlockSpec((tb, GW), lambda i: (i, 1)),
                  pl.BlockSpec((NH, HD), lambda i: (0, 0)), pl.BlockSpec((NH, HD, HD), lambda i: (0, 0, 0)),
                  pl.BlockSpec((HD, NH), lambda i: (0, 0))],
        out_specs=pl.BlockSpec((tb, GW), lambda i: (i, 0)),
        out_shape=jax.ShapeDtypeStruct((s, GW), BF16), compiler_params=_cparams(),
    )(z, z, gv, ws, bst)


def _gmlp_bwd(z, dy, gv, ws, bst, tb):
    s = z.shape[0]
    tb = min(tb, s)
    nb = s // tb

    def body(zu_ref, zv_ref, dy_ref, gv_ref, ws_ref, bst_ref, dzu_ref, dzv_ref, dgv_ref, dws_ref, db_ref):
        tril = _tril()
        for h in range(NH):
            cols = slice(h * HD, (h + 1) * HD)
            (u, vln), vjp = jax.vjp(_gmlp_rows, zu_ref[:, cols], zv_ref[:, cols], gv_ref[h:h + 1, :])
            wmf = ws_ref[h] * tril
            wm = wmf.astype(BF16)
            wmt = wmf.T.astype(BF16)
            vb = vln.astype(BF16)
            dws = jnp.zeros((HD, HD), F32)
            db = jnp.zeros((HD, 1), F32)
            du_parts, dvln_parts = [], []
            for c in range(tb // HD):
                rws = slice(c * HD, (c + 1) * HD)
                mixed = _dot(wm, vb[rws], NN) + bst_ref[:, h:h + 1]
                dyc = dy_ref[rws, cols]
                du_parts.append(dyc * mixed)
                dmixed = dyc * u[rws]
                dmb = dmixed.astype(BF16)
                dws = dws + _dot(dmb, vb[rws], NT)
                db = db + jnp.sum(dmixed, axis=1, keepdims=True)
                dvln_parts.append(_dot(wmt, dmb, NN))
            du = jnp.concatenate(du_parts, axis=0)
            dvln = jnp.concatenate(dvln_parts, axis=0)
            dzu, dzv, dgv = vjp((du, dvln))
            dzu_ref[:, cols] = dzu.astype(BF16)
            dzv_ref[:, cols] = dzv.astype(BF16)
            dgv_ref[0, h:h + 1, :] = dgv
            dws_ref[0, h] = dws * tril
            db_ref[0, h] = jnp.broadcast_to(db, (HD, LANES))

    blk = pl.BlockSpec((tb, GW), lambda i: (i, 0))
    return pl.pallas_call(
        body, name="gmlp_bwd", grid=(nb,),
        in_specs=[blk, pl.BlockSpec((tb, GW), lambda i: (i, 1)), blk,
                  pl.BlockSpec((NH, HD), lambda i: (0, 0)), pl.BlockSpec((NH, HD, HD), lambda i: (0, 0, 0)),
                  pl.BlockSpec((HD, NH), lambda i: (0, 0))],
        out_specs=[blk, blk, pl.BlockSpec((1, NH, HD), lambda i: (i, 0, 0)),
                   pl.BlockSpec((1, NH, HD, HD), lambda i: (i, 0, 0, 0)),
                   pl.BlockSpec((1, NH, HD, LANES), lambda i: (i, 0, 0, 0))],
        out_shape=[jax.ShapeDtypeStruct((s, GW), BF16), jax.ShapeDtypeStruct((s, GW), BF16),
                   jax.ShapeDtypeStruct((nb, NH, HD), F32), jax.ShapeDtypeStruct((nb, NH, HD, HD), F32),
                   jax.ShapeDtypeStruct((nb, NH, HD, LANES), F32)],
        compiler_params=_cparams(),
    )(z, z, dy, gv, ws, bst)


def _pool_count(t0, window):
    pos = (t0 + lax.broadcasted_iota(jnp.int32, (ROW_TILE, LANES), 0)).astype(F32)
    return jnp.minimum(pos + 1.0, float(window))


def _window_sum(win, levels, back):
    n = win.shape[0]
    for lv in range(levels):
        step = 1 << lv
        win = win + pltpu.roll(win, n - step if back else step, 0)
    return win


def _pool_pooled(ppad_ref, t0, g):
    win = ppad_ref[pl.ds(t0, ROW_TILE + POOL_HALO), :]
    wsum = _window_sum(win, g + 1, False)[POOL_HALO:]
    return wsum / _pool_count(t0, POOL_WINDOWS[g]) - win[POOL_HALO:]


def _pool_fwd(z, wp, sp):
    s = z.shape[0]
    nt = s // ROW_TILE

    def body(p_ref, wp_ref, sp_ref, y_ref, ppad):
        for g in range(NG):
            cols = slice(g * LANES, (g + 1) * LANES)
            ppad[pl.ds(0, POOL_HALO), :] = jnp.zeros((POOL_HALO, LANES), F32)
            ppad[pl.ds(POOL_HALO, s), :] = p_ref[:, cols]
            wpb = wp_ref[g].astype(BF16)
            scale = sp_ref[:, cols]

            def tile(t, carry):
                t0 = pl.multiple_of(t * ROW_TILE, ROW_TILE)
                pooled = _pool_pooled(ppad, t0, g)
                y_ref[pl.ds(t0, ROW_TILE), cols] = (_dot(pooled.astype(BF16), wpb, NN) * scale).astype(BF16)
                return carry

            lax.fori_loop(0, nt, tile, 0)

    return pl.pallas_call(
        body, name="pool_fwd", grid=(1,),
        in_specs=[pl.BlockSpec((s, PW), lambda i: (0, 2 * GW // PW)),
                  pl.BlockSpec((NG, LANES, LANES), lambda i: (0, 0, 0)), pl.BlockSpec((1, PW), lambda i: (0, 0))],
        out_specs=pl.BlockSpec((s, PW), lambda i: (0, 0)),
        out_shape=jax.ShapeDtypeStruct((s, PW), BF16),
        scratch_shapes=[pltpu.VMEM((s + POOL_HALO, LANES), F32)], compiler_params=_cparams(),
    )(z, wp, sp)


def _pool_bwd(z, dy, wp, sp):
    s = z.shape[0]
    nt = s // ROW_TILE

    def body(p_ref, dy_ref, wp_ref, sp_ref, dp_ref, dwp_ref, dsp_ref, ppad, rpad, dpool):
        for g in range(NG):
            cols = slice(g * LANES, (g + 1) * LANES)
            ppad[pl.ds(0, POOL_HALO), :] = jnp.zeros((POOL_HALO, LANES), F32)
            ppad[pl.ds(POOL_HALO, s), :] = p_ref[:, cols]
            rpad[pl.ds(s, POOL_HALO), :] = jnp.zeros((POOL_HALO, LANES), F32)
            wpb = wp_ref[g].astype(BF16)
            scale = sp_ref[:, cols]

            def tile(t, carry):
                dwp, dsp = carry
                t0 = pl.multiple_of(t * ROW_TILE, ROW_TILE)
                pooled = _pool_pooled(ppad, t0, g)
                pb = pooled.astype(BF16)
                dyt = dy_ref[pl.ds(t0, ROW_TILE), cols]
                dsp = dsp + jnp.sum(dyt * _dot(pb, wpb, NN), axis=0, keepdims=True)
                dmm = (dyt * scale).astype(BF16)
                dwp = dwp + _dot(pb, dmm, TN)
                dpooled = _dot(dmm, wpb, NT)
                rpad[pl.ds(t0, ROW_TILE), :] = dpooled / _pool_count(t0, POOL_WINDOWS[g])
                dpool[pl.ds(t0, ROW_TILE), :] = dpooled
                return dwp, dsp

            dwp, dsp = lax.fori_loop(0, nt, tile, (jnp.zeros((LANES, LANES), F32), jnp.zeros((1, LANES), F32)))
            dwp_ref[g] = dwp
            dsp_ref[:, cols] = dsp

            def tile2(t, carry):
                t0 = pl.multiple_of(t * ROW_TILE, ROW_TILE)
                win = rpad[pl.ds(t0, ROW_TILE + POOL_HALO), :]
                back = _window_sum(win, g + 1, True)[:ROW_TILE]
                rows = pl.ds(t0, ROW_TILE)
                dp_ref[rows, cols] = (back - dpool[rows, :]).astype(BF16)
                return carry

            lax.fori_loop(0, nt, tile2, 0)

    return pl.pallas_call(
        body, name="pool_bwd", grid=(1,),
        in_specs=[pl.BlockSpec((s, PW), lambda i: (0, 2 * GW // PW)), pl.BlockSpec((s, PW), lambda i: (0, GW // PW)),
                  pl.BlockSpec((NG, LANES, LANES), lambda i: (0, 0, 0)), pl.BlockSpec((1, PW), lambda i: (0, 0))],
        out_specs=[pl.BlockSpec((s, PW), lambda i: (0, 0)), pl.BlockSpec((NG, LANES, LANES), lambda i: (0, 0, 0)),
                   pl.BlockSpec((1, PW), lambda i: (0, 0))],
        out_shape=[jax.ShapeDtypeStruct((s, PW), BF16), jax.ShapeDtypeStruct((NG, LANES, LANES), F32),
                   jax.ShapeDtypeStruct((1, PW), F32)],
        scratch_shapes=[pltpu.VMEM((s + POOL_HALO, LANES), F32), pltpu.VMEM((s + POOL_HALO, LANES), F32),
                        pltpu.VMEM((s, LANES), F32)],
        compiler_params=_cparams(),
    )(z, dy, wp, sp)


CONV_LEAD = CONV_HALO - (CONV_K - 1)


def _conv_taps(win, wdw_ref, lead, reverse):
    n = win.shape[0]
    acc = jnp.zeros((ROW_TILE, CW), F32)
    for j in range(CONV_K):
        tap = (CONV_K - 1 - j) if reverse else j
        acc = acc + wdw_ref[tap:tap + 1, :] * pltpu.roll(win, (n - (lead + j)) % n, 0)[:ROW_TILE]
    return acc


def _conv_fill_glu(cv_ref, cg_ref, xpad, s):
    xpad[pl.ds(0, CONV_HALO), :] = jnp.zeros((CONV_HALO, CW), F32)

    def fill(t, carry):
        t0 = pl.multiple_of(t * ROW_TILE, ROW_TILE)
        rows = pl.ds(t0, ROW_TILE)
        xpad[pl.ds(t0 + CONV_HALO, ROW_TILE), :] = _glu(cv_ref[rows, :], cg_ref[rows, :])
        return carry

    lax.fori_loop(0, s // ROW_TILE, fill, 0)


def _conv_fwd(z, wdw, bdw, lng, lnb):
    s = z.shape[0]

    def body(cv_ref, cg_ref, wdw_ref, bdw_ref, lng_ref, lnb_ref, y_ref, xpad):
        _conv_fill_glu(cv_ref, cg_ref, xpad, s)

        def tile(t, carry):
            t0 = pl.multiple_of(t * ROW_TILE, ROW_TILE)
            win = xpad[pl.ds(t0, ROW_TILE + CONV_HALO), :]
            hc = _conv_taps(win, wdw_ref, CONV_LEAD, False) + bdw_ref[...]
            y_ref[pl.ds(t0, ROW_TILE), :] = _ln_silu(hc, lng_ref[...], lnb_ref[...]).astype(BF16)
            return carry

        lax.fori_loop(0, s // ROW_TILE, tile, 0)

    vec = pl.BlockSpec((1, CW), lambda i: (0, 0))
    return pl.pallas_call(
        body, name="conv_fwd", grid=(1,),
        in_specs=[pl.BlockSpec((s, CW), lambda i: (0, (2 * GW + PW) // CW)),
                  pl.BlockSpec((s, CW), lambda i: (0, (2 * GW + PW) // CW + 1)),
                  pl.BlockSpec((CONV_K + 1, CW), lambda i: (0, 0)), vec, vec, vec],
        out_specs=pl.BlockSpec((s, CW), lambda i: (0, 0)),
        out_shape=jax.ShapeDtypeStruct((s, CW), BF16),
        scratch_shapes=[pltpu.VMEM((s + CONV_HALO, CW), F32)], compiler_params=_cparams(),
    )(z, z, wdw, bdw, lng, lnb)


def _conv_bwd(z, dy, wdw, bdw, lng, lnb):
    s = z.shape[0]

    def body(cv_ref, cg_ref, dy_ref, wdw_ref, bdw_ref, lng_ref, lnb_ref,
             dcv_ref, dcg_ref, dwdw_ref, dbdw_ref, dlng_ref, dlnb_ref, xpad, dpad):
        _conv_fill_glu(cv_ref, cg_ref, xpad, s)
        dpad[pl.ds(s, CONV_HALO), :] = jnp.zeros((CONV_HALO, CW), F32)
        dwdw_ref[...] = jnp.zeros((CONV_K + 1, CW), F32)

        def tile(t, carry):
            db, dg, dbeta = carry
            t0 = pl.multiple_of(t * ROW_TILE, ROW_TILE)
            win = xpad[pl.ds(t0, ROW_TILE + CONV_HALO), :]
            hc = _conv_taps(win, wdw_ref, CONV_LEAD, False) + bdw_ref[...]
            _, vjp = jax.vjp(_ln_silu, hc, lng_ref[...], lnb_ref[...])
            dhc, dg_t, dbeta_t = vjp(dy_ref[pl.ds(t0, ROW_TILE), :])
            dpad[pl.ds(t0, ROW_TILE), :] = dhc
            n = win.shape[0]
            for j in range(CONV_K):
                shifted = pltpu.roll(win, (n - (CONV_LEAD + j)) % n, 0)[:ROW_TILE]
                dwdw_ref[j:j + 1, :] += jnp.sum(dhc * shifted, axis=0, keepdims=True)
            return db + jnp.sum(dhc, axis=0, keepdims=True), dg + dg_t, dbeta + dbeta_t

        zero = jnp.zeros((1, CW), F32)
        db, dg, dbeta = lax.fori_loop(0, s // ROW_TILE, tile, (zero, zero, zero))
        dbdw_ref[...] = db
        dlng_ref[...] = dg
        dlnb_ref[...] = dbeta

        def tile2(t, carry):
            t0 = pl.multiple_of(t * ROW_TILE, ROW_TILE)
            rows = pl.ds(t0, ROW_TILE)
            win = dpad[pl.ds(t0, ROW_TILE + CONV_HALO), :]
            dglu = _conv_taps(win, wdw_ref, 0, True)
            _, vjp = jax.vjp(_glu, cv_ref[rows, :], cg_ref[rows, :])
            dcv, dcg = vjp(dglu)
            dcv_ref[rows, :] = dcv.astype(BF16)
            dcg_ref[rows, :] = dcg.astype(BF16)
            return carry

        lax.fori_loop(0, s // ROW_TILE, tile2, 0)

    vec = pl.BlockSpec((1, CW), lambda i: (0, 0))
    full = pl.BlockSpec((s, CW), lambda i: (0, 0))
    wspec = pl.BlockSpec((CONV_K + 1, CW), lambda i: (0, 0))
    vshape = jax.ShapeDtypeStruct((1, CW), F32)
    return pl.pallas_call(
        body, name="conv_bwd", grid=(1,),
        in_specs=[pl.BlockSpec((s, CW), lambda i: (0, (2 * GW + PW) // CW)),
                  pl.BlockSpec((s, CW), lambda i: (0, (2 * GW + PW) // CW + 1)),
                  pl.BlockSpec((s, CW), lambda i: (0, (GW + PW) // CW)), wspec, vec, vec, vec],
        out_specs=[full, full, wspec, vec, vec, vec],
        out_shape=[jax.ShapeDtypeStruct((s, CW), BF16), jax.ShapeDtypeStruct((s, CW), BF16),
                   jax.ShapeDtypeStruct((CONV_K + 1, CW), F32), vshape, vshape, vshape],
        scratch_shapes=[pltpu.VMEM((s + CONV_HALO, CW), F32), pltpu.VMEM((s + CONV_HALO, CW), F32)],
        compiler_params=_cparams(),
    )(z, z, dy, wdw, bdw, lng, lnb)


def _softmax_rows(sc):
    e = jnp.exp(sc - jnp.max(sc, axis=-1, keepdims=True))
    return e / jnp.sum(e, axis=-1, keepdims=True)


def _attn_fwd(q, k, v, tq):
    s, m = q.shape[0], k.shape[0]
    tq = min(tq, s)

    def body(q_ref, k_ref, v_ref, o_ref):
        for h in range(XH):
            cols = slice(h * XHD, (h + 1) * XHD)
            p = _softmax_rows(_dot(q_ref[:, cols], k_ref[:, cols], NT) * ATT_SCALE)
            o_ref[:, cols] = _dot(p.astype(BF16), v_ref[:, cols], NN).astype(BF16)

    kv = pl.BlockSpec((m, D), lambda i: (0, 0))
    return pl.pallas_call(
        body, name="attn_fwd", grid=(s // tq,),
        in_specs=[pl.BlockSpec((tq, D), lambda i: (i, 0)), kv, kv],
        out_specs=pl.BlockSpec((tq, D), lambda i: (i, 0)),
        out_shape=jax.ShapeDtypeStruct((s, D), BF16), compiler_params=_cparams(),
    )(q, k, v)


def _attn_bwd(q, k, v, do, tq):
    s, m = q.shape[0], k.shape[0]
    tq = min(tq, s)

    def body(q_ref, k_ref, v_ref, do_ref, dq_ref, dk_ref, dv_ref):
        @pl.when(pl.program_id(0) == 0)
        def _():
            dk_ref[...] = jnp.zeros_like(dk_ref)
            dv_ref[...] = jnp.zeros_like(dv_ref)

        for h in range(XH):
            cols = slice(h * XHD, (h + 1) * XHD)
            qh, kh, vh, doh = q_ref[:, cols], k_ref[:, cols], v_ref[:, cols], do_ref[:, cols]
            p = _softmax_rows(_dot(qh, kh, NT) * ATT_SCALE)
            dp = _dot(doh, vh, NT)
            dv_ref[:, cols] += _dot(p.astype(BF16), doh, TN)
            ds = (p * (dp - jnp.sum(p * dp, axis=-1, keepdims=True)) * ATT_SCALE).astype(BF16)
            dq_ref[:, cols] = _dot(ds, kh, NN).astype(BF16)
            dk_ref[:, cols] += _dot(ds, qh, TN)

    kv = pl.BlockSpec((m, D), lambda i: (0, 0))
    qs = pl.BlockSpec((tq, D), lambda i: (i, 0))
    return pl.pallas_call(
        body, name="attn_bwd", grid=(s // tq,),
        in_specs=[qs, kv, kv, qs], out_specs=[qs, kv, kv],
        out_shape=[jax.ShapeDtypeStruct((s, D), BF16), jax.ShapeDtypeStruct((m, D), F32),
                   jax.ShapeDtypeStruct((m, D), F32)],
        compiler_params=_cparams(),
    )(q, k, v, do)


def _loss_head(y, target, tm):
    s = y.shape[0]
    tm = min(tm, s)

    def body(y_ref, t_ref, dy_ref, part_ref):
        err = y_ref[...] - t_ref[...]
        dy_ref[...] = err * (1.0 / D)
        part_ref[...] = jnp.full((1, 8, LANES), 0.5 * jnp.sum(err * err) * (1.0 / D), F32)

    blk = pl.BlockSpec((tm, D), lambda i: (i, 0))
    dy, part = pl.pallas_call(
        body, name="loss_head", grid=(s // tm,), in_specs=[blk, blk],
        out_specs=[blk, pl.BlockSpec((1, 8, LANES), lambda i: (i, 0, 0))],
        out_shape=[jax.ShapeDtypeStruct((s, D), F32), jax.ShapeDtypeStruct((s // tm, 8, LANES), F32)],
        compiler_params=_cparams(),
    )(y, target)
    return dy, jnp.sum(part[:, 0, 0])


def _layer_fwd(x0, mem, w, p, fetch):
    z, hn0 = _rowop_mm("mix_in", "rms", (x0,), p["norm_mix_pre"], w["w_in"], NT, F32)
    ya = _gmlp_fwd(z, p["gmlp_v_gain"], p["w_spatial"], p["b_spatial_t"], 512)
    yb = _pool_fwd(z, p["w_pool"], p["s_pool"])
    yc = _conv_fwd(z, p["w_dw"], p["b_dw"], p["conv_ln_g"], p["conv_ln_b"])
    y = jnp.concatenate([ya, yb, yc], axis=1)
    w.update(fetch("att", (y,)))
    x1, h0 = _mm_rowop("mix_out", "rms_res", [(y, w["w_out"], NN)], (x0,), p["norm_mix_post"])
    q, hn1 = _rowop_mm("att_q", "rms", (x1,), p["norm_xattn_pre"], w["w_q"], NN, BF16)
    k, mn = _rowop_mm("att_k", "rms", (mem,), p["norm_mem"], w["w_k"], NN, BF16)
    v, _ = _rowop_mm("att_v", "rms", (mem,), p["norm_mem"], w["w_v"], NN, BF16)
    o = _attn_fwd(q, k, v, 256)
    x2, h1 = _mm_rowop("att_o", "rms_res", [(o, w["w_o"], NN)], (x1,), p["norm_xattn_post"])
    w.update(fetch("ffn", (x2,)))
    u, hn2 = _rowop_mm("ffn_up", "rms", (x2,), p["norm_ffn_pre"], w["w_up"], NT, F32)
    x3, h2 = _mm_rowop("ffn_down", "rms_res", [(u, w["w_down"], NN)], (x2,), p["norm_ffn_post"], relu2=True)
    saved = dict(x0=x0, z=z, hn0=hn0, y=y, h0=h0, x1=x1, q=q, hn1=hn1, k=k, v=v, mn=mn, o=o, h1=h1, x2=x2, u=u,
                 hn2=hn2, h2=h2)
    return x3, saved


def _layer_bwd(dx3, mem, w, p, sv, red):
    gs = {}
    du, dh2, dg = _rowop_mm("ffn_down_bwd", "rms_bwd", (sv["h2"], dx3), p["norm_ffn_post"], w["w_down"], NT, BF16,
                            u=sv["u"], after=red.after())
    gs["norm_ffn_post"] = jnp.sum(dg, axis=0)
    g_down = _mm_tn("ffn_down_dw", sv["u"], dh2, relu2=True)
    red.advance((g_down,))
    dx2, dg = _mm_rowop("ffn_up_bwd", "rms_bwd_res", [(du, w["w_up"], NN)], (sv["x2"], dx3), p["norm_ffn_pre"],
                        after=red.after())
    gs["norm_ffn_pre"] = jnp.sum(dg, axis=0)
    g_up = _mm_tn("ffn_up_dw", du, sv["hn2"])
    red.add("ffn", ("w_down", "w_up"), [g_down, g_up])
    do, dh1, dg = _rowop_mm("att_o_bwd", "rms_bwd", (sv["h1"], dx2), p["norm_xattn_post"], w["w_o"], NT, BF16,
                            after=red.after())
    gs["norm_xattn_post"] = jnp.sum(dg, axis=0)
    g_o = _mm_tn("att_o_dw", sv["o"], dh1)
    red.advance((g_o,))
    dq, dk, dv = _attn_bwd(sv["q"], sv["k"], sv["v"], do, 256)
    dk, dv = dk.astype(BF16), dv.astype(BF16)
    dx1, dg = _mm_rowop("att_q_bwd", "rms_bwd_res", [(dq, w["w_q"], NT)], (sv["x1"], dx2), p["norm_xattn_pre"],
                        after=red.after())
    gs["norm_xattn_pre"] = jnp.sum(dg, axis=0)
    g_q = _mm_tn("att_q_dw", sv["hn1"], dq)
    g_k = _mm_tn("att_k_dw", sv["mn"], dk)
    g_v = _mm_tn("att_v_dw", sv["mn"], dv)
    (dg,) = _mm_rowop("att_kv_bwd", "rms_bwd_gain", [(dk, w["w_k"], NT), (dv, w["w_v"], NT)], (mem,), p["norm_mem"])
    gs["norm_mem"] = jnp.sum(dg, axis=0)
    red.add("att", ("w_o", "w_q", "w_k", "w_v"), [g_o, g_q, g_k, g_v])
    dy, dh0, dg = _rowop_mm("mix_out_bwd", "rms_bwd", (sv["h0"], dx1), p["norm_mix_post"], w["w_out"], NT, F32,
                            after=red.after())
    gs["norm_mix_post"] = jnp.sum(dg, axis=0)
    g_out = _mm_tn("mix_out_dw", sv["y"], dh0)
    red.advance((g_out,))
    red.add("out", ("w_out",), [g_out])
    z = sv["z"]
    dzu, dzv, dgv, dws, dbs = _gmlp_bwd(z, dy, p["gmlp_v_gain"], p["w_spatial"], p["b_spatial_t"], 512)
    gs["gmlp_v_gain"] = jnp.sum(dgv, axis=0)
    gs["w_spatial"] = jnp.sum(dws, axis=0)
    gs["b_spatial"] = jnp.sum(dbs[..., 0], axis=0)
    dp, gs["w_pool"], gs["s_pool"] = _pool_bwd(z, dy, p["w_pool"], p["s_pool"])
    dcv, dcg, dwdw, gs["b_dw"], gs["conv_ln_g"], gs["conv_ln_b"] = _conv_bwd(
        z, dy, p["w_dw"], p["b_dw"], p["conv_ln_g"], p["conv_ln_b"])
    gs["w_dw"] = dwdw[:CONV_K]
    dz = jnp.concatenate([dzu, dzv, dp, dcv, dcg], axis=1)
    red.advance((dz,))
    g_in = _mm_tn("mix_in_dw", dz, sv["hn0"], after=red.after())
    red.add("in", ("w_in",), [g_in])
    dx0, dg = _mm_rowop("mix_in_bwd", "rms_bwd_res", [(dz, w["w_in"], NN)], (sv["x0"], dx1), p["norm_mix_pre"],
                        after=red.after())
    gs["norm_mix_pre"] = jnp.sum(dg, axis=0)
    return dx0, gs


def _layer_params(small, l):
    p = {n: small[n][l].reshape(1, -1) for n in ("norm_mix_pre", "norm_mix_post", "s_pool", "b_dw", "conv_ln_g",
                                                   "conv_ln_b", "norm_xattn_pre", "norm_mem", "norm_xattn_post",
                                                   "norm_ffn_pre", "norm_ffn_post")}
    p["gmlp_v_gain"] = small["gmlp_v_gain"][l]
    p["w_spatial"] = small["w_spatial"][l]
    p["b_spatial_t"] = small["b_spatial"][l].T
    p["w_pool"] = small["w_pool"][l]
    p["w_dw"] = jnp.pad(small["w_dw"][l], ((0, 1), (0, 0)))
    return p


def _local_step(x, mem, target, fetch, small, red):
    small = dict(small)
    saved, weights, params = [], [], []
    h = x
    marker = ()
    for l in range(DEPTH):
        w = fetch(l, "in", marker)
        if "taps" in w:
            small["w_dw"] = w.pop("taps")
        p = _layer_params(small, l)
        h, sv = _layer_fwd(h, mem, w, p, functools.partial(fetch, l))
        marker = (h,)
        saved.append(sv)
        weights.append(w)
        params.append(p)
    dh, loss = _loss_head(h, target, 512)
    gsmall = [None] * DEPTH
    for l in reversed(range(DEPTH)):
        red.layer = l
        dh, gs = _layer_bwd(dh, mem, weights[l], params[l], saved[l], red)
        gsmall[l] = {n: gs[n].reshape(small[n].shape[1:]) for n in SMALL}
    return loss, dh, gsmall


HBM = pl.BlockSpec(memory_space=pltpu.HBM)


def _position():
    return lax.axis_index("x"), lax.axis_index("y"), lax.axis_index("c")


def _all_gather(name, shards):
    n = len(shards)

    def body(*refs):
        ins, outs = refs[:n], refs[n:2 * n]
        send_sems, recv_sems, local_sems = refs[2 * n:]
        x, y, c = _position()
        me, sibling = (x, y, c), (x, y, 1 - c)
        chips = [(1 - x, y), (x, 1 - y), (1 - x, 1 - y)]

        def rows(a, dev):
            r = shards[a].shape[0]
            return outs[a].at[pl.ds((4 * dev[0] + 2 * dev[1] + dev[2]) * r, r), :]

        def copy(a, k, block, to, src=None):
            return pltpu.make_async_remote_copy(
                src_ref=rows(a, block) if src is None else src, dst_ref=rows(a, block),
                send_sem=send_sems.at[a, k], recv_sem=recv_sems.at[a, k], device_id=to, device_id_type=MESH)

        started = []
        for a in range(n):
            mine = pltpu.make_async_copy(ins[a], rows(a, me), local_sems.at[a])
            mine.start()
            started.append(mine)
        first = []
        for a in range(n):
            for j, chip in enumerate(chips):
                first.append(copy(a, 1 + j, me, (*chip, c), src=ins[a]))
        for a in range(n):
            first.append(copy(a, 0, me, sibling, src=ins[a]))
        for cp in first:
            cp.start()
        passed = []
        for a in range(n):
            for j, chip in enumerate(chips):
                copy(a, 1 + j, (*chip, c), me).wait_recv()
                fwd = copy(a, 4 + j, (*chip, c), sibling)
                fwd.start()
                passed.append(fwd)
        for a in range(n):
            copy(a, 0, sibling, me).wait_recv()
            for j, chip in enumerate(chips):
                copy(a, 4 + j, (*chip, 1 - c), me).wait_recv()
        for cp in first + passed:
            cp.wait_send()
        for mine in started:
            mine.wait()

    return pl.pallas_call(
        body, name=name, in_specs=[HBM] * n, out_specs=[HBM] * n,
        out_shape=[jax.ShapeDtypeStruct((N_DEV * s.shape[0], s.shape[1]), s.dtype) for s in shards],
        scratch_shapes=[pltpu.SemaphoreType.DMA((n, 7)), pltpu.SemaphoreType.DMA((n, 7)), pltpu.SemaphoreType.DMA((n,))],
    )(*shards)


SEM = pl.BlockSpec(memory_space=pltpu.SEMAPHORE)
EFFECT = pltpu.SideEffectType.DATAFLOW_SIDE_EFFECTING
TOKEN = jax.ShapeDtypeStruct((8, LANES), F32)
TOKEN_SPEC = pl.BlockSpec(memory_space=pltpu.VMEM)


def _landing(shape, dtype):
    return pltpu.with_memory_space_constraint(lax.empty(shape, dtype), pltpu.HBM)


def _hbm_shapes(arrays):
    return [pltpu.HBM(a.shape, a.dtype) for a in arrays]


def _block(ref, r, dev):
    return ref.at[pl.ds((4 * dev[0] + 2 * dev[1] + dev[2]) * r, r), :]


def _split_call(name, body, thru, sems_in, after, sems_out, token):
    n = len(thru)
    out_shape = [pltpu.SemaphoreType.DMA(s) for s in sems_out] + _hbm_shapes(thru) + ([TOKEN] if token else [])
    out_specs = [SEM] * len(sems_out) + [HBM] * n + ([TOKEN_SPEC] if token else [])
    return pl.pallas_call(
        body, name=name, in_specs=[HBM] * n + [SEM] * len(sems_in) + [ANY] * len(after),
        out_specs=out_specs, out_shape=out_shape,
        input_output_aliases={i: len(sems_out) + i for i in range(n)},
        compiler_params=pltpu.CompilerParams(has_side_effects=EFFECT),
    )(*thru, *sems_in, *after)


def _place_own(name, srcs, dev, out_dtype, tr):
    n = len(srcs)
    r, cols = srcs[0][0].shape[-2:]
    tr = r if r < 16 else _row_tile(r, tr)
    nb = r // tr

    def body(dev_ref, *refs):
        for a in range(n):
            refs[n + a][...] = refs[a][...].astype(out_dtype)

    in_specs = [pl.BlockSpec((tr, cols), lambda i, d: (i, 0)) if l is None
                else pl.BlockSpec((None, tr, cols), lambda i, d, l=l: (l, i, 0)) for _, l in srcs]
    return pl.pallas_call(
        body, name=name,
        grid_spec=pltpu.PrefetchScalarGridSpec(
            num_scalar_prefetch=1, grid=(nb,), in_specs=in_specs,
            out_specs=[pl.BlockSpec((tr, cols), lambda i, d: (d[0] * nb + i, 0))] * n),
        out_shape=[jax.ShapeDtypeStruct((N_DEV * r, cols), out_dtype)] * n, compiler_params=_cparams(),
    )(dev, *[a for a, _ in srcs])


def _gather_peers(x, y, c):
    return [(1 - x, y, c), (x, 1 - y, c), (1 - x, 1 - y, c), (x, y, 1 - c)]


def _block_rows(land):
    return land.shape[0] // N_DEV


def _gather_start(name, lands, after):
    n = len(lands)

    def body(*refs):
        lz = refs[:n]
        send_sems, recv_sems = refs[n + len(after)], refs[n + len(after) + 1]
        token = refs[-1]
        x, y, c = _position()
        for a in range(n):
            own = _block(lz[a], _block_rows(lands[a]), (x, y, c))
            for k, to in enumerate(_gather_peers(x, y, c)):
                pltpu.make_async_remote_copy(src_ref=own, dst_ref=own, send_sem=send_sems.at[k], recv_sem=recv_sems.at[k],
                                             device_id=to, device_id_type=MESH).start()
        token[...] = jnp.zeros_like(token)

    out = _split_call(name, body, list(lands), [], after, [(4,), (4,)], True)
    return out[0], out[1], out[2:2 + n], out[-1]


def _gather_forward(name, lands, recv_sems, after):
    n = len(lands)

    def body(*refs):
        lz = refs[:n]
        recv0 = refs[n]
        fsend, frecv = refs[n + 1 + len(after)], refs[n + 2 + len(after)]
        token = refs[-1]
        x, y, c = _position()
        chips = _gather_peers(x, y, c)[:3]
        for a in range(n):
            for j, chip in enumerate(chips):
                blk = _block(lz[a], _block_rows(lands[a]), chip)
                pltpu.make_async_remote_copy(src_ref=blk, dst_ref=blk, send_sem=fsend.at[j], recv_sem=recv0.at[j],
                                             device_id=(x, y, c), device_id_type=MESH).wait_recv()
        for a in range(n):
            for j, chip in enumerate(chips):
                blk = _block(lz[a], _block_rows(lands[a]), chip)
                pltpu.make_async_remote_copy(src_ref=blk, dst_ref=blk, send_sem=fsend.at[j], recv_sem=frecv.at[j],
                                             device_id=(x, y, 1 - c), device_id_type=MESH).start()
        token[...] = jnp.zeros_like(token)

    out = _split_call(name, body, list(lands), [recv_sems], after, [(3,), (3,)], True)
    return out[0], out[1], out[2:2 + n], out[-1]


def _gather_finish(name, lands, send_sems, recv_sems, fsend, frecv, after):
    n = len(lands)

    def body(*refs):
        lz = refs[:n]
        send0, recv0, fsend_ref, frecv_ref = refs[n:n + 4]
        x, y, c = _position()
        me = (x, y, c)
        chips = _gather_peers(x, y, c)[:3]
        for a in range(n):
            r = _block_rows(lands[a])
            sib = _block(lz[a], r, (x, y, 1 - c))
            pltpu.make_async_remote_copy(src_ref=sib, dst_ref=sib, send_sem=send0.at[3], recv_sem=recv0.at[3],
                                         device_id=me, device_id_type=MESH).wait_recv()
            for j, chip in enumerate(chips):
                blk = _block(lz[a], r, (chip[0], chip[1], 1 - c))
                pltpu.make_async_remote_copy(src_ref=blk, dst_ref=blk, send_sem=fsend_ref.at[j], recv_sem=frecv_ref.at[j],
                                             device_id=me, device_id_type=MESH).wait_recv()
            own = _block(lz[a], r, me)
            for k in range(4):
                pltpu.make_async_remote_copy(src_ref=own, dst_ref=own, send_sem=send0.at[k], recv_sem=recv0.at[k],
                                             device_id=me, device_id_type=MESH).wait_send()
            for j, chip in enumerate(chips):
                blk = _block(lz[a], r, chip)
                pltpu.make_async_remote_copy(src_ref=blk, dst_ref=blk, send_sem=fsend_ref.at[j], recv_sem=frecv_ref.at[j],
                                             device_id=me, device_id_type=MESH).wait_send()

    return _split_call(name, body, list(lands), [send_sems, recv_sems, fsend, frecv], after, [], False)


def _sibling_start(name, grads, after):
    n = len(grads)
    lands = [_landing((4, g.shape[0] // N_DEV, D), g.dtype) for g in grads]

    def body(*refs):
        ins, lz = refs[:n], refs[n:2 * n]
        send_sem, recv_sem = refs[2 * n + len(after)], refs[2 * n + len(after) + 1]
        token = refs[-1]
        x, y, c = _position()
        for a in range(n):
            r = grads[a].shape[0] // N_DEV
            for q in range(4):
                pltpu.make_async_remote_copy(
                    src_ref=ins[a].at[pl.ds((2 * q + 1 - c) * r, r), :], dst_ref=lz[a].at[q], send_sem=send_sem.at[0],
                    recv_sem=recv_sem.at[0], device_id=(x, y, 1 - c), device_id_type=MESH).start()
        token[...] = jnp.zeros_like(token)

    out = _split_call(name, body, list(grads) + lands, [], after, [(1,), (1,)], True)
    return out[0], out[1], out[2:2 + n], out[2 + n:2 + 2 * n], out[-1]


def _sibling_finish(name, grads, lands, send_sem, recv_sem, after):
    n = len(grads)

    def body(*refs):
        ins, lz = refs[:n], refs[n:2 * n]
        send_ref, recv_ref = refs[2 * n], refs[2 * n + 1]
        x, y, c = _position()
        for a in range(n):
            r = grads[a].shape[0] // N_DEV
            for q in range(4):
                cp = pltpu.make_async_remote_copy(
                    src_ref=ins[a].at[pl.ds((2 * q + 1 - c) * r, r), :], dst_ref=lz[a].at[q], send_sem=send_ref.at[0],
                    recv_sem=recv_ref.at[0], device_id=(x, y, c), device_id_type=MESH)
                cp.wait_send()
                cp.wait_recv()

    out = _split_call(name, body, list(grads) + list(lands), [send_sem, recv_sem], after, [], False)
    return out[:n], out[n:2 * n]


def _chip_start(name, parts, after):
    n = len(parts)
    lands = [_landing((3,) + p.shape[1:], p.dtype) for p in parts]

    def body(*refs):
        ins, lz = refs[:n], refs[n:2 * n]
        send_sems, recv_sems = refs[2 * n + len(after)], refs[2 * n + len(after) + 1]
        token = refs[-1]
        x, y, c = _position()
        for a in range(n):
            for j, chip in enumerate(_gather_peers(x, y, c)[:3]):
                pltpu.make_async_remote_copy(
                    src_ref=ins[a].at[2 * chip[0] + chip[1]], dst_ref=lz[a].at[j], send_sem=send_sems.at[j],
                    recv_sem=recv_sems.at[j], device_id=chip, device_id_type=MESH).start()
        token[...] = jnp.zeros_like(token)

    out = _split_call(name, body, list(parts) + lands, [], after, [(3,), (3,)], True)
    return out[0], out[1], out[2:2 + n], out[2 + n:2 + 2 * n], out[-1]


def _chip_finish(name, parts, lands, send_sems, recv_sems, after):
    n = len(parts)

    def body(*refs):
        ins, lz = refs[:n], refs[n:2 * n]
        send_ref, recv_ref = refs[2 * n], refs[2 * n + 1]
        me = _position()
        for a in range(n):
            for j in range(3):
                cp = pltpu.make_async_remote_copy(
                    src_ref=ins[a].at[j], dst_ref=lz[a].at[j], send_sem=send_ref.at[j], recv_sem=recv_ref.at[j],
                    device_id=me, device_id_type=MESH)
                cp.wait_send()
                cp.wait_recv()

    out = _split_call(name, body, list(parts) + list(lands), [send_sems, recv_sems], after, [], False)
    return out[:n], out[n:2 * n]


def _row_tile(r, target):
    return max(t for t in range(16, min(r, target) + 1, 16) if r % t == 0)


def _chip_partial(name, grad, got, c, tr):
    r = grad.shape[0] // N_DEV
    tr = _row_tile(r, tr)
    g4 = grad.reshape(4, 2, r, D)

    def body(c_ref, g_ref, s_ref, o_ref):
        o_ref[...] = (g_ref[...].astype(F32) + s_ref[...].astype(F32)).astype(BF16)

    return pl.pallas_call(
        body, name=name,
        grid_spec=pltpu.PrefetchScalarGridSpec(
            num_scalar_prefetch=1, grid=(4, r // tr),
            in_specs=[pl.BlockSpec((None, None, tr, D), lambda q, i, c_ref: (q, c_ref[0], i, 0)),
                      pl.BlockSpec((None, tr, D), lambda q, i, c_ref: (q, i, 0))],
            out_specs=pl.BlockSpec((None, tr, D), lambda q, i, c_ref: (q, i, 0))),
        out_shape=jax.ShapeDtypeStruct((4, r, D), BF16), compiler_params=_cparams(),
    )(c, g4, got)


def _chip_sum(name, part, got, chip, tr):
    r = part.shape[1]
    tr = _row_tile(r, tr)

    def body(q_ref, p_ref, g_ref, o_ref):
        acc = p_ref[...].astype(F32)
        for j in range(3):
            acc = acc + g_ref[j].astype(F32)
        o_ref[...] = acc

    return pl.pallas_call(
        body, name=name,
        grid_spec=pltpu.PrefetchScalarGridSpec(
            num_scalar_prefetch=1, grid=(r // tr,),
            in_specs=[pl.BlockSpec((None, tr, D), lambda i, q_ref: (q_ref[0], i, 0)),
                      pl.BlockSpec((3, tr, D), lambda i, q_ref: (0, i, 0))],
            out_specs=pl.BlockSpec((tr, D), lambda i, q_ref: (i, 0))),
        out_shape=jax.ShapeDtypeStruct((r, D), F32), compiler_params=_cparams(),
    )(chip, part, got)


def _sum_blocks(name, gathered, r):
    def body(g_ref, o_ref):
        acc = g_ref[0]
        for d in range(1, N_DEV):
            acc = acc + g_ref[d]
        o_ref[...] = acc

    tr = 8
    return pl.pallas_call(
        body, name=name, grid=(r // tr,),
        in_specs=[pl.BlockSpec((N_DEV, tr, D), lambda i: (0, i, 0))],
        out_specs=pl.BlockSpec((tr, D), lambda i: (i, 0)),
        out_shape=jax.ShapeDtypeStruct((r, D), F32), compiler_params=_cparams(),
    )(gathered.reshape(N_DEV, r, D))


class _WeightGather:
    def __init__(self, groups):
        self.state, token = {}, ()
        for key, names, lands in groups:
            send, recv, lz, tok = _gather_start("gather_start_%s_%d" % key[::-1], lands, token)
            self.state[key] = (names, send, recv, lz)
            token = (tok,)
        self.started = token

    def fetch(self, layer, group, marker):
        names, send, recv, lz = self.state.pop((layer, group))
        tag = "%s_%d" % (group, layer)
        fsend, frecv, lz, tok = _gather_forward("gather_forward_" + tag, lz, recv, marker or self.started)
        lz = _gather_finish("gather_finish_" + tag, lz, send, recv, fsend, frecv, (tok,))
        return dict(zip(names, lz))


class _GradReduce:
    def __init__(self, core, chip):
        self.core, self.chip = core, chip
        self.layer = None
        self.token = ()
        self.at_sibling, self.at_chips = [], []

    def after(self):
        return self.token

    def add(self, group, names, grads):
        tag = "%s_%d" % (group, self.layer)
        send, recv, grads, lands, tok = _sibling_start("grad_sibling_start_" + tag, grads, self.token)
        self.at_sibling.append((tag, [(self.layer, n) for n in names], send, recv, grads, lands))
        self.token = (tok,)

    def advance(self, marker):
        for tag, keys, send, recv, grads, lands in self.at_sibling:
            grads, lands = _sibling_finish("grad_sibling_finish_" + tag, grads, lands, send, recv, marker)
            parts = [_chip_partial("chip_partial_%d_%s" % key, g, got, self.core, 256)
                     for key, g, got in zip(keys, grads, lands)]
            send, recv, parts, lands, tok = _chip_start("grad_chip_start_" + tag, parts, ())
            self.at_chips.append((tag, keys, send, recv, parts, lands))
            self.token = (tok,)
        self.at_sibling = []

    def finish(self, marker):
        self.advance(marker)
        grads = {}
        for tag, keys, send, recv, parts, lands in self.at_chips:
            parts, lands = _chip_finish("grad_chip_finish_" + tag, parts, lands, send, recv, marker)
            for key, part, got in zip(keys, parts, lands):
                grads[key] = _chip_sum("chip_sum_%d_%s" % key, part, got, self.chip, 256)
        return grads


def _adamw_math(w, g, m, v):
    m = ADAM_B1 * m + (1.0 - ADAM_B1) * g
    v = ADAM_B2 * v + (1.0 - ADAM_B2) * jnp.square(g)
    m_hat = m / (1.0 - ADAM_B1 ** ADAM_STEP)
    v_hat = v / (1.0 - ADAM_B2 ** ADAM_STEP)
    delta = -ADAM_LR * (m_hat / (jnp.sqrt(v_hat) + ADAM_EPS) + ADAM_WD * w)
    return delta, m, v


def _adamw(name, w, g, m, v, tr):
    nl, r, cdim = w.shape
    tr = min(tr, r)

    def body(w_ref, g_ref, m_ref, v_ref, d_ref, nm_ref, nv_ref):
        d_ref[...], nm_ref[...], nv_ref[...] = _adamw_math(w_ref[...], g_ref[...], m_ref[...], v_ref[...])

    blk = pl.BlockSpec((None, tr, cdim), lambda l, i: (l, i, 0))
    shape = jax.ShapeDtypeStruct((nl, r, cdim), F32)
    return pl.pallas_call(
        body, name=name, grid=(nl, r // tr), in_specs=[blk] * 4, out_specs=[blk] * 3, out_shape=[shape] * 3,
        compiler_params=_cparams(),
    )(w, g, m, v)


def _adamw_layers(name, w, g0, g1, m, v, tr):
    nl, r, cdim = w.shape
    tr = min(tr, r)
    nb = r // tr

    def body(w_ref, g0_ref, g1_ref, m_ref, v_ref, g_ref, d_ref, nm_ref, nv_ref):
        g = jnp.where(pl.program_id(0) == 0, g0_ref[...], g1_ref[...])
        g_ref[...] = g
        d_ref[...], nm_ref[...], nv_ref[...] = _adamw_math(w_ref[...], g, m_ref[...], v_ref[...])

    blk = pl.BlockSpec((None, tr, cdim), lambda l, i: (l, i, 0))
    g0_spec = pl.BlockSpec((tr, cdim), lambda l, i: (i * (1 - l) + (nb - 1) * l, 0))
    g1_spec = pl.BlockSpec((tr, cdim), lambda l, i: (i * l, 0))
    shape = jax.ShapeDtypeStruct((nl, r, cdim), F32)
    return pl.pallas_call(
        body, name=name, grid=(nl, nb), in_specs=[blk, g0_spec, g1_spec, blk, blk], out_specs=[blk] * 4,
        out_shape=[shape] * 4, compiler_params=_cparams(),
    )(w, g0, g1, m, v)


def _to_gather_layout(name, w):
    if name in ("w_in", "w_up"):
        w = w.T
    return w.astype(BF16)


def _from_gather_layout(name, g):
    return g.T if name in ("w_in", "w_up") else g


def _pack(arrays, rows):
    flat = jnp.concatenate([a.reshape(-1) for a in arrays])
    return jnp.pad(flat, (0, rows * D - flat.shape[0])).reshape(rows, D)


def _unpack(buf, shapes):
    flat = buf.reshape(-1)
    out, off = [], 0
    for shp in shapes:
        size = 1
        for dim in shp:
            size *= dim
        out.append(flat[off:off + size].reshape(shp))
        off += size
    return out


def _rows_for(shapes, mult=8):
    total = 0
    for shp in shapes:
        size = 1
        for dim in shp:
            size *= dim
        total += size
    return -(-total // (mult * D)) * mult


def kernel(x, mem, norm_mix_pre, norm_mix_post, w_in, w_out, gmlp_v_gain, w_spatial, b_spatial, w_pool, s_pool, w_dw, b_dw, conv_ln_g, conv_ln_b, norm_xattn_pre, norm_mem, norm_xattn_post, w_q, w_k, w_v, w_o, norm_ffn_pre, norm_ffn_post, w_up, w_down, loss_target, m_norm_mix_pre, m_norm_mix_post, m_w_in, m_w_out, m_gmlp_v_gain, m_w_spatial, m_b_spatial, m_w_pool, m_s_pool, m_w_dw, m_b_dw, m_conv_ln_g, m_conv_ln_b, m_norm_xattn_pre, m_norm_mem, m_norm_xattn_post, m_w_q, m_w_k, m_w_v, m_w_o, m_norm_ffn_pre, m_norm_ffn_post, m_w_up, m_w_down, v_norm_mix_pre, v_norm_mix_post, v_w_in, v_w_out, v_gmlp_v_gain, v_w_spatial, v_b_spatial, v_w_pool, v_s_pool, v_w_dw, v_b_dw, v_conv_ln_g, v_conv_ln_b, v_norm_xattn_pre, v_norm_mem, v_norm_xattn_post, v_w_q, v_w_k, v_w_v, v_w_o, v_norm_ffn_pre, v_norm_ffn_post, v_w_up, v_w_down):
    args = dict(locals())
    wts = {n: args[n] for n in WEIGHTS}
    mom_m = {n: args["m_" + n] for n in WEIGHTS}
    mom_v = {n: args["v_" + n] for n in WEIGHTS}
    xi, yi, ci = _position()
    me = 4 * xi + 2 * yi + ci

    dev = jnp.reshape(me, (1,)).astype(jnp.int32)
    lands = {}
    for call, names, tr in (("place_att", ("w_out", "w_q", "w_k", "w_v", "w_o"), 64), ("place_down", ("w_down",), 256),
                            ("place_up", ("w_up",), 256), ("place_in", ("w_in",), 256)):
        srcs = [(jnp.swapaxes(wts[n], 1, 2) if n in ("w_in", "w_up") else wts[n], l) for l in range(DEPTH) for n in names]
        placed = _place_own(call, srcs, dev, BF16, tr)
        lands.update(zip([(l, n) for l in range(DEPTH) for n in names], placed))
    (lands[(0, "taps")],) = _place_own("place_taps", [(_pack([w_dw], _rows_for([w_dw.shape])), None)], dev, F32, 8)
    groups = []
    for l in range(DEPTH):
        for group, names in GATHER_GROUPS:
            if (l, group) == (0, "in"):
                names = names + ("taps",)
            groups.append(((l, group), names, [lands[(l, n)] for n in names]))
    gather = _WeightGather(groups)

    def fetch(layer, group, marker):
        w = gather.fetch(layer, group, marker)
        if "taps" in w:
            blocks = w["taps"].reshape(N_DEV, -1)[:, :w_dw.size].reshape((N_DEV,) + w_dw.shape)
            w["taps"] = jnp.moveaxis(blocks, 0, 2).reshape(DEPTH, CONV_K, CW)
        return w

    reduce = _GradReduce(jnp.reshape(ci, (1,)).astype(jnp.int32), jnp.reshape(2 * xi + yi, (1,)).astype(jnp.int32))
    small = {n: wts[n] for n in SMALL if n != "w_dw"}
    loss, dx, gsmall = _local_step(x[0], mem[0], loss_target[0], fetch, small, reduce)
    grads = {key: _from_gather_layout(key[1], g) for key, g in reduce.finish((dx,)).items()}

    small_shapes = [wts[n].shape[1:] for n in SMALL]
    small_shapes[SMALL.index("w_dw")] = (CONV_K, CW)
    srows = _rows_for(small_shapes * DEPTH)
    packed = _pack([gsmall[l][n] for l in range(DEPTH) for n in SMALL], srows)
    (all_small,) = _all_gather("gather_small_grads", [packed])
    summed = _unpack(_sum_blocks("sum_small_grads", all_small, srows), small_shapes * DEPTH)
    for idx, (l, n) in enumerate([(l, n) for l in range(DEPTH) for n in SMALL]):
        grads[(l, n)] = summed[idx]
    shard_cols = CW // N_DEV
    for l in range(DEPTH):
        grads[(l, "w_dw")] = lax.dynamic_slice_in_dim(grads[(l, "w_dw")], me * shard_cols, shard_cols, axis=1)

    grad_w = {n: jnp.stack([grads[(l, n)] for l in range(DEPTH)]) for n in SMALL}
    delta, new_m, new_v = {}, {}, {}
    small_all_shapes = [wts[n].shape for n in SMALL]
    arows = _rows_for(small_all_shapes, 64)
    packs = [_pack([src[n] for n in SMALL], arows)[None] for src in (wts, grad_w, mom_m, mom_v)]
    outs = _adamw("adamw_small", *packs, 64)
    for dst, buf in zip((delta, new_m, new_v), outs):
        dst.update(zip(SMALL, _unpack(buf, small_all_shapes)))
    for n in BIG:
        grad_w[n], delta[n], new_m[n], new_v[n] = _adamw_layers(
            "adamw_" + n, wts[n], grads[(0, n)], grads[(1, n)], mom_m[n], mom_v[n], 256)

    loss = lax.psum(loss, ("x", "y", "c"))
    return (loss, dx[None], *[grad_w[n] for n in WEIGHTS], *[delta[n] for n in WEIGHTS],
            *[new_m[n] for n in WEIGHTS], *[new_v[n] for n in WEIGHTS])
```

```python
import functools

import jax
import jax.numpy as jnp
from jax import lax
from jax.experimental import pallas as pl
from jax.experimental.pallas import tpu as pltpu

F32 = jnp.float32
BF16 = jnp.bfloat16

D = 2048
GW = 1024
PW = 512
CW = 512
HD = 128
NH = 8
NG = 4
POOL_WINDOWS = (2, 4, 8, 16)
CONV_K = 31
IN_COLS = 2 * GW + PW + 2 * CW
DFF = 4 * D
XH = 4
XHD = D // XH
ATT_SCALE = XHD ** -0.5
RMS_EPS = 1e-6
LN_EPS = 1e-5
DEPTH = 2
N_DEV = 8

ADAM_LR = 0.001
ADAM_B1 = 0.9
ADAM_B2 = 0.999
ADAM_EPS = 1e-08
ADAM_WD = 0.01
ADAM_STEP = 10

LANES = 128
CONV_HALO = 32
POOL_HALO = 16
ROW_TILE = 128
VMEM_LIMIT = 60 * 1024 * 1024

MESH = pl.DeviceIdType.MESH
NT = (((1,), (1,)), ((), ()))
NN = (((1,), (0,)), ((), ()))
TN = (((0,), (0,)), ((), ()))

BIG = ("w_out", "w_q", "w_k", "w_v", "w_o", "w_up", "w_down", "w_in")
GATHER_GROUPS = (("in", ("w_in",)), ("att", ("w_out", "w_q", "w_k", "w_v", "w_o")), ("ffn", ("w_up", "w_down")))
SMALL = ("norm_mix_pre", "norm_mix_post", "gmlp_v_gain", "w_spatial", "b_spatial", "w_pool", "s_pool",
         "w_dw", "b_dw", "conv_ln_g", "conv_ln_b", "norm_xattn_pre", "norm_mem", "norm_xattn_post",
         "norm_ffn_pre", "norm_ffn_post")
WEIGHTS = ("norm_mix_pre", "norm_mix_post", "w_in", "w_out", "gmlp_v_gain", "w_spatial", "b_spatial", "w_pool",
           "s_pool", "w_dw", "b_dw", "conv_ln_g", "conv_ln_b", "norm_xattn_pre", "norm_mem", "norm_xattn_post",
           "w_q", "w_k", "w_v", "w_o", "norm_ffn_pre", "norm_ffn_post", "w_up", "w_down")


def _cparams():
    return pltpu.CompilerParams(vmem_limit_bytes=VMEM_LIMIT)


def _dot(a, b, dims):
    return lax.dot_general(a, b, dims, preferred_element_type=F32)


def _rms(x, g):
    y = x * lax.rsqrt(jnp.mean(x * x, axis=-1, keepdims=True) + RMS_EPS)
    return y * g


def _gelu(x):
    cdf = 0.5 * (1.0 + jnp.tanh(0.7978845608028654 * (x + 0.044715 * (x * x * x))))
    return x * cdf


def _layer_norm(x, g, b=None):
    mu = jnp.mean(x, axis=-1, keepdims=True)
    xc = x - mu
    var = jnp.mean(xc * xc, axis=-1, keepdims=True)
    y = xc * lax.rsqrt(var + LN_EPS) * g
    return y if b is None else y + b


def _sigmoid(x):
    return 1.0 / (1.0 + jnp.exp(-x))


def _gmlp_rows(zu, zv, gv):
    return _gelu(zu), _layer_norm(_gelu(zv), gv)


def _glu(cv, cg):
    return cv * _sigmoid(cg)


def _ln_silu(h, g, b):
    y = _layer_norm(h, g, b)
    return y * _sigmoid(y)


ANY = pl.BlockSpec(memory_space=pl.ANY)


ROWS_TILE = 256
COLS_TILE = 512
DW_TILE = 512
RESIDENT_K = 2048
STREAM_K_TILE = 1024
STREAM_ROWS = 512


def _k_tiles(kdim):
    if kdim <= RESIDENT_K:
        return ROWS_TILE, kdim
    return STREAM_ROWS, max(t for t in range(LANES, STREAM_K_TILE + 1, LANES) if kdim % t == 0)


def _rowop_mm(name, kind, rows, g, w, dims, out_dtype, u=None, after=()):
    s = rows[0].shape[0]
    n = w.shape[0] if dims == NT else w.shape[1]
    tm, tn = min(ROWS_TILE, s), min(COLS_TILE, n)
    ni = s // tm
    bwd = kind == "rms_bwd"

    def rows_body(*refs):
        refs = list(refs)
        row_refs = [refs.pop(0) for _ in rows]
        g_ref = refs.pop(0)
        del refs[:len(after)]
        if bwd:
            _, vjp = jax.vjp(_rms, row_refs[0][...], g_ref[...])
            a, dg = vjp(row_refs[1][...])
            refs[1][0] = dg
        else:
            a = _rms(row_refs[0][...], g_ref[...])
        refs[0][...] = a.astype(BF16)

    row_spec = pl.BlockSpec((tm, D), lambda i: (i, 0))
    res = pl.pallas_call(
        rows_body, name=name + "_rows", grid=(ni,),
        in_specs=[row_spec] * len(rows) + [pl.BlockSpec((1, D), lambda i: (0, 0))] + [ANY] * len(after),
        out_specs=[row_spec] + ([pl.BlockSpec((1, 1, D), lambda i: (i, 0, 0))] if bwd else []),
        out_shape=[jax.ShapeDtypeStruct((s, D), BF16)] + ([jax.ShapeDtypeStruct((ni, 1, D), F32)] if bwd else []),
        compiler_params=_cparams(),
    )(*rows, g, *after)
    a = res[0]

    def body(a_ref, w_ref, *rest):
        acc = _dot(a_ref[...], w_ref[...], dims)
        if u is not None:
            acc = acc * (2.0 * jnp.maximum(rest[0][...], 0.0))
        rest[-1][...] = acc.astype(out_dtype)

    w_spec = pl.BlockSpec((tn, D), lambda j: (j, 0)) if dims == NT else pl.BlockSpec((D, tn), lambda j: (0, j))
    tile = pl.BlockSpec((s, tn), lambda j: (0, j))
    out = pl.pallas_call(
        body, name=name, grid=(n // tn,),
        in_specs=[pl.BlockSpec((s, D), lambda j: (0, 0)), w_spec] + ([tile] if u is not None else []),
        out_specs=tile, out_shape=jax.ShapeDtypeStruct((s, n), out_dtype), compiler_params=_cparams(),
    )(a, w, *([u] if u is not None else []))
    return (out, *res)


def _mm_rowop(name, kind, pairs, rows, g, relu2=False, after=()):
    s, kdim = pairs[0][0].shape
    tm, tk = _k_tiles(kdim)
    tm = min(tm, s)
    ni, nk = s // tm, kdim // tk
    npair = len(pairs)

    def body(*refs):
        refs = list(refs)
        a_refs = [refs.pop(0) for _ in range(npair)]
        w_refs = [refs.pop(0) for _ in range(npair)]
        row_refs = [refs.pop(0) for _ in rows]
        g_ref = refs.pop(0)
        del refs[:len(after)]
        acc = refs.pop()
        outs = refs
        k = pl.program_id(1)

        @pl.when(k == 0)
        def _():
            acc[...] = jnp.zeros_like(acc)

        for a_ref, w_ref, (_, _, dims) in zip(a_refs, w_refs, pairs):
            a = a_ref[...]
            if relu2:
                a = jnp.square(jnp.maximum(a, 0.0))
            acc[...] += _dot(a.astype(BF16), w_ref[...], dims)

        @pl.when(k == nk - 1)
        def _():
            h = acc[...]
            if kind == "rms_res":
                outs[0][...] = row_refs[0][...] + _rms(h, g_ref[...])
                outs[1][...] = h
            else:
                _, vjp = jax.vjp(_rms, row_refs[0][...], g_ref[...])
                dx, dg = vjp(h)
                if kind == "rms_bwd_res":
                    outs[0][...] = row_refs[1][...] + dx
                    outs[1][0] = dg
                else:
                    outs[0][0] = dg

    row_spec = pl.BlockSpec((tm, D), lambda i, k: (i, 0))
    dg_shape = jax.ShapeDtypeStruct((ni, 1, D), F32)
    dg_spec = pl.BlockSpec((1, 1, D), lambda i, k: (i, 0, 0))
    in_specs = [pl.BlockSpec((tm, tk), lambda i, k: (i, k))] * npair
    for _, _, dims in pairs:
        in_specs.append(pl.BlockSpec((tk, D), lambda i, k: (k, 0)) if dims == NN
                        else pl.BlockSpec((D, tk), lambda i, k: (0, k)))
    in_specs += [row_spec] * len(rows) + [pl.BlockSpec((1, D), lambda i, k: (0, 0))] + [ANY] * len(after)
    if kind == "rms_res":
        out_shape = [jax.ShapeDtypeStruct((s, D), F32)] * 2
        out_specs = [row_spec, row_spec]
    elif kind == "rms_bwd_res":
        out_shape = [jax.ShapeDtypeStruct((s, D), F32), dg_shape]
        out_specs = [row_spec, dg_spec]
    else:
        out_shape = [dg_shape]
        out_specs = [dg_spec]
    return pl.pallas_call(
        body, name=name, grid=(ni, nk), in_specs=in_specs, out_specs=out_specs, out_shape=out_shape,
        scratch_shapes=[pltpu.VMEM((tm, D), F32)], compiler_params=_cparams(),
    )(*[p[0] for p in pairs], *[p[1] for p in pairs], *rows, g, *after)


def _mm_tn(name, a, gmat, relu2=False, after=()):
    s, m = a.shape
    tm, ts = min(DW_TILE, m), s
    ni, ns = m // tm, s // ts

    def body(a_ref, g_ref, *rest):
        o_ref, acc = rest[len(after):]
        k = pl.program_id(1)

        @pl.when(k == 0)
        def _():
            acc[...] = jnp.zeros_like(acc)

        av = a_ref[...]
        if relu2:
            av = jnp.square(jnp.maximum(av, 0.0))
        acc[...] += _dot(av.astype(BF16), g_ref[...], TN)

        @pl.when(k == ns - 1)
        def _():
            o_ref[...] = acc[...].astype(BF16)

    return pl.pallas_call(
        body, name=name, grid=(ni, ns),
        in_specs=[pl.BlockSpec((ts, tm), lambda i, k: (k, i)), pl.BlockSpec((ts, D), lambda i, k: (k, 0))]
        + [ANY] * len(after),
        out_specs=pl.BlockSpec((tm, D), lambda i, k: (i, 0)),
        out_shape=jax.ShapeDtypeStruct((m, D), BF16),
        scratch_shapes=[pltpu.VMEM((tm, D), F32)], compiler_params=_cparams(),
    )(a, gmat, *after)


def _tril():
    r = lax.broadcasted_iota(jnp.int32, (HD, HD), 0)
    c = lax.broadcasted_iota(jnp.int32, (HD, HD), 1)
    return (c <= r).astype(F32)


def _gmlp_fwd(z, gv, ws, bst, tb):
    s = z.shape[0]
    tb = min(tb, s)

    def body(zu_ref, zv_ref, gv_ref, ws_ref, bst_ref, y_ref):
        tril = _tril()
        for h in range(NH):
            cols = slice(h * HD, (h + 1) * HD)
            u, vln = _gmlp_rows(zu_ref[:, cols], zv_ref[:, cols], gv_ref[h:h + 1, :])
            wm = (ws_ref[h] * tril).astype(BF16)
            vb = vln.astype(BF16)
            for c in range(tb // HD):
                rws = slice(c * HD, (c + 1) * HD)
                mixed = _dot(wm, vb[rws], NN) + bst_ref[:, h:h + 1]
                y_ref[rws, cols] = (u[rws] * mixed).astype(BF16)

    return pl.pallas_call(
        body, name="gmlp_fwd", grid=(s // tb,),
        in_specs=[pl.BlockSpec((tb, GW), lambda i: (i, 0)), pl.BlockSpec((tb, GW), lambda i: (i, 1)),
                  pl.BlockSpec((NH, HD), lambda i: (0, 0)), pl.BlockSpec((NH, HD, HD), lambda i: (0, 0, 0)),
                  pl.BlockSpec((HD, NH), lambda i: (0, 0))],
        out_specs=pl.BlockSpec((tb, GW), lambda i: (i, 0)),
        out_shape=jax.ShapeDtypeStruct((s, GW), BF16), compiler_params=_cparams(),
    )(z, z, gv, ws, bst)


def _gmlp_bwd(z, dy, gv, ws, bst, tb):
    s = z.shape[0]
    tb = min(tb, s)
    nb = s // tb

    def body(zu_ref, zv_ref, dy_ref, gv_ref, ws_ref, bst_ref, dzu_ref, dzv_ref, dgv_ref, dws_ref, db_ref):
        tril = _tril()
        for h in range(NH):
            cols = slice(h * HD, (h + 1) * HD)
            (u, vln), vjp = jax.vjp(_gmlp_rows, zu_ref[:, cols], zv_ref[:, cols], gv_ref[h:h + 1, :])
            wmf = ws_ref[h] * tril
            wm = wmf.astype(BF16)
            wmt = wmf.T.astype(BF16)
            vb = vln.astype(BF16)
            dws = jnp.zeros((HD, HD), F32)
            db = jnp.zeros((HD, 1), F32)
            du_parts, dvln_parts = [], []
            for c in range(tb // HD):
                rws = slice(c * HD, (c + 1) * HD)
                mixed = _dot(wm, vb[rws], NN) + bst_ref[:, h:h + 1]
                dyc = dy_ref[rws, cols]
                du_parts.append(dyc * mixed)
                dmixed = dyc * u[rws]
                dmb = dmixed.astype(BF16)
                dws = dws + _dot(dmb, vb[rws], NT)
                db = db + jnp.sum(dmixed, axis=1, keepdims=True)
                dvln_parts.append(_dot(wmt, dmb, NN))
            du = jnp.concatenate(du_parts, axis=0)
            dvln = jnp.concatenate(dvln_parts, axis=0)
            dzu, dzv, dgv = vjp((du, dvln))
            dzu_ref[:, cols] = dzu.astype(BF16)
            dzv_ref[:, cols] = dzv.astype(BF16)
            dgv_ref[0, h:h + 1, :] = dgv
            dws_ref[0, h] = dws * tril
            db_ref[0, h] = jnp.broadcast_to(db, (HD, LANES))

    blk = pl.BlockSpec((tb, GW), lambda i: (i, 0))
    return pl.pallas_call(
        body, name="gmlp_bwd", grid=(nb,),
        in_specs=[blk, pl.BlockSpec((tb, GW), lambda i: (i, 1)), blk,
                  pl.BlockSpec((NH, HD), lambda i: (0, 0)), pl.BlockSpec((NH, HD, HD), lambda i: (0, 0, 0)),
                  pl.BlockSpec((HD, NH), lambda i: (0, 0))],
        out_specs=[blk, blk, pl.BlockSpec((1, NH, HD), lambda i: (i, 0, 0)),
                   pl.BlockSpec((1, NH, HD, HD), lambda i: (i, 0, 0, 0)),
                   pl.BlockSpec((1, NH, HD, LANES), lambda i: (i, 0, 0, 0))],
        out_shape=[jax.ShapeDtypeStruct((s, GW), BF16), jax.ShapeDtypeStruct((s, GW), BF16),
                   jax.ShapeDtypeStruct((nb, NH, HD), F32), jax.ShapeDtypeStruct((nb, NH, HD, HD), F32),
                   jax.ShapeDtypeStruct((nb, NH, HD, LANES), F32)],
        compiler_params=_cparams(),
    )(z, z, dy, gv, ws, bst)


def _pool_count(t0, window):
    pos = (t0 + lax.broadcasted_iota(jnp.int32, (ROW_TILE, LANES), 0)).astype(F32)
    return jnp.minimum(pos + 1.0, float(window))


def _window_sum(win, levels, back):
    n = win.shape[0]
    for lv in range(levels):
        step = 1 << lv
        win = win + pltpu.roll(win, n - step if back else step, 0)
    return win


def _pool_pooled(ppad_ref, t0, g):
    win = ppad_ref[pl.ds(t0, ROW_TILE + POOL_HALO), :]
    wsum = _window_sum(win, g + 1, False)[POOL_HALO:]
    return wsum / _pool_count(t0, POOL_WINDOWS[g]) - win[POOL_HALO:]


def _pool_fwd(z, wp, sp):
    s = z.shape[0]
    nt = s // ROW_TILE

    def body(p_ref, wp_ref, sp_ref, y_ref, ppad):
        for g in range(NG):
            cols = slice(g * LANES, (g + 1) * LANES)
            ppad[pl.ds(0, POOL_HALO), :] = jnp.zeros((POOL_HALO, LANES), F32)
            ppad[pl.ds(POOL_HALO, s), :] = p_ref[:, cols]
            wpb = wp_ref[g].astype(BF16)
            scale = sp_ref[:, cols]

            def tile(t, carry):
                t0 = pl.multiple_of(t * ROW_TILE, ROW_TILE)
                pooled = _pool_pooled(ppad, t0, g)
                y_ref[pl.ds(t0, ROW_TILE), cols] = (_dot(pooled.astype(BF16), wpb, NN) * scale).astype(BF16)
                return carry

            lax.fori_loop(0, nt, tile, 0)

    return pl.pallas_call(
        body, name="pool_fwd", grid=(1,),
        in_specs=[pl.BlockSpec((s, PW), lambda i: (0, 2 * GW // PW)),
                  pl.BlockSpec((NG, LANES, LANES), lambda i: (0, 0, 0)), pl.BlockSpec((1, PW), lambda i: (0, 0))],
        out_specs=pl.BlockSpec((s, PW), lambda i: (0, 0)),
        out_shape=jax.ShapeDtypeStruct((s, PW), BF16),
        scratch_shapes=[pltpu.VMEM((s + POOL_HALO, LANES), F32)], compiler_params=_cparams(),
    )(z, wp, sp)


def _pool_bwd(z, dy, wp, sp):
    s = z.shape[0]
    nt = s // ROW_TILE

    def body(p_ref, dy_ref, wp_ref, sp_ref, dp_ref, dwp_ref, dsp_ref, ppad, rpad, dpool):
        for g in range(NG):
            cols = slice(g * LANES, (g + 1) * LANES)
            ppad[pl.ds(0, POOL_HALO), :] = jnp.zeros((POOL_HALO, LANES), F32)
            ppad[pl.ds(POOL_HALO, s), :] = p_ref[:, cols]
            rpad[pl.ds(s, POOL_HALO), :] = jnp.zeros((POOL_HALO, LANES), F32)
            wpb = wp_ref[g].astype(BF16)
            scale = sp_ref[:, cols]

            def tile(t, carry):
                dwp, dsp = carry
                t0 = pl.multiple_of(t * ROW_TILE, ROW_TILE)
                pooled = _pool_pooled(ppad, t0, g)
                pb = pooled.astype(BF16)
                dyt = dy_ref[pl.ds(t0, ROW_TILE), cols]
                dsp = dsp + jnp.sum(dyt * _dot(pb, wpb, NN), axis=0, keepdims=True)
                dmm = (dyt * scale).astype(BF16)
                dwp = dwp + _dot(pb, dmm, TN)
                dpooled = _dot(dmm, wpb, NT)
                rpad[pl.ds(t0, ROW_TILE), :] = dpooled / _pool_count(t0, POOL_WINDOWS[g])
                dpool[pl.ds(t0, ROW_TILE), :] = dpooled
                return dwp, dsp

            dwp, dsp = lax.fori_loop(0, nt, tile, (jnp.zeros((LANES, LANES), F32), jnp.zeros((1, LANES), F32)))
            dwp_ref[g] = dwp
            dsp_ref[:, cols] = dsp

            def tile2(t, carry):
                t0 = pl.multiple_of(t * ROW_TILE, ROW_TILE)
                win = rpad[pl.ds(t0, ROW_TILE + POOL_HALO), :]
                back = _window_sum(win, g + 1, True)[:ROW_TILE]
                rows = pl.ds(t0, ROW_TILE)
                dp_ref[rows, cols] = (back - dpool[rows, :]).astype(BF16)
                return carry

            lax.fori_loop(0, nt, tile2, 0)

    return pl.pallas_call(
        body, name="pool_bwd", grid=(1,),
        in_specs=[pl.BlockSpec((s, PW), lambda i: (0, 2 * GW // PW)), pl.BlockSpec((s, PW), lambda i: (0, GW // PW)),
                  pl.BlockSpec((NG, LANES, LANES), lambda i: (0, 0, 0)), pl.BlockSpec((1, PW), lambda i: (0, 0))],
        out_specs=[pl.BlockSpec((s, PW), lambda i: (0, 0)), pl.BlockSpec((NG, LANES, LANES), lambda i: (0, 0, 0)),
                   pl.BlockSpec((1, PW), lambda i: (0, 0))],
        out_shape=[jax.ShapeDtypeStruct((s, PW), BF16), jax.ShapeDtypeStruct((NG, LANES, LANES), F32),
                   jax.ShapeDtypeStruct((1, PW), F32)],
        scratch_shapes=[pltpu.VMEM((s + POOL_HALO, LANES), F32), pltpu.VMEM((s + POOL_HALO, LANES), F32),
                        pltpu.VMEM((s, LANES), F32)],
        compiler_params=_cparams(),
    )(z, dy, wp, sp)


CONV_LEAD = CONV_HALO - (CONV_K - 1)


def _conv_taps(win, wdw_ref, lead, reverse):
    n = win.shape[0]
    acc = jnp.zeros((ROW_TILE, CW), F32)
    for j in range(CONV_K):
        tap = (CONV_K - 1 - j) if reverse else j
        acc = acc + wdw_ref[tap:tap + 1, :] * pltpu.roll(win, (n - (lead + j)) % n, 0)[:ROW_TILE]
    return acc


def _conv_fill_glu(cv_ref, cg_ref, xpad, s):
    xpad[pl.ds(0, CONV_HALO), :] = jnp.zeros((CONV_HALO, CW), F32)

    def fill(t, carry):
        t0 = pl.multiple_of(t * ROW_TILE, ROW_TILE)
        rows = pl.ds(t0, ROW_TILE)
        xpad[pl.ds(t0 + CONV_HALO, ROW_TILE), :] = _glu(cv_ref[rows, :], cg_ref[rows, :])
        return carry

    lax.fori_loop(0, s // ROW_TILE, fill, 0)


def _conv_fwd(z, wdw, bdw, lng, lnb):
    s = z.shape[0]

    def body(cv_ref, cg_ref, wdw_ref, bdw_ref, lng_ref, lnb_ref, y_ref, xpad):
        _conv_fill_glu(cv_ref, cg_ref, xpad, s)

        def tile(t, carry):
            t0 = pl.multiple_of(t * ROW_TILE, ROW_TILE)
            win = xpad[pl.ds(t0, ROW_TILE + CONV_HALO), :]
            hc = _conv_taps(win, wdw_ref, CONV_LEAD, False) + bdw_ref[...]
            y_ref[pl.ds(t0, ROW_TILE), :] = _ln_silu(hc, lng_ref[...], lnb_ref[...]).astype(BF16)
            return carry

        lax.fori_loop(0, s // ROW_TILE, tile, 0)

    vec = pl.BlockSpec((1, CW), lambda i: (0, 0))
    return pl.pallas_call(
        body, name="conv_fwd", grid=(1,),
        in_specs=[pl.BlockSpec((s, CW), lambda i: (0, (2 * GW + PW) // CW)),
                  pl.BlockSpec((s, CW), lambda i: (0, (2 * GW + PW) // CW + 1)),
                  pl.BlockSpec((CONV_K + 1, CW), lambda i: (0, 0)), vec, vec, vec],
        out_specs=pl.BlockSpec((s, CW), lambda i: (0, 0)),
        out_shape=jax.ShapeDtypeStruct((s, CW), BF16),
        scratch_shapes=[pltpu.VMEM((s + CONV_HALO, CW), F32)], compiler_params=_cparams(),
    )(z, z, wdw, bdw, lng, lnb)


def _conv_bwd(z, dy, wdw, bdw, lng, lnb):
    s = z.shape[0]

    def body(cv_ref, cg_ref, dy_ref, wdw_ref, bdw_ref, lng_ref, lnb_ref,
             dcv_ref, dcg_ref, dwdw_ref, dbdw_ref, dlng_ref, dlnb_ref, xpad, dpad):
        _conv_fill_glu(cv_ref, cg_ref, xpad, s)
        dpad[pl.ds(s, CONV_HALO), :] = jnp.zeros((CONV_HALO, CW), F32)
        dwdw_ref[...] = jnp.zeros((CONV_K + 1, CW), F32)

        def tile(t, carry):
            db, dg, dbeta = carry
            t0 = pl.multiple_of(t * ROW_TILE, ROW_TILE)
            win = xpad[pl.ds(t0, ROW_TILE + CONV_HALO), :]
            hc = _conv_taps(win, wdw_ref, CONV_LEAD, False) + bdw_ref[...]
            _, vjp = jax.vjp(_ln_silu, hc, lng_ref[...], lnb_ref[...])
            dhc, dg_t, dbeta_t = vjp(dy_ref[pl.ds(t0, ROW_TILE), :])
            dpad[pl.ds(t0, ROW_TILE), :] = dhc
            n = win.shape[0]
            for j in range(CONV_K):
                shifted = pltpu.roll(win, (n - (CONV_LEAD + j)) % n, 0)[:ROW_TILE]
                dwdw_ref[j:j + 1, :] += jnp.sum(dhc * shifted, axis=0, keepdims=True)
            return db + jnp.sum(dhc, axis=0, keepdims=True), dg + dg_t, dbeta + dbeta_t

        zero = jnp.zeros((1, CW), F32)
        db, dg, dbeta = lax.fori_loop(0, s // ROW_TILE, tile, (zero, zero, zero))
        dbdw_ref[...] = db
        dlng_ref[...] = dg
        dlnb_ref[...] = dbeta

        def tile2(t, carry):
            t0 = pl.multiple_of(t * ROW_TILE, ROW_TILE)
            rows = pl.ds(t0, ROW_TILE)
            win = dpad[pl.ds(t0, ROW_TILE + CONV_HALO), :]
            dglu = _conv_taps(win, wdw_ref, 0, True)
            _, vjp = jax.vjp(_glu, cv_ref[rows, :], cg_ref[rows, :])
            dcv, dcg = vjp(dglu)
            dcv_ref[rows, :] = dcv.astype(BF16)
            dcg_ref[rows, :] = dcg.astype(BF16)
            return carry

        lax.fori_loop(0, s // ROW_TILE, tile2, 0)

    vec = pl.BlockSpec((1, CW), lambda i: (0, 0))
    full = pl.BlockSpec((s, CW), lambda i: (0, 0))
    wspec = pl.BlockSpec((CONV_K + 1, CW), lambda i: (0, 0))
    vshape = jax.ShapeDtypeStruct((1, CW), F32)
    return pl.pallas_call(
        body, name="conv_bwd", grid=(1,),
        in_specs=[pl.BlockSpec((s, CW), lambda i: (0, (2 * GW + PW) // CW)),
                  pl.BlockSpec((s, CW), lambda i: (0, (2 * GW + PW) // CW + 1)),
                  pl.BlockSpec((s, CW), lambda i: (0, (GW + PW) // CW)), wspec, vec, vec, vec],
        out_specs=[full, full, wspec, vec, vec, vec],
        out_shape=[jax.ShapeDtypeStruct((s, CW), BF16), jax.ShapeDtypeStruct((s, CW), BF16),
                   jax.ShapeDtypeStruct((CONV_K + 1, CW), F32), vshape, vshape, vshape],
        scratch_shapes=[pltpu.VMEM((s + CONV_HALO, CW), F32), pltpu.VMEM((s + CONV_HALO, CW), F32)],
        compiler_params=_cparams(),
    )(z, z, dy, wdw, bdw, lng, lnb)


def _softmax_rows(sc):
    e = jnp.exp(sc - jnp.max(sc, axis=-1, keepdims=True))
    return e / jnp.sum(e, axis=-1, keepdims=True)


def _attn_fwd(q, k, v, tq):
    s, m = q.shape[0], k.shape[0]
    tq = min(tq, s)

    def body(q_ref, k_ref, v_ref, o_ref):
        for h in range(XH):
            cols = slice(h * XHD, (h + 1) * XHD)
            p = _softmax_rows(_dot(q_ref[:, cols], k_ref[:, cols], NT) * ATT_SCALE)
            o_ref[:, cols] = _dot(p.astype(BF16), v_ref[:, cols], NN).astype(BF16)

    kv = pl.BlockSpec((m, D), lambda i: (0, 0))
    return pl.pallas_call(
        body, name="attn_fwd", grid=(s // tq,),
        in_specs=[pl.BlockSpec((tq, D), lambda i: (i, 0)), kv, kv],
        out_specs=pl.BlockSpec((tq, D), lambda i: (i, 0)),
        out_shape=jax.ShapeDtypeStruct((s, D), BF16), compiler_params=_cparams(),
    )(q, k, v)


def _attn_bwd(q, k, v, do, tq):
    s, m = q.shape[0], k.shape[0]
    tq = min(tq, s)

    def body(q_ref, k_ref, v_ref, do_ref, dq_ref, dk_ref, dv_ref):
        @pl.when(pl.program_id(0) == 0)
        def _():
            dk_ref[...] = jnp.zeros_like(dk_ref)
            dv_ref[...] = jnp.zeros_like(dv_ref)

        for h in range(XH):
            cols = slice(h * XHD, (h + 1) * XHD)
            qh, kh, vh, doh = q_ref[:, cols], k_ref[:, cols], v_ref[:, cols], do_ref[:, cols]
            p = _softmax_rows(_dot(qh, kh, NT) * ATT_SCALE)
            dp = _dot(doh, vh, NT)
            dv_ref[:, cols] += _dot(p.astype(BF16), doh, TN)
            ds = (p * (dp - jnp.sum(p * dp, axis=-1, keepdims=True)) * ATT_SCALE).astype(BF16)
            dq_ref[:, cols] = _dot(ds, kh, NN).astype(BF16)
            dk_ref[:, cols] += _dot(ds, qh, TN)

    kv = pl.BlockSpec((m, D), lambda i: (0, 0))
    qs = pl.BlockSpec((tq, D), lambda i: (i, 0))
    return pl.pallas_call(
        body, name="attn_bwd", grid=(s // tq,),
        in_specs=[qs, kv, kv, qs], out_specs=[qs, kv, kv],
        out_shape=[jax.ShapeDtypeStruct((s, D), BF16), jax.ShapeDtypeStruct((m, D), F32),
                   jax.ShapeDtypeStruct((m, D), F32)],
        compiler_params=_cparams(),
    )(q, k, v, do)


def _loss_head(y, target, tm):
    s = y.shape[0]
    tm = min(tm, s)

    def body(y_ref, t_ref, dy_ref, part_ref):
        err = y_ref[...] - t_ref[...]
        dy_ref[...] = err * (1.0 / D)
        part_ref[...] = jnp.full((1, 8, LANES), 0.5 * jnp.sum(err * err) * (1.0 / D), F32)

    blk = pl.BlockSpec((tm, D), lambda i: (i, 0))
    dy, part = pl.pallas_call(
        body, name="loss_head", grid=(s // tm,), in_specs=[blk, blk],
        out_specs=[blk, pl.BlockSpec((1, 8, LANES), lambda i: (i, 0, 0))],
        out_shape=[jax.ShapeDtypeStruct((s, D), F32), jax.ShapeDtypeStruct((s // tm, 8, LANES), F32)],
        compiler_params=_cparams(),
    )(y, target)
    return dy, jnp.sum(part[:, 0, 0])


def _layer_fwd(x0, mem, w, p, fetch):
    z, hn0 = _rowop_mm("mix_in", "rms", (x0,), p["norm_mix_pre"], w["w_in"], NT, F32)
    ya = _gmlp_fwd(z, p["gmlp_v_gain"], p["w_spatial"], p["b_spatial_t"], 512)
    yb = _pool_fwd(z, p["w_pool"], p["s_pool"])
    yc = _conv_fwd(z, p["w_dw"], p["b_dw"], p["conv_ln_g"], p["conv_ln_b"])
    y = jnp.concatenate([ya, yb, yc], axis=1)
    w.update(fetch("att", (y,)))
    x1, h0 = _mm_rowop("mix_out", "rms_res", [(y, w["w_out"], NN)], (x0,), p["norm_mix_post"])
    q, hn1 = _rowop_mm("att_q", "rms", (x1,), p["norm_xattn_pre"], w["w_q"], NN, BF16)
    k, mn = _rowop_mm("att_k", "rms", (mem,), p["norm_mem"], w["w_k"], NN, BF16)
    v, _ = _rowop_mm("att_v", "rms", (mem,), p["norm_mem"], w["w_v"], NN, BF16)
    o = _attn_fwd(q, k, v, 256)
    x2, h1 = _mm_rowop("att_o", "rms_res", [(o, w["w_o"], NN)], (x1,), p["norm_xattn_post"])
    w.update(fetch("ffn", (x2,)))
    u, hn2 = _rowop_mm("ffn_up", "rms", (x2,), p["norm_ffn_pre"], w["w_up"], NT, F32)
    x3, h2 = _mm_rowop("ffn_down", "rms_res", [(u, w["w_down"], NN)], (x2,), p["norm_ffn_post"], relu2=True)
    saved = dict(x0=x0, z=z, hn0=hn0, y=y, h0=h0, x1=x1, q=q, hn1=hn1, k=k, v=v, mn=mn, o=o, h1=h1, x2=x2, u=u,
                 hn2=hn2, h2=h2)
    return x3, saved


def _layer_bwd(dx3, mem, w, p, sv, red):
    gs = {}
    du, dh2, dg = _rowop_mm("ffn_down_bwd", "rms_bwd", (sv["h2"], dx3), p["norm_ffn_post"], w["w_down"], NT, BF16,
                            u=sv["u"], after=red.after())
    gs["norm_ffn_post"] = jnp.sum(dg, axis=0)
    g_down = _mm_tn("ffn_down_dw", sv["u"], dh2, relu2=True)
    red.advance((g_down,))
    dx2, dg = _mm_rowop("ffn_up_bwd", "rms_bwd_res", [(du, w["w_up"], NN)], (sv["x2"], dx3), p["norm_ffn_pre"],
                        after=red.after())
    gs["norm_ffn_pre"] = jnp.sum(dg, axis=0)
    g_up = _mm_tn("ffn_up_dw", du, sv["hn2"])
    red.add("ffn", ("w_down", "w_up"), [g_down, g_up])
    do, dh1, dg = _rowop_mm("att_o_bwd", "rms_bwd", (sv["h1"], dx2), p["norm_xattn_post"], w["w_o"], NT, BF16,
                            after=red.after())
    gs["norm_xattn_post"] = jnp.sum(dg, axis=0)
    g_o = _mm_tn("att_o_dw", sv["o"], dh1)
    red.advance((g_o,))
    dq, dk, dv = _attn_bwd(sv["q"], sv["k"], sv["v"], do, 256)
    dk, dv = dk.astype(BF16), dv.astype(BF16)
    dx1, dg = _mm_rowop("att_q_bwd", "rms_bwd_res", [(dq, w["w_q"], NT)], (sv["x1"], dx2), p["norm_xattn_pre"],
                        after=red.after())
    gs["norm_xattn_pre"] = jnp.sum(dg, axis=0)
    g_q = _mm_tn("att_q_dw", sv["hn1"], dq)
    g_k = _mm_tn("att_k_dw", sv["mn"], dk)
    g_v = _mm_tn("att_v_dw", sv["mn"], dv)
    (dg,) = _mm_rowop("att_kv_bwd", "rms_bwd_gain", [(dk, w["w_k"], NT), (dv, w["w_v"], NT)], (mem,), p["norm_mem"])
    gs["norm_mem"] = jnp.sum(dg, axis=0)
    red.add("att", ("w_o", "w_q", "w_k", "w_v"), [g_o, g_q, g_k, g_v])
    dy, dh0, dg = _rowop_mm("mix_out_bwd", "rms_bwd", (sv["h0"], dx1), p["norm_mix_post"], w["w_out"], NT, F32,
                            after=red.after())
    gs["norm_mix_post"] = jnp.sum(dg, axis=0)
    g_out = _mm_tn("mix_out_dw", sv["y"], dh0)
    red.advance((g_out,))
    red.add("out", ("w_out",), [g_out])
    z = sv["z"]
    dzu, dzv, dgv, dws, dbs = _gmlp_bwd(z, dy, p["gmlp_v_gain"], p["w_spatial"], p["b_spatial_t"], 512)
    gs["gmlp_v_gain"] = jnp.sum(dgv, axis=0)
    gs["w_spatial"] = jnp.sum(dws, axis=0)
    gs["b_spatial"] = jnp.sum(dbs[..., 0], axis=0)
    dp, gs["w_pool"], gs["s_pool"] = _pool_bwd(z, dy, p["w_pool"], p["s_pool"])
    dcv, dcg, dwdw, gs["b_dw"], gs["conv_ln_g"], gs["conv_ln_b"] = _conv_bwd(
        z, dy, p["w_dw"], p["b_dw"], p["conv_ln_g"], p["conv_ln_b"])
    dz = jnp.concatenate([dzu, dzv, dp, dcv, dcg], axis=1)
    red.advance((dz,))
    g_in = _mm_tn("mix_in_dw", dz, sv["hn0"], after=red.after())
    red.add("in", ("w_in",), [g_in])
    red.advance((g_in,))
    dx0, dg = _mm_rowop("mix_in_bwd", "rms_bwd_res", [(dz, w["w_in"], NN)], (sv["x0"], dx1), p["norm_mix_pre"],
                        after=red.after())
    gs["norm_mix_pre"] = jnp.sum(dg, axis=0)
    return dx0, _small_grad_arrays(gs, dwdw)


NORM_NAMES = ("norm_mix_pre", "norm_mix_post", "norm_xattn_pre", "norm_mem", "norm_xattn_post", "norm_ffn_pre",
              "norm_ffn_post")
VEC_NAMES = ("s_pool", "b_dw", "conv_ln_g", "conv_ln_b")
SMALL_ARRAYS = ("norms", "gain_bias", "w_spatial", "w_pool", "vecs", "w_dw")


def _small_grad_arrays(gs, dwdw):
    return {"norms": jnp.concatenate([gs[n] for n in NORM_NAMES], axis=0),
            "gain_bias": jnp.concatenate([gs["gmlp_v_gain"], gs["b_spatial"]], axis=0),
            "w_spatial": gs["w_spatial"], "w_pool": gs["w_pool"],
            "vecs": jnp.concatenate([gs[n] for n in VEC_NAMES], axis=0), "w_dw": dwdw}


def _split_small_grads(arrays):
    out = {n: arrays["norms"][k] for k, n in enumerate(NORM_NAMES)}
    out.update({n: arrays["vecs"][k] for k, n in enumerate(VEC_NAMES)})
    out.update(gmlp_v_gain=arrays["gain_bias"][:NH], b_spatial=arrays["gain_bias"][NH:], w_spatial=arrays["w_spatial"],
               w_pool=arrays["w_pool"], w_dw=arrays["w_dw"][:CONV_K])
    return out


def _layer_params(small, l):
    p = {n: small[n][l].reshape(1, -1) for n in ("norm_mix_pre", "norm_mix_post", "s_pool", "b_dw", "conv_ln_g",
                                                   "conv_ln_b", "norm_xattn_pre", "norm_mem", "norm_xattn_post",
                                                   "norm_ffn_pre", "norm_ffn_post")}
    p["gmlp_v_gain"] = small["gmlp_v_gain"][l]
    p["w_spatial"] = small["w_spatial"][l]
    p["b_spatial_t"] = small["b_spatial"][l].T
    p["w_pool"] = small["w_pool"][l]
    p["w_dw"] = jnp.pad(small["w_dw"][l], ((0, 1), (0, 0)))
    return p


def _local_step(x, mem, target, fetch, small, red):
    small = dict(small)
    saved, weights, params = [], [], []
    h = x
    marker = ()
    for l in range(DEPTH):
        w = fetch(l, "in", marker)
        if "taps" in w:
            small["w_dw"] = w.pop("taps")
        p = _layer_params(small, l)
        h, sv = _layer_fwd(h, mem, w, p, functools.partial(fetch, l))
        marker = (h,)
        saved.append(sv)
        weights.append(w)
        params.append(p)
    dh, loss = _loss_head(h, target, 512)
    gsmall = [None] * DEPTH
    for l in reversed(range(DEPTH)):
        red.layer = l
        dh, gsmall[l] = _layer_bwd(dh, mem, weights[l], params[l], saved[l], red)
    return loss, dh, gsmall


HBM = pl.BlockSpec(memory_space=pltpu.HBM)


def _position():
    return lax.axis_index("x"), lax.axis_index("y"), lax.axis_index("c")


def _all_gather(name, shards):
    n = len(shards)

    def body(*refs):
        ins, outs = refs[:n], refs[n:2 * n]
        send_sems, recv_sems, local_sems = refs[2 * n:]
        x, y, c = _position()
        me, sibling = (x, y, c), (x, y, 1 - c)
        chips = [(1 - x, y), (x, 1 - y), (1 - x, 1 - y)]

        def rows(a, dev):
            return outs[a].at[4 * dev[0] + 2 * dev[1] + dev[2]]

        def copy(a, k, block, to, src=None):
            return pltpu.make_async_remote_copy(
                src_ref=rows(a, block) if src is None else src, dst_ref=rows(a, block),
                send_sem=send_sems.at[a, k], recv_sem=recv_sems.at[a, k], device_id=to, device_id_type=MESH)

        started = []
        for a in range(n):
            mine = pltpu.make_async_copy(ins[a], rows(a, me), local_sems.at[a])
            mine.start()
            started.append(mine)
        first = []
        for a in range(n):
            for j, chip in enumerate(chips):
                first.append(copy(a, 1 + j, me, (*chip, c), src=ins[a]))
        for a in range(n):
            first.append(copy(a, 0, me, sibling, src=ins[a]))
        for cp in first:
            cp.start()
        passed = []
        for a in range(n):
            for j, chip in enumerate(chips):
                copy(a, 1 + j, (*chip, c), me).wait_recv()
                fwd = copy(a, 4 + j, (*chip, c), sibling)
                fwd.start()
                passed.append(fwd)
        for a in range(n):
            copy(a, 0, sibling, me).wait_recv()
            for j, chip in enumerate(chips):
                copy(a, 4 + j, (*chip, 1 - c), me).wait_recv()
        for cp in first + passed:
            cp.wait_send()
        for mine in started:
            mine.wait()

    return pl.pallas_call(
        body, name=name, in_specs=[HBM] * n, out_specs=[HBM] * n,
        out_shape=[jax.ShapeDtypeStruct((N_DEV,) + s.shape, s.dtype) for s in shards],
        scratch_shapes=[pltpu.SemaphoreType.DMA((n, 7)), pltpu.SemaphoreType.DMA((n, 7)), pltpu.SemaphoreType.DMA((n,))],
    )(*shards)


SEM = pl.BlockSpec(memory_space=pltpu.SEMAPHORE)
EFFECT = pltpu.SideEffectType.DATAFLOW_SIDE_EFFECTING
TOKEN = jax.ShapeDtypeStruct((8, LANES), F32)
TOKEN_SPEC = pl.BlockSpec(memory_space=pltpu.VMEM)


def _landing(shape, dtype):
    return pltpu.with_memory_space_constraint(lax.empty(shape, dtype), pltpu.HBM)


def _hbm_shapes(arrays):
    return [pltpu.HBM(a.shape, a.dtype) for a in arrays]


def _block(ref, r, dev):
    return ref.at[pl.ds((4 * dev[0] + 2 * dev[1] + dev[2]) * r, r), :]


def _split_call(name, body, thru, sems_in, after, sems_out, token):
    n = len(thru)
    out_shape = [pltpu.SemaphoreType.DMA(s) for s in sems_out] + _hbm_shapes(thru) + ([TOKEN] if token else [])
    out_specs = [SEM] * len(sems_out) + [HBM] * n + ([TOKEN_SPEC] if token else [])
    return pl.pallas_call(
        body, name=name, in_specs=[HBM] * n + [SEM] * len(sems_in) + [ANY] * len(after),
        out_specs=out_specs, out_shape=out_shape,
        input_output_aliases={i: len(sems_out) + i for i in range(n)},
        compiler_params=pltpu.CompilerParams(has_side_effects=EFFECT),
    )(*thru, *sems_in, *after)


def _place_own(name, srcs, dev, out_dtype, tr):
    n = len(srcs)
    r, cols = srcs[0][0].shape[-2:]
    tr = r if r < 16 else _row_tile(r, tr)
    nb = r // tr

    def body(dev_ref, *refs):
        for a in range(n):
            refs[n + a][...] = refs[a][...].astype(out_dtype)

    in_specs = [pl.BlockSpec((tr, cols), lambda i, d: (i, 0)) if l is None
                else pl.BlockSpec((None, tr, cols), lambda i, d, l=l: (l, i, 0)) for _, l in srcs]
    return pl.pallas_call(
        body, name=name,
        grid_spec=pltpu.PrefetchScalarGridSpec(
            num_scalar_prefetch=1, grid=(nb,), in_specs=in_specs,
            out_specs=[pl.BlockSpec((tr, cols), lambda i, d: (d[0] * nb + i, 0))] * n),
        out_shape=[jax.ShapeDtypeStruct((N_DEV * r, cols), out_dtype)] * n, compiler_params=_cparams(),
    )(dev, *[a for a, _ in srcs])


def _gather_peers(x, y, c):
    return [(1 - x, y, c), (x, 1 - y, c), (1 - x, 1 - y, c), (x, y, 1 - c)]


def _block_rows(land):
    return land.shape[0] // N_DEV


def _gather_start(name, lands, after):
    n = len(lands)

    def body(*refs):
        lz = refs[:n]
        send_sems, recv_sems = refs[n + len(after)], refs[n + len(after) + 1]
        token = refs[-1]
        x, y, c = _position()
        for a in range(n):
            own = _block(lz[a], _block_rows(lands[a]), (x, y, c))
            for k, to in enumerate(_gather_peers(x, y, c)):
                pltpu.make_async_remote_copy(src_ref=own, dst_ref=own, send_sem=send_sems.at[k], recv_sem=recv_sems.at[k],
                                             device_id=to, device_id_type=MESH).start()
        token[...] = jnp.zeros_like(token)

    out = _split_call(name, body, list(lands), [], after, [(4,), (4,)], True)
    return out[0], out[1], out[2:2 + n], out[-1]


def _gather_forward(name, lands, recv_sems, after):
    n = len(lands)

    def body(*refs):
        lz = refs[:n]
        recv0 = refs[n]
        fsend, frecv = refs[n + 1 + len(after)], refs[n + 2 + len(after)]
        token = refs[-1]
        x, y, c = _position()
        chips = _gather_peers(x, y, c)[:3]
        for a in range(n):
            for j, chip in enumerate(chips):
                blk = _block(lz[a], _block_rows(lands[a]), chip)
                pltpu.make_async_remote_copy(src_ref=blk, dst_ref=blk, send_sem=fsend.at[j], recv_sem=recv0.at[j],
                                             device_id=(x, y, c), device_id_type=MESH).wait_recv()
        for a in range(n):
            for j, chip in enumerate(chips):
                blk = _block(lz[a], _block_rows(lands[a]), chip)
                pltpu.make_async_remote_copy(src_ref=blk, dst_ref=blk, send_sem=fsend.at[j], recv_sem=frecv.at[j],
                                             device_id=(x, y, 1 - c), device_id_type=MESH).start()
        token[...] = jnp.zeros_like(token)

    out = _split_call(name, body, list(lands), [recv_sems], after, [(3,), (3,)], True)
    return out[0], out[1], out[2:2 + n], out[-1]


def _gather_finish(name, lands, send_sems, recv_sems, fsend, frecv, after):
    n = len(lands)

    def body(*refs):
        lz = refs[:n]
        send0, recv0, fsend_ref, frecv_ref = refs[n:n + 4]
        x, y, c = _position()
        me = (x, y, c)
        chips = _gather_peers(x, y, c)[:3]
        for a in range(n):
            r = _block_rows(lands[a])
            sib = _block(lz[a], r, (x, y, 1 - c))
            pltpu.make_async_remote_copy(src_ref=sib, dst_ref=sib, send_sem=send0.at[3], recv_sem=recv0.at[3],
                                         device_id=me, device_id_type=MESH).wait_recv()
            for j, chip in enumerate(chips):
                blk = _block(lz[a], r, (chip[0], chip[1], 1 - c))
                pltpu.make_async_remote_copy(src_ref=blk, dst_ref=blk, send_sem=fsend_ref.at[j], recv_sem=frecv_ref.at[j],
                                             device_id=me, device_id_type=MESH).wait_recv()
            own = _block(lz[a], r, me)
            for k in range(4):
                pltpu.make_async_remote_copy(src_ref=own, dst_ref=own, send_sem=send0.at[k], recv_sem=recv0.at[k],
                                             device_id=me, device_id_type=MESH).wait_send()
            for j, chip in enumerate(chips):
                blk = _block(lz[a], r, chip)
                pltpu.make_async_remote_copy(src_ref=blk, dst_ref=blk, send_sem=fsend_ref.at[j], recv_sem=frecv_ref.at[j],
                                             device_id=me, device_id_type=MESH).wait_send()

    return _split_call(name, body, list(lands), [send_sems, recv_sems, fsend, frecv], after, [], False)


def _sibling_start(name, grads, after):
    n = len(grads)
    lands = [_landing((4, g.shape[0] // N_DEV, D), g.dtype) for g in grads]

    def body(*refs):
        ins, lz = refs[:n], refs[n:2 * n]
        send_sem, recv_sem = refs[2 * n + len(after)], refs[2 * n + len(after) + 1]
        token = refs[-1]
        x, y, c = _position()
        for a in range(n):
            r = grads[a].shape[0] // N_DEV
            for q in range(4):
                pltpu.make_async_remote_copy(
                    src_ref=ins[a].at[pl.ds((2 * q + 1 - c) * r, r), :], dst_ref=lz[a].at[q], send_sem=send_sem.at[0],
                    recv_sem=recv_sem.at[0], device_id=(x, y, 1 - c), device_id_type=MESH).start()
        token[...] = jnp.zeros_like(token)

    out = _split_call(name, body, list(grads) + lands, [], after, [(1,), (1,)], True)
    return out[0], out[1], out[2:2 + n], out[2 + n:2 + 2 * n], out[-1]


def _sibling_finish(name, grads, lands, send_sem, recv_sem, after):
    n = len(grads)

    def body(*refs):
        ins, lz = refs[:n], refs[n:2 * n]
        send_ref, recv_ref = refs[2 * n], refs[2 * n + 1]
        x, y, c = _position()
        for a in range(n):
            r = grads[a].shape[0] // N_DEV
            for q in range(4):
                cp = pltpu.make_async_remote_copy(
                    src_ref=ins[a].at[pl.ds((2 * q + 1 - c) * r, r), :], dst_ref=lz[a].at[q], send_sem=send_ref.at[0],
                    recv_sem=recv_ref.at[0], device_id=(x, y, c), device_id_type=MESH)
                cp.wait_send()
                cp.wait_recv()

    out = _split_call(name, body, list(grads) + list(lands), [send_sem, recv_sem], after, [], False)
    return out[:n], out[n:2 * n]


def _chip_start(name, parts, after):
    n = len(parts)
    lands = [_landing((3,) + p.shape[1:], p.dtype) for p in parts]

    def body(*refs):
        ins, lz = refs[:n], refs[n:2 * n]
        send_sems, recv_sems = refs[2 * n + len(after)], refs[2 * n + len(after) + 1]
        token = refs[-1]
        x, y, c = _position()
        for a in range(n):
            for j, chip in enumerate(_gather_peers(x, y, c)[:3]):
                pltpu.make_async_remote_copy(
                    src_ref=ins[a].at[2 * chip[0] + chip[1]], dst_ref=lz[a].at[j], send_sem=send_sems.at[j],
                    recv_sem=recv_sems.at[j], device_id=chip, device_id_type=MESH).start()
        token[...] = jnp.zeros_like(token)

    out = _split_call(name, body, list(parts) + lands, [], after, [(3,), (3,)], True)
    return out[0], out[1], out[2:2 + n], out[2 + n:2 + 2 * n], out[-1]


def _chip_finish(name, parts, lands, send_sems, recv_sems, after):
    n = len(parts)

    def body(*refs):
        ins, lz = refs[:n], refs[n:2 * n]
        send_ref, recv_ref = refs[2 * n], refs[2 * n + 1]
        me = _position()
        for a in range(n):
            for j in range(3):
                cp = pltpu.make_async_remote_copy(
                    src_ref=ins[a].at[j], dst_ref=lz[a].at[j], send_sem=send_ref.at[j], recv_sem=recv_ref.at[j],
                    device_id=me, device_id_type=MESH)
                cp.wait_send()
                cp.wait_recv()

    out = _split_call(name, body, list(parts) + list(lands), [send_sems, recv_sems], after, [], False)
    return out[:n], out[n:2 * n]


def _row_tile(r, target):
    return max(t for t in range(16, min(r, target) + 1, 16) if r % t == 0)


def _chip_partial(name, grad, got, c, tr):
    r = grad.shape[0] // N_DEV
    tr = _row_tile(r, tr)
    g4 = grad.reshape(4, 2, r, D)

    def body(c_ref, g_ref, s_ref, o_ref):
        o_ref[...] = (g_ref[...].astype(F32) + s_ref[...].astype(F32)).astype(BF16)

    return pl.pallas_call(
        body, name=name,
        grid_spec=pltpu.PrefetchScalarGridSpec(
            num_scalar_prefetch=1, grid=(4, r // tr),
            in_specs=[pl.BlockSpec((None, None, tr, D), lambda q, i, c_ref: (q, c_ref[0], i, 0)),
                      pl.BlockSpec((None, tr, D), lambda q, i, c_ref: (q, i, 0))],
            out_specs=pl.BlockSpec((None, tr, D), lambda q, i, c_ref: (q, i, 0))),
        out_shape=jax.ShapeDtypeStruct((4, r, D), BF16), compiler_params=_cparams(),
    )(c, g4, got)


def _chip_sum(name, part, got, chip, tr):
    r = part.shape[1]
    tr = _row_tile(r, tr)

    def body(q_ref, p_ref, g_ref, o_ref):
        acc = p_ref[...].astype(F32)
        for j in range(3):
            acc = acc + g_ref[j].astype(F32)
        o_ref[...] = acc

    return pl.pallas_call(
        body, name=name,
        grid_spec=pltpu.PrefetchScalarGridSpec(
            num_scalar_prefetch=1, grid=(r // tr,),
            in_specs=[pl.BlockSpec((None, tr, D), lambda i, q_ref: (q_ref[0], i, 0)),
                      pl.BlockSpec((3, tr, D), lambda i, q_ref: (0, i, 0))],
            out_specs=pl.BlockSpec((tr, D), lambda i, q_ref: (i, 0))),
        out_shape=jax.ShapeDtypeStruct((r, D), F32), compiler_params=_cparams(),
    )(chip, part, got)


class _WeightGather:
    def __init__(self, groups):
        self.state, token = {}, ()
        for key, names, lands in groups:
            send, recv, lz, tok = _gather_start("gather_start_%s_%d" % key[::-1], lands, token)
            self.state[key] = (names, send, recv, lz)
            token = (tok,)
        self.started = token

    def fetch(self, layer, group, marker):
        names, send, recv, lz = self.state.pop((layer, group))
        tag = "%s_%d" % (group, layer)
        fsend, frecv, lz, tok = _gather_forward("gather_forward_" + tag, lz, recv, marker or self.started)
        lz = _gather_finish("gather_finish_" + tag, lz, send, recv, fsend, frecv, (tok,))
        return dict(zip(names, lz))


class _GradReduce:
    def __init__(self, core, chip):
        self.core, self.chip = core, chip
        self.layer = None
        self.token = ()
        self.at_sibling, self.at_chips = [], []

    def after(self):
        return self.token

    def add(self, group, names, grads):
        tag = "%s_%d" % (group, self.layer)
        send, recv, grads, lands, tok = _sibling_start("grad_sibling_start_" + tag, grads, self.token)
        self.at_sibling.append((tag, [(self.layer, n) for n in names], send, recv, grads, lands))
        self.token = (tok,)

    def advance(self, marker):
        for tag, keys, send, recv, grads, lands in self.at_sibling:
            grads, lands = _sibling_finish("grad_sibling_finish_" + tag, grads, lands, send, recv, marker)
            parts = [_chip_partial("chip_partial_%d_%s" % key, g, got, self.core, 256)
                     for key, g, got in zip(keys, grads, lands)]
            send, recv, parts, lands, tok = _chip_start("grad_chip_start_" + tag, parts, ())
            self.at_chips.append((tag, keys, send, recv, parts, lands))
            self.token = (tok,)
        self.at_sibling = []

    def finish(self, marker):
        self.advance(marker)
        grads = {}
        for tag, keys, send, recv, parts, lands in self.at_chips:
            parts, lands = _chip_finish("grad_chip_finish_" + tag, parts, lands, send, recv, marker)
            for key, part, got in zip(keys, parts, lands):
                grads[key] = _chip_sum("chip_sum_%d_%s" % key, part, got, self.chip, 256)
        return grads


def _adamw_math(w, g, m, v):
    m = ADAM_B1 * m + (1.0 - ADAM_B1) * g
    v = ADAM_B2 * v + (1.0 - ADAM_B2) * jnp.square(g)
    m_hat = m / (1.0 - ADAM_B1 ** ADAM_STEP)
    v_hat = v / (1.0 - ADAM_B2 ** ADAM_STEP)
    delta = -ADAM_LR * (m_hat / (jnp.sqrt(v_hat) + ADAM_EPS) + ADAM_WD * w)
    return delta, m, v


def _adamw_small(wts, mom_m, mom_v, gathered):
    names = SMALL
    nw = len(names)
    na = len(SMALL_ARRAYS)

    def body(*refs):
        w_refs, m_refs, v_refs = (dict(zip(names, refs[i * nw:(i + 1) * nw])) for i in range(3))
        g_refs = refs[3 * nw:3 * nw + DEPTH * na]
        outs = refs[3 * nw + DEPTH * na:]
        g_out, d_out, m_out, v_out = (dict(zip(names, outs[i * nw:(i + 1) * nw])) for i in range(4))

        def update(name, at, g):
            g_out[name][at] = g
            d_out[name][at], m_out[name][at], v_out[name][at] = _adamw_math(
                w_refs[name][at], g, m_refs[name][at], v_refs[name][at])

        for l in range(DEPTH):
            got = dict(zip(SMALL_ARRAYS, g_refs[l * na:(l + 1) * na]))

            def total(key, at):
                acc = got[key][(0,) + at]
                for d in range(1, N_DEV):
                    acc = acc + got[key][(d,) + at]
                return acc

            row = (slice(l, l + 1),)
            for k, name in enumerate(NORM_NAMES):
                update(name, row, total("norms", (slice(k, k + 1),)))
            for k, name in enumerate(VEC_NAMES):
                update(name, row, total("vecs", (slice(k, k + 1),)))
            update("gmlp_v_gain", (l,), total("gain_bias", (slice(0, NH),)))
            update("b_spatial", (l,), total("gain_bias", (slice(NH, 2 * NH),)))
            update("w_spatial", (l,), total("w_spatial", ()))
            update("w_pool", (l,), total("w_pool", ()))
            update("w_dw", (l,), total("w_dw", (slice(0, CONV_K),)))

    args = [src[n] for src in (wts, mom_m, mom_v) for n in names] + [gathered[l][k] for l in range(DEPTH)
                                                                      for k in SMALL_ARRAYS]
    outs = pl.pallas_call(
        body, name="adamw_small", out_shape=[jax.ShapeDtypeStruct(wts[n].shape, F32) for _ in range(4) for n in names],
        compiler_params=_cparams(),
    )(*args)
    return tuple(dict(zip(names, outs[i * nw:(i + 1) * nw])) for i in range(4))


def _adamw_layers(name, w, g0, g1, m, v, tr):
    nl, r, cdim = w.shape
    tr = min(tr, r)
    nb = r // tr

    def body(w_ref, g0_ref, g1_ref, m_ref, v_ref, g_ref, d_ref, nm_ref, nv_ref):
        g = jnp.where(pl.program_id(0) == 0, g0_ref[...], g1_ref[...])
        g_ref[...] = g
        d_ref[...], nm_ref[...], nv_ref[...] = _adamw_math(w_ref[...], g, m_ref[...], v_ref[...])

    blk = pl.BlockSpec((None, tr, cdim), lambda l, i: (l, i, 0))
    g0_spec = pl.BlockSpec((tr, cdim), lambda l, i: (i * (1 - l) + (nb - 1) * l, 0))
    g1_spec = pl.BlockSpec((tr, cdim), lambda l, i: (i * l, 0))
    shape = jax.ShapeDtypeStruct((nl, r, cdim), F32)
    return pl.pallas_call(
        body, name=name, grid=(nl, nb), in_specs=[blk, g0_spec, g1_spec, blk, blk], out_specs=[blk] * 4,
        out_shape=[shape] * 4, compiler_params=_cparams(),
    )(w, g0, g1, m, v)


def _to_gather_layout(name, w):
    if name in ("w_in", "w_up"):
        w = w.T
    return w.astype(BF16)


def _from_gather_layout(name, g):
    return g.T if name in ("w_in", "w_up") else g


def _pack(arrays, rows):
    flat = jnp.concatenate([a.reshape(-1) for a in arrays])
    return jnp.pad(flat, (0, rows * D - flat.shape[0])).reshape(rows, D)


def _rows_for(shapes, mult=8):
    total = 0
    for shp in shapes:
        size = 1
        for dim in shp:
            size *= dim
        total += size
    return -(-total // (mult * D)) * mult


def kernel(x, mem, norm_mix_pre, norm_mix_post, w_in, w_out, gmlp_v_gain, w_spatial, b_spatial, w_pool, s_pool, w_dw, b_dw, conv_ln_g, conv_ln_b, norm_xattn_pre, norm_mem, norm_xattn_post, w_q, w_k, w_v, w_o, norm_ffn_pre, norm_ffn_post, w_up, w_down, loss_target, m_norm_mix_pre, m_norm_mix_post, m_w_in, m_w_out, m_gmlp_v_gain, m_w_spatial, m_b_spatial, m_w_pool, m_s_pool, m_w_dw, m_b_dw, m_conv_ln_g, m_conv_ln_b, m_norm_xattn_pre, m_norm_mem, m_norm_xattn_post, m_w_q, m_w_k, m_w_v, m_w_o, m_norm_ffn_pre, m_norm_ffn_post, m_w_up, m_w_down, v_norm_mix_pre, v_norm_mix_post, v_w_in, v_w_out, v_gmlp_v_gain, v_w_spatial, v_b_spatial, v_w_pool, v_s_pool, v_w_dw, v_b_dw, v_conv_ln_g, v_conv_ln_b, v_norm_xattn_pre, v_norm_mem, v_norm_xattn_post, v_w_q, v_w_k, v_w_v, v_w_o, v_norm_ffn_pre, v_norm_ffn_post, v_w_up, v_w_down):
    args = dict(locals())
    wts = {n: args[n] for n in WEIGHTS}
    mom_m = {n: args["m_" + n] for n in WEIGHTS}
    mom_v = {n: args["v_" + n] for n in WEIGHTS}
    xi, yi, ci = _position()
    me = 4 * xi + 2 * yi + ci

    dev = jnp.reshape(me, (1,)).astype(jnp.int32)
    lands = {}
    for call, names, tr in (("place_att", ("w_out", "w_q", "w_k", "w_v", "w_o"), 64), ("place_down", ("w_down",), 256),
                            ("place_up", ("w_up",), 256), ("place_in", ("w_in",), 256)):
        srcs = [(jnp.swapaxes(wts[n], 1, 2) if n in ("w_in", "w_up") else wts[n], l) for l in range(DEPTH) for n in names]
        placed = _place_own(call, srcs, dev, BF16, tr)
        lands.update(zip([(l, n) for l in range(DEPTH) for n in names], placed))
    (lands[(0, "taps")],) = _place_own("place_taps", [(_pack([w_dw], _rows_for([w_dw.shape])), None)], dev, F32, 8)
    groups = []
    for l in range(DEPTH):
        for group, names in GATHER_GROUPS:
            if (l, group) == (0, "in"):
                names = names + ("taps",)
            groups.append(((l, group), names, [lands[(l, n)] for n in names]))
    gather = _WeightGather(groups)

    def fetch(layer, group, marker):
        w = gather.fetch(layer, group, marker)
        if "taps" in w:
            blocks = w["taps"].reshape(N_DEV, -1)[:, :w_dw.size].reshape((N_DEV,) + w_dw.shape)
            w["taps"] = jnp.moveaxis(blocks, 0, 2).reshape(DEPTH, CONV_K, CW)
        return w

    reduce = _GradReduce(jnp.reshape(ci, (1,)).astype(jnp.int32), jnp.reshape(2 * xi + yi, (1,)).astype(jnp.int32))
    small = {n: wts[n] for n in SMALL if n != "w_dw"}
    loss, dx, gsmall = _local_step(x[0], mem[0], loss_target[0], fetch, small, reduce)
    grads = {key: _from_gather_layout(key[1], g) for key, g in reduce.finish((dx,)).items()}

    flat = _all_gather("gather_small_grads", [gsmall[l][k] for l in range(DEPTH) for k in SMALL_ARRAYS])
    gathered = [dict(zip(SMALL_ARRAYS, flat[l * len(SMALL_ARRAYS):(l + 1) * len(SMALL_ARRAYS)])) for l in range(DEPTH)]
    shard_cols = CW // N_DEV
    for l in range(DEPTH):
        gathered[l]["w_dw"] = lax.dynamic_slice_in_dim(gathered[l]["w_dw"], me * shard_cols, shard_cols, axis=2)
    grad_w, delta, new_m, new_v = _adamw_small(wts, mom_m, mom_v, gathered)
    for n in BIG:
        grad_w[n], delta[n], new_m[n], new_v[n] = _adamw_layers(
            "adamw_" + n, wts[n], grads[(0, n)], grads[(1, n)], mom_m[n], mom_v[n], 256)

    loss = lax.psum(loss, ("x", "y", "c"))
    return (loss, dx[None], *[grad_w[n] for n in WEIGHTS], *[delta[n] for n in WEIGHTS],
            *[new_m[n] for n in WEIGHTS], *[new_v[n] for n in WEIGHTS])
```

```python
import functools

import jax
import jax.numpy as jnp
from jax import lax
from jax.experimental import pallas as pl
from jax.experimental.pallas import tpu as pltpu

F32 = jnp.float32
BF16 = jnp.bfloat16

D = 2048
GW = 1024
PW = 512
CW = 512
HD = 128
NH = 8
NG = 4
POOL_WINDOWS = (2, 4, 8, 16)
CONV_K = 31
IN_COLS = 2 * GW + PW + 2 * CW
DFF = 4 * D
XH = 4
XHD = D // XH
ATT_SCALE = XHD ** -0.5
RMS_EPS = 1e-6
LN_EPS = 1e-5
DEPTH = 2
N_DEV = 8

ADAM_LR = 0.001
ADAM_B1 = 0.9
ADAM_B2 = 0.999
ADAM_EPS = 1e-08
ADAM_WD = 0.01
ADAM_STEP = 10

LANES = 128
CONV_HALO = 32
POOL_HALO = 16
ROW_TILE = 128
VMEM_LIMIT = 60 * 1024 * 1024

MESH = pl.DeviceIdType.MESH
NT = (((1,), (1,)), ((), ()))
NN = (((1,), (0,)), ((), ()))
TN = (((0,), (0,)), ((), ()))

BIG = ("w_out", "w_q", "w_k", "w_v", "w_o", "w_up", "w_down", "w_in")
UPDATE_ORDER = ("w_down", "w_up", "w_o", "w_q", "w_k", "w_v", "w_out", "w_in")
GATHER_GROUPS = (("in", ("w_in",)), ("att", ("w_out", "w_q", "w_k", "w_v", "w_o")), ("ffn", ("w_up", "w_down")))
SMALL = ("norm_mix_pre", "norm_mix_post", "gmlp_v_gain", "w_spatial", "b_spatial", "w_pool", "s_pool",
         "w_dw", "b_dw", "conv_ln_g", "conv_ln_b", "norm_xattn_pre", "norm_mem", "norm_xattn_post",
         "norm_ffn_pre", "norm_ffn_post")
WEIGHTS = ("norm_mix_pre", "norm_mix_post", "w_in", "w_out", "gmlp_v_gain", "w_spatial", "b_spatial", "w_pool",
           "s_pool", "w_dw", "b_dw", "conv_ln_g", "conv_ln_b", "norm_xattn_pre", "norm_mem", "norm_xattn_post",
           "w_q", "w_k", "w_v", "w_o", "norm_ffn_pre", "norm_ffn_post", "w_up", "w_down")


def _cparams():
    return pltpu.CompilerParams(vmem_limit_bytes=VMEM_LIMIT)


def _dot(a, b, dims):
    return lax.dot_general(a, b, dims, preferred_element_type=F32)


def _rms(x, g):
    y = x * lax.rsqrt(jnp.mean(x * x, axis=-1, keepdims=True) + RMS_EPS)
    return y * g


def _gelu(x):
    cdf = 0.5 * (1.0 + jnp.tanh(0.7978845608028654 * (x + 0.044715 * (x * x * x))))
    return x * cdf


def _layer_norm(x, g, b=None):
    mu = jnp.mean(x, axis=-1, keepdims=True)
    xc = x - mu
    var = jnp.mean(xc * xc, axis=-1, keepdims=True)
    y = xc * lax.rsqrt(var + LN_EPS) * g
    return y if b is None else y + b


def _sigmoid(x):
    return 1.0 / (1.0 + jnp.exp(-x))


def _gmlp_rows(zu, zv, gv):
    return _gelu(zu), _layer_norm(_gelu(zv), gv)


def _glu(cv, cg):
    return cv * _sigmoid(cg)


def _ln_silu(h, g, b):
    y = _layer_norm(h, g, b)
    return y * _sigmoid(y)


ANY = pl.BlockSpec(memory_space=pl.ANY)


ROWS_TILE = 256
COLS_TILE = 512
DW_TILE = 512
RESIDENT_K = 2048
STREAM_K_TILE = 1024
STREAM_ROWS = 512


def _k_tiles(kdim):
    if kdim <= RESIDENT_K:
        return ROWS_TILE, kdim
    return STREAM_ROWS, max(t for t in range(LANES, STREAM_K_TILE + 1, LANES) if kdim % t == 0)


def _rowop_mm(name, kind, rows, g, w, dims, out_dtype, u=None, after=()):
    s = rows[0].shape[0]
    n = w.shape[0] if dims == NT else w.shape[1]
    tm, tn = min(ROWS_TILE, s), min(COLS_TILE, n)
    ni = s // tm
    bwd = kind == "rms_bwd"

    def rows_body(*refs):
        refs = list(refs)
        row_refs = [refs.pop(0) for _ in rows]
        g_ref = refs.pop(0)
        del refs[:len(after)]
        if bwd:
            _, vjp = jax.vjp(_rms, row_refs[0][...], g_ref[...])
            a, dg = vjp(row_refs[1][...])
            refs[1][0] = dg
        else:
            a = _rms(row_refs[0][...], g_ref[...])
        refs[0][...] = a.astype(BF16)

    row_spec = pl.BlockSpec((tm, D), lambda i: (i, 0))
    res = pl.pallas_call(
        rows_body, name=name + "_rows", grid=(ni,),
        in_specs=[row_spec] * len(rows) + [pl.BlockSpec((1, D), lambda i: (0, 0))] + [ANY] * len(after),
        out_specs=[row_spec] + ([pl.BlockSpec((1, 1, D), lambda i: (i, 0, 0))] if bwd else []),
        out_shape=[jax.ShapeDtypeStruct((s, D), BF16)] + ([jax.ShapeDtypeStruct((ni, 1, D), F32)] if bwd else []),
        compiler_params=_cparams(),
    )(*rows, g, *after)
    a = res[0]

    def body(a_ref, w_ref, *rest):
        acc = _dot(a_ref[...], w_ref[...], dims)
        if u is not None:
            acc = acc * (2.0 * jnp.maximum(rest[0][...], 0.0))
        rest[-1][...] = acc.astype(out_dtype)

    w_spec = pl.BlockSpec((tn, D), lambda j: (j, 0)) if dims == NT else pl.BlockSpec((D, tn), lambda j: (0, j))
    tile = pl.BlockSpec((s, tn), lambda j: (0, j))
    out = pl.pallas_call(
        body, name=name, grid=(n // tn,),
        in_specs=[pl.BlockSpec((s, D), lambda j: (0, 0)), w_spec] + ([tile] if u is not None else []),
        out_specs=tile, out_shape=jax.ShapeDtypeStruct((s, n), out_dtype), compiler_params=_cparams(),
    )(a, w, *([u] if u is not None else []))
    return (out, *res)


def _mm_rowop(name, kind, pairs, rows, g, relu2=False, after=()):
    s, kdim = pairs[0][0].shape
    tm, tk = _k_tiles(kdim)
    tm = min(tm, s)
    ni, nk = s // tm, kdim // tk
    npair = len(pairs)

    def body(*refs):
        refs = list(refs)
        a_refs = [refs.pop(0) for _ in range(npair)]
        w_refs = [refs.pop(0) for _ in range(npair)]
        row_refs = [refs.pop(0) for _ in rows]
        g_ref = refs.pop(0)
        del refs[:len(after)]
        acc = refs.pop()
        outs = refs
        k = pl.program_id(1)

        @pl.when(k == 0)
        def _():
            acc[...] = jnp.zeros_like(acc)

        for a_ref, w_ref, (_, _, dims) in zip(a_refs, w_refs, pairs):
            a = a_ref[...]
            if relu2:
                a = jnp.square(jnp.maximum(a, 0.0))
            acc[...] += _dot(a.astype(BF16), w_ref[...], dims)

        @pl.when(k == nk - 1)
        def _():
            h = acc[...]
            if kind == "rms_res":
                outs[0][...] = row_refs[0][...] + _rms(h, g_ref[...])
                outs[1][...] = h
            else:
                _, vjp = jax.vjp(_rms, row_refs[0][...], g_ref[...])
                dx, dg = vjp(h)
                if kind == "rms_bwd_res":
                    outs[0][...] = row_refs[1][...] + dx
                    outs[1][0] = dg
                else:
                    outs[0][0] = dg

    row_spec = pl.BlockSpec((tm, D), lambda i, k: (i, 0))
    dg_shape = jax.ShapeDtypeStruct((ni, 1, D), F32)
    dg_spec = pl.BlockSpec((1, 1, D), lambda i, k: (i, 0, 0))
    in_specs = [pl.BlockSpec((tm, tk), lambda i, k: (i, k))] * npair
    for _, _, dims in pairs:
        in_specs.append(pl.BlockSpec((tk, D), lambda i, k: (k, 0)) if dims == NN
                        else pl.BlockSpec((D, tk), lambda i, k: (0, k)))
    in_specs += [row_spec] * len(rows) + [pl.BlockSpec((1, D), lambda i, k: (0, 0))] + [ANY] * len(after)
    if kind == "rms_res":
        out_shape = [jax.ShapeDtypeStruct((s, D), F32)] * 2
        out_specs = [row_spec, row_spec]
    elif kind == "rms_bwd_res":
        out_shape = [jax.ShapeDtypeStruct((s, D), F32), dg_shape]
        out_specs = [row_spec, dg_spec]
    else:
        out_shape = [dg_shape]
        out_specs = [dg_spec]
    return pl.pallas_call(
        body, name=name, grid=(ni, nk), in_specs=in_specs, out_specs=out_specs, out_shape=out_shape,
        scratch_shapes=[pltpu.VMEM((tm, D), F32)], compiler_params=_cparams(),
    )(*[p[0] for p in pairs], *[p[1] for p in pairs], *rows, g, *after)


def _mm_tn(name, a, gmat, relu2=False, after=()):
    s, m = a.shape
    tm, ts = min(DW_TILE, m), s
    ni, ns = m // tm, s // ts

    def body(a_ref, g_ref, *rest):
        o_ref, acc = rest[len(after):]
        k = pl.program_id(1)

        @pl.when(k == 0)
        def _():
            acc[...] = jnp.zeros_like(acc)

        av = a_ref[...]
        if relu2:
            av = jnp.square(jnp.maximum(av, 0.0))
        acc[...] += _dot(av.astype(BF16), g_ref[...], TN)

        @pl.when(k == ns - 1)
        def _():
            o_ref[...] = acc[...].astype(BF16)

    return pl.pallas_call(
        body, name=name, grid=(ni, ns),
        in_specs=[pl.BlockSpec((ts, tm), lambda i, k: (k, i)), pl.BlockSpec((ts, D), lambda i, k: (k, 0))]
        + [ANY] * len(after),
        out_specs=pl.BlockSpec((tm, D), lambda i, k: (i, 0)),
        out_shape=jax.ShapeDtypeStruct((m, D), BF16),
        scratch_shapes=[pltpu.VMEM((tm, D), F32)], compiler_params=_cparams(),
    )(a, gmat, *after)


def _tril():
    r = lax.broadcasted_iota(jnp.int32, (HD, HD), 0)
    c = lax.broadcasted_iota(jnp.int32, (HD, HD), 1)
    return (c <= r).astype(F32)


def _gmlp_fwd(z, gv, ws, bst, tb):
    s = z.shape[0]
    tb = min(tb, s)

    def body(zu_ref, zv_ref, gv_ref, ws_ref, bst_ref, y_ref):
        tril = _tril()
        for h in range(NH):
            cols = slice(h * HD, (h + 1) * HD)
            u, vln = _gmlp_rows(zu_ref[:, cols], zv_ref[:, cols], gv_ref[h:h + 1, :])
            wm = (ws_ref[h] * tril).astype(BF16)
            vb = vln.astype(BF16)
            for c in range(tb // HD):
                rws = slice(c * HD, (c + 1) * HD)
                mixed = _dot(wm, vb[rws], NN) + bst_ref[:, h:h + 1]
                y_ref[rws, cols] = (u[rws] * mixed).astype(BF16)

    return pl.pallas_call(
        body, name="gmlp_fwd", grid=(s // tb,),
        in_specs=[pl.BlockSpec((tb, GW), lambda i: (i, 0)), pl.BlockSpec((tb, GW), lambda i: (i, 1)),
                  pl.BlockSpec((NH, HD), lambda i: (0, 0)), pl.BlockSpec((NH, HD, HD), lambda i: (0, 0, 0)),
                  pl.BlockSpec((HD, NH), lambda i: (0, 0))],
        out_specs=pl.BlockSpec((tb, GW), lambda i: (i, 0)),
        out_shape=jax.ShapeDtypeStruct((s, GW), BF16), compiler_params=_cparams(),
    )(z, z, gv, ws, bst)


def _gmlp_bwd(z, dy, gv, ws, bst, tb):
    s = z.shape[0]
    tb = min(tb, s)
    nb = s // tb

    def body(zu_ref, zv_ref, dy_ref, gv_ref, ws_ref, bst_ref, dzu_ref, dzv_ref, dgv_ref, dws_ref, db_ref):
        tril = _tril()
        for h in range(NH):
            cols = slice(h * HD, (h + 1) * HD)
            (u, vln), vjp = jax.vjp(_gmlp_rows, zu_ref[:, cols], zv_ref[:, cols], gv_ref[h:h + 1, :])
            wmf = ws_ref[h] * tril
            wm = wmf.astype(BF16)
            wmt = wmf.T.astype(BF16)
            vb = vln.astype(BF16)
            dws = jnp.zeros((HD, HD), F32)
            db = jnp.zeros((HD, 1), F32)
            du_parts, dvln_parts = [], []
            for c in range(tb // HD):
                rws = slice(c * HD, (c + 1) * HD)
                mixed = _dot(wm, vb[rws], NN) + bst_ref[:, h:h + 1]
                dyc = dy_ref[rws, cols]
                du_parts.append(dyc * mixed)
                dmixed = dyc * u[rws]
                dmb = dmixed.astype(BF16)
                dws = dws + _dot(dmb, vb[rws], NT)
                db = db + jnp.sum(dmixed, axis=1, keepdims=True)
                dvln_parts.append(_dot(wmt, dmb, NN))
            du = jnp.concatenate(du_parts, axis=0)
            dvln = jnp.concatenate(dvln_parts, axis=0)
            dzu, dzv, dgv = vjp((du, dvln))
            dzu_ref[:, cols] = dzu.astype(BF16)
            dzv_ref[:, cols] = dzv.astype(BF16)
            dgv_ref[0, h:h + 1, :] = dgv
            dws_ref[0, h] = dws * tril
            db_ref[0, h] = jnp.broadcast_to(db, (HD, LANES))

    blk = pl.BlockSpec((tb, GW), lambda i: (i, 0))
    return pl.pallas_call(
        body, name="gmlp_bwd", grid=(nb,),
        in_specs=[blk, pl.BlockSpec((tb, GW), lambda i: (i, 1)), blk,
                  pl.BlockSpec((NH, HD), lambda i: (0, 0)), pl.BlockSpec((NH, HD, HD), lambda i: (0, 0, 0)),
                  pl.BlockSpec((HD, NH), lambda i: (0, 0))],
        out_specs=[blk, blk, pl.BlockSpec((1, NH, HD), lambda i: (i, 0, 0)),
                   pl.BlockSpec((1, NH, HD, HD), lambda i: (i, 0, 0, 0)),
                   pl.BlockSpec((1, NH, HD, LANES), lambda i: (i, 0, 0, 0))],
        out_shape=[jax.ShapeDtypeStruct((s, GW), BF16), jax.ShapeDtypeStruct((s, GW), BF16),
                   jax.ShapeDtypeStruct((nb, NH, HD), F32), jax.ShapeDtypeStruct((nb, NH, HD, HD), F32),
                   jax.ShapeDtypeStruct((nb, NH, HD, LANES), F32)],
        compiler_params=_cparams(),
    )(z, z, dy, gv, ws, bst)


def _pool_count(t0, window):
    pos = (t0 + lax.broadcasted_iota(jnp.int32, (ROW_TILE, LANES), 0)).astype(F32)
    return jnp.minimum(pos + 1.0, float(window))


def _window_sum(win, levels, back):
    n = win.shape[0]
    for lv in range(levels):
        step = 1 << lv
        win = win + pltpu.roll(win, n - step if back else step, 0)
    return win


def _pool_pooled(ppad_ref, t0, g):
    win = ppad_ref[pl.ds(t0, ROW_TILE + POOL_HALO), :]
    wsum = _window_sum(win, g + 1, False)[POOL_HALO:]
    return wsum / _pool_count(t0, POOL_WINDOWS[g]) - win[POOL_HALO:]


def _pool_fwd(z, wp, sp):
    s = z.shape[0]
    nt = s // ROW_TILE

    def body(p_ref, wp_ref, sp_ref, y_ref, ppad):
        for g in range(NG):
            cols = slice(g * LANES, (g + 1) * LANES)
            ppad[pl.ds(0, POOL_HALO), :] = jnp.zeros((POOL_HALO, LANES), F32)
            ppad[pl.ds(POOL_HALO, s), :] = p_ref[:, cols]
            wpb = wp_ref[g].astype(BF16)
            scale = sp_ref[:, cols]

            def tile(t, carry):
                t0 = pl.multiple_of(t * ROW_TILE, ROW_TILE)
                pooled = _pool_pooled(ppad, t0, g)
                y_ref[pl.ds(t0, ROW_TILE), cols] = (_dot(pooled.astype(BF16), wpb, NN) * scale).astype(BF16)
                return carry

            lax.fori_loop(0, nt, tile, 0)

    return pl.pallas_call(
        body, name="pool_fwd", grid=(1,),
        in_specs=[pl.BlockSpec((s, PW), lambda i: (0, 2 * GW // PW)),
                  pl.BlockSpec((NG, LANES, LANES), lambda i: (0, 0, 0)), pl.BlockSpec((1, PW), lambda i: (0, 0))],
        out_specs=pl.BlockSpec((s, PW), lambda i: (0, 0)),
        out_shape=jax.ShapeDtypeStruct((s, PW), BF16),
        scratch_shapes=[pltpu.VMEM((s + POOL_HALO, LANES), F32)], compiler_params=_cparams(),
    )(z, wp, sp)


def _pool_bwd(z, dy, wp, sp):
    s = z.shape[0]
    nt = s // ROW_TILE

    def body(p_ref, dy_ref, wp_ref, sp_ref, dp_ref, dwp_ref, dsp_ref, ppad, rpad, dpool):
        for g in range(NG):
            cols = slice(g * LANES, (g + 1) * LANES)
            ppad[pl.ds(0, POOL_HALO), :] = jnp.zeros((POOL_HALO, LANES), F32)
            ppad[pl.ds(POOL_HALO, s), :] = p_ref[:, cols]
            rpad[pl.ds(s, POOL_HALO), :] = jnp.zeros((POOL_HALO, LANES), F32)
            wpb = wp_ref[g].astype(BF16)
            scale = sp_ref[:, cols]

            def tile(t, carry):
                dwp, dsp = carry
                t0 = pl.multiple_of(t * ROW_TILE, ROW_TILE)
                pooled = _pool_pooled(ppad, t0, g)
                pb = pooled.astype(BF16)
                dyt = dy_ref[pl.ds(t0, ROW_TILE), cols]
                dsp = dsp + jnp.sum(dyt * _dot(pb, wpb, NN), axis=0, keepdims=True)
                dmm = (dyt * scale).astype(BF16)
                dwp = dwp + _dot(pb, dmm, TN)
                dpooled = _dot(dmm, wpb, NT)
                rpad[pl.ds(t0, ROW_TILE), :] = dpooled / _pool_count(t0, POOL_WINDOWS[g])
                dpool[pl.ds(t0, ROW_TILE), :] = dpooled
                return dwp, dsp

            dwp, dsp = lax.fori_loop(0, nt, tile, (jnp.zeros((LANES, LANES), F32), jnp.zeros((1, LANES), F32)))
            dwp_ref[g] = dwp
            dsp_ref[:, cols] = dsp

            def tile2(t, carry):
                t0 = pl.multiple_of(t * ROW_TILE, ROW_TILE)
                win = rpad[pl.ds(t0, ROW_TILE + POOL_HALO), :]
                back = _window_sum(win, g + 1, True)[:ROW_TILE]
                rows = pl.ds(t0, ROW_TILE)
                dp_ref[rows, cols] = (back - dpool[rows, :]).astype(BF16)
                return carry

            lax.fori_loop(0, nt, tile2, 0)

    return pl.pallas_call(
        body, name="pool_bwd", grid=(1,),
        in_specs=[pl.BlockSpec((s, PW), lambda i: (0, 2 * GW // PW)), pl.BlockSpec((s, PW), lambda i: (0, GW // PW)),
                  pl.BlockSpec((NG, LANES, LANES), lambda i: (0, 0, 0)), pl.BlockSpec((1, PW), lambda i: (0, 0))],
        out_specs=[pl.BlockSpec((s, PW), lambda i: (0, 0)), pl.BlockSpec((NG, LANES, LANES), lambda i: (0, 0, 0)),
                   pl.BlockSpec((1, PW), lambda i: (0, 0))],
        out_shape=[jax.ShapeDtypeStruct((s, PW), BF16), jax.ShapeDtypeStruct((NG, LANES, LANES), F32),
                   jax.ShapeDtypeStruct((1, PW), F32)],
        scratch_shapes=[pltpu.VMEM((s + POOL_HALO, LANES), F32), pltpu.VMEM((s + POOL_HALO, LANES), F32),
                        pltpu.VMEM((s, LANES), F32)],
        compiler_params=_cparams(),
    )(z, dy, wp, sp)


CONV_LEAD = CONV_HALO - (CONV_K - 1)


def _conv_taps(win, wdw_ref, lead, reverse):
    n = win.shape[0]
    acc = jnp.zeros((ROW_TILE, CW), F32)
    for j in range(CONV_K):
        tap = (CONV_K - 1 - j) if reverse else j
        acc = acc + wdw_ref[tap:tap + 1, :] * pltpu.roll(win, (n - (lead + j)) % n, 0)[:ROW_TILE]
    return acc


def _conv_fill_glu(cv_ref, cg_ref, xpad, s):
    xpad[pl.ds(0, CONV_HALO), :] = jnp.zeros((CONV_HALO, CW), F32)

    def fill(t, carry):
        t0 = pl.multiple_of(t * ROW_TILE, ROW_TILE)
        rows = pl.ds(t0, ROW_TILE)
        xpad[pl.ds(t0 + CONV_HALO, ROW_TILE), :] = _glu(cv_ref[rows, :], cg_ref[rows, :])
        return carry

    lax.fori_loop(0, s // ROW_TILE, fill, 0)


def _conv_fwd(z, wdw, bdw, lng, lnb):
    s = z.shape[0]

    def body(cv_ref, cg_ref, wdw_ref, bdw_ref, lng_ref, lnb_ref, y_ref, xpad):
        _conv_fill_glu(cv_ref, cg_ref, xpad, s)

        def tile(t, carry):
            t0 = pl.multiple_of(t * ROW_TILE, ROW_TILE)
            win = xpad[pl.ds(t0, ROW_TILE + CONV_HALO), :]
            hc = _conv_taps(win, wdw_ref, CONV_LEAD, False) + bdw_ref[...]
            y_ref[pl.ds(t0, ROW_TILE), :] = _ln_silu(hc, lng_ref[...], lnb_ref[...]).astype(BF16)
            return carry

        lax.fori_loop(0, s // ROW_TILE, tile, 0)

    vec = pl.BlockSpec((1, CW), lambda i: (0, 0))
    return pl.pallas_call(
        body, name="conv_fwd", grid=(1,),
        in_specs=[pl.BlockSpec((s, CW), lambda i: (0, (2 * GW + PW) // CW)),
                  pl.BlockSpec((s, CW), lambda i: (0, (2 * GW + PW) // CW + 1)),
                  pl.BlockSpec((CONV_K + 1, CW), lambda i: (0, 0)), vec, vec, vec],
        out_specs=pl.BlockSpec((s, CW), lambda i: (0, 0)),
        out_shape=jax.ShapeDtypeStruct((s, CW), BF16),
        scratch_shapes=[pltpu.VMEM((s + CONV_HALO, CW), F32)], compiler_params=_cparams(),
    )(z, z, wdw, bdw, lng, lnb)


def _conv_bwd(z, dy, wdw, bdw, lng, lnb):
    s = z.shape[0]

    def body(cv_ref, cg_ref, dy_ref, wdw_ref, bdw_ref, lng_ref, lnb_ref,
             dcv_ref, dcg_ref, dwdw_ref, dbdw_ref, dlng_ref, dlnb_ref, xpad, dpad):
        _conv_fill_glu(cv_ref, cg_ref, xpad, s)
        dpad[pl.ds(s, CONV_HALO), :] = jnp.zeros((CONV_HALO, CW), F32)
        dwdw_ref[...] = jnp.zeros((CONV_K + 1, CW), F32)

        def tile(t, carry):
            db, dg, dbeta = carry
            t0 = pl.multiple_of(t * ROW_TILE, ROW_TILE)
            win = xpad[pl.ds(t0, ROW_TILE + CONV_HALO), :]
            hc = _conv_taps(win, wdw_ref, CONV_LEAD, False) + bdw_ref[...]
            _, vjp = jax.vjp(_ln_silu, hc, lng_ref[...], lnb_ref[...])
            dhc, dg_t, dbeta_t = vjp(dy_ref[pl.ds(t0, ROW_TILE), :])
            dpad[pl.ds(t0, ROW_TILE), :] = dhc
            n = win.shape[0]
            for j in range(CONV_K):
                shifted = pltpu.roll(win, (n - (CONV_LEAD + j)) % n, 0)[:ROW_TILE]
                dwdw_ref[j:j + 1, :] += jnp.sum(dhc * shifted, axis=0, keepdims=True)
            return db + jnp.sum(dhc, axis=0, keepdims=True), dg + dg_t, dbeta + dbeta_t

        zero = jnp.zeros((1, CW), F32)
        db, dg, dbeta = lax.fori_loop(0, s // ROW_TILE, tile, (zero, zero, zero))
        dbdw_ref[...] = db
        dlng_ref[...] = dg
        dlnb_ref[...] = dbeta

        def tile2(t, carry):
            t0 = pl.multiple_of(t * ROW_TILE, ROW_TILE)
            rows = pl.ds(t0, ROW_TILE)
            win = dpad[pl.ds(t0, ROW_TILE + CONV_HALO), :]
            dglu = _conv_taps(win, wdw_ref, 0, True)
            _, vjp = jax.vjp(_glu, cv_ref[rows, :], cg_ref[rows, :])
            dcv, dcg = vjp(dglu)
            dcv_ref[rows, :] = dcv.astype(BF16)
            dcg_ref[rows, :] = dcg.astype(BF16)
            return carry

        lax.fori_loop(0, s // ROW_TILE, tile2, 0)

    vec = pl.BlockSpec((1, CW), lambda i: (0, 0))
    full = pl.BlockSpec((s, CW), lambda i: (0, 0))
    wspec = pl.BlockSpec((CONV_K + 1, CW), lambda i: (0, 0))
    vshape = jax.ShapeDtypeStruct((1, CW), F32)
    return pl.pallas_call(
        body, name="conv_bwd", grid=(1,),
        in_specs=[pl.BlockSpec((s, CW), lambda i: (0, (2 * GW + PW) // CW)),
                  pl.BlockSpec((s, CW), lambda i: (0, (2 * GW + PW) // CW + 1)),
                  pl.BlockSpec((s, CW), lambda i: (0, (GW + PW) // CW)), wspec, vec, vec, vec],
        out_specs=[full, full, wspec, vec, vec, vec],
        out_shape=[jax.ShapeDtypeStruct((s, CW), BF16), jax.ShapeDtypeStruct((s, CW), BF16),
                   jax.ShapeDtypeStruct((CONV_K + 1, CW), F32), vshape, vshape, vshape],
        scratch_shapes=[pltpu.VMEM((s + CONV_HALO, CW), F32), pltpu.VMEM((s + CONV_HALO, CW), F32)],
        compiler_params=_cparams(),
    )(z, z, dy, wdw, bdw, lng, lnb)


def _softmax_rows(sc):
    e = jnp.exp(sc - jnp.max(sc, axis=-1, keepdims=True))
    return e / jnp.sum(e, axis=-1, keepdims=True)


def _attn_fwd(q, k, v, tq):
    s, m = q.shape[0], k.shape[0]
    tq = min(tq, s)

    def body(q_ref, k_ref, v_ref, o_ref):
        for h in range(XH):
            cols = slice(h * XHD, (h + 1) * XHD)
            p = _softmax_rows(_dot(q_ref[:, cols], k_ref[:, cols], NT) * ATT_SCALE)
            o_ref[:, cols] = _dot(p.astype(BF16), v_ref[:, cols], NN).astype(BF16)

    kv = pl.BlockSpec((m, D), lambda i: (0, 0))
    return pl.pallas_call(
        body, name="attn_fwd", grid=(s // tq,),
        in_specs=[pl.BlockSpec((tq, D), lambda i: (i, 0)), kv, kv],
        out_specs=pl.BlockSpec((tq, D), lambda i: (i, 0)),
        out_shape=jax.ShapeDtypeStruct((s, D), BF16), compiler_params=_cparams(),
    )(q, k, v)


def _attn_bwd(q, k, v, do, tq):
    s, m = q.shape[0], k.shape[0]
    tq = min(tq, s)

    def body(q_ref, k_ref, v_ref, do_ref, dq_ref, dk_ref, dv_ref):
        @pl.when(pl.program_id(0) == 0)
        def _():
            dk_ref[...] = jnp.zeros_like(dk_ref)
            dv_ref[...] = jnp.zeros_like(dv_ref)

        for h in range(XH):
            cols = slice(h * XHD, (h + 1) * XHD)
            qh, kh, vh, doh = q_ref[:, cols], k_ref[:, cols], v_ref[:, cols], do_ref[:, cols]
            p = _softmax_rows(_dot(qh, kh, NT) * ATT_SCALE)
            dp = _dot(doh, vh, NT)
            dv_ref[:, cols] += _dot(p.astype(BF16), doh, TN)
            ds = (p * (dp - jnp.sum(p * dp, axis=-1, keepdims=True)) * ATT_SCALE).astype(BF16)
            dq_ref[:, cols] = _dot(ds, kh, NN).astype(BF16)
            dk_ref[:, cols] += _dot(ds, qh, TN)

    kv = pl.BlockSpec((m, D), lambda i: (0, 0))
    qs = pl.BlockSpec((tq, D), lambda i: (i, 0))
    return pl.pallas_call(
        body, name="attn_bwd", grid=(s // tq,),
        in_specs=[qs, kv, kv, qs], out_specs=[qs, kv, kv],
        out_shape=[jax.ShapeDtypeStruct((s, D), BF16), jax.ShapeDtypeStruct((m, D), F32),
                   jax.ShapeDtypeStruct((m, D), F32)],
        compiler_params=_cparams(),
    )(q, k, v, do)


def _loss_head(y, target, tm):
    s = y.shape[0]
    tm = min(tm, s)

    def body(y_ref, t_ref, dy_ref, part_ref):
        err = y_ref[...] - t_ref[...]
        dy_ref[...] = err * (1.0 / D)
        part_ref[...] = jnp.full((1, 8, LANES), 0.5 * jnp.sum(err * err) * (1.0 / D), F32)

    blk = pl.BlockSpec((tm, D), lambda i: (i, 0))
    dy, part = pl.pallas_call(
        body, name="loss_head", grid=(s // tm,), in_specs=[blk, blk],
        out_specs=[blk, pl.BlockSpec((1, 8, LANES), lambda i: (i, 0, 0))],
        out_shape=[jax.ShapeDtypeStruct((s, D), F32), jax.ShapeDtypeStruct((s // tm, 8, LANES), F32)],
        compiler_params=_cparams(),
    )(y, target)
    return dy, jnp.sum(part[:, 0, 0])


def _layer_fwd(x0, mem, w, p, fetch):
    z, hn0 = _rowop_mm("mix_in", "rms", (x0,), p["norm_mix_pre"], w["w_in"], NT, F32)
    ya = _gmlp_fwd(z, p["gmlp_v_gain"], p["w_spatial"], p["b_spatial_t"], 512)
    yb = _pool_fwd(z, p["w_pool"], p["s_pool"])
    yc = _conv_fwd(z, p["w_dw"], p["b_dw"], p["conv_ln_g"], p["conv_ln_b"])
    y = jnp.concatenate([ya, yb, yc], axis=1)
    w.update(fetch("att", (y,)))
    x1, h0 = _mm_rowop("mix_out", "rms_res", [(y, w["w_out"], NN)], (x0,), p["norm_mix_post"])
    q, hn1 = _rowop_mm("att_q", "rms", (x1,), p["norm_xattn_pre"], w["w_q"], NN, BF16)
    k, mn = _rowop_mm("att_k", "rms", (mem,), p["norm_mem"], w["w_k"], NN, BF16)
    v, _ = _rowop_mm("att_v", "rms", (mem,), p["norm_mem"], w["w_v"], NN, BF16)
    o = _attn_fwd(q, k, v, 256)
    x2, h1 = _mm_rowop("att_o", "rms_res", [(o, w["w_o"], NN)], (x1,), p["norm_xattn_post"])
    w.update(fetch("ffn", (x2,)))
    u, hn2 = _rowop_mm("ffn_up", "rms", (x2,), p["norm_ffn_pre"], w["w_up"], NT, F32)
    x3, h2 = _mm_rowop("ffn_down", "rms_res", [(u, w["w_down"], NN)], (x2,), p["norm_ffn_post"], relu2=True)
    saved = dict(x0=x0, z=z, hn0=hn0, y=y, h0=h0, x1=x1, q=q, hn1=hn1, k=k, v=v, mn=mn, o=o, h1=h1, x2=x2, u=u,
                 hn2=hn2, h2=h2)
    return x3, saved


def _layer_bwd(dx3, mem, w, p, sv, red):
    gs = {}
    du, dh2, dg = _rowop_mm("ffn_down_bwd", "rms_bwd", (sv["h2"], dx3), p["norm_ffn_post"], w["w_down"], NT, BF16,
                            u=sv["u"], after=red.after())
    gs["norm_ffn_post"] = jnp.sum(dg, axis=0)
    g_down = _mm_tn("ffn_down_dw", sv["u"], dh2, relu2=True)
    red.advance((g_down,))
    dx2, dg = _mm_rowop("ffn_up_bwd", "rms_bwd_res", [(du, w["w_up"], NN)], (sv["x2"], dx3), p["norm_ffn_pre"],
                        after=red.after())
    gs["norm_ffn_pre"] = jnp.sum(dg, axis=0)
    g_up = _mm_tn("ffn_up_dw", du, sv["hn2"])
    red.add("ffn", ("w_down", "w_up"), [g_down, g_up])
    do, dh1, dg = _rowop_mm("att_o_bwd", "rms_bwd", (sv["h1"], dx2), p["norm_xattn_post"], w["w_o"], NT, BF16,
                            after=red.after())
    gs["norm_xattn_post"] = jnp.sum(dg, axis=0)
    g_o = _mm_tn("att_o_dw", sv["o"], dh1)
    red.advance((g_o,))
    dq, dk, dv = _attn_bwd(sv["q"], sv["k"], sv["v"], do, 256)
    dk, dv = dk.astype(BF16), dv.astype(BF16)
    dx1, dg = _mm_rowop("att_q_bwd", "rms_bwd_res", [(dq, w["w_q"], NT)], (sv["x1"], dx2), p["norm_xattn_pre"],
                        after=red.after())
    gs["norm_xattn_pre"] = jnp.sum(dg, axis=0)
    g_q = _mm_tn("att_q_dw", sv["hn1"], dq)
    g_k = _mm_tn("att_k_dw", sv["mn"], dk)
    g_v = _mm_tn("att_v_dw", sv["mn"], dv)
    (dg,) = _mm_rowop("att_kv_bwd", "rms_bwd_gain", [(dk, w["w_k"], NT), (dv, w["w_v"], NT)], (mem,), p["norm_mem"])
    gs["norm_mem"] = jnp.sum(dg, axis=0)
    red.add("att", ("w_o", "w_q", "w_k", "w_v"), [g_o, g_q, g_k, g_v])
    dy, dh0, dg = _rowop_mm("mix_out_bwd", "rms_bwd", (sv["h0"], dx1), p["norm_mix_post"], w["w_out"], NT, F32,
                            after=red.after())
    gs["norm_mix_post"] = jnp.sum(dg, axis=0)
    g_out = _mm_tn("mix_out_dw", sv["y"], dh0)
    red.advance((g_out,))
    red.add("out", ("w_out",), [g_out])
    z = sv["z"]
    dzu, dzv, dgv, dws, dbs = _gmlp_bwd(z, dy, p["gmlp_v_gain"], p["w_spatial"], p["b_spatial_t"], 512)
    gs["gmlp_v_gain"] = jnp.sum(dgv, axis=0)
    gs["w_spatial"] = jnp.sum(dws, axis=0)
    gs["b_spatial"] = jnp.sum(dbs[..., 0], axis=0)
    dp, gs["w_pool"], gs["s_pool"] = _pool_bwd(z, dy, p["w_pool"], p["s_pool"])
    dcv, dcg, dwdw, gs["b_dw"], gs["conv_ln_g"], gs["conv_ln_b"] = _conv_bwd(
        z, dy, p["w_dw"], p["b_dw"], p["conv_ln_g"], p["conv_ln_b"])
    dz = jnp.concatenate([dzu, dzv, dp, dcv, dcg], axis=1)
    red.advance((dz,))
    g_in = _mm_tn("mix_in_dw", dz, sv["hn0"], after=red.after())
    red.add("in", ("w_in",), [g_in])
    red.advance((g_in,))
    dx0, dg = _mm_rowop("mix_in_bwd", "rms_bwd_res", [(dz, w["w_in"], NN)], (sv["x0"], dx1), p["norm_mix_pre"],
                        after=red.after())
    gs["norm_mix_pre"] = jnp.sum(dg, axis=0)
    return dx0, _small_grad_arrays(gs, dwdw)


NORM_NAMES = ("norm_mix_pre", "norm_mix_post", "norm_xattn_pre", "norm_mem", "norm_xattn_post", "norm_ffn_pre",
              "norm_ffn_post")
VEC_NAMES = ("s_pool", "b_dw", "conv_ln_g", "conv_ln_b")
SMALL_ARRAYS = ("norms", "gain_bias", "w_spatial", "w_pool", "vecs", "w_dw")


def _small_grad_arrays(gs, dwdw):
    return {"norms": jnp.concatenate([gs[n] for n in NORM_NAMES], axis=0),
            "gain_bias": jnp.concatenate([gs["gmlp_v_gain"], gs["b_spatial"]], axis=0),
            "w_spatial": gs["w_spatial"], "w_pool": gs["w_pool"],
            "vecs": jnp.concatenate([gs[n] for n in VEC_NAMES], axis=0), "w_dw": dwdw}


def _split_small_grads(arrays):
    out = {n: arrays["norms"][k] for k, n in enumerate(NORM_NAMES)}
    out.update({n: arrays["vecs"][k] for k, n in enumerate(VEC_NAMES)})
    out.update(gmlp_v_gain=arrays["gain_bias"][:NH], b_spatial=arrays["gain_bias"][NH:], w_spatial=arrays["w_spatial"],
               w_pool=arrays["w_pool"], w_dw=arrays["w_dw"][:CONV_K])
    return out


def _layer_params(small, l):
    p = {n: small[n][l].reshape(1, -1) for n in ("norm_mix_pre", "norm_mix_post", "s_pool", "b_dw", "conv_ln_g",
                                                   "conv_ln_b", "norm_xattn_pre", "norm_mem", "norm_xattn_post",
                                                   "norm_ffn_pre", "norm_ffn_post")}
    p["gmlp_v_gain"] = small["gmlp_v_gain"][l]
    p["w_spatial"] = small["w_spatial"][l]
    p["b_spatial_t"] = small["b_spatial"][l].T
    p["w_pool"] = small["w_pool"][l]
    p["w_dw"] = jnp.pad(small["w_dw"][l], ((0, 1), (0, 0)))
    return p


def _local_step(x, mem, target, fetch, small, red):
    small = dict(small)
    saved, weights, params = [], [], []
    h = x
    marker = ()
    for l in range(DEPTH):
        w = fetch(l, "in", marker)
        if "taps" in w:
            small["w_dw"] = w.pop("taps")
        p = _layer_params(small, l)
        h, sv = _layer_fwd(h, mem, w, p, functools.partial(fetch, l))
        marker = (h,)
        saved.append(sv)
        weights.append(w)
        params.append(p)
    dh, loss = _loss_head(h, target, 512)
    gsmall = [None] * DEPTH
    for l in reversed(range(DEPTH)):
        red.layer = l
        dh, gsmall[l] = _layer_bwd(dh, mem, weights[l], params[l], saved[l], red)
    return loss, dh, gsmall


HBM = pl.BlockSpec(memory_space=pltpu.HBM)


def _position():
    return lax.axis_index("x"), lax.axis_index("y"), lax.axis_index("c")


SEM = pl.BlockSpec(memory_space=pltpu.SEMAPHORE)
EFFECT = pltpu.SideEffectType.DATAFLOW_SIDE_EFFECTING
TOKEN = jax.ShapeDtypeStruct((8, LANES), F32)
TOKEN_SPEC = pl.BlockSpec(memory_space=pltpu.VMEM)


def _landing(shape, dtype):
    return pltpu.with_memory_space_constraint(lax.empty(shape, dtype), pltpu.HBM)


def _hbm_shapes(arrays):
    return [pltpu.HBM(a.shape, a.dtype) for a in arrays]


def _block(ref, r, dev):
    return ref.at[pl.ds((4 * dev[0] + 2 * dev[1] + dev[2]) * r, r), :]


def _split_call(name, body, thru, sems_in, after, sems_out, token):
    n = len(thru)
    out_shape = [pltpu.SemaphoreType.DMA(s) for s in sems_out] + _hbm_shapes(thru) + ([TOKEN] if token else [])
    out_specs = [SEM] * len(sems_out) + [HBM] * n + ([TOKEN_SPEC] if token else [])
    return pl.pallas_call(
        body, name=name, in_specs=[HBM] * n + [SEM] * len(sems_in) + [ANY] * len(after),
        out_specs=out_specs, out_shape=out_shape,
        input_output_aliases={i: len(sems_out) + i for i in range(n)},
        compiler_params=pltpu.CompilerParams(has_side_effects=EFFECT),
    )(*thru, *sems_in, *after)


def _place_own(name, srcs, dev, out_dtype, tr):
    n = len(srcs)
    r, cols = srcs[0][0].shape[-2:]
    tr = r if r < 16 else _row_tile(r, tr)
    nb = r // tr

    def body(dev_ref, *refs):
        for a in range(n):
            refs[n + a][...] = refs[a][...].astype(out_dtype)

    in_specs = [pl.BlockSpec((tr, cols), lambda i, d: (i, 0)) if l is None
                else pl.BlockSpec((None, tr, cols), lambda i, d, l=l: (l, i, 0)) for _, l in srcs]
    return pl.pallas_call(
        body, name=name,
        grid_spec=pltpu.PrefetchScalarGridSpec(
            num_scalar_prefetch=1, grid=(nb,), in_specs=in_specs,
            out_specs=[pl.BlockSpec((tr, cols), lambda i, d: (d[0] * nb + i, 0))] * n),
        out_shape=[jax.ShapeDtypeStruct((N_DEV * r, cols), out_dtype)] * n, compiler_params=_cparams(),
    )(dev, *[a for a, _ in srcs])


def _gather_peers(x, y, c):
    return [(1 - x, y, c), (x, 1 - y, c), (1 - x, 1 - y, c), (x, y, 1 - c)]


def _block_rows(land):
    return land.shape[0] // N_DEV


def _gather_start(name, lands, after):
    n = len(lands)

    def body(*refs):
        lz = refs[:n]
        send_sems, recv_sems = refs[n + len(after)], refs[n + len(after) + 1]
        token = refs[-1]
        x, y, c = _position()
        for a in range(n):
            own = _block(lz[a], _block_rows(lands[a]), (x, y, c))
            for k, to in enumerate(_gather_peers(x, y, c)):
                pltpu.make_async_remote_copy(src_ref=own, dst_ref=own, send_sem=send_sems.at[k], recv_sem=recv_sems.at[k],
                                             device_id=to, device_id_type=MESH).start()
        token[...] = jnp.zeros_like(token)

    out = _split_call(name, body, list(lands), [], after, [(4,), (4,)], True)
    return out[0], out[1], out[2:2 + n], out[-1]


def _gather_forward(name, lands, recv_sems, after):
    n = len(lands)

    def body(*refs):
        lz = refs[:n]
        recv0 = refs[n]
        fsend, frecv = refs[n + 1 + len(after)], refs[n + 2 + len(after)]
        token = refs[-1]
        x, y, c = _position()
        chips = _gather_peers(x, y, c)[:3]
        for a in range(n):
            for j, chip in enumerate(chips):
                blk = _block(lz[a], _block_rows(lands[a]), chip)
                pltpu.make_async_remote_copy(src_ref=blk, dst_ref=blk, send_sem=fsend.at[j], recv_sem=recv0.at[j],
                                             device_id=(x, y, c), device_id_type=MESH).wait_recv()
        for a in range(n):
            for j, chip in enumerate(chips):
                blk = _block(lz[a], _block_rows(lands[a]), chip)
                pltpu.make_async_remote_copy(src_ref=blk, dst_ref=blk, send_sem=fsend.at[j], recv_sem=frecv.at[j],
                                             device_id=(x, y, 1 - c), device_id_type=MESH).start()
        token[...] = jnp.zeros_like(token)

    out = _split_call(name, body, list(lands), [recv_sems], after, [(3,), (3,)], True)
    return out[0], out[1], out[2:2 + n], out[-1]


def _gather_finish(name, lands, send_sems, recv_sems, fsend, frecv, after):
    n = len(lands)

    def body(*refs):
        lz = refs[:n]
        send0, recv0, fsend_ref, frecv_ref = refs[n:n + 4]
        x, y, c = _position()
        me = (x, y, c)
        chips = _gather_peers(x, y, c)[:3]
        for a in range(n):
            r = _block_rows(lands[a])
            sib = _block(lz[a], r, (x, y, 1 - c))
            pltpu.make_async_remote_copy(src_ref=sib, dst_ref=sib, send_sem=send0.at[3], recv_sem=recv0.at[3],
                                         device_id=me, device_id_type=MESH).wait_recv()
            for j, chip in enumerate(chips):
                blk = _block(lz[a], r, (chip[0], chip[1], 1 - c))
                pltpu.make_async_remote_copy(src_ref=blk, dst_ref=blk, send_sem=fsend_ref.at[j], recv_sem=frecv_ref.at[j],
                                             device_id=me, device_id_type=MESH).wait_recv()
            own = _block(lz[a], r, me)
            for k in range(4):
                pltpu.make_async_remote_copy(src_ref=own, dst_ref=own, send_sem=send0.at[k], recv_sem=recv0.at[k],
                                             device_id=me, device_id_type=MESH).wait_send()
            for j, chip in enumerate(chips):
                blk = _block(lz[a], r, chip)
                pltpu.make_async_remote_copy(src_ref=blk, dst_ref=blk, send_sem=fsend_ref.at[j], recv_sem=frecv_ref.at[j],
                                             device_id=me, device_id_type=MESH).wait_send()

    return _split_call(name, body, list(lands), [send_sems, recv_sems, fsend, frecv], after, [], False)


def _sibling_start(name, grads, after):
    n = len(grads)
    lands = [_landing((4, g.shape[0] // N_DEV, D), g.dtype) for g in grads]

    def body(*refs):
        ins, lz = refs[:n], refs[n:2 * n]
        send_sem, recv_sem = refs[2 * n + len(after)], refs[2 * n + len(after) + 1]
        token = refs[-1]
        x, y, c = _position()
        for a in range(n):
            r = grads[a].shape[0] // N_DEV
            for q in range(4):
                pltpu.make_async_remote_copy(
                    src_ref=ins[a].at[pl.ds((2 * q + 1 - c) * r, r), :], dst_ref=lz[a].at[q], send_sem=send_sem.at[0],
                    recv_sem=recv_sem.at[0], device_id=(x, y, 1 - c), device_id_type=MESH).start()
        token[...] = jnp.zeros_like(token)

    out = _split_call(name, body, list(grads) + lands, [], after, [(1,), (1,)], True)
    return out[0], out[1], out[2:2 + n], out[2 + n:2 + 2 * n], out[-1]


def _sibling_finish(name, grads, lands, send_sem, recv_sem, after):
    n = len(grads)

    def body(*refs):
        ins, lz = refs[:n], refs[n:2 * n]
        send_ref, recv_ref = refs[2 * n], refs[2 * n + 1]
        x, y, c = _position()
        for a in range(n):
            r = grads[a].shape[0] // N_DEV
            for q in range(4):
                cp = pltpu.make_async_remote_copy(
                    src_ref=ins[a].at[pl.ds((2 * q + 1 - c) * r, r), :], dst_ref=lz[a].at[q], send_sem=send_ref.at[0],
                    recv_sem=recv_ref.at[0], device_id=(x, y, c), device_id_type=MESH)
                cp.wait_send()
                cp.wait_recv()

    out = _split_call(name, body, list(grads) + list(lands), [send_sem, recv_sem], after, [], False)
    return out[:n], out[n:2 * n]


def _chip_start(name, parts, after):
    n = len(parts)
    lands = [_landing((3,) + p.shape[1:], p.dtype) for p in parts]

    def body(*refs):
        ins, lz = refs[:n], refs[n:2 * n]
        send_sems, recv_sems = refs[2 * n + len(after)], refs[2 * n + len(after) + 1]
        token = refs[-1]
        x, y, c = _position()
        for a in range(n):
            for j, chip in enumerate(_gather_peers(x, y, c)[:3]):
                pltpu.make_async_remote_copy(
                    src_ref=ins[a].at[2 * chip[0] + chip[1]], dst_ref=lz[a].at[j], send_sem=send_sems.at[j],
                    recv_sem=recv_sems.at[j], device_id=chip, device_id_type=MESH).start()
        token[...] = jnp.zeros_like(token)

    out = _split_call(name, body, list(parts) + lands, [], after, [(3,), (3,)], True)
    return out[0], out[1], out[2:2 + n], out[2 + n:2 + 2 * n], out[-1]


def _chip_finish(name, parts, lands, send_sems, recv_sems, after):
    n = len(parts)

    def body(*refs):
        ins, lz = refs[:n], refs[n:2 * n]
        send_ref, recv_ref = refs[2 * n], refs[2 * n + 1]
        me = _position()
        for a in range(n):
            for j in range(3):
                cp = pltpu.make_async_remote_copy(
                    src_ref=ins[a].at[j], dst_ref=lz[a].at[j], send_sem=send_ref.at[j], recv_sem=recv_ref.at[j],
                    device_id=me, device_id_type=MESH)
                cp.wait_send()
                cp.wait_recv()

    out = _split_call(name, body, list(parts) + list(lands), [send_sems, recv_sems], after, [], False)
    return out[:n], out[n:2 * n]


def _other_devices(x, y, c):
    return [(x + (k >> 2 & 1) * (1 - 2 * x), y + (k >> 1 & 1) * (1 - 2 * y), c + (k & 1) * (1 - 2 * c))
            for k in range(1, N_DEV)]


def _broadcast_start(name, arrays, after):
    n = len(arrays)
    lands = [_landing((N_DEV,) + a.shape, a.dtype) for a in arrays]

    def body(*refs):
        ins, lz = refs[:n], refs[n:2 * n]
        send_sems, recv_sems = refs[2 * n + len(after)], refs[2 * n + len(after) + 1]
        token = refs[-1]
        x, y, c = _position()
        for a in range(n):
            for k, peer in enumerate(_other_devices(x, y, c)):
                pltpu.make_async_remote_copy(
                    src_ref=ins[a], dst_ref=lz[a].at[4 * x + 2 * y + c], send_sem=send_sems.at[k],
                    recv_sem=recv_sems.at[k], device_id=peer, device_id_type=MESH).start()
        token[...] = jnp.zeros_like(token)

    out = _split_call(name, body, list(arrays) + lands, [], after, [(N_DEV - 1,), (N_DEV - 1,)], True)
    return out[0], out[1], out[2:2 + n], out[2 + n:2 + 2 * n], out[-1]


def _broadcast_finish(name, arrays, lands, send_sems, recv_sems, after):
    n = len(arrays)

    def body(*refs):
        ins, lz = refs[:n], refs[n:2 * n]
        send_ref, recv_ref = refs[2 * n], refs[2 * n + 1]
        x, y, c = _position()
        for a in range(n):
            for k, peer in enumerate(_other_devices(x, y, c)):
                cp = pltpu.make_async_remote_copy(
                    src_ref=ins[a], dst_ref=lz[a].at[4 * peer[0] + 2 * peer[1] + peer[2]], send_sem=send_ref.at[k],
                    recv_sem=recv_ref.at[k], device_id=(x, y, c), device_id_type=MESH)
                cp.wait_send()
                cp.wait_recv()

    out = _split_call(name, body, list(arrays) + list(lands), [send_sems, recv_sems], after, [], False)
    return out[:n], out[n:2 * n]


def _row_tile(r, target):
    return max(t for t in range(16, min(r, target) + 1, 16) if r % t == 0)


def _chip_partial(name, grad, got, c, tr):
    r = grad.shape[0] // N_DEV
    tr = _row_tile(r, tr)
    g4 = grad.reshape(4, 2, r, D)

    def body(c_ref, g_ref, s_ref, o_ref):
        o_ref[...] = (g_ref[...].astype(F32) + s_ref[...].astype(F32)).astype(BF16)

    return pl.pallas_call(
        body, name=name,
        grid_spec=pltpu.PrefetchScalarGridSpec(
            num_scalar_prefetch=1, grid=(4, r // tr),
            in_specs=[pl.BlockSpec((None, None, tr, D), lambda q, i, c_ref: (q, c_ref[0], i, 0)),
                      pl.BlockSpec((None, tr, D), lambda q, i, c_ref: (q, i, 0))],
            out_specs=pl.BlockSpec((None, tr, D), lambda q, i, c_ref: (q, i, 0))),
        out_shape=jax.ShapeDtypeStruct((4, r, D), BF16), compiler_params=_cparams(),
    )(c, g4, got)


def _chip_sum(name, part, got, chip, tr):
    r = part.shape[1]
    tr = _row_tile(r, tr)

    def body(q_ref, p_ref, g_ref, o_ref):
        acc = p_ref[...].astype(F32)
        for j in range(3):
            acc = acc + g_ref[j].astype(F32)
        o_ref[...] = acc

    return pl.pallas_call(
        body, name=name,
        grid_spec=pltpu.PrefetchScalarGridSpec(
            num_scalar_prefetch=1, grid=(r // tr,),
            in_specs=[pl.BlockSpec((None, tr, D), lambda i, q_ref: (q_ref[0], i, 0)),
                      pl.BlockSpec((3, tr, D), lambda i, q_ref: (0, i, 0))],
            out_specs=pl.BlockSpec((tr, D), lambda i, q_ref: (i, 0))),
        out_shape=jax.ShapeDtypeStruct((r, D), F32), compiler_params=_cparams(),
    )(chip, part, got)


class _WeightGather:
    def __init__(self, groups):
        self.state, token = {}, ()
        for key, names, lands in groups:
            send, recv, lz, tok = _gather_start("gather_start_%s_%d" % key[::-1], lands, token)
            self.state[key] = (names, send, recv, lz)
            token = (tok,)
        self.started = token

    def fetch(self, layer, group, marker):
        names, send, recv, lz = self.state.pop((layer, group))
        tag = "%s_%d" % (group, layer)
        fsend, frecv, lz, tok = _gather_forward("gather_forward_" + tag, lz, recv, marker or self.started)
        lz = _gather_finish("gather_finish_" + tag, lz, send, recv, fsend, frecv, (tok,))
        return dict(zip(names, lz))


class _GradReduce:
    def __init__(self, core, chip):
        self.core, self.chip = core, chip
        self.layer = None
        self.token = ()
        self.at_sibling, self.at_chips = [], []

    def after(self):
        return self.token

    def add(self, group, names, grads):
        tag = "%s_%d" % (group, self.layer)
        send, recv, grads, lands, tok = _sibling_start("grad_sibling_start_" + tag, grads, self.token)
        self.at_sibling.append((tag, [(self.layer, n) for n in names], send, recv, grads, lands))
        self.token = (tok,)

    def advance(self, marker):
        for tag, keys, send, recv, grads, lands in self.at_sibling:
            grads, lands = _sibling_finish("grad_sibling_finish_" + tag, grads, lands, send, recv, marker)
            parts = [_chip_partial("chip_partial_%d_%s" % key, g, got, self.core, 256)
                     for key, g, got in zip(keys, grads, lands)]
            send, recv, parts, lands, tok = _chip_start("grad_chip_start_" + tag, parts, ())
            self.at_chips.append([tag, keys, send, recv, parts, lands])
            self.token = (tok,)
        self.at_sibling = []

    def collect(self, key, marker):
        for entry in self.at_chips:
            tag, keys, send, recv, parts, lands = entry
            if key in keys:
                if send is not None:
                    parts, lands = _chip_finish("grad_chip_finish_" + tag, parts, lands, send, recv, marker)
                    entry[2:] = [None, None, parts, lands]
                i = keys.index(key)
                return _chip_sum("chip_sum_%d_%s" % key, parts[i], lands[i], self.chip, 256)
        raise KeyError(key)


def _adamw_math(w, g, m, v):
    m = ADAM_B1 * m + (1.0 - ADAM_B1) * g
    v = ADAM_B2 * v + (1.0 - ADAM_B2) * jnp.square(g)
    m_hat = m / (1.0 - ADAM_B1 ** ADAM_STEP)
    v_hat = v / (1.0 - ADAM_B2 ** ADAM_STEP)
    delta = -ADAM_LR * (m_hat / (jnp.sqrt(v_hat) + ADAM_EPS) + ADAM_WD * w)
    return delta, m, v


def _adamw_small(wts, mom_m, mom_v, own, gathered, dev):
    names = SMALL
    nw = len(names)
    na = len(SMALL_ARRAYS)

    def body(dev_ref, *refs):
        w_refs, m_refs, v_refs = (dict(zip(names, refs[i * nw:(i + 1) * nw])) for i in range(3))
        own_refs = refs[3 * nw:3 * nw + DEPTH * na]
        g_refs = refs[3 * nw + DEPTH * na:3 * nw + 2 * DEPTH * na]
        outs = refs[3 * nw + 2 * DEPTH * na:]
        g_out, d_out, m_out, v_out = (dict(zip(names, outs[i * nw:(i + 1) * nw])) for i in range(4))
        me = dev_ref[0]

        def update(name, at, g):
            g_out[name][at] = g
            d_out[name][at], m_out[name][at], v_out[name][at] = _adamw_math(
                w_refs[name][at], g, m_refs[name][at], v_refs[name][at])

        for l in range(DEPTH):
            mine = dict(zip(SMALL_ARRAYS, own_refs[l * na:(l + 1) * na]))
            got = dict(zip(SMALL_ARRAYS, g_refs[l * na:(l + 1) * na]))

            def total(key, at):
                acc = None
                for d in range(N_DEV):
                    term = jnp.where(me == d, mine[key][at] if at else mine[key][...], got[key][(d,) + at])
                    acc = term if acc is None else acc + term
                return acc

            row = (slice(l, l + 1),)
            for k, name in enumerate(NORM_NAMES):
                update(name, row, total("norms", (slice(k, k + 1),)))
            for k, name in enumerate(VEC_NAMES):
                update(name, row, total("vecs", (slice(k, k + 1),)))
            update("gmlp_v_gain", (l,), total("gain_bias", (slice(0, NH),)))
            update("b_spatial", (l,), total("gain_bias", (slice(NH, 2 * NH),)))
            update("w_spatial", (l,), total("w_spatial", ()))
            update("w_pool", (l,), total("w_pool", ()))
            update("w_dw", (l,), total("w_dw", (slice(0, CONV_K),)))

    args = [src[n] for src in (wts, mom_m, mom_v) for n in names]
    args += [src[l][k] for src in (own, gathered) for l in range(DEPTH) for k in SMALL_ARRAYS]
    outs = pl.pallas_call(
        body, name="adamw_small",
        in_specs=[pl.BlockSpec(memory_space=pltpu.SMEM)] + [pl.BlockSpec(memory_space=pltpu.VMEM)] * len(args),
        out_shape=[jax.ShapeDtypeStruct(wts[n].shape, F32) for _ in range(4) for n in names],
        compiler_params=_cparams(),
    )(dev, *args)
    return tuple(dict(zip(names, outs[i * nw:(i + 1) * nw])) for i in range(4))


def _adamw_layers(name, w, g0, g1, m, v, tr):
    nl, r, cdim = w.shape
    tr = min(tr, r)
    nb = r // tr

    def body(w_ref, g0_ref, g1_ref, m_ref, v_ref, g_ref, d_ref, nm_ref, nv_ref):
        g = jnp.where(pl.program_id(0) == 0, g0_ref[...], g1_ref[...])
        g_ref[...] = g
        d_ref[...], nm_ref[...], nv_ref[...] = _adamw_math(w_ref[...], g, m_ref[...], v_ref[...])

    blk = pl.BlockSpec((None, tr, cdim), lambda l, i: (l, i, 0))
    g0_spec = pl.BlockSpec((tr, cdim), lambda l, i: (i * (1 - l) + (nb - 1) * l, 0))
    g1_spec = pl.BlockSpec((tr, cdim), lambda l, i: (i * l, 0))
    shape = jax.ShapeDtypeStruct((nl, r, cdim), F32)
    return pl.pallas_call(
        body, name=name, grid=(nl, nb), in_specs=[blk, g0_spec, g1_spec, blk, blk], out_specs=[blk] * 4,
        out_shape=[shape] * 4, compiler_params=_cparams(),
    )(w, g0, g1, m, v)


def _to_gather_layout(name, w):
    if name in ("w_in", "w_up"):
        w = w.T
    return w.astype(BF16)


def _from_gather_layout(name, g):
    return g.T if name in ("w_in", "w_up") else g


def _pack(arrays, rows):
    flat = jnp.concatenate([a.reshape(-1) for a in arrays])
    return jnp.pad(flat, (0, rows * D - flat.shape[0])).reshape(rows, D)


def _rows_for(shapes, mult=8):
    total = 0
    for shp in shapes:
        size = 1
        for dim in shp:
            size *= dim
        total += size
    return -(-total // (mult * D)) * mult


def kernel(x, mem, norm_mix_pre, norm_mix_post, w_in, w_out, gmlp_v_gain, w_spatial, b_spatial, w_pool, s_pool, w_dw, b_dw, conv_ln_g, conv_ln_b, norm_xattn_pre, norm_mem, norm_xattn_post, w_q, w_k, w_v, w_o, norm_ffn_pre, norm_ffn_post, w_up, w_down, loss_target, m_norm_mix_pre, m_norm_mix_post, m_w_in, m_w_out, m_gmlp_v_gain, m_w_spatial, m_b_spatial, m_w_pool, m_s_pool, m_w_dw, m_b_dw, m_conv_ln_g, m_conv_ln_b, m_norm_xattn_pre, m_norm_mem, m_norm_xattn_post, m_w_q, m_w_k, m_w_v, m_w_o, m_norm_ffn_pre, m_norm_ffn_post, m_w_up, m_w_down, v_norm_mix_pre, v_norm_mix_post, v_w_in, v_w_out, v_gmlp_v_gain, v_w_spatial, v_b_spatial, v_w_pool, v_s_pool, v_w_dw, v_b_dw, v_conv_ln_g, v_conv_ln_b, v_norm_xattn_pre, v_norm_mem, v_norm_xattn_post, v_w_q, v_w_k, v_w_v, v_w_o, v_norm_ffn_pre, v_norm_ffn_post, v_w_up, v_w_down):
    args = dict(locals())
    wts = {n: args[n] for n in WEIGHTS}
    mom_m = {n: args["m_" + n] for n in WEIGHTS}
    mom_v = {n: args["v_" + n] for n in WEIGHTS}
    xi, yi, ci = _position()
    me = 4 * xi + 2 * yi + ci

    dev = jnp.reshape(me, (1,)).astype(jnp.int32)
    lands = {}
    for call, names, tr in (("place_att", ("w_out", "w_q", "w_k", "w_v", "w_o"), 64), ("place_down", ("w_down",), 256),
                            ("place_up", ("w_up",), 256), ("place_in", ("w_in",), 256)):
        srcs = [(jnp.swapaxes(wts[n], 1, 2) if n in ("w_in", "w_up") else wts[n], l) for l in range(DEPTH) for n in names]
        placed = _place_own(call, srcs, dev, BF16, tr)
        lands.update(zip([(l, n) for l in range(DEPTH) for n in names], placed))
    (lands[(0, "taps")],) = _place_own("place_taps", [(_pack([w_dw], _rows_for([w_dw.shape])), None)], dev, F32, 8)
    groups = []
    for l in range(DEPTH):
        for group, names in GATHER_GROUPS:
            if (l, group) == (0, "in"):
                names = names + ("taps",)
            groups.append(((l, group), names, [lands[(l, n)] for n in names]))
    gather = _WeightGather(groups)

    def fetch(layer, group, marker):
        w = gather.fetch(layer, group, marker)
        if "taps" in w:
            blocks = w["taps"].reshape(N_DEV, -1)[:, :w_dw.size].reshape((N_DEV,) + w_dw.shape)
            w["taps"] = jnp.moveaxis(blocks, 0, 2).reshape(DEPTH, CONV_K, CW)
        return w

    reduce = _GradReduce(jnp.reshape(ci, (1,)).astype(jnp.int32), jnp.reshape(2 * xi + yi, (1,)).astype(jnp.int32))
    small = {n: wts[n] for n in SMALL if n != "w_dw"}
    loss, dx, gsmall = _local_step(x[0], mem[0], loss_target[0], fetch, small, reduce)
    reduce.advance((dx,))

    na = len(SMALL_ARRAYS)
    ssend, srecv, own, slots, token = _broadcast_start(
        "small_grads_start", [gsmall[l][k] for l in range(DEPTH) for k in SMALL_ARRAYS], (dx,))

    grad_w, delta, new_m, new_v = {}, {}, {}, {}
    marker = (token,)
    for n in UPDATE_ORDER:
        g0, g1 = (_from_gather_layout(n, reduce.collect((l, n), marker)) for l in range(DEPTH))
        grad_w[n], delta[n], new_m[n], new_v[n] = _adamw_layers("adamw_" + n, wts[n], g0, g1, mom_m[n], mom_v[n], 256)
        marker = (delta[n],)

    own, slots = _broadcast_finish("small_grads_finish", own, slots, ssend, srecv, marker)
    own = [dict(zip(SMALL_ARRAYS, own[l * na:(l + 1) * na])) for l in range(DEPTH)]
    slots = [dict(zip(SMALL_ARRAYS, slots[l * na:(l + 1) * na])) for l in range(DEPTH)]
    shard_cols = CW // N_DEV
    for l in range(DEPTH):
        own[l]["w_dw"] = lax.dynamic_slice_in_dim(own[l]["w_dw"], me * shard_cols, shard_cols, axis=1)
        slots[l]["w_dw"] = lax.dynamic_slice_in_dim(slots[l]["w_dw"], me * shard_cols, shard_cols, axis=2)
    small_out = _adamw_small(wts, mom_m, mom_v, own, slots, dev)
    for dst, src in zip((grad_w, delta, new_m, new_v), small_out):
        dst.update(src)

    loss = lax.psum(loss, ("x", "y", "c"))
    return (loss, dx[None], *[grad_w[n] for n in WEIGHTS], *[delta[n] for n in WEIGHTS],
            *[new_m[n] for n in WEIGHTS], *[new_v[n] for n in WEIGHTS])
```

```python
import functools

import jax
import jax.numpy as jnp
from jax import lax
from jax.experimental import pallas as pl
from jax.experimental.pallas import tpu as pltpu

F32 = jnp.float32
BF16 = jnp.bfloat16

D = 2048
GW = 1024
PW = 512
CW = 512
HD = 128
NH = 8
NG = 4
POOL_WINDOWS = (2, 4, 8, 16)
CONV_K = 31
IN_COLS = 2 * GW + PW + 2 * CW
DFF = 4 * D
XH = 4
XHD = D // XH
ATT_SCALE = XHD ** -0.5
RMS_EPS = 1e-6
LN_EPS = 1e-5
DEPTH = 2
N_DEV = 8

ADAM_LR = 0.001
ADAM_B1 = 0.9
ADAM_B2 = 0.999
ADAM_EPS = 1e-08
ADAM_WD = 0.01
ADAM_STEP = 10

LANES = 128
CONV_HALO = 32
POOL_HALO = 16
ROW_TILE = 128
VMEM_LIMIT = 60 * 1024 * 1024

MESH = pl.DeviceIdType.MESH
NT = (((1,), (1,)), ((), ()))
NN = (((1,), (0,)), ((), ()))
TN = (((0,), (0,)), ((), ()))

BIG = ("w_out", "w_q", "w_k", "w_v", "w_o", "w_up", "w_down", "w_in")
UPDATE_ORDER = ("w_down", "w_up", "w_o", "w_q", "w_k", "w_v", "w_out", "w_in")
GATHER_GROUPS = (("in", ("w_in",)), ("att", ("w_out", "w_q", "w_k", "w_v", "w_o")), ("ffn", ("w_up", "w_down")))
SMALL = ("norm_mix_pre", "norm_mix_post", "gmlp_v_gain", "w_spatial", "b_spatial", "w_pool", "s_pool",
         "w_dw", "b_dw", "conv_ln_g", "conv_ln_b", "norm_xattn_pre", "norm_mem", "norm_xattn_post",
         "norm_ffn_pre", "norm_ffn_post")
WEIGHTS = ("norm_mix_pre", "norm_mix_post", "w_in", "w_out", "gmlp_v_gain", "w_spatial", "b_spatial", "w_pool",
           "s_pool", "w_dw", "b_dw", "conv_ln_g", "conv_ln_b", "norm_xattn_pre", "norm_mem", "norm_xattn_post",
           "w_q", "w_k", "w_v", "w_o", "norm_ffn_pre", "norm_ffn_post", "w_up", "w_down")


def _cparams():
    return pltpu.CompilerParams(vmem_limit_bytes=VMEM_LIMIT)


def _dot(a, b, dims):
    return lax.dot_general(a, b, dims, preferred_element_type=F32)


def _rms(x, g):
    y = x * lax.rsqrt(jnp.mean(x * x, axis=-1, keepdims=True) + RMS_EPS)
    return y * g


def _gelu(x):
    cdf = 0.5 * (1.0 + jnp.tanh(0.7978845608028654 * (x + 0.044715 * (x * x * x))))
    return x * cdf


def _layer_norm(x, g, b=None):
    mu = jnp.mean(x, axis=-1, keepdims=True)
    xc = x - mu
    var = jnp.mean(xc * xc, axis=-1, keepdims=True)
    y = xc * lax.rsqrt(var + LN_EPS) * g
    return y if b is None else y + b


def _sigmoid(x):
    return 1.0 / (1.0 + jnp.exp(-x))


def _gmlp_rows(zu, zv, gv):
    return _gelu(zu), _layer_norm(_gelu(zv), gv)


def _glu(cv, cg):
    return cv * _sigmoid(cg)


def _ln_silu(h, g, b):
    y = _layer_norm(h, g, b)
    return y * _sigmoid(y)


ANY = pl.BlockSpec(memory_space=pl.ANY)


ROWS_TILE = 256
COLS_TILE = 512
DW_TILE = 512
RESIDENT_K = 2048
STREAM_K_TILE = 1024
STREAM_ROWS = 512


def _k_tiles(kdim):
    if kdim <= RESIDENT_K:
        return ROWS_TILE, kdim
    return STREAM_ROWS, max(t for t in range(LANES, STREAM_K_TILE + 1, LANES) if kdim % t == 0)


def _rowop_mm(name, kind, rows, g, w, dims, out_dtype, u=None, after=()):
    s = rows[0].shape[0]
    n = w.shape[0] if dims == NT else w.shape[1]
    tm, tn = min(ROWS_TILE, s), min(COLS_TILE, n)
    ni = s // tm
    bwd = kind == "rms_bwd"

    def rows_body(*refs):
        refs = list(refs)
        row_refs = [refs.pop(0) for _ in rows]
        g_ref = refs.pop(0)
        del refs[:len(after)]
        if bwd:
            _, vjp = jax.vjp(_rms, row_refs[0][...], g_ref[...])
            a, dg = vjp(row_refs[1][...])
            refs[1][0] = dg
        else:
            a = _rms(row_refs[0][...], g_ref[...])
        refs[0][...] = a.astype(BF16)

    row_spec = pl.BlockSpec((tm, D), lambda i: (i, 0))
    res = pl.pallas_call(
        rows_body, name=name + "_rows", grid=(ni,),
        in_specs=[row_spec] * len(rows) + [pl.BlockSpec((1, D), lambda i: (0, 0))] + [ANY] * len(after),
        out_specs=[row_spec] + ([pl.BlockSpec((1, 1, D), lambda i: (i, 0, 0))] if bwd else []),
        out_shape=[jax.ShapeDtypeStruct((s, D), BF16)] + ([jax.ShapeDtypeStruct((ni, 1, D), F32)] if bwd else []),
        compiler_params=_cparams(),
    )(*rows, g, *after)
    a = res[0]

    def body(a_ref, w_ref, *rest):
        acc = _dot(a_ref[...], w_ref[...], dims)
        if u is not None:
            acc = acc * (2.0 * jnp.maximum(rest[0][...], 0.0))
        rest[-1][...] = acc.astype(out_dtype)

    w_spec = pl.BlockSpec((tn, D), lambda j: (j, 0)) if dims == NT else pl.BlockSpec((D, tn), lambda j: (0, j))
    tile = pl.BlockSpec((s, tn), lambda j: (0, j))
    out = pl.pallas_call(
        body, name=name, grid=(n // tn,),
        in_specs=[pl.BlockSpec((s, D), lambda j: (0, 0)), w_spec] + ([tile] if u is not None else []),
        out_specs=tile, out_shape=jax.ShapeDtypeStruct((s, n), out_dtype), compiler_params=_cparams(),
    )(a, w, *([u] if u is not None else []))
    return (out, *res)


def _mm_rowop(name, kind, pairs, rows, g, relu2=False, after=()):
    s, kdim = pairs[0][0].shape
    tm, tk = _k_tiles(kdim)
    tm = min(tm, s)
    ni, nk = s // tm, kdim // tk
    npair = len(pairs)

    def body(*refs):
        refs = list(refs)
        a_refs = [refs.pop(0) for _ in range(npair)]
        w_refs = [refs.pop(0) for _ in range(npair)]
        row_refs = [refs.pop(0) for _ in rows]
        g_ref = refs.pop(0)
        del refs[:len(after)]
        acc = refs.pop()
        outs = refs
        k = pl.program_id(1)

        @pl.when(k == 0)
        def _():
            acc[...] = jnp.zeros_like(acc)

        for a_ref, w_ref, (_, _, dims) in zip(a_refs, w_refs, pairs):
            a = a_ref[...]
            if relu2:
                a = jnp.square(jnp.maximum(a, 0.0))
            acc[...] += _dot(a.astype(BF16), w_ref[...], dims)

        @pl.when(k == nk - 1)
        def _():
            h = acc[...]
            if kind == "rms_res":
                outs[0][...] = row_refs[0][...] + _rms(h, g_ref[...])
                outs[1][...] = h
            else:
                _, vjp = jax.vjp(_rms, row_refs[0][...], g_ref[...])
                dx, dg = vjp(h)
                if kind == "rms_bwd_res":
                    outs[0][...] = row_refs[1][...] + dx
                    outs[1][0] = dg
                else:
                    outs[0][0] = dg

    row_spec = pl.BlockSpec((tm, D), lambda i, k: (i, 0))
    dg_shape = jax.ShapeDtypeStruct((ni, 1, D), F32)
    dg_spec = pl.BlockSpec((1, 1, D), lambda i, k: (i, 0, 0))
    in_specs = [pl.BlockSpec((tm, tk), lambda i, k: (i, k))] * npair
    for _, _, dims in pairs:
        in_specs.append(pl.BlockSpec((tk, D), lambda i, k: (k, 0)) if dims == NN
                        else pl.BlockSpec((D, tk), lambda i, k: (0, k)))
    in_specs += [row_spec] * len(rows) + [pl.BlockSpec((1, D), lambda i, k: (0, 0))] + [ANY] * len(after)
    if kind == "rms_res":
        out_shape = [jax.ShapeDtypeStruct((s, D), F32)] * 2
        out_specs = [row_spec, row_spec]
    elif kind == "rms_bwd_res":
        out_shape = [jax.ShapeDtypeStruct((s, D), F32), dg_shape]
        out_specs = [row_spec, dg_spec]
    else:
        out_shape = [dg_shape]
        out_specs = [dg_spec]
    return pl.pallas_call(
        body, name=name, grid=(ni, nk), in_specs=in_specs, out_specs=out_specs, out_shape=out_shape,
        scratch_shapes=[pltpu.VMEM((tm, D), F32)], compiler_params=_cparams(),
    )(*[p[0] for p in pairs], *[p[1] for p in pairs], *rows, g, *after)


def _mm_tn(name, a, gmat, relu2=False, after=()):
    s, m = a.shape
    tm, ts = min(DW_TILE, m), s
    ni, ns = m // tm, s // ts

    def body(a_ref, g_ref, *rest):
        o_ref, acc = rest[len(after):]
        k = pl.program_id(1)

        @pl.when(k == 0)
        def _():
            acc[...] = jnp.zeros_like(acc)

        av = a_ref[...]
        if relu2:
            av = jnp.square(jnp.maximum(av, 0.0))
        acc[...] += _dot(av.astype(BF16), g_ref[...], TN)

        @pl.when(k == ns - 1)
        def _():
            o_ref[...] = acc[...].astype(BF16)

    return pl.pallas_call(
        body, name=name, grid=(ni, ns),
        in_specs=[pl.BlockSpec((ts, tm), lambda i, k: (k, i)), pl.BlockSpec((ts, D), lambda i, k: (k, 0))]
        + [ANY] * len(after),
        out_specs=pl.BlockSpec((tm, D), lambda i, k: (i, 0)),
        out_shape=jax.ShapeDtypeStruct((m, D), BF16),
        scratch_shapes=[pltpu.VMEM((tm, D), F32)], compiler_params=_cparams(),
    )(a, gmat, *after)


def _tril():
    r = lax.broadcasted_iota(jnp.int32, (HD, HD), 0)
    c = lax.broadcasted_iota(jnp.int32, (HD, HD), 1)
    return (c <= r).astype(F32)


def _gmlp_fwd(z, gv, ws, bst, tb):
    s = z.shape[0]
    tb = min(tb, s)

    def body(zu_ref, zv_ref, gv_ref, ws_ref, bst_ref, y_ref):
        tril = _tril()
        for h in range(NH):
            cols = slice(h * HD, (h + 1) * HD)
            u, vln = _gmlp_rows(zu_ref[:, cols], zv_ref[:, cols], gv_ref[h:h + 1, :])
            wm = (ws_ref[h] * tril).astype(BF16)
            vb = vln.astype(BF16)
            for c in range(tb // HD):
                rws = slice(c * HD, (c + 1) * HD)
                mixed = _dot(wm, vb[rws], NN) + bst_ref[:, h:h + 1]
                y_ref[rws, cols] = (u[rws] * mixed).astype(BF16)

    return pl.pallas_call(
        body, name="gmlp_fwd", grid=(s // tb,),
        in_specs=[pl.BlockSpec((tb, GW), lambda i: (i, 0)), pl.BlockSpec((tb, GW), lambda i: (i, 1)),
                  pl.BlockSpec((NH, HD), lambda i: (0, 0)), pl.BlockSpec((NH, HD, HD), lambda i: (0, 0, 0)),
                  pl.BlockSpec((HD, NH), lambda i: (0, 0))],
        out_specs=pl.BlockSpec((tb, GW), lambda i: (i, 0)),
        out_shape=jax.ShapeDtypeStruct((s, D), BF16), compiler_params=_cparams(),
    )(z, z, gv, ws, bst)


def _gmlp_bwd(z, dy, gv, ws, bst, tb):
    s = z.shape[0]
    tb = min(tb, s)
    nb = s // tb

    def body(zu_ref, zv_ref, dy_ref, gv_ref, ws_ref, bst_ref, dz_ref, dgv_ref, dws_ref, db_ref):
        tril = _tril()
        for h in range(NH):
            cols = slice(h * HD, (h + 1) * HD)
            (u, vln), vjp = jax.vjp(_gmlp_rows, zu_ref[:, cols], zv_ref[:, cols], gv_ref[h:h + 1, :])
            wmf = ws_ref[h] * tril
            wm = wmf.astype(BF16)
            wmt = wmf.T.astype(BF16)
            vb = vln.astype(BF16)
            dws = jnp.zeros((HD, HD), F32)
            db = jnp.zeros((HD, 1), F32)
            du_parts, dvln_parts = [], []
            for c in range(tb // HD):
                rws = slice(c * HD, (c + 1) * HD)
                mixed = _dot(wm, vb[rws], NN) + bst_ref[:, h:h + 1]
                dyc = dy_ref[rws, cols]
                du_parts.append(dyc * mixed)
                dmixed = dyc * u[rws]
                dmb = dmixed.astype(BF16)
                dws = dws + _dot(dmb, vb[rws], NT)
                db = db + jnp.sum(dmixed, axis=1, keepdims=True)
                dvln_parts.append(_dot(wmt, dmb, NN))
            du = jnp.concatenate(du_parts, axis=0)
            dvln = jnp.concatenate(dvln_parts, axis=0)
            dzu, dzv, dgv = vjp((du, dvln))
            dz_ref[:, cols] = dzu.astype(BF16)
            dz_ref[:, slice(GW + h * HD, GW + (h + 1) * HD)] = dzv.astype(BF16)
            dgv_ref[0, h:h + 1, :] = dgv
            dws_ref[0, h] = dws * tril
            db_ref[0, h] = jnp.broadcast_to(db, (HD, LANES))

    blk = pl.BlockSpec((tb, GW), lambda i: (i, 0))
    return pl.pallas_call(
        body, name="gmlp_bwd", grid=(nb,),
        in_specs=[blk, pl.BlockSpec((tb, GW), lambda i: (i, 1)), blk,
                  pl.BlockSpec((NH, HD), lambda i: (0, 0)), pl.BlockSpec((NH, HD, HD), lambda i: (0, 0, 0)),
                  pl.BlockSpec((HD, NH), lambda i: (0, 0))],
        out_specs=[pl.BlockSpec((tb, 2 * GW), lambda i: (i, 0)), pl.BlockSpec((1, NH, HD), lambda i: (i, 0, 0)),
                   pl.BlockSpec((1, NH, HD, HD), lambda i: (i, 0, 0, 0)),
                   pl.BlockSpec((1, NH, HD, LANES), lambda i: (i, 0, 0, 0))],
        out_shape=[jax.ShapeDtypeStruct((s, IN_COLS), BF16),
                   jax.ShapeDtypeStruct((nb, NH, HD), F32), jax.ShapeDtypeStruct((nb, NH, HD, HD), F32),
                   jax.ShapeDtypeStruct((nb, NH, HD, LANES), F32)],
        compiler_params=_cparams(),
    )(z, z, dy, gv, ws, bst)


def _pool_count(t0, window):
    pos = (t0 + lax.broadcasted_iota(jnp.int32, (ROW_TILE, LANES), 0)).astype(F32)
    return jnp.minimum(pos + 1.0, float(window))


def _window_sum(win, levels, back):
    n = win.shape[0]
    for lv in range(levels):
        step = 1 << lv
        win = win + pltpu.roll(win, n - step if back else step, 0)
    return win


def _pool_pooled(ppad_ref, t0, g):
    win = ppad_ref[pl.ds(t0, ROW_TILE + POOL_HALO), :]
    wsum = _window_sum(win, g + 1, False)[POOL_HALO:]
    return wsum / _pool_count(t0, POOL_WINDOWS[g]) - win[POOL_HALO:]


def _pool_fwd(z, wp, sp, y):
    s = z.shape[0]
    nt = s // ROW_TILE

    def body(p_ref, wp_ref, sp_ref, _, y_ref, ppad):
        for g in range(NG):
            cols = slice(g * LANES, (g + 1) * LANES)
            ppad[pl.ds(0, POOL_HALO), :] = jnp.zeros((POOL_HALO, LANES), F32)
            ppad[pl.ds(POOL_HALO, s), :] = p_ref[:, cols]
            wpb = wp_ref[g].astype(BF16)
            scale = sp_ref[:, cols]

            def tile(t, carry):
                t0 = pl.multiple_of(t * ROW_TILE, ROW_TILE)
                pooled = _pool_pooled(ppad, t0, g)
                y_ref[pl.ds(t0, ROW_TILE), cols] = (_dot(pooled.astype(BF16), wpb, NN) * scale).astype(BF16)
                return carry

            lax.fori_loop(0, nt, tile, 0)

    return pl.pallas_call(
        body, name="pool_fwd", grid=(1,),
        in_specs=[pl.BlockSpec((s, PW), lambda i: (0, 2 * GW // PW)),
                  pl.BlockSpec((NG, LANES, LANES), lambda i: (0, 0, 0)), pl.BlockSpec((1, PW), lambda i: (0, 0)), ANY],
        out_specs=pl.BlockSpec((s, PW), lambda i: (0, GW // PW)),
        out_shape=jax.ShapeDtypeStruct((s, D), BF16), input_output_aliases={3: 0},
        scratch_shapes=[pltpu.VMEM((s + POOL_HALO, LANES), F32)], compiler_params=_cparams(),
    )(z, wp, sp, y)


def _pool_bwd(z, dy, wp, sp, dz):
    s = z.shape[0]
    nt = s // ROW_TILE

    def body(p_ref, dy_ref, wp_ref, sp_ref, _, dp_ref, dwp_ref, dsp_ref, ppad, rpad, dpool):
        for g in range(NG):
            cols = slice(g * LANES, (g + 1) * LANES)
            ppad[pl.ds(0, POOL_HALO), :] = jnp.zeros((POOL_HALO, LANES), F32)
            ppad[pl.ds(POOL_HALO, s), :] = p_ref[:, cols]
            rpad[pl.ds(s, POOL_HALO), :] = jnp.zeros((POOL_HALO, LANES), F32)
            wpb = wp_ref[g].astype(BF16)
            scale = sp_ref[:, cols]

            def tile(t, carry):
                dwp, dsp = carry
                t0 = pl.multiple_of(t * ROW_TILE, ROW_TILE)
                pooled = _pool_pooled(ppad, t0, g)
                pb = pooled.astype(BF16)
                dyt = dy_ref[pl.ds(t0, ROW_TILE), cols]
                dsp = dsp + jnp.sum(dyt * _dot(pb, wpb, NN), axis=0, keepdims=True)
                dmm = (dyt * scale).astype(BF16)
                dwp = dwp + _dot(pb, dmm, TN)
                dpooled = _dot(dmm, wpb, NT)
                rpad[pl.ds(t0, ROW_TILE), :] = dpooled / _pool_count(t0, POOL_WINDOWS[g])
                dpool[pl.ds(t0, ROW_TILE), :] = dpooled
                return dwp, dsp

            dwp, dsp = lax.fori_loop(0, nt, tile, (jnp.zeros((LANES, LANES), F32), jnp.zeros((1, LANES), F32)))
            dwp_ref[g] = dwp
            dsp_ref[:, cols] = dsp

            def tile2(t, carry):
                t0 = pl.multiple_of(t * ROW_TILE, ROW_TILE)
                win = rpad[pl.ds(t0, ROW_TILE + POOL_HALO), :]
                back = _window_sum(win, g + 1, True)[:ROW_TILE]
                rows = pl.ds(t0, ROW_TILE)
                dp_ref[rows, cols] = (back - dpool[rows, :]).astype(BF16)
                return carry

            lax.fori_loop(0, nt, tile2, 0)

    return pl.pallas_call(
        body, name="pool_bwd", grid=(1,),
        in_specs=[pl.BlockSpec((s, PW), lambda i: (0, 2 * GW // PW)), pl.BlockSpec((s, PW), lambda i: (0, GW // PW)),
                  pl.BlockSpec((NG, LANES, LANES), lambda i: (0, 0, 0)), pl.BlockSpec((1, PW), lambda i: (0, 0)), ANY],
        out_specs=[pl.BlockSpec((s, PW), lambda i: (0, 2 * GW // PW)),
                   pl.BlockSpec((NG, LANES, LANES), lambda i: (0, 0, 0)), pl.BlockSpec((1, PW), lambda i: (0, 0))],
        out_shape=[jax.ShapeDtypeStruct((s, IN_COLS), BF16), jax.ShapeDtypeStruct((NG, LANES, LANES), F32),
                   jax.ShapeDtypeStruct((1, PW), F32)],
        input_output_aliases={4: 0},
        scratch_shapes=[pltpu.VMEM((s + POOL_HALO, LANES), F32), pltpu.VMEM((s + POOL_HALO, LANES), F32),
                        pltpu.VMEM((s, LANES), F32)],
        compiler_params=_cparams(),
    )(z, dy, wp, sp, dz)


CONV_LEAD = CONV_HALO - (CONV_K - 1)


def _conv_taps(win, wdw_ref, lead, reverse):
    n = win.shape[0]
    acc = jnp.zeros((ROW_TILE, CW), F32)
    for j in range(CONV_K):
        tap = (CONV_K - 1 - j) if reverse else j
        acc = acc + wdw_ref[tap:tap + 1, :] * pltpu.roll(win, (n - (lead + j)) % n, 0)[:ROW_TILE]
    return acc


def _conv_fill_glu(cv_ref, cg_ref, xpad, s):
    xpad[pl.ds(0, CONV_HALO), :] = jnp.zeros((CONV_HALO, CW), F32)

    def fill(t, carry):
        t0 = pl.multiple_of(t * ROW_TILE, ROW_TILE)
        rows = pl.ds(t0, ROW_TILE)
        xpad[pl.ds(t0 + CONV_HALO, ROW_TILE), :] = _glu(cv_ref[rows, :], cg_ref[rows, :])
        return carry

    lax.fori_loop(0, s // ROW_TILE, fill, 0)


def _conv_fwd(z, wdw, bdw, lng, lnb, y):
    s = z.shape[0]

    def body(cv_ref, cg_ref, wdw_ref, bdw_ref, lng_ref, lnb_ref, _, y_ref, xpad):
        _conv_fill_glu(cv_ref, cg_ref, xpad, s)

        def tile(t, carry):
            t0 = pl.multiple_of(t * ROW_TILE, ROW_TILE)
            win = xpad[pl.ds(t0, ROW_TILE + CONV_HALO), :]
            hc = _conv_taps(win, wdw_ref, CONV_LEAD, False) + bdw_ref[...]
            y_ref[pl.ds(t0, ROW_TILE), :] = _ln_silu(hc, lng_ref[...], lnb_ref[...]).astype(BF16)
            return carry

        lax.fori_loop(0, s // ROW_TILE, tile, 0)

    vec = pl.BlockSpec((1, CW), lambda i: (0, 0))
    return pl.pallas_call(
        body, name="conv_fwd", grid=(1,),
        in_specs=[pl.BlockSpec((s, CW), lambda i: (0, (2 * GW + PW) // CW)),
                  pl.BlockSpec((s, CW), lambda i: (0, (2 * GW + PW) // CW + 1)),
                  pl.BlockSpec((CONV_K + 1, CW), lambda i: (0, 0)), vec, vec, vec, ANY],
        out_specs=pl.BlockSpec((s, CW), lambda i: (0, (GW + PW) // CW)),
        out_shape=jax.ShapeDtypeStruct((s, D), BF16), input_output_aliases={6: 0},
        scratch_shapes=[pltpu.VMEM((s + CONV_HALO, CW), F32)], compiler_params=_cparams(),
    )(z, z, wdw, bdw, lng, lnb, y)


def _conv_bwd(z, dy, wdw, bdw, lng, lnb, dz):
    s = z.shape[0]

    def body(cv_ref, cg_ref, dy_ref, wdw_ref, bdw_ref, lng_ref, lnb_ref, _,
             dz_ref, dwdw_ref, dbdw_ref, dlng_ref, dlnb_ref, xpad, dpad, dcg_keep):
        @pl.when(pl.program_id(0) == 0)
        def _():
            compute(cv_ref, cg_ref, dy_ref, wdw_ref, bdw_ref, lng_ref, lnb_ref,
                    dz_ref, dcg_keep, dwdw_ref, dbdw_ref, dlng_ref, dlnb_ref, xpad, dpad)

        @pl.when(pl.program_id(0) == 1)
        def _():
            dz_ref[...] = dcg_keep[...]

    def compute(cv_ref, cg_ref, dy_ref, wdw_ref, bdw_ref, lng_ref, lnb_ref,
                dcv_ref, dcg_ref, dwdw_ref, dbdw_ref, dlng_ref, dlnb_ref, xpad, dpad):
        _conv_fill_glu(cv_ref, cg_ref, xpad, s)
        dpad[pl.ds(s, CONV_HALO), :] = jnp.zeros((CONV_HALO, CW), F32)
        dwdw_ref[...] = jnp.zeros((CONV_K + 1, CW), F32)

        def tile(t, carry):
            db, dg, dbeta = carry
            t0 = pl.multiple_of(t * ROW_TILE, ROW_TILE)
            win = xpad[pl.ds(t0, ROW_TILE + CONV_HALO), :]
            hc = _conv_taps(win, wdw_ref, CONV_LEAD, False) + bdw_ref[...]
            _, vjp = jax.vjp(_ln_silu, hc, lng_ref[...], lnb_ref[...])
            dhc, dg_t, dbeta_t = vjp(dy_ref[pl.ds(t0, ROW_TILE), :])
            dpad[pl.ds(t0, ROW_TILE), :] = dhc
            n = win.shape[0]
            for j in range(CONV_K):
                shifted = pltpu.roll(win, (n - (CONV_LEAD + j)) % n, 0)[:ROW_TILE]
                dwdw_ref[j:j + 1, :] += jnp.sum(dhc * shifted, axis=0, keepdims=True)
            return db + jnp.sum(dhc, axis=0, keepdims=True), dg + dg_t, dbeta + dbeta_t

        zero = jnp.zeros((1, CW), F32)
        db, dg, dbeta = lax.fori_loop(0, s // ROW_TILE, tile, (zero, zero, zero))
        dbdw_ref[...] = db
        dlng_ref[...] = dg
        dlnb_ref[...] = dbeta

        def tile2(t, carry):
            t0 = pl.multiple_of(t * ROW_TILE, ROW_TILE)
            rows = pl.ds(t0, ROW_TILE)
            win = dpad[pl.ds(t0, ROW_TILE + CONV_HALO), :]
            dglu = _conv_taps(win, wdw_ref, 0, True)
            _, vjp = jax.vjp(_glu, cv_ref[rows, :], cg_ref[rows, :])
            dcv, dcg = vjp(dglu)
            dcv_ref[rows, :] = dcv.astype(BF16)
            dcg_ref[rows, :] = dcg.astype(BF16)
            return carry

        lax.fori_loop(0, s // ROW_TILE, tile2, 0)

    vec = pl.BlockSpec((1, CW), lambda i: (0, 0))
    wspec = pl.BlockSpec((CONV_K + 1, CW), lambda i: (0, 0))
    vshape = jax.ShapeDtypeStruct((1, CW), F32)
    return pl.pallas_call(
        body, name="conv_bwd", grid=(2,),
        in_specs=[pl.BlockSpec((s, CW), lambda i: (0, (2 * GW + PW) // CW)),
                  pl.BlockSpec((s, CW), lambda i: (0, (2 * GW + PW) // CW + 1)),
                  pl.BlockSpec((s, CW), lambda i: (0, (GW + PW) // CW)), wspec, vec, vec, vec, ANY],
        out_specs=[pl.BlockSpec((s, CW), lambda i: (0, (2 * GW + PW) // CW + i)), wspec, vec, vec, vec],
        out_shape=[jax.ShapeDtypeStruct((s, IN_COLS), BF16), jax.ShapeDtypeStruct((CONV_K + 1, CW), F32),
                   vshape, vshape, vshape],
        input_output_aliases={7: 0},
        scratch_shapes=[pltpu.VMEM((s + CONV_HALO, CW), F32), pltpu.VMEM((s + CONV_HALO, CW), F32),
                        pltpu.VMEM((s, CW), BF16)],
        compiler_params=_cparams(),
    )(z, z, dy, wdw, bdw, lng, lnb, dz)


def _softmax_rows(sc):
    e = jnp.exp(sc - jnp.max(sc, axis=-1, keepdims=True))
    return e / jnp.sum(e, axis=-1, keepdims=True)


def _attn_fwd(q, k, v, tq):
    s, m = q.shape[0], k.shape[0]
    tq = min(tq, s)

    def body(q_ref, k_ref, v_ref, o_ref):
        for h in range(XH):
            cols = slice(h * XHD, (h + 1) * XHD)
            p = _softmax_rows(_dot(q_ref[:, cols], k_ref[:, cols], NT) * ATT_SCALE)
            o_ref[:, cols] = _dot(p.astype(BF16), v_ref[:, cols], NN).astype(BF16)

    kv = pl.BlockSpec((m, D), lambda i: (0, 0))
    return pl.pallas_call(
        body, name="attn_fwd", grid=(s // tq,),
        in_specs=[pl.BlockSpec((tq, D), lambda i: (i, 0)), kv, kv],
        out_specs=pl.BlockSpec((tq, D), lambda i: (i, 0)),
        out_shape=jax.ShapeDtypeStruct((s, D), BF16), compiler_params=_cparams(),
    )(q, k, v)


def _attn_bwd(q, k, v, do, tq):
    s, m = q.shape[0], k.shape[0]
    tq = min(tq, s)

    def body(q_ref, k_ref, v_ref, do_ref, dq_ref, dk_ref, dv_ref):
        @pl.when(pl.program_id(0) == 0)
        def _():
            dk_ref[...] = jnp.zeros_like(dk_ref)
            dv_ref[...] = jnp.zeros_like(dv_ref)

        for h in range(XH):
            cols = slice(h * XHD, (h + 1) * XHD)
            qh, kh, vh, doh = q_ref[:, cols], k_ref[:, cols], v_ref[:, cols], do_ref[:, cols]
            p = _softmax_rows(_dot(qh, kh, NT) * ATT_SCALE)
            dp = _dot(doh, vh, NT)
            dv_ref[:, cols] += _dot(p.astype(BF16), doh, TN)
            ds = (p * (dp - jnp.sum(p * dp, axis=-1, keepdims=True)) * ATT_SCALE).astype(BF16)
            dq_ref[:, cols] = _dot(ds, kh, NN).astype(BF16)
            dk_ref[:, cols] += _dot(ds, qh, TN)

    kv = pl.BlockSpec((m, D), lambda i: (0, 0))
    qs = pl.BlockSpec((tq, D), lambda i: (i, 0))
    return pl.pallas_call(
        body, name="attn_bwd", grid=(s // tq,),
        in_specs=[qs, kv, kv, qs], out_specs=[qs, kv, kv],
        out_shape=[jax.ShapeDtypeStruct((s, D), BF16), jax.ShapeDtypeStruct((m, D), F32),
                   jax.ShapeDtypeStruct((m, D), F32)],
        compiler_params=_cparams(),
    )(q, k, v, do)


def _loss_head(y, target, tm):
    s = y.shape[0]
    tm = min(tm, s)

    def body(y_ref, t_ref, dy_ref, part_ref):
        err = y_ref[...] - t_ref[...]
        dy_ref[...] = err * (1.0 / D)
        part_ref[...] = jnp.full((1, 8, LANES), 0.5 * jnp.sum(err * err) * (1.0 / D), F32)

    blk = pl.BlockSpec((tm, D), lambda i: (i, 0))
    dy, part = pl.pallas_call(
        body, name="loss_head", grid=(s // tm,), in_specs=[blk, blk],
        out_specs=[blk, pl.BlockSpec((1, 8, LANES), lambda i: (i, 0, 0))],
        out_shape=[jax.ShapeDtypeStruct((s, D), F32), jax.ShapeDtypeStruct((s // tm, 8, LANES), F32)],
        compiler_params=_cparams(),
    )(y, target)
    return dy, jnp.sum(part[:, 0, 0])


def _layer_fwd(x0, mem, w, p, fetch):
    z, hn0 = _rowop_mm("mix_in", "rms", (x0,), p["norm_mix_pre"], w["w_in"], NT, F32)
    y = _gmlp_fwd(z, p["gmlp_v_gain"], p["w_spatial"], p["b_spatial_t"], 512)
    y = _pool_fwd(z, p["w_pool"], p["s_pool"], y)
    y = _conv_fwd(z, p["w_dw"], p["b_dw"], p["conv_ln_g"], p["conv_ln_b"], y)
    w.update(fetch("att", (y,)))
    x1, h0 = _mm_rowop("mix_out", "rms_res", [(y, w["w_out"], NN)], (x0,), p["norm_mix_post"])
    q, hn1 = _rowop_mm("att_q", "rms", (x1,), p["norm_xattn_pre"], w["w_q"], NN, BF16)
    k, mn = _rowop_mm("att_k", "rms", (mem,), p["norm_mem"], w["w_k"], NN, BF16)
    v, _ = _rowop_mm("att_v", "rms", (mem,), p["norm_mem"], w["w_v"], NN, BF16)
    o = _attn_fwd(q, k, v, 256)
    x2, h1 = _mm_rowop("att_o", "rms_res", [(o, w["w_o"], NN)], (x1,), p["norm_xattn_post"])
    w.update(fetch("ffn", (x2,)))
    u, hn2 = _rowop_mm("ffn_up", "rms", (x2,), p["norm_ffn_pre"], w["w_up"], NT, F32)
    x3, h2 = _mm_rowop("ffn_down", "rms_res", [(u, w["w_down"], NN)], (x2,), p["norm_ffn_post"], relu2=True)
    saved = dict(x0=x0, z=z, hn0=hn0, y=y, h0=h0, x1=x1, q=q, hn1=hn1, k=k, v=v, mn=mn, o=o, h1=h1, x2=x2, u=u,
                 hn2=hn2, h2=h2)
    return x3, saved


def _layer_bwd(dx3, mem, w, p, sv, red):
    gs = {}
    du, dh2, dg = _rowop_mm("ffn_down_bwd", "rms_bwd", (sv["h2"], dx3), p["norm_ffn_post"], w["w_down"], NT, BF16,
                            u=sv["u"], after=red.after())
    gs["norm_ffn_post"] = jnp.sum(dg, axis=0)
    g_down = _mm_tn("ffn_down_dw", sv["u"], dh2, relu2=True)
    red.advance((g_down,))
    dx2, dg = _mm_rowop("ffn_up_bwd", "rms_bwd_res", [(du, w["w_up"], NN)], (sv["x2"], dx3), p["norm_ffn_pre"],
                        after=red.after())
    gs["norm_ffn_pre"] = jnp.sum(dg, axis=0)
    g_up = _mm_tn("ffn_up_dw", du, sv["hn2"])
    red.add("ffn", ("w_down", "w_up"), [g_down, g_up])
    do, dh1, dg = _rowop_mm("att_o_bwd", "rms_bwd", (sv["h1"], dx2), p["norm_xattn_post"], w["w_o"], NT, BF16,
                            after=red.after())
    gs["norm_xattn_post"] = jnp.sum(dg, axis=0)
    g_o = _mm_tn("att_o_dw", sv["o"], dh1)
    red.advance((g_o,))
    dq, dk, dv = _attn_bwd(sv["q"], sv["k"], sv["v"], do, 256)
    dk, dv = dk.astype(BF16), dv.astype(BF16)
    dx1, dg = _mm_rowop("att_q_bwd", "rms_bwd_res", [(dq, w["w_q"], NT)], (sv["x1"], dx2), p["norm_xattn_pre"],
                        after=red.after())
    gs["norm_xattn_pre"] = jnp.sum(dg, axis=0)
    g_q = _mm_tn("att_q_dw", sv["hn1"], dq)
    g_k = _mm_tn("att_k_dw", sv["mn"], dk)
    g_v = _mm_tn("att_v_dw", sv["mn"], dv)
    (dg,) = _mm_rowop("att_kv_bwd", "rms_bwd_gain", [(dk, w["w_k"], NT), (dv, w["w_v"], NT)], (mem,), p["norm_mem"])
    gs["norm_mem"] = jnp.sum(dg, axis=0)
    red.add("att", ("w_o", "w_q", "w_k", "w_v"), [g_o, g_q, g_k, g_v])
    dy, dh0, dg = _rowop_mm("mix_out_bwd", "rms_bwd", (sv["h0"], dx1), p["norm_mix_post"], w["w_out"], NT, F32,
                            after=red.after())
    gs["norm_mix_post"] = jnp.sum(dg, axis=0)
    g_out = _mm_tn("mix_out_dw", sv["y"], dh0)
    red.advance((g_out,))
    red.add("out", ("w_out",), [g_out])
    z = sv["z"]
    dz, dgv, dws, dbs = _gmlp_bwd(z, dy, p["gmlp_v_gain"], p["w_spatial"], p["b_spatial_t"], 512)
    gs["gmlp_v_gain"] = jnp.sum(dgv, axis=0)
    gs["w_spatial"] = jnp.sum(dws, axis=0)
    gs["b_spatial"] = jnp.sum(dbs[..., 0], axis=0)
    dz, gs["w_pool"], gs["s_pool"] = _pool_bwd(z, dy, p["w_pool"], p["s_pool"], dz)
    dz, dwdw, gs["b_dw"], gs["conv_ln_g"], gs["conv_ln_b"] = _conv_bwd(
        z, dy, p["w_dw"], p["b_dw"], p["conv_ln_g"], p["conv_ln_b"], dz)
    red.advance((dz,))
    g_in = _mm_tn("mix_in_dw", dz, sv["hn0"], after=red.after())
    red.add("in", ("w_in",), [g_in])
    red.advance((g_in,))
    dx0, dg = _mm_rowop("mix_in_bwd", "rms_bwd_res", [(dz, w["w_in"], NN)], (sv["x0"], dx1), p["norm_mix_pre"],
                        after=red.after())
    gs["norm_mix_pre"] = jnp.sum(dg, axis=0)
    return dx0, _small_grad_arrays(gs, dwdw)


NORM_NAMES = ("norm_mix_pre", "norm_mix_post", "norm_xattn_pre", "norm_mem", "norm_xattn_post", "norm_ffn_pre",
              "norm_ffn_post")
VEC_NAMES = ("s_pool", "b_dw", "conv_ln_g", "conv_ln_b")
SMALL_ARRAYS = ("norms", "gain_bias", "w_spatial", "w_pool", "vecs", "w_dw")


def _small_grad_arrays(gs, dwdw):
    return {"norms": jnp.concatenate([gs[n] for n in NORM_NAMES], axis=0),
            "gain_bias": jnp.concatenate([gs["gmlp_v_gain"], gs["b_spatial"]], axis=0),
            "w_spatial": gs["w_spatial"], "w_pool": gs["w_pool"],
            "vecs": jnp.concatenate([gs[n] for n in VEC_NAMES], axis=0), "w_dw": dwdw}


def _split_small_grads(arrays):
    out = {n: arrays["norms"][k] for k, n in enumerate(NORM_NAMES)}
    out.update({n: arrays["vecs"][k] for k, n in enumerate(VEC_NAMES)})
    out.update(gmlp_v_gain=arrays["gain_bias"][:NH], b_spatial=arrays["gain_bias"][NH:], w_spatial=arrays["w_spatial"],
               w_pool=arrays["w_pool"], w_dw=arrays["w_dw"][:CONV_K])
    return out


def _layer_params(small, l):
    p = {n: small[n][l].reshape(1, -1) for n in ("norm_mix_pre", "norm_mix_post", "s_pool", "b_dw", "conv_ln_g",
                                                   "conv_ln_b", "norm_xattn_pre", "norm_mem", "norm_xattn_post",
                                                   "norm_ffn_pre", "norm_ffn_post")}
    p["gmlp_v_gain"] = small["gmlp_v_gain"][l]
    p["w_spatial"] = small["w_spatial"][l]
    p["b_spatial_t"] = small["b_spatial"][l].T
    p["w_pool"] = small["w_pool"][l]
    p["w_dw"] = jnp.pad(small["w_dw"][l], ((0, 1), (0, 0)))
    return p


def _local_step(x, mem, target, fetch, small, red):
    small = dict(small)
    saved, weights, params = [], [], []
    h = x
    marker = ()
    for l in range(DEPTH):
        w = fetch(l, "in", marker)
        if "taps" in w:
            small["w_dw"] = w.pop("taps")
        p = _layer_params(small, l)
        h, sv = _layer_fwd(h, mem, w, p, functools.partial(fetch, l))
        marker = (h,)
        saved.append(sv)
        weights.append(w)
        params.append(p)
    dh, loss = _loss_head(h, target, 512)
    gsmall = [None] * DEPTH
    for l in reversed(range(DEPTH)):
        red.layer = l
        dh, gsmall[l] = _layer_bwd(dh, mem, weights[l], params[l], saved[l], red)
    return loss, dh, gsmall


HBM = pl.BlockSpec(memory_space=pltpu.HBM)


def _position():
    return lax.axis_index("x"), lax.axis_index("y"), lax.axis_index("c")


SEM = pl.BlockSpec(memory_space=pltpu.SEMAPHORE)
EFFECT = pltpu.SideEffectType.DATAFLOW_SIDE_EFFECTING
TOKEN = jax.ShapeDtypeStruct((8, LANES), F32)
TOKEN_SPEC = pl.BlockSpec(memory_space=pltpu.VMEM)


def _landing(shape, dtype):
    return pltpu.with_memory_space_constraint(lax.empty(shape, dtype), pltpu.HBM)


def _hbm_shapes(arrays):
    return [pltpu.HBM(a.shape, a.dtype) for a in arrays]


def _block(ref, r, dev):
    return ref.at[pl.ds((4 * dev[0] + 2 * dev[1] + dev[2]) * r, r), :]


def _split_call(name, body, thru, sems_in, after, sems_out, token):
    n = len(thru)
    out_shape = [pltpu.SemaphoreType.DMA(s) for s in sems_out] + _hbm_shapes(thru) + ([TOKEN] if token else [])
    out_specs = [SEM] * len(sems_out) + [HBM] * n + ([TOKEN_SPEC] if token else [])
    return pl.pallas_call(
        body, name=name, in_specs=[HBM] * n + [SEM] * len(sems_in) + [ANY] * len(after),
        out_specs=out_specs, out_shape=out_shape,
        input_output_aliases={i: len(sems_out) + i for i in range(n)},
        compiler_params=pltpu.CompilerParams(has_side_effects=EFFECT),
    )(*thru, *sems_in, *after)


def _place_own(name, srcs, dev, out_dtype, tr):
    n = len(srcs)
    r, cols = srcs[0][0].shape[-2:]
    tr = r if r < 16 else _row_tile(r, tr)
    nb = r // tr

    def body(dev_ref, *refs):
        for a in range(n):
            refs[n + a][...] = refs[a][...].astype(out_dtype)

    in_specs = [pl.BlockSpec((tr, cols), lambda i, d: (i, 0)) if l is None
                else pl.BlockSpec((None, tr, cols), lambda i, d, l=l: (l, i, 0)) for _, l in srcs]
    return pl.pallas_call(
        body, name=name,
        grid_spec=pltpu.PrefetchScalarGridSpec(
            num_scalar_prefetch=1, grid=(nb,), in_specs=in_specs,
            out_specs=[pl.BlockSpec((tr, cols), lambda i, d: (d[0] * nb + i, 0))] * n),
        out_shape=[jax.ShapeDtypeStruct((N_DEV * r, cols), out_dtype)] * n, compiler_params=_cparams(),
    )(dev, *[a for a, _ in srcs])


def _gather_peers(x, y, c):
    return [(1 - x, y, c), (x, 1 - y, c), (1 - x, 1 - y, c), (x, y, 1 - c)]


def _block_rows(land):
    return land.shape[0] // N_DEV


def _gather_start(name, lands, after):
    n = len(lands)

    def body(*refs):
        lz = refs[:n]
        send_sems, recv_sems = refs[n + len(after)], refs[n + len(after) + 1]
        token = refs[-1]
        x, y, c = _position()
        for a in range(n):
            own = _block(lz[a], _block_rows(lands[a]), (x, y, c))
            for k, to in enumerate(_gather_peers(x, y, c)):
                pltpu.make_async_remote_copy(src_ref=own, dst_ref=own, send_sem=send_sems.at[k], recv_sem=recv_sems.at[k],
                                             device_id=to, device_id_type=MESH).start()
        token[...] = jnp.zeros_like(token)

    out = _split_call(name, body, list(lands), [], after, [(4,), (4,)], True)
    return out[0], out[1], out[2:2 + n], out[-1]


def _gather_forward(name, lands, recv_sems, after):
    n = len(lands)

    def body(*refs):
        lz = refs[:n]
        recv0 = refs[n]
        fsend, frecv = refs[n + 1 + len(after)], refs[n + 2 + len(after)]
        token = refs[-1]
        x, y, c = _position()
        chips = _gather_peers(x, y, c)[:3]
        for a in range(n):
            for j, chip in enumerate(chips):
                blk = _block(lz[a], _block_rows(lands[a]), chip)
                pltpu.make_async_remote_copy(src_ref=blk, dst_ref=blk, send_sem=fsend.at[j], recv_sem=recv0.at[j],
                                             device_id=(x, y, c), device_id_type=MESH).wait_recv()
        for a in range(n):
            for j, chip in enumerate(chips):
                blk = _block(lz[a], _block_rows(lands[a]), chip)
                pltpu.make_async_remote_copy(src_ref=blk, dst_ref=blk, send_sem=fsend.at[j], recv_sem=frecv.at[j],
                                             device_id=(x, y, 1 - c), device_id_type=MESH).start()
        token[...] = jnp.zeros_like(token)

    out = _split_call(name, body, list(lands), [recv_sems], after, [(3,), (3,)], True)
    return out[0], out[1], out[2:2 + n], out[-1]


def _gather_finish(name, lands, send_sems, recv_sems, fsend, frecv, after):
    n = len(lands)

    def body(*refs):
        lz = refs[:n]
        send0, recv0, fsend_ref, frecv_ref = refs[n:n + 4]
        x, y, c = _position()
        me = (x, y, c)
        chips = _gather_peers(x, y, c)[:3]
        for a in range(n):
            r = _block_rows(lands[a])
            sib = _block(lz[a], r, (x, y, 1 - c))
            pltpu.make_async_remote_copy(src_ref=sib, dst_ref=sib, send_sem=send0.at[3], recv_sem=recv0.at[3],
                                         device_id=me, device_id_type=MESH).wait_recv()
            for j, chip in enumerate(chips):
                blk = _block(lz[a], r, (chip[0], chip[1], 1 - c))
                pltpu.make_async_remote_copy(src_ref=blk, dst_ref=blk, send_sem=fsend_ref.at[j], recv_sem=frecv_ref.at[j],
                                             device_id=me, device_id_type=MESH).wait_recv()
            own = _block(lz[a], r, me)
            for k in range(4):
                pltpu.make_async_remote_copy(src_ref=own, dst_ref=own, send_sem=send0.at[k], recv_sem=recv0.at[k],
                                             device_id=me, device_id_type=MESH).wait_send()
            for j, chip in enumerate(chips):
                blk = _block(lz[a], r, chip)
                pltpu.make_async_remote_copy(src_ref=blk, dst_ref=blk, send_sem=fsend_ref.at[j], recv_sem=frecv_ref.at[j],
                                             device_id=me, device_id_type=MESH).wait_send()

    return _split_call(name, body, list(lands), [send_sems, recv_sems, fsend, frecv], after, [], False)


def _sibling_start(name, grads, after):
    n = len(grads)
    lands = [_landing((4, g.shape[0] // N_DEV, D), g.dtype) for g in grads]

    def body(*refs):
        ins, lz = refs[:n], refs[n:2 * n]
        send_sem, recv_sem = refs[2 * n + len(after)], refs[2 * n + len(after) + 1]
        token = refs[-1]
        x, y, c = _position()
        for a in range(n):
            r = grads[a].shape[0] // N_DEV
            for q in range(4):
                pltpu.make_async_remote_copy(
                    src_ref=ins[a].at[pl.ds((2 * q + 1 - c) * r, r), :], dst_ref=lz[a].at[q], send_sem=send_sem.at[0],
                    recv_sem=recv_sem.at[0], device_id=(x, y, 1 - c), device_id_type=MESH).start()
        token[...] = jnp.zeros_like(token)

    out = _split_call(name, body, list(grads) + lands, [], after, [(1,), (1,)], True)
    return out[0], out[1], out[2:2 + n], out[2 + n:2 + 2 * n], out[-1]


def _sibling_finish(name, grads, lands, send_sem, recv_sem, after):
    n = len(grads)

    def body(*refs):
        ins, lz = refs[:n], refs[n:2 * n]
        send_ref, recv_ref = refs[2 * n], refs[2 * n + 1]
        x, y, c = _position()
        for a in range(n):
            r = grads[a].shape[0] // N_DEV
            for q in range(4):
                cp = pltpu.make_async_remote_copy(
                    src_ref=ins[a].at[pl.ds((2 * q + 1 - c) * r, r), :], dst_ref=lz[a].at[q], send_sem=send_ref.at[0],
                    recv_sem=recv_ref.at[0], device_id=(x, y, c), device_id_type=MESH)
                cp.wait_send()
                cp.wait_recv()

    out = _split_call(name, body, list(grads) + list(lands), [send_sem, recv_sem], after, [], False)
    return out[:n], out[n:2 * n]


def _chip_start(name, parts, after):
    n = len(parts)
    lands = [_landing((3,) + p.shape[1:], p.dtype) for p in parts]

    def body(*refs):
        ins, lz = refs[:n], refs[n:2 * n]
        send_sems, recv_sems = refs[2 * n + len(after)], refs[2 * n + len(after) + 1]
        token = refs[-1]
        x, y, c = _position()
        for a in range(n):
            for j, chip in enumerate(_gather_peers(x, y, c)[:3]):
                pltpu.make_async_remote_copy(
                    src_ref=ins[a].at[2 * chip[0] + chip[1]], dst_ref=lz[a].at[j], send_sem=send_sems.at[j],
                    recv_sem=recv_sems.at[j], device_id=chip, device_id_type=MESH).start()
        token[...] = jnp.zeros_like(token)

    out = _split_call(name, body, list(parts) + lands, [], after, [(3,), (3,)], True)
    return out[0], out[1], out[2:2 + n], out[2 + n:2 + 2 * n], out[-1]


def _chip_finish(name, parts, lands, send_sems, recv_sems, after):
    n = len(parts)

    def body(*refs):
        ins, lz = refs[:n], refs[n:2 * n]
        send_ref, recv_ref = refs[2 * n], refs[2 * n + 1]
        me = _position()
        for a in range(n):
            for j in range(3):
                cp = pltpu.make_async_remote_copy(
                    src_ref=ins[a].at[j], dst_ref=lz[a].at[j], send_sem=send_ref.at[j], recv_sem=recv_ref.at[j],
                    device_id=me, device_id_type=MESH)
                cp.wait_send()
                cp.wait_recv()

    out = _split_call(name, body, list(parts) + list(lands), [send_sems, recv_sems], after, [], False)
    return out[:n], out[n:2 * n]


def _other_devices(x, y, c):
    return [(x + (k >> 2 & 1) * (1 - 2 * x), y + (k >> 1 & 1) * (1 - 2 * y), c + (k & 1) * (1 - 2 * c))
            for k in range(1, N_DEV)]


def _broadcast_start(name, arrays, after):
    n = len(arrays)
    lands = [_landing((N_DEV,) + a.shape, a.dtype) for a in arrays]

    def body(*refs):
        ins, lz = refs[:n], refs[n:2 * n]
        send_sems, recv_sems = refs[2 * n + len(after)], refs[2 * n + len(after) + 1]
        token = refs[-1]
        x, y, c = _position()
        for a in range(n):
            for k, peer in enumerate(_other_devices(x, y, c)):
                pltpu.make_async_remote_copy(
                    src_ref=ins[a], dst_ref=lz[a].at[4 * x + 2 * y + c], send_sem=send_sems.at[k],
                    recv_sem=recv_sems.at[k], device_id=peer, device_id_type=MESH).start()
        token[...] = jnp.zeros_like(token)

    out = _split_call(name, body, list(arrays) + lands, [], after, [(N_DEV - 1,), (N_DEV - 1,)], True)
    return out[0], out[1], out[2:2 + n], out[2 + n:2 + 2 * n], out[-1]


def _broadcast_finish(name, arrays, lands, send_sems, recv_sems, after):
    n = len(arrays)

    def body(*refs):
        ins, lz = refs[:n], refs[n:2 * n]
        send_ref, recv_ref = refs[2 * n], refs[2 * n + 1]
        x, y, c = _position()
        for a in range(n):
            for k, peer in enumerate(_other_devices(x, y, c)):
                cp = pltpu.make_async_remote_copy(
                    src_ref=ins[a], dst_ref=lz[a].at[4 * peer[0] + 2 * peer[1] + peer[2]], send_sem=send_ref.at[k],
                    recv_sem=recv_ref.at[k], device_id=(x, y, c), device_id_type=MESH)
                cp.wait_send()
                cp.wait_recv()

    out = _split_call(name, body, list(arrays) + list(lands), [send_sems, recv_sems], after, [], False)
    return out[:n], out[n:2 * n]


def _row_tile(r, target):
    return max(t for t in range(16, min(r, target) + 1, 16) if r % t == 0)


def _chip_partial(name, grad, got, c, tr):
    r = grad.shape[0] // N_DEV
    tr = _row_tile(r, tr)
    g4 = grad.reshape(4, 2, r, D)

    def body(c_ref, g_ref, s_ref, o_ref):
        o_ref[...] = (g_ref[...].astype(F32) + s_ref[...].astype(F32)).astype(BF16)

    return pl.pallas_call(
        body, name=name,
        grid_spec=pltpu.PrefetchScalarGridSpec(
            num_scalar_prefetch=1, grid=(4, r // tr),
            in_specs=[pl.BlockSpec((None, None, tr, D), lambda q, i, c_ref: (q, c_ref[0], i, 0)),
                      pl.BlockSpec((None, tr, D), lambda q, i, c_ref: (q, i, 0))],
            out_specs=pl.BlockSpec((None, tr, D), lambda q, i, c_ref: (q, i, 0))),
        out_shape=jax.ShapeDtypeStruct((4, r, D), BF16), compiler_params=_cparams(),
    )(c, g4, got)


class _WeightGather:
    def __init__(self, groups):
        self.state, token = {}, ()
        for key, names, lands in groups:
            send, recv, lz, tok = _gather_start("gather_start_%s_%d" % key[::-1], lands, token)
            self.state[key] = (names, send, recv, lz)
            token = (tok,)
        self.started = token

    def fetch(self, layer, group, marker):
        names, send, recv, lz = self.state.pop((layer, group))
        tag = "%s_%d" % (group, layer)
        fsend, frecv, lz, tok = _gather_forward("gather_forward_" + tag, lz, recv, marker or self.started)
        lz = _gather_finish("gather_finish_" + tag, lz, send, recv, fsend, frecv, (tok,))
        return dict(zip(names, lz))


class _GradReduce:
    def __init__(self, core, chip):
        self.core, self.chip = core, chip
        self.layer = None
        self.token = ()
        self.at_sibling, self.at_chips = [], []

    def after(self):
        return self.token

    def add(self, group, names, grads):
        tag = "%s_%d" % (group, self.layer)
        send, recv, grads, lands, tok = _sibling_start("grad_sibling_start_" + tag, grads, self.token)
        self.at_sibling.append((tag, [(self.layer, n) for n in names], send, recv, grads, lands))
        self.token = (tok,)

    def advance(self, marker):
        for tag, keys, send, recv, grads, lands in self.at_sibling:
            grads, lands = _sibling_finish("grad_sibling_finish_" + tag, grads, lands, send, recv, marker)
            parts = [_chip_partial("chip_partial_%d_%s" % key, g, got, self.core, 256)
                     for key, g, got in zip(keys, grads, lands)]
            send, recv, parts, lands, tok = _chip_start("grad_chip_start_" + tag, parts, ())
            self.at_chips.append([tag, keys, send, recv, parts, lands])
            self.token = (tok,)
        self.at_sibling = []

    def collect(self, key, marker):
        for entry in self.at_chips:
            tag, keys, send, recv, parts, lands = entry
            if key in keys:
                if send is not None:
                    parts, lands = _chip_finish("grad_chip_finish_" + tag, parts, lands, send, recv, marker)
                    entry[2:] = [None, None, parts, lands]
                i = keys.index(key)
                return parts[i], lands[i]
        raise KeyError(key)


def _adamw_math(w, g, m, v):
    m = ADAM_B1 * m + (1.0 - ADAM_B1) * g
    v = ADAM_B2 * v + (1.0 - ADAM_B2) * jnp.square(g)
    m_hat = m / (1.0 - ADAM_B1 ** ADAM_STEP)
    v_hat = v / (1.0 - ADAM_B2 ** ADAM_STEP)
    delta = -ADAM_LR * (m_hat / (jnp.sqrt(v_hat) + ADAM_EPS) + ADAM_WD * w)
    return delta, m, v


def _adamw_small(wts, mom_m, mom_v, own, gathered, dev):
    names = SMALL
    nw = len(names)
    na = len(SMALL_ARRAYS)

    def body(dev_ref, *refs):
        w_refs, m_refs, v_refs = (dict(zip(names, refs[i * nw:(i + 1) * nw])) for i in range(3))
        own_refs = refs[3 * nw:3 * nw + DEPTH * na]
        g_refs = refs[3 * nw + DEPTH * na:3 * nw + 2 * DEPTH * na]
        outs = refs[3 * nw + 2 * DEPTH * na:]
        g_out, d_out, m_out, v_out = (dict(zip(names, outs[i * nw:(i + 1) * nw])) for i in range(4))
        me = dev_ref[0]

        def update(name, at, g):
            g_out[name][at] = g
            d_out[name][at], m_out[name][at], v_out[name][at] = _adamw_math(
                w_refs[name][at], g, m_refs[name][at], v_refs[name][at])

        for l in range(DEPTH):
            mine = dict(zip(SMALL_ARRAYS, own_refs[l * na:(l + 1) * na]))
            got = dict(zip(SMALL_ARRAYS, g_refs[l * na:(l + 1) * na]))

            def total(key, at):
                acc = None
                for d in range(N_DEV):
                    term = jnp.where(me == d, mine[key][at] if at else mine[key][...], got[key][(d,) + at])
                    acc = term if acc is None else acc + term
                return acc

            row = (slice(l, l + 1),)
            for k, name in enumerate(NORM_NAMES):
                update(name, row, total("norms", (slice(k, k + 1),)))
            for k, name in enumerate(VEC_NAMES):
                update(name, row, total("vecs", (slice(k, k + 1),)))
            update("gmlp_v_gain", (l,), total("gain_bias", (slice(0, NH),)))
            update("b_spatial", (l,), total("gain_bias", (slice(NH, 2 * NH),)))
            update("w_spatial", (l,), total("w_spatial", ()))
            update("w_pool", (l,), total("w_pool", ()))
            update("w_dw", (l,), total("w_dw", (slice(0, CONV_K),)))

    args = [src[n] for src in (wts, mom_m, mom_v) for n in names]
    args += [src[l][k] for src in (own, gathered) for l in range(DEPTH) for k in SMALL_ARRAYS]
    outs = pl.pallas_call(
        body, name="adamw_small",
        in_specs=[pl.BlockSpec(memory_space=pltpu.SMEM)] + [pl.BlockSpec(memory_space=pltpu.VMEM)] * len(args),
        out_shape=[jax.ShapeDtypeStruct(wts[n].shape, F32) for _ in range(4) for n in names],
        compiler_params=_cparams(),
    )(dev, *args)
    return tuple(dict(zip(names, outs[i * nw:(i + 1) * nw])) for i in range(4))


def _adamw_layers(name, w, reduced, m, v, chip, tr):
    nl, r, cdim = w.shape
    tr = _row_tile(r, tr)
    nb = r // tr

    def body(q_ref, w_ref, p0_ref, g0_ref, p1_ref, g1_ref, m_ref, v_ref, g_ref, d_ref, nm_ref, nv_ref):
        def total(p_ref, got_ref):
            acc = p_ref[...].astype(F32)
            for j in range(3):
                acc = acc + got_ref[j].astype(F32)
            return acc

        g = jnp.where(pl.program_id(0) == 0, total(p0_ref, g0_ref), total(p1_ref, g1_ref))
        g_ref[...] = g
        d_ref[...], nm_ref[...], nv_ref[...] = _adamw_math(w_ref[...], g, m_ref[...], v_ref[...])

    blk = pl.BlockSpec((None, tr, cdim), lambda l, i, q: (l, i, 0))
    first = lambda l, i: i * (1 - l) + (nb - 1) * l
    second = lambda l, i: i * l
    specs = [blk,
             pl.BlockSpec((None, tr, cdim), lambda l, i, q: (q[0], first(l, i), 0)),
             pl.BlockSpec((3, tr, cdim), lambda l, i, q: (0, first(l, i), 0)),
             pl.BlockSpec((None, tr, cdim), lambda l, i, q: (q[0], second(l, i), 0)),
             pl.BlockSpec((3, tr, cdim), lambda l, i, q: (0, second(l, i), 0)), blk, blk]
    shape = jax.ShapeDtypeStruct((nl, r, cdim), F32)
    return pl.pallas_call(
        body, name=name,
        grid_spec=pltpu.PrefetchScalarGridSpec(num_scalar_prefetch=1, grid=(nl, nb), in_specs=specs, out_specs=[blk] * 4),
        out_shape=[shape] * 4, compiler_params=_cparams(),
    )(chip, w, *reduced[0], *reduced[1], m, v)


def _to_rows(name, a):
    return jnp.swapaxes(a, 1, 2) if name in ("w_in", "w_up") else a


def _pack(arrays, rows):
    flat = jnp.concatenate([a.reshape(-1) for a in arrays])
    return jnp.pad(flat, (0, rows * D - flat.shape[0])).reshape(rows, D)


def _rows_for(shapes, mult=8):
    total = 0
    for shp in shapes:
        size = 1
        for dim in shp:
            size *= dim
        total += size
    return -(-total // (mult * D)) * mult


def kernel(x, mem, norm_mix_pre, norm_mix_post, w_in, w_out, gmlp_v_gain, w_spatial, b_spatial, w_pool, s_pool, w_dw, b_dw, conv_ln_g, conv_ln_b, norm_xattn_pre, norm_mem, norm_xattn_post, w_q, w_k, w_v, w_o, norm_ffn_pre, norm_ffn_post, w_up, w_down, loss_target, m_norm_mix_pre, m_norm_mix_post, m_w_in, m_w_out, m_gmlp_v_gain, m_w_spatial, m_b_spatial, m_w_pool, m_s_pool, m_w_dw, m_b_dw, m_conv_ln_g, m_conv_ln_b, m_norm_xattn_pre, m_norm_mem, m_norm_xattn_post, m_w_q, m_w_k, m_w_v, m_w_o, m_norm_ffn_pre, m_norm_ffn_post, m_w_up, m_w_down, v_norm_mix_pre, v_norm_mix_post, v_w_in, v_w_out, v_gmlp_v_gain, v_w_spatial, v_b_spatial, v_w_pool, v_s_pool, v_w_dw, v_b_dw, v_conv_ln_g, v_conv_ln_b, v_norm_xattn_pre, v_norm_mem, v_norm_xattn_post, v_w_q, v_w_k, v_w_v, v_w_o, v_norm_ffn_pre, v_norm_ffn_post, v_w_up, v_w_down):
    args = dict(locals())
    wts = {n: args[n] for n in WEIGHTS}
    mom_m = {n: args["m_" + n] for n in WEIGHTS}
    mom_v = {n: args["v_" + n] for n in WEIGHTS}
    xi, yi, ci = _position()
    me = 4 * xi + 2 * yi + ci

    dev = jnp.reshape(me, (1,)).astype(jnp.int32)
    lands = {}
    for call, names, tr in (("place_att", ("w_out", "w_q", "w_k", "w_v", "w_o"), 64), ("place_down", ("w_down",), 256),
                            ("place_up", ("w_up",), 256), ("place_in", ("w_in",), 256)):
        srcs = [(_to_rows(n, wts[n]), l) for l in range(DEPTH) for n in names]
        placed = _place_own(call, srcs, dev, BF16, tr)
        lands.update(zip([(l, n) for l in range(DEPTH) for n in names], placed))
    (lands[(0, "taps")],) = _place_own("place_taps", [(_pack([w_dw], _rows_for([w_dw.shape])), None)], dev, F32, 8)
    groups = []
    for l in range(DEPTH):
        for group, names in GATHER_GROUPS:
            if (l, group) == (0, "in"):
                names = names + ("taps",)
            groups.append(((l, group), names, [lands[(l, n)] for n in names]))
    gather = _WeightGather(groups)

    def fetch(layer, group, marker):
        w = gather.fetch(layer, group, marker)
        if "taps" in w:
            blocks = w["taps"].reshape(N_DEV, -1)[:, :w_dw.size].reshape((N_DEV,) + w_dw.shape)
            w["taps"] = jnp.moveaxis(blocks, 0, 2).reshape(DEPTH, CONV_K, CW)
        return w

    reduce = _GradReduce(jnp.reshape(ci, (1,)).astype(jnp.int32), jnp.reshape(2 * xi + yi, (1,)).astype(jnp.int32))
    small = {n: wts[n] for n in SMALL if n != "w_dw"}
    loss, dx, gsmall = _local_step(x[0], mem[0], loss_target[0], fetch, small, reduce)
    reduce.advance((dx,))

    na = len(SMALL_ARRAYS)
    ssend, srecv, own, slots, token = _broadcast_start(
        "small_grads_start", [gsmall[l][k] for l in range(DEPTH) for k in SMALL_ARRAYS], (dx,))

    grad_w, delta, new_m, new_v = {}, {}, {}, {}
    marker = (token,)
    for n in UPDATE_ORDER:
        reduced = [reduce.collect((l, n), marker) for l in range(DEPTH)]
        outs = _adamw_layers("adamw_" + n, _to_rows(n, wts[n]), reduced, _to_rows(n, mom_m[n]), _to_rows(n, mom_v[n]),
                             reduce.chip, 256)
        grad_w[n], delta[n], new_m[n], new_v[n] = (_to_rows(n, o) for o in outs)
        marker = (outs[1],)

    own, slots = _broadcast_finish("small_grads_finish", own, slots, ssend, srecv, marker)
    own = [dict(zip(SMALL_ARRAYS, own[l * na:(l + 1) * na])) for l in range(DEPTH)]
    slots = [dict(zip(SMALL_ARRAYS, slots[l * na:(l + 1) * na])) for l in range(DEPTH)]
    shard_cols = CW // N_DEV
    for l in range(DEPTH):
        own[l]["w_dw"] = lax.dynamic_slice_in_dim(own[l]["w_dw"], me * shard_cols, shard_cols, axis=1)
        slots[l]["w_dw"] = lax.dynamic_slice_in_dim(slots[l]["w_dw"], me * shard_cols, shard_cols, axis=2)
    small_out = _adamw_small(wts, mom_m, mom_v, own, slots, dev)
    for dst, src in zip((grad_w, delta, new_m, new_v), small_out):
        dst.update(src)

    loss = lax.psum(loss, ("x", "y", "c"))
    return (loss, dx[None], *[grad_w[n] for n in WEIGHTS], *[delta[n] for n in WEIGHTS],
            *[new_m[n] for n in WEIGHTS], *[new_v[n] for n in WEIGHTS])
```

```python
import functools

import jax
import jax.numpy as jnp
from jax import lax
from jax.experimental import pallas as pl
from jax.experimental.pallas import tpu as pltpu

F32 = jnp.float32
BF16 = jnp.bfloat16

D = 2048
GW = 1024
PW = 512
CW = 512
HD = 128
NH = 8
NG = 4
POOL_WINDOWS = (2, 4, 8, 16)
CONV_K = 31
IN_COLS = 2 * GW + PW + 2 * CW
DFF = 4 * D
XH = 4
XHD = D // XH
ATT_SCALE = XHD ** -0.5
RMS_EPS = 1e-6
LN_EPS = 1e-5
DEPTH = 2
N_DEV = 8

ADAM_LR = 0.001
ADAM_B1 = 0.9
ADAM_B2 = 0.999
ADAM_EPS = 1e-08
ADAM_WD = 0.01
ADAM_STEP = 10

LANES = 128
CONV_HALO = 32
POOL_HALO = 16
ROW_TILE = 128
VMEM_LIMIT = 60 * 1024 * 1024

MESH = pl.DeviceIdType.MESH
NT = (((1,), (1,)), ((), ()))
NN = (((1,), (0,)), ((), ()))
TN = (((0,), (0,)), ((), ()))

BIG = ("w_out", "w_q", "w_k", "w_v", "w_o", "w_up", "w_down", "w_in")
UPDATE_ORDER = ("w_down", "w_up", "w_o", "w_q", "w_k", "w_v", "w_out", "w_in")
GATHER_GROUPS = (("in", ("w_in",)), ("att", ("w_out", "w_q", "w_k", "w_v", "w_o")), ("ffn", ("w_up", "w_down")))
SMALL = ("norm_mix_pre", "norm_mix_post", "gmlp_v_gain", "w_spatial", "b_spatial", "w_pool", "s_pool",
         "w_dw", "b_dw", "conv_ln_g", "conv_ln_b", "norm_xattn_pre", "norm_mem", "norm_xattn_post",
         "norm_ffn_pre", "norm_ffn_post")
WEIGHTS = ("norm_mix_pre", "norm_mix_post", "w_in", "w_out", "gmlp_v_gain", "w_spatial", "b_spatial", "w_pool",
           "s_pool", "w_dw", "b_dw", "conv_ln_g", "conv_ln_b", "norm_xattn_pre", "norm_mem", "norm_xattn_post",
           "w_q", "w_k", "w_v", "w_o", "norm_ffn_pre", "norm_ffn_post", "w_up", "w_down")


def _cparams():
    return pltpu.CompilerParams(vmem_limit_bytes=VMEM_LIMIT)


def _dot(a, b, dims):
    return lax.dot_general(a, b, dims, preferred_element_type=F32)


def _rms(x, g):
    y = x * lax.rsqrt(jnp.mean(x * x, axis=-1, keepdims=True) + RMS_EPS)
    return y * g


def _gelu(x):
    cdf = 0.5 * (1.0 + jnp.tanh(0.7978845608028654 * (x + 0.044715 * (x * x * x))))
    return x * cdf


def _layer_norm(x, g, b=None):
    mu = jnp.mean(x, axis=-1, keepdims=True)
    xc = x - mu
    var = jnp.mean(xc * xc, axis=-1, keepdims=True)
    y = xc * lax.rsqrt(var + LN_EPS) * g
    return y if b is None else y + b


def _sigmoid(x):
    return 1.0 / (1.0 + jnp.exp(-x))


def _gmlp_rows(zu, zv, gv):
    return _gelu(zu), _layer_norm(_gelu(zv), gv)


def _glu(cv, cg):
    return cv * _sigmoid(cg)


def _ln_silu(h, g, b):
    y = _layer_norm(h, g, b)
    return y * _sigmoid(y)


ANY = pl.BlockSpec(memory_space=pl.ANY)


ROWS_TILE = 256
COLS_TILE = 512
DW_TILE = 512
RESIDENT_K = 2048
STREAM_K_TILE = 1024
STREAM_ROWS = 512


def _k_tiles(kdim):
    if kdim <= RESIDENT_K:
        return ROWS_TILE, kdim
    return STREAM_ROWS, max(t for t in range(LANES, STREAM_K_TILE + 1, LANES) if kdim % t == 0)


def _rowop_mm(name, kind, rows, g, w, dims, out_dtype, u=None, after=()):
    s = rows[0].shape[0]
    n = w.shape[0] if dims == NT else w.shape[1]
    tm, tn = min(ROWS_TILE, s), min(COLS_TILE, n)
    ni = s // tm
    bwd = kind == "rms_bwd"

    def rows_body(*refs):
        refs = list(refs)
        row_refs = [refs.pop(0) for _ in rows]
        g_ref = refs.pop(0)
        del refs[:len(after)]
        if bwd:
            _, vjp = jax.vjp(_rms, row_refs[0][...], g_ref[...])
            a, dg = vjp(row_refs[1][...])
            refs[1][0] = dg
        else:
            a = _rms(row_refs[0][...], g_ref[...])
        refs[0][...] = a.astype(BF16)

    row_spec = pl.BlockSpec((tm, D), lambda i: (i, 0))
    res = pl.pallas_call(
        rows_body, name=name + "_rows", grid=(ni,),
        in_specs=[row_spec] * len(rows) + [pl.BlockSpec((1, D), lambda i: (0, 0))] + [ANY] * len(after),
        out_specs=[row_spec] + ([pl.BlockSpec((1, 1, D), lambda i: (i, 0, 0))] if bwd else []),
        out_shape=[jax.ShapeDtypeStruct((s, D), BF16)] + ([jax.ShapeDtypeStruct((ni, 1, D), F32)] if bwd else []),
        compiler_params=_cparams(),
    )(*rows, g, *after)
    a = res[0]

    def body(a_ref, w_ref, *rest):
        acc = _dot(a_ref[...], w_ref[...], dims)
        if u is not None:
            acc = acc * (2.0 * jnp.maximum(rest[0][...], 0.0))
        rest[-1][...] = acc.astype(out_dtype)

    w_spec = pl.BlockSpec((tn, D), lambda j: (j, 0)) if dims == NT else pl.BlockSpec((D, tn), lambda j: (0, j))
    tile = pl.BlockSpec((s, tn), lambda j: (0, j))
    out = pl.pallas_call(
        body, name=name, grid=(n // tn,),
        in_specs=[pl.BlockSpec((s, D), lambda j: (0, 0)), w_spec] + ([tile] if u is not None else []),
        out_specs=tile, out_shape=jax.ShapeDtypeStruct((s, n), out_dtype), compiler_params=_cparams(),
    )(a, w, *([u] if u is not None else []))
    return (out, *res)


def _mm_rowop(name, kind, pairs, rows, g, relu2=False, after=()):
    s, kdim = pairs[0][0].shape
    tm, tk = _k_tiles(kdim)
    tm = min(tm, s)
    ni, nk = s // tm, kdim // tk
    npair = len(pairs)

    def body(*refs):
        refs = list(refs)
        a_refs = [refs.pop(0) for _ in range(npair)]
        w_refs = [refs.pop(0) for _ in range(npair)]
        row_refs = [refs.pop(0) for _ in rows]
        g_ref = refs.pop(0)
        del refs[:len(after)]
        acc = refs.pop()
        outs = refs
        k = pl.program_id(1)

        @pl.when(k == 0)
        def _():
            acc[...] = jnp.zeros_like(acc)

        for a_ref, w_ref, (_, _, dims) in zip(a_refs, w_refs, pairs):
            a = a_ref[...]
            if relu2:
                a = jnp.square(jnp.maximum(a, 0.0))
            acc[...] += _dot(a.astype(BF16), w_ref[...], dims)

        @pl.when(k == nk - 1)
        def _():
            h = acc[...]
            if kind == "rms_res":
                outs[0][...] = row_refs[0][...] + _rms(h, g_ref[...])
                outs[1][...] = h
            else:
                _, vjp = jax.vjp(_rms, row_refs[0][...], g_ref[...])
                dx, dg = vjp(h)
                if kind == "rms_bwd_res":
                    outs[0][...] = row_refs[1][...] + dx
                    outs[1][0] = dg
                else:
                    outs[0][0] = dg

    row_spec = pl.BlockSpec((tm, D), lambda i, k: (i, 0))
    dg_shape = jax.ShapeDtypeStruct((ni, 1, D), F32)
    dg_spec = pl.BlockSpec((1, 1, D), lambda i, k: (i, 0, 0))
    in_specs = [pl.BlockSpec((tm, tk), lambda i, k: (i, k))] * npair
    for _, _, dims in pairs:
        in_specs.append(pl.BlockSpec((tk, D), lambda i, k: (k, 0)) if dims == NN
                        else pl.BlockSpec((D, tk), lambda i, k: (0, k)))
    in_specs += [row_spec] * len(rows) + [pl.BlockSpec((1, D), lambda i, k: (0, 0))] + [ANY] * len(after)
    if kind == "rms_res":
        out_shape = [jax.ShapeDtypeStruct((s, D), F32)] * 2
        out_specs = [row_spec, row_spec]
    elif kind == "rms_bwd_res":
        out_shape = [jax.ShapeDtypeStruct((s, D), F32), dg_shape]
        out_specs = [row_spec, dg_spec]
    else:
        out_shape = [dg_shape]
        out_specs = [dg_spec]
    return pl.pallas_call(
        body, name=name, grid=(ni, nk), in_specs=in_specs, out_specs=out_specs, out_shape=out_shape,
        scratch_shapes=[pltpu.VMEM((tm, D), F32)], compiler_params=_cparams(),
    )(*[p[0] for p in pairs], *[p[1] for p in pairs], *rows, g, *after)


def _mm_tn(name, a, gmat, relu2=False, after=()):
    s, m = a.shape
    tm, ts = min(DW_TILE, m), s
    ni, ns = m // tm, s // ts

    def body(a_ref, g_ref, *rest):
        o_ref, acc = rest[len(after):]
        k = pl.program_id(1)

        @pl.when(k == 0)
        def _():
            acc[...] = jnp.zeros_like(acc)

        av = a_ref[...]
        if relu2:
            av = jnp.square(jnp.maximum(av, 0.0))
        acc[...] += _dot(av.astype(BF16), g_ref[...], TN)

        @pl.when(k == ns - 1)
        def _():
            o_ref[...] = acc[...].astype(BF16)

    return pl.pallas_call(
        body, name=name, grid=(ni, ns),
        in_specs=[pl.BlockSpec((ts, tm), lambda i, k: (k, i)), pl.BlockSpec((ts, D), lambda i, k: (k, 0))]
        + [ANY] * len(after),
        out_specs=pl.BlockSpec((tm, D), lambda i, k: (i, 0)),
        out_shape=jax.ShapeDtypeStruct((m, D), BF16),
        scratch_shapes=[pltpu.VMEM((tm, D), F32)], compiler_params=_cparams(),
    )(a, gmat, *after)


def _tril():
    r = lax.broadcasted_iota(jnp.int32, (HD, HD), 0)
    c = lax.broadcasted_iota(jnp.int32, (HD, HD), 1)
    return (c <= r).astype(F32)


def _gmlp_fwd(z, gv, ws, bst, tb):
    s = z.shape[0]
    tb = min(tb, s)

    def body(zu_ref, zv_ref, gv_ref, ws_ref, bst_ref, y_ref):
        tril = _tril()
        for h in range(NH):
            cols = slice(h * HD, (h + 1) * HD)
            u, vln = _gmlp_rows(zu_ref[:, cols], zv_ref[:, cols], gv_ref[h:h + 1, :])
            wm = (ws_ref[h] * tril).astype(BF16)
            vb = vln.astype(BF16)
            for c in range(tb // HD):
                rws = slice(c * HD, (c + 1) * HD)
                mixed = _dot(wm, vb[rws], NN) + bst_ref[:, h:h + 1]
                y_ref[rws, cols] = (u[rws] * mixed).astype(BF16)

    return pl.pallas_call(
        body, name="gmlp_fwd", grid=(s // tb,),
        in_specs=[pl.BlockSpec((tb, GW), lambda i: (i, 0)), pl.BlockSpec((tb, GW), lambda i: (i, 1)),
                  pl.BlockSpec((NH, HD), lambda i: (0, 0)), pl.BlockSpec((NH, HD, HD), lambda i: (0, 0, 0)),
                  pl.BlockSpec((HD, NH), lambda i: (0, 0))],
        out_specs=pl.BlockSpec((tb, GW), lambda i: (i, 0)),
        out_shape=jax.ShapeDtypeStruct((s, D), BF16), compiler_params=_cparams(),
    )(z, z, gv, ws, bst)


def _gmlp_bwd(z, dy, gv, ws, bst, tb):
    s = z.shape[0]
    tb = min(tb, s)
    nb = s // tb

    def body(zu_ref, zv_ref, dy_ref, gv_ref, ws_ref, bst_ref, dz_ref, dgv_ref, dws_ref, db_ref):
        tril = _tril()
        for h in range(NH):
            cols = slice(h * HD, (h + 1) * HD)
            (u, vln), vjp = jax.vjp(_gmlp_rows, zu_ref[:, cols], zv_ref[:, cols], gv_ref[h:h + 1, :])
            wmf = ws_ref[h] * tril
            wm = wmf.astype(BF16)
            wmt = wmf.T.astype(BF16)
            vb = vln.astype(BF16)
            dws = jnp.zeros((HD, HD), F32)
            db = jnp.zeros((HD, 1), F32)
            du_parts, dvln_parts = [], []
            for c in range(tb // HD):
                rws = slice(c * HD, (c + 1) * HD)
                mixed = _dot(wm, vb[rws], NN) + bst_ref[:, h:h + 1]
                dyc = dy_ref[rws, cols]
                du_parts.append(dyc * mixed)
                dmixed = dyc * u[rws]
                dmb = dmixed.astype(BF16)
                dws = dws + _dot(dmb, vb[rws], NT)
                db = db + jnp.sum(dmixed, axis=1, keepdims=True)
                dvln_parts.append(_dot(wmt, dmb, NN))
            du = jnp.concatenate(du_parts, axis=0)
            dvln = jnp.concatenate(dvln_parts, axis=0)
            dzu, dzv, dgv = vjp((du, dvln))
            dz_ref[:, cols] = dzu.astype(BF16)
            dz_ref[:, slice(GW + h * HD, GW + (h + 1) * HD)] = dzv.astype(BF16)
            dgv_ref[0, h:h + 1, :] = dgv
            dws_ref[0, h] = dws * tril
            db_ref[0, h] = jnp.broadcast_to(db, (HD, LANES))

    blk = pl.BlockSpec((tb, GW), lambda i: (i, 0))
    return pl.pallas_call(
        body, name="gmlp_bwd", grid=(nb,),
        in_specs=[blk, pl.BlockSpec((tb, GW), lambda i: (i, 1)), blk,
                  pl.BlockSpec((NH, HD), lambda i: (0, 0)), pl.BlockSpec((NH, HD, HD), lambda i: (0, 0, 0)),
                  pl.BlockSpec((HD, NH), lambda i: (0, 0))],
        out_specs=[pl.BlockSpec((tb, 2 * GW), lambda i: (i, 0)), pl.BlockSpec((1, NH, HD), lambda i: (i, 0, 0)),
                   pl.BlockSpec((1, NH, HD, HD), lambda i: (i, 0, 0, 0)),
                   pl.BlockSpec((1, NH, HD, LANES), lambda i: (i, 0, 0, 0))],
        out_shape=[jax.ShapeDtypeStruct((s, IN_COLS), BF16),
                   jax.ShapeDtypeStruct((nb, NH, HD), F32), jax.ShapeDtypeStruct((nb, NH, HD, HD), F32),
                   jax.ShapeDtypeStruct((nb, NH, HD, LANES), F32)],
        compiler_params=_cparams(),
    )(z, z, dy, gv, ws, bst)


def _pool_count(t0, window):
    pos = (t0 + lax.broadcasted_iota(jnp.int32, (ROW_TILE, LANES), 0)).astype(F32)
    return jnp.minimum(pos + 1.0, float(window))


def _window_sum(win, levels, back):
    n = win.shape[0]
    for lv in range(levels):
        step = 1 << lv
        win = win + pltpu.roll(win, n - step if back else step, 0)
    return win


def _pool_pooled(ppad_ref, t0, g):
    win = ppad_ref[pl.ds(t0, ROW_TILE + POOL_HALO), :]
    wsum = _window_sum(win, g + 1, False)[POOL_HALO:]
    return wsum / _pool_count(t0, POOL_WINDOWS[g]) - win[POOL_HALO:]


def _pool_fwd(z, wp, sp, y):
    s = z.shape[0]
    nt = s // ROW_TILE

    def body(p_ref, wp_ref, sp_ref, _, y_ref, ppad):
        for g in range(NG):
            cols = slice(g * LANES, (g + 1) * LANES)
            ppad[pl.ds(0, POOL_HALO), :] = jnp.zeros((POOL_HALO, LANES), F32)
            ppad[pl.ds(POOL_HALO, s), :] = p_ref[:, cols]
            wpb = wp_ref[g].astype(BF16)
            scale = sp_ref[:, cols]

            def tile(t, carry):
                t0 = pl.multiple_of(t * ROW_TILE, ROW_TILE)
                pooled = _pool_pooled(ppad, t0, g)
                y_ref[pl.ds(t0, ROW_TILE), cols] = (_dot(pooled.astype(BF16), wpb, NN) * scale).astype(BF16)
                return carry

            lax.fori_loop(0, nt, tile, 0)

    return pl.pallas_call(
        body, name="pool_fwd", grid=(1,),
        in_specs=[pl.BlockSpec((s, PW), lambda i: (0, 2 * GW // PW)),
                  pl.BlockSpec((NG, LANES, LANES), lambda i: (0, 0, 0)), pl.BlockSpec((1, PW), lambda i: (0, 0)), ANY],
        out_specs=pl.BlockSpec((s, PW), lambda i: (0, GW // PW)),
        out_shape=jax.ShapeDtypeStruct((s, D), BF16), input_output_aliases={3: 0},
        scratch_shapes=[pltpu.VMEM((s + POOL_HALO, LANES), F32)], compiler_params=_cparams(),
    )(z, wp, sp, y)


def _pool_bwd(z, dy, wp, sp, dz):
    s = z.shape[0]
    nt = s // ROW_TILE

    def body(p_ref, dy_ref, wp_ref, sp_ref, _, dp_ref, dwp_ref, dsp_ref, ppad, rpad, dpool):
        for g in range(NG):
            cols = slice(g * LANES, (g + 1) * LANES)
            ppad[pl.ds(0, POOL_HALO), :] = jnp.zeros((POOL_HALO, LANES), F32)
            ppad[pl.ds(POOL_HALO, s), :] = p_ref[:, cols]
            rpad[pl.ds(s, POOL_HALO), :] = jnp.zeros((POOL_HALO, LANES), F32)
            wpb = wp_ref[g].astype(BF16)
            scale = sp_ref[:, cols]

            def tile(t, carry):
                dwp, dsp = carry
                t0 = pl.multiple_of(t * ROW_TILE, ROW_TILE)
                pooled = _pool_pooled(ppad, t0, g)
                pb = pooled.astype(BF16)
                dyt = dy_ref[pl.ds(t0, ROW_TILE), cols]
                dsp = dsp + jnp.sum(dyt * _dot(pb, wpb, NN), axis=0, keepdims=True)
                dmm = (dyt * scale).astype(BF16)
                dwp = dwp + _dot(pb, dmm, TN)
                dpooled = _dot(dmm, wpb, NT)
                rpad[pl.ds(t0, ROW_TILE), :] = dpooled / _pool_count(t0, POOL_WINDOWS[g])
                dpool[pl.ds(t0, ROW_TILE), :] = dpooled
                return dwp, dsp

            dwp, dsp = lax.fori_loop(0, nt, tile, (jnp.zeros((LANES, LANES), F32), jnp.zeros((1, LANES), F32)))
            dwp_ref[g] = dwp
            dsp_ref[:, cols] = dsp

            def tile2(t, carry):
                t0 = pl.multiple_of(t * ROW_TILE, ROW_TILE)
                win = rpad[pl.ds(t0, ROW_TILE + POOL_HALO), :]
                back = _window_sum(win, g + 1, True)[:ROW_TILE]
                rows = pl.ds(t0, ROW_TILE)
                dp_ref[rows, cols] = (back - dpool[rows, :]).astype(BF16)
                return carry

            lax.fori_loop(0, nt, tile2, 0)

    return pl.pallas_call(
        body, name="pool_bwd", grid=(1,),
        in_specs=[pl.BlockSpec((s, PW), lambda i: (0, 2 * GW // PW)), pl.BlockSpec((s, PW), lambda i: (0, GW // PW)),
                  pl.BlockSpec((NG, LANES, LANES), lambda i: (0, 0, 0)), pl.BlockSpec((1, PW), lambda i: (0, 0)), ANY],
        out_specs=[pl.BlockSpec((s, PW), lambda i: (0, 2 * GW // PW)),
                   pl.BlockSpec((NG, LANES, LANES), lambda i: (0, 0, 0)), pl.BlockSpec((1, PW), lambda i: (0, 0))],
        out_shape=[jax.ShapeDtypeStruct((s, IN_COLS), BF16), jax.ShapeDtypeStruct((NG, LANES, LANES), F32),
                   jax.ShapeDtypeStruct((1, PW), F32)],
        input_output_aliases={4: 0},
        scratch_shapes=[pltpu.VMEM((s + POOL_HALO, LANES), F32), pltpu.VMEM((s + POOL_HALO, LANES), F32),
                        pltpu.VMEM((s, LANES), F32)],
        compiler_params=_cparams(),
    )(z, dy, wp, sp, dz)


CONV_LEAD = CONV_HALO - (CONV_K - 1)


def _conv_taps(win, wdw_ref, lead, reverse):
    n = win.shape[0]
    acc = jnp.zeros((ROW_TILE, CW), F32)
    for j in range(CONV_K):
        tap = (CONV_K - 1 - j) if reverse else j
        acc = acc + wdw_ref[tap:tap + 1, :] * pltpu.roll(win, (n - (lead + j)) % n, 0)[:ROW_TILE]
    return acc


def _conv_fill_glu(cv_ref, cg_ref, xpad, s):
    xpad[pl.ds(0, CONV_HALO), :] = jnp.zeros((CONV_HALO, CW), F32)

    def fill(t, carry):
        t0 = pl.multiple_of(t * ROW_TILE, ROW_TILE)
        rows = pl.ds(t0, ROW_TILE)
        xpad[pl.ds(t0 + CONV_HALO, ROW_TILE), :] = _glu(cv_ref[rows, :], cg_ref[rows, :])
        return carry

    lax.fori_loop(0, s // ROW_TILE, fill, 0)


def _conv_fwd(z, wdw, bdw, lng, lnb, y):
    s = z.shape[0]

    def body(cv_ref, cg_ref, wdw_ref, bdw_ref, lng_ref, lnb_ref, _, y_ref, xpad):
        _conv_fill_glu(cv_ref, cg_ref, xpad, s)

        def tile(t, carry):
            t0 = pl.multiple_of(t * ROW_TILE, ROW_TILE)
            win = xpad[pl.ds(t0, ROW_TILE + CONV_HALO), :]
            hc = _conv_taps(win, wdw_ref, CONV_LEAD, False) + bdw_ref[...]
            y_ref[pl.ds(t0, ROW_TILE), :] = _ln_silu(hc, lng_ref[...], lnb_ref[...]).astype(BF16)
            return carry

        lax.fori_loop(0, s // ROW_TILE, tile, 0)

    vec = pl.BlockSpec((1, CW), lambda i: (0, 0))
    return pl.pallas_call(
        body, name="conv_fwd", grid=(1,),
        in_specs=[pl.BlockSpec((s, CW), lambda i: (0, (2 * GW + PW) // CW)),
                  pl.BlockSpec((s, CW), lambda i: (0, (2 * GW + PW) // CW + 1)),
                  pl.BlockSpec((CONV_K + 1, CW), lambda i: (0, 0)), vec, vec, vec, ANY],
        out_specs=pl.BlockSpec((s, CW), lambda i: (0, (GW + PW) // CW)),
        out_shape=jax.ShapeDtypeStruct((s, D), BF16), input_output_aliases={6: 0},
        scratch_shapes=[pltpu.VMEM((s + CONV_HALO, CW), F32)], compiler_params=_cparams(),
    )(z, z, wdw, bdw, lng, lnb, y)


def _conv_bwd(z, dy, wdw, bdw, lng, lnb, dz):
    s = z.shape[0]

    def body(cv_ref, cg_ref, dy_ref, wdw_ref, bdw_ref, lng_ref, lnb_ref, _,
             dz_ref, dwdw_ref, dbdw_ref, dlng_ref, dlnb_ref, xpad, dpad, dcg_keep):
        @pl.when(pl.program_id(0) == 0)
        def _():
            compute(cv_ref, cg_ref, dy_ref, wdw_ref, bdw_ref, lng_ref, lnb_ref,
                    dz_ref, dcg_keep, dwdw_ref, dbdw_ref, dlng_ref, dlnb_ref, xpad, dpad)

        @pl.when(pl.program_id(0) == 1)
        def _():
            dz_ref[...] = dcg_keep[...]

    def compute(cv_ref, cg_ref, dy_ref, wdw_ref, bdw_ref, lng_ref, lnb_ref,
                dcv_ref, dcg_ref, dwdw_ref, dbdw_ref, dlng_ref, dlnb_ref, xpad, dpad):
        _conv_fill_glu(cv_ref, cg_ref, xpad, s)
        dpad[pl.ds(s, CONV_HALO), :] = jnp.zeros((CONV_HALO, CW), F32)
        dwdw_ref[...] = jnp.zeros((CONV_K + 1, CW), F32)

        def tile(t, carry):
            db, dg, dbeta = carry
            t0 = pl.multiple_of(t * ROW_TILE, ROW_TILE)
            win = xpad[pl.ds(t0, ROW_TILE + CONV_HALO), :]
            hc = _conv_taps(win, wdw_ref, CONV_LEAD, False) + bdw_ref[...]
            _, vjp = jax.vjp(_ln_silu, hc, lng_ref[...], lnb_ref[...])
            dhc, dg_t, dbeta_t = vjp(dy_ref[pl.ds(t0, ROW_TILE), :])
            dpad[pl.ds(t0, ROW_TILE), :] = dhc
            n = win.shape[0]
            for j in range(CONV_K):
                shifted = pltpu.roll(win, (n - (CONV_LEAD + j)) % n, 0)[:ROW_TILE]
                dwdw_ref[j:j + 1, :] += jnp.sum(dhc * shifted, axis=0, keepdims=True)
            return db + jnp.sum(dhc, axis=0, keepdims=True), dg + dg_t, dbeta + dbeta_t

        zero = jnp.zeros((1, CW), F32)
        db, dg, dbeta = lax.fori_loop(0, s // ROW_TILE, tile, (zero, zero, zero))
        dbdw_ref[...] = db
        dlng_ref[...] = dg
        dlnb_ref[...] = dbeta

        def tile2(t, carry):
            t0 = pl.multiple_of(t * ROW_TILE, ROW_TILE)
            rows = pl.ds(t0, ROW_TILE)
            win = dpad[pl.ds(t0, ROW_TILE + CONV_HALO), :]
            dglu = _conv_taps(win, wdw_ref, 0, True)
            _, vjp = jax.vjp(_glu, cv_ref[rows, :], cg_ref[rows, :])
            dcv, dcg = vjp(dglu)
            dcv_ref[rows, :] = dcv.astype(BF16)
            dcg_ref[rows, :] = dcg.astype(BF16)
            return carry

        lax.fori_loop(0, s // ROW_TILE, tile2, 0)

    vec = pl.BlockSpec((1, CW), lambda i: (0, 0))
    wspec = pl.BlockSpec((CONV_K + 1, CW), lambda i: (0, 0))
    vshape = jax.ShapeDtypeStruct((1, CW), F32)
    return pl.pallas_call(
        body, name="conv_bwd", grid=(2,),
        in_specs=[pl.BlockSpec((s, CW), lambda i: (0, (2 * GW + PW) // CW)),
                  pl.BlockSpec((s, CW), lambda i: (0, (2 * GW + PW) // CW + 1)),
                  pl.BlockSpec((s, CW), lambda i: (0, (GW + PW) // CW)), wspec, vec, vec, vec, ANY],
        out_specs=[pl.BlockSpec((s, CW), lambda i: (0, (2 * GW + PW) // CW + i)), wspec, vec, vec, vec],
        out_shape=[jax.ShapeDtypeStruct((s, IN_COLS), BF16), jax.ShapeDtypeStruct((CONV_K + 1, CW), F32),
                   vshape, vshape, vshape],
        input_output_aliases={7: 0},
        scratch_shapes=[pltpu.VMEM((s + CONV_HALO, CW), F32), pltpu.VMEM((s + CONV_HALO, CW), F32),
                        pltpu.VMEM((s, CW), BF16)],
        compiler_params=_cparams(),
    )(z, z, dy, wdw, bdw, lng, lnb, dz)


def _softmax_rows(sc):
    e = jnp.exp(sc - jnp.max(sc, axis=-1, keepdims=True))
    return e / jnp.sum(e, axis=-1, keepdims=True)


def _attn_fwd(q, k, v, tq):
    s, m = q.shape[0], k.shape[0]
    tq = min(tq, s)

    def body(q_ref, k_ref, v_ref, o_ref):
        for h in range(XH):
            cols = slice(h * XHD, (h + 1) * XHD)
            p = _softmax_rows(_dot(q_ref[:, cols], k_ref[:, cols], NT) * ATT_SCALE)
            o_ref[:, cols] = _dot(p.astype(BF16), v_ref[:, cols], NN).astype(BF16)

    kv = pl.BlockSpec((m, D), lambda i: (0, 0))
    return pl.pallas_call(
        body, name="attn_fwd", grid=(s // tq,),
        in_specs=[pl.BlockSpec((tq, D), lambda i: (i, 0)), kv, kv],
        out_specs=pl.BlockSpec((tq, D), lambda i: (i, 0)),
        out_shape=jax.ShapeDtypeStruct((s, D), BF16), compiler_params=_cparams(),
    )(q, k, v)


def _attn_bwd(q, k, v, do, tq):
    s, m = q.shape[0], k.shape[0]
    tq = min(tq, s)

    def body(q_ref, k_ref, v_ref, do_ref, dq_ref, dk_ref, dv_ref):
        @pl.when(pl.program_id(0) == 0)
        def _():
            dk_ref[...] = jnp.zeros_like(dk_ref)
            dv_ref[...] = jnp.zeros_like(dv_ref)

        for h in range(XH):
            cols = slice(h * XHD, (h + 1) * XHD)
            qh, kh, vh, doh = q_ref[:, cols], k_ref[:, cols], v_ref[:, cols], do_ref[:, cols]
            p = _softmax_rows(_dot(qh, kh, NT) * ATT_SCALE)
            dp = _dot(doh, vh, NT)
            dv_ref[:, cols] += _dot(p.astype(BF16), doh, TN)
            ds = (p * (dp - jnp.sum(p * dp, axis=-1, keepdims=True)) * ATT_SCALE).astype(BF16)
            dq_ref[:, cols] = _dot(ds, kh, NN).astype(BF16)
            dk_ref[:, cols] += _dot(ds, qh, TN)

    kv = pl.BlockSpec((m, D), lambda i: (0, 0))
    qs = pl.BlockSpec((tq, D), lambda i: (i, 0))
    return pl.pallas_call(
        body, name="attn_bwd", grid=(s // tq,),
        in_specs=[qs, kv, kv, qs], out_specs=[qs, kv, kv],
        out_shape=[jax.ShapeDtypeStruct((s, D), BF16), jax.ShapeDtypeStruct((m, D), F32),
                   jax.ShapeDtypeStruct((m, D), F32)],
        compiler_params=_cparams(),
    )(q, k, v, do)


def _loss_head(y, target, tm):
    s = y.shape[0]
    tm = min(tm, s)

    def body(y_ref, t_ref, dy_ref, part_ref):
        err = y_ref[...] - t_ref[...]
        dy_ref[...] = err * (1.0 / D)
        part_ref[...] = jnp.full((1, 8, LANES), 0.5 * jnp.sum(err * err) * (1.0 / D), F32)

    blk = pl.BlockSpec((tm, D), lambda i: (i, 0))
    return pl.pallas_call(
        body, name="loss_head", grid=(s // tm,), in_specs=[blk, blk],
        out_specs=[blk, pl.BlockSpec((1, 8, LANES), lambda i: (i, 0, 0))],
        out_shape=[jax.ShapeDtypeStruct((s, D), F32), jax.ShapeDtypeStruct((s // tm, 8, LANES), F32)],
        compiler_params=_cparams(),
    )(y, target)


def _layer_fwd(x0, mem, w, p, fetch):
    z, hn0 = _rowop_mm("mix_in", "rms", (x0,), p["norm_mix_pre"], w["w_in"], NT, F32)
    y = _gmlp_fwd(z, p["gmlp_v_gain"], p["w_spatial"], p["b_spatial_t"], 512)
    y = _pool_fwd(z, p["w_pool"], p["s_pool"], y)
    y = _conv_fwd(z, p["w_dw"], p["b_dw"], p["conv_ln_g"], p["conv_ln_b"], y)
    w.update(fetch("att", (y,)))
    x1, h0 = _mm_rowop("mix_out", "rms_res", [(y, w["w_out"], NN)], (x0,), p["norm_mix_post"])
    q, hn1 = _rowop_mm("att_q", "rms", (x1,), p["norm_xattn_pre"], w["w_q"], NN, BF16)
    k, mn = _rowop_mm("att_k", "rms", (mem,), p["norm_mem"], w["w_k"], NN, BF16)
    v, _ = _rowop_mm("att_v", "rms", (mem,), p["norm_mem"], w["w_v"], NN, BF16)
    o = _attn_fwd(q, k, v, 256)
    x2, h1 = _mm_rowop("att_o", "rms_res", [(o, w["w_o"], NN)], (x1,), p["norm_xattn_post"])
    w.update(fetch("ffn", (x2,)))
    u, hn2 = _rowop_mm("ffn_up", "rms", (x2,), p["norm_ffn_pre"], w["w_up"], NT, F32)
    x3, h2 = _mm_rowop("ffn_down", "rms_res", [(u, w["w_down"], NN)], (x2,), p["norm_ffn_post"], relu2=True)
    saved = dict(x0=x0, z=z, hn0=hn0, y=y, h0=h0, x1=x1, q=q, hn1=hn1, k=k, v=v, mn=mn, o=o, h1=h1, x2=x2, u=u,
                 hn2=hn2, h2=h2)
    return x3, saved


def _layer_bwd(dx3, mem, w, p, sv, red):
    gs = {}
    du, dh2, dg = _rowop_mm("ffn_down_bwd", "rms_bwd", (sv["h2"], dx3), p["norm_ffn_post"], w["w_down"], NT, BF16,
                            u=sv["u"], after=red.after())
    gs["norm_ffn_post"] = jnp.sum(dg, axis=0)
    g_down = _mm_tn("ffn_down_dw", sv["u"], dh2, relu2=True)
    red.advance((g_down,))
    dx2, dg = _mm_rowop("ffn_up_bwd", "rms_bwd_res", [(du, w["w_up"], NN)], (sv["x2"], dx3), p["norm_ffn_pre"],
                        after=red.after())
    gs["norm_ffn_pre"] = jnp.sum(dg, axis=0)
    g_up = _mm_tn("ffn_up_dw", du, sv["hn2"])
    red.add("ffn", ("w_down", "w_up"), [g_down, g_up])
    do, dh1, dg = _rowop_mm("att_o_bwd", "rms_bwd", (sv["h1"], dx2), p["norm_xattn_post"], w["w_o"], NT, BF16,
                            after=red.after())
    gs["norm_xattn_post"] = jnp.sum(dg, axis=0)
    g_o = _mm_tn("att_o_dw", sv["o"], dh1)
    red.advance((g_o,))
    dq, dk, dv = _attn_bwd(sv["q"], sv["k"], sv["v"], do, 256)
    dk, dv = dk.astype(BF16), dv.astype(BF16)
    dx1, dg = _mm_rowop("att_q_bwd", "rms_bwd_res", [(dq, w["w_q"], NT)], (sv["x1"], dx2), p["norm_xattn_pre"],
                        after=red.after())
    gs["norm_xattn_pre"] = jnp.sum(dg, axis=0)
    g_q = _mm_tn("att_q_dw", sv["hn1"], dq)
    g_k = _mm_tn("att_k_dw", sv["mn"], dk)
    g_v = _mm_tn("att_v_dw", sv["mn"], dv)
    (dg,) = _mm_rowop("att_kv_bwd", "rms_bwd_gain", [(dk, w["w_k"], NT), (dv, w["w_v"], NT)], (mem,), p["norm_mem"])
    gs["norm_mem"] = jnp.sum(dg, axis=0)
    red.add("att", ("w_o", "w_q", "w_k", "w_v"), [g_o, g_q, g_k, g_v])
    dy, dh0, dg = _rowop_mm("mix_out_bwd", "rms_bwd", (sv["h0"], dx1), p["norm_mix_post"], w["w_out"], NT, F32,
                            after=red.after())
    gs["norm_mix_post"] = jnp.sum(dg, axis=0)
    g_out = _mm_tn("mix_out_dw", sv["y"], dh0)
    red.advance((g_out,))
    red.add("out", ("w_out",), [g_out])
    z = sv["z"]
    dz, dgv, dws, dbs = _gmlp_bwd(z, dy, p["gmlp_v_gain"], p["w_spatial"], p["b_spatial_t"], 512)
    gs["gmlp_v_gain"] = jnp.sum(dgv, axis=0)
    gs["w_spatial"] = jnp.sum(dws, axis=0)
    gs["b_spatial"] = jnp.sum(dbs[..., 0], axis=0)
    dz, gs["w_pool"], gs["s_pool"] = _pool_bwd(z, dy, p["w_pool"], p["s_pool"], dz)
    dz, dwdw, gs["b_dw"], gs["conv_ln_g"], gs["conv_ln_b"] = _conv_bwd(
        z, dy, p["w_dw"], p["b_dw"], p["conv_ln_g"], p["conv_ln_b"], dz)
    red.advance((dz,))
    g_in = _mm_tn("mix_in_dw", dz, sv["hn0"], after=red.after())
    red.add("in", ("w_in",), [g_in])
    red.advance((g_in,))
    dx0, dg = _mm_rowop("mix_in_bwd", "rms_bwd_res", [(dz, w["w_in"], NN)], (sv["x0"], dx1), p["norm_mix_pre"],
                        after=red.after())
    gs["norm_mix_pre"] = jnp.sum(dg, axis=0)
    return dx0, _small_grad_arrays(gs, dwdw)


NORM_NAMES = ("norm_mix_pre", "norm_mix_post", "norm_xattn_pre", "norm_mem", "norm_xattn_post", "norm_ffn_pre",
              "norm_ffn_post")
VEC_NAMES = ("s_pool", "b_dw", "conv_ln_g", "conv_ln_b")
SMALL_ARRAYS = ("norms", "gain_bias", "w_spatial", "w_pool", "vecs", "w_dw")


def _small_grad_arrays(gs, dwdw):
    return {"norms": jnp.concatenate([gs[n] for n in NORM_NAMES], axis=0),
            "gain_bias": jnp.concatenate([gs["gmlp_v_gain"], gs["b_spatial"]], axis=0),
            "w_spatial": gs["w_spatial"], "w_pool": gs["w_pool"],
            "vecs": jnp.concatenate([gs[n] for n in VEC_NAMES], axis=0), "w_dw": dwdw}


def _split_small_grads(arrays):
    out = {n: arrays["norms"][k] for k, n in enumerate(NORM_NAMES)}
    out.update({n: arrays["vecs"][k] for k, n in enumerate(VEC_NAMES)})
    out.update(gmlp_v_gain=arrays["gain_bias"][:NH], b_spatial=arrays["gain_bias"][NH:], w_spatial=arrays["w_spatial"],
               w_pool=arrays["w_pool"], w_dw=arrays["w_dw"][:CONV_K])
    return out


def _layer_params(small, l):
    p = {n: small[n][l].reshape(1, -1) for n in ("norm_mix_pre", "norm_mix_post", "s_pool", "b_dw", "conv_ln_g",
                                                   "conv_ln_b", "norm_xattn_pre", "norm_mem", "norm_xattn_post",
                                                   "norm_ffn_pre", "norm_ffn_post")}
    p["gmlp_v_gain"] = small["gmlp_v_gain"][l]
    p["w_spatial"] = small["w_spatial"][l]
    p["b_spatial_t"] = small["b_spatial"][l].T
    p["w_pool"] = small["w_pool"][l]
    p["w_dw"] = jnp.pad(small["w_dw"][l], ((0, 1), (0, 0)))
    return p


def _local_step(x, mem, target, fetch, small, red):
    small = dict(small)
    saved, weights, params = [], [], []
    h = x
    marker = ()
    for l in range(DEPTH):
        w = fetch(l, "in", marker)
        if "taps" in w:
            small["w_dw"] = w.pop("taps")
        p = _layer_params(small, l)
        h, sv = _layer_fwd(h, mem, w, p, functools.partial(fetch, l))
        marker = (h,)
        saved.append(sv)
        weights.append(w)
        params.append(p)
    dh, loss = _loss_head(h, target, 512)
    gsmall = [None] * DEPTH
    for l in reversed(range(DEPTH)):
        red.layer = l
        dh, gsmall[l] = _layer_bwd(dh, mem, weights[l], params[l], saved[l], red)
    return loss, dh, gsmall


HBM = pl.BlockSpec(memory_space=pltpu.HBM)


def _position():
    return lax.axis_index("x"), lax.axis_index("y"), lax.axis_index("c")


SEM = pl.BlockSpec(memory_space=pltpu.SEMAPHORE)
EFFECT = pltpu.SideEffectType.DATAFLOW_SIDE_EFFECTING
TOKEN = jax.ShapeDtypeStruct((8, LANES), F32)
TOKEN_SPEC = pl.BlockSpec(memory_space=pltpu.VMEM)


def _landing(shape, dtype):
    return pltpu.with_memory_space_constraint(lax.empty(shape, dtype), pltpu.HBM)


def _hbm_shapes(arrays):
    return [pltpu.HBM(a.shape, a.dtype) for a in arrays]


def _block(ref, r, dev):
    return ref.at[pl.ds((4 * dev[0] + 2 * dev[1] + dev[2]) * r, r), :]


def _split_call(name, body, thru, sems_in, after, sems_out, token):
    n = len(thru)
    out_shape = [pltpu.SemaphoreType.DMA(s) for s in sems_out] + _hbm_shapes(thru) + ([TOKEN] if token else [])
    out_specs = [SEM] * len(sems_out) + [HBM] * n + ([TOKEN_SPEC] if token else [])
    return pl.pallas_call(
        body, name=name, in_specs=[HBM] * n + [SEM] * len(sems_in) + [ANY] * len(after),
        out_specs=out_specs, out_shape=out_shape,
        input_output_aliases={i: len(sems_out) + i for i in range(n)},
        compiler_params=pltpu.CompilerParams(has_side_effects=EFFECT),
    )(*thru, *sems_in, *after)


def _place_own(name, srcs, dev, out_dtype, tr):
    n = len(srcs)
    r, cols = srcs[0][0].shape[-2:]
    tr = r if r < 16 else _row_tile(r, tr)
    nb = r // tr

    def body(dev_ref, *refs):
        for a in range(n):
            refs[n + a][...] = refs[a][...].astype(out_dtype)

    in_specs = [pl.BlockSpec((tr, cols), lambda i, d: (i, 0)) if l is None
                else pl.BlockSpec((None, tr, cols), lambda i, d, l=l: (l, i, 0)) for _, l in srcs]
    return pl.pallas_call(
        body, name=name,
        grid_spec=pltpu.PrefetchScalarGridSpec(
            num_scalar_prefetch=1, grid=(nb,), in_specs=in_specs,
            out_specs=[pl.BlockSpec((tr, cols), lambda i, d: (d[0] * nb + i, 0))] * n),
        out_shape=[jax.ShapeDtypeStruct((N_DEV * r, cols), out_dtype)] * n, compiler_params=_cparams(),
    )(dev, *[a for a, _ in srcs])


def _gather_peers(x, y, c):
    return [(1 - x, y, c), (x, 1 - y, c), (1 - x, 1 - y, c), (x, y, 1 - c)]


def _block_rows(land):
    return land.shape[0] // N_DEV


def _gather_start(name, lands, after):
    n = len(lands)

    def body(*refs):
        lz = refs[:n]
        send_sems, recv_sems = refs[n + len(after)], refs[n + len(after) + 1]
        token = refs[-1]
        x, y, c = _position()
        for a in range(n):
            own = _block(lz[a], _block_rows(lands[a]), (x, y, c))
            for k, to in enumerate(_gather_peers(x, y, c)):
                pltpu.make_async_remote_copy(src_ref=own, dst_ref=own, send_sem=send_sems.at[k], recv_sem=recv_sems.at[k],
                                             device_id=to, device_id_type=MESH).start()
        token[...] = jnp.zeros_like(token)

    out = _split_call(name, body, list(lands), [], after, [(4,), (4,)], True)
    return out[0], out[1], out[2:2 + n], out[-1]


def _gather_forward(name, lands, recv_sems, after):
    n = len(lands)

    def body(*refs):
        lz = refs[:n]
        recv0 = refs[n]
        fsend, frecv = refs[n + 1 + len(after)], refs[n + 2 + len(after)]
        token = refs[-1]
        x, y, c = _position()
        chips = _gather_peers(x, y, c)[:3]
        for a in range(n):
            for j, chip in enumerate(chips):
                blk = _block(lz[a], _block_rows(lands[a]), chip)
                pltpu.make_async_remote_copy(src_ref=blk, dst_ref=blk, send_sem=fsend.at[j], recv_sem=recv0.at[j],
                                             device_id=(x, y, c), device_id_type=MESH).wait_recv()
        for a in range(n):
            for j, chip in enumerate(chips):
                blk = _block(lz[a], _block_rows(lands[a]), chip)
                pltpu.make_async_remote_copy(src_ref=blk, dst_ref=blk, send_sem=fsend.at[j], recv_sem=frecv.at[j],
                                             device_id=(x, y, 1 - c), device_id_type=MESH).start()
        token[...] = jnp.zeros_like(token)

    out = _split_call(name, body, list(lands), [recv_sems], after, [(3,), (3,)], True)
    return out[0], out[1], out[2:2 + n], out[-1]


def _gather_finish(name, lands, send_sems, recv_sems, fsend, frecv, after):
    n = len(lands)

    def body(*refs):
        lz = refs[:n]
        send0, recv0, fsend_ref, frecv_ref = refs[n:n + 4]
        x, y, c = _position()
        me = (x, y, c)
        chips = _gather_peers(x, y, c)[:3]
        for a in range(n):
            r = _block_rows(lands[a])
            sib = _block(lz[a], r, (x, y, 1 - c))
            pltpu.make_async_remote_copy(src_ref=sib, dst_ref=sib, send_sem=send0.at[3], recv_sem=recv0.at[3],
                                         device_id=me, device_id_type=MESH).wait_recv()
            for j, chip in enumerate(chips):
                blk = _block(lz[a], r, (chip[0], chip[1], 1 - c))
                pltpu.make_async_remote_copy(src_ref=blk, dst_ref=blk, send_sem=fsend_ref.at[j], recv_sem=frecv_ref.at[j],
                                             device_id=me, device_id_type=MESH).wait_recv()
            own = _block(lz[a], r, me)
            for k in range(4):
                pltpu.make_async_remote_copy(src_ref=own, dst_ref=own, send_sem=send0.at[k], recv_sem=recv0.at[k],
                                             device_id=me, device_id_type=MESH).wait_send()
            for j, chip in enumerate(chips):
                blk = _block(lz[a], r, chip)
                pltpu.make_async_remote_copy(src_ref=blk, dst_ref=blk, send_sem=fsend_ref.at[j], recv_sem=frecv_ref.at[j],
                                             device_id=me, device_id_type=MESH).wait_send()

    return _split_call(name, body, list(lands), [send_sems, recv_sems, fsend, frecv], after, [], False)


def _sibling_start(name, grads, after):
    n = len(grads)
    lands = [_landing((4, g.shape[0] // N_DEV, D), g.dtype) for g in grads]

    def body(*refs):
        ins, lz = refs[:n], refs[n:2 * n]
        send_sem, recv_sem = refs[2 * n + len(after)], refs[2 * n + len(after) + 1]
        token = refs[-1]
        x, y, c = _position()
        for a in range(n):
            r = grads[a].shape[0] // N_DEV
            for q in range(4):
                pltpu.make_async_remote_copy(
                    src_ref=ins[a].at[pl.ds((2 * q + 1 - c) * r, r), :], dst_ref=lz[a].at[q], send_sem=send_sem.at[0],
                    recv_sem=recv_sem.at[0], device_id=(x, y, 1 - c), device_id_type=MESH).start()
        token[...] = jnp.zeros_like(token)

    out = _split_call(name, body, list(grads) + lands, [], after, [(1,), (1,)], True)
    return out[0], out[1], out[2:2 + n], out[2 + n:2 + 2 * n], out[-1]


def _sibling_finish(name, grads, lands, send_sem, recv_sem, after):
    n = len(grads)

    def body(*refs):
        ins, lz = refs[:n], refs[n:2 * n]
        send_ref, recv_ref = refs[2 * n], refs[2 * n + 1]
        x, y, c = _position()
        for a in range(n):
            r = grads[a].shape[0] // N_DEV
            for q in range(4):
                cp = pltpu.make_async_remote_copy(
                    src_ref=ins[a].at[pl.ds((2 * q + 1 - c) * r, r), :], dst_ref=lz[a].at[q], send_sem=send_ref.at[0],
                    recv_sem=recv_ref.at[0], device_id=(x, y, c), device_id_type=MESH)
                cp.wait_send()
                cp.wait_recv()

    out = _split_call(name, body, list(grads) + list(lands), [send_sem, recv_sem], after, [], False)
    return out[:n], out[n:2 * n]


def _chip_start(name, parts, after):
    n = len(parts)
    lands = [_landing((3,) + p.shape[1:], p.dtype) for p in parts]

    def body(*refs):
        ins, lz = refs[:n], refs[n:2 * n]
        send_sems, recv_sems = refs[2 * n + len(after)], refs[2 * n + len(after) + 1]
        token = refs[-1]
        x, y, c = _position()
        for a in range(n):
            for j, chip in enumerate(_gather_peers(x, y, c)[:3]):
                pltpu.make_async_remote_copy(
                    src_ref=ins[a].at[2 * chip[0] + chip[1]], dst_ref=lz[a].at[j], send_sem=send_sems.at[j],
                    recv_sem=recv_sems.at[j], device_id=chip, device_id_type=MESH).start()
        token[...] = jnp.zeros_like(token)

    out = _split_call(name, body, list(parts) + lands, [], after, [(3,), (3,)], True)
    return out[0], out[1], out[2:2 + n], out[2 + n:2 + 2 * n], out[-1]


def _chip_finish(name, parts, lands, send_sems, recv_sems, after):
    n = len(parts)

    def body(*refs):
        ins, lz = refs[:n], refs[n:2 * n]
        send_ref, recv_ref = refs[2 * n], refs[2 * n + 1]
        me = _position()
        for a in range(n):
            for j in range(3):
                cp = pltpu.make_async_remote_copy(
                    src_ref=ins[a].at[j], dst_ref=lz[a].at[j], send_sem=send_ref.at[j], recv_sem=recv_ref.at[j],
                    device_id=me, device_id_type=MESH)
                cp.wait_send()
                cp.wait_recv()

    out = _split_call(name, body, list(parts) + list(lands), [send_sems, recv_sems], after, [], False)
    return out[:n], out[n:2 * n]


def _other_devices(x, y, c):
    return [(x + (k >> 2 & 1) * (1 - 2 * x), y + (k >> 1 & 1) * (1 - 2 * y), c + (k & 1) * (1 - 2 * c))
            for k in range(1, N_DEV)]


def _broadcast_start(name, arrays, after):
    n = len(arrays)
    lands = [_landing((N_DEV,) + a.shape, a.dtype) for a in arrays]

    def body(*refs):
        ins, lz = refs[:n], refs[n:2 * n]
        send_sems, recv_sems = refs[2 * n + len(after)], refs[2 * n + len(after) + 1]
        token = refs[-1]
        x, y, c = _position()
        for a in range(n):
            for k, peer in enumerate(_other_devices(x, y, c)):
                pltpu.make_async_remote_copy(
                    src_ref=ins[a], dst_ref=lz[a].at[4 * x + 2 * y + c], send_sem=send_sems.at[k],
                    recv_sem=recv_sems.at[k], device_id=peer, device_id_type=MESH).start()
        token[...] = jnp.zeros_like(token)

    out = _split_call(name, body, list(arrays) + lands, [], after, [(N_DEV - 1,), (N_DEV - 1,)], True)
    return out[0], out[1], out[2:2 + n], out[2 + n:2 + 2 * n], out[-1]


def _broadcast_finish(name, arrays, lands, send_sems, recv_sems, after):
    n = len(arrays)

    def body(*refs):
        ins, lz = refs[:n], refs[n:2 * n]
        send_ref, recv_ref = refs[2 * n], refs[2 * n + 1]
        x, y, c = _position()
        for a in range(n):
            for k, peer in enumerate(_other_devices(x, y, c)):
                cp = pltpu.make_async_remote_copy(
                    src_ref=ins[a], dst_ref=lz[a].at[4 * peer[0] + 2 * peer[1] + peer[2]], send_sem=send_ref.at[k],
                    recv_sem=recv_ref.at[k], device_id=(x, y, c), device_id_type=MESH)
                cp.wait_send()
                cp.wait_recv()

    out = _split_call(name, body, list(arrays) + list(lands), [send_sems, recv_sems], after, [], False)
    return out[:n], out[n:2 * n]


def _row_tile(r, target):
    return max(t for t in range(16, min(r, target) + 1, 16) if r % t == 0)


def _chip_partial(name, grad, got, c, tr):
    r = grad.shape[0] // N_DEV
    tr = _row_tile(r, tr)
    g4 = grad.reshape(4, 2, r, D)

    def body(c_ref, g_ref, s_ref, o_ref):
        o_ref[...] = (g_ref[...].astype(F32) + s_ref[...].astype(F32)).astype(BF16)

    return pl.pallas_call(
        body, name=name,
        grid_spec=pltpu.PrefetchScalarGridSpec(
            num_scalar_prefetch=1, grid=(4, r // tr),
            in_specs=[pl.BlockSpec((None, None, tr, D), lambda q, i, c_ref: (q, c_ref[0], i, 0)),
                      pl.BlockSpec((None, tr, D), lambda q, i, c_ref: (q, i, 0))],
            out_specs=pl.BlockSpec((None, tr, D), lambda q, i, c_ref: (q, i, 0))),
        out_shape=jax.ShapeDtypeStruct((4, r, D), BF16), compiler_params=_cparams(),
    )(c, g4, got)


class _WeightGather:
    def __init__(self, groups):
        self.state, token = {}, ()
        for key, names, lands in groups:
            send, recv, lz, tok = _gather_start("gather_start_%s_%d" % key[::-1], lands, token)
            self.state[key] = (names, send, recv, lz)
            token = (tok,)
        self.started = token

    def fetch(self, layer, group, marker):
        names, send, recv, lz = self.state.pop((layer, group))
        tag = "%s_%d" % (group, layer)
        fsend, frecv, lz, tok = _gather_forward("gather_forward_" + tag, lz, recv, marker or self.started)
        lz = _gather_finish("gather_finish_" + tag, lz, send, recv, fsend, frecv, (tok,))
        return dict(zip(names, lz))


class _GradReduce:
    def __init__(self, core, chip):
        self.core, self.chip = core, chip
        self.layer = None
        self.token = ()
        self.at_sibling, self.at_chips = [], []

    def after(self):
        return self.token

    def add(self, group, names, grads):
        tag = "%s_%d" % (group, self.layer)
        send, recv, grads, lands, tok = _sibling_start("grad_sibling_start_" + tag, grads, self.token)
        self.at_sibling.append((tag, [(self.layer, n) for n in names], send, recv, grads, lands))
        self.token = (tok,)

    def advance(self, marker):
        for tag, keys, send, recv, grads, lands in self.at_sibling:
            grads, lands = _sibling_finish("grad_sibling_finish_" + tag, grads, lands, send, recv, marker)
            parts = [_chip_partial("chip_partial_%d_%s" % key, g, got, self.core, 256)
                     for key, g, got in zip(keys, grads, lands)]
            send, recv, parts, lands, tok = _chip_start("grad_chip_start_" + tag, parts, ())
            self.at_chips.append([tag, keys, send, recv, parts, lands])
            self.token = (tok,)
        self.at_sibling = []

    def collect(self, key, marker):
        for entry in self.at_chips:
            tag, keys, send, recv, parts, lands = entry
            if key in keys:
                if send is not None:
                    parts, lands = _chip_finish("grad_chip_finish_" + tag, parts, lands, send, recv, marker)
                    entry[2:] = [None, None, parts, lands]
                i = keys.index(key)
                return parts[i], lands[i]
        raise KeyError(key)


def _adamw_math(w, g, m, v):
    m = ADAM_B1 * m + (1.0 - ADAM_B1) * g
    v = ADAM_B2 * v + (1.0 - ADAM_B2) * jnp.square(g)
    m_hat = m / (1.0 - ADAM_B1 ** ADAM_STEP)
    v_hat = v / (1.0 - ADAM_B2 ** ADAM_STEP)
    delta = -ADAM_LR * (m_hat / (jnp.sqrt(v_hat) + ADAM_EPS) + ADAM_WD * w)
    return delta, m, v


def _adamw_small(wts, mom_m, mom_v, own, gathered, loss_own, loss_gathered, dev):
    names = SMALL
    nw = len(names)
    na = len(SMALL_ARRAYS)

    def body(dev_ref, *refs):
        w_refs, m_refs, v_refs = (dict(zip(names, refs[i * nw:(i + 1) * nw])) for i in range(3))
        own_refs = refs[3 * nw:3 * nw + DEPTH * na]
        g_refs = refs[3 * nw + DEPTH * na:3 * nw + 2 * DEPTH * na]
        loss_own_ref, loss_got_ref = refs[3 * nw + 2 * DEPTH * na:3 * nw + 2 * DEPTH * na + 2]
        outs = refs[3 * nw + 2 * DEPTH * na + 2:]
        g_out, d_out, m_out, v_out = (dict(zip(names, outs[i * nw:(i + 1) * nw])) for i in range(4))
        me = dev_ref[0]

        loss = None
        for d in range(N_DEV):
            for b in range(loss_own.shape[0]):
                term = jnp.where(me == d, loss_own_ref[b], loss_got_ref[d, b])
                loss = term if loss is None else loss + term
        outs[4 * nw][...] = loss

        def update(name, at, g):
            g_out[name][at] = g
            d_out[name][at], m_out[name][at], v_out[name][at] = _adamw_math(
                w_refs[name][at], g, m_refs[name][at], v_refs[name][at])

        for l in range(DEPTH):
            mine = dict(zip(SMALL_ARRAYS, own_refs[l * na:(l + 1) * na]))
            got = dict(zip(SMALL_ARRAYS, g_refs[l * na:(l + 1) * na]))

            def total(key, at):
                acc = None
                for d in range(N_DEV):
                    term = jnp.where(me == d, mine[key][at] if at else mine[key][...], got[key][(d,) + at])
                    acc = term if acc is None else acc + term
                return acc

            row = (slice(l, l + 1),)
            for k, name in enumerate(NORM_NAMES):
                update(name, row, total("norms", (slice(k, k + 1),)))
            for k, name in enumerate(VEC_NAMES):
                update(name, row, total("vecs", (slice(k, k + 1),)))
            update("gmlp_v_gain", (l,), total("gain_bias", (slice(0, NH),)))
            update("b_spatial", (l,), total("gain_bias", (slice(NH, 2 * NH),)))
            update("w_spatial", (l,), total("w_spatial", ()))
            update("w_pool", (l,), total("w_pool", ()))
            update("w_dw", (l,), total("w_dw", (slice(0, CONV_K),)))

    args = [src[n] for src in (wts, mom_m, mom_v) for n in names]
    args += [src[l][k] for src in (own, gathered) for l in range(DEPTH) for k in SMALL_ARRAYS]
    args += [loss_own, loss_gathered]
    outs = pl.pallas_call(
        body, name="adamw_small",
        in_specs=[pl.BlockSpec(memory_space=pltpu.SMEM)] + [pl.BlockSpec(memory_space=pltpu.VMEM)] * len(args),
        out_shape=[jax.ShapeDtypeStruct(wts[n].shape, F32) for _ in range(4) for n in names]
        + [jax.ShapeDtypeStruct((8, LANES), F32)],
        compiler_params=_cparams(),
    )(dev, *args)
    return tuple(dict(zip(names, outs[i * nw:(i + 1) * nw])) for i in range(4)) + (outs[4 * nw],)


def _adamw_layers(name, w, reduced, m, v, chip, tr):
    nl, r, cdim = w.shape
    tr = _row_tile(r, tr)
    nb = r // tr

    def body(q_ref, w_ref, p0_ref, g0_ref, p1_ref, g1_ref, m_ref, v_ref, g_ref, d_ref, nm_ref, nv_ref):
        def total(p_ref, got_ref):
            acc = p_ref[...].astype(F32)
            for j in range(3):
                acc = acc + got_ref[j].astype(F32)
            return acc

        g = jnp.where(pl.program_id(0) == 0, total(p0_ref, g0_ref), total(p1_ref, g1_ref))
        g_ref[...] = g
        d_ref[...], nm_ref[...], nv_ref[...] = _adamw_math(w_ref[...], g, m_ref[...], v_ref[...])

    blk = pl.BlockSpec((None, tr, cdim), lambda l, i, q: (l, i, 0))
    first = lambda l, i: i * (1 - l) + (nb - 1) * l
    second = lambda l, i: i * l
    specs = [blk,
             pl.BlockSpec((None, tr, cdim), lambda l, i, q: (q[0], first(l, i), 0)),
             pl.BlockSpec((3, tr, cdim), lambda l, i, q: (0, first(l, i), 0)),
             pl.BlockSpec((None, tr, cdim), lambda l, i, q: (q[0], second(l, i), 0)),
             pl.BlockSpec((3, tr, cdim), lambda l, i, q: (0, second(l, i), 0)), blk, blk]
    shape = jax.ShapeDtypeStruct((nl, r, cdim), F32)
    return pl.pallas_call(
        body, name=name,
        grid_spec=pltpu.PrefetchScalarGridSpec(num_scalar_prefetch=1, grid=(nl, nb), in_specs=specs, out_specs=[blk] * 4),
        out_shape=[shape] * 4, compiler_params=_cparams(),
    )(chip, w, *reduced[0], *reduced[1], m, v)


def _to_rows(name, a):
    return jnp.swapaxes(a, 1, 2) if name in ("w_in", "w_up") else a


def _pack(arrays, rows):
    flat = jnp.concatenate([a.reshape(-1) for a in arrays])
    return jnp.pad(flat, (0, rows * D - flat.shape[0])).reshape(rows, D)


def _rows_for(shapes, mult=8):
    total = 0
    for shp in shapes:
        size = 1
        for dim in shp:
            size *= dim
        total += size
    return -(-total // (mult * D)) * mult


def kernel(x, mem, norm_mix_pre, norm_mix_post, w_in, w_out, gmlp_v_gain, w_spatial, b_spatial, w_pool, s_pool, w_dw, b_dw, conv_ln_g, conv_ln_b, norm_xattn_pre, norm_mem, norm_xattn_post, w_q, w_k, w_v, w_o, norm_ffn_pre, norm_ffn_post, w_up, w_down, loss_target, m_norm_mix_pre, m_norm_mix_post, m_w_in, m_w_out, m_gmlp_v_gain, m_w_spatial, m_b_spatial, m_w_pool, m_s_pool, m_w_dw, m_b_dw, m_conv_ln_g, m_conv_ln_b, m_norm_xattn_pre, m_norm_mem, m_norm_xattn_post, m_w_q, m_w_k, m_w_v, m_w_o, m_norm_ffn_pre, m_norm_ffn_post, m_w_up, m_w_down, v_norm_mix_pre, v_norm_mix_post, v_w_in, v_w_out, v_gmlp_v_gain, v_w_spatial, v_b_spatial, v_w_pool, v_s_pool, v_w_dw, v_b_dw, v_conv_ln_g, v_conv_ln_b, v_norm_xattn_pre, v_norm_mem, v_norm_xattn_post, v_w_q, v_w_k, v_w_v, v_w_o, v_norm_ffn_pre, v_norm_ffn_post, v_w_up, v_w_down):
    args = dict(locals())
    wts = {n: args[n] for n in WEIGHTS}
    mom_m = {n: args["m_" + n] for n in WEIGHTS}
    mom_v = {n: args["v_" + n] for n in WEIGHTS}
    xi, yi, ci = _position()
    me = 4 * xi + 2 * yi + ci

    dev = jnp.reshape(me, (1,)).astype(jnp.int32)
    lands = {}
    for call, names, tr in (("place_att", ("w_out", "w_q", "w_k", "w_v", "w_o"), 64), ("place_down", ("w_down",), 256),
                            ("place_up", ("w_up",), 256), ("place_in", ("w_in",), 256)):
        srcs = [(_to_rows(n, wts[n]), l) for l in range(DEPTH) for n in names]
        placed = _place_own(call, srcs, dev, BF16, tr)
        lands.update(zip([(l, n) for l in range(DEPTH) for n in names], placed))
    (lands[(0, "taps")],) = _place_own("place_taps", [(_pack([w_dw], _rows_for([w_dw.shape])), None)], dev, F32, 8)
    groups = []
    for l in range(DEPTH):
        for group, names in GATHER_GROUPS:
            if (l, group) == (0, "in"):
                names = names + ("taps",)
            groups.append(((l, group), names, [lands[(l, n)] for n in names]))
    gather = _WeightGather(groups)

    def fetch(layer, group, marker):
        w = gather.fetch(layer, group, marker)
        if "taps" in w:
            blocks = w["taps"].reshape(N_DEV, -1)[:, :w_dw.size].reshape((N_DEV,) + w_dw.shape)
            w["taps"] = jnp.moveaxis(blocks, 0, 2).reshape(DEPTH, CONV_K, CW)
        return w

    reduce = _GradReduce(jnp.reshape(ci, (1,)).astype(jnp.int32), jnp.reshape(2 * xi + yi, (1,)).astype(jnp.int32))
    small = {n: wts[n] for n in SMALL if n != "w_dw"}
    loss_tiles, dx, gsmall = _local_step(x[0], mem[0], loss_target[0], fetch, small, reduce)
    reduce.advance((dx,))

    na = len(SMALL_ARRAYS)
    ssend, srecv, own, slots, token = _broadcast_start(
        "small_grads_start", [gsmall[l][k] for l in range(DEPTH) for k in SMALL_ARRAYS] + [loss_tiles], (dx,))

    grad_w, delta, new_m, new_v = {}, {}, {}, {}
    marker = (token,)
    for n in UPDATE_ORDER:
        reduced = [reduce.collect((l, n), marker) for l in range(DEPTH)]
        outs = _adamw_layers("adamw_" + n, _to_rows(n, wts[n]), reduced, _to_rows(n, mom_m[n]), _to_rows(n, mom_v[n]),
                             reduce.chip, 256)
        grad_w[n], delta[n], new_m[n], new_v[n] = (_to_rows(n, o) for o in outs)
        marker = (outs[1],)

    own, slots = _broadcast_finish("small_grads_finish", own, slots, ssend, srecv, marker)
    loss_own, loss_slots = own[-1], slots[-1]
    own = [dict(zip(SMALL_ARRAYS, own[l * na:(l + 1) * na])) for l in range(DEPTH)]
    slots = [dict(zip(SMALL_ARRAYS, slots[l * na:(l + 1) * na])) for l in range(DEPTH)]
    shard_cols = CW // N_DEV
    for l in range(DEPTH):
        own[l]["w_dw"] = lax.dynamic_slice_in_dim(own[l]["w_dw"], me * shard_cols, shard_cols, axis=1)
        slots[l]["w_dw"] = lax.dynamic_slice_in_dim(slots[l]["w_dw"], me * shard_cols, shard_cols, axis=2)
    *small_out, loss_tile = _adamw_small(wts, mom_m, mom_v, own, slots, loss_own, loss_slots, dev)
    for dst, src in zip((grad_w, delta, new_m, new_v), small_out):
        dst.update(src)

    return (loss_tile[0, 0], dx[None], *[grad_w[n] for n in WEIGHTS], *[delta[n] for n in WEIGHTS],
            *[new_m[n] for n in WEIGHTS], *[new_v[n] for n in WEIGHTS])
```

```python
import functools

import jax
import jax.numpy as jnp
from jax import lax
from jax.experimental import pallas as pl
from jax.experimental.pallas import tpu as pltpu

F32 = jnp.float32
BF16 = jnp.bfloat16

D = 2048
GW = 1024
PW = 512
CW = 512
HD = 128
NH = 8
NG = 4
POOL_WINDOWS = (2, 4, 8, 16)
CONV_K = 31
IN_COLS = 2 * GW + PW + 2 * CW
DFF = 4 * D
XH = 4
XHD = D // XH
ATT_SCALE = XHD ** -0.5
RMS_EPS = 1e-6
LN_EPS = 1e-5
DEPTH = 2
N_DEV = 8

ADAM_LR = 0.001
ADAM_B1 = 0.9
ADAM_B2 = 0.999
ADAM_EPS = 1e-08
ADAM_WD = 0.01
ADAM_STEP = 10

LANES = 128
CONV_HALO = 32
POOL_HALO = 16
ROW_TILE = 128
VMEM_LIMIT = 60 * 1024 * 1024

MESH = pl.DeviceIdType.MESH
NT = (((1,), (1,)), ((), ()))
NN = (((1,), (0,)), ((), ()))
TN = (((0,), (0,)), ((), ()))

BIG = ("w_out", "w_q", "w_k", "w_v", "w_o", "w_up", "w_down", "w_in")
UPDATE_ORDER = ("w_down", "w_up", "w_o", "w_q", "w_k", "w_v", "w_out", "w_in")
GATHER_GROUPS = (("in", ("w_in",)), ("att", ("w_out", "w_q", "w_k", "w_v", "w_o")), ("up", ("w_up",)),
                 ("down", ("w_down",)))
SMALL = ("norm_mix_pre", "norm_mix_post", "gmlp_v_gain", "w_spatial", "b_spatial", "w_pool", "s_pool",
         "w_dw", "b_dw", "conv_ln_g", "conv_ln_b", "norm_xattn_pre", "norm_mem", "norm_xattn_post",
         "norm_ffn_pre", "norm_ffn_post")
WEIGHTS = ("norm_mix_pre", "norm_mix_post", "w_in", "w_out", "gmlp_v_gain", "w_spatial", "b_spatial", "w_pool",
           "s_pool", "w_dw", "b_dw", "conv_ln_g", "conv_ln_b", "norm_xattn_pre", "norm_mem", "norm_xattn_post",
           "w_q", "w_k", "w_v", "w_o", "norm_ffn_pre", "norm_ffn_post", "w_up", "w_down")


def _cparams():
    return pltpu.CompilerParams(vmem_limit_bytes=VMEM_LIMIT)


def _dot(a, b, dims):
    return lax.dot_general(a, b, dims, preferred_element_type=F32)


def _rms(x, g):
    y = x * lax.rsqrt(jnp.mean(x * x, axis=-1, keepdims=True) + RMS_EPS)
    return y * g


def _gelu(x):
    cdf = 0.5 * (1.0 + jnp.tanh(0.7978845608028654 * (x + 0.044715 * (x * x * x))))
    return x * cdf


def _layer_norm(x, g, b=None):
    mu = jnp.mean(x, axis=-1, keepdims=True)
    xc = x - mu
    var = jnp.mean(xc * xc, axis=-1, keepdims=True)
    y = xc * lax.rsqrt(var + LN_EPS) * g
    return y if b is None else y + b


def _sigmoid(x):
    return 1.0 / (1.0 + jnp.exp(-x))


def _gmlp_rows(zu, zv, gv):
    return _gelu(zu), _layer_norm(_gelu(zv), gv)


def _glu(cv, cg):
    return cv * _sigmoid(cg)


def _ln_silu(h, g, b):
    y = _layer_norm(h, g, b)
    return y * _sigmoid(y)


ANY = pl.BlockSpec(memory_space=pl.ANY)


ROWS_TILE = 256
COLS_TILE = 512
DW_TILE = 512
RESIDENT_K = 2048
STREAM_K_TILE = 1024
STREAM_ROWS = 512


def _k_tiles(kdim):
    if kdim <= RESIDENT_K:
        return ROWS_TILE, kdim
    return STREAM_ROWS, max(t for t in range(LANES, STREAM_K_TILE + 1, LANES) if kdim % t == 0)


def _rowop_mm(name, kind, rows, g, w, dims, out_dtype, u=None, after=()):
    s = rows[0].shape[0]
    n = w.shape[0] if dims == NT else w.shape[1]
    tm, tn = min(ROWS_TILE, s), min(COLS_TILE, n)
    ni = s // tm
    bwd = kind == "rms_bwd"

    def rows_body(*refs):
        refs = list(refs)
        row_refs = [refs.pop(0) for _ in rows]
        g_ref = refs.pop(0)
        del refs[:len(after)]
        if bwd:
            _, vjp = jax.vjp(_rms, row_refs[0][...], g_ref[...])
            a, dg = vjp(row_refs[1][...])
            refs[1][0] = dg
        else:
            a = _rms(row_refs[0][...], g_ref[...])
        refs[0][...] = a.astype(BF16)

    row_spec = pl.BlockSpec((tm, D), lambda i: (i, 0))
    res = pl.pallas_call(
        rows_body, name=name + "_rows", grid=(ni,),
        in_specs=[row_spec] * len(rows) + [pl.BlockSpec((1, D), lambda i: (0, 0))] + [ANY] * len(after),
        out_specs=[row_spec] + ([pl.BlockSpec((1, 1, D), lambda i: (i, 0, 0))] if bwd else []),
        out_shape=[jax.ShapeDtypeStruct((s, D), BF16)] + ([jax.ShapeDtypeStruct((ni, 1, D), F32)] if bwd else []),
        compiler_params=_cparams(),
    )(*rows, g, *after)
    a = res[0]

    def body(a_ref, w_ref, *rest):
        acc = _dot(a_ref[...], w_ref[...], dims)
        if u is not None:
            acc = acc * (2.0 * jnp.maximum(rest[0][...], 0.0))
        rest[-1][...] = acc.astype(out_dtype)

    w_spec = pl.BlockSpec((tn, D), lambda j: (j, 0)) if dims == NT else pl.BlockSpec((D, tn), lambda j: (0, j))
    tile = pl.BlockSpec((s, tn), lambda j: (0, j))
    out = pl.pallas_call(
        body, name=name, grid=(n // tn,),
        in_specs=[pl.BlockSpec((s, D), lambda j: (0, 0)), w_spec] + ([tile] if u is not None else []),
        out_specs=tile, out_shape=jax.ShapeDtypeStruct((s, n), out_dtype), compiler_params=_cparams(),
    )(a, w, *([u] if u is not None else []))
    return (out, *res)


def _mm_rowop(name, kind, pairs, rows, g, relu2=False, after=()):
    s, kdim = pairs[0][0].shape
    tm, tk = _k_tiles(kdim)
    tm = min(tm, s)
    ni, nk = s // tm, kdim // tk
    npair = len(pairs)

    def body(*refs):
        refs = list(refs)
        a_refs = [refs.pop(0) for _ in range(npair)]
        w_refs = [refs.pop(0) for _ in range(npair)]
        row_refs = [refs.pop(0) for _ in rows]
        g_ref = refs.pop(0)
        del refs[:len(after)]
        acc = refs.pop()
        outs = refs
        k = pl.program_id(1)

        @pl.when(k == 0)
        def _():
            acc[...] = jnp.zeros_like(acc)

        for a_ref, w_ref, (_, _, dims) in zip(a_refs, w_refs, pairs):
            a = a_ref[...]
            if relu2:
                a = jnp.square(jnp.maximum(a, 0.0))
            acc[...] += _dot(a.astype(BF16), w_ref[...], dims)

        @pl.when(k == nk - 1)
        def _():
            h = acc[...]
            if kind == "rms_res":
                outs[0][...] = row_refs[0][...] + _rms(h, g_ref[...])
                outs[1][...] = h
            else:
                _, vjp = jax.vjp(_rms, row_refs[0][...], g_ref[...])
                dx, dg = vjp(h)
                if kind == "rms_bwd_res":
                    outs[0][...] = row_refs[1][...] + dx
                    outs[1][0] = dg
                else:
                    outs[0][0] = dg

    row_spec = pl.BlockSpec((tm, D), lambda i, k: (i, 0))
    dg_shape = jax.ShapeDtypeStruct((ni, 1, D), F32)
    dg_spec = pl.BlockSpec((1, 1, D), lambda i, k: (i, 0, 0))
    in_specs = [pl.BlockSpec((tm, tk), lambda i, k: (i, k))] * npair
    for _, _, dims in pairs:
        in_specs.append(pl.BlockSpec((tk, D), lambda i, k: (k, 0)) if dims == NN
                        else pl.BlockSpec((D, tk), lambda i, k: (0, k)))
    in_specs += [row_spec] * len(rows) + [pl.BlockSpec((1, D), lambda i, k: (0, 0))] + [ANY] * len(after)
    if kind == "rms_res":
        out_shape = [jax.ShapeDtypeStruct((s, D), F32)] * 2
        out_specs = [row_spec, row_spec]
    elif kind == "rms_bwd_res":
        out_shape = [jax.ShapeDtypeStruct((s, D), F32), dg_shape]
        out_specs = [row_spec, dg_spec]
    else:
        out_shape = [dg_shape]
        out_specs = [dg_spec]
    return pl.pallas_call(
        body, name=name, grid=(ni, nk), in_specs=in_specs, out_specs=out_specs, out_shape=out_shape,
        scratch_shapes=[pltpu.VMEM((tm, D), F32)], compiler_params=_cparams(),
    )(*[p[0] for p in pairs], *[p[1] for p in pairs], *rows, g, *after)


def _mm_tn(name, a, gmat, relu2=False, after=()):
    s, m = a.shape
    tm, ts = min(DW_TILE, m), s
    ni, ns = m // tm, s // ts

    def body(a_ref, g_ref, *rest):
        o_ref, acc = rest[len(after):]
        k = pl.program_id(1)

        @pl.when(k == 0)
        def _():
            acc[...] = jnp.zeros_like(acc)

        av = a_ref[...]
        if relu2:
            av = jnp.square(jnp.maximum(av, 0.0))
        acc[...] += _dot(av.astype(BF16), g_ref[...], TN)

        @pl.when(k == ns - 1)
        def _():
            o_ref[...] = acc[...].astype(BF16)

    return pl.pallas_call(
        body, name=name, grid=(ni, ns),
        in_specs=[pl.BlockSpec((ts, tm), lambda i, k: (k, i)), pl.BlockSpec((ts, D), lambda i, k: (k, 0))]
        + [ANY] * len(after),
        out_specs=pl.BlockSpec((tm, D), lambda i, k: (i, 0)),
        out_shape=jax.ShapeDtypeStruct((m, D), BF16),
        scratch_shapes=[pltpu.VMEM((tm, D), F32)], compiler_params=_cparams(),
    )(a, gmat, *after)


def _tril():
    r = lax.broadcasted_iota(jnp.int32, (HD, HD), 0)
    c = lax.broadcasted_iota(jnp.int32, (HD, HD), 1)
    return (c <= r).astype(F32)


def _gmlp_fwd(z, gv, ws, bst, tb):
    s = z.shape[0]
    tb = min(tb, s)

    def body(zu_ref, zv_ref, gv_ref, ws_ref, bst_ref, y_ref):
        tril = _tril()
        for h in range(NH):
            cols = slice(h * HD, (h + 1) * HD)
            u, vln = _gmlp_rows(zu_ref[:, cols], zv_ref[:, cols], gv_ref[h:h + 1, :])
            wm = (ws_ref[h] * tril).astype(BF16)
            vb = vln.astype(BF16)
            for c in range(tb // HD):
                rws = slice(c * HD, (c + 1) * HD)
                mixed = _dot(wm, vb[rws], NN) + bst_ref[:, h:h + 1]
                y_ref[rws, cols] = (u[rws] * mixed).astype(BF16)

    return pl.pallas_call(
        body, name="gmlp_fwd", grid=(s // tb,),
        in_specs=[pl.BlockSpec((tb, GW), lambda i: (i, 0)), pl.BlockSpec((tb, GW), lambda i: (i, 1)),
                  pl.BlockSpec((NH, HD), lambda i: (0, 0)), pl.BlockSpec((NH, HD, HD), lambda i: (0, 0, 0)),
                  pl.BlockSpec((HD, NH), lambda i: (0, 0))],
        out_specs=pl.BlockSpec((tb, GW), lambda i: (i, 0)),
        out_shape=jax.ShapeDtypeStruct((s, D), BF16), compiler_params=_cparams(),
    )(z, z, gv, ws, bst)


def _gmlp_bwd(z, dy, gv, ws, bst, tb):
    s = z.shape[0]
    tb = min(tb, s)
    nb = s // tb

    def body(zu_ref, zv_ref, dy_ref, gv_ref, ws_ref, bst_ref, dz_ref, dgv_ref, dws_ref, db_ref):
        tril = _tril()
        for h in range(NH):
            cols = slice(h * HD, (h + 1) * HD)
            (u, vln), vjp = jax.vjp(_gmlp_rows, zu_ref[:, cols], zv_ref[:, cols], gv_ref[h:h + 1, :])
            wmf = ws_ref[h] * tril
            wm = wmf.astype(BF16)
            wmt = wmf.T.astype(BF16)
            vb = vln.astype(BF16)
            dws = jnp.zeros((HD, HD), F32)
            db = jnp.zeros((HD, 1), F32)
            du_parts, dvln_parts = [], []
            for c in range(tb // HD):
                rws = slice(c * HD, (c + 1) * HD)
                mixed = _dot(wm, vb[rws], NN) + bst_ref[:, h:h + 1]
                dyc = dy_ref[rws, cols]
                du_parts.append(dyc * mixed)
                dmixed = dyc * u[rws]
                dmb = dmixed.astype(BF16)
                dws = dws + _dot(dmb, vb[rws], NT)
                db = db + jnp.sum(dmixed, axis=1, keepdims=True)
                dvln_parts.append(_dot(wmt, dmb, NN))
            du = jnp.concatenate(du_parts, axis=0)
            dvln = jnp.concatenate(dvln_parts, axis=0)
            dzu, dzv, dgv = vjp((du, dvln))
            dz_ref[:, cols] = dzu.astype(BF16)
            dz_ref[:, slice(GW + h * HD, GW + (h + 1) * HD)] = dzv.astype(BF16)
            dgv_ref[0, h:h + 1, :] = dgv
            dws_ref[0, h] = dws * tril
            db_ref[0, h] = jnp.broadcast_to(db, (HD, LANES))

    blk = pl.BlockSpec((tb, GW), lambda i: (i, 0))
    return pl.pallas_call(
        body, name="gmlp_bwd", grid=(nb,),
        in_specs=[blk, pl.BlockSpec((tb, GW), lambda i: (i, 1)), blk,
                  pl.BlockSpec((NH, HD), lambda i: (0, 0)), pl.BlockSpec((NH, HD, HD), lambda i: (0, 0, 0)),
                  pl.BlockSpec((HD, NH), lambda i: (0, 0))],
        out_specs=[pl.BlockSpec((tb, 2 * GW), lambda i: (i, 0)), pl.BlockSpec((1, NH, HD), lambda i: (i, 0, 0)),
                   pl.BlockSpec((1, NH, HD, HD), lambda i: (i, 0, 0, 0)),
                   pl.BlockSpec((1, NH, HD, LANES), lambda i: (i, 0, 0, 0))],
        out_shape=[jax.ShapeDtypeStruct((s, IN_COLS), BF16),
                   jax.ShapeDtypeStruct((nb, NH, HD), F32), jax.ShapeDtypeStruct((nb, NH, HD, HD), F32),
                   jax.ShapeDtypeStruct((nb, NH, HD, LANES), F32)],
        compiler_params=_cparams(),
    )(z, z, dy, gv, ws, bst)


def _pool_count(t0, window):
    pos = (t0 + lax.broadcasted_iota(jnp.int32, (ROW_TILE, LANES), 0)).astype(F32)
    return jnp.minimum(pos + 1.0, float(window))


def _window_sum(win, levels, back):
    n = win.shape[0]
    for lv in range(levels):
        step = 1 << lv
        win = win + pltpu.roll(win, n - step if back else step, 0)
    return win


def _pool_pooled(ppad_ref, t0, g):
    win = ppad_ref[pl.ds(t0, ROW_TILE + POOL_HALO), :]
    wsum = _window_sum(win, g + 1, False)[POOL_HALO:]
    return wsum / _pool_count(t0, POOL_WINDOWS[g]) - win[POOL_HALO:]


def _pool_fwd(z, wp, sp, y):
    s = z.shape[0]
    nt = s // ROW_TILE

    def body(p_ref, wp_ref, sp_ref, _, y_ref, ppad):
        for g in range(NG):
            cols = slice(g * LANES, (g + 1) * LANES)
            ppad[pl.ds(0, POOL_HALO), :] = jnp.zeros((POOL_HALO, LANES), F32)
            ppad[pl.ds(POOL_HALO, s), :] = p_ref[:, cols]
            wpb = wp_ref[g].astype(BF16)
            scale = sp_ref[:, cols]

            def tile(t, carry):
                t0 = pl.multiple_of(t * ROW_TILE, ROW_TILE)
                pooled = _pool_pooled(ppad, t0, g)
                y_ref[pl.ds(t0, ROW_TILE), cols] = (_dot(pooled.astype(BF16), wpb, NN) * scale).astype(BF16)
                return carry

            lax.fori_loop(0, nt, tile, 0)

    return pl.pallas_call(
        body, name="pool_fwd", grid=(1,),
        in_specs=[pl.BlockSpec((s, PW), lambda i: (0, 2 * GW // PW)),
                  pl.BlockSpec((NG, LANES, LANES), lambda i: (0, 0, 0)), pl.BlockSpec((1, PW), lambda i: (0, 0)), ANY],
        out_specs=pl.BlockSpec((s, PW), lambda i: (0, GW // PW)),
        out_shape=jax.ShapeDtypeStruct((s, D), BF16), input_output_aliases={3: 0},
        scratch_shapes=[pltpu.VMEM((s + POOL_HALO, LANES), F32)], compiler_params=_cparams(),
    )(z, wp, sp, y)


def _pool_bwd(z, dy, wp, sp, dz):
    s = z.shape[0]
    nt = s // ROW_TILE

    def body(p_ref, dy_ref, wp_ref, sp_ref, _, dp_ref, dwp_ref, dsp_ref, ppad, rpad, dpool):
        for g in range(NG):
            cols = slice(g * LANES, (g + 1) * LANES)
            ppad[pl.ds(0, POOL_HALO), :] = jnp.zeros((POOL_HALO, LANES), F32)
            ppad[pl.ds(POOL_HALO, s), :] = p_ref[:, cols]
            rpad[pl.ds(s, POOL_HALO), :] = jnp.zeros((POOL_HALO, LANES), F32)
            wpb = wp_ref[g].astype(BF16)
            scale = sp_ref[:, cols]

            def tile(t, carry):
                dwp, dsp = carry
                t0 = pl.multiple_of(t * ROW_TILE, ROW_TILE)
                pooled = _pool_pooled(ppad, t0, g)
                pb = pooled.astype(BF16)
                dyt = dy_ref[pl.ds(t0, ROW_TILE), cols]
                dsp = dsp + jnp.sum(dyt * _dot(pb, wpb, NN), axis=0, keepdims=True)
                dmm = (dyt * scale).astype(BF16)
                dwp = dwp + _dot(pb, dmm, TN)
                dpooled = _dot(dmm, wpb, NT)
                rpad[pl.ds(t0, ROW_TILE), :] = dpooled / _pool_count(t0, POOL_WINDOWS[g])
                dpool[pl.ds(t0, ROW_TILE), :] = dpooled
                return dwp, dsp

            dwp, dsp = lax.fori_loop(0, nt, tile, (jnp.zeros((LANES, LANES), F32), jnp.zeros((1, LANES), F32)))
            dwp_ref[g] = dwp
            dsp_ref[:, cols] = dsp

            def tile2(t, carry):
                t0 = pl.multiple_of(t * ROW_TILE, ROW_TILE)
                win = rpad[pl.ds(t0, ROW_TILE + POOL_HALO), :]
                back = _window_sum(win, g + 1, True)[:ROW_TILE]
                rows = pl.ds(t0, ROW_TILE)
                dp_ref[rows, cols] = (back - dpool[rows, :]).astype(BF16)
                return carry

            lax.fori_loop(0, nt, tile2, 0)

    return pl.pallas_call(
        body, name="pool_bwd", grid=(1,),
        in_specs=[pl.BlockSpec((s, PW), lambda i: (0, 2 * GW // PW)), pl.BlockSpec((s, PW), lambda i: (0, GW // PW)),
                  pl.BlockSpec((NG, LANES, LANES), lambda i: (0, 0, 0)), pl.BlockSpec((1, PW), lambda i: (0, 0)), ANY],
        out_specs=[pl.BlockSpec((s, PW), lambda i: (0, 2 * GW // PW)),
                   pl.BlockSpec((NG, LANES, LANES), lambda i: (0, 0, 0)), pl.BlockSpec((1, PW), lambda i: (0, 0))],
        out_shape=[jax.ShapeDtypeStruct((s, IN_COLS), BF16), jax.ShapeDtypeStruct((NG, LANES, LANES), F32),
                   jax.ShapeDtypeStruct((1, PW), F32)],
        input_output_aliases={4: 0},
        scratch_shapes=[pltpu.VMEM((s + POOL_HALO, LANES), F32), pltpu.VMEM((s + POOL_HALO, LANES), F32),
                        pltpu.VMEM((s, LANES), F32)],
        compiler_params=_cparams(),
    )(z, dy, wp, sp, dz)


CONV_LEAD = CONV_HALO - (CONV_K - 1)


def _conv_taps(win, wdw_ref, lead, reverse):
    n = win.shape[0]
    acc = jnp.zeros((ROW_TILE, CW), F32)
    for j in range(CONV_K):
        tap = (CONV_K - 1 - j) if reverse else j
        acc = acc + wdw_ref[tap:tap + 1, :] * pltpu.roll(win, (n - (lead + j)) % n, 0)[:ROW_TILE]
    return acc


def _conv_fill_glu(cv_ref, cg_ref, xpad, s):
    xpad[pl.ds(0, CONV_HALO), :] = jnp.zeros((CONV_HALO, CW), F32)

    def fill(t, carry):
        t0 = pl.multiple_of(t * ROW_TILE, ROW_TILE)
        rows = pl.ds(t0, ROW_TILE)
        xpad[pl.ds(t0 + CONV_HALO, ROW_TILE), :] = _glu(cv_ref[rows, :], cg_ref[rows, :])
        return carry

    lax.fori_loop(0, s // ROW_TILE, fill, 0)


def _conv_fwd(z, wdw, bdw, lng, lnb, y):
    s = z.shape[0]

    def body(cv_ref, cg_ref, wdw_ref, bdw_ref, lng_ref, lnb_ref, _, y_ref, xpad):
        _conv_fill_glu(cv_ref, cg_ref, xpad, s)

        def tile(t, carry):
            t0 = pl.multiple_of(t * ROW_TILE, ROW_TILE)
            win = xpad[pl.ds(t0, ROW_TILE + CONV_HALO), :]
            hc = _conv_taps(win, wdw_ref, CONV_LEAD, False) + bdw_ref[...]
            y_ref[pl.ds(t0, ROW_TILE), :] = _ln_silu(hc, lng_ref[...], lnb_ref[...]).astype(BF16)
            return carry

        lax.fori_loop(0, s // ROW_TILE, tile, 0)

    vec = pl.BlockSpec((1, CW), lambda i: (0, 0))
    return pl.pallas_call(
        body, name="conv_fwd", grid=(1,),
        in_specs=[pl.BlockSpec((s, CW), lambda i: (0, (2 * GW + PW) // CW)),
                  pl.BlockSpec((s, CW), lambda i: (0, (2 * GW + PW) // CW + 1)),
                  pl.BlockSpec((CONV_K + 1, CW), lambda i: (0, 0)), vec, vec, vec, ANY],
        out_specs=pl.BlockSpec((s, CW), lambda i: (0, (GW + PW) // CW)),
        out_shape=jax.ShapeDtypeStruct((s, D), BF16), input_output_aliases={6: 0},
        scratch_shapes=[pltpu.VMEM((s + CONV_HALO, CW), F32)], compiler_params=_cparams(),
    )(z, z, wdw, bdw, lng, lnb, y)


def _conv_bwd(z, dy, wdw, bdw, lng, lnb, dz):
    s = z.shape[0]

    def body(cv_ref, cg_ref, dy_ref, wdw_ref, bdw_ref, lng_ref, lnb_ref, _,
             dz_ref, dwdw_ref, dbdw_ref, dlng_ref, dlnb_ref, xpad, dpad, dcg_keep):
        @pl.when(pl.program_id(0) == 0)
        def _():
            compute(cv_ref, cg_ref, dy_ref, wdw_ref, bdw_ref, lng_ref, lnb_ref,
                    dz_ref, dcg_keep, dwdw_ref, dbdw_ref, dlng_ref, dlnb_ref, xpad, dpad)

        @pl.when(pl.program_id(0) == 1)
        def _():
            dz_ref[...] = dcg_keep[...]

    def compute(cv_ref, cg_ref, dy_ref, wdw_ref, bdw_ref, lng_ref, lnb_ref,
                dcv_ref, dcg_ref, dwdw_ref, dbdw_ref, dlng_ref, dlnb_ref, xpad, dpad):
        _conv_fill_glu(cv_ref, cg_ref, xpad, s)
        dpad[pl.ds(s, CONV_HALO), :] = jnp.zeros((CONV_HALO, CW), F32)
        dwdw_ref[...] = jnp.zeros((CONV_K + 1, CW), F32)

        def tile(t, carry):
            db, dg, dbeta = carry
            t0 = pl.multiple_of(t * ROW_TILE, ROW_TILE)
            win = xpad[pl.ds(t0, ROW_TILE + CONV_HALO), :]
            hc = _conv_taps(win, wdw_ref, CONV_LEAD, False) + bdw_ref[...]
            _, vjp = jax.vjp(_ln_silu, hc, lng_ref[...], lnb_ref[...])
            dhc, dg_t, dbeta_t = vjp(dy_ref[pl.ds(t0, ROW_TILE), :])
            dpad[pl.ds(t0, ROW_TILE), :] = dhc
            n = win.shape[0]
            for j in range(CONV_K):
                shifted = pltpu.roll(win, (n - (CONV_LEAD + j)) % n, 0)[:ROW_TILE]
                dwdw_ref[j:j + 1, :] += jnp.sum(dhc * shifted, axis=0, keepdims=True)
            return db + jnp.sum(dhc, axis=0, keepdims=True), dg + dg_t, dbeta + dbeta_t

        zero = jnp.zeros((1, CW), F32)
        db, dg, dbeta = lax.fori_loop(0, s // ROW_TILE, tile, (zero, zero, zero))
        dbdw_ref[...] = db
        dlng_ref[...] = dg
        dlnb_ref[...] = dbeta

        def tile2(t, carry):
            t0 = pl.multiple_of(t * ROW_TILE, ROW_TILE)
            rows = pl.ds(t0, ROW_TILE)
            win = dpad[pl.ds(t0, ROW_TILE + CONV_HALO), :]
            dglu = _conv_taps(win, wdw_ref, 0, True)
            _, vjp = jax.vjp(_glu, cv_ref[rows, :], cg_ref[rows, :])
            dcv, dcg = vjp(dglu)
            dcv_ref[rows, :] = dcv.astype(BF16)
            dcg_ref[rows, :] = dcg.astype(BF16)
            return carry

        lax.fori_loop(0, s // ROW_TILE, tile2, 0)

    vec = pl.BlockSpec((1, CW), lambda i: (0, 0))
    wspec = pl.BlockSpec((CONV_K + 1, CW), lambda i: (0, 0))
    vshape = jax.ShapeDtypeStruct((1, CW), F32)
    return pl.pallas_call(
        body, name="conv_bwd", grid=(2,),
        in_specs=[pl.BlockSpec((s, CW), lambda i: (0, (2 * GW + PW) // CW)),
                  pl.BlockSpec((s, CW), lambda i: (0, (2 * GW + PW) // CW + 1)),
                  pl.BlockSpec((s, CW), lambda i: (0, (GW + PW) // CW)), wspec, vec, vec, vec, ANY],
        out_specs=[pl.BlockSpec((s, CW), lambda i: (0, (2 * GW + PW) // CW + i)), wspec, vec, vec, vec],
        out_shape=[jax.ShapeDtypeStruct((s, IN_COLS), BF16), jax.ShapeDtypeStruct((CONV_K + 1, CW), F32),
                   vshape, vshape, vshape],
        input_output_aliases={7: 0},
        scratch_shapes=[pltpu.VMEM((s + CONV_HALO, CW), F32), pltpu.VMEM((s + CONV_HALO, CW), F32),
                        pltpu.VMEM((s, CW), BF16)],
        compiler_params=_cparams(),
    )(z, z, dy, wdw, bdw, lng, lnb, dz)


def _softmax_rows(sc):
    e = jnp.exp(sc - jnp.max(sc, axis=-1, keepdims=True))
    return e / jnp.sum(e, axis=-1, keepdims=True)


def _attn_fwd(q, k, v, tq):
    s, m = q.shape[0], k.shape[0]
    tq = min(tq, s)

    def body(q_ref, k_ref, v_ref, o_ref):
        for h in range(XH):
            cols = slice(h * XHD, (h + 1) * XHD)
            p = _softmax_rows(_dot(q_ref[:, cols], k_ref[:, cols], NT) * ATT_SCALE)
            o_ref[:, cols] = _dot(p.astype(BF16), v_ref[:, cols], NN).astype(BF16)

    kv = pl.BlockSpec((m, D), lambda i: (0, 0))
    return pl.pallas_call(
        body, name="attn_fwd", grid=(s // tq,),
        in_specs=[pl.BlockSpec((tq, D), lambda i: (i, 0)), kv, kv],
        out_specs=pl.BlockSpec((tq, D), lambda i: (i, 0)),
        out_shape=jax.ShapeDtypeStruct((s, D), BF16), compiler_params=_cparams(),
    )(q, k, v)


def _attn_bwd(q, k, v, do, tq):
    s, m = q.shape[0], k.shape[0]
    tq = min(tq, s)

    def body(q_ref, k_ref, v_ref, do_ref, dq_ref, dk_ref, dv_ref):
        @pl.when(pl.program_id(0) == 0)
        def _():
            dk_ref[...] = jnp.zeros_like(dk_ref)
            dv_ref[...] = jnp.zeros_like(dv_ref)

        for h in range(XH):
            cols = slice(h * XHD, (h + 1) * XHD)
            qh, kh, vh, doh = q_ref[:, cols], k_ref[:, cols], v_ref[:, cols], do_ref[:, cols]
            p = _softmax_rows(_dot(qh, kh, NT) * ATT_SCALE)
            dp = _dot(doh, vh, NT)
            dv_ref[:, cols] += _dot(p.astype(BF16), doh, TN)
            ds = (p * (dp - jnp.sum(p * dp, axis=-1, keepdims=True)) * ATT_SCALE).astype(BF16)
            dq_ref[:, cols] = _dot(ds, kh, NN).astype(BF16)
            dk_ref[:, cols] += _dot(ds, qh, TN)

    kv = pl.BlockSpec((m, D), lambda i: (0, 0))
    qs = pl.BlockSpec((tq, D), lambda i: (i, 0))
    return pl.pallas_call(
        body, name="attn_bwd", grid=(s // tq,),
        in_specs=[qs, kv, kv, qs], out_specs=[qs, kv, kv],
        out_shape=[jax.ShapeDtypeStruct((s, D), BF16), jax.ShapeDtypeStruct((m, D), F32),
                   jax.ShapeDtypeStruct((m, D), F32)],
        compiler_params=_cparams(),
    )(q, k, v, do)


def _loss_head(y, target, tm):
    s = y.shape[0]
    tm = min(tm, s)

    def body(y_ref, t_ref, dy_ref, part_ref):
        err = y_ref[...] - t_ref[...]
        dy_ref[...] = err * (1.0 / D)
        part_ref[...] = jnp.full((1, 8, LANES), 0.5 * jnp.sum(err * err) * (1.0 / D), F32)

    blk = pl.BlockSpec((tm, D), lambda i: (i, 0))
    return pl.pallas_call(
        body, name="loss_head", grid=(s // tm,), in_specs=[blk, blk],
        out_specs=[blk, pl.BlockSpec((1, 8, LANES), lambda i: (i, 0, 0))],
        out_shape=[jax.ShapeDtypeStruct((s, D), F32), jax.ShapeDtypeStruct((s // tm, 8, LANES), F32)],
        compiler_params=_cparams(),
    )(y, target)


def _layer_fwd(x0, mem, w, p, fetch):
    z, hn0 = _rowop_mm("mix_in", "rms", (x0,), p["norm_mix_pre"], w["w_in"], NT, F32)
    y = _gmlp_fwd(z, p["gmlp_v_gain"], p["w_spatial"], p["b_spatial_t"], 512)
    y = _pool_fwd(z, p["w_pool"], p["s_pool"], y)
    y = _conv_fwd(z, p["w_dw"], p["b_dw"], p["conv_ln_g"], p["conv_ln_b"], y)
    w.update(fetch("att", (y,)))
    x1, h0 = _mm_rowop("mix_out", "rms_res", [(y, w["w_out"], NN)], (x0,), p["norm_mix_post"])
    q, hn1 = _rowop_mm("att_q", "rms", (x1,), p["norm_xattn_pre"], w["w_q"], NN, BF16)
    k, mn = _rowop_mm("att_k", "rms", (mem,), p["norm_mem"], w["w_k"], NN, BF16)
    v, _ = _rowop_mm("att_v", "rms", (mem,), p["norm_mem"], w["w_v"], NN, BF16)
    o = _attn_fwd(q, k, v, 256)
    x2, h1 = _mm_rowop("att_o", "rms_res", [(o, w["w_o"], NN)], (x1,), p["norm_xattn_post"])
    w.update(fetch("up", (x2,)))
    u, hn2 = _rowop_mm("ffn_up", "rms", (x2,), p["norm_ffn_pre"], w["w_up"], NT, F32)
    w.update(fetch("down", (u,)))
    x3, h2 = _mm_rowop("ffn_down", "rms_res", [(u, w["w_down"], NN)], (x2,), p["norm_ffn_post"], relu2=True)
    saved = dict(x0=x0, z=z, hn0=hn0, y=y, h0=h0, x1=x1, q=q, hn1=hn1, k=k, v=v, mn=mn, o=o, h1=h1, x2=x2, u=u,
                 hn2=hn2, h2=h2)
    return x3, saved


def _layer_bwd(dx3, mem, w, p, sv, red):
    gs = {}
    du, dh2, dg = _rowop_mm("ffn_down_bwd", "rms_bwd", (sv["h2"], dx3), p["norm_ffn_post"], w["w_down"], NT, BF16,
                            u=sv["u"], after=red.after())
    gs["norm_ffn_post"] = jnp.sum(dg, axis=0)
    g_down = _mm_tn("ffn_down_dw", sv["u"], dh2, relu2=True)
    red.advance((g_down,))
    dx2, dg = _mm_rowop("ffn_up_bwd", "rms_bwd_res", [(du, w["w_up"], NN)], (sv["x2"], dx3), p["norm_ffn_pre"],
                        after=red.after())
    gs["norm_ffn_pre"] = jnp.sum(dg, axis=0)
    g_up = _mm_tn("ffn_up_dw", du, sv["hn2"])
    red.add("ffn", ("w_down", "w_up"), [g_down, g_up])
    do, dh1, dg = _rowop_mm("att_o_bwd", "rms_bwd", (sv["h1"], dx2), p["norm_xattn_post"], w["w_o"], NT, BF16,
                            after=red.after())
    gs["norm_xattn_post"] = jnp.sum(dg, axis=0)
    g_o = _mm_tn("att_o_dw", sv["o"], dh1)
    red.advance((g_o,))
    dq, dk, dv = _attn_bwd(sv["q"], sv["k"], sv["v"], do, 256)
    dk, dv = dk.astype(BF16), dv.astype(BF16)
    dx1, dg = _mm_rowop("att_q_bwd", "rms_bwd_res", [(dq, w["w_q"], NT)], (sv["x1"], dx2), p["norm_xattn_pre"],
                        after=red.after())
    gs["norm_xattn_pre"] = jnp.sum(dg, axis=0)
    g_q = _mm_tn("att_q_dw", sv["hn1"], dq)
    g_k = _mm_tn("att_k_dw", sv["mn"], dk)
    g_v = _mm_tn("att_v_dw", sv["mn"], dv)
    (dg,) = _mm_rowop("att_kv_bwd", "rms_bwd_gain", [(dk, w["w_k"], NT), (dv, w["w_v"], NT)], (mem,), p["norm_mem"])
    gs["norm_mem"] = jnp.sum(dg, axis=0)
    red.add("att", ("w_o", "w_q", "w_k", "w_v"), [g_o, g_q, g_k, g_v])
    dy, dh0, dg = _rowop_mm("mix_out_bwd", "rms_bwd", (sv["h0"], dx1), p["norm_mix_post"], w["w_out"], NT, F32,
                            after=red.after())
    gs["norm_mix_post"] = jnp.sum(dg, axis=0)
    g_out = _mm_tn("mix_out_dw", sv["y"], dh0)
    red.advance((g_out,))
    red.add("out", ("w_out",), [g_out])
    z = sv["z"]
    dz, dgv, dws, dbs = _gmlp_bwd(z, dy, p["gmlp_v_gain"], p["w_spatial"], p["b_spatial_t"], 512)
    gs["gmlp_v_gain"] = jnp.sum(dgv, axis=0)
    gs["w_spatial"] = jnp.sum(dws, axis=0)
    gs["b_spatial"] = jnp.sum(dbs[..., 0], axis=0)
    dz, gs["w_pool"], gs["s_pool"] = _pool_bwd(z, dy, p["w_pool"], p["s_pool"], dz)
    dz, dwdw, gs["b_dw"], gs["conv_ln_g"], gs["conv_ln_b"] = _conv_bwd(
        z, dy, p["w_dw"], p["b_dw"], p["conv_ln_g"], p["conv_ln_b"], dz)
    red.advance((dz,))
    g_in = _mm_tn("mix_in_dw", dz, sv["hn0"], after=red.after())
    red.add("in", ("w_in",), [g_in])
    red.advance((g_in,))
    dx0, dg = _mm_rowop("mix_in_bwd", "rms_bwd_res", [(dz, w["w_in"], NN)], (sv["x0"], dx1), p["norm_mix_pre"],
                        after=red.after())
    gs["norm_mix_pre"] = jnp.sum(dg, axis=0)
    red.small(_small_grad_arrays(gs, dwdw))
    return dx0


NORM_NAMES = ("norm_mix_pre", "norm_mix_post", "norm_xattn_pre", "norm_mem", "norm_xattn_post", "norm_ffn_pre",
              "norm_ffn_post")
VEC_NAMES = ("s_pool", "b_dw", "conv_ln_g", "conv_ln_b")
SMALL_ARRAYS = ("norms", "gain_bias", "w_spatial", "w_pool", "vecs", "w_dw")


def _small_grad_arrays(gs, dwdw):
    return {"norms": jnp.concatenate([gs[n] for n in NORM_NAMES], axis=0),
            "gain_bias": jnp.concatenate([gs["gmlp_v_gain"], gs["b_spatial"]], axis=0),
            "w_spatial": gs["w_spatial"], "w_pool": gs["w_pool"],
            "vecs": jnp.concatenate([gs[n] for n in VEC_NAMES], axis=0), "w_dw": dwdw}


def _split_small_grads(arrays):
    out = {n: arrays["norms"][k] for k, n in enumerate(NORM_NAMES)}
    out.update({n: arrays["vecs"][k] for k, n in enumerate(VEC_NAMES)})
    out.update(gmlp_v_gain=arrays["gain_bias"][:NH], b_spatial=arrays["gain_bias"][NH:], w_spatial=arrays["w_spatial"],
               w_pool=arrays["w_pool"], w_dw=arrays["w_dw"][:CONV_K])
    return out


def _layer_params(small, l):
    p = {n: small[n][l].reshape(1, -1) for n in ("norm_mix_pre", "norm_mix_post", "s_pool", "b_dw", "conv_ln_g",
                                                   "conv_ln_b", "norm_xattn_pre", "norm_mem", "norm_xattn_post",
                                                   "norm_ffn_pre", "norm_ffn_post")}
    p["gmlp_v_gain"] = small["gmlp_v_gain"][l]
    p["w_spatial"] = small["w_spatial"][l]
    p["b_spatial_t"] = small["b_spatial"][l].T
    p["w_pool"] = small["w_pool"][l]
    p["w_dw"] = jnp.pad(small["w_dw"][l], ((0, 1), (0, 0)))
    return p


def _local_step(x, mem, target, fetch, small, red):
    small = dict(small)
    saved, weights, params = [], [], []
    h = x
    marker = ()
    for l in range(DEPTH):
        w = fetch(l, "in", marker)
        if "taps" in w:
            small["w_dw"] = w.pop("taps")
        p = _layer_params(small, l)
        h, sv = _layer_fwd(h, mem, w, p, functools.partial(fetch, l))
        marker = (h,)
        saved.append(sv)
        weights.append(w)
        params.append(p)
    dh, loss = _loss_head(h, target, 512)
    red.extra = (loss,)
    for l in reversed(range(DEPTH)):
        red.layer = l
        dh = _layer_bwd(dh, mem, weights[l], params[l], saved[l], red)
    return loss, dh


HBM = pl.BlockSpec(memory_space=pltpu.HBM)


def _position():
    return lax.axis_index("x"), lax.axis_index("y"), lax.axis_index("c")


SEM = pl.BlockSpec(memory_space=pltpu.SEMAPHORE)
EFFECT = pltpu.SideEffectType.DATAFLOW_SIDE_EFFECTING
TOKEN = jax.ShapeDtypeStruct((8, LANES), F32)
TOKEN_SPEC = pl.BlockSpec(memory_space=pltpu.VMEM)


def _landing(shape, dtype):
    return pltpu.with_memory_space_constraint(lax.empty(shape, dtype), pltpu.HBM)


def _hbm_shapes(arrays):
    return [pltpu.HBM(a.shape, a.dtype) for a in arrays]


def _block(ref, r, dev):
    return ref.at[pl.ds((4 * dev[0] + 2 * dev[1] + dev[2]) * r, r), :]


def _split_call(name, body, thru, sems_in, after, sems_out, token):
    n = len(thru)
    out_shape = [pltpu.SemaphoreType.DMA(s) for s in sems_out] + _hbm_shapes(thru) + ([TOKEN] if token else [])
    out_specs = [SEM] * len(sems_out) + [HBM] * n + ([TOKEN_SPEC] if token else [])
    return pl.pallas_call(
        body, name=name, in_specs=[HBM] * n + [SEM] * len(sems_in) + [ANY] * len(after),
        out_specs=out_specs, out_shape=out_shape,
        input_output_aliases={i: len(sems_out) + i for i in range(n)},
        compiler_params=pltpu.CompilerParams(has_side_effects=EFFECT),
    )(*thru, *sems_in, *after)


def _place_own(name, srcs, dev, out_dtype, tr):
    n = len(srcs)
    r, cols = srcs[0][0].shape[-2:]
    tr = r if r < 16 else _row_tile(r, tr)
    nb = r // tr

    def body(dev_ref, *refs):
        for a in range(n):
            refs[n + a][...] = refs[a][...].astype(out_dtype)

    in_specs = [pl.BlockSpec((tr, cols), lambda i, d: (i, 0)) if l is None
                else pl.BlockSpec((None, tr, cols), lambda i, d, l=l: (l, i, 0)) for _, l in srcs]
    return pl.pallas_call(
        body, name=name,
        grid_spec=pltpu.PrefetchScalarGridSpec(
            num_scalar_prefetch=1, grid=(nb,), in_specs=in_specs,
            out_specs=[pl.BlockSpec((tr, cols), lambda i, d: (d[0] * nb + i, 0))] * n),
        out_shape=[jax.ShapeDtypeStruct((N_DEV * r, cols), out_dtype)] * n, compiler_params=_cparams(),
    )(dev, *[a for a, _ in srcs])


def _gather_peers(x, y, c):
    return [(1 - x, y, c), (x, 1 - y, c), (1 - x, 1 - y, c), (x, y, 1 - c)]


def _block_rows(land):
    return land.shape[0] // N_DEV


def _gather_start(name, lands, after):
    n = len(lands)

    def body(*refs):
        lz = refs[:n]
        send_sems, recv_sems = refs[n + len(after)], refs[n + len(after) + 1]
        token = refs[-1]
        x, y, c = _position()
        for a in range(n):
            own = _block(lz[a], _block_rows(lands[a]), (x, y, c))
            for k, to in enumerate(_gather_peers(x, y, c)):
                pltpu.make_async_remote_copy(src_ref=own, dst_ref=own, send_sem=send_sems.at[k], recv_sem=recv_sems.at[k],
                                             device_id=to, device_id_type=MESH).start()
        token[...] = jnp.zeros_like(token)

    out = _split_call(name, body, list(lands), [], after, [(4,), (4,)], True)
    return out[0], out[1], out[2:2 + n], out[-1]


def _gather_forward(name, lands, recv_sems, after):
    n = len(lands)

    def body(*refs):
        lz = refs[:n]
        recv0 = refs[n]
        fsend, frecv = refs[n + 1 + len(after)], refs[n + 2 + len(after)]
        token = refs[-1]
        x, y, c = _position()
        chips = _gather_peers(x, y, c)[:3]
        for a in range(n):
            for j, chip in enumerate(chips):
                blk = _block(lz[a], _block_rows(lands[a]), chip)
                pltpu.make_async_remote_copy(src_ref=blk, dst_ref=blk, send_sem=fsend.at[j], recv_sem=recv0.at[j],
                                             device_id=(x, y, c), device_id_type=MESH).wait_recv()
        for a in range(n):
            for j, chip in enumerate(chips):
                blk = _block(lz[a], _block_rows(lands[a]), chip)
                pltpu.make_async_remote_copy(src_ref=blk, dst_ref=blk, send_sem=fsend.at[j], recv_sem=frecv.at[j],
                                             device_id=(x, y, 1 - c), device_id_type=MESH).start()
        token[...] = jnp.zeros_like(token)

    out = _split_call(name, body, list(lands), [recv_sems], after, [(3,), (3,)], True)
    return out[0], out[1], out[2:2 + n], out[-1]


def _gather_finish(name, lands, send_sems, recv_sems, fsend, frecv, after):
    n = len(lands)

    def body(*refs):
        lz = refs[:n]
        send0, recv0, fsend_ref, frecv_ref = refs[n:n + 4]
        x, y, c = _position()
        me = (x, y, c)
        chips = _gather_peers(x, y, c)[:3]
        for a in range(n):
            r = _block_rows(lands[a])
            sib = _block(lz[a], r, (x, y, 1 - c))
            pltpu.make_async_remote_copy(src_ref=sib, dst_ref=sib, send_sem=send0.at[3], recv_sem=recv0.at[3],
                                         device_id=me, device_id_type=MESH).wait_recv()
            for j, chip in enumerate(chips):
                blk = _block(lz[a], r, (chip[0], chip[1], 1 - c))
                pltpu.make_async_remote_copy(src_ref=blk, dst_ref=blk, send_sem=fsend_ref.at[j], recv_sem=frecv_ref.at[j],
                                             device_id=me, device_id_type=MESH).wait_recv()
            own = _block(lz[a], r, me)
            for k in range(4):
                pltpu.make_async_remote_copy(src_ref=own, dst_ref=own, send_sem=send0.at[k], recv_sem=recv0.at[k],
                                             device_id=me, device_id_type=MESH).wait_send()
            for j, chip in enumerate(chips):
                blk = _block(lz[a], r, chip)
                pltpu.make_async_remote_copy(src_ref=blk, dst_ref=blk, send_sem=fsend_ref.at[j], recv_sem=frecv_ref.at[j],
                                             device_id=me, device_id_type=MESH).wait_send()

    return _split_call(name, body, list(lands), [send_sems, recv_sems, fsend, frecv], after, [], False)


def _sibling_start(name, grads, after):
    n = len(grads)
    lands = [_landing((4, g.shape[0] // N_DEV, D), g.dtype) for g in grads]

    def body(*refs):
        ins, lz = refs[:n], refs[n:2 * n]
        send_sem, recv_sem = refs[2 * n + len(after)], refs[2 * n + len(after) + 1]
        token = refs[-1]
        x, y, c = _position()
        for a in range(n):
            r = grads[a].shape[0] // N_DEV
            for q in range(4):
                pltpu.make_async_remote_copy(
                    src_ref=ins[a].at[pl.ds((2 * q + 1 - c) * r, r), :], dst_ref=lz[a].at[q], send_sem=send_sem.at[0],
                    recv_sem=recv_sem.at[0], device_id=(x, y, 1 - c), device_id_type=MESH).start()
        token[...] = jnp.zeros_like(token)

    out = _split_call(name, body, list(grads) + lands, [], after, [(1,), (1,)], True)
    return out[0], out[1], out[2:2 + n], out[2 + n:2 + 2 * n], out[-1]


def _sibling_finish(name, grads, lands, send_sem, recv_sem, after):
    n = len(grads)

    def body(*refs):
        ins, lz = refs[:n], refs[n:2 * n]
        send_ref, recv_ref = refs[2 * n], refs[2 * n + 1]
        x, y, c = _position()
        for a in range(n):
            r = grads[a].shape[0] // N_DEV
            for q in range(4):
                cp = pltpu.make_async_remote_copy(
                    src_ref=ins[a].at[pl.ds((2 * q + 1 - c) * r, r), :], dst_ref=lz[a].at[q], send_sem=send_ref.at[0],
                    recv_sem=recv_ref.at[0], device_id=(x, y, c), device_id_type=MESH)
                cp.wait_send()
                cp.wait_recv()

    out = _split_call(name, body, list(grads) + list(lands), [send_sem, recv_sem], after, [], False)
    return out[:n], out[n:2 * n]


def _chip_start(name, parts, after):
    n = len(parts)
    lands = [_landing((3,) + p.shape[1:], p.dtype) for p in parts]

    def body(*refs):
        ins, lz = refs[:n], refs[n:2 * n]
        send_sems, recv_sems = refs[2 * n + len(after)], refs[2 * n + len(after) + 1]
        token = refs[-1]
        x, y, c = _position()
        for a in range(n):
            for j, chip in enumerate(_gather_peers(x, y, c)[:3]):
                pltpu.make_async_remote_copy(
                    src_ref=ins[a].at[2 * chip[0] + chip[1]], dst_ref=lz[a].at[j], send_sem=send_sems.at[j],
                    recv_sem=recv_sems.at[j], device_id=chip, device_id_type=MESH).start()
        token[...] = jnp.zeros_like(token)

    out = _split_call(name, body, list(parts) + lands, [], after, [(3,), (3,)], True)
    return out[0], out[1], out[2:2 + n], out[2 + n:2 + 2 * n], out[-1]


def _chip_finish(name, parts, lands, send_sems, recv_sems, after):
    n = len(parts)

    def body(*refs):
        ins, lz = refs[:n], refs[n:2 * n]
        send_ref, recv_ref = refs[2 * n], refs[2 * n + 1]
        me = _position()
        for a in range(n):
            for j in range(3):
                cp = pltpu.make_async_remote_copy(
                    src_ref=ins[a].at[j], dst_ref=lz[a].at[j], send_sem=send_ref.at[j], recv_sem=recv_ref.at[j],
                    device_id=me, device_id_type=MESH)
                cp.wait_send()
                cp.wait_recv()

    out = _split_call(name, body, list(parts) + list(lands), [send_sems, recv_sems], after, [], False)
    return out[:n], out[n:2 * n]


def _other_devices(x, y, c):
    return [(x + (k >> 2 & 1) * (1 - 2 * x), y + (k >> 1 & 1) * (1 - 2 * y), c + (k & 1) * (1 - 2 * c))
            for k in range(1, N_DEV)]


def _broadcast_start(name, arrays, after):
    n = len(arrays)
    lands = [_landing((N_DEV,) + a.shape, a.dtype) for a in arrays]

    def body(*refs):
        ins, lz = refs[:n], refs[n:2 * n]
        send_sems, recv_sems = refs[2 * n + len(after)], refs[2 * n + len(after) + 1]
        token = refs[-1]
        x, y, c = _position()
        for a in range(n):
            for k, peer in enumerate(_other_devices(x, y, c)):
                pltpu.make_async_remote_copy(
                    src_ref=ins[a], dst_ref=lz[a].at[4 * x + 2 * y + c], send_sem=send_sems.at[k],
                    recv_sem=recv_sems.at[k], device_id=peer, device_id_type=MESH).start()
        token[...] = jnp.zeros_like(token)

    out = _split_call(name, body, list(arrays) + lands, [], after, [(N_DEV - 1,), (N_DEV - 1,)], True)
    return out[0], out[1], out[2:2 + n], out[2 + n:2 + 2 * n], out[-1]


def _broadcast_finish(name, arrays, lands, send_sems, recv_sems, after):
    n = len(arrays)

    def body(*refs):
        ins, lz = refs[:n], refs[n:2 * n]
        send_ref, recv_ref = refs[2 * n], refs[2 * n + 1]
        x, y, c = _position()
        for a in range(n):
            for k, peer in enumerate(_other_devices(x, y, c)):
                cp = pltpu.make_async_remote_copy(
                    src_ref=ins[a], dst_ref=lz[a].at[4 * peer[0] + 2 * peer[1] + peer[2]], send_sem=send_ref.at[k],
                    recv_sem=recv_ref.at[k], device_id=(x, y, c), device_id_type=MESH)
                cp.wait_send()
                cp.wait_recv()

    out = _split_call(name, body, list(arrays) + list(lands), [send_sems, recv_sems], after, [], False)
    return out[:n], out[n:2 * n]


def _row_tile(r, target):
    return max(t for t in range(16, min(r, target) + 1, 16) if r % t == 0)


def _chip_partial(name, grad, got, c, tr):
    r = grad.shape[0] // N_DEV
    tr = _row_tile(r, tr)
    g4 = grad.reshape(4, 2, r, D)

    def body(c_ref, g_ref, s_ref, o_ref):
        o_ref[...] = (g_ref[...].astype(F32) + s_ref[...].astype(F32)).astype(BF16)

    return pl.pallas_call(
        body, name=name,
        grid_spec=pltpu.PrefetchScalarGridSpec(
            num_scalar_prefetch=1, grid=(4, r // tr),
            in_specs=[pl.BlockSpec((None, None, tr, D), lambda q, i, c_ref: (q, c_ref[0], i, 0)),
                      pl.BlockSpec((None, tr, D), lambda q, i, c_ref: (q, i, 0))],
            out_specs=pl.BlockSpec((None, tr, D), lambda q, i, c_ref: (q, i, 0))),
        out_shape=jax.ShapeDtypeStruct((4, r, D), BF16), compiler_params=_cparams(),
    )(c, g4, got)


class _WeightGather:
    def __init__(self, groups):
        self.state, token = {}, ()
        for key, names, lands in groups:
            send, recv, lz, tok = _gather_start("gather_start_%s_%d" % key[::-1], lands, token)
            self.state[key] = (names, send, recv, lz)
            token = (tok,)
        self.started = token

    def fetch(self, layer, group, marker):
        names, send, recv, lz = self.state.pop((layer, group))
        tag = "%s_%d" % (group, layer)
        fsend, frecv, lz, tok = _gather_forward("gather_forward_" + tag, lz, recv, marker or self.started)
        lz = _gather_finish("gather_finish_" + tag, lz, send, recv, fsend, frecv, (tok,))
        return dict(zip(names, lz))


class _GradReduce:
    def __init__(self, core, chip):
        self.core, self.chip = core, chip
        self.layer = None
        self.token = ()
        self.at_sibling, self.at_chips = [], []
        self.extra, self.smalls = (), {}

    def after(self):
        return self.token

    def add(self, group, names, grads):
        tag = "%s_%d" % (group, self.layer)
        send, recv, grads, lands, tok = _sibling_start("grad_sibling_start_" + tag, grads, self.token)
        self.at_sibling.append((tag, [(self.layer, n) for n in names], send, recv, grads, lands))
        self.token = (tok,)

    def advance(self, marker):
        for tag, keys, send, recv, grads, lands in self.at_sibling:
            grads, lands = _sibling_finish("grad_sibling_finish_" + tag, grads, lands, send, recv, marker)
            parts = [_chip_partial("chip_partial_%d_%s" % key, g, got, self.core, 256)
                     for key, g, got in zip(keys, grads, lands)]
            send, recv, parts, lands, tok = _chip_start("grad_chip_start_" + tag, parts, ())
            self.at_chips.append([tag, keys, send, recv, parts, lands])
            self.token = (tok,)
        self.at_sibling = []

    def small(self, arrays):
        extra = list(self.extra) if self.layer == 0 else []
        send, recv, own, slots, tok = _broadcast_start(
            "small_grads_start_%d" % self.layer, [arrays[k] for k in SMALL_ARRAYS] + extra, self.token)
        self.smalls[self.layer] = (send, recv, own, slots)
        self.token = (tok,)

    def small_finish(self, layer, marker):
        send, recv, own, slots = self.smalls[layer]
        return _broadcast_finish("small_grads_finish_%d" % layer, own, slots, send, recv, marker)

    def collect(self, key, marker):
        for entry in self.at_chips:
            tag, keys, send, recv, parts, lands = entry
            if key in keys:
                if send is not None:
                    parts, lands = _chip_finish("grad_chip_finish_" + tag, parts, lands, send, recv, marker)
                    entry[2:] = [None, None, parts, lands]
                i = keys.index(key)
                return parts[i], lands[i]
        raise KeyError(key)


def _adamw_math(w, g, m, v):
    m = ADAM_B1 * m + (1.0 - ADAM_B1) * g
    v = ADAM_B2 * v + (1.0 - ADAM_B2) * jnp.square(g)
    m_hat = m / (1.0 - ADAM_B1 ** ADAM_STEP)
    v_hat = v / (1.0 - ADAM_B2 ** ADAM_STEP)
    delta = -ADAM_LR * (m_hat / (jnp.sqrt(v_hat) + ADAM_EPS) + ADAM_WD * w)
    return delta, m, v


def _adamw_small(wts, mom_m, mom_v, own, gathered, loss_own, loss_gathered, dev):
    names = SMALL
    nw = len(names)
    na = len(SMALL_ARRAYS)

    def body(dev_ref, *refs):
        w_refs, m_refs, v_refs = (dict(zip(names, refs[i * nw:(i + 1) * nw])) for i in range(3))
        own_refs = refs[3 * nw:3 * nw + DEPTH * na]
        g_refs = refs[3 * nw + DEPTH * na:3 * nw + 2 * DEPTH * na]
        loss_own_ref, loss_got_ref = refs[3 * nw + 2 * DEPTH * na:3 * nw + 2 * DEPTH * na + 2]
        outs = refs[3 * nw + 2 * DEPTH * na + 2:]
        g_out, d_out, m_out, v_out = (dict(zip(names, outs[i * nw:(i + 1) * nw])) for i in range(4))
        me = dev_ref[0]

        loss = None
        for d in range(N_DEV):
            for b in range(loss_own.shape[0]):
                term = jnp.where(me == d, loss_own_ref[b], loss_got_ref[d, b])
                loss = term if loss is None else loss + term
        outs[4 * nw][...] = loss

        def update(name, at, g):
            g_out[name][at] = g
            d_out[name][at], m_out[name][at], v_out[name][at] = _adamw_math(
                w_refs[name][at], g, m_refs[name][at], v_refs[name][at])

        for l in range(DEPTH):
            mine = dict(zip(SMALL_ARRAYS, own_refs[l * na:(l + 1) * na]))
            got = dict(zip(SMALL_ARRAYS, g_refs[l * na:(l + 1) * na]))

            def total(key, at):
                acc = None
                for d in range(N_DEV):
                    term = jnp.where(me == d, mine[key][at] if at else mine[key][...], got[key][(d,) + at])
                    acc = term if acc is None else acc + term
                return acc

            row = (slice(l, l + 1),)
            for k, name in enumerate(NORM_NAMES):
                update(name, row, total("norms", (slice(k, k + 1),)))
            for k, name in enumerate(VEC_NAMES):
                update(name, row, total("vecs", (slice(k, k + 1),)))
            update("gmlp_v_gain", (l,), total("gain_bias", (slice(0, NH),)))
            update("b_spatial", (l,), total("gain_bias", (slice(NH, 2 * NH),)))
            update("w_spatial", (l,), total("w_spatial", ()))
            update("w_pool", (l,), total("w_pool", ()))
            update("w_dw", (l,), total("w_dw", (slice(0, CONV_K),)))

    args = [src[n] for src in (wts, mom_m, mom_v) for n in names]
    args += [src[l][k] for src in (own, gathered) for l in range(DEPTH) for k in SMALL_ARRAYS]
    args += [loss_own, loss_gathered]
    outs = pl.pallas_call(
        body, name="adamw_small",
        in_specs=[pl.BlockSpec(memory_space=pltpu.SMEM)] + [pl.BlockSpec(memory_space=pltpu.VMEM)] * len(args),
        out_shape=[jax.ShapeDtypeStruct(wts[n].shape, F32) for _ in range(4) for n in names]
        + [jax.ShapeDtypeStruct((8, LANES), F32)],
        compiler_params=_cparams(),
    )(dev, *args)
    return tuple(dict(zip(names, outs[i * nw:(i + 1) * nw])) for i in range(4)) + (outs[4 * nw],)


def _adamw_layers(name, w, reduced, m, v, chip, tr):
    nl, r, cdim = w.shape
    tr = _row_tile(r, tr)
    nb = r // tr

    def body(q_ref, w_ref, p0_ref, g0_ref, p1_ref, g1_ref, m_ref, v_ref, g_ref, d_ref, nm_ref, nv_ref):
        def total(p_ref, got_ref):
            acc = p_ref[...].astype(F32)
            for j in range(3):
                acc = acc + got_ref[j].astype(F32)
            return acc

        g = jnp.where(pl.program_id(0) == 0, total(p0_ref, g0_ref), total(p1_ref, g1_ref))
        g_ref[...] = g
        d_ref[...], nm_ref[...], nv_ref[...] = _adamw_math(w_ref[...], g, m_ref[...], v_ref[...])

    blk = pl.BlockSpec((None, tr, cdim), lambda l, i, q: (l, i, 0))
    first = lambda l, i: i * (1 - l) + (nb - 1) * l
    second = lambda l, i: i * l
    specs = [blk,
             pl.BlockSpec((None, tr, cdim), lambda l, i, q: (q[0], first(l, i), 0)),
             pl.BlockSpec((3, tr, cdim), lambda l, i, q: (0, first(l, i), 0)),
             pl.BlockSpec((None, tr, cdim), lambda l, i, q: (q[0], second(l, i), 0)),
             pl.BlockSpec((3, tr, cdim), lambda l, i, q: (0, second(l, i), 0)), blk, blk]
    shape = jax.ShapeDtypeStruct((nl, r, cdim), F32)
    return pl.pallas_call(
        body, name=name,
        grid_spec=pltpu.PrefetchScalarGridSpec(num_scalar_prefetch=1, grid=(nl, nb), in_specs=specs, out_specs=[blk] * 4),
        out_shape=[shape] * 4, compiler_params=_cparams(),
    )(chip, w, *reduced[0], *reduced[1], m, v)


def _to_rows(name, a):
    return jnp.swapaxes(a, 1, 2) if name in ("w_in", "w_up") else a


def _pack(arrays, rows):
    flat = jnp.concatenate([a.reshape(-1) for a in arrays])
    return jnp.pad(flat, (0, rows * D - flat.shape[0])).reshape(rows, D)


def _rows_for(shapes, mult=8):
    total = 0
    for shp in shapes:
        size = 1
        for dim in shp:
            size *= dim
        total += size
    return -(-total // (mult * D)) * mult


def kernel(x, mem, norm_mix_pre, norm_mix_post, w_in, w_out, gmlp_v_gain, w_spatial, b_spatial, w_pool, s_pool, w_dw, b_dw, conv_ln_g, conv_ln_b, norm_xattn_pre, norm_mem, norm_xattn_post, w_q, w_k, w_v, w_o, norm_ffn_pre, norm_ffn_post, w_up, w_down, loss_target, m_norm_mix_pre, m_norm_mix_post, m_w_in, m_w_out, m_gmlp_v_gain, m_w_spatial, m_b_spatial, m_w_pool, m_s_pool, m_w_dw, m_b_dw, m_conv_ln_g, m_conv_ln_b, m_norm_xattn_pre, m_norm_mem, m_norm_xattn_post, m_w_q, m_w_k, m_w_v, m_w_o, m_norm_ffn_pre, m_norm_ffn_post, m_w_up, m_w_down, v_norm_mix_pre, v_norm_mix_post, v_w_in, v_w_out, v_gmlp_v_gain, v_w_spatial, v_b_spatial, v_w_pool, v_s_pool, v_w_dw, v_b_dw, v_conv_ln_g, v_conv_ln_b, v_norm_xattn_pre, v_norm_mem, v_norm_xattn_post, v_w_q, v_w_k, v_w_v, v_w_o, v_norm_ffn_pre, v_norm_ffn_post, v_w_up, v_w_down):
    args = dict(locals())
    wts = {n: args[n] for n in WEIGHTS}
    mom_m = {n: args["m_" + n] for n in WEIGHTS}
    mom_v = {n: args["v_" + n] for n in WEIGHTS}
    xi, yi, ci = _position()
    me = 4 * xi + 2 * yi + ci

    dev = jnp.reshape(me, (1,)).astype(jnp.int32)
    lands = {}
    for call, names, tr in (("place_att", ("w_out", "w_q", "w_k", "w_v", "w_o"), 64), ("place_down", ("w_down",), 256),
                            ("place_up", ("w_up",), 256), ("place_in", ("w_in",), 256)):
        srcs = [(_to_rows(n, wts[n]), l) for l in range(DEPTH) for n in names]
        placed = _place_own(call, srcs, dev, BF16, tr)
        lands.update(zip([(l, n) for l in range(DEPTH) for n in names], placed))
    (lands[(0, "taps")],) = _place_own("place_taps", [(_pack([w_dw], _rows_for([w_dw.shape])), None)], dev, F32, 8)
    groups = []
    for l in range(DEPTH):
        for group, names in GATHER_GROUPS:
            if (l, group) == (0, "in"):
                names = names + ("taps",)
            groups.append(((l, group), names, [lands[(l, n)] for n in names]))
    gather = _WeightGather(groups)

    def fetch(layer, group, marker):
        w = gather.fetch(layer, group, marker)
        if "taps" in w:
            blocks = w["taps"].reshape(N_DEV, -1)[:, :w_dw.size].reshape((N_DEV,) + w_dw.shape)
            w["taps"] = jnp.moveaxis(blocks, 0, 2).reshape(DEPTH, CONV_K, CW)
        return w

    reduce = _GradReduce(jnp.reshape(ci, (1,)).astype(jnp.int32), jnp.reshape(2 * xi + yi, (1,)).astype(jnp.int32))
    small = {n: wts[n] for n in SMALL if n != "w_dw"}
    _, dx = _local_step(x[0], mem[0], loss_target[0], fetch, small, reduce)
    reduce.advance((dx,))

    grad_w, delta, new_m, new_v = {}, {}, {}, {}
    marker = (dx,) + tuple(reduce.after())
    for n in UPDATE_ORDER:
        reduced = [reduce.collect((l, n), marker) for l in range(DEPTH)]
        outs = _adamw_layers("adamw_" + n, _to_rows(n, wts[n]), reduced, _to_rows(n, mom_m[n]), _to_rows(n, mom_v[n]),
                             reduce.chip, 256)
        grad_w[n], delta[n], new_m[n], new_v[n] = (_to_rows(n, o) for o in outs)
        marker = (outs[1],)

    own, slots = [None] * DEPTH, [None] * DEPTH
    for l in reversed(range(DEPTH)):
        mine, theirs = reduce.small_finish(l, marker)
        if l == 0:
            loss_own, loss_slots = mine[-1], theirs[-1]
        own[l], slots[l] = dict(zip(SMALL_ARRAYS, mine)), dict(zip(SMALL_ARRAYS, theirs))
    shard_cols = CW // N_DEV
    for l in range(DEPTH):
        own[l]["w_dw"] = lax.dynamic_slice_in_dim(own[l]["w_dw"], me * shard_cols, shard_cols, axis=1)
        slots[l]["w_dw"] = lax.dynamic_slice_in_dim(slots[l]["w_dw"], me * shard_cols, shard_cols, axis=2)
    *small_out, loss_tile = _adamw_small(wts, mom_m, mom_v, own, slots, loss_own, loss_slots, dev)
    for dst, src in zip((grad_w, delta, new_m, new_v), small_out):
        dst.update(src)

    return (loss_tile[0, 0], dx[None], *[grad_w[n] for n in WEIGHTS], *[delta[n] for n in WEIGHTS],
            *[new_m[n] for n in WEIGHTS], *[new_v[n] for n in WEIGHTS])
```

```python
import functools

import jax
import jax.numpy as jnp
from jax import lax
from jax.experimental import pallas as pl
from jax.experimental.pallas import tpu as pltpu

F32 = jnp.float32
BF16 = jnp.bfloat16

D = 2048
GW = 1024
PW = 512
CW = 512
HD = 128
NH = 8
NG = 4
POOL_WINDOWS = (2, 4, 8, 16)
CONV_K = 31
IN_COLS = 2 * GW + PW + 2 * CW
DFF = 4 * D
XH = 4
XHD = D // XH
ATT_SCALE = XHD ** -0.5
RMS_EPS = 1e-6
LN_EPS = 1e-5
DEPTH = 2
N_DEV = 8

ADAM_LR = 0.001
ADAM_B1 = 0.9
ADAM_B2 = 0.999
ADAM_EPS = 1e-08
ADAM_WD = 0.01
ADAM_STEP = 10

LANES = 128
CONV_HALO = 32
POOL_HALO = 16
ROW_TILE = 128
VMEM_LIMIT = 60 * 1024 * 1024

MESH = pl.DeviceIdType.MESH
NT = (((1,), (1,)), ((), ()))
NN = (((1,), (0,)), ((), ()))
TN = (((0,), (0,)), ((), ()))

BIG = ("w_out", "w_q", "w_k", "w_v", "w_o", "w_up", "w_down", "w_in")
UPDATE_ORDER = ("w_down", "w_up", "w_o", "w_q", "w_k", "w_v", "w_out", "w_in")
GATHER_GROUPS = (("in", ("w_in",)), ("out", ("w_out",)), ("att", ("w_q", "w_k", "w_v", "w_o")), ("up", ("w_up",)),
                 ("down", ("w_down",)))
SMALL = ("norm_mix_pre", "norm_mix_post", "gmlp_v_gain", "w_spatial", "b_spatial", "w_pool", "s_pool",
         "w_dw", "b_dw", "conv_ln_g", "conv_ln_b", "norm_xattn_pre", "norm_mem", "norm_xattn_post",
         "norm_ffn_pre", "norm_ffn_post")
WEIGHTS = ("norm_mix_pre", "norm_mix_post", "w_in", "w_out", "gmlp_v_gain", "w_spatial", "b_spatial", "w_pool",
           "s_pool", "w_dw", "b_dw", "conv_ln_g", "conv_ln_b", "norm_xattn_pre", "norm_mem", "norm_xattn_post",
           "w_q", "w_k", "w_v", "w_o", "norm_ffn_pre", "norm_ffn_post", "w_up", "w_down")


def _cparams():
    return pltpu.CompilerParams(vmem_limit_bytes=VMEM_LIMIT)


def _dot(a, b, dims):
    return lax.dot_general(a, b, dims, preferred_element_type=F32)


def _rms(x, g):
    y = x * lax.rsqrt(jnp.mean(x * x, axis=-1, keepdims=True) + RMS_EPS)
    return y * g


def _gelu(x):
    cdf = 0.5 * (1.0 + jnp.tanh(0.7978845608028654 * (x + 0.044715 * (x * x * x))))
    return x * cdf


def _layer_norm(x, g, b=None):
    mu = jnp.mean(x, axis=-1, keepdims=True)
    xc = x - mu
    var = jnp.mean(xc * xc, axis=-1, keepdims=True)
    y = xc * lax.rsqrt(var + LN_EPS) * g
    return y if b is None else y + b


def _sigmoid(x):
    return 1.0 / (1.0 + jnp.exp(-x))


def _gmlp_rows(zu, zv, gv):
    return _gelu(zu), _layer_norm(_gelu(zv), gv)


def _glu(cv, cg):
    return cv * _sigmoid(cg)


def _ln_silu(h, g, b):
    y = _layer_norm(h, g, b)
    return y * _sigmoid(y)


ANY = pl.BlockSpec(memory_space=pl.ANY)


ROWS_TILE = 256
COLS_TILE = 512
DW_TILE = 512
RESIDENT_K = 2048
STREAM_K_TILE = 1024
STREAM_ROWS = 512


def _k_tiles(kdim):
    if kdim <= RESIDENT_K:
        return ROWS_TILE, kdim
    return STREAM_ROWS, max(t for t in range(LANES, STREAM_K_TILE + 1, LANES) if kdim % t == 0)


def _rowop_mm(name, kind, rows, g, w, dims, out_dtype, u=None, after=()):
    s = rows[0].shape[0]
    n = w.shape[0] if dims == NT else w.shape[1]
    tm, tn = min(ROWS_TILE, s), min(COLS_TILE, n)
    ni = s // tm
    bwd = kind == "rms_bwd"

    def rows_body(*refs):
        refs = list(refs)
        row_refs = [refs.pop(0) for _ in rows]
        g_ref = refs.pop(0)
        del refs[:len(after)]
        if bwd:
            _, vjp = jax.vjp(_rms, row_refs[0][...], g_ref[...])
            a, dg = vjp(row_refs[1][...])
            refs[1][0] = dg
        else:
            a = _rms(row_refs[0][...], g_ref[...])
        refs[0][...] = a.astype(BF16)

    row_spec = pl.BlockSpec((tm, D), lambda i: (i, 0))
    res = pl.pallas_call(
        rows_body, name=name + "_rows", grid=(ni,),
        in_specs=[row_spec] * len(rows) + [pl.BlockSpec((1, D), lambda i: (0, 0))] + [ANY] * len(after),
        out_specs=[row_spec] + ([pl.BlockSpec((1, 1, D), lambda i: (i, 0, 0))] if bwd else []),
        out_shape=[jax.ShapeDtypeStruct((s, D), BF16)] + ([jax.ShapeDtypeStruct((ni, 1, D), F32)] if bwd else []),
        compiler_params=_cparams(),
    )(*rows, g, *after)
    a = res[0]

    def body(a_ref, w_ref, *rest):
        acc = _dot(a_ref[...], w_ref[...], dims)
        if u is not None:
            acc = acc * (2.0 * jnp.maximum(rest[0][...], 0.0))
        rest[-1][...] = acc.astype(out_dtype)

    w_spec = pl.BlockSpec((tn, D), lambda j: (j, 0)) if dims == NT else pl.BlockSpec((D, tn), lambda j: (0, j))
    tile = pl.BlockSpec((s, tn), lambda j: (0, j))
    out = pl.pallas_call(
        body, name=name, grid=(n // tn,),
        in_specs=[pl.BlockSpec((s, D), lambda j: (0, 0)), w_spec] + ([tile] if u is not None else []),
        out_specs=tile, out_shape=jax.ShapeDtypeStruct((s, n), out_dtype), compiler_params=_cparams(),
    )(a, w, *([u] if u is not None else []))
    return (out, *res)


def _mm_rowop(name, kind, pairs, rows, g, relu2=False, after=()):
    s, kdim = pairs[0][0].shape
    tm, tk = _k_tiles(kdim)
    tm = min(tm, s)
    ni, nk = s // tm, kdim // tk
    npair = len(pairs)

    def body(*refs):
        refs = list(refs)
        a_refs = [refs.pop(0) for _ in range(npair)]
        w_refs = [refs.pop(0) for _ in range(npair)]
        row_refs = [refs.pop(0) for _ in rows]
        g_ref = refs.pop(0)
        del refs[:len(after)]
        acc = refs.pop()
        outs = refs
        k = pl.program_id(1)

        @pl.when(k == 0)
        def _():
            acc[...] = jnp.zeros_like(acc)

        for a_ref, w_ref, (_, _, dims) in zip(a_refs, w_refs, pairs):
            a = a_ref[...]
            if relu2:
                a = jnp.square(jnp.maximum(a, 0.0))
            acc[...] += _dot(a.astype(BF16), w_ref[...], dims)

        @pl.when(k == nk - 1)
        def _():
            h = acc[...]
            if kind == "rms_res":
                outs[0][...] = row_refs[0][...] + _rms(h, g_ref[...])
                outs[1][...] = h
            else:
                _, vjp = jax.vjp(_rms, row_refs[0][...], g_ref[...])
                dx, dg = vjp(h)
                if kind == "rms_bwd_res":
                    outs[0][...] = row_refs[1][...] + dx
                    outs[1][0] = dg
                else:
                    outs[0][0] = dg

    row_spec = pl.BlockSpec((tm, D), lambda i, k: (i, 0))
    dg_shape = jax.ShapeDtypeStruct((ni, 1, D), F32)
    dg_spec = pl.BlockSpec((1, 1, D), lambda i, k: (i, 0, 0))
    in_specs = [pl.BlockSpec((tm, tk), lambda i, k: (i, k))] * npair
    for _, _, dims in pairs:
        in_specs.append(pl.BlockSpec((tk, D), lambda i, k: (k, 0)) if dims == NN
                        else pl.BlockSpec((D, tk), lambda i, k: (0, k)))
    in_specs += [row_spec] * len(rows) + [pl.BlockSpec((1, D), lambda i, k: (0, 0))] + [ANY] * len(after)
    if kind == "rms_res":
        out_shape = [jax.ShapeDtypeStruct((s, D), F32)] * 2
        out_specs = [row_spec, row_spec]
    elif kind == "rms_bwd_res":
        out_shape = [jax.ShapeDtypeStruct((s, D), F32), dg_shape]
        out_specs = [row_spec, dg_spec]
    else:
        out_shape = [dg_shape]
        out_specs = [dg_spec]
    return pl.pallas_call(
        body, name=name, grid=(ni, nk), in_specs=in_specs, out_specs=out_specs, out_shape=out_shape,
        scratch_shapes=[pltpu.VMEM((tm, D), F32)], compiler_params=_cparams(),
    )(*[p[0] for p in pairs], *[p[1] for p in pairs], *rows, g, *after)


def _mm_tn(name, a, gmat, relu2=False, after=()):
    s, m = a.shape
    tm, ts = min(DW_TILE, m), s
    ni, ns = m // tm, s // ts

    def body(a_ref, g_ref, *rest):
        o_ref, acc = rest[len(after):]
        k = pl.program_id(1)

        @pl.when(k == 0)
        def _():
            acc[...] = jnp.zeros_like(acc)

        av = a_ref[...]
        if relu2:
            av = jnp.square(jnp.maximum(av, 0.0))
        acc[...] += _dot(av.astype(BF16), g_ref[...], TN)

        @pl.when(k == ns - 1)
        def _():
            o_ref[...] = acc[...].astype(BF16)

    return pl.pallas_call(
        body, name=name, grid=(ni, ns),
        in_specs=[pl.BlockSpec((ts, tm), lambda i, k: (k, i)), pl.BlockSpec((ts, D), lambda i, k: (k, 0))]
        + [ANY] * len(after),
        out_specs=pl.BlockSpec((tm, D), lambda i, k: (i, 0)),
        out_shape=jax.ShapeDtypeStruct((m, D), BF16),
        scratch_shapes=[pltpu.VMEM((tm, D), F32)], compiler_params=_cparams(),
    )(a, gmat, *after)


def _tril():
    r = lax.broadcasted_iota(jnp.int32, (HD, HD), 0)
    c = lax.broadcasted_iota(jnp.int32, (HD, HD), 1)
    return (c <= r).astype(F32)


def _gmlp_fwd(z, gv, ws, bst, tb):
    s = z.shape[0]
    tb = min(tb, s)

    def body(zu_ref, zv_ref, gv_ref, ws_ref, bst_ref, y_ref):
        tril = _tril()
        for h in range(NH):
            cols = slice(h * HD, (h + 1) * HD)
            u, vln = _gmlp_rows(zu_ref[:, cols], zv_ref[:, cols], gv_ref[h:h + 1, :])
            wm = (ws_ref[h] * tril).astype(BF16)
            vb = vln.astype(BF16)
            for c in range(tb // HD):
                rws = slice(c * HD, (c + 1) * HD)
                mixed = _dot(wm, vb[rws], NN) + bst_ref[:, h:h + 1]
                y_ref[rws, cols] = (u[rws] * mixed).astype(BF16)

    return pl.pallas_call(
        body, name="gmlp_fwd", grid=(s // tb,),
        in_specs=[pl.BlockSpec((tb, GW), lambda i: (i, 0)), pl.BlockSpec((tb, GW), lambda i: (i, 1)),
                  pl.BlockSpec((NH, HD), lambda i: (0, 0)), pl.BlockSpec((NH, HD, HD), lambda i: (0, 0, 0)),
                  pl.BlockSpec((HD, NH), lambda i: (0, 0))],
        out_specs=pl.BlockSpec((tb, GW), lambda i: (i, 0)),
        out_shape=jax.ShapeDtypeStruct((s, D), BF16), compiler_params=_cparams(),
    )(z, z, gv, ws, bst)


def _gmlp_bwd(z, dy, gv, ws, bst, tb):
    s = z.shape[0]
    tb = min(tb, s)
    nb = s // tb

    def body(zu_ref, zv_ref, dy_ref, gv_ref, ws_ref, bst_ref, dz_ref, dgv_ref, dws_ref, db_ref):
        tril = _tril()
        for h in range(NH):
            cols = slice(h * HD, (h + 1) * HD)
            (u, vln), vjp = jax.vjp(_gmlp_rows, zu_ref[:, cols], zv_ref[:, cols], gv_ref[h:h + 1, :])
            wmf = ws_ref[h] * tril
            wm = wmf.astype(BF16)
            wmt = wmf.T.astype(BF16)
            vb = vln.astype(BF16)
            dws = jnp.zeros((HD, HD), F32)
            db = jnp.zeros((HD, 1), F32)
            du_parts, dvln_parts = [], []
            for c in range(tb // HD):
                rws = slice(c * HD, (c + 1) * HD)
                mixed = _dot(wm, vb[rws], NN) + bst_ref[:, h:h + 1]
                dyc = dy_ref[rws, cols]
                du_parts.append(dyc * mixed)
                dmixed = dyc * u[rws]
                dmb = dmixed.astype(BF16)
                dws = dws + _dot(dmb, vb[rws], NT)
                db = db + jnp.sum(dmixed, axis=1, keepdims=True)
                dvln_parts.append(_dot(wmt, dmb, NN))
            du = jnp.concatenate(du_parts, axis=0)
            dvln = jnp.concatenate(dvln_parts, axis=0)
            dzu, dzv, dgv = vjp((du, dvln))
            dz_ref[:, cols] = dzu.astype(BF16)
            dz_ref[:, slice(GW + h * HD, GW + (h + 1) * HD)] = dzv.astype(BF16)
            dgv_ref[0, h:h + 1, :] = dgv
            dws_ref[0, h] = dws * tril
            db_ref[0, h] = jnp.broadcast_to(db, (HD, LANES))

    blk = pl.BlockSpec((tb, GW), lambda i: (i, 0))
    return pl.pallas_call(
        body, name="gmlp_bwd", grid=(nb,),
        in_specs=[blk, pl.BlockSpec((tb, GW), lambda i: (i, 1)), blk,
                  pl.BlockSpec((NH, HD), lambda i: (0, 0)), pl.BlockSpec((NH, HD, HD), lambda i: (0, 0, 0)),
                  pl.BlockSpec((HD, NH), lambda i: (0, 0))],
        out_specs=[pl.BlockSpec((tb, 2 * GW), lambda i: (i, 0)), pl.BlockSpec((1, NH, HD), lambda i: (i, 0, 0)),
                   pl.BlockSpec((1, NH, HD, HD), lambda i: (i, 0, 0, 0)),
                   pl.BlockSpec((1, NH, HD, LANES), lambda i: (i, 0, 0, 0))],
        out_shape=[jax.ShapeDtypeStruct((s, IN_COLS), BF16),
                   jax.ShapeDtypeStruct((nb, NH, HD), F32), jax.ShapeDtypeStruct((nb, NH, HD, HD), F32),
                   jax.ShapeDtypeStruct((nb, NH, HD, LANES), F32)],
        compiler_params=_cparams(),
    )(z, z, dy, gv, ws, bst)


def _pool_count(t0, window):
    pos = (t0 + lax.broadcasted_iota(jnp.int32, (ROW_TILE, LANES), 0)).astype(F32)
    return jnp.minimum(pos + 1.0, float(window))


def _window_sum(win, levels, back):
    n = win.shape[0]
    for lv in range(levels):
        step = 1 << lv
        win = win + pltpu.roll(win, n - step if back else step, 0)
    return win


def _pool_pooled(ppad_ref, t0, g):
    win = ppad_ref[pl.ds(t0, ROW_TILE + POOL_HALO), :]
    wsum = _window_sum(win, g + 1, False)[POOL_HALO:]
    return wsum / _pool_count(t0, POOL_WINDOWS[g]) - win[POOL_HALO:]


def _pool_fwd(z, wp, sp, y):
    s = z.shape[0]
    nt = s // ROW_TILE

    def body(p_ref, wp_ref, sp_ref, _, y_ref, ppad):
        for g in range(NG):
            cols = slice(g * LANES, (g + 1) * LANES)
            ppad[pl.ds(0, POOL_HALO), :] = jnp.zeros((POOL_HALO, LANES), F32)
            ppad[pl.ds(POOL_HALO, s), :] = p_ref[:, cols]
            wpb = wp_ref[g].astype(BF16)
            scale = sp_ref[:, cols]

            def tile(t, carry):
                t0 = pl.multiple_of(t * ROW_TILE, ROW_TILE)
                pooled = _pool_pooled(ppad, t0, g)
                y_ref[pl.ds(t0, ROW_TILE), cols] = (_dot(pooled.astype(BF16), wpb, NN) * scale).astype(BF16)
                return carry

            lax.fori_loop(0, nt, tile, 0)

    return pl.pallas_call(
        body, name="pool_fwd", grid=(1,),
        in_specs=[pl.BlockSpec((s, PW), lambda i: (0, 2 * GW // PW)),
                  pl.BlockSpec((NG, LANES, LANES), lambda i: (0, 0, 0)), pl.BlockSpec((1, PW), lambda i: (0, 0)), ANY],
        out_specs=pl.BlockSpec((s, PW), lambda i: (0, GW // PW)),
        out_shape=jax.ShapeDtypeStruct((s, D), BF16), input_output_aliases={3: 0},
        scratch_shapes=[pltpu.VMEM((s + POOL_HALO, LANES), F32)], compiler_params=_cparams(),
    )(z, wp, sp, y)


def _pool_bwd(z, dy, wp, sp, dz):
    s = z.shape[0]
    nt = s // ROW_TILE

    def body(p_ref, dy_ref, wp_ref, sp_ref, _, dp_ref, dwp_ref, dsp_ref, ppad, rpad, dpool):
        for g in range(NG):
            cols = slice(g * LANES, (g + 1) * LANES)
            ppad[pl.ds(0, POOL_HALO), :] = jnp.zeros((POOL_HALO, LANES), F32)
            ppad[pl.ds(POOL_HALO, s), :] = p_ref[:, cols]
            rpad[pl.ds(s, POOL_HALO), :] = jnp.zeros((POOL_HALO, LANES), F32)
            wpb = wp_ref[g].astype(BF16)
            scale = sp_ref[:, cols]

            def tile(t, carry):
                dwp, dsp = carry
                t0 = pl.multiple_of(t * ROW_TILE, ROW_TILE)
                pooled = _pool_pooled(ppad, t0, g)
                pb = pooled.astype(BF16)
                dyt = dy_ref[pl.ds(t0, ROW_TILE), cols]
                dsp = dsp + jnp.sum(dyt * _dot(pb, wpb, NN), axis=0, keepdims=True)
                dmm = (dyt * scale).astype(BF16)
                dwp = dwp + _dot(pb, dmm, TN)
                dpooled = _dot(dmm, wpb, NT)
                rpad[pl.ds(t0, ROW_TILE), :] = dpooled / _pool_count(t0, POOL_WINDOWS[g])
                dpool[pl.ds(t0, ROW_TILE), :] = dpooled
                return dwp, dsp

            dwp, dsp = lax.fori_loop(0, nt, tile, (jnp.zeros((LANES, LANES), F32), jnp.zeros((1, LANES), F32)))
            dwp_ref[g] = dwp
            dsp_ref[:, cols] = dsp

            def tile2(t, carry):
                t0 = pl.multiple_of(t * ROW_TILE, ROW_TILE)
                win = rpad[pl.ds(t0, ROW_TILE + POOL_HALO), :]
                back = _window_sum(win, g + 1, True)[:ROW_TILE]
                rows = pl.ds(t0, ROW_TILE)
                dp_ref[rows, cols] = (back - dpool[rows, :]).astype(BF16)
                return carry

            lax.fori_loop(0, nt, tile2, 0)

    return pl.pallas_call(
        body, name="pool_bwd", grid=(1,),
        in_specs=[pl.BlockSpec((s, PW), lambda i: (0, 2 * GW // PW)), pl.BlockSpec((s, PW), lambda i: (0, GW // PW)),
                  pl.BlockSpec((NG, LANES, LANES), lambda i: (0, 0, 0)), pl.BlockSpec((1, PW), lambda i: (0, 0)), ANY],
        out_specs=[pl.BlockSpec((s, PW), lambda i: (0, 2 * GW // PW)),
                   pl.BlockSpec((NG, LANES, LANES), lambda i: (0, 0, 0)), pl.BlockSpec((1, PW), lambda i: (0, 0))],
        out_shape=[jax.ShapeDtypeStruct((s, IN_COLS), BF16), jax.ShapeDtypeStruct((NG, LANES, LANES), F32),
                   jax.ShapeDtypeStruct((1, PW), F32)],
        input_output_aliases={4: 0},
        scratch_shapes=[pltpu.VMEM((s + POOL_HALO, LANES), F32), pltpu.VMEM((s + POOL_HALO, LANES), F32),
                        pltpu.VMEM((s, LANES), F32)],
        compiler_params=_cparams(),
    )(z, dy, wp, sp, dz)


CONV_LEAD = CONV_HALO - (CONV_K - 1)


def _conv_taps(win, wdw_ref, lead, reverse):
    n = win.shape[0]
    acc = jnp.zeros((ROW_TILE, CW), F32)
    for j in range(CONV_K):
        tap = (CONV_K - 1 - j) if reverse else j
        acc = acc + wdw_ref[tap:tap + 1, :] * pltpu.roll(win, (n - (lead + j)) % n, 0)[:ROW_TILE]
    return acc


def _conv_fill_glu(cv_ref, cg_ref, xpad, s):
    xpad[pl.ds(0, CONV_HALO), :] = jnp.zeros((CONV_HALO, CW), F32)

    def fill(t, carry):
        t0 = pl.multiple_of(t * ROW_TILE, ROW_TILE)
        rows = pl.ds(t0, ROW_TILE)
        xpad[pl.ds(t0 + CONV_HALO, ROW_TILE), :] = _glu(cv_ref[rows, :], cg_ref[rows, :])
        return carry

    lax.fori_loop(0, s // ROW_TILE, fill, 0)


def _conv_fwd(z, wdw, bdw, lng, lnb, y):
    s = z.shape[0]

    def body(cv_ref, cg_ref, wdw_ref, bdw_ref, lng_ref, lnb_ref, _, y_ref, xpad):
        _conv_fill_glu(cv_ref, cg_ref, xpad, s)

        def tile(t, carry):
            t0 = pl.multiple_of(t * ROW_TILE, ROW_TILE)
            win = xpad[pl.ds(t0, ROW_TILE + CONV_HALO), :]
            hc = _conv_taps(win, wdw_ref, CONV_LEAD, False) + bdw_ref[...]
            y_ref[pl.ds(t0, ROW_TILE), :] = _ln_silu(hc, lng_ref[...], lnb_ref[...]).astype(BF16)
            return carry

        lax.fori_loop(0, s // ROW_TILE, tile, 0)

    vec = pl.BlockSpec((1, CW), lambda i: (0, 0))
    return pl.pallas_call(
        body, name="conv_fwd", grid=(1,),
        in_specs=[pl.BlockSpec((s, CW), lambda i: (0, (2 * GW + PW) // CW)),
                  pl.BlockSpec((s, CW), lambda i: (0, (2 * GW + PW) // CW + 1)),
                  pl.BlockSpec((CONV_K + 1, CW), lambda i: (0, 0)), vec, vec, vec, ANY],
        out_specs=pl.BlockSpec((s, CW), lambda i: (0, (GW + PW) // CW)),
        out_shape=jax.ShapeDtypeStruct((s, D), BF16), input_output_aliases={6: 0},
        scratch_shapes=[pltpu.VMEM((s + CONV_HALO, CW), F32)], compiler_params=_cparams(),
    )(z, z, wdw, bdw, lng, lnb, y)


def _conv_bwd(z, dy, wdw, bdw, lng, lnb, dz):
    s = z.shape[0]

    def body(cv_ref, cg_ref, dy_ref, wdw_ref, bdw_ref, lng_ref, lnb_ref, _,
             dz_ref, dwdw_ref, dbdw_ref, dlng_ref, dlnb_ref, xpad, dpad, dcg_keep):
        @pl.when(pl.program_id(0) == 0)
        def _():
            compute(cv_ref, cg_ref, dy_ref, wdw_ref, bdw_ref, lng_ref, lnb_ref,
                    dz_ref, dcg_keep, dwdw_ref, dbdw_ref, dlng_ref, dlnb_ref, xpad, dpad)

        @pl.when(pl.program_id(0) == 1)
        def _():
            dz_ref[...] = dcg_keep[...]

    def compute(cv_ref, cg_ref, dy_ref, wdw_ref, bdw_ref, lng_ref, lnb_ref,
                dcv_ref, dcg_ref, dwdw_ref, dbdw_ref, dlng_ref, dlnb_ref, xpad, dpad):
        _conv_fill_glu(cv_ref, cg_ref, xpad, s)
        dpad[pl.ds(s, CONV_HALO), :] = jnp.zeros((CONV_HALO, CW), F32)
        dwdw_ref[...] = jnp.zeros((CONV_K + 1, CW), F32)

        def tile(t, carry):
            db, dg, dbeta = carry
            t0 = pl.multiple_of(t * ROW_TILE, ROW_TILE)
            win = xpad[pl.ds(t0, ROW_TILE + CONV_HALO), :]
            hc = _conv_taps(win, wdw_ref, CONV_LEAD, False) + bdw_ref[...]
            _, vjp = jax.vjp(_ln_silu, hc, lng_ref[...], lnb_ref[...])
            dhc, dg_t, dbeta_t = vjp(dy_ref[pl.ds(t0, ROW_TILE), :])
            dpad[pl.ds(t0, ROW_TILE), :] = dhc
            n = win.shape[0]
            for j in range(CONV_K):
                shifted = pltpu.roll(win, (n - (CONV_LEAD + j)) % n, 0)[:ROW_TILE]
                dwdw_ref[j:j + 1, :] += jnp.sum(dhc * shifted, axis=0, keepdims=True)
            return db + jnp.sum(dhc, axis=0, keepdims=True), dg + dg_t, dbeta + dbeta_t

        zero = jnp.zeros((1, CW), F32)
        db, dg, dbeta = lax.fori_loop(0, s // ROW_TILE, tile, (zero, zero, zero))
        dbdw_ref[...] = db
        dlng_ref[...] = dg
        dlnb_ref[...] = dbeta

        def tile2(t, carry):
            t0 = pl.multiple_of(t * ROW_TILE, ROW_TILE)
            rows = pl.ds(t0, ROW_TILE)
            win = dpad[pl.ds(t0, ROW_TILE + CONV_HALO), :]
            dglu = _conv_taps(win, wdw_ref, 0, True)
            _, vjp = jax.vjp(_glu, cv_ref[rows, :], cg_ref[rows, :])
            dcv, dcg = vjp(dglu)
            dcv_ref[rows, :] = dcv.astype(BF16)
            dcg_ref[rows, :] = dcg.astype(BF16)
            return carry

        lax.fori_loop(0, s // ROW_TILE, tile2, 0)

    vec = pl.BlockSpec((1, CW), lambda i: (0, 0))
    wspec = pl.BlockSpec((CONV_K + 1, CW), lambda i: (0, 0))
    vshape = jax.ShapeDtypeStruct((1, CW), F32)
    return pl.pallas_call(
        body, name="conv_bwd", grid=(2,),
        in_specs=[pl.BlockSpec((s, CW), lambda i: (0, (2 * GW + PW) // CW)),
                  pl.BlockSpec((s, CW), lambda i: (0, (2 * GW + PW) // CW + 1)),
                  pl.BlockSpec((s, CW), lambda i: (0, (GW + PW) // CW)), wspec, vec, vec, vec, ANY],
        out_specs=[pl.BlockSpec((s, CW), lambda i: (0, (2 * GW + PW) // CW + i)), wspec, vec, vec, vec],
        out_shape=[jax.ShapeDtypeStruct((s, IN_COLS), BF16), jax.ShapeDtypeStruct((CONV_K + 1, CW), F32),
                   vshape, vshape, vshape],
        input_output_aliases={7: 0},
        scratch_shapes=[pltpu.VMEM((s + CONV_HALO, CW), F32), pltpu.VMEM((s + CONV_HALO, CW), F32),
                        pltpu.VMEM((s, CW), BF16)],
        compiler_params=_cparams(),
    )(z, z, dy, wdw, bdw, lng, lnb, dz)


def _softmax_rows(sc):
    e = jnp.exp(sc - jnp.max(sc, axis=-1, keepdims=True))
    return e / jnp.sum(e, axis=-1, keepdims=True)


def _attn_fwd(q, k, v, tq):
    s, m = q.shape[0], k.shape[0]
    tq = min(tq, s)

    def body(q_ref, k_ref, v_ref, o_ref):
        for h in range(XH):
            cols = slice(h * XHD, (h + 1) * XHD)
            p = _softmax_rows(_dot(q_ref[:, cols], k_ref[:, cols], NT) * ATT_SCALE)
            o_ref[:, cols] = _dot(p.astype(BF16), v_ref[:, cols], NN).astype(BF16)

    kv = pl.BlockSpec((m, D), lambda i: (0, 0))
    return pl.pallas_call(
        body, name="attn_fwd", grid=(s // tq,),
        in_specs=[pl.BlockSpec((tq, D), lambda i: (i, 0)), kv, kv],
        out_specs=pl.BlockSpec((tq, D), lambda i: (i, 0)),
        out_shape=jax.ShapeDtypeStruct((s, D), BF16), compiler_params=_cparams(),
    )(q, k, v)


def _attn_bwd(q, k, v, do, tq):
    s, m = q.shape[0], k.shape[0]
    tq = min(tq, s)

    def body(q_ref, k_ref, v_ref, do_ref, dq_ref, dk_ref, dv_ref):
        @pl.when(pl.program_id(0) == 0)
        def _():
            dk_ref[...] = jnp.zeros_like(dk_ref)
            dv_ref[...] = jnp.zeros_like(dv_ref)

        for h in range(XH):
            cols = slice(h * XHD, (h + 1) * XHD)
            qh, kh, vh, doh = q_ref[:, cols], k_ref[:, cols], v_ref[:, cols], do_ref[:, cols]
            p = _softmax_rows(_dot(qh, kh, NT) * ATT_SCALE)
            dp = _dot(doh, vh, NT)
            dv_ref[:, cols] += _dot(p.astype(BF16), doh, TN)
            ds = (p * (dp - jnp.sum(p * dp, axis=-1, keepdims=True)) * ATT_SCALE).astype(BF16)
            dq_ref[:, cols] = _dot(ds, kh, NN).astype(BF16)
            dk_ref[:, cols] += _dot(ds, qh, TN)

    kv = pl.BlockSpec((m, D), lambda i: (0, 0))
    qs = pl.BlockSpec((tq, D), lambda i: (i, 0))
    return pl.pallas_call(
        body, name="attn_bwd", grid=(s // tq,),
        in_specs=[qs, kv, kv, qs], out_specs=[qs, kv, kv],
        out_shape=[jax.ShapeDtypeStruct((s, D), BF16), jax.ShapeDtypeStruct((m, D), F32),
                   jax.ShapeDtypeStruct((m, D), F32)],
        compiler_params=_cparams(),
    )(q, k, v, do)


def _loss_head(y, target, tm):
    s = y.shape[0]
    tm = min(tm, s)

    def body(y_ref, t_ref, dy_ref, part_ref):
        err = y_ref[...] - t_ref[...]
        dy_ref[...] = err * (1.0 / D)
        part_ref[...] = jnp.full((1, 8, LANES), 0.5 * jnp.sum(err * err) * (1.0 / D), F32)

    blk = pl.BlockSpec((tm, D), lambda i: (i, 0))
    return pl.pallas_call(
        body, name="loss_head", grid=(s // tm,), in_specs=[blk, blk],
        out_specs=[blk, pl.BlockSpec((1, 8, LANES), lambda i: (i, 0, 0))],
        out_shape=[jax.ShapeDtypeStruct((s, D), F32), jax.ShapeDtypeStruct((s // tm, 8, LANES), F32)],
        compiler_params=_cparams(),
    )(y, target)


def _layer_fwd(x0, mem, w, p, fetch):
    z, hn0 = _rowop_mm("mix_in", "rms", (x0,), p["norm_mix_pre"], w["w_in"], NT, F32)
    y = _gmlp_fwd(z, p["gmlp_v_gain"], p["w_spatial"], p["b_spatial_t"], 512)
    y = _pool_fwd(z, p["w_pool"], p["s_pool"], y)
    y = _conv_fwd(z, p["w_dw"], p["b_dw"], p["conv_ln_g"], p["conv_ln_b"], y)
    w.update(fetch("out", (y,)))
    x1, h0 = _mm_rowop("mix_out", "rms_res", [(y, w["w_out"], NN)], (x0,), p["norm_mix_post"])
    w.update(fetch("att", (x1,)))
    q, hn1 = _rowop_mm("att_q", "rms", (x1,), p["norm_xattn_pre"], w["w_q"], NN, BF16)
    k, mn = _rowop_mm("att_k", "rms", (mem,), p["norm_mem"], w["w_k"], NN, BF16)
    v, _ = _rowop_mm("att_v", "rms", (mem,), p["norm_mem"], w["w_v"], NN, BF16)
    o = _attn_fwd(q, k, v, 256)
    x2, h1 = _mm_rowop("att_o", "rms_res", [(o, w["w_o"], NN)], (x1,), p["norm_xattn_post"])
    w.update(fetch("up", (x2,)))
    u, hn2 = _rowop_mm("ffn_up", "rms", (x2,), p["norm_ffn_pre"], w["w_up"], NT, F32)
    w.update(fetch("down", (u,)))
    x3, h2 = _mm_rowop("ffn_down", "rms_res", [(u, w["w_down"], NN)], (x2,), p["norm_ffn_post"], relu2=True)
    saved = dict(x0=x0, z=z, hn0=hn0, y=y, h0=h0, x1=x1, q=q, hn1=hn1, k=k, v=v, mn=mn, o=o, h1=h1, x2=x2, u=u,
                 hn2=hn2, h2=h2)
    return x3, saved


def _layer_bwd(dx3, mem, w, p, sv, red):
    gs = {}
    du, dh2, dg = _rowop_mm("ffn_down_bwd", "rms_bwd", (sv["h2"], dx3), p["norm_ffn_post"], w["w_down"], NT, BF16,
                            u=sv["u"], after=red.after())
    gs["norm_ffn_post"] = jnp.sum(dg, axis=0)
    g_down = _mm_tn("ffn_down_dw", sv["u"], dh2, relu2=True)
    red.advance((g_down,))
    dx2, dg = _mm_rowop("ffn_up_bwd", "rms_bwd_res", [(du, w["w_up"], NN)], (sv["x2"], dx3), p["norm_ffn_pre"],
                        after=red.after())
    gs["norm_ffn_pre"] = jnp.sum(dg, axis=0)
    g_up = _mm_tn("ffn_up_dw", du, sv["hn2"])
    red.add("ffn", ("w_down", "w_up"), [g_down, g_up])
    do, dh1, dg = _rowop_mm("att_o_bwd", "rms_bwd", (sv["h1"], dx2), p["norm_xattn_post"], w["w_o"], NT, BF16,
                            after=red.after())
    gs["norm_xattn_post"] = jnp.sum(dg, axis=0)
    g_o = _mm_tn("att_o_dw", sv["o"], dh1)
    red.advance((g_o,))
    dq, dk, dv = _attn_bwd(sv["q"], sv["k"], sv["v"], do, 256)
    dk, dv = dk.astype(BF16), dv.astype(BF16)
    dx1, dg = _mm_rowop("att_q_bwd", "rms_bwd_res", [(dq, w["w_q"], NT)], (sv["x1"], dx2), p["norm_xattn_pre"],
                        after=red.after())
    gs["norm_xattn_pre"] = jnp.sum(dg, axis=0)
    g_q = _mm_tn("att_q_dw", sv["hn1"], dq)
    g_k = _mm_tn("att_k_dw", sv["mn"], dk)
    g_v = _mm_tn("att_v_dw", sv["mn"], dv)
    (dg,) = _mm_rowop("att_kv_bwd", "rms_bwd_gain", [(dk, w["w_k"], NT), (dv, w["w_v"], NT)], (mem,), p["norm_mem"])
    gs["norm_mem"] = jnp.sum(dg, axis=0)
    red.add("att", ("w_o", "w_q", "w_k", "w_v"), [g_o, g_q, g_k, g_v])
    dy, dh0, dg = _rowop_mm("mix_out_bwd", "rms_bwd", (sv["h0"], dx1), p["norm_mix_post"], w["w_out"], NT, F32,
                            after=red.after())
    gs["norm_mix_post"] = jnp.sum(dg, axis=0)
    g_out = _mm_tn("mix_out_dw", sv["y"], dh0)
    red.advance((g_out,))
    red.add("out", ("w_out",), [g_out])
    z = sv["z"]
    dz, dgv, dws, dbs = _gmlp_bwd(z, dy, p["gmlp_v_gain"], p["w_spatial"], p["b_spatial_t"], 512)
    gs["gmlp_v_gain"] = jnp.sum(dgv, axis=0)
    gs["w_spatial"] = jnp.sum(dws, axis=0)
    gs["b_spatial"] = jnp.sum(dbs[..., 0], axis=0)
    dz, gs["w_pool"], gs["s_pool"] = _pool_bwd(z, dy, p["w_pool"], p["s_pool"], dz)
    dz, dwdw, gs["b_dw"], gs["conv_ln_g"], gs["conv_ln_b"] = _conv_bwd(
        z, dy, p["w_dw"], p["b_dw"], p["conv_ln_g"], p["conv_ln_b"], dz)
    red.advance((dz,))
    g_in = _mm_tn("mix_in_dw", dz, sv["hn0"], after=red.after())
    red.add("in", ("w_in",), [g_in])
    red.advance((g_in,))
    dx0, dg = _mm_rowop("mix_in_bwd", "rms_bwd_res", [(dz, w["w_in"], NN)], (sv["x0"], dx1), p["norm_mix_pre"],
                        after=red.after())
    gs["norm_mix_pre"] = jnp.sum(dg, axis=0)
    red.small(_small_grad_arrays(gs, dwdw))
    return dx0


NORM_NAMES = ("norm_mix_pre", "norm_mix_post", "norm_xattn_pre", "norm_mem", "norm_xattn_post", "norm_ffn_pre",
              "norm_ffn_post")
VEC_NAMES = ("s_pool", "b_dw", "conv_ln_g", "conv_ln_b")
SMALL_ARRAYS = ("norms", "gain_bias", "w_spatial", "w_pool", "vecs", "w_dw")


def _small_grad_arrays(gs, dwdw):
    return {"norms": jnp.concatenate([gs[n] for n in NORM_NAMES], axis=0),
            "gain_bias": jnp.concatenate([gs["gmlp_v_gain"], gs["b_spatial"]], axis=0),
            "w_spatial": gs["w_spatial"], "w_pool": gs["w_pool"],
            "vecs": jnp.concatenate([gs[n] for n in VEC_NAMES], axis=0), "w_dw": dwdw}


def _split_small_grads(arrays):
    out = {n: arrays["norms"][k] for k, n in enumerate(NORM_NAMES)}
    out.update({n: arrays["vecs"][k] for k, n in enumerate(VEC_NAMES)})
    out.update(gmlp_v_gain=arrays["gain_bias"][:NH], b_spatial=arrays["gain_bias"][NH:], w_spatial=arrays["w_spatial"],
               w_pool=arrays["w_pool"], w_dw=arrays["w_dw"][:CONV_K])
    return out


def _layer_params(small, l):
    p = {n: small[n][l].reshape(1, -1) for n in ("norm_mix_pre", "norm_mix_post", "s_pool", "b_dw", "conv_ln_g",
                                                   "conv_ln_b", "norm_xattn_pre", "norm_mem", "norm_xattn_post",
                                                   "norm_ffn_pre", "norm_ffn_post")}
    p["gmlp_v_gain"] = small["gmlp_v_gain"][l]
    p["w_spatial"] = small["w_spatial"][l]
    p["b_spatial_t"] = small["b_spatial"][l].T
    p["w_pool"] = small["w_pool"][l]
    p["w_dw"] = jnp.pad(small["w_dw"][l], ((0, 1), (0, 0)))
    return p


def _local_step(x, mem, target, fetch, small, red):
    small = dict(small)
    saved, weights, params = [], [], []
    h = x
    marker = ()
    for l in range(DEPTH):
        w = fetch(l, "in", marker)
        if "taps" in w:
            small["w_dw"] = w.pop("taps")
        p = _layer_params(small, l)
        h, sv = _layer_fwd(h, mem, w, p, functools.partial(fetch, l))
        marker = (h,)
        saved.append(sv)
        weights.append(w)
        params.append(p)
    dh, loss = _loss_head(h, target, 512)
    red.extra = (loss,)
    for l in reversed(range(DEPTH)):
        red.layer = l
        dh = _layer_bwd(dh, mem, weights[l], params[l], saved[l], red)
    return loss, dh


HBM = pl.BlockSpec(memory_space=pltpu.HBM)


def _position():
    return lax.axis_index("x"), lax.axis_index("y"), lax.axis_index("c")


SEM = pl.BlockSpec(memory_space=pltpu.SEMAPHORE)
EFFECT = pltpu.SideEffectType.DATAFLOW_SIDE_EFFECTING
TOKEN = jax.ShapeDtypeStruct((8, LANES), F32)
TOKEN_SPEC = pl.BlockSpec(memory_space=pltpu.VMEM)


def _landing(shape, dtype):
    return pltpu.with_memory_space_constraint(lax.empty(shape, dtype), pltpu.HBM)


def _hbm_shapes(arrays):
    return [pltpu.HBM(a.shape, a.dtype) for a in arrays]


def _block(ref, r, dev):
    return ref.at[pl.ds((4 * dev[0] + 2 * dev[1] + dev[2]) * r, r), :]


def _split_call(name, body, thru, sems_in, after, sems_out, token):
    n = len(thru)
    out_shape = [pltpu.SemaphoreType.DMA(s) for s in sems_out] + _hbm_shapes(thru) + ([TOKEN] if token else [])
    out_specs = [SEM] * len(sems_out) + [HBM] * n + ([TOKEN_SPEC] if token else [])
    return pl.pallas_call(
        body, name=name, in_specs=[HBM] * n + [SEM] * len(sems_in) + [ANY] * len(after),
        out_specs=out_specs, out_shape=out_shape,
        input_output_aliases={i: len(sems_out) + i for i in range(n)},
        compiler_params=pltpu.CompilerParams(has_side_effects=EFFECT),
    )(*thru, *sems_in, *after)


def _place_own(name, srcs, dev, out_dtype, tr):
    n = len(srcs)
    r, cols = srcs[0][0].shape[-2:]
    tr = r if r < 16 else _row_tile(r, tr)
    nb = r // tr

    def body(dev_ref, *refs):
        for a in range(n):
            refs[n + a][...] = refs[a][...].astype(out_dtype)

    in_specs = [pl.BlockSpec((tr, cols), lambda i, d: (i, 0)) if l is None
                else pl.BlockSpec((None, tr, cols), lambda i, d, l=l: (l, i, 0)) for _, l in srcs]
    return pl.pallas_call(
        body, name=name,
        grid_spec=pltpu.PrefetchScalarGridSpec(
            num_scalar_prefetch=1, grid=(nb,), in_specs=in_specs,
            out_specs=[pl.BlockSpec((tr, cols), lambda i, d: (d[0] * nb + i, 0))] * n),
        out_shape=[jax.ShapeDtypeStruct((N_DEV * r, cols), out_dtype)] * n, compiler_params=_cparams(),
    )(dev, *[a for a, _ in srcs])


def _gather_peers(x, y, c):
    return [(1 - x, y, c), (x, 1 - y, c), (1 - x, 1 - y, c), (x, y, 1 - c)]


def _block_rows(land):
    return land.shape[0] // N_DEV


def _near_peers(x, y, c):
    return [(1 - x, y, c), (x, 1 - y, c), (x, y, 1 - c)]


def _relay_route(x, y, c):
    origin = (x + c * (1 - 2 * x), y + (1 - c) * (1 - 2 * y), c)
    target = (x + (1 - c) * (1 - 2 * x), y + c * (1 - 2 * y), c)
    return origin, target


def _same_block_copy(blk, send_sem, recv_sem, to):
    return pltpu.make_async_remote_copy(src_ref=blk, dst_ref=blk, send_sem=send_sem, recv_sem=recv_sem, device_id=to,
                                        device_id_type=MESH)


def _gather_start(name, lands, after):
    n = len(lands)

    def body(*refs):
        lz = refs[:n]
        send_sems, recv_sems = refs[n + len(after)], refs[n + len(after) + 1]
        token = refs[-1]
        x, y, c = _position()
        for a in range(n):
            own = _block(lz[a], _block_rows(lands[a]), (x, y, c))
            for k, to in enumerate(_near_peers(x, y, c)):
                _same_block_copy(own, send_sems.at[k], recv_sems.at[k], to).start()
        token[...] = jnp.zeros_like(token)

    out = _split_call(name, body, list(lands), [], after, [(3,), (3,)], True)
    return out[0], out[1], out[2:2 + n], out[-1]


def _gather_forward(name, lands, recv_sems, after):
    n = len(lands)

    def body(*refs):
        lz = refs[:n]
        recv0 = refs[n]
        fsend, frecv, rsend, rrecv = refs[n + 1 + len(after):n + 5 + len(after)]
        token = refs[-1]
        x, y, c = _position()
        near = _near_peers(x, y, c)[:2]
        origin, target = _relay_route(x, y, c)
        for a in range(n):
            for j, chip in enumerate(near):
                blk = _block(lz[a], _block_rows(lands[a]), chip)
                _same_block_copy(blk, fsend.at[j], recv0.at[j], (x, y, c)).wait_recv()
        for a in range(n):
            r = _block_rows(lands[a])
            _same_block_copy(_block(lz[a], r, origin), rsend.at[0], rrecv.at[0], target).start()
            for j, chip in enumerate(near):
                _same_block_copy(_block(lz[a], r, chip), fsend.at[j], frecv.at[j], (x, y, 1 - c)).start()
        token[...] = jnp.zeros_like(token)

    out = _split_call(name, body, list(lands), [recv_sems], after, [(2,), (2,), (1,), (1,)], True)
    return out[0], out[1], out[2], out[3], out[4:4 + n], out[-1]


def _gather_forward_far(name, lands, rrecv, after):
    n = len(lands)

    def body(*refs):
        lz = refs[:n]
        rrecv_ref = refs[n]
        f2send, f2recv = refs[n + 1 + len(after)], refs[n + 2 + len(after)]
        token = refs[-1]
        x, y, c = _position()
        far = (1 - x, 1 - y, c)
        for a in range(n):
            blk = _block(lz[a], _block_rows(lands[a]), far)
            _same_block_copy(blk, f2send.at[0], rrecv_ref.at[0], (x, y, c)).wait_recv()
        for a in range(n):
            blk = _block(lz[a], _block_rows(lands[a]), far)
            _same_block_copy(blk, f2send.at[0], f2recv.at[0], (x, y, 1 - c)).start()
        token[...] = jnp.zeros_like(token)

    out = _split_call(name, body, list(lands), [rrecv], after, [(1,), (1,)], True)
    return out[0], out[1], out[2:2 + n], out[-1]


def _gather_finish(name, lands, send_sems, recv_sems, fsend, frecv, rsend, f2send, f2recv, after):
    n = len(lands)

    def body(*refs):
        lz = refs[:n]
        send0, recv0, fsend_ref, frecv_ref, rsend_ref, f2send_ref, f2recv_ref = refs[n:n + 7]
        x, y, c = _position()
        me = (x, y, c)
        near = _near_peers(x, y, c)[:2]
        origin, _ = _relay_route(x, y, c)
        for a in range(n):
            r = _block_rows(lands[a])
            sib = _block(lz[a], r, (x, y, 1 - c))
            _same_block_copy(sib, send0.at[2], recv0.at[2], me).wait_recv()
            for j, chip in enumerate(near):
                blk = _block(lz[a], r, (chip[0], chip[1], 1 - c))
                _same_block_copy(blk, fsend_ref.at[j], frecv_ref.at[j], me).wait_recv()
            far = _block(lz[a], r, (1 - x, 1 - y, 1 - c))
            _same_block_copy(far, f2send_ref.at[0], f2recv_ref.at[0], me).wait_recv()
            own = _block(lz[a], r, me)
            for k in range(3):
                _same_block_copy(own, send0.at[k], recv0.at[k], me).wait_send()
            for j, chip in enumerate(near):
                _same_block_copy(_block(lz[a], r, chip), fsend_ref.at[j], frecv_ref.at[j], me).wait_send()
            _same_block_copy(_block(lz[a], r, origin), rsend_ref.at[0], recv0.at[0], me).wait_send()
            _same_block_copy(_block(lz[a], r, (1 - x, 1 - y, c)), f2send_ref.at[0], f2recv_ref.at[0], me).wait_send()

    return _split_call(name, body, list(lands), [send_sems, recv_sems, fsend, frecv, rsend, f2send, f2recv], after, [],
                       False)


def _sibling_start(name, grads, after):
    n = len(grads)
    lands = [_landing((4, g.shape[0] // N_DEV, D), g.dtype) for g in grads]

    def body(*refs):
        ins, lz = refs[:n], refs[n:2 * n]
        send_sem, recv_sem = refs[2 * n + len(after)], refs[2 * n + len(after) + 1]
        token = refs[-1]
        x, y, c = _position()
        for a in range(n):
            r = grads[a].shape[0] // N_DEV
            for q in range(4):
                pltpu.make_async_remote_copy(
                    src_ref=ins[a].at[pl.ds((2 * q + 1 - c) * r, r), :], dst_ref=lz[a].at[q], send_sem=send_sem.at[0],
                    recv_sem=recv_sem.at[0], device_id=(x, y, 1 - c), device_id_type=MESH).start()
        token[...] = jnp.zeros_like(token)

    out = _split_call(name, body, list(grads) + lands, [], after, [(1,), (1,)], True)
    return out[0], out[1], out[2:2 + n], out[2 + n:2 + 2 * n], out[-1]


def _sibling_finish(name, grads, lands, send_sem, recv_sem, after):
    n = len(grads)

    def body(*refs):
        ins, lz = refs[:n], refs[n:2 * n]
        send_ref, recv_ref = refs[2 * n], refs[2 * n + 1]
        x, y, c = _position()
        for a in range(n):
            r = grads[a].shape[0] // N_DEV
            for q in range(4):
                cp = pltpu.make_async_remote_copy(
                    src_ref=ins[a].at[pl.ds((2 * q + 1 - c) * r, r), :], dst_ref=lz[a].at[q], send_sem=send_ref.at[0],
                    recv_sem=recv_ref.at[0], device_id=(x, y, c), device_id_type=MESH)
                cp.wait_send()
                cp.wait_recv()

    out = _split_call(name, body, list(grads) + list(lands), [send_sem, recv_sem], after, [], False)
    return out[:n], out[n:2 * n]


def _chip_start(name, parts, after):
    n = len(parts)
    lands = [_landing((3,) + p.shape[1:], p.dtype) for p in parts]

    def body(*refs):
        ins, lz = refs[:n], refs[n:2 * n]
        send_sems, recv_sems = refs[2 * n + len(after)], refs[2 * n + len(after) + 1]
        token = refs[-1]
        x, y, c = _position()
        for a in range(n):
            for j, chip in enumerate(_gather_peers(x, y, c)[:3]):
                pltpu.make_async_remote_copy(
                    src_ref=ins[a].at[2 * chip[0] + chip[1]], dst_ref=lz[a].at[j], send_sem=send_sems.at[j],
                    recv_sem=recv_sems.at[j], device_id=chip, device_id_type=MESH).start()
        token[...] = jnp.zeros_like(token)

    out = _split_call(name, body, list(parts) + lands, [], after, [(3,), (3,)], True)
    return out[0], out[1], out[2:2 + n], out[2 + n:2 + 2 * n], out[-1]


def _chip_finish(name, parts, lands, send_sems, recv_sems, after):
    n = len(parts)

    def body(*refs):
        ins, lz = refs[:n], refs[n:2 * n]
        send_ref, recv_ref = refs[2 * n], refs[2 * n + 1]
        me = _position()
        for a in range(n):
            for j in range(3):
                cp = pltpu.make_async_remote_copy(
                    src_ref=ins[a].at[j], dst_ref=lz[a].at[j], send_sem=send_ref.at[j], recv_sem=recv_ref.at[j],
                    device_id=me, device_id_type=MESH)
                cp.wait_send()
                cp.wait_recv()

    out = _split_call(name, body, list(parts) + list(lands), [send_sems, recv_sems], after, [], False)
    return out[:n], out[n:2 * n]


def _other_devices(x, y, c):
    return [(x + (k >> 2 & 1) * (1 - 2 * x), y + (k >> 1 & 1) * (1 - 2 * y), c + (k & 1) * (1 - 2 * c))
            for k in range(1, N_DEV)]


def _broadcast_start(name, arrays, after):
    n = len(arrays)
    lands = [_landing((N_DEV,) + a.shape, a.dtype) for a in arrays]

    def body(*refs):
        ins, lz = refs[:n], refs[n:2 * n]
        send_sems, recv_sems = refs[2 * n + len(after)], refs[2 * n + len(after) + 1]
        token = refs[-1]
        x, y, c = _position()
        for a in range(n):
            for k, peer in enumerate(_other_devices(x, y, c)):
                pltpu.make_async_remote_copy(
                    src_ref=ins[a], dst_ref=lz[a].at[4 * x + 2 * y + c], send_sem=send_sems.at[k],
                    recv_sem=recv_sems.at[k], device_id=peer, device_id_type=MESH).start()
        token[...] = jnp.zeros_like(token)

    out = _split_call(name, body, list(arrays) + lands, [], after, [(N_DEV - 1,), (N_DEV - 1,)], True)
    return out[0], out[1], out[2:2 + n], out[2 + n:2 + 2 * n], out[-1]


def _broadcast_finish(name, arrays, lands, send_sems, recv_sems, after):
    n = len(arrays)

    def body(*refs):
        ins, lz = refs[:n], refs[n:2 * n]
        send_ref, recv_ref = refs[2 * n], refs[2 * n + 1]
        x, y, c = _position()
        for a in range(n):
            for k, peer in enumerate(_other_devices(x, y, c)):
                cp = pltpu.make_async_remote_copy(
                    src_ref=ins[a], dst_ref=lz[a].at[4 * peer[0] + 2 * peer[1] + peer[2]], send_sem=send_ref.at[k],
                    recv_sem=recv_ref.at[k], device_id=(x, y, c), device_id_type=MESH)
                cp.wait_send()
                cp.wait_recv()

    out = _split_call(name, body, list(arrays) + list(lands), [send_sems, recv_sems], after, [], False)
    return out[:n], out[n:2 * n]


def _row_tile(r, target):
    return max(t for t in range(16, min(r, target) + 1, 16) if r % t == 0)


def _chip_partial(name, grad, got, c, tr):
    r = grad.shape[0] // N_DEV
    tr = _row_tile(r, tr)
    g4 = grad.reshape(4, 2, r, D)

    def body(c_ref, g_ref, s_ref, o_ref):
        o_ref[...] = (g_ref[...].astype(F32) + s_ref[...].astype(F32)).astype(BF16)

    return pl.pallas_call(
        body, name=name,
        grid_spec=pltpu.PrefetchScalarGridSpec(
            num_scalar_prefetch=1, grid=(4, r // tr),
            in_specs=[pl.BlockSpec((None, None, tr, D), lambda q, i, c_ref: (q, c_ref[0], i, 0)),
                      pl.BlockSpec((None, tr, D), lambda q, i, c_ref: (q, i, 0))],
            out_specs=pl.BlockSpec((None, tr, D), lambda q, i, c_ref: (q, i, 0))),
        out_shape=jax.ShapeDtypeStruct((4, r, D), BF16), compiler_params=_cparams(),
    )(c, g4, got)


class _WeightGather:
    def __init__(self, groups):
        self.groups = list(groups)
        self.index = {key: i for i, (key, _, _) in enumerate(groups)}
        self.state = [None] * len(groups)
        self.token = ()
        for i in range(min(2, len(groups))):
            self._start(i)

    def _tag(self, i):
        return "%s_%d" % self.groups[i][0][::-1]

    def _start(self, i):
        send, recv, lz, tok = _gather_start("gather_start_" + self._tag(i), self.groups[i][2], self.token)
        self.state[i] = dict(send=send, recv=recv, lands=lz)
        self.token = (tok,)

    def _forward(self, i, marker):
        st = self.state[i]
        st["fsend"], st["frecv"], st["rsend"], st["rrecv"], st["lands"], tok = _gather_forward(
            "gather_forward_" + self._tag(i), st["lands"], st["recv"], tuple(marker) + self.token)
        self.token = (tok,)

    def fetch(self, layer, group, marker):
        k = self.index[(layer, group)]
        if k == 0:
            self._forward(0, marker)
        if k + 1 < len(self.groups):
            self._forward(k + 1, marker)
        if k + 2 < len(self.groups):
            self._start(k + 2)
        st = self.state[k]
        f2send, f2recv, lz, tok = _gather_forward_far("gather_far_" + self._tag(k), st["lands"], st["rrecv"],
                                                      tuple(marker) + self.token)
        self.token = (tok,)
        lz = _gather_finish("gather_finish_" + self._tag(k), lz, st["send"], st["recv"], st["fsend"], st["frecv"],
                            st["rsend"], f2send, f2recv, self.token)
        self.state[k] = None
        return dict(zip(self.groups[k][1], lz))


class _GradReduce:
    def __init__(self, core, chip):
        self.core, self.chip = core, chip
        self.layer = None
        self.token = ()
        self.at_sibling, self.at_chips = [], []
        self.extra, self.smalls = (), {}

    def after(self):
        return self.token

    def add(self, group, names, grads):
        tag = "%s_%d" % (group, self.layer)
        send, recv, grads, lands, tok = _sibling_start("grad_sibling_start_" + tag, grads, self.token)
        self.at_sibling.append((tag, [(self.layer, n) for n in names], send, recv, grads, lands))
        self.token = (tok,)

    def advance(self, marker):
        for tag, keys, send, recv, grads, lands in self.at_sibling:
            grads, lands = _sibling_finish("grad_sibling_finish_" + tag, grads, lands, send, recv, marker)
            parts = [_chip_partial("chip_partial_%d_%s" % key, g, got, self.core, 256)
                     for key, g, got in zip(keys, grads, lands)]
            send, recv, parts, lands, tok = _chip_start("grad_chip_start_" + tag, parts, ())
            self.at_chips.append([tag, keys, send, recv, parts, lands])
            self.token = (tok,)
        self.at_sibling = []

    def small(self, arrays):
        extra = list(self.extra) if self.layer == 0 else []
        send, recv, own, slots, tok = _broadcast_start(
            "small_grads_start_%d" % self.layer, [arrays[k] for k in SMALL_ARRAYS] + extra, self.token)
        self.smalls[self.layer] = (send, recv, own, slots)
        self.token = (tok,)

    def small_finish(self, layer, marker):
        send, recv, own, slots = self.smalls[layer]
        return _broadcast_finish("small_grads_finish_%d" % layer, own, slots, send, recv, marker)

    def collect(self, key, marker):
        for entry in self.at_chips:
            tag, keys, send, recv, parts, lands = entry
            if key in keys:
                if send is not None:
                    parts, lands = _chip_finish("grad_chip_finish_" + tag, parts, lands, send, recv, marker)
                    entry[2:] = [None, None, parts, lands]
                i = keys.index(key)
                return parts[i], lands[i]
        raise KeyError(key)


def _adamw_math(w, g, m, v):
    m = ADAM_B1 * m + (1.0 - ADAM_B1) * g
    v = ADAM_B2 * v + (1.0 - ADAM_B2) * jnp.square(g)
    m_hat = m / (1.0 - ADAM_B1 ** ADAM_STEP)
    v_hat = v / (1.0 - ADAM_B2 ** ADAM_STEP)
    delta = -ADAM_LR * (m_hat / (jnp.sqrt(v_hat) + ADAM_EPS) + ADAM_WD * w)
    return delta, m, v


def _adamw_small(wts, mom_m, mom_v, own, gathered, loss_own, loss_gathered, dev):
    names = SMALL
    nw = len(names)
    na = len(SMALL_ARRAYS)

    def body(dev_ref, *refs):
        w_refs, m_refs, v_refs = (dict(zip(names, refs[i * nw:(i + 1) * nw])) for i in range(3))
        own_refs = refs[3 * nw:3 * nw + DEPTH * na]
        g_refs = refs[3 * nw + DEPTH * na:3 * nw + 2 * DEPTH * na]
        loss_own_ref, loss_got_ref = refs[3 * nw + 2 * DEPTH * na:3 * nw + 2 * DEPTH * na + 2]
        outs = refs[3 * nw + 2 * DEPTH * na + 2:]
        g_out, d_out, m_out, v_out = (dict(zip(names, outs[i * nw:(i + 1) * nw])) for i in range(4))
        me = dev_ref[0]

        loss = None
        for d in range(N_DEV):
            for b in range(loss_own.shape[0]):
                term = jnp.where(me == d, loss_own_ref[b], loss_got_ref[d, b])
                loss = term if loss is None else loss + term
        outs[4 * nw][...] = loss

        def update(name, at, g):
            g_out[name][at] = g
            d_out[name][at], m_out[name][at], v_out[name][at] = _adamw_math(
                w_refs[name][at], g, m_refs[name][at], v_refs[name][at])

        for l in range(DEPTH):
            mine = dict(zip(SMALL_ARRAYS, own_refs[l * na:(l + 1) * na]))
            got = dict(zip(SMALL_ARRAYS, g_refs[l * na:(l + 1) * na]))

            def total(key, at):
                acc = None
                for d in range(N_DEV):
                    term = jnp.where(me == d, mine[key][at] if at else mine[key][...], got[key][(d,) + at])
                    acc = term if acc is None else acc + term
                return acc

            row = (slice(l, l + 1),)
            for k, name in enumerate(NORM_NAMES):
                update(name, row, total("norms", (slice(k, k + 1),)))
            for k, name in enumerate(VEC_NAMES):
                update(name, row, total("vecs", (slice(k, k + 1),)))
            update("gmlp_v_gain", (l,), total("gain_bias", (slice(0, NH),)))
            update("b_spatial", (l,), total("gain_bias", (slice(NH, 2 * NH),)))
            update("w_spatial", (l,), total("w_spatial", ()))
            update("w_pool", (l,), total("w_pool", ()))
            update("w_dw", (l,), total("w_dw", (slice(0, CONV_K),)))

    args = [src[n] for src in (wts, mom_m, mom_v) for n in names]
    args += [src[l][k] for src in (own, gathered) for l in range(DEPTH) for k in SMALL_ARRAYS]
    args += [loss_own, loss_gathered]
    outs = pl.pallas_call(
        body, name="adamw_small",
        in_specs=[pl.BlockSpec(memory_space=pltpu.SMEM)] + [pl.BlockSpec(memory_space=pltpu.VMEM)] * len(args),
        out_shape=[jax.ShapeDtypeStruct(wts[n].shape, F32) for _ in range(4) for n in names]
        + [jax.ShapeDtypeStruct((8, LANES), F32)],
        compiler_params=_cparams(),
    )(dev, *args)
    return tuple(dict(zip(names, outs[i * nw:(i + 1) * nw])) for i in range(4)) + (outs[4 * nw],)


def _adamw_layers(name, w, reduced, m, v, chip, tr):
    nl, r, cdim = w.shape
    tr = _row_tile(r, tr)
    nb = r // tr

    def body(q_ref, w_ref, p0_ref, g0_ref, p1_ref, g1_ref, m_ref, v_ref, g_ref, d_ref, nm_ref, nv_ref):
        def total(p_ref, got_ref):
            acc = p_ref[...].astype(F32)
            for j in range(3):
                acc = acc + got_ref[j].astype(F32)
            return acc

        g = jnp.where(pl.program_id(0) == 0, total(p0_ref, g0_ref), total(p1_ref, g1_ref))
        g_ref[...] = g
        d_ref[...], nm_ref[...], nv_ref[...] = _adamw_math(w_ref[...], g, m_ref[...], v_ref[...])

    blk = pl.BlockSpec((None, tr, cdim), lambda l, i, q: (l, i, 0))
    first = lambda l, i: i * (1 - l) + (nb - 1) * l
    second = lambda l, i: i * l
    specs = [blk,
             pl.BlockSpec((None, tr, cdim), lambda l, i, q: (q[0], first(l, i), 0)),
             pl.BlockSpec((3, tr, cdim), lambda l, i, q: (0, first(l, i), 0)),
             pl.BlockSpec((None, tr, cdim), lambda l, i, q: (q[0], second(l, i), 0)),
             pl.BlockSpec((3, tr, cdim), lambda l, i, q: (0, second(l, i), 0)), blk, blk]
    shape = jax.ShapeDtypeStruct((nl, r, cdim), F32)
    return pl.pallas_call(
        body, name=name,
        grid_spec=pltpu.PrefetchScalarGridSpec(num_scalar_prefetch=1, grid=(nl, nb), in_specs=specs, out_specs=[blk] * 4),
        out_shape=[shape] * 4, compiler_params=_cparams(),
    )(chip, w, *reduced[0], *reduced[1], m, v)


def _to_rows(name, a):
    return jnp.swapaxes(a, 1, 2) if name in ("w_in", "w_up") else a


def _pack(arrays, rows):
    flat = jnp.concatenate([a.reshape(-1) for a in arrays])
    return jnp.pad(flat, (0, rows * D - flat.shape[0])).reshape(rows, D)


def _rows_for(shapes, mult=8):
    total = 0
    for shp in shapes:
        size = 1
        for dim in shp:
            size *= dim
        total += size
    return -(-total // (mult * D)) * mult


def kernel(x, mem, norm_mix_pre, norm_mix_post, w_in, w_out, gmlp_v_gain, w_spatial, b_spatial, w_pool, s_pool, w_dw, b_dw, conv_ln_g, conv_ln_b, norm_xattn_pre, norm_mem, norm_xattn_post, w_q, w_k, w_v, w_o, norm_ffn_pre, norm_ffn_post, w_up, w_down, loss_target, m_norm_mix_pre, m_norm_mix_post, m_w_in, m_w_out, m_gmlp_v_gain, m_w_spatial, m_b_spatial, m_w_pool, m_s_pool, m_w_dw, m_b_dw, m_conv_ln_g, m_conv_ln_b, m_norm_xattn_pre, m_norm_mem, m_norm_xattn_post, m_w_q, m_w_k, m_w_v, m_w_o, m_norm_ffn_pre, m_norm_ffn_post, m_w_up, m_w_down, v_norm_mix_pre, v_norm_mix_post, v_w_in, v_w_out, v_gmlp_v_gain, v_w_spatial, v_b_spatial, v_w_pool, v_s_pool, v_w_dw, v_b_dw, v_conv_ln_g, v_conv_ln_b, v_norm_xattn_pre, v_norm_mem, v_norm_xattn_post, v_w_q, v_w_k, v_w_v, v_w_o, v_norm_ffn_pre, v_norm_ffn_post, v_w_up, v_w_down):
    args = dict(locals())
    wts = {n: args[n] for n in WEIGHTS}
    mom_m = {n: args["m_" + n] for n in WEIGHTS}
    mom_v = {n: args["v_" + n] for n in WEIGHTS}
    xi, yi, ci = _position()
    me = 4 * xi + 2 * yi + ci

    dev = jnp.reshape(me, (1,)).astype(jnp.int32)
    lands = {}
    for call, names, tr in (("place_att", ("w_out", "w_q", "w_k", "w_v", "w_o"), 64), ("place_down", ("w_down",), 256),
                            ("place_up", ("w_up",), 256), ("place_in", ("w_in",), 256)):
        srcs = [(_to_rows(n, wts[n]), l) for l in range(DEPTH) for n in names]
        placed = _place_own(call, srcs, dev, BF16, tr)
        lands.update(zip([(l, n) for l in range(DEPTH) for n in names], placed))
    (lands[(0, "taps")],) = _place_own("place_taps", [(_pack([w_dw], _rows_for([w_dw.shape])), None)], dev, F32, 8)
    groups = []
    for l in range(DEPTH):
        for group, names in GATHER_GROUPS:
            if (l, group) == (0, "in"):
                names = names + ("taps",)
            groups.append(((l, group), names, [lands[(l, n)] for n in names]))
    gather = _WeightGather(groups)

    def fetch(layer, group, marker):
        w = gather.fetch(layer, group, marker)
        if "taps" in w:
            blocks = w["taps"].reshape(N_DEV, -1)[:, :w_dw.size].reshape((N_DEV,) + w_dw.shape)
            w["taps"] = jnp.moveaxis(blocks, 0, 2).reshape(DEPTH, CONV_K, CW)
        return w

    reduce = _GradReduce(jnp.reshape(ci, (1,)).astype(jnp.int32), jnp.reshape(2 * xi + yi, (1,)).astype(jnp.int32))
    small = {n: wts[n] for n in SMALL if n != "w_dw"}
    _, dx = _local_step(x[0], mem[0], loss_target[0], fetch, small, reduce)
    reduce.advance((dx,))

    grad_w, delta, new_m, new_v = {}, {}, {}, {}
    marker = (dx,) + tuple(reduce.after())
    for n in UPDATE_ORDER:
        reduced = [reduce.collect((l, n), marker) for l in range(DEPTH)]
        outs = _adamw_layers("adamw_" + n, _to_rows(n, wts[n]), reduced, _to_rows(n, mom_m[n]), _to_rows(n, mom_v[n]),
                             reduce.chip, 256)
        grad_w[n], delta[n], new_m[n], new_v[n] = (_to_rows(n, o) for o in outs)
        marker = (outs[1],)

    own, slots = [None] * DEPTH, [None] * DEPTH
    for l in reversed(range(DEPTH)):
        mine, theirs = reduce.small_finish(l, marker)
        if l == 0:
            loss_own, loss_slots = mine[-1], theirs[-1]
        own[l], slots[l] = dict(zip(SMALL_ARRAYS, mine)), dict(zip(SMALL_ARRAYS, theirs))
    shard_cols = CW // N_DEV
    for l in range(DEPTH):
        own[l]["w_dw"] = lax.dynamic_slice_in_dim(own[l]["w_dw"], me * shard_cols, shard_cols, axis=1)
        slots[l]["w_dw"] = lax.dynamic_slice_in_dim(slots[l]["w_dw"], me * shard_cols, shard_cols, axis=2)
    *small_out, loss_tile = _adamw_small(wts, mom_m, mom_v, own, slots, loss_own, loss_slots, dev)
    for dst, src in zip((grad_w, delta, new_m, new_v), small_out):
        dst.update(src)

    return (loss_tile[0, 0], dx[None], *[grad_w[n] for n in WEIGHTS], *[delta[n] for n in WEIGHTS],
            *[new_m[n] for n in WEIGHTS], *[new_v[n] for n in WEIGHTS])
```

```python
import functools

import jax
import jax.numpy as jnp
from jax import lax
from jax.experimental import pallas as pl
from jax.experimental.pallas import tpu as pltpu

F32 = jnp.float32
BF16 = jnp.bfloat16

D = 2048
GW = 1024
PW = 512
CW = 512
HD = 128
NH = 8
NG = 4
POOL_WINDOWS = (2, 4, 8, 16)
CONV_K = 31
IN_COLS = 2 * GW + PW + 2 * CW
DFF = 4 * D
XH = 4
XHD = D // XH
ATT_SCALE = XHD ** -0.5
RMS_EPS = 1e-6
LN_EPS = 1e-5
DEPTH = 2
N_DEV = 8

ADAM_LR = 0.001
ADAM_B1 = 0.9
ADAM_B2 = 0.999
ADAM_EPS = 1e-08
ADAM_WD = 0.01
ADAM_STEP = 10

LANES = 128
CONV_HALO = 32
POOL_HALO = 16
ROW_TILE = 128
VMEM_LIMIT = 60 * 1024 * 1024

MESH = pl.DeviceIdType.MESH
NT = (((1,), (1,)), ((), ()))
NN = (((1,), (0,)), ((), ()))
TN = (((0,), (0,)), ((), ()))

BIG = ("w_out", "w_q", "w_k", "w_v", "w_o", "w_up", "w_down", "w_in")
UPDATE_ORDER = ("w_down", "w_up", "w_o", "w_q", "w_k", "w_v", "w_out", "w_in")
GATHER_GROUPS = (("in", ("w_in",)), ("out", ("w_out",)), ("att", ("w_q", "w_k", "w_v", "w_o")), ("up", ("w_up",)),
                 ("down", ("w_down",)))
SMALL = ("norm_mix_pre", "norm_mix_post", "gmlp_v_gain", "w_spatial", "b_spatial", "w_pool", "s_pool",
         "w_dw", "b_dw", "conv_ln_g", "conv_ln_b", "norm_xattn_pre", "norm_mem", "norm_xattn_post",
         "norm_ffn_pre", "norm_ffn_post")
WEIGHTS = ("norm_mix_pre", "norm_mix_post", "w_in", "w_out", "gmlp_v_gain", "w_spatial", "b_spatial", "w_pool",
           "s_pool", "w_dw", "b_dw", "conv_ln_g", "conv_ln_b", "norm_xattn_pre", "norm_mem", "norm_xattn_post",
           "w_q", "w_k", "w_v", "w_o", "norm_ffn_pre", "norm_ffn_post", "w_up", "w_down")


def _cparams():
    return pltpu.CompilerParams(vmem_limit_bytes=VMEM_LIMIT)


def _dot(a, b, dims):
    return lax.dot_general(a, b, dims, preferred_element_type=F32)


def _rms(x, g):
    y = x * lax.rsqrt(jnp.mean(x * x, axis=-1, keepdims=True) + RMS_EPS)
    return y * g


def _gelu(x):
    cdf = 0.5 * (1.0 + jnp.tanh(0.7978845608028654 * (x + 0.044715 * (x * x * x))))
    return x * cdf


def _layer_norm(x, g, b=None):
    mu = jnp.mean(x, axis=-1, keepdims=True)
    xc = x - mu
    var = jnp.mean(xc * xc, axis=-1, keepdims=True)
    y = xc * lax.rsqrt(var + LN_EPS) * g
    return y if b is None else y + b


def _sigmoid(x):
    return 1.0 / (1.0 + jnp.exp(-x))


def _gmlp_rows(zu, zv, gv):
    return _gelu(zu), _layer_norm(_gelu(zv), gv)


def _glu(cv, cg):
    return cv * _sigmoid(cg)


def _ln_silu(h, g, b):
    y = _layer_norm(h, g, b)
    return y * _sigmoid(y)


ANY = pl.BlockSpec(memory_space=pl.ANY)


ROWS_TILE = 256
COLS_TILE = 512
DW_TILE = 512
RESIDENT_K = 2048
STREAM_K_TILE = 1024
STREAM_ROWS = 512


def _k_tiles(kdim):
    if kdim <= RESIDENT_K:
        return ROWS_TILE, kdim
    return STREAM_ROWS, max(t for t in range(LANES, STREAM_K_TILE + 1, LANES) if kdim % t == 0)


def _rowop_mm(name, kind, rows, g, w, dims, out_dtype, u=None, after=()):
    s = rows[0].shape[0]
    n = w.shape[0] if dims == NT else w.shape[1]
    tm, tn = min(ROWS_TILE, s), min(COLS_TILE, n)
    ni = s // tm
    bwd = kind == "rms_bwd"

    def rows_body(*refs):
        refs = list(refs)
        row_refs = [refs.pop(0) for _ in rows]
        g_ref = refs.pop(0)
        del refs[:len(after)]
        if bwd:
            _, vjp = jax.vjp(_rms, row_refs[0][...], g_ref[...])
            a, dg = vjp(row_refs[1][...])
            refs[1][0] = dg
        else:
            a = _rms(row_refs[0][...], g_ref[...])
        refs[0][...] = a.astype(BF16)

    row_spec = pl.BlockSpec((tm, D), lambda i: (i, 0))
    res = pl.pallas_call(
        rows_body, name=name + "_rows", grid=(ni,),
        in_specs=[row_spec] * len(rows) + [pl.BlockSpec((1, D), lambda i: (0, 0))] + [ANY] * len(after),
        out_specs=[row_spec] + ([pl.BlockSpec((1, 1, D), lambda i: (i, 0, 0))] if bwd else []),
        out_shape=[jax.ShapeDtypeStruct((s, D), BF16)] + ([jax.ShapeDtypeStruct((ni, 1, D), F32)] if bwd else []),
        compiler_params=_cparams(),
    )(*rows, g, *after)
    a = res[0]

    def body(a_ref, w_ref, *rest):
        acc = _dot(a_ref[...], w_ref[...], dims)
        if u is not None:
            acc = acc * (2.0 * jnp.maximum(rest[0][...], 0.0))
        rest[-1][...] = acc.astype(out_dtype)

    w_spec = pl.BlockSpec((tn, D), lambda j: (j, 0)) if dims == NT else pl.BlockSpec((D, tn), lambda j: (0, j))
    tile = pl.BlockSpec((s, tn), lambda j: (0, j))
    out = pl.pallas_call(
        body, name=name, grid=(n // tn,),
        in_specs=[pl.BlockSpec((s, D), lambda j: (0, 0)), w_spec] + ([tile] if u is not None else []),
        out_specs=tile, out_shape=jax.ShapeDtypeStruct((s, n), out_dtype), compiler_params=_cparams(),
    )(a, w, *([u] if u is not None else []))
    return (out, *res)


def _mm_rowop(name, kind, pairs, rows, g, relu2=False, after=()):
    s, kdim = pairs[0][0].shape
    tm, tk = _k_tiles(kdim)
    tm = min(tm, s)
    ni, nk = s // tm, kdim // tk
    npair = len(pairs)

    def body(*refs):
        refs = list(refs)
        a_refs = [refs.pop(0) for _ in range(npair)]
        w_refs = [refs.pop(0) for _ in range(npair)]
        row_refs = [refs.pop(0) for _ in rows]
        g_ref = refs.pop(0)
        del refs[:len(after)]
        acc = refs.pop()
        outs = refs
        k = pl.program_id(1)

        @pl.when(k == 0)
        def _():
            acc[...] = jnp.zeros_like(acc)

        for a_ref, w_ref, (_, _, dims) in zip(a_refs, w_refs, pairs):
            a = a_ref[...]
            if relu2:
                a = jnp.square(jnp.maximum(a, 0.0))
            acc[...] += _dot(a.astype(BF16), w_ref[...], dims)

        @pl.when(k == nk - 1)
        def _():
            h = acc[...]
            if kind == "rms_res":
                outs[0][...] = row_refs[0][...] + _rms(h, g_ref[...])
                outs[1][...] = h
            else:
                _, vjp = jax.vjp(_rms, row_refs[0][...], g_ref[...])
                dx, dg = vjp(h)
                if kind == "rms_bwd_res":
                    outs[0][...] = row_refs[1][...] + dx
                    outs[1][0] = dg
                else:
                    outs[0][0] = dg

    row_spec = pl.BlockSpec((tm, D), lambda i, k: (i, 0))
    dg_shape = jax.ShapeDtypeStruct((ni, 1, D), F32)
    dg_spec = pl.BlockSpec((1, 1, D), lambda i, k: (i, 0, 0))
    in_specs = [pl.BlockSpec((tm, tk), lambda i, k: (i, k))] * npair
    for _, _, dims in pairs:
        in_specs.append(pl.BlockSpec((tk, D), lambda i, k: (k, 0)) if dims == NN
                        else pl.BlockSpec((D, tk), lambda i, k: (0, k)))
    in_specs += [row_spec] * len(rows) + [pl.BlockSpec((1, D), lambda i, k: (0, 0))] + [ANY] * len(after)
    if kind == "rms_res":
        out_shape = [jax.ShapeDtypeStruct((s, D), F32)] * 2
        out_specs = [row_spec, row_spec]
    elif kind == "rms_bwd_res":
        out_shape = [jax.ShapeDtypeStruct((s, D), F32), dg_shape]
        out_specs = [row_spec, dg_spec]
    else:
        out_shape = [dg_shape]
        out_specs = [dg_spec]
    return pl.pallas_call(
        body, name=name, grid=(ni, nk), in_specs=in_specs, out_specs=out_specs, out_shape=out_shape,
        scratch_shapes=[pltpu.VMEM((tm, D), F32)], compiler_params=_cparams(),
    )(*[p[0] for p in pairs], *[p[1] for p in pairs], *rows, g, *after)


def _mm_tn(name, a, gmat, relu2=False, after=()):
    s, m = a.shape
    tm, ts = min(DW_TILE, m), s
    ni, ns = m // tm, s // ts

    def body(a_ref, g_ref, *rest):
        o_ref, acc = rest[len(after):]
        k = pl.program_id(1)

        @pl.when(k == 0)
        def _():
            acc[...] = jnp.zeros_like(acc)

        av = a_ref[...]
        if relu2:
            av = jnp.square(jnp.maximum(av, 0.0))
        acc[...] += _dot(av.astype(BF16), g_ref[...], TN)

        @pl.when(k == ns - 1)
        def _():
            o_ref[...] = acc[...].astype(BF16)

    return pl.pallas_call(
        body, name=name, grid=(ni, ns),
        in_specs=[pl.BlockSpec((ts, tm), lambda i, k: (k, i)), pl.BlockSpec((ts, D), lambda i, k: (k, 0))]
        + [ANY] * len(after),
        out_specs=pl.BlockSpec((tm, D), lambda i, k: (i, 0)),
        out_shape=jax.ShapeDtypeStruct((m, D), BF16),
        scratch_shapes=[pltpu.VMEM((tm, D), F32)], compiler_params=_cparams(),
    )(a, gmat, *after)


def _tril():
    r = lax.broadcasted_iota(jnp.int32, (HD, HD), 0)
    c = lax.broadcasted_iota(jnp.int32, (HD, HD), 1)
    return (c <= r).astype(F32)


def _gmlp_fwd(z, gv, ws, bst, tb):
    s = z.shape[0]
    tb = min(tb, s)

    def body(zu_ref, zv_ref, gv_ref, ws_ref, bst_ref, y_ref):
        tril = _tril()
        for h in range(NH):
            cols = slice(h * HD, (h + 1) * HD)
            u, vln = _gmlp_rows(zu_ref[:, cols], zv_ref[:, cols], gv_ref[h:h + 1, :])
            wm = (ws_ref[h] * tril).astype(BF16)
            vb = vln.astype(BF16)
            for c in range(tb // HD):
                rws = slice(c * HD, (c + 1) * HD)
                mixed = _dot(wm, vb[rws], NN) + bst_ref[:, h:h + 1]
                y_ref[rws, cols] = (u[rws] * mixed).astype(BF16)

    return pl.pallas_call(
        body, name="gmlp_fwd", grid=(s // tb,),
        in_specs=[pl.BlockSpec((tb, GW), lambda i: (i, 0)), pl.BlockSpec((tb, GW), lambda i: (i, 1)),
                  pl.BlockSpec((NH, HD), lambda i: (0, 0)), pl.BlockSpec((NH, HD, HD), lambda i: (0, 0, 0)),
                  pl.BlockSpec((HD, NH), lambda i: (0, 0))],
        out_specs=pl.BlockSpec((tb, GW), lambda i: (i, 0)),
        out_shape=jax.ShapeDtypeStruct((s, D), BF16), compiler_params=_cparams(),
    )(z, z, gv, ws, bst)


def _gmlp_bwd(z, dy, gv, ws, bst, tb):
    s = z.shape[0]
    tb = min(tb, s)
    nb = s // tb

    def body(zu_ref, zv_ref, dy_ref, gv_ref, ws_ref, bst_ref, dz_ref, dgv_ref, dws_ref, db_ref):
        tril = _tril()
        for h in range(NH):
            cols = slice(h * HD, (h + 1) * HD)
            (u, vln), vjp = jax.vjp(_gmlp_rows, zu_ref[:, cols], zv_ref[:, cols], gv_ref[h:h + 1, :])
            wmf = ws_ref[h] * tril
            wm = wmf.astype(BF16)
            wmt = wmf.T.astype(BF16)
            vb = vln.astype(BF16)
            dws = jnp.zeros((HD, HD), F32)
            db = jnp.zeros((HD, 1), F32)
            du_parts, dvln_parts = [], []
            for c in range(tb // HD):
                rws = slice(c * HD, (c + 1) * HD)
                mixed = _dot(wm, vb[rws], NN) + bst_ref[:, h:h + 1]
                dyc = dy_ref[rws, cols]
                du_parts.append(dyc * mixed)
                dmixed = dyc * u[rws]
                dmb = dmixed.astype(BF16)
                dws = dws + _dot(dmb, vb[rws], NT)
                db = db + jnp.sum(dmixed, axis=1, keepdims=True)
                dvln_parts.append(_dot(wmt, dmb, NN))
            du = jnp.concatenate(du_parts, axis=0)
            dvln = jnp.concatenate(dvln_parts, axis=0)
            dzu, dzv, dgv = vjp((du, dvln))
            dz_ref[:, cols] = dzu.astype(BF16)
            dz_ref[:, slice(GW + h * HD, GW + (h + 1) * HD)] = dzv.astype(BF16)
            dgv_ref[0, h:h + 1, :] = dgv
            dws_ref[0, h] = dws * tril
            db_ref[0, h] = jnp.broadcast_to(db, (HD, LANES))

    blk = pl.BlockSpec((tb, GW), lambda i: (i, 0))
    return pl.pallas_call(
        body, name="gmlp_bwd", grid=(nb,),
        in_specs=[blk, pl.BlockSpec((tb, GW), lambda i: (i, 1)), blk,
                  pl.BlockSpec((NH, HD), lambda i: (0, 0)), pl.BlockSpec((NH, HD, HD), lambda i: (0, 0, 0)),
                  pl.BlockSpec((HD, NH), lambda i: (0, 0))],
        out_specs=[pl.BlockSpec((tb, 2 * GW), lambda i: (i, 0)), pl.BlockSpec((1, NH, HD), lambda i: (i, 0, 0)),
                   pl.BlockSpec((1, NH, HD, HD), lambda i: (i, 0, 0, 0)),
                   pl.BlockSpec((1, NH, HD, LANES), lambda i: (i, 0, 0, 0))],
        out_shape=[jax.ShapeDtypeStruct((s, IN_COLS), BF16),
                   jax.ShapeDtypeStruct((nb, NH, HD), F32), jax.ShapeDtypeStruct((nb, NH, HD, HD), F32),
                   jax.ShapeDtypeStruct((nb, NH, HD, LANES), F32)],
        compiler_params=_cparams(),
    )(z, z, dy, gv, ws, bst)


def _pool_count(t0, window):
    pos = (t0 + lax.broadcasted_iota(jnp.int32, (ROW_TILE, LANES), 0)).astype(F32)
    return jnp.minimum(pos + 1.0, float(window))


def _window_sum(win, levels, back):
    n = win.shape[0]
    for lv in range(levels):
        step = 1 << lv
        win = win + pltpu.roll(win, n - step if back else step, 0)
    return win


def _pool_pooled(ppad_ref, t0, g):
    win = ppad_ref[pl.ds(t0, ROW_TILE + POOL_HALO), :]
    wsum = _window_sum(win, g + 1, False)[POOL_HALO:]
    return wsum / _pool_count(t0, POOL_WINDOWS[g]) - win[POOL_HALO:]


def _pool_fwd(z, wp, sp, y):
    s = z.shape[0]
    nt = s // ROW_TILE

    def body(p_ref, wp_ref, sp_ref, _, y_ref, ppad):
        for g in range(NG):
            cols = slice(g * LANES, (g + 1) * LANES)
            ppad[pl.ds(0, POOL_HALO), :] = jnp.zeros((POOL_HALO, LANES), F32)
            ppad[pl.ds(POOL_HALO, s), :] = p_ref[:, cols]
            wpb = wp_ref[g].astype(BF16)
            scale = sp_ref[:, cols]

            def tile(t, carry):
                t0 = pl.multiple_of(t * ROW_TILE, ROW_TILE)
                pooled = _pool_pooled(ppad, t0, g)
                y_ref[pl.ds(t0, ROW_TILE), cols] = (_dot(pooled.astype(BF16), wpb, NN) * scale).astype(BF16)
                return carry

            lax.fori_loop(0, nt, tile, 0)

    return pl.pallas_call(
        body, name="pool_fwd", grid=(1,),
        in_specs=[pl.BlockSpec((s, PW), lambda i: (0, 2 * GW // PW)),
                  pl.BlockSpec((NG, LANES, LANES), lambda i: (0, 0, 0)), pl.BlockSpec((1, PW), lambda i: (0, 0)), ANY],
        out_specs=pl.BlockSpec((s, PW), lambda i: (0, GW // PW)),
        out_shape=jax.ShapeDtypeStruct((s, D), BF16), input_output_aliases={3: 0},
        scratch_shapes=[pltpu.VMEM((s + POOL_HALO, LANES), F32)], compiler_params=_cparams(),
    )(z, wp, sp, y)


def _pool_bwd(z, dy, wp, sp, dz):
    s = z.shape[0]
    nt = s // ROW_TILE

    def body(p_ref, dy_ref, wp_ref, sp_ref, _, dp_ref, dwp_ref, dsp_ref, ppad, rpad, dpool):
        for g in range(NG):
            cols = slice(g * LANES, (g + 1) * LANES)
            ppad[pl.ds(0, POOL_HALO), :] = jnp.zeros((POOL_HALO, LANES), F32)
            ppad[pl.ds(POOL_HALO, s), :] = p_ref[:, cols]
            rpad[pl.ds(s, POOL_HALO), :] = jnp.zeros((POOL_HALO, LANES), F32)
            wpb = wp_ref[g].astype(BF16)
            scale = sp_ref[:, cols]

            def tile(t, carry):
                dwp, dsp = carry
                t0 = pl.multiple_of(t * ROW_TILE, ROW_TILE)
                pooled = _pool_pooled(ppad, t0, g)
                pb = pooled.astype(BF16)
                dyt = dy_ref[pl.ds(t0, ROW_TILE), cols]
                dsp = dsp + jnp.sum(dyt * _dot(pb, wpb, NN), axis=0, keepdims=True)
                dmm = (dyt * scale).astype(BF16)
                dwp = dwp + _dot(pb, dmm, TN)
                dpooled = _dot(dmm, wpb, NT)
                rpad[pl.ds(t0, ROW_TILE), :] = dpooled / _pool_count(t0, POOL_WINDOWS[g])
                dpool[pl.ds(t0, ROW_TILE), :] = dpooled
                return dwp, dsp

            dwp, dsp = lax.fori_loop(0, nt, tile, (jnp.zeros((LANES, LANES), F32), jnp.zeros((1, LANES), F32)))
            dwp_ref[g] = dwp
            dsp_ref[:, cols] = dsp

            def tile2(t, carry):
                t0 = pl.multiple_of(t * ROW_TILE, ROW_TILE)
                win = rpad[pl.ds(t0, ROW_TILE + POOL_HALO), :]
                back = _window_sum(win, g + 1, True)[:ROW_TILE]
                rows = pl.ds(t0, ROW_TILE)
                dp_ref[rows, cols] = (back - dpool[rows, :]).astype(BF16)
                return carry

            lax.fori_loop(0, nt, tile2, 0)

    return pl.pallas_call(
        body, name="pool_bwd", grid=(1,),
        in_specs=[pl.BlockSpec((s, PW), lambda i: (0, 2 * GW // PW)), pl.BlockSpec((s, PW), lambda i: (0, GW // PW)),
                  pl.BlockSpec((NG, LANES, LANES), lambda i: (0, 0, 0)), pl.BlockSpec((1, PW), lambda i: (0, 0)), ANY],
        out_specs=[pl.BlockSpec((s, PW), lambda i: (0, 2 * GW // PW)),
                   pl.BlockSpec((NG, LANES, LANES), lambda i: (0, 0, 0)), pl.BlockSpec((1, PW), lambda i: (0, 0))],
        out_shape=[jax.ShapeDtypeStruct((s, IN_COLS), BF16), jax.ShapeDtypeStruct((NG, LANES, LANES), F32),
                   jax.ShapeDtypeStruct((1, PW), F32)],
        input_output_aliases={4: 0},
        scratch_shapes=[pltpu.VMEM((s + POOL_HALO, LANES), F32), pltpu.VMEM((s + POOL_HALO, LANES), F32),
                        pltpu.VMEM((s, LANES), F32)],
        compiler_params=_cparams(),
    )(z, dy, wp, sp, dz)


CONV_LEAD = CONV_HALO - (CONV_K - 1)


SUBLANES = 8


def _sublane_shifts(win):
    n = win.shape[0]
    return [win] + [pltpu.roll(win, n - b, 0) for b in range(1, SUBLANES)]


def _shifted(shifts, offset):
    a, b = divmod(offset, SUBLANES)
    return shifts[b][a * SUBLANES:a * SUBLANES + ROW_TILE]


def _conv_taps(shifts, wdw_ref, lead, reverse):
    acc = jnp.zeros((ROW_TILE, CW), F32)
    for j in range(CONV_K):
        tap = (CONV_K - 1 - j) if reverse else j
        acc = acc + wdw_ref[tap:tap + 1, :] * _shifted(shifts, lead + j)
    return acc


def _conv_fill_glu(cv_ref, cg_ref, xpad, s):
    xpad[pl.ds(0, CONV_HALO), :] = jnp.zeros((CONV_HALO, CW), F32)

    def fill(t, carry):
        t0 = pl.multiple_of(t * ROW_TILE, ROW_TILE)
        rows = pl.ds(t0, ROW_TILE)
        xpad[pl.ds(t0 + CONV_HALO, ROW_TILE), :] = _glu(cv_ref[rows, :], cg_ref[rows, :])
        return carry

    lax.fori_loop(0, s // ROW_TILE, fill, 0)


def _conv_fwd(z, wdw, bdw, lng, lnb, y):
    s = z.shape[0]

    def body(cv_ref, cg_ref, wdw_ref, bdw_ref, lng_ref, lnb_ref, _, y_ref, xpad):
        _conv_fill_glu(cv_ref, cg_ref, xpad, s)

        def tile(t, carry):
            t0 = pl.multiple_of(t * ROW_TILE, ROW_TILE)
            shifts = _sublane_shifts(xpad[pl.ds(t0, ROW_TILE + CONV_HALO), :])
            hc = _conv_taps(shifts, wdw_ref, CONV_LEAD, False) + bdw_ref[...]
            y_ref[pl.ds(t0, ROW_TILE), :] = _ln_silu(hc, lng_ref[...], lnb_ref[...]).astype(BF16)
            return carry

        lax.fori_loop(0, s // ROW_TILE, tile, 0)

    vec = pl.BlockSpec((1, CW), lambda i: (0, 0))
    return pl.pallas_call(
        body, name="conv_fwd", grid=(1,),
        in_specs=[pl.BlockSpec((s, CW), lambda i: (0, (2 * GW + PW) // CW)),
                  pl.BlockSpec((s, CW), lambda i: (0, (2 * GW + PW) // CW + 1)),
                  pl.BlockSpec((CONV_K + 1, CW), lambda i: (0, 0)), vec, vec, vec, ANY],
        out_specs=pl.BlockSpec((s, CW), lambda i: (0, (GW + PW) // CW)),
        out_shape=jax.ShapeDtypeStruct((s, D), BF16), input_output_aliases={6: 0},
        scratch_shapes=[pltpu.VMEM((s + CONV_HALO, CW), F32)], compiler_params=_cparams(),
    )(z, z, wdw, bdw, lng, lnb, y)


def _conv_bwd(z, dy, wdw, bdw, lng, lnb, dz):
    s = z.shape[0]

    def body(cv_ref, cg_ref, dy_ref, wdw_ref, bdw_ref, lng_ref, lnb_ref, _,
             dz_ref, dwdw_ref, dbdw_ref, dlng_ref, dlnb_ref, xpad, dpad, dcg_keep):
        @pl.when(pl.program_id(0) == 0)
        def _():
            compute(cv_ref, cg_ref, dy_ref, wdw_ref, bdw_ref, lng_ref, lnb_ref,
                    dz_ref, dcg_keep, dwdw_ref, dbdw_ref, dlng_ref, dlnb_ref, xpad, dpad)

        @pl.when(pl.program_id(0) == 1)
        def _():
            dz_ref[...] = dcg_keep[...]

    def compute(cv_ref, cg_ref, dy_ref, wdw_ref, bdw_ref, lng_ref, lnb_ref,
                dcv_ref, dcg_ref, dwdw_ref, dbdw_ref, dlng_ref, dlnb_ref, xpad, dpad):
        _conv_fill_glu(cv_ref, cg_ref, xpad, s)
        dpad[pl.ds(s, CONV_HALO), :] = jnp.zeros((CONV_HALO, CW), F32)
        dwdw_ref[...] = jnp.zeros((CONV_K + 1, CW), F32)

        def tile(t, carry):
            db, dg, dbeta = carry
            t0 = pl.multiple_of(t * ROW_TILE, ROW_TILE)
            shifts = _sublane_shifts(xpad[pl.ds(t0, ROW_TILE + CONV_HALO), :])
            hc = _conv_taps(shifts, wdw_ref, CONV_LEAD, False) + bdw_ref[...]
            _, vjp = jax.vjp(_ln_silu, hc, lng_ref[...], lnb_ref[...])
            dhc, dg_t, dbeta_t = vjp(dy_ref[pl.ds(t0, ROW_TILE), :])
            dpad[pl.ds(t0, ROW_TILE), :] = dhc
            for j in range(CONV_K):
                dwdw_ref[j:j + 1, :] += jnp.sum(dhc * _shifted(shifts, CONV_LEAD + j), axis=0, keepdims=True)
            return db + jnp.sum(dhc, axis=0, keepdims=True), dg + dg_t, dbeta + dbeta_t

        zero = jnp.zeros((1, CW), F32)
        db, dg, dbeta = lax.fori_loop(0, s // ROW_TILE, tile, (zero, zero, zero))
        dbdw_ref[...] = db
        dlng_ref[...] = dg
        dlnb_ref[...] = dbeta

        def tile2(t, carry):
            t0 = pl.multiple_of(t * ROW_TILE, ROW_TILE)
            rows = pl.ds(t0, ROW_TILE)
            dglu = _conv_taps(_sublane_shifts(dpad[pl.ds(t0, ROW_TILE + CONV_HALO), :]), wdw_ref, 0, True)
            _, vjp = jax.vjp(_glu, cv_ref[rows, :], cg_ref[rows, :])
            dcv, dcg = vjp(dglu)
            dcv_ref[rows, :] = dcv.astype(BF16)
            dcg_ref[rows, :] = dcg.astype(BF16)
            return carry

        lax.fori_loop(0, s // ROW_TILE, tile2, 0)

    vec = pl.BlockSpec((1, CW), lambda i: (0, 0))
    wspec = pl.BlockSpec((CONV_K + 1, CW), lambda i: (0, 0))
    vshape = jax.ShapeDtypeStruct((1, CW), F32)
    return pl.pallas_call(
        body, name="conv_bwd", grid=(2,),
        in_specs=[pl.BlockSpec((s, CW), lambda i: (0, (2 * GW + PW) // CW)),
                  pl.BlockSpec((s, CW), lambda i: (0, (2 * GW + PW) // CW + 1)),
                  pl.BlockSpec((s, CW), lambda i: (0, (GW + PW) // CW)), wspec, vec, vec, vec, ANY],
        out_specs=[pl.BlockSpec((s, CW), lambda i: (0, (2 * GW + PW) // CW + i)), wspec, vec, vec, vec],
        out_shape=[jax.ShapeDtypeStruct((s, IN_COLS), BF16), jax.ShapeDtypeStruct((CONV_K + 1, CW), F32),
                   vshape, vshape, vshape],
        input_output_aliases={7: 0},
        scratch_shapes=[pltpu.VMEM((s + CONV_HALO, CW), F32), pltpu.VMEM((s + CONV_HALO, CW), F32),
                        pltpu.VMEM((s, CW), BF16)],
        compiler_params=_cparams(),
    )(z, z, dy, wdw, bdw, lng, lnb, dz)


def _softmax_rows(sc):
    e = jnp.exp(sc - jnp.max(sc, axis=-1, keepdims=True))
    return e / jnp.sum(e, axis=-1, keepdims=True)


def _attn_fwd(q, k, v, tq):
    s, m = q.shape[0], k.shape[0]
    tq = min(tq, s)

    def body(q_ref, k_ref, v_ref, o_ref):
        for h in range(XH):
            cols = slice(h * XHD, (h + 1) * XHD)
            p = _softmax_rows(_dot(q_ref[:, cols], k_ref[:, cols], NT) * ATT_SCALE)
            o_ref[:, cols] = _dot(p.astype(BF16), v_ref[:, cols], NN).astype(BF16)

    kv = pl.BlockSpec((m, D), lambda i: (0, 0))
    return pl.pallas_call(
        body, name="attn_fwd", grid=(s // tq,),
        in_specs=[pl.BlockSpec((tq, D), lambda i: (i, 0)), kv, kv],
        out_specs=pl.BlockSpec((tq, D), lambda i: (i, 0)),
        out_shape=jax.ShapeDtypeStruct((s, D), BF16), compiler_params=_cparams(),
    )(q, k, v)


def _attn_bwd(q, k, v, do, tq):
    s, m = q.shape[0], k.shape[0]
    tq = min(tq, s)

    def body(q_ref, k_ref, v_ref, do_ref, dq_ref, dk_ref, dv_ref):
        @pl.when(pl.program_id(0) == 0)
        def _():
            dk_ref[...] = jnp.zeros_like(dk_ref)
            dv_ref[...] = jnp.zeros_like(dv_ref)

        for h in range(XH):
            cols = slice(h * XHD, (h + 1) * XHD)
            qh, kh, vh, doh = q_ref[:, cols], k_ref[:, cols], v_ref[:, cols], do_ref[:, cols]
            p = _softmax_rows(_dot(qh, kh, NT) * ATT_SCALE)
            dp = _dot(doh, vh, NT)
            dv_ref[:, cols] += _dot(p.astype(BF16), doh, TN)
            ds = (p * (dp - jnp.sum(p * dp, axis=-1, keepdims=True)) * ATT_SCALE).astype(BF16)
            dq_ref[:, cols] = _dot(ds, kh, NN).astype(BF16)
            dk_ref[:, cols] += _dot(ds, qh, TN)

    kv = pl.BlockSpec((m, D), lambda i: (0, 0))
    qs = pl.BlockSpec((tq, D), lambda i: (i, 0))
    return pl.pallas_call(
        body, name="attn_bwd", grid=(s // tq,),
        in_specs=[qs, kv, kv, qs], out_specs=[qs, kv, kv],
        out_shape=[jax.ShapeDtypeStruct((s, D), BF16), jax.ShapeDtypeStruct((m, D), F32),
                   jax.ShapeDtypeStruct((m, D), F32)],
        compiler_params=_cparams(),
    )(q, k, v, do)


def _loss_head(y, target, tm):
    s = y.shape[0]
    tm = min(tm, s)

    def body(y_ref, t_ref, dy_ref, part_ref):
        err = y_ref[...] - t_ref[...]
        dy_ref[...] = err * (1.0 / D)
        part_ref[...] = jnp.full((1, 8, LANES), 0.5 * jnp.sum(err * err) * (1.0 / D), F32)

    blk = pl.BlockSpec((tm, D), lambda i: (i, 0))
    return pl.pallas_call(
        body, name="loss_head", grid=(s // tm,), in_specs=[blk, blk],
        out_specs=[blk, pl.BlockSpec((1, 8, LANES), lambda i: (i, 0, 0))],
        out_shape=[jax.ShapeDtypeStruct((s, D), F32), jax.ShapeDtypeStruct((s // tm, 8, LANES), F32)],
        compiler_params=_cparams(),
    )(y, target)


def _layer_fwd(x0, mem, w, p, fetch):
    z, hn0 = _rowop_mm("mix_in", "rms", (x0,), p["norm_mix_pre"], w["w_in"], NT, F32)
    y = _gmlp_fwd(z, p["gmlp_v_gain"], p["w_spatial"], p["b_spatial_t"], 512)
    y = _pool_fwd(z, p["w_pool"], p["s_pool"], y)
    y = _conv_fwd(z, p["w_dw"], p["b_dw"], p["conv_ln_g"], p["conv_ln_b"], y)
    w.update(fetch("out", (y,)))
    x1, h0 = _mm_rowop("mix_out", "rms_res", [(y, w["w_out"], NN)], (x0,), p["norm_mix_post"])
    w.update(fetch("att", (x1,)))
    q, hn1 = _rowop_mm("att_q", "rms", (x1,), p["norm_xattn_pre"], w["w_q"], NN, BF16)
    k, mn = _rowop_mm("att_k", "rms", (mem,), p["norm_mem"], w["w_k"], NN, BF16)
    v, _ = _rowop_mm("att_v", "rms", (mem,), p["norm_mem"], w["w_v"], NN, BF16)
    o = _attn_fwd(q, k, v, 256)
    x2, h1 = _mm_rowop("att_o", "rms_res", [(o, w["w_o"], NN)], (x1,), p["norm_xattn_post"])
    w.update(fetch("up", (x2,)))
    u, hn2 = _rowop_mm("ffn_up", "rms", (x2,), p["norm_ffn_pre"], w["w_up"], NT, F32)
    w.update(fetch("down", (u,)))
    x3, h2 = _mm_rowop("ffn_down", "rms_res", [(u, w["w_down"], NN)], (x2,), p["norm_ffn_post"], relu2=True)
    saved = dict(x0=x0, z=z, hn0=hn0, y=y, h0=h0, x1=x1, q=q, hn1=hn1, k=k, v=v, mn=mn, o=o, h1=h1, x2=x2, u=u,
                 hn2=hn2, h2=h2)
    return x3, saved


def _layer_bwd(dx3, mem, w, p, sv, red):
    gs = {}
    du, dh2, dg = _rowop_mm("ffn_down_bwd", "rms_bwd", (sv["h2"], dx3), p["norm_ffn_post"], w["w_down"], NT, BF16,
                            u=sv["u"], after=red.after())
    gs["norm_ffn_post"] = jnp.sum(dg, axis=0)
    g_down = _mm_tn("ffn_down_dw", sv["u"], dh2, relu2=True)
    red.advance((g_down,))
    dx2, dg = _mm_rowop("ffn_up_bwd", "rms_bwd_res", [(du, w["w_up"], NN)], (sv["x2"], dx3), p["norm_ffn_pre"],
                        after=red.after())
    gs["norm_ffn_pre"] = jnp.sum(dg, axis=0)
    g_up = _mm_tn("ffn_up_dw", du, sv["hn2"])
    red.add("ffn", ("w_down", "w_up"), [g_down, g_up])
    do, dh1, dg = _rowop_mm("att_o_bwd", "rms_bwd", (sv["h1"], dx2), p["norm_xattn_post"], w["w_o"], NT, BF16,
                            after=red.after())
    gs["norm_xattn_post"] = jnp.sum(dg, axis=0)
    g_o = _mm_tn("att_o_dw", sv["o"], dh1)
    red.advance((g_o,))
    dq, dk, dv = _attn_bwd(sv["q"], sv["k"], sv["v"], do, 256)
    dk, dv = dk.astype(BF16), dv.astype(BF16)
    dx1, dg = _mm_rowop("att_q_bwd", "rms_bwd_res", [(dq, w["w_q"], NT)], (sv["x1"], dx2), p["norm_xattn_pre"],
                        after=red.after())
    gs["norm_xattn_pre"] = jnp.sum(dg, axis=0)
    g_q = _mm_tn("att_q_dw", sv["hn1"], dq)
    g_k = _mm_tn("att_k_dw", sv["mn"], dk)
    g_v = _mm_tn("att_v_dw", sv["mn"], dv)
    (dg,) = _mm_rowop("att_kv_bwd", "rms_bwd_gain", [(dk, w["w_k"], NT), (dv, w["w_v"], NT)], (mem,), p["norm_mem"])
    gs["norm_mem"] = jnp.sum(dg, axis=0)
    red.add("att", ("w_o", "w_q", "w_k", "w_v"), [g_o, g_q, g_k, g_v])
    dy, dh0, dg = _rowop_mm("mix_out_bwd", "rms_bwd", (sv["h0"], dx1), p["norm_mix_post"], w["w_out"], NT, F32,
                            after=red.after())
    gs["norm_mix_post"] = jnp.sum(dg, axis=0)
    g_out = _mm_tn("mix_out_dw", sv["y"], dh0)
    red.advance((g_out,))
    red.add("out", ("w_out",), [g_out])
    z = sv["z"]
    dz, dgv, dws, dbs = _gmlp_bwd(z, dy, p["gmlp_v_gain"], p["w_spatial"], p["b_spatial_t"], 512)
    gs["gmlp_v_gain"] = jnp.sum(dgv, axis=0)
    gs["w_spatial"] = jnp.sum(dws, axis=0)
    gs["b_spatial"] = jnp.sum(dbs[..., 0], axis=0)
    dz, gs["w_pool"], gs["s_pool"] = _pool_bwd(z, dy, p["w_pool"], p["s_pool"], dz)
    dz, dwdw, gs["b_dw"], gs["conv_ln_g"], gs["conv_ln_b"] = _conv_bwd(
        z, dy, p["w_dw"], p["b_dw"], p["conv_ln_g"], p["conv_ln_b"], dz)
    red.advance((dz,))
    g_in = _mm_tn("mix_in_dw", dz, sv["hn0"], after=red.after())
    red.add("in", ("w_in",), [g_in])
    red.advance((g_in,))
    dx0, dg = _mm_rowop("mix_in_bwd", "rms_bwd_res", [(dz, w["w_in"], NN)], (sv["x0"], dx1), p["norm_mix_pre"],
                        after=red.after())
    gs["norm_mix_pre"] = jnp.sum(dg, axis=0)
    red.small(_small_grad_arrays(gs, dwdw))
    return dx0


NORM_NAMES = ("norm_mix_pre", "norm_mix_post", "norm_xattn_pre", "norm_mem", "norm_xattn_post", "norm_ffn_pre",
              "norm_ffn_post")
VEC_NAMES = ("s_pool", "b_dw", "conv_ln_g", "conv_ln_b")
SMALL_ARRAYS = ("norms", "gain_bias", "w_spatial", "w_pool", "vecs", "w_dw")


def _small_grad_arrays(gs, dwdw):
    return {"norms": jnp.concatenate([gs[n] for n in NORM_NAMES], axis=0),
            "gain_bias": jnp.concatenate([gs["gmlp_v_gain"], gs["b_spatial"]], axis=0),
            "w_spatial": gs["w_spatial"], "w_pool": gs["w_pool"],
            "vecs": jnp.concatenate([gs[n] for n in VEC_NAMES], axis=0), "w_dw": dwdw}


def _split_small_grads(arrays):
    out = {n: arrays["norms"][k] for k, n in enumerate(NORM_NAMES)}
    out.update({n: arrays["vecs"][k] for k, n in enumerate(VEC_NAMES)})
    out.update(gmlp_v_gain=arrays["gain_bias"][:NH], b_spatial=arrays["gain_bias"][NH:], w_spatial=arrays["w_spatial"],
               w_pool=arrays["w_pool"], w_dw=arrays["w_dw"][:CONV_K])
    return out


def _layer_params(small, l):
    p = {n: small[n][l].reshape(1, -1) for n in ("norm_mix_pre", "norm_mix_post", "s_pool", "b_dw", "conv_ln_g",
                                                   "conv_ln_b", "norm_xattn_pre", "norm_mem", "norm_xattn_post",
                                                   "norm_ffn_pre", "norm_ffn_post")}
    p["gmlp_v_gain"] = small["gmlp_v_gain"][l]
    p["w_spatial"] = small["w_spatial"][l]
    p["b_spatial_t"] = small["b_spatial"][l].T
    p["w_pool"] = small["w_pool"][l]
    p["w_dw"] = jnp.pad(small["w_dw"][l], ((0, 1), (0, 0)))
    return p


def _local_step(x, mem, target, fetch, small, red):
    small = dict(small)
    saved, weights, params = [], [], []
    h = x
    marker = ()
    for l in range(DEPTH):
        w = fetch(l, "in", marker)
        if "taps" in w:
            small["w_dw"] = w.pop("taps")
        p = _layer_params(small, l)
        h, sv = _layer_fwd(h, mem, w, p, functools.partial(fetch, l))
        marker = (h,)
        saved.append(sv)
        weights.append(w)
        params.append(p)
    dh, loss = _loss_head(h, target, 512)
    red.extra = (loss,)
    for l in reversed(range(DEPTH)):
        red.layer = l
        dh = _layer_bwd(dh, mem, weights[l], params[l], saved[l], red)
    return loss, dh


HBM = pl.BlockSpec(memory_space=pltpu.HBM)


def _position():
    return lax.axis_index("x"), lax.axis_index("y"), lax.axis_index("c")


SEM = pl.BlockSpec(memory_space=pltpu.SEMAPHORE)
EFFECT = pltpu.SideEffectType.DATAFLOW_SIDE_EFFECTING
TOKEN = jax.ShapeDtypeStruct((8, LANES), F32)
TOKEN_SPEC = pl.BlockSpec(memory_space=pltpu.VMEM)


def _landing(shape, dtype):
    return pltpu.with_memory_space_constraint(lax.empty(shape, dtype), pltpu.HBM)


def _hbm_shapes(arrays):
    return [pltpu.HBM(a.shape, a.dtype) for a in arrays]


def _block(ref, r, dev):
    return ref.at[pl.ds((4 * dev[0] + 2 * dev[1] + dev[2]) * r, r), :]


def _split_call(name, body, thru, sems_in, after, sems_out, token):
    n = len(thru)
    out_shape = [pltpu.SemaphoreType.DMA(s) for s in sems_out] + _hbm_shapes(thru) + ([TOKEN] if token else [])
    out_specs = [SEM] * len(sems_out) + [HBM] * n + ([TOKEN_SPEC] if token else [])
    return pl.pallas_call(
        body, name=name, in_specs=[HBM] * n + [SEM] * len(sems_in) + [ANY] * len(after),
        out_specs=out_specs, out_shape=out_shape,
        input_output_aliases={i: len(sems_out) + i for i in range(n)},
        compiler_params=pltpu.CompilerParams(has_side_effects=EFFECT),
    )(*thru, *sems_in, *after)


def _place_own(name, srcs, dev, out_dtype, tr):
    n = len(srcs)
    r, cols = srcs[0][0].shape[-2:]
    tr = r if r < 16 else _row_tile(r, tr)
    nb = r // tr

    def body(dev_ref, *refs):
        for a in range(n):
            refs[n + a][...] = refs[a][...].astype(out_dtype)

    in_specs = [pl.BlockSpec((tr, cols), lambda i, d: (i, 0)) if l is None
                else pl.BlockSpec((None, tr, cols), lambda i, d, l=l: (l, i, 0)) for _, l in srcs]
    return pl.pallas_call(
        body, name=name,
        grid_spec=pltpu.PrefetchScalarGridSpec(
            num_scalar_prefetch=1, grid=(nb,), in_specs=in_specs,
            out_specs=[pl.BlockSpec((tr, cols), lambda i, d: (d[0] * nb + i, 0))] * n),
        out_shape=[jax.ShapeDtypeStruct((N_DEV * r, cols), out_dtype)] * n, compiler_params=_cparams(),
    )(dev, *[a for a, _ in srcs])


def _gather_peers(x, y, c):
    return [(1 - x, y, c), (x, 1 - y, c), (1 - x, 1 - y, c), (x, y, 1 - c)]


def _block_rows(land):
    return land.shape[0] // N_DEV


def _near_peers(x, y, c):
    return [(1 - x, y, c), (x, 1 - y, c), (x, y, 1 - c)]


def _relay_route(x, y, c):
    origin = (x + c * (1 - 2 * x), y + (1 - c) * (1 - 2 * y), c)
    target = (x + (1 - c) * (1 - 2 * x), y + c * (1 - 2 * y), c)
    return origin, target


def _same_block_copy(blk, send_sem, recv_sem, to):
    return pltpu.make_async_remote_copy(src_ref=blk, dst_ref=blk, send_sem=send_sem, recv_sem=recv_sem, device_id=to,
                                        device_id_type=MESH)


def _gather_start(name, lands, after):
    n = len(lands)

    def body(*refs):
        lz = refs[:n]
        send_sems, recv_sems = refs[n + len(after)], refs[n + len(after) + 1]
        token = refs[-1]
        x, y, c = _position()
        for a in range(n):
            own = _block(lz[a], _block_rows(lands[a]), (x, y, c))
            for k, to in enumerate(_near_peers(x, y, c)):
                _same_block_copy(own, send_sems.at[k], recv_sems.at[k], to).start()
        token[...] = jnp.zeros_like(token)

    out = _split_call(name, body, list(lands), [], after, [(3,), (3,)], True)
    return out[0], out[1], out[2:2 + n], out[-1]


def _gather_step(name, near, far, fresh, after):
    groups = [g for g in (near and near[0], far and far[0], fresh) if g]
    counts = [len(near[0]) if near else 0, len(far[0]) if far else 0, len(fresh) if fresh else 0]
    n = sum(counts)
    sems_in = ([near[1]] if near else []) + ([far[1]] if far else [])
    sems_out = ([(2,), (2,), (1,), (1,)] if near else []) + ([(1,), (1,)] if far else []) + ([(3,), (3,)] if fresh else [])

    def body(*refs):
        lz = list(refs[:n])
        ins = list(refs[n:n + len(sems_in)])
        outs = list(refs[n + len(sems_in) + len(after):n + len(sems_in) + len(after) + len(sems_out)])
        token = refs[-1]
        x, y, c = _position()
        me, sibling = (x, y, c), (x, y, 1 - c)
        near_lz, far_lz, fresh_lz = (lz[sum(counts[:i]):sum(counts[:i + 1])] for i in range(3))
        neighbours = _near_peers(x, y, c)[:2]
        origin, target = _relay_route(x, y, c)
        diagonal = (1 - x, 1 - y, c)
        if near:
            recv0 = ins.pop(0)
            fsend, frecv, rsend, rrecv = (outs.pop(0) for _ in range(4))
            for a, land in enumerate(near[0]):
                for j, chip in enumerate(neighbours):
                    _same_block_copy(_block(near_lz[a], _block_rows(land), chip), fsend.at[j], recv0.at[j], me).wait_recv()
        if far:
            rrecv_in = ins.pop(0)
            f2send, f2recv = outs.pop(0), outs.pop(0)
            for a, land in enumerate(far[0]):
                _same_block_copy(_block(far_lz[a], _block_rows(land), diagonal), f2send.at[0], rrecv_in.at[0], me).wait_recv()
            for a, land in enumerate(far[0]):
                _same_block_copy(_block(far_lz[a], _block_rows(land), diagonal), f2send.at[0], f2recv.at[0], sibling).start()
        if near:
            for a, land in enumerate(near[0]):
                r = _block_rows(land)
                _same_block_copy(_block(near_lz[a], r, origin), rsend.at[0], rrecv.at[0], target).start()
                for j, chip in enumerate(neighbours):
                    _same_block_copy(_block(near_lz[a], r, chip), fsend.at[j], frecv.at[j], sibling).start()
        if fresh:
            send_sems, recv_sems = outs.pop(0), outs.pop(0)
            for a, land in enumerate(fresh):
                own = _block(fresh_lz[a], _block_rows(land), me)
                for k, to in enumerate(_near_peers(x, y, c)):
                    _same_block_copy(own, send_sems.at[k], recv_sems.at[k], to).start()
        token[...] = jnp.zeros_like(token)

    out = list(_split_call(name, body, [l for g in groups for l in g], sems_in, after, sems_out, True))
    res = {"token": out.pop()}
    if near:
        res.update(fsend=out.pop(0), frecv=out.pop(0), rsend=out.pop(0), rrecv=out.pop(0))
    if far:
        res.update(f2send=out.pop(0), f2recv=out.pop(0))
    if fresh:
        res.update(send=out.pop(0), recv=out.pop(0))
    res["near"], res["far"], res["fresh"] = (out[sum(counts[:i]):sum(counts[:i + 1])] for i in range(3))
    return res


def _gather_finish(name, lands, send_sems, recv_sems, fsend, frecv, rsend, f2send, f2recv, after):
    n = len(lands)

    def body(*refs):
        lz = refs[:n]
        send0, recv0, fsend_ref, frecv_ref, rsend_ref, f2send_ref, f2recv_ref = refs[n:n + 7]
        x, y, c = _position()
        me = (x, y, c)
        near = _near_peers(x, y, c)[:2]
        origin, _ = _relay_route(x, y, c)
        for a in range(n):
            r = _block_rows(lands[a])
            sib = _block(lz[a], r, (x, y, 1 - c))
            _same_block_copy(sib, send0.at[2], recv0.at[2], me).wait_recv()
            for j, chip in enumerate(near):
                blk = _block(lz[a], r, (chip[0], chip[1], 1 - c))
                _same_block_copy(blk, fsend_ref.at[j], frecv_ref.at[j], me).wait_recv()
            far = _block(lz[a], r, (1 - x, 1 - y, 1 - c))
            _same_block_copy(far, f2send_ref.at[0], f2recv_ref.at[0], me).wait_recv()
            own = _block(lz[a], r, me)
            for k in range(3):
                _same_block_copy(own, send0.at[k], recv0.at[k], me).wait_send()
            for j, chip in enumerate(near):
                _same_block_copy(_block(lz[a], r, chip), fsend_ref.at[j], frecv_ref.at[j], me).wait_send()
            _same_block_copy(_block(lz[a], r, origin), rsend_ref.at[0], recv0.at[0], me).wait_send()
            _same_block_copy(_block(lz[a], r, (1 - x, 1 - y, c)), f2send_ref.at[0], f2recv_ref.at[0], me).wait_send()

    return _split_call(name, body, list(lands), [send_sems, recv_sems, fsend, frecv, rsend, f2send, f2recv], after, [],
                       False)


def _sibling_start(name, grads, after):
    n = len(grads)
    lands = [_landing((4, g.shape[0] // N_DEV, D), g.dtype) for g in grads]

    def body(*refs):
        ins, lz = refs[:n], refs[n:2 * n]
        send_sem, recv_sem = refs[2 * n + len(after)], refs[2 * n + len(after) + 1]
        token = refs[-1]
        x, y, c = _position()
        for a in range(n):
            r = grads[a].shape[0] // N_DEV
            for q in range(4):
                pltpu.make_async_remote_copy(
                    src_ref=ins[a].at[pl.ds((2 * q + 1 - c) * r, r), :], dst_ref=lz[a].at[q], send_sem=send_sem.at[0],
                    recv_sem=recv_sem.at[0], device_id=(x, y, 1 - c), device_id_type=MESH).start()
        token[...] = jnp.zeros_like(token)

    out = _split_call(name, body, list(grads) + lands, [], after, [(1,), (1,)], True)
    return out[0], out[1], out[2:2 + n], out[2 + n:2 + 2 * n], out[-1]


def _sibling_finish(name, grads, lands, send_sem, recv_sem, after):
    n = len(grads)

    def body(*refs):
        ins, lz = refs[:n], refs[n:2 * n]
        send_ref, recv_ref = refs[2 * n], refs[2 * n + 1]
        x, y, c = _position()
        for a in range(n):
            r = grads[a].shape[0] // N_DEV
            for q in range(4):
                cp = pltpu.make_async_remote_copy(
                    src_ref=ins[a].at[pl.ds((2 * q + 1 - c) * r, r), :], dst_ref=lz[a].at[q], send_sem=send_ref.at[0],
                    recv_sem=recv_ref.at[0], device_id=(x, y, c), device_id_type=MESH)
                cp.wait_send()
                cp.wait_recv()

    out = _split_call(name, body, list(grads) + list(lands), [send_sem, recv_sem], after, [], False)
    return out[:n], out[n:2 * n]


def _chip_start(name, parts, after):
    n = len(parts)
    lands = [_landing((3,) + p.shape[1:], p.dtype) for p in parts]

    def body(*refs):
        ins, lz = refs[:n], refs[n:2 * n]
        send_sems, recv_sems = refs[2 * n + len(after)], refs[2 * n + len(after) + 1]
        token = refs[-1]
        x, y, c = _position()
        for a in range(n):
            for j, chip in enumerate(_gather_peers(x, y, c)[:3]):
                pltpu.make_async_remote_copy(
                    src_ref=ins[a].at[2 * chip[0] + chip[1]], dst_ref=lz[a].at[j], send_sem=send_sems.at[j],
                    recv_sem=recv_sems.at[j], device_id=chip, device_id_type=MESH).start()
        token[...] = jnp.zeros_like(token)

    out = _split_call(name, body, list(parts) + lands, [], after, [(3,), (3,)], True)
    return out[0], out[1], out[2:2 + n], out[2 + n:2 + 2 * n], out[-1]


def _chip_finish(name, parts, lands, send_sems, recv_sems, after):
    n = len(parts)

    def body(*refs):
        ins, lz = refs[:n], refs[n:2 * n]
        send_ref, recv_ref = refs[2 * n], refs[2 * n + 1]
        me = _position()
        for a in range(n):
            for j in range(3):
                cp = pltpu.make_async_remote_copy(
                    src_ref=ins[a].at[j], dst_ref=lz[a].at[j], send_sem=send_ref.at[j], recv_sem=recv_ref.at[j],
                    device_id=me, device_id_type=MESH)
                cp.wait_send()
                cp.wait_recv()

    out = _split_call(name, body, list(parts) + list(lands), [send_sems, recv_sems], after, [], False)
    return out[:n], out[n:2 * n]


def _other_devices(x, y, c):
    return [(x + (k >> 2 & 1) * (1 - 2 * x), y + (k >> 1 & 1) * (1 - 2 * y), c + (k & 1) * (1 - 2 * c))
            for k in range(1, N_DEV)]


def _broadcast_start(name, arrays, after):
    n = len(arrays)
    lands = [_landing((N_DEV,) + a.shape, a.dtype) for a in arrays]

    def body(*refs):
        ins, lz = refs[:n], refs[n:2 * n]
        send_sems, recv_sems = refs[2 * n + len(after)], refs[2 * n + len(after) + 1]
        token = refs[-1]
        x, y, c = _position()
        for a in range(n):
            for k, peer in enumerate(_other_devices(x, y, c)):
                pltpu.make_async_remote_copy(
                    src_ref=ins[a], dst_ref=lz[a].at[4 * x + 2 * y + c], send_sem=send_sems.at[k],
                    recv_sem=recv_sems.at[k], device_id=peer, device_id_type=MESH).start()
        token[...] = jnp.zeros_like(token)

    out = _split_call(name, body, list(arrays) + lands, [], after, [(N_DEV - 1,), (N_DEV - 1,)], True)
    return out[0], out[1], out[2:2 + n], out[2 + n:2 + 2 * n], out[-1]


def _broadcast_finish(name, arrays, lands, send_sems, recv_sems, after):
    n = len(arrays)

    def body(*refs):
        ins, lz = refs[:n], refs[n:2 * n]
        send_ref, recv_ref = refs[2 * n], refs[2 * n + 1]
        x, y, c = _position()
        for a in range(n):
            for k, peer in enumerate(_other_devices(x, y, c)):
                cp = pltpu.make_async_remote_copy(
                    src_ref=ins[a], dst_ref=lz[a].at[4 * peer[0] + 2 * peer[1] + peer[2]], send_sem=send_ref.at[k],
                    recv_sem=recv_ref.at[k], device_id=(x, y, c), device_id_type=MESH)
                cp.wait_send()
                cp.wait_recv()

    out = _split_call(name, body, list(arrays) + list(lands), [send_sems, recv_sems], after, [], False)
    return out[:n], out[n:2 * n]


def _row_tile(r, target):
    return max(t for t in range(16, min(r, target) + 1, 16) if r % t == 0)


def _chip_partial(name, grad, got, c, tr):
    r = grad.shape[0] // N_DEV
    tr = _row_tile(r, tr)
    g4 = grad.reshape(4, 2, r, D)

    def body(c_ref, g_ref, s_ref, o_ref):
        o_ref[...] = (g_ref[...].astype(F32) + s_ref[...].astype(F32)).astype(BF16)

    return pl.pallas_call(
        body, name=name,
        grid_spec=pltpu.PrefetchScalarGridSpec(
            num_scalar_prefetch=1, grid=(4, r // tr),
            in_specs=[pl.BlockSpec((None, None, tr, D), lambda q, i, c_ref: (q, c_ref[0], i, 0)),
                      pl.BlockSpec((None, tr, D), lambda q, i, c_ref: (q, i, 0))],
            out_specs=pl.BlockSpec((None, tr, D), lambda q, i, c_ref: (q, i, 0))),
        out_shape=jax.ShapeDtypeStruct((4, r, D), BF16), compiler_params=_cparams(),
    )(c, g4, got)


class _WeightGather:
    def __init__(self, groups):
        self.groups = list(groups)
        self.index = {key: i for i, (key, _, _) in enumerate(groups)}
        self.state = [None] * len(groups)
        self.token = ()
        for i in range(min(2, len(groups))):
            self._start(i)

    def _tag(self, i):
        return "%s_%d" % self.groups[i][0][::-1]

    def _start(self, i):
        send, recv, lz, tok = _gather_start("gather_start_" + self._tag(i), self.groups[i][2], self.token)
        self.state[i] = dict(send=send, recv=recv, lands=lz)
        self.token = (tok,)

    def _step(self, name, near, far, fresh, marker):
        exists = lambda i: i is not None and i < len(self.groups)
        near, far, fresh = (i if exists(i) else None for i in (near, far, fresh))
        res = _gather_step(
            name, None if near is None else (self.state[near]["lands"], self.state[near]["recv"]),
            None if far is None else (self.state[far]["lands"], self.state[far]["rrecv"]),
            None if fresh is None else self.groups[fresh][2], tuple(marker) + self.token)
        self.token = (res["token"],)
        if near is not None:
            self.state[near].update(lands=res["near"], fsend=res["fsend"], frecv=res["frecv"], rsend=res["rsend"],
                                    rrecv=res["rrecv"])
        if far is not None:
            self.state[far].update(lands=res["far"], f2send=res["f2send"], f2recv=res["f2recv"])
        if fresh is not None:
            self.state[fresh] = dict(send=res["send"], recv=res["recv"], lands=res["fresh"])

    def fetch(self, layer, group, marker):
        k = self.index[(layer, group)]
        if k == 0:
            self._step("gather_step_first", 0, None, None, marker)
        self._step("gather_step_" + self._tag(k), k + 1, k, k + 2, marker)
        st = self.state[k]
        lz = _gather_finish("gather_finish_" + self._tag(k), st["lands"], st["send"], st["recv"], st["fsend"],
                            st["frecv"], st["rsend"], st["f2send"], st["f2recv"], self.token)
        self.state[k] = None
        return dict(zip(self.groups[k][1], lz))


class _GradReduce:
    def __init__(self, core, chip):
        self.core, self.chip = core, chip
        self.layer = None
        self.token = ()
        self.at_sibling, self.at_chips = [], []
        self.extra, self.smalls = (), {}

    def after(self):
        return self.token

    def add(self, group, names, grads):
        tag = "%s_%d" % (group, self.layer)
        send, recv, grads, lands, tok = _sibling_start("grad_sibling_start_" + tag, grads, self.token)
        self.at_sibling.append((tag, [(self.layer, n) for n in names], send, recv, grads, lands))
        self.token = (tok,)

    def advance(self, marker):
        for tag, keys, send, recv, grads, lands in self.at_sibling:
            grads, lands = _sibling_finish("grad_sibling_finish_" + tag, grads, lands, send, recv, marker)
            parts = [_chip_partial("chip_partial_%d_%s" % key, g, got, self.core, 256)
                     for key, g, got in zip(keys, grads, lands)]
            send, recv, parts, lands, tok = _chip_start("grad_chip_start_" + tag, parts, ())
            self.at_chips.append([tag, keys, send, recv, parts, lands])
            self.token = (tok,)
        self.at_sibling = []

    def small(self, arrays):
        extra = list(self.extra) if self.layer == 0 else []
        send, recv, own, slots, tok = _broadcast_start(
            "small_grads_start_%d" % self.layer, [arrays[k] for k in SMALL_ARRAYS] + extra, self.token)
        self.smalls[self.layer] = (send, recv, own, slots)
        self.token = (tok,)

    def small_finish(self, layer, marker):
        send, recv, own, slots = self.smalls[layer]
        return _broadcast_finish("small_grads_finish_%d" % layer, own, slots, send, recv, marker)

    def collect(self, key, marker):
        for entry in self.at_chips:
            tag, keys, send, recv, parts, lands = entry
            if key in keys:
                if send is not None:
                    parts, lands = _chip_finish("grad_chip_finish_" + tag, parts, lands, send, recv, marker)
                    entry[2:] = [None, None, parts, lands]
                i = keys.index(key)
                return parts[i], lands[i]
        raise KeyError(key)


def _adamw_math(w, g, m, v):
    m = ADAM_B1 * m + (1.0 - ADAM_B1) * g
    v = ADAM_B2 * v + (1.0 - ADAM_B2) * jnp.square(g)
    m_hat = m / (1.0 - ADAM_B1 ** ADAM_STEP)
    v_hat = v / (1.0 - ADAM_B2 ** ADAM_STEP)
    delta = -ADAM_LR * (m_hat / (jnp.sqrt(v_hat) + ADAM_EPS) + ADAM_WD * w)
    return delta, m, v


def _adamw_small(wts, mom_m, mom_v, own, gathered, loss_own, loss_gathered, dev):
    names = SMALL
    nw = len(names)
    na = len(SMALL_ARRAYS)

    def body(dev_ref, *refs):
        w_refs, m_refs, v_refs = (dict(zip(names, refs[i * nw:(i + 1) * nw])) for i in range(3))
        own_refs = refs[3 * nw:3 * nw + DEPTH * na]
        g_refs = refs[3 * nw + DEPTH * na:3 * nw + 2 * DEPTH * na]
        loss_own_ref, loss_got_ref = refs[3 * nw + 2 * DEPTH * na:3 * nw + 2 * DEPTH * na + 2]
        outs = refs[3 * nw + 2 * DEPTH * na + 2:]
        g_out, d_out, m_out, v_out = (dict(zip(names, outs[i * nw:(i + 1) * nw])) for i in range(4))
        me = dev_ref[0]

        loss = None
        for d in range(N_DEV):
            for b in range(loss_own.shape[0]):
                term = jnp.where(me == d, loss_own_ref[b], loss_got_ref[d, b])
                loss = term if loss is None else loss + term
        outs[4 * nw][...] = loss

        def update(name, at, g):
            g_out[name][at] = g
            d_out[name][at], m_out[name][at], v_out[name][at] = _adamw_math(
                w_refs[name][at], g, m_refs[name][at], v_refs[name][at])

        for l in range(DEPTH):
            mine = dict(zip(SMALL_ARRAYS, own_refs[l * na:(l + 1) * na]))
            got = dict(zip(SMALL_ARRAYS, g_refs[l * na:(l + 1) * na]))

            def total(key, at):
                acc = None
                for d in range(N_DEV):
                    term = jnp.where(me == d, mine[key][at] if at else mine[key][...], got[key][(d,) + at])
                    acc = term if acc is None else acc + term
                return acc

            row = (slice(l, l + 1),)
            for k, name in enumerate(NORM_NAMES):
                update(name, row, total("norms", (slice(k, k + 1),)))
            for k, name in enumerate(VEC_NAMES):
                update(name, row, total("vecs", (slice(k, k + 1),)))
            update("gmlp_v_gain", (l,), total("gain_bias", (slice(0, NH),)))
            update("b_spatial", (l,), total("gain_bias", (slice(NH, 2 * NH),)))
            update("w_spatial", (l,), total("w_spatial", ()))
            update("w_pool", (l,), total("w_pool", ()))
            update("w_dw", (l,), total("w_dw", (slice(0, CONV_K),)))

    args = [src[n] for src in (wts, mom_m, mom_v) for n in names]
    args += [src[l][k] for src in (own, gathered) for l in range(DEPTH) for k in SMALL_ARRAYS]
    args += [loss_own, loss_gathered]
    outs = pl.pallas_call(
        body, name="adamw_small",
        in_specs=[pl.BlockSpec(memory_space=pltpu.SMEM)] + [pl.BlockSpec(memory_space=pltpu.VMEM)] * len(args),
        out_shape=[jax.ShapeDtypeStruct(wts[n].shape, F32) for _ in range(4) for n in names]
        + [jax.ShapeDtypeStruct((8, LANES), F32)],
        compiler_params=_cparams(),
    )(dev, *args)
    return tuple(dict(zip(names, outs[i * nw:(i + 1) * nw])) for i in range(4)) + (outs[4 * nw],)


def _adamw_layers(name, w, reduced, m, v, chip, tr, transposed=False, after=()):
    nl, r, cdim = w.shape
    tr = _row_tile(r, tr)
    nb = r // tr

    def body(q_ref, w_ref, p0_ref, g0_ref, p1_ref, g1_ref, m_ref, v_ref, *rest):
        g_ref, d_ref, nm_ref, nv_ref = rest[len(after):]

        def total(p_ref, got_ref):
            acc = p_ref[...].astype(F32)
            for j in range(3):
                acc = acc + got_ref[j].astype(F32)
            return acc

        g = jnp.where(pl.program_id(0) == 0, total(p0_ref, g0_ref), total(p1_ref, g1_ref))
        if transposed:
            g = g.T
        g_ref[...] = g
        d_ref[...], nm_ref[...], nv_ref[...] = _adamw_math(w_ref[...], g, m_ref[...], v_ref[...])

    blk = pl.BlockSpec((None, tr, cdim), lambda l, i, q: (l, i, 0))
    first = lambda l, i: i * (1 - l) + (nb - 1) * l
    second = lambda l, i: i * l
    if transposed:
        gshape = (cdim, tr)
        at = lambda lead, i: (lead, 0, i)
    else:
        gshape = (tr, cdim)
        at = lambda lead, i: (lead, i, 0)
    specs = [blk,
             pl.BlockSpec((None,) + gshape, lambda l, i, q: at(q[0], first(l, i))),
             pl.BlockSpec((3,) + gshape, lambda l, i, q: at(0, first(l, i))),
             pl.BlockSpec((None,) + gshape, lambda l, i, q: at(q[0], second(l, i))),
             pl.BlockSpec((3,) + gshape, lambda l, i, q: at(0, second(l, i))), blk, blk] + [ANY] * len(after)
    shape = jax.ShapeDtypeStruct((nl, r, cdim), F32)
    return pl.pallas_call(
        body, name=name,
        grid_spec=pltpu.PrefetchScalarGridSpec(num_scalar_prefetch=1, grid=(nl, nb), in_specs=specs, out_specs=[blk] * 4),
        out_shape=[shape] * 4, compiler_params=_cparams(),
    )(chip, w, *reduced[0], *reduced[1], m, v, *after)


def _to_rows(name, a):
    return jnp.swapaxes(a, 1, 2) if name == "w_in" else a


def _place_own_transposed(name, srcs, dev, out_dtype, tc):
    n = len(srcs)
    kdim, cdim = srcs[0][0].shape[-2:]

    def body(dev_ref, *refs):
        for a in range(n):
            refs[n + a][...] = refs[a][...].T.astype(out_dtype)

    return pl.pallas_call(
        body, name=name,
        grid_spec=pltpu.PrefetchScalarGridSpec(
            num_scalar_prefetch=1, grid=(kdim // tc,),
            in_specs=[pl.BlockSpec((None, tc, cdim), lambda i, d, l=l: (l, i, 0)) for _, l in srcs],
            out_specs=[pl.BlockSpec((cdim, tc), lambda i, d: (d[0], i))] * n),
        out_shape=[jax.ShapeDtypeStruct((N_DEV * cdim, kdim), out_dtype)] * n, compiler_params=_cparams(),
    )(dev, *[a for a, _ in srcs])


def _pack(arrays, rows):
    flat = jnp.concatenate([a.reshape(-1) for a in arrays])
    return jnp.pad(flat, (0, rows * D - flat.shape[0])).reshape(rows, D)


def _rows_for(shapes, mult=8):
    total = 0
    for shp in shapes:
        size = 1
        for dim in shp:
            size *= dim
        total += size
    return -(-total // (mult * D)) * mult


def kernel(x, mem, norm_mix_pre, norm_mix_post, w_in, w_out, gmlp_v_gain, w_spatial, b_spatial, w_pool, s_pool, w_dw, b_dw, conv_ln_g, conv_ln_b, norm_xattn_pre, norm_mem, norm_xattn_post, w_q, w_k, w_v, w_o, norm_ffn_pre, norm_ffn_post, w_up, w_down, loss_target, m_norm_mix_pre, m_norm_mix_post, m_w_in, m_w_out, m_gmlp_v_gain, m_w_spatial, m_b_spatial, m_w_pool, m_s_pool, m_w_dw, m_b_dw, m_conv_ln_g, m_conv_ln_b, m_norm_xattn_pre, m_norm_mem, m_norm_xattn_post, m_w_q, m_w_k, m_w_v, m_w_o, m_norm_ffn_pre, m_norm_ffn_post, m_w_up, m_w_down, v_norm_mix_pre, v_norm_mix_post, v_w_in, v_w_out, v_gmlp_v_gain, v_w_spatial, v_b_spatial, v_w_pool, v_s_pool, v_w_dw, v_b_dw, v_conv_ln_g, v_conv_ln_b, v_norm_xattn_pre, v_norm_mem, v_norm_xattn_post, v_w_q, v_w_k, v_w_v, v_w_o, v_norm_ffn_pre, v_norm_ffn_post, v_w_up, v_w_down):
    args = dict(locals())
    wts = {n: args[n] for n in WEIGHTS}
    mom_m = {n: args["m_" + n] for n in WEIGHTS}
    mom_v = {n: args["v_" + n] for n in WEIGHTS}
    xi, yi, ci = _position()
    me = 4 * xi + 2 * yi + ci

    dev = jnp.reshape(me, (1,)).astype(jnp.int32)
    lands = {}
    for call, names, tr in (("place_in", ("w_in",), 256), ("place_att", ("w_out", "w_q", "w_k", "w_v", "w_o"), 64),
                            ("place_up", ("w_up",), 256), ("place_down", ("w_down",), 256)):
        srcs = [(_to_rows(n, wts[n]), l) for l in range(DEPTH) for n in names]
        placed = (_place_own_transposed if names == ("w_up",) else _place_own)(call, srcs, dev, BF16, tr)
        lands.update(zip([(l, n) for l in range(DEPTH) for n in names], placed))
    (lands[(0, "taps")],) = _place_own("place_taps", [(_pack([w_dw], _rows_for([w_dw.shape])), None)], dev, F32, 8)
    groups = []
    for l in range(DEPTH):
        for group, names in GATHER_GROUPS:
            if (l, group) == (0, "in"):
                names = names + ("taps",)
            groups.append(((l, group), names, [lands[(l, n)] for n in names]))
    gather = _WeightGather(groups)

    def fetch(layer, group, marker):
        w = gather.fetch(layer, group, marker)
        if "taps" in w:
            blocks = w["taps"].reshape(N_DEV, -1)[:, :w_dw.size].reshape((N_DEV,) + w_dw.shape)
            w["taps"] = jnp.moveaxis(blocks, 0, 2).reshape(DEPTH, CONV_K, CW)
        return w

    reduce = _GradReduce(jnp.reshape(ci, (1,)).astype(jnp.int32), jnp.reshape(2 * xi + yi, (1,)).astype(jnp.int32))
    small = {n: wts[n] for n in SMALL if n != "w_dw"}
    _, dx = _local_step(x[0], mem[0], loss_target[0], fetch, small, reduce)
    reduce.advance((dx,))

    grad_w, delta, new_m, new_v = {}, {}, {}, {}
    marker = (dx,) + tuple(reduce.after())
    for n in UPDATE_ORDER:
        reduced = [reduce.collect((l, n), marker) for l in range(DEPTH)]
        outs = _adamw_layers("adamw_" + n, _to_rows(n, wts[n]), reduced, _to_rows(n, mom_m[n]), _to_rows(n, mom_v[n]),
                             reduce.chip, 256, transposed=n == "w_up", after=marker)
        grad_w[n], delta[n], new_m[n], new_v[n] = (_to_rows(n, o) for o in outs)
        marker = (outs[1],)

    own, slots = [None] * DEPTH, [None] * DEPTH
    for l in reversed(range(DEPTH)):
        mine, theirs = reduce.small_finish(l, marker)
        if l == 0:
            loss_own, loss_slots = mine[-1], theirs[-1]
        own[l], slots[l] = dict(zip(SMALL_ARRAYS, mine)), dict(zip(SMALL_ARRAYS, theirs))
    shard_cols = CW // N_DEV
    for l in range(DEPTH):
        own[l]["w_dw"] = lax.dynamic_slice_in_dim(own[l]["w_dw"], me * shard_cols, shard_cols, axis=1)
        slots[l]["w_dw"] = lax.dynamic_slice_in_dim(slots[l]["w_dw"], me * shard_cols, shard_cols, axis=2)
    *small_out, loss_tile = _adamw_small(wts, mom_m, mom_v, own, slots, loss_own, loss_slots, dev)
    for dst, src in zip((grad_w, delta, new_m, new_v), small_out):
        dst.update(src)

    return (loss_tile[0, 0], dx[None], *[grad_w[n] for n in WEIGHTS], *[delta[n] for n in WEIGHTS],
            *[new_m[n] for n in WEIGHTS], *[new_v[n] for n in WEIGHTS])
```

```python
import functools

import jax
import jax.numpy as jnp
from jax import lax
from jax.experimental import pallas as pl
from jax.experimental.pallas import tpu as pltpu

F32 = jnp.float32
BF16 = jnp.bfloat16

D = 2048
GW = 1024
PW = 512
CW = 512
HD = 128
NH = 8
NG = 4
POOL_WINDOWS = (2, 4, 8, 16)
CONV_K = 31
IN_COLS = 2 * GW + PW + 2 * CW
DFF = 4 * D
XH = 4
XHD = D // XH
ATT_SCALE = XHD ** -0.5
RMS_EPS = 1e-6
LN_EPS = 1e-5
DEPTH = 2
N_DEV = 8

ADAM_LR = 0.001
ADAM_B1 = 0.9
ADAM_B2 = 0.999
ADAM_EPS = 1e-08
ADAM_WD = 0.01
ADAM_STEP = 10

LANES = 128
CONV_HALO = 32
POOL_HALO = 16
ROW_TILE = 128
VMEM_LIMIT = 60 * 1024 * 1024

MESH = pl.DeviceIdType.MESH
NT = (((1,), (1,)), ((), ()))
NN = (((1,), (0,)), ((), ()))
TN = (((0,), (0,)), ((), ()))

BIG = ("w_out", "w_q", "w_k", "w_v", "w_o", "w_up", "w_down", "w_in")
UPDATE_ORDER = ("w_down", "w_up", "w_o", "w_q", "w_k", "w_v", "w_out", "w_in")
GATHER_GROUPS = (("in", ("w_in",)), ("out", ("w_out",)), ("att", ("w_q", "w_k", "w_v", "w_o")), ("up", ("w_up",)),
                 ("down", ("w_down",)))
SMALL = ("norm_mix_pre", "norm_mix_post", "gmlp_v_gain", "w_spatial", "b_spatial", "w_pool", "s_pool",
         "w_dw", "b_dw", "conv_ln_g", "conv_ln_b", "norm_xattn_pre", "norm_mem", "norm_xattn_post",
         "norm_ffn_pre", "norm_ffn_post")
WEIGHTS = ("norm_mix_pre", "norm_mix_post", "w_in", "w_out", "gmlp_v_gain", "w_spatial", "b_spatial", "w_pool",
           "s_pool", "w_dw", "b_dw", "conv_ln_g", "conv_ln_b", "norm_xattn_pre", "norm_mem", "norm_xattn_post",
           "w_q", "w_k", "w_v", "w_o", "norm_ffn_pre", "norm_ffn_post", "w_up", "w_down")


def _cparams():
    return pltpu.CompilerParams(vmem_limit_bytes=VMEM_LIMIT)


def _dot(a, b, dims):
    return lax.dot_general(a, b, dims, preferred_element_type=F32)


def _rms(x, g):
    y = x * lax.rsqrt(jnp.mean(x * x, axis=-1, keepdims=True) + RMS_EPS)
    return y * g


def _rms_bwd(x, g, dy):
    r = lax.rsqrt(jnp.mean(x * x, axis=-1, keepdims=True) + RMS_EPS)
    xh = x * r
    t = dy * g
    dx = r * (t - xh * jnp.mean(t * xh, axis=-1, keepdims=True))
    return dx, jnp.sum(dy * xh, axis=0, keepdims=True)


def _gelu(x):
    cdf = 0.5 * (1.0 + jnp.tanh(0.7978845608028654 * (x + 0.044715 * (x * x * x))))
    return x * cdf


def _layer_norm(x, g, b=None):
    mu = jnp.mean(x, axis=-1, keepdims=True)
    xc = x - mu
    var = jnp.mean(xc * xc, axis=-1, keepdims=True)
    y = xc * lax.rsqrt(var + LN_EPS) * g
    return y if b is None else y + b


def _sigmoid(x):
    return 1.0 / (1.0 + jnp.exp(-x))


def _gmlp_rows(zu, zv, gv):
    return _gelu(zu), _layer_norm(_gelu(zv), gv)


def _glu(cv, cg):
    return cv * _sigmoid(cg)


def _ln_silu(h, g, b):
    y = _layer_norm(h, g, b)
    return y * _sigmoid(y)


ANY = pl.BlockSpec(memory_space=pl.ANY)


ROWS_TILE = 256
COLS_TILE = 512
DW_TILE = 512
RESIDENT_K = 2048
STREAM_K_TILE = 1024
STREAM_ROWS = 512


def _k_tiles(kdim):
    if kdim <= RESIDENT_K:
        return ROWS_TILE, kdim
    return STREAM_ROWS, max(t for t in range(LANES, STREAM_K_TILE + 1, LANES) if kdim % t == 0)


def _rowop_mm(name, kind, rows, g, w, dims, out_dtype, u=None, after=()):
    s = rows[0].shape[0]
    n = w.shape[0] if dims == NT else w.shape[1]
    tm, tn = min(ROWS_TILE, s), min(COLS_TILE, n)
    ni = s // tm
    bwd = kind == "rms_bwd"

    def rows_body(*refs):
        refs = list(refs)
        row_refs = [refs.pop(0) for _ in rows]
        g_ref = refs.pop(0)
        del refs[:len(after)]
        if bwd:
            a, dg = _rms_bwd(row_refs[0][...], g_ref[...], row_refs[1][...])
            refs[1][0] = dg
        else:
            a = _rms(row_refs[0][...], g_ref[...])
        refs[0][...] = a.astype(BF16)

    row_spec = pl.BlockSpec((tm, D), lambda i: (i, 0))
    res = pl.pallas_call(
        rows_body, name=name + "_rows", grid=(ni,),
        in_specs=[row_spec] * len(rows) + [pl.BlockSpec((1, D), lambda i: (0, 0))] + [ANY] * len(after),
        out_specs=[row_spec] + ([pl.BlockSpec((1, 1, D), lambda i: (i, 0, 0))] if bwd else []),
        out_shape=[jax.ShapeDtypeStruct((s, D), BF16)] + ([jax.ShapeDtypeStruct((ni, 1, D), F32)] if bwd else []),
        compiler_params=_cparams(),
    )(*rows, g, *after)
    a = res[0]

    def body(a_ref, w_ref, *rest):
        acc = _dot(a_ref[...], w_ref[...], dims)
        if u is not None:
            acc = acc * (2.0 * jnp.maximum(rest[0][...], 0.0))
        rest[-1][...] = acc.astype(out_dtype)

    w_spec = pl.BlockSpec((tn, D), lambda j: (j, 0)) if dims == NT else pl.BlockSpec((D, tn), lambda j: (0, j))
    tile = pl.BlockSpec((s, tn), lambda j: (0, j))
    out = pl.pallas_call(
        body, name=name, grid=(n // tn,),
        in_specs=[pl.BlockSpec((s, D), lambda j: (0, 0)), w_spec] + ([tile] if u is not None else []),
        out_specs=tile, out_shape=jax.ShapeDtypeStruct((s, n), out_dtype), compiler_params=_cparams(),
    )(a, w, *([u] if u is not None else []))
    return (out, *res)


def _mm_rowop(name, kind, pairs, rows, g, relu2=False, after=()):
    s, kdim = pairs[0][0].shape
    tm, tk = _k_tiles(kdim)
    tm = min(tm, s)
    ni, nk = s // tm, kdim // tk
    npair = len(pairs)

    def body(*refs):
        refs = list(refs)
        a_refs = [refs.pop(0) for _ in range(npair)]
        w_refs = [refs.pop(0) for _ in range(npair)]
        row_refs = [refs.pop(0) for _ in rows]
        g_ref = refs.pop(0)
        del refs[:len(after)]
        acc = refs.pop()
        outs = refs
        k = pl.program_id(1)

        @pl.when(k == 0)
        def _():
            acc[...] = jnp.zeros_like(acc)

        for a_ref, w_ref, (_, _, dims) in zip(a_refs, w_refs, pairs):
            a = a_ref[...]
            if relu2:
                a = jnp.square(jnp.maximum(a, 0.0))
            acc[...] += _dot(a.astype(BF16), w_ref[...], dims)

        @pl.when(k == nk - 1)
        def _():
            h = acc[...]
            if kind == "rms_res":
                outs[0][...] = row_refs[0][...] + _rms(h, g_ref[...])
                outs[1][...] = h
            else:
                dx, dg = _rms_bwd(row_refs[0][...], g_ref[...], h)
                if kind == "rms_bwd_res":
                    outs[0][...] = row_refs[1][...] + dx
                    outs[1][0] = dg
                else:
                    outs[0][0] = dg

    row_spec = pl.BlockSpec((tm, D), lambda i, k: (i, 0))
    dg_shape = jax.ShapeDtypeStruct((ni, 1, D), F32)
    dg_spec = pl.BlockSpec((1, 1, D), lambda i, k: (i, 0, 0))
    in_specs = [pl.BlockSpec((tm, tk), lambda i, k: (i, k))] * npair
    for _, _, dims in pairs:
        in_specs.append(pl.BlockSpec((tk, D), lambda i, k: (k, 0)) if dims == NN
                        else pl.BlockSpec((D, tk), lambda i, k: (0, k)))
    in_specs += [row_spec] * len(rows) + [pl.BlockSpec((1, D), lambda i, k: (0, 0))] + [ANY] * len(after)
    if kind == "rms_res":
        out_shape = [jax.ShapeDtypeStruct((s, D), F32)] * 2
        out_specs = [row_spec, row_spec]
    elif kind == "rms_bwd_res":
        out_shape = [jax.ShapeDtypeStruct((s, D), F32), dg_shape]
        out_specs = [row_spec, dg_spec]
    else:
        out_shape = [dg_shape]
        out_specs = [dg_spec]
    return pl.pallas_call(
        body, name=name, grid=(ni, nk), in_specs=in_specs, out_specs=out_specs, out_shape=out_shape,
        scratch_shapes=[pltpu.VMEM((tm, D), F32)], compiler_params=_cparams(),
    )(*[p[0] for p in pairs], *[p[1] for p in pairs], *rows, g, *after)


def _mm_tn(name, a, gmat, relu2=False, after=()):
    s, m = a.shape
    tm, ts = min(DW_TILE, m), s
    ni, ns = m // tm, s // ts

    def body(a_ref, g_ref, *rest):
        o_ref, acc = rest[len(after):]
        k = pl.program_id(1)

        @pl.when(k == 0)
        def _():
            acc[...] = jnp.zeros_like(acc)

        av = a_ref[...]
        if relu2:
            av = jnp.square(jnp.maximum(av, 0.0))
        acc[...] += _dot(av.astype(BF16), g_ref[...], TN)

        @pl.when(k == ns - 1)
        def _():
            o_ref[...] = acc[...].astype(BF16)

    return pl.pallas_call(
        body, name=name, grid=(ni, ns),
        in_specs=[pl.BlockSpec((ts, tm), lambda i, k: (k, i)), pl.BlockSpec((ts, D), lambda i, k: (k, 0))]
        + [ANY] * len(after),
        out_specs=pl.BlockSpec((tm, D), lambda i, k: (i, 0)),
        out_shape=jax.ShapeDtypeStruct((m, D), BF16),
        scratch_shapes=[pltpu.VMEM((tm, D), F32)], compiler_params=_cparams(),
    )(a, gmat, *after)


def _tril():
    r = lax.broadcasted_iota(jnp.int32, (HD, HD), 0)
    c = lax.broadcasted_iota(jnp.int32, (HD, HD), 1)
    return (c <= r).astype(F32)


def _gmlp_fwd(z, gv, ws, bst, tb):
    s = z.shape[0]
    tb = min(tb, s)

    def body(zu_ref, zv_ref, gv_ref, ws_ref, bst_ref, y_ref):
        tril = _tril()
        for h in range(NH):
            cols = slice(h * HD, (h + 1) * HD)
            u, vln = _gmlp_rows(zu_ref[:, cols], zv_ref[:, cols], gv_ref[h:h + 1, :])
            wm = (ws_ref[h] * tril).astype(BF16)
            vb = vln.astype(BF16)
            for c in range(tb // HD):
                rws = slice(c * HD, (c + 1) * HD)
                mixed = _dot(wm, vb[rws], NN) + bst_ref[:, h:h + 1]
                y_ref[rws, cols] = (u[rws] * mixed).astype(BF16)

    return pl.pallas_call(
        body, name="gmlp_fwd", grid=(s // tb,),
        in_specs=[pl.BlockSpec((tb, GW), lambda i: (i, 0)), pl.BlockSpec((tb, GW), lambda i: (i, 1)),
                  pl.BlockSpec((NH, HD), lambda i: (0, 0)), pl.BlockSpec((NH, HD, HD), lambda i: (0, 0, 0)),
                  pl.BlockSpec((HD, NH), lambda i: (0, 0))],
        out_specs=pl.BlockSpec((tb, GW), lambda i: (i, 0)),
        out_shape=jax.ShapeDtypeStruct((s, D), BF16), compiler_params=_cparams(),
    )(z, z, gv, ws, bst)


def _gmlp_bwd(z, dy, gv, ws, bst, tb):
    s = z.shape[0]
    tb = min(tb, s)
    nb = s // tb

    def body(zu_ref, zv_ref, dy_ref, gv_ref, ws_ref, bst_ref, dz_ref, dgv_ref, dws_ref, db_ref):
        tril = _tril()
        for h in range(NH):
            cols = slice(h * HD, (h + 1) * HD)
            (u, vln), vjp = jax.vjp(_gmlp_rows, zu_ref[:, cols], zv_ref[:, cols], gv_ref[h:h + 1, :])
            wmf = ws_ref[h] * tril
            wm = wmf.astype(BF16)
            wmt = wmf.T.astype(BF16)
            vb = vln.astype(BF16)
            dws = jnp.zeros((HD, HD), F32)
            db = jnp.zeros((HD, 1), F32)
            du_parts, dvln_parts = [], []
            for c in range(tb // HD):
                rws = slice(c * HD, (c + 1) * HD)
                mixed = _dot(wm, vb[rws], NN) + bst_ref[:, h:h + 1]
                dyc = dy_ref[rws, cols]
                du_parts.append(dyc * mixed)
                dmixed = dyc * u[rws]
                dmb = dmixed.astype(BF16)
                dws = dws + _dot(dmb, vb[rws], NT)
                db = db + jnp.sum(dmixed, axis=1, keepdims=True)
                dvln_parts.append(_dot(wmt, dmb, NN))
            du = jnp.concatenate(du_parts, axis=0)
            dvln = jnp.concatenate(dvln_parts, axis=0)
            dzu, dzv, dgv = vjp((du, dvln))
            dz_ref[:, cols] = dzu.astype(BF16)
            dz_ref[:, slice(GW + h * HD, GW + (h + 1) * HD)] = dzv.astype(BF16)
            dgv_ref[0, h:h + 1, :] = dgv
            dws_ref[0, h] = dws * tril
            db_ref[0, h] = jnp.broadcast_to(db, (HD, LANES))

    blk = pl.BlockSpec((tb, GW), lambda i: (i, 0))
    return pl.pallas_call(
        body, name="gmlp_bwd", grid=(nb,),
        in_specs=[blk, pl.BlockSpec((tb, GW), lambda i: (i, 1)), blk,
                  pl.BlockSpec((NH, HD), lambda i: (0, 0)), pl.BlockSpec((NH, HD, HD), lambda i: (0, 0, 0)),
                  pl.BlockSpec((HD, NH), lambda i: (0, 0))],
        out_specs=[pl.BlockSpec((tb, 2 * GW), lambda i: (i, 0)), pl.BlockSpec((1, NH, HD), lambda i: (i, 0, 0)),
                   pl.BlockSpec((1, NH, HD, HD), lambda i: (i, 0, 0, 0)),
                   pl.BlockSpec((1, NH, HD, LANES), lambda i: (i, 0, 0, 0))],
        out_shape=[jax.ShapeDtypeStruct((s, IN_COLS), BF16),
                   jax.ShapeDtypeStruct((nb, NH, HD), F32), jax.ShapeDtypeStruct((nb, NH, HD, HD), F32),
                   jax.ShapeDtypeStruct((nb, NH, HD, LANES), F32)],
        compiler_params=_cparams(),
    )(z, z, dy, gv, ws, bst)


def _pool_count(t0, window):
    pos = (t0 + lax.broadcasted_iota(jnp.int32, (ROW_TILE, LANES), 0)).astype(F32)
    return jnp.minimum(pos + 1.0, float(window))


def _window_sum(win, levels, back):
    n = win.shape[0]
    for lv in range(levels):
        step = 1 << lv
        win = win + pltpu.roll(win, n - step if back else step, 0)
    return win


def _pool_pooled(ppad_ref, t0, g):
    win = ppad_ref[pl.ds(t0, ROW_TILE + POOL_HALO), :]
    wsum = _window_sum(win, g + 1, False)[POOL_HALO:]
    return wsum / _pool_count(t0, POOL_WINDOWS[g]) - win[POOL_HALO:]


def _pool_fwd(z, wp, sp, y):
    s = z.shape[0]
    nt = s // ROW_TILE

    def body(p_ref, wp_ref, sp_ref, _, y_ref, ppad):
        for g in range(NG):
            cols = slice(g * LANES, (g + 1) * LANES)
            ppad[pl.ds(0, POOL_HALO), :] = jnp.zeros((POOL_HALO, LANES), F32)
            ppad[pl.ds(POOL_HALO, s), :] = p_ref[:, cols]
            wpb = wp_ref[g].astype(BF16)
            scale = sp_ref[:, cols]

            def tile(t, carry):
                t0 = pl.multiple_of(t * ROW_TILE, ROW_TILE)
                pooled = _pool_pooled(ppad, t0, g)
                y_ref[pl.ds(t0, ROW_TILE), cols] = (_dot(pooled.astype(BF16), wpb, NN) * scale).astype(BF16)
                return carry

            lax.fori_loop(0, nt, tile, 0)

    return pl.pallas_call(
        body, name="pool_fwd", grid=(1,),
        in_specs=[pl.BlockSpec((s, PW), lambda i: (0, 2 * GW // PW)),
                  pl.BlockSpec((NG, LANES, LANES), lambda i: (0, 0, 0)), pl.BlockSpec((1, PW), lambda i: (0, 0)), ANY],
        out_specs=pl.BlockSpec((s, PW), lambda i: (0, GW // PW)),
        out_shape=jax.ShapeDtypeStruct((s, D), BF16), input_output_aliases={3: 0},
        scratch_shapes=[pltpu.VMEM((s + POOL_HALO, LANES), F32)], compiler_params=_cparams(),
    )(z, wp, sp, y)


def _pool_bwd(z, dy, wp, sp, dz):
    s = z.shape[0]
    nt = s // ROW_TILE

    def body(p_ref, dy_ref, wp_ref, sp_ref, _, dp_ref, dwp_ref, dsp_ref, ppad, rpad, dpool):
        for g in range(NG):
            cols = slice(g * LANES, (g + 1) * LANES)
            ppad[pl.ds(0, POOL_HALO), :] = jnp.zeros((POOL_HALO, LANES), F32)
            ppad[pl.ds(POOL_HALO, s), :] = p_ref[:, cols]
            rpad[pl.ds(s, POOL_HALO), :] = jnp.zeros((POOL_HALO, LANES), F32)
            wpb = wp_ref[g].astype(BF16)
            scale = sp_ref[:, cols]

            def tile(t, carry):
                dwp, dsp = carry
                t0 = pl.multiple_of(t * ROW_TILE, ROW_TILE)
                pooled = _pool_pooled(ppad, t0, g)
                pb = pooled.astype(BF16)
                dyt = dy_ref[pl.ds(t0, ROW_TILE), cols]
                dsp = dsp + jnp.sum(dyt * _dot(pb, wpb, NN), axis=0, keepdims=True)
                dmm = (dyt * scale).astype(BF16)
                dwp = dwp + _dot(pb, dmm, TN)
                dpooled = _dot(dmm, wpb, NT)
                rpad[pl.ds(t0, ROW_TILE), :] = dpooled / _pool_count(t0, POOL_WINDOWS[g])
                dpool[pl.ds(t0, ROW_TILE), :] = dpooled
                return dwp, dsp

            dwp, dsp = lax.fori_loop(0, nt, tile, (jnp.zeros((LANES, LANES), F32), jnp.zeros((1, LANES), F32)))
            dwp_ref[g] = dwp
            dsp_ref[:, cols] = dsp

            def tile2(t, carry):
                t0 = pl.multiple_of(t * ROW_TILE, ROW_TILE)
                win = rpad[pl.ds(t0, ROW_TILE + POOL_HALO), :]
                back = _window_sum(win, g + 1, True)[:ROW_TILE]
                rows = pl.ds(t0, ROW_TILE)
                dp_ref[rows, cols] = (back - dpool[rows, :]).astype(BF16)
                return carry

            lax.fori_loop(0, nt, tile2, 0)

    return pl.pallas_call(
        body, name="pool_bwd", grid=(1,),
        in_specs=[pl.BlockSpec((s, PW), lambda i: (0, 2 * GW // PW)), pl.BlockSpec((s, PW), lambda i: (0, GW // PW)),
                  pl.BlockSpec((NG, LANES, LANES), lambda i: (0, 0, 0)), pl.BlockSpec((1, PW), lambda i: (0, 0)), ANY],
        out_specs=[pl.BlockSpec((s, PW), lambda i: (0, 2 * GW // PW)),
                   pl.BlockSpec((NG, LANES, LANES), lambda i: (0, 0, 0)), pl.BlockSpec((1, PW), lambda i: (0, 0))],
        out_shape=[jax.ShapeDtypeStruct((s, IN_COLS), BF16), jax.ShapeDtypeStruct((NG, LANES, LANES), F32),
                   jax.ShapeDtypeStruct((1, PW), F32)],
        input_output_aliases={4: 0},
        scratch_shapes=[pltpu.VMEM((s + POOL_HALO, LANES), F32), pltpu.VMEM((s + POOL_HALO, LANES), F32),
                        pltpu.VMEM((s, LANES), F32)],
        compiler_params=_cparams(),
    )(z, dy, wp, sp, dz)


CONV_LEAD = CONV_HALO - (CONV_K - 1)


SUBLANES = 8


def _sublane_shifts(win):
    n = win.shape[0]
    return [win] + [pltpu.roll(win, n - b, 0) for b in range(1, SUBLANES)]


def _shifted(shifts, offset):
    a, b = divmod(offset, SUBLANES)
    return shifts[b][a * SUBLANES:a * SUBLANES + ROW_TILE]


def _conv_taps(shifts, wdw_ref, lead, reverse):
    acc = jnp.zeros((ROW_TILE, CW), F32)
    for j in range(CONV_K):
        tap = (CONV_K - 1 - j) if reverse else j
        acc = acc + wdw_ref[tap:tap + 1, :] * _shifted(shifts, lead + j)
    return acc


def _conv_fill_glu(cv_ref, cg_ref, xpad, s):
    xpad[pl.ds(0, CONV_HALO), :] = jnp.zeros((CONV_HALO, CW), F32)

    def fill(t, carry):
        t0 = pl.multiple_of(t * ROW_TILE, ROW_TILE)
        rows = pl.ds(t0, ROW_TILE)
        xpad[pl.ds(t0 + CONV_HALO, ROW_TILE), :] = _glu(cv_ref[rows, :], cg_ref[rows, :])
        return carry

    lax.fori_loop(0, s // ROW_TILE, fill, 0)


def _conv_fwd(z, wdw, bdw, lng, lnb, y):
    s = z.shape[0]

    def body(cv_ref, cg_ref, wdw_ref, bdw_ref, lng_ref, lnb_ref, _, y_ref, xpad):
        _conv_fill_glu(cv_ref, cg_ref, xpad, s)

        def tile(t, carry):
            t0 = pl.multiple_of(t * ROW_TILE, ROW_TILE)
            shifts = _sublane_shifts(xpad[pl.ds(t0, ROW_TILE + CONV_HALO), :])
            hc = _conv_taps(shifts, wdw_ref, CONV_LEAD, False) + bdw_ref[...]
            y_ref[pl.ds(t0, ROW_TILE), :] = _ln_silu(hc, lng_ref[...], lnb_ref[...]).astype(BF16)
            return carry

        lax.fori_loop(0, s // ROW_TILE, tile, 0)

    vec = pl.BlockSpec((1, CW), lambda i: (0, 0))
    return pl.pallas_call(
        body, name="conv_fwd", grid=(1,),
        in_specs=[pl.BlockSpec((s, CW), lambda i: (0, (2 * GW + PW) // CW)),
                  pl.BlockSpec((s, CW), lambda i: (0, (2 * GW + PW) // CW + 1)),
                  pl.BlockSpec((CONV_K + 1, CW), lambda i: (0, 0)), vec, vec, vec, ANY],
        out_specs=pl.BlockSpec((s, CW), lambda i: (0, (GW + PW) // CW)),
        out_shape=jax.ShapeDtypeStruct((s, D), BF16), input_output_aliases={6: 0},
        scratch_shapes=[pltpu.VMEM((s + CONV_HALO, CW), F32)], compiler_params=_cparams(),
    )(z, z, wdw, bdw, lng, lnb, y)


def _conv_bwd(z, dy, wdw, bdw, lng, lnb, dz):
    s = z.shape[0]

    def body(cv_ref, cg_ref, dy_ref, wdw_ref, bdw_ref, lng_ref, lnb_ref, _,
             dz_ref, dwdw_ref, dbdw_ref, dlng_ref, dlnb_ref, xpad, dpad, dcg_keep):
        @pl.when(pl.program_id(0) == 0)
        def _():
            compute(cv_ref, cg_ref, dy_ref, wdw_ref, bdw_ref, lng_ref, lnb_ref,
                    dz_ref, dcg_keep, dwdw_ref, dbdw_ref, dlng_ref, dlnb_ref, xpad, dpad)

        @pl.when(pl.program_id(0) == 1)
        def _():
            dz_ref[...] = dcg_keep[...]

    def compute(cv_ref, cg_ref, dy_ref, wdw_ref, bdw_ref, lng_ref, lnb_ref,
                dcv_ref, dcg_ref, dwdw_ref, dbdw_ref, dlng_ref, dlnb_ref, xpad, dpad):
        _conv_fill_glu(cv_ref, cg_ref, xpad, s)
        dpad[pl.ds(s, CONV_HALO), :] = jnp.zeros((CONV_HALO, CW), F32)
        dwdw_ref[...] = jnp.zeros((CONV_K + 1, CW), F32)

        def tile(t, carry):
            db, dg, dbeta = carry
            t0 = pl.multiple_of(t * ROW_TILE, ROW_TILE)
            shifts = _sublane_shifts(xpad[pl.ds(t0, ROW_TILE + CONV_HALO), :])
            hc = _conv_taps(shifts, wdw_ref, CONV_LEAD, False) + bdw_ref[...]
            _, vjp = jax.vjp(_ln_silu, hc, lng_ref[...], lnb_ref[...])
            dhc, dg_t, dbeta_t = vjp(dy_ref[pl.ds(t0, ROW_TILE), :])
            dpad[pl.ds(t0, ROW_TILE), :] = dhc
            for j in range(CONV_K):
                dwdw_ref[j:j + 1, :] += jnp.sum(dhc * _shifted(shifts, CONV_LEAD + j), axis=0, keepdims=True)
            return db + jnp.sum(dhc, axis=0, keepdims=True), dg + dg_t, dbeta + dbeta_t

        zero = jnp.zeros((1, CW), F32)
        db, dg, dbeta = lax.fori_loop(0, s // ROW_TILE, tile, (zero, zero, zero))
        dbdw_ref[...] = db
        dlng_ref[...] = dg
        dlnb_ref[...] = dbeta

        def tile2(t, carry):
            t0 = pl.multiple_of(t * ROW_TILE, ROW_TILE)
            rows = pl.ds(t0, ROW_TILE)
            dglu = _conv_taps(_sublane_shifts(dpad[pl.ds(t0, ROW_TILE + CONV_HALO), :]), wdw_ref, 0, True)
            _, vjp = jax.vjp(_glu, cv_ref[rows, :], cg_ref[rows, :])
            dcv, dcg = vjp(dglu)
            dcv_ref[rows, :] = dcv.astype(BF16)
            dcg_ref[rows, :] = dcg.astype(BF16)
            return carry

        lax.fori_loop(0, s // ROW_TILE, tile2, 0)

    vec = pl.BlockSpec((1, CW), lambda i: (0, 0))
    wspec = pl.BlockSpec((CONV_K + 1, CW), lambda i: (0, 0))
    vshape = jax.ShapeDtypeStruct((1, CW), F32)
    return pl.pallas_call(
        body, name="conv_bwd", grid=(2,),
        in_specs=[pl.BlockSpec((s, CW), lambda i: (0, (2 * GW + PW) // CW)),
                  pl.BlockSpec((s, CW), lambda i: (0, (2 * GW + PW) // CW + 1)),
                  pl.BlockSpec((s, CW), lambda i: (0, (GW + PW) // CW)), wspec, vec, vec, vec, ANY],
        out_specs=[pl.BlockSpec((s, CW), lambda i: (0, (2 * GW + PW) // CW + i)), wspec, vec, vec, vec],
        out_shape=[jax.ShapeDtypeStruct((s, IN_COLS), BF16), jax.ShapeDtypeStruct((CONV_K + 1, CW), F32),
                   vshape, vshape, vshape],
        input_output_aliases={7: 0},
        scratch_shapes=[pltpu.VMEM((s + CONV_HALO, CW), F32), pltpu.VMEM((s + CONV_HALO, CW), F32),
                        pltpu.VMEM((s, CW), BF16)],
        compiler_params=_cparams(),
    )(z, z, dy, wdw, bdw, lng, lnb, dz)


def _softmax_rows(sc):
    e = jnp.exp(sc - jnp.max(sc, axis=-1, keepdims=True))
    return e / jnp.sum(e, axis=-1, keepdims=True)


def _attn_fwd(q, k, v, tq):
    s, m = q.shape[0], k.shape[0]
    tq = min(tq, s)

    def body(q_ref, k_ref, v_ref, o_ref):
        for h in range(XH):
            cols = slice(h * XHD, (h + 1) * XHD)
            p = _softmax_rows(_dot(q_ref[:, cols], k_ref[:, cols], NT) * ATT_SCALE)
            o_ref[:, cols] = _dot(p.astype(BF16), v_ref[:, cols], NN).astype(BF16)

    kv = pl.BlockSpec((m, D), lambda i: (0, 0))
    return pl.pallas_call(
        body, name="attn_fwd", grid=(s // tq,),
        in_specs=[pl.BlockSpec((tq, D), lambda i: (i, 0)), kv, kv],
        out_specs=pl.BlockSpec((tq, D), lambda i: (i, 0)),
        out_shape=jax.ShapeDtypeStruct((s, D), BF16), compiler_params=_cparams(),
    )(q, k, v)


def _attn_bwd(q, k, v, do, tq):
    s, m = q.shape[0], k.shape[0]
    tq = min(tq, s)

    def body(q_ref, k_ref, v_ref, do_ref, dq_ref, dk_ref, dv_ref):
        @pl.when(pl.program_id(0) == 0)
        def _():
            dk_ref[...] = jnp.zeros_like(dk_ref)
            dv_ref[...] = jnp.zeros_like(dv_ref)

        for h in range(XH):
            cols = slice(h * XHD, (h + 1) * XHD)
            qh, kh, vh, doh = q_ref[:, cols], k_ref[:, cols], v_ref[:, cols], do_ref[:, cols]
            p = _softmax_rows(_dot(qh, kh, NT) * ATT_SCALE)
            dp = _dot(doh, vh, NT)
            dv_ref[:, cols] += _dot(p.astype(BF16), doh, TN)
            ds = (p * (dp - jnp.sum(p * dp, axis=-1, keepdims=True)) * ATT_SCALE).astype(BF16)
            dq_ref[:, cols] = _dot(ds, kh, NN).astype(BF16)
            dk_ref[:, cols] += _dot(ds, qh, TN)

    kv = pl.BlockSpec((m, D), lambda i: (0, 0))
    qs = pl.BlockSpec((tq, D), lambda i: (i, 0))
    return pl.pallas_call(
        body, name="attn_bwd", grid=(s // tq,),
        in_specs=[qs, kv, kv, qs], out_specs=[qs, kv, kv],
        out_shape=[jax.ShapeDtypeStruct((s, D), BF16), jax.ShapeDtypeStruct((m, D), F32),
                   jax.ShapeDtypeStruct((m, D), F32)],
        compiler_params=_cparams(),
    )(q, k, v, do)


def _loss_head(y, target, tm):
    s = y.shape[0]
    tm = min(tm, s)

    def body(y_ref, t_ref, dy_ref, part_ref):
        err = y_ref[...] - t_ref[...]
        dy_ref[...] = err * (1.0 / D)
        part_ref[...] = jnp.full((1, 8, LANES), 0.5 * jnp.sum(err * err) * (1.0 / D), F32)

    blk = pl.BlockSpec((tm, D), lambda i: (i, 0))
    return pl.pallas_call(
        body, name="loss_head", grid=(s // tm,), in_specs=[blk, blk],
        out_specs=[blk, pl.BlockSpec((1, 8, LANES), lambda i: (i, 0, 0))],
        out_shape=[jax.ShapeDtypeStruct((s, D), F32), jax.ShapeDtypeStruct((s // tm, 8, LANES), F32)],
        compiler_params=_cparams(),
    )(y, target)


def _layer_fwd(x0, mem, w, p, fetch):
    z, hn0 = _rowop_mm("mix_in", "rms", (x0,), p["norm_mix_pre"], w["w_in"], NT, F32)
    y = _gmlp_fwd(z, p["gmlp_v_gain"], p["w_spatial"], p["b_spatial_t"], 512)
    y = _pool_fwd(z, p["w_pool"], p["s_pool"], y)
    y = _conv_fwd(z, p["w_dw"], p["b_dw"], p["conv_ln_g"], p["conv_ln_b"], y)
    w.update(fetch("out", (y,)))
    x1, h0 = _mm_rowop("mix_out", "rms_res", [(y, w["w_out"], NN)], (x0,), p["norm_mix_post"])
    w.update(fetch("att", (x1,)))
    q, hn1 = _rowop_mm("att_q", "rms", (x1,), p["norm_xattn_pre"], w["w_q"], NN, BF16)
    k, mn = _rowop_mm("att_k", "rms", (mem,), p["norm_mem"], w["w_k"], NN, BF16)
    v, _ = _rowop_mm("att_v", "rms", (mem,), p["norm_mem"], w["w_v"], NN, BF16)
    o = _attn_fwd(q, k, v, 256)
    x2, h1 = _mm_rowop("att_o", "rms_res", [(o, w["w_o"], NN)], (x1,), p["norm_xattn_post"])
    w.update(fetch("up", (x2,)))
    u, hn2 = _rowop_mm("ffn_up", "rms", (x2,), p["norm_ffn_pre"], w["w_up"], NT, F32)
    w.update(fetch("down", (u,)))
    x3, h2 = _mm_rowop("ffn_down", "rms_res", [(u, w["w_down"], NN)], (x2,), p["norm_ffn_post"], relu2=True)
    saved = dict(x0=x0, z=z, hn0=hn0, y=y, h0=h0, x1=x1, q=q, hn1=hn1, k=k, v=v, mn=mn, o=o, h1=h1, x2=x2, u=u,
                 hn2=hn2, h2=h2)
    return x3, saved


def _layer_bwd(dx3, mem, w, p, sv, red):
    gs = {}
    du, dh2, dg = _rowop_mm("ffn_down_bwd", "rms_bwd", (sv["h2"], dx3), p["norm_ffn_post"], w["w_down"], NT, BF16,
                            u=sv["u"], after=red.after())
    gs["norm_ffn_post"] = jnp.sum(dg, axis=0)
    g_down = _mm_tn("ffn_down_dw", sv["u"], dh2, relu2=True)
    red.advance((g_down,))
    dx2, dg = _mm_rowop("ffn_up_bwd", "rms_bwd_res", [(du, w["w_up"], NN)], (sv["x2"], dx3), p["norm_ffn_pre"],
                        after=red.after())
    gs["norm_ffn_pre"] = jnp.sum(dg, axis=0)
    g_up = _mm_tn("ffn_up_dw", du, sv["hn2"])
    red.add("ffn", ("w_down", "w_up"), [g_down, g_up])
    do, dh1, dg = _rowop_mm("att_o_bwd", "rms_bwd", (sv["h1"], dx2), p["norm_xattn_post"], w["w_o"], NT, BF16,
                            after=red.after())
    gs["norm_xattn_post"] = jnp.sum(dg, axis=0)
    g_o = _mm_tn("att_o_dw", sv["o"], dh1)
    red.advance((g_o,))
    dq, dk, dv = _attn_bwd(sv["q"], sv["k"], sv["v"], do, 256)
    dk, dv = dk.astype(BF16), dv.astype(BF16)
    dx1, dg = _mm_rowop("att_q_bwd", "rms_bwd_res", [(dq, w["w_q"], NT)], (sv["x1"], dx2), p["norm_xattn_pre"],
                        after=red.after())
    gs["norm_xattn_pre"] = jnp.sum(dg, axis=0)
    g_q = _mm_tn("att_q_dw", sv["hn1"], dq)
    g_k = _mm_tn("att_k_dw", sv["mn"], dk)
    g_v = _mm_tn("att_v_dw", sv["mn"], dv)
    (dg,) = _mm_rowop("att_kv_bwd", "rms_bwd_gain", [(dk, w["w_k"], NT), (dv, w["w_v"], NT)], (mem,), p["norm_mem"])
    gs["norm_mem"] = jnp.sum(dg, axis=0)
    red.add("att", ("w_o", "w_q", "w_k", "w_v"), [g_o, g_q, g_k, g_v])
    dy, dh0, dg = _rowop_mm("mix_out_bwd", "rms_bwd", (sv["h0"], dx1), p["norm_mix_post"], w["w_out"], NT, F32,
                            after=red.after())
    gs["norm_mix_post"] = jnp.sum(dg, axis=0)
    g_out = _mm_tn("mix_out_dw", sv["y"], dh0)
    red.advance((g_out,))
    red.add("out", ("w_out",), [g_out])
    z = sv["z"]
    dz, dgv, dws, dbs = _gmlp_bwd(z, dy, p["gmlp_v_gain"], p["w_spatial"], p["b_spatial_t"], 512)
    gs["gmlp_v_gain"] = jnp.sum(dgv, axis=0)
    gs["w_spatial"] = jnp.sum(dws, axis=0)
    gs["b_spatial"] = jnp.sum(dbs[..., 0], axis=0)
    dz, gs["w_pool"], gs["s_pool"] = _pool_bwd(z, dy, p["w_pool"], p["s_pool"], dz)
    dz, dwdw, gs["b_dw"], gs["conv_ln_g"], gs["conv_ln_b"] = _conv_bwd(
        z, dy, p["w_dw"], p["b_dw"], p["conv_ln_g"], p["conv_ln_b"], dz)
    red.advance((dz,))
    red.small("mixer", _small_grad_arrays(gs, dwdw, norms=False))
    g_in = _mm_tn("mix_in_dw", dz, sv["hn0"], after=red.after())
    red.add("in", ("w_in",), [g_in])
    red.advance((g_in,))
    dx0, dg = _mm_rowop("mix_in_bwd", "rms_bwd_res", [(dz, w["w_in"], NN)], (sv["x0"], dx1), p["norm_mix_pre"],
                        after=red.after())
    gs["norm_mix_pre"] = jnp.sum(dg, axis=0)
    late = {"norms": jnp.concatenate([gs[n] for n in NORM_NAMES], axis=0)}
    if red.layer == 0:
        late["loss"] = red.extra[0]
    red.small("norms", late)
    return dx0


NORM_NAMES = ("norm_mix_pre", "norm_mix_post", "norm_xattn_pre", "norm_mem", "norm_xattn_post", "norm_ffn_pre",
              "norm_ffn_post")
VEC_NAMES = ("s_pool", "b_dw", "conv_ln_g", "conv_ln_b")
SMALL_ARRAYS = ("norms", "gain_bias", "w_spatial", "w_pool", "vecs", "w_dw")


def _small_grad_arrays(gs, dwdw, norms=True):
    out = {"norms": jnp.concatenate([gs[n] for n in NORM_NAMES], axis=0)} if norms else {}
    out.update({"gain_bias": jnp.concatenate([gs["gmlp_v_gain"], gs["b_spatial"]], axis=0),
                "w_spatial": gs["w_spatial"], "w_pool": gs["w_pool"],
                "vecs": jnp.concatenate([gs[n] for n in VEC_NAMES], axis=0), "w_dw": dwdw})
    return out


def _split_small_grads(arrays):
    out = {n: arrays["norms"][k] for k, n in enumerate(NORM_NAMES)}
    out.update({n: arrays["vecs"][k] for k, n in enumerate(VEC_NAMES)})
    out.update(gmlp_v_gain=arrays["gain_bias"][:NH], b_spatial=arrays["gain_bias"][NH:], w_spatial=arrays["w_spatial"],
               w_pool=arrays["w_pool"], w_dw=arrays["w_dw"][:CONV_K])
    return out


def _layer_params(small, l):
    p = {n: small[n][l].reshape(1, -1) for n in ("norm_mix_pre", "norm_mix_post", "s_pool", "b_dw", "conv_ln_g",
                                                   "conv_ln_b", "norm_xattn_pre", "norm_mem", "norm_xattn_post",
                                                   "norm_ffn_pre", "norm_ffn_post")}
    p["gmlp_v_gain"] = small["gmlp_v_gain"][l]
    p["w_spatial"] = small["w_spatial"][l]
    p["b_spatial_t"] = small["b_spatial"][l].T
    p["w_pool"] = small["w_pool"][l]
    p["w_dw"] = jnp.pad(small["w_dw"][l], ((0, 1), (0, 0)))
    return p


def _local_step(x, mem, target, fetch, small, red):
    small = dict(small)
    saved, weights, params = [], [], []
    h = x
    marker = ()
    for l in range(DEPTH):
        w = fetch(l, "in", marker)
        if "taps" in w:
            small["w_dw"] = w.pop("taps")
        p = _layer_params(small, l)
        h, sv = _layer_fwd(h, mem, w, p, functools.partial(fetch, l))
        marker = (h,)
        saved.append(sv)
        weights.append(w)
        params.append(p)
    dh, loss = _loss_head(h, target, 512)
    red.extra = (loss,)
    for l in reversed(range(DEPTH)):
        red.layer = l
        dh = _layer_bwd(dh, mem, weights[l], params[l], saved[l], red)
    return loss, dh


HBM = pl.BlockSpec(memory_space=pltpu.HBM)


def _position():
    return lax.axis_index("x"), lax.axis_index("y"), lax.axis_index("c")


SEM = pl.BlockSpec(memory_space=pltpu.SEMAPHORE)
EFFECT = pltpu.SideEffectType.DATAFLOW_SIDE_EFFECTING
TOKEN = jax.ShapeDtypeStruct((8, LANES), F32)
TOKEN_SPEC = pl.BlockSpec(memory_space=pltpu.VMEM)


def _landing(shape, dtype):
    return pltpu.with_memory_space_constraint(lax.empty(shape, dtype), pltpu.HBM)


def _hbm_shapes(arrays):
    return [pltpu.HBM(a.shape, a.dtype) for a in arrays]


def _block(ref, r, dev):
    return ref.at[pl.ds((4 * dev[0] + 2 * dev[1] + dev[2]) * r, r), :]


def _split_call(name, body, thru, sems_in, after, sems_out, token):
    n = len(thru)
    out_shape = [pltpu.SemaphoreType.DMA(s) for s in sems_out] + _hbm_shapes(thru) + ([TOKEN] if token else [])
    out_specs = [SEM] * len(sems_out) + [HBM] * n + ([TOKEN_SPEC] if token else [])
    return pl.pallas_call(
        body, name=name, in_specs=[HBM] * n + [SEM] * len(sems_in) + [ANY] * len(after),
        out_specs=out_specs, out_shape=out_shape,
        input_output_aliases={i: len(sems_out) + i for i in range(n)},
        compiler_params=pltpu.CompilerParams(has_side_effects=EFFECT),
    )(*thru, *sems_in, *after)


def _place_own(name, srcs, dev, out_dtype, tr):
    n = len(srcs)
    r, cols = srcs[0][0].shape[-2:]
    tr = r if r < 16 else _row_tile(r, tr)
    nb = r // tr

    def body(dev_ref, *refs):
        for a in range(n):
            refs[n + a][...] = refs[a][...].astype(out_dtype)

    in_specs = [pl.BlockSpec((tr, cols), lambda i, d: (i, 0)) if l is None
                else pl.BlockSpec((None, tr, cols), lambda i, d, l=l: (l, i, 0)) for _, l in srcs]
    return pl.pallas_call(
        body, name=name,
        grid_spec=pltpu.PrefetchScalarGridSpec(
            num_scalar_prefetch=1, grid=(nb,), in_specs=in_specs,
            out_specs=[pl.BlockSpec((tr, cols), lambda i, d: (d[0] * nb + i, 0))] * n),
        out_shape=[jax.ShapeDtypeStruct((N_DEV * r, cols), out_dtype)] * n, compiler_params=_cparams(),
    )(dev, *[a for a, _ in srcs])


def _gather_peers(x, y, c):
    return [(1 - x, y, c), (x, 1 - y, c), (1 - x, 1 - y, c), (x, y, 1 - c)]


def _block_rows(land):
    return land.shape[0] // N_DEV


def _near_peers(x, y, c):
    return [(1 - x, y, c), (x, 1 - y, c), (x, y, 1 - c)]


def _relay_route(x, y, c):
    origin = (x + c * (1 - 2 * x), y + (1 - c) * (1 - 2 * y), c)
    target = (x + (1 - c) * (1 - 2 * x), y + c * (1 - 2 * y), c)
    return origin, target


def _same_block_copy(blk, send_sem, recv_sem, to):
    return pltpu.make_async_remote_copy(src_ref=blk, dst_ref=blk, send_sem=send_sem, recv_sem=recv_sem, device_id=to,
                                        device_id_type=MESH)


def _gather_start(name, lands, after):
    n = len(lands)

    def body(*refs):
        lz = refs[:n]
        send_sems, recv_sems = refs[n + len(after)], refs[n + len(after) + 1]
        token = refs[-1]
        x, y, c = _position()
        for a in range(n):
            own = _block(lz[a], _block_rows(lands[a]), (x, y, c))
            for k, to in enumerate(_near_peers(x, y, c)):
                _same_block_copy(own, send_sems.at[k], recv_sems.at[k], to).start()
        token[...] = jnp.zeros_like(token)

    out = _split_call(name, body, list(lands), [], after, [(3,), (3,)], True)
    return out[0], out[1], out[2:2 + n], out[-1]


def _gather_step(name, near, far, fresh, after):
    groups = [g for g in (near and near[0], far and far[0], fresh) if g]
    counts = [len(near[0]) if near else 0, len(far[0]) if far else 0, len(fresh) if fresh else 0]
    n = sum(counts)
    sems_in = ([near[1]] if near else []) + ([far[1]] if far else [])
    sems_out = ([(2,), (2,), (1,), (1,)] if near else []) + ([(1,), (1,)] if far else []) + ([(3,), (3,)] if fresh else [])

    def body(*refs):
        lz = list(refs[:n])
        ins = list(refs[n:n + len(sems_in)])
        outs = list(refs[n + len(sems_in) + len(after):n + len(sems_in) + len(after) + len(sems_out)])
        token = refs[-1]
        x, y, c = _position()
        me, sibling = (x, y, c), (x, y, 1 - c)
        near_lz, far_lz, fresh_lz = (lz[sum(counts[:i]):sum(counts[:i + 1])] for i in range(3))
        neighbours = _near_peers(x, y, c)[:2]
        origin, target = _relay_route(x, y, c)
        diagonal = (1 - x, 1 - y, c)
        if near:
            recv0 = ins.pop(0)
            fsend, frecv, rsend, rrecv = (outs.pop(0) for _ in range(4))
            for a, land in enumerate(near[0]):
                for j, chip in enumerate(neighbours):
                    _same_block_copy(_block(near_lz[a], _block_rows(land), chip), fsend.at[j], recv0.at[j], me).wait_recv()
        if far:
            rrecv_in = ins.pop(0)
            f2send, f2recv = outs.pop(0), outs.pop(0)
            for a, land in enumerate(far[0]):
                _same_block_copy(_block(far_lz[a], _block_rows(land), diagonal), f2send.at[0], rrecv_in.at[0], me).wait_recv()
            for a, land in enumerate(far[0]):
                _same_block_copy(_block(far_lz[a], _block_rows(land), diagonal), f2send.at[0], f2recv.at[0], sibling).start()
        if near:
            for a, land in enumerate(near[0]):
                r = _block_rows(land)
                _same_block_copy(_block(near_lz[a], r, origin), rsend.at[0], rrecv.at[0], target).start()
                for j, chip in enumerate(neighbours):
                    _same_block_copy(_block(near_lz[a], r, chip), fsend.at[j], frecv.at[j], sibling).start()
        if fresh:
            send_sems, recv_sems = outs.pop(0), outs.pop(0)
            for a, land in enumerate(fresh):
                own = _block(fresh_lz[a], _block_rows(land), me)
                for k, to in enumerate(_near_peers(x, y, c)):
                    _same_block_copy(own, send_sems.at[k], recv_sems.at[k], to).start()
        token[...] = jnp.zeros_like(token)

    out = list(_split_call(name, body, [l for g in groups for l in g], sems_in, after, sems_out, True))
    res = {"token": out.pop()}
    if near:
        res.update(fsend=out.pop(0), frecv=out.pop(0), rsend=out.pop(0), rrecv=out.pop(0))
    if far:
        res.update(f2send=out.pop(0), f2recv=out.pop(0))
    if fresh:
        res.update(send=out.pop(0), recv=out.pop(0))
    res["near"], res["far"], res["fresh"] = (out[sum(counts[:i]):sum(counts[:i + 1])] for i in range(3))
    return res


def _gather_finish(name, lands, send_sems, recv_sems, fsend, frecv, rsend, f2send, f2recv, after):
    n = len(lands)

    def body(*refs):
        lz = refs[:n]
        send0, recv0, fsend_ref, frecv_ref, rsend_ref, f2send_ref, f2recv_ref = refs[n:n + 7]
        x, y, c = _position()
        me = (x, y, c)
        near = _near_peers(x, y, c)[:2]
        origin, _ = _relay_route(x, y, c)
        for a in range(n):
            r = _block_rows(lands[a])
            sib = _block(lz[a], r, (x, y, 1 - c))
            _same_block_copy(sib, send0.at[2], recv0.at[2], me).wait_recv()
            for j, chip in enumerate(near):
                blk = _block(lz[a], r, (chip[0], chip[1], 1 - c))
                _same_block_copy(blk, fsend_ref.at[j], frecv_ref.at[j], me).wait_recv()
            far = _block(lz[a], r, (1 - x, 1 - y, 1 - c))
            _same_block_copy(far, f2send_ref.at[0], f2recv_ref.at[0], me).wait_recv()
            own = _block(lz[a], r, me)
            for k in range(3):
                _same_block_copy(own, send0.at[k], recv0.at[k], me).wait_send()
            for j, chip in enumerate(near):
                _same_block_copy(_block(lz[a], r, chip), fsend_ref.at[j], frecv_ref.at[j], me).wait_send()
            _same_block_copy(_block(lz[a], r, origin), rsend_ref.at[0], recv0.at[0], me).wait_send()
            _same_block_copy(_block(lz[a], r, (1 - x, 1 - y, c)), f2send_ref.at[0], f2recv_ref.at[0], me).wait_send()

    return _split_call(name, body, list(lands), [send_sems, recv_sems, fsend, frecv, rsend, f2send, f2recv], after, [],
                       False)


def _sibling_start(name, grads, after):
    n = len(grads)
    lands = [_landing((4, g.shape[0] // N_DEV, D), g.dtype) for g in grads]

    def body(*refs):
        ins, lz = refs[:n], refs[n:2 * n]
        send_sem, recv_sem = refs[2 * n + len(after)], refs[2 * n + len(after) + 1]
        token = refs[-1]
        x, y, c = _position()
        for a in range(n):
            r = grads[a].shape[0] // N_DEV
            for q in range(4):
                pltpu.make_async_remote_copy(
                    src_ref=ins[a].at[pl.ds((2 * q + 1 - c) * r, r), :], dst_ref=lz[a].at[q], send_sem=send_sem.at[0],
                    recv_sem=recv_sem.at[0], device_id=(x, y, 1 - c), device_id_type=MESH).start()
        token[...] = jnp.zeros_like(token)

    out = _split_call(name, body, list(grads) + lands, [], after, [(1,), (1,)], True)
    return out[0], out[1], out[2:2 + n], out[2 + n:2 + 2 * n], out[-1]


def _sibling_finish(name, grads, lands, send_sem, recv_sem, after):
    n = len(grads)

    def body(*refs):
        ins, lz = refs[:n], refs[n:2 * n]
        send_ref, recv_ref = refs[2 * n], refs[2 * n + 1]
        x, y, c = _position()
        for a in range(n):
            r = grads[a].shape[0] // N_DEV
            for q in range(4):
                cp = pltpu.make_async_remote_copy(
                    src_ref=ins[a].at[pl.ds((2 * q + 1 - c) * r, r), :], dst_ref=lz[a].at[q], send_sem=send_ref.at[0],
                    recv_sem=recv_ref.at[0], device_id=(x, y, c), device_id_type=MESH)
                cp.wait_send()
                cp.wait_recv()

    out = _split_call(name, body, list(grads) + list(lands), [send_sem, recv_sem], after, [], False)
    return out[:n], out[n:2 * n]


def _chip_start(name, parts, after):
    n = len(parts)
    lands = [_landing((3,) + p.shape[1:], p.dtype) for p in parts]

    def body(*refs):
        ins, lz = refs[:n], refs[n:2 * n]
        send_sems, recv_sems = refs[2 * n + len(after)], refs[2 * n + len(after) + 1]
        token = refs[-1]
        x, y, c = _position()
        for a in range(n):
            for j, chip in enumerate(_gather_peers(x, y, c)[:3]):
                pltpu.make_async_remote_copy(
                    src_ref=ins[a].at[2 * chip[0] + chip[1]], dst_ref=lz[a].at[j], send_sem=send_sems.at[j],
                    recv_sem=recv_sems.at[j], device_id=chip, device_id_type=MESH).start()
        token[...] = jnp.zeros_like(token)

    out = _split_call(name, body, list(parts) + lands, [], after, [(3,), (3,)], True)
    return out[0], out[1], out[2:2 + n], out[2 + n:2 + 2 * n], out[-1]


def _chip_finish(name, parts, lands, send_sems, recv_sems, after):
    n = len(parts)

    def body(*refs):
        ins, lz = refs[:n], refs[n:2 * n]
        send_ref, recv_ref = refs[2 * n], refs[2 * n + 1]
        me = _position()
        for a in range(n):
            for j in range(3):
                cp = pltpu.make_async_remote_copy(
                    src_ref=ins[a].at[j], dst_ref=lz[a].at[j], send_sem=send_ref.at[j], recv_sem=recv_ref.at[j],
                    device_id=me, device_id_type=MESH)
                cp.wait_send()
                cp.wait_recv()

    out = _split_call(name, body, list(parts) + list(lands), [send_sems, recv_sems], after, [], False)
    return out[:n], out[n:2 * n]


def _other_devices(x, y, c):
    return [(x + (k >> 2 & 1) * (1 - 2 * x), y + (k >> 1 & 1) * (1 - 2 * y), c + (k & 1) * (1 - 2 * c))
            for k in range(1, N_DEV)]


def _broadcast_start(name, arrays, after):
    n = len(arrays)
    lands = [_landing((N_DEV,) + a.shape, a.dtype) for a in arrays]

    def body(*refs):
        ins, lz = refs[:n], refs[n:2 * n]
        send_sems, recv_sems = refs[2 * n + len(after)], refs[2 * n + len(after) + 1]
        token = refs[-1]
        x, y, c = _position()
        for a in range(n):
            for k, peer in enumerate(_other_devices(x, y, c)):
                pltpu.make_async_remote_copy(
                    src_ref=ins[a], dst_ref=lz[a].at[4 * x + 2 * y + c], send_sem=send_sems.at[k],
                    recv_sem=recv_sems.at[k], device_id=peer, device_id_type=MESH).start()
        token[...] = jnp.zeros_like(token)

    out = _split_call(name, body, list(arrays) + lands, [], after, [(N_DEV - 1,), (N_DEV - 1,)], True)
    return out[0], out[1], out[2:2 + n], out[2 + n:2 + 2 * n], out[-1]


def _broadcast_finish(name, arrays, lands, send_sems, recv_sems, after):
    n = len(arrays)

    def body(*refs):
        ins, lz = refs[:n], refs[n:2 * n]
        send_ref, recv_ref = refs[2 * n], refs[2 * n + 1]
        x, y, c = _position()
        for a in range(n):
            for k, peer in enumerate(_other_devices(x, y, c)):
                cp = pltpu.make_async_remote_copy(
                    src_ref=ins[a], dst_ref=lz[a].at[4 * peer[0] + 2 * peer[1] + peer[2]], send_sem=send_ref.at[k],
                    recv_sem=recv_ref.at[k], device_id=(x, y, c), device_id_type=MESH)
                cp.wait_send()
                cp.wait_recv()

    out = _split_call(name, body, list(arrays) + list(lands), [send_sems, recv_sems], after, [], False)
    return out[:n], out[n:2 * n]


def _row_tile(r, target):
    return max(t for t in range(16, min(r, target) + 1, 16) if r % t == 0)


def _chip_partial(name, grad, got, c, tr):
    r = grad.shape[0] // N_DEV
    tr = _row_tile(r, tr)
    g4 = grad.reshape(4, 2, r, D)

    def body(c_ref, g_ref, s_ref, o_ref):
        o_ref[...] = (g_ref[...].astype(F32) + s_ref[...].astype(F32)).astype(BF16)

    return pl.pallas_call(
        body, name=name,
        grid_spec=pltpu.PrefetchScalarGridSpec(
            num_scalar_prefetch=1, grid=(4, r // tr),
            in_specs=[pl.BlockSpec((None, None, tr, D), lambda q, i, c_ref: (q, c_ref[0], i, 0)),
                      pl.BlockSpec((None, tr, D), lambda q, i, c_ref: (q, i, 0))],
            out_specs=pl.BlockSpec((None, tr, D), lambda q, i, c_ref: (q, i, 0))),
        out_shape=jax.ShapeDtypeStruct((4, r, D), BF16), compiler_params=_cparams(),
    )(c, g4, got)


class _WeightGather:
    def __init__(self, groups):
        self.groups = list(groups)
        self.index = {key: i for i, (key, _, _) in enumerate(groups)}
        self.state = [None] * len(groups)
        self.token = ()
        for i in range(min(2, len(groups))):
            self._start(i)

    def _tag(self, i):
        return "%s_%d" % self.groups[i][0][::-1]

    def _start(self, i):
        send, recv, lz, tok = _gather_start("gather_start_" + self._tag(i), self.groups[i][2], self.token)
        self.state[i] = dict(send=send, recv=recv, lands=lz)
        self.token = (tok,)

    def _step(self, name, near, far, fresh, marker):
        exists = lambda i: i is not None and i < len(self.groups)
        near, far, fresh = (i if exists(i) else None for i in (near, far, fresh))
        res = _gather_step(
            name, None if near is None else (self.state[near]["lands"], self.state[near]["recv"]),
            None if far is None else (self.state[far]["lands"], self.state[far]["rrecv"]),
            None if fresh is None else self.groups[fresh][2], tuple(marker) + self.token)
        self.token = (res["token"],)
        if near is not None:
            self.state[near].update(lands=res["near"], fsend=res["fsend"], frecv=res["frecv"], rsend=res["rsend"],
                                    rrecv=res["rrecv"])
        if far is not None:
            self.state[far].update(lands=res["far"], f2send=res["f2send"], f2recv=res["f2recv"])
        if fresh is not None:
            self.state[fresh] = dict(send=res["send"], recv=res["recv"], lands=res["fresh"])

    def fetch(self, layer, group, marker):
        k = self.index[(layer, group)]
        if k == 0:
            self._step("gather_step_first", 0, None, None, marker)
        self._step("gather_step_" + self._tag(k), k + 1, k, k + 2, marker)
        st = self.state[k]
        lz = _gather_finish("gather_finish_" + self._tag(k), st["lands"], st["send"], st["recv"], st["fsend"],
                            st["frecv"], st["rsend"], st["f2send"], st["f2recv"], self.token)
        self.state[k] = None
        return dict(zip(self.groups[k][1], lz))


class _GradReduce:
    def __init__(self, core, chip):
        self.core, self.chip = core, chip
        self.layer = None
        self.token = ()
        self.at_sibling, self.at_chips = [], []
        self.extra, self.smalls = (), {}

    def after(self):
        return self.token

    def add(self, group, names, grads):
        tag = "%s_%d" % (group, self.layer)
        send, recv, grads, lands, tok = _sibling_start("grad_sibling_start_" + tag, grads, self.token)
        self.at_sibling.append((tag, [(self.layer, n) for n in names], send, recv, grads, lands))
        self.token = (tok,)

    def advance(self, marker):
        for tag, keys, send, recv, grads, lands in self.at_sibling:
            grads, lands = _sibling_finish("grad_sibling_finish_" + tag, grads, lands, send, recv, marker)
            parts = [_chip_partial("chip_partial_%d_%s" % key, g, got, self.core, 512)
                     for key, g, got in zip(keys, grads, lands)]
            send, recv, parts, lands, tok = _chip_start("grad_chip_start_" + tag, parts, ())
            self.at_chips.append([tag, keys, send, recv, parts, lands])
            self.token = (tok,)
        self.at_sibling = []

    def small(self, part, arrays):
        keys = list(arrays)
        send, recv, own, slots, tok = _broadcast_start(
            "small_grads_start_%d_%s" % (self.layer, part), [arrays[k] for k in keys], self.token)
        self.smalls.setdefault(self.layer, []).append((part, keys, send, recv, own, slots))
        self.token = (tok,)

    def small_finish(self, layer, marker):
        mine, theirs = {}, {}
        for part, keys, send, recv, own, slots in self.smalls[layer]:
            own, slots = _broadcast_finish("small_grads_finish_%d_%s" % (layer, part), own, slots, send, recv, marker)
            mine.update(zip(keys, own))
            theirs.update(zip(keys, slots))
        return mine, theirs

    def collect(self, key, marker):
        for entry in self.at_chips:
            tag, keys, send, recv, parts, lands = entry
            if key in keys:
                if send is not None:
                    parts, lands = _chip_finish("grad_chip_finish_" + tag, parts, lands, send, recv, marker)
                    entry[2:] = [None, None, parts, lands]
                i = keys.index(key)
                return parts[i], lands[i]
        raise KeyError(key)


def _adamw_math(w, g, m, v):
    m = ADAM_B1 * m + (1.0 - ADAM_B1) * g
    v = ADAM_B2 * v + (1.0 - ADAM_B2) * jnp.square(g)
    m_hat = m / (1.0 - ADAM_B1 ** ADAM_STEP)
    v_hat = v / (1.0 - ADAM_B2 ** ADAM_STEP)
    delta = -ADAM_LR * (m_hat / (jnp.sqrt(v_hat) + ADAM_EPS) + ADAM_WD * w)
    return delta, m, v


def _adamw_small(wts, mom_m, mom_v, own, gathered, loss_own, loss_gathered, dev):
    names = SMALL
    nw = len(names)
    na = len(SMALL_ARRAYS)

    def body(dev_ref, *refs):
        w_refs, m_refs, v_refs = (dict(zip(names, refs[i * nw:(i + 1) * nw])) for i in range(3))
        own_refs = refs[3 * nw:3 * nw + DEPTH * na]
        g_refs = refs[3 * nw + DEPTH * na:3 * nw + 2 * DEPTH * na]
        loss_own_ref, loss_got_ref = refs[3 * nw + 2 * DEPTH * na:3 * nw + 2 * DEPTH * na + 2]
        outs = refs[3 * nw + 2 * DEPTH * na + 2:]
        g_out, d_out, m_out, v_out = (dict(zip(names, outs[i * nw:(i + 1) * nw])) for i in range(4))
        me = dev_ref[0]

        loss = None
        for d in range(N_DEV):
            for b in range(loss_own.shape[0]):
                term = jnp.where(me == d, loss_own_ref[b], loss_got_ref[d, b])
                loss = term if loss is None else loss + term
        outs[4 * nw][...] = loss

        def update(name, at, g):
            g_out[name][at] = g
            d_out[name][at], m_out[name][at], v_out[name][at] = _adamw_math(
                w_refs[name][at], g, m_refs[name][at], v_refs[name][at])

        for l in range(DEPTH):
            mine = dict(zip(SMALL_ARRAYS, own_refs[l * na:(l + 1) * na]))
            got = dict(zip(SMALL_ARRAYS, g_refs[l * na:(l + 1) * na]))

            def total(key, at):
                acc = None
                for d in range(N_DEV):
                    term = jnp.where(me == d, mine[key][at] if at else mine[key][...], got[key][(d,) + at])
                    acc = term if acc is None else acc + term
                return acc

            row = (slice(l, l + 1),)
            for k, name in enumerate(NORM_NAMES):
                update(name, row, total("norms", (slice(k, k + 1),)))
            for k, name in enumerate(VEC_NAMES):
                update(name, row, total("vecs", (slice(k, k + 1),)))
            update("gmlp_v_gain", (l,), total("gain_bias", (slice(0, NH),)))
            update("b_spatial", (l,), total("gain_bias", (slice(NH, 2 * NH),)))
            update("w_spatial", (l,), total("w_spatial", ()))
            update("w_pool", (l,), total("w_pool", ()))
            update("w_dw", (l,), total("w_dw", (slice(0, CONV_K),)))

    args = [src[n] for src in (wts, mom_m, mom_v) for n in names]
    args += [src[l][k] for src in (own, gathered) for l in range(DEPTH) for k in SMALL_ARRAYS]
    args += [loss_own, loss_gathered]
    outs = pl.pallas_call(
        body, name="adamw_small",
        in_specs=[pl.BlockSpec(memory_space=pltpu.SMEM)] + [pl.BlockSpec(memory_space=pltpu.VMEM)] * len(args),
        out_shape=[jax.ShapeDtypeStruct(wts[n].shape, F32) for _ in range(4) for n in names]
        + [jax.ShapeDtypeStruct((8, LANES), F32)],
        compiler_params=_cparams(),
    )(dev, *args)
    return tuple(dict(zip(names, outs[i * nw:(i + 1) * nw])) for i in range(4)) + (outs[4 * nw],)


def _adamw_layers(name, w, reduced, m, v, chip, tr, transposed=False, after=()):
    nl, r, cdim = w.shape
    tr = _row_tile(r, tr)
    nb = r // tr

    def body(q_ref, w_ref, p0_ref, g0_ref, p1_ref, g1_ref, m_ref, v_ref, *rest):
        g_ref, d_ref, nm_ref, nv_ref = rest[len(after):]

        def total(p_ref, got_ref):
            acc = p_ref[...].astype(F32)
            for j in range(3):
                acc = acc + got_ref[j].astype(F32)
            return acc

        g = jnp.where(pl.program_id(0) == 0, total(p0_ref, g0_ref), total(p1_ref, g1_ref))
        if transposed:
            g = g.T
        g_ref[...] = g
        d_ref[...], nm_ref[...], nv_ref[...] = _adamw_math(w_ref[...], g, m_ref[...], v_ref[...])

    blk = pl.BlockSpec((None, tr, cdim), lambda l, i, q: (l, i, 0))
    first = lambda l, i: i * (1 - l) + (nb - 1) * l
    second = lambda l, i: i * l
    if transposed:
        gshape = (cdim, tr)
        at = lambda lead, i: (lead, 0, i)
    else:
        gshape = (tr, cdim)
        at = lambda lead, i: (lead, i, 0)
    specs = [blk,
             pl.BlockSpec((None,) + gshape, lambda l, i, q: at(q[0], first(l, i))),
             pl.BlockSpec((3,) + gshape, lambda l, i, q: at(0, first(l, i))),
             pl.BlockSpec((None,) + gshape, lambda l, i, q: at(q[0], second(l, i))),
             pl.BlockSpec((3,) + gshape, lambda l, i, q: at(0, second(l, i))), blk, blk] + [ANY] * len(after)
    shape = jax.ShapeDtypeStruct((nl, r, cdim), F32)
    return pl.pallas_call(
        body, name=name,
        grid_spec=pltpu.PrefetchScalarGridSpec(num_scalar_prefetch=1, grid=(nl, nb), in_specs=specs, out_specs=[blk] * 4),
        out_shape=[shape] * 4, compiler_params=_cparams(),
    )(chip, w, *reduced[0], *reduced[1], m, v, *after)


def _to_rows(name, a):
    return jnp.swapaxes(a, 1, 2) if name == "w_in" else a


def _place_own_transposed(name, srcs, dev, out_dtype, tc):
    n = len(srcs)
    kdim, cdim = srcs[0][0].shape[-2:]

    def body(dev_ref, *refs):
        for a in range(n):
            refs[n + a][...] = refs[a][...].T.astype(out_dtype)

    return pl.pallas_call(
        body, name=name,
        grid_spec=pltpu.PrefetchScalarGridSpec(
            num_scalar_prefetch=1, grid=(kdim // tc,),
            in_specs=[pl.BlockSpec((None, tc, cdim), lambda i, d, l=l: (l, i, 0)) for _, l in srcs],
            out_specs=[pl.BlockSpec((cdim, tc), lambda i, d: (d[0], i))] * n),
        out_shape=[jax.ShapeDtypeStruct((N_DEV * cdim, kdim), out_dtype)] * n, compiler_params=_cparams(),
    )(dev, *[a for a, _ in srcs])


def _pack(arrays, rows):
    flat = jnp.concatenate([a.reshape(-1) for a in arrays])
    return jnp.pad(flat, (0, rows * D - flat.shape[0])).reshape(rows, D)


def _rows_for(shapes, mult=8):
    total = 0
    for shp in shapes:
        size = 1
        for dim in shp:
            size *= dim
        total += size
    return -(-total // (mult * D)) * mult


def kernel(x, mem, norm_mix_pre, norm_mix_post, w_in, w_out, gmlp_v_gain, w_spatial, b_spatial, w_pool, s_pool, w_dw, b_dw, conv_ln_g, conv_ln_b, norm_xattn_pre, norm_mem, norm_xattn_post, w_q, w_k, w_v, w_o, norm_ffn_pre, norm_ffn_post, w_up, w_down, loss_target, m_norm_mix_pre, m_norm_mix_post, m_w_in, m_w_out, m_gmlp_v_gain, m_w_spatial, m_b_spatial, m_w_pool, m_s_pool, m_w_dw, m_b_dw, m_conv_ln_g, m_conv_ln_b, m_norm_xattn_pre, m_norm_mem, m_norm_xattn_post, m_w_q, m_w_k, m_w_v, m_w_o, m_norm_ffn_pre, m_norm_ffn_post, m_w_up, m_w_down, v_norm_mix_pre, v_norm_mix_post, v_w_in, v_w_out, v_gmlp_v_gain, v_w_spatial, v_b_spatial, v_w_pool, v_s_pool, v_w_dw, v_b_dw, v_conv_ln_g, v_conv_ln_b, v_norm_xattn_pre, v_norm_mem, v_norm_xattn_post, v_w_q, v_w_k, v_w_v, v_w_o, v_norm_ffn_pre, v_norm_ffn_post, v_w_up, v_w_down):
    args = dict(locals())
    wts = {n: args[n] for n in WEIGHTS}
    mom_m = {n: args["m_" + n] for n in WEIGHTS}
    mom_v = {n: args["v_" + n] for n in WEIGHTS}
    xi, yi, ci = _position()
    me = 4 * xi + 2 * yi + ci

    dev = jnp.reshape(me, (1,)).astype(jnp.int32)
    lands = {}
    for call, names, tr in (("place_in", ("w_in",), 256), ("place_att", ("w_out", "w_q", "w_k", "w_v", "w_o"), 64),
                            ("place_up", ("w_up",), 256), ("place_down", ("w_down",), 256)):
        srcs = [(_to_rows(n, wts[n]), l) for l in range(DEPTH) for n in names]
        placed = (_place_own_transposed if names == ("w_up",) else _place_own)(call, srcs, dev, BF16, tr)
        lands.update(zip([(l, n) for l in range(DEPTH) for n in names], placed))
    (lands[(0, "taps")],) = _place_own("place_taps", [(_pack([w_dw], _rows_for([w_dw.shape])), None)], dev, F32, 8)
    groups = []
    for l in range(DEPTH):
        for group, names in GATHER_GROUPS:
            if (l, group) == (0, "in"):
                names = names + ("taps",)
            groups.append(((l, group), names, [lands[(l, n)] for n in names]))
    gather = _WeightGather(groups)

    def fetch(layer, group, marker):
        w = gather.fetch(layer, group, marker)
        if "taps" in w:
            blocks = w["taps"].reshape(N_DEV, -1)[:, :w_dw.size].reshape((N_DEV,) + w_dw.shape)
            w["taps"] = jnp.moveaxis(blocks, 0, 2).reshape(DEPTH, CONV_K, CW)
        return w

    reduce = _GradReduce(jnp.reshape(ci, (1,)).astype(jnp.int32), jnp.reshape(2 * xi + yi, (1,)).astype(jnp.int32))
    small = {n: wts[n] for n in SMALL if n != "w_dw"}
    _, dx = _local_step(x[0], mem[0], loss_target[0], fetch, small, reduce)
    reduce.advance((dx,))

    grad_w, delta, new_m, new_v = {}, {}, {}, {}
    marker = (dx,) + tuple(reduce.after())
    for n in UPDATE_ORDER:
        reduced = [reduce.collect((l, n), marker) for l in range(DEPTH)]
        outs = _adamw_layers("adamw_" + n, _to_rows(n, wts[n]), reduced, _to_rows(n, mom_m[n]), _to_rows(n, mom_v[n]),
                             reduce.chip, 256, transposed=n == "w_up", after=marker)
        grad_w[n], delta[n], new_m[n], new_v[n] = (_to_rows(n, o) for o in outs)
        marker = (outs[1],)

    own, slots = [None] * DEPTH, [None] * DEPTH
    for l in reversed(range(DEPTH)):
        own[l], slots[l] = reduce.small_finish(l, marker)
        if l == 0:
            loss_own, loss_slots = own[l].pop("loss"), slots[l].pop("loss")
    shard_cols = CW // N_DEV
    for l in range(DEPTH):
        own[l]["w_dw"] = lax.dynamic_slice_in_dim(own[l]["w_dw"], me * shard_cols, shard_cols, axis=1)
        slots[l]["w_dw"] = lax.dynamic_slice_in_dim(slots[l]["w_dw"], me * shard_cols, shard_cols, axis=2)
    *small_out, loss_tile = _adamw_small(wts, mom_m, mom_v, own, slots, loss_own, loss_slots, dev)
    for dst, src in zip((grad_w, delta, new_m, new_v), small_out):
        dst.update(src)

    return (loss_tile[0, 0], dx[None], *[grad_w[n] for n in WEIGHTS], *[delta[n] for n in WEIGHTS],
            *[new_m[n] for n in WEIGHTS], *[new_v[n] for n in WEIGHTS])
```

```python
import functools

import jax
import jax.numpy as jnp
from jax import lax
from jax.experimental import pallas as pl
from jax.experimental.pallas import tpu as pltpu

F32 = jnp.float32
BF16 = jnp.bfloat16

D = 2048
GW = 1024
PW = 512
CW = 512
HD = 128
NH = 8
NG = 4
POOL_WINDOWS = (2, 4, 8, 16)
CONV_K = 31
IN_COLS = 2 * GW + PW + 2 * CW
DFF = 4 * D
XH = 4
XHD = D // XH
ATT_SCALE = XHD ** -0.5
RMS_EPS = 1e-6
LN_EPS = 1e-5
DEPTH = 2
N_DEV = 8

ADAM_LR = 0.001
ADAM_B1 = 0.9
ADAM_B2 = 0.999
ADAM_EPS = 1e-08
ADAM_WD = 0.01
ADAM_STEP = 10

LANES = 128
CONV_HALO = 32
POOL_HALO = 16
ROW_TILE = 128
VMEM_LIMIT = 60 * 1024 * 1024

MESH = pl.DeviceIdType.MESH
NT = (((1,), (1,)), ((), ()))
NN = (((1,), (0,)), ((), ()))
TN = (((0,), (0,)), ((), ()))

BIG = ("w_out", "w_q", "w_k", "w_v", "w_o", "w_up", "w_down", "w_in")
UPDATE_ORDER = ("w_down", "w_up", "w_o", "w_q", "w_k", "w_v", "w_out", "w_in")
GATHER_GROUPS = (("in", ("w_in",)), ("out", ("w_out",)), ("att", ("w_q", "w_k", "w_v", "w_o")), ("up", ("w_up",)),
                 ("down", ("w_down",)))
SMALL = ("norm_mix_pre", "norm_mix_post", "gmlp_v_gain", "w_spatial", "b_spatial", "w_pool", "s_pool",
         "w_dw", "b_dw", "conv_ln_g", "conv_ln_b", "norm_xattn_pre", "norm_mem", "norm_xattn_post",
         "norm_ffn_pre", "norm_ffn_post")
WEIGHTS = ("norm_mix_pre", "norm_mix_post", "w_in", "w_out", "gmlp_v_gain", "w_spatial", "b_spatial", "w_pool",
           "s_pool", "w_dw", "b_dw", "conv_ln_g", "conv_ln_b", "norm_xattn_pre", "norm_mem", "norm_xattn_post",
           "w_q", "w_k", "w_v", "w_o", "norm_ffn_pre", "norm_ffn_post", "w_up", "w_down")


def _cparams():
    return pltpu.CompilerParams(vmem_limit_bytes=VMEM_LIMIT)


def _dot(a, b, dims):
    return lax.dot_general(a, b, dims, preferred_element_type=F32)


def _rms(x, g):
    y = x * lax.rsqrt(jnp.mean(x * x, axis=-1, keepdims=True) + RMS_EPS)
    return y * g


def _rms_bwd(x, g, dy):
    r = lax.rsqrt(jnp.mean(x * x, axis=-1, keepdims=True) + RMS_EPS)
    xh = x * r
    t = dy * g
    dx = r * (t - xh * jnp.mean(t * xh, axis=-1, keepdims=True))
    return dx, jnp.sum(dy * xh, axis=0, keepdims=True)


def _gelu(x):
    cdf = 0.5 * (1.0 + jnp.tanh(0.7978845608028654 * (x + 0.044715 * (x * x * x))))
    return x * cdf


def _layer_norm(x, g, b=None):
    mu = jnp.mean(x, axis=-1, keepdims=True)
    xc = x - mu
    var = jnp.mean(xc * xc, axis=-1, keepdims=True)
    y = xc * lax.rsqrt(var + LN_EPS) * g
    return y if b is None else y + b


def _sigmoid(x):
    return 1.0 / (1.0 + jnp.exp(-x))


def _gmlp_rows(zu, zv, gv):
    return _gelu(zu), _layer_norm(_gelu(zv), gv)


def _glu(cv, cg):
    return cv * _sigmoid(cg)


def _ln_silu(h, g, b):
    y = _layer_norm(h, g, b)
    return y * _sigmoid(y)


ANY = pl.BlockSpec(memory_space=pl.ANY)


ROWS_TILE = 256
COLS_TILE = 512
DW_TILE = 512
RESIDENT_K = 2048
STREAM_K_TILE = 1024
STREAM_ROWS = 512


def _k_tiles(kdim):
    if kdim <= RESIDENT_K:
        return ROWS_TILE, kdim
    return STREAM_ROWS, max(t for t in range(LANES, STREAM_K_TILE + 1, LANES) if kdim % t == 0)


def _rowop_mm(name, kind, rows, g, w, dims, out_dtype, u=None, after=()):
    s = rows[0].shape[0]
    n = w.shape[0] if dims == NT else w.shape[1]
    tm, tn = min(ROWS_TILE, s), min(COLS_TILE, n)
    ni = s // tm
    bwd = kind == "rms_bwd"

    def rows_body(*refs):
        refs = list(refs)
        row_refs = [refs.pop(0) for _ in rows]
        g_ref = refs.pop(0)
        del refs[:len(after)]
        if bwd:
            a, dg = _rms_bwd(row_refs[0][...], g_ref[...], row_refs[1][...])
            refs[1][0] = dg
        else:
            a = _rms(row_refs[0][...], g_ref[...])
        refs[0][...] = a.astype(BF16)

    row_spec = pl.BlockSpec((tm, D), lambda i: (i, 0))
    res = pl.pallas_call(
        rows_body, name=name + "_rows", grid=(ni,),
        in_specs=[row_spec] * len(rows) + [pl.BlockSpec((1, D), lambda i: (0, 0))] + [ANY] * len(after),
        out_specs=[row_spec] + ([pl.BlockSpec((1, 1, D), lambda i: (i, 0, 0))] if bwd else []),
        out_shape=[jax.ShapeDtypeStruct((s, D), BF16)] + ([jax.ShapeDtypeStruct((ni, 1, D), F32)] if bwd else []),
        compiler_params=_cparams(),
    )(*rows, g, *after)
    a = res[0]

    def body(a_ref, w_ref, *rest):
        acc = _dot(a_ref[...], w_ref[...], dims)
        if u is not None:
            acc = acc * (2.0 * jnp.maximum(rest[0][...].astype(F32), 0.0))
        rest[-1][...] = acc.astype(out_dtype)

    w_spec = pl.BlockSpec((tn, D), lambda j: (j, 0)) if dims == NT else pl.BlockSpec((D, tn), lambda j: (0, j))
    tile = pl.BlockSpec((s, tn), lambda j: (0, j))
    out = pl.pallas_call(
        body, name=name, grid=(n // tn,),
        in_specs=[pl.BlockSpec((s, D), lambda j: (0, 0)), w_spec] + ([tile] if u is not None else []),
        out_specs=tile, out_shape=jax.ShapeDtypeStruct((s, n), out_dtype), compiler_params=_cparams(),
    )(a, w, *([u] if u is not None else []))
    return (out, *res)


def _mm_rowop(name, kind, pairs, rows, g, relu2=False, after=()):
    s, kdim = pairs[0][0].shape
    tm, tk = _k_tiles(kdim)
    tm = min(tm, s)
    ni, nk = s // tm, kdim // tk
    npair = len(pairs)

    def body(*refs):
        refs = list(refs)
        a_refs = [refs.pop(0) for _ in range(npair)]
        w_refs = [refs.pop(0) for _ in range(npair)]
        row_refs = [refs.pop(0) for _ in rows]
        g_ref = refs.pop(0)
        del refs[:len(after)]
        acc = refs.pop()
        outs = refs
        k = pl.program_id(1)

        @pl.when(k == 0)
        def _():
            acc[...] = jnp.zeros_like(acc)

        for a_ref, w_ref, (_, _, dims) in zip(a_refs, w_refs, pairs):
            a = a_ref[...]
            if relu2:
                a = jnp.square(jnp.maximum(a.astype(F32), 0.0))
            acc[...] += _dot(a.astype(BF16), w_ref[...], dims)

        @pl.when(k == nk - 1)
        def _():
            h = acc[...]
            if kind == "rms_res":
                outs[0][...] = row_refs[0][...] + _rms(h, g_ref[...])
                outs[1][...] = h
            else:
                dx, dg = _rms_bwd(row_refs[0][...], g_ref[...], h)
                if kind == "rms_bwd_res":
                    outs[0][...] = row_refs[1][...] + dx
                    outs[1][0] = dg
                else:
                    outs[0][0] = dg

    row_spec = pl.BlockSpec((tm, D), lambda i, k: (i, 0))
    dg_shape = jax.ShapeDtypeStruct((ni, 1, D), F32)
    dg_spec = pl.BlockSpec((1, 1, D), lambda i, k: (i, 0, 0))
    in_specs = [pl.BlockSpec((tm, tk), lambda i, k: (i, k))] * npair
    for _, _, dims in pairs:
        in_specs.append(pl.BlockSpec((tk, D), lambda i, k: (k, 0)) if dims == NN
                        else pl.BlockSpec((D, tk), lambda i, k: (0, k)))
    in_specs += [row_spec] * len(rows) + [pl.BlockSpec((1, D), lambda i, k: (0, 0))] + [ANY] * len(after)
    if kind == "rms_res":
        out_shape = [jax.ShapeDtypeStruct((s, D), F32)] * 2
        out_specs = [row_spec, row_spec]
    elif kind == "rms_bwd_res":
        out_shape = [jax.ShapeDtypeStruct((s, D), F32), dg_shape]
        out_specs = [row_spec, dg_spec]
    else:
        out_shape = [dg_shape]
        out_specs = [dg_spec]
    return pl.pallas_call(
        body, name=name, grid=(ni, nk), in_specs=in_specs, out_specs=out_specs, out_shape=out_shape,
        scratch_shapes=[pltpu.VMEM((tm, D), F32)], compiler_params=_cparams(),
    )(*[p[0] for p in pairs], *[p[1] for p in pairs], *rows, g, *after)


def _mm_tn(name, a, gmat, relu2=False, after=()):
    s, m = a.shape
    tm, ts = min(DW_TILE, m), s
    ni, ns = m // tm, s // ts

    def body(a_ref, g_ref, *rest):
        o_ref, acc = rest[len(after):]
        k = pl.program_id(1)

        @pl.when(k == 0)
        def _():
            acc[...] = jnp.zeros_like(acc)

        av = a_ref[...]
        if relu2:
            av = jnp.square(jnp.maximum(av.astype(F32), 0.0))
        acc[...] += _dot(av.astype(BF16), g_ref[...], TN)

        @pl.when(k == ns - 1)
        def _():
            o_ref[...] = acc[...].astype(BF16)

    return pl.pallas_call(
        body, name=name, grid=(ni, ns),
        in_specs=[pl.BlockSpec((ts, tm), lambda i, k: (k, i)), pl.BlockSpec((ts, D), lambda i, k: (k, 0))]
        + [ANY] * len(after),
        out_specs=pl.BlockSpec((tm, D), lambda i, k: (i, 0)),
        out_shape=jax.ShapeDtypeStruct((m, D), BF16),
        scratch_shapes=[pltpu.VMEM((tm, D), F32)], compiler_params=_cparams(),
    )(a, gmat, *after)


def _tril():
    r = lax.broadcasted_iota(jnp.int32, (HD, HD), 0)
    c = lax.broadcasted_iota(jnp.int32, (HD, HD), 1)
    return (c <= r).astype(F32)


def _gmlp_fwd(z, gv, ws, bst, tb):
    s = z.shape[0]
    tb = min(tb, s)

    def body(zu_ref, zv_ref, gv_ref, ws_ref, bst_ref, y_ref):
        tril = _tril()
        for h in range(NH):
            cols = slice(h * HD, (h + 1) * HD)
            u, vln = _gmlp_rows(zu_ref[:, cols], zv_ref[:, cols], gv_ref[h:h + 1, :])
            wm = (ws_ref[h] * tril).astype(BF16)
            vb = vln.astype(BF16)
            for c in range(tb // HD):
                rws = slice(c * HD, (c + 1) * HD)
                mixed = _dot(wm, vb[rws], NN) + bst_ref[:, h:h + 1]
                y_ref[rws, cols] = (u[rws] * mixed).astype(BF16)

    return pl.pallas_call(
        body, name="gmlp_fwd", grid=(s // tb,),
        in_specs=[pl.BlockSpec((tb, GW), lambda i: (i, 0)), pl.BlockSpec((tb, GW), lambda i: (i, 1)),
                  pl.BlockSpec((NH, HD), lambda i: (0, 0)), pl.BlockSpec((NH, HD, HD), lambda i: (0, 0, 0)),
                  pl.BlockSpec((HD, NH), lambda i: (0, 0))],
        out_specs=pl.BlockSpec((tb, GW), lambda i: (i, 0)),
        out_shape=jax.ShapeDtypeStruct((s, D), BF16), compiler_params=_cparams(),
    )(z, z, gv, ws, bst)


def _gmlp_bwd(z, dy, gv, ws, bst, tb):
    s = z.shape[0]
    tb = min(tb, s)
    nb = s // tb

    def body(zu_ref, zv_ref, dy_ref, gv_ref, ws_ref, bst_ref, dz_ref, dgv_ref, dws_ref, db_ref):
        tril = _tril()
        for h in range(NH):
            cols = slice(h * HD, (h + 1) * HD)
            (u, vln), vjp = jax.vjp(_gmlp_rows, zu_ref[:, cols], zv_ref[:, cols], gv_ref[h:h + 1, :])
            wmf = ws_ref[h] * tril
            wm = wmf.astype(BF16)
            wmt = wmf.T.astype(BF16)
            vb = vln.astype(BF16)
            dws = jnp.zeros((HD, HD), F32)
            db = jnp.zeros((HD, 1), F32)
            du_parts, dvln_parts = [], []
            for c in range(tb // HD):
                rws = slice(c * HD, (c + 1) * HD)
                mixed = _dot(wm, vb[rws], NN) + bst_ref[:, h:h + 1]
                dyc = dy_ref[rws, cols]
                du_parts.append(dyc * mixed)
                dmixed = dyc * u[rws]
                dmb = dmixed.astype(BF16)
                dws = dws + _dot(dmb, vb[rws], NT)
                db = db + jnp.sum(dmixed, axis=1, keepdims=True)
                dvln_parts.append(_dot(wmt, dmb, NN))
            du = jnp.concatenate(du_parts, axis=0)
            dvln = jnp.concatenate(dvln_parts, axis=0)
            dzu, dzv, dgv = vjp((du, dvln))
            dz_ref[:, cols] = dzu.astype(BF16)
            dz_ref[:, slice(GW + h * HD, GW + (h + 1) * HD)] = dzv.astype(BF16)
            dgv_ref[0, h:h + 1, :] = dgv
            dws_ref[0, h] = dws * tril
            db_ref[0, h] = jnp.broadcast_to(db, (HD, LANES))

    blk = pl.BlockSpec((tb, GW), lambda i: (i, 0))
    return pl.pallas_call(
        body, name="gmlp_bwd", grid=(nb,),
        in_specs=[blk, pl.BlockSpec((tb, GW), lambda i: (i, 1)), blk,
                  pl.BlockSpec((NH, HD), lambda i: (0, 0)), pl.BlockSpec((NH, HD, HD), lambda i: (0, 0, 0)),
                  pl.BlockSpec((HD, NH), lambda i: (0, 0))],
        out_specs=[pl.BlockSpec((tb, 2 * GW), lambda i: (i, 0)), pl.BlockSpec((1, NH, HD), lambda i: (i, 0, 0)),
                   pl.BlockSpec((1, NH, HD, HD), lambda i: (i, 0, 0, 0)),
                   pl.BlockSpec((1, NH, HD, LANES), lambda i: (i, 0, 0, 0))],
        out_shape=[jax.ShapeDtypeStruct((s, IN_COLS), BF16),
                   jax.ShapeDtypeStruct((nb, NH, HD), F32), jax.ShapeDtypeStruct((nb, NH, HD, HD), F32),
                   jax.ShapeDtypeStruct((nb, NH, HD, LANES), F32)],
        compiler_params=_cparams(),
    )(z, z, dy, gv, ws, bst)


def _pool_count(t0, window):
    pos = (t0 + lax.broadcasted_iota(jnp.int32, (ROW_TILE, LANES), 0)).astype(F32)
    return jnp.minimum(pos + 1.0, float(window))


def _window_sum(win, levels, back):
    n = win.shape[0]
    for lv in range(levels):
        step = 1 << lv
        win = win + pltpu.roll(win, n - step if back else step, 0)
    return win


def _pool_pooled(ppad_ref, t0, g):
    win = ppad_ref[pl.ds(t0, ROW_TILE + POOL_HALO), :]
    wsum = _window_sum(win, g + 1, False)[POOL_HALO:]
    return wsum / _pool_count(t0, POOL_WINDOWS[g]) - win[POOL_HALO:]


def _pool_fwd(z, wp, sp, y):
    s = z.shape[0]
    nt = s // ROW_TILE

    def body(p_ref, wp_ref, sp_ref, _, y_ref, ppad):
        for g in range(NG):
            cols = slice(g * LANES, (g + 1) * LANES)
            ppad[pl.ds(0, POOL_HALO), :] = jnp.zeros((POOL_HALO, LANES), F32)
            ppad[pl.ds(POOL_HALO, s), :] = p_ref[:, cols]
            wpb = wp_ref[g].astype(BF16)
            scale = sp_ref[:, cols]

            def tile(t, carry):
                t0 = pl.multiple_of(t * ROW_TILE, ROW_TILE)
                pooled = _pool_pooled(ppad, t0, g)
                y_ref[pl.ds(t0, ROW_TILE), cols] = (_dot(pooled.astype(BF16), wpb, NN) * scale).astype(BF16)
                return carry

            lax.fori_loop(0, nt, tile, 0)

    return pl.pallas_call(
        body, name="pool_fwd", grid=(1,),
        in_specs=[pl.BlockSpec((s, PW), lambda i: (0, 2 * GW // PW)),
                  pl.BlockSpec((NG, LANES, LANES), lambda i: (0, 0, 0)), pl.BlockSpec((1, PW), lambda i: (0, 0)), ANY],
        out_specs=pl.BlockSpec((s, PW), lambda i: (0, GW // PW)),
        out_shape=jax.ShapeDtypeStruct((s, D), BF16), input_output_aliases={3: 0},
        scratch_shapes=[pltpu.VMEM((s + POOL_HALO, LANES), F32)], compiler_params=_cparams(),
    )(z, wp, sp, y)


def _pool_bwd(z, dy, wp, sp, dz):
    s = z.shape[0]
    nt = s // ROW_TILE

    def body(p_ref, dy_ref, wp_ref, sp_ref, _, dp_ref, dwp_ref, dsp_ref, ppad, rpad, dpool):
        for g in range(NG):
            cols = slice(g * LANES, (g + 1) * LANES)
            ppad[pl.ds(0, POOL_HALO), :] = jnp.zeros((POOL_HALO, LANES), F32)
            ppad[pl.ds(POOL_HALO, s), :] = p_ref[:, cols]
            rpad[pl.ds(s, POOL_HALO), :] = jnp.zeros((POOL_HALO, LANES), F32)
            wpb = wp_ref[g].astype(BF16)
            scale = sp_ref[:, cols]

            def tile(t, carry):
                dwp, dsp = carry
                t0 = pl.multiple_of(t * ROW_TILE, ROW_TILE)
                pooled = _pool_pooled(ppad, t0, g)
                pb = pooled.astype(BF16)
                dyt = dy_ref[pl.ds(t0, ROW_TILE), cols]
                dsp = dsp + jnp.sum(dyt * _dot(pb, wpb, NN), axis=0, keepdims=True)
                dmm = (dyt * scale).astype(BF16)
                dwp = dwp + _dot(pb, dmm, TN)
                dpooled = _dot(dmm, wpb, NT)
                rpad[pl.ds(t0, ROW_TILE), :] = dpooled / _pool_count(t0, POOL_WINDOWS[g])
                dpool[pl.ds(t0, ROW_TILE), :] = dpooled
                return dwp, dsp

            dwp, dsp = lax.fori_loop(0, nt, tile, (jnp.zeros((LANES, LANES), F32), jnp.zeros((1, LANES), F32)))
            dwp_ref[g] = dwp
            dsp_ref[:, cols] = dsp

            def tile2(t, carry):
                t0 = pl.multiple_of(t * ROW_TILE, ROW_TILE)
                win = rpad[pl.ds(t0, ROW_TILE + POOL_HALO), :]
                back = _window_sum(win, g + 1, True)[:ROW_TILE]
                rows = pl.ds(t0, ROW_TILE)
                dp_ref[rows, cols] = (back - dpool[rows, :]).astype(BF16)
                return carry

            lax.fori_loop(0, nt, tile2, 0)

    return pl.pallas_call(
        body, name="pool_bwd", grid=(1,),
        in_specs=[pl.BlockSpec((s, PW), lambda i: (0, 2 * GW // PW)), pl.BlockSpec((s, PW), lambda i: (0, GW // PW)),
                  pl.BlockSpec((NG, LANES, LANES), lambda i: (0, 0, 0)), pl.BlockSpec((1, PW), lambda i: (0, 0)), ANY],
        out_specs=[pl.BlockSpec((s, PW), lambda i: (0, 2 * GW // PW)),
                   pl.BlockSpec((NG, LANES, LANES), lambda i: (0, 0, 0)), pl.BlockSpec((1, PW), lambda i: (0, 0))],
        out_shape=[jax.ShapeDtypeStruct((s, IN_COLS), BF16), jax.ShapeDtypeStruct((NG, LANES, LANES), F32),
                   jax.ShapeDtypeStruct((1, PW), F32)],
        input_output_aliases={4: 0},
        scratch_shapes=[pltpu.VMEM((s + POOL_HALO, LANES), F32), pltpu.VMEM((s + POOL_HALO, LANES), F32),
                        pltpu.VMEM((s, LANES), F32)],
        compiler_params=_cparams(),
    )(z, dy, wp, sp, dz)


CONV_LEAD = CONV_HALO - (CONV_K - 1)


SUBLANES = 8


def _sublane_shifts(win):
    n = win.shape[0]
    return [win] + [pltpu.roll(win, n - b, 0) for b in range(1, SUBLANES)]


def _shifted(shifts, offset):
    a, b = divmod(offset, SUBLANES)
    return shifts[b][a * SUBLANES:a * SUBLANES + ROW_TILE]


def _conv_taps(shifts, wdw_ref, lead, reverse):
    acc = jnp.zeros((ROW_TILE, CW), F32)
    for j in range(CONV_K):
        tap = (CONV_K - 1 - j) if reverse else j
        acc = acc + wdw_ref[tap:tap + 1, :] * _shifted(shifts, lead + j)
    return acc


def _conv_fill_glu(cv_ref, cg_ref, xpad, s):
    xpad[pl.ds(0, CONV_HALO), :] = jnp.zeros((CONV_HALO, CW), F32)

    def fill(t, carry):
        t0 = pl.multiple_of(t * ROW_TILE, ROW_TILE)
        rows = pl.ds(t0, ROW_TILE)
        xpad[pl.ds(t0 + CONV_HALO, ROW_TILE), :] = _glu(cv_ref[rows, :], cg_ref[rows, :])
        return carry

    lax.fori_loop(0, s // ROW_TILE, fill, 0)


def _conv_fwd(z, wdw, bdw, lng, lnb, y):
    s = z.shape[0]

    def body(cv_ref, cg_ref, wdw_ref, bdw_ref, lng_ref, lnb_ref, _, y_ref, xpad):
        _conv_fill_glu(cv_ref, cg_ref, xpad, s)

        def tile(t, carry):
            t0 = pl.multiple_of(t * ROW_TILE, ROW_TILE)
            shifts = _sublane_shifts(xpad[pl.ds(t0, ROW_TILE + CONV_HALO), :])
            hc = _conv_taps(shifts, wdw_ref, CONV_LEAD, False) + bdw_ref[...]
            y_ref[pl.ds(t0, ROW_TILE), :] = _ln_silu(hc, lng_ref[...], lnb_ref[...]).astype(BF16)
            return carry

        lax.fori_loop(0, s // ROW_TILE, tile, 0)

    vec = pl.BlockSpec((1, CW), lambda i: (0, 0))
    return pl.pallas_call(
        body, name="conv_fwd", grid=(1,),
        in_specs=[pl.BlockSpec((s, CW), lambda i: (0, (2 * GW + PW) // CW)),
                  pl.BlockSpec((s, CW), lambda i: (0, (2 * GW + PW) // CW + 1)),
                  pl.BlockSpec((CONV_K + 1, CW), lambda i: (0, 0)), vec, vec, vec, ANY],
        out_specs=pl.BlockSpec((s, CW), lambda i: (0, (GW + PW) // CW)),
        out_shape=jax.ShapeDtypeStruct((s, D), BF16), input_output_aliases={6: 0},
        scratch_shapes=[pltpu.VMEM((s + CONV_HALO, CW), F32)], compiler_params=_cparams(),
    )(z, z, wdw, bdw, lng, lnb, y)


def _conv_bwd(z, dy, wdw, bdw, lng, lnb, dz):
    s = z.shape[0]

    def body(cv_ref, cg_ref, dy_ref, wdw_ref, bdw_ref, lng_ref, lnb_ref, _,
             dz_ref, dwdw_ref, dbdw_ref, dlng_ref, dlnb_ref, xpad, dpad, dcg_keep):
        @pl.when(pl.program_id(0) == 0)
        def _():
            compute(cv_ref, cg_ref, dy_ref, wdw_ref, bdw_ref, lng_ref, lnb_ref,
                    dz_ref, dcg_keep, dwdw_ref, dbdw_ref, dlng_ref, dlnb_ref, xpad, dpad)

        @pl.when(pl.program_id(0) == 1)
        def _():
            dz_ref[...] = dcg_keep[...]

    def compute(cv_ref, cg_ref, dy_ref, wdw_ref, bdw_ref, lng_ref, lnb_ref,
                dcv_ref, dcg_ref, dwdw_ref, dbdw_ref, dlng_ref, dlnb_ref, xpad, dpad):
        _conv_fill_glu(cv_ref, cg_ref, xpad, s)
        dpad[pl.ds(s, CONV_HALO), :] = jnp.zeros((CONV_HALO, CW), F32)
        dwdw_ref[...] = jnp.zeros((CONV_K + 1, CW), F32)

        def tile(t, carry):
            db, dg, dbeta = carry
            t0 = pl.multiple_of(t * ROW_TILE, ROW_TILE)
            shifts = _sublane_shifts(xpad[pl.ds(t0, ROW_TILE + CONV_HALO), :])
            hc = _conv_taps(shifts, wdw_ref, CONV_LEAD, False) + bdw_ref[...]
            _, vjp = jax.vjp(_ln_silu, hc, lng_ref[...], lnb_ref[...])
            dhc, dg_t, dbeta_t = vjp(dy_ref[pl.ds(t0, ROW_TILE), :])
            dpad[pl.ds(t0, ROW_TILE), :] = dhc
            for j in range(CONV_K):
                dwdw_ref[j:j + 1, :] += jnp.sum(dhc * _shifted(shifts, CONV_LEAD + j), axis=0, keepdims=True)
            return db + jnp.sum(dhc, axis=0, keepdims=True), dg + dg_t, dbeta + dbeta_t

        zero = jnp.zeros((1, CW), F32)
        db, dg, dbeta = lax.fori_loop(0, s // ROW_TILE, tile, (zero, zero, zero))
        dbdw_ref[...] = db
        dlng_ref[...] = dg
        dlnb_ref[...] = dbeta

        def tile2(t, carry):
            t0 = pl.multiple_of(t * ROW_TILE, ROW_TILE)
            rows = pl.ds(t0, ROW_TILE)
            dglu = _conv_taps(_sublane_shifts(dpad[pl.ds(t0, ROW_TILE + CONV_HALO), :]), wdw_ref, 0, True)
            _, vjp = jax.vjp(_glu, cv_ref[rows, :], cg_ref[rows, :])
            dcv, dcg = vjp(dglu)
            dcv_ref[rows, :] = dcv.astype(BF16)
            dcg_ref[rows, :] = dcg.astype(BF16)
            return carry

        lax.fori_loop(0, s // ROW_TILE, tile2, 0)

    vec = pl.BlockSpec((1, CW), lambda i: (0, 0))
    wspec = pl.BlockSpec((CONV_K + 1, CW), lambda i: (0, 0))
    vshape = jax.ShapeDtypeStruct((1, CW), F32)
    return pl.pallas_call(
        body, name="conv_bwd", grid=(2,),
        in_specs=[pl.BlockSpec((s, CW), lambda i: (0, (2 * GW + PW) // CW)),
                  pl.BlockSpec((s, CW), lambda i: (0, (2 * GW + PW) // CW + 1)),
                  pl.BlockSpec((s, CW), lambda i: (0, (GW + PW) // CW)), wspec, vec, vec, vec, ANY],
        out_specs=[pl.BlockSpec((s, CW), lambda i: (0, (2 * GW + PW) // CW + i)), wspec, vec, vec, vec],
        out_shape=[jax.ShapeDtypeStruct((s, IN_COLS), BF16), jax.ShapeDtypeStruct((CONV_K + 1, CW), F32),
                   vshape, vshape, vshape],
        input_output_aliases={7: 0},
        scratch_shapes=[pltpu.VMEM((s + CONV_HALO, CW), F32), pltpu.VMEM((s + CONV_HALO, CW), F32),
                        pltpu.VMEM((s, CW), BF16)],
        compiler_params=_cparams(),
    )(z, z, dy, wdw, bdw, lng, lnb, dz)


def _softmax_rows(sc):
    e = jnp.exp(sc - jnp.max(sc, axis=-1, keepdims=True))
    return e / jnp.sum(e, axis=-1, keepdims=True)


def _attn_fwd(q, k, v, tq):
    s, m = q.shape[0], k.shape[0]
    tq = min(tq, s)

    def body(q_ref, k_ref, v_ref, o_ref):
        for h in range(XH):
            cols = slice(h * XHD, (h + 1) * XHD)
            p = _softmax_rows(_dot(q_ref[:, cols], k_ref[:, cols], NT) * ATT_SCALE)
            o_ref[:, cols] = _dot(p.astype(BF16), v_ref[:, cols], NN).astype(BF16)

    kv = pl.BlockSpec((m, D), lambda i: (0, 0))
    return pl.pallas_call(
        body, name="attn_fwd", grid=(s // tq,),
        in_specs=[pl.BlockSpec((tq, D), lambda i: (i, 0)), kv, kv],
        out_specs=pl.BlockSpec((tq, D), lambda i: (i, 0)),
        out_shape=jax.ShapeDtypeStruct((s, D), BF16), compiler_params=_cparams(),
    )(q, k, v)


def _attn_bwd(q, k, v, do, tq):
    s, m = q.shape[0], k.shape[0]
    tq = min(tq, s)

    def body(q_ref, k_ref, v_ref, do_ref, dq_ref, dk_ref, dv_ref):
        @pl.when(pl.program_id(0) == 0)
        def _():
            dk_ref[...] = jnp.zeros_like(dk_ref)
            dv_ref[...] = jnp.zeros_like(dv_ref)

        for h in range(XH):
            cols = slice(h * XHD, (h + 1) * XHD)
            qh, kh, vh, doh = q_ref[:, cols], k_ref[:, cols], v_ref[:, cols], do_ref[:, cols]
            p = _softmax_rows(_dot(qh, kh, NT) * ATT_SCALE)
            dp = _dot(doh, vh, NT)
            dv_ref[:, cols] += _dot(p.astype(BF16), doh, TN)
            ds = (p * (dp - jnp.sum(p * dp, axis=-1, keepdims=True)) * ATT_SCALE).astype(BF16)
            dq_ref[:, cols] = _dot(ds, kh, NN).astype(BF16)
            dk_ref[:, cols] += _dot(ds, qh, TN)

    kv = pl.BlockSpec((m, D), lambda i: (0, 0))
    qs = pl.BlockSpec((tq, D), lambda i: (i, 0))
    return pl.pallas_call(
        body, name="attn_bwd", grid=(s // tq,),
        in_specs=[qs, kv, kv, qs], out_specs=[qs, kv, kv],
        out_shape=[jax.ShapeDtypeStruct((s, D), BF16), jax.ShapeDtypeStruct((m, D), F32),
                   jax.ShapeDtypeStruct((m, D), F32)],
        compiler_params=_cparams(),
    )(q, k, v, do)


def _loss_head(y, target, tm):
    s = y.shape[0]
    tm = min(tm, s)

    def body(y_ref, t_ref, dy_ref, part_ref):
        err = y_ref[...] - t_ref[...]
        dy_ref[...] = err * (1.0 / D)
        part_ref[...] = jnp.full((1, 8, LANES), 0.5 * jnp.sum(err * err) * (1.0 / D), F32)

    blk = pl.BlockSpec((tm, D), lambda i: (i, 0))
    return pl.pallas_call(
        body, name="loss_head", grid=(s // tm,), in_specs=[blk, blk],
        out_specs=[blk, pl.BlockSpec((1, 8, LANES), lambda i: (i, 0, 0))],
        out_shape=[jax.ShapeDtypeStruct((s, D), F32), jax.ShapeDtypeStruct((s // tm, 8, LANES), F32)],
        compiler_params=_cparams(),
    )(y, target)


def _layer_fwd(x0, mem, w, p, fetch):
    z, hn0 = _rowop_mm("mix_in", "rms", (x0,), p["norm_mix_pre"], w["w_in"], NT, F32)
    y = _gmlp_fwd(z, p["gmlp_v_gain"], p["w_spatial"], p["b_spatial_t"], 512)
    y = _pool_fwd(z, p["w_pool"], p["s_pool"], y)
    y = _conv_fwd(z, p["w_dw"], p["b_dw"], p["conv_ln_g"], p["conv_ln_b"], y)
    w.update(fetch("out", (y,)))
    x1, h0 = _mm_rowop("mix_out", "rms_res", [(y, w["w_out"], NN)], (x0,), p["norm_mix_post"])
    w.update(fetch("att", (x1,)))
    q, hn1 = _rowop_mm("att_q", "rms", (x1,), p["norm_xattn_pre"], w["w_q"], NN, BF16)
    k, mn = _rowop_mm("att_k", "rms", (mem,), p["norm_mem"], w["w_k"], NN, BF16)
    v, _ = _rowop_mm("att_v", "rms", (mem,), p["norm_mem"], w["w_v"], NN, BF16)
    o = _attn_fwd(q, k, v, 512)
    x2, h1 = _mm_rowop("att_o", "rms_res", [(o, w["w_o"], NN)], (x1,), p["norm_xattn_post"])
    w.update(fetch("up", (x2,)))
    u, hn2 = _rowop_mm("ffn_up", "rms", (x2,), p["norm_ffn_pre"], w["w_up"], NT, BF16)
    w.update(fetch("down", (u,)))
    x3, h2 = _mm_rowop("ffn_down", "rms_res", [(u, w["w_down"], NN)], (x2,), p["norm_ffn_post"], relu2=True)
    saved = dict(x0=x0, z=z, hn0=hn0, y=y, h0=h0, x1=x1, q=q, hn1=hn1, k=k, v=v, mn=mn, o=o, h1=h1, x2=x2, u=u,
                 hn2=hn2, h2=h2)
    return x3, saved


def _layer_bwd(dx3, mem, w, p, sv, red):
    gs = {}
    du, dh2, dg = _rowop_mm("ffn_down_bwd", "rms_bwd", (sv["h2"], dx3), p["norm_ffn_post"], w["w_down"], NT, BF16,
                            u=sv["u"], after=red.after())
    gs["norm_ffn_post"] = jnp.sum(dg, axis=0)
    g_down = _mm_tn("ffn_down_dw", sv["u"], dh2, relu2=True)
    red.advance((g_down,))
    dx2, dg = _mm_rowop("ffn_up_bwd", "rms_bwd_res", [(du, w["w_up"], NN)], (sv["x2"], dx3), p["norm_ffn_pre"],
                        after=red.after())
    gs["norm_ffn_pre"] = jnp.sum(dg, axis=0)
    g_up = _mm_tn("ffn_up_dw", du, sv["hn2"])
    red.add("ffn", ("w_down", "w_up"), [g_down, g_up])
    do, dh1, dg = _rowop_mm("att_o_bwd", "rms_bwd", (sv["h1"], dx2), p["norm_xattn_post"], w["w_o"], NT, BF16,
                            after=red.after())
    gs["norm_xattn_post"] = jnp.sum(dg, axis=0)
    g_o = _mm_tn("att_o_dw", sv["o"], dh1)
    red.advance((g_o,))
    dq, dk, dv = _attn_bwd(sv["q"], sv["k"], sv["v"], do, 512)
    dk, dv = dk.astype(BF16), dv.astype(BF16)
    dx1, dg = _mm_rowop("att_q_bwd", "rms_bwd_res", [(dq, w["w_q"], NT)], (sv["x1"], dx2), p["norm_xattn_pre"],
                        after=red.after())
    gs["norm_xattn_pre"] = jnp.sum(dg, axis=0)
    g_q = _mm_tn("att_q_dw", sv["hn1"], dq)
    g_k = _mm_tn("att_k_dw", sv["mn"], dk)
    g_v = _mm_tn("att_v_dw", sv["mn"], dv)
    (dg,) = _mm_rowop("att_kv_bwd", "rms_bwd_gain", [(dk, w["w_k"], NT), (dv, w["w_v"], NT)], (mem,), p["norm_mem"])
    gs["norm_mem"] = jnp.sum(dg, axis=0)
    red.add("att", ("w_o", "w_q", "w_k", "w_v"), [g_o, g_q, g_k, g_v])
    dy, dh0, dg = _rowop_mm("mix_out_bwd", "rms_bwd", (sv["h0"], dx1), p["norm_mix_post"], w["w_out"], NT, F32,
                            after=red.after())
    gs["norm_mix_post"] = jnp.sum(dg, axis=0)
    g_out = _mm_tn("mix_out_dw", sv["y"], dh0)
    red.advance((g_out,))
    red.add("out", ("w_out",), [g_out])
    z = sv["z"]
    dz, dgv, dws, dbs = _gmlp_bwd(z, dy, p["gmlp_v_gain"], p["w_spatial"], p["b_spatial_t"], 512)
    gs["gmlp_v_gain"] = jnp.sum(dgv, axis=0)
    gs["w_spatial"] = jnp.sum(dws, axis=0)
    gs["b_spatial"] = jnp.sum(dbs[..., 0], axis=0)
    dz, gs["w_pool"], gs["s_pool"] = _pool_bwd(z, dy, p["w_pool"], p["s_pool"], dz)
    dz, dwdw, gs["b_dw"], gs["conv_ln_g"], gs["conv_ln_b"] = _conv_bwd(
        z, dy, p["w_dw"], p["b_dw"], p["conv_ln_g"], p["conv_ln_b"], dz)
    red.advance((dz,))
    red.small("mixer", _small_grad_arrays(gs, dwdw, norms=False))
    g_in = _mm_tn("mix_in_dw", dz, sv["hn0"], after=red.after())
    red.add("in", ("w_in",), [g_in])
    red.advance((g_in,))
    dx0, dg = _mm_rowop("mix_in_bwd", "rms_bwd_res", [(dz, w["w_in"], NN)], (sv["x0"], dx1), p["norm_mix_pre"],
                        after=red.after())
    gs["norm_mix_pre"] = jnp.sum(dg, axis=0)
    late = {"norms": jnp.concatenate([gs[n] for n in NORM_NAMES], axis=0)}
    if red.layer == 0:
        late["loss"] = red.extra[0]
    red.small("norms", late)
    return dx0


NORM_NAMES = ("norm_mix_pre", "norm_mix_post", "norm_xattn_pre", "norm_mem", "norm_xattn_post", "norm_ffn_pre",
              "norm_ffn_post")
VEC_NAMES = ("s_pool", "b_dw", "conv_ln_g", "conv_ln_b")
SMALL_ARRAYS = ("norms", "gain_bias", "w_spatial", "w_pool", "vecs", "w_dw")


def _small_grad_arrays(gs, dwdw, norms=True):
    out = {"norms": jnp.concatenate([gs[n] for n in NORM_NAMES], axis=0)} if norms else {}
    out.update({"gain_bias": jnp.concatenate([gs["gmlp_v_gain"], gs["b_spatial"]], axis=0),
                "w_spatial": gs["w_spatial"], "w_pool": gs["w_pool"],
                "vecs": jnp.concatenate([gs[n] for n in VEC_NAMES], axis=0), "w_dw": dwdw})
    return out


def _split_small_grads(arrays):
    out = {n: arrays["norms"][k] for k, n in enumerate(NORM_NAMES)}
    out.update({n: arrays["vecs"][k] for k, n in enumerate(VEC_NAMES)})
    out.update(gmlp_v_gain=arrays["gain_bias"][:NH], b_spatial=arrays["gain_bias"][NH:], w_spatial=arrays["w_spatial"],
               w_pool=arrays["w_pool"], w_dw=arrays["w_dw"][:CONV_K])
    return out


def _layer_params(small, l):
    p = {n: small[n][l].reshape(1, -1) for n in ("norm_mix_pre", "norm_mix_post", "s_pool", "b_dw", "conv_ln_g",
                                                   "conv_ln_b", "norm_xattn_pre", "norm_mem", "norm_xattn_post",
                                                   "norm_ffn_pre", "norm_ffn_post")}
    p["gmlp_v_gain"] = small["gmlp_v_gain"][l]
    p["w_spatial"] = small["w_spatial"][l]
    p["b_spatial_t"] = small["b_spatial"][l].T
    p["w_pool"] = small["w_pool"][l]
    p["w_dw"] = jnp.pad(small["w_dw"][l], ((0, 1), (0, 0)))
    return p


def _local_step(x, mem, target, fetch, small, red):
    small = dict(small)
    saved, weights, params = [], [], []
    h = x
    marker = ()
    for l in range(DEPTH):
        w = fetch(l, "in", marker)
        if "taps" in w:
            small["w_dw"] = w.pop("taps")
        p = _layer_params(small, l)
        h, sv = _layer_fwd(h, mem, w, p, functools.partial(fetch, l))
        marker = (h,)
        saved.append(sv)
        weights.append(w)
        params.append(p)
    dh, loss = _loss_head(h, target, 512)
    red.extra = (loss,)
    for l in reversed(range(DEPTH)):
        red.layer = l
        dh = _layer_bwd(dh, mem, weights[l], params[l], saved[l], red)
    return loss, dh


HBM = pl.BlockSpec(memory_space=pltpu.HBM)


def _position():
    return lax.axis_index("x"), lax.axis_index("y"), lax.axis_index("c")


SEM = pl.BlockSpec(memory_space=pltpu.SEMAPHORE)
EFFECT = pltpu.SideEffectType.DATAFLOW_SIDE_EFFECTING
TOKEN = jax.ShapeDtypeStruct((8, LANES), F32)
TOKEN_SPEC = pl.BlockSpec(memory_space=pltpu.VMEM)


def _landing(shape, dtype):
    return pltpu.with_memory_space_constraint(lax.empty(shape, dtype), pltpu.HBM)


def _hbm_shapes(arrays):
    return [pltpu.HBM(a.shape, a.dtype) for a in arrays]


def _block(ref, r, dev):
    return ref.at[pl.ds((4 * dev[0] + 2 * dev[1] + dev[2]) * r, r), :]


def _split_call(name, body, thru, sems_in, after, sems_out, token):
    n = len(thru)
    out_shape = [pltpu.SemaphoreType.DMA(s) for s in sems_out] + _hbm_shapes(thru) + ([TOKEN] if token else [])
    out_specs = [SEM] * len(sems_out) + [HBM] * n + ([TOKEN_SPEC] if token else [])
    return pl.pallas_call(
        body, name=name, in_specs=[HBM] * n + [SEM] * len(sems_in) + [ANY] * len(after),
        out_specs=out_specs, out_shape=out_shape,
        input_output_aliases={i: len(sems_out) + i for i in range(n)},
        compiler_params=pltpu.CompilerParams(has_side_effects=EFFECT),
    )(*thru, *sems_in, *after)


def _place_own(name, srcs, dev, out_dtype, tr):
    n = len(srcs)
    r, cols = srcs[0][0].shape[-2:]
    tr = r if r < 16 else _row_tile(r, tr)
    nb = r // tr

    def body(dev_ref, *refs):
        for a in range(n):
            refs[n + a][...] = refs[a][...].astype(out_dtype)

    in_specs = [pl.BlockSpec((tr, cols), lambda i, d: (i, 0)) if l is None
                else pl.BlockSpec((None, tr, cols), lambda i, d, l=l: (l, i, 0)) for _, l in srcs]
    return pl.pallas_call(
        body, name=name,
        grid_spec=pltpu.PrefetchScalarGridSpec(
            num_scalar_prefetch=1, grid=(nb,), in_specs=in_specs,
            out_specs=[pl.BlockSpec((tr, cols), lambda i, d: (d[0] * nb + i, 0))] * n),
        out_shape=[jax.ShapeDtypeStruct((N_DEV * r, cols), out_dtype)] * n, compiler_params=_cparams(),
    )(dev, *[a for a, _ in srcs])


def _gather_peers(x, y, c):
    return [(1 - x, y, c), (x, 1 - y, c), (1 - x, 1 - y, c), (x, y, 1 - c)]


def _block_rows(land):
    return land.shape[0] // N_DEV


def _near_peers(x, y, c):
    return [(1 - x, y, c), (x, 1 - y, c), (x, y, 1 - c)]


def _relay_route(x, y, c):
    origin = (x + c * (1 - 2 * x), y + (1 - c) * (1 - 2 * y), c)
    target = (x + (1 - c) * (1 - 2 * x), y + c * (1 - 2 * y), c)
    return origin, target


def _same_block_copy(blk, send_sem, recv_sem, to):
    return pltpu.make_async_remote_copy(src_ref=blk, dst_ref=blk, send_sem=send_sem, recv_sem=recv_sem, device_id=to,
                                        device_id_type=MESH)


def _gather_start(name, lands, after):
    n = len(lands)

    def body(*refs):
        lz = refs[:n]
        send_sems, recv_sems = refs[n + len(after)], refs[n + len(after) + 1]
        token = refs[-1]
        x, y, c = _position()
        for a in range(n):
            own = _block(lz[a], _block_rows(lands[a]), (x, y, c))
            for k, to in enumerate(_near_peers(x, y, c)):
                _same_block_copy(own, send_sems.at[k], recv_sems.at[k], to).start()
        token[...] = jnp.zeros_like(token)

    out = _split_call(name, body, list(lands), [], after, [(3,), (3,)], True)
    return out[0], out[1], out[2:2 + n], out[-1]


def _gather_step(name, near, far, fresh, after):
    groups = [g for g in (near and near[0], far and far[0], fresh) if g]
    counts = [len(near[0]) if near else 0, len(far[0]) if far else 0, len(fresh) if fresh else 0]
    n = sum(counts)
    sems_in = ([near[1]] if near else []) + ([far[1]] if far else [])
    sems_out = ([(2,), (2,), (1,), (1,)] if near else []) + ([(1,), (1,)] if far else []) + ([(3,), (3,)] if fresh else [])

    def body(*refs):
        lz = list(refs[:n])
        ins = list(refs[n:n + len(sems_in)])
        outs = list(refs[n + len(sems_in) + len(after):n + len(sems_in) + len(after) + len(sems_out)])
        token = refs[-1]
        x, y, c = _position()
        me, sibling = (x, y, c), (x, y, 1 - c)
        near_lz, far_lz, fresh_lz = (lz[sum(counts[:i]):sum(counts[:i + 1])] for i in range(3))
        neighbours = _near_peers(x, y, c)[:2]
        origin, target = _relay_route(x, y, c)
        diagonal = (1 - x, 1 - y, c)
        if near:
            recv0 = ins.pop(0)
            fsend, frecv, rsend, rrecv = (outs.pop(0) for _ in range(4))
            for a, land in enumerate(near[0]):
                for j, chip in enumerate(neighbours):
                    _same_block_copy(_block(near_lz[a], _block_rows(land), chip), fsend.at[j], recv0.at[j], me).wait_recv()
        if far:
            rrecv_in = ins.pop(0)
            f2send, f2recv = outs.pop(0), outs.pop(0)
            for a, land in enumerate(far[0]):
                _same_block_copy(_block(far_lz[a], _block_rows(land), diagonal), f2send.at[0], rrecv_in.at[0], me).wait_recv()
            for a, land in enumerate(far[0]):
                _same_block_copy(_block(far_lz[a], _block_rows(land), diagonal), f2send.at[0], f2recv.at[0], sibling).start()
        if near:
            for a, land in enumerate(near[0]):
                r = _block_rows(land)
                _same_block_copy(_block(near_lz[a], r, origin), rsend.at[0], rrecv.at[0], target).start()
                for j, chip in enumerate(neighbours):
                    _same_block_copy(_block(near_lz[a], r, chip), fsend.at[j], frecv.at[j], sibling).start()
        if fresh:
            send_sems, recv_sems = outs.pop(0), outs.pop(0)
            for a, land in enumerate(fresh):
                own = _block(fresh_lz[a], _block_rows(land), me)
                for k, to in enumerate(_near_peers(x, y, c)):
                    _same_block_copy(own, send_sems.at[k], recv_sems.at[k], to).start()
        token[...] = jnp.zeros_like(token)

    out = list(_split_call(name, body, [l for g in groups for l in g], sems_in, after, sems_out, True))
    res = {"token": out.pop()}
    if near:
        res.update(fsend=out.pop(0), frecv=out.pop(0), rsend=out.pop(0), rrecv=out.pop(0))
    if far:
        res.update(f2send=out.pop(0), f2recv=out.pop(0))
    if fresh:
        res.update(send=out.pop(0), recv=out.pop(0))
    res["near"], res["far"], res["fresh"] = (out[sum(counts[:i]):sum(counts[:i + 1])] for i in range(3))
    return res


def _gather_finish(name, lands, send_sems, recv_sems, fsend, frecv, rsend, f2send, f2recv, after):
    n = len(lands)

    def body(*refs):
        lz = refs[:n]
        send0, recv0, fsend_ref, frecv_ref, rsend_ref, f2send_ref, f2recv_ref = refs[n:n + 7]
        x, y, c = _position()
        me = (x, y, c)
        near = _near_peers(x, y, c)[:2]
        origin, _ = _relay_route(x, y, c)
        for a in range(n):
            r = _block_rows(lands[a])
            sib = _block(lz[a], r, (x, y, 1 - c))
            _same_block_copy(sib, send0.at[2], recv0.at[2], me).wait_recv()
            for j, chip in enumerate(near):
                blk = _block(lz[a], r, (chip[0], chip[1], 1 - c))
                _same_block_copy(blk, fsend_ref.at[j], frecv_ref.at[j], me).wait_recv()
            far = _block(lz[a], r, (1 - x, 1 - y, 1 - c))
            _same_block_copy(far, f2send_ref.at[0], f2recv_ref.at[0], me).wait_recv()
            own = _block(lz[a], r, me)
            for k in range(3):
                _same_block_copy(own, send0.at[k], recv0.at[k], me).wait_send()
            for j, chip in enumerate(near):
                _same_block_copy(_block(lz[a], r, chip), fsend_ref.at[j], frecv_ref.at[j], me).wait_send()
            _same_block_copy(_block(lz[a], r, origin), rsend_ref.at[0], recv0.at[0], me).wait_send()
            _same_block_copy(_block(lz[a], r, (1 - x, 1 - y, c)), f2send_ref.at[0], f2recv_ref.at[0], me).wait_send()

    return _split_call(name, body, list(lands), [send_sems, recv_sems, fsend, frecv, rsend, f2send, f2recv], after, [],
                       False)


def _sibling_start(name, grads, after):
    n = len(grads)
    lands = [_landing((4, g.shape[0] // N_DEV, D), g.dtype) for g in grads]

    def body(*refs):
        ins, lz = refs[:n], refs[n:2 * n]
        send_sem, recv_sem = refs[2 * n + len(after)], refs[2 * n + len(after) + 1]
        token = refs[-1]
        x, y, c = _position()
        for a in range(n):
            r = grads[a].shape[0] // N_DEV
            for q in range(4):
                pltpu.make_async_remote_copy(
                    src_ref=ins[a].at[pl.ds((2 * q + 1 - c) * r, r), :], dst_ref=lz[a].at[q], send_sem=send_sem.at[0],
                    recv_sem=recv_sem.at[0], device_id=(x, y, 1 - c), device_id_type=MESH).start()
        token[...] = jnp.zeros_like(token)

    out = _split_call(name, body, list(grads) + lands, [], after, [(1,), (1,)], True)
    return out[0], out[1], out[2:2 + n], out[2 + n:2 + 2 * n], out[-1]


def _sibling_finish(name, grads, lands, send_sem, recv_sem, after):
    n = len(grads)

    def body(*refs):
        ins, lz = refs[:n], refs[n:2 * n]
        send_ref, recv_ref = refs[2 * n], refs[2 * n + 1]
        x, y, c = _position()
        for a in range(n):
            r = grads[a].shape[0] // N_DEV
            for q in range(4):
                cp = pltpu.make_async_remote_copy(
                    src_ref=ins[a].at[pl.ds((2 * q + 1 - c) * r, r), :], dst_ref=lz[a].at[q], send_sem=send_ref.at[0],
                    recv_sem=recv_ref.at[0], device_id=(x, y, c), device_id_type=MESH)
                cp.wait_send()
                cp.wait_recv()

    out = _split_call(name, body, list(grads) + list(lands), [send_sem, recv_sem], after, [], False)
    return out[:n], out[n:2 * n]


def _chip_start(name, parts, after):
    n = len(parts)
    lands = [_landing((3,) + p.shape[1:], p.dtype) for p in parts]

    def body(*refs):
        ins, lz = refs[:n], refs[n:2 * n]
        send_sems, recv_sems = refs[2 * n + len(after)], refs[2 * n + len(after) + 1]
        token = refs[-1]
        x, y, c = _position()
        for a in range(n):
            for j, chip in enumerate(_gather_peers(x, y, c)[:3]):
                pltpu.make_async_remote_copy(
                    src_ref=ins[a].at[2 * chip[0] + chip[1]], dst_ref=lz[a].at[j], send_sem=send_sems.at[j],
                    recv_sem=recv_sems.at[j], device_id=chip, device_id_type=MESH).start()
        token[...] = jnp.zeros_like(token)

    out = _split_call(name, body, list(parts) + lands, [], after, [(3,), (3,)], True)
    return out[0], out[1], out[2:2 + n], out[2 + n:2 + 2 * n], out[-1]


def _chip_finish(name, parts, lands, send_sems, recv_sems, after):
    n = len(parts)

    def body(*refs):
        ins, lz = refs[:n], refs[n:2 * n]
        send_ref, recv_ref = refs[2 * n], refs[2 * n + 1]
        me = _position()
        for a in range(n):
            for j in range(3):
                cp = pltpu.make_async_remote_copy(
                    src_ref=ins[a].at[j], dst_ref=lz[a].at[j], send_sem=send_ref.at[j], recv_sem=recv_ref.at[j],
                    device_id=me, device_id_type=MESH)
                cp.wait_send()
                cp.wait_recv()

    out = _split_call(name, body, list(parts) + list(lands), [send_sems, recv_sems], after, [], False)
    return out[:n], out[n:2 * n]


def _other_devices(x, y, c):
    return [(x + (k >> 2 & 1) * (1 - 2 * x), y + (k >> 1 & 1) * (1 - 2 * y), c + (k & 1) * (1 - 2 * c))
            for k in range(1, N_DEV)]


def _broadcast_start(name, arrays, after):
    n = len(arrays)
    lands = [_landing((N_DEV,) + a.shape, a.dtype) for a in arrays]

    def body(*refs):
        ins, lz = refs[:n], refs[n:2 * n]
        send_sems, recv_sems = refs[2 * n + len(after)], refs[2 * n + len(after) + 1]
        token = refs[-1]
        x, y, c = _position()
        for a in range(n):
            for k, peer in enumerate(_other_devices(x, y, c)):
                pltpu.make_async_remote_copy(
                    src_ref=ins[a], dst_ref=lz[a].at[4 * x + 2 * y + c], send_sem=send_sems.at[k],
                    recv_sem=recv_sems.at[k], device_id=peer, device_id_type=MESH).start()
        token[...] = jnp.zeros_like(token)

    out = _split_call(name, body, list(arrays) + lands, [], after, [(N_DEV - 1,), (N_DEV - 1,)], True)
    return out[0], out[1], out[2:2 + n], out[2 + n:2 + 2 * n], out[-1]


def _broadcast_finish(name, arrays, lands, send_sems, recv_sems, after):
    n = len(arrays)

    def body(*refs):
        ins, lz = refs[:n], refs[n:2 * n]
        send_ref, recv_ref = refs[2 * n], refs[2 * n + 1]
        x, y, c = _position()
        for a in range(n):
            for k, peer in enumerate(_other_devices(x, y, c)):
                cp = pltpu.make_async_remote_copy(
                    src_ref=ins[a], dst_ref=lz[a].at[4 * peer[0] + 2 * peer[1] + peer[2]], send_sem=send_ref.at[k],
                    recv_sem=recv_ref.at[k], device_id=(x, y, c), device_id_type=MESH)
                cp.wait_send()
                cp.wait_recv()

    out = _split_call(name, body, list(arrays) + list(lands), [send_sems, recv_sems], after, [], False)
    return out[:n], out[n:2 * n]


def _row_tile(r, target):
    return max(t for t in range(16, min(r, target) + 1, 16) if r % t == 0)


def _chip_partial(name, grad, got, c, tr):
    r = grad.shape[0] // N_DEV
    tr = _row_tile(r, tr)
    g4 = grad.reshape(4, 2, r, D)

    def body(c_ref, g_ref, s_ref, o_ref):
        o_ref[...] = (g_ref[...].astype(F32) + s_ref[...].astype(F32)).astype(BF16)

    return pl.pallas_call(
        body, name=name,
        grid_spec=pltpu.PrefetchScalarGridSpec(
            num_scalar_prefetch=1, grid=(4, r // tr),
            in_specs=[pl.BlockSpec((None, None, tr, D), lambda q, i, c_ref: (q, c_ref[0], i, 0)),
                      pl.BlockSpec((None, tr, D), lambda q, i, c_ref: (q, i, 0))],
            out_specs=pl.BlockSpec((None, tr, D), lambda q, i, c_ref: (q, i, 0))),
        out_shape=jax.ShapeDtypeStruct((4, r, D), BF16), compiler_params=_cparams(),
    )(c, g4, got)


class _WeightGather:
    def __init__(self, groups):
        self.groups = list(groups)
        self.index = {key: i for i, (key, _, _) in enumerate(groups)}
        self.state = [None] * len(groups)
        self.token = ()
        for i in range(min(2, len(groups))):
            self._start(i)

    def _tag(self, i):
        return "%s_%d" % self.groups[i][0][::-1]

    def _start(self, i):
        send, recv, lz, tok = _gather_start("gather_start_" + self._tag(i), self.groups[i][2], self.token)
        self.state[i] = dict(send=send, recv=recv, lands=lz)
        self.token = (tok,)

    def _step(self, name, near, far, fresh, marker):
        exists = lambda i: i is not None and i < len(self.groups)
        near, far, fresh = (i if exists(i) else None for i in (near, far, fresh))
        res = _gather_step(
            name, None if near is None else (self.state[near]["lands"], self.state[near]["recv"]),
            None if far is None else (self.state[far]["lands"], self.state[far]["rrecv"]),
            None if fresh is None else self.groups[fresh][2], tuple(marker) + self.token)
        self.token = (res["token"],)
        if near is not None:
            self.state[near].update(lands=res["near"], fsend=res["fsend"], frecv=res["frecv"], rsend=res["rsend"],
                                    rrecv=res["rrecv"])
        if far is not None:
            self.state[far].update(lands=res["far"], f2send=res["f2send"], f2recv=res["f2recv"])
        if fresh is not None:
            self.state[fresh] = dict(send=res["send"], recv=res["recv"], lands=res["fresh"])

    def fetch(self, layer, group, marker):
        k = self.index[(layer, group)]
        if k == 0:
            self._step("gather_step_first", 0, None, None, marker)
        self._step("gather_step_" + self._tag(k), k + 1, k, k + 2, marker)
        st = self.state[k]
        lz = _gather_finish("gather_finish_" + self._tag(k), st["lands"], st["send"], st["recv"], st["fsend"],
                            st["frecv"], st["rsend"], st["f2send"], st["f2recv"], self.token)
        self.state[k] = None
        return dict(zip(self.groups[k][1], lz))


class _GradReduce:
    def __init__(self, core, chip):
        self.core, self.chip = core, chip
        self.layer = None
        self.token = ()
        self.at_sibling, self.at_chips = [], []
        self.extra, self.smalls = (), {}

    def after(self):
        return self.token

    def add(self, group, names, grads):
        tag = "%s_%d" % (group, self.layer)
        send, recv, grads, lands, tok = _sibling_start("grad_sibling_start_" + tag, grads, self.token)
        self.at_sibling.append((tag, [(self.layer, n) for n in names], send, recv, grads, lands))
        self.token = (tok,)

    def advance(self, marker):
        for tag, keys, send, recv, grads, lands in self.at_sibling:
            grads, lands = _sibling_finish("grad_sibling_finish_" + tag, grads, lands, send, recv, marker)
            parts = [_chip_partial("chip_partial_%d_%s" % key, g, got, self.core, 512)
                     for key, g, got in zip(keys, grads, lands)]
            send, recv, parts, lands, tok = _chip_start("grad_chip_start_" + tag, parts, ())
            self.at_chips.append([tag, keys, send, recv, parts, lands])
            self.token = (tok,)
        self.at_sibling = []

    def small(self, part, arrays):
        keys = list(arrays)
        send, recv, own, slots, tok = _broadcast_start(
            "small_grads_start_%d_%s" % (self.layer, part), [arrays[k] for k in keys], self.token)
        self.smalls.setdefault(self.layer, []).append((part, keys, send, recv, own, slots))
        self.token = (tok,)

    def small_finish(self, layer, marker):
        mine, theirs = {}, {}
        for part, keys, send, recv, own, slots in self.smalls[layer]:
            own, slots = _broadcast_finish("small_grads_finish_%d_%s" % (layer, part), own, slots, send, recv, marker)
            mine.update(zip(keys, own))
            theirs.update(zip(keys, slots))
        return mine, theirs

    def collect(self, key, marker):
        for entry in self.at_chips:
            tag, keys, send, recv, parts, lands = entry
            if key in keys:
                if send is not None:
                    parts, lands = _chip_finish("grad_chip_finish_" + tag, parts, lands, send, recv, marker)
                    entry[2:] = [None, None, parts, lands]
                i = keys.index(key)
                return parts[i], lands[i]
        raise KeyError(key)


def _adamw_math(w, g, m, v):
    m = ADAM_B1 * m + (1.0 - ADAM_B1) * g
    v = ADAM_B2 * v + (1.0 - ADAM_B2) * jnp.square(g)
    m_hat = m / (1.0 - ADAM_B1 ** ADAM_STEP)
    v_hat = v / (1.0 - ADAM_B2 ** ADAM_STEP)
    delta = -ADAM_LR * (m_hat / (jnp.sqrt(v_hat) + ADAM_EPS) + ADAM_WD * w)
    return delta, m, v


def _adamw_small(wts, mom_m, mom_v, own, gathered, loss_own, loss_gathered, dev):
    names = SMALL
    nw = len(names)
    na = len(SMALL_ARRAYS)

    def body(dev_ref, *refs):
        w_refs, m_refs, v_refs = (dict(zip(names, refs[i * nw:(i + 1) * nw])) for i in range(3))
        own_refs = refs[3 * nw:3 * nw + DEPTH * na]
        g_refs = refs[3 * nw + DEPTH * na:3 * nw + 2 * DEPTH * na]
        loss_own_ref, loss_got_ref = refs[3 * nw + 2 * DEPTH * na:3 * nw + 2 * DEPTH * na + 2]
        outs = refs[3 * nw + 2 * DEPTH * na + 2:]
        g_out, d_out, m_out, v_out = (dict(zip(names, outs[i * nw:(i + 1) * nw])) for i in range(4))
        me = dev_ref[0]

        loss = None
        for d in range(N_DEV):
            for b in range(loss_own.shape[0]):
                term = jnp.where(me == d, loss_own_ref[b], loss_got_ref[d, b])
                loss = term if loss is None else loss + term
        outs[4 * nw][...] = loss

        def update(name, at, g):
            g_out[name][at] = g
            d_out[name][at], m_out[name][at], v_out[name][at] = _adamw_math(
                w_refs[name][at], g, m_refs[name][at], v_refs[name][at])

        for l in range(DEPTH):
            mine = dict(zip(SMALL_ARRAYS, own_refs[l * na:(l + 1) * na]))
            got = dict(zip(SMALL_ARRAYS, g_refs[l * na:(l + 1) * na]))

            def total(key, at):
                acc = None
                for d in range(N_DEV):
                    term = jnp.where(me == d, mine[key][at] if at else mine[key][...], got[key][(d,) + at])
                    acc = term if acc is None else acc + term
                return acc

            row = (slice(l, l + 1),)
            for k, name in enumerate(NORM_NAMES):
                update(name, row, total("norms", (slice(k, k + 1),)))
            for k, name in enumerate(VEC_NAMES):
                update(name, row, total("vecs", (slice(k, k + 1),)))
            update("gmlp_v_gain", (l,), total("gain_bias", (slice(0, NH),)))
            update("b_spatial", (l,), total("gain_bias", (slice(NH, 2 * NH),)))
            update("w_spatial", (l,), total("w_spatial", ()))
            update("w_pool", (l,), total("w_pool", ()))
            update("w_dw", (l,), total("w_dw", (slice(0, CONV_K),)))

    args = [src[n] for src in (wts, mom_m, mom_v) for n in names]
    args += [src[l][k] for src in (own, gathered) for l in range(DEPTH) for k in SMALL_ARRAYS]
    args += [loss_own, loss_gathered]
    outs = pl.pallas_call(
        body, name="adamw_small",
        in_specs=[pl.BlockSpec(memory_space=pltpu.SMEM)] + [pl.BlockSpec(memory_space=pltpu.VMEM)] * len(args),
        out_shape=[jax.ShapeDtypeStruct(wts[n].shape, F32) for _ in range(4) for n in names]
        + [jax.ShapeDtypeStruct((8, LANES), F32)],
        compiler_params=_cparams(),
    )(dev, *args)
    return tuple(dict(zip(names, outs[i * nw:(i + 1) * nw])) for i in range(4)) + (outs[4 * nw],)


def _adamw_layers(name, w, reduced, m, v, chip, tr, transposed=False, after=()):
    nl, r, cdim = w.shape
    tr = _row_tile(r, tr)
    nb = r // tr

    def body(q_ref, w_ref, p0_ref, g0_ref, p1_ref, g1_ref, m_ref, v_ref, *rest):
        g_ref, d_ref, nm_ref, nv_ref = rest[len(after):]

        def total(p_ref, got_ref):
            acc = p_ref[...].astype(F32)
            for j in range(3):
                acc = acc + got_ref[j].astype(F32)
            return acc

        g = jnp.where(pl.program_id(0) == 0, total(p0_ref, g0_ref), total(p1_ref, g1_ref))
        if transposed:
            g = g.T
        g_ref[...] = g
        d_ref[...], nm_ref[...], nv_ref[...] = _adamw_math(w_ref[...], g, m_ref[...], v_ref[...])

    blk = pl.BlockSpec((None, tr, cdim), lambda l, i, q: (l, i, 0))
    first = lambda l, i: i * (1 - l) + (nb - 1) * l
    second = lambda l, i: i * l
    if transposed:
        gshape = (cdim, tr)
        at = lambda lead, i: (lead, 0, i)
    else:
        gshape = (tr, cdim)
        at = lambda lead, i: (lead, i, 0)
    specs = [blk,
             pl.BlockSpec((None,) + gshape, lambda l, i, q: at(q[0], first(l, i))),
             pl.BlockSpec((3,) + gshape, lambda l, i, q: at(0, first(l, i))),
             pl.BlockSpec((None,) + gshape, lambda l, i, q: at(q[0], second(l, i))),
             pl.BlockSpec((3,) + gshape, lambda l, i, q: at(0, second(l, i))), blk, blk] + [ANY] * len(after)
    shape = jax.ShapeDtypeStruct((nl, r, cdim), F32)
    return pl.pallas_call(
        body, name=name,
        grid_spec=pltpu.PrefetchScalarGridSpec(num_scalar_prefetch=1, grid=(nl, nb), in_specs=specs, out_specs=[blk] * 4),
        out_shape=[shape] * 4, compiler_params=_cparams(),
    )(chip, w, *reduced[0], *reduced[1], m, v, *after)


def _to_rows(name, a):
    return jnp.swapaxes(a, 1, 2) if name == "w_in" else a


def _place_own_transposed(name, srcs, dev, out_dtype, tc):
    n = len(srcs)
    kdim, cdim = srcs[0][0].shape[-2:]

    def body(dev_ref, *refs):
        for a in range(n):
            refs[n + a][...] = refs[a][...].T.astype(out_dtype)

    return pl.pallas_call(
        body, name=name,
        grid_spec=pltpu.PrefetchScalarGridSpec(
            num_scalar_prefetch=1, grid=(kdim // tc,),
            in_specs=[pl.BlockSpec((None, tc, cdim), lambda i, d, l=l: (l, i, 0)) for _, l in srcs],
            out_specs=[pl.BlockSpec((cdim, tc), lambda i, d: (d[0], i))] * n),
        out_shape=[jax.ShapeDtypeStruct((N_DEV * cdim, kdim), out_dtype)] * n, compiler_params=_cparams(),
    )(dev, *[a for a, _ in srcs])


def _pack(arrays, rows):
    flat = jnp.concatenate([a.reshape(-1) for a in arrays])
    return jnp.pad(flat, (0, rows * D - flat.shape[0])).reshape(rows, D)


def _rows_for(shapes, mult=8):
    total = 0
    for shp in shapes:
        size = 1
        for dim in shp:
            size *= dim
        total += size
    return -(-total // (mult * D)) * mult


def kernel(x, mem, norm_mix_pre, norm_mix_post, w_in, w_out, gmlp_v_gain, w_spatial, b_spatial, w_pool, s_pool, w_dw, b_dw, conv_ln_g, conv_ln_b, norm_xattn_pre, norm_mem, norm_xattn_post, w_q, w_k, w_v, w_o, norm_ffn_pre, norm_ffn_post, w_up, w_down, loss_target, m_norm_mix_pre, m_norm_mix_post, m_w_in, m_w_out, m_gmlp_v_gain, m_w_spatial, m_b_spatial, m_w_pool, m_s_pool, m_w_dw, m_b_dw, m_conv_ln_g, m_conv_ln_b, m_norm_xattn_pre, m_norm_mem, m_norm_xattn_post, m_w_q, m_w_k, m_w_v, m_w_o, m_norm_ffn_pre, m_norm_ffn_post, m_w_up, m_w_down, v_norm_mix_pre, v_norm_mix_post, v_w_in, v_w_out, v_gmlp_v_gain, v_w_spatial, v_b_spatial, v_w_pool, v_s_pool, v_w_dw, v_b_dw, v_conv_ln_g, v_conv_ln_b, v_norm_xattn_pre, v_norm_mem, v_norm_xattn_post, v_w_q, v_w_k, v_w_v, v_w_o, v_norm_ffn_pre, v_norm_ffn_post, v_w_up, v_w_down):
    args = dict(locals())
    wts = {n: args[n] for n in WEIGHTS}
    mom_m = {n: args["m_" + n] for n in WEIGHTS}
    mom_v = {n: args["v_" + n] for n in WEIGHTS}
    xi, yi, ci = _position()
    me = 4 * xi + 2 * yi + ci

    dev = jnp.reshape(me, (1,)).astype(jnp.int32)
    lands = {}
    for call, names, tr in (("place_in", ("w_in",), 256), ("place_att", ("w_out", "w_q", "w_k", "w_v", "w_o"), 64),
                            ("place_up", ("w_up",), 256), ("place_down", ("w_down",), 256)):
        srcs = [(_to_rows(n, wts[n]), l) for l in range(DEPTH) for n in names]
        placed = (_place_own_transposed if names == ("w_up",) else _place_own)(call, srcs, dev, BF16, tr)
        lands.update(zip([(l, n) for l in range(DEPTH) for n in names], placed))
    (lands[(0, "taps")],) = _place_own("place_taps", [(_pack([w_dw], _rows_for([w_dw.shape])), None)], dev, F32, 8)
    groups = []
    for l in range(DEPTH):
        for group, names in GATHER_GROUPS:
            if (l, group) == (0, "in"):
                names = names + ("taps",)
            groups.append(((l, group), names, [lands[(l, n)] for n in names]))
    gather = _WeightGather(groups)

    def fetch(layer, group, marker):
        w = gather.fetch(layer, group, marker)
        if "taps" in w:
            blocks = w["taps"].reshape(N_DEV, -1)[:, :w_dw.size].reshape((N_DEV,) + w_dw.shape)
            w["taps"] = jnp.moveaxis(blocks, 0, 2).reshape(DEPTH, CONV_K, CW)
        return w

    reduce = _GradReduce(jnp.reshape(ci, (1,)).astype(jnp.int32), jnp.reshape(2 * xi + yi, (1,)).astype(jnp.int32))
    small = {n: wts[n] for n in SMALL if n != "w_dw"}
    _, dx = _local_step(x[0], mem[0], loss_target[0], fetch, small, reduce)
    reduce.advance((dx,))

    grad_w, delta, new_m, new_v = {}, {}, {}, {}
    marker = (dx,) + tuple(reduce.after())
    for n in UPDATE_ORDER:
        reduced = [reduce.collect((l, n), marker) for l in range(DEPTH)]
        outs = _adamw_layers("adamw_" + n, _to_rows(n, wts[n]), reduced, _to_rows(n, mom_m[n]), _to_rows(n, mom_v[n]),
                             reduce.chip, 256, transposed=n == "w_up", after=marker)
        grad_w[n], delta[n], new_m[n], new_v[n] = (_to_rows(n, o) for o in outs)
        marker = (outs[1],)

    own, slots = [None] * DEPTH, [None] * DEPTH
    for l in reversed(range(DEPTH)):
        own[l], slots[l] = reduce.small_finish(l, marker)
        if l == 0:
            loss_own, loss_slots = own[l].pop("loss"), slots[l].pop("loss")
    shard_cols = CW // N_DEV
    for l in range(DEPTH):
        own[l]["w_dw"] = lax.dynamic_slice_in_dim(own[l]["w_dw"], me * shard_cols, shard_cols, axis=1)
        slots[l]["w_dw"] = lax.dynamic_slice_in_dim(slots[l]["w_dw"], me * shard_cols, shard_cols, axis=2)
    *small_out, loss_tile = _adamw_small(wts, mom_m, mom_v, own, slots, loss_own, loss_slots, dev)
    for dst, src in zip((grad_w, delta, new_m, new_v), small_out):
        dst.update(src)

    return (loss_tile[0, 0], dx[None], *[grad_w[n] for n in WEIGHTS], *[delta[n] for n in WEIGHTS],
            *[new_m[n] for n in WEIGHTS], *[new_v[n] for n in WEIGHTS])
```

```python
import functools

import jax
import jax.numpy as jnp
from jax import lax
from jax.experimental import pallas as pl
from jax.experimental.pallas import tpu as pltpu

F32 = jnp.float32
BF16 = jnp.bfloat16

D = 2048
GW = 1024
PW = 512
CW = 512
HD = 128
NH = 8
NG = 4
POOL_WINDOWS = (2, 4, 8, 16)
CONV_K = 31
IN_COLS = 2 * GW + PW + 2 * CW
DFF = 4 * D
XH = 4
XHD = D // XH
ATT_SCALE = XHD ** -0.5
RMS_EPS = 1e-6
LN_EPS = 1e-5
DEPTH = 2
N_DEV = 8

ADAM_LR = 0.001
ADAM_B1 = 0.9
ADAM_B2 = 0.999
ADAM_EPS = 1e-08
ADAM_WD = 0.01
ADAM_STEP = 10

LANES = 128
CONV_HALO = 32
POOL_HALO = 16
ROW_TILE = 128
VMEM_LIMIT = 60 * 1024 * 1024

MESH = pl.DeviceIdType.MESH
NT = (((1,), (1,)), ((), ()))
NN = (((1,), (0,)), ((), ()))
TN = (((0,), (0,)), ((), ()))

BIG = ("w_out", "w_q", "w_k", "w_v", "w_o", "w_up", "w_down", "w_in")
UPDATE_ORDER = ("w_down", "w_up", "w_o", "w_q", "w_k", "w_v", "w_out", "w_in")
GATHER_GROUPS = (("in", ("w_in",)), ("out", ("w_out",)), ("att", ("w_q", "w_k", "w_v", "w_o")), ("up", ("w_up",)),
                 ("down", ("w_down",)))
SMALL = ("norm_mix_pre", "norm_mix_post", "gmlp_v_gain", "w_spatial", "b_spatial", "w_pool", "s_pool",
         "w_dw", "b_dw", "conv_ln_g", "conv_ln_b", "norm_xattn_pre", "norm_mem", "norm_xattn_post",
         "norm_ffn_pre", "norm_ffn_post")
WEIGHTS = ("norm_mix_pre", "norm_mix_post", "w_in", "w_out", "gmlp_v_gain", "w_spatial", "b_spatial", "w_pool",
           "s_pool", "w_dw", "b_dw", "conv_ln_g", "conv_ln_b", "norm_xattn_pre", "norm_mem", "norm_xattn_post",
           "w_q", "w_k", "w_v", "w_o", "norm_ffn_pre", "norm_ffn_post", "w_up", "w_down")


def _cparams():
    return pltpu.CompilerParams(vmem_limit_bytes=VMEM_LIMIT)


def _dot(a, b, dims):
    return lax.dot_general(a, b, dims, preferred_element_type=F32)


def _rms(x, g):
    y = x * lax.rsqrt(jnp.mean(x * x, axis=-1, keepdims=True) + RMS_EPS)
    return y * g


def _rms_bwd(x, g, dy):
    r = lax.rsqrt(jnp.mean(x * x, axis=-1, keepdims=True) + RMS_EPS)
    xh = x * r
    t = dy * g
    dx = r * (t - xh * jnp.mean(t * xh, axis=-1, keepdims=True))
    return dx, jnp.sum(dy * xh, axis=0, keepdims=True)


def _gelu(x):
    cdf = 0.5 * (1.0 + jnp.tanh(0.7978845608028654 * (x + 0.044715 * (x * x * x))))
    return x * cdf


def _layer_norm(x, g, b=None):
    mu = jnp.mean(x, axis=-1, keepdims=True)
    xc = x - mu
    var = jnp.mean(xc * xc, axis=-1, keepdims=True)
    y = xc * lax.rsqrt(var + LN_EPS) * g
    return y if b is None else y + b


def _sigmoid(x):
    return 1.0 / (1.0 + jnp.exp(-x))


def _gmlp_rows(zu, zv, gv):
    return _gelu(zu), _layer_norm(_gelu(zv), gv)


def _glu(cv, cg):
    return cv * _sigmoid(cg)


def _ln_silu(h, g, b):
    y = _layer_norm(h, g, b)
    return y * _sigmoid(y)


ANY = pl.BlockSpec(memory_space=pl.ANY)


ROWS_TILE = 256
COLS_TILE = 512
DW_TILE = 512
RESIDENT_K = 2048
STREAM_K_TILE = 1024
STREAM_ROWS = 512


def _k_tiles(kdim):
    if kdim <= RESIDENT_K:
        return ROWS_TILE, kdim
    return STREAM_ROWS, max(t for t in range(LANES, STREAM_K_TILE + 1, LANES) if kdim % t == 0)


RING_SLOTS = 3


def _aligned(offset, multiple):
    return offset if isinstance(offset, int) else pl.multiple_of(offset, multiple)


def _ring_fetch(block_of, buf, sems, step, total):
    def copy(t):
        slot = t % RING_SLOTS
        return pltpu.make_async_copy(block_of(t), buf.at[slot], sems.at[slot])

    @pl.when(step == 0)
    def _():
        for t in range(min(RING_SLOTS - 1, total)):
            copy(t).start()

    @pl.when(step + (RING_SLOTS - 1) < total)
    def _():
        copy(step + (RING_SLOTS - 1)).start()

    copy(step).wait()
    return buf.at[step % RING_SLOTS]


def _rowop_mm(name, kind, rows, g, w, dims, out_dtype, u=None, after=()):
    s = rows[0].shape[0]
    n = w.shape[0] if dims == NT else w.shape[1]
    tm, tn = min(ROWS_TILE, s), min(COLS_TILE, n)
    ni = s // tm
    bwd = kind == "rms_bwd"

    def rows_body(*refs):
        refs = list(refs)
        row_refs = [refs.pop(0) for _ in rows]
        g_ref = refs.pop(0)
        del refs[:len(after)]
        if bwd:
            a, dg = _rms_bwd(row_refs[0][...], g_ref[...], row_refs[1][...])
            refs[1][0] = dg
        else:
            a = _rms(row_refs[0][...], g_ref[...])
        refs[0][...] = a.astype(BF16)

    row_spec = pl.BlockSpec((tm, D), lambda i: (i, 0))
    res = pl.pallas_call(
        rows_body, name=name + "_rows", grid=(ni,),
        in_specs=[row_spec] * len(rows) + [pl.BlockSpec((1, D), lambda i: (0, 0))] + [ANY] * len(after),
        out_specs=[row_spec] + ([pl.BlockSpec((1, 1, D), lambda i: (i, 0, 0))] if bwd else []),
        out_shape=[jax.ShapeDtypeStruct((s, D), BF16)] + ([jax.ShapeDtypeStruct((ni, 1, D), F32)] if bwd else []),
        compiler_params=_cparams(),
    )(*rows, g, *after)
    a = res[0]

    nj = n // tn

    def body(a_ref, w_hbm, *rest):
        wbuf, sems = rest[-2:]
        j = pl.program_id(0)

        def block_of(t):
            cols = pl.ds(_aligned(t * tn, tn), tn)
            return w_hbm.at[cols, :] if dims == NT else w_hbm.at[:, cols]

        w_ref = _ring_fetch(block_of, wbuf, sems, j, nj)
        acc = _dot(a_ref[...], w_ref[...], dims)
        if u is not None:
            acc = acc * (2.0 * jnp.maximum(rest[0][...].astype(F32), 0.0))
        rest[-3][...] = acc.astype(out_dtype)

    tile = pl.BlockSpec((s, tn), lambda j: (0, j))
    out = pl.pallas_call(
        body, name=name, grid=(nj,),
        in_specs=[pl.BlockSpec((s, D), lambda j: (0, 0)), ANY] + ([tile] if u is not None else []),
        out_specs=tile, out_shape=jax.ShapeDtypeStruct((s, n), out_dtype),
        scratch_shapes=[pltpu.VMEM((RING_SLOTS,) + ((tn, D) if dims == NT else (D, tn)), w.dtype),
                        pltpu.SemaphoreType.DMA((RING_SLOTS,))],
        compiler_params=_cparams(),
    )(a, w, *([u] if u is not None else []))
    return (out, *res)


def _mm_rowop(name, kind, pairs, rows, g, relu2=False, after=()):
    s, kdim = pairs[0][0].shape
    tm, tk = _k_tiles(kdim)
    tm = min(tm, s)
    ni, nk = s // tm, kdim // tk
    npair = len(pairs)
    ring = npair == 1 and nk > 1 and pairs[0][2] == NN

    def body(*refs):
        refs = list(refs)
        a_refs = [refs.pop(0) for _ in range(npair)]
        w_refs = [refs.pop(0) for _ in range(npair)]
        row_refs = [refs.pop(0) for _ in rows]
        g_ref = refs.pop(0)
        del refs[:len(after)]
        if ring:
            sems, wbuf = refs.pop(), refs.pop()
        acc = refs.pop()
        outs = refs
        k = pl.program_id(1)
        if ring:
            w_hbm = w_refs[0]
            w_refs = [_ring_fetch(lambda t: w_hbm.at[pl.ds(_aligned((t % nk) * tk, tk), tk), :], wbuf, sems,
                                  pl.program_id(0) * nk + k, ni * nk)]

        @pl.when(k == 0)
        def _():
            acc[...] = jnp.zeros_like(acc)

        for a_ref, w_ref, (_, _, dims) in zip(a_refs, w_refs, pairs):
            a = a_ref[...]
            if relu2:
                a = jnp.square(jnp.maximum(a.astype(F32), 0.0))
            acc[...] += _dot(a.astype(BF16), w_ref[...], dims)

        @pl.when(k == nk - 1)
        def _():
            h = acc[...]
            if kind == "rms_res":
                outs[0][...] = row_refs[0][...] + _rms(h, g_ref[...])
                outs[1][...] = h
            else:
                dx, dg = _rms_bwd(row_refs[0][...], g_ref[...], h)
                if kind == "rms_bwd_res":
                    outs[0][...] = row_refs[1][...] + dx
                    outs[1][0] = dg
                else:
                    outs[0][0] = dg

    row_spec = pl.BlockSpec((tm, D), lambda i, k: (i, 0))
    dg_shape = jax.ShapeDtypeStruct((ni, 1, D), F32)
    dg_spec = pl.BlockSpec((1, 1, D), lambda i, k: (i, 0, 0))
    in_specs = [pl.BlockSpec((tm, tk), lambda i, k: (i, k))] * npair
    for _, _, dims in pairs:
        in_specs.append(ANY if ring else pl.BlockSpec((tk, D), lambda i, k: (k, 0)) if dims == NN
                        else pl.BlockSpec((D, tk), lambda i, k: (0, k)))
    in_specs += [row_spec] * len(rows) + [pl.BlockSpec((1, D), lambda i, k: (0, 0))] + [ANY] * len(after)
    if kind == "rms_res":
        out_shape = [jax.ShapeDtypeStruct((s, D), F32)] * 2
        out_specs = [row_spec, row_spec]
    elif kind == "rms_bwd_res":
        out_shape = [jax.ShapeDtypeStruct((s, D), F32), dg_shape]
        out_specs = [row_spec, dg_spec]
    else:
        out_shape = [dg_shape]
        out_specs = [dg_spec]
    scratch = [pltpu.VMEM((tm, D), F32)]
    if ring:
        scratch += [pltpu.VMEM((RING_SLOTS, tk, D), pairs[0][1].dtype), pltpu.SemaphoreType.DMA((RING_SLOTS,))]
    return pl.pallas_call(
        body, name=name, grid=(ni, nk), in_specs=in_specs, out_specs=out_specs, out_shape=out_shape,
        scratch_shapes=scratch, compiler_params=_cparams(),
    )(*[p[0] for p in pairs], *[p[1] for p in pairs], *rows, g, *after)


def _mm_tn(name, a, gmat, relu2=False, after=()):
    s, m = a.shape
    tm = min(DW_TILE, m)
    ni = m // tm

    def body(a_hbm, g_ref, *rest):
        o_ref, abuf, sems = rest[len(after):]
        a_ref = _ring_fetch(lambda t: a_hbm.at[:, pl.ds(_aligned(t * tm, tm), tm)], abuf, sems, pl.program_id(0), ni)
        av = a_ref[...]
        if relu2:
            av = jnp.square(jnp.maximum(av.astype(F32), 0.0))
        o_ref[...] = _dot(av.astype(BF16), g_ref[...], TN).astype(BF16)

    return pl.pallas_call(
        body, name=name, grid=(ni,),
        in_specs=[ANY, pl.BlockSpec((s, D), lambda i: (0, 0))] + [ANY] * len(after),
        out_specs=pl.BlockSpec((tm, D), lambda i: (i, 0)),
        out_shape=jax.ShapeDtypeStruct((m, D), BF16),
        scratch_shapes=[pltpu.VMEM((RING_SLOTS, s, tm), a.dtype), pltpu.SemaphoreType.DMA((RING_SLOTS,))],
        compiler_params=_cparams(),
    )(a, gmat, *after)


def _tril():
    r = lax.broadcasted_iota(jnp.int32, (HD, HD), 0)
    c = lax.broadcasted_iota(jnp.int32, (HD, HD), 1)
    return (c <= r).astype(F32)


def _gmlp_fwd(z, gv, ws, bst, tb):
    s = z.shape[0]
    tb = min(tb, s)

    def body(zu_ref, zv_ref, gv_ref, ws_ref, bst_ref, y_ref):
        tril = _tril()
        for h in range(NH):
            cols = slice(h * HD, (h + 1) * HD)
            u, vln = _gmlp_rows(zu_ref[:, cols], zv_ref[:, cols], gv_ref[h:h + 1, :])
            wm = (ws_ref[h] * tril).astype(BF16)
            vb = vln.astype(BF16)
            for c in range(tb // HD):
                rws = slice(c * HD, (c + 1) * HD)
                mixed = _dot(wm, vb[rws], NN) + bst_ref[:, h:h + 1]
                y_ref[rws, cols] = (u[rws] * mixed).astype(BF16)

    return pl.pallas_call(
        body, name="gmlp_fwd", grid=(s // tb,),
        in_specs=[pl.BlockSpec((tb, GW), lambda i: (i, 0)), pl.BlockSpec((tb, GW), lambda i: (i, 1)),
                  pl.BlockSpec((NH, HD), lambda i: (0, 0)), pl.BlockSpec((NH, HD, HD), lambda i: (0, 0, 0)),
                  pl.BlockSpec((HD, NH), lambda i: (0, 0))],
        out_specs=pl.BlockSpec((tb, GW), lambda i: (i, 0)),
        out_shape=jax.ShapeDtypeStruct((s, D), BF16), compiler_params=_cparams(),
    )(z, z, gv, ws, bst)


def _gmlp_bwd(z, dy, gv, ws, bst, tb):
    s = z.shape[0]
    tb = min(tb, s)
    nb = s // tb

    def body(zu_ref, zv_ref, dy_ref, gv_ref, ws_ref, bst_ref, dz_ref, dgv_ref, dws_ref, db_ref):
        tril = _tril()
        for h in range(NH):
            cols = slice(h * HD, (h + 1) * HD)
            (u, vln), vjp = jax.vjp(_gmlp_rows, zu_ref[:, cols], zv_ref[:, cols], gv_ref[h:h + 1, :])
            wmf = ws_ref[h] * tril
            wm = wmf.astype(BF16)
            wmt = wmf.T.astype(BF16)
            vb = vln.astype(BF16)
            dws = jnp.zeros((HD, HD), F32)
            db = jnp.zeros((HD, 1), F32)
            du_parts, dvln_parts = [], []
            for c in range(tb // HD):
                rws = slice(c * HD, (c + 1) * HD)
                mixed = _dot(wm, vb[rws], NN) + bst_ref[:, h:h + 1]
                dyc = dy_ref[rws, cols]
                du_parts.append(dyc * mixed)
                dmixed = dyc * u[rws]
                dmb = dmixed.astype(BF16)
                dws = dws + _dot(dmb, vb[rws], NT)
                db = db + jnp.sum(dmixed, axis=1, keepdims=True)
                dvln_parts.append(_dot(wmt, dmb, NN))
            du = jnp.concatenate(du_parts, axis=0)
            dvln = jnp.concatenate(dvln_parts, axis=0)
            dzu, dzv, dgv = vjp((du, dvln))
            dz_ref[:, cols] = dzu.astype(BF16)
            dz_ref[:, slice(GW + h * HD, GW + (h + 1) * HD)] = dzv.astype(BF16)
            dgv_ref[0, h:h + 1, :] = dgv
            dws_ref[0, h] = dws * tril
            db_ref[0, h] = jnp.broadcast_to(db, (HD, LANES))

    blk = pl.BlockSpec((tb, GW), lambda i: (i, 0))
    return pl.pallas_call(
        body, name="gmlp_bwd", grid=(nb,),
        in_specs=[blk, pl.BlockSpec((tb, GW), lambda i: (i, 1)), blk,
                  pl.BlockSpec((NH, HD), lambda i: (0, 0)), pl.BlockSpec((NH, HD, HD), lambda i: (0, 0, 0)),
                  pl.BlockSpec((HD, NH), lambda i: (0, 0))],
        out_specs=[pl.BlockSpec((tb, 2 * GW), lambda i: (i, 0)), pl.BlockSpec((1, NH, HD), lambda i: (i, 0, 0)),
                   pl.BlockSpec((1, NH, HD, HD), lambda i: (i, 0, 0, 0)),
                   pl.BlockSpec((1, NH, HD, LANES), lambda i: (i, 0, 0, 0))],
        out_shape=[jax.ShapeDtypeStruct((s, IN_COLS), BF16),
                   jax.ShapeDtypeStruct((nb, NH, HD), F32), jax.ShapeDtypeStruct((nb, NH, HD, HD), F32),
                   jax.ShapeDtypeStruct((nb, NH, HD, LANES), F32)],
        compiler_params=_cparams(),
    )(z, z, dy, gv, ws, bst)


def _pool_count(t0, window):
    pos = (t0 + lax.broadcasted_iota(jnp.int32, (ROW_TILE, LANES), 0)).astype(F32)
    return jnp.minimum(pos + 1.0, float(window))


def _window_sum(win, levels, back):
    n = win.shape[0]
    for lv in range(levels):
        step = 1 << lv
        win = win + pltpu.roll(win, n - step if back else step, 0)
    return win


def _pool_pooled(ppad_ref, t0, g):
    win = ppad_ref[pl.ds(t0, ROW_TILE + POOL_HALO), :]
    wsum = _window_sum(win, g + 1, False)[POOL_HALO:]
    return wsum / _pool_count(t0, POOL_WINDOWS[g]) - win[POOL_HALO:]


def _pool_fwd(z, wp, sp, y):
    s = z.shape[0]
    nt = s // ROW_TILE

    def body(p_ref, wp_ref, sp_ref, _, y_ref, ppad):
        for g in range(NG):
            cols = slice(g * LANES, (g + 1) * LANES)
            ppad[pl.ds(0, POOL_HALO), :] = jnp.zeros((POOL_HALO, LANES), F32)
            ppad[pl.ds(POOL_HALO, s), :] = p_ref[:, cols]
            wpb = wp_ref[g].astype(BF16)
            scale = sp_ref[:, cols]

            def tile(t, carry):
                t0 = pl.multiple_of(t * ROW_TILE, ROW_TILE)
                pooled = _pool_pooled(ppad, t0, g)
                y_ref[pl.ds(t0, ROW_TILE), cols] = (_dot(pooled.astype(BF16), wpb, NN) * scale).astype(BF16)
                return carry

            lax.fori_loop(0, nt, tile, 0)

    return pl.pallas_call(
        body, name="pool_fwd", grid=(1,),
        in_specs=[pl.BlockSpec((s, PW), lambda i: (0, 2 * GW // PW)),
                  pl.BlockSpec((NG, LANES, LANES), lambda i: (0, 0, 0)), pl.BlockSpec((1, PW), lambda i: (0, 0)), ANY],
        out_specs=pl.BlockSpec((s, PW), lambda i: (0, GW // PW)),
        out_shape=jax.ShapeDtypeStruct((s, D), BF16), input_output_aliases={3: 0},
        scratch_shapes=[pltpu.VMEM((s + POOL_HALO, LANES), F32)], compiler_params=_cparams(),
    )(z, wp, sp, y)


def _pool_bwd(z, dy, wp, sp, dz):
    s = z.shape[0]
    nt = s // ROW_TILE

    def body(p_ref, dy_ref, wp_ref, sp_ref, _, dp_ref, dwp_ref, dsp_ref, ppad, rpad, dpool):
        for g in range(NG):
            cols = slice(g * LANES, (g + 1) * LANES)
            ppad[pl.ds(0, POOL_HALO), :] = jnp.zeros((POOL_HALO, LANES), F32)
            ppad[pl.ds(POOL_HALO, s), :] = p_ref[:, cols]
            rpad[pl.ds(s, POOL_HALO), :] = jnp.zeros((POOL_HALO, LANES), F32)
            wpb = wp_ref[g].astype(BF16)
            scale = sp_ref[:, cols]

            def tile(t, carry):
                dwp, dsp = carry
                t0 = pl.multiple_of(t * ROW_TILE, ROW_TILE)
                pooled = _pool_pooled(ppad, t0, g)
                pb = pooled.astype(BF16)
                dyt = dy_ref[pl.ds(t0, ROW_TILE), cols]
                dsp = dsp + jnp.sum(dyt * _dot(pb, wpb, NN), axis=0, keepdims=True)
                dmm = (dyt * scale).astype(BF16)
                dwp = dwp + _dot(pb, dmm, TN)
                dpooled = _dot(dmm, wpb, NT)
                rpad[pl.ds(t0, ROW_TILE), :] = dpooled / _pool_count(t0, POOL_WINDOWS[g])
                dpool[pl.ds(t0, ROW_TILE), :] = dpooled
                return dwp, dsp

            dwp, dsp = lax.fori_loop(0, nt, tile, (jnp.zeros((LANES, LANES), F32), jnp.zeros((1, LANES), F32)))
            dwp_ref[g] = dwp
            dsp_ref[:, cols] = dsp

            def tile2(t, carry):
                t0 = pl.multiple_of(t * ROW_TILE, ROW_TILE)
                win = rpad[pl.ds(t0, ROW_TILE + POOL_HALO), :]
                back = _window_sum(win, g + 1, True)[:ROW_TILE]
                rows = pl.ds(t0, ROW_TILE)
                dp_ref[rows, cols] = (back - dpool[rows, :]).astype(BF16)
                return carry

            lax.fori_loop(0, nt, tile2, 0)

    return pl.pallas_call(
        body, name="pool_bwd", grid=(1,),
        in_specs=[pl.BlockSpec((s, PW), lambda i: (0, 2 * GW // PW)), pl.BlockSpec((s, PW), lambda i: (0, GW // PW)),
                  pl.BlockSpec((NG, LANES, LANES), lambda i: (0, 0, 0)), pl.BlockSpec((1, PW), lambda i: (0, 0)), ANY],
        out_specs=[pl.BlockSpec((s, PW), lambda i: (0, 2 * GW // PW)),
                   pl.BlockSpec((NG, LANES, LANES), lambda i: (0, 0, 0)), pl.BlockSpec((1, PW), lambda i: (0, 0))],
        out_shape=[jax.ShapeDtypeStruct((s, IN_COLS), BF16), jax.ShapeDtypeStruct((NG, LANES, LANES), F32),
                   jax.ShapeDtypeStruct((1, PW), F32)],
        input_output_aliases={4: 0},
        scratch_shapes=[pltpu.VMEM((s + POOL_HALO, LANES), F32), pltpu.VMEM((s + POOL_HALO, LANES), F32),
                        pltpu.VMEM((s, LANES), F32)],
        compiler_params=_cparams(),
    )(z, dy, wp, sp, dz)


CONV_LEAD = CONV_HALO - (CONV_K - 1)


SUBLANES = 8


def _sublane_shifts(win):
    n = win.shape[0]
    return [win] + [pltpu.roll(win, n - b, 0) for b in range(1, SUBLANES)]


def _shifted(shifts, offset):
    a, b = divmod(offset, SUBLANES)
    return shifts[b][a * SUBLANES:a * SUBLANES + ROW_TILE]


def _conv_taps(shifts, wdw_ref, lead, reverse):
    acc = jnp.zeros((ROW_TILE, CW), F32)
    for j in range(CONV_K):
        tap = (CONV_K - 1 - j) if reverse else j
        acc = acc + wdw_ref[tap:tap + 1, :] * _shifted(shifts, lead + j)
    return acc


def _conv_fill_glu(cv_ref, cg_ref, xpad, s):
    xpad[pl.ds(0, CONV_HALO), :] = jnp.zeros((CONV_HALO, CW), F32)

    def fill(t, carry):
        t0 = pl.multiple_of(t * ROW_TILE, ROW_TILE)
        rows = pl.ds(t0, ROW_TILE)
        xpad[pl.ds(t0 + CONV_HALO, ROW_TILE), :] = _glu(cv_ref[rows, :], cg_ref[rows, :])
        return carry

    lax.fori_loop(0, s // ROW_TILE, fill, 0)


def _conv_fwd(z, wdw, bdw, lng, lnb, y):
    s = z.shape[0]

    def body(cv_ref, cg_ref, wdw_ref, bdw_ref, lng_ref, lnb_ref, _, y_ref, xpad):
        _conv_fill_glu(cv_ref, cg_ref, xpad, s)

        def tile(t, carry):
            t0 = pl.multiple_of(t * ROW_TILE, ROW_TILE)
            shifts = _sublane_shifts(xpad[pl.ds(t0, ROW_TILE + CONV_HALO), :])
            hc = _conv_taps(shifts, wdw_ref, CONV_LEAD, False) + bdw_ref[...]
            y_ref[pl.ds(t0, ROW_TILE), :] = _ln_silu(hc, lng_ref[...], lnb_ref[...]).astype(BF16)
            return carry

        lax.fori_loop(0, s // ROW_TILE, tile, 0)

    vec = pl.BlockSpec((1, CW), lambda i: (0, 0))
    return pl.pallas_call(
        body, name="conv_fwd", grid=(1,),
        in_specs=[pl.BlockSpec((s, CW), lambda i: (0, (2 * GW + PW) // CW)),
                  pl.BlockSpec((s, CW), lambda i: (0, (2 * GW + PW) // CW + 1)),
                  pl.BlockSpec((CONV_K + 1, CW), lambda i: (0, 0)), vec, vec, vec, ANY],
        out_specs=pl.BlockSpec((s, CW), lambda i: (0, (GW + PW) // CW)),
        out_shape=jax.ShapeDtypeStruct((s, D), BF16), input_output_aliases={6: 0},
        scratch_shapes=[pltpu.VMEM((s + CONV_HALO, CW), F32)], compiler_params=_cparams(),
    )(z, z, wdw, bdw, lng, lnb, y)


def _conv_bwd(z, dy, wdw, bdw, lng, lnb, dz):
    s = z.shape[0]

    def body(cv_ref, cg_ref, dy_ref, wdw_ref, bdw_ref, lng_ref, lnb_ref, _,
             dz_ref, dwdw_ref, dbdw_ref, dlng_ref, dlnb_ref, xpad, dpad, dcg_keep):
        @pl.when(pl.program_id(0) == 0)
        def _():
            compute(cv_ref, cg_ref, dy_ref, wdw_ref, bdw_ref, lng_ref, lnb_ref,
                    dz_ref, dcg_keep, dwdw_ref, dbdw_ref, dlng_ref, dlnb_ref, xpad, dpad)

        @pl.when(pl.program_id(0) == 1)
        def _():
            dz_ref[...] = dcg_keep[...]

    def compute(cv_ref, cg_ref, dy_ref, wdw_ref, bdw_ref, lng_ref, lnb_ref,
                dcv_ref, dcg_ref, dwdw_ref, dbdw_ref, dlng_ref, dlnb_ref, xpad, dpad):
        _conv_fill_glu(cv_ref, cg_ref, xpad, s)
        dpad[pl.ds(s, CONV_HALO), :] = jnp.zeros((CONV_HALO, CW), F32)
        dwdw_ref[...] = jnp.zeros((CONV_K + 1, CW), F32)

        def tile(t, carry):
            db, dg, dbeta = carry
            t0 = pl.multiple_of(t * ROW_TILE, ROW_TILE)
            shifts = _sublane_shifts(xpad[pl.ds(t0, ROW_TILE + CONV_HALO), :])
            hc = _conv_taps(shifts, wdw_ref, CONV_LEAD, False) + bdw_ref[...]
            _, vjp = jax.vjp(_ln_silu, hc, lng_ref[...], lnb_ref[...])
            dhc, dg_t, dbeta_t = vjp(dy_ref[pl.ds(t0, ROW_TILE), :])
            dpad[pl.ds(t0, ROW_TILE), :] = dhc
            for j in range(CONV_K):
                dwdw_ref[j:j + 1, :] += jnp.sum(dhc * _shifted(shifts, CONV_LEAD + j), axis=0, keepdims=True)
            return db + jnp.sum(dhc, axis=0, keepdims=True), dg + dg_t, dbeta + dbeta_t

        zero = jnp.zeros((1, CW), F32)
        db, dg, dbeta = lax.fori_loop(0, s // ROW_TILE, tile, (zero, zero, zero))
        dbdw_ref[...] = db
        dlng_ref[...] = dg
        dlnb_ref[...] = dbeta

        def tile2(t, carry):
            t0 = pl.multiple_of(t * ROW_TILE, ROW_TILE)
            rows = pl.ds(t0, ROW_TILE)
            dglu = _conv_taps(_sublane_shifts(dpad[pl.ds(t0, ROW_TILE + CONV_HALO), :]), wdw_ref, 0, True)
            _, vjp = jax.vjp(_glu, cv_ref[rows, :], cg_ref[rows, :])
            dcv, dcg = vjp(dglu)
            dcv_ref[rows, :] = dcv.astype(BF16)
            dcg_ref[rows, :] = dcg.astype(BF16)
            return carry

        lax.fori_loop(0, s // ROW_TILE, tile2, 0)

    vec = pl.BlockSpec((1, CW), lambda i: (0, 0))
    wspec = pl.BlockSpec((CONV_K + 1, CW), lambda i: (0, 0))
    vshape = jax.ShapeDtypeStruct((1, CW), F32)
    return pl.pallas_call(
        body, name="conv_bwd", grid=(2,),
        in_specs=[pl.BlockSpec((s, CW), lambda i: (0, (2 * GW + PW) // CW)),
                  pl.BlockSpec((s, CW), lambda i: (0, (2 * GW + PW) // CW + 1)),
                  pl.BlockSpec((s, CW), lambda i: (0, (GW + PW) // CW)), wspec, vec, vec, vec, ANY],
        out_specs=[pl.BlockSpec((s, CW), lambda i: (0, (2 * GW + PW) // CW + i)), wspec, vec, vec, vec],
        out_shape=[jax.ShapeDtypeStruct((s, IN_COLS), BF16), jax.ShapeDtypeStruct((CONV_K + 1, CW), F32),
                   vshape, vshape, vshape],
        input_output_aliases={7: 0},
        scratch_shapes=[pltpu.VMEM((s + CONV_HALO, CW), F32), pltpu.VMEM((s + CONV_HALO, CW), F32),
                        pltpu.VMEM((s, CW), BF16)],
        compiler_params=_cparams(),
    )(z, z, dy, wdw, bdw, lng, lnb, dz)


def _softmax_rows(sc):
    e = jnp.exp(sc - jnp.max(sc, axis=-1, keepdims=True))
    return e / jnp.sum(e, axis=-1, keepdims=True)


def _attn_fwd(q, k, v, tq):
    s, m = q.shape[0], k.shape[0]
    tq = min(tq, s)

    def body(q_ref, k_ref, v_ref, o_ref):
        for h in range(XH):
            cols = slice(h * XHD, (h + 1) * XHD)
            p = _softmax_rows(_dot(q_ref[:, cols], k_ref[:, cols], NT) * ATT_SCALE)
            o_ref[:, cols] = _dot(p.astype(BF16), v_ref[:, cols], NN).astype(BF16)

    kv = pl.BlockSpec((m, D), lambda i: (0, 0))
    return pl.pallas_call(
        body, name="attn_fwd", grid=(s // tq,),
        in_specs=[pl.BlockSpec((tq, D), lambda i: (i, 0)), kv, kv],
        out_specs=pl.BlockSpec((tq, D), lambda i: (i, 0)),
        out_shape=jax.ShapeDtypeStruct((s, D), BF16), compiler_params=_cparams(),
    )(q, k, v)


def _attn_bwd(q, k, v, do, tq):
    s, m = q.shape[0], k.shape[0]
    tq = min(tq, s)

    def body(q_ref, k_ref, v_ref, do_ref, dq_ref, dk_ref, dv_ref):
        @pl.when(pl.program_id(0) == 0)
        def _():
            dk_ref[...] = jnp.zeros_like(dk_ref)
            dv_ref[...] = jnp.zeros_like(dv_ref)

        for h in range(XH):
            cols = slice(h * XHD, (h + 1) * XHD)
            qh, kh, vh, doh = q_ref[:, cols], k_ref[:, cols], v_ref[:, cols], do_ref[:, cols]
            p = _softmax_rows(_dot(qh, kh, NT) * ATT_SCALE)
            dp = _dot(doh, vh, NT)
            dv_ref[:, cols] += _dot(p.astype(BF16), doh, TN)
            ds = (p * (dp - jnp.sum(p * dp, axis=-1, keepdims=True)) * ATT_SCALE).astype(BF16)
            dq_ref[:, cols] = _dot(ds, kh, NN).astype(BF16)
            dk_ref[:, cols] += _dot(ds, qh, TN)

    kv = pl.BlockSpec((m, D), lambda i: (0, 0))
    qs = pl.BlockSpec((tq, D), lambda i: (i, 0))
    return pl.pallas_call(
        body, name="attn_bwd", grid=(s // tq,),
        in_specs=[qs, kv, kv, qs], out_specs=[qs, kv, kv],
        out_shape=[jax.ShapeDtypeStruct((s, D), BF16), jax.ShapeDtypeStruct((m, D), F32),
                   jax.ShapeDtypeStruct((m, D), F32)],
        compiler_params=_cparams(),
    )(q, k, v, do)


def _loss_head(y, target, tm):
    s = y.shape[0]
    tm = min(tm, s)

    def body(y_ref, t_ref, dy_ref, part_ref):
        err = y_ref[...] - t_ref[...]
        dy_ref[...] = err * (1.0 / D)
        part_ref[...] = jnp.full((1, 8, LANES), 0.5 * jnp.sum(err * err) * (1.0 / D), F32)

    blk = pl.BlockSpec((tm, D), lambda i: (i, 0))
    return pl.pallas_call(
        body, name="loss_head", grid=(s // tm,), in_specs=[blk, blk],
        out_specs=[blk, pl.BlockSpec((1, 8, LANES), lambda i: (i, 0, 0))],
        out_shape=[jax.ShapeDtypeStruct((s, D), F32), jax.ShapeDtypeStruct((s // tm, 8, LANES), F32)],
        compiler_params=_cparams(),
    )(y, target)


def _layer_fwd(x0, mem, w, p, fetch):
    z, hn0 = _rowop_mm("mix_in", "rms", (x0,), p["norm_mix_pre"], w["w_in"], NT, F32)
    y = _gmlp_fwd(z, p["gmlp_v_gain"], p["w_spatial"], p["b_spatial_t"], 512)
    y = _pool_fwd(z, p["w_pool"], p["s_pool"], y)
    y = _conv_fwd(z, p["w_dw"], p["b_dw"], p["conv_ln_g"], p["conv_ln_b"], y)
    w.update(fetch("out", (y,)))
    x1, h0 = _mm_rowop("mix_out", "rms_res", [(y, w["w_out"], NN)], (x0,), p["norm_mix_post"])
    w.update(fetch("att", (x1,)))
    q, hn1 = _rowop_mm("att_q", "rms", (x1,), p["norm_xattn_pre"], w["w_q"], NN, BF16)
    k, mn = _rowop_mm("att_k", "rms", (mem,), p["norm_mem"], w["w_k"], NN, BF16)
    v, _ = _rowop_mm("att_v", "rms", (mem,), p["norm_mem"], w["w_v"], NN, BF16)
    o = _attn_fwd(q, k, v, 512)
    x2, h1 = _mm_rowop("att_o", "rms_res", [(o, w["w_o"], NN)], (x1,), p["norm_xattn_post"])
    w.update(fetch("up", (x2,)))
    u, hn2 = _rowop_mm("ffn_up", "rms", (x2,), p["norm_ffn_pre"], w["w_up"], NT, BF16)
    w.update(fetch("down", (u,)))
    x3, h2 = _mm_rowop("ffn_down", "rms_res", [(u, w["w_down"], NN)], (x2,), p["norm_ffn_post"], relu2=True)
    saved = dict(x0=x0, z=z, hn0=hn0, y=y, h0=h0, x1=x1, q=q, hn1=hn1, k=k, v=v, mn=mn, o=o, h1=h1, x2=x2, u=u,
                 hn2=hn2, h2=h2)
    return x3, saved


def _layer_bwd(dx3, mem, w, p, sv, red):
    gs = {}
    du, dh2, dg = _rowop_mm("ffn_down_bwd", "rms_bwd", (sv["h2"], dx3), p["norm_ffn_post"], w["w_down"], NT, BF16,
                            u=sv["u"], after=red.after())
    gs["norm_ffn_post"] = jnp.sum(dg, axis=0)
    g_down = _mm_tn("ffn_down_dw", sv["u"], dh2, relu2=True)
    red.advance((g_down,))
    dx2, dg = _mm_rowop("ffn_up_bwd", "rms_bwd_res", [(du, w["w_up"], NN)], (sv["x2"], dx3), p["norm_ffn_pre"],
                        after=red.after())
    gs["norm_ffn_pre"] = jnp.sum(dg, axis=0)
    g_up = _mm_tn("ffn_up_dw", du, sv["hn2"])
    red.add("ffn", ("w_down", "w_up"), [g_down, g_up])
    do, dh1, dg = _rowop_mm("att_o_bwd", "rms_bwd", (sv["h1"], dx2), p["norm_xattn_post"], w["w_o"], NT, BF16,
                            after=red.after())
    gs["norm_xattn_post"] = jnp.sum(dg, axis=0)
    g_o = _mm_tn("att_o_dw", sv["o"], dh1)
    red.advance((g_o,))
    dq, dk, dv = _attn_bwd(sv["q"], sv["k"], sv["v"], do, 512)
    dk, dv = dk.astype(BF16), dv.astype(BF16)
    dx1, dg = _mm_rowop("att_q_bwd", "rms_bwd_res", [(dq, w["w_q"], NT)], (sv["x1"], dx2), p["norm_xattn_pre"],
                        after=red.after())
    gs["norm_xattn_pre"] = jnp.sum(dg, axis=0)
    g_q = _mm_tn("att_q_dw", sv["hn1"], dq)
    g_k = _mm_tn("att_k_dw", sv["mn"], dk)
    g_v = _mm_tn("att_v_dw", sv["mn"], dv)
    (dg,) = _mm_rowop("att_kv_bwd", "rms_bwd_gain", [(dk, w["w_k"], NT), (dv, w["w_v"], NT)], (mem,), p["norm_mem"])
    gs["norm_mem"] = jnp.sum(dg, axis=0)
    red.add("att", ("w_o", "w_q", "w_k", "w_v"), [g_o, g_q, g_k, g_v])
    dy, dh0, dg = _rowop_mm("mix_out_bwd", "rms_bwd", (sv["h0"], dx1), p["norm_mix_post"], w["w_out"], NT, F32,
                            after=red.after())
    gs["norm_mix_post"] = jnp.sum(dg, axis=0)
    g_out = _mm_tn("mix_out_dw", sv["y"], dh0)
    red.advance((g_out,))
    red.add("out", ("w_out",), [g_out])
    z = sv["z"]
    dz, dgv, dws, dbs = _gmlp_bwd(z, dy, p["gmlp_v_gain"], p["w_spatial"], p["b_spatial_t"], 512)
    gs["gmlp_v_gain"] = jnp.sum(dgv, axis=0)
    gs["w_spatial"] = jnp.sum(dws, axis=0)
    gs["b_spatial"] = jnp.sum(dbs[..., 0], axis=0)
    dz, gs["w_pool"], gs["s_pool"] = _pool_bwd(z, dy, p["w_pool"], p["s_pool"], dz)
    dz, dwdw, gs["b_dw"], gs["conv_ln_g"], gs["conv_ln_b"] = _conv_bwd(
        z, dy, p["w_dw"], p["b_dw"], p["conv_ln_g"], p["conv_ln_b"], dz)
    red.advance((dz,))
    red.small("mixer", _small_grad_arrays(gs, dwdw, norms=False))
    g_in = _mm_tn("mix_in_dw", dz, sv["hn0"], after=red.after())
    red.add("in", ("w_in",), [g_in])
    red.advance((g_in,))
    dx0, dg = _mm_rowop("mix_in_bwd", "rms_bwd_res", [(dz, w["w_in"], NN)], (sv["x0"], dx1), p["norm_mix_pre"],
                        after=red.after())
    gs["norm_mix_pre"] = jnp.sum(dg, axis=0)
    late = {"norms": jnp.concatenate([gs[n] for n in NORM_NAMES], axis=0)}
    if red.layer == 0:
        late["loss"] = red.extra[0]
    red.small("norms", late)
    return dx0


NORM_NAMES = ("norm_mix_pre", "norm_mix_post", "norm_xattn_pre", "norm_mem", "norm_xattn_post", "norm_ffn_pre",
              "norm_ffn_post")
VEC_NAMES = ("s_pool", "b_dw", "conv_ln_g", "conv_ln_b")
SMALL_ARRAYS = ("norms", "gain_bias", "w_spatial", "w_pool", "vecs", "w_dw")


def _small_grad_arrays(gs, dwdw, norms=True):
    out = {"norms": jnp.concatenate([gs[n] for n in NORM_NAMES], axis=0)} if norms else {}
    out.update({"gain_bias": jnp.concatenate([gs["gmlp_v_gain"], gs["b_spatial"]], axis=0),
                "w_spatial": gs["w_spatial"], "w_pool": gs["w_pool"],
                "vecs": jnp.concatenate([gs[n] for n in VEC_NAMES], axis=0), "w_dw": dwdw})
    return out


def _split_small_grads(arrays):
    out = {n: arrays["norms"][k] for k, n in enumerate(NORM_NAMES)}
    out.update({n: arrays["vecs"][k] for k, n in enumerate(VEC_NAMES)})
    out.update(gmlp_v_gain=arrays["gain_bias"][:NH], b_spatial=arrays["gain_bias"][NH:], w_spatial=arrays["w_spatial"],
               w_pool=arrays["w_pool"], w_dw=arrays["w_dw"][:CONV_K])
    return out


def _layer_params(small, l):
    p = {n: small[n][l].reshape(1, -1) for n in ("norm_mix_pre", "norm_mix_post", "s_pool", "b_dw", "conv_ln_g",
                                                   "conv_ln_b", "norm_xattn_pre", "norm_mem", "norm_xattn_post",
                                                   "norm_ffn_pre", "norm_ffn_post")}
    p["gmlp_v_gain"] = small["gmlp_v_gain"][l]
    p["w_spatial"] = small["w_spatial"][l]
    p["b_spatial_t"] = small["b_spatial"][l].T
    p["w_pool"] = small["w_pool"][l]
    p["w_dw"] = jnp.pad(small["w_dw"][l], ((0, 1), (0, 0)))
    return p


def _local_step(x, mem, target, fetch, small, red):
    small = dict(small)
    saved, weights, params = [], [], []
    h = x
    marker = ()
    for l in range(DEPTH):
        w = fetch(l, "in", marker)
        if "taps" in w:
            small["w_dw"] = w.pop("taps")
        p = _layer_params(small, l)
        h, sv = _layer_fwd(h, mem, w, p, functools.partial(fetch, l))
        marker = (h,)
        saved.append(sv)
        weights.append(w)
        params.append(p)
    dh, loss = _loss_head(h, target, 512)
    red.extra = (loss,)
    for l in reversed(range(DEPTH)):
        red.layer = l
        dh = _layer_bwd(dh, mem, weights[l], params[l], saved[l], red)
    return loss, dh


HBM = pl.BlockSpec(memory_space=pltpu.HBM)


def _position():
    return lax.axis_index("x"), lax.axis_index("y"), lax.axis_index("c")


SEM = pl.BlockSpec(memory_space=pltpu.SEMAPHORE)
EFFECT = pltpu.SideEffectType.DATAFLOW_SIDE_EFFECTING
TOKEN = jax.ShapeDtypeStruct((8, LANES), F32)
TOKEN_SPEC = pl.BlockSpec(memory_space=pltpu.VMEM)


def _landing(shape, dtype):
    return pltpu.with_memory_space_constraint(lax.empty(shape, dtype), pltpu.HBM)


def _hbm_shapes(arrays):
    return [pltpu.HBM(a.shape, a.dtype) for a in arrays]


def _block(ref, r, dev):
    return ref.at[pl.ds((4 * dev[0] + 2 * dev[1] + dev[2]) * r, r), :]


def _split_call(name, body, thru, sems_in, after, sems_out, token):
    n = len(thru)
    out_shape = [pltpu.SemaphoreType.DMA(s) for s in sems_out] + _hbm_shapes(thru) + ([TOKEN] if token else [])
    out_specs = [SEM] * len(sems_out) + [HBM] * n + ([TOKEN_SPEC] if token else [])
    return pl.pallas_call(
        body, name=name, in_specs=[HBM] * n + [SEM] * len(sems_in) + [ANY] * len(after),
        out_specs=out_specs, out_shape=out_shape,
        input_output_aliases={i: len(sems_out) + i for i in range(n)},
        compiler_params=pltpu.CompilerParams(has_side_effects=EFFECT),
    )(*thru, *sems_in, *after)


def _place_own(name, srcs, dev, out_dtype, tr):
    n = len(srcs)
    r, cols = srcs[0][0].shape[-2:]
    tr = r if r < 16 else _row_tile(r, tr)
    nb = r // tr

    def body(dev_ref, *refs):
        for a in range(n):
            refs[n + a][...] = refs[a][...].astype(out_dtype)

    in_specs = [pl.BlockSpec((tr, cols), lambda i, d: (i, 0)) if l is None
                else pl.BlockSpec((None, tr, cols), lambda i, d, l=l: (l, i, 0)) for _, l in srcs]
    return pl.pallas_call(
        body, name=name,
        grid_spec=pltpu.PrefetchScalarGridSpec(
            num_scalar_prefetch=1, grid=(nb,), in_specs=in_specs,
            out_specs=[pl.BlockSpec((tr, cols), lambda i, d: (d[0] * nb + i, 0))] * n),
        out_shape=[jax.ShapeDtypeStruct((N_DEV * r, cols), out_dtype)] * n, compiler_params=_cparams(),
    )(dev, *[a for a, _ in srcs])


def _gather_peers(x, y, c):
    return [(1 - x, y, c), (x, 1 - y, c), (1 - x, 1 - y, c), (x, y, 1 - c)]


def _block_rows(land):
    return land.shape[0] // N_DEV


def _near_peers(x, y, c):
    return [(1 - x, y, c), (x, 1 - y, c), (x, y, 1 - c)]


def _relay_route(x, y, c):
    origin = (x + c * (1 - 2 * x), y + (1 - c) * (1 - 2 * y), c)
    target = (x + (1 - c) * (1 - 2 * x), y + c * (1 - 2 * y), c)
    return origin, target


def _same_block_copy(blk, send_sem, recv_sem, to):
    return pltpu.make_async_remote_copy(src_ref=blk, dst_ref=blk, send_sem=send_sem, recv_sem=recv_sem, device_id=to,
                                        device_id_type=MESH)


def _gather_start(name, lands, after):
    n = len(lands)

    def body(*refs):
        lz = refs[:n]
        send_sems, recv_sems = refs[n + len(after)], refs[n + len(after) + 1]
        token = refs[-1]
        x, y, c = _position()
        for a in range(n):
            own = _block(lz[a], _block_rows(lands[a]), (x, y, c))
            for k, to in enumerate(_near_peers(x, y, c)):
                _same_block_copy(own, send_sems.at[k], recv_sems.at[k], to).start()
        token[...] = jnp.zeros_like(token)

    out = _split_call(name, body, list(lands), [], after, [(3,), (3,)], True)
    return out[0], out[1], out[2:2 + n], out[-1]


def _gather_step(name, near, far, fresh, after):
    groups = [g for g in (near and near[0], far and far[0], fresh) if g]
    counts = [len(near[0]) if near else 0, len(far[0]) if far else 0, len(fresh) if fresh else 0]
    n = sum(counts)
    sems_in = ([near[1]] if near else []) + ([far[1]] if far else [])
    sems_out = ([(2,), (2,), (1,), (1,)] if near else []) + ([(1,), (1,)] if far else []) + ([(3,), (3,)] if fresh else [])

    def body(*refs):
        lz = list(refs[:n])
        ins = list(refs[n:n + len(sems_in)])
        outs = list(refs[n + len(sems_in) + len(after):n + len(sems_in) + len(after) + len(sems_out)])
        token = refs[-1]
        x, y, c = _position()
        me, sibling = (x, y, c), (x, y, 1 - c)
        near_lz, far_lz, fresh_lz = (lz[sum(counts[:i]):sum(counts[:i + 1])] for i in range(3))
        neighbours = _near_peers(x, y, c)[:2]
        origin, target = _relay_route(x, y, c)
        diagonal = (1 - x, 1 - y, c)
        if near:
            recv0 = ins.pop(0)
            fsend, frecv, rsend, rrecv = (outs.pop(0) for _ in range(4))
            for a, land in enumerate(near[0]):
                for j, chip in enumerate(neighbours):
                    _same_block_copy(_block(near_lz[a], _block_rows(land), chip), fsend.at[j], recv0.at[j], me).wait_recv()
        if far:
            rrecv_in = ins.pop(0)
            f2send, f2recv = outs.pop(0), outs.pop(0)
            for a, land in enumerate(far[0]):
                _same_block_copy(_block(far_lz[a], _block_rows(land), diagonal), f2send.at[0], rrecv_in.at[0], me).wait_recv()
            for a, land in enumerate(far[0]):
                _same_block_copy(_block(far_lz[a], _block_rows(land), diagonal), f2send.at[0], f2recv.at[0], sibling).start()
        if near:
            for a, land in enumerate(near[0]):
                r = _block_rows(land)
                _same_block_copy(_block(near_lz[a], r, origin), rsend.at[0], rrecv.at[0], target).start()
                for j, chip in enumerate(neighbours):
                    _same_block_copy(_block(near_lz[a], r, chip), fsend.at[j], frecv.at[j], sibling).start()
        if fresh:
            send_sems, recv_sems = outs.pop(0), outs.pop(0)
            for a, land in enumerate(fresh):
                own = _block(fresh_lz[a], _block_rows(land), me)
                for k, to in enumerate(_near_peers(x, y, c)):
                    _same_block_copy(own, send_sems.at[k], recv_sems.at[k], to).start()
        token[...] = jnp.zeros_like(token)

    out = list(_split_call(name, body, [l for g in groups for l in g], sems_in, after, sems_out, True))
    res = {"token": out.pop()}
    if near:
        res.update(fsend=out.pop(0), frecv=out.pop(0), rsend=out.pop(0), rrecv=out.pop(0))
    if far:
        res.update(f2send=out.pop(0), f2recv=out.pop(0))
    if fresh:
        res.update(send=out.pop(0), recv=out.pop(0))
    res["near"], res["far"], res["fresh"] = (out[sum(counts[:i]):sum(counts[:i + 1])] for i in range(3))
    return res


def _gather_finish(name, lands, send_sems, recv_sems, fsend, frecv, rsend, f2send, f2recv, after):
    n = len(lands)

    def body(*refs):
        lz = refs[:n]
        send0, recv0, fsend_ref, frecv_ref, rsend_ref, f2send_ref, f2recv_ref = refs[n:n + 7]
        x, y, c = _position()
        me = (x, y, c)
        near = _near_peers(x, y, c)[:2]
        origin, _ = _relay_route(x, y, c)
        for a in range(n):
            r = _block_rows(lands[a])
            sib = _block(lz[a], r, (x, y, 1 - c))
            _same_block_copy(sib, send0.at[2], recv0.at[2], me).wait_recv()
            for j, chip in enumerate(near):
                blk = _block(lz[a], r, (chip[0], chip[1], 1 - c))
                _same_block_copy(blk, fsend_ref.at[j], frecv_ref.at[j], me).wait_recv()
            far = _block(lz[a], r, (1 - x, 1 - y, 1 - c))
            _same_block_copy(far, f2send_ref.at[0], f2recv_ref.at[0], me).wait_recv()
            own = _block(lz[a], r, me)
            for k in range(3):
                _same_block_copy(own, send0.at[k], recv0.at[k], me).wait_send()
            for j, chip in enumerate(near):
                _same_block_copy(_block(lz[a], r, chip), fsend_ref.at[j], frecv_ref.at[j], me).wait_send()
            _same_block_copy(_block(lz[a], r, origin), rsend_ref.at[0], recv0.at[0], me).wait_send()
            _same_block_copy(_block(lz[a], r, (1 - x, 1 - y, c)), f2send_ref.at[0], f2recv_ref.at[0], me).wait_send()

    return _split_call(name, body, list(lands), [send_sems, recv_sems, fsend, frecv, rsend, f2send, f2recv], after, [],
                       False)


def _sibling_start(name, grads, after):
    n = len(grads)
    lands = [_landing((4, g.shape[0] // N_DEV, D), g.dtype) for g in grads]

    def body(*refs):
        ins, lz = refs[:n], refs[n:2 * n]
        send_sem, recv_sem = refs[2 * n + len(after)], refs[2 * n + len(after) + 1]
        token = refs[-1]
        x, y, c = _position()
        for a in range(n):
            r = grads[a].shape[0] // N_DEV
            for q in range(4):
                pltpu.make_async_remote_copy(
                    src_ref=ins[a].at[pl.ds((2 * q + 1 - c) * r, r), :], dst_ref=lz[a].at[q], send_sem=send_sem.at[0],
                    recv_sem=recv_sem.at[0], device_id=(x, y, 1 - c), device_id_type=MESH).start()
        token[...] = jnp.zeros_like(token)

    out = _split_call(name, body, list(grads) + lands, [], after, [(1,), (1,)], True)
    return out[0], out[1], out[2:2 + n], out[2 + n:2 + 2 * n], out[-1]


def _sibling_finish(name, grads, lands, send_sem, recv_sem, after):
    n = len(grads)

    def body(*refs):
        ins, lz = refs[:n], refs[n:2 * n]
        send_ref, recv_ref = refs[2 * n], refs[2 * n + 1]
        x, y, c = _position()
        for a in range(n):
            r = grads[a].shape[0] // N_DEV
            for q in range(4):
                cp = pltpu.make_async_remote_copy(
                    src_ref=ins[a].at[pl.ds((2 * q + 1 - c) * r, r), :], dst_ref=lz[a].at[q], send_sem=send_ref.at[0],
                    recv_sem=recv_ref.at[0], device_id=(x, y, c), device_id_type=MESH)
                cp.wait_send()
                cp.wait_recv()

    out = _split_call(name, body, list(grads) + list(lands), [send_sem, recv_sem], after, [], False)
    return out[:n], out[n:2 * n]


def _chip_start(name, parts, after):
    n = len(parts)
    lands = [_landing((3,) + p.shape[1:], p.dtype) for p in parts]

    def body(*refs):
        ins, lz = refs[:n], refs[n:2 * n]
        send_sems, recv_sems = refs[2 * n + len(after)], refs[2 * n + len(after) + 1]
        token = refs[-1]
        x, y, c = _position()
        for a in range(n):
            for j, chip in enumerate(_gather_peers(x, y, c)[:3]):
                pltpu.make_async_remote_copy(
                    src_ref=ins[a].at[2 * chip[0] + chip[1]], dst_ref=lz[a].at[j], send_sem=send_sems.at[j],
                    recv_sem=recv_sems.at[j], device_id=chip, device_id_type=MESH).start()
        token[...] = jnp.zeros_like(token)

    out = _split_call(name, body, list(parts) + lands, [], after, [(3,), (3,)], True)
    return out[0], out[1], out[2:2 + n], out[2 + n:2 + 2 * n], out[-1]


def _chip_finish(name, parts, lands, send_sems, recv_sems, after):
    n = len(parts)

    def body(*refs):
        ins, lz = refs[:n], refs[n:2 * n]
        send_ref, recv_ref = refs[2 * n], refs[2 * n + 1]
        me = _position()
        for a in range(n):
            for j in range(3):
                cp = pltpu.make_async_remote_copy(
                    src_ref=ins[a].at[j], dst_ref=lz[a].at[j], send_sem=send_ref.at[j], recv_sem=recv_ref.at[j],
                    device_id=me, device_id_type=MESH)
                cp.wait_send()
                cp.wait_recv()

    out = _split_call(name, body, list(parts) + list(lands), [send_sems, recv_sems], after, [], False)
    return out[:n], out[n:2 * n]


def _other_devices(x, y, c):
    return [(x + (k >> 2 & 1) * (1 - 2 * x), y + (k >> 1 & 1) * (1 - 2 * y), c + (k & 1) * (1 - 2 * c))
            for k in range(1, N_DEV)]


def _broadcast_start(name, arrays, after):
    n = len(arrays)
    lands = [_landing((N_DEV,) + a.shape, a.dtype) for a in arrays]

    def body(*refs):
        ins, lz = refs[:n], refs[n:2 * n]
        send_sems, recv_sems = refs[2 * n + len(after)], refs[2 * n + len(after) + 1]
        token = refs[-1]
        x, y, c = _position()
        for a in range(n):
            for k, peer in enumerate(_other_devices(x, y, c)):
                pltpu.make_async_remote_copy(
                    src_ref=ins[a], dst_ref=lz[a].at[4 * x + 2 * y + c], send_sem=send_sems.at[k],
                    recv_sem=recv_sems.at[k], device_id=peer, device_id_type=MESH).start()
        token[...] = jnp.zeros_like(token)

    out = _split_call(name, body, list(arrays) + lands, [], after, [(N_DEV - 1,), (N_DEV - 1,)], True)
    return out[0], out[1], out[2:2 + n], out[2 + n:2 + 2 * n], out[-1]


def _broadcast_finish(name, arrays, lands, send_sems, recv_sems, after):
    n = len(arrays)

    def body(*refs):
        ins, lz = refs[:n], refs[n:2 * n]
        send_ref, recv_ref = refs[2 * n], refs[2 * n + 1]
        x, y, c = _position()
        for a in range(n):
            for k, peer in enumerate(_other_devices(x, y, c)):
                cp = pltpu.make_async_remote_copy(
                    src_ref=ins[a], dst_ref=lz[a].at[4 * peer[0] + 2 * peer[1] + peer[2]], send_sem=send_ref.at[k],
                    recv_sem=recv_ref.at[k], device_id=(x, y, c), device_id_type=MESH)
                cp.wait_send()
                cp.wait_recv()

    out = _split_call(name, body, list(arrays) + list(lands), [send_sems, recv_sems], after, [], False)
    return out[:n], out[n:2 * n]


def _row_tile(r, target):
    return max(t for t in range(16, min(r, target) + 1, 16) if r % t == 0)


def _chip_partial(name, grad, got, c, tr):
    r = grad.shape[0] // N_DEV
    tr = _row_tile(r, tr)
    g4 = grad.reshape(4, 2, r, D)

    def body(c_ref, g_ref, s_ref, o_ref):
        o_ref[...] = (g_ref[...].astype(F32) + s_ref[...].astype(F32)).astype(BF16)

    return pl.pallas_call(
        body, name=name,
        grid_spec=pltpu.PrefetchScalarGridSpec(
            num_scalar_prefetch=1, grid=(4, r // tr),
            in_specs=[pl.BlockSpec((None, None, tr, D), lambda q, i, c_ref: (q, c_ref[0], i, 0)),
                      pl.BlockSpec((None, tr, D), lambda q, i, c_ref: (q, i, 0))],
            out_specs=pl.BlockSpec((None, tr, D), lambda q, i, c_ref: (q, i, 0))),
        out_shape=jax.ShapeDtypeStruct((4, r, D), BF16), compiler_params=_cparams(),
    )(c, g4, got)


class _WeightGather:
    def __init__(self, groups):
        self.groups = list(groups)
        self.index = {key: i for i, (key, _, _) in enumerate(groups)}
        self.state = [None] * len(groups)
        self.token = ()
        for i in range(min(2, len(groups))):
            self._start(i)

    def _tag(self, i):
        return "%s_%d" % self.groups[i][0][::-1]

    def _start(self, i):
        send, recv, lz, tok = _gather_start("gather_start_" + self._tag(i), self.groups[i][2], self.token)
        self.state[i] = dict(send=send, recv=recv, lands=lz)
        self.token = (tok,)

    def _step(self, name, near, far, fresh, marker):
        exists = lambda i: i is not None and i < len(self.groups)
        near, far, fresh = (i if exists(i) else None for i in (near, far, fresh))
        res = _gather_step(
            name, None if near is None else (self.state[near]["lands"], self.state[near]["recv"]),
            None if far is None else (self.state[far]["lands"], self.state[far]["rrecv"]),
            None if fresh is None else self.groups[fresh][2], tuple(marker) + self.token)
        self.token = (res["token"],)
        if near is not None:
            self.state[near].update(lands=res["near"], fsend=res["fsend"], frecv=res["frecv"], rsend=res["rsend"],
                                    rrecv=res["rrecv"])
        if far is not None:
            self.state[far].update(lands=res["far"], f2send=res["f2send"], f2recv=res["f2recv"])
        if fresh is not None:
            self.state[fresh] = dict(send=res["send"], recv=res["recv"], lands=res["fresh"])

    def fetch(self, layer, group, marker):
        k = self.index[(layer, group)]
        if k == 0:
            self._step("gather_step_first", 0, None, None, marker)
        self._step("gather_step_" + self._tag(k), k + 1, k, k + 2, marker)
        st = self.state[k]
        lz = _gather_finish("gather_finish_" + self._tag(k), st["lands"], st["send"], st["recv"], st["fsend"],
                            st["frecv"], st["rsend"], st["f2send"], st["f2recv"], self.token)
        self.state[k] = None
        return dict(zip(self.groups[k][1], lz))


class _GradReduce:
    def __init__(self, core, chip):
        self.core, self.chip = core, chip
        self.layer = None
        self.token = ()
        self.at_sibling, self.at_chips = [], []
        self.extra, self.smalls = (), {}

    def after(self):
        return self.token

    def add(self, group, names, grads):
        tag = "%s_%d" % (group, self.layer)
        send, recv, grads, lands, tok = _sibling_start("grad_sibling_start_" + tag, grads, self.token)
        self.at_sibling.append((tag, [(self.layer, n) for n in names], send, recv, grads, lands))
        self.token = (tok,)

    def advance(self, marker):
        for tag, keys, send, recv, grads, lands in self.at_sibling:
            grads, lands = _sibling_finish("grad_sibling_finish_" + tag, grads, lands, send, recv, marker)
            parts = [_chip_partial("chip_partial_%d_%s" % key, g, got, self.core, 512)
                     for key, g, got in zip(keys, grads, lands)]
            send, recv, parts, lands, tok = _chip_start("grad_chip_start_" + tag, parts, ())
            self.at_chips.append([tag, keys, send, recv, parts, lands])
            self.token = (tok,)
        self.at_sibling = []

    def small(self, part, arrays):
        keys = list(arrays)
        send, recv, own, slots, tok = _broadcast_start(
            "small_grads_start_%d_%s" % (self.layer, part), [arrays[k] for k in keys], self.token)
        self.smalls.setdefault(self.layer, []).append((part, keys, send, recv, own, slots))
        self.token = (tok,)

    def small_finish(self, layer, marker):
        mine, theirs = {}, {}
        for part, keys, send, recv, own, slots in self.smalls[layer]:
            own, slots = _broadcast_finish("small_grads_finish_%d_%s" % (layer, part), own, slots, send, recv, marker)
            mine.update(zip(keys, own))
            theirs.update(zip(keys, slots))
        return mine, theirs

    def collect(self, key, marker):
        for entry in self.at_chips:
            tag, keys, send, recv, parts, lands = entry
            if key in keys:
                if send is not None:
                    parts, lands = _chip_finish("grad_chip_finish_" + tag, parts, lands, send, recv, marker)
                    entry[2:] = [None, None, parts, lands]
                i = keys.index(key)
                return parts[i], lands[i]
        raise KeyError(key)


def _adamw_math(w, g, m, v):
    m = ADAM_B1 * m + (1.0 - ADAM_B1) * g
    v = ADAM_B2 * v + (1.0 - ADAM_B2) * jnp.square(g)
    m_hat = m / (1.0 - ADAM_B1 ** ADAM_STEP)
    v_hat = v / (1.0 - ADAM_B2 ** ADAM_STEP)
    delta = -ADAM_LR * (m_hat / (jnp.sqrt(v_hat) + ADAM_EPS) + ADAM_WD * w)
    return delta, m, v


def _adamw_small(wts, mom_m, mom_v, own, gathered, loss_own, loss_gathered, dev):
    names = SMALL
    nw = len(names)
    na = len(SMALL_ARRAYS)

    def body(dev_ref, *refs):
        w_refs, m_refs, v_refs = (dict(zip(names, refs[i * nw:(i + 1) * nw])) for i in range(3))
        own_refs = refs[3 * nw:3 * nw + DEPTH * na]
        g_refs = refs[3 * nw + DEPTH * na:3 * nw + 2 * DEPTH * na]
        loss_own_ref, loss_got_ref = refs[3 * nw + 2 * DEPTH * na:3 * nw + 2 * DEPTH * na + 2]
        outs = refs[3 * nw + 2 * DEPTH * na + 2:]
        g_out, d_out, m_out, v_out = (dict(zip(names, outs[i * nw:(i + 1) * nw])) for i in range(4))
        me = dev_ref[0]

        loss = None
        for d in range(N_DEV):
            for b in range(loss_own.shape[0]):
                term = jnp.where(me == d, loss_own_ref[b], loss_got_ref[d, b])
                loss = term if loss is None else loss + term
        outs[4 * nw][...] = loss

        def update(name, at, g):
            g_out[name][at] = g
            d_out[name][at], m_out[name][at], v_out[name][at] = _adamw_math(
                w_refs[name][at], g, m_refs[name][at], v_refs[name][at])

        for l in range(DEPTH):
            mine = dict(zip(SMALL_ARRAYS, own_refs[l * na:(l + 1) * na]))
            got = dict(zip(SMALL_ARRAYS, g_refs[l * na:(l + 1) * na]))

            def total(key, at):
                acc = None
                for d in range(N_DEV):
                    term = jnp.where(me == d, mine[key][at] if at else mine[key][...], got[key][(d,) + at])
                    acc = term if acc is None else acc + term
                return acc

            row = (slice(l, l + 1),)
            for k, name in enumerate(NORM_NAMES):
                update(name, row, total("norms", (slice(k, k + 1),)))
            for k, name in enumerate(VEC_NAMES):
                update(name, row, total("vecs", (slice(k, k + 1),)))
            update("gmlp_v_gain", (l,), total("gain_bias", (slice(0, NH),)))
            update("b_spatial", (l,), total("gain_bias", (slice(NH, 2 * NH),)))
            update("w_spatial", (l,), total("w_spatial", ()))
            update("w_pool", (l,), total("w_pool", ()))
            update("w_dw", (l,), total("w_dw", (slice(0, CONV_K),)))

    args = [src[n] for src in (wts, mom_m, mom_v) for n in names]
    args += [src[l][k] for src in (own, gathered) for l in range(DEPTH) for k in SMALL_ARRAYS]
    args += [loss_own, loss_gathered]
    outs = pl.pallas_call(
        body, name="adamw_small",
        in_specs=[pl.BlockSpec(memory_space=pltpu.SMEM)] + [pl.BlockSpec(memory_space=pltpu.VMEM)] * len(args),
        out_shape=[jax.ShapeDtypeStruct(wts[n].shape, F32) for _ in range(4) for n in names]
        + [jax.ShapeDtypeStruct((8, LANES), F32)],
        compiler_params=_cparams(),
    )(dev, *args)
    return tuple(dict(zip(names, outs[i * nw:(i + 1) * nw])) for i in range(4)) + (outs[4 * nw],)


def _adamw_layers(name, w, reduced, m, v, chip, tr, transposed=False, after=()):
    nl, r, cdim = w.shape
    tr = _row_tile(r, tr)
    nb = r // tr

    def body(q_ref, w_ref, p0_ref, g0_ref, p1_ref, g1_ref, m_ref, v_ref, *rest):
        g_ref, d_ref, nm_ref, nv_ref = rest[len(after):]

        def total(p_ref, got_ref):
            acc = p_ref[...].astype(F32)
            for j in range(3):
                acc = acc + got_ref[j].astype(F32)
            return acc

        g = jnp.where(pl.program_id(0) == 0, total(p0_ref, g0_ref), total(p1_ref, g1_ref))
        if transposed:
            g = g.T
        g_ref[...] = g
        d_ref[...], nm_ref[...], nv_ref[...] = _adamw_math(w_ref[...], g, m_ref[...], v_ref[...])

    blk = pl.BlockSpec((None, tr, cdim), lambda l, i, q: (l, i, 0))
    first = lambda l, i: i * (1 - l) + (nb - 1) * l
    second = lambda l, i: i * l
    if transposed:
        gshape = (cdim, tr)
        at = lambda lead, i: (lead, 0, i)
    else:
        gshape = (tr, cdim)
        at = lambda lead, i: (lead, i, 0)
    specs = [blk,
             pl.BlockSpec((None,) + gshape, lambda l, i, q: at(q[0], first(l, i))),
             pl.BlockSpec((3,) + gshape, lambda l, i, q: at(0, first(l, i))),
             pl.BlockSpec((None,) + gshape, lambda l, i, q: at(q[0], second(l, i))),
             pl.BlockSpec((3,) + gshape, lambda l, i, q: at(0, second(l, i))), blk, blk] + [ANY] * len(after)
    shape = jax.ShapeDtypeStruct((nl, r, cdim), F32)
    return pl.pallas_call(
        body, name=name,
        grid_spec=pltpu.PrefetchScalarGridSpec(num_scalar_prefetch=1, grid=(nl, nb), in_specs=specs, out_specs=[blk] * 4),
        out_shape=[shape] * 4, compiler_params=_cparams(),
    )(chip, w, *reduced[0], *reduced[1], m, v, *after)


def _to_rows(name, a):
    return jnp.swapaxes(a, 1, 2) if name == "w_in" else a


def _place_own_transposed(name, srcs, dev, out_dtype, tc):
    n = len(srcs)
    kdim, cdim = srcs[0][0].shape[-2:]

    def body(dev_ref, *refs):
        for a in range(n):
            refs[n + a][...] = refs[a][...].T.astype(out_dtype)

    return pl.pallas_call(
        body, name=name,
        grid_spec=pltpu.PrefetchScalarGridSpec(
            num_scalar_prefetch=1, grid=(kdim // tc,),
            in_specs=[pl.BlockSpec((None, tc, cdim), lambda i, d, l=l: (l, i, 0)) for _, l in srcs],
            out_specs=[pl.BlockSpec((cdim, tc), lambda i, d: (d[0], i))] * n),
        out_shape=[jax.ShapeDtypeStruct((N_DEV * cdim, kdim), out_dtype)] * n, compiler_params=_cparams(),
    )(dev, *[a for a, _ in srcs])


def _pack(arrays, rows):
    flat = jnp.concatenate([a.reshape(-1) for a in arrays])
    return jnp.pad(flat, (0, rows * D - flat.shape[0])).reshape(rows, D)


def _rows_for(shapes, mult=8):
    total = 0
    for shp in shapes:
        size = 1
        for dim in shp:
            size *= dim
        total += size
    return -(-total // (mult * D)) * mult


def kernel(x, mem, norm_mix_pre, norm_mix_post, w_in, w_out, gmlp_v_gain, w_spatial, b_spatial, w_pool, s_pool, w_dw, b_dw, conv_ln_g, conv_ln_b, norm_xattn_pre, norm_mem, norm_xattn_post, w_q, w_k, w_v, w_o, norm_ffn_pre, norm_ffn_post, w_up, w_down, loss_target, m_norm_mix_pre, m_norm_mix_post, m_w_in, m_w_out, m_gmlp_v_gain, m_w_spatial, m_b_spatial, m_w_pool, m_s_pool, m_w_dw, m_b_dw, m_conv_ln_g, m_conv_ln_b, m_norm_xattn_pre, m_norm_mem, m_norm_xattn_post, m_w_q, m_w_k, m_w_v, m_w_o, m_norm_ffn_pre, m_norm_ffn_post, m_w_up, m_w_down, v_norm_mix_pre, v_norm_mix_post, v_w_in, v_w_out, v_gmlp_v_gain, v_w_spatial, v_b_spatial, v_w_pool, v_s_pool, v_w_dw, v_b_dw, v_conv_ln_g, v_conv_ln_b, v_norm_xattn_pre, v_norm_mem, v_norm_xattn_post, v_w_q, v_w_k, v_w_v, v_w_o, v_norm_ffn_pre, v_norm_ffn_post, v_w_up, v_w_down):
    args = dict(locals())
    wts = {n: args[n] for n in WEIGHTS}
    mom_m = {n: args["m_" + n] for n in WEIGHTS}
    mom_v = {n: args["v_" + n] for n in WEIGHTS}
    xi, yi, ci = _position()
    me = 4 * xi + 2 * yi + ci

    dev = jnp.reshape(me, (1,)).astype(jnp.int32)
    lands = {}
    for call, names, tr in (("place_in", ("w_in",), 256), ("place_att", ("w_out", "w_q", "w_k", "w_v", "w_o"), 64),
                            ("place_up", ("w_up",), 256), ("place_down", ("w_down",), 256)):
        srcs = [(_to_rows(n, wts[n]), l) for l in range(DEPTH) for n in names]
        placed = (_place_own_transposed if names == ("w_up",) else _place_own)(call, srcs, dev, BF16, tr)
        lands.update(zip([(l, n) for l in range(DEPTH) for n in names], placed))
    (lands[(0, "taps")],) = _place_own("place_taps", [(_pack([w_dw], _rows_for([w_dw.shape])), None)], dev, F32, 8)
    groups = []
    for l in range(DEPTH):
        for group, names in GATHER_GROUPS:
            if (l, group) == (0, "in"):
                names = names + ("taps",)
            groups.append(((l, group), names, [lands[(l, n)] for n in names]))
    gather = _WeightGather(groups)

    def fetch(layer, group, marker):
        w = gather.fetch(layer, group, marker)
        if "taps" in w:
            blocks = w["taps"].reshape(N_DEV, -1)[:, :w_dw.size].reshape((N_DEV,) + w_dw.shape)
            w["taps"] = jnp.moveaxis(blocks, 0, 2).reshape(DEPTH, CONV_K, CW)
        return w

    reduce = _GradReduce(jnp.reshape(ci, (1,)).astype(jnp.int32), jnp.reshape(2 * xi + yi, (1,)).astype(jnp.int32))
    small = {n: wts[n] for n in SMALL if n != "w_dw"}
    _, dx = _local_step(x[0], mem[0], loss_target[0], fetch, small, reduce)
    reduce.advance((dx,))

    grad_w, delta, new_m, new_v = {}, {}, {}, {}
    marker = (dx,) + tuple(reduce.after())
    for n in UPDATE_ORDER:
        reduced = [reduce.collect((l, n), marker) for l in range(DEPTH)]
        outs = _adamw_layers("adamw_" + n, _to_rows(n, wts[n]), reduced, _to_rows(n, mom_m[n]), _to_rows(n, mom_v[n]),
                             reduce.chip, 256, transposed=n == "w_up", after=marker)
        grad_w[n], delta[n], new_m[n], new_v[n] = (_to_rows(n, o) for o in outs)
        marker = (outs[1],)

    own, slots = [None] * DEPTH, [None] * DEPTH
    for l in reversed(range(DEPTH)):
        own[l], slots[l] = reduce.small_finish(l, marker)
        if l == 0:
            loss_own, loss_slots = own[l].pop("loss"), slots[l].pop("loss")
    shard_cols = CW // N_DEV
    for l in range(DEPTH):
        own[l]["w_dw"] = lax.dynamic_slice_in_dim(own[l]["w_dw"], me * shard_cols, shard_cols, axis=1)
        slots[l]["w_dw"] = lax.dynamic_slice_in_dim(slots[l]["w_dw"], me * shard_cols, shard_cols, axis=2)
    *small_out, loss_tile = _adamw_small(wts, mom_m, mom_v, own, slots, loss_own, loss_slots, dev)
    for dst, src in zip((grad_w, delta, new_m, new_v), small_out):
        dst.update(src)

    return (loss_tile[0, 0], dx[None], *[grad_w[n] for n in WEIGHTS], *[delta[n] for n in WEIGHTS],
            *[new_m[n] for n in WEIGHTS], *[new_v[n] for n in WEIGHTS])
```

```python
import functools

import jax
import jax.numpy as jnp
from jax import lax
from jax.experimental import pallas as pl
from jax.experimental.pallas import tpu as pltpu

F32 = jnp.float32
BF16 = jnp.bfloat16

D = 2048
GW = 1024
PW = 512
CW = 512
HD = 128
NH = 8
NG = 4
POOL_WINDOWS = (2, 4, 8, 16)
CONV_K = 31
IN_COLS = 2 * GW + PW + 2 * CW
DFF = 4 * D
XH = 4
XHD = D // XH
ATT_SCALE = XHD ** -0.5
RMS_EPS = 1e-6
LN_EPS = 1e-5
DEPTH = 2
N_DEV = 8

ADAM_LR = 0.001
ADAM_B1 = 0.9
ADAM_B2 = 0.999
ADAM_EPS = 1e-08
ADAM_WD = 0.01
ADAM_STEP = 10

LANES = 128
CONV_HALO = 32
POOL_HALO = 16
ROW_TILE = 128
VMEM_LIMIT = 60 * 1024 * 1024

MESH = pl.DeviceIdType.MESH
NT = (((1,), (1,)), ((), ()))
NN = (((1,), (0,)), ((), ()))
TN = (((0,), (0,)), ((), ()))

BIG = ("w_out", "w_q", "w_k", "w_v", "w_o", "w_up", "w_down", "w_in")
UPDATE_ORDER = ("w_down", "w_up", "w_o", "w_q", "w_k", "w_v", "w_out", "w_in")
GATHER_GROUPS = (("in", ("w_in",)), ("out", ("w_out",)), ("att", ("w_q", "w_k", "w_v", "w_o")), ("up", ("w_up",)),
                 ("down", ("w_down",)))
SMALL = ("norm_mix_pre", "norm_mix_post", "gmlp_v_gain", "w_spatial", "b_spatial", "w_pool", "s_pool",
         "w_dw", "b_dw", "conv_ln_g", "conv_ln_b", "norm_xattn_pre", "norm_mem", "norm_xattn_post",
         "norm_ffn_pre", "norm_ffn_post")
WEIGHTS = ("norm_mix_pre", "norm_mix_post", "w_in", "w_out", "gmlp_v_gain", "w_spatial", "b_spatial", "w_pool",
           "s_pool", "w_dw", "b_dw", "conv_ln_g", "conv_ln_b", "norm_xattn_pre", "norm_mem", "norm_xattn_post",
           "w_q", "w_k", "w_v", "w_o", "norm_ffn_pre", "norm_ffn_post", "w_up", "w_down")


def _cparams():
    return pltpu.CompilerParams(vmem_limit_bytes=VMEM_LIMIT)


def _dot(a, b, dims):
    return lax.dot_general(a, b, dims, preferred_element_type=F32)


def _rms(x, g):
    y = x * lax.rsqrt(jnp.mean(x * x, axis=-1, keepdims=True) + RMS_EPS)
    return y * g


def _rms_bwd(x, g, dy):
    r = lax.rsqrt(jnp.mean(x * x, axis=-1, keepdims=True) + RMS_EPS)
    xh = x * r
    t = dy * g
    dx = r * (t - xh * jnp.mean(t * xh, axis=-1, keepdims=True))
    return dx, jnp.sum(dy * xh, axis=0, keepdims=True)


def _gelu(x):
    cdf = 0.5 * (1.0 + jnp.tanh(0.7978845608028654 * (x + 0.044715 * (x * x * x))))
    return x * cdf


def _layer_norm(x, g, b=None):
    mu = jnp.mean(x, axis=-1, keepdims=True)
    xc = x - mu
    var = jnp.mean(xc * xc, axis=-1, keepdims=True)
    y = xc * lax.rsqrt(var + LN_EPS) * g
    return y if b is None else y + b


def _sigmoid(x):
    return 1.0 / (1.0 + jnp.exp(-x))


def _gmlp_rows(zu, zv, gv):
    return _gelu(zu), _layer_norm(_gelu(zv), gv)


def _glu(cv, cg):
    return cv * _sigmoid(cg)


def _ln_silu(h, g, b):
    y = _layer_norm(h, g, b)
    return y * _sigmoid(y)


ANY = pl.BlockSpec(memory_space=pl.ANY)


ROWS_TILE = 256
COLS_TILE = 512
DW_TILE = 512
RESIDENT_K = 2048
STREAM_K_TILE = 1024
STREAM_ROWS = 512


def _k_tiles(kdim):
    if kdim <= RESIDENT_K:
        return ROWS_TILE, kdim
    return STREAM_ROWS, max(t for t in range(LANES, STREAM_K_TILE + 1, LANES) if kdim % t == 0)


def _rowop_mm(name, kind, rows, g, w, dims, out_dtype, u=None, after=()):
    s = rows[0].shape[0]
    n = w.shape[0] if dims == NT else w.shape[1]
    tm, tn = min(ROWS_TILE, s), min(COLS_TILE, n)
    ni = s // tm
    bwd = kind == "rms_bwd"

    def rows_body(*refs):
        refs = list(refs)
        row_refs = [refs.pop(0) for _ in rows]
        g_ref = refs.pop(0)
        del refs[:len(after)]
        if bwd:
            a, dg = _rms_bwd(row_refs[0][...], g_ref[...], row_refs[1][...])
            refs[1][0] = dg
        else:
            a = _rms(row_refs[0][...], g_ref[...])
        refs[0][...] = a.astype(BF16)

    row_spec = pl.BlockSpec((tm, D), lambda i: (i, 0))
    res = pl.pallas_call(
        rows_body, name=name + "_rows", grid=(ni,),
        in_specs=[row_spec] * len(rows) + [pl.BlockSpec((1, D), lambda i: (0, 0))] + [ANY] * len(after),
        out_specs=[row_spec] + ([pl.BlockSpec((1, 1, D), lambda i: (i, 0, 0))] if bwd else []),
        out_shape=[jax.ShapeDtypeStruct((s, D), BF16)] + ([jax.ShapeDtypeStruct((ni, 1, D), F32)] if bwd else []),
        compiler_params=_cparams(),
    )(*rows, g, *after)
    a = res[0]

    def body(a_ref, w_ref, *rest):
        acc = _dot(a_ref[...], w_ref[...], dims)
        if u is not None:
            acc = acc * (2.0 * jnp.maximum(rest[0][...], 0.0))
        rest[-1][...] = acc.astype(out_dtype)

    w_spec = pl.BlockSpec((tn, D), lambda j: (j, 0)) if dims == NT else pl.BlockSpec((D, tn), lambda j: (0, j))
    tile = pl.BlockSpec((s, tn), lambda j: (0, j))
    out = pl.pallas_call(
        body, name=name, grid=(n // tn,),
        in_specs=[pl.BlockSpec((s, D), lambda j: (0, 0)), w_spec] + ([tile] if u is not None else []),
        out_specs=tile, out_shape=jax.ShapeDtypeStruct((s, n), out_dtype), compiler_params=_cparams(),
    )(a, w, *([u] if u is not None else []))
    return (out, *res)


def _mm_rowop(name, kind, pairs, rows, g, relu2=False, after=()):
    s, kdim = pairs[0][0].shape
    tm, tk = _k_tiles(kdim)
    tm = min(tm, s)
    ni, nk = s // tm, kdim // tk
    npair = len(pairs)

    def body(*refs):
        refs = list(refs)
        a_refs = [refs.pop(0) for _ in range(npair)]
        w_refs = [refs.pop(0) for _ in range(npair)]
        row_refs = [refs.pop(0) for _ in rows]
        g_ref = refs.pop(0)
        del refs[:len(after)]
        acc = refs.pop()
        outs = refs
        k = pl.program_id(1)

        @pl.when(k == 0)
        def _():
            acc[...] = jnp.zeros_like(acc)

        for a_ref, w_ref, (_, _, dims) in zip(a_refs, w_refs, pairs):
            a = a_ref[...]
            if relu2:
                a = jnp.square(jnp.maximum(a, 0.0))
            acc[...] += _dot(a.astype(BF16), w_ref[...], dims)

        @pl.when(k == nk - 1)
        def _():
            h = acc[...]
            if kind == "rms_res":
                outs[0][...] = row_refs[0][...] + _rms(h, g_ref[...])
                outs[1][...] = h
            else:
                dx, dg = _rms_bwd(row_refs[0][...], g_ref[...], h)
                if kind == "rms_bwd_res":
                    outs[0][...] = row_refs[1][...] + dx
                    outs[1][0] = dg
                else:
                    outs[0][0] = dg

    row_spec = pl.BlockSpec((tm, D), lambda i, k: (i, 0))
    dg_shape = jax.ShapeDtypeStruct((ni, 1, D), F32)
    dg_spec = pl.BlockSpec((1, 1, D), lambda i, k: (i, 0, 0))
    in_specs = [pl.BlockSpec((tm, tk), lambda i, k: (i, k))] * npair
    for _, _, dims in pairs:
        in_specs.append(pl.BlockSpec((tk, D), lambda i, k: (k, 0)) if dims == NN
                        else pl.BlockSpec((D, tk), lambda i, k: (0, k)))
    in_specs += [row_spec] * len(rows) + [pl.BlockSpec((1, D), lambda i, k: (0, 0))] + [ANY] * len(after)
    if kind == "rms_res":
        out_shape = [jax.ShapeDtypeStruct((s, D), F32)] * 2
        out_specs = [row_spec, row_spec]
    elif kind == "rms_bwd_res":
        out_shape = [jax.ShapeDtypeStruct((s, D), F32), dg_shape]
        out_specs = [row_spec, dg_spec]
    else:
        out_shape = [dg_shape]
        out_specs = [dg_spec]
    return pl.pallas_call(
        body, name=name, grid=(ni, nk), in_specs=in_specs, out_specs=out_specs, out_shape=out_shape,
        scratch_shapes=[pltpu.VMEM((tm, D), F32)], compiler_params=_cparams(),
    )(*[p[0] for p in pairs], *[p[1] for p in pairs], *rows, g, *after)


def _mm_tn(name, a, gmat, relu2=False, after=()):
    s, m = a.shape
    tm = min(DW_TILE, m)
    ni = m // tm

    def body(a_ref, g_ref, *rest):
        av = a_ref[...]
        if relu2:
            av = jnp.square(jnp.maximum(av, 0.0))
        rest[len(after)][...] = _dot(av.astype(BF16), g_ref[...], TN).astype(BF16)

    return pl.pallas_call(
        body, name=name, grid=(ni,),
        in_specs=[pl.BlockSpec((s, tm), lambda i: (0, i)), pl.BlockSpec((s, D), lambda i: (0, 0))] + [ANY] * len(after),
        out_specs=pl.BlockSpec((tm, D), lambda i: (i, 0)),
        out_shape=jax.ShapeDtypeStruct((m, D), BF16), compiler_params=_cparams(),
    )(a, gmat, *after)


def _tril():
    r = lax.broadcasted_iota(jnp.int32, (HD, HD), 0)
    c = lax.broadcasted_iota(jnp.int32, (HD, HD), 1)
    return (c <= r).astype(F32)


def _gmlp_fwd(z, gv, ws, bst, tb):
    s = z.shape[0]
    tb = min(tb, s)

    def body(zu_ref, zv_ref, gv_ref, ws_ref, bst_ref, y_ref):
        tril = _tril()
        for h in range(NH):
            cols = slice(h * HD, (h + 1) * HD)
            u, vln = _gmlp_rows(zu_ref[:, cols], zv_ref[:, cols], gv_ref[h:h + 1, :])
            wm = (ws_ref[h] * tril).astype(BF16)
            vb = vln.astype(BF16)
            for c in range(tb // HD):
                rws = slice(c * HD, (c + 1) * HD)
                mixed = _dot(wm, vb[rws], NN) + bst_ref[:, h:h + 1]
                y_ref[rws, cols] = (u[rws] * mixed).astype(BF16)

    return pl.pallas_call(
        body, name="gmlp_fwd", grid=(s // tb,),
        in_specs=[pl.BlockSpec((tb, GW), lambda i: (i, 0)), pl.BlockSpec((tb, GW), lambda i: (i, 1)),
                  pl.BlockSpec((NH, HD), lambda i: (0, 0)), pl.BlockSpec((NH, HD, HD), lambda i: (0, 0, 0)),
                  pl.BlockSpec((HD, NH), lambda i: (0, 0))],
        out_specs=pl.BlockSpec((tb, GW), lambda i: (i, 0)),
        out_shape=jax.ShapeDtypeStruct((s, D), BF16), compiler_params=_cparams(),
    )(z, z, gv, ws, bst)


def _gmlp_bwd(z, dy, gv, ws, bst, tb):
    s = z.shape[0]
    tb = min(tb, s)
    nb = s // tb

    def body(zu_ref, zv_ref, dy_ref, gv_ref, ws_ref, bst_ref, dz_ref, dgv_ref, dws_ref, db_ref):
        tril = _tril()
        for h in range(NH):
            cols = slice(h * HD, (h + 1) * HD)
            (u, vln), vjp = jax.vjp(_gmlp_rows, zu_ref[:, cols], zv_ref[:, cols], gv_ref[h:h + 1, :])
            wmf = ws_ref[h] * tril
            wm = wmf.astype(BF16)
            wmt = wmf.T.astype(BF16)
            vb = vln.astype(BF16)
            dws = jnp.zeros((HD, HD), F32)
            db = jnp.zeros((HD, 1), F32)
            du_parts, dvln_parts = [], []
            for c in range(tb // HD):
                rws = slice(c * HD, (c + 1) * HD)
                mixed = _dot(wm, vb[rws], NN) + bst_ref[:, h:h + 1]
                dyc = dy_ref[rws, cols]
                du_parts.append(dyc * mixed)
                dmixed = dyc * u[rws]
                dmb = dmixed.astype(BF16)
                dws = dws + _dot(dmb, vb[rws], NT)
                db = db + jnp.sum(dmixed, axis=1, keepdims=True)
                dvln_parts.append(_dot(wmt, dmb, NN))
            du = jnp.concatenate(du_parts, axis=0)
            dvln = jnp.concatenate(dvln_parts, axis=0)
            dzu, dzv, dgv = vjp((du, dvln))
            dz_ref[:, cols] = dzu.astype(BF16)
            dz_ref[:, slice(GW + h * HD, GW + (h + 1) * HD)] = dzv.astype(BF16)
            dgv_ref[0, h:h + 1, :] = dgv
            dws_ref[0, h] = dws * tril
            db_ref[0, h] = jnp.broadcast_to(db, (HD, LANES))

    blk = pl.BlockSpec((tb, GW), lambda i: (i, 0))
    return pl.pallas_call(
        body, name="gmlp_bwd", grid=(nb,),
        in_specs=[blk, pl.BlockSpec((tb, GW), lambda i: (i, 1)), blk,
                  pl.BlockSpec((NH, HD), lambda i: (0, 0)), pl.BlockSpec((NH, HD, HD), lambda i: (0, 0, 0)),
                  pl.BlockSpec((HD, NH), lambda i: (0, 0))],
        out_specs=[pl.BlockSpec((tb, 2 * GW), lambda i: (i, 0)), pl.BlockSpec((1, NH, HD), lambda i: (i, 0, 0)),
                   pl.BlockSpec((1, NH, HD, HD), lambda i: (i, 0, 0, 0)),
                   pl.BlockSpec((1, NH, HD, LANES), lambda i: (i, 0, 0, 0))],
        out_shape=[jax.ShapeDtypeStruct((s, IN_COLS), BF16),
                   jax.ShapeDtypeStruct((nb, NH, HD), F32), jax.ShapeDtypeStruct((nb, NH, HD, HD), F32),
                   jax.ShapeDtypeStruct((nb, NH, HD, LANES), F32)],
        compiler_params=_cparams(),
    )(z, z, dy, gv, ws, bst)


def _pool_count(t0, window):
    pos = (t0 + lax.broadcasted_iota(jnp.int32, (ROW_TILE, LANES), 0)).astype(F32)
    return jnp.minimum(pos + 1.0, float(window))


def _window_sum(win, levels, back):
    n = win.shape[0]
    for lv in range(levels):
        step = 1 << lv
        win = win + pltpu.roll(win, n - step if back else step, 0)
    return win


def _pool_pooled(ppad_ref, t0, g):
    win = ppad_ref[pl.ds(t0, ROW_TILE + POOL_HALO), :]
    wsum = _window_sum(win, g + 1, False)[POOL_HALO:]
    return wsum / _pool_count(t0, POOL_WINDOWS[g]) - win[POOL_HALO:]


def _pool_fwd(z, wp, sp, y):
    s = z.shape[0]
    nt = s // ROW_TILE

    def body(p_ref, wp_ref, sp_ref, _, y_ref, ppad):
        for g in range(NG):
            cols = slice(g * LANES, (g + 1) * LANES)
            ppad[pl.ds(0, POOL_HALO), :] = jnp.zeros((POOL_HALO, LANES), F32)
            ppad[pl.ds(POOL_HALO, s), :] = p_ref[:, cols]
            wpb = wp_ref[g].astype(BF16)
            scale = sp_ref[:, cols]

            def tile(t, carry):
                t0 = pl.multiple_of(t * ROW_TILE, ROW_TILE)
                pooled = _pool_pooled(ppad, t0, g)
                y_ref[pl.ds(t0, ROW_TILE), cols] = (_dot(pooled.astype(BF16), wpb, NN) * scale).astype(BF16)
                return carry

            lax.fori_loop(0, nt, tile, 0)

    return pl.pallas_call(
        body, name="pool_fwd", grid=(1,),
        in_specs=[pl.BlockSpec((s, PW), lambda i: (0, 2 * GW // PW)),
                  pl.BlockSpec((NG, LANES, LANES), lambda i: (0, 0, 0)), pl.BlockSpec((1, PW), lambda i: (0, 0)), ANY],
        out_specs=pl.BlockSpec((s, PW), lambda i: (0, GW // PW)),
        out_shape=jax.ShapeDtypeStruct((s, D), BF16), input_output_aliases={3: 0},
        scratch_shapes=[pltpu.VMEM((s + POOL_HALO, LANES), F32)], compiler_params=_cparams(),
    )(z, wp, sp, y)


def _pool_bwd(z, dy, wp, sp, dz):
    s = z.shape[0]
    nt = s // ROW_TILE

    def body(p_ref, dy_ref, wp_ref, sp_ref, _, dp_ref, dwp_ref, dsp_ref, ppad, rpad, dpool):
        for g in range(NG):
            cols = slice(g * LANES, (g + 1) * LANES)
            ppad[pl.ds(0, POOL_HALO), :] = jnp.zeros((POOL_HALO, LANES), F32)
            ppad[pl.ds(POOL_HALO, s), :] = p_ref[:, cols]
            rpad[pl.ds(s, POOL_HALO), :] = jnp.zeros((POOL_HALO, LANES), F32)
            wpb = wp_ref[g].astype(BF16)
            scale = sp_ref[:, cols]

            def tile(t, carry):
                dwp, dsp = carry
                t0 = pl.multiple_of(t * ROW_TILE, ROW_TILE)
                pooled = _pool_pooled(ppad, t0, g)
                pb = pooled.astype(BF16)
                dyt = dy_ref[pl.ds(t0, ROW_TILE), cols]
                dsp = dsp + jnp.sum(dyt * _dot(pb, wpb, NN), axis=0, keepdims=True)
                dmm = (dyt * scale).astype(BF16)
                dwp = dwp + _dot(pb, dmm, TN)
                dpooled = _dot(dmm, wpb, NT)
                rpad[pl.ds(t0, ROW_TILE), :] = dpooled / _pool_count(t0, POOL_WINDOWS[g])
                dpool[pl.ds(t0, ROW_TILE), :] = dpooled
                return dwp, dsp

            dwp, dsp = lax.fori_loop(0, nt, tile, (jnp.zeros((LANES, LANES), F32), jnp.zeros((1, LANES), F32)))
            dwp_ref[g] = dwp
            dsp_ref[:, cols] = dsp

            def tile2(t, carry):
                t0 = pl.multiple_of(t * ROW_TILE, ROW_TILE)
                win = rpad[pl.ds(t0, ROW_TILE + POOL_HALO), :]
                back = _window_sum(win, g + 1, True)[:ROW_TILE]
                rows = pl.ds(t0, ROW_TILE)
                dp_ref[rows, cols] = (back - dpool[rows, :]).astype(BF16)
                return carry

            lax.fori_loop(0, nt, tile2, 0)

    return pl.pallas_call(
        body, name="pool_bwd", grid=(1,),
        in_specs=[pl.BlockSpec((s, PW), lambda i: (0, 2 * GW // PW)), pl.BlockSpec((s, PW), lambda i: (0, GW // PW)),
                  pl.BlockSpec((NG, LANES, LANES), lambda i: (0, 0, 0)), pl.BlockSpec((1, PW), lambda i: (0, 0)), ANY],
        out_specs=[pl.BlockSpec((s, PW), lambda i: (0, 2 * GW // PW)),
                   pl.BlockSpec((NG, LANES, LANES), lambda i: (0, 0, 0)), pl.BlockSpec((1, PW), lambda i: (0, 0))],
        out_shape=[jax.ShapeDtypeStruct((s, IN_COLS), BF16), jax.ShapeDtypeStruct((NG, LANES, LANES), F32),
                   jax.ShapeDtypeStruct((1, PW), F32)],
        input_output_aliases={4: 0},
        scratch_shapes=[pltpu.VMEM((s + POOL_HALO, LANES), F32), pltpu.VMEM((s + POOL_HALO, LANES), F32),
                        pltpu.VMEM((s, LANES), F32)],
        compiler_params=_cparams(),
    )(z, dy, wp, sp, dz)


CONV_LEAD = CONV_HALO - (CONV_K - 1)


SUBLANES = 8


def _sublane_shifts(win):
    n = win.shape[0]
    return [win] + [pltpu.roll(win, n - b, 0) for b in range(1, SUBLANES)]


def _shifted(shifts, offset):
    a, b = divmod(offset, SUBLANES)
    return shifts[b][a * SUBLANES:a * SUBLANES + ROW_TILE]


def _conv_taps(shifts, wdw_ref, lead, reverse):
    acc = jnp.zeros((ROW_TILE, CW), F32)
    for j in range(CONV_K):
        tap = (CONV_K - 1 - j) if reverse else j
        acc = acc + wdw_ref[tap:tap + 1, :] * _shifted(shifts, lead + j)
    return acc


def _conv_fill_glu(cv_ref, cg_ref, xpad, s):
    xpad[pl.ds(0, CONV_HALO), :] = jnp.zeros((CONV_HALO, CW), F32)

    def fill(t, carry):
        t0 = pl.multiple_of(t * ROW_TILE, ROW_TILE)
        rows = pl.ds(t0, ROW_TILE)
        xpad[pl.ds(t0 + CONV_HALO, ROW_TILE), :] = _glu(cv_ref[rows, :], cg_ref[rows, :])
        return carry

    lax.fori_loop(0, s // ROW_TILE, fill, 0)


def _conv_fwd(z, wdw, bdw, lng, lnb, y):
    s = z.shape[0]

    def body(cv_ref, cg_ref, wdw_ref, bdw_ref, lng_ref, lnb_ref, _, y_ref, xpad):
        _conv_fill_glu(cv_ref, cg_ref, xpad, s)

        def tile(t, carry):
            t0 = pl.multiple_of(t * ROW_TILE, ROW_TILE)
            shifts = _sublane_shifts(xpad[pl.ds(t0, ROW_TILE + CONV_HALO), :])
            hc = _conv_taps(shifts, wdw_ref, CONV_LEAD, False) + bdw_ref[...]
            y_ref[pl.ds(t0, ROW_TILE), :] = _ln_silu(hc, lng_ref[...], lnb_ref[...]).astype(BF16)
            return carry

        lax.fori_loop(0, s // ROW_TILE, tile, 0)

    vec = pl.BlockSpec((1, CW), lambda i: (0, 0))
    return pl.pallas_call(
        body, name="conv_fwd", grid=(1,),
        in_specs=[pl.BlockSpec((s, CW), lambda i: (0, (2 * GW + PW) // CW)),
                  pl.BlockSpec((s, CW), lambda i: (0, (2 * GW + PW) // CW + 1)),
                  pl.BlockSpec((CONV_K + 1, CW), lambda i: (0, 0)), vec, vec, vec, ANY],
        out_specs=pl.BlockSpec((s, CW), lambda i: (0, (GW + PW) // CW)),
        out_shape=jax.ShapeDtypeStruct((s, D), BF16), input_output_aliases={6: 0},
        scratch_shapes=[pltpu.VMEM((s + CONV_HALO, CW), F32)], compiler_params=_cparams(),
    )(z, z, wdw, bdw, lng, lnb, y)


def _conv_bwd(z, dy, wdw, bdw, lng, lnb, dz):
    s = z.shape[0]

    def body(cv_ref, cg_ref, dy_ref, wdw_ref, bdw_ref, lng_ref, lnb_ref, _,
             dz_ref, dwdw_ref, dbdw_ref, dlng_ref, dlnb_ref, xpad, dpad, dcg_keep):
        @pl.when(pl.program_id(0) == 0)
        def _():
            compute(cv_ref, cg_ref, dy_ref, wdw_ref, bdw_ref, lng_ref, lnb_ref,
                    dz_ref, dcg_keep, dwdw_ref, dbdw_ref, dlng_ref, dlnb_ref, xpad, dpad)

        @pl.when(pl.program_id(0) == 1)
        def _():
            dz_ref[...] = dcg_keep[...]

    def compute(cv_ref, cg_ref, dy_ref, wdw_ref, bdw_ref, lng_ref, lnb_ref,
                dcv_ref, dcg_ref, dwdw_ref, dbdw_ref, dlng_ref, dlnb_ref, xpad, dpad):
        _conv_fill_glu(cv_ref, cg_ref, xpad, s)
        dpad[pl.ds(s, CONV_HALO), :] = jnp.zeros((CONV_HALO, CW), F32)
        dwdw_ref[...] = jnp.zeros((CONV_K + 1, CW), F32)

        def tile(t, carry):
            db, dg, dbeta = carry
            t0 = pl.multiple_of(t * ROW_TILE, ROW_TILE)
            shifts = _sublane_shifts(xpad[pl.ds(t0, ROW_TILE + CONV_HALO), :])
            hc = _conv_taps(shifts, wdw_ref, CONV_LEAD, False) + bdw_ref[...]
            _, vjp = jax.vjp(_ln_silu, hc, lng_ref[...], lnb_ref[...])
            dhc, dg_t, dbeta_t = vjp(dy_ref[pl.ds(t0, ROW_TILE), :])
            dpad[pl.ds(t0, ROW_TILE), :] = dhc
            for j in range(CONV_K):
                dwdw_ref[j:j + 1, :] += jnp.sum(dhc * _shifted(shifts, CONV_LEAD + j), axis=0, keepdims=True)
            return db + jnp.sum(dhc, axis=0, keepdims=True), dg + dg_t, dbeta + dbeta_t

        zero = jnp.zeros((1, CW), F32)
        db, dg, dbeta = lax.fori_loop(0, s // ROW_TILE, tile, (zero, zero, zero))
        dbdw_ref[...] = db
        dlng_ref[...] = dg
        dlnb_ref[...] = dbeta

        def tile2(t, carry):
            t0 = pl.multiple_of(t * ROW_TILE, ROW_TILE)
            rows = pl.ds(t0, ROW_TILE)
            dglu = _conv_taps(_sublane_shifts(dpad[pl.ds(t0, ROW_TILE + CONV_HALO), :]), wdw_ref, 0, True)
            _, vjp = jax.vjp(_glu, cv_ref[rows, :], cg_ref[rows, :])
            dcv, dcg = vjp(dglu)
            dcv_ref[rows, :] = dcv.astype(BF16)
            dcg_ref[rows, :] = dcg.astype(BF16)
            return carry

        lax.fori_loop(0, s // ROW_TILE, tile2, 0)

    vec = pl.BlockSpec((1, CW), lambda i: (0, 0))
    wspec = pl.BlockSpec((CONV_K + 1, CW), lambda i: (0, 0))
    vshape = jax.ShapeDtypeStruct((1, CW), F32)
    return pl.pallas_call(
        body, name="conv_bwd", grid=(2,),
        in_specs=[pl.BlockSpec((s, CW), lambda i: (0, (2 * GW + PW) // CW)),
                  pl.BlockSpec((s, CW), lambda i: (0, (2 * GW + PW) // CW + 1)),
                  pl.BlockSpec((s, CW), lambda i: (0, (GW + PW) // CW)), wspec, vec, vec, vec, ANY],
        out_specs=[pl.BlockSpec((s, CW), lambda i: (0, (2 * GW + PW) // CW + i)), wspec, vec, vec, vec],
        out_shape=[jax.ShapeDtypeStruct((s, IN_COLS), BF16), jax.ShapeDtypeStruct((CONV_K + 1, CW), F32),
                   vshape, vshape, vshape],
        input_output_aliases={7: 0},
        scratch_shapes=[pltpu.VMEM((s + CONV_HALO, CW), F32), pltpu.VMEM((s + CONV_HALO, CW), F32),
                        pltpu.VMEM((s, CW), BF16)],
        compiler_params=_cparams(),
    )(z, z, dy, wdw, bdw, lng, lnb, dz)


def _softmax_rows(sc):
    e = jnp.exp(sc - jnp.max(sc, axis=-1, keepdims=True))
    return e / jnp.sum(e, axis=-1, keepdims=True)


def _attn_fwd(q, k, v, tq):
    s, m = q.shape[0], k.shape[0]
    tq = min(tq, s)

    def body(q_ref, k_ref, v_ref, o_ref):
        for h in range(XH):
            cols = slice(h * XHD, (h + 1) * XHD)
            p = _softmax_rows(_dot(q_ref[:, cols], k_ref[:, cols], NT) * ATT_SCALE)
            o_ref[:, cols] = _dot(p.astype(BF16), v_ref[:, cols], NN).astype(BF16)

    kv = pl.BlockSpec((m, D), lambda i: (0, 0))
    return pl.pallas_call(
        body, name="attn_fwd", grid=(s // tq,),
        in_specs=[pl.BlockSpec((tq, D), lambda i: (i, 0)), kv, kv],
        out_specs=pl.BlockSpec((tq, D), lambda i: (i, 0)),
        out_shape=jax.ShapeDtypeStruct((s, D), BF16), compiler_params=_cparams(),
    )(q, k, v)


def _attn_bwd(q, k, v, do, tq):
    s, m = q.shape[0], k.shape[0]
    tq = min(tq, s)

    def body(q_ref, k_ref, v_ref, do_ref, dq_ref, dk_ref, dv_ref):
        @pl.when(pl.program_id(0) == 0)
        def _():
            dk_ref[...] = jnp.zeros_like(dk_ref)
            dv_ref[...] = jnp.zeros_like(dv_ref)

        for h in range(XH):
            cols = slice(h * XHD, (h + 1) * XHD)
            qh, kh, vh, doh = q_ref[:, cols], k_ref[:, cols], v_ref[:, cols], do_ref[:, cols]
            p = _softmax_rows(_dot(qh, kh, NT) * ATT_SCALE)
            dp = _dot(doh, vh, NT)
            dv_ref[:, cols] += _dot(p.astype(BF16), doh, TN)
            ds = (p * (dp - jnp.sum(p * dp, axis=-1, keepdims=True)) * ATT_SCALE).astype(BF16)
            dq_ref[:, cols] = _dot(ds, kh, NN).astype(BF16)
            dk_ref[:, cols] += _dot(ds, qh, TN)

    kv = pl.BlockSpec((m, D), lambda i: (0, 0))
    qs = pl.BlockSpec((tq, D), lambda i: (i, 0))
    return pl.pallas_call(
        body, name="attn_bwd", grid=(s // tq,),
        in_specs=[qs, kv, kv, qs], out_specs=[qs, kv, kv],
        out_shape=[jax.ShapeDtypeStruct((s, D), BF16), jax.ShapeDtypeStruct((m, D), F32),
                   jax.ShapeDtypeStruct((m, D), F32)],
        compiler_params=_cparams(),
    )(q, k, v, do)


def _loss_head(y, target, tm):
    s = y.shape[0]
    tm = min(tm, s)

    def body(y_ref, t_ref, dy_ref, part_ref):
        err = y_ref[...] - t_ref[...]
        dy_ref[...] = err * (1.0 / D)
        part_ref[...] = jnp.full((1, 8, LANES), 0.5 * jnp.sum(err * err) * (1.0 / D), F32)

    blk = pl.BlockSpec((tm, D), lambda i: (i, 0))
    return pl.pallas_call(
        body, name="loss_head", grid=(s // tm,), in_specs=[blk, blk],
        out_specs=[blk, pl.BlockSpec((1, 8, LANES), lambda i: (i, 0, 0))],
        out_shape=[jax.ShapeDtypeStruct((s, D), F32), jax.ShapeDtypeStruct((s // tm, 8, LANES), F32)],
        compiler_params=_cparams(),
    )(y, target)


def _layer_fwd(x0, mem, w, p, fetch):
    z, hn0 = _rowop_mm("mix_in", "rms", (x0,), p["norm_mix_pre"], w["w_in"], NT, F32)
    y = _gmlp_fwd(z, p["gmlp_v_gain"], p["w_spatial"], p["b_spatial_t"], 512)
    y = _pool_fwd(z, p["w_pool"], p["s_pool"], y)
    y = _conv_fwd(z, p["w_dw"], p["b_dw"], p["conv_ln_g"], p["conv_ln_b"], y)
    w.update(fetch("out", (y,)))
    x1, h0 = _mm_rowop("mix_out", "rms_res", [(y, w["w_out"], NN)], (x0,), p["norm_mix_post"])
    w.update(fetch("att", (x1,)))
    q, hn1 = _rowop_mm("att_q", "rms", (x1,), p["norm_xattn_pre"], w["w_q"], NN, BF16)
    k, mn = _rowop_mm("att_k", "rms", (mem,), p["norm_mem"], w["w_k"], NN, BF16, after=(x1,))
    v, _ = _rowop_mm("att_v", "rms", (mem,), p["norm_mem"], w["w_v"], NN, BF16, after=(x1,))
    o = _attn_fwd(q, k, v, 512)
    x2, h1 = _mm_rowop("att_o", "rms_res", [(o, w["w_o"], NN)], (x1,), p["norm_xattn_post"])
    w.update(fetch("up", (x2,)))
    u, hn2 = _rowop_mm("ffn_up", "rms", (x2,), p["norm_ffn_pre"], w["w_up"], NT, F32)
    w.update(fetch("down", (u,)))
    x3, h2 = _mm_rowop("ffn_down", "rms_res", [(u, w["w_down"], NN)], (x2,), p["norm_ffn_post"], relu2=True)
    saved = dict(x0=x0, z=z, hn0=hn0, y=y, h0=h0, x1=x1, q=q, hn1=hn1, k=k, v=v, mn=mn, o=o, h1=h1, x2=x2, u=u,
                 hn2=hn2, h2=h2)
    return x3, saved


def _layer_bwd(dx3, mem, w, p, sv, red):
    gs = {}
    du, dh2, dg = _rowop_mm("ffn_down_bwd", "rms_bwd", (sv["h2"], dx3), p["norm_ffn_post"], w["w_down"], NT, BF16,
                            u=sv["u"], after=red.after())
    gs["norm_ffn_post"] = jnp.sum(dg, axis=0)
    g_down = _mm_tn("ffn_down_dw", sv["u"], dh2, relu2=True)
    red.advance((g_down,))
    dx2, dg = _mm_rowop("ffn_up_bwd", "rms_bwd_res", [(du, w["w_up"], NN)], (sv["x2"], dx3), p["norm_ffn_pre"],
                        after=red.after())
    gs["norm_ffn_pre"] = jnp.sum(dg, axis=0)
    g_up = _mm_tn("ffn_up_dw", du, sv["hn2"])
    red.add("ffn", ("w_down", "w_up"), [g_down, g_up])
    do, dh1, dg = _rowop_mm("att_o_bwd", "rms_bwd", (sv["h1"], dx2), p["norm_xattn_post"], w["w_o"], NT, BF16,
                            after=red.after())
    gs["norm_xattn_post"] = jnp.sum(dg, axis=0)
    g_o = _mm_tn("att_o_dw", sv["o"], dh1)
    red.advance((g_o,))
    dq, dk, dv = _attn_bwd(sv["q"], sv["k"], sv["v"], do, 512)
    dk, dv = dk.astype(BF16), dv.astype(BF16)
    dx1, dg = _mm_rowop("att_q_bwd", "rms_bwd_res", [(dq, w["w_q"], NT)], (sv["x1"], dx2), p["norm_xattn_pre"],
                        after=red.after())
    gs["norm_xattn_pre"] = jnp.sum(dg, axis=0)
    g_q = _mm_tn("att_q_dw", sv["hn1"], dq)
    g_k = _mm_tn("att_k_dw", sv["mn"], dk)
    g_v = _mm_tn("att_v_dw", sv["mn"], dv)
    (dg,) = _mm_rowop("att_kv_bwd", "rms_bwd_gain", [(dk, w["w_k"], NT), (dv, w["w_v"], NT)], (mem,), p["norm_mem"])
    gs["norm_mem"] = jnp.sum(dg, axis=0)
    red.add("att", ("w_o", "w_q", "w_k", "w_v"), [g_o, g_q, g_k, g_v])
    dy, dh0, dg = _rowop_mm("mix_out_bwd", "rms_bwd", (sv["h0"], dx1), p["norm_mix_post"], w["w_out"], NT, F32,
                            after=red.after())
    gs["norm_mix_post"] = jnp.sum(dg, axis=0)
    g_out = _mm_tn("mix_out_dw", sv["y"], dh0)
    red.advance((g_out,))
    red.add("out", ("w_out",), [g_out])
    z = sv["z"]
    dz, dgv, dws, dbs = _gmlp_bwd(z, dy, p["gmlp_v_gain"], p["w_spatial"], p["b_spatial_t"], 512)
    gs["gmlp_v_gain"] = jnp.sum(dgv, axis=0)
    gs["w_spatial"] = jnp.sum(dws, axis=0)
    gs["b_spatial"] = jnp.sum(dbs[..., 0], axis=0)
    dz, gs["w_pool"], gs["s_pool"] = _pool_bwd(z, dy, p["w_pool"], p["s_pool"], dz)
    dz, dwdw, gs["b_dw"], gs["conv_ln_g"], gs["conv_ln_b"] = _conv_bwd(
        z, dy, p["w_dw"], p["b_dw"], p["conv_ln_g"], p["conv_ln_b"], dz)
    red.advance((dz,))
    g_in = _mm_tn("mix_in_dw", dz, sv["hn0"], after=red.after())
    red.add("in", ("w_in",), [g_in])
    if red.layer == 0:
        red.advance((g_in,))
    red.small("mixer", _small_grad_arrays(gs, dwdw, norms=False))
    dx0, dg = _mm_rowop("mix_in_bwd", "rms_bwd_res", [(dz, w["w_in"], NN)], (sv["x0"], dx1), p["norm_mix_pre"],
                        after=red.after())
    gs["norm_mix_pre"] = jnp.sum(dg, axis=0)
    late = {"norms": jnp.concatenate([gs[n] for n in NORM_NAMES], axis=0)}
    if red.layer == 0:
        late["loss"] = red.extra[0]
    red.small("norms", late)
    return dx0


NORM_NAMES = ("norm_mix_pre", "norm_mix_post", "norm_xattn_pre", "norm_mem", "norm_xattn_post", "norm_ffn_pre",
              "norm_ffn_post")
VEC_NAMES = ("s_pool", "b_dw", "conv_ln_g", "conv_ln_b")
SMALL_ARRAYS = ("norms", "gain_bias", "w_spatial", "w_pool", "vecs", "w_dw")


def _small_grad_arrays(gs, dwdw, norms=True):
    out = {"norms": jnp.concatenate([gs[n] for n in NORM_NAMES], axis=0)} if norms else {}
    out.update({"gain_bias": jnp.concatenate([gs["gmlp_v_gain"], gs["b_spatial"]], axis=0),
                "w_spatial": gs["w_spatial"], "w_pool": gs["w_pool"],
                "vecs": jnp.concatenate([gs[n] for n in VEC_NAMES], axis=0), "w_dw": dwdw})
    return out


def _split_small_grads(arrays):
    out = {n: arrays["norms"][k] for k, n in enumerate(NORM_NAMES)}
    out.update({n: arrays["vecs"][k] for k, n in enumerate(VEC_NAMES)})
    out.update(gmlp_v_gain=arrays["gain_bias"][:NH], b_spatial=arrays["gain_bias"][NH:], w_spatial=arrays["w_spatial"],
               w_pool=arrays["w_pool"], w_dw=arrays["w_dw"][:CONV_K])
    return out


def _layer_params(small, l):
    p = {n: small[n][l].reshape(1, -1) for n in ("norm_mix_pre", "norm_mix_post", "s_pool", "b_dw", "conv_ln_g",
                                                   "conv_ln_b", "norm_xattn_pre", "norm_mem", "norm_xattn_post",
                                                   "norm_ffn_pre", "norm_ffn_post")}
    p["gmlp_v_gain"] = small["gmlp_v_gain"][l]
    p["w_spatial"] = small["w_spatial"][l]
    p["b_spatial_t"] = small["b_spatial"][l].T
    p["w_pool"] = small["w_pool"][l]
    p["w_dw"] = jnp.pad(small["w_dw"][l], ((0, 1), (0, 0)))
    return p


def _local_step(x, mem, target, fetch, small, red):
    small = dict(small)
    saved, weights, params = [], [], []
    h = x
    marker = ()
    for l in range(DEPTH):
        w = fetch(l, "in", marker)
        if "taps" in w:
            small["w_dw"] = w.pop("taps")
        p = _layer_params(small, l)
        h, sv = _layer_fwd(h, mem, w, p, functools.partial(fetch, l))
        marker = (h,)
        saved.append(sv)
        weights.append(w)
        params.append(p)
    dh, loss = _loss_head(h, target, 512)
    red.extra = (loss,)
    for l in reversed(range(DEPTH)):
        red.layer = l
        dh = _layer_bwd(dh, mem, weights[l], params[l], saved[l], red)
    return loss, dh


HBM = pl.BlockSpec(memory_space=pltpu.HBM)


def _position():
    return lax.axis_index("x"), lax.axis_index("y"), lax.axis_index("c")


SEM = pl.BlockSpec(memory_space=pltpu.SEMAPHORE)
EFFECT = pltpu.SideEffectType.DATAFLOW_SIDE_EFFECTING
TOKEN = jax.ShapeDtypeStruct((8, LANES), F32)
TOKEN_SPEC = pl.BlockSpec(memory_space=pltpu.VMEM)


def _landing(shape, dtype):
    return pltpu.with_memory_space_constraint(lax.empty(shape, dtype), pltpu.HBM)


def _hbm_shapes(arrays):
    return [pltpu.HBM(a.shape, a.dtype) for a in arrays]


def _block(ref, r, dev):
    return ref.at[pl.ds((4 * dev[0] + 2 * dev[1] + dev[2]) * r, r), :]


def _split_call(name, body, thru, sems_in, after, sems_out, token):
    n = len(thru)
    out_shape = [pltpu.SemaphoreType.DMA(s) for s in sems_out] + _hbm_shapes(thru) + ([TOKEN] if token else [])
    out_specs = [SEM] * len(sems_out) + [HBM] * n + ([TOKEN_SPEC] if token else [])
    return pl.pallas_call(
        body, name=name, in_specs=[HBM] * n + [SEM] * len(sems_in) + [ANY] * len(after),
        out_specs=out_specs, out_shape=out_shape,
        input_output_aliases={i: len(sems_out) + i for i in range(n)},
        compiler_params=pltpu.CompilerParams(has_side_effects=EFFECT),
    )(*thru, *sems_in, *after)


def _place_own(name, srcs, dev, out_dtype, tr):
    n = len(srcs)
    r, cols = srcs[0][0].shape[-2:]
    tr = r if r < 16 else _row_tile(r, tr)
    nb = r // tr

    def body(dev_ref, *refs):
        for a in range(n):
            refs[n + a][...] = refs[a][...].astype(out_dtype)

    in_specs = [pl.BlockSpec((tr, cols), lambda i, d: (i, 0)) if l is None
                else pl.BlockSpec((None, tr, cols), lambda i, d, l=l: (l, i, 0)) for _, l in srcs]
    return pl.pallas_call(
        body, name=name,
        grid_spec=pltpu.PrefetchScalarGridSpec(
            num_scalar_prefetch=1, grid=(nb,), in_specs=in_specs,
            out_specs=[pl.BlockSpec((tr, cols), lambda i, d: (d[0] * nb + i, 0))] * n),
        out_shape=[jax.ShapeDtypeStruct((N_DEV * r, cols), out_dtype)] * n, compiler_params=_cparams(),
    )(dev, *[a for a, _ in srcs])


def _gather_peers(x, y, c):
    return [(1 - x, y, c), (x, 1 - y, c), (1 - x, 1 - y, c), (x, y, 1 - c)]


def _block_rows(land):
    return land.shape[0] // N_DEV


def _near_peers(x, y, c):
    return [(1 - x, y, c), (x, 1 - y, c), (x, y, 1 - c)]


def _relay_route(x, y, c):
    origin = (x + c * (1 - 2 * x), y + (1 - c) * (1 - 2 * y), c)
    target = (x + (1 - c) * (1 - 2 * x), y + c * (1 - 2 * y), c)
    return origin, target


def _same_block_copy(blk, send_sem, recv_sem, to):
    return pltpu.make_async_remote_copy(src_ref=blk, dst_ref=blk, send_sem=send_sem, recv_sem=recv_sem, device_id=to,
                                        device_id_type=MESH)


def _gather_start(name, lands, after):
    n = len(lands)

    def body(*refs):
        lz = refs[:n]
        send_sems, recv_sems = refs[n + len(after)], refs[n + len(after) + 1]
        token = refs[-1]
        x, y, c = _position()
        for a in range(n):
            own = _block(lz[a], _block_rows(lands[a]), (x, y, c))
            for k, to in enumerate(_near_peers(x, y, c)):
                _same_block_copy(own, send_sems.at[k], recv_sems.at[k], to).start()
        token[...] = jnp.zeros_like(token)

    out = _split_call(name, body, list(lands), [], after, [(3,), (3,)], True)
    return out[0], out[1], out[2:2 + n], out[-1]


def _gather_step(name, near, far, fresh, after):
    groups = [g for g in (near and near[0], far and far[0], fresh) if g]
    counts = [len(near[0]) if near else 0, len(far[0]) if far else 0, len(fresh) if fresh else 0]
    n = sum(counts)
    sems_in = ([near[1]] if near else []) + ([far[1]] if far else [])
    sems_out = ([(2,), (2,), (1,), (1,)] if near else []) + ([(1,), (1,)] if far else []) + ([(3,), (3,)] if fresh else [])

    def body(*refs):
        lz = list(refs[:n])
        ins = list(refs[n:n + len(sems_in)])
        outs = list(refs[n + len(sems_in) + len(after):n + len(sems_in) + len(after) + len(sems_out)])
        token = refs[-1]
        x, y, c = _position()
        me, sibling = (x, y, c), (x, y, 1 - c)
        near_lz, far_lz, fresh_lz = (lz[sum(counts[:i]):sum(counts[:i + 1])] for i in range(3))
        neighbours = _near_peers(x, y, c)[:2]
        origin, target = _relay_route(x, y, c)
        diagonal = (1 - x, 1 - y, c)
        if near:
            recv0 = ins.pop(0)
            fsend, frecv, rsend, rrecv = (outs.pop(0) for _ in range(4))
            for a, land in enumerate(near[0]):
                for j, chip in enumerate(neighbours):
                    _same_block_copy(_block(near_lz[a], _block_rows(land), chip), fsend.at[j], recv0.at[j], me).wait_recv()
        if far:
            rrecv_in = ins.pop(0)
            f2send, f2recv = outs.pop(0), outs.pop(0)
            for a, land in enumerate(far[0]):
                _same_block_copy(_block(far_lz[a], _block_rows(land), diagonal), f2send.at[0], rrecv_in.at[0], me).wait_recv()
            for a, land in enumerate(far[0]):
                _same_block_copy(_block(far_lz[a], _block_rows(land), diagonal), f2send.at[0], f2recv.at[0], sibling).start()
        if near:
            for a, land in enumerate(near[0]):
                r = _block_rows(land)
                _same_block_copy(_block(near_lz[a], r, origin), rsend.at[0], rrecv.at[0], target).start()
                for j, chip in enumerate(neighbours):
                    _same_block_copy(_block(near_lz[a], r, chip), fsend.at[j], frecv.at[j], sibling).start()
        if fresh:
            send_sems, recv_sems = outs.pop(0), outs.pop(0)
            for a, land in enumerate(fresh):
                own = _block(fresh_lz[a], _block_rows(land), me)
                for k, to in enumerate(_near_peers(x, y, c)):
                    _same_block_copy(own, send_sems.at[k], recv_sems.at[k], to).start()
        token[...] = jnp.zeros_like(token)

    out = list(_split_call(name, body, [l for g in groups for l in g], sems_in, after, sems_out, True))
    res = {"token": out.pop()}
    if near:
        res.update(fsend=out.pop(0), frecv=out.pop(0), rsend=out.pop(0), rrecv=out.pop(0))
    if far:
        res.update(f2send=out.pop(0), f2recv=out.pop(0))
    if fresh:
        res.update(send=out.pop(0), recv=out.pop(0))
    res["near"], res["far"], res["fresh"] = (out[sum(counts[:i]):sum(counts[:i + 1])] for i in range(3))
    return res


def _gather_finish(name, lands, send_sems, recv_sems, fsend, frecv, rsend, f2send, f2recv, after):
    n = len(lands)

    def body(*refs):
        lz = refs[:n]
        send0, recv0, fsend_ref, frecv_ref, rsend_ref, f2send_ref, f2recv_ref = refs[n:n + 7]
        x, y, c = _position()
        me = (x, y, c)
        near = _near_peers(x, y, c)[:2]
        origin, _ = _relay_route(x, y, c)
        for a in range(n):
            r = _block_rows(lands[a])
            sib = _block(lz[a], r, (x, y, 1 - c))
            _same_block_copy(sib, send0.at[2], recv0.at[2], me).wait_recv()
            for j, chip in enumerate(near):
                blk = _block(lz[a], r, (chip[0], chip[1], 1 - c))
                _same_block_copy(blk, fsend_ref.at[j], frecv_ref.at[j], me).wait_recv()
            far = _block(lz[a], r, (1 - x, 1 - y, 1 - c))
            _same_block_copy(far, f2send_ref.at[0], f2recv_ref.at[0], me).wait_recv()
            own = _block(lz[a], r, me)
            for k in range(3):
                _same_block_copy(own, send0.at[k], recv0.at[k], me).wait_send()
            for j, chip in enumerate(near):
                _same_block_copy(_block(lz[a], r, chip), fsend_ref.at[j], frecv_ref.at[j], me).wait_send()
            _same_block_copy(_block(lz[a], r, origin), rsend_ref.at[0], recv0.at[0], me).wait_send()
            _same_block_copy(_block(lz[a], r, (1 - x, 1 - y, c)), f2send_ref.at[0], f2recv_ref.at[0], me).wait_send()

    return _split_call(name, body, list(lands), [send_sems, recv_sems, fsend, frecv, rsend, f2send, f2recv], after, [],
                       False)


def _sibling_start(name, grads, after):
    n = len(grads)
    lands = [_landing((4, g.shape[0] // N_DEV, D), g.dtype) for g in grads]

    def body(*refs):
        ins, lz = refs[:n], refs[n:2 * n]
        send_sem, recv_sem = refs[2 * n + len(after)], refs[2 * n + len(after) + 1]
        token = refs[-1]
        x, y, c = _position()
        for a in range(n):
            r = grads[a].shape[0] // N_DEV
            for q in range(4):
                pltpu.make_async_remote_copy(
                    src_ref=ins[a].at[pl.ds((2 * q + 1 - c) * r, r), :], dst_ref=lz[a].at[q], send_sem=send_sem.at[0],
                    recv_sem=recv_sem.at[0], device_id=(x, y, 1 - c), device_id_type=MESH).start()
        token[...] = jnp.zeros_like(token)

    out = _split_call(name, body, list(grads) + lands, [], after, [(1,), (1,)], True)
    return out[0], out[1], out[2:2 + n], out[2 + n:2 + 2 * n], out[-1]


def _sibling_finish(name, grads, lands, send_sem, recv_sem, after):
    n = len(grads)

    def body(*refs):
        ins, lz = refs[:n], refs[n:2 * n]
        send_ref, recv_ref = refs[2 * n], refs[2 * n + 1]
        x, y, c = _position()
        for a in range(n):
            r = grads[a].shape[0] // N_DEV
            for q in range(4):
                cp = pltpu.make_async_remote_copy(
                    src_ref=ins[a].at[pl.ds((2 * q + 1 - c) * r, r), :], dst_ref=lz[a].at[q], send_sem=send_ref.at[0],
                    recv_sem=recv_ref.at[0], device_id=(x, y, c), device_id_type=MESH)
                cp.wait_send()
                cp.wait_recv()

    out = _split_call(name, body, list(grads) + list(lands), [send_sem, recv_sem], after, [], False)
    return out[:n], out[n:2 * n]


def _chip_start(name, parts, after):
    n = len(parts)
    lands = [_landing((3,) + p.shape[1:], p.dtype) for p in parts]

    def body(*refs):
        ins, lz = refs[:n], refs[n:2 * n]
        send_sems, recv_sems = refs[2 * n + len(after)], refs[2 * n + len(after) + 1]
        token = refs[-1]
        x, y, c = _position()
        for a in range(n):
            for j, chip in enumerate(_gather_peers(x, y, c)[:3]):
                pltpu.make_async_remote_copy(
                    src_ref=ins[a].at[2 * chip[0] + chip[1]], dst_ref=lz[a].at[j], send_sem=send_sems.at[j],
                    recv_sem=recv_sems.at[j], device_id=chip, device_id_type=MESH).start()
        token[...] = jnp.zeros_like(token)

    out = _split_call(name, body, list(parts) + lands, [], after, [(3,), (3,)], True)
    return out[0], out[1], out[2:2 + n], out[2 + n:2 + 2 * n], out[-1]


def _chip_finish(name, parts, lands, send_sems, recv_sems, after):
    n = len(parts)

    def body(*refs):
        ins, lz = refs[:n], refs[n:2 * n]
        send_ref, recv_ref = refs[2 * n], refs[2 * n + 1]
        me = _position()
        for a in range(n):
            for j in range(3):
                cp = pltpu.make_async_remote_copy(
                    src_ref=ins[a].at[j], dst_ref=lz[a].at[j], send_sem=send_ref.at[j], recv_sem=recv_ref.at[j],
                    device_id=me, device_id_type=MESH)
                cp.wait_send()
                cp.wait_recv()

    out = _split_call(name, body, list(parts) + list(lands), [send_sems, recv_sems], after, [], False)
    return out[:n], out[n:2 * n]


def _other_devices(x, y, c):
    return [(x + (k >> 2 & 1) * (1 - 2 * x), y + (k >> 1 & 1) * (1 - 2 * y), c + (k & 1) * (1 - 2 * c))
            for k in range(1, N_DEV)]


def _broadcast_start(name, arrays, after):
    n = len(arrays)
    lands = [_landing((N_DEV,) + a.shape, a.dtype) for a in arrays]

    def body(*refs):
        ins, lz = refs[:n], refs[n:2 * n]
        send_sems, recv_sems = refs[2 * n + len(after)], refs[2 * n + len(after) + 1]
        token = refs[-1]
        x, y, c = _position()
        for a in range(n):
            for k, peer in enumerate(_other_devices(x, y, c)):
                pltpu.make_async_remote_copy(
                    src_ref=ins[a], dst_ref=lz[a].at[4 * x + 2 * y + c], send_sem=send_sems.at[k],
                    recv_sem=recv_sems.at[k], device_id=peer, device_id_type=MESH).start()
        token[...] = jnp.zeros_like(token)

    out = _split_call(name, body, list(arrays) + lands, [], after, [(N_DEV - 1,), (N_DEV - 1,)], True)
    return out[0], out[1], out[2:2 + n], out[2 + n:2 + 2 * n], out[-1]


def _broadcast_finish(name, arrays, lands, send_sems, recv_sems, after):
    n = len(arrays)

    def body(*refs):
        ins, lz = refs[:n], refs[n:2 * n]
        send_ref, recv_ref = refs[2 * n], refs[2 * n + 1]
        x, y, c = _position()
        for a in range(n):
            for k, peer in enumerate(_other_devices(x, y, c)):
                cp = pltpu.make_async_remote_copy(
                    src_ref=ins[a], dst_ref=lz[a].at[4 * peer[0] + 2 * peer[1] + peer[2]], send_sem=send_ref.at[k],
                    recv_sem=recv_ref.at[k], device_id=(x, y, c), device_id_type=MESH)
                cp.wait_send()
                cp.wait_recv()

    out = _split_call(name, body, list(arrays) + list(lands), [send_sems, recv_sems], after, [], False)
    return out[:n], out[n:2 * n]


def _row_tile(r, target):
    return max(t for t in range(16, min(r, target) + 1, 16) if r % t == 0)


def _chip_partial(name, grad, got, c, tr):
    r = grad.shape[0] // N_DEV
    tr = _row_tile(r, tr)
    g4 = grad.reshape(4, 2, r, D)

    def body(c_ref, g_ref, s_ref, o_ref):
        o_ref[...] = (g_ref[...].astype(F32) + s_ref[...].astype(F32)).astype(BF16)

    return pl.pallas_call(
        body, name=name,
        grid_spec=pltpu.PrefetchScalarGridSpec(
            num_scalar_prefetch=1, grid=(4, r // tr),
            in_specs=[pl.BlockSpec((None, None, tr, D), lambda q, i, c_ref: (q, c_ref[0], i, 0)),
                      pl.BlockSpec((None, tr, D), lambda q, i, c_ref: (q, i, 0))],
            out_specs=pl.BlockSpec((None, tr, D), lambda q, i, c_ref: (q, i, 0))),
        out_shape=jax.ShapeDtypeStruct((4, r, D), BF16), compiler_params=_cparams(),
    )(c, g4, got)


class _WeightGather:
    def __init__(self, groups):
        self.groups = list(groups)
        self.index = {key: i for i, (key, _, _) in enumerate(groups)}
        self.state = [None] * len(groups)
        self.token = ()
        for i in range(min(2, len(groups))):
            self._start(i)

    def _tag(self, i):
        return "%s_%d" % self.groups[i][0][::-1]

    def _start(self, i):
        send, recv, lz, tok = _gather_start("gather_start_" + self._tag(i), self.groups[i][2], self.token)
        self.state[i] = dict(send=send, recv=recv, lands=lz)
        self.token = (tok,)

    def _step(self, name, near, far, fresh, marker):
        exists = lambda i: i is not None and i < len(self.groups)
        near, far, fresh = (i if exists(i) else None for i in (near, far, fresh))
        res = _gather_step(
            name, None if near is None else (self.state[near]["lands"], self.state[near]["recv"]),
            None if far is None else (self.state[far]["lands"], self.state[far]["rrecv"]),
            None if fresh is None else self.groups[fresh][2], tuple(marker) + self.token)
        self.token = (res["token"],)
        if near is not None:
            self.state[near].update(lands=res["near"], fsend=res["fsend"], frecv=res["frecv"], rsend=res["rsend"],
                                    rrecv=res["rrecv"])
        if far is not None:
            self.state[far].update(lands=res["far"], f2send=res["f2send"], f2recv=res["f2recv"])
        if fresh is not None:
            self.state[fresh] = dict(send=res["send"], recv=res["recv"], lands=res["fresh"])

    def fetch(self, layer, group, marker):
        k = self.index[(layer, group)]
        if k == 0:
            self._step("gather_step_first", 0, None, None, marker)
        self._step("gather_step_" + self._tag(k), k + 1, k, k + 2, marker)
        st = self.state[k]
        lz = _gather_finish("gather_finish_" + self._tag(k), st["lands"], st["send"], st["recv"], st["fsend"],
                            st["frecv"], st["rsend"], st["f2send"], st["f2recv"], self.token)
        self.state[k] = None
        return dict(zip(self.groups[k][1], lz))


class _GradReduce:
    def __init__(self, core, chip):
        self.core, self.chip = core, chip
        self.layer = None
        self.token = ()
        self.at_sibling, self.at_chips = [], []
        self.extra, self.smalls = (), {}

    def after(self):
        return self.token

    def add(self, group, names, grads):
        tag = "%s_%d" % (group, self.layer)
        send, recv, grads, lands, tok = _sibling_start("grad_sibling_start_" + tag, grads, self.token)
        self.at_sibling.append((tag, [(self.layer, n) for n in names], send, recv, grads, lands))
        self.token = (tok,)

    def advance(self, marker):
        for tag, keys, send, recv, grads, lands in self.at_sibling:
            grads, lands = _sibling_finish("grad_sibling_finish_" + tag, grads, lands, send, recv, marker)
            parts = [_chip_partial("chip_partial_%d_%s" % key, g, got, self.core, 512)
                     for key, g, got in zip(keys, grads, lands)]
            send, recv, parts, lands, tok = _chip_start("grad_chip_start_" + tag, parts, ())
            self.at_chips.append([tag, keys, send, recv, parts, lands])
            self.token = (tok,)
        self.at_sibling = []

    def small(self, part, arrays):
        keys = list(arrays)
        send, recv, own, slots, tok = _broadcast_start(
            "small_grads_start_%d_%s" % (self.layer, part), [arrays[k] for k in keys], self.token)
        self.smalls.setdefault(self.layer, []).append((part, keys, send, recv, own, slots))
        self.token = (tok,)

    def small_finish(self, layer, marker):
        mine, theirs = {}, {}
        for part, keys, send, recv, own, slots in self.smalls[layer]:
            own, slots = _broadcast_finish("small_grads_finish_%d_%s" % (layer, part), own, slots, send, recv, marker)
            mine.update(zip(keys, own))
            theirs.update(zip(keys, slots))
        return mine, theirs

    def collect(self, key, marker):
        for entry in self.at_chips:
            tag, keys, send, recv, parts, lands = entry
            if key in keys:
                if send is not None:
                    parts, lands = _chip_finish("grad_chip_finish_" + tag, parts, lands, send, recv, marker)
                    entry[2:] = [None, None, parts, lands]
                i = keys.index(key)
                return parts[i], lands[i]
        raise KeyError(key)


def _adamw_math(w, g, m, v):
    m = ADAM_B1 * m + (1.0 - ADAM_B1) * g
    v = ADAM_B2 * v + (1.0 - ADAM_B2) * jnp.square(g)
    m_hat = m / (1.0 - ADAM_B1 ** ADAM_STEP)
    v_hat = v / (1.0 - ADAM_B2 ** ADAM_STEP)
    delta = -ADAM_LR * (m_hat / (jnp.sqrt(v_hat) + ADAM_EPS) + ADAM_WD * w)
    return delta, m, v


def _adamw_small(wts, mom_m, mom_v, own, gathered, loss_own, loss_gathered, dev):
    names = SMALL
    nw = len(names)
    na = len(SMALL_ARRAYS)

    def body(dev_ref, *refs):
        w_refs, m_refs, v_refs = (dict(zip(names, refs[i * nw:(i + 1) * nw])) for i in range(3))
        own_refs = refs[3 * nw:3 * nw + DEPTH * na]
        g_refs = refs[3 * nw + DEPTH * na:3 * nw + 2 * DEPTH * na]
        loss_own_ref, loss_got_ref = refs[3 * nw + 2 * DEPTH * na:3 * nw + 2 * DEPTH * na + 2]
        outs = refs[3 * nw + 2 * DEPTH * na + 2:]
        g_out, d_out, m_out, v_out = (dict(zip(names, outs[i * nw:(i + 1) * nw])) for i in range(4))
        me = dev_ref[0]

        loss = None
        for d in range(N_DEV):
            for b in range(loss_own.shape[0]):
                term = jnp.where(me == d, loss_own_ref[b], loss_got_ref[d, b])
                loss = term if loss is None else loss + term
        outs[4 * nw][...] = loss

        def update(name, at, g):
            g_out[name][at] = g
            d_out[name][at], m_out[name][at], v_out[name][at] = _adamw_math(
                w_refs[name][at], g, m_refs[name][at], v_refs[name][at])

        for l in range(DEPTH):
            mine = dict(zip(SMALL_ARRAYS, own_refs[l * na:(l + 1) * na]))
            got = dict(zip(SMALL_ARRAYS, g_refs[l * na:(l + 1) * na]))

            def total(key, at):
                acc = None
                for d in range(N_DEV):
                    term = jnp.where(me == d, mine[key][at] if at else mine[key][...], got[key][(d,) + at])
                    acc = term if acc is None else acc + term
                return acc

            row = (slice(l, l + 1),)
            for k, name in enumerate(NORM_NAMES):
                update(name, row, total("norms", (slice(k, k + 1),)))
            for k, name in enumerate(VEC_NAMES):
                update(name, row, total("vecs", (slice(k, k + 1),)))
            update("gmlp_v_gain", (l,), total("gain_bias", (slice(0, NH),)))
            update("b_spatial", (l,), total("gain_bias", (slice(NH, 2 * NH),)))
            update("w_spatial", (l,), total("w_spatial", ()))
            update("w_pool", (l,), total("w_pool", ()))
            update("w_dw", (l,), total("w_dw", (slice(0, CONV_K),)))

    args = [src[n] for src in (wts, mom_m, mom_v) for n in names]
    args += [src[l][k] for src in (own, gathered) for l in range(DEPTH) for k in SMALL_ARRAYS]
    args += [loss_own, loss_gathered]
    outs = pl.pallas_call(
        body, name="adamw_small",
        in_specs=[pl.BlockSpec(memory_space=pltpu.SMEM)] + [pl.BlockSpec(memory_space=pltpu.VMEM)] * len(args),
        out_shape=[jax.ShapeDtypeStruct(wts[n].shape, F32) for _ in range(4) for n in names]
        + [jax.ShapeDtypeStruct((8, LANES), F32)],
        compiler_params=_cparams(),
    )(dev, *args)
    return tuple(dict(zip(names, outs[i * nw:(i + 1) * nw])) for i in range(4)) + (outs[4 * nw],)


def _adamw_layers(name, w, reduced, m, v, chip, tr, transposed=False, after=()):
    nl, r, cdim = w.shape
    tr = _row_tile(r, tr)
    nb = r // tr

    def body(q_ref, w_ref, p0_ref, g0_ref, p1_ref, g1_ref, m_ref, v_ref, *rest):
        g_ref, d_ref, nm_ref, nv_ref = rest[len(after):]

        def total(p_ref, got_ref):
            acc = p_ref[...].astype(F32)
            for j in range(3):
                acc = acc + got_ref[j].astype(F32)
            return acc

        g = jnp.where(pl.program_id(0) == 0, total(p0_ref, g0_ref), total(p1_ref, g1_ref))
        if transposed:
            g = g.T
        g_ref[...] = g
        d_ref[...], nm_ref[...], nv_ref[...] = _adamw_math(w_ref[...], g, m_ref[...], v_ref[...])

    blk = pl.BlockSpec((None, tr, cdim), lambda l, i, q: (l, i, 0))
    first = lambda l, i: i * (1 - l) + (nb - 1) * l
    second = lambda l, i: i * l
    if transposed:
        gshape = (cdim, tr)
        at = lambda lead, i: (lead, 0, i)
    else:
        gshape = (tr, cdim)
        at = lambda lead, i: (lead, i, 0)
    specs = [blk,
             pl.BlockSpec((None,) + gshape, lambda l, i, q: at(q[0], first(l, i))),
             pl.BlockSpec((3,) + gshape, lambda l, i, q: at(0, first(l, i))),
             pl.BlockSpec((None,) + gshape, lambda l, i, q: at(q[0], second(l, i))),
             pl.BlockSpec((3,) + gshape, lambda l, i, q: at(0, second(l, i))), blk, blk] + [ANY] * len(after)
    shape = jax.ShapeDtypeStruct((nl, r, cdim), F32)
    return pl.pallas_call(
        body, name=name,
        grid_spec=pltpu.PrefetchScalarGridSpec(num_scalar_prefetch=1, grid=(nl, nb), in_specs=specs, out_specs=[blk] * 4),
        out_shape=[shape] * 4, compiler_params=_cparams(),
    )(chip, w, *reduced[0], *reduced[1], m, v, *after)


def _to_rows(name, a):
    return jnp.swapaxes(a, 1, 2) if name == "w_in" else a


def _place_own_transposed(name, srcs, dev, out_dtype, tc):
    n = len(srcs)
    kdim, cdim = srcs[0][0].shape[-2:]

    def body(dev_ref, *refs):
        for a in range(n):
            refs[n + a][...] = refs[a][...].T.astype(out_dtype)

    return pl.pallas_call(
        body, name=name,
        grid_spec=pltpu.PrefetchScalarGridSpec(
            num_scalar_prefetch=1, grid=(kdim // tc,),
            in_specs=[pl.BlockSpec((None, tc, cdim), lambda i, d, l=l: (l, i, 0)) for _, l in srcs],
            out_specs=[pl.BlockSpec((cdim, tc), lambda i, d: (d[0], i))] * n),
        out_shape=[jax.ShapeDtypeStruct((N_DEV * cdim, kdim), out_dtype)] * n, compiler_params=_cparams(),
    )(dev, *[a for a, _ in srcs])


def _pack(arrays, rows):
    flat = jnp.concatenate([a.reshape(-1) for a in arrays])
    return jnp.pad(flat, (0, rows * D - flat.shape[0])).reshape(rows, D)


def _rows_for(shapes, mult=8):
    total = 0
    for shp in shapes:
        size = 1
        for dim in shp:
            size *= dim
        total += size
    return -(-total // (mult * D)) * mult


def kernel(x, mem, norm_mix_pre, norm_mix_post, w_in, w_out, gmlp_v_gain, w_spatial, b_spatial, w_pool, s_pool, w_dw, b_dw, conv_ln_g, conv_ln_b, norm_xattn_pre, norm_mem, norm_xattn_post, w_q, w_k, w_v, w_o, norm_ffn_pre, norm_ffn_post, w_up, w_down, loss_target, m_norm_mix_pre, m_norm_mix_post, m_w_in, m_w_out, m_gmlp_v_gain, m_w_spatial, m_b_spatial, m_w_pool, m_s_pool, m_w_dw, m_b_dw, m_conv_ln_g, m_conv_ln_b, m_norm_xattn_pre, m_norm_mem, m_norm_xattn_post, m_w_q, m_w_k, m_w_v, m_w_o, m_norm_ffn_pre, m_norm_ffn_post, m_w_up, m_w_down, v_norm_mix_pre, v_norm_mix_post, v_w_in, v_w_out, v_gmlp_v_gain, v_w_spatial, v_b_spatial, v_w_pool, v_s_pool, v_w_dw, v_b_dw, v_conv_ln_g, v_conv_ln_b, v_norm_xattn_pre, v_norm_mem, v_norm_xattn_post, v_w_q, v_w_k, v_w_v, v_w_o, v_norm_ffn_pre, v_norm_ffn_post, v_w_up, v_w_down):
    args = dict(locals())
    wts = {n: args[n] for n in WEIGHTS}
    mom_m = {n: args["m_" + n] for n in WEIGHTS}
    mom_v = {n: args["v_" + n] for n in WEIGHTS}
    xi, yi, ci = _position()
    me = 4 * xi + 2 * yi + ci

    dev = jnp.reshape(me, (1,)).astype(jnp.int32)
    lands = {}
    for call, names, tr in (("place_in", ("w_in",), 256), ("place_att", ("w_out", "w_q", "w_k", "w_v", "w_o"), 64),
                            ("place_up", ("w_up",), 256), ("place_down", ("w_down",), 256)):
        srcs = [(_to_rows(n, wts[n]), l) for l in range(DEPTH) for n in names]
        placed = (_place_own_transposed if names == ("w_up",) else _place_own)(call, srcs, dev, BF16, tr)
        lands.update(zip([(l, n) for l in range(DEPTH) for n in names], placed))
    (lands[(0, "taps")],) = _place_own("place_taps", [(_pack([w_dw], _rows_for([w_dw.shape])), None)], dev, F32, 8)
    groups = []
    for l in range(DEPTH):
        for group, names in GATHER_GROUPS:
            if (l, group) == (0, "in"):
                names = names + ("taps",)
            groups.append(((l, group), names, [lands[(l, n)] for n in names]))
    gather = _WeightGather(groups)

    def fetch(layer, group, marker):
        w = gather.fetch(layer, group, marker)
        if "taps" in w:
            blocks = w["taps"].reshape(N_DEV, -1)[:, :w_dw.size].reshape((N_DEV,) + w_dw.shape)
            w["taps"] = jnp.moveaxis(blocks, 0, 2).reshape(DEPTH, CONV_K, CW)
        return w

    reduce = _GradReduce(jnp.reshape(ci, (1,)).astype(jnp.int32), jnp.reshape(2 * xi + yi, (1,)).astype(jnp.int32))
    small = {n: wts[n] for n in SMALL if n != "w_dw"}
    _, dx = _local_step(x[0], mem[0], loss_target[0], fetch, small, reduce)
    reduce.advance((dx,))

    grad_w, delta, new_m, new_v = {}, {}, {}, {}
    marker = (dx,) + tuple(reduce.after())
    for n in UPDATE_ORDER:
        reduced = [reduce.collect((l, n), marker) for l in range(DEPTH)]
        outs = _adamw_layers("adamw_" + n, _to_rows(n, wts[n]), reduced, _to_rows(n, mom_m[n]), _to_rows(n, mom_v[n]),
                             reduce.chip, 256, transposed=n == "w_up", after=marker)
        grad_w[n], delta[n], new_m[n], new_v[n] = (_to_rows(n, o) for o in outs)
        marker = (outs[1],)

    own, slots = [None] * DEPTH, [None] * DEPTH
    for l in reversed(range(DEPTH)):
        own[l], slots[l] = reduce.small_finish(l, marker)
        if l == 0:
            loss_own, loss_slots = own[l].pop("loss"), slots[l].pop("loss")
    shard_cols = CW // N_DEV
    for l in range(DEPTH):
        own[l]["w_dw"] = lax.dynamic_slice_in_dim(own[l]["w_dw"], me * shard_cols, shard_cols, axis=1)
        slots[l]["w_dw"] = lax.dynamic_slice_in_dim(slots[l]["w_dw"], me * shard_cols, shard_cols, axis=2)
    *small_out, loss_tile = _adamw_small(wts, mom_m, mom_v, own, slots, loss_own, loss_slots, dev)
    for dst, src in zip((grad_w, delta, new_m, new_v), small_out):
        dst.update(src)

    return (loss_tile[0, 0], dx[None], *[grad_w[n] for n in WEIGHTS], *[delta[n] for n in WEIGHTS],
            *[new_m[n] for n in WEIGHTS], *[new_v[n] for n in WEIGHTS])
```

```python
import functools

import jax
import jax.numpy as jnp
from jax import lax
from jax.experimental import pallas as pl
from jax.experimental.pallas import tpu as pltpu

F32 = jnp.float32
BF16 = jnp.bfloat16

D = 2048
GW = 1024
PW = 512
CW = 512
HD = 128
NH = 8
NG = 4
POOL_WINDOWS = (2, 4, 8, 16)
CONV_K = 31
IN_COLS = 2 * GW + PW + 2 * CW
DFF = 4 * D
XH = 4
XHD = D // XH
ATT_SCALE = XHD ** -0.5
RMS_EPS = 1e-6
LN_EPS = 1e-5
DEPTH = 2
N_DEV = 8

ADAM_LR = 0.001
ADAM_B1 = 0.9
ADAM_B2 = 0.999
ADAM_EPS = 1e-08
ADAM_WD = 0.01
ADAM_STEP = 10

LANES = 128
CONV_HALO = 32
POOL_HALO = 16
ROW_TILE = 128
VMEM_LIMIT = 60 * 1024 * 1024

MESH = pl.DeviceIdType.MESH
NT = (((1,), (1,)), ((), ()))
NN = (((1,), (0,)), ((), ()))
TN = (((0,), (0,)), ((), ()))

BIG = ("w_out", "w_q", "w_k", "w_v", "w_o", "w_up", "w_down", "w_in")
UPDATE_ORDER = ("w_down", "w_up", "w_o", "w_q", "w_k", "w_v", "w_out", "w_in")
GATHER_GROUPS = (("in", ("w_in",)), ("out", ("w_out",)), ("att", ("w_q", "w_k", "w_v", "w_o")), ("up", ("w_up",)),
                 ("down", ("w_down",)))
SMALL = ("norm_mix_pre", "norm_mix_post", "gmlp_v_gain", "w_spatial", "b_spatial", "w_pool", "s_pool",
         "w_dw", "b_dw", "conv_ln_g", "conv_ln_b", "norm_xattn_pre", "norm_mem", "norm_xattn_post",
         "norm_ffn_pre", "norm_ffn_post")
WEIGHTS = ("norm_mix_pre", "norm_mix_post", "w_in", "w_out", "gmlp_v_gain", "w_spatial", "b_spatial", "w_pool",
           "s_pool", "w_dw", "b_dw", "conv_ln_g", "conv_ln_b", "norm_xattn_pre", "norm_mem", "norm_xattn_post",
           "w_q", "w_k", "w_v", "w_o", "norm_ffn_pre", "norm_ffn_post", "w_up", "w_down")


def _cparams():
    return pltpu.CompilerParams(vmem_limit_bytes=VMEM_LIMIT)


def _dot(a, b, dims):
    return lax.dot_general(a, b, dims, preferred_element_type=F32)


def _rms(x, g):
    y = x * lax.rsqrt(jnp.mean(x * x, axis=-1, keepdims=True) + RMS_EPS)
    return y * g


def _rms_bwd(x, g, dy):
    r = lax.rsqrt(jnp.mean(x * x, axis=-1, keepdims=True) + RMS_EPS)
    xh = x * r
    t = dy * g
    dx = r * (t - xh * jnp.mean(t * xh, axis=-1, keepdims=True))
    return dx, jnp.sum(dy * xh, axis=0, keepdims=True)


def _gelu(x):
    cdf = 0.5 * (1.0 + jnp.tanh(0.7978845608028654 * (x + 0.044715 * (x * x * x))))
    return x * cdf


def _layer_norm(x, g, b=None):
    mu = jnp.mean(x, axis=-1, keepdims=True)
    xc = x - mu
    var = jnp.mean(xc * xc, axis=-1, keepdims=True)
    y = xc * lax.rsqrt(var + LN_EPS) * g
    return y if b is None else y + b


def _sigmoid(x):
    return 1.0 / (1.0 + jnp.exp(-x))


def _gmlp_rows(zu, zv, gv):
    return _gelu(zu), _layer_norm(_gelu(zv), gv)


def _glu(cv, cg):
    return cv * _sigmoid(cg)


def _ln_silu(h, g, b):
    y = _layer_norm(h, g, b)
    return y * _sigmoid(y)


ANY = pl.BlockSpec(memory_space=pl.ANY)


ROWS_TILE = 256
COLS_TILE = 512
DW_TILE = 512
RESIDENT_K = 2048
STREAM_K_TILE = 1024
STREAM_ROWS = 512


def _k_tiles(kdim):
    if kdim <= RESIDENT_K:
        return ROWS_TILE, kdim
    return STREAM_ROWS, max(t for t in range(LANES, STREAM_K_TILE + 1, LANES) if kdim % t == 0)


def _rowop_mm(name, kind, rows, g, w, dims, out_dtype, u=None, after=()):
    s = rows[0].shape[0]
    n = w.shape[0] if dims == NT else w.shape[1]
    tm, tn = min(ROWS_TILE, s), min(COLS_TILE, n)
    ni = s // tm
    bwd = kind == "rms_bwd"

    def rows_body(*refs):
        refs = list(refs)
        row_refs = [refs.pop(0) for _ in rows]
        g_ref = refs.pop(0)
        del refs[:len(after)]
        if bwd:
            a, dg = _rms_bwd(row_refs[0][...], g_ref[...], row_refs[1][...])
            refs[1][0] = dg
        else:
            a = _rms(row_refs[0][...], g_ref[...])
        refs[0][...] = a.astype(BF16)

    row_spec = pl.BlockSpec((tm, D), lambda i: (i, 0))
    res = pl.pallas_call(
        rows_body, name=name + "_rows", grid=(ni,),
        in_specs=[row_spec] * len(rows) + [pl.BlockSpec((1, D), lambda i: (0, 0))] + [ANY] * len(after),
        out_specs=[row_spec] + ([pl.BlockSpec((1, 1, D), lambda i: (i, 0, 0))] if bwd else []),
        out_shape=[jax.ShapeDtypeStruct((s, D), BF16)] + ([jax.ShapeDtypeStruct((ni, 1, D), F32)] if bwd else []),
        compiler_params=_cparams(),
    )(*rows, g, *after)
    a = res[0]

    def body(a_ref, w_ref, *rest):
        acc = _dot(a_ref[...], w_ref[...], dims)
        if u is not None:
            acc = acc * (2.0 * jnp.maximum(rest[0][...], 0.0))
        rest[-1][...] = acc.astype(out_dtype)

    w_spec = pl.BlockSpec((tn, D), lambda j: (j, 0)) if dims == NT else pl.BlockSpec((D, tn), lambda j: (0, j))
    tile = pl.BlockSpec((s, tn), lambda j: (0, j))
    out = pl.pallas_call(
        body, name=name, grid=(n // tn,),
        in_specs=[pl.BlockSpec((s, D), lambda j: (0, 0)), w_spec] + ([tile] if u is not None else []),
        out_specs=tile, out_shape=jax.ShapeDtypeStruct((s, n), out_dtype), compiler_params=_cparams(),
    )(a, w, *([u] if u is not None else []))
    return (out, *res)


def _mm_rowop(name, kind, pairs, rows, g, relu2=False, after=()):
    s, kdim = pairs[0][0].shape
    tm, tk = _k_tiles(kdim)
    tm = min(tm, s)
    ni, nk = s // tm, kdim // tk
    npair = len(pairs)

    def body(*refs):
        refs = list(refs)
        a_refs = [refs.pop(0) for _ in range(npair)]
        w_refs = [refs.pop(0) for _ in range(npair)]
        row_refs = [refs.pop(0) for _ in rows]
        g_ref = refs.pop(0)
        del refs[:len(after)]
        acc = refs.pop()
        outs = refs
        k = pl.program_id(1)

        @pl.when(k == 0)
        def _():
            acc[...] = jnp.zeros_like(acc)

        for a_ref, w_ref, (_, _, dims) in zip(a_refs, w_refs, pairs):
            a = a_ref[...]
            if relu2:
                a = jnp.square(jnp.maximum(a, 0.0))
            acc[...] += _dot(a.astype(BF16), w_ref[...], dims)

        @pl.when(k == nk - 1)
        def _():
            h = acc[...]
            if kind == "rms_res":
                outs[0][...] = row_refs[0][...] + _rms(h, g_ref[...])
                outs[1][...] = h
            else:
                dx, dg = _rms_bwd(row_refs[0][...], g_ref[...], h)
                if kind == "rms_bwd_res":
                    outs[0][...] = row_refs[1][...] + dx
                    outs[1][0] = dg
                else:
                    outs[0][0] = dg

    row_spec = pl.BlockSpec((tm, D), lambda i, k: (i, 0))
    dg_shape = jax.ShapeDtypeStruct((ni, 1, D), F32)
    dg_spec = pl.BlockSpec((1, 1, D), lambda i, k: (i, 0, 0))
    in_specs = [pl.BlockSpec((tm, tk), lambda i, k: (i, k))] * npair
    for _, _, dims in pairs:
        in_specs.append(pl.BlockSpec((tk, D), lambda i, k: (k, 0)) if dims == NN
                        else pl.BlockSpec((D, tk), lambda i, k: (0, k)))
    in_specs += [row_spec] * len(rows) + [pl.BlockSpec((1, D), lambda i, k: (0, 0))] + [ANY] * len(after)
    if kind == "rms_res":
        out_shape = [jax.ShapeDtypeStruct((s, D), F32)] * 2
        out_specs = [row_spec, row_spec]
    elif kind == "rms_bwd_res":
        out_shape = [jax.ShapeDtypeStruct((s, D), F32), dg_shape]
        out_specs = [row_spec, dg_spec]
    else:
        out_shape = [dg_shape]
        out_specs = [dg_spec]
    return pl.pallas_call(
        body, name=name, grid=(ni, nk), in_specs=in_specs, out_specs=out_specs, out_shape=out_shape,
        scratch_shapes=[pltpu.VMEM((tm, D), F32)], compiler_params=_cparams(),
    )(*[p[0] for p in pairs], *[p[1] for p in pairs], *rows, g, *after)


def _mm_tn(name, a, gmat, relu2=False, after=()):
    s, m = a.shape
    tm = min(DW_TILE, m)
    ni = m // tm

    def body(a_ref, g_ref, *rest):
        av = a_ref[...]
        if relu2:
            av = jnp.square(jnp.maximum(av, 0.0))
        rest[len(after)][...] = _dot(av.astype(BF16), g_ref[...], TN).astype(BF16)

    return pl.pallas_call(
        body, name=name, grid=(ni,),
        in_specs=[pl.BlockSpec((s, tm), lambda i: (0, i)), pl.BlockSpec((s, D), lambda i: (0, 0))] + [ANY] * len(after),
        out_specs=pl.BlockSpec((tm, D), lambda i: (i, 0)),
        out_shape=jax.ShapeDtypeStruct((m, D), BF16), compiler_params=_cparams(),
    )(a, gmat, *after)


def _tril():
    r = lax.broadcasted_iota(jnp.int32, (HD, HD), 0)
    c = lax.broadcasted_iota(jnp.int32, (HD, HD), 1)
    return (c <= r).astype(F32)


def _gmlp_fwd(z, gv, ws, bst, tb):
    s = z.shape[0]
    tb = min(tb, s)

    def body(zu_ref, zv_ref, gv_ref, ws_ref, bst_ref, y_ref):
        tril = _tril()
        for h in range(NH):
            cols = slice(h * HD, (h + 1) * HD)
            u, vln = _gmlp_rows(zu_ref[:, cols], zv_ref[:, cols], gv_ref[h:h + 1, :])
            wm = (ws_ref[h] * tril).astype(BF16)
            vb = vln.astype(BF16)
            for c in range(tb // HD):
                rws = slice(c * HD, (c + 1) * HD)
                mixed = _dot(wm, vb[rws], NN) + bst_ref[:, h:h + 1]
                y_ref[rws, cols] = (u[rws] * mixed).astype(BF16)

    return pl.pallas_call(
        body, name="gmlp_fwd", grid=(s // tb,),
        in_specs=[pl.BlockSpec((tb, GW), lambda i: (i, 0)), pl.BlockSpec((tb, GW), lambda i: (i, 1)),
                  pl.BlockSpec((NH, HD), lambda i: (0, 0)), pl.BlockSpec((NH, HD, HD), lambda i: (0, 0, 0)),
                  pl.BlockSpec((HD, NH), lambda i: (0, 0))],
        out_specs=pl.BlockSpec((tb, GW), lambda i: (i, 0)),
        out_shape=jax.ShapeDtypeStruct((s, D), BF16), compiler_params=_cparams(),
    )(z, z, gv, ws, bst)


def _gmlp_bwd(z, dy, gv, ws, bst, tb, after=()):
    s = z.shape[0]
    tb = min(tb, s)
    nb = s // tb

    def body(zu_ref, zv_ref, dy_ref, gv_ref, ws_ref, bst_ref, *rest):
        dz_ref, dgv_ref, dws_ref, db_ref = rest[len(after):]
        tril = _tril()
        for h in range(NH):
            cols = slice(h * HD, (h + 1) * HD)
            (u, vln), vjp = jax.vjp(_gmlp_rows, zu_ref[:, cols], zv_ref[:, cols], gv_ref[h:h + 1, :])
            wmf = ws_ref[h] * tril
            wm = wmf.astype(BF16)
            wmt = wmf.T.astype(BF16)
            vb = vln.astype(BF16)
            dws = jnp.zeros((HD, HD), F32)
            db = jnp.zeros((HD, 1), F32)
            du_parts, dvln_parts = [], []
            for c in range(tb // HD):
                rws = slice(c * HD, (c + 1) * HD)
                mixed = _dot(wm, vb[rws], NN) + bst_ref[:, h:h + 1]
                dyc = dy_ref[rws, cols]
                du_parts.append(dyc * mixed)
                dmixed = dyc * u[rws]
                dmb = dmixed.astype(BF16)
                dws = dws + _dot(dmb, vb[rws], NT)
                db = db + jnp.sum(dmixed, axis=1, keepdims=True)
                dvln_parts.append(_dot(wmt, dmb, NN))
            du = jnp.concatenate(du_parts, axis=0)
            dvln = jnp.concatenate(dvln_parts, axis=0)
            dzu, dzv, dgv = vjp((du, dvln))
            dz_ref[:, cols] = dzu.astype(BF16)
            dz_ref[:, slice(GW + h * HD, GW + (h + 1) * HD)] = dzv.astype(BF16)
            dgv_ref[0, h:h + 1, :] = dgv
            dws_ref[0, h] = dws * tril
            db_ref[0, h] = jnp.broadcast_to(db, (HD, LANES))

    blk = pl.BlockSpec((tb, GW), lambda i: (i, 0))
    return pl.pallas_call(
        body, name="gmlp_bwd", grid=(nb,),
        in_specs=[blk, pl.BlockSpec((tb, GW), lambda i: (i, 1)), blk,
                  pl.BlockSpec((NH, HD), lambda i: (0, 0)), pl.BlockSpec((NH, HD, HD), lambda i: (0, 0, 0)),
                  pl.BlockSpec((HD, NH), lambda i: (0, 0))] + [ANY] * len(after),
        out_specs=[pl.BlockSpec((tb, 2 * GW), lambda i: (i, 0)), pl.BlockSpec((1, NH, HD), lambda i: (i, 0, 0)),
                   pl.BlockSpec((1, NH, HD, HD), lambda i: (i, 0, 0, 0)),
                   pl.BlockSpec((1, NH, HD, LANES), lambda i: (i, 0, 0, 0))],
        out_shape=[jax.ShapeDtypeStruct((s, IN_COLS), BF16),
                   jax.ShapeDtypeStruct((nb, NH, HD), F32), jax.ShapeDtypeStruct((nb, NH, HD, HD), F32),
                   jax.ShapeDtypeStruct((nb, NH, HD, LANES), F32)],
        compiler_params=_cparams(),
    )(z, z, dy, gv, ws, bst, *after)


def _pool_count(t0, window):
    pos = (t0 + lax.broadcasted_iota(jnp.int32, (ROW_TILE, LANES), 0)).astype(F32)
    return jnp.minimum(pos + 1.0, float(window))


def _window_sum(win, levels, back):
    n = win.shape[0]
    for lv in range(levels):
        step = 1 << lv
        win = win + pltpu.roll(win, n - step if back else step, 0)
    return win


def _pool_pooled(ppad_ref, t0, g):
    win = ppad_ref[pl.ds(t0, ROW_TILE + POOL_HALO), :]
    wsum = _window_sum(win, g + 1, False)[POOL_HALO:]
    return wsum / _pool_count(t0, POOL_WINDOWS[g]) - win[POOL_HALO:]


def _pool_fwd(z, wp, sp, y):
    s = z.shape[0]
    nt = s // ROW_TILE

    def body(p_ref, wp_ref, sp_ref, _, y_ref, ppad):
        for g in range(NG):
            cols = slice(g * LANES, (g + 1) * LANES)
            ppad[pl.ds(0, POOL_HALO), :] = jnp.zeros((POOL_HALO, LANES), F32)
            ppad[pl.ds(POOL_HALO, s), :] = p_ref[:, cols]
            wpb = wp_ref[g].astype(BF16)
            scale = sp_ref[:, cols]

            def tile(t, carry):
                t0 = pl.multiple_of(t * ROW_TILE, ROW_TILE)
                pooled = _pool_pooled(ppad, t0, g)
                y_ref[pl.ds(t0, ROW_TILE), cols] = (_dot(pooled.astype(BF16), wpb, NN) * scale).astype(BF16)
                return carry

            lax.fori_loop(0, nt, tile, 0)

    return pl.pallas_call(
        body, name="pool_fwd", grid=(1,),
        in_specs=[pl.BlockSpec((s, PW), lambda i: (0, 2 * GW // PW)),
                  pl.BlockSpec((NG, LANES, LANES), lambda i: (0, 0, 0)), pl.BlockSpec((1, PW), lambda i: (0, 0)), ANY],
        out_specs=pl.BlockSpec((s, PW), lambda i: (0, GW // PW)),
        out_shape=jax.ShapeDtypeStruct((s, D), BF16), input_output_aliases={3: 0},
        scratch_shapes=[pltpu.VMEM((s + POOL_HALO, LANES), F32)], compiler_params=_cparams(),
    )(z, wp, sp, y)


def _pool_bwd(z, dy, wp, sp, dz):
    s = z.shape[0]
    nt = s // ROW_TILE

    def body(p_ref, dy_ref, wp_ref, sp_ref, _, dp_ref, dwp_ref, dsp_ref, ppad, rpad, dpool):
        for g in range(NG):
            cols = slice(g * LANES, (g + 1) * LANES)
            ppad[pl.ds(0, POOL_HALO), :] = jnp.zeros((POOL_HALO, LANES), F32)
            ppad[pl.ds(POOL_HALO, s), :] = p_ref[:, cols]
            rpad[pl.ds(s, POOL_HALO), :] = jnp.zeros((POOL_HALO, LANES), F32)
            wpb = wp_ref[g].astype(BF16)
            scale = sp_ref[:, cols]

            def tile(t, carry):
                dwp, dsp = carry
                t0 = pl.multiple_of(t * ROW_TILE, ROW_TILE)
                pooled = _pool_pooled(ppad, t0, g)
                pb = pooled.astype(BF16)
                dyt = dy_ref[pl.ds(t0, ROW_TILE), cols]
                dsp = dsp + jnp.sum(dyt * _dot(pb, wpb, NN), axis=0, keepdims=True)
                dmm = (dyt * scale).astype(BF16)
                dwp = dwp + _dot(pb, dmm, TN)
                dpooled = _dot(dmm, wpb, NT)
                rpad[pl.ds(t0, ROW_TILE), :] = dpooled / _pool_count(t0, POOL_WINDOWS[g])
                dpool[pl.ds(t0, ROW_TILE), :] = dpooled
                return dwp, dsp

            dwp, dsp = lax.fori_loop(0, nt, tile, (jnp.zeros((LANES, LANES), F32), jnp.zeros((1, LANES), F32)))
            dwp_ref[g] = dwp
            dsp_ref[:, cols] = dsp

            def tile2(t, carry):
                t0 = pl.multiple_of(t * ROW_TILE, ROW_TILE)
                win = rpad[pl.ds(t0, ROW_TILE + POOL_HALO), :]
                back = _window_sum(win, g + 1, True)[:ROW_TILE]
                rows = pl.ds(t0, ROW_TILE)
                dp_ref[rows, cols] = (back - dpool[rows, :]).astype(BF16)
                return carry

            lax.fori_loop(0, nt, tile2, 0)

    return pl.pallas_call(
        body, name="pool_bwd", grid=(1,),
        in_specs=[pl.BlockSpec((s, PW), lambda i: (0, 2 * GW // PW)), pl.BlockSpec((s, PW), lambda i: (0, GW // PW)),
                  pl.BlockSpec((NG, LANES, LANES), lambda i: (0, 0, 0)), pl.BlockSpec((1, PW), lambda i: (0, 0)), ANY],
        out_specs=[pl.BlockSpec((s, PW), lambda i: (0, 2 * GW // PW)),
                   pl.BlockSpec((NG, LANES, LANES), lambda i: (0, 0, 0)), pl.BlockSpec((1, PW), lambda i: (0, 0))],
        out_shape=[jax.ShapeDtypeStruct((s, IN_COLS), BF16), jax.ShapeDtypeStruct((NG, LANES, LANES), F32),
                   jax.ShapeDtypeStruct((1, PW), F32)],
        input_output_aliases={4: 0},
        scratch_shapes=[pltpu.VMEM((s + POOL_HALO, LANES), F32), pltpu.VMEM((s + POOL_HALO, LANES), F32),
                        pltpu.VMEM((s, LANES), F32)],
        compiler_params=_cparams(),
    )(z, dy, wp, sp, dz)


CONV_LEAD = CONV_HALO - (CONV_K - 1)


SUBLANES = 8


def _sublane_shifts(win):
    n = win.shape[0]
    return [win] + [pltpu.roll(win, n - b, 0) for b in range(1, SUBLANES)]


def _shifted(shifts, offset):
    a, b = divmod(offset, SUBLANES)
    return shifts[b][a * SUBLANES:a * SUBLANES + ROW_TILE]


def _conv_taps(shifts, wdw_ref, lead, reverse):
    acc = jnp.zeros((ROW_TILE, CW), F32)
    for j in range(CONV_K):
        tap = (CONV_K - 1 - j) if reverse else j
        acc = acc + wdw_ref[tap:tap + 1, :] * _shifted(shifts, lead + j)
    return acc


def _conv_fill_glu(cv_ref, cg_ref, xpad, s):
    xpad[pl.ds(0, CONV_HALO), :] = jnp.zeros((CONV_HALO, CW), F32)

    def fill(t, carry):
        t0 = pl.multiple_of(t * ROW_TILE, ROW_TILE)
        rows = pl.ds(t0, ROW_TILE)
        xpad[pl.ds(t0 + CONV_HALO, ROW_TILE), :] = _glu(cv_ref[rows, :], cg_ref[rows, :])
        return carry

    lax.fori_loop(0, s // ROW_TILE, fill, 0)


def _conv_fwd(z, wdw, bdw, lng, lnb, y):
    s = z.shape[0]

    def body(cv_ref, cg_ref, wdw_ref, bdw_ref, lng_ref, lnb_ref, _, y_ref, xpad):
        _conv_fill_glu(cv_ref, cg_ref, xpad, s)

        def tile(t, carry):
            t0 = pl.multiple_of(t * ROW_TILE, ROW_TILE)
            shifts = _sublane_shifts(xpad[pl.ds(t0, ROW_TILE + CONV_HALO), :])
            hc = _conv_taps(shifts, wdw_ref, CONV_LEAD, False) + bdw_ref[...]
            y_ref[pl.ds(t0, ROW_TILE), :] = _ln_silu(hc, lng_ref[...], lnb_ref[...]).astype(BF16)
            return carry

        lax.fori_loop(0, s // ROW_TILE, tile, 0)

    vec = pl.BlockSpec((1, CW), lambda i: (0, 0))
    return pl.pallas_call(
        body, name="conv_fwd", grid=(1,),
        in_specs=[pl.BlockSpec((s, CW), lambda i: (0, (2 * GW + PW) // CW)),
                  pl.BlockSpec((s, CW), lambda i: (0, (2 * GW + PW) // CW + 1)),
                  pl.BlockSpec((CONV_K + 1, CW), lambda i: (0, 0)), vec, vec, vec, ANY],
        out_specs=pl.BlockSpec((s, CW), lambda i: (0, (GW + PW) // CW)),
        out_shape=jax.ShapeDtypeStruct((s, D), BF16), input_output_aliases={6: 0},
        scratch_shapes=[pltpu.VMEM((s + CONV_HALO, CW), F32)], compiler_params=_cparams(),
    )(z, z, wdw, bdw, lng, lnb, y)


def _conv_bwd(z, dy, wdw, bdw, lng, lnb, dz):
    s = z.shape[0]

    def body(cv_ref, cg_ref, dy_ref, wdw_ref, bdw_ref, lng_ref, lnb_ref, _,
             dz_ref, dwdw_ref, dbdw_ref, dlng_ref, dlnb_ref, xpad, dpad, dcg_keep):
        @pl.when(pl.program_id(0) == 0)
        def _():
            compute(cv_ref, cg_ref, dy_ref, wdw_ref, bdw_ref, lng_ref, lnb_ref,
                    dz_ref, dcg_keep, dwdw_ref, dbdw_ref, dlng_ref, dlnb_ref, xpad, dpad)

        @pl.when(pl.program_id(0) == 1)
        def _():
            dz_ref[...] = dcg_keep[...]

    def compute(cv_ref, cg_ref, dy_ref, wdw_ref, bdw_ref, lng_ref, lnb_ref,
                dcv_ref, dcg_ref, dwdw_ref, dbdw_ref, dlng_ref, dlnb_ref, xpad, dpad):
        _conv_fill_glu(cv_ref, cg_ref, xpad, s)
        dpad[pl.ds(s, CONV_HALO), :] = jnp.zeros((CONV_HALO, CW), F32)
        dwdw_ref[...] = jnp.zeros((CONV_K + 1, CW), F32)

        def tile(t, carry):
            db, dg, dbeta = carry
            t0 = pl.multiple_of(t * ROW_TILE, ROW_TILE)
            shifts = _sublane_shifts(xpad[pl.ds(t0, ROW_TILE + CONV_HALO), :])
            hc = _conv_taps(shifts, wdw_ref, CONV_LEAD, False) + bdw_ref[...]
            _, vjp = jax.vjp(_ln_silu, hc, lng_ref[...], lnb_ref[...])
            dhc, dg_t, dbeta_t = vjp(dy_ref[pl.ds(t0, ROW_TILE), :])
            dpad[pl.ds(t0, ROW_TILE), :] = dhc
            for j in range(CONV_K):
                dwdw_ref[j:j + 1, :] += jnp.sum(dhc * _shifted(shifts, CONV_LEAD + j), axis=0, keepdims=True)
            return db + jnp.sum(dhc, axis=0, keepdims=True), dg + dg_t, dbeta + dbeta_t

        zero = jnp.zeros((1, CW), F32)
        db, dg, dbeta = lax.fori_loop(0, s // ROW_TILE, tile, (zero, zero, zero))
        dbdw_ref[...] = db
        dlng_ref[...] = dg
        dlnb_ref[...] = dbeta

        def tile2(t, carry):
            t0 = pl.multiple_of(t * ROW_TILE, ROW_TILE)
            rows = pl.ds(t0, ROW_TILE)
            dglu = _conv_taps(_sublane_shifts(dpad[pl.ds(t0, ROW_TILE + CONV_HALO), :]), wdw_ref, 0, True)
            _, vjp = jax.vjp(_glu, cv_ref[rows, :], cg_ref[rows, :])
            dcv, dcg = vjp(dglu)
            dcv_ref[rows, :] = dcv.astype(BF16)
            dcg_ref[rows, :] = dcg.astype(BF16)
            return carry

        lax.fori_loop(0, s // ROW_TILE, tile2, 0)

    vec = pl.BlockSpec((1, CW), lambda i: (0, 0))
    wspec = pl.BlockSpec((CONV_K + 1, CW), lambda i: (0, 0))
    vshape = jax.ShapeDtypeStruct((1, CW), F32)
    return pl.pallas_call(
        body, name="conv_bwd", grid=(2,),
        in_specs=[pl.BlockSpec((s, CW), lambda i: (0, (2 * GW + PW) // CW)),
                  pl.BlockSpec((s, CW), lambda i: (0, (2 * GW + PW) // CW + 1)),
                  pl.BlockSpec((s, CW), lambda i: (0, (GW + PW) // CW)), wspec, vec, vec, vec, ANY],
        out_specs=[pl.BlockSpec((s, CW), lambda i: (0, (2 * GW + PW) // CW + i)), wspec, vec, vec, vec],
        out_shape=[jax.ShapeDtypeStruct((s, IN_COLS), BF16), jax.ShapeDtypeStruct((CONV_K + 1, CW), F32),
                   vshape, vshape, vshape],
        input_output_aliases={7: 0},
        scratch_shapes=[pltpu.VMEM((s + CONV_HALO, CW), F32), pltpu.VMEM((s + CONV_HALO, CW), F32),
                        pltpu.VMEM((s, CW), BF16)],
        compiler_params=_cparams(),
    )(z, z, dy, wdw, bdw, lng, lnb, dz)


def _softmax_rows(sc):
    e = jnp.exp(sc - jnp.max(sc, axis=-1, keepdims=True))
    return e / jnp.sum(e, axis=-1, keepdims=True)


def _attn_fwd(q, k, v, tq):
    s, m = q.shape[0], k.shape[0]
    tq = min(tq, s)

    def body(q_ref, k_ref, v_ref, o_ref):
        for h in range(XH):
            cols = slice(h * XHD, (h + 1) * XHD)
            p = _softmax_rows(_dot(q_ref[:, cols], k_ref[:, cols], NT) * ATT_SCALE)
            o_ref[:, cols] = _dot(p.astype(BF16), v_ref[:, cols], NN).astype(BF16)

    kv = pl.BlockSpec((m, D), lambda i: (0, 0))
    return pl.pallas_call(
        body, name="attn_fwd", grid=(s // tq,),
        in_specs=[pl.BlockSpec((tq, D), lambda i: (i, 0)), kv, kv],
        out_specs=pl.BlockSpec((tq, D), lambda i: (i, 0)),
        out_shape=jax.ShapeDtypeStruct((s, D), BF16), compiler_params=_cparams(),
    )(q, k, v)


def _attn_bwd(q, k, v, do, tq, after=()):
    s, m = q.shape[0], k.shape[0]
    tq = min(tq, s)

    def body(q_ref, k_ref, v_ref, do_ref, *rest):
        dq_ref, dk_ref, dv_ref = rest[len(after):]

        @pl.when(pl.program_id(0) == 0)
        def _():
            dk_ref[...] = jnp.zeros_like(dk_ref)
            dv_ref[...] = jnp.zeros_like(dv_ref)

        for h in range(XH):
            cols = slice(h * XHD, (h + 1) * XHD)
            qh, kh, vh, doh = q_ref[:, cols], k_ref[:, cols], v_ref[:, cols], do_ref[:, cols]
            p = _softmax_rows(_dot(qh, kh, NT) * ATT_SCALE)
            dp = _dot(doh, vh, NT)
            dv_ref[:, cols] += _dot(p.astype(BF16), doh, TN)
            ds = (p * (dp - jnp.sum(p * dp, axis=-1, keepdims=True)) * ATT_SCALE).astype(BF16)
            dq_ref[:, cols] = _dot(ds, kh, NN).astype(BF16)
            dk_ref[:, cols] += _dot(ds, qh, TN)

    kv = pl.BlockSpec((m, D), lambda i: (0, 0))
    qs = pl.BlockSpec((tq, D), lambda i: (i, 0))
    return pl.pallas_call(
        body, name="attn_bwd", grid=(s // tq,),
        in_specs=[qs, kv, kv, qs] + [ANY] * len(after), out_specs=[qs, kv, kv],
        out_shape=[jax.ShapeDtypeStruct((s, D), BF16), jax.ShapeDtypeStruct((m, D), F32),
                   jax.ShapeDtypeStruct((m, D), F32)],
        compiler_params=_cparams(),
    )(q, k, v, do, *after)


def _loss_head(y, target, tm):
    s = y.shape[0]
    tm = min(tm, s)

    def body(y_ref, t_ref, dy_ref, part_ref):
        err = y_ref[...] - t_ref[...]
        dy_ref[...] = err * (1.0 / D)
        part_ref[...] = jnp.full((1, 8, LANES), 0.5 * jnp.sum(err * err) * (1.0 / D), F32)

    blk = pl.BlockSpec((tm, D), lambda i: (i, 0))
    return pl.pallas_call(
        body, name="loss_head", grid=(s // tm,), in_specs=[blk, blk],
        out_specs=[blk, pl.BlockSpec((1, 8, LANES), lambda i: (i, 0, 0))],
        out_shape=[jax.ShapeDtypeStruct((s, D), F32), jax.ShapeDtypeStruct((s // tm, 8, LANES), F32)],
        compiler_params=_cparams(),
    )(y, target)


def _layer_fwd(x0, mem, w, p, fetch):
    z, hn0 = _rowop_mm("mix_in", "rms", (x0,), p["norm_mix_pre"], w["w_in"], NT, F32)
    y = _gmlp_fwd(z, p["gmlp_v_gain"], p["w_spatial"], p["b_spatial_t"], 512)
    y = _pool_fwd(z, p["w_pool"], p["s_pool"], y)
    y = _conv_fwd(z, p["w_dw"], p["b_dw"], p["conv_ln_g"], p["conv_ln_b"], y)
    w.update(fetch("out", (y,)))
    x1, h0 = _mm_rowop("mix_out", "rms_res", [(y, w["w_out"], NN)], (x0,), p["norm_mix_post"])
    w.update(fetch("att", (x1,)))
    q, hn1 = _rowop_mm("att_q", "rms", (x1,), p["norm_xattn_pre"], w["w_q"], NN, BF16)
    k, mn = _rowop_mm("att_k", "rms", (mem,), p["norm_mem"], w["w_k"], NN, BF16, after=(x1,))
    v, _ = _rowop_mm("att_v", "rms", (mem,), p["norm_mem"], w["w_v"], NN, BF16, after=(x1,))
    o = _attn_fwd(q, k, v, 512)
    x2, h1 = _mm_rowop("att_o", "rms_res", [(o, w["w_o"], NN)], (x1,), p["norm_xattn_post"])
    w.update(fetch("up", (x2,)))
    u, hn2 = _rowop_mm("ffn_up", "rms", (x2,), p["norm_ffn_pre"], w["w_up"], NT, F32)
    w.update(fetch("down", (u,)))
    x3, h2 = _mm_rowop("ffn_down", "rms_res", [(u, w["w_down"], NN)], (x2,), p["norm_ffn_post"], relu2=True)
    saved = dict(x0=x0, z=z, hn0=hn0, y=y, h0=h0, x1=x1, q=q, hn1=hn1, k=k, v=v, mn=mn, o=o, h1=h1, x2=x2, u=u,
                 hn2=hn2, h2=h2)
    return x3, saved


def _layer_bwd(dx3, mem, w, p, sv, red):
    gs = {}
    du, dh2, dg = _rowop_mm("ffn_down_bwd", "rms_bwd", (sv["h2"], dx3), p["norm_ffn_post"], w["w_down"], NT, BF16,
                            u=sv["u"], after=red.after())
    gs["norm_ffn_post"] = jnp.sum(dg, axis=0)
    g_down = _mm_tn("ffn_down_dw", sv["u"], dh2, relu2=True)
    red.advance((g_down,))
    dx2, dg = _mm_rowop("ffn_up_bwd", "rms_bwd_res", [(du, w["w_up"], NN)], (sv["x2"], dx3), p["norm_ffn_pre"],
                        after=red.after())
    gs["norm_ffn_pre"] = jnp.sum(dg, axis=0)
    g_up = _mm_tn("ffn_up_dw", du, sv["hn2"])
    red.add("ffn", ("w_down", "w_up"), [g_down, g_up])
    do, dh1, dg = _rowop_mm("att_o_bwd", "rms_bwd", (sv["h1"], dx2), p["norm_xattn_post"], w["w_o"], NT, BF16,
                            after=red.after())
    gs["norm_xattn_post"] = jnp.sum(dg, axis=0)
    g_o = _mm_tn("att_o_dw", sv["o"], dh1)
    red.advance((g_o,))
    dq, dk, dv = _attn_bwd(sv["q"], sv["k"], sv["v"], do, 512, after=red.after())
    dk, dv = dk.astype(BF16), dv.astype(BF16)
    dx1, dg = _mm_rowop("att_q_bwd", "rms_bwd_res", [(dq, w["w_q"], NT)], (sv["x1"], dx2), p["norm_xattn_pre"],
                        after=red.after())
    gs["norm_xattn_pre"] = jnp.sum(dg, axis=0)
    g_q = _mm_tn("att_q_dw", sv["hn1"], dq)
    g_k = _mm_tn("att_k_dw", sv["mn"], dk)
    g_v = _mm_tn("att_v_dw", sv["mn"], dv)
    (dg,) = _mm_rowop("att_kv_bwd", "rms_bwd_gain", [(dk, w["w_k"], NT), (dv, w["w_v"], NT)], (mem,), p["norm_mem"])
    gs["norm_mem"] = jnp.sum(dg, axis=0)
    red.add("att", ("w_o", "w_q", "w_k", "w_v"), [g_o, g_q, g_k, g_v])
    dy, dh0, dg = _rowop_mm("mix_out_bwd", "rms_bwd", (sv["h0"], dx1), p["norm_mix_post"], w["w_out"], NT, F32,
                            after=red.after())
    gs["norm_mix_post"] = jnp.sum(dg, axis=0)
    g_out = _mm_tn("mix_out_dw", sv["y"], dh0)
    red.advance((g_out,))
    red.add("out", ("w_out",), [g_out])
    z = sv["z"]
    dz, dgv, dws, dbs = _gmlp_bwd(z, dy, p["gmlp_v_gain"], p["w_spatial"], p["b_spatial_t"], 512, after=red.after())
    gs["gmlp_v_gain"] = jnp.sum(dgv, axis=0)
    gs["w_spatial"] = jnp.sum(dws, axis=0)
    gs["b_spatial"] = jnp.sum(dbs[..., 0], axis=0)
    dz, gs["w_pool"], gs["s_pool"] = _pool_bwd(z, dy, p["w_pool"], p["s_pool"], dz)
    dz, dwdw, gs["b_dw"], gs["conv_ln_g"], gs["conv_ln_b"] = _conv_bwd(
        z, dy, p["w_dw"], p["b_dw"], p["conv_ln_g"], p["conv_ln_b"], dz)
    red.advance((dz,))
    g_in = _mm_tn("mix_in_dw", dz, sv["hn0"], after=red.after())
    red.add("in", ("w_in",), [g_in])
    if red.layer == 0:
        red.advance(())
    red.small("mixer", _small_grad_arrays(gs, dwdw, norms=False))
    dx0, dg = _mm_rowop("mix_in_bwd", "rms_bwd_res", [(dz, w["w_in"], NN)], (sv["x0"], dx1), p["norm_mix_pre"],
                        after=red.after())
    gs["norm_mix_pre"] = jnp.sum(dg, axis=0)
    late = {"norms": jnp.concatenate([gs[n] for n in NORM_NAMES], axis=0)}
    if red.layer == 0:
        late["loss"] = red.extra[0]
    red.small("norms", late)
    return dx0


NORM_NAMES = ("norm_mix_pre", "norm_mix_post", "norm_xattn_pre", "norm_mem", "norm_xattn_post", "norm_ffn_pre",
              "norm_ffn_post")
VEC_NAMES = ("s_pool", "b_dw", "conv_ln_g", "conv_ln_b")
SMALL_ARRAYS = ("norms", "gain_bias", "w_spatial", "w_pool", "vecs", "w_dw")


def _small_grad_arrays(gs, dwdw, norms=True):
    out = {"norms": jnp.concatenate([gs[n] for n in NORM_NAMES], axis=0)} if norms else {}
    out.update({"gain_bias": jnp.concatenate([gs["gmlp_v_gain"], gs["b_spatial"]], axis=0),
                "w_spatial": gs["w_spatial"], "w_pool": gs["w_pool"],
                "vecs": jnp.concatenate([gs[n] for n in VEC_NAMES], axis=0), "w_dw": dwdw})
    return out


def _split_small_grads(arrays):
    out = {n: arrays["norms"][k] for k, n in enumerate(NORM_NAMES)}
    out.update({n: arrays["vecs"][k] for k, n in enumerate(VEC_NAMES)})
    out.update(gmlp_v_gain=arrays["gain_bias"][:NH], b_spatial=arrays["gain_bias"][NH:], w_spatial=arrays["w_spatial"],
               w_pool=arrays["w_pool"], w_dw=arrays["w_dw"][:CONV_K])
    return out


def _layer_params(small, l):
    p = {n: small[n][l].reshape(1, -1) for n in ("norm_mix_pre", "norm_mix_post", "s_pool", "b_dw", "conv_ln_g",
                                                   "conv_ln_b", "norm_xattn_pre", "norm_mem", "norm_xattn_post",
                                                   "norm_ffn_pre", "norm_ffn_post")}
    p["gmlp_v_gain"] = small["gmlp_v_gain"][l]
    p["w_spatial"] = small["w_spatial"][l]
    p["b_spatial_t"] = small["b_spatial"][l].T
    p["w_pool"] = small["w_pool"][l]
    p["w_dw"] = jnp.pad(small["w_dw"][l], ((0, 1), (0, 0)))
    return p


def _local_step(x, mem, target, fetch, small, red):
    small = dict(small)
    saved, weights, params = [], [], []
    h = x
    marker = ()
    for l in range(DEPTH):
        w = fetch(l, "in", marker)
        if "taps" in w:
            small["w_dw"] = w.pop("taps")
        p = _layer_params(small, l)
        h, sv = _layer_fwd(h, mem, w, p, functools.partial(fetch, l))
        marker = (h,)
        saved.append(sv)
        weights.append(w)
        params.append(p)
    dh, loss = _loss_head(h, target, 512)
    red.extra = (loss,)
    for l in reversed(range(DEPTH)):
        red.layer = l
        dh = _layer_bwd(dh, mem, weights[l], params[l], saved[l], red)
    return loss, dh


HBM = pl.BlockSpec(memory_space=pltpu.HBM)


def _position():
    return lax.axis_index("x"), lax.axis_index("y"), lax.axis_index("c")


SEM = pl.BlockSpec(memory_space=pltpu.SEMAPHORE)
EFFECT = pltpu.SideEffectType.DATAFLOW_SIDE_EFFECTING
TOKEN = jax.ShapeDtypeStruct((8, LANES), F32)
TOKEN_SPEC = pl.BlockSpec(memory_space=pltpu.VMEM)


def _landing(shape, dtype):
    return pltpu.with_memory_space_constraint(lax.empty(shape, dtype), pltpu.HBM)


def _hbm_shapes(arrays):
    return [pltpu.HBM(a.shape, a.dtype) for a in arrays]


def _block(ref, r, dev):
    return ref.at[pl.ds((4 * dev[0] + 2 * dev[1] + dev[2]) * r, r), :]


def _split_call(name, body, thru, sems_in, after, sems_out, token):
    n = len(thru)
    out_shape = [pltpu.SemaphoreType.DMA(s) for s in sems_out] + _hbm_shapes(thru) + ([TOKEN] if token else [])
    out_specs = [SEM] * len(sems_out) + [HBM] * n + ([TOKEN_SPEC] if token else [])
    return pl.pallas_call(
        body, name=name, in_specs=[HBM] * n + [SEM] * len(sems_in) + [ANY] * len(after),
        out_specs=out_specs, out_shape=out_shape,
        input_output_aliases={i: len(sems_out) + i for i in range(n)},
        compiler_params=pltpu.CompilerParams(has_side_effects=EFFECT),
    )(*thru, *sems_in, *after)


def _place_own(name, srcs, dev, out_dtype, tr):
    n = len(srcs)
    r, cols = srcs[0][0].shape[-2:]
    tr = r if r < 16 else _row_tile(r, tr)
    nb = r // tr

    def body(dev_ref, *refs):
        for a in range(n):
            refs[n + a][...] = refs[a][...].astype(out_dtype)

    in_specs = [pl.BlockSpec((tr, cols), lambda i, d: (i, 0)) if l is None
                else pl.BlockSpec((None, tr, cols), lambda i, d, l=l: (l, i, 0)) for _, l in srcs]
    return pl.pallas_call(
        body, name=name,
        grid_spec=pltpu.PrefetchScalarGridSpec(
            num_scalar_prefetch=1, grid=(nb,), in_specs=in_specs,
            out_specs=[pl.BlockSpec((tr, cols), lambda i, d: (d[0] * nb + i, 0))] * n),
        out_shape=[jax.ShapeDtypeStruct((N_DEV * r, cols), out_dtype)] * n, compiler_params=_cparams(),
    )(dev, *[a for a, _ in srcs])


def _gather_peers(x, y, c):
    return [(1 - x, y, c), (x, 1 - y, c), (1 - x, 1 - y, c), (x, y, 1 - c)]


def _block_rows(land):
    return land.shape[0] // N_DEV


def _near_peers(x, y, c):
    return [(1 - x, y, c), (x, 1 - y, c), (x, y, 1 - c)]


def _relay_route(x, y, c):
    origin = (x + c * (1 - 2 * x), y + (1 - c) * (1 - 2 * y), c)
    target = (x + (1 - c) * (1 - 2 * x), y + c * (1 - 2 * y), c)
    return origin, target


def _same_block_copy(blk, send_sem, recv_sem, to):
    return pltpu.make_async_remote_copy(src_ref=blk, dst_ref=blk, send_sem=send_sem, recv_sem=recv_sem, device_id=to,
                                        device_id_type=MESH)


def _gather_start(name, lands, after):
    n = len(lands)

    def body(*refs):
        lz = refs[:n]
        send_sems, recv_sems = refs[n + len(after)], refs[n + len(after) + 1]
        token = refs[-1]
        x, y, c = _position()
        for a in range(n):
            own = _block(lz[a], _block_rows(lands[a]), (x, y, c))
            for k, to in enumerate(_near_peers(x, y, c)):
                _same_block_copy(own, send_sems.at[k], recv_sems.at[k], to).start()
        token[...] = jnp.zeros_like(token)

    out = _split_call(name, body, list(lands), [], after, [(3,), (3,)], True)
    return out[0], out[1], out[2:2 + n], out[-1]


def _gather_step(name, near, far, fresh, after):
    groups = [g for g in (near and near[0], far and far[0], fresh) if g]
    counts = [len(near[0]) if near else 0, len(far[0]) if far else 0, len(fresh) if fresh else 0]
    n = sum(counts)
    sems_in = ([near[1]] if near else []) + ([far[1]] if far else [])
    sems_out = ([(2,), (2,), (1,), (1,)] if near else []) + ([(1,), (1,)] if far else []) + ([(3,), (3,)] if fresh else [])

    def body(*refs):
        lz = list(refs[:n])
        ins = list(refs[n:n + len(sems_in)])
        outs = list(refs[n + len(sems_in) + len(after):n + len(sems_in) + len(after) + len(sems_out)])
        token = refs[-1]
        x, y, c = _position()
        me, sibling = (x, y, c), (x, y, 1 - c)
        near_lz, far_lz, fresh_lz = (lz[sum(counts[:i]):sum(counts[:i + 1])] for i in range(3))
        neighbours = _near_peers(x, y, c)[:2]
        origin, target = _relay_route(x, y, c)
        diagonal = (1 - x, 1 - y, c)
        if near:
            recv0 = ins.pop(0)
            fsend, frecv, rsend, rrecv = (outs.pop(0) for _ in range(4))
            for a, land in enumerate(near[0]):
                for j, chip in enumerate(neighbours):
                    _same_block_copy(_block(near_lz[a], _block_rows(land), chip), fsend.at[j], recv0.at[j], me).wait_recv()
        if far:
            rrecv_in = ins.pop(0)
            f2send, f2recv = outs.pop(0), outs.pop(0)
            for a, land in enumerate(far[0]):
                _same_block_copy(_block(far_lz[a], _block_rows(land), diagonal), f2send.at[0], rrecv_in.at[0], me).wait_recv()
            for a, land in enumerate(far[0]):
                _same_block_copy(_block(far_lz[a], _block_rows(land), diagonal), f2send.at[0], f2recv.at[0], sibling).start()
        if near:
            for a, land in enumerate(near[0]):
                r = _block_rows(land)
                _same_block_copy(_block(near_lz[a], r, origin), rsend.at[0], rrecv.at[0], target).start()
                for j, chip in enumerate(neighbours):
                    _same_block_copy(_block(near_lz[a], r, chip), fsend.at[j], frecv.at[j], sibling).start()
        if fresh:
            send_sems, recv_sems = outs.pop(0), outs.pop(0)
            for a, land in enumerate(fresh):
                own = _block(fresh_lz[a], _block_rows(land), me)
                for k, to in enumerate(_near_peers(x, y, c)):
                    _same_block_copy(own, send_sems.at[k], recv_sems.at[k], to).start()
        token[...] = jnp.zeros_like(token)

    out = list(_split_call(name, body, [l for g in groups for l in g], sems_in, after, sems_out, True))
    res = {"token": out.pop()}
    if near:
        res.update(fsend=out.pop(0), frecv=out.pop(0), rsend=out.pop(0), rrecv=out.pop(0))
    if far:
        res.update(f2send=out.pop(0), f2recv=out.pop(0))
    if fresh:
        res.update(send=out.pop(0), recv=out.pop(0))
    res["near"], res["far"], res["fresh"] = (out[sum(counts[:i]):sum(counts[:i + 1])] for i in range(3))
    return res


def _gather_finish(name, lands, send_sems, recv_sems, fsend, frecv, rsend, f2send, f2recv, after):
    n = len(lands)

    def body(*refs):
        lz = refs[:n]
        send0, recv0, fsend_ref, frecv_ref, rsend_ref, f2send_ref, f2recv_ref = refs[n:n + 7]
        x, y, c = _position()
        me = (x, y, c)
        near = _near_peers(x, y, c)[:2]
        origin, _ = _relay_route(x, y, c)
        for a in range(n):
            r = _block_rows(lands[a])
            sib = _block(lz[a], r, (x, y, 1 - c))
            _same_block_copy(sib, send0.at[2], recv0.at[2], me).wait_recv()
            for j, chip in enumerate(near):
                blk = _block(lz[a], r, (chip[0], chip[1], 1 - c))
                _same_block_copy(blk, fsend_ref.at[j], frecv_ref.at[j], me).wait_recv()
            far = _block(lz[a], r, (1 - x, 1 - y, 1 - c))
            _same_block_copy(far, f2send_ref.at[0], f2recv_ref.at[0], me).wait_recv()
            own = _block(lz[a], r, me)
            for k in range(3):
                _same_block_copy(own, send0.at[k], recv0.at[k], me).wait_send()
            for j, chip in enumerate(near):
                _same_block_copy(_block(lz[a], r, chip), fsend_ref.at[j], frecv_ref.at[j], me).wait_send()
            _same_block_copy(_block(lz[a], r, origin), rsend_ref.at[0], recv0.at[0], me).wait_send()
            _same_block_copy(_block(lz[a], r, (1 - x, 1 - y, c)), f2send_ref.at[0], f2recv_ref.at[0], me).wait_send()

    return _split_call(name, body, list(lands), [send_sems, recv_sems, fsend, frecv, rsend, f2send, f2recv], after, [],
                       False)


def _sibling_start(name, grads, after):
    n = len(grads)
    lands = [_landing((4, g.shape[0] // N_DEV, D), g.dtype) for g in grads]

    def body(*refs):
        ins, lz = refs[:n], refs[n:2 * n]
        send_sem, recv_sem = refs[2 * n + len(after)], refs[2 * n + len(after) + 1]
        token = refs[-1]
        x, y, c = _position()
        for a in range(n):
            r = grads[a].shape[0] // N_DEV
            for q in range(4):
                pltpu.make_async_remote_copy(
                    src_ref=ins[a].at[pl.ds((2 * q + 1 - c) * r, r), :], dst_ref=lz[a].at[q], send_sem=send_sem.at[0],
                    recv_sem=recv_sem.at[0], device_id=(x, y, 1 - c), device_id_type=MESH).start()
        token[...] = jnp.zeros_like(token)

    out = _split_call(name, body, list(grads) + lands, [], after, [(1,), (1,)], True)
    return out[0], out[1], out[2:2 + n], out[2 + n:2 + 2 * n], out[-1]


def _sibling_finish(name, grads, lands, send_sem, recv_sem, after):
    n = len(grads)

    def body(*refs):
        ins, lz = refs[:n], refs[n:2 * n]
        send_ref, recv_ref = refs[2 * n], refs[2 * n + 1]
        x, y, c = _position()
        for a in range(n):
            r = grads[a].shape[0] // N_DEV
            for q in range(4):
                cp = pltpu.make_async_remote_copy(
                    src_ref=ins[a].at[pl.ds((2 * q + 1 - c) * r, r), :], dst_ref=lz[a].at[q], send_sem=send_ref.at[0],
                    recv_sem=recv_ref.at[0], device_id=(x, y, c), device_id_type=MESH)
                cp.wait_send()
                cp.wait_recv()

    out = _split_call(name, body, list(grads) + list(lands), [send_sem, recv_sem], after, [], False)
    return out[:n], out[n:2 * n]


def _chip_start(name, parts, after):
    n = len(parts)
    lands = [_landing((3,) + p.shape[1:], p.dtype) for p in parts]

    def body(*refs):
        ins, lz = refs[:n], refs[n:2 * n]
        send_sems, recv_sems = refs[2 * n + len(after)], refs[2 * n + len(after) + 1]
        token = refs[-1]
        x, y, c = _position()
        for a in range(n):
            for j, chip in enumerate(_gather_peers(x, y, c)[:3]):
                pltpu.make_async_remote_copy(
                    src_ref=ins[a].at[2 * chip[0] + chip[1]], dst_ref=lz[a].at[j], send_sem=send_sems.at[j],
                    recv_sem=recv_sems.at[j], device_id=chip, device_id_type=MESH).start()
        token[...] = jnp.zeros_like(token)

    out = _split_call(name, body, list(parts) + lands, [], after, [(3,), (3,)], True)
    return out[0], out[1], out[2:2 + n], out[2 + n:2 + 2 * n], out[-1]


def _chip_finish(name, parts, lands, send_sems, recv_sems, after):
    n = len(parts)

    def body(*refs):
        ins, lz = refs[:n], refs[n:2 * n]
        send_ref, recv_ref = refs[2 * n], refs[2 * n + 1]
        me = _position()
        for a in range(n):
            for j in range(3):
                cp = pltpu.make_async_remote_copy(
                    src_ref=ins[a].at[j], dst_ref=lz[a].at[j], send_sem=send_ref.at[j], recv_sem=recv_ref.at[j],
                    device_id=me, device_id_type=MESH)
                cp.wait_send()
                cp.wait_recv()

    out = _split_call(name, body, list(parts) + list(lands), [send_sems, recv_sems], after, [], False)
    return out[:n], out[n:2 * n]


def _other_devices(x, y, c):
    return [(x + (k >> 2 & 1) * (1 - 2 * x), y + (k >> 1 & 1) * (1 - 2 * y), c + (k & 1) * (1 - 2 * c))
            for k in range(1, N_DEV)]


def _broadcast_start(name, arrays, after):
    n = len(arrays)
    lands = [_landing((N_DEV,) + a.shape, a.dtype) for a in arrays]

    def body(*refs):
        ins, lz = refs[:n], refs[n:2 * n]
        send_sems, recv_sems = refs[2 * n + len(after)], refs[2 * n + len(after) + 1]
        token = refs[-1]
        x, y, c = _position()
        for a in range(n):
            for k, peer in enumerate(_other_devices(x, y, c)):
                pltpu.make_async_remote_copy(
                    src_ref=ins[a], dst_ref=lz[a].at[4 * x + 2 * y + c], send_sem=send_sems.at[k],
                    recv_sem=recv_sems.at[k], device_id=peer, device_id_type=MESH).start()
        token[...] = jnp.zeros_like(token)

    out = _split_call(name, body, list(arrays) + lands, [], after, [(N_DEV - 1,), (N_DEV - 1,)], True)
    return out[0], out[1], out[2:2 + n], out[2 + n:2 + 2 * n], out[-1]


def _broadcast_finish(name, arrays, lands, send_sems, recv_sems, after):
    n = len(arrays)

    def body(*refs):
        ins, lz = refs[:n], refs[n:2 * n]
        send_ref, recv_ref = refs[2 * n], refs[2 * n + 1]
        x, y, c = _position()
        for a in range(n):
            for k, peer in enumerate(_other_devices(x, y, c)):
                cp = pltpu.make_async_remote_copy(
                    src_ref=ins[a], dst_ref=lz[a].at[4 * peer[0] + 2 * peer[1] + peer[2]], send_sem=send_ref.at[k],
                    recv_sem=recv_ref.at[k], device_id=(x, y, c), device_id_type=MESH)
                cp.wait_send()
                cp.wait_recv()

    out = _split_call(name, body, list(arrays) + list(lands), [send_sems, recv_sems], after, [], False)
    return out[:n], out[n:2 * n]


def _row_tile(r, target):
    return max(t for t in range(16, min(r, target) + 1, 16) if r % t == 0)


def _chip_partial(name, grad, got, c, tr):
    r = grad.shape[0] // N_DEV
    tr = _row_tile(r, tr)
    g4 = grad.reshape(4, 2, r, D)

    def body(c_ref, g_ref, s_ref, o_ref):
        o_ref[...] = (g_ref[...].astype(F32) + s_ref[...].astype(F32)).astype(BF16)

    return pl.pallas_call(
        body, name=name,
        grid_spec=pltpu.PrefetchScalarGridSpec(
            num_scalar_prefetch=1, grid=(4, r // tr),
            in_specs=[pl.BlockSpec((None, None, tr, D), lambda q, i, c_ref: (q, c_ref[0], i, 0)),
                      pl.BlockSpec((None, tr, D), lambda q, i, c_ref: (q, i, 0))],
            out_specs=pl.BlockSpec((None, tr, D), lambda q, i, c_ref: (q, i, 0))),
        out_shape=jax.ShapeDtypeStruct((4, r, D), BF16), compiler_params=_cparams(),
    )(c, g4, got)


class _WeightGather:
    def __init__(self, groups):
        self.groups = list(groups)
        self.index = {key: i for i, (key, _, _) in enumerate(groups)}
        self.state = [None] * len(groups)
        self.token = ()
        for i in range(min(2, len(groups))):
            self._start(i)

    def _tag(self, i):
        return "%s_%d" % self.groups[i][0][::-1]

    def _start(self, i):
        send, recv, lz, tok = _gather_start("gather_start_" + self._tag(i), self.groups[i][2], self.token)
        self.state[i] = dict(send=send, recv=recv, lands=lz)
        self.token = (tok,)

    def _step(self, name, near, far, fresh, marker):
        exists = lambda i: i is not None and i < len(self.groups)
        near, far, fresh = (i if exists(i) else None for i in (near, far, fresh))
        res = _gather_step(
            name, None if near is None else (self.state[near]["lands"], self.state[near]["recv"]),
            None if far is None else (self.state[far]["lands"], self.state[far]["rrecv"]),
            None if fresh is None else self.groups[fresh][2], tuple(marker) + self.token)
        self.token = (res["token"],)
        if near is not None:
            self.state[near].update(lands=res["near"], fsend=res["fsend"], frecv=res["frecv"], rsend=res["rsend"],
                                    rrecv=res["rrecv"])
        if far is not None:
            self.state[far].update(lands=res["far"], f2send=res["f2send"], f2recv=res["f2recv"])
        if fresh is not None:
            self.state[fresh] = dict(send=res["send"], recv=res["recv"], lands=res["fresh"])

    def fetch(self, layer, group, marker):
        k = self.index[(layer, group)]
        if k == 0:
            self._step("gather_step_first", 0, None, None, marker)
        self._step("gather_step_" + self._tag(k), k + 1, k, k + 2, marker)
        st = self.state[k]
        lz = _gather_finish("gather_finish_" + self._tag(k), st["lands"], st["send"], st["recv"], st["fsend"],
                            st["frecv"], st["rsend"], st["f2send"], st["f2recv"], self.token)
        self.state[k] = None
        return dict(zip(self.groups[k][1], lz))


class _GradReduce:
    def __init__(self, core, chip):
        self.core, self.chip = core, chip
        self.layer = None
        self.token = ()
        self.at_sibling, self.at_chips = [], []
        self.extra, self.smalls = (), {}

    def after(self):
        return self.token

    def add(self, group, names, grads):
        tag = "%s_%d" % (group, self.layer)
        send, recv, grads, lands, tok = _sibling_start("grad_sibling_start_" + tag, grads, self.token)
        self.at_sibling.append((tag, [(self.layer, n) for n in names], send, recv, grads, lands))
        self.token = (tok,)

    def advance(self, marker):
        for tag, keys, send, recv, grads, lands in self.at_sibling:
            grads, lands = _sibling_finish("grad_sibling_finish_" + tag, grads, lands, send, recv, marker)
            parts = [_chip_partial("chip_partial_%d_%s" % key, g, got, self.core, 512)
                     for key, g, got in zip(keys, grads, lands)]
            send, recv, parts, lands, tok = _chip_start("grad_chip_start_" + tag, parts, ())
            self.at_chips.append([tag, keys, send, recv, parts, lands])
            self.token = (tok,)
        self.at_sibling = []

    def small(self, part, arrays):
        keys = list(arrays)
        send, recv, own, slots, tok = _broadcast_start(
            "small_grads_start_%d_%s" % (self.layer, part), [arrays[k] for k in keys], self.token)
        self.smalls.setdefault(self.layer, []).append((part, keys, send, recv, own, slots))
        self.token = (tok,)

    def small_finish(self, layer, marker):
        mine, theirs = {}, {}
        for part, keys, send, recv, own, slots in self.smalls[layer]:
            own, slots = _broadcast_finish("small_grads_finish_%d_%s" % (layer, part), own, slots, send, recv, marker)
            mine.update(zip(keys, own))
            theirs.update(zip(keys, slots))
        return mine, theirs

    def collect(self, key, marker):
        for entry in self.at_chips:
            tag, keys, send, recv, parts, lands = entry
            if key in keys:
                if send is not None:
                    parts, lands = _chip_finish("grad_chip_finish_" + tag, parts, lands, send, recv, marker)
                    entry[2:] = [None, None, parts, lands]
                i = keys.index(key)
                return parts[i], lands[i]
        raise KeyError(key)


def _adamw_math(w, g, m, v):
    m = ADAM_B1 * m + (1.0 - ADAM_B1) * g
    v = ADAM_B2 * v + (1.0 - ADAM_B2) * jnp.square(g)
    m_hat = m / (1.0 - ADAM_B1 ** ADAM_STEP)
    v_hat = v / (1.0 - ADAM_B2 ** ADAM_STEP)
    delta = -ADAM_LR * (m_hat / (jnp.sqrt(v_hat) + ADAM_EPS) + ADAM_WD * w)
    return delta, m, v


def _adamw_small(wts, mom_m, mom_v, own, gathered, loss_own, loss_gathered, dev):
    names = SMALL
    nw = len(names)
    na = len(SMALL_ARRAYS)

    def body(dev_ref, *refs):
        w_refs, m_refs, v_refs = (dict(zip(names, refs[i * nw:(i + 1) * nw])) for i in range(3))
        own_refs = refs[3 * nw:3 * nw + DEPTH * na]
        g_refs = refs[3 * nw + DEPTH * na:3 * nw + 2 * DEPTH * na]
        loss_own_ref, loss_got_ref = refs[3 * nw + 2 * DEPTH * na:3 * nw + 2 * DEPTH * na + 2]
        outs = refs[3 * nw + 2 * DEPTH * na + 2:]
        g_out, d_out, m_out, v_out = (dict(zip(names, outs[i * nw:(i + 1) * nw])) for i in range(4))
        me = dev_ref[0]

        loss = None
        for d in range(N_DEV):
            for b in range(loss_own.shape[0]):
                term = jnp.where(me == d, loss_own_ref[b], loss_got_ref[d, b])
                loss = term if loss is None else loss + term
        outs[4 * nw][...] = loss

        def update(name, at, g):
            g_out[name][at] = g
            d_out[name][at], m_out[name][at], v_out[name][at] = _adamw_math(
                w_refs[name][at], g, m_refs[name][at], v_refs[name][at])

        for l in range(DEPTH):
            mine = dict(zip(SMALL_ARRAYS, own_refs[l * na:(l + 1) * na]))
            got = dict(zip(SMALL_ARRAYS, g_refs[l * na:(l + 1) * na]))

            def total(key, at):
                acc = None
                for d in range(N_DEV):
                    term = jnp.where(me == d, mine[key][at] if at else mine[key][...], got[key][(d,) + at])
                    acc = term if acc is None else acc + term
                return acc

            row = (slice(l, l + 1),)
            for k, name in enumerate(NORM_NAMES):
                update(name, row, total("norms", (slice(k, k + 1),)))
            for k, name in enumerate(VEC_NAMES):
                update(name, row, total("vecs", (slice(k, k + 1),)))
            update("gmlp_v_gain", (l,), total("gain_bias", (slice(0, NH),)))
            update("b_spatial", (l,), total("gain_bias", (slice(NH, 2 * NH),)))
            update("w_spatial", (l,), total("w_spatial", ()))
            update("w_pool", (l,), total("w_pool", ()))
            update("w_dw", (l,), total("w_dw", (slice(0, CONV_K),)))

    args = [src[n] for src in (wts, mom_m, mom_v) for n in names]
    args += [src[l][k] for src in (own, gathered) for l in range(DEPTH) for k in SMALL_ARRAYS]
    args += [loss_own, loss_gathered]
    outs = pl.pallas_call(
        body, name="adamw_small",
        in_specs=[pl.BlockSpec(memory_space=pltpu.SMEM)] + [pl.BlockSpec(memory_space=pltpu.VMEM)] * len(args),
        out_shape=[jax.ShapeDtypeStruct(wts[n].shape, F32) for _ in range(4) for n in names]
        + [jax.ShapeDtypeStruct((8, LANES), F32)],
        compiler_params=_cparams(),
    )(dev, *args)
    return tuple(dict(zip(names, outs[i * nw:(i + 1) * nw])) for i in range(4)) + (outs[4 * nw],)


def _adamw_layers(name, w, reduced, m, v, chip, tr, transposed=False, after=()):
    nl, r, cdim = w.shape
    tr = _row_tile(r, tr)
    nb = r // tr

    def body(q_ref, w_ref, p0_ref, g0_ref, p1_ref, g1_ref, m_ref, v_ref, *rest):
        g_ref, d_ref, nm_ref, nv_ref = rest[len(after):]

        def total(p_ref, got_ref):
            acc = p_ref[...].astype(F32)
            for j in range(3):
                acc = acc + got_ref[j].astype(F32)
            return acc

        g = jnp.where(pl.program_id(0) == 0, total(p0_ref, g0_ref), total(p1_ref, g1_ref))
        if transposed:
            g = g.T
        g_ref[...] = g
        d_ref[...], nm_ref[...], nv_ref[...] = _adamw_math(w_ref[...], g, m_ref[...], v_ref[...])

    blk = pl.BlockSpec((None, tr, cdim), lambda l, i, q: (l, i, 0))
    first = lambda l, i: i * (1 - l) + (nb - 1) * l
    second = lambda l, i: i * l
    if transposed:
        gshape = (cdim, tr)
        at = lambda lead, i: (lead, 0, i)
    else:
        gshape = (tr, cdim)
        at = lambda lead, i: (lead, i, 0)
    specs = [blk,
             pl.BlockSpec((None,) + gshape, lambda l, i, q: at(q[0], first(l, i))),
             pl.BlockSpec((3,) + gshape, lambda l, i, q: at(0, first(l, i))),
             pl.BlockSpec((None,) + gshape, lambda l, i, q: at(q[0], second(l, i))),
             pl.BlockSpec((3,) + gshape, lambda l, i, q: at(0, second(l, i))), blk, blk] + [ANY] * len(after)
    shape = jax.ShapeDtypeStruct((nl, r, cdim), F32)
    return pl.pallas_call(
        body, name=name,
        grid_spec=pltpu.PrefetchScalarGridSpec(num_scalar_prefetch=1, grid=(nl, nb), in_specs=specs, out_specs=[blk] * 4),
        out_shape=[shape] * 4, compiler_params=_cparams(),
    )(chip, w, *reduced[0], *reduced[1], m, v, *after)


def _to_rows(name, a):
    return jnp.swapaxes(a, 1, 2) if name == "w_in" else a


def _place_own_transposed(name, srcs, dev, out_dtype, tc):
    n = len(srcs)
    kdim, cdim = srcs[0][0].shape[-2:]

    def body(dev_ref, *refs):
        for a in range(n):
            refs[n + a][...] = refs[a][...].T.astype(out_dtype)

    return pl.pallas_call(
        body, name=name,
        grid_spec=pltpu.PrefetchScalarGridSpec(
            num_scalar_prefetch=1, grid=(kdim // tc,),
            in_specs=[pl.BlockSpec((None, tc, cdim), lambda i, d, l=l: (l, i, 0)) for _, l in srcs],
            out_specs=[pl.BlockSpec((cdim, tc), lambda i, d: (d[0], i))] * n),
        out_shape=[jax.ShapeDtypeStruct((N_DEV * cdim, kdim), out_dtype)] * n, compiler_params=_cparams(),
    )(dev, *[a for a, _ in srcs])


def _pack(arrays, rows):
    flat = jnp.concatenate([a.reshape(-1) for a in arrays])
    return jnp.pad(flat, (0, rows * D - flat.shape[0])).reshape(rows, D)


def _rows_for(shapes, mult=8):
    total = 0
    for shp in shapes:
        size = 1
        for dim in shp:
            size *= dim
        total += size
    return -(-total // (mult * D)) * mult


def kernel(x, mem, norm_mix_pre, norm_mix_post, w_in, w_out, gmlp_v_gain, w_spatial, b_spatial, w_pool, s_pool, w_dw, b_dw, conv_ln_g, conv_ln_b, norm_xattn_pre, norm_mem, norm_xattn_post, w_q, w_k, w_v, w_o, norm_ffn_pre, norm_ffn_post, w_up, w_down, loss_target, m_norm_mix_pre, m_norm_mix_post, m_w_in, m_w_out, m_gmlp_v_gain, m_w_spatial, m_b_spatial, m_w_pool, m_s_pool, m_w_dw, m_b_dw, m_conv_ln_g, m_conv_ln_b, m_norm_xattn_pre, m_norm_mem, m_norm_xattn_post, m_w_q, m_w_k, m_w_v, m_w_o, m_norm_ffn_pre, m_norm_ffn_post, m_w_up, m_w_down, v_norm_mix_pre, v_norm_mix_post, v_w_in, v_w_out, v_gmlp_v_gain, v_w_spatial, v_b_spatial, v_w_pool, v_s_pool, v_w_dw, v_b_dw, v_conv_ln_g, v_conv_ln_b, v_norm_xattn_pre, v_norm_mem, v_norm_xattn_post, v_w_q, v_w_k, v_w_v, v_w_o, v_norm_ffn_pre, v_norm_ffn_post, v_w_up, v_w_down):
    args = dict(locals())
    wts = {n: args[n] for n in WEIGHTS}
    mom_m = {n: args["m_" + n] for n in WEIGHTS}
    mom_v = {n: args["v_" + n] for n in WEIGHTS}
    xi, yi, ci = _position()
    me = 4 * xi + 2 * yi + ci

    dev = jnp.reshape(me, (1,)).astype(jnp.int32)
    lands = {}
    for call, names, tr in (("place_in", ("w_in",), 256), ("place_att", ("w_out", "w_q", "w_k", "w_v", "w_o"), 64),
                            ("place_up", ("w_up",), 256), ("place_down", ("w_down",), 256)):
        srcs = [(_to_rows(n, wts[n]), l) for l in range(DEPTH) for n in names]
        placed = (_place_own_transposed if names == ("w_up",) else _place_own)(call, srcs, dev, BF16, tr)
        lands.update(zip([(l, n) for l in range(DEPTH) for n in names], placed))
    (lands[(0, "taps")],) = _place_own("place_taps", [(_pack([w_dw], _rows_for([w_dw.shape])), None)], dev, F32, 8)
    groups = []
    for l in range(DEPTH):
        for group, names in GATHER_GROUPS:
            if (l, group) == (0, "in"):
                names = names + ("taps",)
            groups.append(((l, group), names, [lands[(l, n)] for n in names]))
    gather = _WeightGather(groups)

    def fetch(layer, group, marker):
        w = gather.fetch(layer, group, marker)
        if "taps" in w:
            blocks = w["taps"].reshape(N_DEV, -1)[:, :w_dw.size].reshape((N_DEV,) + w_dw.shape)
            w["taps"] = jnp.moveaxis(blocks, 0, 2).reshape(DEPTH, CONV_K, CW)
        return w

    reduce = _GradReduce(jnp.reshape(ci, (1,)).astype(jnp.int32), jnp.reshape(2 * xi + yi, (1,)).astype(jnp.int32))
    small = {n: wts[n] for n in SMALL if n != "w_dw"}
    _, dx = _local_step(x[0], mem[0], loss_target[0], fetch, small, reduce)
    reduce.advance((dx,))

    grad_w, delta, new_m, new_v = {}, {}, {}, {}
    marker = (dx,) + tuple(reduce.after())
    for n in UPDATE_ORDER:
        reduced = [reduce.collect((l, n), marker) for l in range(DEPTH)]
        outs = _adamw_layers("adamw_" + n, _to_rows(n, wts[n]), reduced, _to_rows(n, mom_m[n]), _to_rows(n, mom_v[n]),
                             reduce.chip, 256, transposed=n == "w_up", after=marker)
        grad_w[n], delta[n], new_m[n], new_v[n] = (_to_rows(n, o) for o in outs)
        marker = (outs[1],)

    own, slots = [None] * DEPTH, [None] * DEPTH
    for l in reversed(range(DEPTH)):
        own[l], slots[l] = reduce.small_finish(l, marker)
        if l == 0:
            loss_own, loss_slots = own[l].pop("loss"), slots[l].pop("loss")
    shard_cols = CW // N_DEV
    for l in range(DEPTH):
        own[l]["w_dw"] = lax.dynamic_slice_in_dim(own[l]["w_dw"], me * shard_cols, shard_cols, axis=1)
        slots[l]["w_dw"] = lax.dynamic_slice_in_dim(slots[l]["w_dw"], me * shard_cols, shard_cols, axis=2)
    *small_out, loss_tile = _adamw_small(wts, mom_m, mom_v, own, slots, loss_own, loss_slots, dev)
    for dst, src in zip((grad_w, delta, new_m, new_v), small_out):
        dst.update(src)

    return (loss_tile[0, 0], dx[None], *[grad_w[n] for n in WEIGHTS], *[delta[n] for n in WEIGHTS],
            *[new_m[n] for n in WEIGHTS], *[new_v[n] for n in WEIGHTS])
```

```python
import functools

import jax
import jax.numpy as jnp
from jax import lax
from jax.experimental import pallas as pl
from jax.experimental.pallas import tpu as pltpu

F32 = jnp.float32
BF16 = jnp.bfloat16

D = 2048
GW = 1024
PW = 512
CW = 512
HD = 128
NH = 8
NG = 4
POOL_WINDOWS = (2, 4, 8, 16)
CONV_K = 31
IN_COLS = 2 * GW + PW + 2 * CW
DFF = 4 * D
XH = 4
XHD = D // XH
ATT_SCALE = XHD ** -0.5
RMS_EPS = 1e-6
LN_EPS = 1e-5
DEPTH = 2
N_DEV = 8

ADAM_LR = 0.001
ADAM_B1 = 0.9
ADAM_B2 = 0.999
ADAM_EPS = 1e-08
ADAM_WD = 0.01
ADAM_STEP = 10

LANES = 128
CONV_HALO = 32
POOL_HALO = 16
ROW_TILE = 128
VMEM_LIMIT = 60 * 1024 * 1024

MESH = pl.DeviceIdType.MESH
NT = (((1,), (1,)), ((), ()))
NN = (((1,), (0,)), ((), ()))
TN = (((0,), (0,)), ((), ()))

BIG = ("w_out", "w_q", "w_k", "w_v", "w_o", "w_up", "w_down", "w_in")
UPDATE_ORDER = ("w_down", "w_up", "w_o", "w_q", "w_k", "w_v", "w_out", "w_in")
GATHER_GROUPS = (("in", ("w_in",)), ("out", ("w_out",)), ("att", ("w_q", "w_k", "w_v", "w_o")), ("up", ("w_up",)),
                 ("down", ("w_down",)))
SMALL = ("norm_mix_pre", "norm_mix_post", "gmlp_v_gain", "w_spatial", "b_spatial", "w_pool", "s_pool",
         "w_dw", "b_dw", "conv_ln_g", "conv_ln_b", "norm_xattn_pre", "norm_mem", "norm_xattn_post",
         "norm_ffn_pre", "norm_ffn_post")
WEIGHTS = ("norm_mix_pre", "norm_mix_post", "w_in", "w_out", "gmlp_v_gain", "w_spatial", "b_spatial", "w_pool",
           "s_pool", "w_dw", "b_dw", "conv_ln_g", "conv_ln_b", "norm_xattn_pre", "norm_mem", "norm_xattn_post",
           "w_q", "w_k", "w_v", "w_o", "norm_ffn_pre", "norm_ffn_post", "w_up", "w_down")


def _cparams():
    return pltpu.CompilerParams(vmem_limit_bytes=VMEM_LIMIT)


def _dot(a, b, dims):
    return lax.dot_general(a, b, dims, preferred_element_type=F32)


def _rms(x, g):
    y = x * lax.rsqrt(jnp.mean(x * x, axis=-1, keepdims=True) + RMS_EPS)
    return y * g


def _rms_bwd(x, g, dy):
    r = lax.rsqrt(jnp.mean(x * x, axis=-1, keepdims=True) + RMS_EPS)
    xh = x * r
    t = dy * g
    dx = r * (t - xh * jnp.mean(t * xh, axis=-1, keepdims=True))
    return dx, jnp.sum(dy * xh, axis=0, keepdims=True)


def _gelu(x):
    cdf = 0.5 * (1.0 + jnp.tanh(0.7978845608028654 * (x + 0.044715 * (x * x * x))))
    return x * cdf


def _layer_norm(x, g, b=None):
    mu = jnp.mean(x, axis=-1, keepdims=True)
    xc = x - mu
    var = jnp.mean(xc * xc, axis=-1, keepdims=True)
    y = xc * lax.rsqrt(var + LN_EPS) * g
    return y if b is None else y + b


def _sigmoid(x):
    return 1.0 / (1.0 + jnp.exp(-x))


def _gmlp_rows(zu, zv, gv):
    return _gelu(zu), _layer_norm(_gelu(zv), gv)


def _glu(cv, cg):
    return cv * _sigmoid(cg)


def _ln_silu(h, g, b):
    y = _layer_norm(h, g, b)
    return y * _sigmoid(y)


ANY = pl.BlockSpec(memory_space=pl.ANY)


ROWS_TILE = 256
COLS_TILE = 512
DW_TILE = 512
RESIDENT_K = 2048
STREAM_K_TILE = 1024
STREAM_ROWS = 512


def _k_tiles(kdim):
    if kdim <= RESIDENT_K:
        return ROWS_TILE, kdim
    return STREAM_ROWS, max(t for t in range(LANES, STREAM_K_TILE + 1, LANES) if kdim % t == 0)


def _rowop_mm(name, kind, rows, g, w, dims, out_dtype, u=None, after=()):
    s = rows[0].shape[0]
    n = w.shape[0] if dims == NT else w.shape[1]
    tm, tn = min(ROWS_TILE, s), min(COLS_TILE, n)
    ni, nj = s // tm, n // tn
    bwd = kind == "rms_bwd"

    def body(*refs):
        refs = list(refs)
        row_refs = [refs.pop(0) for _ in rows]
        g_ref, w_ref = refs.pop(0), refs.pop(0)
        u_ref = refs.pop(0) if u is not None else None
        del refs[:len(after)]
        out_ref, a_ref = refs.pop(0), refs.pop(0)
        dg_ref = refs.pop(0) if bwd else None
        a_all = refs.pop(0)
        t = pl.program_id(0)

        @pl.when(t < ni)
        def _():
            if bwd:
                a, dg = _rms_bwd(row_refs[0][...], g_ref[...], row_refs[1][...])
                dg_ref[0] = dg
            else:
                a = _rms(row_refs[0][...], g_ref[...])
            a_ref[...] = a.astype(BF16)
            a_all[pl.ds(pl.multiple_of(t * tm, tm), tm), :] = a.astype(BF16)

        @pl.when(t >= ni)
        def _():
            acc = _dot(a_all[...], w_ref[...], dims)
            if u_ref is not None:
                acc = acc * (2.0 * jnp.maximum(u_ref[...], 0.0))
            out_ref[...] = acc.astype(out_dtype)

    rows_at = lambda t: jnp.minimum(t, ni - 1)
    cols_at = lambda t: jnp.maximum(t - ni, 0)
    row_spec = pl.BlockSpec((tm, D), lambda t: (rows_at(t), 0))
    w_spec = (pl.BlockSpec((tn, D), lambda t: (cols_at(t), 0)) if dims == NT
              else pl.BlockSpec((D, tn), lambda t: (0, cols_at(t))))
    tile = pl.BlockSpec((s, tn), lambda t: (0, cols_at(t)))
    in_specs = [row_spec] * len(rows) + [pl.BlockSpec((1, D), lambda t: (0, 0)), w_spec]
    in_specs += ([tile] if u is not None else []) + [ANY] * len(after)
    out_shape = [jax.ShapeDtypeStruct((s, n), out_dtype), jax.ShapeDtypeStruct((s, D), BF16)]
    out_specs = [tile, row_spec]
    if bwd:
        out_shape.append(jax.ShapeDtypeStruct((ni, 1, D), F32))
        out_specs.append(pl.BlockSpec((1, 1, D), lambda t: (rows_at(t), 0, 0)))
    return pl.pallas_call(
        body, name=name, grid=(ni + nj,), in_specs=in_specs, out_specs=out_specs, out_shape=out_shape,
        scratch_shapes=[pltpu.VMEM((s, D), BF16)], compiler_params=_cparams(),
    )(*rows, g, w, *([u] if u is not None else []), *after)


def _mm_rowop(name, kind, pairs, rows, g, relu2=False, after=()):
    s, kdim = pairs[0][0].shape
    tm, tk = _k_tiles(kdim)
    tm = min(tm, s)
    ni, nk = s // tm, kdim // tk
    npair = len(pairs)

    def body(*refs):
        refs = list(refs)
        a_refs = [refs.pop(0) for _ in range(npair)]
        w_refs = [refs.pop(0) for _ in range(npair)]
        row_refs = [refs.pop(0) for _ in rows]
        g_ref = refs.pop(0)
        del refs[:len(after)]
        acc = refs.pop()
        outs = refs
        k = pl.program_id(1)

        @pl.when(k == 0)
        def _():
            acc[...] = jnp.zeros_like(acc)

        for a_ref, w_ref, (_, _, dims) in zip(a_refs, w_refs, pairs):
            a = a_ref[...]
            if relu2:
                a = jnp.square(jnp.maximum(a, 0.0))
            acc[...] += _dot(a.astype(BF16), w_ref[...], dims)

        @pl.when(k == nk - 1)
        def _():
            h = acc[...]
            if kind == "rms_res":
                outs[0][...] = row_refs[0][...] + _rms(h, g_ref[...])
                outs[1][...] = h
            else:
                dx, dg = _rms_bwd(row_refs[0][...], g_ref[...], h)
                if kind == "rms_bwd_res":
                    outs[0][...] = row_refs[1][...] + dx
                    outs[1][0] = dg
                else:
                    outs[0][0] = dg

    row_spec = pl.BlockSpec((tm, D), lambda i, k: (i, 0))
    dg_shape = jax.ShapeDtypeStruct((ni, 1, D), F32)
    dg_spec = pl.BlockSpec((1, 1, D), lambda i, k: (i, 0, 0))
    in_specs = [pl.BlockSpec((tm, tk), lambda i, k: (i, k))] * npair
    for _, _, dims in pairs:
        in_specs.append(pl.BlockSpec((tk, D), lambda i, k: (k, 0)) if dims == NN
                        else pl.BlockSpec((D, tk), lambda i, k: (0, k)))
    in_specs += [row_spec] * len(rows) + [pl.BlockSpec((1, D), lambda i, k: (0, 0))] + [ANY] * len(after)
    if kind == "rms_res":
        out_shape = [jax.ShapeDtypeStruct((s, D), F32)] * 2
        out_specs = [row_spec, row_spec]
    elif kind == "rms_bwd_res":
        out_shape = [jax.ShapeDtypeStruct((s, D), F32), dg_shape]
        out_specs = [row_spec, dg_spec]
    else:
        out_shape = [dg_shape]
        out_specs = [dg_spec]
    return pl.pallas_call(
        body, name=name, grid=(ni, nk), in_specs=in_specs, out_specs=out_specs, out_shape=out_shape,
        scratch_shapes=[pltpu.VMEM((tm, D), F32)], compiler_params=_cparams(),
    )(*[p[0] for p in pairs], *[p[1] for p in pairs], *rows, g, *after)


def _mm_tn(name, a, gmat, relu2=False, after=()):
    s, m = a.shape
    tm = min(DW_TILE, m)
    ni = m // tm

    def body(a_ref, g_ref, *rest):
        av = a_ref[...]
        if relu2:
            av = jnp.square(jnp.maximum(av, 0.0))
        rest[len(after)][...] = _dot(av.astype(BF16), g_ref[...], TN).astype(BF16)

    return pl.pallas_call(
        body, name=name, grid=(ni,),
        in_specs=[pl.BlockSpec((s, tm), lambda i: (0, i)), pl.BlockSpec((s, D), lambda i: (0, 0))] + [ANY] * len(after),
        out_specs=pl.BlockSpec((tm, D), lambda i: (i, 0)),
        out_shape=jax.ShapeDtypeStruct((m, D), BF16), compiler_params=_cparams(),
    )(a, gmat, *after)


def _tril():
    r = lax.broadcasted_iota(jnp.int32, (HD, HD), 0)
    c = lax.broadcasted_iota(jnp.int32, (HD, HD), 1)
    return (c <= r).astype(F32)


def _gmlp_fwd(z, gv, ws, bst, tb):
    s = z.shape[0]
    tb = min(tb, s)

    def body(zu_ref, zv_ref, gv_ref, ws_ref, bst_ref, y_ref):
        tril = _tril()
        for h in range(NH):
            cols = slice(h * HD, (h + 1) * HD)
            u, vln = _gmlp_rows(zu_ref[:, cols], zv_ref[:, cols], gv_ref[h:h + 1, :])
            wm = (ws_ref[h] * tril).astype(BF16)
            vb = vln.astype(BF16)
            for c in range(tb // HD):
                rws = slice(c * HD, (c + 1) * HD)
                mixed = _dot(wm, vb[rws], NN) + bst_ref[:, h:h + 1]
                y_ref[rws, cols] = (u[rws] * mixed).astype(BF16)

    return pl.pallas_call(
        body, name="gmlp_fwd", grid=(s // tb,),
        in_specs=[pl.BlockSpec((tb, GW), lambda i: (i, 0)), pl.BlockSpec((tb, GW), lambda i: (i, 1)),
                  pl.BlockSpec((NH, HD), lambda i: (0, 0)), pl.BlockSpec((NH, HD, HD), lambda i: (0, 0, 0)),
                  pl.BlockSpec((HD, NH), lambda i: (0, 0))],
        out_specs=pl.BlockSpec((tb, GW), lambda i: (i, 0)),
        out_shape=jax.ShapeDtypeStruct((s, D), BF16), compiler_params=_cparams(),
    )(z, z, gv, ws, bst)


def _gmlp_bwd(z, dy, gv, ws, bst, tb, after=()):
    s = z.shape[0]
    tb = min(tb, s)
    nb = s // tb

    def body(zu_ref, zv_ref, dy_ref, gv_ref, ws_ref, bst_ref, *rest):
        dz_ref, dgv_ref, dws_ref, db_ref = rest[len(after):]
        tril = _tril()
        for h in range(NH):
            cols = slice(h * HD, (h + 1) * HD)
            (u, vln), vjp = jax.vjp(_gmlp_rows, zu_ref[:, cols], zv_ref[:, cols], gv_ref[h:h + 1, :])
            wmf = ws_ref[h] * tril
            wm = wmf.astype(BF16)
            wmt = wmf.T.astype(BF16)
            vb = vln.astype(BF16)
            dws = jnp.zeros((HD, HD), F32)
            db = jnp.zeros((HD, 1), F32)
            du_parts, dvln_parts = [], []
            for c in range(tb // HD):
                rws = slice(c * HD, (c + 1) * HD)
                mixed = _dot(wm, vb[rws], NN) + bst_ref[:, h:h + 1]
                dyc = dy_ref[rws, cols]
                du_parts.append(dyc * mixed)
                dmixed = dyc * u[rws]
                dmb = dmixed.astype(BF16)
                dws = dws + _dot(dmb, vb[rws], NT)
                db = db + jnp.sum(dmixed, axis=1, keepdims=True)
                dvln_parts.append(_dot(wmt, dmb, NN))
            du = jnp.concatenate(du_parts, axis=0)
            dvln = jnp.concatenate(dvln_parts, axis=0)
            dzu, dzv, dgv = vjp((du, dvln))
            dz_ref[:, cols] = dzu.astype(BF16)
            dz_ref[:, slice(GW + h * HD, GW + (h + 1) * HD)] = dzv.astype(BF16)
            dgv_ref[0, h:h + 1, :] = dgv
            dws_ref[0, h] = dws * tril
            db_ref[0, h] = jnp.broadcast_to(db, (HD, LANES))

    blk = pl.BlockSpec((tb, GW), lambda i: (i, 0))
    return pl.pallas_call(
        body, name="gmlp_bwd", grid=(nb,),
        in_specs=[blk, pl.BlockSpec((tb, GW), lambda i: (i, 1)), blk,
                  pl.BlockSpec((NH, HD), lambda i: (0, 0)), pl.BlockSpec((NH, HD, HD), lambda i: (0, 0, 0)),
                  pl.BlockSpec((HD, NH), lambda i: (0, 0))] + [ANY] * len(after),
        out_specs=[pl.BlockSpec((tb, 2 * GW), lambda i: (i, 0)), pl.BlockSpec((1, NH, HD), lambda i: (i, 0, 0)),
                   pl.BlockSpec((1, NH, HD, HD), lambda i: (i, 0, 0, 0)),
                   pl.BlockSpec((1, NH, HD, LANES), lambda i: (i, 0, 0, 0))],
        out_shape=[jax.ShapeDtypeStruct((s, IN_COLS), BF16),
                   jax.ShapeDtypeStruct((nb, NH, HD), F32), jax.ShapeDtypeStruct((nb, NH, HD, HD), F32),
                   jax.ShapeDtypeStruct((nb, NH, HD, LANES), F32)],
        compiler_params=_cparams(),
    )(z, z, dy, gv, ws, bst, *after)


def _pool_count(t0, window):
    pos = (t0 + lax.broadcasted_iota(jnp.int32, (ROW_TILE, LANES), 0)).astype(F32)
    return jnp.minimum(pos + 1.0, float(window))


def _window_sum(win, levels, back):
    n = win.shape[0]
    for lv in range(levels):
        step = 1 << lv
        win = win + pltpu.roll(win, n - step if back else step, 0)
    return win


def _pool_pooled(ppad_ref, t0, g):
    win = ppad_ref[pl.ds(t0, ROW_TILE + POOL_HALO), :]
    wsum = _window_sum(win, g + 1, False)[POOL_HALO:]
    return wsum / _pool_count(t0, POOL_WINDOWS[g]) - win[POOL_HALO:]


def _pool_fwd(z, wp, sp, y):
    s = z.shape[0]
    nt = s // ROW_TILE

    def body(p_ref, wp_ref, sp_ref, _, y_ref, ppad):
        for g in range(NG):
            cols = slice(g * LANES, (g + 1) * LANES)
            ppad[pl.ds(0, POOL_HALO), :] = jnp.zeros((POOL_HALO, LANES), F32)
            ppad[pl.ds(POOL_HALO, s), :] = p_ref[:, cols]
            wpb = wp_ref[g].astype(BF16)
            scale = sp_ref[:, cols]

            def tile(t, carry):
                t0 = pl.multiple_of(t * ROW_TILE, ROW_TILE)
                pooled = _pool_pooled(ppad, t0, g)
                y_ref[pl.ds(t0, ROW_TILE), cols] = (_dot(pooled.astype(BF16), wpb, NN) * scale).astype(BF16)
                return carry

            lax.fori_loop(0, nt, tile, 0)

    return pl.pallas_call(
        body, name="pool_fwd", grid=(1,),
        in_specs=[pl.BlockSpec((s, PW), lambda i: (0, 2 * GW // PW)),
                  pl.BlockSpec((NG, LANES, LANES), lambda i: (0, 0, 0)), pl.BlockSpec((1, PW), lambda i: (0, 0)), ANY],
        out_specs=pl.BlockSpec((s, PW), lambda i: (0, GW // PW)),
        out_shape=jax.ShapeDtypeStruct((s, D), BF16), input_output_aliases={3: 0},
        scratch_shapes=[pltpu.VMEM((s + POOL_HALO, LANES), F32)], compiler_params=_cparams(),
    )(z, wp, sp, y)


def _pool_bwd(z, dy, wp, sp, dz):
    s = z.shape[0]
    nt = s // ROW_TILE

    def body(p_ref, dy_ref, wp_ref, sp_ref, _, dp_ref, dwp_ref, dsp_ref, ppad, rpad, dpool):
        for g in range(NG):
            cols = slice(g * LANES, (g + 1) * LANES)
            ppad[pl.ds(0, POOL_HALO), :] = jnp.zeros((POOL_HALO, LANES), F32)
            ppad[pl.ds(POOL_HALO, s), :] = p_ref[:, cols]
            rpad[pl.ds(s, POOL_HALO), :] = jnp.zeros((POOL_HALO, LANES), F32)
            wpb = wp_ref[g].astype(BF16)
            scale = sp_ref[:, cols]

            def tile(t, carry):
                dwp, dsp = carry
                t0 = pl.multiple_of(t * ROW_TILE, ROW_TILE)
                pooled = _pool_pooled(ppad, t0, g)
                pb = pooled.astype(BF16)
                dyt = dy_ref[pl.ds(t0, ROW_TILE), cols]
                dsp = dsp + jnp.sum(dyt * _dot(pb, wpb, NN), axis=0, keepdims=True)
                dmm = (dyt * scale).astype(BF16)
                dwp = dwp + _dot(pb, dmm, TN)
                dpooled = _dot(dmm, wpb, NT)
                rpad[pl.ds(t0, ROW_TILE), :] = dpooled / _pool_count(t0, POOL_WINDOWS[g])
                dpool[pl.ds(t0, ROW_TILE), :] = dpooled
                return dwp, dsp

            dwp, dsp = lax.fori_loop(0, nt, tile, (jnp.zeros((LANES, LANES), F32), jnp.zeros((1, LANES), F32)))
            dwp_ref[g] = dwp
            dsp_ref[:, cols] = dsp

            def tile2(t, carry):
                t0 = pl.multiple_of(t * ROW_TILE, ROW_TILE)
                win = rpad[pl.ds(t0, ROW_TILE + POOL_HALO), :]
                back = _window_sum(win, g + 1, True)[:ROW_TILE]
                rows = pl.ds(t0, ROW_TILE)
                dp_ref[rows, cols] = (back - dpool[rows, :]).astype(BF16)
                return carry

            lax.fori_loop(0, nt, tile2, 0)

    return pl.pallas_call(
        body, name="pool_bwd", grid=(1,),
        in_specs=[pl.BlockSpec((s, PW), lambda i: (0, 2 * GW // PW)), pl.BlockSpec((s, PW), lambda i: (0, GW // PW)),
                  pl.BlockSpec((NG, LANES, LANES), lambda i: (0, 0, 0)), pl.BlockSpec((1, PW), lambda i: (0, 0)), ANY],
        out_specs=[pl.BlockSpec((s, PW), lambda i: (0, 2 * GW // PW)),
                   pl.BlockSpec((NG, LANES, LANES), lambda i: (0, 0, 0)), pl.BlockSpec((1, PW), lambda i: (0, 0))],
        out_shape=[jax.ShapeDtypeStruct((s, IN_COLS), BF16), jax.ShapeDtypeStruct((NG, LANES, LANES), F32),
                   jax.ShapeDtypeStruct((1, PW), F32)],
        input_output_aliases={4: 0},
        scratch_shapes=[pltpu.VMEM((s + POOL_HALO, LANES), F32), pltpu.VMEM((s + POOL_HALO, LANES), F32),
                        pltpu.VMEM((s, LANES), F32)],
        compiler_params=_cparams(),
    )(z, dy, wp, sp, dz)


CONV_LEAD = CONV_HALO - (CONV_K - 1)


SUBLANES = 8


def _sublane_shifts(win):
    n = win.shape[0]
    return [win] + [pltpu.roll(win, n - b, 0) for b in range(1, SUBLANES)]


def _shifted(shifts, offset):
    a, b = divmod(offset, SUBLANES)
    return shifts[b][a * SUBLANES:a * SUBLANES + ROW_TILE]


def _conv_taps(shifts, wdw_ref, lead, reverse):
    acc = jnp.zeros((ROW_TILE, CW), F32)
    for j in range(CONV_K):
        tap = (CONV_K - 1 - j) if reverse else j
        acc = acc + wdw_ref[tap:tap + 1, :] * _shifted(shifts, lead + j)
    return acc


def _conv_fill_glu(cv_ref, cg_ref, xpad, s):
    xpad[pl.ds(0, CONV_HALO), :] = jnp.zeros((CONV_HALO, CW), F32)

    def fill(t, carry):
        t0 = pl.multiple_of(t * ROW_TILE, ROW_TILE)
        rows = pl.ds(t0, ROW_TILE)
        xpad[pl.ds(t0 + CONV_HALO, ROW_TILE), :] = _glu(cv_ref[rows, :], cg_ref[rows, :])
        return carry

    lax.fori_loop(0, s // ROW_TILE, fill, 0)


def _conv_fwd(z, wdw, bdw, lng, lnb, y):
    s = z.shape[0]

    def body(cv_ref, cg_ref, wdw_ref, bdw_ref, lng_ref, lnb_ref, _, y_ref, xpad):
        _conv_fill_glu(cv_ref, cg_ref, xpad, s)

        def tile(t, carry):
            t0 = pl.multiple_of(t * ROW_TILE, ROW_TILE)
            shifts = _sublane_shifts(xpad[pl.ds(t0, ROW_TILE + CONV_HALO), :])
            hc = _conv_taps(shifts, wdw_ref, CONV_LEAD, False) + bdw_ref[...]
            y_ref[pl.ds(t0, ROW_TILE), :] = _ln_silu(hc, lng_ref[...], lnb_ref[...]).astype(BF16)
            return carry

        lax.fori_loop(0, s // ROW_TILE, tile, 0)

    vec = pl.BlockSpec((1, CW), lambda i: (0, 0))
    return pl.pallas_call(
        body, name="conv_fwd", grid=(1,),
        in_specs=[pl.BlockSpec((s, CW), lambda i: (0, (2 * GW + PW) // CW)),
                  pl.BlockSpec((s, CW), lambda i: (0, (2 * GW + PW) // CW + 1)),
                  pl.BlockSpec((CONV_K + 1, CW), lambda i: (0, 0)), vec, vec, vec, ANY],
        out_specs=pl.BlockSpec((s, CW), lambda i: (0, (GW + PW) // CW)),
        out_shape=jax.ShapeDtypeStruct((s, D), BF16), input_output_aliases={6: 0},
        scratch_shapes=[pltpu.VMEM((s + CONV_HALO, CW), F32)], compiler_params=_cparams(),
    )(z, z, wdw, bdw, lng, lnb, y)


def _conv_bwd(z, dy, wdw, bdw, lng, lnb, dz):
    s = z.shape[0]

    def body(cv_ref, cg_ref, dy_ref, wdw_ref, bdw_ref, lng_ref, lnb_ref, _,
             dz_ref, dwdw_ref, dbdw_ref, dlng_ref, dlnb_ref, xpad, dpad, dcg_keep):
        @pl.when(pl.program_id(0) == 0)
        def _():
            compute(cv_ref, cg_ref, dy_ref, wdw_ref, bdw_ref, lng_ref, lnb_ref,
                    dz_ref, dcg_keep, dwdw_ref, dbdw_ref, dlng_ref, dlnb_ref, xpad, dpad)

        @pl.when(pl.program_id(0) == 1)
        def _():
            dz_ref[...] = dcg_keep[...]

    def compute(cv_ref, cg_ref, dy_ref, wdw_ref, bdw_ref, lng_ref, lnb_ref,
                dcv_ref, dcg_ref, dwdw_ref, dbdw_ref, dlng_ref, dlnb_ref, xpad, dpad):
        _conv_fill_glu(cv_ref, cg_ref, xpad, s)
        dpad[pl.ds(s, CONV_HALO), :] = jnp.zeros((CONV_HALO, CW), F32)
        dwdw_ref[...] = jnp.zeros((CONV_K + 1, CW), F32)

        def tile(t, carry):
            db, dg, dbeta = carry
            t0 = pl.multiple_of(t * ROW_TILE, ROW_TILE)
            shifts = _sublane_shifts(xpad[pl.ds(t0, ROW_TILE + CONV_HALO), :])
            hc = _conv_taps(shifts, wdw_ref, CONV_LEAD, False) + bdw_ref[...]
            _, vjp = jax.vjp(_ln_silu, hc, lng_ref[...], lnb_ref[...])
            dhc, dg_t, dbeta_t = vjp(dy_ref[pl.ds(t0, ROW_TILE), :])
            dpad[pl.ds(t0, ROW_TILE), :] = dhc
            for j in range(CONV_K):
                dwdw_ref[j:j + 1, :] += jnp.sum(dhc * _shifted(shifts, CONV_LEAD + j), axis=0, keepdims=True)
            return db + jnp.sum(dhc, axis=0, keepdims=True), dg + dg_t, dbeta + dbeta_t

        zero = jnp.zeros((1, CW), F32)
        db, dg, dbeta = lax.fori_loop(0, s // ROW_TILE, tile, (zero, zero, zero))
        dbdw_ref[...] = db
        dlng_ref[...] = dg
        dlnb_ref[...] = dbeta

        def tile2(t, carry):
            t0 = pl.multiple_of(t * ROW_TILE, ROW_TILE)
            rows = pl.ds(t0, ROW_TILE)
            dglu = _conv_taps(_sublane_shifts(dpad[pl.ds(t0, ROW_TILE + CONV_HALO), :]), wdw_ref, 0, True)
            _, vjp = jax.vjp(_glu, cv_ref[rows, :], cg_ref[rows, :])
            dcv, dcg = vjp(dglu)
            dcv_ref[rows, :] = dcv.astype(BF16)
            dcg_ref[rows, :] = dcg.astype(BF16)
            return carry

        lax.fori_loop(0, s // ROW_TILE, tile2, 0)

    vec = pl.BlockSpec((1, CW), lambda i: (0, 0))
    wspec = pl.BlockSpec((CONV_K + 1, CW), lambda i: (0, 0))
    vshape = jax.ShapeDtypeStruct((1, CW), F32)
    return pl.pallas_call(
        body, name="conv_bwd", grid=(2,),
        in_specs=[pl.BlockSpec((s, CW), lambda i: (0, (2 * GW + PW) // CW)),
                  pl.BlockSpec((s, CW), lambda i: (0, (2 * GW + PW) // CW + 1)),
                  pl.BlockSpec((s, CW), lambda i: (0, (GW + PW) // CW)), wspec, vec, vec, vec, ANY],
        out_specs=[pl.BlockSpec((s, CW), lambda i: (0, (2 * GW + PW) // CW + i)), wspec, vec, vec, vec],
        out_shape=[jax.ShapeDtypeStruct((s, IN_COLS), BF16), jax.ShapeDtypeStruct((CONV_K + 1, CW), F32),
                   vshape, vshape, vshape],
        input_output_aliases={7: 0},
        scratch_shapes=[pltpu.VMEM((s + CONV_HALO, CW), F32), pltpu.VMEM((s + CONV_HALO, CW), F32),
                        pltpu.VMEM((s, CW), BF16)],
        compiler_params=_cparams(),
    )(z, z, dy, wdw, bdw, lng, lnb, dz)


def _softmax_rows(sc):
    e = jnp.exp(sc - jnp.max(sc, axis=-1, keepdims=True))
    return e / jnp.sum(e, axis=-1, keepdims=True)


def _attn_fwd(q, k, v, tq):
    s, m = q.shape[0], k.shape[0]
    tq = min(tq, s)

    def body(q_ref, k_ref, v_ref, o_ref):
        for h in range(XH):
            cols = slice(h * XHD, (h + 1) * XHD)
            p = _softmax_rows(_dot(q_ref[:, cols], k_ref[:, cols], NT) * ATT_SCALE)
            o_ref[:, cols] = _dot(p.astype(BF16), v_ref[:, cols], NN).astype(BF16)

    kv = pl.BlockSpec((m, D), lambda i: (0, 0))
    return pl.pallas_call(
        body, name="attn_fwd", grid=(s // tq,),
        in_specs=[pl.BlockSpec((tq, D), lambda i: (i, 0)), kv, kv],
        out_specs=pl.BlockSpec((tq, D), lambda i: (i, 0)),
        out_shape=jax.ShapeDtypeStruct((s, D), BF16), compiler_params=_cparams(),
    )(q, k, v)


def _attn_bwd(q, k, v, do, tq, after=()):
    s, m = q.shape[0], k.shape[0]
    tq = min(tq, s)

    def body(q_ref, k_ref, v_ref, do_ref, *rest):
        dq_ref, dk_ref, dv_ref = rest[len(after):]

        @pl.when(pl.program_id(0) == 0)
        def _():
            dk_ref[...] = jnp.zeros_like(dk_ref)
            dv_ref[...] = jnp.zeros_like(dv_ref)

        for h in range(XH):
            cols = slice(h * XHD, (h + 1) * XHD)
            qh, kh, vh, doh = q_ref[:, cols], k_ref[:, cols], v_ref[:, cols], do_ref[:, cols]
            p = _softmax_rows(_dot(qh, kh, NT) * ATT_SCALE)
            dp = _dot(doh, vh, NT)
            dv_ref[:, cols] += _dot(p.astype(BF16), doh, TN)
            ds = (p * (dp - jnp.sum(p * dp, axis=-1, keepdims=True)) * ATT_SCALE).astype(BF16)
            dq_ref[:, cols] = _dot(ds, kh, NN).astype(BF16)
            dk_ref[:, cols] += _dot(ds, qh, TN)

    kv = pl.BlockSpec((m, D), lambda i: (0, 0))
    qs = pl.BlockSpec((tq, D), lambda i: (i, 0))
    return pl.pallas_call(
        body, name="attn_bwd", grid=(s // tq,),
        in_specs=[qs, kv, kv, qs] + [ANY] * len(after), out_specs=[qs, kv, kv],
        out_shape=[jax.ShapeDtypeStruct((s, D), BF16), jax.ShapeDtypeStruct((m, D), F32),
                   jax.ShapeDtypeStruct((m, D), F32)],
        compiler_params=_cparams(),
    )(q, k, v, do, *after)


def _loss_head(y, target, tm):
    s = y.shape[0]
    tm = min(tm, s)

    def body(y_ref, t_ref, dy_ref, part_ref):
        err = y_ref[...] - t_ref[...]
        dy_ref[...] = err * (1.0 / D)
        part_ref[...] = jnp.full((1, 8, LANES), 0.5 * jnp.sum(err * err) * (1.0 / D), F32)

    blk = pl.BlockSpec((tm, D), lambda i: (i, 0))
    return pl.pallas_call(
        body, name="loss_head", grid=(s // tm,), in_specs=[blk, blk],
        out_specs=[blk, pl.BlockSpec((1, 8, LANES), lambda i: (i, 0, 0))],
        out_shape=[jax.ShapeDtypeStruct((s, D), F32), jax.ShapeDtypeStruct((s // tm, 8, LANES), F32)],
        compiler_params=_cparams(),
    )(y, target)


def _layer_fwd(x0, mem, w, p, fetch):
    z, hn0 = _rowop_mm("mix_in", "rms", (x0,), p["norm_mix_pre"], w["w_in"], NT, F32)
    y = _gmlp_fwd(z, p["gmlp_v_gain"], p["w_spatial"], p["b_spatial_t"], 512)
    y = _pool_fwd(z, p["w_pool"], p["s_pool"], y)
    y = _conv_fwd(z, p["w_dw"], p["b_dw"], p["conv_ln_g"], p["conv_ln_b"], y)
    w.update(fetch("out", (y,)))
    x1, h0 = _mm_rowop("mix_out", "rms_res", [(y, w["w_out"], NN)], (x0,), p["norm_mix_post"])
    w.update(fetch("att", (x1,)))
    q, hn1 = _rowop_mm("att_q", "rms", (x1,), p["norm_xattn_pre"], w["w_q"], NN, BF16)
    k, mn = _rowop_mm("att_k", "rms", (mem,), p["norm_mem"], w["w_k"], NN, BF16, after=(x1,))
    v, _ = _rowop_mm("att_v", "rms", (mem,), p["norm_mem"], w["w_v"], NN, BF16, after=(x1,))
    o = _attn_fwd(q, k, v, 512)
    x2, h1 = _mm_rowop("att_o", "rms_res", [(o, w["w_o"], NN)], (x1,), p["norm_xattn_post"])
    w.update(fetch("up", (x2,)))
    u, hn2 = _rowop_mm("ffn_up", "rms", (x2,), p["norm_ffn_pre"], w["w_up"], NT, F32)
    w.update(fetch("down", (u,)))
    x3, h2 = _mm_rowop("ffn_down", "rms_res", [(u, w["w_down"], NN)], (x2,), p["norm_ffn_post"], relu2=True)
    saved = dict(x0=x0, z=z, hn0=hn0, y=y, h0=h0, x1=x1, q=q, hn1=hn1, k=k, v=v, mn=mn, o=o, h1=h1, x2=x2, u=u,
                 hn2=hn2, h2=h2)
    return x3, saved


def _layer_bwd(dx3, mem, w, p, sv, red):
    gs = {}
    du, dh2, dg = _rowop_mm("ffn_down_bwd", "rms_bwd", (sv["h2"], dx3), p["norm_ffn_post"], w["w_down"], NT, BF16,
                            u=sv["u"], after=red.after())
    gs["norm_ffn_post"] = jnp.sum(dg, axis=0)
    g_down = _mm_tn("ffn_down_dw", sv["u"], dh2, relu2=True)
    red.advance((g_down,))
    dx2, dg = _mm_rowop("ffn_up_bwd", "rms_bwd_res", [(du, w["w_up"], NN)], (sv["x2"], dx3), p["norm_ffn_pre"],
                        after=red.after())
    gs["norm_ffn_pre"] = jnp.sum(dg, axis=0)
    g_up = _mm_tn("ffn_up_dw", du, sv["hn2"])
    red.add("ffn", ("w_down", "w_up"), [g_down, g_up])
    do, dh1, dg = _rowop_mm("att_o_bwd", "rms_bwd", (sv["h1"], dx2), p["norm_xattn_post"], w["w_o"], NT, BF16,
                            after=red.after())
    gs["norm_xattn_post"] = jnp.sum(dg, axis=0)
    g_o = _mm_tn("att_o_dw", sv["o"], dh1)
    red.advance((g_o,))
    dq, dk, dv = _attn_bwd(sv["q"], sv["k"], sv["v"], do, 512, after=red.after())
    dk, dv = dk.astype(BF16), dv.astype(BF16)
    dx1, dg = _mm_rowop("att_q_bwd", "rms_bwd_res", [(dq, w["w_q"], NT)], (sv["x1"], dx2), p["norm_xattn_pre"],
                        after=red.after())
    gs["norm_xattn_pre"] = jnp.sum(dg, axis=0)
    g_q = _mm_tn("att_q_dw", sv["hn1"], dq)
    g_k = _mm_tn("att_k_dw", sv["mn"], dk)
    g_v = _mm_tn("att_v_dw", sv["mn"], dv)
    (dg,) = _mm_rowop("att_kv_bwd", "rms_bwd_gain", [(dk, w["w_k"], NT), (dv, w["w_v"], NT)], (mem,), p["norm_mem"])
    gs["norm_mem"] = jnp.sum(dg, axis=0)
    red.add("att", ("w_o", "w_q", "w_k", "w_v"), [g_o, g_q, g_k, g_v])
    dy, dh0, dg = _rowop_mm("mix_out_bwd", "rms_bwd", (sv["h0"], dx1), p["norm_mix_post"], w["w_out"], NT, F32,
                            after=red.after())
    gs["norm_mix_post"] = jnp.sum(dg, axis=0)
    g_out = _mm_tn("mix_out_dw", sv["y"], dh0)
    red.advance((g_out,))
    red.add("out", ("w_out",), [g_out])
    z = sv["z"]
    dz, dgv, dws, dbs = _gmlp_bwd(z, dy, p["gmlp_v_gain"], p["w_spatial"], p["b_spatial_t"], 512, after=red.after())
    gs["gmlp_v_gain"] = jnp.sum(dgv, axis=0)
    gs["w_spatial"] = jnp.sum(dws, axis=0)
    gs["b_spatial"] = jnp.sum(dbs[..., 0], axis=0)
    dz, gs["w_pool"], gs["s_pool"] = _pool_bwd(z, dy, p["w_pool"], p["s_pool"], dz)
    dz, dwdw, gs["b_dw"], gs["conv_ln_g"], gs["conv_ln_b"] = _conv_bwd(
        z, dy, p["w_dw"], p["b_dw"], p["conv_ln_g"], p["conv_ln_b"], dz)
    red.advance((dz,))
    g_in = _mm_tn("mix_in_dw", dz, sv["hn0"], after=red.after())
    red.add("in", ("w_in",), [g_in])
    if red.layer == 0:
        red.advance(())
    red.small("mixer", _small_grad_arrays(gs, dwdw, norms=False))
    dx0, dg = _mm_rowop("mix_in_bwd", "rms_bwd_res", [(dz, w["w_in"], NN)], (sv["x0"], dx1), p["norm_mix_pre"],
                        after=red.after())
    gs["norm_mix_pre"] = jnp.sum(dg, axis=0)
    late = {"norms": jnp.concatenate([gs[n] for n in NORM_NAMES], axis=0)}
    if red.layer == 0:
        late["loss"] = red.extra[0]
    red.small("norms", late)
    return dx0


NORM_NAMES = ("norm_mix_pre", "norm_mix_post", "norm_xattn_pre", "norm_mem", "norm_xattn_post", "norm_ffn_pre",
              "norm_ffn_post")
VEC_NAMES = ("s_pool", "b_dw", "conv_ln_g", "conv_ln_b")
SMALL_ARRAYS = ("norms", "gain_bias", "w_spatial", "w_pool", "vecs", "w_dw")


def _small_grad_arrays(gs, dwdw, norms=True):
    out = {"norms": jnp.concatenate([gs[n] for n in NORM_NAMES], axis=0)} if norms else {}
    out.update({"gain_bias": jnp.concatenate([gs["gmlp_v_gain"], gs["b_spatial"]], axis=0),
                "w_spatial": gs["w_spatial"], "w_pool": gs["w_pool"],
                "vecs": jnp.concatenate([gs[n] for n in VEC_NAMES], axis=0), "w_dw": dwdw})
    return out


def _split_small_grads(arrays):
    out = {n: arrays["norms"][k] for k, n in enumerate(NORM_NAMES)}
    out.update({n: arrays["vecs"][k] for k, n in enumerate(VEC_NAMES)})
    out.update(gmlp_v_gain=arrays["gain_bias"][:NH], b_spatial=arrays["gain_bias"][NH:], w_spatial=arrays["w_spatial"],
               w_pool=arrays["w_pool"], w_dw=arrays["w_dw"][:CONV_K])
    return out


def _layer_params(small, l):
    p = {n: small[n][l].reshape(1, -1) for n in ("norm_mix_pre", "norm_mix_post", "s_pool", "b_dw", "conv_ln_g",
                                                   "conv_ln_b", "norm_xattn_pre", "norm_mem", "norm_xattn_post",
                                                   "norm_ffn_pre", "norm_ffn_post")}
    p["gmlp_v_gain"] = small["gmlp_v_gain"][l]
    p["w_spatial"] = small["w_spatial"][l]
    p["b_spatial_t"] = small["b_spatial"][l].T
    p["w_pool"] = small["w_pool"][l]
    p["w_dw"] = jnp.pad(small["w_dw"][l], ((0, 1), (0, 0)))
    return p


def _local_step(x, mem, target, fetch, small, red):
    small = dict(small)
    saved, weights, params = [], [], []
    h = x
    marker = ()
    for l in range(DEPTH):
        w = fetch(l, "in", marker)
        if "taps" in w:
            small["w_dw"] = w.pop("taps")
        p = _layer_params(small, l)
        h, sv = _layer_fwd(h, mem, w, p, functools.partial(fetch, l))
        marker = (h,)
        saved.append(sv)
        weights.append(w)
        params.append(p)
    dh, loss = _loss_head(h, target, 512)
    red.extra = (loss,)
    for l in reversed(range(DEPTH)):
        red.layer = l
        dh = _layer_bwd(dh, mem, weights[l], params[l], saved[l], red)
    return loss, dh


HBM = pl.BlockSpec(memory_space=pltpu.HBM)


def _position():
    return lax.axis_index("x"), lax.axis_index("y"), lax.axis_index("c")


SEM = pl.BlockSpec(memory_space=pltpu.SEMAPHORE)
EFFECT = pltpu.SideEffectType.DATAFLOW_SIDE_EFFECTING
TOKEN = jax.ShapeDtypeStruct((8, LANES), F32)
TOKEN_SPEC = pl.BlockSpec(memory_space=pltpu.VMEM)


def _landing(shape, dtype):
    return pltpu.with_memory_space_constraint(lax.empty(shape, dtype), pltpu.HBM)


def _hbm_shapes(arrays):
    return [pltpu.HBM(a.shape, a.dtype) for a in arrays]


def _block(ref, r, dev):
    return ref.at[pl.ds((4 * dev[0] + 2 * dev[1] + dev[2]) * r, r), :]


def _split_call(name, body, thru, sems_in, after, sems_out, token):
    n = len(thru)
    out_shape = [pltpu.SemaphoreType.DMA(s) for s in sems_out] + _hbm_shapes(thru) + ([TOKEN] if token else [])
    out_specs = [SEM] * len(sems_out) + [HBM] * n + ([TOKEN_SPEC] if token else [])
    return pl.pallas_call(
        body, name=name, in_specs=[HBM] * n + [SEM] * len(sems_in) + [ANY] * len(after),
        out_specs=out_specs, out_shape=out_shape,
        input_output_aliases={i: len(sems_out) + i for i in range(n)},
        compiler_params=pltpu.CompilerParams(has_side_effects=EFFECT),
    )(*thru, *sems_in, *after)


def _place_own(name, srcs, dev, out_dtype, tr):
    n = len(srcs)
    r, cols = srcs[0][0].shape[-2:]
    tr = r if r < 16 else _row_tile(r, tr)
    nb = r // tr

    def body(dev_ref, *refs):
        for a in range(n):
            refs[n + a][...] = refs[a][...].astype(out_dtype)

    in_specs = [pl.BlockSpec((tr, cols), lambda i, d: (i, 0)) if l is None
                else pl.BlockSpec((None, tr, cols), lambda i, d, l=l: (l, i, 0)) for _, l in srcs]
    return pl.pallas_call(
        body, name=name,
        grid_spec=pltpu.PrefetchScalarGridSpec(
            num_scalar_prefetch=1, grid=(nb,), in_specs=in_specs,
            out_specs=[pl.BlockSpec((tr, cols), lambda i, d: (d[0] * nb + i, 0))] * n),
        out_shape=[jax.ShapeDtypeStruct((N_DEV * r, cols), out_dtype)] * n, compiler_params=_cparams(),
    )(dev, *[a for a, _ in srcs])


def _gather_peers(x, y, c):
    return [(1 - x, y, c), (x, 1 - y, c), (1 - x, 1 - y, c), (x, y, 1 - c)]


def _block_rows(land):
    return land.shape[0] // N_DEV


def _near_peers(x, y, c):
    return [(1 - x, y, c), (x, 1 - y, c), (x, y, 1 - c)]


def _relay_route(x, y, c):
    origin = (x + c * (1 - 2 * x), y + (1 - c) * (1 - 2 * y), c)
    target = (x + (1 - c) * (1 - 2 * x), y + c * (1 - 2 * y), c)
    return origin, target


def _same_block_copy(blk, send_sem, recv_sem, to):
    return pltpu.make_async_remote_copy(src_ref=blk, dst_ref=blk, send_sem=send_sem, recv_sem=recv_sem, device_id=to,
                                        device_id_type=MESH)


def _gather_start(name, lands, after):
    n = len(lands)

    def body(*refs):
        lz = refs[:n]
        send_sems, recv_sems = refs[n + len(after)], refs[n + len(after) + 1]
        token = refs[-1]
        x, y, c = _position()
        for a in range(n):
            own = _block(lz[a], _block_rows(lands[a]), (x, y, c))
            for k, to in enumerate(_near_peers(x, y, c)):
                _same_block_copy(own, send_sems.at[k], recv_sems.at[k], to).start()
        token[...] = jnp.zeros_like(token)

    out = _split_call(name, body, list(lands), [], after, [(3,), (3,)], True)
    return out[0], out[1], out[2:2 + n], out[-1]


def _gather_step(name, near, far, fresh, after):
    groups = [g for g in (near and near[0], far and far[0], fresh) if g]
    counts = [len(near[0]) if near else 0, len(far[0]) if far else 0, len(fresh) if fresh else 0]
    n = sum(counts)
    sems_in = ([near[1]] if near else []) + ([far[1]] if far else [])
    sems_out = ([(2,), (2,), (1,), (1,)] if near else []) + ([(1,), (1,)] if far else []) + ([(3,), (3,)] if fresh else [])

    def body(*refs):
        lz = list(refs[:n])
        ins = list(refs[n:n + len(sems_in)])
        outs = list(refs[n + len(sems_in) + len(after):n + len(sems_in) + len(after) + len(sems_out)])
        token = refs[-1]
        x, y, c = _position()
        me, sibling = (x, y, c), (x, y, 1 - c)
        near_lz, far_lz, fresh_lz = (lz[sum(counts[:i]):sum(counts[:i + 1])] for i in range(3))
        neighbours = _near_peers(x, y, c)[:2]
        origin, target = _relay_route(x, y, c)
        diagonal = (1 - x, 1 - y, c)
        if near:
            recv0 = ins.pop(0)
            fsend, frecv, rsend, rrecv = (outs.pop(0) for _ in range(4))
            for a, land in enumerate(near[0]):
                for j, chip in enumerate(neighbours):
                    _same_block_copy(_block(near_lz[a], _block_rows(land), chip), fsend.at[j], recv0.at[j], me).wait_recv()
        if far:
            rrecv_in = ins.pop(0)
            f2send, f2recv = outs.pop(0), outs.pop(0)
            for a, land in enumerate(far[0]):
                _same_block_copy(_block(far_lz[a], _block_rows(land), diagonal), f2send.at[0], rrecv_in.at[0], me).wait_recv()
            for a, land in enumerate(far[0]):
                _same_block_copy(_block(far_lz[a], _block_rows(land), diagonal), f2send.at[0], f2recv.at[0], sibling).start()
        if near:
            for a, land in enumerate(near[0]):
                r = _block_rows(land)
                _same_block_copy(_block(near_lz[a], r, origin), rsend.at[0], rrecv.at[0], target).start()
                for j, chip in enumerate(neighbours):
                    _same_block_copy(_block(near_lz[a], r, chip), fsend.at[j], frecv.at[j], sibling).start()
        if fresh:
            send_sems, recv_sems = outs.pop(0), outs.pop(0)
            for a, land in enumerate(fresh):
                own = _block(fresh_lz[a], _block_rows(land), me)
                for k, to in enumerate(_near_peers(x, y, c)):
                    _same_block_copy(own, send_sems.at[k], recv_sems.at[k], to).start()
        token[...] = jnp.zeros_like(token)

    out = list(_split_call(name, body, [l for g in groups for l in g], sems_in, after, sems_out, True))
    res = {"token": out.pop()}
    if near:
        res.update(fsend=out.pop(0), frecv=out.pop(0), rsend=out.pop(0), rrecv=out.pop(0))
    if far:
        res.update(f2send=out.pop(0), f2recv=out.pop(0))
    if fresh:
        res.update(send=out.pop(0), recv=out.pop(0))
    res["near"], res["far"], res["fresh"] = (out[sum(counts[:i]):sum(counts[:i + 1])] for i in range(3))
    return res


def _gather_finish(name, lands, send_sems, recv_sems, fsend, frecv, rsend, f2send, f2recv, after):
    n = len(lands)

    def body(*refs):
        lz = refs[:n]
        send0, recv0, fsend_ref, frecv_ref, rsend_ref, f2send_ref, f2recv_ref = refs[n:n + 7]
        x, y, c = _position()
        me = (x, y, c)
        near = _near_peers(x, y, c)[:2]
        origin, _ = _relay_route(x, y, c)
        for a in range(n):
            r = _block_rows(lands[a])
            sib = _block(lz[a], r, (x, y, 1 - c))
            _same_block_copy(sib, send0.at[2], recv0.at[2], me).wait_recv()
            for j, chip in enumerate(near):
                blk = _block(lz[a], r, (chip[0], chip[1], 1 - c))
                _same_block_copy(blk, fsend_ref.at[j], frecv_ref.at[j], me).wait_recv()
            far = _block(lz[a], r, (1 - x, 1 - y, 1 - c))
            _same_block_copy(far, f2send_ref.at[0], f2recv_ref.at[0], me).wait_recv()
            own = _block(lz[a], r, me)
            for k in range(3):
                _same_block_copy(own, send0.at[k], recv0.at[k], me).wait_send()
            for j, chip in enumerate(near):
                _same_block_copy(_block(lz[a], r, chip), fsend_ref.at[j], frecv_ref.at[j], me).wait_send()
            _same_block_copy(_block(lz[a], r, origin), rsend_ref.at[0], recv0.at[0], me).wait_send()
            _same_block_copy(_block(lz[a], r, (1 - x, 1 - y, c)), f2send_ref.at[0], f2recv_ref.at[0], me).wait_send()

    return _split_call(name, body, list(lands), [send_sems, recv_sems, fsend, frecv, rsend, f2send, f2recv], after, [],
                       False)


def _sibling_start(name, grads, after):
    n = len(grads)
    lands = [_landing((4, g.shape[0] // N_DEV, D), g.dtype) for g in grads]

    def body(*refs):
        ins, lz = refs[:n], refs[n:2 * n]
        send_sem, recv_sem = refs[2 * n + len(after)], refs[2 * n + len(after) + 1]
        token = refs[-1]
        x, y, c = _position()
        for a in range(n):
            r = grads[a].shape[0] // N_DEV
            for q in range(4):
                pltpu.make_async_remote_copy(
                    src_ref=ins[a].at[pl.ds((2 * q + 1 - c) * r, r), :], dst_ref=lz[a].at[q], send_sem=send_sem.at[0],
                    recv_sem=recv_sem.at[0], device_id=(x, y, 1 - c), device_id_type=MESH).start()
        token[...] = jnp.zeros_like(token)

    out = _split_call(name, body, list(grads) + lands, [], after, [(1,), (1,)], True)
    return out[0], out[1], out[2:2 + n], out[2 + n:2 + 2 * n], out[-1]


def _sibling_finish(name, grads, lands, send_sem, recv_sem, after):
    n = len(grads)

    def body(*refs):
        ins, lz = refs[:n], refs[n:2 * n]
        send_ref, recv_ref = refs[2 * n], refs[2 * n + 1]
        x, y, c = _position()
        for a in range(n):
            r = grads[a].shape[0] // N_DEV
            for q in range(4):
                cp = pltpu.make_async_remote_copy(
                    src_ref=ins[a].at[pl.ds((2 * q + 1 - c) * r, r), :], dst_ref=lz[a].at[q], send_sem=send_ref.at[0],
                    recv_sem=recv_ref.at[0], device_id=(x, y, c), device_id_type=MESH)
                cp.wait_send()
                cp.wait_recv()

    out = _split_call(name, body, list(grads) + list(lands), [send_sem, recv_sem], after, [], False)
    return out[:n], out[n:2 * n]


def _chip_start(name, parts, after):
    n = len(parts)
    lands = [_landing((3,) + p.shape[1:], p.dtype) for p in parts]

    def body(*refs):
        ins, lz = refs[:n], refs[n:2 * n]
        send_sems, recv_sems = refs[2 * n + len(after)], refs[2 * n + len(after) + 1]
        token = refs[-1]
        x, y, c = _position()
        for a in range(n):
            for j, chip in enumerate(_gather_peers(x, y, c)[:3]):
                pltpu.make_async_remote_copy(
                    src_ref=ins[a].at[2 * chip[0] + chip[1]], dst_ref=lz[a].at[j], send_sem=send_sems.at[j],
                    recv_sem=recv_sems.at[j], device_id=chip, device_id_type=MESH).start()
        token[...] = jnp.zeros_like(token)

    out = _split_call(name, body, list(parts) + lands, [], after, [(3,), (3,)], True)
    return out[0], out[1], out[2:2 + n], out[2 + n:2 + 2 * n], out[-1]


def _chip_finish(name, parts, lands, send_sems, recv_sems, after):
    n = len(parts)

    def body(*refs):
        ins, lz = refs[:n], refs[n:2 * n]
        send_ref, recv_ref = refs[2 * n], refs[2 * n + 1]
        me = _position()
        for a in range(n):
            for j in range(3):
                cp = pltpu.make_async_remote_copy(
                    src_ref=ins[a].at[j], dst_ref=lz[a].at[j], send_sem=send_ref.at[j], recv_sem=recv_ref.at[j],
                    device_id=me, device_id_type=MESH)
                cp.wait_send()
                cp.wait_recv()

    out = _split_call(name, body, list(parts) + list(lands), [send_sems, recv_sems], after, [], False)
    return out[:n], out[n:2 * n]


def _other_devices(x, y, c):
    return [(x + (k >> 2 & 1) * (1 - 2 * x), y + (k >> 1 & 1) * (1 - 2 * y), c + (k & 1) * (1 - 2 * c))
            for k in range(1, N_DEV)]


def _broadcast_start(name, arrays, after):
    n = len(arrays)
    lands = [_landing((N_DEV,) + a.shape, a.dtype) for a in arrays]

    def body(*refs):
        ins, lz = refs[:n], refs[n:2 * n]
        send_sems, recv_sems = refs[2 * n + len(after)], refs[2 * n + len(after) + 1]
        token = refs[-1]
        x, y, c = _position()
        for a in range(n):
            for k, peer in enumerate(_other_devices(x, y, c)):
                pltpu.make_async_remote_copy(
                    src_ref=ins[a], dst_ref=lz[a].at[4 * x + 2 * y + c], send_sem=send_sems.at[k],
                    recv_sem=recv_sems.at[k], device_id=peer, device_id_type=MESH).start()
        token[...] = jnp.zeros_like(token)

    out = _split_call(name, body, list(arrays) + lands, [], after, [(N_DEV - 1,), (N_DEV - 1,)], True)
    return out[0], out[1], out[2:2 + n], out[2 + n:2 + 2 * n], out[-1]


def _broadcast_finish(name, arrays, lands, send_sems, recv_sems, after):
    n = len(arrays)

    def body(*refs):
        ins, lz = refs[:n], refs[n:2 * n]
        send_ref, recv_ref = refs[2 * n], refs[2 * n + 1]
        x, y, c = _position()
        for a in range(n):
            for k, peer in enumerate(_other_devices(x, y, c)):
                cp = pltpu.make_async_remote_copy(
                    src_ref=ins[a], dst_ref=lz[a].at[4 * peer[0] + 2 * peer[1] + peer[2]], send_sem=send_ref.at[k],
                    recv_sem=recv_ref.at[k], device_id=(x, y, c), device_id_type=MESH)
                cp.wait_send()
                cp.wait_recv()

    out = _split_call(name, body, list(arrays) + list(lands), [send_sems, recv_sems], after, [], False)
    return out[:n], out[n:2 * n]


def _row_tile(r, target):
    return max(t for t in range(16, min(r, target) + 1, 16) if r % t == 0)


CHIP_PARTIAL_BYTES = 12 * 1024 * 1024


def _chip_partial(name, grads, gots, c):
    n = len(grads)
    r = grads[0].shape[0] // N_DEV
    tr = _row_tile(r, CHIP_PARTIAL_BYTES // (n * 3 * D * 2))

    def body(c_ref, *refs):
        for a in range(n):
            refs[2 * n + a][...] = (refs[a][...].astype(F32) + refs[n + a][...].astype(F32)).astype(BF16)

    blk = pl.BlockSpec((None, tr, D), lambda q, i, c_ref: (q, i, 0))
    return pl.pallas_call(
        body, name=name,
        grid_spec=pltpu.PrefetchScalarGridSpec(
            num_scalar_prefetch=1, grid=(4, r // tr),
            in_specs=[pl.BlockSpec((None, None, tr, D), lambda q, i, c_ref: (q, c_ref[0], i, 0))] * n + [blk] * n,
            out_specs=[blk] * n),
        out_shape=[jax.ShapeDtypeStruct((4, r, D), BF16)] * n, compiler_params=_cparams(),
    )(c, *[g.reshape(4, 2, r, D) for g in grads], *gots)


class _WeightGather:
    def __init__(self, groups):
        self.groups = list(groups)
        self.index = {key: i for i, (key, _, _) in enumerate(groups)}
        self.state = [None] * len(groups)
        self.token = ()
        for i in range(min(2, len(groups))):
            self._start(i)

    def _tag(self, i):
        return "%s_%d" % self.groups[i][0][::-1]

    def _start(self, i):
        send, recv, lz, tok = _gather_start("gather_start_" + self._tag(i), self.groups[i][2], self.token)
        self.state[i] = dict(send=send, recv=recv, lands=lz)
        self.token = (tok,)

    def _step(self, name, near, far, fresh, marker):
        exists = lambda i: i is not None and i < len(self.groups)
        near, far, fresh = (i if exists(i) else None for i in (near, far, fresh))
        res = _gather_step(
            name, None if near is None else (self.state[near]["lands"], self.state[near]["recv"]),
            None if far is None else (self.state[far]["lands"], self.state[far]["rrecv"]),
            None if fresh is None else self.groups[fresh][2], tuple(marker) + self.token)
        self.token = (res["token"],)
        if near is not None:
            self.state[near].update(lands=res["near"], fsend=res["fsend"], frecv=res["frecv"], rsend=res["rsend"],
                                    rrecv=res["rrecv"])
        if far is not None:
            self.state[far].update(lands=res["far"], f2send=res["f2send"], f2recv=res["f2recv"])
        if fresh is not None:
            self.state[fresh] = dict(send=res["send"], recv=res["recv"], lands=res["fresh"])

    def fetch(self, layer, group, marker):
        k = self.index[(layer, group)]
        if k == 0:
            self._step("gather_step_first", 0, None, None, marker)
        self._step("gather_step_" + self._tag(k), k + 1, k, k + 2, marker)
        st = self.state[k]
        lz = _gather_finish("gather_finish_" + self._tag(k), st["lands"], st["send"], st["recv"], st["fsend"],
                            st["frecv"], st["rsend"], st["f2send"], st["f2recv"], self.token)
        self.state[k] = None
        return dict(zip(self.groups[k][1], lz))


class _GradReduce:
    def __init__(self, core, chip):
        self.core, self.chip = core, chip
        self.layer = None
        self.token = ()
        self.at_sibling, self.at_chips = [], []
        self.extra, self.smalls = (), {}

    def after(self):
        return self.token

    def add(self, group, names, grads):
        tag = "%s_%d" % (group, self.layer)
        send, recv, grads, lands, tok = _sibling_start("grad_sibling_start_" + tag, grads, self.token)
        self.at_sibling.append((tag, [(self.layer, n) for n in names], send, recv, grads, lands))
        self.token = (tok,)

    def advance(self, marker):
        for tag, keys, send, recv, grads, lands in self.at_sibling:
            grads, lands = _sibling_finish("grad_sibling_finish_" + tag, grads, lands, send, recv, marker)
            parts = _chip_partial("chip_partial_" + tag, grads, lands, self.core)
            send, recv, parts, lands, tok = _chip_start("grad_chip_start_" + tag, parts, ())
            self.at_chips.append([tag, keys, send, recv, parts, lands])
            self.token = (tok,)
        self.at_sibling = []

    def small(self, part, arrays):
        keys = list(arrays)
        send, recv, own, slots, tok = _broadcast_start(
            "small_grads_start_%d_%s" % (self.layer, part), [arrays[k] for k in keys], self.token)
        self.smalls.setdefault(self.layer, []).append((part, keys, send, recv, own, slots))
        self.token = (tok,)

    def small_finish(self, layer, marker):
        mine, theirs = {}, {}
        for part, keys, send, recv, own, slots in self.smalls[layer]:
            own, slots = _broadcast_finish("small_grads_finish_%d_%s" % (layer, part), own, slots, send, recv, marker)
            mine.update(zip(keys, own))
            theirs.update(zip(keys, slots))
        return mine, theirs

    def collect(self, key, marker):
        for entry in self.at_chips:
            tag, keys, send, recv, parts, lands = entry
            if key in keys:
                if send is not None:
                    parts, lands = _chip_finish("grad_chip_finish_" + tag, parts, lands, send, recv, marker)
                    entry[2:] = [None, None, parts, lands]
                i = keys.index(key)
                return parts[i], lands[i]
        raise KeyError(key)


def _adamw_math(w, g, m, v):
    m = ADAM_B1 * m + (1.0 - ADAM_B1) * g
    v = ADAM_B2 * v + (1.0 - ADAM_B2) * jnp.square(g)
    m_hat = m / (1.0 - ADAM_B1 ** ADAM_STEP)
    v_hat = v / (1.0 - ADAM_B2 ** ADAM_STEP)
    delta = -ADAM_LR * (m_hat / (jnp.sqrt(v_hat) + ADAM_EPS) + ADAM_WD * w)
    return delta, m, v


def _adamw_small(wts, mom_m, mom_v, own, gathered, loss_own, loss_gathered, dev):
    names = SMALL
    nw = len(names)
    na = len(SMALL_ARRAYS)

    def body(dev_ref, *refs):
        w_refs, m_refs, v_refs = (dict(zip(names, refs[i * nw:(i + 1) * nw])) for i in range(3))
        own_refs = refs[3 * nw:3 * nw + DEPTH * na]
        g_refs = refs[3 * nw + DEPTH * na:3 * nw + 2 * DEPTH * na]
        loss_own_ref, loss_got_ref = refs[3 * nw + 2 * DEPTH * na:3 * nw + 2 * DEPTH * na + 2]
        outs = refs[3 * nw + 2 * DEPTH * na + 2:]
        g_out, d_out, m_out, v_out = (dict(zip(names, outs[i * nw:(i + 1) * nw])) for i in range(4))
        me = dev_ref[0]

        loss = None
        for d in range(N_DEV):
            for b in range(loss_own.shape[0]):
                term = jnp.where(me == d, loss_own_ref[b], loss_got_ref[d, b])
                loss = term if loss is None else loss + term
        outs[4 * nw][...] = loss

        def update(name, at, g):
            g_out[name][at] = g
            d_out[name][at], m_out[name][at], v_out[name][at] = _adamw_math(
                w_refs[name][at], g, m_refs[name][at], v_refs[name][at])

        for l in range(DEPTH):
            mine = dict(zip(SMALL_ARRAYS, own_refs[l * na:(l + 1) * na]))
            got = dict(zip(SMALL_ARRAYS, g_refs[l * na:(l + 1) * na]))

            def total(key, at):
                acc = None
                for d in range(N_DEV):
                    term = jnp.where(me == d, mine[key][at] if at else mine[key][...], got[key][(d,) + at])
                    acc = term if acc is None else acc + term
                return acc

            row = (slice(l, l + 1),)
            for k, name in enumerate(NORM_NAMES):
                update(name, row, total("norms", (slice(k, k + 1),)))
            for k, name in enumerate(VEC_NAMES):
                update(name, row, total("vecs", (slice(k, k + 1),)))
            update("gmlp_v_gain", (l,), total("gain_bias", (slice(0, NH),)))
            update("b_spatial", (l,), total("gain_bias", (slice(NH, 2 * NH),)))
            update("w_spatial", (l,), total("w_spatial", ()))
            update("w_pool", (l,), total("w_pool", ()))
            update("w_dw", (l,), total("w_dw", (slice(0, CONV_K),)))

    args = [src[n] for src in (wts, mom_m, mom_v) for n in names]
    args += [src[l][k] for src in (own, gathered) for l in range(DEPTH) for k in SMALL_ARRAYS]
    args += [loss_own, loss_gathered]
    outs = pl.pallas_call(
        body, name="adamw_small",
        in_specs=[pl.BlockSpec(memory_space=pltpu.SMEM)] + [pl.BlockSpec(memory_space=pltpu.VMEM)] * len(args),
        out_shape=[jax.ShapeDtypeStruct(wts[n].shape, F32) for _ in range(4) for n in names]
        + [jax.ShapeDtypeStruct((8, LANES), F32)],
        compiler_params=_cparams(),
    )(dev, *args)
    return tuple(dict(zip(names, outs[i * nw:(i + 1) * nw])) for i in range(4)) + (outs[4 * nw],)


def _adamw_layers(name, w, reduced, m, v, chip, tr, transposed=False, after=()):
    nl, r, cdim = w.shape
    tr = _row_tile(r, tr)
    nb = r // tr

    def body(q_ref, w_ref, p0_ref, g0_ref, p1_ref, g1_ref, m_ref, v_ref, *rest):
        g_ref, d_ref, nm_ref, nv_ref = rest[len(after):]

        def total(p_ref, got_ref):
            acc = p_ref[...].astype(F32)
            for j in range(3):
                acc = acc + got_ref[j].astype(F32)
            return acc

        g = jnp.where(pl.program_id(0) == 0, total(p0_ref, g0_ref), total(p1_ref, g1_ref))
        if transposed:
            g = g.T
        g_ref[...] = g
        d_ref[...], nm_ref[...], nv_ref[...] = _adamw_math(w_ref[...], g, m_ref[...], v_ref[...])

    blk = pl.BlockSpec((None, tr, cdim), lambda l, i, q: (l, i, 0))
    first = lambda l, i: i * (1 - l) + (nb - 1) * l
    second = lambda l, i: i * l
    if transposed:
        gshape = (cdim, tr)
        at = lambda lead, i: (lead, 0, i)
    else:
        gshape = (tr, cdim)
        at = lambda lead, i: (lead, i, 0)
    specs = [blk,
             pl.BlockSpec((None,) + gshape, lambda l, i, q: at(q[0], first(l, i))),
             pl.BlockSpec((3,) + gshape, lambda l, i, q: at(0, first(l, i))),
             pl.BlockSpec((None,) + gshape, lambda l, i, q: at(q[0], second(l, i))),
             pl.BlockSpec((3,) + gshape, lambda l, i, q: at(0, second(l, i))), blk, blk] + [ANY] * len(after)
    shape = jax.ShapeDtypeStruct((nl, r, cdim), F32)
    return pl.pallas_call(
        body, name=name,
        grid_spec=pltpu.PrefetchScalarGridSpec(num_scalar_prefetch=1, grid=(nl, nb), in_specs=specs, out_specs=[blk] * 4),
        out_shape=[shape] * 4, compiler_params=_cparams(),
    )(chip, w, *reduced[0], *reduced[1], m, v, *after)


def _to_rows(name, a):
    return jnp.swapaxes(a, 1, 2) if name == "w_in" else a


def _place_own_transposed(name, srcs, dev, out_dtype, tc):
    n = len(srcs)
    kdim, cdim = srcs[0][0].shape[-2:]

    def body(dev_ref, *refs):
        for a in range(n):
            refs[n + a][...] = refs[a][...].T.astype(out_dtype)

    return pl.pallas_call(
        body, name=name,
        grid_spec=pltpu.PrefetchScalarGridSpec(
            num_scalar_prefetch=1, grid=(kdim // tc,),
            in_specs=[pl.BlockSpec((None, tc, cdim), lambda i, d, l=l: (l, i, 0)) for _, l in srcs],
            out_specs=[pl.BlockSpec((cdim, tc), lambda i, d: (d[0], i))] * n),
        out_shape=[jax.ShapeDtypeStruct((N_DEV * cdim, kdim), out_dtype)] * n, compiler_params=_cparams(),
    )(dev, *[a for a, _ in srcs])


def _pack(arrays, rows):
    flat = jnp.concatenate([a.reshape(-1) for a in arrays])
    return jnp.pad(flat, (0, rows * D - flat.shape[0])).reshape(rows, D)


def _rows_for(shapes, mult=8):
    total = 0
    for shp in shapes:
        size = 1
        for dim in shp:
            size *= dim
        total += size
    return -(-total // (mult * D)) * mult


def kernel(x, mem, norm_mix_pre, norm_mix_post, w_in, w_out, gmlp_v_gain, w_spatial, b_spatial, w_pool, s_pool, w_dw, b_dw, conv_ln_g, conv_ln_b, norm_xattn_pre, norm_mem, norm_xattn_post, w_q, w_k, w_v, w_o, norm_ffn_pre, norm_ffn_post, w_up, w_down, loss_target, m_norm_mix_pre, m_norm_mix_post, m_w_in, m_w_out, m_gmlp_v_gain, m_w_spatial, m_b_spatial, m_w_pool, m_s_pool, m_w_dw, m_b_dw, m_conv_ln_g, m_conv_ln_b, m_norm_xattn_pre, m_norm_mem, m_norm_xattn_post, m_w_q, m_w_k, m_w_v, m_w_o, m_norm_ffn_pre, m_norm_ffn_post, m_w_up, m_w_down, v_norm_mix_pre, v_norm_mix_post, v_w_in, v_w_out, v_gmlp_v_gain, v_w_spatial, v_b_spatial, v_w_pool, v_s_pool, v_w_dw, v_b_dw, v_conv_ln_g, v_conv_ln_b, v_norm_xattn_pre, v_norm_mem, v_norm_xattn_post, v_w_q, v_w_k, v_w_v, v_w_o, v_norm_ffn_pre, v_norm_ffn_post, v_w_up, v_w_down):
    args = dict(locals())
    wts = {n: args[n] for n in WEIGHTS}
    mom_m = {n: args["m_" + n] for n in WEIGHTS}
    mom_v = {n: args["v_" + n] for n in WEIGHTS}
    xi, yi, ci = _position()
    me = 4 * xi + 2 * yi + ci

    dev = jnp.reshape(me, (1,)).astype(jnp.int32)
    lands = {}
    for call, names, tr in (("place_in", ("w_in",), 256), ("place_att", ("w_out", "w_q", "w_k", "w_v", "w_o"), 64),
                            ("place_up", ("w_up",), 256), ("place_down", ("w_down",), 256)):
        srcs = [(_to_rows(n, wts[n]), l) for l in range(DEPTH) for n in names]
        placed = (_place_own_transposed if names == ("w_up",) else _place_own)(call, srcs, dev, BF16, tr)
        lands.update(zip([(l, n) for l in range(DEPTH) for n in names], placed))
    (lands[(0, "taps")],) = _place_own("place_taps", [(_pack([w_dw], _rows_for([w_dw.shape])), None)], dev, F32, 8)
    groups = []
    for l in range(DEPTH):
        for group, names in GATHER_GROUPS:
            if (l, group) == (0, "in"):
                names = names + ("taps",)
            groups.append(((l, group), names, [lands[(l, n)] for n in names]))
    gather = _WeightGather(groups)

    def fetch(layer, group, marker):
        w = gather.fetch(layer, group, marker)
        if "taps" in w:
            blocks = w["taps"].reshape(N_DEV, -1)[:, :w_dw.size].reshape((N_DEV,) + w_dw.shape)
            w["taps"] = jnp.moveaxis(blocks, 0, 2).reshape(DEPTH, CONV_K, CW)
        return w

    reduce = _GradReduce(jnp.reshape(ci, (1,)).astype(jnp.int32), jnp.reshape(2 * xi + yi, (1,)).astype(jnp.int32))
    small = {n: wts[n] for n in SMALL if n != "w_dw"}
    _, dx = _local_step(x[0], mem[0], loss_target[0], fetch, small, reduce)
    reduce.advance((dx,))

    grad_w, delta, new_m, new_v = {}, {}, {}, {}
    marker = (dx,) + tuple(reduce.after())
    for n in UPDATE_ORDER:
        reduced = [reduce.collect((l, n), marker) for l in range(DEPTH)]
        outs = _adamw_layers("adamw_" + n, _to_rows(n, wts[n]), reduced, _to_rows(n, mom_m[n]), _to_rows(n, mom_v[n]),
                             reduce.chip, 256, transposed=n == "w_up", after=marker)
        grad_w[n], delta[n], new_m[n], new_v[n] = (_to_rows(n, o) for o in outs)
        marker = (outs[1],)

    own, slots = [None] * DEPTH, [None] * DEPTH
    for l in reversed(range(DEPTH)):
        own[l], slots[l] = reduce.small_finish(l, marker)
        if l == 0:
            loss_own, loss_slots = own[l].pop("loss"), slots[l].pop("loss")
    shard_cols = CW // N_DEV
    for l in range(DEPTH):
        own[l]["w_dw"] = lax.dynamic_slice_in_dim(own[l]["w_dw"], me * shard_cols, shard_cols, axis=1)
        slots[l]["w_dw"] = lax.dynamic_slice_in_dim(slots[l]["w_dw"], me * shard_cols, shard_cols, axis=2)
    *small_out, loss_tile = _adamw_small(wts, mom_m, mom_v, own, slots, loss_own, loss_slots, dev)
    for dst, src in zip((grad_w, delta, new_m, new_v), small_out):
        dst.update(src)

    return (loss_tile[0, 0], dx[None], *[grad_w[n] for n in WEIGHTS], *[delta[n] for n in WEIGHTS],
            *[new_m[n] for n in WEIGHTS], *[new_v[n] for n in WEIGHTS])
```

```python
import functools

import jax
import jax.numpy as jnp
from jax import lax
from jax.experimental import pallas as pl
from jax.experimental.pallas import tpu as pltpu

F32 = jnp.float32
BF16 = jnp.bfloat16

D = 2048
GW = 1024
PW = 512
CW = 512
HD = 128
NH = 8
NG = 4
POOL_WINDOWS = (2, 4, 8, 16)
CONV_K = 31
IN_COLS = 2 * GW + PW + 2 * CW
XH = 4
XHD = D // XH
ATT_SCALE = XHD ** -0.5
RMS_EPS = 1e-6
LN_EPS = 1e-5
DEPTH = 2
N_DEV = 8

ADAM_LR = 0.001
ADAM_B1 = 0.9
ADAM_B2 = 0.999
ADAM_EPS = 1e-08
ADAM_WD = 0.01
ADAM_STEP = 10

LANES = 128
CONV_HALO = 32
POOL_HALO = 16
ROW_TILE = 64
VMEM_LIMIT = 60 * 1024 * 1024

MESH = pl.DeviceIdType.MESH
NT = (((1,), (1,)), ((), ()))
NN = (((1,), (0,)), ((), ()))
TN = (((0,), (0,)), ((), ()))

UPDATE_ORDER = ("w_down", "w_up", "w_o", "w_q", "w_k", "w_v", "w_out", "w_in")
GATHER_GROUPS = (("in", ("w_in",)), ("out", ("w_out",)), ("att", ("w_q", "w_k", "w_v", "w_o")), ("up", ("w_up",)),
                 ("down", ("w_down",)))
SMALL = ("norm_mix_pre", "norm_mix_post", "gmlp_v_gain", "w_spatial", "b_spatial", "w_pool", "s_pool",
         "w_dw", "b_dw", "conv_ln_g", "conv_ln_b", "norm_xattn_pre", "norm_mem", "norm_xattn_post",
         "norm_ffn_pre", "norm_ffn_post")
WEIGHTS = ("norm_mix_pre", "norm_mix_post", "w_in", "w_out", "gmlp_v_gain", "w_spatial", "b_spatial", "w_pool",
           "s_pool", "w_dw", "b_dw", "conv_ln_g", "conv_ln_b", "norm_xattn_pre", "norm_mem", "norm_xattn_post",
           "w_q", "w_k", "w_v", "w_o", "norm_ffn_pre", "norm_ffn_post", "w_up", "w_down")


def _cparams():
    return pltpu.CompilerParams(vmem_limit_bytes=VMEM_LIMIT)


def _dot(a, b, dims):
    return lax.dot_general(a, b, dims, preferred_element_type=F32)


def _rms(x, g):
    y = x * lax.rsqrt(jnp.mean(x * x, axis=-1, keepdims=True) + RMS_EPS)
    return y * g


def _rms_bwd(x, g, dy):
    r = lax.rsqrt(jnp.mean(x * x, axis=-1, keepdims=True) + RMS_EPS)
    xh = x * r
    t = dy * g
    dx = r * (t - xh * jnp.mean(t * xh, axis=-1, keepdims=True))
    return dx, jnp.sum(dy * xh, axis=0, keepdims=True)


def _gelu(x):
    cdf = 0.5 * (1.0 + jnp.tanh(0.7978845608028654 * (x + 0.044715 * (x * x * x))))
    return x * cdf


def _layer_norm(x, g, b=None):
    mu = jnp.mean(x, axis=-1, keepdims=True)
    xc = x - mu
    var = jnp.mean(xc * xc, axis=-1, keepdims=True)
    y = xc * lax.rsqrt(var + LN_EPS) * g
    return y if b is None else y + b


def _sigmoid(x):
    return 1.0 / (1.0 + jnp.exp(-x))


def _gmlp_rows(zu, zv, gv):
    return _gelu(zu), _layer_norm(_gelu(zv), gv)


def _glu(cv, cg):
    return cv * _sigmoid(cg)


def _ln_silu(h, g, b):
    y = _layer_norm(h, g, b)
    return y * _sigmoid(y)


ANY = pl.BlockSpec(memory_space=pl.ANY)


ROWS_TILE = 256
COLS_TILE = 512
DW_TILE = 512
RESIDENT_K = 2048
STREAM_K_TILE = 1024
STREAM_ROWS = 512


def _k_tiles(kdim):
    if kdim <= RESIDENT_K:
        return ROWS_TILE, kdim
    return STREAM_ROWS, max(t for t in range(LANES, STREAM_K_TILE + 1, LANES) if kdim % t == 0)


def _rowop_mm(name, kind, rows, g, w, dims, out_dtype, u=None, after=()):
    s = rows[0].shape[0]
    n = w.shape[0] if dims == NT else w.shape[1]
    tm, tn = min(ROWS_TILE, s), min(COLS_TILE, n)
    ni, nj = s // tm, n // tn
    bwd = kind == "rms_bwd"

    def body(*refs):
        refs = list(refs)
        row_refs = [refs.pop(0) for _ in rows]
        g_ref, w_ref = refs.pop(0), refs.pop(0)
        u_ref = refs.pop(0) if u is not None else None
        del refs[:len(after)]
        out_ref, a_ref = refs.pop(0), refs.pop(0)
        dg_ref = refs.pop(0) if bwd else None
        a_all = refs.pop(0)
        t = pl.program_id(0)

        @pl.when(t < ni)
        def _():
            if bwd:
                a, dg = _rms_bwd(row_refs[0][...], g_ref[...], row_refs[1][...])
                dg_ref[0] = dg
            else:
                a = _rms(row_refs[0][...], g_ref[...])
            a_ref[...] = a.astype(BF16)
            a_all[pl.ds(pl.multiple_of(t * tm, tm), tm), :] = a.astype(BF16)

        @pl.when(t >= ni)
        def _():
            acc = _dot(a_all[...], w_ref[...], dims)
            if u_ref is not None:
                acc = acc * (2.0 * jnp.maximum(u_ref[...], 0.0))
            out_ref[...] = acc.astype(out_dtype)

    rows_at = lambda t: jnp.minimum(t, ni - 1)
    cols_at = lambda t: jnp.maximum(t - ni, 0)
    row_spec = pl.BlockSpec((tm, D), lambda t: (rows_at(t), 0))
    w_spec = (pl.BlockSpec((tn, D), lambda t: (cols_at(t), 0)) if dims == NT
              else pl.BlockSpec((D, tn), lambda t: (0, cols_at(t))))
    tile = pl.BlockSpec((s, tn), lambda t: (0, cols_at(t)))
    in_specs = [row_spec] * len(rows) + [pl.BlockSpec((1, D), lambda t: (0, 0)), w_spec]
    in_specs += ([tile] if u is not None else []) + [ANY] * len(after)
    out_shape = [jax.ShapeDtypeStruct((s, n), out_dtype), jax.ShapeDtypeStruct((s, D), BF16)]
    out_specs = [tile, row_spec]
    if bwd:
        out_shape.append(jax.ShapeDtypeStruct((ni, 1, D), F32))
        out_specs.append(pl.BlockSpec((1, 1, D), lambda t: (rows_at(t), 0, 0)))
    return pl.pallas_call(
        body, name=name, grid=(ni + nj,), in_specs=in_specs, out_specs=out_specs, out_shape=out_shape,
        scratch_shapes=[pltpu.VMEM((s, D), BF16)], compiler_params=_cparams(),
    )(*rows, g, w, *([u] if u is not None else []), *after)


def _mm_rowop(name, kind, pairs, rows, g, relu2=False, after=()):
    s, kdim = pairs[0][0].shape
    tm, tk = _k_tiles(kdim)
    tm = min(tm, s)
    ni, nk = s // tm, kdim // tk
    npair = len(pairs)

    def body(*refs):
        refs = list(refs)
        a_refs = [refs.pop(0) for _ in range(npair)]
        w_refs = [refs.pop(0) for _ in range(npair)]
        row_refs = [refs.pop(0) for _ in rows]
        g_ref = refs.pop(0)
        del refs[:len(after)]
        acc = refs.pop()
        outs = refs
        k = pl.program_id(1)

        @pl.when(k == 0)
        def _():
            acc[...] = jnp.zeros_like(acc)

        for a_ref, w_ref, (_, _, dims) in zip(a_refs, w_refs, pairs):
            a = a_ref[...]
            if relu2:
                a = jnp.square(jnp.maximum(a, 0.0))
            acc[...] += _dot(a.astype(BF16), w_ref[...], dims)

        @pl.when(k == nk - 1)
        def _():
            h = acc[...]
            if kind == "rms_res":
                outs[0][...] = row_refs[0][...] + _rms(h, g_ref[...])
                outs[1][...] = h
            else:
                dx, dg = _rms_bwd(row_refs[0][...], g_ref[...], h)
                if kind == "rms_bwd_res":
                    outs[0][...] = row_refs[1][...] + dx
                    outs[1][0] = dg
                else:
                    outs[0][0] = dg

    row_spec = pl.BlockSpec((tm, D), lambda i, k: (i, 0))
    dg_shape = jax.ShapeDtypeStruct((ni, 1, D), F32)
    dg_spec = pl.BlockSpec((1, 1, D), lambda i, k: (i, 0, 0))
    in_specs = [pl.BlockSpec((tm, tk), lambda i, k: (i, k))] * npair
    for _, _, dims in pairs:
        in_specs.append(pl.BlockSpec((tk, D), lambda i, k: (k, 0)) if dims == NN
                        else pl.BlockSpec((D, tk), lambda i, k: (0, k)))
    in_specs += [row_spec] * len(rows) + [pl.BlockSpec((1, D), lambda i, k: (0, 0))] + [ANY] * len(after)
    if kind == "rms_res":
        out_shape = [jax.ShapeDtypeStruct((s, D), F32)] * 2
        out_specs = [row_spec, row_spec]
    elif kind == "rms_bwd_res":
        out_shape = [jax.ShapeDtypeStruct((s, D), F32), dg_shape]
        out_specs = [row_spec, dg_spec]
    else:
        out_shape = [dg_shape]
        out_specs = [dg_spec]
    return pl.pallas_call(
        body, name=name, grid=(ni, nk), in_specs=in_specs, out_specs=out_specs, out_shape=out_shape,
        scratch_shapes=[pltpu.VMEM((tm, D), F32)], compiler_params=_cparams(),
    )(*[p[0] for p in pairs], *[p[1] for p in pairs], *rows, g, *after)


def _mm_tn(name, a, gmat, relu2=False, after=()):
    s, m = a.shape
    tm = min(DW_TILE, m)
    ni = m // tm

    def body(a_ref, g_ref, *rest):
        av = a_ref[...]
        if relu2:
            av = jnp.square(jnp.maximum(av, 0.0))
        rest[len(after)][...] = _dot(av.astype(BF16), g_ref[...], TN).astype(BF16)

    return pl.pallas_call(
        body, name=name, grid=(ni,),
        in_specs=[pl.BlockSpec((s, tm), lambda i: (0, i)), pl.BlockSpec((s, D), lambda i: (0, 0))] + [ANY] * len(after),
        out_specs=pl.BlockSpec((tm, D), lambda i: (i, 0)),
        out_shape=jax.ShapeDtypeStruct((m, D), BF16), compiler_params=_cparams(),
    )(a, gmat, *after)


def _tril():
    r = lax.broadcasted_iota(jnp.int32, (HD, HD), 0)
    c = lax.broadcasted_iota(jnp.int32, (HD, HD), 1)
    return (c <= r).astype(F32)


def _gmlp_fwd(z, gv, ws, bst, tb):
    s = z.shape[0]
    tb = min(tb, s)

    def body(zu_ref, zv_ref, gv_ref, ws_ref, bst_ref, y_ref):
        tril = _tril()
        for h in range(NH):
            cols = slice(h * HD, (h + 1) * HD)
            u, vln = _gmlp_rows(zu_ref[:, cols], zv_ref[:, cols], gv_ref[h:h + 1, :])
            wm = (ws_ref[h] * tril).astype(BF16)
            vb = vln.astype(BF16)
            for c in range(tb // HD):
                rws = slice(c * HD, (c + 1) * HD)
                mixed = _dot(wm, vb[rws], NN) + bst_ref[:, h:h + 1]
                y_ref[rws, cols] = (u[rws] * mixed).astype(BF16)

    return pl.pallas_call(
        body, name="gmlp_fwd", grid=(s // tb,),
        in_specs=[pl.BlockSpec((tb, GW), lambda i: (i, 0)), pl.BlockSpec((tb, GW), lambda i: (i, 1)),
                  pl.BlockSpec((NH, HD), lambda i: (0, 0)), pl.BlockSpec((NH, HD, HD), lambda i: (0, 0, 0)),
                  pl.BlockSpec((HD, NH), lambda i: (0, 0))],
        out_specs=pl.BlockSpec((tb, GW), lambda i: (i, 0)),
        out_shape=jax.ShapeDtypeStruct((s, D), BF16), compiler_params=_cparams(),
    )(z, z, gv, ws, bst)


def _gmlp_bwd(z, dy, gv, ws, bst, tb, after=()):
    s = z.shape[0]
    tb = min(tb, s)
    nb = s // tb

    def body(zu_ref, zv_ref, dy_ref, gv_ref, ws_ref, bst_ref, *rest):
        dz_ref, dgv_ref, dws_ref, db_ref = rest[len(after):]
        tril = _tril()
        for h in range(NH):
            cols = slice(h * HD, (h + 1) * HD)
            (u, vln), vjp = jax.vjp(_gmlp_rows, zu_ref[:, cols], zv_ref[:, cols], gv_ref[h:h + 1, :])
            wmf = ws_ref[h] * tril
            wm = wmf.astype(BF16)
            wmt = wmf.T.astype(BF16)
            vb = vln.astype(BF16)
            dws = jnp.zeros((HD, HD), F32)
            db = jnp.zeros((HD, 1), F32)
            du_parts, dvln_parts = [], []
            for c in range(tb // HD):
                rws = slice(c * HD, (c + 1) * HD)
                mixed = _dot(wm, vb[rws], NN) + bst_ref[:, h:h + 1]
                dyc = dy_ref[rws, cols]
                du_parts.append(dyc * mixed)
                dmixed = dyc * u[rws]
                dmb = dmixed.astype(BF16)
                dws = dws + _dot(dmb, vb[rws], NT)
                db = db + jnp.sum(dmixed, axis=1, keepdims=True)
                dvln_parts.append(_dot(wmt, dmb, NN))
            du = jnp.concatenate(du_parts, axis=0)
            dvln = jnp.concatenate(dvln_parts, axis=0)
            dzu, dzv, dgv = vjp((du, dvln))
            dz_ref[:, cols] = dzu.astype(BF16)
            dz_ref[:, slice(GW + h * HD, GW + (h + 1) * HD)] = dzv.astype(BF16)
            dgv_ref[0, h:h + 1, :] = dgv
            dws_ref[0, h] = dws * tril
            db_ref[0, h] = jnp.broadcast_to(db, (HD, LANES))

    blk = pl.BlockSpec((tb, GW), lambda i: (i, 0))
    return pl.pallas_call(
        body, name="gmlp_bwd", grid=(nb,),
        in_specs=[blk, pl.BlockSpec((tb, GW), lambda i: (i, 1)), blk,
                  pl.BlockSpec((NH, HD), lambda i: (0, 0)), pl.BlockSpec((NH, HD, HD), lambda i: (0, 0, 0)),
                  pl.BlockSpec((HD, NH), lambda i: (0, 0))] + [ANY] * len(after),
        out_specs=[pl.BlockSpec((tb, 2 * GW), lambda i: (i, 0)), pl.BlockSpec((1, NH, HD), lambda i: (i, 0, 0)),
                   pl.BlockSpec((1, NH, HD, HD), lambda i: (i, 0, 0, 0)),
                   pl.BlockSpec((1, NH, HD, LANES), lambda i: (i, 0, 0, 0))],
        out_shape=[jax.ShapeDtypeStruct((s, IN_COLS), BF16),
                   jax.ShapeDtypeStruct((nb, NH, HD), F32), jax.ShapeDtypeStruct((nb, NH, HD, HD), F32),
                   jax.ShapeDtypeStruct((nb, NH, HD, LANES), F32)],
        compiler_params=_cparams(),
    )(z, z, dy, gv, ws, bst, *after)


def _pool_count(t0, window):
    pos = (t0 + lax.broadcasted_iota(jnp.int32, (ROW_TILE, LANES), 0)).astype(F32)
    return jnp.minimum(pos + 1.0, float(window))


def _window_sum(win, levels, back):
    n = win.shape[0]
    for lv in range(levels):
        step = 1 << lv
        win = win + pltpu.roll(win, n - step if back else step, 0)
    return win


def _pool_pooled(ppad_ref, t0, g):
    win = ppad_ref[pl.ds(t0, ROW_TILE + POOL_HALO), :]
    wsum = _window_sum(win, g + 1, False)[POOL_HALO:]
    return wsum / _pool_count(t0, POOL_WINDOWS[g]) - win[POOL_HALO:]


def _pool_fwd(z, wp, sp, y):
    s = z.shape[0]
    nt = s // ROW_TILE

    def body(p_ref, wp_ref, sp_ref, _, y_ref, ppad):
        for g in range(NG):
            cols = slice(g * LANES, (g + 1) * LANES)
            ppad[pl.ds(0, POOL_HALO), :] = jnp.zeros((POOL_HALO, LANES), F32)
            ppad[pl.ds(POOL_HALO, s), :] = p_ref[:, cols]
            wpb = wp_ref[g].astype(BF16)
            scale = sp_ref[:, cols]

            def tile(t, carry):
                t0 = pl.multiple_of(t * ROW_TILE, ROW_TILE)
                pooled = _pool_pooled(ppad, t0, g)
                y_ref[pl.ds(t0, ROW_TILE), cols] = (_dot(pooled.astype(BF16), wpb, NN) * scale).astype(BF16)
                return carry

            lax.fori_loop(0, nt, tile, 0)

    return pl.pallas_call(
        body, name="pool_fwd", grid=(1,),
        in_specs=[pl.BlockSpec((s, PW), lambda i: (0, 2 * GW // PW)),
                  pl.BlockSpec((NG, LANES, LANES), lambda i: (0, 0, 0)), pl.BlockSpec((1, PW), lambda i: (0, 0)), ANY],
        out_specs=pl.BlockSpec((s, PW), lambda i: (0, GW // PW)),
        out_shape=jax.ShapeDtypeStruct((s, D), BF16), input_output_aliases={3: 0},
        scratch_shapes=[pltpu.VMEM((s + POOL_HALO, LANES), F32)], compiler_params=_cparams(),
    )(z, wp, sp, y)


def _pool_bwd(z, dy, wp, sp, dz):
    s = z.shape[0]
    nt = s // ROW_TILE

    def body(p_ref, dy_ref, wp_ref, sp_ref, _, dp_ref, dwp_ref, dsp_ref, ppad, rpad, dpool):
        for g in range(NG):
            cols = slice(g * LANES, (g + 1) * LANES)
            ppad[pl.ds(0, POOL_HALO), :] = jnp.zeros((POOL_HALO, LANES), F32)
            ppad[pl.ds(POOL_HALO, s), :] = p_ref[:, cols]
            rpad[pl.ds(s, POOL_HALO), :] = jnp.zeros((POOL_HALO, LANES), F32)
            wpb = wp_ref[g].astype(BF16)
            scale = sp_ref[:, cols]

            def tile(t, carry):
                dwp, dsp = carry
                t0 = pl.multiple_of(t * ROW_TILE, ROW_TILE)
                pooled = _pool_pooled(ppad, t0, g)
                pb = pooled.astype(BF16)
                dyt = dy_ref[pl.ds(t0, ROW_TILE), cols]
                dsp = dsp + jnp.sum(dyt * _dot(pb, wpb, NN), axis=0, keepdims=True)
                dmm = (dyt * scale).astype(BF16)
                dwp = dwp + _dot(pb, dmm, TN)
                dpooled = _dot(dmm, wpb, NT)
                rpad[pl.ds(t0, ROW_TILE), :] = dpooled / _pool_count(t0, POOL_WINDOWS[g])
                dpool[pl.ds(t0, ROW_TILE), :] = dpooled
                return dwp, dsp

            dwp, dsp = lax.fori_loop(0, nt, tile, (jnp.zeros((LANES, LANES), F32), jnp.zeros((1, LANES), F32)))
            dwp_ref[g] = dwp
            dsp_ref[:, cols] = dsp

            def tile2(t, carry):
                t0 = pl.multiple_of(t * ROW_TILE, ROW_TILE)
                win = rpad[pl.ds(t0, ROW_TILE + POOL_HALO), :]
                back = _window_sum(win, g + 1, True)[:ROW_TILE]
                rows = pl.ds(t0, ROW_TILE)
                dp_ref[rows, cols] = (back - dpool[rows, :]).astype(BF16)
                return carry

            lax.fori_loop(0, nt, tile2, 0)

    return pl.pallas_call(
        body, name="pool_bwd", grid=(1,),
        in_specs=[pl.BlockSpec((s, PW), lambda i: (0, 2 * GW // PW)), pl.BlockSpec((s, PW), lambda i: (0, GW // PW)),
                  pl.BlockSpec((NG, LANES, LANES), lambda i: (0, 0, 0)), pl.BlockSpec((1, PW), lambda i: (0, 0)), ANY],
        out_specs=[pl.BlockSpec((s, PW), lambda i: (0, 2 * GW // PW)),
                   pl.BlockSpec((NG, LANES, LANES), lambda i: (0, 0, 0)), pl.BlockSpec((1, PW), lambda i: (0, 0))],
        out_shape=[jax.ShapeDtypeStruct((s, IN_COLS), BF16), jax.ShapeDtypeStruct((NG, LANES, LANES), F32),
                   jax.ShapeDtypeStruct((1, PW), F32)],
        input_output_aliases={4: 0},
        scratch_shapes=[pltpu.VMEM((s + POOL_HALO, LANES), F32), pltpu.VMEM((s + POOL_HALO, LANES), F32),
                        pltpu.VMEM((s, LANES), F32)],
        compiler_params=_cparams(),
    )(z, dy, wp, sp, dz)


CONV_LEAD = CONV_HALO - (CONV_K - 1)


SUBLANES = 8


def _sublane_shifts(win):
    n = win.shape[0]
    return [win] + [pltpu.roll(win, n - b, 0) for b in range(1, SUBLANES)]


def _shifted(shifts, offset):
    a, b = divmod(offset, SUBLANES)
    return shifts[b][a * SUBLANES:a * SUBLANES + ROW_TILE]


def _conv_taps(shifts, wdw_ref, lead, reverse):
    acc = jnp.zeros((ROW_TILE, CW), F32)
    for j in range(CONV_K):
        tap = (CONV_K - 1 - j) if reverse else j
        acc = acc + wdw_ref[tap:tap + 1, :] * _shifted(shifts, lead + j)
    return acc


def _conv_fill_glu(cv_ref, cg_ref, xpad, s):
    xpad[pl.ds(0, CONV_HALO), :] = jnp.zeros((CONV_HALO, CW), F32)

    def fill(t, carry):
        t0 = pl.multiple_of(t * ROW_TILE, ROW_TILE)
        rows = pl.ds(t0, ROW_TILE)
        xpad[pl.ds(t0 + CONV_HALO, ROW_TILE), :] = _glu(cv_ref[rows, :], cg_ref[rows, :])
        return carry

    lax.fori_loop(0, s // ROW_TILE, fill, 0)


def _conv_fwd(z, wdw, bdw, lng, lnb, y):
    s = z.shape[0]

    def body(cv_ref, cg_ref, wdw_ref, bdw_ref, lng_ref, lnb_ref, _, y_ref, xpad):
        _conv_fill_glu(cv_ref, cg_ref, xpad, s)

        def tile(t, carry):
            t0 = pl.multiple_of(t * ROW_TILE, ROW_TILE)
            shifts = _sublane_shifts(xpad[pl.ds(t0, ROW_TILE + CONV_HALO), :])
            hc = _conv_taps(shifts, wdw_ref, CONV_LEAD, False) + bdw_ref[...]
            y_ref[pl.ds(t0, ROW_TILE), :] = _ln_silu(hc, lng_ref[...], lnb_ref[...]).astype(BF16)
            return carry

        lax.fori_loop(0, s // ROW_TILE, tile, 0)

    vec = pl.BlockSpec((1, CW), lambda i: (0, 0))
    return pl.pallas_call(
        body, name="conv_fwd", grid=(1,),
        in_specs=[pl.BlockSpec((s, CW), lambda i: (0, (2 * GW + PW) // CW)),
                  pl.BlockSpec((s, CW), lambda i: (0, (2 * GW + PW) // CW + 1)),
                  pl.BlockSpec((CONV_K + 1, CW), lambda i: (0, 0)), vec, vec, vec, ANY],
        out_specs=pl.BlockSpec((s, CW), lambda i: (0, (GW + PW) // CW)),
        out_shape=jax.ShapeDtypeStruct((s, D), BF16), input_output_aliases={6: 0},
        scratch_shapes=[pltpu.VMEM((s + CONV_HALO, CW), F32)], compiler_params=_cparams(),
    )(z, z, wdw, bdw, lng, lnb, y)


def _conv_bwd(z, dy, wdw, bdw, lng, lnb, dz):
    s = z.shape[0]

    def body(cv_ref, cg_ref, dy_ref, wdw_ref, bdw_ref, lng_ref, lnb_ref, _,
             dz_ref, dwdw_ref, dbdw_ref, dlng_ref, dlnb_ref, xpad, dpad, dcg_keep):
        @pl.when(pl.program_id(0) == 0)
        def _():
            compute(cv_ref, cg_ref, dy_ref, wdw_ref, bdw_ref, lng_ref, lnb_ref,
                    dz_ref, dcg_keep, dwdw_ref, dbdw_ref, dlng_ref, dlnb_ref, xpad, dpad)

        @pl.when(pl.program_id(0) == 1)
        def _():
            dz_ref[...] = dcg_keep[...]

    def compute(cv_ref, cg_ref, dy_ref, wdw_ref, bdw_ref, lng_ref, lnb_ref,
                dcv_ref, dcg_ref, dwdw_ref, dbdw_ref, dlng_ref, dlnb_ref, xpad, dpad):
        _conv_fill_glu(cv_ref, cg_ref, xpad, s)
        dpad[pl.ds(s, CONV_HALO), :] = jnp.zeros((CONV_HALO, CW), F32)
        dwdw_ref[...] = jnp.zeros((CONV_K + 1, CW), F32)

        def tile(t, carry):
            db, dg, dbeta = carry
            t0 = pl.multiple_of(t * ROW_TILE, ROW_TILE)
            shifts = _sublane_shifts(xpad[pl.ds(t0, ROW_TILE + CONV_HALO), :])
            hc = _conv_taps(shifts, wdw_ref, CONV_LEAD, False) + bdw_ref[...]
            _, vjp = jax.vjp(_ln_silu, hc, lng_ref[...], lnb_ref[...])
            dhc, dg_t, dbeta_t = vjp(dy_ref[pl.ds(t0, ROW_TILE), :])
            dpad[pl.ds(t0, ROW_TILE), :] = dhc
            for j in range(CONV_K):
                dwdw_ref[j:j + 1, :] += jnp.sum(dhc * _shifted(shifts, CONV_LEAD + j), axis=0, keepdims=True)
            return db + jnp.sum(dhc, axis=0, keepdims=True), dg + dg_t, dbeta + dbeta_t

        zero = jnp.zeros((1, CW), F32)
        db, dg, dbeta = lax.fori_loop(0, s // ROW_TILE, tile, (zero, zero, zero))
        dbdw_ref[...] = db
        dlng_ref[...] = dg
        dlnb_ref[...] = dbeta

        def tile2(t, carry):
            t0 = pl.multiple_of(t * ROW_TILE, ROW_TILE)
            rows = pl.ds(t0, ROW_TILE)
            dglu = _conv_taps(_sublane_shifts(dpad[pl.ds(t0, ROW_TILE + CONV_HALO), :]), wdw_ref, 0, True)
            _, vjp = jax.vjp(_glu, cv_ref[rows, :], cg_ref[rows, :])
            dcv, dcg = vjp(dglu)
            dcv_ref[rows, :] = dcv.astype(BF16)
            dcg_ref[rows, :] = dcg.astype(BF16)
            return carry

        lax.fori_loop(0, s // ROW_TILE, tile2, 0)

    vec = pl.BlockSpec((1, CW), lambda i: (0, 0))
    wspec = pl.BlockSpec((CONV_K + 1, CW), lambda i: (0, 0))
    vshape = jax.ShapeDtypeStruct((1, CW), F32)
    return pl.pallas_call(
        body, name="conv_bwd", grid=(2,),
        in_specs=[pl.BlockSpec((s, CW), lambda i: (0, (2 * GW + PW) // CW)),
                  pl.BlockSpec((s, CW), lambda i: (0, (2 * GW + PW) // CW + 1)),
                  pl.BlockSpec((s, CW), lambda i: (0, (GW + PW) // CW)), wspec, vec, vec, vec, ANY],
        out_specs=[pl.BlockSpec((s, CW), lambda i: (0, (2 * GW + PW) // CW + i)), wspec, vec, vec, vec],
        out_shape=[jax.ShapeDtypeStruct((s, IN_COLS), BF16), jax.ShapeDtypeStruct((CONV_K + 1, CW), F32),
                   vshape, vshape, vshape],
        input_output_aliases={7: 0},
        scratch_shapes=[pltpu.VMEM((s + CONV_HALO, CW), F32), pltpu.VMEM((s + CONV_HALO, CW), F32),
                        pltpu.VMEM((s, CW), BF16)],
        compiler_params=_cparams(),
    )(z, z, dy, wdw, bdw, lng, lnb, dz)


def _softmax_rows(sc):
    e = jnp.exp(sc - jnp.max(sc, axis=-1, keepdims=True))
    return e / jnp.sum(e, axis=-1, keepdims=True)


def _attn_fwd(q, k, v, tq):
    s, m = q.shape[0], k.shape[0]
    tq = min(tq, s)

    def body(q_ref, k_ref, v_ref, o_ref):
        for h in range(XH):
            cols = slice(h * XHD, (h + 1) * XHD)
            p = _softmax_rows(_dot(q_ref[:, cols], k_ref[:, cols], NT) * ATT_SCALE)
            o_ref[:, cols] = _dot(p.astype(BF16), v_ref[:, cols], NN).astype(BF16)

    kv = pl.BlockSpec((m, D), lambda i: (0, 0))
    return pl.pallas_call(
        body, name="attn_fwd", grid=(s // tq,),
        in_specs=[pl.BlockSpec((tq, D), lambda i: (i, 0)), kv, kv],
        out_specs=pl.BlockSpec((tq, D), lambda i: (i, 0)),
        out_shape=jax.ShapeDtypeStruct((s, D), BF16), compiler_params=_cparams(),
    )(q, k, v)


def _attn_bwd(q, k, v, do, tq, after=()):
    s, m = q.shape[0], k.shape[0]
    tq = min(tq, s)

    def body(q_ref, k_ref, v_ref, do_ref, *rest):
        dq_ref, dk_ref, dv_ref = rest[len(after):]

        @pl.when(pl.program_id(0) == 0)
        def _():
            dk_ref[...] = jnp.zeros_like(dk_ref)
            dv_ref[...] = jnp.zeros_like(dv_ref)

        for h in range(XH):
            cols = slice(h * XHD, (h + 1) * XHD)
            qh, kh, vh, doh = q_ref[:, cols], k_ref[:, cols], v_ref[:, cols], do_ref[:, cols]
            p = _softmax_rows(_dot(qh, kh, NT) * ATT_SCALE)
            dp = _dot(doh, vh, NT)
            dv_ref[:, cols] += _dot(p.astype(BF16), doh, TN)
            ds = (p * (dp - jnp.sum(p * dp, axis=-1, keepdims=True)) * ATT_SCALE).astype(BF16)
            dq_ref[:, cols] = _dot(ds, kh, NN).astype(BF16)
            dk_ref[:, cols] += _dot(ds, qh, TN)

    kv = pl.BlockSpec((m, D), lambda i: (0, 0))
    qs = pl.BlockSpec((tq, D), lambda i: (i, 0))
    return pl.pallas_call(
        body, name="attn_bwd", grid=(s // tq,),
        in_specs=[qs, kv, kv, qs] + [ANY] * len(after), out_specs=[qs, kv, kv],
        out_shape=[jax.ShapeDtypeStruct((s, D), BF16), jax.ShapeDtypeStruct((m, D), F32),
                   jax.ShapeDtypeStruct((m, D), F32)],
        compiler_params=_cparams(),
    )(q, k, v, do, *after)


def _loss_head(y, target, tm):
    s = y.shape[0]
    tm = min(tm, s)

    def body(y_ref, t_ref, dy_ref, part_ref):
        err = y_ref[...] - t_ref[...]
        dy_ref[...] = err * (1.0 / D)
        part_ref[...] = jnp.full((1, 8, LANES), 0.5 * jnp.sum(err * err) * (1.0 / D), F32)

    blk = pl.BlockSpec((tm, D), lambda i: (i, 0))
    return pl.pallas_call(
        body, name="loss_head", grid=(s // tm,), in_specs=[blk, blk],
        out_specs=[blk, pl.BlockSpec((1, 8, LANES), lambda i: (i, 0, 0))],
        out_shape=[jax.ShapeDtypeStruct((s, D), F32), jax.ShapeDtypeStruct((s // tm, 8, LANES), F32)],
        compiler_params=_cparams(),
    )(y, target)


def _layer_fwd(x0, mem, w, p, fetch):
    z, hn0 = _rowop_mm("mix_in", "rms", (x0,), p["norm_mix_pre"], w["w_in"], NT, F32)
    y = _gmlp_fwd(z, p["gmlp_v_gain"], p["w_spatial"], p["b_spatial_t"], 512)
    y = _pool_fwd(z, p["w_pool"], p["s_pool"], y)
    y = _conv_fwd(z, p["w_dw"], p["b_dw"], p["conv_ln_g"], p["conv_ln_b"], y)
    w.update(fetch("out", (y,)))
    x1, h0 = _mm_rowop("mix_out", "rms_res", [(y, w["w_out"], NN)], (x0,), p["norm_mix_post"])
    w.update(fetch("att", (x1,)))
    q, hn1 = _rowop_mm("att_q", "rms", (x1,), p["norm_xattn_pre"], w["w_q"], NN, BF16)
    k, mn = _rowop_mm("att_k", "rms", (mem,), p["norm_mem"], w["w_k"], NN, BF16, after=(x1,))
    v, _ = _rowop_mm("att_v", "rms", (mem,), p["norm_mem"], w["w_v"], NN, BF16, after=(x1,))
    o = _attn_fwd(q, k, v, 512)
    x2, h1 = _mm_rowop("att_o", "rms_res", [(o, w["w_o"], NN)], (x1,), p["norm_xattn_post"])
    w.update(fetch("up", (x2,)))
    u, hn2 = _rowop_mm("ffn_up", "rms", (x2,), p["norm_ffn_pre"], w["w_up"], NT, F32)
    w.update(fetch("down", (u,)))
    x3, h2 = _mm_rowop("ffn_down", "rms_res", [(u, w["w_down"], NN)], (x2,), p["norm_ffn_post"], relu2=True)
    saved = dict(x0=x0, z=z, hn0=hn0, y=y, h0=h0, x1=x1, q=q, hn1=hn1, k=k, v=v, mn=mn, o=o, h1=h1, x2=x2, u=u,
                 hn2=hn2, h2=h2)
    return x3, saved


def _layer_bwd(dx3, mem, w, p, sv, red):
    gs = {}
    du, dh2, dg = _rowop_mm("ffn_down_bwd", "rms_bwd", (sv["h2"], dx3), p["norm_ffn_post"], w["w_down"], NT, BF16,
                            u=sv["u"], after=red.after())
    gs["norm_ffn_post"] = jnp.sum(dg, axis=0)
    g_down = _mm_tn("ffn_down_dw", sv["u"], dh2, relu2=True)
    red.advance((g_down,))
    dx2, dg = _mm_rowop("ffn_up_bwd", "rms_bwd_res", [(du, w["w_up"], NN)], (sv["x2"], dx3), p["norm_ffn_pre"],
                        after=red.after())
    gs["norm_ffn_pre"] = jnp.sum(dg, axis=0)
    g_up = _mm_tn("ffn_up_dw", du, sv["hn2"])
    red.add("ffn", ("w_down", "w_up"), [g_down, g_up])
    do, dh1, dg = _rowop_mm("att_o_bwd", "rms_bwd", (sv["h1"], dx2), p["norm_xattn_post"], w["w_o"], NT, BF16,
                            after=red.after())
    gs["norm_xattn_post"] = jnp.sum(dg, axis=0)
    g_o = _mm_tn("att_o_dw", sv["o"], dh1)
    red.advance((g_o,))
    dq, dk, dv = _attn_bwd(sv["q"], sv["k"], sv["v"], do, 512, after=red.after())
    dk, dv = dk.astype(BF16), dv.astype(BF16)
    dx1, dg = _mm_rowop("att_q_bwd", "rms_bwd_res", [(dq, w["w_q"], NT)], (sv["x1"], dx2), p["norm_xattn_pre"],
                        after=red.after())
    gs["norm_xattn_pre"] = jnp.sum(dg, axis=0)
    g_q = _mm_tn("att_q_dw", sv["hn1"], dq)
    g_k = _mm_tn("att_k_dw", sv["mn"], dk)
    g_v = _mm_tn("att_v_dw", sv["mn"], dv)
    (dg,) = _mm_rowop("att_kv_bwd", "rms_bwd_gain", [(dk, w["w_k"], NT), (dv, w["w_v"], NT)], (mem,), p["norm_mem"])
    gs["norm_mem"] = jnp.sum(dg, axis=0)
    red.add("att", ("w_o", "w_q", "w_k", "w_v"), [g_o, g_q, g_k, g_v])
    dy, dh0, dg = _rowop_mm("mix_out_bwd", "rms_bwd", (sv["h0"], dx1), p["norm_mix_post"], w["w_out"], NT, F32,
                            after=red.after())
    gs["norm_mix_post"] = jnp.sum(dg, axis=0)
    g_out = _mm_tn("mix_out_dw", sv["y"], dh0)
    red.advance((g_out,))
    red.add("out", ("w_out",), [g_out])
    z = sv["z"]
    dz, dgv, dws, dbs = _gmlp_bwd(z, dy, p["gmlp_v_gain"], p["w_spatial"], p["b_spatial_t"], 512, after=red.after())
    gs["gmlp_v_gain"] = jnp.sum(dgv, axis=0)
    gs["w_spatial"] = jnp.sum(dws, axis=0)
    gs["b_spatial"] = jnp.sum(dbs[..., 0], axis=0)
    dz, gs["w_pool"], gs["s_pool"] = _pool_bwd(z, dy, p["w_pool"], p["s_pool"], dz)
    dz, dwdw, gs["b_dw"], gs["conv_ln_g"], gs["conv_ln_b"] = _conv_bwd(
        z, dy, p["w_dw"], p["b_dw"], p["conv_ln_g"], p["conv_ln_b"], dz)
    red.advance((dz,))
    g_in = _mm_tn("mix_in_dw", dz, sv["hn0"], after=red.after())
    red.add("in", ("w_in",), [g_in])
    if red.layer == 0:
        red.advance(())
    red.small("mixer", _small_grad_arrays(gs, dwdw, norms=False))
    dx0, dg = _mm_rowop("mix_in_bwd", "rms_bwd_res", [(dz, w["w_in"], NN)], (sv["x0"], dx1), p["norm_mix_pre"],
                        after=red.after())
    gs["norm_mix_pre"] = jnp.sum(dg, axis=0)
    late = {"norms": jnp.concatenate([gs[n] for n in NORM_NAMES], axis=0)}
    if red.layer == 0:
        late["loss"] = red.extra[0]
    red.small("norms", late)
    return dx0


NORM_NAMES = ("norm_mix_pre", "norm_mix_post", "norm_xattn_pre", "norm_mem", "norm_xattn_post", "norm_ffn_pre",
              "norm_ffn_post")
VEC_NAMES = ("s_pool", "b_dw", "conv_ln_g", "conv_ln_b")
SMALL_ARRAYS = ("norms", "gain_bias", "w_spatial", "w_pool", "vecs", "w_dw")


def _small_grad_arrays(gs, dwdw, norms=True):
    out = {"norms": jnp.concatenate([gs[n] for n in NORM_NAMES], axis=0)} if norms else {}
    out.update({"gain_bias": jnp.concatenate([gs["gmlp_v_gain"], gs["b_spatial"]], axis=0),
                "w_spatial": gs["w_spatial"], "w_pool": gs["w_pool"],
                "vecs": jnp.concatenate([gs[n] for n in VEC_NAMES], axis=0), "w_dw": dwdw})
    return out


def _layer_params(small, l):
    p = {n: small[n][l].reshape(1, -1) for n in ("norm_mix_pre", "norm_mix_post", "s_pool", "b_dw", "conv_ln_g",
                                                   "conv_ln_b", "norm_xattn_pre", "norm_mem", "norm_xattn_post",
                                                   "norm_ffn_pre", "norm_ffn_post")}
    p["gmlp_v_gain"] = small["gmlp_v_gain"][l]
    p["w_spatial"] = small["w_spatial"][l]
    p["b_spatial_t"] = small["b_spatial"][l].T
    p["w_pool"] = small["w_pool"][l]
    p["w_dw"] = jnp.pad(small["w_dw"][l], ((0, 1), (0, 0)))
    return p


def _local_step(x, mem, target, fetch, small, red):
    small = dict(small)
    saved, weights, params = [], [], []
    h = x
    marker = ()
    for l in range(DEPTH):
        w = fetch(l, "in", marker)
        if "taps" in w:
            small["w_dw"] = w.pop("taps")
        p = _layer_params(small, l)
        h, sv = _layer_fwd(h, mem, w, p, functools.partial(fetch, l))
        marker = (h,)
        saved.append(sv)
        weights.append(w)
        params.append(p)
    dh, loss = _loss_head(h, target, 512)
    red.extra = (loss,)
    for l in reversed(range(DEPTH)):
        red.layer = l
        dh = _layer_bwd(dh, mem, weights[l], params[l], saved[l], red)
    return loss, dh


HBM = pl.BlockSpec(memory_space=pltpu.HBM)


def _position():
    return lax.axis_index("x"), lax.axis_index("y"), lax.axis_index("c")


SEM = pl.BlockSpec(memory_space=pltpu.SEMAPHORE)
EFFECT = pltpu.SideEffectType.DATAFLOW_SIDE_EFFECTING
TOKEN = jax.ShapeDtypeStruct((8, LANES), F32)
TOKEN_SPEC = pl.BlockSpec(memory_space=pltpu.VMEM)


def _landing(shape, dtype):
    return pltpu.with_memory_space_constraint(lax.empty(shape, dtype), pltpu.HBM)


def _hbm_shapes(arrays):
    return [pltpu.HBM(a.shape, a.dtype) for a in arrays]


def _block(ref, r, dev):
    return ref.at[pl.ds((4 * dev[0] + 2 * dev[1] + dev[2]) * r, r), :]


def _split_call(name, body, thru, sems_in, after, sems_out, token):
    n = len(thru)
    out_shape = [pltpu.SemaphoreType.DMA(s) for s in sems_out] + _hbm_shapes(thru) + ([TOKEN] if token else [])
    out_specs = [SEM] * len(sems_out) + [HBM] * n + ([TOKEN_SPEC] if token else [])
    return pl.pallas_call(
        body, name=name, in_specs=[HBM] * n + [SEM] * len(sems_in) + [ANY] * len(after),
        out_specs=out_specs, out_shape=out_shape,
        input_output_aliases={i: len(sems_out) + i for i in range(n)},
        compiler_params=pltpu.CompilerParams(has_side_effects=EFFECT),
    )(*thru, *sems_in, *after)


def _place_own(name, srcs, dev, out_dtype, tr):
    n = len(srcs)
    r, cols = srcs[0][0].shape[-2:]
    tr = r if r < 16 else _row_tile(r, tr)
    nb = r // tr

    def body(dev_ref, *refs):
        for a in range(n):
            refs[n + a][...] = refs[a][...].astype(out_dtype)

    in_specs = [pl.BlockSpec((tr, cols), lambda i, d: (i, 0)) if l is None
                else pl.BlockSpec((None, tr, cols), lambda i, d, l=l: (l, i, 0)) for _, l in srcs]
    return pl.pallas_call(
        body, name=name,
        grid_spec=pltpu.PrefetchScalarGridSpec(
            num_scalar_prefetch=1, grid=(nb,), in_specs=in_specs,
            out_specs=[pl.BlockSpec((tr, cols), lambda i, d: (d[0] * nb + i, 0))] * n),
        out_shape=[jax.ShapeDtypeStruct((N_DEV * r, cols), out_dtype)] * n, compiler_params=_cparams(),
    )(dev, *[a for a, _ in srcs])


def _gather_peers(x, y, c):
    return [(1 - x, y, c), (x, 1 - y, c), (1 - x, 1 - y, c), (x, y, 1 - c)]


def _block_rows(land):
    return land.shape[0] // N_DEV


def _near_peers(x, y, c):
    return [(1 - x, y, c), (x, 1 - y, c), (x, y, 1 - c)]


def _relay_route(x, y, c):
    origin = (x + c * (1 - 2 * x), y + (1 - c) * (1 - 2 * y), c)
    target = (x + (1 - c) * (1 - 2 * x), y + c * (1 - 2 * y), c)
    return origin, target


def _same_block_copy(blk, send_sem, recv_sem, to):
    return pltpu.make_async_remote_copy(src_ref=blk, dst_ref=blk, send_sem=send_sem, recv_sem=recv_sem, device_id=to,
                                        device_id_type=MESH)


def _gather_start(name, lands, after):
    n = len(lands)

    def body(*refs):
        lz = refs[:n]
        send_sems, recv_sems = refs[n + len(after)], refs[n + len(after) + 1]
        token = refs[-1]
        x, y, c = _position()
        for a in range(n):
            own = _block(lz[a], _block_rows(lands[a]), (x, y, c))
            for k, to in enumerate(_near_peers(x, y, c)):
                _same_block_copy(own, send_sems.at[k], recv_sems.at[k], to).start()
        token[...] = jnp.zeros_like(token)

    out = _split_call(name, body, list(lands), [], after, [(3,), (3,)], True)
    return out[0], out[1], out[2:2 + n], out[-1]


def _gather_step(name, near, far, fresh, after):
    groups = [g for g in (near and near[0], far and far[0], fresh) if g]
    counts = [len(near[0]) if near else 0, len(far[0]) if far else 0, len(fresh) if fresh else 0]
    n = sum(counts)
    sems_in = ([near[1]] if near else []) + ([far[1]] if far else [])
    sems_out = ([(2,), (2,), (1,), (1,)] if near else []) + ([(1,), (1,)] if far else []) + ([(3,), (3,)] if fresh else [])

    def body(*refs):
        lz = list(refs[:n])
        ins = list(refs[n:n + len(sems_in)])
        outs = list(refs[n + len(sems_in) + len(after):n + len(sems_in) + len(after) + len(sems_out)])
        token = refs[-1]
        x, y, c = _position()
        me, sibling = (x, y, c), (x, y, 1 - c)
        near_lz, far_lz, fresh_lz = (lz[sum(counts[:i]):sum(counts[:i + 1])] for i in range(3))
        neighbours = _near_peers(x, y, c)[:2]
        origin, target = _relay_route(x, y, c)
        diagonal = (1 - x, 1 - y, c)
        if near:
            recv0 = ins.pop(0)
            fsend, frecv, rsend, rrecv = (outs.pop(0) for _ in range(4))
            for a, land in enumerate(near[0]):
                for j, chip in enumerate(neighbours):
                    _same_block_copy(_block(near_lz[a], _block_rows(land), chip), fsend.at[j], recv0.at[j], me).wait_recv()
        if far:
            rrecv_in = ins.pop(0)
            f2send, f2recv = outs.pop(0), outs.pop(0)
            for a, land in enumerate(far[0]):
                _same_block_copy(_block(far_lz[a], _block_rows(land), diagonal), f2send.at[0], rrecv_in.at[0], me).wait_recv()
            for a, land in enumerate(far[0]):
                _same_block_copy(_block(far_lz[a], _block_rows(land), diagonal), f2send.at[0], f2recv.at[0], sibling).start()
        if near:
            for a, land in enumerate(near[0]):
                r = _block_rows(land)
                _same_block_copy(_block(near_lz[a], r, origin), rsend.at[0], rrecv.at[0], target).start()
                for j, chip in enumerate(neighbours):
                    _same_block_copy(_block(near_lz[a], r, chip), fsend.at[j], frecv.at[j], sibling).start()
        if fresh:
            send_sems, recv_sems = outs.pop(0), outs.pop(0)
            for a, land in enumerate(fresh):
                own = _block(fresh_lz[a], _block_rows(land), me)
                for k, to in enumerate(_near_peers(x, y, c)):
                    _same_block_copy(own, send_sems.at[k], recv_sems.at[k], to).start()
        token[...] = jnp.zeros_like(token)

    out = list(_split_call(name, body, [l for g in groups for l in g], sems_in, after, sems_out, True))
    res = {"token": out.pop()}
    if near:
        res.update(fsend=out.pop(0), frecv=out.pop(0), rsend=out.pop(0), rrecv=out.pop(0))
    if far:
        res.update(f2send=out.pop(0), f2recv=out.pop(0))
    if fresh:
        res.update(send=out.pop(0), recv=out.pop(0))
    res["near"], res["far"], res["fresh"] = (out[sum(counts[:i]):sum(counts[:i + 1])] for i in range(3))
    return res


def _gather_finish(name, lands, send_sems, recv_sems, fsend, frecv, rsend, f2send, f2recv, after):
    n = len(lands)

    def body(*refs):
        lz = refs[:n]
        send0, recv0, fsend_ref, frecv_ref, rsend_ref, f2send_ref, f2recv_ref = refs[n:n + 7]
        x, y, c = _position()
        me = (x, y, c)
        near = _near_peers(x, y, c)[:2]
        origin, _ = _relay_route(x, y, c)
        for a in range(n):
            r = _block_rows(lands[a])
            sib = _block(lz[a], r, (x, y, 1 - c))
            _same_block_copy(sib, send0.at[2], recv0.at[2], me).wait_recv()
            for j, chip in enumerate(near):
                blk = _block(lz[a], r, (chip[0], chip[1], 1 - c))
                _same_block_copy(blk, fsend_ref.at[j], frecv_ref.at[j], me).wait_recv()
            far = _block(lz[a], r, (1 - x, 1 - y, 1 - c))
            _same_block_copy(far, f2send_ref.at[0], f2recv_ref.at[0], me).wait_recv()
            own = _block(lz[a], r, me)
            for k in range(3):
                _same_block_copy(own, send0.at[k], recv0.at[k], me).wait_send()
            for j, chip in enumerate(near):
                _same_block_copy(_block(lz[a], r, chip), fsend_ref.at[j], frecv_ref.at[j], me).wait_send()
            _same_block_copy(_block(lz[a], r, origin), rsend_ref.at[0], recv0.at[0], me).wait_send()
            _same_block_copy(_block(lz[a], r, (1 - x, 1 - y, c)), f2send_ref.at[0], f2recv_ref.at[0], me).wait_send()

    return _split_call(name, body, list(lands), [send_sems, recv_sems, fsend, frecv, rsend, f2send, f2recv], after, [],
                       False)


def _sibling_start(name, grads, after):
    n = len(grads)
    lands = [_landing((4, g.shape[0] // N_DEV, D), g.dtype) for g in grads]

    def body(*refs):
        ins, lz = refs[:n], refs[n:2 * n]
        send_sem, recv_sem = refs[2 * n + len(after)], refs[2 * n + len(after) + 1]
        token = refs[-1]
        x, y, c = _position()
        for a in range(n):
            r = grads[a].shape[0] // N_DEV
            for q in range(4):
                pltpu.make_async_remote_copy(
                    src_ref=ins[a].at[pl.ds((2 * q + 1 - c) * r, r), :], dst_ref=lz[a].at[q], send_sem=send_sem.at[0],
                    recv_sem=recv_sem.at[0], device_id=(x, y, 1 - c), device_id_type=MESH).start()
        token[...] = jnp.zeros_like(token)

    out = _split_call(name, body, list(grads) + lands, [], after, [(1,), (1,)], True)
    return out[0], out[1], out[2:2 + n], out[2 + n:2 + 2 * n], out[-1]


def _sibling_finish(name, grads, lands, send_sem, recv_sem, after):
    n = len(grads)

    def body(*refs):
        ins, lz = refs[:n], refs[n:2 * n]
        send_ref, recv_ref = refs[2 * n], refs[2 * n + 1]
        x, y, c = _position()
        for a in range(n):
            r = grads[a].shape[0] // N_DEV
            for q in range(4):
                cp = pltpu.make_async_remote_copy(
                    src_ref=ins[a].at[pl.ds((2 * q + 1 - c) * r, r), :], dst_ref=lz[a].at[q], send_sem=send_ref.at[0],
                    recv_sem=recv_ref.at[0], device_id=(x, y, c), device_id_type=MESH)
                cp.wait_send()
                cp.wait_recv()

    out = _split_call(name, body, list(grads) + list(lands), [send_sem, recv_sem], after, [], False)
    return out[:n], out[n:2 * n]


def _chip_start(name, parts, after):
    n = len(parts)
    lands = [_landing((3,) + p.shape[1:], p.dtype) for p in parts]

    def body(*refs):
        ins, lz = refs[:n], refs[n:2 * n]
        send_sems, recv_sems = refs[2 * n + len(after)], refs[2 * n + len(after) + 1]
        token = refs[-1]
        x, y, c = _position()
        for a in range(n):
            for j, chip in enumerate(_gather_peers(x, y, c)[:3]):
                pltpu.make_async_remote_copy(
                    src_ref=ins[a].at[2 * chip[0] + chip[1]], dst_ref=lz[a].at[j], send_sem=send_sems.at[j],
                    recv_sem=recv_sems.at[j], device_id=chip, device_id_type=MESH).start()
        token[...] = jnp.zeros_like(token)

    out = _split_call(name, body, list(parts) + lands, [], after, [(3,), (3,)], True)
    return out[0], out[1], out[2:2 + n], out[2 + n:2 + 2 * n], out[-1]


def _chip_finish(name, parts, lands, send_sems, recv_sems, after):
    n = len(parts)

    def body(*refs):
        ins, lz = refs[:n], refs[n:2 * n]
        send_ref, recv_ref = refs[2 * n], refs[2 * n + 1]
        me = _position()
        for a in range(n):
            for j in range(3):
                cp = pltpu.make_async_remote_copy(
                    src_ref=ins[a].at[j], dst_ref=lz[a].at[j], send_sem=send_ref.at[j], recv_sem=recv_ref.at[j],
                    device_id=me, device_id_type=MESH)
                cp.wait_send()
                cp.wait_recv()

    out = _split_call(name, body, list(parts) + list(lands), [send_sems, recv_sems], after, [], False)
    return out[:n], out[n:2 * n]


def _other_devices(x, y, c):
    return [(x + (k >> 2 & 1) * (1 - 2 * x), y + (k >> 1 & 1) * (1 - 2 * y), c + (k & 1) * (1 - 2 * c))
            for k in range(1, N_DEV)]


def _broadcast_start(name, arrays, after):
    n = len(arrays)
    lands = [_landing((N_DEV,) + a.shape, a.dtype) for a in arrays]

    def body(*refs):
        ins, lz = refs[:n], refs[n:2 * n]
        send_sems, recv_sems = refs[2 * n + len(after)], refs[2 * n + len(after) + 1]
        token = refs[-1]
        x, y, c = _position()
        for a in range(n):
            for k, peer in enumerate(_other_devices(x, y, c)):
                pltpu.make_async_remote_copy(
                    src_ref=ins[a], dst_ref=lz[a].at[4 * x + 2 * y + c], send_sem=send_sems.at[k],
                    recv_sem=recv_sems.at[k], device_id=peer, device_id_type=MESH).start()
        token[...] = jnp.zeros_like(token)

    out = _split_call(name, body, list(arrays) + lands, [], after, [(N_DEV - 1,), (N_DEV - 1,)], True)
    return out[0], out[1], out[2:2 + n], out[2 + n:2 + 2 * n], out[-1]


def _broadcast_finish(name, arrays, lands, send_sems, recv_sems, after):
    n = len(arrays)

    def body(*refs):
        ins, lz = refs[:n], refs[n:2 * n]
        send_ref, recv_ref = refs[2 * n], refs[2 * n + 1]
        x, y, c = _position()
        for a in range(n):
            for k, peer in enumerate(_other_devices(x, y, c)):
                cp = pltpu.make_async_remote_copy(
                    src_ref=ins[a], dst_ref=lz[a].at[4 * peer[0] + 2 * peer[1] + peer[2]], send_sem=send_ref.at[k],
                    recv_sem=recv_ref.at[k], device_id=(x, y, c), device_id_type=MESH)
                cp.wait_send()
                cp.wait_recv()

    out = _split_call(name, body, list(arrays) + list(lands), [send_sems, recv_sems], after, [], False)
    return out[:n], out[n:2 * n]


def _row_tile(r, target):
    return max(t for t in range(16, min(r, target) + 1, 16) if r % t == 0)


CHIP_PARTIAL_BYTES = 12 * 1024 * 1024


def _chip_partial(name, grads, gots, c):
    n = len(grads)
    r = grads[0].shape[0] // N_DEV
    tr = _row_tile(r, CHIP_PARTIAL_BYTES // (n * 3 * D * 2))

    def body(c_ref, *refs):
        for a in range(n):
            refs[2 * n + a][...] = (refs[a][...].astype(F32) + refs[n + a][...].astype(F32)).astype(BF16)

    blk = pl.BlockSpec((None, tr, D), lambda q, i, c_ref: (q, i, 0))
    return pl.pallas_call(
        body, name=name,
        grid_spec=pltpu.PrefetchScalarGridSpec(
            num_scalar_prefetch=1, grid=(4, r // tr),
            in_specs=[pl.BlockSpec((None, None, tr, D), lambda q, i, c_ref: (q, c_ref[0], i, 0))] * n + [blk] * n,
            out_specs=[blk] * n),
        out_shape=[jax.ShapeDtypeStruct((4, r, D), BF16)] * n, compiler_params=_cparams(),
    )(c, *[g.reshape(4, 2, r, D) for g in grads], *gots)


class _WeightGather:
    def __init__(self, groups):
        self.groups = list(groups)
        self.index = {key: i for i, (key, _, _) in enumerate(groups)}
        self.state = [None] * len(groups)
        self.token = ()
        for i in range(min(2, len(groups))):
            self._start(i)

    def _tag(self, i):
        return "%s_%d" % self.groups[i][0][::-1]

    def _start(self, i):
        send, recv, lz, tok = _gather_start("gather_start_" + self._tag(i), self.groups[i][2], self.token)
        self.state[i] = dict(send=send, recv=recv, lands=lz)
        self.token = (tok,)

    def _step(self, name, near, far, fresh, marker):
        exists = lambda i: i is not None and i < len(self.groups)
        near, far, fresh = (i if exists(i) else None for i in (near, far, fresh))
        res = _gather_step(
            name, None if near is None else (self.state[near]["lands"], self.state[near]["recv"]),
            None if far is None else (self.state[far]["lands"], self.state[far]["rrecv"]),
            None if fresh is None else self.groups[fresh][2], tuple(marker) + self.token)
        self.token = (res["token"],)
        if near is not None:
            self.state[near].update(lands=res["near"], fsend=res["fsend"], frecv=res["frecv"], rsend=res["rsend"],
                                    rrecv=res["rrecv"])
        if far is not None:
            self.state[far].update(lands=res["far"], f2send=res["f2send"], f2recv=res["f2recv"])
        if fresh is not None:
            self.state[fresh] = dict(send=res["send"], recv=res["recv"], lands=res["fresh"])

    def fetch(self, layer, group, marker):
        k = self.index[(layer, group)]
        if k == 0:
            self._step("gather_step_first", 0, None, None, marker)
        self._step("gather_step_" + self._tag(k), k + 1, k, k + 2, marker)
        st = self.state[k]
        lz = _gather_finish("gather_finish_" + self._tag(k), st["lands"], st["send"], st["recv"], st["fsend"],
                            st["frecv"], st["rsend"], st["f2send"], st["f2recv"], self.token)
        self.state[k] = None
        return dict(zip(self.groups[k][1], lz))


class _GradReduce:
    def __init__(self, core, chip):
        self.core, self.chip = core, chip
        self.layer = None
        self.token = ()
        self.at_sibling, self.at_chips = [], []
        self.extra, self.smalls = (), {}

    def after(self):
        return self.token

    def add(self, group, names, grads):
        tag = "%s_%d" % (group, self.layer)
        send, recv, grads, lands, tok = _sibling_start("grad_sibling_start_" + tag, grads, self.token)
        self.at_sibling.append((tag, [(self.layer, n) for n in names], send, recv, grads, lands))
        self.token = (tok,)

    def advance(self, marker):
        for tag, keys, send, recv, grads, lands in self.at_sibling:
            grads, lands = _sibling_finish("grad_sibling_finish_" + tag, grads, lands, send, recv, marker)
            parts = _chip_partial("chip_partial_" + tag, grads, lands, self.core)
            send, recv, parts, lands, tok = _chip_start("grad_chip_start_" + tag, parts, ())
            self.at_chips.append([tag, keys, send, recv, parts, lands])
            self.token = (tok,)
        self.at_sibling = []

    def small(self, part, arrays):
        keys = list(arrays)
        send, recv, own, slots, tok = _broadcast_start(
            "small_grads_start_%d_%s" % (self.layer, part), [arrays[k] for k in keys], self.token)
        self.smalls.setdefault(self.layer, []).append((part, keys, send, recv, own, slots))
        self.token = (tok,)

    def small_finish(self, layer, marker):
        mine, theirs = {}, {}
        for part, keys, send, recv, own, slots in self.smalls[layer]:
            own, slots = _broadcast_finish("small_grads_finish_%d_%s" % (layer, part), own, slots, send, recv, marker)
            mine.update(zip(keys, own))
            theirs.update(zip(keys, slots))
        return mine, theirs

    def collect(self, key, marker):
        for entry in self.at_chips:
            tag, keys, send, recv, parts, lands = entry
            if key in keys:
                if send is not None:
                    parts, lands = _chip_finish("grad_chip_finish_" + tag, parts, lands, send, recv, marker)
                    entry[2:] = [None, None, parts, lands]
                i = keys.index(key)
                return parts[i], lands[i]
        raise KeyError(key)


def _adamw_math(w, g, m, v):
    m = ADAM_B1 * m + (1.0 - ADAM_B1) * g
    v = ADAM_B2 * v + (1.0 - ADAM_B2) * jnp.square(g)
    m_hat = m / (1.0 - ADAM_B1 ** ADAM_STEP)
    v_hat = v / (1.0 - ADAM_B2 ** ADAM_STEP)
    delta = -ADAM_LR * (m_hat / (jnp.sqrt(v_hat) + ADAM_EPS) + ADAM_WD * w)
    return delta, m, v


def _adamw_small(wts, mom_m, mom_v, own, gathered, loss_own, loss_gathered, dev):
    names = SMALL
    nw = len(names)
    na = len(SMALL_ARRAYS)

    def body(dev_ref, *refs):
        w_refs, m_refs, v_refs = (dict(zip(names, refs[i * nw:(i + 1) * nw])) for i in range(3))
        own_refs = refs[3 * nw:3 * nw + DEPTH * na]
        g_refs = refs[3 * nw + DEPTH * na:3 * nw + 2 * DEPTH * na]
        loss_own_ref, loss_got_ref = refs[3 * nw + 2 * DEPTH * na:3 * nw + 2 * DEPTH * na + 2]
        outs = refs[3 * nw + 2 * DEPTH * na + 2:]
        g_out, d_out, m_out, v_out = (dict(zip(names, outs[i * nw:(i + 1) * nw])) for i in range(4))
        me = dev_ref[0]

        loss = None
        for d in range(N_DEV):
            for b in range(loss_own.shape[0]):
                term = jnp.where(me == d, loss_own_ref[b], loss_got_ref[d, b])
                loss = term if loss is None else loss + term
        outs[4 * nw][...] = loss

        def update(name, at, g):
            g_out[name][at] = g
            d_out[name][at], m_out[name][at], v_out[name][at] = _adamw_math(
                w_refs[name][at], g, m_refs[name][at], v_refs[name][at])

        for l in range(DEPTH):
            mine = dict(zip(SMALL_ARRAYS, own_refs[l * na:(l + 1) * na]))
            got = dict(zip(SMALL_ARRAYS, g_refs[l * na:(l + 1) * na]))

            def total(key, at):
                acc = None
                for d in range(N_DEV):
                    term = jnp.where(me == d, mine[key][at] if at else mine[key][...], got[key][(d,) + at])
                    acc = term if acc is None else acc + term
                return acc

            row = (slice(l, l + 1),)
            for k, name in enumerate(NORM_NAMES):
                update(name, row, total("norms", (slice(k, k + 1),)))
            for k, name in enumerate(VEC_NAMES):
                update(name, row, total("vecs", (slice(k, k + 1),)))
            update("gmlp_v_gain", (l,), total("gain_bias", (slice(0, NH),)))
            update("b_spatial", (l,), total("gain_bias", (slice(NH, 2 * NH),)))
            update("w_spatial", (l,), total("w_spatial", ()))
            update("w_pool", (l,), total("w_pool", ()))
            update("w_dw", (l,), total("w_dw", (slice(0, CONV_K),)))

    args = [src[n] for src in (wts, mom_m, mom_v) for n in names]
    args += [src[l][k] for src in (own, gathered) for l in range(DEPTH) for k in SMALL_ARRAYS]
    args += [loss_own, loss_gathered]
    outs = pl.pallas_call(
        body, name="adamw_small",
        in_specs=[pl.BlockSpec(memory_space=pltpu.SMEM)] + [pl.BlockSpec(memory_space=pltpu.VMEM)] * len(args),
        out_shape=[jax.ShapeDtypeStruct(wts[n].shape, F32) for _ in range(4) for n in names]
        + [jax.ShapeDtypeStruct((8, LANES), F32)],
        compiler_params=_cparams(),
    )(dev, *args)
    return tuple(dict(zip(names, outs[i * nw:(i + 1) * nw])) for i in range(4)) + (outs[4 * nw],)


def _adamw_layers(name, w, reduced, m, v, chip, tr, transposed=False, after=()):
    nl, r, cdim = w.shape
    tr = _row_tile(r, tr)
    nb = r // tr

    def body(q_ref, w_ref, p0_ref, g0_ref, p1_ref, g1_ref, m_ref, v_ref, *rest):
        g_ref, d_ref, nm_ref, nv_ref = rest[len(after):]

        def total(p_ref, got_ref):
            acc = p_ref[...].astype(F32)
            for j in range(3):
                acc = acc + got_ref[j].astype(F32)
            return acc

        g = jnp.where(pl.program_id(0) == 0, total(p0_ref, g0_ref), total(p1_ref, g1_ref))
        if transposed:
            g = g.T
        g_ref[...] = g
        d_ref[...], nm_ref[...], nv_ref[...] = _adamw_math(w_ref[...], g, m_ref[...], v_ref[...])

    blk = pl.BlockSpec((None, tr, cdim), lambda l, i, q: (l, i, 0))
    first = lambda l, i: i * (1 - l) + (nb - 1) * l
    second = lambda l, i: i * l
    if transposed:
        gshape = (cdim, tr)
        at = lambda lead, i: (lead, 0, i)
    else:
        gshape = (tr, cdim)
        at = lambda lead, i: (lead, i, 0)
    specs = [blk,
             pl.BlockSpec((None,) + gshape, lambda l, i, q: at(q[0], first(l, i))),
             pl.BlockSpec((3,) + gshape, lambda l, i, q: at(0, first(l, i))),
             pl.BlockSpec((None,) + gshape, lambda l, i, q: at(q[0], second(l, i))),
             pl.BlockSpec((3,) + gshape, lambda l, i, q: at(0, second(l, i))), blk, blk] + [ANY] * len(after)
    shape = jax.ShapeDtypeStruct((nl, r, cdim), F32)
    return pl.pallas_call(
        body, name=name,
        grid_spec=pltpu.PrefetchScalarGridSpec(num_scalar_prefetch=1, grid=(nl, nb), in_specs=specs, out_specs=[blk] * 4),
        out_shape=[shape] * 4, compiler_params=_cparams(),
    )(chip, w, *reduced[0], *reduced[1], m, v, *after)


def _to_rows(name, a):
    return jnp.swapaxes(a, 1, 2) if name == "w_in" else a


def _place_own_transposed(name, srcs, dev, out_dtype, tc):
    n = len(srcs)
    kdim, cdim = srcs[0][0].shape[-2:]

    def body(dev_ref, *refs):
        for a in range(n):
            refs[n + a][...] = refs[a][...].T.astype(out_dtype)

    return pl.pallas_call(
        body, name=name,
        grid_spec=pltpu.PrefetchScalarGridSpec(
            num_scalar_prefetch=1, grid=(kdim // tc,),
            in_specs=[pl.BlockSpec((None, tc, cdim), lambda i, d, l=l: (l, i, 0)) for _, l in srcs],
            out_specs=[pl.BlockSpec((cdim, tc), lambda i, d: (d[0], i))] * n),
        out_shape=[jax.ShapeDtypeStruct((N_DEV * cdim, kdim), out_dtype)] * n, compiler_params=_cparams(),
    )(dev, *[a for a, _ in srcs])


def _pack(arrays, rows):
    flat = jnp.concatenate([a.reshape(-1) for a in arrays])
    return jnp.pad(flat, (0, rows * D - flat.shape[0])).reshape(rows, D)


def _rows_for(shapes, mult=8):
    total = 0
    for shp in shapes:
        size = 1
        for dim in shp:
            size *= dim
        total += size
    return -(-total // (mult * D)) * mult


def kernel(x, mem, norm_mix_pre, norm_mix_post, w_in, w_out, gmlp_v_gain, w_spatial, b_spatial, w_pool, s_pool, w_dw, b_dw, conv_ln_g, conv_ln_b, norm_xattn_pre, norm_mem, norm_xattn_post, w_q, w_k, w_v, w_o, norm_ffn_pre, norm_ffn_post, w_up, w_down, loss_target, m_norm_mix_pre, m_norm_mix_post, m_w_in, m_w_out, m_gmlp_v_gain, m_w_spatial, m_b_spatial, m_w_pool, m_s_pool, m_w_dw, m_b_dw, m_conv_ln_g, m_conv_ln_b, m_norm_xattn_pre, m_norm_mem, m_norm_xattn_post, m_w_q, m_w_k, m_w_v, m_w_o, m_norm_ffn_pre, m_norm_ffn_post, m_w_up, m_w_down, v_norm_mix_pre, v_norm_mix_post, v_w_in, v_w_out, v_gmlp_v_gain, v_w_spatial, v_b_spatial, v_w_pool, v_s_pool, v_w_dw, v_b_dw, v_conv_ln_g, v_conv_ln_b, v_norm_xattn_pre, v_norm_mem, v_norm_xattn_post, v_w_q, v_w_k, v_w_v, v_w_o, v_norm_ffn_pre, v_norm_ffn_post, v_w_up, v_w_down):
    args = dict(locals())
    wts = {n: args[n] for n in WEIGHTS}
    mom_m = {n: args["m_" + n] for n in WEIGHTS}
    mom_v = {n: args["v_" + n] for n in WEIGHTS}
    xi, yi, ci = _position()
    me = 4 * xi + 2 * yi + ci

    dev = jnp.reshape(me, (1,)).astype(jnp.int32)
    lands = {}
    for call, names, tr in (("place_in", ("w_in",), 256), ("place_att", ("w_out", "w_q", "w_k", "w_v", "w_o"), 64),
                            ("place_up", ("w_up",), 256), ("place_down", ("w_down",), 256)):
        srcs = [(_to_rows(n, wts[n]), l) for l in range(DEPTH) for n in names]
        placed = (_place_own_transposed if names == ("w_up",) else _place_own)(call, srcs, dev, BF16, tr)
        lands.update(zip([(l, n) for l in range(DEPTH) for n in names], placed))
    (lands[(0, "taps")],) = _place_own("place_taps", [(_pack([w_dw], _rows_for([w_dw.shape])), None)], dev, F32, 8)
    groups = []
    for l in range(DEPTH):
        for group, names in GATHER_GROUPS:
            if (l, group) == (0, "in"):
                names = names + ("taps",)
            groups.append(((l, group), names, [lands[(l, n)] for n in names]))
    gather = _WeightGather(groups)

    def fetch(layer, group, marker):
        w = gather.fetch(layer, group, marker)
        if "taps" in w:
            blocks = w["taps"].reshape(N_DEV, -1)[:, :w_dw.size].reshape((N_DEV,) + w_dw.shape)
            w["taps"] = jnp.moveaxis(blocks, 0, 2).reshape(DEPTH, CONV_K, CW)
        return w

    reduce = _GradReduce(jnp.reshape(ci, (1,)).astype(jnp.int32), jnp.reshape(2 * xi + yi, (1,)).astype(jnp.int32))
    small = {n: wts[n] for n in SMALL if n != "w_dw"}
    _, dx = _local_step(x[0], mem[0], loss_target[0], fetch, small, reduce)
    reduce.advance((dx,))

    grad_w, delta, new_m, new_v = {}, {}, {}, {}
    marker = (dx,) + tuple(reduce.after())
    for n in UPDATE_ORDER:
        reduced = [reduce.collect((l, n), marker) for l in range(DEPTH)]
        outs = _adamw_layers("adamw_" + n, _to_rows(n, wts[n]), reduced, _to_rows(n, mom_m[n]), _to_rows(n, mom_v[n]),
                             reduce.chip, 256, transposed=n == "w_up", after=marker)
        grad_w[n], delta[n], new_m[n], new_v[n] = (_to_rows(n, o) for o in outs)
        marker = (outs[1],)

    own, slots = [None] * DEPTH, [None] * DEPTH
    for l in reversed(range(DEPTH)):
        own[l], slots[l] = reduce.small_finish(l, marker)
        if l == 0:
            loss_own, loss_slots = own[l].pop("loss"), slots[l].pop("loss")
    shard_cols = CW // N_DEV
    for l in range(DEPTH):
        own[l]["w_dw"] = lax.dynamic_slice_in_dim(own[l]["w_dw"], me * shard_cols, shard_cols, axis=1)
        slots[l]["w_dw"] = lax.dynamic_slice_in_dim(slots[l]["w_dw"], me * shard_cols, shard_cols, axis=2)
    *small_out, loss_tile = _adamw_small(wts, mom_m, mom_v, own, slots, loss_own, loss_slots, dev)
    for dst, src in zip((grad_w, delta, new_m, new_v), small_out):
        dst.update(src)

    return (loss_tile[0, 0], dx[None], *[grad_w[n] for n in WEIGHTS], *[delta[n] for n in WEIGHTS],
            *[new_m[n] for n in WEIGHTS], *[new_v[n] for n in WEIGHTS])
```

```python
import functools

import jax
import jax.numpy as jnp
from jax import lax
from jax.experimental import pallas as pl
from jax.experimental.pallas import tpu as pltpu

F32 = jnp.float32
BF16 = jnp.bfloat16

D = 2048
GW = 1024
PW = 512
CW = 512
HD = 128
NH = 8
NG = 4
POOL_WINDOWS = (2, 4, 8, 16)
CONV_K = 31
IN_COLS = 2 * GW + PW + 2 * CW
XH = 4
XHD = D // XH
ATT_SCALE = XHD ** -0.5
RMS_EPS = 1e-6
LN_EPS = 1e-5
DEPTH = 2
N_DEV = 8

ADAM_LR = 0.001
ADAM_B1 = 0.9
ADAM_B2 = 0.999
ADAM_EPS = 1e-08
ADAM_WD = 0.01
ADAM_STEP = 10

LANES = 128
CONV_HALO = 32
POOL_HALO = 16
ROW_TILE = 128
VMEM_LIMIT = 60 * 1024 * 1024

MESH = pl.DeviceIdType.MESH
NT = (((1,), (1,)), ((), ()))
NN = (((1,), (0,)), ((), ()))
TN = (((0,), (0,)), ((), ()))

UPDATE_ORDER = ("w_down", "w_up", "w_o", "w_q", "w_k", "w_v", "w_out", "w_in")
GATHER_GROUPS = (("in", ("w_in",)), ("out", ("w_out",)), ("att", ("w_q", "w_k", "w_v", "w_o")), ("up", ("w_up",)),
                 ("down", ("w_down",)))
SMALL = ("norm_mix_pre", "norm_mix_post", "gmlp_v_gain", "w_spatial", "b_spatial", "w_pool", "s_pool",
         "w_dw", "b_dw", "conv_ln_g", "conv_ln_b", "norm_xattn_pre", "norm_mem", "norm_xattn_post",
         "norm_ffn_pre", "norm_ffn_post")
WEIGHTS = ("norm_mix_pre", "norm_mix_post", "w_in", "w_out", "gmlp_v_gain", "w_spatial", "b_spatial", "w_pool",
           "s_pool", "w_dw", "b_dw", "conv_ln_g", "conv_ln_b", "norm_xattn_pre", "norm_mem", "norm_xattn_post",
           "w_q", "w_k", "w_v", "w_o", "norm_ffn_pre", "norm_ffn_post", "w_up", "w_down")


def _cparams():
    return pltpu.CompilerParams(vmem_limit_bytes=VMEM_LIMIT)


def _dot(a, b, dims):
    return lax.dot_general(a, b, dims, preferred_element_type=F32)


def _rms(x, g):
    y = x * lax.rsqrt(jnp.mean(x * x, axis=-1, keepdims=True) + RMS_EPS)
    return y * g


def _rms_bwd(x, g, dy):
    r = lax.rsqrt(jnp.mean(x * x, axis=-1, keepdims=True) + RMS_EPS)
    xh = x * r
    t = dy * g
    dx = r * (t - xh * jnp.mean(t * xh, axis=-1, keepdims=True))
    return dx, jnp.sum(dy * xh, axis=0, keepdims=True)


def _gelu(x):
    cdf = 0.5 * (1.0 + jnp.tanh(0.7978845608028654 * (x + 0.044715 * (x * x * x))))
    return x * cdf


def _layer_norm(x, g, b=None):
    mu = jnp.mean(x, axis=-1, keepdims=True)
    xc = x - mu
    var = jnp.mean(xc * xc, axis=-1, keepdims=True)
    y = xc * lax.rsqrt(var + LN_EPS) * g
    return y if b is None else y + b


def _sigmoid(x):
    return 1.0 / (1.0 + jnp.exp(-x))


def _gmlp_rows(zu, zv, gv):
    return _gelu(zu), _layer_norm(_gelu(zv), gv)


def _glu(cv, cg):
    return cv * _sigmoid(cg)


def _ln_silu(h, g, b):
    y = _layer_norm(h, g, b)
    return y * _sigmoid(y)


ANY = pl.BlockSpec(memory_space=pl.ANY)


ROWS_TILE = 256
COLS_TILE = 512
DW_TILE = 512
RESIDENT_K = 2048
STREAM_K_TILE = 1024
STREAM_ROWS = 512


def _k_tiles(kdim):
    if kdim <= RESIDENT_K:
        return ROWS_TILE, kdim
    return STREAM_ROWS, max(t for t in range(LANES, STREAM_K_TILE + 1, LANES) if kdim % t == 0)


def _rowop_mm(name, kind, rows, g, w, dims, out_dtype, u=None, after=()):
    s = rows[0].shape[0]
    n = w.shape[0] if dims == NT else w.shape[1]
    tm, tn = min(ROWS_TILE, s), min(COLS_TILE, n)
    ni, nj = s // tm, n // tn
    bwd = kind == "rms_bwd"

    def body(*refs):
        refs = list(refs)
        row_refs = [refs.pop(0) for _ in rows]
        g_ref, w_ref = refs.pop(0), refs.pop(0)
        u_ref = refs.pop(0) if u is not None else None
        del refs[:len(after)]
        out_ref, a_ref = refs.pop(0), refs.pop(0)
        dg_ref = refs.pop(0) if bwd else None
        a_all = refs.pop(0)
        t = pl.program_id(0)

        @pl.when(t < ni)
        def _():
            if bwd:
                a, dg = _rms_bwd(row_refs[0][...], g_ref[...], row_refs[1][...])
                dg_ref[0] = dg
            else:
                a = _rms(row_refs[0][...], g_ref[...])
            a_ref[...] = a.astype(BF16)
            a_all[pl.ds(pl.multiple_of(t * tm, tm), tm), :] = a.astype(BF16)

        @pl.when(t >= ni)
        def _():
            acc = _dot(a_all[...], w_ref[...], dims)
            if u_ref is not None:
                acc = acc * (2.0 * jnp.maximum(u_ref[...], 0.0))
            out_ref[...] = acc.astype(out_dtype)

    rows_at = lambda t: jnp.minimum(t, ni - 1)
    cols_at = lambda t: jnp.maximum(t - ni, 0)
    row_spec = pl.BlockSpec((tm, D), lambda t: (rows_at(t), 0))
    w_spec = (pl.BlockSpec((tn, D), lambda t: (cols_at(t), 0)) if dims == NT
              else pl.BlockSpec((D, tn), lambda t: (0, cols_at(t))))
    tile = pl.BlockSpec((s, tn), lambda t: (0, cols_at(t)))
    in_specs = [row_spec] * len(rows) + [pl.BlockSpec((1, D), lambda t: (0, 0)), w_spec]
    in_specs += ([tile] if u is not None else []) + [ANY] * len(after)
    out_shape = [jax.ShapeDtypeStruct((s, n), out_dtype), jax.ShapeDtypeStruct((s, D), BF16)]
    out_specs = [tile, row_spec]
    if bwd:
        out_shape.append(jax.ShapeDtypeStruct((ni, 1, D), F32))
        out_specs.append(pl.BlockSpec((1, 1, D), lambda t: (rows_at(t), 0, 0)))
    return pl.pallas_call(
        body, name=name, grid=(ni + nj,), in_specs=in_specs, out_specs=out_specs, out_shape=out_shape,
        scratch_shapes=[pltpu.VMEM((s, D), BF16)], compiler_params=_cparams(),
    )(*rows, g, w, *([u] if u is not None else []), *after)


def _mm_rowop(name, kind, pairs, rows, g, relu2=False, after=()):
    s, kdim = pairs[0][0].shape
    tm, tk = _k_tiles(kdim)
    tm = min(tm, s)
    ni, nk = s // tm, kdim // tk
    npair = len(pairs)

    def body(*refs):
        refs = list(refs)
        a_refs = [refs.pop(0) for _ in range(npair)]
        w_refs = [refs.pop(0) for _ in range(npair)]
        row_refs = [refs.pop(0) for _ in rows]
        g_ref = refs.pop(0)
        del refs[:len(after)]
        acc = refs.pop()
        outs = refs
        k = pl.program_id(1)

        @pl.when(k == 0)
        def _():
            acc[...] = jnp.zeros_like(acc)

        for a_ref, w_ref, (_, _, dims) in zip(a_refs, w_refs, pairs):
            a = a_ref[...]
            if relu2:
                a = jnp.square(jnp.maximum(a, 0.0))
            acc[...] += _dot(a.astype(BF16), w_ref[...], dims)

        @pl.when(k == nk - 1)
        def _():
            h = acc[...]
            if kind == "rms_res":
                outs[0][...] = row_refs[0][...] + _rms(h, g_ref[...])
                outs[1][...] = h
            else:
                dx, dg = _rms_bwd(row_refs[0][...], g_ref[...], h)
                if kind == "rms_bwd_res":
                    outs[0][...] = row_refs[1][...] + dx
                    outs[1][0] = dg
                else:
                    outs[0][0] = dg

    row_spec = pl.BlockSpec((tm, D), lambda i, k: (i, 0))
    dg_shape = jax.ShapeDtypeStruct((ni, 1, D), F32)
    dg_spec = pl.BlockSpec((1, 1, D), lambda i, k: (i, 0, 0))
    in_specs = [pl.BlockSpec((tm, tk), lambda i, k: (i, k))] * npair
    for _, _, dims in pairs:
        in_specs.append(pl.BlockSpec((tk, D), lambda i, k: (k, 0)) if dims == NN
                        else pl.BlockSpec((D, tk), lambda i, k: (0, k)))
    in_specs += [row_spec] * len(rows) + [pl.BlockSpec((1, D), lambda i, k: (0, 0))] + [ANY] * len(after)
    if kind == "rms_res":
        out_shape = [jax.ShapeDtypeStruct((s, D), F32)] * 2
        out_specs = [row_spec, row_spec]
    elif kind == "rms_bwd_res":
        out_shape = [jax.ShapeDtypeStruct((s, D), F32), dg_shape]
        out_specs = [row_spec, dg_spec]
    else:
        out_shape = [dg_shape]
        out_specs = [dg_spec]
    return pl.pallas_call(
        body, name=name, grid=(ni, nk), in_specs=in_specs, out_specs=out_specs, out_shape=out_shape,
        scratch_shapes=[pltpu.VMEM((tm, D), F32)], compiler_params=_cparams(),
    )(*[p[0] for p in pairs], *[p[1] for p in pairs], *rows, g, *after)


def _mm_tn(name, a, gmat, relu2=False, after=()):
    s, m = a.shape
    tm = min(DW_TILE, m)
    ni = m // tm

    def body(a_ref, g_ref, *rest):
        av = a_ref[...]
        if relu2:
            av = jnp.square(jnp.maximum(av, 0.0))
        rest[len(after)][...] = _dot(av.astype(BF16), g_ref[...], TN).astype(BF16)

    return pl.pallas_call(
        body, name=name, grid=(ni,),
        in_specs=[pl.BlockSpec((s, tm), lambda i: (0, i)), pl.BlockSpec((s, D), lambda i: (0, 0))] + [ANY] * len(after),
        out_specs=pl.BlockSpec((tm, D), lambda i: (i, 0)),
        out_shape=jax.ShapeDtypeStruct((m, D), BF16), compiler_params=_cparams(),
    )(a, gmat, *after)


def _tril():
    r = lax.broadcasted_iota(jnp.int32, (HD, HD), 0)
    c = lax.broadcasted_iota(jnp.int32, (HD, HD), 1)
    return (c <= r).astype(F32)


def _gmlp_fwd(z, gv, ws, bst, tb):
    s = z.shape[0]
    tb = min(tb, s)

    def body(zu_ref, zv_ref, gv_ref, ws_ref, bst_ref, y_ref):
        tril = _tril()
        for h in range(NH):
            cols = slice(h * HD, (h + 1) * HD)
            u, vln = _gmlp_rows(zu_ref[:, cols], zv_ref[:, cols], gv_ref[h:h + 1, :])
            wm = (ws_ref[h] * tril).astype(BF16)
            vb = vln.astype(BF16)
            for c in range(tb // HD):
                rws = slice(c * HD, (c + 1) * HD)
                mixed = _dot(wm, vb[rws], NN) + bst_ref[:, h:h + 1]
                y_ref[rws, cols] = (u[rws] * mixed).astype(BF16)

    return pl.pallas_call(
        body, name="gmlp_fwd", grid=(s // tb,),
        in_specs=[pl.BlockSpec((tb, GW), lambda i: (i, 0)), pl.BlockSpec((tb, GW), lambda i: (i, 1)),
                  pl.BlockSpec((NH, HD), lambda i: (0, 0)), pl.BlockSpec((NH, HD, HD), lambda i: (0, 0, 0)),
                  pl.BlockSpec((HD, NH), lambda i: (0, 0))],
        out_specs=pl.BlockSpec((tb, GW), lambda i: (i, 0)),
        out_shape=jax.ShapeDtypeStruct((s, D), BF16), compiler_params=_cparams(),
    )(z, z, gv, ws, bst)


def _gmlp_bwd(z, dy, gv, ws, bst, tb, after=()):
    s = z.shape[0]
    tb = min(tb, s)
    nb = s // tb

    def body(zu_ref, zv_ref, dy_ref, gv_ref, ws_ref, bst_ref, *rest):
        dz_ref, dgv_ref, dws_ref, db_ref = rest[len(after):]
        tril = _tril()
        for h in range(NH):
            cols = slice(h * HD, (h + 1) * HD)
            (u, vln), vjp = jax.vjp(_gmlp_rows, zu_ref[:, cols], zv_ref[:, cols], gv_ref[h:h + 1, :])
            wmf = ws_ref[h] * tril
            wm = wmf.astype(BF16)
            wmt = wmf.T.astype(BF16)
            vb = vln.astype(BF16)
            dws = jnp.zeros((HD, HD), F32)
            db = jnp.zeros((HD, 1), F32)
            du_parts, dvln_parts = [], []
            for c in range(tb // HD):
                rws = slice(c * HD, (c + 1) * HD)
                mixed = _dot(wm, vb[rws], NN) + bst_ref[:, h:h + 1]
                dyc = dy_ref[rws, cols]
                du_parts.append(dyc * mixed)
                dmixed = dyc * u[rws]
                dmb = dmixed.astype(BF16)
                dws = dws + _dot(dmb, vb[rws], NT)
                db = db + jnp.sum(dmixed, axis=1, keepdims=True)
                dvln_parts.append(_dot(wmt, dmb, NN))
            du = jnp.concatenate(du_parts, axis=0)
            dvln = jnp.concatenate(dvln_parts, axis=0)
            dzu, dzv, dgv = vjp((du, dvln))
            dz_ref[:, cols] = dzu.astype(BF16)
            dz_ref[:, slice(GW + h * HD, GW + (h + 1) * HD)] = dzv.astype(BF16)
            dgv_ref[0, h:h + 1, :] = dgv
            dws_ref[0, h] = dws * tril
            db_ref[0, h] = jnp.broadcast_to(db, (HD, LANES))

    blk = pl.BlockSpec((tb, GW), lambda i: (i, 0))
    return pl.pallas_call(
        body, name="gmlp_bwd", grid=(nb,),
        in_specs=[blk, pl.BlockSpec((tb, GW), lambda i: (i, 1)), blk,
                  pl.BlockSpec((NH, HD), lambda i: (0, 0)), pl.BlockSpec((NH, HD, HD), lambda i: (0, 0, 0)),
                  pl.BlockSpec((HD, NH), lambda i: (0, 0))] + [ANY] * len(after),
        out_specs=[pl.BlockSpec((tb, 2 * GW), lambda i: (i, 0)), pl.BlockSpec((1, NH, HD), lambda i: (i, 0, 0)),
                   pl.BlockSpec((1, NH, HD, HD), lambda i: (i, 0, 0, 0)),
                   pl.BlockSpec((1, NH, HD, LANES), lambda i: (i, 0, 0, 0))],
        out_shape=[jax.ShapeDtypeStruct((s, IN_COLS), BF16),
                   jax.ShapeDtypeStruct((nb, NH, HD), F32), jax.ShapeDtypeStruct((nb, NH, HD, HD), F32),
                   jax.ShapeDtypeStruct((nb, NH, HD, LANES), F32)],
        compiler_params=_cparams(),
    )(z, z, dy, gv, ws, bst, *after)


POOL_TILE = 256


def _pool_count(t0, window):
    pos = (t0 + lax.broadcasted_iota(jnp.int32, (POOL_TILE, LANES), 0)).astype(F32)
    return jnp.minimum(pos + 1.0, float(window))


def _window_sum(win, levels, back):
    n = win.shape[0]
    for lv in range(levels):
        step = 1 << lv
        win = win + pltpu.roll(win, n - step if back else step, 0)
    return win


def _pool_pooled(ppad_ref, t0, g):
    win = ppad_ref[pl.ds(t0, POOL_TILE + POOL_HALO), :]
    wsum = _window_sum(win, g + 1, False)[POOL_HALO:]
    return wsum / _pool_count(t0, POOL_WINDOWS[g]) - win[POOL_HALO:]


def _pool_fwd(z, wp, sp, y):
    s = z.shape[0]
    nt = s // POOL_TILE

    def body(p_ref, wp_ref, sp_ref, _, y_ref, ppad):
        for g in range(NG):
            cols = slice(g * LANES, (g + 1) * LANES)
            ppad[pl.ds(0, POOL_HALO), :] = jnp.zeros((POOL_HALO, LANES), F32)
            ppad[pl.ds(POOL_HALO, s), :] = p_ref[:, cols]
            wpb = wp_ref[g].astype(BF16)
            scale = sp_ref[:, cols]

            def tile(t, carry):
                t0 = pl.multiple_of(t * POOL_TILE, POOL_TILE)
                pooled = _pool_pooled(ppad, t0, g)
                y_ref[pl.ds(t0, POOL_TILE), cols] = (_dot(pooled.astype(BF16), wpb, NN) * scale).astype(BF16)
                return carry

            lax.fori_loop(0, nt, tile, 0)

    return pl.pallas_call(
        body, name="pool_fwd", grid=(1,),
        in_specs=[pl.BlockSpec((s, PW), lambda i: (0, 2 * GW // PW)),
                  pl.BlockSpec((NG, LANES, LANES), lambda i: (0, 0, 0)), pl.BlockSpec((1, PW), lambda i: (0, 0)), ANY],
        out_specs=pl.BlockSpec((s, PW), lambda i: (0, GW // PW)),
        out_shape=jax.ShapeDtypeStruct((s, D), BF16), input_output_aliases={3: 0},
        scratch_shapes=[pltpu.VMEM((s + POOL_HALO, LANES), F32)], compiler_params=_cparams(),
    )(z, wp, sp, y)


def _pool_bwd(z, dy, wp, sp, dz):
    s = z.shape[0]
    nt = s // POOL_TILE

    def body(p_ref, dy_ref, wp_ref, sp_ref, _, dp_ref, dwp_ref, dsp_ref, ppad, rpad, dpool):
        for g in range(NG):
            cols = slice(g * LANES, (g + 1) * LANES)
            ppad[pl.ds(0, POOL_HALO), :] = jnp.zeros((POOL_HALO, LANES), F32)
            ppad[pl.ds(POOL_HALO, s), :] = p_ref[:, cols]
            rpad[pl.ds(s, POOL_HALO), :] = jnp.zeros((POOL_HALO, LANES), F32)
            wpb = wp_ref[g].astype(BF16)
            scale = sp_ref[:, cols]

            def tile(t, carry):
                dwp, dsp = carry
                t0 = pl.multiple_of(t * POOL_TILE, POOL_TILE)
                pooled = _pool_pooled(ppad, t0, g)
                pb = pooled.astype(BF16)
                dyt = dy_ref[pl.ds(t0, POOL_TILE), cols]
                dsp = dsp + jnp.sum(dyt * _dot(pb, wpb, NN), axis=0, keepdims=True)
                dmm = (dyt * scale).astype(BF16)
                dwp = dwp + _dot(pb, dmm, TN)
                dpooled = _dot(dmm, wpb, NT)
                rpad[pl.ds(t0, POOL_TILE), :] = dpooled / _pool_count(t0, POOL_WINDOWS[g])
                dpool[pl.ds(t0, POOL_TILE), :] = dpooled
                return dwp, dsp

            dwp, dsp = lax.fori_loop(0, nt, tile, (jnp.zeros((LANES, LANES), F32), jnp.zeros((1, LANES), F32)))
            dwp_ref[g] = dwp
            dsp_ref[:, cols] = dsp

            def tile2(t, carry):
                t0 = pl.multiple_of(t * POOL_TILE, POOL_TILE)
                win = rpad[pl.ds(t0, POOL_TILE + POOL_HALO), :]
                back = _window_sum(win, g + 1, True)[:POOL_TILE]
                rows = pl.ds(t0, POOL_TILE)
                dp_ref[rows, cols] = (back - dpool[rows, :]).astype(BF16)
                return carry

            lax.fori_loop(0, nt, tile2, 0)

    return pl.pallas_call(
        body, name="pool_bwd", grid=(1,),
        in_specs=[pl.BlockSpec((s, PW), lambda i: (0, 2 * GW // PW)), pl.BlockSpec((s, PW), lambda i: (0, GW // PW)),
                  pl.BlockSpec((NG, LANES, LANES), lambda i: (0, 0, 0)), pl.BlockSpec((1, PW), lambda i: (0, 0)), ANY],
        out_specs=[pl.BlockSpec((s, PW), lambda i: (0, 2 * GW // PW)),
                   pl.BlockSpec((NG, LANES, LANES), lambda i: (0, 0, 0)), pl.BlockSpec((1, PW), lambda i: (0, 0))],
        out_shape=[jax.ShapeDtypeStruct((s, IN_COLS), BF16), jax.ShapeDtypeStruct((NG, LANES, LANES), F32),
                   jax.ShapeDtypeStruct((1, PW), F32)],
        input_output_aliases={4: 0},
        scratch_shapes=[pltpu.VMEM((s + POOL_HALO, LANES), F32), pltpu.VMEM((s + POOL_HALO, LANES), F32),
                        pltpu.VMEM((s, LANES), F32)],
        compiler_params=_cparams(),
    )(z, dy, wp, sp, dz)


CONV_LEAD = CONV_HALO - (CONV_K - 1)


SUBLANES = 8


def _sublane_shifts(win):
    n = win.shape[0]
    return [win] + [pltpu.roll(win, n - b, 0) for b in range(1, SUBLANES)]


def _shifted(shifts, offset):
    a, b = divmod(offset, SUBLANES)
    return shifts[b][a * SUBLANES:a * SUBLANES + ROW_TILE]


def _conv_taps(shifts, wdw_ref, lead, reverse):
    acc = jnp.zeros((ROW_TILE, CW), F32)
    for j in range(CONV_K):
        tap = (CONV_K - 1 - j) if reverse else j
        acc = acc + wdw_ref[tap:tap + 1, :] * _shifted(shifts, lead + j)
    return acc


def _conv_fill_glu(cv_ref, cg_ref, xpad, s):
    xpad[pl.ds(0, CONV_HALO), :] = jnp.zeros((CONV_HALO, CW), F32)

    def fill(t, carry):
        t0 = pl.multiple_of(t * ROW_TILE, ROW_TILE)
        rows = pl.ds(t0, ROW_TILE)
        xpad[pl.ds(t0 + CONV_HALO, ROW_TILE), :] = _glu(cv_ref[rows, :], cg_ref[rows, :])
        return carry

    lax.fori_loop(0, s // ROW_TILE, fill, 0)


def _conv_fwd(z, wdw, bdw, lng, lnb, y):
    s = z.shape[0]

    def body(cv_ref, cg_ref, wdw_ref, bdw_ref, lng_ref, lnb_ref, _, y_ref, xpad):
        _conv_fill_glu(cv_ref, cg_ref, xpad, s)

        def tile(t, carry):
            t0 = pl.multiple_of(t * ROW_TILE, ROW_TILE)
            shifts = _sublane_shifts(xpad[pl.ds(t0, ROW_TILE + CONV_HALO), :])
            hc = _conv_taps(shifts, wdw_ref, CONV_LEAD, False) + bdw_ref[...]
            y_ref[pl.ds(t0, ROW_TILE), :] = _ln_silu(hc, lng_ref[...], lnb_ref[...]).astype(BF16)
            return carry

        lax.fori_loop(0, s // ROW_TILE, tile, 0)

    vec = pl.BlockSpec((1, CW), lambda i: (0, 0))
    return pl.pallas_call(
        body, name="conv_fwd", grid=(1,),
        in_specs=[pl.BlockSpec((s, CW), lambda i: (0, (2 * GW + PW) // CW)),
                  pl.BlockSpec((s, CW), lambda i: (0, (2 * GW + PW) // CW + 1)),
                  pl.BlockSpec((CONV_K + 1, CW), lambda i: (0, 0)), vec, vec, vec, ANY],
        out_specs=pl.BlockSpec((s, CW), lambda i: (0, (GW + PW) // CW)),
        out_shape=jax.ShapeDtypeStruct((s, D), BF16), input_output_aliases={6: 0},
        scratch_shapes=[pltpu.VMEM((s + CONV_HALO, CW), F32)], compiler_params=_cparams(),
    )(z, z, wdw, bdw, lng, lnb, y)


def _conv_bwd(z, dy, wdw, bdw, lng, lnb, dz):
    s = z.shape[0]

    def body(cv_ref, cg_ref, dy_ref, wdw_ref, bdw_ref, lng_ref, lnb_ref, _,
             dz_ref, dwdw_ref, dbdw_ref, dlng_ref, dlnb_ref, xpad, dpad, dcg_keep):
        @pl.when(pl.program_id(0) == 0)
        def _():
            compute(cv_ref, cg_ref, dy_ref, wdw_ref, bdw_ref, lng_ref, lnb_ref,
                    dz_ref, dcg_keep, dwdw_ref, dbdw_ref, dlng_ref, dlnb_ref, xpad, dpad)

        @pl.when(pl.program_id(0) == 1)
        def _():
            dz_ref[...] = dcg_keep[...]

    def compute(cv_ref, cg_ref, dy_ref, wdw_ref, bdw_ref, lng_ref, lnb_ref,
                dcv_ref, dcg_ref, dwdw_ref, dbdw_ref, dlng_ref, dlnb_ref, xpad, dpad):
        _conv_fill_glu(cv_ref, cg_ref, xpad, s)
        dpad[pl.ds(s, CONV_HALO), :] = jnp.zeros((CONV_HALO, CW), F32)
        dwdw_ref[...] = jnp.zeros((CONV_K + 1, CW), F32)

        def tile(t, carry):
            db, dg, dbeta = carry
            t0 = pl.multiple_of(t * ROW_TILE, ROW_TILE)
            shifts = _sublane_shifts(xpad[pl.ds(t0, ROW_TILE + CONV_HALO), :])
            hc = _conv_taps(shifts, wdw_ref, CONV_LEAD, False) + bdw_ref[...]
            _, vjp = jax.vjp(_ln_silu, hc, lng_ref[...], lnb_ref[...])
            dhc, dg_t, dbeta_t = vjp(dy_ref[pl.ds(t0, ROW_TILE), :])
            dpad[pl.ds(t0, ROW_TILE), :] = dhc
            for j in range(CONV_K):
                dwdw_ref[j:j + 1, :] += jnp.sum(dhc * _shifted(shifts, CONV_LEAD + j), axis=0, keepdims=True)
            return db + jnp.sum(dhc, axis=0, keepdims=True), dg + dg_t, dbeta + dbeta_t

        zero = jnp.zeros((1, CW), F32)
        db, dg, dbeta = lax.fori_loop(0, s // ROW_TILE, tile, (zero, zero, zero))
        dbdw_ref[...] = db
        dlng_ref[...] = dg
        dlnb_ref[...] = dbeta

        def tile2(t, carry):
            t0 = pl.multiple_of(t * ROW_TILE, ROW_TILE)
            rows = pl.ds(t0, ROW_TILE)
            dglu = _conv_taps(_sublane_shifts(dpad[pl.ds(t0, ROW_TILE + CONV_HALO), :]), wdw_ref, 0, True)
            _, vjp = jax.vjp(_glu, cv_ref[rows, :], cg_ref[rows, :])
            dcv, dcg = vjp(dglu)
            dcv_ref[rows, :] = dcv.astype(BF16)
            dcg_ref[rows, :] = dcg.astype(BF16)
            return carry

        lax.fori_loop(0, s // ROW_TILE, tile2, 0)

    vec = pl.BlockSpec((1, CW), lambda i: (0, 0))
    wspec = pl.BlockSpec((CONV_K + 1, CW), lambda i: (0, 0))
    vshape = jax.ShapeDtypeStruct((1, CW), F32)
    return pl.pallas_call(
        body, name="conv_bwd", grid=(2,),
        in_specs=[pl.BlockSpec((s, CW), lambda i: (0, (2 * GW + PW) // CW)),
                  pl.BlockSpec((s, CW), lambda i: (0, (2 * GW + PW) // CW + 1)),
                  pl.BlockSpec((s, CW), lambda i: (0, (GW + PW) // CW)), wspec, vec, vec, vec, ANY],
        out_specs=[pl.BlockSpec((s, CW), lambda i: (0, (2 * GW + PW) // CW + i)), wspec, vec, vec, vec],
        out_shape=[jax.ShapeDtypeStruct((s, IN_COLS), BF16), jax.ShapeDtypeStruct((CONV_K + 1, CW), F32),
                   vshape, vshape, vshape],
        input_output_aliases={7: 0},
        scratch_shapes=[pltpu.VMEM((s + CONV_HALO, CW), F32), pltpu.VMEM((s + CONV_HALO, CW), F32),
                        pltpu.VMEM((s, CW), BF16)],
        compiler_params=_cparams(),
    )(z, z, dy, wdw, bdw, lng, lnb, dz)


def _softmax_rows(sc):
    e = jnp.exp(sc - jnp.max(sc, axis=-1, keepdims=True))
    return e / jnp.sum(e, axis=-1, keepdims=True)


def _attn_fwd(q, k, v, tq):
    s, m = q.shape[0], k.shape[0]
    tq = min(tq, s)

    def body(q_ref, k_ref, v_ref, o_ref):
        for h in range(XH):
            cols = slice(h * XHD, (h + 1) * XHD)
            p = _softmax_rows(_dot(q_ref[:, cols], k_ref[:, cols], NT) * ATT_SCALE)
            o_ref[:, cols] = _dot(p.astype(BF16), v_ref[:, cols], NN).astype(BF16)

    kv = pl.BlockSpec((m, D), lambda i: (0, 0))
    return pl.pallas_call(
        body, name="attn_fwd", grid=(s // tq,),
        in_specs=[pl.BlockSpec((tq, D), lambda i: (i, 0)), kv, kv],
        out_specs=pl.BlockSpec((tq, D), lambda i: (i, 0)),
        out_shape=jax.ShapeDtypeStruct((s, D), BF16), compiler_params=_cparams(),
    )(q, k, v)


def _attn_bwd(q, k, v, do, tq, after=()):
    s, m = q.shape[0], k.shape[0]
    tq = min(tq, s)

    def body(q_ref, k_ref, v_ref, do_ref, *rest):
        dq_ref, dk_ref, dv_ref = rest[len(after):]

        @pl.when(pl.program_id(0) == 0)
        def _():
            dk_ref[...] = jnp.zeros_like(dk_ref)
            dv_ref[...] = jnp.zeros_like(dv_ref)

        for h in range(XH):
            cols = slice(h * XHD, (h + 1) * XHD)
            qh, kh, vh, doh = q_ref[:, cols], k_ref[:, cols], v_ref[:, cols], do_ref[:, cols]
            p = _softmax_rows(_dot(qh, kh, NT) * ATT_SCALE)
            dp = _dot(doh, vh, NT)
            dv_ref[:, cols] += _dot(p.astype(BF16), doh, TN)
            ds = (p * (dp - jnp.sum(p * dp, axis=-1, keepdims=True)) * ATT_SCALE).astype(BF16)
            dq_ref[:, cols] = _dot(ds, kh, NN).astype(BF16)
            dk_ref[:, cols] += _dot(ds, qh, TN)

    kv = pl.BlockSpec((m, D), lambda i: (0, 0))
    qs = pl.BlockSpec((tq, D), lambda i: (i, 0))
    return pl.pallas_call(
        body, name="attn_bwd", grid=(s // tq,),
        in_specs=[qs, kv, kv, qs] + [ANY] * len(after), out_specs=[qs, kv, kv],
        out_shape=[jax.ShapeDtypeStruct((s, D), BF16), jax.ShapeDtypeStruct((m, D), F32),
                   jax.ShapeDtypeStruct((m, D), F32)],
        compiler_params=_cparams(),
    )(q, k, v, do, *after)


def _loss_head(y, target, tm):
    s = y.shape[0]
    tm = min(tm, s)

    def body(y_ref, t_ref, dy_ref, part_ref):
        err = y_ref[...] - t_ref[...]
        dy_ref[...] = err * (1.0 / D)
        part_ref[...] = jnp.full((1, 8, LANES), 0.5 * jnp.sum(err * err) * (1.0 / D), F32)

    blk = pl.BlockSpec((tm, D), lambda i: (i, 0))
    return pl.pallas_call(
        body, name="loss_head", grid=(s // tm,), in_specs=[blk, blk],
        out_specs=[blk, pl.BlockSpec((1, 8, LANES), lambda i: (i, 0, 0))],
        out_shape=[jax.ShapeDtypeStruct((s, D), F32), jax.ShapeDtypeStruct((s // tm, 8, LANES), F32)],
        compiler_params=_cparams(),
    )(y, target)


def _layer_fwd(x0, mem, w, p, fetch):
    z, hn0 = _rowop_mm("mix_in", "rms", (x0,), p["norm_mix_pre"], w["w_in"], NT, F32)
    y = _gmlp_fwd(z, p["gmlp_v_gain"], p["w_spatial"], p["b_spatial_t"], 512)
    y = _pool_fwd(z, p["w_pool"], p["s_pool"], y)
    y = _conv_fwd(z, p["w_dw"], p["b_dw"], p["conv_ln_g"], p["conv_ln_b"], y)
    w.update(fetch("out", (y,)))
    x1, h0 = _mm_rowop("mix_out", "rms_res", [(y, w["w_out"], NN)], (x0,), p["norm_mix_post"])
    w.update(fetch("att", (x1,)))
    q, hn1 = _rowop_mm("att_q", "rms", (x1,), p["norm_xattn_pre"], w["w_q"], NN, BF16)
    k, mn = _rowop_mm("att_k", "rms", (mem,), p["norm_mem"], w["w_k"], NN, BF16, after=(x1,))
    v, _ = _rowop_mm("att_v", "rms", (mem,), p["norm_mem"], w["w_v"], NN, BF16, after=(x1,))
    o = _attn_fwd(q, k, v, 512)
    x2, h1 = _mm_rowop("att_o", "rms_res", [(o, w["w_o"], NN)], (x1,), p["norm_xattn_post"])
    w.update(fetch("up", (x2,)))
    u, hn2 = _rowop_mm("ffn_up", "rms", (x2,), p["norm_ffn_pre"], w["w_up"], NT, F32)
    w.update(fetch("down", (u,)))
    x3, h2 = _mm_rowop("ffn_down", "rms_res", [(u, w["w_down"], NN)], (x2,), p["norm_ffn_post"], relu2=True)
    saved = dict(x0=x0, z=z, hn0=hn0, y=y, h0=h0, x1=x1, q=q, hn1=hn1, k=k, v=v, mn=mn, o=o, h1=h1, x2=x2, u=u,
                 hn2=hn2, h2=h2)
    return x3, saved


def _layer_bwd(dx3, mem, w, p, sv, red):
    gs = {}
    du, dh2, dg = _rowop_mm("ffn_down_bwd", "rms_bwd", (sv["h2"], dx3), p["norm_ffn_post"], w["w_down"], NT, BF16,
                            u=sv["u"], after=red.after())
    gs["norm_ffn_post"] = jnp.sum(dg, axis=0)
    g_down = _mm_tn("ffn_down_dw", sv["u"], dh2, relu2=True)
    red.advance((g_down,))
    dx2, dg = _mm_rowop("ffn_up_bwd", "rms_bwd_res", [(du, w["w_up"], NN)], (sv["x2"], dx3), p["norm_ffn_pre"],
                        after=red.after())
    gs["norm_ffn_pre"] = jnp.sum(dg, axis=0)
    g_up = _mm_tn("ffn_up_dw", du, sv["hn2"])
    red.add("ffn", ("w_down", "w_up"), [g_down, g_up])
    do, dh1, dg = _rowop_mm("att_o_bwd", "rms_bwd", (sv["h1"], dx2), p["norm_xattn_post"], w["w_o"], NT, BF16,
                            after=red.after())
    gs["norm_xattn_post"] = jnp.sum(dg, axis=0)
    g_o = _mm_tn("att_o_dw", sv["o"], dh1)
    red.advance((g_o,))
    dq, dk, dv = _attn_bwd(sv["q"], sv["k"], sv["v"], do, 512, after=red.after())
    dk, dv = dk.astype(BF16), dv.astype(BF16)
    dx1, dg = _mm_rowop("att_q_bwd", "rms_bwd_res", [(dq, w["w_q"], NT)], (sv["x1"], dx2), p["norm_xattn_pre"],
                        after=red.after())
    gs["norm_xattn_pre"] = jnp.sum(dg, axis=0)
    g_q = _mm_tn("att_q_dw", sv["hn1"], dq)
    g_k = _mm_tn("att_k_dw", sv["mn"], dk)
    g_v = _mm_tn("att_v_dw", sv["mn"], dv)
    (dg,) = _mm_rowop("att_kv_bwd", "rms_bwd_gain", [(dk, w["w_k"], NT), (dv, w["w_v"], NT)], (mem,), p["norm_mem"])
    gs["norm_mem"] = jnp.sum(dg, axis=0)
    red.add("att", ("w_o", "w_q", "w_k", "w_v"), [g_o, g_q, g_k, g_v])
    dy, dh0, dg = _rowop_mm("mix_out_bwd", "rms_bwd", (sv["h0"], dx1), p["norm_mix_post"], w["w_out"], NT, F32,
                            after=red.after())
    gs["norm_mix_post"] = jnp.sum(dg, axis=0)
    g_out = _mm_tn("mix_out_dw", sv["y"], dh0)
    red.advance((g_out,))
    red.add("out", ("w_out",), [g_out])
    z = sv["z"]
    dz, dgv, dws, dbs = _gmlp_bwd(z, dy, p["gmlp_v_gain"], p["w_spatial"], p["b_spatial_t"], 512, after=red.after())
    gs["gmlp_v_gain"] = jnp.sum(dgv, axis=0)
    gs["w_spatial"] = jnp.sum(dws, axis=0)
    gs["b_spatial"] = jnp.sum(dbs[..., 0], axis=0)
    dz, gs["w_pool"], gs["s_pool"] = _pool_bwd(z, dy, p["w_pool"], p["s_pool"], dz)
    dz, dwdw, gs["b_dw"], gs["conv_ln_g"], gs["conv_ln_b"] = _conv_bwd(
        z, dy, p["w_dw"], p["b_dw"], p["conv_ln_g"], p["conv_ln_b"], dz)
    red.advance((dz,))
    g_in = _mm_tn("mix_in_dw", dz, sv["hn0"], after=red.after())
    red.add("in", ("w_in",), [g_in])
    if red.layer == 0:
        red.advance(())
    red.small("mixer", _small_grad_arrays(gs, dwdw, norms=False))
    dx0, dg = _mm_rowop("mix_in_bwd", "rms_bwd_res", [(dz, w["w_in"], NN)], (sv["x0"], dx1), p["norm_mix_pre"],
                        after=red.after())
    gs["norm_mix_pre"] = jnp.sum(dg, axis=0)
    late = {"norms": jnp.concatenate([gs[n] for n in NORM_NAMES], axis=0)}
    if red.layer == 0:
        late["loss"] = red.extra[0]
    red.small("norms", late)
    return dx0


NORM_NAMES = ("norm_mix_pre", "norm_mix_post", "norm_xattn_pre", "norm_mem", "norm_xattn_post", "norm_ffn_pre",
              "norm_ffn_post")
VEC_NAMES = ("s_pool", "b_dw", "conv_ln_g", "conv_ln_b")
SMALL_ARRAYS = ("norms", "gain_bias", "w_spatial", "w_pool", "vecs", "w_dw")


def _small_grad_arrays(gs, dwdw, norms=True):
    out = {"norms": jnp.concatenate([gs[n] for n in NORM_NAMES], axis=0)} if norms else {}
    out.update({"gain_bias": jnp.concatenate([gs["gmlp_v_gain"], gs["b_spatial"]], axis=0),
                "w_spatial": gs["w_spatial"], "w_pool": gs["w_pool"],
                "vecs": jnp.concatenate([gs[n] for n in VEC_NAMES], axis=0), "w_dw": dwdw})
    return out


def _layer_params(small, l):
    p = {n: small[n][l].reshape(1, -1) for n in ("norm_mix_pre", "norm_mix_post", "s_pool", "b_dw", "conv_ln_g",
                                                   "conv_ln_b", "norm_xattn_pre", "norm_mem", "norm_xattn_post",
                                                   "norm_ffn_pre", "norm_ffn_post")}
    p["gmlp_v_gain"] = small["gmlp_v_gain"][l]
    p["w_spatial"] = small["w_spatial"][l]
    p["b_spatial_t"] = small["b_spatial"][l].T
    p["w_pool"] = small["w_pool"][l]
    p["w_dw"] = jnp.pad(small["w_dw"][l], ((0, 1), (0, 0)))
    return p


def _local_step(x, mem, target, fetch, small, red):
    small = dict(small)
    saved, weights, params = [], [], []
    h = x
    marker = ()
    for l in range(DEPTH):
        w = fetch(l, "in", marker)
        if "taps" in w:
            small["w_dw"] = w.pop("taps")
        p = _layer_params(small, l)
        h, sv = _layer_fwd(h, mem, w, p, functools.partial(fetch, l))
        marker = (h,)
        saved.append(sv)
        weights.append(w)
        params.append(p)
    dh, loss = _loss_head(h, target, 512)
    red.extra = (loss,)
    for l in reversed(range(DEPTH)):
        red.layer = l
        dh = _layer_bwd(dh, mem, weights[l], params[l], saved[l], red)
    return loss, dh


HBM = pl.BlockSpec(memory_space=pltpu.HBM)


def _position():
    return lax.axis_index("x"), lax.axis_index("y"), lax.axis_index("c")


SEM = pl.BlockSpec(memory_space=pltpu.SEMAPHORE)
EFFECT = pltpu.SideEffectType.DATAFLOW_SIDE_EFFECTING
TOKEN = jax.ShapeDtypeStruct((8, LANES), F32)
TOKEN_SPEC = pl.BlockSpec(memory_space=pltpu.VMEM)


def _landing(shape, dtype):
    return pltpu.with_memory_space_constraint(lax.empty(shape, dtype), pltpu.HBM)


def _hbm_shapes(arrays):
    return [pltpu.HBM(a.shape, a.dtype) for a in arrays]


def _block(ref, r, dev):
    return ref.at[pl.ds((4 * dev[0] + 2 * dev[1] + dev[2]) * r, r), :]


def _split_call(name, body, thru, sems_in, after, sems_out, token):
    n = len(thru)
    out_shape = [pltpu.SemaphoreType.DMA(s) for s in sems_out] + _hbm_shapes(thru) + ([TOKEN] if token else [])
    out_specs = [SEM] * len(sems_out) + [HBM] * n + ([TOKEN_SPEC] if token else [])
    return pl.pallas_call(
        body, name=name, in_specs=[HBM] * n + [SEM] * len(sems_in) + [ANY] * len(after),
        out_specs=out_specs, out_shape=out_shape,
        input_output_aliases={i: len(sems_out) + i for i in range(n)},
        compiler_params=pltpu.CompilerParams(has_side_effects=EFFECT),
    )(*thru, *sems_in, *after)


def _place_own(name, srcs, dev, out_dtype, tr):
    n = len(srcs)
    r, cols = srcs[0][0].shape[-2:]
    tr = r if r < 16 else _row_tile(r, tr)
    nb = r // tr

    def body(dev_ref, *refs):
        for a in range(n):
            refs[n + a][...] = refs[a][...].astype(out_dtype)

    in_specs = [pl.BlockSpec((tr, cols), lambda i, d: (i, 0)) if l is None
                else pl.BlockSpec((None, tr, cols), lambda i, d, l=l: (l, i, 0)) for _, l in srcs]
    return pl.pallas_call(
        body, name=name,
        grid_spec=pltpu.PrefetchScalarGridSpec(
            num_scalar_prefetch=1, grid=(nb,), in_specs=in_specs,
            out_specs=[pl.BlockSpec((tr, cols), lambda i, d: (d[0] * nb + i, 0))] * n),
        out_shape=[jax.ShapeDtypeStruct((N_DEV * r, cols), out_dtype)] * n, compiler_params=_cparams(),
    )(dev, *[a for a, _ in srcs])


def _gather_peers(x, y, c):
    return [(1 - x, y, c), (x, 1 - y, c), (1 - x, 1 - y, c), (x, y, 1 - c)]


def _block_rows(land):
    return land.shape[0] // N_DEV


def _near_peers(x, y, c):
    return [(1 - x, y, c), (x, 1 - y, c), (x, y, 1 - c)]


def _relay_route(x, y, c):
    origin = (x + c * (1 - 2 * x), y + (1 - c) * (1 - 2 * y), c)
    target = (x + (1 - c) * (1 - 2 * x), y + c * (1 - 2 * y), c)
    return origin, target


def _same_block_copy(blk, send_sem, recv_sem, to):
    return pltpu.make_async_remote_copy(src_ref=blk, dst_ref=blk, send_sem=send_sem, recv_sem=recv_sem, device_id=to,
                                        device_id_type=MESH)


def _gather_start(name, lands, after):
    n = len(lands)

    def body(*refs):
        lz = refs[:n]
        send_sems, recv_sems = refs[n + len(after)], refs[n + len(after) + 1]
        token = refs[-1]
        x, y, c = _position()
        for a in range(n):
            own = _block(lz[a], _block_rows(lands[a]), (x, y, c))
            for k, to in enumerate(_near_peers(x, y, c)):
                _same_block_copy(own, send_sems.at[k], recv_sems.at[k], to).start()
        token[...] = jnp.zeros_like(token)

    out = _split_call(name, body, list(lands), [], after, [(3,), (3,)], True)
    return out[0], out[1], out[2:2 + n], out[-1]


def _gather_step(name, near, far, fresh, after):
    groups = [g for g in (near and near[0], far and far[0], fresh) if g]
    counts = [len(near[0]) if near else 0, len(far[0]) if far else 0, len(fresh) if fresh else 0]
    n = sum(counts)
    sems_in = ([near[1]] if near else []) + ([far[1]] if far else [])
    sems_out = ([(2,), (2,), (1,), (1,)] if near else []) + ([(1,), (1,)] if far else []) + ([(3,), (3,)] if fresh else [])

    def body(*refs):
        lz = list(refs[:n])
        ins = list(refs[n:n + len(sems_in)])
        outs = list(refs[n + len(sems_in) + len(after):n + len(sems_in) + len(after) + len(sems_out)])
        token = refs[-1]
        x, y, c = _position()
        me, sibling = (x, y, c), (x, y, 1 - c)
        near_lz, far_lz, fresh_lz = (lz[sum(counts[:i]):sum(counts[:i + 1])] for i in range(3))
        neighbours = _near_peers(x, y, c)[:2]
        origin, target = _relay_route(x, y, c)
        diagonal = (1 - x, 1 - y, c)
        if near:
            recv0 = ins.pop(0)
            fsend, frecv, rsend, rrecv = (outs.pop(0) for _ in range(4))
            for a, land in enumerate(near[0]):
                for j, chip in enumerate(neighbours):
                    _same_block_copy(_block(near_lz[a], _block_rows(land), chip), fsend.at[j], recv0.at[j], me).wait_recv()
        if far:
            rrecv_in = ins.pop(0)
            f2send, f2recv = outs.pop(0), outs.pop(0)
            for a, land in enumerate(far[0]):
                _same_block_copy(_block(far_lz[a], _block_rows(land), diagonal), f2send.at[0], rrecv_in.at[0], me).wait_recv()
            for a, land in enumerate(far[0]):
                _same_block_copy(_block(far_lz[a], _block_rows(land), diagonal), f2send.at[0], f2recv.at[0], sibling).start()
        if near:
            for a, land in enumerate(near[0]):
                r = _block_rows(land)
                _same_block_copy(_block(near_lz[a], r, origin), rsend.at[0], rrecv.at[0], target).start()
                for j, chip in enumerate(neighbours):
                    _same_block_copy(_block(near_lz[a], r, chip), fsend.at[j], frecv.at[j], sibling).start()
        if fresh:
            send_sems, recv_sems = outs.pop(0), outs.pop(0)
            for a, land in enumerate(fresh):
                own = _block(fresh_lz[a], _block_rows(land), me)
                for k, to in enumerate(_near_peers(x, y, c)):
                    _same_block_copy(own, send_sems.at[k], recv_sems.at[k], to).start()
        token[...] = jnp.zeros_like(token)

    out = list(_split_call(name, body, [l for g in groups for l in g], sems_in, after, sems_out, True))
    res = {"token": out.pop()}
    if near:
        res.update(fsend=out.pop(0), frecv=out.pop(0), rsend=out.pop(0), rrecv=out.pop(0))
    if far:
        res.update(f2send=out.pop(0), f2recv=out.pop(0))
    if fresh:
        res.update(send=out.pop(0), recv=out.pop(0))
    res["near"], res["far"], res["fresh"] = (out[sum(counts[:i]):sum(counts[:i + 1])] for i in range(3))
    return res


def _gather_finish(name, lands, send_sems, recv_sems, fsend, frecv, rsend, f2send, f2recv, after):
    n = len(lands)

    def body(*refs):
        lz = refs[:n]
        send0, recv0, fsend_ref, frecv_ref, rsend_ref, f2send_ref, f2recv_ref = refs[n:n + 7]
        x, y, c = _position()
        me = (x, y, c)
        near = _near_peers(x, y, c)[:2]
        origin, _ = _relay_route(x, y, c)
        for a in range(n):
            r = _block_rows(lands[a])
            sib = _block(lz[a], r, (x, y, 1 - c))
            _same_block_copy(sib, send0.at[2], recv0.at[2], me).wait_recv()
            for j, chip in enumerate(near):
                blk = _block(lz[a], r, (chip[0], chip[1], 1 - c))
                _same_block_copy(blk, fsend_ref.at[j], frecv_ref.at[j], me).wait_recv()
            far = _block(lz[a], r, (1 - x, 1 - y, 1 - c))
            _same_block_copy(far, f2send_ref.at[0], f2recv_ref.at[0], me).wait_recv()
            own = _block(lz[a], r, me)
            for k in range(3):
                _same_block_copy(own, send0.at[k], recv0.at[k], me).wait_send()
            for j, chip in enumerate(near):
                _same_block_copy(_block(lz[a], r, chip), fsend_ref.at[j], frecv_ref.at[j], me).wait_send()
            _same_block_copy(_block(lz[a], r, origin), rsend_ref.at[0], recv0.at[0], me).wait_send()
            _same_block_copy(_block(lz[a], r, (1 - x, 1 - y, c)), f2send_ref.at[0], f2recv_ref.at[0], me).wait_send()

    return _split_call(name, body, list(lands), [send_sems, recv_sems, fsend, frecv, rsend, f2send, f2recv], after, [],
                       False)


def _sibling_start(name, grads, after):
    n = len(grads)
    lands = [_landing((4, g.shape[0] // N_DEV, D), g.dtype) for g in grads]

    def body(*refs):
        ins, lz = refs[:n], refs[n:2 * n]
        send_sem, recv_sem = refs[2 * n + len(after)], refs[2 * n + len(after) + 1]
        token = refs[-1]
        x, y, c = _position()
        for a in range(n):
            r = grads[a].shape[0] // N_DEV
            for q in range(4):
                pltpu.make_async_remote_copy(
                    src_ref=ins[a].at[pl.ds((2 * q + 1 - c) * r, r), :], dst_ref=lz[a].at[q], send_sem=send_sem.at[0],
                    recv_sem=recv_sem.at[0], device_id=(x, y, 1 - c), device_id_type=MESH).start()
        token[...] = jnp.zeros_like(token)

    out = _split_call(name, body, list(grads) + lands, [], after, [(1,), (1,)], True)
    return out[0], out[1], out[2:2 + n], out[2 + n:2 + 2 * n], out[-1]


def _sibling_finish(name, grads, lands, send_sem, recv_sem, after):
    n = len(grads)

    def body(*refs):
        ins, lz = refs[:n], refs[n:2 * n]
        send_ref, recv_ref = refs[2 * n], refs[2 * n + 1]
        x, y, c = _position()
        for a in range(n):
            r = grads[a].shape[0] // N_DEV
            for q in range(4):
                cp = pltpu.make_async_remote_copy(
                    src_ref=ins[a].at[pl.ds((2 * q + 1 - c) * r, r), :], dst_ref=lz[a].at[q], send_sem=send_ref.at[0],
                    recv_sem=recv_ref.at[0], device_id=(x, y, c), device_id_type=MESH)
                cp.wait_send()
                cp.wait_recv()

    out = _split_call(name, body, list(grads) + list(lands), [send_sem, recv_sem], after, [], False)
    return out[:n], out[n:2 * n]


def _chip_start(name, parts, after):
    n = len(parts)
    lands = [_landing((3,) + p.shape[1:], p.dtype) for p in parts]

    def body(*refs):
        ins, lz = refs[:n], refs[n:2 * n]
        send_sems, recv_sems = refs[2 * n + len(after)], refs[2 * n + len(after) + 1]
        token = refs[-1]
        x, y, c = _position()
        for a in range(n):
            for j, chip in enumerate(_gather_peers(x, y, c)[:3]):
                pltpu.make_async_remote_copy(
                    src_ref=ins[a].at[2 * chip[0] + chip[1]], dst_ref=lz[a].at[j], send_sem=send_sems.at[j],
                    recv_sem=recv_sems.at[j], device_id=chip, device_id_type=MESH).start()
        token[...] = jnp.zeros_like(token)

    out = _split_call(name, body, list(parts) + lands, [], after, [(3,), (3,)], True)
    return out[0], out[1], out[2:2 + n], out[2 + n:2 + 2 * n], out[-1]


def _chip_finish(name, parts, lands, send_sems, recv_sems, after):
    n = len(parts)

    def body(*refs):
        ins, lz = refs[:n], refs[n:2 * n]
        send_ref, recv_ref = refs[2 * n], refs[2 * n + 1]
        me = _position()
        for a in range(n):
            for j in range(3):
                cp = pltpu.make_async_remote_copy(
                    src_ref=ins[a].at[j], dst_ref=lz[a].at[j], send_sem=send_ref.at[j], recv_sem=recv_ref.at[j],
                    device_id=me, device_id_type=MESH)
                cp.wait_send()
                cp.wait_recv()

    out = _split_call(name, body, list(parts) + list(lands), [send_sems, recv_sems], after, [], False)
    return out[:n], out[n:2 * n]


def _other_devices(x, y, c):
    return [(x + (k >> 2 & 1) * (1 - 2 * x), y + (k >> 1 & 1) * (1 - 2 * y), c + (k & 1) * (1 - 2 * c))
            for k in range(1, N_DEV)]


def _broadcast_start(name, arrays, after):
    n = len(arrays)
    lands = [_landing((N_DEV,) + a.shape, a.dtype) for a in arrays]

    def body(*refs):
        ins, lz = refs[:n], refs[n:2 * n]
        send_sems, recv_sems = refs[2 * n + len(after)], refs[2 * n + len(after) + 1]
        token = refs[-1]
        x, y, c = _position()
        for a in range(n):
            for k, peer in enumerate(_other_devices(x, y, c)):
                pltpu.make_async_remote_copy(
                    src_ref=ins[a], dst_ref=lz[a].at[4 * x + 2 * y + c], send_sem=send_sems.at[k],
                    recv_sem=recv_sems.at[k], device_id=peer, device_id_type=MESH).start()
        token[...] = jnp.zeros_like(token)

    out = _split_call(name, body, list(arrays) + lands, [], after, [(N_DEV - 1,), (N_DEV - 1,)], True)
    return out[0], out[1], out[2:2 + n], out[2 + n:2 + 2 * n], out[-1]


def _broadcast_finish(name, arrays, lands, send_sems, recv_sems, after):
    n = len(arrays)

    def body(*refs):
        ins, lz = refs[:n], refs[n:2 * n]
        send_ref, recv_ref = refs[2 * n], refs[2 * n + 1]
        x, y, c = _position()
        for a in range(n):
            for k, peer in enumerate(_other_devices(x, y, c)):
                cp = pltpu.make_async_remote_copy(
                    src_ref=ins[a], dst_ref=lz[a].at[4 * peer[0] + 2 * peer[1] + peer[2]], send_sem=send_ref.at[k],
                    recv_sem=recv_ref.at[k], device_id=(x, y, c), device_id_type=MESH)
                cp.wait_send()
                cp.wait_recv()

    out = _split_call(name, body, list(arrays) + list(lands), [send_sems, recv_sems], after, [], False)
    return out[:n], out[n:2 * n]


def _row_tile(r, target):
    return max(t for t in range(16, min(r, target) + 1, 16) if r % t == 0)


CHIP_PARTIAL_BYTES = 12 * 1024 * 1024


def _chip_partial(name, grads, gots, c):
    n = len(grads)
    r = grads[0].shape[0] // N_DEV
    tr = _row_tile(r, CHIP_PARTIAL_BYTES // (n * 3 * D * 2))

    def body(c_ref, *refs):
        for a in range(n):
            refs[2 * n + a][...] = (refs[a][...].astype(F32) + refs[n + a][...].astype(F32)).astype(BF16)

    blk = pl.BlockSpec((None, tr, D), lambda q, i, c_ref: (q, i, 0))
    return pl.pallas_call(
        body, name=name,
        grid_spec=pltpu.PrefetchScalarGridSpec(
            num_scalar_prefetch=1, grid=(4, r // tr),
            in_specs=[pl.BlockSpec((None, None, tr, D), lambda q, i, c_ref: (q, c_ref[0], i, 0))] * n + [blk] * n,
            out_specs=[blk] * n),
        out_shape=[jax.ShapeDtypeStruct((4, r, D), BF16)] * n, compiler_params=_cparams(),
    )(c, *[g.reshape(4, 2, r, D) for g in grads], *gots)


class _WeightGather:
    def __init__(self, groups):
        self.groups = list(groups)
        self.index = {key: i for i, (key, _, _) in enumerate(groups)}
        self.state = [None] * len(groups)
        self.token = ()
        for i in range(min(2, len(groups))):
            self._start(i)

    def _tag(self, i):
        return "%s_%d" % self.groups[i][0][::-1]

    def _start(self, i):
        send, recv, lz, tok = _gather_start("gather_start_" + self._tag(i), self.groups[i][2], self.token)
        self.state[i] = dict(send=send, recv=recv, lands=lz)
        self.token = (tok,)

    def _step(self, name, near, far, fresh, marker):
        exists = lambda i: i is not None and i < len(self.groups)
        near, far, fresh = (i if exists(i) else None for i in (near, far, fresh))
        res = _gather_step(
            name, None if near is None else (self.state[near]["lands"], self.state[near]["recv"]),
            None if far is None else (self.state[far]["lands"], self.state[far]["rrecv"]),
            None if fresh is None else self.groups[fresh][2], tuple(marker) + self.token)
        self.token = (res["token"],)
        if near is not None:
            self.state[near].update(lands=res["near"], fsend=res["fsend"], frecv=res["frecv"], rsend=res["rsend"],
                                    rrecv=res["rrecv"])
        if far is not None:
            self.state[far].update(lands=res["far"], f2send=res["f2send"], f2recv=res["f2recv"])
        if fresh is not None:
            self.state[fresh] = dict(send=res["send"], recv=res["recv"], lands=res["fresh"])

    def fetch(self, layer, group, marker):
        k = self.index[(layer, group)]
        if k == 0:
            self._step("gather_step_first", 0, None, None, marker)
        self._step("gather_step_" + self._tag(k), k + 1, k, k + 2, marker)
        st = self.state[k]
        lz = _gather_finish("gather_finish_" + self._tag(k), st["lands"], st["send"], st["recv"], st["fsend"],
                            st["frecv"], st["rsend"], st["f2send"], st["f2recv"], self.token)
        self.state[k] = None
        return dict(zip(self.groups[k][1], lz))


class _GradReduce:
    def __init__(self, core, chip):
        self.core, self.chip = core, chip
        self.layer = None
        self.token = ()
        self.at_sibling, self.at_chips = [], []
        self.extra, self.smalls = (), {}

    def after(self):
        return self.token

    def add(self, group, names, grads):
        tag = "%s_%d" % (group, self.layer)
        send, recv, grads, lands, tok = _sibling_start("grad_sibling_start_" + tag, grads, self.token)
        self.at_sibling.append((tag, [(self.layer, n) for n in names], send, recv, grads, lands))
        self.token = (tok,)

    def advance(self, marker):
        for tag, keys, send, recv, grads, lands in self.at_sibling:
            grads, lands = _sibling_finish("grad_sibling_finish_" + tag, grads, lands, send, recv, marker)
            parts = _chip_partial("chip_partial_" + tag, grads, lands, self.core)
            send, recv, parts, lands, tok = _chip_start("grad_chip_start_" + tag, parts, ())
            self.at_chips.append([tag, keys, send, recv, parts, lands])
            self.token = (tok,)
        self.at_sibling = []

    def small(self, part, arrays):
        keys = list(arrays)
        send, recv, own, slots, tok = _broadcast_start(
            "small_grads_start_%d_%s" % (self.layer, part), [arrays[k] for k in keys], self.token)
        self.smalls.setdefault(self.layer, []).append((part, keys, send, recv, own, slots))
        self.token = (tok,)

    def small_finish(self, layer, marker):
        mine, theirs = {}, {}
        for part, keys, send, recv, own, slots in self.smalls[layer]:
            own, slots = _broadcast_finish("small_grads_finish_%d_%s" % (layer, part), own, slots, send, recv, marker)
            mine.update(zip(keys, own))
            theirs.update(zip(keys, slots))
        return mine, theirs

    def collect(self, key, marker):
        for entry in self.at_chips:
            tag, keys, send, recv, parts, lands = entry
            if key in keys:
                if send is not None:
                    parts, lands = _chip_finish("grad_chip_finish_" + tag, parts, lands, send, recv, marker)
                    entry[2:] = [None, None, parts, lands]
                i = keys.index(key)
                return parts[i], lands[i]
        raise KeyError(key)


def _adamw_math(w, g, m, v):
    m = ADAM_B1 * m + (1.0 - ADAM_B1) * g
    v = ADAM_B2 * v + (1.0 - ADAM_B2) * jnp.square(g)
    m_hat = m / (1.0 - ADAM_B1 ** ADAM_STEP)
    v_hat = v / (1.0 - ADAM_B2 ** ADAM_STEP)
    delta = -ADAM_LR * (m_hat / (jnp.sqrt(v_hat) + ADAM_EPS) + ADAM_WD * w)
    return delta, m, v


def _adamw_small(wts, mom_m, mom_v, own, gathered, loss_own, loss_gathered, dev):
    names = SMALL
    nw = len(names)
    na = len(SMALL_ARRAYS)

    def body(dev_ref, *refs):
        w_refs, m_refs, v_refs = (dict(zip(names, refs[i * nw:(i + 1) * nw])) for i in range(3))
        own_refs = refs[3 * nw:3 * nw + DEPTH * na]
        g_refs = refs[3 * nw + DEPTH * na:3 * nw + 2 * DEPTH * na]
        loss_own_ref, loss_got_ref = refs[3 * nw + 2 * DEPTH * na:3 * nw + 2 * DEPTH * na + 2]
        outs = refs[3 * nw + 2 * DEPTH * na + 2:]
        g_out, d_out, m_out, v_out = (dict(zip(names, outs[i * nw:(i + 1) * nw])) for i in range(4))
        me = dev_ref[0]

        loss = None
        for d in range(N_DEV):
            for b in range(loss_own.shape[0]):
                term = jnp.where(me == d, loss_own_ref[b], loss_got_ref[d, b])
                loss = term if loss is None else loss + term
        outs[4 * nw][...] = loss

        def update(name, at, g):
            g_out[name][at] = g
            d_out[name][at], m_out[name][at], v_out[name][at] = _adamw_math(
                w_refs[name][at], g, m_refs[name][at], v_refs[name][at])

        for l in range(DEPTH):
            mine = dict(zip(SMALL_ARRAYS, own_refs[l * na:(l + 1) * na]))
            got = dict(zip(SMALL_ARRAYS, g_refs[l * na:(l + 1) * na]))

            def total(key, at):
                acc = None
                for d in range(N_DEV):
                    term = jnp.where(me == d, mine[key][at] if at else mine[key][...], got[key][(d,) + at])
                    acc = term if acc is None else acc + term
                return acc

            row = (slice(l, l + 1),)
            for k, name in enumerate(NORM_NAMES):
                update(name, row, total("norms", (slice(k, k + 1),)))
            for k, name in enumerate(VEC_NAMES):
                update(name, row, total("vecs", (slice(k, k + 1),)))
            update("gmlp_v_gain", (l,), total("gain_bias", (slice(0, NH),)))
            update("b_spatial", (l,), total("gain_bias", (slice(NH, 2 * NH),)))
            update("w_spatial", (l,), total("w_spatial", ()))
            update("w_pool", (l,), total("w_pool", ()))
            update("w_dw", (l,), total("w_dw", (slice(0, CONV_K),)))

    args = [src[n] for src in (wts, mom_m, mom_v) for n in names]
    args += [src[l][k] for src in (own, gathered) for l in range(DEPTH) for k in SMALL_ARRAYS]
    args += [loss_own, loss_gathered]
    outs = pl.pallas_call(
        body, name="adamw_small",
        in_specs=[pl.BlockSpec(memory_space=pltpu.SMEM)] + [pl.BlockSpec(memory_space=pltpu.VMEM)] * len(args),
        out_shape=[jax.ShapeDtypeStruct(wts[n].shape, F32) for _ in range(4) for n in names]
        + [jax.ShapeDtypeStruct((8, LANES), F32)],
        compiler_params=_cparams(),
    )(dev, *args)
    return tuple(dict(zip(names, outs[i * nw:(i + 1) * nw])) for i in range(4)) + (outs[4 * nw],)


def _adamw_layers(name, w, reduced, m, v, chip, tr, transposed=False, after=()):
    nl, r, cdim = w.shape
    tr = _row_tile(r, tr)
    nb = r // tr

    def body(q_ref, w_ref, p0_ref, g0_ref, p1_ref, g1_ref, m_ref, v_ref, *rest):
        g_ref, d_ref, nm_ref, nv_ref = rest[len(after):]

        def total(p_ref, got_ref):
            acc = p_ref[...].astype(F32)
            for j in range(3):
                acc = acc + got_ref[j].astype(F32)
            return acc

        g = jnp.where(pl.program_id(0) == 0, total(p0_ref, g0_ref), total(p1_ref, g1_ref))
        if transposed:
            g = g.T
        g_ref[...] = g
        d_ref[...], nm_ref[...], nv_ref[...] = _adamw_math(w_ref[...], g, m_ref[...], v_ref[...])

    blk = pl.BlockSpec((None, tr, cdim), lambda l, i, q: (l, i, 0))
    first = lambda l, i: i * (1 - l) + (nb - 1) * l
    second = lambda l, i: i * l
    if transposed:
        gshape = (cdim, tr)
        at = lambda lead, i: (lead, 0, i)
    else:
        gshape = (tr, cdim)
        at = lambda lead, i: (lead, i, 0)
    specs = [blk,
             pl.BlockSpec((None,) + gshape, lambda l, i, q: at(q[0], first(l, i))),
             pl.BlockSpec((3,) + gshape, lambda l, i, q: at(0, first(l, i))),
             pl.BlockSpec((None,) + gshape, lambda l, i, q: at(q[0], second(l, i))),
             pl.BlockSpec((3,) + gshape, lambda l, i, q: at(0, second(l, i))), blk, blk] + [ANY] * len(after)
    shape = jax.ShapeDtypeStruct((nl, r, cdim), F32)
    return pl.pallas_call(
        body, name=name,
        grid_spec=pltpu.PrefetchScalarGridSpec(num_scalar_prefetch=1, grid=(nl, nb), in_specs=specs, out_specs=[blk] * 4),
        out_shape=[shape] * 4, compiler_params=_cparams(),
    )(chip, w, *reduced[0], *reduced[1], m, v, *after)


def _to_rows(name, a):
    return jnp.swapaxes(a, 1, 2) if name == "w_in" else a


def _place_own_transposed(name, srcs, dev, out_dtype, tc):
    n = len(srcs)
    kdim, cdim = srcs[0][0].shape[-2:]

    def body(dev_ref, *refs):
        for a in range(n):
            refs[n + a][...] = refs[a][...].T.astype(out_dtype)

    return pl.pallas_call(
        body, name=name,
        grid_spec=pltpu.PrefetchScalarGridSpec(
            num_scalar_prefetch=1, grid=(kdim // tc,),
            in_specs=[pl.BlockSpec((None, tc, cdim), lambda i, d, l=l: (l, i, 0)) for _, l in srcs],
            out_specs=[pl.BlockSpec((cdim, tc), lambda i, d: (d[0], i))] * n),
        out_shape=[jax.ShapeDtypeStruct((N_DEV * cdim, kdim), out_dtype)] * n, compiler_params=_cparams(),
    )(dev, *[a for a, _ in srcs])


def _pack(arrays, rows):
    flat = jnp.concatenate([a.reshape(-1) for a in arrays])
    return jnp.pad(flat, (0, rows * D - flat.shape[0])).reshape(rows, D)


def _rows_for(shapes, mult=8):
    total = 0
    for shp in shapes:
        size = 1
        for dim in shp:
            size *= dim
        total += size
    return -(-total // (mult * D)) * mult


def kernel(x, mem, norm_mix_pre, norm_mix_post, w_in, w_out, gmlp_v_gain, w_spatial, b_spatial, w_pool, s_pool, w_dw, b_dw, conv_ln_g, conv_ln_b, norm_xattn_pre, norm_mem, norm_xattn_post, w_q, w_k, w_v, w_o, norm_ffn_pre, norm_ffn_post, w_up, w_down, loss_target, m_norm_mix_pre, m_norm_mix_post, m_w_in, m_w_out, m_gmlp_v_gain, m_w_spatial, m_b_spatial, m_w_pool, m_s_pool, m_w_dw, m_b_dw, m_conv_ln_g, m_conv_ln_b, m_norm_xattn_pre, m_norm_mem, m_norm_xattn_post, m_w_q, m_w_k, m_w_v, m_w_o, m_norm_ffn_pre, m_norm_ffn_post, m_w_up, m_w_down, v_norm_mix_pre, v_norm_mix_post, v_w_in, v_w_out, v_gmlp_v_gain, v_w_spatial, v_b_spatial, v_w_pool, v_s_pool, v_w_dw, v_b_dw, v_conv_ln_g, v_conv_ln_b, v_norm_xattn_pre, v_norm_mem, v_norm_xattn_post, v_w_q, v_w_k, v_w_v, v_w_o, v_norm_ffn_pre, v_norm_ffn_post, v_w_up, v_w_down):
    args = dict(locals())
    wts = {n: args[n] for n in WEIGHTS}
    mom_m = {n: args["m_" + n] for n in WEIGHTS}
    mom_v = {n: args["v_" + n] for n in WEIGHTS}
    xi, yi, ci = _position()
    me = 4 * xi + 2 * yi + ci

    dev = jnp.reshape(me, (1,)).astype(jnp.int32)
    lands = {}
    for call, names, tr in (("place_in", ("w_in",), 256), ("place_att", ("w_out", "w_q", "w_k", "w_v", "w_o"), 64),
                            ("place_up", ("w_up",), 256), ("place_down", ("w_down",), 256)):
        srcs = [(_to_rows(n, wts[n]), l) for l in range(DEPTH) for n in names]
        placed = (_place_own_transposed if names == ("w_up",) else _place_own)(call, srcs, dev, BF16, tr)
        lands.update(zip([(l, n) for l in range(DEPTH) for n in names], placed))
    (lands[(0, "taps")],) = _place_own("place_taps", [(_pack([w_dw], _rows_for([w_dw.shape])), None)], dev, F32, 8)
    groups = []
    for l in range(DEPTH):
        for group, names in GATHER_GROUPS:
            if (l, group) == (0, "in"):
                names = names + ("taps",)
            groups.append(((l, group), names, [lands[(l, n)] for n in names]))
    gather = _WeightGather(groups)

    def fetch(layer, group, marker):
        w = gather.fetch(layer, group, marker)
        if "taps" in w:
            blocks = w["taps"].reshape(N_DEV, -1)[:, :w_dw.size].reshape((N_DEV,) + w_dw.shape)
            w["taps"] = jnp.moveaxis(blocks, 0, 2).reshape(DEPTH, CONV_K, CW)
        return w

    reduce = _GradReduce(jnp.reshape(ci, (1,)).astype(jnp.int32), jnp.reshape(2 * xi + yi, (1,)).astype(jnp.int32))
    small = {n: wts[n] for n in SMALL if n != "w_dw"}
    _, dx = _local_step(x[0], mem[0], loss_target[0], fetch, small, reduce)
    reduce.advance((dx,))

    grad_w, delta, new_m, new_v = {}, {}, {}, {}
    marker = (dx,) + tuple(reduce.after())
    for n in UPDATE_ORDER:
        reduced = [reduce.collect((l, n), marker) for l in range(DEPTH)]
        outs = _adamw_layers("adamw_" + n, _to_rows(n, wts[n]), reduced, _to_rows(n, mom_m[n]), _to_rows(n, mom_v[n]),
                             reduce.chip, 256, transposed=n == "w_up", after=marker)
        grad_w[n], delta[n], new_m[n], new_v[n] = (_to_rows(n, o) for o in outs)
        marker = (outs[1],)

    own, slots = [None] * DEPTH, [None] * DEPTH
    for l in reversed(range(DEPTH)):
        own[l], slots[l] = reduce.small_finish(l, marker)
        if l == 0:
            loss_own, loss_slots = own[l].pop("loss"), slots[l].pop("loss")
    shard_cols = CW // N_DEV
    for l in range(DEPTH):
        own[l]["w_dw"] = lax.dynamic_slice_in_dim(own[l]["w_dw"], me * shard_cols, shard_cols, axis=1)
        slots[l]["w_dw"] = lax.dynamic_slice_in_dim(slots[l]["w_dw"], me * shard_cols, shard_cols, axis=2)
    *small_out, loss_tile = _adamw_small(wts, mom_m, mom_v, own, slots, loss_own, loss_slots, dev)
    for dst, src in zip((grad_w, delta, new_m, new_v), small_out):
        dst.update(src)

    return (loss_tile[0, 0], dx[None], *[grad_w[n] for n in WEIGHTS], *[delta[n] for n in WEIGHTS],
            *[new_m[n] for n in WEIGHTS], *[new_v[n] for n in WEIGHTS])
```

```python
import functools

import jax
import jax.numpy as jnp
from jax import lax
from jax.experimental import pallas as pl
from jax.experimental.pallas import tpu as pltpu

F32 = jnp.float32
BF16 = jnp.bfloat16

D = 2048
GW = 1024
PW = 512
CW = 512
HD = 128
NH = 8
NG = 4
POOL_WINDOWS = (2, 4, 8, 16)
CONV_K = 31
IN_COLS = 2 * GW + PW + 2 * CW
XH = 4
XHD = D // XH
ATT_SCALE = XHD ** -0.5
RMS_EPS = 1e-6
LN_EPS = 1e-5
DEPTH = 2
N_DEV = 8

ADAM_LR = 0.001
ADAM_B1 = 0.9
ADAM_B2 = 0.999
ADAM_EPS = 1e-08
ADAM_WD = 0.01
ADAM_STEP = 10

LANES = 128
CONV_HALO = 32
POOL_HALO = 16
ROW_TILE = 128
VMEM_LIMIT = 60 * 1024 * 1024

MESH = pl.DeviceIdType.MESH
NT = (((1,), (1,)), ((), ()))
NN = (((1,), (0,)), ((), ()))
TN = (((0,), (0,)), ((), ()))

UPDATE_ORDER = ("w_down", "w_up", "w_o", "w_q", "w_k", "w_v", "w_out", "w_in")
GATHER_GROUPS = (("in", ("w_in",)), ("out", ("w_out",)), ("att", ("w_q", "w_k", "w_v", "w_o")), ("up", ("w_up",)),
                 ("down", ("w_down",)))
SMALL = ("norm_mix_pre", "norm_mix_post", "gmlp_v_gain", "w_spatial", "b_spatial", "w_pool", "s_pool",
         "w_dw", "b_dw", "conv_ln_g", "conv_ln_b", "norm_xattn_pre", "norm_mem", "norm_xattn_post",
         "norm_ffn_pre", "norm_ffn_post")
WEIGHTS = ("norm_mix_pre", "norm_mix_post", "w_in", "w_out", "gmlp_v_gain", "w_spatial", "b_spatial", "w_pool",
           "s_pool", "w_dw", "b_dw", "conv_ln_g", "conv_ln_b", "norm_xattn_pre", "norm_mem", "norm_xattn_post",
           "w_q", "w_k", "w_v", "w_o", "norm_ffn_pre", "norm_ffn_post", "w_up", "w_down")


def _cparams():
    return pltpu.CompilerParams(vmem_limit_bytes=VMEM_LIMIT)


def _dot(a, b, dims):
    return lax.dot_general(a, b, dims, preferred_element_type=F32)


def _rms(x, g):
    y = x * lax.rsqrt(jnp.mean(x * x, axis=-1, keepdims=True) + RMS_EPS)
    return y * g


def _rms_bwd(x, g, dy):
    r = lax.rsqrt(jnp.mean(x * x, axis=-1, keepdims=True) + RMS_EPS)
    xh = x * r
    t = dy * g
    dx = r * (t - xh * jnp.mean(t * xh, axis=-1, keepdims=True))
    return dx, jnp.sum(dy * xh, axis=0, keepdims=True)


def _gelu(x):
    cdf = 0.5 * (1.0 + jnp.tanh(0.7978845608028654 * (x + 0.044715 * (x * x * x))))
    return x * cdf


def _layer_norm(x, g, b=None):
    mu = jnp.mean(x, axis=-1, keepdims=True)
    xc = x - mu
    var = jnp.mean(xc * xc, axis=-1, keepdims=True)
    y = xc * lax.rsqrt(var + LN_EPS) * g
    return y if b is None else y + b


def _sigmoid(x):
    return 1.0 / (1.0 + jnp.exp(-x))


def _gmlp_rows(zu, zv, gv):
    return _gelu(zu), _layer_norm(_gelu(zv), gv)


def _glu(cv, cg):
    return cv * _sigmoid(cg)


def _ln_silu(h, g, b):
    y = _layer_norm(h, g, b)
    return y * _sigmoid(y)


ANY = pl.BlockSpec(memory_space=pl.ANY)


ROWS_TILE = 256
COLS_TILE = 512
DW_TILE = 512
RESIDENT_K = 2048
STREAM_K_TILE = 1024
STREAM_ROWS = 512


def _k_tiles(kdim):
    if kdim <= RESIDENT_K:
        return ROWS_TILE, kdim
    return STREAM_ROWS, max(t for t in range(LANES, STREAM_K_TILE + 1, LANES) if kdim % t == 0)


def _rowop_mm(name, kind, rows, g, w, dims, out_dtype, u=None, after=()):
    s = rows[0].shape[0]
    n = w.shape[0] if dims == NT else w.shape[1]
    tm, tn = min(ROWS_TILE, s), min(COLS_TILE, n)
    ni, nj = s // tm, n // tn
    bwd = kind == "rms_bwd"

    def body(*refs):
        refs = list(refs)
        row_refs = [refs.pop(0) for _ in rows]
        g_ref, w_ref = refs.pop(0), refs.pop(0)
        u_ref = refs.pop(0) if u is not None else None
        del refs[:len(after)]
        out_ref, a_ref = refs.pop(0), refs.pop(0)
        dg_ref = refs.pop(0) if bwd else None
        a_all = refs.pop(0)
        t = pl.program_id(0)

        @pl.when(t < ni)
        def _():
            if bwd:
                a, dg = _rms_bwd(row_refs[0][...], g_ref[...], row_refs[1][...])
                dg_ref[0] = dg
            else:
                a = _rms(row_refs[0][...], g_ref[...])
            a_ref[...] = a.astype(BF16)
            a_all[pl.ds(pl.multiple_of(t * tm, tm), tm), :] = a.astype(BF16)

        @pl.when(t >= ni)
        def _():
            acc = _dot(a_all[...], w_ref[...], dims)
            if u_ref is not None:
                acc = acc * (2.0 * jnp.maximum(u_ref[...], 0.0))
            out_ref[...] = acc.astype(out_dtype)

    rows_at = lambda t: jnp.minimum(t, ni - 1)
    cols_at = lambda t: jnp.maximum(t - ni, 0)
    row_spec = pl.BlockSpec((tm, D), lambda t: (rows_at(t), 0))
    w_spec = (pl.BlockSpec((tn, D), lambda t: (cols_at(t), 0)) if dims == NT
              else pl.BlockSpec((D, tn), lambda t: (0, cols_at(t))))
    tile = pl.BlockSpec((s, tn), lambda t: (0, cols_at(t)))
    in_specs = [row_spec] * len(rows) + [pl.BlockSpec((1, D), lambda t: (0, 0)), w_spec]
    in_specs += ([tile] if u is not None else []) + [ANY] * len(after)
    out_shape = [jax.ShapeDtypeStruct((s, n), out_dtype), jax.ShapeDtypeStruct((s, D), BF16)]
    out_specs = [tile, row_spec]
    if bwd:
        out_shape.append(jax.ShapeDtypeStruct((ni, 1, D), F32))
        out_specs.append(pl.BlockSpec((1, 1, D), lambda t: (rows_at(t), 0, 0)))
    return pl.pallas_call(
        body, name=name, grid=(ni + nj,), in_specs=in_specs, out_specs=out_specs, out_shape=out_shape,
        scratch_shapes=[pltpu.VMEM((s, D), BF16)], compiler_params=_cparams(),
    )(*rows, g, w, *([u] if u is not None else []), *after)


def _mm_rowop(name, kind, pairs, rows, g, relu2=False, after=()):
    s, kdim = pairs[0][0].shape
    tm, tk = _k_tiles(kdim)
    tm = min(tm, s)
    ni, nk = s // tm, kdim // tk
    npair = len(pairs)

    def body(*refs):
        refs = list(refs)
        a_refs = [refs.pop(0) for _ in range(npair)]
        w_refs = [refs.pop(0) for _ in range(npair)]
        row_refs = [refs.pop(0) for _ in rows]
        g_ref = refs.pop(0)
        del refs[:len(after)]
        acc = refs.pop()
        outs = refs
        k = pl.program_id(1)

        @pl.when(k == 0)
        def _():
            acc[...] = jnp.zeros_like(acc)

        for a_ref, w_ref, (_, _, dims) in zip(a_refs, w_refs, pairs):
            a = a_ref[...]
            if relu2:
                a = jnp.square(jnp.maximum(a, 0.0))
            acc[...] += _dot(a.astype(BF16), w_ref[...], dims)

        @pl.when(k == nk - 1)
        def _():
            h = acc[...]
            if kind == "rms_res":
                outs[0][...] = row_refs[0][...] + _rms(h, g_ref[...])
                outs[1][...] = h
            else:
                dx, dg = _rms_bwd(row_refs[0][...], g_ref[...], h)
                if kind == "rms_bwd_res":
                    outs[0][...] = row_refs[1][...] + dx
                    outs[1][0] = dg
                else:
                    outs[0][0] = dg

    row_spec = pl.BlockSpec((tm, D), lambda i, k: (i, 0))
    dg_shape = jax.ShapeDtypeStruct((ni, 1, D), F32)
    dg_spec = pl.BlockSpec((1, 1, D), lambda i, k: (i, 0, 0))
    in_specs = [pl.BlockSpec((tm, tk), lambda i, k: (i, k))] * npair
    for _, _, dims in pairs:
        in_specs.append(pl.BlockSpec((tk, D), lambda i, k: (k, 0)) if dims == NN
                        else pl.BlockSpec((D, tk), lambda i, k: (0, k)))
    in_specs += [row_spec] * len(rows) + [pl.BlockSpec((1, D), lambda i, k: (0, 0))] + [ANY] * len(after)
    if kind == "rms_res":
        out_shape = [jax.ShapeDtypeStruct((s, D), F32)] * 2
        out_specs = [row_spec, row_spec]
    elif kind == "rms_bwd_res":
        out_shape = [jax.ShapeDtypeStruct((s, D), F32), dg_shape]
        out_specs = [row_spec, dg_spec]
    else:
        out_shape = [dg_shape]
        out_specs = [dg_spec]
    return pl.pallas_call(
        body, name=name, grid=(ni, nk), in_specs=in_specs, out_specs=out_specs, out_shape=out_shape,
        scratch_shapes=[pltpu.VMEM((tm, D), F32)], compiler_params=_cparams(),
    )(*[p[0] for p in pairs], *[p[1] for p in pairs], *rows, g, *after)


def _mm_tn(name, a, gmat, relu2=False, after=()):
    s, m = a.shape
    tm = min(DW_TILE, m)
    ni = m // tm

    def body(a_ref, g_ref, *rest):
        av = a_ref[...]
        if relu2:
            av = jnp.square(jnp.maximum(av, 0.0))
        rest[len(after)][...] = _dot(av.astype(BF16), g_ref[...], TN).astype(BF16)

    return pl.pallas_call(
        body, name=name, grid=(ni,),
        in_specs=[pl.BlockSpec((s, tm), lambda i: (0, i)), pl.BlockSpec((s, D), lambda i: (0, 0))] + [ANY] * len(after),
        out_specs=pl.BlockSpec((tm, D), lambda i: (i, 0)),
        out_shape=jax.ShapeDtypeStruct((m, D), BF16), compiler_params=_cparams(),
    )(a, gmat, *after)


def _tril():
    r = lax.broadcasted_iota(jnp.int32, (HD, HD), 0)
    c = lax.broadcasted_iota(jnp.int32, (HD, HD), 1)
    return (c <= r).astype(F32)


def _gmlp_fwd(z, gv, ws, bst, tb):
    s = z.shape[0]
    tb = min(tb, s)

    def body(zu_ref, zv_ref, gv_ref, ws_ref, bst_ref, y_ref):
        tril = _tril()
        for h in range(NH):
            cols = slice(h * HD, (h + 1) * HD)
            u, vln = _gmlp_rows(zu_ref[:, cols], zv_ref[:, cols], gv_ref[h:h + 1, :])
            wm = (ws_ref[h] * tril).astype(BF16)
            vb = vln.astype(BF16)
            for c in range(tb // HD):
                rws = slice(c * HD, (c + 1) * HD)
                mixed = _dot(wm, vb[rws], NN) + bst_ref[:, h:h + 1]
                y_ref[rws, cols] = (u[rws] * mixed).astype(BF16)

    return pl.pallas_call(
        body, name="gmlp_fwd", grid=(s // tb,),
        in_specs=[pl.BlockSpec((tb, GW), lambda i: (i, 0)), pl.BlockSpec((tb, GW), lambda i: (i, 1)),
                  pl.BlockSpec((NH, HD), lambda i: (0, 0)), pl.BlockSpec((NH, HD, HD), lambda i: (0, 0, 0)),
                  pl.BlockSpec((HD, NH), lambda i: (0, 0))],
        out_specs=pl.BlockSpec((tb, GW), lambda i: (i, 0)),
        out_shape=jax.ShapeDtypeStruct((s, D), BF16), compiler_params=_cparams(),
    )(z, z, gv, ws, bst)


def _gmlp_bwd(z, dy, gv, ws, bst, tb, after=()):
    s = z.shape[0]
    tb = min(tb, s)
    nb = s // tb

    def body(zu_ref, zv_ref, dy_ref, gv_ref, ws_ref, bst_ref, *rest):
        dz_ref, dgv_ref, dws_ref, db_ref = rest[len(after):]
        tril = _tril()
        for h in range(NH):
            cols = slice(h * HD, (h + 1) * HD)
            (u, vln), vjp = jax.vjp(_gmlp_rows, zu_ref[:, cols], zv_ref[:, cols], gv_ref[h:h + 1, :])
            wmf = ws_ref[h] * tril
            wm = wmf.astype(BF16)
            wmt = wmf.T.astype(BF16)
            vb = vln.astype(BF16)
            dws = jnp.zeros((HD, HD), F32)
            db = jnp.zeros((HD, 1), F32)
            du_parts, dvln_parts = [], []
            for c in range(tb // HD):
                rws = slice(c * HD, (c + 1) * HD)
                mixed = _dot(wm, vb[rws], NN) + bst_ref[:, h:h + 1]
                dyc = dy_ref[rws, cols]
                du_parts.append(dyc * mixed)
                dmixed = dyc * u[rws]
                dmb = dmixed.astype(BF16)
                dws = dws + _dot(dmb, vb[rws], NT)
                db = db + jnp.sum(dmixed, axis=1, keepdims=True)
                dvln_parts.append(_dot(wmt, dmb, NN))
            du = jnp.concatenate(du_parts, axis=0)
            dvln = jnp.concatenate(dvln_parts, axis=0)
            dzu, dzv, dgv = vjp((du, dvln))
            dz_ref[:, cols] = dzu.astype(BF16)
            dz_ref[:, slice(GW + h * HD, GW + (h + 1) * HD)] = dzv.astype(BF16)
            dgv_ref[0, h:h + 1, :] = dgv
            dws_ref[0, h] = dws * tril
            db_ref[0, h] = jnp.broadcast_to(db, (HD, LANES))

    blk = pl.BlockSpec((tb, GW), lambda i: (i, 0))
    return pl.pallas_call(
        body, name="gmlp_bwd", grid=(nb,),
        in_specs=[blk, pl.BlockSpec((tb, GW), lambda i: (i, 1)), blk,
                  pl.BlockSpec((NH, HD), lambda i: (0, 0)), pl.BlockSpec((NH, HD, HD), lambda i: (0, 0, 0)),
                  pl.BlockSpec((HD, NH), lambda i: (0, 0))] + [ANY] * len(after),
        out_specs=[pl.BlockSpec((tb, 2 * GW), lambda i: (i, 0)), pl.BlockSpec((1, NH, HD), lambda i: (i, 0, 0)),
                   pl.BlockSpec((1, NH, HD, HD), lambda i: (i, 0, 0, 0)),
                   pl.BlockSpec((1, NH, HD, LANES), lambda i: (i, 0, 0, 0))],
        out_shape=[jax.ShapeDtypeStruct((s, IN_COLS), BF16),
                   jax.ShapeDtypeStruct((nb, NH, HD), F32), jax.ShapeDtypeStruct((nb, NH, HD, HD), F32),
                   jax.ShapeDtypeStruct((nb, NH, HD, LANES), F32)],
        compiler_params=_cparams(),
    )(z, z, dy, gv, ws, bst, *after)


POOL_TILE = 512


def _pool_count(t0, window):
    pos = (t0 + lax.broadcasted_iota(jnp.int32, (POOL_TILE, LANES), 0)).astype(F32)
    return jnp.minimum(pos + 1.0, float(window))


def _window_sum(win, levels, back):
    n = win.shape[0]
    for lv in range(levels):
        step = 1 << lv
        win = win + pltpu.roll(win, n - step if back else step, 0)
    return win


def _pool_pooled(ppad_ref, t0, g):
    win = ppad_ref[pl.ds(t0, POOL_TILE + POOL_HALO), :]
    wsum = _window_sum(win, g + 1, False)[POOL_HALO:]
    return wsum / _pool_count(t0, POOL_WINDOWS[g]) - win[POOL_HALO:]


def _pool_fwd(z, wp, sp, y):
    s = z.shape[0]
    nt = s // POOL_TILE

    def body(p_ref, wp_ref, sp_ref, _, y_ref, ppad):
        for g in range(NG):
            cols = slice(g * LANES, (g + 1) * LANES)
            ppad[pl.ds(0, POOL_HALO), :] = jnp.zeros((POOL_HALO, LANES), F32)
            ppad[pl.ds(POOL_HALO, s), :] = p_ref[:, cols]
            wpb = wp_ref[g].astype(BF16)
            scale = sp_ref[:, cols]

            def tile(t, carry):
                t0 = pl.multiple_of(t * POOL_TILE, POOL_TILE)
                pooled = _pool_pooled(ppad, t0, g)
                y_ref[pl.ds(t0, POOL_TILE), cols] = (_dot(pooled.astype(BF16), wpb, NN) * scale).astype(BF16)
                return carry

            lax.fori_loop(0, nt, tile, 0)

    return pl.pallas_call(
        body, name="pool_fwd", grid=(1,),
        in_specs=[pl.BlockSpec((s, PW), lambda i: (0, 2 * GW // PW)),
                  pl.BlockSpec((NG, LANES, LANES), lambda i: (0, 0, 0)), pl.BlockSpec((1, PW), lambda i: (0, 0)), ANY],
        out_specs=pl.BlockSpec((s, PW), lambda i: (0, GW // PW)),
        out_shape=jax.ShapeDtypeStruct((s, D), BF16), input_output_aliases={3: 0},
        scratch_shapes=[pltpu.VMEM((s + POOL_HALO, LANES), F32)], compiler_params=_cparams(),
    )(z, wp, sp, y)


def _pool_bwd(z, dy, wp, sp, dz):
    s = z.shape[0]
    nt = s // POOL_TILE

    def body(p_ref, dy_ref, wp_ref, sp_ref, _, dp_ref, dwp_ref, dsp_ref, ppad, rpad, dpool):
        for g in range(NG):
            cols = slice(g * LANES, (g + 1) * LANES)
            ppad[pl.ds(0, POOL_HALO), :] = jnp.zeros((POOL_HALO, LANES), F32)
            ppad[pl.ds(POOL_HALO, s), :] = p_ref[:, cols]
            rpad[pl.ds(s, POOL_HALO), :] = jnp.zeros((POOL_HALO, LANES), F32)
            wpb = wp_ref[g].astype(BF16)
            scale = sp_ref[:, cols]

            def tile(t, carry):
                dwp, dsp = carry
                t0 = pl.multiple_of(t * POOL_TILE, POOL_TILE)
                pooled = _pool_pooled(ppad, t0, g)
                pb = pooled.astype(BF16)
                dyt = dy_ref[pl.ds(t0, POOL_TILE), cols]
                dsp = dsp + jnp.sum(dyt * _dot(pb, wpb, NN), axis=0, keepdims=True)
                dmm = (dyt * scale).astype(BF16)
                dwp = dwp + _dot(pb, dmm, TN)
                dpooled = _dot(dmm, wpb, NT)
                rpad[pl.ds(t0, POOL_TILE), :] = dpooled / _pool_count(t0, POOL_WINDOWS[g])
                dpool[pl.ds(t0, POOL_TILE), :] = dpooled
                return dwp, dsp

            dwp, dsp = lax.fori_loop(0, nt, tile, (jnp.zeros((LANES, LANES), F32), jnp.zeros((1, LANES), F32)))
            dwp_ref[g] = dwp
            dsp_ref[:, cols] = dsp

            def tile2(t, carry):
                t0 = pl.multiple_of(t * POOL_TILE, POOL_TILE)
                win = rpad[pl.ds(t0, POOL_TILE + POOL_HALO), :]
                back = _window_sum(win, g + 1, True)[:POOL_TILE]
                rows = pl.ds(t0, POOL_TILE)
                dp_ref[rows, cols] = (back - dpool[rows, :]).astype(BF16)
                return carry

            lax.fori_loop(0, nt, tile2, 0)

    return pl.pallas_call(
        body, name="pool_bwd", grid=(1,),
        in_specs=[pl.BlockSpec((s, PW), lambda i: (0, 2 * GW // PW)), pl.BlockSpec((s, PW), lambda i: (0, GW // PW)),
                  pl.BlockSpec((NG, LANES, LANES), lambda i: (0, 0, 0)), pl.BlockSpec((1, PW), lambda i: (0, 0)), ANY],
        out_specs=[pl.BlockSpec((s, PW), lambda i: (0, 2 * GW // PW)),
                   pl.BlockSpec((NG, LANES, LANES), lambda i: (0, 0, 0)), pl.BlockSpec((1, PW), lambda i: (0, 0))],
        out_shape=[jax.ShapeDtypeStruct((s, IN_COLS), BF16), jax.ShapeDtypeStruct((NG, LANES, LANES), F32),
                   jax.ShapeDtypeStruct((1, PW), F32)],
        input_output_aliases={4: 0},
        scratch_shapes=[pltpu.VMEM((s + POOL_HALO, LANES), F32), pltpu.VMEM((s + POOL_HALO, LANES), F32),
                        pltpu.VMEM((s, LANES), F32)],
        compiler_params=_cparams(),
    )(z, dy, wp, sp, dz)


CONV_LEAD = CONV_HALO - (CONV_K - 1)


SUBLANES = 8


def _sublane_shifts(win):
    n = win.shape[0]
    return [win] + [pltpu.roll(win, n - b, 0) for b in range(1, SUBLANES)]


def _shifted(shifts, offset):
    a, b = divmod(offset, SUBLANES)
    return shifts[b][a * SUBLANES:a * SUBLANES + ROW_TILE]


def _conv_taps(shifts, wdw_ref, lead, reverse):
    acc = jnp.zeros((ROW_TILE, CW), F32)
    for j in range(CONV_K):
        tap = (CONV_K - 1 - j) if reverse else j
        acc = acc + wdw_ref[tap:tap + 1, :] * _shifted(shifts, lead + j)
    return acc


def _conv_fill_glu(cv_ref, cg_ref, xpad, s):
    xpad[pl.ds(0, CONV_HALO), :] = jnp.zeros((CONV_HALO, CW), F32)

    def fill(t, carry):
        t0 = pl.multiple_of(t * ROW_TILE, ROW_TILE)
        rows = pl.ds(t0, ROW_TILE)
        xpad[pl.ds(t0 + CONV_HALO, ROW_TILE), :] = _glu(cv_ref[rows, :], cg_ref[rows, :])
        return carry

    lax.fori_loop(0, s // ROW_TILE, fill, 0)


def _conv_fwd(z, wdw, bdw, lng, lnb, y):
    s = z.shape[0]

    def body(cv_ref, cg_ref, wdw_ref, bdw_ref, lng_ref, lnb_ref, _, y_ref, xpad):
        _conv_fill_glu(cv_ref, cg_ref, xpad, s)

        def tile(t, carry):
            t0 = pl.multiple_of(t * ROW_TILE, ROW_TILE)
            shifts = _sublane_shifts(xpad[pl.ds(t0, ROW_TILE + CONV_HALO), :])
            hc = _conv_taps(shifts, wdw_ref, CONV_LEAD, False) + bdw_ref[...]
            y_ref[pl.ds(t0, ROW_TILE), :] = _ln_silu(hc, lng_ref[...], lnb_ref[...]).astype(BF16)
            return carry

        lax.fori_loop(0, s // ROW_TILE, tile, 0)

    vec = pl.BlockSpec((1, CW), lambda i: (0, 0))
    return pl.pallas_call(
        body, name="conv_fwd", grid=(1,),
        in_specs=[pl.BlockSpec((s, CW), lambda i: (0, (2 * GW + PW) // CW)),
                  pl.BlockSpec((s, CW), lambda i: (0, (2 * GW + PW) // CW + 1)),
                  pl.BlockSpec((CONV_K + 1, CW), lambda i: (0, 0)), vec, vec, vec, ANY],
        out_specs=pl.BlockSpec((s, CW), lambda i: (0, (GW + PW) // CW)),
        out_shape=jax.ShapeDtypeStruct((s, D), BF16), input_output_aliases={6: 0},
        scratch_shapes=[pltpu.VMEM((s + CONV_HALO, CW), F32)], compiler_params=_cparams(),
    )(z, z, wdw, bdw, lng, lnb, y)


def _conv_bwd(z, dy, wdw, bdw, lng, lnb, dz):
    s = z.shape[0]

    def body(cv_ref, cg_ref, dy_ref, wdw_ref, bdw_ref, lng_ref, lnb_ref, _,
             dz_ref, dwdw_ref, dbdw_ref, dlng_ref, dlnb_ref, xpad, dpad, dcg_keep):
        @pl.when(pl.program_id(0) == 0)
        def _():
            compute(cv_ref, cg_ref, dy_ref, wdw_ref, bdw_ref, lng_ref, lnb_ref,
                    dz_ref, dcg_keep, dwdw_ref, dbdw_ref, dlng_ref, dlnb_ref, xpad, dpad)

        @pl.when(pl.program_id(0) == 1)
        def _():
            dz_ref[...] = dcg_keep[...]

    def compute(cv_ref, cg_ref, dy_ref, wdw_ref, bdw_ref, lng_ref, lnb_ref,
                dcv_ref, dcg_ref, dwdw_ref, dbdw_ref, dlng_ref, dlnb_ref, xpad, dpad):
        _conv_fill_glu(cv_ref, cg_ref, xpad, s)
        dpad[pl.ds(s, CONV_HALO), :] = jnp.zeros((CONV_HALO, CW), F32)
        dwdw_ref[...] = jnp.zeros((CONV_K + 1, CW), F32)

        def tile(t, carry):
            db, dg, dbeta = carry
            t0 = pl.multiple_of(t * ROW_TILE, ROW_TILE)
            shifts = _sublane_shifts(xpad[pl.ds(t0, ROW_TILE + CONV_HALO), :])
            hc = _conv_taps(shifts, wdw_ref, CONV_LEAD, False) + bdw_ref[...]
            _, vjp = jax.vjp(_ln_silu, hc, lng_ref[...], lnb_ref[...])
            dhc, dg_t, dbeta_t = vjp(dy_ref[pl.ds(t0, ROW_TILE), :])
            dpad[pl.ds(t0, ROW_TILE), :] = dhc
            for j in range(CONV_K):
                dwdw_ref[j:j + 1, :] += jnp.sum(dhc * _shifted(shifts, CONV_LEAD + j), axis=0, keepdims=True)
            return db + jnp.sum(dhc, axis=0, keepdims=True), dg + dg_t, dbeta + dbeta_t

        zero = jnp.zeros((1, CW), F32)
        db, dg, dbeta = lax.fori_loop(0, s // ROW_TILE, tile, (zero, zero, zero))
        dbdw_ref[...] = db
        dlng_ref[...] = dg
        dlnb_ref[...] = dbeta

        def tile2(t, carry):
            t0 = pl.multiple_of(t * ROW_TILE, ROW_TILE)
            rows = pl.ds(t0, ROW_TILE)
            dglu = _conv_taps(_sublane_shifts(dpad[pl.ds(t0, ROW_TILE + CONV_HALO), :]), wdw_ref, 0, True)
            _, vjp = jax.vjp(_glu, cv_ref[rows, :], cg_ref[rows, :])
            dcv, dcg = vjp(dglu)
            dcv_ref[rows, :] = dcv.astype(BF16)
            dcg_ref[rows, :] = dcg.astype(BF16)
            return carry

        lax.fori_loop(0, s // ROW_TILE, tile2, 0)

    vec = pl.BlockSpec((1, CW), lambda i: (0, 0))
    wspec = pl.BlockSpec((CONV_K + 1, CW), lambda i: (0, 0))
    vshape = jax.ShapeDtypeStruct((1, CW), F32)
    return pl.pallas_call(
        body, name="conv_bwd", grid=(2,),
        in_specs=[pl.BlockSpec((s, CW), lambda i: (0, (2 * GW + PW) // CW)),
                  pl.BlockSpec((s, CW), lambda i: (0, (2 * GW + PW) // CW + 1)),
                  pl.BlockSpec((s, CW), lambda i: (0, (GW + PW) // CW)), wspec, vec, vec, vec, ANY],
        out_specs=[pl.BlockSpec((s, CW), lambda i: (0, (2 * GW + PW) // CW + i)), wspec, vec, vec, vec],
        out_shape=[jax.ShapeDtypeStruct((s, IN_COLS), BF16), jax.ShapeDtypeStruct((CONV_K + 1, CW), F32),
                   vshape, vshape, vshape],
        input_output_aliases={7: 0},
        scratch_shapes=[pltpu.VMEM((s + CONV_HALO, CW), F32), pltpu.VMEM((s + CONV_HALO, CW), F32),
                        pltpu.VMEM((s, CW), BF16)],
        compiler_params=_cparams(),
    )(z, z, dy, wdw, bdw, lng, lnb, dz)


def _softmax_rows(sc):
    e = jnp.exp(sc - jnp.max(sc, axis=-1, keepdims=True))
    return e / jnp.sum(e, axis=-1, keepdims=True)


def _attn_fwd(q, k, v, tq):
    s, m = q.shape[0], k.shape[0]
    tq = min(tq, s)

    def body(q_ref, k_ref, v_ref, o_ref):
        for h in range(XH):
            cols = slice(h * XHD, (h + 1) * XHD)
            p = _softmax_rows(_dot(q_ref[:, cols], k_ref[:, cols], NT) * ATT_SCALE)
            o_ref[:, cols] = _dot(p.astype(BF16), v_ref[:, cols], NN).astype(BF16)

    kv = pl.BlockSpec((m, D), lambda i: (0, 0))
    return pl.pallas_call(
        body, name="attn_fwd", grid=(s // tq,),
        in_specs=[pl.BlockSpec((tq, D), lambda i: (i, 0)), kv, kv],
        out_specs=pl.BlockSpec((tq, D), lambda i: (i, 0)),
        out_shape=jax.ShapeDtypeStruct((s, D), BF16), compiler_params=_cparams(),
    )(q, k, v)


def _attn_bwd(q, k, v, do, tq, after=()):
    s, m = q.shape[0], k.shape[0]
    tq = min(tq, s)

    def body(q_ref, k_ref, v_ref, do_ref, *rest):
        dq_ref, dk_ref, dv_ref = rest[len(after):]

        @pl.when(pl.program_id(0) == 0)
        def _():
            dk_ref[...] = jnp.zeros_like(dk_ref)
            dv_ref[...] = jnp.zeros_like(dv_ref)

        for h in range(XH):
            cols = slice(h * XHD, (h + 1) * XHD)
            qh, kh, vh, doh = q_ref[:, cols], k_ref[:, cols], v_ref[:, cols], do_ref[:, cols]
            p = _softmax_rows(_dot(qh, kh, NT) * ATT_SCALE)
            dp = _dot(doh, vh, NT)
            dv_ref[:, cols] += _dot(p.astype(BF16), doh, TN)
            ds = (p * (dp - jnp.sum(p * dp, axis=-1, keepdims=True)) * ATT_SCALE).astype(BF16)
            dq_ref[:, cols] = _dot(ds, kh, NN).astype(BF16)
            dk_ref[:, cols] += _dot(ds, qh, TN)

    kv = pl.BlockSpec((m, D), lambda i: (0, 0))
    qs = pl.BlockSpec((tq, D), lambda i: (i, 0))
    return pl.pallas_call(
        body, name="attn_bwd", grid=(s // tq,),
        in_specs=[qs, kv, kv, qs] + [ANY] * len(after), out_specs=[qs, kv, kv],
        out_shape=[jax.ShapeDtypeStruct((s, D), BF16), jax.ShapeDtypeStruct((m, D), F32),
                   jax.ShapeDtypeStruct((m, D), F32)],
        compiler_params=_cparams(),
    )(q, k, v, do, *after)


def _loss_head(y, target, tm):
    s = y.shape[0]
    tm = min(tm, s)

    def body(y_ref, t_ref, dy_ref, part_ref):
        err = y_ref[...] - t_ref[...]
        dy_ref[...] = err * (1.0 / D)
        part_ref[...] = jnp.full((1, 8, LANES), 0.5 * jnp.sum(err * err) * (1.0 / D), F32)

    blk = pl.BlockSpec((tm, D), lambda i: (i, 0))
    return pl.pallas_call(
        body, name="loss_head", grid=(s // tm,), in_specs=[blk, blk],
        out_specs=[blk, pl.BlockSpec((1, 8, LANES), lambda i: (i, 0, 0))],
        out_shape=[jax.ShapeDtypeStruct((s, D), F32), jax.ShapeDtypeStruct((s // tm, 8, LANES), F32)],
        compiler_params=_cparams(),
    )(y, target)


def _layer_fwd(x0, mem, w, p, fetch):
    z, hn0 = _rowop_mm("mix_in", "rms", (x0,), p["norm_mix_pre"], w["w_in"], NT, F32)
    y = _gmlp_fwd(z, p["gmlp_v_gain"], p["w_spatial"], p["b_spatial_t"], 512)
    y = _pool_fwd(z, p["w_pool"], p["s_pool"], y)
    y = _conv_fwd(z, p["w_dw"], p["b_dw"], p["conv_ln_g"], p["conv_ln_b"], y)
    w.update(fetch("out", (y,)))
    x1, h0 = _mm_rowop("mix_out", "rms_res", [(y, w["w_out"], NN)], (x0,), p["norm_mix_post"])
    w.update(fetch("att", (x1,)))
    q, hn1 = _rowop_mm("att_q", "rms", (x1,), p["norm_xattn_pre"], w["w_q"], NN, BF16)
    k, mn = _rowop_mm("att_k", "rms", (mem,), p["norm_mem"], w["w_k"], NN, BF16, after=(x1,))
    v, _ = _rowop_mm("att_v", "rms", (mem,), p["norm_mem"], w["w_v"], NN, BF16, after=(x1,))
    o = _attn_fwd(q, k, v, 1024)
    x2, h1 = _mm_rowop("att_o", "rms_res", [(o, w["w_o"], NN)], (x1,), p["norm_xattn_post"])
    w.update(fetch("up", (x2,)))
    u, hn2 = _rowop_mm("ffn_up", "rms", (x2,), p["norm_ffn_pre"], w["w_up"], NT, F32)
    w.update(fetch("down", (u,)))
    x3, h2 = _mm_rowop("ffn_down", "rms_res", [(u, w["w_down"], NN)], (x2,), p["norm_ffn_post"], relu2=True)
    saved = dict(x0=x0, z=z, hn0=hn0, y=y, h0=h0, x1=x1, q=q, hn1=hn1, k=k, v=v, mn=mn, o=o, h1=h1, x2=x2, u=u,
                 hn2=hn2, h2=h2)
    return x3, saved


def _layer_bwd(dx3, mem, w, p, sv, red):
    gs = {}
    du, dh2, dg = _rowop_mm("ffn_down_bwd", "rms_bwd", (sv["h2"], dx3), p["norm_ffn_post"], w["w_down"], NT, BF16,
                            u=sv["u"], after=red.after())
    gs["norm_ffn_post"] = jnp.sum(dg, axis=0)
    g_down = _mm_tn("ffn_down_dw", sv["u"], dh2, relu2=True)
    red.advance((g_down,))
    dx2, dg = _mm_rowop("ffn_up_bwd", "rms_bwd_res", [(du, w["w_up"], NN)], (sv["x2"], dx3), p["norm_ffn_pre"],
                        after=red.after())
    gs["norm_ffn_pre"] = jnp.sum(dg, axis=0)
    g_up = _mm_tn("ffn_up_dw", du, sv["hn2"])
    red.add("ffn", ("w_down", "w_up"), [g_down, g_up])
    do, dh1, dg = _rowop_mm("att_o_bwd", "rms_bwd", (sv["h1"], dx2), p["norm_xattn_post"], w["w_o"], NT, BF16,
                            after=red.after())
    gs["norm_xattn_post"] = jnp.sum(dg, axis=0)
    g_o = _mm_tn("att_o_dw", sv["o"], dh1)
    red.advance((g_o,))
    dq, dk, dv = _attn_bwd(sv["q"], sv["k"], sv["v"], do, 1024, after=red.after())
    dk, dv = dk.astype(BF16), dv.astype(BF16)
    dx1, dg = _mm_rowop("att_q_bwd", "rms_bwd_res", [(dq, w["w_q"], NT)], (sv["x1"], dx2), p["norm_xattn_pre"],
                        after=red.after())
    gs["norm_xattn_pre"] = jnp.sum(dg, axis=0)
    g_q = _mm_tn("att_q_dw", sv["hn1"], dq)
    g_k = _mm_tn("att_k_dw", sv["mn"], dk)
    g_v = _mm_tn("att_v_dw", sv["mn"], dv)
    (dg,) = _mm_rowop("att_kv_bwd", "rms_bwd_gain", [(dk, w["w_k"], NT), (dv, w["w_v"], NT)], (mem,), p["norm_mem"])
    gs["norm_mem"] = jnp.sum(dg, axis=0)
    red.add("att", ("w_o", "w_q", "w_k", "w_v"), [g_o, g_q, g_k, g_v])
    dy, dh0, dg = _rowop_mm("mix_out_bwd", "rms_bwd", (sv["h0"], dx1), p["norm_mix_post"], w["w_out"], NT, F32,
                            after=red.after())
    gs["norm_mix_post"] = jnp.sum(dg, axis=0)
    g_out = _mm_tn("mix_out_dw", sv["y"], dh0)
    red.advance((g_out,))
    red.add("out", ("w_out",), [g_out])
    z = sv["z"]
    dz, dgv, dws, dbs = _gmlp_bwd(z, dy, p["gmlp_v_gain"], p["w_spatial"], p["b_spatial_t"], 512, after=red.after())
    gs["gmlp_v_gain"] = jnp.sum(dgv, axis=0)
    gs["w_spatial"] = jnp.sum(dws, axis=0)
    gs["b_spatial"] = jnp.sum(dbs[..., 0], axis=0)
    dz, gs["w_pool"], gs["s_pool"] = _pool_bwd(z, dy, p["w_pool"], p["s_pool"], dz)
    dz, dwdw, gs["b_dw"], gs["conv_ln_g"], gs["conv_ln_b"] = _conv_bwd(
        z, dy, p["w_dw"], p["b_dw"], p["conv_ln_g"], p["conv_ln_b"], dz)
    red.advance((dz,))
    g_in = _mm_tn("mix_in_dw", dz, sv["hn0"], after=red.after())
    red.add("in", ("w_in",), [g_in])
    if red.layer == 0:
        red.advance(())
    red.small("mixer", _small_grad_arrays(gs, dwdw, norms=False))
    dx0, dg = _mm_rowop("mix_in_bwd", "rms_bwd_res", [(dz, w["w_in"], NN)], (sv["x0"], dx1), p["norm_mix_pre"],
                        after=red.after())
    gs["norm_mix_pre"] = jnp.sum(dg, axis=0)
    late = {"norms": jnp.concatenate([gs[n] for n in NORM_NAMES], axis=0)}
    if red.layer == 0:
        late["loss"] = red.extra[0]
    red.small("norms", late)
    return dx0


NORM_NAMES = ("norm_mix_pre", "norm_mix_post", "norm_xattn_pre", "norm_mem", "norm_xattn_post", "norm_ffn_pre",
              "norm_ffn_post")
VEC_NAMES = ("s_pool", "b_dw", "conv_ln_g", "conv_ln_b")
SMALL_ARRAYS = ("norms", "gain_bias", "w_spatial", "w_pool", "vecs", "w_dw")


def _small_grad_arrays(gs, dwdw, norms=True):
    out = {"norms": jnp.concatenate([gs[n] for n in NORM_NAMES], axis=0)} if norms else {}
    out.update({"gain_bias": jnp.concatenate([gs["gmlp_v_gain"], gs["b_spatial"]], axis=0),
                "w_spatial": gs["w_spatial"], "w_pool": gs["w_pool"],
                "vecs": jnp.concatenate([gs[n] for n in VEC_NAMES], axis=0), "w_dw": dwdw})
    return out


def _layer_params(small, l):
    p = {n: small[n][l].reshape(1, -1) for n in ("norm_mix_pre", "norm_mix_post", "s_pool", "b_dw", "conv_ln_g",
                                                   "conv_ln_b", "norm_xattn_pre", "norm_mem", "norm_xattn_post",
                                                   "norm_ffn_pre", "norm_ffn_post")}
    p["gmlp_v_gain"] = small["gmlp_v_gain"][l]
    p["w_spatial"] = small["w_spatial"][l]
    p["b_spatial_t"] = small["b_spatial"][l].T
    p["w_pool"] = small["w_pool"][l]
    p["w_dw"] = jnp.pad(small["w_dw"][l], ((0, 1), (0, 0)))
    return p


def _local_step(x, mem, target, fetch, small, red):
    small = dict(small)
    saved, weights, params = [], [], []
    h = x
    marker = ()
    for l in range(DEPTH):
        w = fetch(l, "in", marker)
        if "taps" in w:
            small["w_dw"] = w.pop("taps")
        p = _layer_params(small, l)
        h, sv = _layer_fwd(h, mem, w, p, functools.partial(fetch, l))
        marker = (h,)
        saved.append(sv)
        weights.append(w)
        params.append(p)
    dh, loss = _loss_head(h, target, 512)
    red.extra = (loss,)
    for l in reversed(range(DEPTH)):
        red.layer = l
        dh = _layer_bwd(dh, mem, weights[l], params[l], saved[l], red)
    return loss, dh


HBM = pl.BlockSpec(memory_space=pltpu.HBM)


def _position():
    return lax.axis_index("x"), lax.axis_index("y"), lax.axis_index("c")


SEM = pl.BlockSpec(memory_space=pltpu.SEMAPHORE)
EFFECT = pltpu.SideEffectType.DATAFLOW_SIDE_EFFECTING
TOKEN = jax.ShapeDtypeStruct((8, LANES), F32)
TOKEN_SPEC = pl.BlockSpec(memory_space=pltpu.VMEM)


def _landing(shape, dtype):
    return pltpu.with_memory_space_constraint(lax.empty(shape, dtype), pltpu.HBM)


def _hbm_shapes(arrays):
    return [pltpu.HBM(a.shape, a.dtype) for a in arrays]


def _block(ref, r, dev):
    return ref.at[pl.ds((4 * dev[0] + 2 * dev[1] + dev[2]) * r, r), :]


def _split_call(name, body, thru, sems_in, after, sems_out, token):
    n = len(thru)
    out_shape = [pltpu.SemaphoreType.DMA(s) for s in sems_out] + _hbm_shapes(thru) + ([TOKEN] if token else [])
    out_specs = [SEM] * len(sems_out) + [HBM] * n + ([TOKEN_SPEC] if token else [])
    return pl.pallas_call(
        body, name=name, in_specs=[HBM] * n + [SEM] * len(sems_in) + [ANY] * len(after),
        out_specs=out_specs, out_shape=out_shape,
        input_output_aliases={i: len(sems_out) + i for i in range(n)},
        compiler_params=pltpu.CompilerParams(has_side_effects=EFFECT),
    )(*thru, *sems_in, *after)


def _place_own(name, srcs, dev, out_dtype, tr):
    n = len(srcs)
    r, cols = srcs[0][0].shape[-2:]
    tr = r if r < 16 else _row_tile(r, tr)
    nb = r // tr

    def body(dev_ref, *refs):
        for a in range(n):
            refs[n + a][...] = refs[a][...].astype(out_dtype)

    in_specs = [pl.BlockSpec((tr, cols), lambda i, d: (i, 0)) if l is None
                else pl.BlockSpec((None, tr, cols), lambda i, d, l=l: (l, i, 0)) for _, l in srcs]
    return pl.pallas_call(
        body, name=name,
        grid_spec=pltpu.PrefetchScalarGridSpec(
            num_scalar_prefetch=1, grid=(nb,), in_specs=in_specs,
            out_specs=[pl.BlockSpec((tr, cols), lambda i, d: (d[0] * nb + i, 0))] * n),
        out_shape=[jax.ShapeDtypeStruct((N_DEV * r, cols), out_dtype)] * n, compiler_params=_cparams(),
    )(dev, *[a for a, _ in srcs])


def _gather_peers(x, y, c):
    return [(1 - x, y, c), (x, 1 - y, c), (1 - x, 1 - y, c), (x, y, 1 - c)]


def _block_rows(land):
    return land.shape[0] // N_DEV


def _near_peers(x, y, c):
    return [(1 - x, y, c), (x, 1 - y, c), (x, y, 1 - c)]


def _relay_route(x, y, c):
    origin = (x + c * (1 - 2 * x), y + (1 - c) * (1 - 2 * y), c)
    target = (x + (1 - c) * (1 - 2 * x), y + c * (1 - 2 * y), c)
    return origin, target


def _same_block_copy(blk, send_sem, recv_sem, to):
    return pltpu.make_async_remote_copy(src_ref=blk, dst_ref=blk, send_sem=send_sem, recv_sem=recv_sem, device_id=to,
                                        device_id_type=MESH)


def _gather_start(name, lands, after):
    n = len(lands)

    def body(*refs):
        lz = refs[:n]
        send_sems, recv_sems = refs[n + len(after)], refs[n + len(after) + 1]
        token = refs[-1]
        x, y, c = _position()
        for a in range(n):
            own = _block(lz[a], _block_rows(lands[a]), (x, y, c))
            for k, to in enumerate(_near_peers(x, y, c)):
                _same_block_copy(own, send_sems.at[k], recv_sems.at[k], to).start()
        token[...] = jnp.zeros_like(token)

    out = _split_call(name, body, list(lands), [], after, [(3,), (3,)], True)
    return out[0], out[1], out[2:2 + n], out[-1]


def _gather_step(name, near, far, fresh, after):
    groups = [g for g in (near and near[0], far and far[0], fresh) if g]
    counts = [len(near[0]) if near else 0, len(far[0]) if far else 0, len(fresh) if fresh else 0]
    n = sum(counts)
    sems_in = ([near[1]] if near else []) + ([far[1]] if far else [])
    sems_out = ([(2,), (2,), (1,), (1,)] if near else []) + ([(1,), (1,)] if far else []) + ([(3,), (3,)] if fresh else [])

    def body(*refs):
        lz = list(refs[:n])
        ins = list(refs[n:n + len(sems_in)])
        outs = list(refs[n + len(sems_in) + len(after):n + len(sems_in) + len(after) + len(sems_out)])
        token = refs[-1]
        x, y, c = _position()
        me, sibling = (x, y, c), (x, y, 1 - c)
        near_lz, far_lz, fresh_lz = (lz[sum(counts[:i]):sum(counts[:i + 1])] for i in range(3))
        neighbours = _near_peers(x, y, c)[:2]
        origin, target = _relay_route(x, y, c)
        diagonal = (1 - x, 1 - y, c)
        if near:
            recv0 = ins.pop(0)
            fsend, frecv, rsend, rrecv = (outs.pop(0) for _ in range(4))
            for a, land in enumerate(near[0]):
                for j, chip in enumerate(neighbours):
                    _same_block_copy(_block(near_lz[a], _block_rows(land), chip), fsend.at[j], recv0.at[j], me).wait_recv()
        if far:
            rrecv_in = ins.pop(0)
            f2send, f2recv = outs.pop(0), outs.pop(0)
            for a, land in enumerate(far[0]):
                _same_block_copy(_block(far_lz[a], _block_rows(land), diagonal), f2send.at[0], rrecv_in.at[0], me).wait_recv()
            for a, land in enumerate(far[0]):
                _same_block_copy(_block(far_lz[a], _block_rows(land), diagonal), f2send.at[0], f2recv.at[0], sibling).start()
        if near:
            for a, land in enumerate(near[0]):
                r = _block_rows(land)
                _same_block_copy(_block(near_lz[a], r, origin), rsend.at[0], rrecv.at[0], target).start()
                for j, chip in enumerate(neighbours):
                    _same_block_copy(_block(near_lz[a], r, chip), fsend.at[j], frecv.at[j], sibling).start()
        if fresh:
            send_sems, recv_sems = outs.pop(0), outs.pop(0)
            for a, land in enumerate(fresh):
                own = _block(fresh_lz[a], _block_rows(land), me)
                for k, to in enumerate(_near_peers(x, y, c)):
                    _same_block_copy(own, send_sems.at[k], recv_sems.at[k], to).start()
        token[...] = jnp.zeros_like(token)

    out = list(_split_call(name, body, [l for g in groups for l in g], sems_in, after, sems_out, True))
    res = {"token": out.pop()}
    if near:
        res.update(fsend=out.pop(0), frecv=out.pop(0), rsend=out.pop(0), rrecv=out.pop(0))
    if far:
        res.update(f2send=out.pop(0), f2recv=out.pop(0))
    if fresh:
        res.update(send=out.pop(0), recv=out.pop(0))
    res["near"], res["far"], res["fresh"] = (out[sum(counts[:i]):sum(counts[:i + 1])] for i in range(3))
    return res


def _gather_finish(name, lands, send_sems, recv_sems, fsend, frecv, rsend, f2send, f2recv, after):
    n = len(lands)

    def body(*refs):
        lz = refs[:n]
        send0, recv0, fsend_ref, frecv_ref, rsend_ref, f2send_ref, f2recv_ref = refs[n:n + 7]
        x, y, c = _position()
        me = (x, y, c)
        near = _near_peers(x, y, c)[:2]
        origin, _ = _relay_route(x, y, c)
        for a in range(n):
            r = _block_rows(lands[a])
            sib = _block(lz[a], r, (x, y, 1 - c))
            _same_block_copy(sib, send0.at[2], recv0.at[2], me).wait_recv()
            for j, chip in enumerate(near):
                blk = _block(lz[a], r, (chip[0], chip[1], 1 - c))
                _same_block_copy(blk, fsend_ref.at[j], frecv_ref.at[j], me).wait_recv()
            far = _block(lz[a], r, (1 - x, 1 - y, 1 - c))
            _same_block_copy(far, f2send_ref.at[0], f2recv_ref.at[0], me).wait_recv()
            own = _block(lz[a], r, me)
            for k in range(3):
                _same_block_copy(own, send0.at[k], recv0.at[k], me).wait_send()
            for j, chip in enumerate(near):
                _same_block_copy(_block(lz[a], r, chip), fsend_ref.at[j], frecv_ref.at[j], me).wait_send()
            _same_block_copy(_block(lz[a], r, origin), rsend_ref.at[0], recv0.at[0], me).wait_send()
            _same_block_copy(_block(lz[a], r, (1 - x, 1 - y, c)), f2send_ref.at[0], f2recv_ref.at[0], me).wait_send()

    return _split_call(name, body, list(lands), [send_sems, recv_sems, fsend, frecv, rsend, f2send, f2recv], after, [],
                       False)


def _sibling_start(name, grads, after):
    n = len(grads)
    lands = [_landing((4, g.shape[0] // N_DEV, D), g.dtype) for g in grads]

    def body(*refs):
        ins, lz = refs[:n], refs[n:2 * n]
        send_sem, recv_sem = refs[2 * n + len(after)], refs[2 * n + len(after) + 1]
        token = refs[-1]
        x, y, c = _position()
        for a in range(n):
            r = grads[a].shape[0] // N_DEV
            for q in range(4):
                pltpu.make_async_remote_copy(
                    src_ref=ins[a].at[pl.ds((2 * q + 1 - c) * r, r), :], dst_ref=lz[a].at[q], send_sem=send_sem.at[0],
                    recv_sem=recv_sem.at[0], device_id=(x, y, 1 - c), device_id_type=MESH).start()
        token[...] = jnp.zeros_like(token)

    out = _split_call(name, body, list(grads) + lands, [], after, [(1,), (1,)], True)
    return out[0], out[1], out[2:2 + n], out[2 + n:2 + 2 * n], out[-1]


def _sibling_finish(name, grads, lands, send_sem, recv_sem, after):
    n = len(grads)

    def body(*refs):
        ins, lz = refs[:n], refs[n:2 * n]
        send_ref, recv_ref = refs[2 * n], refs[2 * n + 1]
        x, y, c = _position()
        for a in range(n):
            r = grads[a].shape[0] // N_DEV
            for q in range(4):
                cp = pltpu.make_async_remote_copy(
                    src_ref=ins[a].at[pl.ds((2 * q + 1 - c) * r, r), :], dst_ref=lz[a].at[q], send_sem=send_ref.at[0],
                    recv_sem=recv_ref.at[0], device_id=(x, y, c), device_id_type=MESH)
                cp.wait_send()
                cp.wait_recv()

    out = _split_call(name, body, list(grads) + list(lands), [send_sem, recv_sem], after, [], False)
    return out[:n], out[n:2 * n]


def _chip_start(name, parts, after):
    n = len(parts)
    lands = [_landing((3,) + p.shape[1:], p.dtype) for p in parts]

    def body(*refs):
        ins, lz = refs[:n], refs[n:2 * n]
        send_sems, recv_sems = refs[2 * n + len(after)], refs[2 * n + len(after) + 1]
        token = refs[-1]
        x, y, c = _position()
        for a in range(n):
            for j, chip in enumerate(_gather_peers(x, y, c)[:3]):
                pltpu.make_async_remote_copy(
                    src_ref=ins[a].at[2 * chip[0] + chip[1]], dst_ref=lz[a].at[j], send_sem=send_sems.at[j],
                    recv_sem=recv_sems.at[j], device_id=chip, device_id_type=MESH).start()
        token[...] = jnp.zeros_like(token)

    out = _split_call(name, body, list(parts) + lands, [], after, [(3,), (3,)], True)
    return out[0], out[1], out[2:2 + n], out[2 + n:2 + 2 * n], out[-1]


def _chip_finish(name, parts, lands, send_sems, recv_sems, after):
    n = len(parts)

    def body(*refs):
        ins, lz = refs[:n], refs[n:2 * n]
        send_ref, recv_ref = refs[2 * n], refs[2 * n + 1]
        me = _position()
        for a in range(n):
            for j in range(3):
                cp = pltpu.make_async_remote_copy(
                    src_ref=ins[a].at[j], dst_ref=lz[a].at[j], send_sem=send_ref.at[j], recv_sem=recv_ref.at[j],
                    device_id=me, device_id_type=MESH)
                cp.wait_send()
                cp.wait_recv()

    out = _split_call(name, body, list(parts) + list(lands), [send_sems, recv_sems], after, [], False)
    return out[:n], out[n:2 * n]


def _other_devices(x, y, c):
    return [(x + (k >> 2 & 1) * (1 - 2 * x), y + (k >> 1 & 1) * (1 - 2 * y), c + (k & 1) * (1 - 2 * c))
            for k in range(1, N_DEV)]


def _broadcast_start(name, arrays, after):
    n = len(arrays)
    lands = [_landing((N_DEV,) + a.shape, a.dtype) for a in arrays]

    def body(*refs):
        ins, lz = refs[:n], refs[n:2 * n]
        send_sems, recv_sems = refs[2 * n + len(after)], refs[2 * n + len(after) + 1]
        token = refs[-1]
        x, y, c = _position()
        for a in range(n):
            for k, peer in enumerate(_other_devices(x, y, c)):
                pltpu.make_async_remote_copy(
                    src_ref=ins[a], dst_ref=lz[a].at[4 * x + 2 * y + c], send_sem=send_sems.at[k],
                    recv_sem=recv_sems.at[k], device_id=peer, device_id_type=MESH).start()
        token[...] = jnp.zeros_like(token)

    out = _split_call(name, body, list(arrays) + lands, [], after, [(N_DEV - 1,), (N_DEV - 1,)], True)
    return out[0], out[1], out[2:2 + n], out[2 + n:2 + 2 * n], out[-1]


def _broadcast_finish(name, arrays, lands, send_sems, recv_sems, after):
    n = len(arrays)

    def body(*refs):
        ins, lz = refs[:n], refs[n:2 * n]
        send_ref, recv_ref = refs[2 * n], refs[2 * n + 1]
        x, y, c = _position()
        for a in range(n):
            for k, peer in enumerate(_other_devices(x, y, c)):
                cp = pltpu.make_async_remote_copy(
                    src_ref=ins[a], dst_ref=lz[a].at[4 * peer[0] + 2 * peer[1] + peer[2]], send_sem=send_ref.at[k],
                    recv_sem=recv_ref.at[k], device_id=(x, y, c), device_id_type=MESH)
                cp.wait_send()
                cp.wait_recv()

    out = _split_call(name, body, list(arrays) + list(lands), [send_sems, recv_sems], after, [], False)
    return out[:n], out[n:2 * n]


def _row_tile(r, target):
    return max(t for t in range(16, min(r, target) + 1, 16) if r % t == 0)


CHIP_PARTIAL_BYTES = 12 * 1024 * 1024


def _chip_partial(name, grads, gots, c):
    n = len(grads)
    r = grads[0].shape[0] // N_DEV
    tr = _row_tile(r, CHIP_PARTIAL_BYTES // (n * 3 * D * 2))

    def body(c_ref, *refs):
        for a in range(n):
            refs[2 * n + a][...] = (refs[a][...].astype(F32) + refs[n + a][...].astype(F32)).astype(BF16)

    blk = pl.BlockSpec((None, tr, D), lambda q, i, c_ref: (q, i, 0))
    return pl.pallas_call(
        body, name=name,
        grid_spec=pltpu.PrefetchScalarGridSpec(
            num_scalar_prefetch=1, grid=(4, r // tr),
            in_specs=[pl.BlockSpec((None, None, tr, D), lambda q, i, c_ref: (q, c_ref[0], i, 0))] * n + [blk] * n,
            out_specs=[blk] * n),
        out_shape=[jax.ShapeDtypeStruct((4, r, D), BF16)] * n, compiler_params=_cparams(),
    )(c, *[g.reshape(4, 2, r, D) for g in grads], *gots)


class _WeightGather:
    def __init__(self, groups):
        self.groups = list(groups)
        self.index = {key: i for i, (key, _, _) in enumerate(groups)}
        self.state = [None] * len(groups)
        self.token = ()
        for i in range(min(2, len(groups))):
            self._start(i)

    def _tag(self, i):
        return "%s_%d" % self.groups[i][0][::-1]

    def _start(self, i):
        send, recv, lz, tok = _gather_start("gather_start_" + self._tag(i), self.groups[i][2], self.token)
        self.state[i] = dict(send=send, recv=recv, lands=lz)
        self.token = (tok,)

    def _step(self, name, near, far, fresh, marker):
        exists = lambda i: i is not None and i < len(self.groups)
        near, far, fresh = (i if exists(i) else None for i in (near, far, fresh))
        res = _gather_step(
            name, None if near is None else (self.state[near]["lands"], self.state[near]["recv"]),
            None if far is None else (self.state[far]["lands"], self.state[far]["rrecv"]),
            None if fresh is None else self.groups[fresh][2], tuple(marker) + self.token)
        self.token = (res["token"],)
        if near is not None:
            self.state[near].update(lands=res["near"], fsend=res["fsend"], frecv=res["frecv"], rsend=res["rsend"],
                                    rrecv=res["rrecv"])
        if far is not None:
            self.state[far].update(lands=res["far"], f2send=res["f2send"], f2recv=res["f2recv"])
        if fresh is not None:
            self.state[fresh] = dict(send=res["send"], recv=res["recv"], lands=res["fresh"])

    def fetch(self, layer, group, marker):
        k = self.index[(layer, group)]
        if k == 0:
            self._step("gather_step_first", 0, None, None, marker)
        self._step("gather_step_" + self._tag(k), k + 1, k, k + 2, marker)
        st = self.state[k]
        lz = _gather_finish("gather_finish_" + self._tag(k), st["lands"], st["send"], st["recv"], st["fsend"],
                            st["frecv"], st["rsend"], st["f2send"], st["f2recv"], self.token)
        self.state[k] = None
        return dict(zip(self.groups[k][1], lz))


class _GradReduce:
    def __init__(self, core, chip):
        self.core, self.chip = core, chip
        self.layer = None
        self.token = ()
        self.at_sibling, self.at_chips = [], []
        self.extra, self.smalls = (), {}

    def after(self):
        return self.token

    def add(self, group, names, grads):
        tag = "%s_%d" % (group, self.layer)
        send, recv, grads, lands, tok = _sibling_start("grad_sibling_start_" + tag, grads, self.token)
        self.at_sibling.append((tag, [(self.layer, n) for n in names], send, recv, grads, lands))
        self.token = (tok,)

    def advance(self, marker):
        for tag, keys, send, recv, grads, lands in self.at_sibling:
            grads, lands = _sibling_finish("grad_sibling_finish_" + tag, grads, lands, send, recv, marker)
            parts = _chip_partial("chip_partial_" + tag, grads, lands, self.core)
            send, recv, parts, lands, tok = _chip_start("grad_chip_start_" + tag, parts, ())
            self.at_chips.append([tag, keys, send, recv, parts, lands])
            self.token = (tok,)
        self.at_sibling = []

    def small(self, part, arrays):
        keys = list(arrays)
        send, recv, own, slots, tok = _broadcast_start(
            "small_grads_start_%d_%s" % (self.layer, part), [arrays[k] for k in keys], self.token)
        self.smalls.setdefault(self.layer, []).append((part, keys, send, recv, own, slots))
        self.token = (tok,)

    def small_finish(self, layer, marker):
        mine, theirs = {}, {}
        for part, keys, send, recv, own, slots in self.smalls[layer]:
            own, slots = _broadcast_finish("small_grads_finish_%d_%s" % (layer, part), own, slots, send, recv, marker)
            mine.update(zip(keys, own))
            theirs.update(zip(keys, slots))
        return mine, theirs

    def collect(self, key, marker):
        for entry in self.at_chips:
            tag, keys, send, recv, parts, lands = entry
            if key in keys:
                if send is not None:
                    parts, lands = _chip_finish("grad_chip_finish_" + tag, parts, lands, send, recv, marker)
                    entry[2:] = [None, None, parts, lands]
                i = keys.index(key)
                return parts[i], lands[i]
        raise KeyError(key)


def _adamw_math(w, g, m, v):
    m = ADAM_B1 * m + (1.0 - ADAM_B1) * g
    v = ADAM_B2 * v + (1.0 - ADAM_B2) * jnp.square(g)
    m_hat = m / (1.0 - ADAM_B1 ** ADAM_STEP)
    v_hat = v / (1.0 - ADAM_B2 ** ADAM_STEP)
    delta = -ADAM_LR * (m_hat / (jnp.sqrt(v_hat) + ADAM_EPS) + ADAM_WD * w)
    return delta, m, v


def _adamw_small(wts, mom_m, mom_v, own, gathered, loss_own, loss_gathered, dev):
    names = SMALL
    nw = len(names)
    na = len(SMALL_ARRAYS)

    def body(dev_ref, *refs):
        w_refs, m_refs, v_refs = (dict(zip(names, refs[i * nw:(i + 1) * nw])) for i in range(3))
        own_refs = refs[3 * nw:3 * nw + DEPTH * na]
        g_refs = refs[3 * nw + DEPTH * na:3 * nw + 2 * DEPTH * na]
        loss_own_ref, loss_got_ref = refs[3 * nw + 2 * DEPTH * na:3 * nw + 2 * DEPTH * na + 2]
        outs = refs[3 * nw + 2 * DEPTH * na + 2:]
        g_out, d_out, m_out, v_out = (dict(zip(names, outs[i * nw:(i + 1) * nw])) for i in range(4))
        me = dev_ref[0]

        loss = None
        for d in range(N_DEV):
            for b in range(loss_own.shape[0]):
                term = jnp.where(me == d, loss_own_ref[b], loss_got_ref[d, b])
                loss = term if loss is None else loss + term
        outs[4 * nw][...] = loss

        def update(name, at, g):
            g_out[name][at] = g
            d_out[name][at], m_out[name][at], v_out[name][at] = _adamw_math(
                w_refs[name][at], g, m_refs[name][at], v_refs[name][at])

        for l in range(DEPTH):
            mine = dict(zip(SMALL_ARRAYS, own_refs[l * na:(l + 1) * na]))
            got = dict(zip(SMALL_ARRAYS, g_refs[l * na:(l + 1) * na]))

            def total(key, at):
                acc = None
                for d in range(N_DEV):
                    term = jnp.where(me == d, mine[key][at] if at else mine[key][...], got[key][(d,) + at])
                    acc = term if acc is None else acc + term
                return acc

            row = (slice(l, l + 1),)
            for k, name in enumerate(NORM_NAMES):
                update(name, row, total("norms", (slice(k, k + 1),)))
            for k, name in enumerate(VEC_NAMES):
                update(name, row, total("vecs", (slice(k, k + 1),)))
            update("gmlp_v_gain", (l,), total("gain_bias", (slice(0, NH),)))
            update("b_spatial", (l,), total("gain_bias", (slice(NH, 2 * NH),)))
            update("w_spatial", (l,), total("w_spatial", ()))
            update("w_pool", (l,), total("w_pool", ()))
            update("w_dw", (l,), total("w_dw", (slice(0, CONV_K),)))

    args = [src[n] for src in (wts, mom_m, mom_v) for n in names]
    args += [src[l][k] for src in (own, gathered) for l in range(DEPTH) for k in SMALL_ARRAYS]
    args += [loss_own, loss_gathered]
    outs = pl.pallas_call(
        body, name="adamw_small",
        in_specs=[pl.BlockSpec(memory_space=pltpu.SMEM)] + [pl.BlockSpec(memory_space=pltpu.VMEM)] * len(args),
        out_shape=[jax.ShapeDtypeStruct(wts[n].shape, F32) for _ in range(4) for n in names]
        + [jax.ShapeDtypeStruct((8, LANES), F32)],
        compiler_params=_cparams(),
    )(dev, *args)
    return tuple(dict(zip(names, outs[i * nw:(i + 1) * nw])) for i in range(4)) + (outs[4 * nw],)


def _adamw_layers(name, w, reduced, m, v, chip, tr, transposed=False, after=()):
    nl, r, cdim = w.shape
    tr = _row_tile(r, tr)
    nb = r // tr

    def body(q_ref, w_ref, p0_ref, g0_ref, p1_ref, g1_ref, m_ref, v_ref, *rest):
        g_ref, d_ref, nm_ref, nv_ref = rest[len(after):]

        def total(p_ref, got_ref):
            acc = p_ref[...].astype(F32)
            for j in range(3):
                acc = acc + got_ref[j].astype(F32)
            return acc

        g = jnp.where(pl.program_id(0) == 0, total(p0_ref, g0_ref), total(p1_ref, g1_ref))
        if transposed:
            g = g.T
        g_ref[...] = g
        d_ref[...], nm_ref[...], nv_ref[...] = _adamw_math(w_ref[...], g, m_ref[...], v_ref[...])

    blk = pl.BlockSpec((None, tr, cdim), lambda l, i, q: (l, i, 0))
    first = lambda l, i: i * (1 - l) + (nb - 1) * l
    second = lambda l, i: i * l
    if transposed:
        gshape = (cdim, tr)
        at = lambda lead, i: (lead, 0, i)
    else:
        gshape = (tr, cdim)
        at = lambda lead, i: (lead, i, 0)
    specs = [blk,
             pl.BlockSpec((None,) + gshape, lambda l, i, q: at(q[0], first(l, i))),
             pl.BlockSpec((3,) + gshape, lambda l, i, q: at(0, first(l, i))),
             pl.BlockSpec((None,) + gshape, lambda l, i, q: at(q[0], second(l, i))),
             pl.BlockSpec((3,) + gshape, lambda l, i, q: at(0, second(l, i))), blk, blk] + [ANY] * len(after)
    shape = jax.ShapeDtypeStruct((nl, r, cdim), F32)
    return pl.pallas_call(
        body, name=name,
        grid_spec=pltpu.PrefetchScalarGridSpec(num_scalar_prefetch=1, grid=(nl, nb), in_specs=specs, out_specs=[blk] * 4),
        out_shape=[shape] * 4, compiler_params=_cparams(),
    )(chip, w, *reduced[0], *reduced[1], m, v, *after)


def _to_rows(name, a):
    return jnp.swapaxes(a, 1, 2) if name == "w_in" else a


def _place_own_transposed(name, srcs, dev, out_dtype, tc):
    n = len(srcs)
    kdim, cdim = srcs[0][0].shape[-2:]

    def body(dev_ref, *refs):
        for a in range(n):
            refs[n + a][...] = refs[a][...].T.astype(out_dtype)

    return pl.pallas_call(
        body, name=name,
        grid_spec=pltpu.PrefetchScalarGridSpec(
            num_scalar_prefetch=1, grid=(kdim // tc,),
            in_specs=[pl.BlockSpec((None, tc, cdim), lambda i, d, l=l: (l, i, 0)) for _, l in srcs],
            out_specs=[pl.BlockSpec((cdim, tc), lambda i, d: (d[0], i))] * n),
        out_shape=[jax.ShapeDtypeStruct((N_DEV * cdim, kdim), out_dtype)] * n, compiler_params=_cparams(),
    )(dev, *[a for a, _ in srcs])


def _pack(arrays, rows):
    flat = jnp.concatenate([a.reshape(-1) for a in arrays])
    return jnp.pad(flat, (0, rows * D - flat.shape[0])).reshape(rows, D)


def _rows_for(shapes, mult=8):
    total = 0
    for shp in shapes:
        size = 1
        for dim in shp:
            size *= dim
        total += size
    return -(-total // (mult * D)) * mult


def kernel(x, mem, norm_mix_pre, norm_mix_post, w_in, w_out, gmlp_v_gain, w_spatial, b_spatial, w_pool, s_pool, w_dw, b_dw, conv_ln_g, conv_ln_b, norm_xattn_pre, norm_mem, norm_xattn_post, w_q, w_k, w_v, w_o, norm_ffn_pre, norm_ffn_post, w_up, w_down, loss_target, m_norm_mix_pre, m_norm_mix_post, m_w_in, m_w_out, m_gmlp_v_gain, m_w_spatial, m_b_spatial, m_w_pool, m_s_pool, m_w_dw, m_b_dw, m_conv_ln_g, m_conv_ln_b, m_norm_xattn_pre, m_norm_mem, m_norm_xattn_post, m_w_q, m_w_k, m_w_v, m_w_o, m_norm_ffn_pre, m_norm_ffn_post, m_w_up, m_w_down, v_norm_mix_pre, v_norm_mix_post, v_w_in, v_w_out, v_gmlp_v_gain, v_w_spatial, v_b_spatial, v_w_pool, v_s_pool, v_w_dw, v_b_dw, v_conv_ln_g, v_conv_ln_b, v_norm_xattn_pre, v_norm_mem, v_norm_xattn_post, v_w_q, v_w_k, v_w_v, v_w_o, v_norm_ffn_pre, v_norm_ffn_post, v_w_up, v_w_down):
    args = dict(locals())
    wts = {n: args[n] for n in WEIGHTS}
    mom_m = {n: args["m_" + n] for n in WEIGHTS}
    mom_v = {n: args["v_" + n] for n in WEIGHTS}
    xi, yi, ci = _position()
    me = 4 * xi + 2 * yi + ci

    dev = jnp.reshape(me, (1,)).astype(jnp.int32)
    lands = {}
    for call, names, tr in (("place_in", ("w_in",), 256), ("place_att", ("w_out", "w_q", "w_k", "w_v", "w_o"), 64),
                            ("place_up", ("w_up",), 256), ("place_down", ("w_down",), 256)):
        srcs = [(_to_rows(n, wts[n]), l) for l in range(DEPTH) for n in names]
        placed = (_place_own_transposed if names == ("w_up",) else _place_own)(call, srcs, dev, BF16, tr)
        lands.update(zip([(l, n) for l in range(DEPTH) for n in names], placed))
    (lands[(0, "taps")],) = _place_own("place_taps", [(_pack([w_dw], _rows_for([w_dw.shape])), None)], dev, F32, 8)
    groups = []
    for l in range(DEPTH):
        for group, names in GATHER_GROUPS:
            if (l, group) == (0, "in"):
                names = names + ("taps",)
            groups.append(((l, group), names, [lands[(l, n)] for n in names]))
    gather = _WeightGather(groups)

    def fetch(layer, group, marker):
        w = gather.fetch(layer, group, marker)
        if "taps" in w:
            blocks = w["taps"].reshape(N_DEV, -1)[:, :w_dw.size].reshape((N_DEV,) + w_dw.shape)
            w["taps"] = jnp.moveaxis(blocks, 0, 2).reshape(DEPTH, CONV_K, CW)
        return w

    reduce = _GradReduce(jnp.reshape(ci, (1,)).astype(jnp.int32), jnp.reshape(2 * xi + yi, (1,)).astype(jnp.int32))
    small = {n: wts[n] for n in SMALL if n != "w_dw"}
    _, dx = _local_step(x[0], mem[0], loss_target[0], fetch, small, reduce)
    reduce.advance((dx,))

    grad_w, delta, new_m, new_v = {}, {}, {}, {}
    marker = (dx,) + tuple(reduce.after())
    for n in UPDATE_ORDER:
        reduced = [reduce.collect((l, n), marker) for l in range(DEPTH)]
        outs = _adamw_layers("adamw_" + n, _to_rows(n, wts[n]), reduced, _to_rows(n, mom_m[n]), _to_rows(n, mom_v[n]),
                             reduce.chip, 256, transposed=n == "w_up", after=marker)
        grad_w[n], delta[n], new_m[n], new_v[n] = (_to_rows(n, o) for o in outs)
        marker = (outs[1],)

    own, slots = [None] * DEPTH, [None] * DEPTH
    for l in reversed(range(DEPTH)):
        own[l], slots[l] = reduce.small_finish(l, marker)
        if l == 0:
            loss_own, loss_slots = own[l].pop("loss"), slots[l].pop("loss")
    shard_cols = CW // N_DEV
    for l in range(DEPTH):
        own[l]["w_dw"] = lax.dynamic_slice_in_dim(own[l]["w_dw"], me * shard_cols, shard_cols, axis=1)
        slots[l]["w_dw"] = lax.dynamic_slice_in_dim(slots[l]["w_dw"], me * shard_cols, shard_cols, axis=2)
    *small_out, loss_tile = _adamw_small(wts, mom_m, mom_v, own, slots, loss_own, loss_slots, dev)
    for dst, src in zip((grad_w, delta, new_m, new_v), small_out):
        dst.update(src)

    return (loss_tile[0, 0], dx[None], *[grad_w[n] for n in WEIGHTS], *[delta[n] for n in WEIGHTS],
            *[new_m[n] for n in WEIGHTS], *[new_v[n] for n in WEIGHTS])
```

```python
import functools

import jax
import jax.numpy as jnp
from jax import lax
from jax.experimental import pallas as pl
from jax.experimental.pallas import tpu as pltpu

F32 = jnp.float32
BF16 = jnp.bfloat16

D = 2048
GW = 1024
PW = 512
CW = 512
HD = 128
NH = 8
NG = 4
POOL_WINDOWS = (2, 4, 8, 16)
CONV_K = 31
IN_COLS = 2 * GW + PW + 2 * CW
XH = 4
XHD = D // XH
ATT_SCALE = XHD ** -0.5
RMS_EPS = 1e-6
LN_EPS = 1e-5
DEPTH = 2
N_DEV = 8

ADAM_LR = 0.001
ADAM_B1 = 0.9
ADAM_B2 = 0.999
ADAM_EPS = 1e-08
ADAM_WD = 0.01
ADAM_STEP = 10

LANES = 128
CONV_HALO = 32
POOL_HALO = 16
ROW_TILE = 128
VMEM_LIMIT = 60 * 1024 * 1024

MESH = pl.DeviceIdType.MESH
NT = (((1,), (1,)), ((), ()))
NN = (((1,), (0,)), ((), ()))
TN = (((0,), (0,)), ((), ()))

UPDATE_ORDER = ("w_down", "w_up", "w_o", "w_q", "w_k", "w_v", "w_out", "w_in")
GATHER_GROUPS = (("in", ("w_in",)), ("out", ("w_out",)), ("att", ("w_q", "w_k", "w_v", "w_o")), ("up", ("w_up",)),
                 ("down", ("w_down",)))
SMALL = ("norm_mix_pre", "norm_mix_post", "gmlp_v_gain", "w_spatial", "b_spatial", "w_pool", "s_pool",
         "w_dw", "b_dw", "conv_ln_g", "conv_ln_b", "norm_xattn_pre", "norm_mem", "norm_xattn_post",
         "norm_ffn_pre", "norm_ffn_post")
WEIGHTS = ("norm_mix_pre", "norm_mix_post", "w_in", "w_out", "gmlp_v_gain", "w_spatial", "b_spatial", "w_pool",
           "s_pool", "w_dw", "b_dw", "conv_ln_g", "conv_ln_b", "norm_xattn_pre", "norm_mem", "norm_xattn_post",
           "w_q", "w_k", "w_v", "w_o", "norm_ffn_pre", "norm_ffn_post", "w_up", "w_down")


def _cparams():
    return pltpu.CompilerParams(vmem_limit_bytes=VMEM_LIMIT)


def _dot(a, b, dims):
    return lax.dot_general(a, b, dims, preferred_element_type=F32)


def _rms(x, g):
    y = x * lax.rsqrt(jnp.mean(x * x, axis=-1, keepdims=True) + RMS_EPS)
    return y * g


def _rms_bwd(x, g, dy):
    r = lax.rsqrt(jnp.mean(x * x, axis=-1, keepdims=True) + RMS_EPS)
    xh = x * r
    t = dy * g
    dx = r * (t - xh * jnp.mean(t * xh, axis=-1, keepdims=True))
    return dx, jnp.sum(dy * xh, axis=0, keepdims=True)


def _gelu(x):
    cdf = 0.5 * (1.0 + jnp.tanh(0.7978845608028654 * (x + 0.044715 * (x * x * x))))
    return x * cdf


def _layer_norm(x, g, b=None):
    mu = jnp.mean(x, axis=-1, keepdims=True)
    xc = x - mu
    var = jnp.mean(xc * xc, axis=-1, keepdims=True)
    y = xc * lax.rsqrt(var + LN_EPS) * g
    return y if b is None else y + b


def _sigmoid(x):
    return 1.0 / (1.0 + jnp.exp(-x))


def _gmlp_rows(zu, zv, gv):
    return _gelu(zu), _layer_norm(_gelu(zv), gv)


def _glu(cv, cg):
    return cv * _sigmoid(cg)


def _ln_silu(h, g, b):
    y = _layer_norm(h, g, b)
    return y * _sigmoid(y)


ANY = pl.BlockSpec(memory_space=pl.ANY)


ROWS_TILE = 256
COLS_TILE = 512
DW_TILE = 512
RESIDENT_K = 2048
STREAM_K_TILE = 1024
STREAM_ROWS = 512


def _k_tiles(kdim):
    if kdim <= RESIDENT_K:
        return ROWS_TILE, kdim
    return STREAM_ROWS, max(t for t in range(LANES, STREAM_K_TILE + 1, LANES) if kdim % t == 0)


def _rowop_mm(name, kind, rows, g, w, dims, out_dtype, u=None, after=()):
    s = rows[0].shape[0]
    n = w.shape[0] if dims == NT else w.shape[1]
    tm, tn = min(ROWS_TILE, s), min(COLS_TILE, n)
    ni, nj = s // tm, n // tn
    bwd = kind == "rms_bwd"

    def body(*refs):
        refs = list(refs)
        row_refs = [refs.pop(0) for _ in rows]
        g_ref, w_ref = refs.pop(0), refs.pop(0)
        u_ref = refs.pop(0) if u is not None else None
        del refs[:len(after)]
        out_ref, a_ref = refs.pop(0), refs.pop(0)
        dg_ref = refs.pop(0) if bwd else None
        a_all = refs.pop(0)
        t = pl.program_id(0)

        @pl.when(t < ni)
        def _():
            if bwd:
                a, dg = _rms_bwd(row_refs[0][...], g_ref[...], row_refs[1][...])
                dg_ref[0] = dg
            else:
                a = _rms(row_refs[0][...], g_ref[...])
            a_ref[...] = a.astype(BF16)
            a_all[pl.ds(pl.multiple_of(t * tm, tm), tm), :] = a.astype(BF16)

        @pl.when(t >= ni)
        def _():
            acc = _dot(a_all[...], w_ref[...], dims)
            if u_ref is not None:
                acc = acc * (2.0 * jnp.maximum(u_ref[...], 0.0))
            out_ref[...] = acc.astype(out_dtype)

    rows_at = lambda t: jnp.minimum(t, ni - 1)
    cols_at = lambda t: jnp.maximum(t - ni, 0)
    row_spec = pl.BlockSpec((tm, D), lambda t: (rows_at(t), 0))
    w_spec = (pl.BlockSpec((tn, D), lambda t: (cols_at(t), 0)) if dims == NT
              else pl.BlockSpec((D, tn), lambda t: (0, cols_at(t))))
    tile = pl.BlockSpec((s, tn), lambda t: (0, cols_at(t)))
    in_specs = [row_spec] * len(rows) + [pl.BlockSpec((1, D), lambda t: (0, 0)), w_spec]
    in_specs += ([tile] if u is not None else []) + [ANY] * len(after)
    out_shape = [jax.ShapeDtypeStruct((s, n), out_dtype), jax.ShapeDtypeStruct((s, D), BF16)]
    out_specs = [tile, row_spec]
    if bwd:
        out_shape.append(jax.ShapeDtypeStruct((ni, 1, D), F32))
        out_specs.append(pl.BlockSpec((1, 1, D), lambda t: (rows_at(t), 0, 0)))
    return pl.pallas_call(
        body, name=name, grid=(ni + nj,), in_specs=in_specs, out_specs=out_specs, out_shape=out_shape,
        scratch_shapes=[pltpu.VMEM((s, D), BF16)], compiler_params=_cparams(),
    )(*rows, g, w, *([u] if u is not None else []), *after)


def _mm_rowop(name, kind, pairs, rows, g, relu2=False, after=()):
    s, kdim = pairs[0][0].shape
    tm, tk = _k_tiles(kdim)
    tm = min(tm, s)
    ni, nk = s // tm, kdim // tk
    npair = len(pairs)

    def body(*refs):
        refs = list(refs)
        a_refs = [refs.pop(0) for _ in range(npair)]
        w_refs = [refs.pop(0) for _ in range(npair)]
        row_refs = [refs.pop(0) for _ in rows]
        g_ref = refs.pop(0)
        del refs[:len(after)]
        acc = refs.pop()
        outs = refs
        k = pl.program_id(1)

        @pl.when(k == 0)
        def _():
            acc[...] = jnp.zeros_like(acc)

        for a_ref, w_ref, (_, _, dims) in zip(a_refs, w_refs, pairs):
            a = a_ref[...]
            if relu2:
                a = jnp.square(jnp.maximum(a, 0.0))
            acc[...] += _dot(a.astype(BF16), w_ref[...], dims)

        @pl.when(k == nk - 1)
        def _():
            h = acc[...]
            if kind == "rms_res":
                outs[0][...] = row_refs[0][...] + _rms(h, g_ref[...])
                outs[1][...] = h
            else:
                dx, dg = _rms_bwd(row_refs[0][...], g_ref[...], h)
                if kind == "rms_bwd_res":
                    outs[0][...] = row_refs[1][...] + dx
                    outs[1][0] = dg
                else:
                    outs[0][0] = dg

    row_spec = pl.BlockSpec((tm, D), lambda i, k: (i, 0))
    dg_shape = jax.ShapeDtypeStruct((ni, 1, D), F32)
    dg_spec = pl.BlockSpec((1, 1, D), lambda i, k: (i, 0, 0))
    in_specs = [pl.BlockSpec((tm, tk), lambda i, k: (i, k))] * npair
    for _, _, dims in pairs:
        in_specs.append(pl.BlockSpec((tk, D), lambda i, k: (k, 0)) if dims == NN
                        else pl.BlockSpec((D, tk), lambda i, k: (0, k)))
    in_specs += [row_spec] * len(rows) + [pl.BlockSpec((1, D), lambda i, k: (0, 0))] + [ANY] * len(after)
    if kind == "rms_res":
        out_shape = [jax.ShapeDtypeStruct((s, D), F32)] * 2
        out_specs = [row_spec, row_spec]
    elif kind == "rms_bwd_res":
        out_shape = [jax.ShapeDtypeStruct((s, D), F32), dg_shape]
        out_specs = [row_spec, dg_spec]
    else:
        out_shape = [dg_shape]
        out_specs = [dg_spec]
    return pl.pallas_call(
        body, name=name, grid=(ni, nk), in_specs=in_specs, out_specs=out_specs, out_shape=out_shape,
        scratch_shapes=[pltpu.VMEM((tm, D), F32)], compiler_params=_cparams(),
    )(*[p[0] for p in pairs], *[p[1] for p in pairs], *rows, g, *after)


def _mm_tn(name, a, gmat, relu2=False, after=()):
    s, m = a.shape
    tm = min(DW_TILE, m)
    ni = m // tm

    def body(a_ref, g_ref, *rest):
        av = a_ref[...]
        if relu2:
            av = jnp.square(jnp.maximum(av, 0.0))
        rest[len(after)][...] = _dot(av.astype(BF16), g_ref[...], TN).astype(BF16)

    return pl.pallas_call(
        body, name=name, grid=(ni,),
        in_specs=[pl.BlockSpec((s, tm), lambda i: (0, i)), pl.BlockSpec((s, D), lambda i: (0, 0))] + [ANY] * len(after),
        out_specs=pl.BlockSpec((tm, D), lambda i: (i, 0)),
        out_shape=jax.ShapeDtypeStruct((m, D), BF16), compiler_params=_cparams(),
    )(a, gmat, *after)


def _tril():
    r = lax.broadcasted_iota(jnp.int32, (HD, HD), 0)
    c = lax.broadcasted_iota(jnp.int32, (HD, HD), 1)
    return (c <= r).astype(F32)


def _gmlp_fwd(z, gv, ws, bst, tb):
    s = z.shape[0]
    tb = min(tb, s)

    def body(zu_ref, zv_ref, gv_ref, ws_ref, bst_ref, y_ref):
        tril = _tril()
        for h in range(NH):
            cols = slice(h * HD, (h + 1) * HD)
            u, vln = _gmlp_rows(zu_ref[:, cols], zv_ref[:, cols], gv_ref[h:h + 1, :])
            wm = (ws_ref[h] * tril).astype(BF16)
            vb = vln.astype(BF16)
            for c in range(tb // HD):
                rws = slice(c * HD, (c + 1) * HD)
                mixed = _dot(wm, vb[rws], NN) + bst_ref[:, h:h + 1]
                y_ref[rws, cols] = (u[rws] * mixed).astype(BF16)

    return pl.pallas_call(
        body, name="gmlp_fwd", grid=(s // tb,),
        in_specs=[pl.BlockSpec((tb, GW), lambda i: (i, 0)), pl.BlockSpec((tb, GW), lambda i: (i, 1)),
                  pl.BlockSpec((NH, HD), lambda i: (0, 0)), pl.BlockSpec((NH, HD, HD), lambda i: (0, 0, 0)),
                  pl.BlockSpec((HD, NH), lambda i: (0, 0))],
        out_specs=pl.BlockSpec((tb, GW), lambda i: (i, 0)),
        out_shape=jax.ShapeDtypeStruct((s, D), BF16), compiler_params=_cparams(),
    )(z, z, gv, ws, bst)


def _gmlp_bwd(z, dy, gv, ws, bst, tb, after=()):
    s = z.shape[0]
    tb = min(tb, s)
    nb = s // tb

    def body(zu_ref, zv_ref, dy_ref, gv_ref, ws_ref, bst_ref, *rest):
        dz_ref, dgv_ref, dws_ref, db_ref = rest[len(after):]
        tril = _tril()
        for h in range(NH):
            cols = slice(h * HD, (h + 1) * HD)
            (u, vln), vjp = jax.vjp(_gmlp_rows, zu_ref[:, cols], zv_ref[:, cols], gv_ref[h:h + 1, :])
            wmf = ws_ref[h] * tril
            wm = wmf.astype(BF16)
            wmt = wmf.T.astype(BF16)
            vb = vln.astype(BF16)
            dws = jnp.zeros((HD, HD), F32)
            db = jnp.zeros((HD, 1), F32)
            du_parts, dvln_parts = [], []
            for c in range(tb // HD):
                rws = slice(c * HD, (c + 1) * HD)
                mixed = _dot(wm, vb[rws], NN) + bst_ref[:, h:h + 1]
                dyc = dy_ref[rws, cols]
                du_parts.append(dyc * mixed)
                dmixed = dyc * u[rws]
                dmb = dmixed.astype(BF16)
                dws = dws + _dot(dmb, vb[rws], NT)
                db = db + jnp.sum(dmixed, axis=1, keepdims=True)
                dvln_parts.append(_dot(wmt, dmb, NN))
            du = jnp.concatenate(du_parts, axis=0)
            dvln = jnp.concatenate(dvln_parts, axis=0)
            dzu, dzv, dgv = vjp((du, dvln))
            dz_ref[:, cols] = dzu.astype(BF16)
            dz_ref[:, slice(GW + h * HD, GW + (h + 1) * HD)] = dzv.astype(BF16)
            dgv_ref[0, h:h + 1, :] = dgv
            dws_ref[0, h] = dws * tril
            db_ref[0, h] = jnp.broadcast_to(db, (HD, LANES))

    blk = pl.BlockSpec((tb, GW), lambda i: (i, 0))
    return pl.pallas_call(
        body, name="gmlp_bwd", grid=(nb,),
        in_specs=[blk, pl.BlockSpec((tb, GW), lambda i: (i, 1)), blk,
                  pl.BlockSpec((NH, HD), lambda i: (0, 0)), pl.BlockSpec((NH, HD, HD), lambda i: (0, 0, 0)),
                  pl.BlockSpec((HD, NH), lambda i: (0, 0))] + [ANY] * len(after),
        out_specs=[pl.BlockSpec((tb, 2 * GW), lambda i: (i, 0)), pl.BlockSpec((1, NH, HD), lambda i: (i, 0, 0)),
                   pl.BlockSpec((1, NH, HD, HD), lambda i: (i, 0, 0, 0)),
                   pl.BlockSpec((1, NH, HD, LANES), lambda i: (i, 0, 0, 0))],
        out_shape=[jax.ShapeDtypeStruct((s, IN_COLS), BF16),
                   jax.ShapeDtypeStruct((nb, NH, HD), F32), jax.ShapeDtypeStruct((nb, NH, HD, HD), F32),
                   jax.ShapeDtypeStruct((nb, NH, HD, LANES), F32)],
        compiler_params=_cparams(),
    )(z, z, dy, gv, ws, bst, *after)


POOL_TILE = 1024


def _pool_count(t0, window):
    pos = (t0 + lax.broadcasted_iota(jnp.int32, (POOL_TILE, LANES), 0)).astype(F32)
    return jnp.minimum(pos + 1.0, float(window))


def _window_sum(win, levels, back):
    n = win.shape[0]
    for lv in range(levels):
        step = 1 << lv
        win = win + pltpu.roll(win, n - step if back else step, 0)
    return win


def _pool_pooled(ppad_ref, t0, g):
    win = ppad_ref[pl.ds(t0, POOL_TILE + POOL_HALO), :]
    wsum = _window_sum(win, g + 1, False)[POOL_HALO:]
    return wsum / _pool_count(t0, POOL_WINDOWS[g]) - win[POOL_HALO:]


def _pool_fwd(z, wp, sp, y):
    s = z.shape[0]
    nt = s // POOL_TILE

    def body(p_ref, wp_ref, sp_ref, _, y_ref, ppad):
        for g in range(NG):
            cols = slice(g * LANES, (g + 1) * LANES)
            ppad[pl.ds(0, POOL_HALO), :] = jnp.zeros((POOL_HALO, LANES), F32)
            ppad[pl.ds(POOL_HALO, s), :] = p_ref[:, cols]
            wpb = wp_ref[g].astype(BF16)
            scale = sp_ref[:, cols]

            def tile(t, carry):
                t0 = pl.multiple_of(t * POOL_TILE, POOL_TILE)
                pooled = _pool_pooled(ppad, t0, g)
                y_ref[pl.ds(t0, POOL_TILE), cols] = (_dot(pooled.astype(BF16), wpb, NN) * scale).astype(BF16)
                return carry

            lax.fori_loop(0, nt, tile, 0)

    return pl.pallas_call(
        body, name="pool_fwd", grid=(1,),
        in_specs=[pl.BlockSpec((s, PW), lambda i: (0, 2 * GW // PW)),
                  pl.BlockSpec((NG, LANES, LANES), lambda i: (0, 0, 0)), pl.BlockSpec((1, PW), lambda i: (0, 0)), ANY],
        out_specs=pl.BlockSpec((s, PW), lambda i: (0, GW // PW)),
        out_shape=jax.ShapeDtypeStruct((s, D), BF16), input_output_aliases={3: 0},
        scratch_shapes=[pltpu.VMEM((s + POOL_HALO, LANES), F32)], compiler_params=_cparams(),
    )(z, wp, sp, y)


def _pool_bwd(z, dy, wp, sp, dz):
    s = z.shape[0]
    nt = s // POOL_TILE

    def body(p_ref, dy_ref, wp_ref, sp_ref, _, dp_ref, dwp_ref, dsp_ref, ppad, rpad, dpool):
        for g in range(NG):
            cols = slice(g * LANES, (g + 1) * LANES)
            ppad[pl.ds(0, POOL_HALO), :] = jnp.zeros((POOL_HALO, LANES), F32)
            ppad[pl.ds(POOL_HALO, s), :] = p_ref[:, cols]
            rpad[pl.ds(s, POOL_HALO), :] = jnp.zeros((POOL_HALO, LANES), F32)
            wpb = wp_ref[g].astype(BF16)
            scale = sp_ref[:, cols]

            def tile(t, carry):
                dwp, dsp = carry
                t0 = pl.multiple_of(t * POOL_TILE, POOL_TILE)
                pooled = _pool_pooled(ppad, t0, g)
                pb = pooled.astype(BF16)
                dyt = dy_ref[pl.ds(t0, POOL_TILE), cols]
                dsp = dsp + jnp.sum(dyt * _dot(pb, wpb, NN), axis=0, keepdims=True)
                dmm = (dyt * scale).astype(BF16)
                dwp = dwp + _dot(pb, dmm, TN)
                dpooled = _dot(dmm, wpb, NT)
                rpad[pl.ds(t0, POOL_TILE), :] = dpooled / _pool_count(t0, POOL_WINDOWS[g])
                dpool[pl.ds(t0, POOL_TILE), :] = dpooled
                return dwp, dsp

            dwp, dsp = lax.fori_loop(0, nt, tile, (jnp.zeros((LANES, LANES), F32), jnp.zeros((1, LANES), F32)))
            dwp_ref[g] = dwp
            dsp_ref[:, cols] = dsp

            def tile2(t, carry):
                t0 = pl.multiple_of(t * POOL_TILE, POOL_TILE)
                win = rpad[pl.ds(t0, POOL_TILE + POOL_HALO), :]
                back = _window_sum(win, g + 1, True)[:POOL_TILE]
                rows = pl.ds(t0, POOL_TILE)
                dp_ref[rows, cols] = (back - dpool[rows, :]).astype(BF16)
                return carry

            lax.fori_loop(0, nt, tile2, 0)

    return pl.pallas_call(
        body, name="pool_bwd", grid=(1,),
        in_specs=[pl.BlockSpec((s, PW), lambda i: (0, 2 * GW // PW)), pl.BlockSpec((s, PW), lambda i: (0, GW // PW)),
                  pl.BlockSpec((NG, LANES, LANES), lambda i: (0, 0, 0)), pl.BlockSpec((1, PW), lambda i: (0, 0)), ANY],
        out_specs=[pl.BlockSpec((s, PW), lambda i: (0, 2 * GW // PW)),
                   pl.BlockSpec((NG, LANES, LANES), lambda i: (0, 0, 0)), pl.BlockSpec((1, PW), lambda i: (0, 0))],
        out_shape=[jax.ShapeDtypeStruct((s, IN_COLS), BF16), jax.ShapeDtypeStruct((NG, LANES, LANES), F32),
                   jax.ShapeDtypeStruct((1, PW), F32)],
        input_output_aliases={4: 0},
        scratch_shapes=[pltpu.VMEM((s + POOL_HALO, LANES), F32), pltpu.VMEM((s + POOL_HALO, LANES), F32),
                        pltpu.VMEM((s, LANES), F32)],
        compiler_params=_cparams(),
    )(z, dy, wp, sp, dz)


CONV_LEAD = CONV_HALO - (CONV_K - 1)


SUBLANES = 8


def _sublane_shifts(win):
    n = win.shape[0]
    return [win] + [pltpu.roll(win, n - b, 0) for b in range(1, SUBLANES)]


def _shifted(shifts, offset):
    a, b = divmod(offset, SUBLANES)
    return shifts[b][a * SUBLANES:a * SUBLANES + ROW_TILE]


def _conv_taps(shifts, wdw_ref, lead, reverse):
    acc = jnp.zeros((ROW_TILE, CW), F32)
    for j in range(CONV_K):
        tap = (CONV_K - 1 - j) if reverse else j
        acc = acc + wdw_ref[tap:tap + 1, :] * _shifted(shifts, lead + j)
    return acc


def _conv_fill_glu(cv_ref, cg_ref, xpad, s):
    xpad[pl.ds(0, CONV_HALO), :] = jnp.zeros((CONV_HALO, CW), F32)

    def fill(t, carry):
        t0 = pl.multiple_of(t * ROW_TILE, ROW_TILE)
        rows = pl.ds(t0, ROW_TILE)
        xpad[pl.ds(t0 + CONV_HALO, ROW_TILE), :] = _glu(cv_ref[rows, :], cg_ref[rows, :])
        return carry

    lax.fori_loop(0, s // ROW_TILE, fill, 0)


def _conv_fwd(z, wdw, bdw, lng, lnb, y):
    s = z.shape[0]

    def body(cv_ref, cg_ref, wdw_ref, bdw_ref, lng_ref, lnb_ref, _, y_ref, xpad):
        _conv_fill_glu(cv_ref, cg_ref, xpad, s)

        def tile(t, carry):
            t0 = pl.multiple_of(t * ROW_TILE, ROW_TILE)
            shifts = _sublane_shifts(xpad[pl.ds(t0, ROW_TILE + CONV_HALO), :])
            hc = _conv_taps(shifts, wdw_ref, CONV_LEAD, False) + bdw_ref[...]
            y_ref[pl.ds(t0, ROW_TILE), :] = _ln_silu(hc, lng_ref[...], lnb_ref[...]).astype(BF16)
            return carry

        lax.fori_loop(0, s // ROW_TILE, tile, 0)

    vec = pl.BlockSpec((1, CW), lambda i: (0, 0))
    return pl.pallas_call(
        body, name="conv_fwd", grid=(1,),
        in_specs=[pl.BlockSpec((s, CW), lambda i: (0, (2 * GW + PW) // CW)),
                  pl.BlockSpec((s, CW), lambda i: (0, (2 * GW + PW) // CW + 1)),
                  pl.BlockSpec((CONV_K + 1, CW), lambda i: (0, 0)), vec, vec, vec, ANY],
        out_specs=pl.BlockSpec((s, CW), lambda i: (0, (GW + PW) // CW)),
        out_shape=jax.ShapeDtypeStruct((s, D), BF16), input_output_aliases={6: 0},
        scratch_shapes=[pltpu.VMEM((s + CONV_HALO, CW), F32)], compiler_params=_cparams(),
    )(z, z, wdw, bdw, lng, lnb, y)


def _conv_bwd(z, dy, wdw, bdw, lng, lnb, dz):
    s = z.shape[0]

    def body(cv_ref, cg_ref, dy_ref, wdw_ref, bdw_ref, lng_ref, lnb_ref, _,
             dz_ref, dwdw_ref, dbdw_ref, dlng_ref, dlnb_ref, xpad, dpad, dcg_keep):
        @pl.when(pl.program_id(0) == 0)
        def _():
            compute(cv_ref, cg_ref, dy_ref, wdw_ref, bdw_ref, lng_ref, lnb_ref,
                    dz_ref, dcg_keep, dwdw_ref, dbdw_ref, dlng_ref, dlnb_ref, xpad, dpad)

        @pl.when(pl.program_id(0) == 1)
        def _():
            dz_ref[...] = dcg_keep[...]

    def compute(cv_ref, cg_ref, dy_ref, wdw_ref, bdw_ref, lng_ref, lnb_ref,
                dcv_ref, dcg_ref, dwdw_ref, dbdw_ref, dlng_ref, dlnb_ref, xpad, dpad):
        _conv_fill_glu(cv_ref, cg_ref, xpad, s)
        dpad[pl.ds(s, CONV_HALO), :] = jnp.zeros((CONV_HALO, CW), F32)
        dwdw_ref[...] = jnp.zeros((CONV_K + 1, CW), F32)

        def tile(t, carry):
            db, dg, dbeta = carry
            t0 = pl.multiple_of(t * ROW_TILE, ROW_TILE)
            shifts = _sublane_shifts(xpad[pl.ds(t0, ROW_TILE + CONV_HALO), :])
            hc = _conv_taps(shifts, wdw_ref, CONV_LEAD, False) + bdw_ref[...]
            _, vjp = jax.vjp(_ln_silu, hc, lng_ref[...], lnb_ref[...])
            dhc, dg_t, dbeta_t = vjp(dy_ref[pl.ds(t0, ROW_TILE), :])
            dpad[pl.ds(t0, ROW_TILE), :] = dhc
            for j in range(CONV_K):
                dwdw_ref[j:j + 1, :] += jnp.sum(dhc * _shifted(shifts, CONV_LEAD + j), axis=0, keepdims=True)
            return db + jnp.sum(dhc, axis=0, keepdims=True), dg + dg_t, dbeta + dbeta_t

        zero = jnp.zeros((1, CW), F32)
        db, dg, dbeta = lax.fori_loop(0, s // ROW_TILE, tile, (zero, zero, zero))
        dbdw_ref[...] = db
        dlng_ref[...] = dg
        dlnb_ref[...] = dbeta

        def tile2(t, carry):
            t0 = pl.multiple_of(t * ROW_TILE, ROW_TILE)
            rows = pl.ds(t0, ROW_TILE)
            dglu = _conv_taps(_sublane_shifts(dpad[pl.ds(t0, ROW_TILE + CONV_HALO), :]), wdw_ref, 0, True)
            _, vjp = jax.vjp(_glu, cv_ref[rows, :], cg_ref[rows, :])
            dcv, dcg = vjp(dglu)
            dcv_ref[rows, :] = dcv.astype(BF16)
            dcg_ref[rows, :] = dcg.astype(BF16)
            return carry

        lax.fori_loop(0, s // ROW_TILE, tile2, 0)

    vec = pl.BlockSpec((1, CW), lambda i: (0, 0))
    wspec = pl.BlockSpec((CONV_K + 1, CW), lambda i: (0, 0))
    vshape = jax.ShapeDtypeStruct((1, CW), F32)
    return pl.pallas_call(
        body, name="conv_bwd", grid=(2,),
        in_specs=[pl.BlockSpec((s, CW), lambda i: (0, (2 * GW + PW) // CW)),
                  pl.BlockSpec((s, CW), lambda i: (0, (2 * GW + PW) // CW + 1)),
                  pl.BlockSpec((s, CW), lambda i: (0, (GW + PW) // CW)), wspec, vec, vec, vec, ANY],
        out_specs=[pl.BlockSpec((s, CW), lambda i: (0, (2 * GW + PW) // CW + i)), wspec, vec, vec, vec],
        out_shape=[jax.ShapeDtypeStruct((s, IN_COLS), BF16), jax.ShapeDtypeStruct((CONV_K + 1, CW), F32),
                   vshape, vshape, vshape],
        input_output_aliases={7: 0},
        scratch_shapes=[pltpu.VMEM((s + CONV_HALO, CW), F32), pltpu.VMEM((s + CONV_HALO, CW), F32),
                        pltpu.VMEM((s, CW), BF16)],
        compiler_params=_cparams(),
    )(z, z, dy, wdw, bdw, lng, lnb, dz)


def _softmax_rows(sc):
    e = jnp.exp(sc - jnp.max(sc, axis=-1, keepdims=True))
    return e / jnp.sum(e, axis=-1, keepdims=True)


def _attn_fwd(q, k, v, tq):
    s, m = q.shape[0], k.shape[0]
    tq = min(tq, s)

    def body(q_ref, k_ref, v_ref, o_ref):
        for h in range(XH):
            cols = slice(h * XHD, (h + 1) * XHD)
            p = _softmax_rows(_dot(q_ref[:, cols], k_ref[:, cols], NT) * ATT_SCALE)
            o_ref[:, cols] = _dot(p.astype(BF16), v_ref[:, cols], NN).astype(BF16)

    kv = pl.BlockSpec((m, D), lambda i: (0, 0))
    return pl.pallas_call(
        body, name="attn_fwd", grid=(s // tq,),
        in_specs=[pl.BlockSpec((tq, D), lambda i: (i, 0)), kv, kv],
        out_specs=pl.BlockSpec((tq, D), lambda i: (i, 0)),
        out_shape=jax.ShapeDtypeStruct((s, D), BF16), compiler_params=_cparams(),
    )(q, k, v)


def _attn_bwd(q, k, v, do, tq, after=()):
    s, m = q.shape[0], k.shape[0]
    tq = min(tq, s)

    def body(q_ref, k_ref, v_ref, do_ref, *rest):
        dq_ref, dk_ref, dv_ref = rest[len(after):]

        @pl.when(pl.program_id(0) == 0)
        def _():
            dk_ref[...] = jnp.zeros_like(dk_ref)
            dv_ref[...] = jnp.zeros_like(dv_ref)

        for h in range(XH):
            cols = slice(h * XHD, (h + 1) * XHD)
            qh, kh, vh, doh = q_ref[:, cols], k_ref[:, cols], v_ref[:, cols], do_ref[:, cols]
            p = _softmax_rows(_dot(qh, kh, NT) * ATT_SCALE)
            dp = _dot(doh, vh, NT)
            dv_ref[:, cols] += _dot(p.astype(BF16), doh, TN)
            ds = (p * (dp - jnp.sum(p * dp, axis=-1, keepdims=True)) * ATT_SCALE).astype(BF16)
            dq_ref[:, cols] = _dot(ds, kh, NN).astype(BF16)
            dk_ref[:, cols] += _dot(ds, qh, TN)

    kv = pl.BlockSpec((m, D), lambda i: (0, 0))
    qs = pl.BlockSpec((tq, D), lambda i: (i, 0))
    return pl.pallas_call(
        body, name="attn_bwd", grid=(s // tq,),
        in_specs=[qs, kv, kv, qs] + [ANY] * len(after), out_specs=[qs, kv, kv],
        out_shape=[jax.ShapeDtypeStruct((s, D), BF16), jax.ShapeDtypeStruct((m, D), F32),
                   jax.ShapeDtypeStruct((m, D), F32)],
        compiler_params=_cparams(),
    )(q, k, v, do, *after)


def _loss_head(y, target, tm):
    s = y.shape[0]
    tm = min(tm, s)

    def body(y_ref, t_ref, dy_ref, part_ref):
        err = y_ref[...] - t_ref[...]
        dy_ref[...] = err * (1.0 / D)
        part_ref[...] = jnp.full((1, 8, LANES), 0.5 * jnp.sum(err * err) * (1.0 / D), F32)

    blk = pl.BlockSpec((tm, D), lambda i: (i, 0))
    return pl.pallas_call(
        body, name="loss_head", grid=(s // tm,), in_specs=[blk, blk],
        out_specs=[blk, pl.BlockSpec((1, 8, LANES), lambda i: (i, 0, 0))],
        out_shape=[jax.ShapeDtypeStruct((s, D), F32), jax.ShapeDtypeStruct((s // tm, 8, LANES), F32)],
        compiler_params=_cparams(),
    )(y, target)


def _layer_fwd(x0, mem, w, p, fetch):
    z, hn0 = _rowop_mm("mix_in", "rms", (x0,), p["norm_mix_pre"], w["w_in"], NT, F32)
    y = _gmlp_fwd(z, p["gmlp_v_gain"], p["w_spatial"], p["b_spatial_t"], 1024)
    y = _pool_fwd(z, p["w_pool"], p["s_pool"], y)
    y = _conv_fwd(z, p["w_dw"], p["b_dw"], p["conv_ln_g"], p["conv_ln_b"], y)
    w.update(fetch("out", (y,)))
    x1, h0 = _mm_rowop("mix_out", "rms_res", [(y, w["w_out"], NN)], (x0,), p["norm_mix_post"])
    w.update(fetch("att", (x1,)))
    q, hn1 = _rowop_mm("att_q", "rms", (x1,), p["norm_xattn_pre"], w["w_q"], NN, BF16)
    k, mn = _rowop_mm("att_k", "rms", (mem,), p["norm_mem"], w["w_k"], NN, BF16, after=(x1,))
    v, _ = _rowop_mm("att_v", "rms", (mem,), p["norm_mem"], w["w_v"], NN, BF16, after=(x1,))
    o = _attn_fwd(q, k, v, 1024)
    x2, h1 = _mm_rowop("att_o", "rms_res", [(o, w["w_o"], NN)], (x1,), p["norm_xattn_post"])
    w.update(fetch("up", (x2,)))
    u, hn2 = _rowop_mm("ffn_up", "rms", (x2,), p["norm_ffn_pre"], w["w_up"], NT, F32)
    w.update(fetch("down", (u,)))
    x3, h2 = _mm_rowop("ffn_down", "rms_res", [(u, w["w_down"], NN)], (x2,), p["norm_ffn_post"], relu2=True)
    saved = dict(x0=x0, z=z, hn0=hn0, y=y, h0=h0, x1=x1, q=q, hn1=hn1, k=k, v=v, mn=mn, o=o, h1=h1, x2=x2, u=u,
                 hn2=hn2, h2=h2)
    return x3, saved


def _layer_bwd(dx3, mem, w, p, sv, red):
    gs = {}
    du, dh2, dg = _rowop_mm("ffn_down_bwd", "rms_bwd", (sv["h2"], dx3), p["norm_ffn_post"], w["w_down"], NT, BF16,
                            u=sv["u"], after=red.after())
    gs["norm_ffn_post"] = jnp.sum(dg, axis=0)
    g_down = _mm_tn("ffn_down_dw", sv["u"], dh2, relu2=True)
    red.advance((g_down,))
    dx2, dg = _mm_rowop("ffn_up_bwd", "rms_bwd_res", [(du, w["w_up"], NN)], (sv["x2"], dx3), p["norm_ffn_pre"],
                        after=red.after())
    gs["norm_ffn_pre"] = jnp.sum(dg, axis=0)
    g_up = _mm_tn("ffn_up_dw", du, sv["hn2"])
    red.add("ffn", ("w_down", "w_up"), [g_down, g_up])
    do, dh1, dg = _rowop_mm("att_o_bwd", "rms_bwd", (sv["h1"], dx2), p["norm_xattn_post"], w["w_o"], NT, BF16,
                            after=red.after())
    gs["norm_xattn_post"] = jnp.sum(dg, axis=0)
    g_o = _mm_tn("att_o_dw", sv["o"], dh1)
    red.advance((g_o,))
    dq, dk, dv = _attn_bwd(sv["q"], sv["k"], sv["v"], do, 1024, after=red.after())
    dk, dv = dk.astype(BF16), dv.astype(BF16)
    dx1, dg = _mm_rowop("att_q_bwd", "rms_bwd_res", [(dq, w["w_q"], NT)], (sv["x1"], dx2), p["norm_xattn_pre"],
                        after=red.after())
    gs["norm_xattn_pre"] = jnp.sum(dg, axis=0)
    g_q = _mm_tn("att_q_dw", sv["hn1"], dq)
    g_k = _mm_tn("att_k_dw", sv["mn"], dk)
    g_v = _mm_tn("att_v_dw", sv["mn"], dv)
    (dg,) = _mm_rowop("att_kv_bwd", "rms_bwd_gain", [(dk, w["w_k"], NT), (dv, w["w_v"], NT)], (mem,), p["norm_mem"])
    gs["norm_mem"] = jnp.sum(dg, axis=0)
    red.add("att", ("w_o", "w_q", "w_k", "w_v"), [g_o, g_q, g_k, g_v])
    dy, dh0, dg = _rowop_mm("mix_out_bwd", "rms_bwd", (sv["h0"], dx1), p["norm_mix_post"], w["w_out"], NT, F32,
                            after=red.after())
    gs["norm_mix_post"] = jnp.sum(dg, axis=0)
    g_out = _mm_tn("mix_out_dw", sv["y"], dh0)
    red.advance((g_out,))
    red.add("out", ("w_out",), [g_out])
    z = sv["z"]
    dz, dgv, dws, dbs = _gmlp_bwd(z, dy, p["gmlp_v_gain"], p["w_spatial"], p["b_spatial_t"], 512, after=red.after())
    gs["gmlp_v_gain"] = jnp.sum(dgv, axis=0)
    gs["w_spatial"] = jnp.sum(dws, axis=0)
    gs["b_spatial"] = jnp.sum(dbs[..., 0], axis=0)
    dz, gs["w_pool"], gs["s_pool"] = _pool_bwd(z, dy, p["w_pool"], p["s_pool"], dz)
    dz, dwdw, gs["b_dw"], gs["conv_ln_g"], gs["conv_ln_b"] = _conv_bwd(
        z, dy, p["w_dw"], p["b_dw"], p["conv_ln_g"], p["conv_ln_b"], dz)
    red.advance((dz,))
    g_in = _mm_tn("mix_in_dw", dz, sv["hn0"], after=red.after())
    red.add("in", ("w_in",), [g_in])
    if red.layer == 0:
        red.advance(())
    red.small("mixer", _small_grad_arrays(gs, dwdw, norms=False))
    dx0, dg = _mm_rowop("mix_in_bwd", "rms_bwd_res", [(dz, w["w_in"], NN)], (sv["x0"], dx1), p["norm_mix_pre"],
                        after=red.after())
    gs["norm_mix_pre"] = jnp.sum(dg, axis=0)
    late = {"norms": jnp.concatenate([gs[n] for n in NORM_NAMES], axis=0)}
    if red.layer == 0:
        late["loss"] = red.extra[0]
    red.small("norms", late)
    return dx0


NORM_NAMES = ("norm_mix_pre", "norm_mix_post", "norm_xattn_pre", "norm_mem", "norm_xattn_post", "norm_ffn_pre",
              "norm_ffn_post")
VEC_NAMES = ("s_pool", "b_dw", "conv_ln_g", "conv_ln_b")
SMALL_ARRAYS = ("norms", "gain_bias", "w_spatial", "w_pool", "vecs", "w_dw")


def _small_grad_arrays(gs, dwdw, norms=True):
    out = {"norms": jnp.concatenate([gs[n] for n in NORM_NAMES], axis=0)} if norms else {}
    out.update({"gain_bias": jnp.concatenate([gs["gmlp_v_gain"], gs["b_spatial"]], axis=0),
                "w_spatial": gs["w_spatial"], "w_pool": gs["w_pool"],
                "vecs": jnp.concatenate([gs[n] for n in VEC_NAMES], axis=0), "w_dw": dwdw})
    return out


def _layer_params(small, l):
    p = {n: small[n][l].reshape(1, -1) for n in ("norm_mix_pre", "norm_mix_post", "s_pool", "b_dw", "conv_ln_g",
                                                   "conv_ln_b", "norm_xattn_pre", "norm_mem", "norm_xattn_post",
                                                   "norm_ffn_pre", "norm_ffn_post")}
    p["gmlp_v_gain"] = small["gmlp_v_gain"][l]
    p["w_spatial"] = small["w_spatial"][l]
    p["b_spatial_t"] = small["b_spatial"][l].T
    p["w_pool"] = small["w_pool"][l]
    p["w_dw"] = jnp.pad(small["w_dw"][l], ((0, 1), (0, 0)))
    return p


def _local_step(x, mem, target, fetch, small, red):
    small = dict(small)
    saved, weights, params = [], [], []
    h = x
    marker = ()
    for l in range(DEPTH):
        w = fetch(l, "in", marker)
        if "taps" in w:
            small["w_dw"] = w.pop("taps")
        p = _layer_params(small, l)
        h, sv = _layer_fwd(h, mem, w, p, functools.partial(fetch, l))
        marker = (h,)
        saved.append(sv)
        weights.append(w)
        params.append(p)
    dh, loss = _loss_head(h, target, 1024)
    red.extra = (loss,)
    for l in reversed(range(DEPTH)):
        red.layer = l
        dh = _layer_bwd(dh, mem, weights[l], params[l], saved[l], red)
    return loss, dh


HBM = pl.BlockSpec(memory_space=pltpu.HBM)


def _position():
    return lax.axis_index("x"), lax.axis_index("y"), lax.axis_index("c")


SEM = pl.BlockSpec(memory_space=pltpu.SEMAPHORE)
EFFECT = pltpu.SideEffectType.DATAFLOW_SIDE_EFFECTING
TOKEN = jax.ShapeDtypeStruct((8, LANES), F32)
TOKEN_SPEC = pl.BlockSpec(memory_space=pltpu.VMEM)


def _landing(shape, dtype):
    return pltpu.with_memory_space_constraint(lax.empty(shape, dtype), pltpu.HBM)


def _hbm_shapes(arrays):
    return [pltpu.HBM(a.shape, a.dtype) for a in arrays]


def _block(ref, r, dev):
    return ref.at[pl.ds((4 * dev[0] + 2 * dev[1] + dev[2]) * r, r), :]


def _split_call(name, body, thru, sems_in, after, sems_out, token):
    n = len(thru)
    out_shape = [pltpu.SemaphoreType.DMA(s) for s in sems_out] + _hbm_shapes(thru) + ([TOKEN] if token else [])
    out_specs = [SEM] * len(sems_out) + [HBM] * n + ([TOKEN_SPEC] if token else [])
    return pl.pallas_call(
        body, name=name, in_specs=[HBM] * n + [SEM] * len(sems_in) + [ANY] * len(after),
        out_specs=out_specs, out_shape=out_shape,
        input_output_aliases={i: len(sems_out) + i for i in range(n)},
        compiler_params=pltpu.CompilerParams(has_side_effects=EFFECT),
    )(*thru, *sems_in, *after)


def _place_own(name, srcs, dev, out_dtype, tr):
    n = len(srcs)
    r, cols = srcs[0][0].shape[-2:]
    tr = r if r < 16 else _row_tile(r, tr)
    nb = r // tr

    def body(dev_ref, *refs):
        for a in range(n):
            refs[n + a][...] = refs[a][...].astype(out_dtype)

    in_specs = [pl.BlockSpec((tr, cols), lambda i, d: (i, 0)) if l is None
                else pl.BlockSpec((None, tr, cols), lambda i, d, l=l: (l, i, 0)) for _, l in srcs]
    return pl.pallas_call(
        body, name=name,
        grid_spec=pltpu.PrefetchScalarGridSpec(
            num_scalar_prefetch=1, grid=(nb,), in_specs=in_specs,
            out_specs=[pl.BlockSpec((tr, cols), lambda i, d: (d[0] * nb + i, 0))] * n),
        out_shape=[jax.ShapeDtypeStruct((N_DEV * r, cols), out_dtype)] * n, compiler_params=_cparams(),
    )(dev, *[a for a, _ in srcs])


def _gather_peers(x, y, c):
    return [(1 - x, y, c), (x, 1 - y, c), (1 - x, 1 - y, c), (x, y, 1 - c)]


def _block_rows(land):
    return land.shape[0] // N_DEV


def _near_peers(x, y, c):
    return [(1 - x, y, c), (x, 1 - y, c), (x, y, 1 - c)]


def _relay_route(x, y, c):
    origin = (x + c * (1 - 2 * x), y + (1 - c) * (1 - 2 * y), c)
    target = (x + (1 - c) * (1 - 2 * x), y + c * (1 - 2 * y), c)
    return origin, target


def _same_block_copy(blk, send_sem, recv_sem, to):
    return pltpu.make_async_remote_copy(src_ref=blk, dst_ref=blk, send_sem=send_sem, recv_sem=recv_sem, device_id=to,
                                        device_id_type=MESH)


def _gather_start(name, lands, after):
    n = len(lands)

    def body(*refs):
        lz = refs[:n]
        send_sems, recv_sems = refs[n + len(after)], refs[n + len(after) + 1]
        token = refs[-1]
        x, y, c = _position()
        for a in range(n):
            own = _block(lz[a], _block_rows(lands[a]), (x, y, c))
            for k, to in enumerate(_near_peers(x, y, c)):
                _same_block_copy(own, send_sems.at[k], recv_sems.at[k], to).start()
        token[...] = jnp.zeros_like(token)

    out = _split_call(name, body, list(lands), [], after, [(3,), (3,)], True)
    return out[0], out[1], out[2:2 + n], out[-1]


def _gather_step(name, near, far, fresh, after):
    groups = [g for g in (near and near[0], far and far[0], fresh) if g]
    counts = [len(near[0]) if near else 0, len(far[0]) if far else 0, len(fresh) if fresh else 0]
    n = sum(counts)
    sems_in = ([near[1]] if near else []) + ([far[1]] if far else [])
    sems_out = ([(2,), (2,), (1,), (1,)] if near else []) + ([(1,), (1,)] if far else []) + ([(3,), (3,)] if fresh else [])

    def body(*refs):
        lz = list(refs[:n])
        ins = list(refs[n:n + len(sems_in)])
        outs = list(refs[n + len(sems_in) + len(after):n + len(sems_in) + len(after) + len(sems_out)])
        token = refs[-1]
        x, y, c = _position()
        me, sibling = (x, y, c), (x, y, 1 - c)
        near_lz, far_lz, fresh_lz = (lz[sum(counts[:i]):sum(counts[:i + 1])] for i in range(3))
        neighbours = _near_peers(x, y, c)[:2]
        origin, target = _relay_route(x, y, c)
        diagonal = (1 - x, 1 - y, c)
        if near:
            recv0 = ins.pop(0)
            fsend, frecv, rsend, rrecv = (outs.pop(0) for _ in range(4))
            for a, land in enumerate(near[0]):
                for j, chip in enumerate(neighbours):
                    _same_block_copy(_block(near_lz[a], _block_rows(land), chip), fsend.at[j], recv0.at[j], me).wait_recv()
        if far:
            rrecv_in = ins.pop(0)
            f2send, f2recv = outs.pop(0), outs.pop(0)
            for a, land in enumerate(far[0]):
                _same_block_copy(_block(far_lz[a], _block_rows(land), diagonal), f2send.at[0], rrecv_in.at[0], me).wait_recv()
            for a, land in enumerate(far[0]):
                _same_block_copy(_block(far_lz[a], _block_rows(land), diagonal), f2send.at[0], f2recv.at[0], sibling).start()
        if near:
            for a, land in enumerate(near[0]):
                r = _block_rows(land)
                _same_block_copy(_block(near_lz[a], r, origin), rsend.at[0], rrecv.at[0], target).start()
                for j, chip in enumerate(neighbours):
                    _same_block_copy(_block(near_lz[a], r, chip), fsend.at[j], frecv.at[j], sibling).start()
        if fresh:
            send_sems, recv_sems = outs.pop(0), outs.pop(0)
            for a, land in enumerate(fresh):
                own = _block(fresh_lz[a], _block_rows(land), me)
                for k, to in enumerate(_near_peers(x, y, c)):
                    _same_block_copy(own, send_sems.at[k], recv_sems.at[k], to).start()
        token[...] = jnp.zeros_like(token)

    out = list(_split_call(name, body, [l for g in groups for l in g], sems_in, after, sems_out, True))
    res = {"token": out.pop()}
    if near:
        res.update(fsend=out.pop(0), frecv=out.pop(0), rsend=out.pop(0), rrecv=out.pop(0))
    if far:
        res.update(f2send=out.pop(0), f2recv=out.pop(0))
    if fresh:
        res.update(send=out.pop(0), recv=out.pop(0))
    res["near"], res["far"], res["fresh"] = (out[sum(counts[:i]):sum(counts[:i + 1])] for i in range(3))
    return res


def _gather_finish(name, lands, send_sems, recv_sems, fsend, frecv, rsend, f2send, f2recv, after):
    n = len(lands)

    def body(*refs):
        lz = refs[:n]
        send0, recv0, fsend_ref, frecv_ref, rsend_ref, f2send_ref, f2recv_ref = refs[n:n + 7]
        x, y, c = _position()
        me = (x, y, c)
        near = _near_peers(x, y, c)[:2]
        origin, _ = _relay_route(x, y, c)
        for a in range(n):
            r = _block_rows(lands[a])
            sib = _block(lz[a], r, (x, y, 1 - c))
            _same_block_copy(sib, send0.at[2], recv0.at[2], me).wait_recv()
            for j, chip in enumerate(near):
                blk = _block(lz[a], r, (chip[0], chip[1], 1 - c))
                _same_block_copy(blk, fsend_ref.at[j], frecv_ref.at[j], me).wait_recv()
            far = _block(lz[a], r, (1 - x, 1 - y, 1 - c))
            _same_block_copy(far, f2send_ref.at[0], f2recv_ref.at[0], me).wait_recv()
            own = _block(lz[a], r, me)
            for k in range(3):
                _same_block_copy(own, send0.at[k], recv0.at[k], me).wait_send()
            for j, chip in enumerate(near):
                _same_block_copy(_block(lz[a], r, chip), fsend_ref.at[j], frecv_ref.at[j], me).wait_send()
            _same_block_copy(_block(lz[a], r, origin), rsend_ref.at[0], recv0.at[0], me).wait_send()
            _same_block_copy(_block(lz[a], r, (1 - x, 1 - y, c)), f2send_ref.at[0], f2recv_ref.at[0], me).wait_send()

    return _split_call(name, body, list(lands), [send_sems, recv_sems, fsend, frecv, rsend, f2send, f2recv], after, [],
                       False)


def _sibling_start(name, grads, after):
    n = len(grads)
    lands = [_landing((4, g.shape[0] // N_DEV, D), g.dtype) for g in grads]

    def body(*refs):
        ins, lz = refs[:n], refs[n:2 * n]
        send_sem, recv_sem = refs[2 * n + len(after)], refs[2 * n + len(after) + 1]
        token = refs[-1]
        x, y, c = _position()
        for a in range(n):
            r = grads[a].shape[0] // N_DEV
            for q in range(4):
                pltpu.make_async_remote_copy(
                    src_ref=ins[a].at[pl.ds((2 * q + 1 - c) * r, r), :], dst_ref=lz[a].at[q], send_sem=send_sem.at[0],
                    recv_sem=recv_sem.at[0], device_id=(x, y, 1 - c), device_id_type=MESH).start()
        token[...] = jnp.zeros_like(token)

    out = _split_call(name, body, list(grads) + lands, [], after, [(1,), (1,)], True)
    return out[0], out[1], out[2:2 + n], out[2 + n:2 + 2 * n], out[-1]


def _sibling_finish(name, grads, lands, send_sem, recv_sem, after):
    n = len(grads)

    def body(*refs):
        ins, lz = refs[:n], refs[n:2 * n]
        send_ref, recv_ref = refs[2 * n], refs[2 * n + 1]
        x, y, c = _position()
        for a in range(n):
            r = grads[a].shape[0] // N_DEV
            for q in range(4):
                cp = pltpu.make_async_remote_copy(
                    src_ref=ins[a].at[pl.ds((2 * q + 1 - c) * r, r), :], dst_ref=lz[a].at[q], send_sem=send_ref.at[0],
                    recv_sem=recv_ref.at[0], device_id=(x, y, c), device_id_type=MESH)
                cp.wait_send()
                cp.wait_recv()

    out = _split_call(name, body, list(grads) + list(lands), [send_sem, recv_sem], after, [], False)
    return out[:n], out[n:2 * n]


def _chip_start(name, parts, after):
    n = len(parts)
    lands = [_landing((3,) + p.shape[1:], p.dtype) for p in parts]

    def body(*refs):
        ins, lz = refs[:n], refs[n:2 * n]
        send_sems, recv_sems = refs[2 * n + len(after)], refs[2 * n + len(after) + 1]
        token = refs[-1]
        x, y, c = _position()
        for a in range(n):
            for j, chip in enumerate(_gather_peers(x, y, c)[:3]):
                pltpu.make_async_remote_copy(
                    src_ref=ins[a].at[2 * chip[0] + chip[1]], dst_ref=lz[a].at[j], send_sem=send_sems.at[j],
                    recv_sem=recv_sems.at[j], device_id=chip, device_id_type=MESH).start()
        token[...] = jnp.zeros_like(token)

    out = _split_call(name, body, list(parts) + lands, [], after, [(3,), (3,)], True)
    return out[0], out[1], out[2:2 + n], out[2 + n:2 + 2 * n], out[-1]


def _chip_finish(name, parts, lands, send_sems, recv_sems, after):
    n = len(parts)

    def body(*refs):
        ins, lz = refs[:n], refs[n:2 * n]
        send_ref, recv_ref = refs[2 * n], refs[2 * n + 1]
        me = _position()
        for a in range(n):
            for j in range(3):
                cp = pltpu.make_async_remote_copy(
                    src_ref=ins[a].at[j], dst_ref=lz[a].at[j], send_sem=send_ref.at[j], recv_sem=recv_ref.at[j],
                    device_id=me, device_id_type=MESH)
                cp.wait_send()
                cp.wait_recv()

    out = _split_call(name, body, list(parts) + list(lands), [send_sems, recv_sems], after, [], False)
    return out[:n], out[n:2 * n]


def _other_devices(x, y, c):
    return [(x + (k >> 2 & 1) * (1 - 2 * x), y + (k >> 1 & 1) * (1 - 2 * y), c + (k & 1) * (1 - 2 * c))
            for k in range(1, N_DEV)]


def _broadcast_start(name, arrays, after):
    n = len(arrays)
    lands = [_landing((N_DEV,) + a.shape, a.dtype) for a in arrays]

    def body(*refs):
        ins, lz = refs[:n], refs[n:2 * n]
        send_sems, recv_sems = refs[2 * n + len(after)], refs[2 * n + len(after) + 1]
        token = refs[-1]
        x, y, c = _position()
        for a in range(n):
            for k, peer in enumerate(_other_devices(x, y, c)):
                pltpu.make_async_remote_copy(
                    src_ref=ins[a], dst_ref=lz[a].at[4 * x + 2 * y + c], send_sem=send_sems.at[k],
                    recv_sem=recv_sems.at[k], device_id=peer, device_id_type=MESH).start()
        token[...] = jnp.zeros_like(token)

    out = _split_call(name, body, list(arrays) + lands, [], after, [(N_DEV - 1,), (N_DEV - 1,)], True)
    return out[0], out[1], out[2:2 + n], out[2 + n:2 + 2 * n], out[-1]


def _broadcast_finish(name, arrays, lands, send_sems, recv_sems, after):
    n = len(arrays)

    def body(*refs):
        ins, lz = refs[:n], refs[n:2 * n]
        send_ref, recv_ref = refs[2 * n], refs[2 * n + 1]
        x, y, c = _position()
        for a in range(n):
            for k, peer in enumerate(_other_devices(x, y, c)):
                cp = pltpu.make_async_remote_copy(
                    src_ref=ins[a], dst_ref=lz[a].at[4 * peer[0] + 2 * peer[1] + peer[2]], send_sem=send_ref.at[k],
                    recv_sem=recv_ref.at[k], device_id=(x, y, c), device_id_type=MESH)
                cp.wait_send()
                cp.wait_recv()

    out = _split_call(name, body, list(arrays) + list(lands), [send_sems, recv_sems], after, [], False)
    return out[:n], out[n:2 * n]


def _row_tile(r, target):
    return max(t for t in range(16, min(r, target) + 1, 16) if r % t == 0)


CHIP_PARTIAL_BYTES = 12 * 1024 * 1024


def _chip_partial(name, grads, gots, c):
    n = len(grads)
    r = grads[0].shape[0] // N_DEV
    tr = _row_tile(r, CHIP_PARTIAL_BYTES // (n * 3 * D * 2))

    def body(c_ref, *refs):
        for a in range(n):
            refs[2 * n + a][...] = (refs[a][...].astype(F32) + refs[n + a][...].astype(F32)).astype(BF16)

    blk = pl.BlockSpec((None, tr, D), lambda q, i, c_ref: (q, i, 0))
    return pl.pallas_call(
        body, name=name,
        grid_spec=pltpu.PrefetchScalarGridSpec(
            num_scalar_prefetch=1, grid=(4, r // tr),
            in_specs=[pl.BlockSpec((None, None, tr, D), lambda q, i, c_ref: (q, c_ref[0], i, 0))] * n + [blk] * n,
            out_specs=[blk] * n),
        out_shape=[jax.ShapeDtypeStruct((4, r, D), BF16)] * n, compiler_params=_cparams(),
    )(c, *[g.reshape(4, 2, r, D) for g in grads], *gots)


class _WeightGather:
    def __init__(self, groups):
        self.groups = list(groups)
        self.index = {key: i for i, (key, _, _) in enumerate(groups)}
        self.state = [None] * len(groups)
        self.token = ()
        for i in range(min(2, len(groups))):
            self._start(i)

    def _tag(self, i):
        return "%s_%d" % self.groups[i][0][::-1]

    def _start(self, i):
        send, recv, lz, tok = _gather_start("gather_start_" + self._tag(i), self.groups[i][2], self.token)
        self.state[i] = dict(send=send, recv=recv, lands=lz)
        self.token = (tok,)

    def _step(self, name, near, far, fresh, marker):
        exists = lambda i: i is not None and i < len(self.groups)
        near, far, fresh = (i if exists(i) else None for i in (near, far, fresh))
        res = _gather_step(
            name, None if near is None else (self.state[near]["lands"], self.state[near]["recv"]),
            None if far is None else (self.state[far]["lands"], self.state[far]["rrecv"]),
            None if fresh is None else self.groups[fresh][2], tuple(marker) + self.token)
        self.token = (res["token"],)
        if near is not None:
            self.state[near].update(lands=res["near"], fsend=res["fsend"], frecv=res["frecv"], rsend=res["rsend"],
                                    rrecv=res["rrecv"])
        if far is not None:
            self.state[far].update(lands=res["far"], f2send=res["f2send"], f2recv=res["f2recv"])
        if fresh is not None:
            self.state[fresh] = dict(send=res["send"], recv=res["recv"], lands=res["fresh"])

    def fetch(self, layer, group, marker):
        k = self.index[(layer, group)]
        if k == 0:
            self._step("gather_step_first", 0, None, None, marker)
        self._step("gather_step_" + self._tag(k), k + 1, k, k + 2, marker)
        st = self.state[k]
        lz = _gather_finish("gather_finish_" + self._tag(k), st["lands"], st["send"], st["recv"], st["fsend"],
                            st["frecv"], st["rsend"], st["f2send"], st["f2recv"], self.token)
        self.state[k] = None
        return dict(zip(self.groups[k][1], lz))


class _GradReduce:
    def __init__(self, core, chip):
        self.core, self.chip = core, chip
        self.layer = None
        self.token = ()
        self.at_sibling, self.at_chips = [], []
        self.extra, self.smalls = (), {}

    def after(self):
        return self.token

    def add(self, group, names, grads):
        tag = "%s_%d" % (group, self.layer)
        send, recv, grads, lands, tok = _sibling_start("grad_sibling_start_" + tag, grads, self.token)
        self.at_sibling.append((tag, [(self.layer, n) for n in names], send, recv, grads, lands))
        self.token = (tok,)

    def advance(self, marker):
        for tag, keys, send, recv, grads, lands in self.at_sibling:
            grads, lands = _sibling_finish("grad_sibling_finish_" + tag, grads, lands, send, recv, marker)
            parts = _chip_partial("chip_partial_" + tag, grads, lands, self.core)
            send, recv, parts, lands, tok = _chip_start("grad_chip_start_" + tag, parts, ())
            self.at_chips.append([tag, keys, send, recv, parts, lands])
            self.token = (tok,)
        self.at_sibling = []

    def small(self, part, arrays):
        keys = list(arrays)
        send, recv, own, slots, tok = _broadcast_start(
            "small_grads_start_%d_%s" % (self.layer, part), [arrays[k] for k in keys], self.token)
        self.smalls.setdefault(self.layer, []).append((part, keys, send, recv, own, slots))
        self.token = (tok,)

    def small_finish(self, layer, marker):
        mine, theirs = {}, {}
        for part, keys, send, recv, own, slots in self.smalls[layer]:
            own, slots = _broadcast_finish("small_grads_finish_%d_%s" % (layer, part), own, slots, send, recv, marker)
            mine.update(zip(keys, own))
            theirs.update(zip(keys, slots))
        return mine, theirs

    def collect(self, key, marker):
        for entry in self.at_chips:
            tag, keys, send, recv, parts, lands = entry
            if key in keys:
                if send is not None:
                    parts, lands = _chip_finish("grad_chip_finish_" + tag, parts, lands, send, recv, marker)
                    entry[2:] = [None, None, parts, lands]
                i = keys.index(key)
                return parts[i], lands[i]
        raise KeyError(key)


def _adamw_math(w, g, m, v):
    m = ADAM_B1 * m + (1.0 - ADAM_B1) * g
    v = ADAM_B2 * v + (1.0 - ADAM_B2) * jnp.square(g)
    m_hat = m / (1.0 - ADAM_B1 ** ADAM_STEP)
    v_hat = v / (1.0 - ADAM_B2 ** ADAM_STEP)
    delta = -ADAM_LR * (m_hat / (jnp.sqrt(v_hat) + ADAM_EPS) + ADAM_WD * w)
    return delta, m, v


def _adamw_small(wts, mom_m, mom_v, own, gathered, loss_own, loss_gathered, dev):
    names = SMALL
    nw = len(names)
    na = len(SMALL_ARRAYS)

    def body(dev_ref, *refs):
        w_refs, m_refs, v_refs = (dict(zip(names, refs[i * nw:(i + 1) * nw])) for i in range(3))
        own_refs = refs[3 * nw:3 * nw + DEPTH * na]
        g_refs = refs[3 * nw + DEPTH * na:3 * nw + 2 * DEPTH * na]
        loss_own_ref, loss_got_ref = refs[3 * nw + 2 * DEPTH * na:3 * nw + 2 * DEPTH * na + 2]
        outs = refs[3 * nw + 2 * DEPTH * na + 2:]
        g_out, d_out, m_out, v_out = (dict(zip(names, outs[i * nw:(i + 1) * nw])) for i in range(4))
        me = dev_ref[0]

        loss = None
        for d in range(N_DEV):
            for b in range(loss_own.shape[0]):
                term = jnp.where(me == d, loss_own_ref[b], loss_got_ref[d, b])
                loss = term if loss is None else loss + term
        outs[4 * nw][...] = loss

        def update(name, at, g):
            g_out[name][at] = g
            d_out[name][at], m_out[name][at], v_out[name][at] = _adamw_math(
                w_refs[name][at], g, m_refs[name][at], v_refs[name][at])

        for l in range(DEPTH):
            mine = dict(zip(SMALL_ARRAYS, own_refs[l * na:(l + 1) * na]))
            got = dict(zip(SMALL_ARRAYS, g_refs[l * na:(l + 1) * na]))

            def total(key, at):
                acc = None
                for d in range(N_DEV):
                    term = jnp.where(me == d, mine[key][at] if at else mine[key][...], got[key][(d,) + at])
                    acc = term if acc is None else acc + term
                return acc

            row = (slice(l, l + 1),)
            for k, name in enumerate(NORM_NAMES):
                update(name, row, total("norms", (slice(k, k + 1),)))
            for k, name in enumerate(VEC_NAMES):
                update(name, row, total("vecs", (slice(k, k + 1),)))
            update("gmlp_v_gain", (l,), total("gain_bias", (slice(0, NH),)))
            update("b_spatial", (l,), total("gain_bias", (slice(NH, 2 * NH),)))
            update("w_spatial", (l,), total("w_spatial", ()))
            update("w_pool", (l,), total("w_pool", ()))
            update("w_dw", (l,), total("w_dw", (slice(0, CONV_K),)))

    args = [src[n] for src in (wts, mom_m, mom_v) for n in names]
    args += [src[l][k] for src in (own, gathered) for l in range(DEPTH) for k in SMALL_ARRAYS]
    args += [loss_own, loss_gathered]
    outs = pl.pallas_call(
        body, name="adamw_small",
        in_specs=[pl.BlockSpec(memory_space=pltpu.SMEM)] + [pl.BlockSpec(memory_space=pltpu.VMEM)] * len(args),
        out_shape=[jax.ShapeDtypeStruct(wts[n].shape, F32) for _ in range(4) for n in names]
        + [jax.ShapeDtypeStruct((8, LANES), F32)],
        compiler_params=_cparams(),
    )(dev, *args)
    return tuple(dict(zip(names, outs[i * nw:(i + 1) * nw])) for i in range(4)) + (outs[4 * nw],)


def _adamw_layers(name, w, reduced, m, v, chip, tr, transposed=False, after=()):
    nl, r, cdim = w.shape
    tr = _row_tile(r, tr)
    nb = r // tr

    def body(q_ref, w_ref, p0_ref, g0_ref, p1_ref, g1_ref, m_ref, v_ref, *rest):
        g_ref, d_ref, nm_ref, nv_ref = rest[len(after):]

        def total(p_ref, got_ref):
            acc = p_ref[...].astype(F32)
            for j in range(3):
                acc = acc + got_ref[j].astype(F32)
            return acc

        g = jnp.where(pl.program_id(0) == 0, total(p0_ref, g0_ref), total(p1_ref, g1_ref))
        if transposed:
            g = g.T
        g_ref[...] = g
        d_ref[...], nm_ref[...], nv_ref[...] = _adamw_math(w_ref[...], g, m_ref[...], v_ref[...])

    blk = pl.BlockSpec((None, tr, cdim), lambda l, i, q: (l, i, 0))
    first = lambda l, i: i * (1 - l) + (nb - 1) * l
    second = lambda l, i: i * l
    if transposed:
        gshape = (cdim, tr)
        at = lambda lead, i: (lead, 0, i)
    else:
        gshape = (tr, cdim)
        at = lambda lead, i: (lead, i, 0)
    specs = [blk,
             pl.BlockSpec((None,) + gshape, lambda l, i, q: at(q[0], first(l, i))),
             pl.BlockSpec((3,) + gshape, lambda l, i, q: at(0, first(l, i))),
             pl.BlockSpec((None,) + gshape, lambda l, i, q: at(q[0], second(l, i))),
             pl.BlockSpec((3,) + gshape, lambda l, i, q: at(0, second(l, i))), blk, blk] + [ANY] * len(after)
    shape = jax.ShapeDtypeStruct((nl, r, cdim), F32)
    return pl.pallas_call(
        body, name=name,
        grid_spec=pltpu.PrefetchScalarGridSpec(num_scalar_prefetch=1, grid=(nl, nb), in_specs=specs, out_specs=[blk] * 4),
        out_shape=[shape] * 4, compiler_params=_cparams(),
    )(chip, w, *reduced[0], *reduced[1], m, v, *after)


def _to_rows(name, a):
    return jnp.swapaxes(a, 1, 2) if name == "w_in" else a


def _place_own_transposed(name, srcs, dev, out_dtype, tc):
    n = len(srcs)
    kdim, cdim = srcs[0][0].shape[-2:]

    def body(dev_ref, *refs):
        for a in range(n):
            refs[n + a][...] = refs[a][...].T.astype(out_dtype)

    return pl.pallas_call(
        body, name=name,
        grid_spec=pltpu.PrefetchScalarGridSpec(
            num_scalar_prefetch=1, grid=(kdim // tc,),
            in_specs=[pl.BlockSpec((None, tc, cdim), lambda i, d, l=l: (l, i, 0)) for _, l in srcs],
            out_specs=[pl.BlockSpec((cdim, tc), lambda i, d: (d[0], i))] * n),
        out_shape=[jax.ShapeDtypeStruct((N_DEV * cdim, kdim), out_dtype)] * n, compiler_params=_cparams(),
    )(dev, *[a for a, _ in srcs])


def _pack(arrays, rows):
    flat = jnp.concatenate([a.reshape(-1) for a in arrays])
    return jnp.pad(flat, (0, rows * D - flat.shape[0])).reshape(rows, D)


def _rows_for(shapes, mult=8):
    total = 0
    for shp in shapes:
        size = 1
        for dim in shp:
            size *= dim
        total += size
    return -(-total // (mult * D)) * mult


def kernel(x, mem, norm_mix_pre, norm_mix_post, w_in, w_out, gmlp_v_gain, w_spatial, b_spatial, w_pool, s_pool, w_dw, b_dw, conv_ln_g, conv_ln_b, norm_xattn_pre, norm_mem, norm_xattn_post, w_q, w_k, w_v, w_o, norm_ffn_pre, norm_ffn_post, w_up, w_down, loss_target, m_norm_mix_pre, m_norm_mix_post, m_w_in, m_w_out, m_gmlp_v_gain, m_w_spatial, m_b_spatial, m_w_pool, m_s_pool, m_w_dw, m_b_dw, m_conv_ln_g, m_conv_ln_b, m_norm_xattn_pre, m_norm_mem, m_norm_xattn_post, m_w_q, m_w_k, m_w_v, m_w_o, m_norm_ffn_pre, m_norm_ffn_post, m_w_up, m_w_down, v_norm_mix_pre, v_norm_mix_post, v_w_in, v_w_out, v_gmlp_v_gain, v_w_spatial, v_b_spatial, v_w_pool, v_s_pool, v_w_dw, v_b_dw, v_conv_ln_g, v_conv_ln_b, v_norm_xattn_pre, v_norm_mem, v_norm_xattn_post, v_w_q, v_w_k, v_w_v, v_w_o, v_norm_ffn_pre, v_norm_ffn_post, v_w_up, v_w_down):
    args = dict(locals())
    wts = {n: args[n] for n in WEIGHTS}
    mom_m = {n: args["m_" + n] for n in WEIGHTS}
    mom_v = {n: args["v_" + n] for n in WEIGHTS}
    xi, yi, ci = _position()
    me = 4 * xi + 2 * yi + ci

    dev = jnp.reshape(me, (1,)).astype(jnp.int32)
    lands = {}
    for call, names, tr in (("place_in", ("w_in",), 256), ("place_att", ("w_out", "w_q", "w_k", "w_v", "w_o"), 64),
                            ("place_up", ("w_up",), 256), ("place_down", ("w_down",), 256)):
        srcs = [(_to_rows(n, wts[n]), l) for l in range(DEPTH) for n in names]
        placed = (_place_own_transposed if names == ("w_up",) else _place_own)(call, srcs, dev, BF16, tr)
        lands.update(zip([(l, n) for l in range(DEPTH) for n in names], placed))
    (lands[(0, "taps")],) = _place_own("place_taps", [(_pack([w_dw], _rows_for([w_dw.shape])), None)], dev, F32, 8)
    groups = []
    for l in range(DEPTH):
        for group, names in GATHER_GROUPS:
            if (l, group) == (0, "in"):
                names = names + ("taps",)
            groups.append(((l, group), names, [lands[(l, n)] for n in names]))
    gather = _WeightGather(groups)

    def fetch(layer, group, marker):
        w = gather.fetch(layer, group, marker)
        if "taps" in w:
            blocks = w["taps"].reshape(N_DEV, -1)[:, :w_dw.size].reshape((N_DEV,) + w_dw.shape)
            w["taps"] = jnp.moveaxis(blocks, 0, 2).reshape(DEPTH, CONV_K, CW)
        return w

    reduce = _GradReduce(jnp.reshape(ci, (1,)).astype(jnp.int32), jnp.reshape(2 * xi + yi, (1,)).astype(jnp.int32))
    small = {n: wts[n] for n in SMALL if n != "w_dw"}
    _, dx = _local_step(x[0], mem[0], loss_target[0], fetch, small, reduce)
    reduce.advance((dx,))

    grad_w, delta, new_m, new_v = {}, {}, {}, {}
    marker = (dx,) + tuple(reduce.after())
    for n in UPDATE_ORDER:
        reduced = [reduce.collect((l, n), marker) for l in range(DEPTH)]
        outs = _adamw_layers("adamw_" + n, _to_rows(n, wts[n]), reduced, _to_rows(n, mom_m[n]), _to_rows(n, mom_v[n]),
                             reduce.chip, 256, transposed=n == "w_up", after=marker)
        grad_w[n], delta[n], new_m[n], new_v[n] = (_to_rows(n, o) for o in outs)
        marker = (outs[1],)

    own, slots = [None] * DEPTH, [None] * DEPTH
    for l in reversed(range(DEPTH)):
        own[l], slots[l] = reduce.small_finish(l, marker)
        if l == 0:
            loss_own, loss_slots = own[l].pop("loss"), slots[l].pop("loss")
    shard_cols = CW // N_DEV
    for l in range(DEPTH):
        own[l]["w_dw"] = lax.dynamic_slice_in_dim(own[l]["w_dw"], me * shard_cols, shard_cols, axis=1)
        slots[l]["w_dw"] = lax.dynamic_slice_in_dim(slots[l]["w_dw"], me * shard_cols, shard_cols, axis=2)
    *small_out, loss_tile = _adamw_small(wts, mom_m, mom_v, own, slots, loss_own, loss_slots, dev)
    for dst, src in zip((grad_w, delta, new_m, new_v), small_out):
        dst.update(src)

    return (loss_tile[0, 0], dx[None], *[grad_w[n] for n in WEIGHTS], *[delta[n] for n in WEIGHTS],
            *[new_m[n] for n in WEIGHTS], *[new_v[n] for n in WEIGHTS])
```

```python
import functools

import jax
import jax.numpy as jnp
from jax import lax
from jax.experimental import pallas as pl
from jax.experimental.pallas import tpu as pltpu

F32 = jnp.float32
BF16 = jnp.bfloat16

D = 2048
GW = 1024
PW = 512
CW = 512
HD = 128
NH = 8
NG = 4
POOL_WINDOWS = (2, 4, 8, 16)
CONV_K = 31
IN_COLS = 2 * GW + PW + 2 * CW
XH = 4
XHD = D // XH
ATT_SCALE = XHD ** -0.5
RMS_EPS = 1e-6
LN_EPS = 1e-5
DEPTH = 2
N_DEV = 8

ADAM_LR = 0.001
ADAM_B1 = 0.9
ADAM_B2 = 0.999
ADAM_EPS = 1e-08
ADAM_WD = 0.01
ADAM_STEP = 10

LANES = 128
CONV_HALO = 32
POOL_HALO = 16
ROW_TILE = 128
VMEM_LIMIT = 60 * 1024 * 1024

MESH = pl.DeviceIdType.MESH
NT = (((1,), (1,)), ((), ()))
NN = (((1,), (0,)), ((), ()))
TN = (((0,), (0,)), ((), ()))

UPDATE_ORDER = ("w_down", "w_up", "w_o", "w_q", "w_k", "w_v", "w_out", "w_in")
GATHER_GROUPS = (("in", ("w_in",)), ("out", ("w_out",)), ("att", ("w_q", "w_k", "w_v", "w_o")), ("up", ("w_up",)),
                 ("down", ("w_down",)))
SMALL = ("norm_mix_pre", "norm_mix_post", "gmlp_v_gain", "w_spatial", "b_spatial", "w_pool", "s_pool",
         "w_dw", "b_dw", "conv_ln_g", "conv_ln_b", "norm_xattn_pre", "norm_mem", "norm_xattn_post",
         "norm_ffn_pre", "norm_ffn_post")
WEIGHTS = ("norm_mix_pre", "norm_mix_post", "w_in", "w_out", "gmlp_v_gain", "w_spatial", "b_spatial", "w_pool",
           "s_pool", "w_dw", "b_dw", "conv_ln_g", "conv_ln_b", "norm_xattn_pre", "norm_mem", "norm_xattn_post",
           "w_q", "w_k", "w_v", "w_o", "norm_ffn_pre", "norm_ffn_post", "w_up", "w_down")


def _cparams():
    return pltpu.CompilerParams(vmem_limit_bytes=VMEM_LIMIT)


def _dot(a, b, dims):
    return lax.dot_general(a, b, dims, preferred_element_type=F32)


def _rms(x, g):
    y = x * lax.rsqrt(jnp.mean(x * x, axis=-1, keepdims=True) + RMS_EPS)
    return y * g


def _rms_bwd(x, g, dy):
    r = lax.rsqrt(jnp.mean(x * x, axis=-1, keepdims=True) + RMS_EPS)
    xh = x * r
    t = dy * g
    dx = r * (t - xh * jnp.mean(t * xh, axis=-1, keepdims=True))
    return dx, jnp.sum(dy * xh, axis=0, keepdims=True)


def _gelu(x):
    cdf = 0.5 * (1.0 + jnp.tanh(0.7978845608028654 * (x + 0.044715 * (x * x * x))))
    return x * cdf


def _layer_norm(x, g, b=None):
    mu = jnp.mean(x, axis=-1, keepdims=True)
    xc = x - mu
    var = jnp.mean(xc * xc, axis=-1, keepdims=True)
    y = xc * lax.rsqrt(var + LN_EPS) * g
    return y if b is None else y + b


def _sigmoid(x):
    return 1.0 / (1.0 + jnp.exp(-x))


def _gmlp_rows(zu, zv, gv):
    return _gelu(zu), _layer_norm(_gelu(zv), gv)


def _glu(cv, cg):
    return cv * _sigmoid(cg)


def _ln_silu(h, g, b):
    y = _layer_norm(h, g, b)
    return y * _sigmoid(y)


ANY = pl.BlockSpec(memory_space=pl.ANY)


ROWS_TILE = 256
COLS_TILE = 512
DW_TILE = 512
RESIDENT_K = 2048
STREAM_K_TILE = 1024
STREAM_ROWS = 512


def _k_tiles(kdim):
    if kdim <= RESIDENT_K:
        return ROWS_TILE, kdim
    return STREAM_ROWS, max(t for t in range(LANES, STREAM_K_TILE + 1, LANES) if kdim % t == 0)


def _rowop_mm(name, kind, rows, g, w, dims, out_dtype, u=None, after=()):
    s = rows[0].shape[0]
    n = w.shape[0] if dims == NT else w.shape[1]
    tm, tn = min(ROWS_TILE, s), min(COLS_TILE, n)
    ni, nj = s // tm, n // tn
    bwd = kind == "rms_bwd"
    out_shape = [jax.ShapeDtypeStruct((s, n), out_dtype), jax.ShapeDtypeStruct((s, D), BF16)]
    if bwd:
        out_shape.append(jax.ShapeDtypeStruct((ni, 1, D), F32))

    if n <= RESIDENT_K and u is None:
        def row_body(*refs):
            refs = list(refs)
            row_refs = [refs.pop(0) for _ in rows]
            g_ref, w_ref = refs.pop(0), refs.pop(0)
            del refs[:len(after)]
            if bwd:
                a, dg = _rms_bwd(row_refs[0][...], g_ref[...], row_refs[1][...])
                refs[2][0] = dg
            else:
                a = _rms(row_refs[0][...], g_ref[...])
            a = a.astype(BF16)
            refs[1][...] = a
            refs[0][...] = _dot(a, w_ref[...], dims).astype(out_dtype)

        blk = pl.BlockSpec((tm, D), lambda i: (i, 0))
        return pl.pallas_call(
            row_body, name=name, grid=(ni,),
            in_specs=[blk] * len(rows) + [pl.BlockSpec((1, D), lambda i: (0, 0)), pl.BlockSpec(w.shape, lambda i: (0, 0))]
            + [ANY] * len(after),
            out_specs=[pl.BlockSpec((tm, n), lambda i: (i, 0)), blk]
            + ([pl.BlockSpec((1, 1, D), lambda i: (i, 0, 0))] if bwd else []),
            out_shape=out_shape, compiler_params=_cparams(),
        )(*rows, g, w, *after)

    def body(*refs):
        refs = list(refs)
        row_refs = [refs.pop(0) for _ in rows]
        g_ref, w_ref = refs.pop(0), refs.pop(0)
        u_ref = refs.pop(0) if u is not None else None
        del refs[:len(after)]
        out_ref, a_ref = refs.pop(0), refs.pop(0)
        dg_ref = refs.pop(0) if bwd else None
        a_all = refs.pop(0)
        t = pl.program_id(0)

        @pl.when(t < ni)
        def _():
            if bwd:
                a, dg = _rms_bwd(row_refs[0][...], g_ref[...], row_refs[1][...])
                dg_ref[0] = dg
            else:
                a = _rms(row_refs[0][...], g_ref[...])
            a_ref[...] = a.astype(BF16)
            a_all[pl.ds(pl.multiple_of(t * tm, tm), tm), :] = a.astype(BF16)

        @pl.when(t >= ni)
        def _():
            acc = _dot(a_all[...], w_ref[...], dims)
            if u_ref is not None:
                acc = acc * (2.0 * jnp.maximum(u_ref[...], 0.0))
            out_ref[...] = acc.astype(out_dtype)

    rows_at = lambda t: jnp.minimum(t, ni - 1)
    cols_at = lambda t: jnp.maximum(t - ni, 0)
    row_spec = pl.BlockSpec((tm, D), lambda t: (rows_at(t), 0))
    w_spec = (pl.BlockSpec((tn, D), lambda t: (cols_at(t), 0)) if dims == NT
              else pl.BlockSpec((D, tn), lambda t: (0, cols_at(t))))
    tile = pl.BlockSpec((s, tn), lambda t: (0, cols_at(t)))
    in_specs = [row_spec] * len(rows) + [pl.BlockSpec((1, D), lambda t: (0, 0)), w_spec]
    in_specs += ([tile] if u is not None else []) + [ANY] * len(after)
    out_specs = [tile, row_spec]
    if bwd:
        out_specs.append(pl.BlockSpec((1, 1, D), lambda t: (rows_at(t), 0, 0)))
    return pl.pallas_call(
        body, name=name, grid=(ni + nj,), in_specs=in_specs, out_specs=out_specs, out_shape=out_shape,
        scratch_shapes=[pltpu.VMEM((s, D), BF16)], compiler_params=_cparams(),
    )(*rows, g, w, *([u] if u is not None else []), *after)


def _mm_rowop(name, kind, pairs, rows, g, relu2=False, after=()):
    s, kdim = pairs[0][0].shape
    tm, tk = _k_tiles(kdim)
    tm = min(tm, s)
    ni, nk = s // tm, kdim // tk
    npair = len(pairs)

    def body(*refs):
        refs = list(refs)
        a_refs = [refs.pop(0) for _ in range(npair)]
        w_refs = [refs.pop(0) for _ in range(npair)]
        row_refs = [refs.pop(0) for _ in rows]
        g_ref = refs.pop(0)
        del refs[:len(after)]
        acc = refs.pop()
        outs = refs
        k = pl.program_id(1)

        @pl.when(k == 0)
        def _():
            acc[...] = jnp.zeros_like(acc)

        for a_ref, w_ref, (_, _, dims) in zip(a_refs, w_refs, pairs):
            a = a_ref[...]
            if relu2:
                a = jnp.square(jnp.maximum(a, 0.0))
            acc[...] += _dot(a.astype(BF16), w_ref[...], dims)

        @pl.when(k == nk - 1)
        def _():
            h = acc[...]
            if kind == "rms_res":
                outs[0][...] = row_refs[0][...] + _rms(h, g_ref[...])
                outs[1][...] = h
            else:
                dx, dg = _rms_bwd(row_refs[0][...], g_ref[...], h)
                if kind == "rms_bwd_res":
                    outs[0][...] = row_refs[1][...] + dx
                    outs[1][0] = dg
                else:
                    outs[0][0] = dg

    row_spec = pl.BlockSpec((tm, D), lambda i, k: (i, 0))
    dg_shape = jax.ShapeDtypeStruct((ni, 1, D), F32)
    dg_spec = pl.BlockSpec((1, 1, D), lambda i, k: (i, 0, 0))
    in_specs = [pl.BlockSpec((tm, tk), lambda i, k: (i, k))] * npair
    for _, _, dims in pairs:
        in_specs.append(pl.BlockSpec((tk, D), lambda i, k: (k, 0)) if dims == NN
                        else pl.BlockSpec((D, tk), lambda i, k: (0, k)))
    in_specs += [row_spec] * len(rows) + [pl.BlockSpec((1, D), lambda i, k: (0, 0))] + [ANY] * len(after)
    if kind == "rms_res":
        out_shape = [jax.ShapeDtypeStruct((s, D), F32)] * 2
        out_specs = [row_spec, row_spec]
    elif kind == "rms_bwd_res":
        out_shape = [jax.ShapeDtypeStruct((s, D), F32), dg_shape]
        out_specs = [row_spec, dg_spec]
    else:
        out_shape = [dg_shape]
        out_specs = [dg_spec]
    return pl.pallas_call(
        body, name=name, grid=(ni, nk), in_specs=in_specs, out_specs=out_specs, out_shape=out_shape,
        scratch_shapes=[pltpu.VMEM((tm, D), F32)], compiler_params=_cparams(),
    )(*[p[0] for p in pairs], *[p[1] for p in pairs], *rows, g, *after)


def _mm_tn(name, a, gmat, relu2=False, after=()):
    s, m = a.shape
    tm = min(DW_TILE, m)
    ni = m // tm

    def body(a_ref, g_ref, *rest):
        av = a_ref[...]
        if relu2:
            av = jnp.square(jnp.maximum(av, 0.0))
        rest[len(after)][...] = _dot(av.astype(BF16), g_ref[...], TN).astype(BF16)

    return pl.pallas_call(
        body, name=name, grid=(ni,),
        in_specs=[pl.BlockSpec((s, tm), lambda i: (0, i)), pl.BlockSpec((s, D), lambda i: (0, 0))] + [ANY] * len(after),
        out_specs=pl.BlockSpec((tm, D), lambda i: (i, 0)),
        out_shape=jax.ShapeDtypeStruct((m, D), BF16), compiler_params=_cparams(),
    )(a, gmat, *after)


def _tril():
    r = lax.broadcasted_iota(jnp.int32, (HD, HD), 0)
    c = lax.broadcasted_iota(jnp.int32, (HD, HD), 1)
    return (c <= r).astype(F32)


def _gmlp_fwd(z, gv, ws, bst, tb):
    s = z.shape[0]
    tb = min(tb, s)

    def body(zu_ref, zv_ref, gv_ref, ws_ref, bst_ref, y_ref):
        tril = _tril()
        for h in range(NH):
            cols = slice(h * HD, (h + 1) * HD)
            u, vln = _gmlp_rows(zu_ref[:, cols], zv_ref[:, cols], gv_ref[h:h + 1, :])
            wm = (ws_ref[h] * tril).astype(BF16)
            vb = vln.astype(BF16)
            for c in range(tb // HD):
                rws = slice(c * HD, (c + 1) * HD)
                mixed = _dot(wm, vb[rws], NN) + bst_ref[:, h:h + 1]
                y_ref[rws, cols] = (u[rws] * mixed).astype(BF16)

    return pl.pallas_call(
        body, name="gmlp_fwd", grid=(s // tb,),
        in_specs=[pl.BlockSpec((tb, GW), lambda i: (i, 0)), pl.BlockSpec((tb, GW), lambda i: (i, 1)),
                  pl.BlockSpec((NH, HD), lambda i: (0, 0)), pl.BlockSpec((NH, HD, HD), lambda i: (0, 0, 0)),
                  pl.BlockSpec((HD, NH), lambda i: (0, 0))],
        out_specs=pl.BlockSpec((tb, GW), lambda i: (i, 0)),
        out_shape=jax.ShapeDtypeStruct((s, D), BF16), compiler_params=_cparams(),
    )(z, z, gv, ws, bst)


def _gmlp_bwd(z, dy, gv, ws, bst, tb, after=()):
    s = z.shape[0]
    tb = min(tb, s)
    nb = s // tb

    def body(zu_ref, zv_ref, dy_ref, gv_ref, ws_ref, bst_ref, *rest):
        dz_ref, dgv_ref, dws_ref, db_ref = rest[len(after):]
        tril = _tril()
        for h in range(NH):
            cols = slice(h * HD, (h + 1) * HD)
            (u, vln), vjp = jax.vjp(_gmlp_rows, zu_ref[:, cols], zv_ref[:, cols], gv_ref[h:h + 1, :])
            wmf = ws_ref[h] * tril
            wm = wmf.astype(BF16)
            wmt = wmf.T.astype(BF16)
            vb = vln.astype(BF16)
            dws = jnp.zeros((HD, HD), F32)
            db = jnp.zeros((HD, 1), F32)
            du_parts, dvln_parts = [], []
            for c in range(tb // HD):
                rws = slice(c * HD, (c + 1) * HD)
                mixed = _dot(wm, vb[rws], NN) + bst_ref[:, h:h + 1]
                dyc = dy_ref[rws, cols]
                du_parts.append(dyc * mixed)
                dmixed = dyc * u[rws]
                dmb = dmixed.astype(BF16)
                dws = dws + _dot(dmb, vb[rws], NT)
                db = db + jnp.sum(dmixed, axis=1, keepdims=True)
                dvln_parts.append(_dot(wmt, dmb, NN))
            du = jnp.concatenate(du_parts, axis=0)
            dvln = jnp.concatenate(dvln_parts, axis=0)
            dzu, dzv, dgv = vjp((du, dvln))
            dz_ref[:, cols] = dzu.astype(BF16)
            dz_ref[:, slice(GW + h * HD, GW + (h + 1) * HD)] = dzv.astype(BF16)
            dgv_ref[0, h:h + 1, :] = dgv
            dws_ref[0, h] = dws * tril
            db_ref[0, h] = jnp.broadcast_to(db, (HD, LANES))

    blk = pl.BlockSpec((tb, GW), lambda i: (i, 0))
    return pl.pallas_call(
        body, name="gmlp_bwd", grid=(nb,),
        in_specs=[blk, pl.BlockSpec((tb, GW), lambda i: (i, 1)), blk,
                  pl.BlockSpec((NH, HD), lambda i: (0, 0)), pl.BlockSpec((NH, HD, HD), lambda i: (0, 0, 0)),
                  pl.BlockSpec((HD, NH), lambda i: (0, 0))] + [ANY] * len(after),
        out_specs=[pl.BlockSpec((tb, 2 * GW), lambda i: (i, 0)), pl.BlockSpec((1, NH, HD), lambda i: (i, 0, 0)),
                   pl.BlockSpec((1, NH, HD, HD), lambda i: (i, 0, 0, 0)),
                   pl.BlockSpec((1, NH, HD, LANES), lambda i: (i, 0, 0, 0))],
        out_shape=[jax.ShapeDtypeStruct((s, IN_COLS), BF16),
                   jax.ShapeDtypeStruct((nb, NH, HD), F32), jax.ShapeDtypeStruct((nb, NH, HD, HD), F32),
                   jax.ShapeDtypeStruct((nb, NH, HD, LANES), F32)],
        compiler_params=_cparams(),
    )(z, z, dy, gv, ws, bst, *after)


POOL_TILE = 1024


def _pool_count(t0, window):
    pos = (t0 + lax.broadcasted_iota(jnp.int32, (POOL_TILE, LANES), 0)).astype(F32)
    return jnp.minimum(pos + 1.0, float(window))


def _window_sum(win, levels, back):
    n = win.shape[0]
    for lv in range(levels):
        step = 1 << lv
        win = win + pltpu.roll(win, n - step if back else step, 0)
    return win


def _pool_pooled(ppad_ref, t0, g):
    win = ppad_ref[pl.ds(t0, POOL_TILE + POOL_HALO), :]
    wsum = _window_sum(win, g + 1, False)[POOL_HALO:]
    return wsum / _pool_count(t0, POOL_WINDOWS[g]) - win[POOL_HALO:]


def _pool_fwd(z, wp, sp, y):
    s = z.shape[0]
    nt = s // POOL_TILE

    def body(p_ref, wp_ref, sp_ref, _, y_ref, ppad):
        for g in range(NG):
            cols = slice(g * LANES, (g + 1) * LANES)
            ppad[pl.ds(0, POOL_HALO), :] = jnp.zeros((POOL_HALO, LANES), F32)
            ppad[pl.ds(POOL_HALO, s), :] = p_ref[:, cols]
            wpb = wp_ref[g].astype(BF16)
            scale = sp_ref[:, cols]

            def tile(t, carry):
                t0 = pl.multiple_of(t * POOL_TILE, POOL_TILE)
                pooled = _pool_pooled(ppad, t0, g)
                y_ref[pl.ds(t0, POOL_TILE), cols] = (_dot(pooled.astype(BF16), wpb, NN) * scale).astype(BF16)
                return carry

            lax.fori_loop(0, nt, tile, 0)

    return pl.pallas_call(
        body, name="pool_fwd", grid=(1,),
        in_specs=[pl.BlockSpec((s, PW), lambda i: (0, 2 * GW // PW)),
                  pl.BlockSpec((NG, LANES, LANES), lambda i: (0, 0, 0)), pl.BlockSpec((1, PW), lambda i: (0, 0)), ANY],
        out_specs=pl.BlockSpec((s, PW), lambda i: (0, GW // PW)),
        out_shape=jax.ShapeDtypeStruct((s, D), BF16), input_output_aliases={3: 0},
        scratch_shapes=[pltpu.VMEM((s + POOL_HALO, LANES), F32)], compiler_params=_cparams(),
    )(z, wp, sp, y)


def _pool_bwd(z, dy, wp, sp, dz):
    s = z.shape[0]
    nt = s // POOL_TILE

    def body(p_ref, dy_ref, wp_ref, sp_ref, _, dp_ref, dwp_ref, dsp_ref, ppad, rpad, dpool):
        for g in range(NG):
            cols = slice(g * LANES, (g + 1) * LANES)
            ppad[pl.ds(0, POOL_HALO), :] = jnp.zeros((POOL_HALO, LANES), F32)
            ppad[pl.ds(POOL_HALO, s), :] = p_ref[:, cols]
            rpad[pl.ds(s, POOL_HALO), :] = jnp.zeros((POOL_HALO, LANES), F32)
            wpb = wp_ref[g].astype(BF16)
            scale = sp_ref[:, cols]

            def tile(t, carry):
                dwp, dsp = carry
                t0 = pl.multiple_of(t * POOL_TILE, POOL_TILE)
                pooled = _pool_pooled(ppad, t0, g)
                pb = pooled.astype(BF16)
                dyt = dy_ref[pl.ds(t0, POOL_TILE), cols]
                dsp = dsp + jnp.sum(dyt * _dot(pb, wpb, NN), axis=0, keepdims=True)
                dmm = (dyt * scale).astype(BF16)
                dwp = dwp + _dot(pb, dmm, TN)
                dpooled = _dot(dmm, wpb, NT)
                rpad[pl.ds(t0, POOL_TILE), :] = dpooled / _pool_count(t0, POOL_WINDOWS[g])
                dpool[pl.ds(t0, POOL_TILE), :] = dpooled
                return dwp, dsp

            dwp, dsp = lax.fori_loop(0, nt, tile, (jnp.zeros((LANES, LANES), F32), jnp.zeros((1, LANES), F32)))
            dwp_ref[g] = dwp
            dsp_ref[:, cols] = dsp

            def tile2(t, carry):
                t0 = pl.multiple_of(t * POOL_TILE, POOL_TILE)
                win = rpad[pl.ds(t0, POOL_TILE + POOL_HALO), :]
                back = _window_sum(win, g + 1, True)[:POOL_TILE]
                rows = pl.ds(t0, POOL_TILE)
                dp_ref[rows, cols] = (back - dpool[rows, :]).astype(BF16)
                return carry

            lax.fori_loop(0, nt, tile2, 0)

    return pl.pallas_call(
        body, name="pool_bwd", grid=(1,),
        in_specs=[pl.BlockSpec((s, PW), lambda i: (0, 2 * GW // PW)), pl.BlockSpec((s, PW), lambda i: (0, GW // PW)),
                  pl.BlockSpec((NG, LANES, LANES), lambda i: (0, 0, 0)), pl.BlockSpec((1, PW), lambda i: (0, 0)), ANY],
        out_specs=[pl.BlockSpec((s, PW), lambda i: (0, 2 * GW // PW)),
                   pl.BlockSpec((NG, LANES, LANES), lambda i: (0, 0, 0)), pl.BlockSpec((1, PW), lambda i: (0, 0))],
        out_shape=[jax.ShapeDtypeStruct((s, IN_COLS), BF16), jax.ShapeDtypeStruct((NG, LANES, LANES), F32),
                   jax.ShapeDtypeStruct((1, PW), F32)],
        input_output_aliases={4: 0},
        scratch_shapes=[pltpu.VMEM((s + POOL_HALO, LANES), F32), pltpu.VMEM((s + POOL_HALO, LANES), F32),
                        pltpu.VMEM((s, LANES), F32)],
        compiler_params=_cparams(),
    )(z, dy, wp, sp, dz)


CONV_LEAD = CONV_HALO - (CONV_K - 1)


SUBLANES = 8


def _sublane_shifts(win):
    n = win.shape[0]
    return [win] + [pltpu.roll(win, n - b, 0) for b in range(1, SUBLANES)]


def _shifted(shifts, offset):
    a, b = divmod(offset, SUBLANES)
    return shifts[b][a * SUBLANES:a * SUBLANES + ROW_TILE]


def _conv_taps(shifts, wdw_ref, lead, reverse):
    acc = jnp.zeros((ROW_TILE, CW), F32)
    for j in range(CONV_K):
        tap = (CONV_K - 1 - j) if reverse else j
        acc = acc + wdw_ref[tap:tap + 1, :] * _shifted(shifts, lead + j)
    return acc


def _conv_fill_glu(cv_ref, cg_ref, xpad, s):
    xpad[pl.ds(0, CONV_HALO), :] = jnp.zeros((CONV_HALO, CW), F32)

    def fill(t, carry):
        t0 = pl.multiple_of(t * ROW_TILE, ROW_TILE)
        rows = pl.ds(t0, ROW_TILE)
        xpad[pl.ds(t0 + CONV_HALO, ROW_TILE), :] = _glu(cv_ref[rows, :], cg_ref[rows, :])
        return carry

    lax.fori_loop(0, s // ROW_TILE, fill, 0)


def _conv_fwd(z, wdw, bdw, lng, lnb, y):
    s = z.shape[0]

    def body(cv_ref, cg_ref, wdw_ref, bdw_ref, lng_ref, lnb_ref, _, y_ref, xpad):
        _conv_fill_glu(cv_ref, cg_ref, xpad, s)

        def tile(t, carry):
            t0 = pl.multiple_of(t * ROW_TILE, ROW_TILE)
            shifts = _sublane_shifts(xpad[pl.ds(t0, ROW_TILE + CONV_HALO), :])
            hc = _conv_taps(shifts, wdw_ref, CONV_LEAD, False) + bdw_ref[...]
            y_ref[pl.ds(t0, ROW_TILE), :] = _ln_silu(hc, lng_ref[...], lnb_ref[...]).astype(BF16)
            return carry

        lax.fori_loop(0, s // ROW_TILE, tile, 0)

    vec = pl.BlockSpec((1, CW), lambda i: (0, 0))
    return pl.pallas_call(
        body, name="conv_fwd", grid=(1,),
        in_specs=[pl.BlockSpec((s, CW), lambda i: (0, (2 * GW + PW) // CW)),
                  pl.BlockSpec((s, CW), lambda i: (0, (2 * GW + PW) // CW + 1)),
                  pl.BlockSpec((CONV_K + 1, CW), lambda i: (0, 0)), vec, vec, vec, ANY],
        out_specs=pl.BlockSpec((s, CW), lambda i: (0, (GW + PW) // CW)),
        out_shape=jax.ShapeDtypeStruct((s, D), BF16), input_output_aliases={6: 0},
        scratch_shapes=[pltpu.VMEM((s + CONV_HALO, CW), F32)], compiler_params=_cparams(),
    )(z, z, wdw, bdw, lng, lnb, y)


def _conv_bwd(z, dy, wdw, bdw, lng, lnb, dz):
    s = z.shape[0]

    def body(cv_ref, cg_ref, dy_ref, wdw_ref, bdw_ref, lng_ref, lnb_ref, _,
             dz_ref, dwdw_ref, dbdw_ref, dlng_ref, dlnb_ref, xpad, dpad, dcg_keep):
        @pl.when(pl.program_id(0) == 0)
        def _():
            compute(cv_ref, cg_ref, dy_ref, wdw_ref, bdw_ref, lng_ref, lnb_ref,
                    dz_ref, dcg_keep, dwdw_ref, dbdw_ref, dlng_ref, dlnb_ref, xpad, dpad)

        @pl.when(pl.program_id(0) == 1)
        def _():
            dz_ref[...] = dcg_keep[...]

    def compute(cv_ref, cg_ref, dy_ref, wdw_ref, bdw_ref, lng_ref, lnb_ref,
                dcv_ref, dcg_ref, dwdw_ref, dbdw_ref, dlng_ref, dlnb_ref, xpad, dpad):
        _conv_fill_glu(cv_ref, cg_ref, xpad, s)
        dpad[pl.ds(s, CONV_HALO), :] = jnp.zeros((CONV_HALO, CW), F32)
        dwdw_ref[...] = jnp.zeros((CONV_K + 1, CW), F32)

        def tile(t, carry):
            db, dg, dbeta = carry
            t0 = pl.multiple_of(t * ROW_TILE, ROW_TILE)
            shifts = _sublane_shifts(xpad[pl.ds(t0, ROW_TILE + CONV_HALO), :])
            hc = _conv_taps(shifts, wdw_ref, CONV_LEAD, False) + bdw_ref[...]
            _, vjp = jax.vjp(_ln_silu, hc, lng_ref[...], lnb_ref[...])
            dhc, dg_t, dbeta_t = vjp(dy_ref[pl.ds(t0, ROW_TILE), :])
            dpad[pl.ds(t0, ROW_TILE), :] = dhc
            for j in range(CONV_K):
                dwdw_ref[j:j + 1, :] += jnp.sum(dhc * _shifted(shifts, CONV_LEAD + j), axis=0, keepdims=True)
            return db + jnp.sum(dhc, axis=0, keepdims=True), dg + dg_t, dbeta + dbeta_t

        zero = jnp.zeros((1, CW), F32)
        db, dg, dbeta = lax.fori_loop(0, s // ROW_TILE, tile, (zero, zero, zero))
        dbdw_ref[...] = db
        dlng_ref[...] = dg
        dlnb_ref[...] = dbeta

        def tile2(t, carry):
            t0 = pl.multiple_of(t * ROW_TILE, ROW_TILE)
            rows = pl.ds(t0, ROW_TILE)
            dglu = _conv_taps(_sublane_shifts(dpad[pl.ds(t0, ROW_TILE + CONV_HALO), :]), wdw_ref, 0, True)
            _, vjp = jax.vjp(_glu, cv_ref[rows, :], cg_ref[rows, :])
            dcv, dcg = vjp(dglu)
            dcv_ref[rows, :] = dcv.astype(BF16)
            dcg_ref[rows, :] = dcg.astype(BF16)
            return carry

        lax.fori_loop(0, s // ROW_TILE, tile2, 0)

    vec = pl.BlockSpec((1, CW), lambda i: (0, 0))
    wspec = pl.BlockSpec((CONV_K + 1, CW), lambda i: (0, 0))
    vshape = jax.ShapeDtypeStruct((1, CW), F32)
    return pl.pallas_call(
        body, name="conv_bwd", grid=(2,),
        in_specs=[pl.BlockSpec((s, CW), lambda i: (0, (2 * GW + PW) // CW)),
                  pl.BlockSpec((s, CW), lambda i: (0, (2 * GW + PW) // CW + 1)),
                  pl.BlockSpec((s, CW), lambda i: (0, (GW + PW) // CW)), wspec, vec, vec, vec, ANY],
        out_specs=[pl.BlockSpec((s, CW), lambda i: (0, (2 * GW + PW) // CW + i)), wspec, vec, vec, vec],
        out_shape=[jax.ShapeDtypeStruct((s, IN_COLS), BF16), jax.ShapeDtypeStruct((CONV_K + 1, CW), F32),
                   vshape, vshape, vshape],
        input_output_aliases={7: 0},
        scratch_shapes=[pltpu.VMEM((s + CONV_HALO, CW), F32), pltpu.VMEM((s + CONV_HALO, CW), F32),
                        pltpu.VMEM((s, CW), BF16)],
        compiler_params=_cparams(),
    )(z, z, dy, wdw, bdw, lng, lnb, dz)


def _softmax_rows(sc):
    e = jnp.exp(sc - jnp.max(sc, axis=-1, keepdims=True))
    return e / jnp.sum(e, axis=-1, keepdims=True)


def _attn_fwd(q, k, v, tq):
    s, m = q.shape[0], k.shape[0]
    tq = min(tq, s)

    def body(q_ref, k_ref, v_ref, o_ref):
        for h in range(XH):
            cols = slice(h * XHD, (h + 1) * XHD)
            p = _softmax_rows(_dot(q_ref[:, cols], k_ref[:, cols], NT) * ATT_SCALE)
            o_ref[:, cols] = _dot(p.astype(BF16), v_ref[:, cols], NN).astype(BF16)

    kv = pl.BlockSpec((m, D), lambda i: (0, 0))
    return pl.pallas_call(
        body, name="attn_fwd", grid=(s // tq,),
        in_specs=[pl.BlockSpec((tq, D), lambda i: (i, 0)), kv, kv],
        out_specs=pl.BlockSpec((tq, D), lambda i: (i, 0)),
        out_shape=jax.ShapeDtypeStruct((s, D), BF16), compiler_params=_cparams(),
    )(q, k, v)


def _attn_bwd(q, k, v, do, tq, after=()):
    s, m = q.shape[0], k.shape[0]
    tq = min(tq, s)

    def body(q_ref, k_ref, v_ref, do_ref, *rest):
        dq_ref, dk_ref, dv_ref = rest[len(after):]

        @pl.when(pl.program_id(0) == 0)
        def _():
            dk_ref[...] = jnp.zeros_like(dk_ref)
            dv_ref[...] = jnp.zeros_like(dv_ref)

        for h in range(XH):
            cols = slice(h * XHD, (h + 1) * XHD)
            qh, kh, vh, doh = q_ref[:, cols], k_ref[:, cols], v_ref[:, cols], do_ref[:, cols]
            p = _softmax_rows(_dot(qh, kh, NT) * ATT_SCALE)
            dp = _dot(doh, vh, NT)
            dv_ref[:, cols] += _dot(p.astype(BF16), doh, TN)
            ds = (p * (dp - jnp.sum(p * dp, axis=-1, keepdims=True)) * ATT_SCALE).astype(BF16)
            dq_ref[:, cols] = _dot(ds, kh, NN).astype(BF16)
            dk_ref[:, cols] += _dot(ds, qh, TN)

    kv = pl.BlockSpec((m, D), lambda i: (0, 0))
    qs = pl.BlockSpec((tq, D), lambda i: (i, 0))
    return pl.pallas_call(
        body, name="attn_bwd", grid=(s // tq,),
        in_specs=[qs, kv, kv, qs] + [ANY] * len(after), out_specs=[qs, kv, kv],
        out_shape=[jax.ShapeDtypeStruct((s, D), BF16), jax.ShapeDtypeStruct((m, D), F32),
                   jax.ShapeDtypeStruct((m, D), F32)],
        compiler_params=_cparams(),
    )(q, k, v, do, *after)


def _loss_head(y, target, tm):
    s = y.shape[0]
    tm = min(tm, s)

    def body(y_ref, t_ref, dy_ref, part_ref):
        err = y_ref[...] - t_ref[...]
        dy_ref[...] = err * (1.0 / D)
        part_ref[...] = jnp.full((1, 8, LANES), 0.5 * jnp.sum(err * err) * (1.0 / D), F32)

    blk = pl.BlockSpec((tm, D), lambda i: (i, 0))
    return pl.pallas_call(
        body, name="loss_head", grid=(s // tm,), in_specs=[blk, blk],
        out_specs=[blk, pl.BlockSpec((1, 8, LANES), lambda i: (i, 0, 0))],
        out_shape=[jax.ShapeDtypeStruct((s, D), F32), jax.ShapeDtypeStruct((s // tm, 8, LANES), F32)],
        compiler_params=_cparams(),
    )(y, target)


def _layer_fwd(x0, mem, w, p, fetch):
    z, hn0 = _rowop_mm("mix_in", "rms", (x0,), p["norm_mix_pre"], w["w_in"], NT, F32)
    y = _gmlp_fwd(z, p["gmlp_v_gain"], p["w_spatial"], p["b_spatial_t"], 1024)
    y = _pool_fwd(z, p["w_pool"], p["s_pool"], y)
    y = _conv_fwd(z, p["w_dw"], p["b_dw"], p["conv_ln_g"], p["conv_ln_b"], y)
    w.update(fetch("out", (y,)))
    x1, h0 = _mm_rowop("mix_out", "rms_res", [(y, w["w_out"], NN)], (x0,), p["norm_mix_post"])
    w.update(fetch("att", (x1,)))
    q, hn1 = _rowop_mm("att_q", "rms", (x1,), p["norm_xattn_pre"], w["w_q"], NN, BF16)
    k, mn = _rowop_mm("att_k", "rms", (mem,), p["norm_mem"], w["w_k"], NN, BF16, after=(x1,))
    v, _ = _rowop_mm("att_v", "rms", (mem,), p["norm_mem"], w["w_v"], NN, BF16, after=(x1,))
    o = _attn_fwd(q, k, v, 1024)
    x2, h1 = _mm_rowop("att_o", "rms_res", [(o, w["w_o"], NN)], (x1,), p["norm_xattn_post"])
    w.update(fetch("up", (x2,)))
    u, hn2 = _rowop_mm("ffn_up", "rms", (x2,), p["norm_ffn_pre"], w["w_up"], NT, F32)
    w.update(fetch("down", (u,)))
    x3, h2 = _mm_rowop("ffn_down", "rms_res", [(u, w["w_down"], NN)], (x2,), p["norm_ffn_post"], relu2=True)
    saved = dict(x0=x0, z=z, hn0=hn0, y=y, h0=h0, x1=x1, q=q, hn1=hn1, k=k, v=v, mn=mn, o=o, h1=h1, x2=x2, u=u,
                 hn2=hn2, h2=h2)
    return x3, saved


def _layer_bwd(dx3, mem, w, p, sv, red):
    gs = {}
    du, dh2, dg = _rowop_mm("ffn_down_bwd", "rms_bwd", (sv["h2"], dx3), p["norm_ffn_post"], w["w_down"], NT, BF16,
                            u=sv["u"], after=red.after())
    gs["norm_ffn_post"] = jnp.sum(dg, axis=0)
    g_down = _mm_tn("ffn_down_dw", sv["u"], dh2, relu2=True)
    red.advance((g_down,))
    dx2, dg = _mm_rowop("ffn_up_bwd", "rms_bwd_res", [(du, w["w_up"], NN)], (sv["x2"], dx3), p["norm_ffn_pre"],
                        after=red.after())
    gs["norm_ffn_pre"] = jnp.sum(dg, axis=0)
    g_up = _mm_tn("ffn_up_dw", du, sv["hn2"])
    red.add("ffn", ("w_down", "w_up"), [g_down, g_up])
    do, dh1, dg = _rowop_mm("att_o_bwd", "rms_bwd", (sv["h1"], dx2), p["norm_xattn_post"], w["w_o"], NT, BF16,
                            after=red.after())
    gs["norm_xattn_post"] = jnp.sum(dg, axis=0)
    g_o = _mm_tn("att_o_dw", sv["o"], dh1)
    red.advance((g_o,))
    dq, dk, dv = _attn_bwd(sv["q"], sv["k"], sv["v"], do, 1024, after=red.after())
    dk, dv = dk.astype(BF16), dv.astype(BF16)
    dx1, dg = _mm_rowop("att_q_bwd", "rms_bwd_res", [(dq, w["w_q"], NT)], (sv["x1"], dx2), p["norm_xattn_pre"],
                        after=red.after())
    gs["norm_xattn_pre"] = jnp.sum(dg, axis=0)
    g_q = _mm_tn("att_q_dw", sv["hn1"], dq)
    g_k = _mm_tn("att_k_dw", sv["mn"], dk)
    g_v = _mm_tn("att_v_dw", sv["mn"], dv)
    (dg,) = _mm_rowop("att_kv_bwd", "rms_bwd_gain", [(dk, w["w_k"], NT), (dv, w["w_v"], NT)], (mem,), p["norm_mem"])
    gs["norm_mem"] = jnp.sum(dg, axis=0)
    red.add("att", ("w_o", "w_q", "w_k", "w_v"), [g_o, g_q, g_k, g_v])
    dy, dh0, dg = _rowop_mm("mix_out_bwd", "rms_bwd", (sv["h0"], dx1), p["norm_mix_post"], w["w_out"], NT, F32,
                            after=red.after())
    gs["norm_mix_post"] = jnp.sum(dg, axis=0)
    g_out = _mm_tn("mix_out_dw", sv["y"], dh0)
    red.advance((g_out,))
    red.add("out", ("w_out",), [g_out])
    z = sv["z"]
    dz, dgv, dws, dbs = _gmlp_bwd(z, dy, p["gmlp_v_gain"], p["w_spatial"], p["b_spatial_t"], 512, after=red.after())
    gs["gmlp_v_gain"] = jnp.sum(dgv, axis=0)
    gs["w_spatial"] = jnp.sum(dws, axis=0)
    gs["b_spatial"] = jnp.sum(dbs[..., 0], axis=0)
    dz, gs["w_pool"], gs["s_pool"] = _pool_bwd(z, dy, p["w_pool"], p["s_pool"], dz)
    dz, dwdw, gs["b_dw"], gs["conv_ln_g"], gs["conv_ln_b"] = _conv_bwd(
        z, dy, p["w_dw"], p["b_dw"], p["conv_ln_g"], p["conv_ln_b"], dz)
    red.advance((dz,))
    g_in = _mm_tn("mix_in_dw", dz, sv["hn0"], after=red.after())
    red.add("in", ("w_in",), [g_in])
    if red.layer == 0:
        red.advance(())
    red.small("mixer", _small_grad_arrays(gs, dwdw, norms=False))
    dx0, dg = _mm_rowop("mix_in_bwd", "rms_bwd_res", [(dz, w["w_in"], NN)], (sv["x0"], dx1), p["norm_mix_pre"],
                        after=red.after())
    gs["norm_mix_pre"] = jnp.sum(dg, axis=0)
    late = {"norms": jnp.concatenate([gs[n] for n in NORM_NAMES], axis=0)}
    if red.layer == 0:
        late["loss"] = red.extra[0]
    red.small("norms", late)
    return dx0


NORM_NAMES = ("norm_mix_pre", "norm_mix_post", "norm_xattn_pre", "norm_mem", "norm_xattn_post", "norm_ffn_pre",
              "norm_ffn_post")
VEC_NAMES = ("s_pool", "b_dw", "conv_ln_g", "conv_ln_b")
SMALL_ARRAYS = ("norms", "gain_bias", "w_spatial", "w_pool", "vecs", "w_dw")


def _small_grad_arrays(gs, dwdw, norms=True):
    out = {"norms": jnp.concatenate([gs[n] for n in NORM_NAMES], axis=0)} if norms else {}
    out.update({"gain_bias": jnp.concatenate([gs["gmlp_v_gain"], gs["b_spatial"]], axis=0),
                "w_spatial": gs["w_spatial"], "w_pool": gs["w_pool"],
                "vecs": jnp.concatenate([gs[n] for n in VEC_NAMES], axis=0), "w_dw": dwdw})
    return out


def _layer_params(small, l):
    p = {n: small[n][l].reshape(1, -1) for n in ("norm_mix_pre", "norm_mix_post", "s_pool", "b_dw", "conv_ln_g",
                                                   "conv_ln_b", "norm_xattn_pre", "norm_mem", "norm_xattn_post",
                                                   "norm_ffn_pre", "norm_ffn_post")}
    p["gmlp_v_gain"] = small["gmlp_v_gain"][l]
    p["w_spatial"] = small["w_spatial"][l]
    p["b_spatial_t"] = small["b_spatial"][l].T
    p["w_pool"] = small["w_pool"][l]
    p["w_dw"] = jnp.pad(small["w_dw"][l], ((0, 1), (0, 0)))
    return p


def _local_step(x, mem, target, fetch, small, red):
    small = dict(small)
    saved, weights, params = [], [], []
    h = x
    marker = ()
    for l in range(DEPTH):
        w = fetch(l, "in", marker)
        if "taps" in w:
            small["w_dw"] = w.pop("taps")
        p = _layer_params(small, l)
        h, sv = _layer_fwd(h, mem, w, p, functools.partial(fetch, l))
        marker = (h,)
        saved.append(sv)
        weights.append(w)
        params.append(p)
    dh, loss = _loss_head(h, target, 1024)
    red.extra = (loss,)
    for l in reversed(range(DEPTH)):
        red.layer = l
        dh = _layer_bwd(dh, mem, weights[l], params[l], saved[l], red)
    return loss, dh


HBM = pl.BlockSpec(memory_space=pltpu.HBM)


def _position():
    return lax.axis_index("x"), lax.axis_index("y"), lax.axis_index("c")


SEM = pl.BlockSpec(memory_space=pltpu.SEMAPHORE)
EFFECT = pltpu.SideEffectType.DATAFLOW_SIDE_EFFECTING
TOKEN = jax.ShapeDtypeStruct((8, LANES), F32)
TOKEN_SPEC = pl.BlockSpec(memory_space=pltpu.VMEM)


def _landing(shape, dtype):
    return pltpu.with_memory_space_constraint(lax.empty(shape, dtype), pltpu.HBM)


def _hbm_shapes(arrays):
    return [pltpu.HBM(a.shape, a.dtype) for a in arrays]


def _block(ref, r, dev):
    return ref.at[pl.ds((4 * dev[0] + 2 * dev[1] + dev[2]) * r, r), :]


def _split_call(name, body, thru, sems_in, after, sems_out, token):
    n = len(thru)
    out_shape = [pltpu.SemaphoreType.DMA(s) for s in sems_out] + _hbm_shapes(thru) + ([TOKEN] if token else [])
    out_specs = [SEM] * len(sems_out) + [HBM] * n + ([TOKEN_SPEC] if token else [])
    return pl.pallas_call(
        body, name=name, in_specs=[HBM] * n + [SEM] * len(sems_in) + [ANY] * len(after),
        out_specs=out_specs, out_shape=out_shape,
        input_output_aliases={i: len(sems_out) + i for i in range(n)},
        compiler_params=pltpu.CompilerParams(has_side_effects=EFFECT),
    )(*thru, *sems_in, *after)


def _place_own(name, srcs, dev, out_dtype, tr):
    n = len(srcs)
    r, cols = srcs[0][0].shape[-2:]
    tr = r if r < 16 else _row_tile(r, tr)
    nb = r // tr

    def body(dev_ref, *refs):
        for a in range(n):
            refs[n + a][...] = refs[a][...].astype(out_dtype)

    in_specs = [pl.BlockSpec((tr, cols), lambda i, d: (i, 0)) if l is None
                else pl.BlockSpec((None, tr, cols), lambda i, d, l=l: (l, i, 0)) for _, l in srcs]
    return pl.pallas_call(
        body, name=name,
        grid_spec=pltpu.PrefetchScalarGridSpec(
            num_scalar_prefetch=1, grid=(nb,), in_specs=in_specs,
            out_specs=[pl.BlockSpec((tr, cols), lambda i, d: (d[0] * nb + i, 0))] * n),
        out_shape=[jax.ShapeDtypeStruct((N_DEV * r, cols), out_dtype)] * n, compiler_params=_cparams(),
    )(dev, *[a for a, _ in srcs])


def _gather_peers(x, y, c):
    return [(1 - x, y, c), (x, 1 - y, c), (1 - x, 1 - y, c), (x, y, 1 - c)]


def _block_rows(land):
    return land.shape[0] // N_DEV


def _near_peers(x, y, c):
    return [(1 - x, y, c), (x, 1 - y, c), (x, y, 1 - c)]


def _relay_route(x, y, c):
    origin = (x + c * (1 - 2 * x), y + (1 - c) * (1 - 2 * y), c)
    target = (x + (1 - c) * (1 - 2 * x), y + c * (1 - 2 * y), c)
    return origin, target


def _same_block_copy(blk, send_sem, recv_sem, to):
    return pltpu.make_async_remote_copy(src_ref=blk, dst_ref=blk, send_sem=send_sem, recv_sem=recv_sem, device_id=to,
                                        device_id_type=MESH)


def _gather_start(name, lands, after):
    n = len(lands)

    def body(*refs):
        lz = refs[:n]
        send_sems, recv_sems = refs[n + len(after)], refs[n + len(after) + 1]
        token = refs[-1]
        x, y, c = _position()
        for a in range(n):
            own = _block(lz[a], _block_rows(lands[a]), (x, y, c))
            for k, to in enumerate(_near_peers(x, y, c)):
                _same_block_copy(own, send_sems.at[k], recv_sems.at[k], to).start()
        token[...] = jnp.zeros_like(token)

    out = _split_call(name, body, list(lands), [], after, [(3,), (3,)], True)
    return out[0], out[1], out[2:2 + n], out[-1]


def _gather_step(name, near, far, fresh, after):
    groups = [g for g in (near and near[0], far and far[0], fresh) if g]
    counts = [len(near[0]) if near else 0, len(far[0]) if far else 0, len(fresh) if fresh else 0]
    n = sum(counts)
    sems_in = ([near[1]] if near else []) + ([far[1]] if far else [])
    sems_out = ([(2,), (2,), (1,), (1,)] if near else []) + ([(1,), (1,)] if far else []) + ([(3,), (3,)] if fresh else [])

    def body(*refs):
        lz = list(refs[:n])
        ins = list(refs[n:n + len(sems_in)])
        outs = list(refs[n + len(sems_in) + len(after):n + len(sems_in) + len(after) + len(sems_out)])
        token = refs[-1]
        x, y, c = _position()
        me, sibling = (x, y, c), (x, y, 1 - c)
        near_lz, far_lz, fresh_lz = (lz[sum(counts[:i]):sum(counts[:i + 1])] for i in range(3))
        neighbours = _near_peers(x, y, c)[:2]
        origin, target = _relay_route(x, y, c)
        diagonal = (1 - x, 1 - y, c)
        if near:
            recv0 = ins.pop(0)
            fsend, frecv, rsend, rrecv = (outs.pop(0) for _ in range(4))
            for a, land in enumerate(near[0]):
                for j, chip in enumerate(neighbours):
                    _same_block_copy(_block(near_lz[a], _block_rows(land), chip), fsend.at[j], recv0.at[j], me).wait_recv()
        if far:
            rrecv_in = ins.pop(0)
            f2send, f2recv = outs.pop(0), outs.pop(0)
            for a, land in enumerate(far[0]):
                _same_block_copy(_block(far_lz[a], _block_rows(land), diagonal), f2send.at[0], rrecv_in.at[0], me).wait_recv()
            for a, land in enumerate(far[0]):
                _same_block_copy(_block(far_lz[a], _block_rows(land), diagonal), f2send.at[0], f2recv.at[0], sibling).start()
        if near:
            for a, land in enumerate(near[0]):
                r = _block_rows(land)
                _same_block_copy(_block(near_lz[a], r, origin), rsend.at[0], rrecv.at[0], target).start()
                for j, chip in enumerate(neighbours):
                    _same_block_copy(_block(near_lz[a], r, chip), fsend.at[j], frecv.at[j], sibling).start()
        if fresh:
            send_sems, recv_sems = outs.pop(0), outs.pop(0)
            for a, land in enumerate(fresh):
                own = _block(fresh_lz[a], _block_rows(land), me)
                for k, to in enumerate(_near_peers(x, y, c)):
                    _same_block_copy(own, send_sems.at[k], recv_sems.at[k], to).start()
        token[...] = jnp.zeros_like(token)

    out = list(_split_call(name, body, [l for g in groups for l in g], sems_in, after, sems_out, True))
    res = {"token": out.pop()}
    if near:
        res.update(fsend=out.pop(0), frecv=out.pop(0), rsend=out.pop(0), rrecv=out.pop(0))
    if far:
        res.update(f2send=out.pop(0), f2recv=out.pop(0))
    if fresh:
        res.update(send=out.pop(0), recv=out.pop(0))
    res["near"], res["far"], res["fresh"] = (out[sum(counts[:i]):sum(counts[:i + 1])] for i in range(3))
    return res


def _gather_finish(name, lands, send_sems, recv_sems, fsend, frecv, rsend, f2send, f2recv, after):
    n = len(lands)

    def body(*refs):
        lz = refs[:n]
        send0, recv0, fsend_ref, frecv_ref, rsend_ref, f2send_ref, f2recv_ref = refs[n:n + 7]
        x, y, c = _position()
        me = (x, y, c)
        near = _near_peers(x, y, c)[:2]
        origin, _ = _relay_route(x, y, c)
        for a in range(n):
            r = _block_rows(lands[a])
            sib = _block(lz[a], r, (x, y, 1 - c))
            _same_block_copy(sib, send0.at[2], recv0.at[2], me).wait_recv()
            for j, chip in enumerate(near):
                blk = _block(lz[a], r, (chip[0], chip[1], 1 - c))
                _same_block_copy(blk, fsend_ref.at[j], frecv_ref.at[j], me).wait_recv()
            far = _block(lz[a], r, (1 - x, 1 - y, 1 - c))
            _same_block_copy(far, f2send_ref.at[0], f2recv_ref.at[0], me).wait_recv()
            own = _block(lz[a], r, me)
            for k in range(3):
                _same_block_copy(own, send0.at[k], recv0.at[k], me).wait_send()
            for j, chip in enumerate(near):
                _same_block_copy(_block(lz[a], r, chip), fsend_ref.at[j], frecv_ref.at[j], me).wait_send()
            _same_block_copy(_block(lz[a], r, origin), rsend_ref.at[0], recv0.at[0], me).wait_send()
            _same_block_copy(_block(lz[a], r, (1 - x, 1 - y, c)), f2send_ref.at[0], f2recv_ref.at[0], me).wait_send()

    return _split_call(name, body, list(lands), [send_sems, recv_sems, fsend, frecv, rsend, f2send, f2recv], after, [],
                       False)


def _sibling_start(name, grads, after):
    n = len(grads)
    lands = [_landing((4, g.shape[0] // N_DEV, D), g.dtype) for g in grads]

    def body(*refs):
        ins, lz = refs[:n], refs[n:2 * n]
        send_sem, recv_sem = refs[2 * n + len(after)], refs[2 * n + len(after) + 1]
        token = refs[-1]
        x, y, c = _position()
        for a in range(n):
            r = grads[a].shape[0] // N_DEV
            for q in range(4):
                pltpu.make_async_remote_copy(
                    src_ref=ins[a].at[pl.ds((2 * q + 1 - c) * r, r), :], dst_ref=lz[a].at[q], send_sem=send_sem.at[0],
                    recv_sem=recv_sem.at[0], device_id=(x, y, 1 - c), device_id_type=MESH).start()
        token[...] = jnp.zeros_like(token)

    out = _split_call(name, body, list(grads) + lands, [], after, [(1,), (1,)], True)
    return out[0], out[1], out[2:2 + n], out[2 + n:2 + 2 * n], out[-1]


def _sibling_finish(name, grads, lands, send_sem, recv_sem, after):
    n = len(grads)

    def body(*refs):
        ins, lz = refs[:n], refs[n:2 * n]
        send_ref, recv_ref = refs[2 * n], refs[2 * n + 1]
        x, y, c = _position()
        for a in range(n):
            r = grads[a].shape[0] // N_DEV
            for q in range(4):
                cp = pltpu.make_async_remote_copy(
                    src_ref=ins[a].at[pl.ds((2 * q + 1 - c) * r, r), :], dst_ref=lz[a].at[q], send_sem=send_ref.at[0],
                    recv_sem=recv_ref.at[0], device_id=(x, y, c), device_id_type=MESH)
                cp.wait_send()
                cp.wait_recv()

    out = _split_call(name, body, list(grads) + list(lands), [send_sem, recv_sem], after, [], False)
    return out[:n], out[n:2 * n]


def _chip_start(name, parts, after):
    n = len(parts)
    lands = [_landing((3,) + p.shape[1:], p.dtype) for p in parts]

    def body(*refs):
        ins, lz = refs[:n], refs[n:2 * n]
        send_sems, recv_sems = refs[2 * n + len(after)], refs[2 * n + len(after) + 1]
        token = refs[-1]
        x, y, c = _position()
        for a in range(n):
            for j, chip in enumerate(_gather_peers(x, y, c)[:3]):
                pltpu.make_async_remote_copy(
                    src_ref=ins[a].at[2 * chip[0] + chip[1]], dst_ref=lz[a].at[j], send_sem=send_sems.at[j],
                    recv_sem=recv_sems.at[j], device_id=chip, device_id_type=MESH).start()
        token[...] = jnp.zeros_like(token)

    out = _split_call(name, body, list(parts) + lands, [], after, [(3,), (3,)], True)
    return out[0], out[1], out[2:2 + n], out[2 + n:2 + 2 * n], out[-1]


def _chip_finish(name, parts, lands, send_sems, recv_sems, after):
    n = len(parts)

    def body(*refs):
        ins, lz = refs[:n], refs[n:2 * n]
        send_ref, recv_ref = refs[2 * n], refs[2 * n + 1]
        me = _position()
        for a in range(n):
            for j in range(3):
                cp = pltpu.make_async_remote_copy(
                    src_ref=ins[a].at[j], dst_ref=lz[a].at[j], send_sem=send_ref.at[j], recv_sem=recv_ref.at[j],
                    device_id=me, device_id_type=MESH)
                cp.wait_send()
                cp.wait_recv()

    out = _split_call(name, body, list(parts) + list(lands), [send_sems, recv_sems], after, [], False)
    return out[:n], out[n:2 * n]


def _other_devices(x, y, c):
    return [(x + (k >> 2 & 1) * (1 - 2 * x), y + (k >> 1 & 1) * (1 - 2 * y), c + (k & 1) * (1 - 2 * c))
            for k in range(1, N_DEV)]


def _broadcast_start(name, arrays, after):
    n = len(arrays)
    lands = [_landing((N_DEV,) + a.shape, a.dtype) for a in arrays]

    def body(*refs):
        ins, lz = refs[:n], refs[n:2 * n]
        send_sems, recv_sems = refs[2 * n + len(after)], refs[2 * n + len(after) + 1]
        token = refs[-1]
        x, y, c = _position()
        for a in range(n):
            for k, peer in enumerate(_other_devices(x, y, c)):
                pltpu.make_async_remote_copy(
                    src_ref=ins[a], dst_ref=lz[a].at[4 * x + 2 * y + c], send_sem=send_sems.at[k],
                    recv_sem=recv_sems.at[k], device_id=peer, device_id_type=MESH).start()
        token[...] = jnp.zeros_like(token)

    out = _split_call(name, body, list(arrays) + lands, [], after, [(N_DEV - 1,), (N_DEV - 1,)], True)
    return out[0], out[1], out[2:2 + n], out[2 + n:2 + 2 * n], out[-1]


def _broadcast_finish(name, arrays, lands, send_sems, recv_sems, after):
    n = len(arrays)

    def body(*refs):
        ins, lz = refs[:n], refs[n:2 * n]
        send_ref, recv_ref = refs[2 * n], refs[2 * n + 1]
        x, y, c = _position()
        for a in range(n):
            for k, peer in enumerate(_other_devices(x, y, c)):
                cp = pltpu.make_async_remote_copy(
                    src_ref=ins[a], dst_ref=lz[a].at[4 * peer[0] + 2 * peer[1] + peer[2]], send_sem=send_ref.at[k],
                    recv_sem=recv_ref.at[k], device_id=(x, y, c), device_id_type=MESH)
                cp.wait_send()
                cp.wait_recv()

    out = _split_call(name, body, list(arrays) + list(lands), [send_sems, recv_sems], after, [], False)
    return out[:n], out[n:2 * n]


def _row_tile(r, target):
    return max(t for t in range(16, min(r, target) + 1, 16) if r % t == 0)


CHIP_PARTIAL_BYTES = 12 * 1024 * 1024


def _chip_partial(name, grads, gots, c):
    n = len(grads)
    r = grads[0].shape[0] // N_DEV
    tr = _row_tile(r, CHIP_PARTIAL_BYTES // (n * 3 * D * 2))

    def body(c_ref, *refs):
        for a in range(n):
            refs[2 * n + a][...] = (refs[a][...].astype(F32) + refs[n + a][...].astype(F32)).astype(BF16)

    blk = pl.BlockSpec((None, tr, D), lambda q, i, c_ref: (q, i, 0))
    return pl.pallas_call(
        body, name=name,
        grid_spec=pltpu.PrefetchScalarGridSpec(
            num_scalar_prefetch=1, grid=(4, r // tr),
            in_specs=[pl.BlockSpec((None, None, tr, D), lambda q, i, c_ref: (q, c_ref[0], i, 0))] * n + [blk] * n,
            out_specs=[blk] * n),
        out_shape=[jax.ShapeDtypeStruct((4, r, D), BF16)] * n, compiler_params=_cparams(),
    )(c, *[g.reshape(4, 2, r, D) for g in grads], *gots)


class _WeightGather:
    def __init__(self, groups):
        self.groups = list(groups)
        self.index = {key: i for i, (key, _, _) in enumerate(groups)}
        self.state = [None] * len(groups)
        self.token = ()
        for i in range(min(2, len(groups))):
            self._start(i)

    def _tag(self, i):
        return "%s_%d" % self.groups[i][0][::-1]

    def _start(self, i):
        send, recv, lz, tok = _gather_start("gather_start_" + self._tag(i), self.groups[i][2], self.token)
        self.state[i] = dict(send=send, recv=recv, lands=lz)
        self.token = (tok,)

    def _step(self, name, near, far, fresh, marker):
        exists = lambda i: i is not None and i < len(self.groups)
        near, far, fresh = (i if exists(i) else None for i in (near, far, fresh))
        res = _gather_step(
            name, None if near is None else (self.state[near]["lands"], self.state[near]["recv"]),
            None if far is None else (self.state[far]["lands"], self.state[far]["rrecv"]),
            None if fresh is None else self.groups[fresh][2], tuple(marker) + self.token)
        self.token = (res["token"],)
        if near is not None:
            self.state[near].update(lands=res["near"], fsend=res["fsend"], frecv=res["frecv"], rsend=res["rsend"],
                                    rrecv=res["rrecv"])
        if far is not None:
            self.state[far].update(lands=res["far"], f2send=res["f2send"], f2recv=res["f2recv"])
        if fresh is not None:
            self.state[fresh] = dict(send=res["send"], recv=res["recv"], lands=res["fresh"])

    def fetch(self, layer, group, marker):
        k = self.index[(layer, group)]
        if k == 0:
            self._step("gather_step_first", 0, None, None, marker)
        self._step("gather_step_" + self._tag(k), k + 1, k, k + 2, marker)
        st = self.state[k]
        lz = _gather_finish("gather_finish_" + self._tag(k), st["lands"], st["send"], st["recv"], st["fsend"],
                            st["frecv"], st["rsend"], st["f2send"], st["f2recv"], self.token)
        self.state[k] = None
        return dict(zip(self.groups[k][1], lz))


class _GradReduce:
    def __init__(self, core, chip):
        self.core, self.chip = core, chip
        self.layer = None
        self.token = ()
        self.at_sibling, self.at_chips = [], []
        self.extra, self.smalls = (), {}

    def after(self):
        return self.token

    def add(self, group, names, grads):
        tag = "%s_%d" % (group, self.layer)
        send, recv, grads, lands, tok = _sibling_start("grad_sibling_start_" + tag, grads, self.token)
        self.at_sibling.append((tag, [(self.layer, n) for n in names], send, recv, grads, lands))
        self.token = (tok,)

    def advance(self, marker):
        for tag, keys, send, recv, grads, lands in self.at_sibling:
            grads, lands = _sibling_finish("grad_sibling_finish_" + tag, grads, lands, send, recv, marker)
            parts = _chip_partial("chip_partial_" + tag, grads, lands, self.core)
            send, recv, parts, lands, tok = _chip_start("grad_chip_start_" + tag, parts, ())
            self.at_chips.append([tag, keys, send, recv, parts, lands])
            self.token = (tok,)
        self.at_sibling = []

    def small(self, part, arrays):
        keys = list(arrays)
        send, recv, own, slots, tok = _broadcast_start(
            "small_grads_start_%d_%s" % (self.layer, part), [arrays[k] for k in keys], self.token)
        self.smalls.setdefault(self.layer, []).append((part, keys, send, recv, own, slots))
        self.token = (tok,)

    def small_finish(self, layer, marker):
        mine, theirs = {}, {}
        for part, keys, send, recv, own, slots in self.smalls[layer]:
            own, slots = _broadcast_finish("small_grads_finish_%d_%s" % (layer, part), own, slots, send, recv, marker)
            mine.update(zip(keys, own))
            theirs.update(zip(keys, slots))
        return mine, theirs

    def collect(self, key, marker):
        for entry in self.at_chips:
            tag, keys, send, recv, parts, lands = entry
            if key in keys:
                if send is not None:
                    parts, lands = _chip_finish("grad_chip_finish_" + tag, parts, lands, send, recv, marker)
                    entry[2:] = [None, None, parts, lands]
                i = keys.index(key)
                return parts[i], lands[i]
        raise KeyError(key)


def _adamw_math(w, g, m, v):
    m = ADAM_B1 * m + (1.0 - ADAM_B1) * g
    v = ADAM_B2 * v + (1.0 - ADAM_B2) * jnp.square(g)
    m_hat = m / (1.0 - ADAM_B1 ** ADAM_STEP)
    v_hat = v / (1.0 - ADAM_B2 ** ADAM_STEP)
    delta = -ADAM_LR * (m_hat / (jnp.sqrt(v_hat) + ADAM_EPS) + ADAM_WD * w)
    return delta, m, v


def _adamw_small(wts, mom_m, mom_v, own, gathered, loss_own, loss_gathered, dev):
    names = SMALL
    nw = len(names)
    na = len(SMALL_ARRAYS)

    def body(dev_ref, *refs):
        w_refs, m_refs, v_refs = (dict(zip(names, refs[i * nw:(i + 1) * nw])) for i in range(3))
        own_refs = refs[3 * nw:3 * nw + DEPTH * na]
        g_refs = refs[3 * nw + DEPTH * na:3 * nw + 2 * DEPTH * na]
        loss_own_ref, loss_got_ref = refs[3 * nw + 2 * DEPTH * na:3 * nw + 2 * DEPTH * na + 2]
        outs = refs[3 * nw + 2 * DEPTH * na + 2:]
        g_out, d_out, m_out, v_out = (dict(zip(names, outs[i * nw:(i + 1) * nw])) for i in range(4))
        me = dev_ref[0]

        loss = None
        for d in range(N_DEV):
            for b in range(loss_own.shape[0]):
                term = jnp.where(me == d, loss_own_ref[b], loss_got_ref[d, b])
                loss = term if loss is None else loss + term
        outs[4 * nw][...] = loss

        def update(name, at, g):
            g_out[name][at] = g
            d_out[name][at], m_out[name][at], v_out[name][at] = _adamw_math(
                w_refs[name][at], g, m_refs[name][at], v_refs[name][at])

        for l in range(DEPTH):
            mine = dict(zip(SMALL_ARRAYS, own_refs[l * na:(l + 1) * na]))
            got = dict(zip(SMALL_ARRAYS, g_refs[l * na:(l + 1) * na]))

            def total(key, at):
                acc = None
                for d in range(N_DEV):
                    term = jnp.where(me == d, mine[key][at] if at else mine[key][...], got[key][(d,) + at])
                    acc = term if acc is None else acc + term
                return acc

            row = (slice(l, l + 1),)
            for k, name in enumerate(NORM_NAMES):
                update(name, row, total("norms", (slice(k, k + 1),)))
            for k, name in enumerate(VEC_NAMES):
                update(name, row, total("vecs", (slice(k, k + 1),)))
            update("gmlp_v_gain", (l,), total("gain_bias", (slice(0, NH),)))
            update("b_spatial", (l,), total("gain_bias", (slice(NH, 2 * NH),)))
            update("w_spatial", (l,), total("w_spatial", ()))
            update("w_pool", (l,), total("w_pool", ()))
            update("w_dw", (l,), total("w_dw", (slice(0, CONV_K),)))

    args = [src[n] for src in (wts, mom_m, mom_v) for n in names]
    args += [src[l][k] for src in (own, gathered) for l in range(DEPTH) for k in SMALL_ARRAYS]
    args += [loss_own, loss_gathered]
    outs = pl.pallas_call(
        body, name="adamw_small",
        in_specs=[pl.BlockSpec(memory_space=pltpu.SMEM)] + [pl.BlockSpec(memory_space=pltpu.VMEM)] * len(args),
        out_shape=[jax.ShapeDtypeStruct(wts[n].shape, F32) for _ in range(4) for n in names]
        + [jax.ShapeDtypeStruct((8, LANES), F32)],
        compiler_params=_cparams(),
    )(dev, *args)
    return tuple(dict(zip(names, outs[i * nw:(i + 1) * nw])) for i in range(4)) + (outs[4 * nw],)


def _adamw_layers(name, w, reduced, m, v, chip, tr, transposed=False, after=()):
    nl, r, cdim = w.shape
    tr = _row_tile(r, tr)
    nb = r // tr

    def body(q_ref, w_ref, p0_ref, g0_ref, p1_ref, g1_ref, m_ref, v_ref, *rest):
        g_ref, d_ref, nm_ref, nv_ref = rest[len(after):]

        def total(p_ref, got_ref):
            acc = p_ref[...].astype(F32)
            for j in range(3):
                acc = acc + got_ref[j].astype(F32)
            return acc

        g = jnp.where(pl.program_id(0) == 0, total(p0_ref, g0_ref), total(p1_ref, g1_ref))
        if transposed:
            g = g.T
        g_ref[...] = g
        d_ref[...], nm_ref[...], nv_ref[...] = _adamw_math(w_ref[...], g, m_ref[...], v_ref[...])

    blk = pl.BlockSpec((None, tr, cdim), lambda l, i, q: (l, i, 0))
    first = lambda l, i: i * (1 - l) + (nb - 1) * l
    second = lambda l, i: i * l
    if transposed:
        gshape = (cdim, tr)
        at = lambda lead, i: (lead, 0, i)
    else:
        gshape = (tr, cdim)
        at = lambda lead, i: (lead, i, 0)
    specs = [blk,
             pl.BlockSpec((None,) + gshape, lambda l, i, q: at(q[0], first(l, i))),
             pl.BlockSpec((3,) + gshape, lambda l, i, q: at(0, first(l, i))),
             pl.BlockSpec((None,) + gshape, lambda l, i, q: at(q[0], second(l, i))),
             pl.BlockSpec((3,) + gshape, lambda l, i, q: at(0, second(l, i))), blk, blk] + [ANY] * len(after)
    shape = jax.ShapeDtypeStruct((nl, r, cdim), F32)
    return pl.pallas_call(
        body, name=name,
        grid_spec=pltpu.PrefetchScalarGridSpec(num_scalar_prefetch=1, grid=(nl, nb), in_specs=specs, out_specs=[blk] * 4),
        out_shape=[shape] * 4, compiler_params=_cparams(),
    )(chip, w, *reduced[0], *reduced[1], m, v, *after)


def _to_rows(name, a):
    return jnp.swapaxes(a, 1, 2) if name == "w_in" else a


def _place_own_transposed(name, srcs, dev, out_dtype, tc):
    n = len(srcs)
    kdim, cdim = srcs[0][0].shape[-2:]

    def body(dev_ref, *refs):
        for a in range(n):
            refs[n + a][...] = refs[a][...].T.astype(out_dtype)

    return pl.pallas_call(
        body, name=name,
        grid_spec=pltpu.PrefetchScalarGridSpec(
            num_scalar_prefetch=1, grid=(kdim // tc,),
            in_specs=[pl.BlockSpec((None, tc, cdim), lambda i, d, l=l: (l, i, 0)) for _, l in srcs],
            out_specs=[pl.BlockSpec((cdim, tc), lambda i, d: (d[0], i))] * n),
        out_shape=[jax.ShapeDtypeStruct((N_DEV * cdim, kdim), out_dtype)] * n, compiler_params=_cparams(),
    )(dev, *[a for a, _ in srcs])


def _pack(arrays, rows):
    flat = jnp.concatenate([a.reshape(-1) for a in arrays])
    return jnp.pad(flat, (0, rows * D - flat.shape[0])).reshape(rows, D)


def _rows_for(shapes, mult=8):
    total = 0
    for shp in shapes:
        size = 1
        for dim in shp:
            size *= dim
        total += size
    return -(-total // (mult * D)) * mult


def kernel(x, mem, norm_mix_pre, norm_mix_post, w_in, w_out, gmlp_v_gain, w_spatial, b_spatial, w_pool, s_pool, w_dw, b_dw, conv_ln_g, conv_ln_b, norm_xattn_pre, norm_mem, norm_xattn_post, w_q, w_k, w_v, w_o, norm_ffn_pre, norm_ffn_post, w_up, w_down, loss_target, m_norm_mix_pre, m_norm_mix_post, m_w_in, m_w_out, m_gmlp_v_gain, m_w_spatial, m_b_spatial, m_w_pool, m_s_pool, m_w_dw, m_b_dw, m_conv_ln_g, m_conv_ln_b, m_norm_xattn_pre, m_norm_mem, m_norm_xattn_post, m_w_q, m_w_k, m_w_v, m_w_o, m_norm_ffn_pre, m_norm_ffn_post, m_w_up, m_w_down, v_norm_mix_pre, v_norm_mix_post, v_w_in, v_w_out, v_gmlp_v_gain, v_w_spatial, v_b_spatial, v_w_pool, v_s_pool, v_w_dw, v_b_dw, v_conv_ln_g, v_conv_ln_b, v_norm_xattn_pre, v_norm_mem, v_norm_xattn_post, v_w_q, v_w_k, v_w_v, v_w_o, v_norm_ffn_pre, v_norm_ffn_post, v_w_up, v_w_down):
    args = dict(locals())
    wts = {n: args[n] for n in WEIGHTS}
    mom_m = {n: args["m_" + n] for n in WEIGHTS}
    mom_v = {n: args["v_" + n] for n in WEIGHTS}
    xi, yi, ci = _position()
    me = 4 * xi + 2 * yi + ci

    dev = jnp.reshape(me, (1,)).astype(jnp.int32)
    lands = {}
    for call, names, tr in (("place_in", ("w_in",), 256), ("place_att", ("w_out", "w_q", "w_k", "w_v", "w_o"), 64),
                            ("place_up", ("w_up",), 256), ("place_down", ("w_down",), 256)):
        srcs = [(_to_rows(n, wts[n]), l) for l in range(DEPTH) for n in names]
        placed = (_place_own_transposed if names == ("w_up",) else _place_own)(call, srcs, dev, BF16, tr)
        lands.update(zip([(l, n) for l in range(DEPTH) for n in names], placed))
    (lands[(0, "taps")],) = _place_own("place_taps", [(_pack([w_dw], _rows_for([w_dw.shape])), None)], dev, F32, 8)
    groups = []
    for l in range(DEPTH):
        for group, names in GATHER_GROUPS:
            if (l, group) == (0, "in"):
                names = names + ("taps",)
            groups.append(((l, group), names, [lands[(l, n)] for n in names]))
    gather = _WeightGather(groups)

    def fetch(layer, group, marker):
        w = gather.fetch(layer, group, marker)
        if "taps" in w:
            blocks = w["taps"].reshape(N_DEV, -1)[:, :w_dw.size].reshape((N_DEV,) + w_dw.shape)
            w["taps"] = jnp.moveaxis(blocks, 0, 2).reshape(DEPTH, CONV_K, CW)
        return w

    reduce = _GradReduce(jnp.reshape(ci, (1,)).astype(jnp.int32), jnp.reshape(2 * xi + yi, (1,)).astype(jnp.int32))
    small = {n: wts[n] for n in SMALL if n != "w_dw"}
    _, dx = _local_step(x[0], mem[0], loss_target[0], fetch, small, reduce)
    reduce.advance((dx,))

    grad_w, delta, new_m, new_v = {}, {}, {}, {}
    marker = (dx,) + tuple(reduce.after())
    for n in UPDATE_ORDER:
        reduced = [reduce.collect((l, n), marker) for l in range(DEPTH)]
        outs = _adamw_layers("adamw_" + n, _to_rows(n, wts[n]), reduced, _to_rows(n, mom_m[n]), _to_rows(n, mom_v[n]),
                             reduce.chip, 256, transposed=n == "w_up", after=marker)
        grad_w[n], delta[n], new_m[n], new_v[n] = (_to_rows(n, o) for o in outs)
        marker = (outs[1],)

    own, slots = [None] * DEPTH, [None] * DEPTH
    for l in reversed(range(DEPTH)):
        own[l], slots[l] = reduce.small_finish(l, marker)
        if l == 0:
            loss_own, loss_slots = own[l].pop("loss"), slots[l].pop("loss")
    shard_cols = CW // N_DEV
    for l in range(DEPTH):
        own[l]["w_dw"] = lax.dynamic_slice_in_dim(own[l]["w_dw"], me * shard_cols, shard_cols, axis=1)
        slots[l]["w_dw"] = lax.dynamic_slice_in_dim(slots[l]["w_dw"], me * shard_cols, shard_cols, axis=2)
    *small_out, loss_tile = _adamw_small(wts, mom_m, mom_v, own, slots, loss_own, loss_slots, dev)
    for dst, src in zip((grad_w, delta, new_m, new_v), small_out):
        dst.update(src)

    return (loss_tile[0, 0], dx[None], *[grad_w[n] for n in WEIGHTS], *[delta[n] for n in WEIGHTS],
            *[new_m[n] for n in WEIGHTS], *[new_v[n] for n in WEIGHTS])
```

```python
import functools

import jax
import jax.numpy as jnp
from jax import lax
from jax.experimental import pallas as pl
from jax.experimental.pallas import tpu as pltpu

F32 = jnp.float32
BF16 = jnp.bfloat16

D = 2048
GW = 1024
PW = 512
CW = 512
HD = 128
NH = 8
NG = 4
POOL_WINDOWS = (2, 4, 8, 16)
CONV_K = 31
IN_COLS = 2 * GW + PW + 2 * CW
XH = 4
XHD = D // XH
ATT_SCALE = XHD ** -0.5
RMS_EPS = 1e-6
LN_EPS = 1e-5
DEPTH = 2
N_DEV = 8

ADAM_LR = 0.001
ADAM_B1 = 0.9
ADAM_B2 = 0.999
ADAM_EPS = 1e-08
ADAM_WD = 0.01
ADAM_STEP = 10

LANES = 128
CONV_HALO = 32
POOL_HALO = 16
ROW_TILE = 128
VMEM_LIMIT = 60 * 1024 * 1024

MESH = pl.DeviceIdType.MESH
NT = (((1,), (1,)), ((), ()))
NN = (((1,), (0,)), ((), ()))
TN = (((0,), (0,)), ((), ()))

UPDATE_ORDER = ("w_down", "w_up", "w_o", "w_q", "w_k", "w_v", "w_out", "w_in")
GATHER_GROUPS = (("in", ("w_in",)), ("out", ("w_out",)), ("att", ("w_q", "w_k", "w_v", "w_o")), ("up", ("w_up",)),
                 ("down", ("w_down",)))
SMALL = ("norm_mix_pre", "norm_mix_post", "gmlp_v_gain", "w_spatial", "b_spatial", "w_pool", "s_pool",
         "w_dw", "b_dw", "conv_ln_g", "conv_ln_b", "norm_xattn_pre", "norm_mem", "norm_xattn_post",
         "norm_ffn_pre", "norm_ffn_post")
WEIGHTS = ("norm_mix_pre", "norm_mix_post", "w_in", "w_out", "gmlp_v_gain", "w_spatial", "b_spatial", "w_pool",
           "s_pool", "w_dw", "b_dw", "conv_ln_g", "conv_ln_b", "norm_xattn_pre", "norm_mem", "norm_xattn_post",
           "w_q", "w_k", "w_v", "w_o", "norm_ffn_pre", "norm_ffn_post", "w_up", "w_down")


def _cparams():
    return pltpu.CompilerParams(vmem_limit_bytes=VMEM_LIMIT)


def _dot(a, b, dims):
    return lax.dot_general(a, b, dims, preferred_element_type=F32)


def _rms(x, g):
    y = x * lax.rsqrt(jnp.mean(x * x, axis=-1, keepdims=True) + RMS_EPS)
    return y * g


def _rms_bwd(x, g, dy):
    r = lax.rsqrt(jnp.mean(x * x, axis=-1, keepdims=True) + RMS_EPS)
    xh = x * r
    t = dy * g
    dx = r * (t - xh * jnp.mean(t * xh, axis=-1, keepdims=True))
    return dx, jnp.sum(dy * xh, axis=0, keepdims=True)


def _gelu(x):
    cdf = 0.5 * (1.0 + jnp.tanh(0.7978845608028654 * (x + 0.044715 * (x * x * x))))
    return x * cdf


def _layer_norm(x, g, b=None):
    mu = jnp.mean(x, axis=-1, keepdims=True)
    xc = x - mu
    var = jnp.mean(xc * xc, axis=-1, keepdims=True)
    y = xc * lax.rsqrt(var + LN_EPS) * g
    return y if b is None else y + b


def _sigmoid(x):
    return 1.0 / (1.0 + jnp.exp(-x))


def _gmlp_rows(zu, zv, gv):
    return _gelu(zu), _layer_norm(_gelu(zv), gv)


def _glu(cv, cg):
    return cv * _sigmoid(cg)


def _ln_silu(h, g, b):
    y = _layer_norm(h, g, b)
    return y * _sigmoid(y)


ANY = pl.BlockSpec(memory_space=pl.ANY)


ROWS_TILE = 256
COLS_TILE = 512
DW_TILE = 512
RESIDENT_K = 2048
STREAM_K_TILE = 1024
STREAM_ROWS = 512


def _k_tiles(kdim):
    if kdim <= RESIDENT_K:
        return ROWS_TILE, kdim
    return STREAM_ROWS, max(t for t in range(LANES, STREAM_K_TILE + 1, LANES) if kdim % t == 0)


ROW_PARTS = 2


def _row_parts(tm):
    step = tm // ROW_PARTS
    return [slice(p * step, (p + 1) * step) for p in range(ROW_PARTS)]


def _rowop_mm(name, kind, rows, g, w, dims, out_dtype, u=None, after=()):
    s = rows[0].shape[0]
    n = w.shape[0] if dims == NT else w.shape[1]
    tm, tn = min(ROWS_TILE, s), min(COLS_TILE, n)
    ni, nj = s // tm, n // tn
    bwd = kind == "rms_bwd"
    out_shape = [jax.ShapeDtypeStruct((s, n), out_dtype), jax.ShapeDtypeStruct((s, D), BF16)]
    if bwd:
        out_shape.append(jax.ShapeDtypeStruct((ni, 1, D), F32))

    if n <= RESIDENT_K and u is None:
        def row_body(*refs):
            refs = list(refs)
            row_refs = [refs.pop(0) for _ in rows]
            g_ref, w_ref = refs.pop(0), refs.pop(0)
            del refs[:len(after)]
            dg = None
            for part in _row_parts(tm):
                if bwd:
                    a, dg_part = _rms_bwd(row_refs[0][part, :], g_ref[...], row_refs[1][part, :])
                    dg = dg_part if dg is None else dg + dg_part
                else:
                    a = _rms(row_refs[0][part, :], g_ref[...])
                a = a.astype(BF16)
                refs[1][part, :] = a
                refs[0][part, :] = _dot(a, w_ref[...], dims).astype(out_dtype)
            if bwd:
                refs[2][0] = dg

        blk = pl.BlockSpec((tm, D), lambda i: (i, 0))
        return pl.pallas_call(
            row_body, name=name, grid=(ni,),
            in_specs=[blk] * len(rows) + [pl.BlockSpec((1, D), lambda i: (0, 0)), pl.BlockSpec(w.shape, lambda i: (0, 0))]
            + [ANY] * len(after),
            out_specs=[pl.BlockSpec((tm, n), lambda i: (i, 0)), blk]
            + ([pl.BlockSpec((1, 1, D), lambda i: (i, 0, 0))] if bwd else []),
            out_shape=out_shape, compiler_params=_cparams(),
        )(*rows, g, w, *after)

    def body(*refs):
        refs = list(refs)
        row_refs = [refs.pop(0) for _ in rows]
        g_ref, w_ref = refs.pop(0), refs.pop(0)
        u_ref = refs.pop(0) if u is not None else None
        del refs[:len(after)]
        out_ref, a_ref = refs.pop(0), refs.pop(0)
        dg_ref = refs.pop(0) if bwd else None
        a_all = refs.pop(0)
        t = pl.program_id(0)

        @pl.when(t < ni)
        def _():
            if bwd:
                a, dg = _rms_bwd(row_refs[0][...], g_ref[...], row_refs[1][...])
                dg_ref[0] = dg
            else:
                a = _rms(row_refs[0][...], g_ref[...])
            a_ref[...] = a.astype(BF16)
            a_all[pl.ds(pl.multiple_of(t * tm, tm), tm), :] = a.astype(BF16)

        @pl.when(t >= ni)
        def _():
            acc = _dot(a_all[...], w_ref[...], dims)
            if u_ref is not None:
                acc = acc * (2.0 * jnp.maximum(u_ref[...], 0.0))
            out_ref[...] = acc.astype(out_dtype)

    rows_at = lambda t: jnp.minimum(t, ni - 1)
    cols_at = lambda t: jnp.maximum(t - ni, 0)
    row_spec = pl.BlockSpec((tm, D), lambda t: (rows_at(t), 0))
    w_spec = (pl.BlockSpec((tn, D), lambda t: (cols_at(t), 0)) if dims == NT
              else pl.BlockSpec((D, tn), lambda t: (0, cols_at(t))))
    tile = pl.BlockSpec((s, tn), lambda t: (0, cols_at(t)))
    in_specs = [row_spec] * len(rows) + [pl.BlockSpec((1, D), lambda t: (0, 0)), w_spec]
    in_specs += ([tile] if u is not None else []) + [ANY] * len(after)
    out_specs = [tile, row_spec]
    if bwd:
        out_specs.append(pl.BlockSpec((1, 1, D), lambda t: (rows_at(t), 0, 0)))
    return pl.pallas_call(
        body, name=name, grid=(ni + nj,), in_specs=in_specs, out_specs=out_specs, out_shape=out_shape,
        scratch_shapes=[pltpu.VMEM((s, D), BF16)], compiler_params=_cparams(),
    )(*rows, g, w, *([u] if u is not None else []), *after)


def _mm_rowop(name, kind, pairs, rows, g, relu2=False, after=()):
    s, kdim = pairs[0][0].shape
    tm, tk = _k_tiles(kdim)
    tm = min(tm, s)
    ni, nk = s // tm, kdim // tk
    npair = len(pairs)

    def body(*refs):
        refs = list(refs)
        a_refs = [refs.pop(0) for _ in range(npair)]
        w_refs = [refs.pop(0) for _ in range(npair)]
        row_refs = [refs.pop(0) for _ in rows]
        g_ref = refs.pop(0)
        del refs[:len(after)]
        acc = refs.pop()
        outs = refs
        k = pl.program_id(1)

        def product(part):
            total = None
            for a_ref, w_ref, (_, _, dims) in zip(a_refs, w_refs, pairs):
                a = a_ref[part, :]
                if relu2:
                    a = jnp.square(jnp.maximum(a, 0.0))
                term = _dot(a.astype(BF16), w_ref[...], dims)
                total = term if total is None else total + term
            return total

        def finish(parts, h_of):
            dg = None
            for part in parts:
                h = h_of(part)
                if kind == "rms_res":
                    outs[0][part, :] = row_refs[0][part, :] + _rms(h, g_ref[...])
                    outs[1][part, :] = h
                else:
                    dx, dg_part = _rms_bwd(row_refs[0][part, :], g_ref[...], h)
                    dg = dg_part if dg is None else dg + dg_part
                    if kind == "rms_bwd_res":
                        outs[0][part, :] = row_refs[1][part, :] + dx
            if kind == "rms_bwd_res":
                outs[1][0] = dg
            elif kind == "rms_bwd_gain":
                outs[0][0] = dg

        if nk == 1:
            finish(_row_parts(tm), product)
            return

        @pl.when(k == 0)
        def _():
            acc[...] = jnp.zeros_like(acc)

        acc[...] += product(slice(None))

        @pl.when(k == nk - 1)
        def _():
            finish([slice(None)], lambda part: acc[part, :])

    row_spec = pl.BlockSpec((tm, D), lambda i, k: (i, 0))
    dg_shape = jax.ShapeDtypeStruct((ni, 1, D), F32)
    dg_spec = pl.BlockSpec((1, 1, D), lambda i, k: (i, 0, 0))
    in_specs = [pl.BlockSpec((tm, tk), lambda i, k: (i, k))] * npair
    for _, _, dims in pairs:
        in_specs.append(pl.BlockSpec((tk, D), lambda i, k: (k, 0)) if dims == NN
                        else pl.BlockSpec((D, tk), lambda i, k: (0, k)))
    in_specs += [row_spec] * len(rows) + [pl.BlockSpec((1, D), lambda i, k: (0, 0))] + [ANY] * len(after)
    if kind == "rms_res":
        out_shape = [jax.ShapeDtypeStruct((s, D), F32)] * 2
        out_specs = [row_spec, row_spec]
    elif kind == "rms_bwd_res":
        out_shape = [jax.ShapeDtypeStruct((s, D), F32), dg_shape]
        out_specs = [row_spec, dg_spec]
    else:
        out_shape = [dg_shape]
        out_specs = [dg_spec]
    return pl.pallas_call(
        body, name=name, grid=(ni, nk), in_specs=in_specs, out_specs=out_specs, out_shape=out_shape,
        scratch_shapes=[pltpu.VMEM((tm, D), F32)], compiler_params=_cparams(),
    )(*[p[0] for p in pairs], *[p[1] for p in pairs], *rows, g, *after)


def _mm_tn(name, a, gmat, relu2=False, after=()):
    s, m = a.shape
    tm = min(DW_TILE, m)
    ni = m // tm

    def body(a_ref, g_ref, *rest):
        av = a_ref[...]
        if relu2:
            av = jnp.square(jnp.maximum(av, 0.0))
        rest[len(after)][...] = _dot(av.astype(BF16), g_ref[...], TN).astype(BF16)

    return pl.pallas_call(
        body, name=name, grid=(ni,),
        in_specs=[pl.BlockSpec((s, tm), lambda i: (0, i)), pl.BlockSpec((s, D), lambda i: (0, 0))] + [ANY] * len(after),
        out_specs=pl.BlockSpec((tm, D), lambda i: (i, 0)),
        out_shape=jax.ShapeDtypeStruct((m, D), BF16), compiler_params=_cparams(),
    )(a, gmat, *after)


def _tril():
    r = lax.broadcasted_iota(jnp.int32, (HD, HD), 0)
    c = lax.broadcasted_iota(jnp.int32, (HD, HD), 1)
    return (c <= r).astype(F32)


def _gmlp_fwd(z, gv, ws, bst, tb):
    s = z.shape[0]
    tb = min(tb, s)

    def body(zu_ref, zv_ref, gv_ref, ws_ref, bst_ref, y_ref):
        tril = _tril()
        for h in range(NH):
            cols = slice(h * HD, (h + 1) * HD)
            u, vln = _gmlp_rows(zu_ref[:, cols], zv_ref[:, cols], gv_ref[h:h + 1, :])
            wm = (ws_ref[h] * tril).astype(BF16)
            vb = vln.astype(BF16)
            for c in range(tb // HD):
                rws = slice(c * HD, (c + 1) * HD)
                mixed = _dot(wm, vb[rws], NN) + bst_ref[:, h:h + 1]
                y_ref[rws, cols] = (u[rws] * mixed).astype(BF16)

    return pl.pallas_call(
        body, name="gmlp_fwd", grid=(s // tb,),
        in_specs=[pl.BlockSpec((tb, GW), lambda i: (i, 0)), pl.BlockSpec((tb, GW), lambda i: (i, 1)),
                  pl.BlockSpec((NH, HD), lambda i: (0, 0)), pl.BlockSpec((NH, HD, HD), lambda i: (0, 0, 0)),
                  pl.BlockSpec((HD, NH), lambda i: (0, 0))],
        out_specs=pl.BlockSpec((tb, GW), lambda i: (i, 0)),
        out_shape=jax.ShapeDtypeStruct((s, D), BF16), compiler_params=_cparams(),
    )(z, z, gv, ws, bst)


def _gmlp_bwd(z, dy, gv, ws, bst, tb, after=()):
    s = z.shape[0]
    tb = min(tb, s)
    nb = s // tb

    def body(zu_ref, zv_ref, dy_ref, gv_ref, ws_ref, bst_ref, *rest):
        dz_ref, dgv_ref, dws_ref, db_ref = rest[len(after):]
        tril = _tril()
        for h in range(NH):
            cols = slice(h * HD, (h + 1) * HD)
            (u, vln), vjp = jax.vjp(_gmlp_rows, zu_ref[:, cols], zv_ref[:, cols], gv_ref[h:h + 1, :])
            wmf = ws_ref[h] * tril
            wm = wmf.astype(BF16)
            wmt = wmf.T.astype(BF16)
            vb = vln.astype(BF16)
            dws = jnp.zeros((HD, HD), F32)
            db = jnp.zeros((HD, 1), F32)
            du_parts, dvln_parts = [], []
            for c in range(tb // HD):
                rws = slice(c * HD, (c + 1) * HD)
                mixed = _dot(wm, vb[rws], NN) + bst_ref[:, h:h + 1]
                dyc = dy_ref[rws, cols]
                du_parts.append(dyc * mixed)
                dmixed = dyc * u[rws]
                dmb = dmixed.astype(BF16)
                dws = dws + _dot(dmb, vb[rws], NT)
                db = db + jnp.sum(dmixed, axis=1, keepdims=True)
                dvln_parts.append(_dot(wmt, dmb, NN))
            du = jnp.concatenate(du_parts, axis=0)
            dvln = jnp.concatenate(dvln_parts, axis=0)
            dzu, dzv, dgv = vjp((du, dvln))
            dz_ref[:, cols] = dzu.astype(BF16)
            dz_ref[:, slice(GW + h * HD, GW + (h + 1) * HD)] = dzv.astype(BF16)
            dgv_ref[0, h:h + 1, :] = dgv
            dws_ref[0, h] = dws * tril
            db_ref[0, h] = jnp.broadcast_to(db, (HD, LANES))

    blk = pl.BlockSpec((tb, GW), lambda i: (i, 0))
    return pl.pallas_call(
        body, name="gmlp_bwd", grid=(nb,),
        in_specs=[blk, pl.BlockSpec((tb, GW), lambda i: (i, 1)), blk,
                  pl.BlockSpec((NH, HD), lambda i: (0, 0)), pl.BlockSpec((NH, HD, HD), lambda i: (0, 0, 0)),
                  pl.BlockSpec((HD, NH), lambda i: (0, 0))] + [ANY] * len(after),
        out_specs=[pl.BlockSpec((tb, 2 * GW), lambda i: (i, 0)), pl.BlockSpec((1, NH, HD), lambda i: (i, 0, 0)),
                   pl.BlockSpec((1, NH, HD, HD), lambda i: (i, 0, 0, 0)),
                   pl.BlockSpec((1, NH, HD, LANES), lambda i: (i, 0, 0, 0))],
        out_shape=[jax.ShapeDtypeStruct((s, IN_COLS), BF16),
                   jax.ShapeDtypeStruct((nb, NH, HD), F32), jax.ShapeDtypeStruct((nb, NH, HD, HD), F32),
                   jax.ShapeDtypeStruct((nb, NH, HD, LANES), F32)],
        compiler_params=_cparams(),
    )(z, z, dy, gv, ws, bst, *after)


POOL_TILE = 1024


def _pool_count(t0, window):
    pos = (t0 + lax.broadcasted_iota(jnp.int32, (POOL_TILE, LANES), 0)).astype(F32)
    return jnp.minimum(pos + 1.0, float(window))


def _window_sum(win, levels, back):
    n = win.shape[0]
    for lv in range(levels):
        step = 1 << lv
        win = win + pltpu.roll(win, n - step if back else step, 0)
    return win


def _pool_pooled(ppad_ref, t0, g):
    win = ppad_ref[pl.ds(t0, POOL_TILE + POOL_HALO), :]
    wsum = _window_sum(win, g + 1, False)[POOL_HALO:]
    return wsum / _pool_count(t0, POOL_WINDOWS[g]) - win[POOL_HALO:]


def _pool_fwd(z, wp, sp, y):
    s = z.shape[0]
    nt = s // POOL_TILE

    def body(p_ref, wp_ref, sp_ref, _, y_ref, ppad):
        for g in range(NG):
            cols = slice(g * LANES, (g + 1) * LANES)
            ppad[pl.ds(0, POOL_HALO), :] = jnp.zeros((POOL_HALO, LANES), F32)
            ppad[pl.ds(POOL_HALO, s), :] = p_ref[:, cols]
            wpb = wp_ref[g].astype(BF16)
            scale = sp_ref[:, cols]

            def tile(t, carry):
                t0 = pl.multiple_of(t * POOL_TILE, POOL_TILE)
                pooled = _pool_pooled(ppad, t0, g)
                y_ref[pl.ds(t0, POOL_TILE), cols] = (_dot(pooled.astype(BF16), wpb, NN) * scale).astype(BF16)
                return carry

            lax.fori_loop(0, nt, tile, 0)

    return pl.pallas_call(
        body, name="pool_fwd", grid=(1,),
        in_specs=[pl.BlockSpec((s, PW), lambda i: (0, 2 * GW // PW)),
                  pl.BlockSpec((NG, LANES, LANES), lambda i: (0, 0, 0)), pl.BlockSpec((1, PW), lambda i: (0, 0)), ANY],
        out_specs=pl.BlockSpec((s, PW), lambda i: (0, GW // PW)),
        out_shape=jax.ShapeDtypeStruct((s, D), BF16), input_output_aliases={3: 0},
        scratch_shapes=[pltpu.VMEM((s + POOL_HALO, LANES), F32)], compiler_params=_cparams(),
    )(z, wp, sp, y)


def _pool_bwd(z, dy, wp, sp, dz):
    s = z.shape[0]
    nt = s // POOL_TILE

    def body(p_ref, dy_ref, wp_ref, sp_ref, _, dp_ref, dwp_ref, dsp_ref, ppad, rpad, dpool):
        for g in range(NG):
            cols = slice(g * LANES, (g + 1) * LANES)
            ppad[pl.ds(0, POOL_HALO), :] = jnp.zeros((POOL_HALO, LANES), F32)
            ppad[pl.ds(POOL_HALO, s), :] = p_ref[:, cols]
            rpad[pl.ds(s, POOL_HALO), :] = jnp.zeros((POOL_HALO, LANES), F32)
            wpb = wp_ref[g].astype(BF16)
            scale = sp_ref[:, cols]

            def tile(t, carry):
                dwp, dsp = carry
                t0 = pl.multiple_of(t * POOL_TILE, POOL_TILE)
                pooled = _pool_pooled(ppad, t0, g)
                pb = pooled.astype(BF16)
                dyt = dy_ref[pl.ds(t0, POOL_TILE), cols]
                dsp = dsp + jnp.sum(dyt * _dot(pb, wpb, NN), axis=0, keepdims=True)
                dmm = (dyt * scale).astype(BF16)
                dwp = dwp + _dot(pb, dmm, TN)
                dpooled = _dot(dmm, wpb, NT)
                rpad[pl.ds(t0, POOL_TILE), :] = dpooled / _pool_count(t0, POOL_WINDOWS[g])
                dpool[pl.ds(t0, POOL_TILE), :] = dpooled
                return dwp, dsp

            dwp, dsp = lax.fori_loop(0, nt, tile, (jnp.zeros((LANES, LANES), F32), jnp.zeros((1, LANES), F32)))
            dwp_ref[g] = dwp
            dsp_ref[:, cols] = dsp

            def tile2(t, carry):
                t0 = pl.multiple_of(t * POOL_TILE, POOL_TILE)
                win = rpad[pl.ds(t0, POOL_TILE + POOL_HALO), :]
                back = _window_sum(win, g + 1, True)[:POOL_TILE]
                rows = pl.ds(t0, POOL_TILE)
                dp_ref[rows, cols] = (back - dpool[rows, :]).astype(BF16)
                return carry

            lax.fori_loop(0, nt, tile2, 0)

    return pl.pallas_call(
        body, name="pool_bwd", grid=(1,),
        in_specs=[pl.BlockSpec((s, PW), lambda i: (0, 2 * GW // PW)), pl.BlockSpec((s, PW), lambda i: (0, GW // PW)),
                  pl.BlockSpec((NG, LANES, LANES), lambda i: (0, 0, 0)), pl.BlockSpec((1, PW), lambda i: (0, 0)), ANY],
        out_specs=[pl.BlockSpec((s, PW), lambda i: (0, 2 * GW // PW)),
                   pl.BlockSpec((NG, LANES, LANES), lambda i: (0, 0, 0)), pl.BlockSpec((1, PW), lambda i: (0, 0))],
        out_shape=[jax.ShapeDtypeStruct((s, IN_COLS), BF16), jax.ShapeDtypeStruct((NG, LANES, LANES), F32),
                   jax.ShapeDtypeStruct((1, PW), F32)],
        input_output_aliases={4: 0},
        scratch_shapes=[pltpu.VMEM((s + POOL_HALO, LANES), F32), pltpu.VMEM((s + POOL_HALO, LANES), F32),
                        pltpu.VMEM((s, LANES), F32)],
        compiler_params=_cparams(),
    )(z, dy, wp, sp, dz)


CONV_LEAD = CONV_HALO - (CONV_K - 1)


SUBLANES = 8


def _sublane_shifts(win):
    n = win.shape[0]
    return [win] + [pltpu.roll(win, n - b, 0) for b in range(1, SUBLANES)]


def _shifted(shifts, offset):
    a, b = divmod(offset, SUBLANES)
    return shifts[b][a * SUBLANES:a * SUBLANES + ROW_TILE]


def _conv_taps(shifts, wdw_ref, lead, reverse):
    acc = jnp.zeros((ROW_TILE, CW), F32)
    for j in range(CONV_K):
        tap = (CONV_K - 1 - j) if reverse else j
        acc = acc + wdw_ref[tap:tap + 1, :] * _shifted(shifts, lead + j)
    return acc


def _conv_fill_glu(cv_ref, cg_ref, xpad, s):
    xpad[pl.ds(0, CONV_HALO), :] = jnp.zeros((CONV_HALO, CW), F32)

    def fill(t, carry):
        t0 = pl.multiple_of(t * ROW_TILE, ROW_TILE)
        rows = pl.ds(t0, ROW_TILE)
        xpad[pl.ds(t0 + CONV_HALO, ROW_TILE), :] = _glu(cv_ref[rows, :], cg_ref[rows, :])
        return carry

    lax.fori_loop(0, s // ROW_TILE, fill, 0)


def _conv_fwd(z, wdw, bdw, lng, lnb, y):
    s = z.shape[0]

    def body(cv_ref, cg_ref, wdw_ref, bdw_ref, lng_ref, lnb_ref, _, y_ref, xpad):
        _conv_fill_glu(cv_ref, cg_ref, xpad, s)

        def tile(t, carry):
            t0 = pl.multiple_of(t * ROW_TILE, ROW_TILE)
            shifts = _sublane_shifts(xpad[pl.ds(t0, ROW_TILE + CONV_HALO), :])
            hc = _conv_taps(shifts, wdw_ref, CONV_LEAD, False) + bdw_ref[...]
            y_ref[pl.ds(t0, ROW_TILE), :] = _ln_silu(hc, lng_ref[...], lnb_ref[...]).astype(BF16)
            return carry

        lax.fori_loop(0, s // ROW_TILE, tile, 0)

    vec = pl.BlockSpec((1, CW), lambda i: (0, 0))
    return pl.pallas_call(
        body, name="conv_fwd", grid=(1,),
        in_specs=[pl.BlockSpec((s, CW), lambda i: (0, (2 * GW + PW) // CW)),
                  pl.BlockSpec((s, CW), lambda i: (0, (2 * GW + PW) // CW + 1)),
                  pl.BlockSpec((CONV_K + 1, CW), lambda i: (0, 0)), vec, vec, vec, ANY],
        out_specs=pl.BlockSpec((s, CW), lambda i: (0, (GW + PW) // CW)),
        out_shape=jax.ShapeDtypeStruct((s, D), BF16), input_output_aliases={6: 0},
        scratch_shapes=[pltpu.VMEM((s + CONV_HALO, CW), F32)], compiler_params=_cparams(),
    )(z, z, wdw, bdw, lng, lnb, y)


def _conv_bwd(z, dy, wdw, bdw, lng, lnb, dz):
    s = z.shape[0]

    def body(cv_ref, cg_ref, dy_ref, wdw_ref, bdw_ref, lng_ref, lnb_ref, _,
             dz_ref, dwdw_ref, dbdw_ref, dlng_ref, dlnb_ref, xpad, dpad, dcg_keep):
        @pl.when(pl.program_id(0) == 0)
        def _():
            compute(cv_ref, cg_ref, dy_ref, wdw_ref, bdw_ref, lng_ref, lnb_ref,
                    dz_ref, dcg_keep, dwdw_ref, dbdw_ref, dlng_ref, dlnb_ref, xpad, dpad)

        @pl.when(pl.program_id(0) == 1)
        def _():
            dz_ref[...] = dcg_keep[...]

    def compute(cv_ref, cg_ref, dy_ref, wdw_ref, bdw_ref, lng_ref, lnb_ref,
                dcv_ref, dcg_ref, dwdw_ref, dbdw_ref, dlng_ref, dlnb_ref, xpad, dpad):
        _conv_fill_glu(cv_ref, cg_ref, xpad, s)
        dpad[pl.ds(s, CONV_HALO), :] = jnp.zeros((CONV_HALO, CW), F32)
        dwdw_ref[...] = jnp.zeros((CONV_K + 1, CW), F32)

        def tile(t, carry):
            db, dg, dbeta = carry
            t0 = pl.multiple_of(t * ROW_TILE, ROW_TILE)
            shifts = _sublane_shifts(xpad[pl.ds(t0, ROW_TILE + CONV_HALO), :])
            hc = _conv_taps(shifts, wdw_ref, CONV_LEAD, False) + bdw_ref[...]
            _, vjp = jax.vjp(_ln_silu, hc, lng_ref[...], lnb_ref[...])
            dhc, dg_t, dbeta_t = vjp(dy_ref[pl.ds(t0, ROW_TILE), :])
            dpad[pl.ds(t0, ROW_TILE), :] = dhc
            for j in range(CONV_K):
                dwdw_ref[j:j + 1, :] += jnp.sum(dhc * _shifted(shifts, CONV_LEAD + j), axis=0, keepdims=True)
            return db + jnp.sum(dhc, axis=0, keepdims=True), dg + dg_t, dbeta + dbeta_t

        zero = jnp.zeros((1, CW), F32)
        db, dg, dbeta = lax.fori_loop(0, s // ROW_TILE, tile, (zero, zero, zero))
        dbdw_ref[...] = db
        dlng_ref[...] = dg
        dlnb_ref[...] = dbeta

        def tile2(t, carry):
            t0 = pl.multiple_of(t * ROW_TILE, ROW_TILE)
            rows = pl.ds(t0, ROW_TILE)
            dglu = _conv_taps(_sublane_shifts(dpad[pl.ds(t0, ROW_TILE + CONV_HALO), :]), wdw_ref, 0, True)
            _, vjp = jax.vjp(_glu, cv_ref[rows, :], cg_ref[rows, :])
            dcv, dcg = vjp(dglu)
            dcv_ref[rows, :] = dcv.astype(BF16)
            dcg_ref[rows, :] = dcg.astype(BF16)
            return carry

        lax.fori_loop(0, s // ROW_TILE, tile2, 0)

    vec = pl.BlockSpec((1, CW), lambda i: (0, 0))
    wspec = pl.BlockSpec((CONV_K + 1, CW), lambda i: (0, 0))
    vshape = jax.ShapeDtypeStruct((1, CW), F32)
    return pl.pallas_call(
        body, name="conv_bwd", grid=(2,),
        in_specs=[pl.BlockSpec((s, CW), lambda i: (0, (2 * GW + PW) // CW)),
                  pl.BlockSpec((s, CW), lambda i: (0, (2 * GW + PW) // CW + 1)),
                  pl.BlockSpec((s, CW), lambda i: (0, (GW + PW) // CW)), wspec, vec, vec, vec, ANY],
        out_specs=[pl.BlockSpec((s, CW), lambda i: (0, (2 * GW + PW) // CW + i)), wspec, vec, vec, vec],
        out_shape=[jax.ShapeDtypeStruct((s, IN_COLS), BF16), jax.ShapeDtypeStruct((CONV_K + 1, CW), F32),
                   vshape, vshape, vshape],
        input_output_aliases={7: 0},
        scratch_shapes=[pltpu.VMEM((s + CONV_HALO, CW), F32), pltpu.VMEM((s + CONV_HALO, CW), F32),
                        pltpu.VMEM((s, CW), BF16)],
        compiler_params=_cparams(),
    )(z, z, dy, wdw, bdw, lng, lnb, dz)


def _softmax_rows(sc):
    e = jnp.exp(sc - jnp.max(sc, axis=-1, keepdims=True))
    return e / jnp.sum(e, axis=-1, keepdims=True)


def _attn_fwd(q, k, v, tq):
    s, m = q.shape[0], k.shape[0]
    tq = min(tq, s)

    def body(q_ref, k_ref, v_ref, o_ref):
        for h in range(XH):
            cols = slice(h * XHD, (h + 1) * XHD)
            p = _softmax_rows(_dot(q_ref[:, cols], k_ref[:, cols], NT) * ATT_SCALE)
            o_ref[:, cols] = _dot(p.astype(BF16), v_ref[:, cols], NN).astype(BF16)

    kv = pl.BlockSpec((m, D), lambda i: (0, 0))
    return pl.pallas_call(
        body, name="attn_fwd", grid=(s // tq,),
        in_specs=[pl.BlockSpec((tq, D), lambda i: (i, 0)), kv, kv],
        out_specs=pl.BlockSpec((tq, D), lambda i: (i, 0)),
        out_shape=jax.ShapeDtypeStruct((s, D), BF16), compiler_params=_cparams(),
    )(q, k, v)


def _attn_bwd(q, k, v, do, tq, after=()):
    s, m = q.shape[0], k.shape[0]
    tq = min(tq, s)

    def body(q_ref, k_ref, v_ref, do_ref, *rest):
        dq_ref, dk_ref, dv_ref = rest[len(after):]

        @pl.when(pl.program_id(0) == 0)
        def _():
            dk_ref[...] = jnp.zeros_like(dk_ref)
            dv_ref[...] = jnp.zeros_like(dv_ref)

        for h in range(XH):
            cols = slice(h * XHD, (h + 1) * XHD)
            qh, kh, vh, doh = q_ref[:, cols], k_ref[:, cols], v_ref[:, cols], do_ref[:, cols]
            p = _softmax_rows(_dot(qh, kh, NT) * ATT_SCALE)
            dp = _dot(doh, vh, NT)
            dv_ref[:, cols] += _dot(p.astype(BF16), doh, TN)
            ds = (p * (dp - jnp.sum(p * dp, axis=-1, keepdims=True)) * ATT_SCALE).astype(BF16)
            dq_ref[:, cols] = _dot(ds, kh, NN).astype(BF16)
            dk_ref[:, cols] += _dot(ds, qh, TN)

    kv = pl.BlockSpec((m, D), lambda i: (0, 0))
    qs = pl.BlockSpec((tq, D), lambda i: (i, 0))
    return pl.pallas_call(
        body, name="attn_bwd", grid=(s // tq,),
        in_specs=[qs, kv, kv, qs] + [ANY] * len(after), out_specs=[qs, kv, kv],
        out_shape=[jax.ShapeDtypeStruct((s, D), BF16), jax.ShapeDtypeStruct((m, D), F32),
                   jax.ShapeDtypeStruct((m, D), F32)],
        compiler_params=_cparams(),
    )(q, k, v, do, *after)


def _loss_head(y, target, tm):
    s = y.shape[0]
    tm = min(tm, s)

    def body(y_ref, t_ref, dy_ref, part_ref):
        err = y_ref[...] - t_ref[...]
        dy_ref[...] = err * (1.0 / D)
        part_ref[...] = jnp.full((1, 8, LANES), 0.5 * jnp.sum(err * err) * (1.0 / D), F32)

    blk = pl.BlockSpec((tm, D), lambda i: (i, 0))
    return pl.pallas_call(
        body, name="loss_head", grid=(s // tm,), in_specs=[blk, blk],
        out_specs=[blk, pl.BlockSpec((1, 8, LANES), lambda i: (i, 0, 0))],
        out_shape=[jax.ShapeDtypeStruct((s, D), F32), jax.ShapeDtypeStruct((s // tm, 8, LANES), F32)],
        compiler_params=_cparams(),
    )(y, target)


def _layer_fwd(x0, mem, w, p, fetch):
    z, hn0 = _rowop_mm("mix_in", "rms", (x0,), p["norm_mix_pre"], w["w_in"], NT, F32)
    y = _gmlp_fwd(z, p["gmlp_v_gain"], p["w_spatial"], p["b_spatial_t"], 1024)
    y = _pool_fwd(z, p["w_pool"], p["s_pool"], y)
    y = _conv_fwd(z, p["w_dw"], p["b_dw"], p["conv_ln_g"], p["conv_ln_b"], y)
    w.update(fetch("out", (y,)))
    x1, h0 = _mm_rowop("mix_out", "rms_res", [(y, w["w_out"], NN)], (x0,), p["norm_mix_post"])
    w.update(fetch("att", (x1,)))
    q, hn1 = _rowop_mm("att_q", "rms", (x1,), p["norm_xattn_pre"], w["w_q"], NN, BF16)
    k, mn = _rowop_mm("att_k", "rms", (mem,), p["norm_mem"], w["w_k"], NN, BF16, after=(x1,))
    v, _ = _rowop_mm("att_v", "rms", (mem,), p["norm_mem"], w["w_v"], NN, BF16, after=(x1,))
    o = _attn_fwd(q, k, v, 1024)
    x2, h1 = _mm_rowop("att_o", "rms_res", [(o, w["w_o"], NN)], (x1,), p["norm_xattn_post"])
    w.update(fetch("up", (x2,)))
    u, hn2 = _rowop_mm("ffn_up", "rms", (x2,), p["norm_ffn_pre"], w["w_up"], NT, F32)
    w.update(fetch("down", (u,)))
    x3, h2 = _mm_rowop("ffn_down", "rms_res", [(u, w["w_down"], NN)], (x2,), p["norm_ffn_post"], relu2=True)
    saved = dict(x0=x0, z=z, hn0=hn0, y=y, h0=h0, x1=x1, q=q, hn1=hn1, k=k, v=v, mn=mn, o=o, h1=h1, x2=x2, u=u,
                 hn2=hn2, h2=h2)
    return x3, saved


def _layer_bwd(dx3, mem, w, p, sv, red):
    gs = {}
    du, dh2, dg = _rowop_mm("ffn_down_bwd", "rms_bwd", (sv["h2"], dx3), p["norm_ffn_post"], w["w_down"], NT, BF16,
                            u=sv["u"], after=red.after())
    gs["norm_ffn_post"] = jnp.sum(dg, axis=0)
    g_down = _mm_tn("ffn_down_dw", sv["u"], dh2, relu2=True)
    red.advance((g_down,))
    dx2, dg = _mm_rowop("ffn_up_bwd", "rms_bwd_res", [(du, w["w_up"], NN)], (sv["x2"], dx3), p["norm_ffn_pre"],
                        after=red.after())
    gs["norm_ffn_pre"] = jnp.sum(dg, axis=0)
    g_up = _mm_tn("ffn_up_dw", du, sv["hn2"])
    red.add("ffn", ("w_down", "w_up"), [g_down, g_up])
    do, dh1, dg = _rowop_mm("att_o_bwd", "rms_bwd", (sv["h1"], dx2), p["norm_xattn_post"], w["w_o"], NT, BF16,
                            after=red.after())
    gs["norm_xattn_post"] = jnp.sum(dg, axis=0)
    g_o = _mm_tn("att_o_dw", sv["o"], dh1)
    red.advance((g_o,))
    dq, dk, dv = _attn_bwd(sv["q"], sv["k"], sv["v"], do, 1024, after=red.after())
    dk, dv = dk.astype(BF16), dv.astype(BF16)
    dx1, dg = _mm_rowop("att_q_bwd", "rms_bwd_res", [(dq, w["w_q"], NT)], (sv["x1"], dx2), p["norm_xattn_pre"],
                        after=red.after())
    gs["norm_xattn_pre"] = jnp.sum(dg, axis=0)
    g_q = _mm_tn("att_q_dw", sv["hn1"], dq)
    g_k = _mm_tn("att_k_dw", sv["mn"], dk)
    g_v = _mm_tn("att_v_dw", sv["mn"], dv)
    (dg,) = _mm_rowop("att_kv_bwd", "rms_bwd_gain", [(dk, w["w_k"], NT), (dv, w["w_v"], NT)], (mem,), p["norm_mem"])
    gs["norm_mem"] = jnp.sum(dg, axis=0)
    red.add("att", ("w_o", "w_q", "w_k", "w_v"), [g_o, g_q, g_k, g_v])
    dy, dh0, dg = _rowop_mm("mix_out_bwd", "rms_bwd", (sv["h0"], dx1), p["norm_mix_post"], w["w_out"], NT, F32,
                            after=red.after())
    gs["norm_mix_post"] = jnp.sum(dg, axis=0)
    g_out = _mm_tn("mix_out_dw", sv["y"], dh0)
    red.advance((g_out,))
    red.add("out", ("w_out",), [g_out])
    z = sv["z"]
    dz, dgv, dws, dbs = _gmlp_bwd(z, dy, p["gmlp_v_gain"], p["w_spatial"], p["b_spatial_t"], 512, after=red.after())
    gs["gmlp_v_gain"] = jnp.sum(dgv, axis=0)
    gs["w_spatial"] = jnp.sum(dws, axis=0)
    gs["b_spatial"] = jnp.sum(dbs[..., 0], axis=0)
    dz, gs["w_pool"], gs["s_pool"] = _pool_bwd(z, dy, p["w_pool"], p["s_pool"], dz)
    dz, dwdw, gs["b_dw"], gs["conv_ln_g"], gs["conv_ln_b"] = _conv_bwd(
        z, dy, p["w_dw"], p["b_dw"], p["conv_ln_g"], p["conv_ln_b"], dz)
    red.advance((dz,))
    g_in = _mm_tn("mix_in_dw", dz, sv["hn0"], after=red.after())
    red.add("in", ("w_in",), [g_in])
    if red.layer == 0:
        red.advance(())
    red.small("mixer", _small_grad_arrays(gs, dwdw, norms=False))
    dx0, dg = _mm_rowop("mix_in_bwd", "rms_bwd_res", [(dz, w["w_in"], NN)], (sv["x0"], dx1), p["norm_mix_pre"],
                        after=red.after())
    gs["norm_mix_pre"] = jnp.sum(dg, axis=0)
    late = {"norms": jnp.concatenate([gs[n] for n in NORM_NAMES], axis=0)}
    if red.layer == 0:
        late["loss"] = red.extra[0]
    red.small("norms", late)
    return dx0


NORM_NAMES = ("norm_mix_pre", "norm_mix_post", "norm_xattn_pre", "norm_mem", "norm_xattn_post", "norm_ffn_pre",
              "norm_ffn_post")
VEC_NAMES = ("s_pool", "b_dw", "conv_ln_g", "conv_ln_b")
SMALL_ARRAYS = ("norms", "gain_bias", "w_spatial", "w_pool", "vecs", "w_dw")


def _small_grad_arrays(gs, dwdw, norms=True):
    out = {"norms": jnp.concatenate([gs[n] for n in NORM_NAMES], axis=0)} if norms else {}
    out.update({"gain_bias": jnp.concatenate([gs["gmlp_v_gain"], gs["b_spatial"]], axis=0),
                "w_spatial": gs["w_spatial"], "w_pool": gs["w_pool"],
                "vecs": jnp.concatenate([gs[n] for n in VEC_NAMES], axis=0), "w_dw": dwdw})
    return out


def _layer_params(small, l):
    p = {n: small[n][l].reshape(1, -1) for n in ("norm_mix_pre", "norm_mix_post", "s_pool", "b_dw", "conv_ln_g",
                                                   "conv_ln_b", "norm_xattn_pre", "norm_mem", "norm_xattn_post",
                                                   "norm_ffn_pre", "norm_ffn_post")}
    p["gmlp_v_gain"] = small["gmlp_v_gain"][l]
    p["w_spatial"] = small["w_spatial"][l]
    p["b_spatial_t"] = small["b_spatial"][l].T
    p["w_pool"] = small["w_pool"][l]
    p["w_dw"] = jnp.pad(small["w_dw"][l], ((0, 1), (0, 0)))
    return p


def _local_step(x, mem, target, fetch, small, red):
    small = dict(small)
    saved, weights, params = [], [], []
    h = x
    marker = ()
    for l in range(DEPTH):
        w = fetch(l, "in", marker)
        if "taps" in w:
            small["w_dw"] = w.pop("taps")
        p = _layer_params(small, l)
        h, sv = _layer_fwd(h, mem, w, p, functools.partial(fetch, l))
        marker = (h,)
        saved.append(sv)
        weights.append(w)
        params.append(p)
    dh, loss = _loss_head(h, target, 1024)
    red.extra = (loss,)
    for l in reversed(range(DEPTH)):
        red.layer = l
        dh = _layer_bwd(dh, mem, weights[l], params[l], saved[l], red)
    return loss, dh


HBM = pl.BlockSpec(memory_space=pltpu.HBM)


def _position():
    return lax.axis_index("x"), lax.axis_index("y"), lax.axis_index("c")


SEM = pl.BlockSpec(memory_space=pltpu.SEMAPHORE)
EFFECT = pltpu.SideEffectType.DATAFLOW_SIDE_EFFECTING
TOKEN = jax.ShapeDtypeStruct((8, LANES), F32)
TOKEN_SPEC = pl.BlockSpec(memory_space=pltpu.VMEM)


def _landing(shape, dtype):
    return pltpu.with_memory_space_constraint(lax.empty(shape, dtype), pltpu.HBM)


def _hbm_shapes(arrays):
    return [pltpu.HBM(a.shape, a.dtype) for a in arrays]


def _block(ref, r, dev):
    return ref.at[pl.ds((4 * dev[0] + 2 * dev[1] + dev[2]) * r, r), :]


def _split_call(name, body, thru, sems_in, after, sems_out, token):
    n = len(thru)
    out_shape = [pltpu.SemaphoreType.DMA(s) for s in sems_out] + _hbm_shapes(thru) + ([TOKEN] if token else [])
    out_specs = [SEM] * len(sems_out) + [HBM] * n + ([TOKEN_SPEC] if token else [])
    return pl.pallas_call(
        body, name=name, in_specs=[HBM] * n + [SEM] * len(sems_in) + [ANY] * len(after),
        out_specs=out_specs, out_shape=out_shape,
        input_output_aliases={i: len(sems_out) + i for i in range(n)},
        compiler_params=pltpu.CompilerParams(has_side_effects=EFFECT),
    )(*thru, *sems_in, *after)


def _place_own(name, srcs, dev, out_dtype, tr):
    n = len(srcs)
    r, cols = srcs[0][0].shape[-2:]
    tr = r if r < 16 else _row_tile(r, tr)
    nb = r // tr

    def body(dev_ref, *refs):
        for a in range(n):
            refs[n + a][...] = refs[a][...].astype(out_dtype)

    in_specs = [pl.BlockSpec((tr, cols), lambda i, d: (i, 0)) if l is None
                else pl.BlockSpec((None, tr, cols), lambda i, d, l=l: (l, i, 0)) for _, l in srcs]
    return pl.pallas_call(
        body, name=name,
        grid_spec=pltpu.PrefetchScalarGridSpec(
            num_scalar_prefetch=1, grid=(nb,), in_specs=in_specs,
            out_specs=[pl.BlockSpec((tr, cols), lambda i, d: (d[0] * nb + i, 0))] * n),
        out_shape=[jax.ShapeDtypeStruct((N_DEV * r, cols), out_dtype)] * n, compiler_params=_cparams(),
    )(dev, *[a for a, _ in srcs])


def _gather_peers(x, y, c):
    return [(1 - x, y, c), (x, 1 - y, c), (1 - x, 1 - y, c), (x, y, 1 - c)]


def _block_rows(land):
    return land.shape[0] // N_DEV


def _near_peers(x, y, c):
    return [(1 - x, y, c), (x, 1 - y, c), (x, y, 1 - c)]


def _relay_route(x, y, c):
    origin = (x + c * (1 - 2 * x), y + (1 - c) * (1 - 2 * y), c)
    target = (x + (1 - c) * (1 - 2 * x), y + c * (1 - 2 * y), c)
    return origin, target


def _same_block_copy(blk, send_sem, recv_sem, to):
    return pltpu.make_async_remote_copy(src_ref=blk, dst_ref=blk, send_sem=send_sem, recv_sem=recv_sem, device_id=to,
                                        device_id_type=MESH)


def _gather_start(name, lands, after):
    n = len(lands)

    def body(*refs):
        lz = refs[:n]
        send_sems, recv_sems = refs[n + len(after)], refs[n + len(after) + 1]
        token = refs[-1]
        x, y, c = _position()
        for a in range(n):
            own = _block(lz[a], _block_rows(lands[a]), (x, y, c))
            for k, to in enumerate(_near_peers(x, y, c)):
                _same_block_copy(own, send_sems.at[k], recv_sems.at[k], to).start()
        token[...] = jnp.zeros_like(token)

    out = _split_call(name, body, list(lands), [], after, [(3,), (3,)], True)
    return out[0], out[1], out[2:2 + n], out[-1]


def _gather_step(name, near, far, fresh, after):
    groups = [g for g in (near and near[0], far and far[0], fresh) if g]
    counts = [len(near[0]) if near else 0, len(far[0]) if far else 0, len(fresh) if fresh else 0]
    n = sum(counts)
    sems_in = ([near[1]] if near else []) + ([far[1]] if far else [])
    sems_out = ([(2,), (2,), (1,), (1,)] if near else []) + ([(1,), (1,)] if far else []) + ([(3,), (3,)] if fresh else [])

    def body(*refs):
        lz = list(refs[:n])
        ins = list(refs[n:n + len(sems_in)])
        outs = list(refs[n + len(sems_in) + len(after):n + len(sems_in) + len(after) + len(sems_out)])
        token = refs[-1]
        x, y, c = _position()
        me, sibling = (x, y, c), (x, y, 1 - c)
        near_lz, far_lz, fresh_lz = (lz[sum(counts[:i]):sum(counts[:i + 1])] for i in range(3))
        neighbours = _near_peers(x, y, c)[:2]
        origin, target = _relay_route(x, y, c)
        diagonal = (1 - x, 1 - y, c)
        if near:
            recv0 = ins.pop(0)
            fsend, frecv, rsend, rrecv = (outs.pop(0) for _ in range(4))
            for a, land in enumerate(near[0]):
                for j, chip in enumerate(neighbours):
                    _same_block_copy(_block(near_lz[a], _block_rows(land), chip), fsend.at[j], recv0.at[j], me).wait_recv()
        if far:
            rrecv_in = ins.pop(0)
            f2send, f2recv = outs.pop(0), outs.pop(0)
            for a, land in enumerate(far[0]):
                _same_block_copy(_block(far_lz[a], _block_rows(land), diagonal), f2send.at[0], rrecv_in.at[0], me).wait_recv()
            for a, land in enumerate(far[0]):
                _same_block_copy(_block(far_lz[a], _block_rows(land), diagonal), f2send.at[0], f2recv.at[0], sibling).start()
        if near:
            for a, land in enumerate(near[0]):
                r = _block_rows(land)
                _same_block_copy(_block(near_lz[a], r, origin), rsend.at[0], rrecv.at[0], target).start()
                for j, chip in enumerate(neighbours):
                    _same_block_copy(_block(near_lz[a], r, chip), fsend.at[j], frecv.at[j], sibling).start()
        if fresh:
            send_sems, recv_sems = outs.pop(0), outs.pop(0)
            for a, land in enumerate(fresh):
                own = _block(fresh_lz[a], _block_rows(land), me)
                for k, to in enumerate(_near_peers(x, y, c)):
                    _same_block_copy(own, send_sems.at[k], recv_sems.at[k], to).start()
        token[...] = jnp.zeros_like(token)

    out = list(_split_call(name, body, [l for g in groups for l in g], sems_in, after, sems_out, True))
    res = {"token": out.pop()}
    if near:
        res.update(fsend=out.pop(0), frecv=out.pop(0), rsend=out.pop(0), rrecv=out.pop(0))
    if far:
        res.update(f2send=out.pop(0), f2recv=out.pop(0))
    if fresh:
        res.update(send=out.pop(0), recv=out.pop(0))
    res["near"], res["far"], res["fresh"] = (out[sum(counts[:i]):sum(counts[:i + 1])] for i in range(3))
    return res


def _gather_finish(name, lands, send_sems, recv_sems, fsend, frecv, rsend, f2send, f2recv, after):
    n = len(lands)

    def body(*refs):
        lz = refs[:n]
        send0, recv0, fsend_ref, frecv_ref, rsend_ref, f2send_ref, f2recv_ref = refs[n:n + 7]
        x, y, c = _position()
        me = (x, y, c)
        near = _near_peers(x, y, c)[:2]
        origin, _ = _relay_route(x, y, c)
        for a in range(n):
            r = _block_rows(lands[a])
            sib = _block(lz[a], r, (x, y, 1 - c))
            _same_block_copy(sib, send0.at[2], recv0.at[2], me).wait_recv()
            for j, chip in enumerate(near):
                blk = _block(lz[a], r, (chip[0], chip[1], 1 - c))
                _same_block_copy(blk, fsend_ref.at[j], frecv_ref.at[j], me).wait_recv()
            far = _block(lz[a], r, (1 - x, 1 - y, 1 - c))
            _same_block_copy(far, f2send_ref.at[0], f2recv_ref.at[0], me).wait_recv()
            own = _block(lz[a], r, me)
            for k in range(3):
                _same_block_copy(own, send0.at[k], recv0.at[k], me).wait_send()
            for j, chip in enumerate(near):
                _same_block_copy(_block(lz[a], r, chip), fsend_ref.at[j], frecv_ref.at[j], me).wait_send()
            _same_block_copy(_block(lz[a], r, origin), rsend_ref.at[0], recv0.at[0], me).wait_send()
            _same_block_copy(_block(lz[a], r, (1 - x, 1 - y, c)), f2send_ref.at[0], f2recv_ref.at[0], me).wait_send()

    return _split_call(name, body, list(lands), [send_sems, recv_sems, fsend, frecv, rsend, f2send, f2recv], after, [],
                       False)


def _sibling_start(name, grads, after):
    n = len(grads)
    lands = [_landing((4, g.shape[0] // N_DEV, D), g.dtype) for g in grads]

    def body(*refs):
        ins, lz = refs[:n], refs[n:2 * n]
        send_sem, recv_sem = refs[2 * n + len(after)], refs[2 * n + len(after) + 1]
        token = refs[-1]
        x, y, c = _position()
        for a in range(n):
            r = grads[a].shape[0] // N_DEV
            for q in range(4):
                pltpu.make_async_remote_copy(
                    src_ref=ins[a].at[pl.ds((2 * q + 1 - c) * r, r), :], dst_ref=lz[a].at[q], send_sem=send_sem.at[0],
                    recv_sem=recv_sem.at[0], device_id=(x, y, 1 - c), device_id_type=MESH).start()
        token[...] = jnp.zeros_like(token)

    out = _split_call(name, body, list(grads) + lands, [], after, [(1,), (1,)], True)
    return out[0], out[1], out[2:2 + n], out[2 + n:2 + 2 * n], out[-1]


def _sibling_finish(name, grads, lands, send_sem, recv_sem, after):
    n = len(grads)

    def body(*refs):
        ins, lz = refs[:n], refs[n:2 * n]
        send_ref, recv_ref = refs[2 * n], refs[2 * n + 1]
        x, y, c = _position()
        for a in range(n):
            r = grads[a].shape[0] // N_DEV
            for q in range(4):
                cp = pltpu.make_async_remote_copy(
                    src_ref=ins[a].at[pl.ds((2 * q + 1 - c) * r, r), :], dst_ref=lz[a].at[q], send_sem=send_ref.at[0],
                    recv_sem=recv_ref.at[0], device_id=(x, y, c), device_id_type=MESH)
                cp.wait_send()
                cp.wait_recv()

    out = _split_call(name, body, list(grads) + list(lands), [send_sem, recv_sem], after, [], False)
    return out[:n], out[n:2 * n]


def _chip_start(name, parts, after):
    n = len(parts)
    lands = [_landing((3,) + p.shape[1:], p.dtype) for p in parts]

    def body(*refs):
        ins, lz = refs[:n], refs[n:2 * n]
        send_sems, recv_sems = refs[2 * n + len(after)], refs[2 * n + len(after) + 1]
        token = refs[-1]
        x, y, c = _position()
        for a in range(n):
            for j, chip in enumerate(_gather_peers(x, y, c)[:3]):
                pltpu.make_async_remote_copy(
                    src_ref=ins[a].at[2 * chip[0] + chip[1]], dst_ref=lz[a].at[j], send_sem=send_sems.at[j],
                    recv_sem=recv_sems.at[j], device_id=chip, device_id_type=MESH).start()
        token[...] = jnp.zeros_like(token)

    out = _split_call(name, body, list(parts) + lands, [], after, [(3,), (3,)], True)
    return out[0], out[1], out[2:2 + n], out[2 + n:2 + 2 * n], out[-1]


def _chip_finish(name, parts, lands, send_sems, recv_sems, after):
    n = len(parts)

    def body(*refs):
        ins, lz = refs[:n], refs[n:2 * n]
        send_ref, recv_ref = refs[2 * n], refs[2 * n + 1]
        me = _position()
        for a in range(n):
            for j in range(3):
                cp = pltpu.make_async_remote_copy(
                    src_ref=ins[a].at[j], dst_ref=lz[a].at[j], send_sem=send_ref.at[j], recv_sem=recv_ref.at[j],
                    device_id=me, device_id_type=MESH)
                cp.wait_send()
                cp.wait_recv()

    out = _split_call(name, body, list(parts) + list(lands), [send_sems, recv_sems], after, [], False)
    return out[:n], out[n:2 * n]


def _other_devices(x, y, c):
    return [(x + (k >> 2 & 1) * (1 - 2 * x), y + (k >> 1 & 1) * (1 - 2 * y), c + (k & 1) * (1 - 2 * c))
            for k in range(1, N_DEV)]


def _broadcast_start(name, arrays, after):
    n = len(arrays)
    lands = [_landing((N_DEV,) + a.shape, a.dtype) for a in arrays]

    def body(*refs):
        ins, lz = refs[:n], refs[n:2 * n]
        send_sems, recv_sems = refs[2 * n + len(after)], refs[2 * n + len(after) + 1]
        token = refs[-1]
        x, y, c = _position()
        for a in range(n):
            for k, peer in enumerate(_other_devices(x, y, c)):
                pltpu.make_async_remote_copy(
                    src_ref=ins[a], dst_ref=lz[a].at[4 * x + 2 * y + c], send_sem=send_sems.at[k],
                    recv_sem=recv_sems.at[k], device_id=peer, device_id_type=MESH).start()
        token[...] = jnp.zeros_like(token)

    out = _split_call(name, body, list(arrays) + lands, [], after, [(N_DEV - 1,), (N_DEV - 1,)], True)
    return out[0], out[1], out[2:2 + n], out[2 + n:2 + 2 * n], out[-1]


def _broadcast_finish(name, arrays, lands, send_sems, recv_sems, after):
    n = len(arrays)

    def body(*refs):
        ins, lz = refs[:n], refs[n:2 * n]
        send_ref, recv_ref = refs[2 * n], refs[2 * n + 1]
        x, y, c = _position()
        for a in range(n):
            for k, peer in enumerate(_other_devices(x, y, c)):
                cp = pltpu.make_async_remote_copy(
                    src_ref=ins[a], dst_ref=lz[a].at[4 * peer[0] + 2 * peer[1] + peer[2]], send_sem=send_ref.at[k],
                    recv_sem=recv_ref.at[k], device_id=(x, y, c), device_id_type=MESH)
                cp.wait_send()
                cp.wait_recv()

    out = _split_call(name, body, list(arrays) + list(lands), [send_sems, recv_sems], after, [], False)
    return out[:n], out[n:2 * n]


def _row_tile(r, target):
    return max(t for t in range(16, min(r, target) + 1, 16) if r % t == 0)


CHIP_PARTIAL_BYTES = 12 * 1024 * 1024


def _chip_partial(name, grads, gots, c):
    n = len(grads)
    r = grads[0].shape[0] // N_DEV
    tr = _row_tile(r, CHIP_PARTIAL_BYTES // (n * 3 * D * 2))

    def body(c_ref, *refs):
        for a in range(n):
            refs[2 * n + a][...] = (refs[a][...].astype(F32) + refs[n + a][...].astype(F32)).astype(BF16)

    blk = pl.BlockSpec((None, tr, D), lambda q, i, c_ref: (q, i, 0))
    return pl.pallas_call(
        body, name=name,
        grid_spec=pltpu.PrefetchScalarGridSpec(
            num_scalar_prefetch=1, grid=(4, r // tr),
            in_specs=[pl.BlockSpec((None, None, tr, D), lambda q, i, c_ref: (q, c_ref[0], i, 0))] * n + [blk] * n,
            out_specs=[blk] * n),
        out_shape=[jax.ShapeDtypeStruct((4, r, D), BF16)] * n, compiler_params=_cparams(),
    )(c, *[g.reshape(4, 2, r, D) for g in grads], *gots)


class _WeightGather:
    def __init__(self, groups):
        self.groups = list(groups)
        self.index = {key: i for i, (key, _, _) in enumerate(groups)}
        self.state = [None] * len(groups)
        self.token = ()
        for i in range(min(2, len(groups))):
            self._start(i)

    def _tag(self, i):
        return "%s_%d" % self.groups[i][0][::-1]

    def _start(self, i):
        send, recv, lz, tok = _gather_start("gather_start_" + self._tag(i), self.groups[i][2], self.token)
        self.state[i] = dict(send=send, recv=recv, lands=lz)
        self.token = (tok,)

    def _step(self, name, near, far, fresh, marker):
        exists = lambda i: i is not None and i < len(self.groups)
        near, far, fresh = (i if exists(i) else None for i in (near, far, fresh))
        res = _gather_step(
            name, None if near is None else (self.state[near]["lands"], self.state[near]["recv"]),
            None if far is None else (self.state[far]["lands"], self.state[far]["rrecv"]),
            None if fresh is None else self.groups[fresh][2], tuple(marker) + self.token)
        self.token = (res["token"],)
        if near is not None:
            self.state[near].update(lands=res["near"], fsend=res["fsend"], frecv=res["frecv"], rsend=res["rsend"],
                                    rrecv=res["rrecv"])
        if far is not None:
            self.state[far].update(lands=res["far"], f2send=res["f2send"], f2recv=res["f2recv"])
        if fresh is not None:
            self.state[fresh] = dict(send=res["send"], recv=res["recv"], lands=res["fresh"])

    def fetch(self, layer, group, marker):
        k = self.index[(layer, group)]
        if k == 0:
            self._step("gather_step_first", 0, None, None, marker)
        self._step("gather_step_" + self._tag(k), k + 1, k, k + 2, marker)
        st = self.state[k]
        lz = _gather_finish("gather_finish_" + self._tag(k), st["lands"], st["send"], st["recv"], st["fsend"],
                            st["frecv"], st["rsend"], st["f2send"], st["f2recv"], self.token)
        self.state[k] = None
        return dict(zip(self.groups[k][1], lz))


class _GradReduce:
    def __init__(self, core, chip):
        self.core, self.chip = core, chip
        self.layer = None
        self.token = ()
        self.at_sibling, self.at_chips = [], []
        self.extra, self.smalls = (), {}

    def after(self):
        return self.token

    def add(self, group, names, grads):
        tag = "%s_%d" % (group, self.layer)
        send, recv, grads, lands, tok = _sibling_start("grad_sibling_start_" + tag, grads, self.token)
        self.at_sibling.append((tag, [(self.layer, n) for n in names], send, recv, grads, lands))
        self.token = (tok,)

    def advance(self, marker):
        for tag, keys, send, recv, grads, lands in self.at_sibling:
            grads, lands = _sibling_finish("grad_sibling_finish_" + tag, grads, lands, send, recv, marker)
            parts = _chip_partial("chip_partial_" + tag, grads, lands, self.core)
            send, recv, parts, lands, tok = _chip_start("grad_chip_start_" + tag, parts, ())
            self.at_chips.append([tag, keys, send, recv, parts, lands])
            self.token = (tok,)
        self.at_sibling = []

    def small(self, part, arrays):
        keys = list(arrays)
        send, recv, own, slots, tok = _broadcast_start(
            "small_grads_start_%d_%s" % (self.layer, part), [arrays[k] for k in keys], self.token)
        self.smalls.setdefault(self.layer, []).append((part, keys, send, recv, own, slots))
        self.token = (tok,)

    def small_finish(self, layer, marker):
        mine, theirs = {}, {}
        for part, keys, send, recv, own, slots in self.smalls[layer]:
            own, slots = _broadcast_finish("small_grads_finish_%d_%s" % (layer, part), own, slots, send, recv, marker)
            mine.update(zip(keys, own))
            theirs.update(zip(keys, slots))
        return mine, theirs

    def collect(self, key, marker):
        for entry in self.at_chips:
            tag, keys, send, recv, parts, lands = entry
            if key in keys:
                if send is not None:
                    parts, lands = _chip_finish("grad_chip_finish_" + tag, parts, lands, send, recv, marker)
                    entry[2:] = [None, None, parts, lands]
                i = keys.index(key)
                return parts[i], lands[i]
        raise KeyError(key)


def _adamw_math(w, g, m, v):
    m = ADAM_B1 * m + (1.0 - ADAM_B1) * g
    v = ADAM_B2 * v + (1.0 - ADAM_B2) * jnp.square(g)
    m_hat = m / (1.0 - ADAM_B1 ** ADAM_STEP)
    v_hat = v / (1.0 - ADAM_B2 ** ADAM_STEP)
    delta = -ADAM_LR * (m_hat / (jnp.sqrt(v_hat) + ADAM_EPS) + ADAM_WD * w)
    return delta, m, v


def _adamw_small(wts, mom_m, mom_v, own, gathered, loss_own, loss_gathered, dev):
    names = SMALL
    nw = len(names)
    na = len(SMALL_ARRAYS)

    def body(dev_ref, *refs):
        w_refs, m_refs, v_refs = (dict(zip(names, refs[i * nw:(i + 1) * nw])) for i in range(3))
        own_refs = refs[3 * nw:3 * nw + DEPTH * na]
        g_refs = refs[3 * nw + DEPTH * na:3 * nw + 2 * DEPTH * na]
        loss_own_ref, loss_got_ref = refs[3 * nw + 2 * DEPTH * na:3 * nw + 2 * DEPTH * na + 2]
        outs = refs[3 * nw + 2 * DEPTH * na + 2:]
        g_out, d_out, m_out, v_out = (dict(zip(names, outs[i * nw:(i + 1) * nw])) for i in range(4))
        me = dev_ref[0]

        loss = None
        for d in range(N_DEV):
            for b in range(loss_own.shape[0]):
                term = jnp.where(me == d, loss_own_ref[b], loss_got_ref[d, b])
                loss = term if loss is None else loss + term
        outs[4 * nw][...] = loss

        def update(name, at, g):
            g_out[name][at] = g
            d_out[name][at], m_out[name][at], v_out[name][at] = _adamw_math(
                w_refs[name][at], g, m_refs[name][at], v_refs[name][at])

        for l in range(DEPTH):
            mine = dict(zip(SMALL_ARRAYS, own_refs[l * na:(l + 1) * na]))
            got = dict(zip(SMALL_ARRAYS, g_refs[l * na:(l + 1) * na]))

            def total(key, at):
                acc = None
                for d in range(N_DEV):
                    term = jnp.where(me == d, mine[key][at] if at else mine[key][...], got[key][(d,) + at])
                    acc = term if acc is None else acc + term
                return acc

            row = (slice(l, l + 1),)
            for k, name in enumerate(NORM_NAMES):
                update(name, row, total("norms", (slice(k, k + 1),)))
            for k, name in enumerate(VEC_NAMES):
                update(name, row, total("vecs", (slice(k, k + 1),)))
            update("gmlp_v_gain", (l,), total("gain_bias", (slice(0, NH),)))
            update("b_spatial", (l,), total("gain_bias", (slice(NH, 2 * NH),)))
            update("w_spatial", (l,), total("w_spatial", ()))
            update("w_pool", (l,), total("w_pool", ()))
            update("w_dw", (l,), total("w_dw", (slice(0, CONV_K),)))

    args = [src[n] for src in (wts, mom_m, mom_v) for n in names]
    args += [src[l][k] for src in (own, gathered) for l in range(DEPTH) for k in SMALL_ARRAYS]
    args += [loss_own, loss_gathered]
    outs = pl.pallas_call(
        body, name="adamw_small",
        in_specs=[pl.BlockSpec(memory_space=pltpu.SMEM)] + [pl.BlockSpec(memory_space=pltpu.VMEM)] * len(args),
        out_shape=[jax.ShapeDtypeStruct(wts[n].shape, F32) for _ in range(4) for n in names]
        + [jax.ShapeDtypeStruct((8, LANES), F32)],
        compiler_params=_cparams(),
    )(dev, *args)
    return tuple(dict(zip(names, outs[i * nw:(i + 1) * nw])) for i in range(4)) + (outs[4 * nw],)


def _adamw_layers(name, w, reduced, m, v, chip, tr, transposed=False, after=()):
    nl, r, cdim = w.shape
    tr = _row_tile(r, tr)
    nb = r // tr

    def body(q_ref, w_ref, p0_ref, g0_ref, p1_ref, g1_ref, m_ref, v_ref, *rest):
        g_ref, d_ref, nm_ref, nv_ref = rest[len(after):]

        def total(p_ref, got_ref):
            acc = p_ref[...].astype(F32)
            for j in range(3):
                acc = acc + got_ref[j].astype(F32)
            return acc

        g = jnp.where(pl.program_id(0) == 0, total(p0_ref, g0_ref), total(p1_ref, g1_ref))
        if transposed:
            g = g.T
        g_ref[...] = g
        d_ref[...], nm_ref[...], nv_ref[...] = _adamw_math(w_ref[...], g, m_ref[...], v_ref[...])

    blk = pl.BlockSpec((None, tr, cdim), lambda l, i, q: (l, i, 0))
    first = lambda l, i: i * (1 - l) + (nb - 1) * l
    second = lambda l, i: i * l
    if transposed:
        gshape = (cdim, tr)
        at = lambda lead, i: (lead, 0, i)
    else:
        gshape = (tr, cdim)
        at = lambda lead, i: (lead, i, 0)
    specs = [blk,
             pl.BlockSpec((None,) + gshape, lambda l, i, q: at(q[0], first(l, i))),
             pl.BlockSpec((3,) + gshape, lambda l, i, q: at(0, first(l, i))),
             pl.BlockSpec((None,) + gshape, lambda l, i, q: at(q[0], second(l, i))),
             pl.BlockSpec((3,) + gshape, lambda l, i, q: at(0, second(l, i))), blk, blk] + [ANY] * len(after)
    shape = jax.ShapeDtypeStruct((nl, r, cdim), F32)
    return pl.pallas_call(
        body, name=name,
        grid_spec=pltpu.PrefetchScalarGridSpec(num_scalar_prefetch=1, grid=(nl, nb), in_specs=specs, out_specs=[blk] * 4),
        out_shape=[shape] * 4, compiler_params=_cparams(),
    )(chip, w, *reduced[0], *reduced[1], m, v, *after)


def _to_rows(name, a):
    return jnp.swapaxes(a, 1, 2) if name == "w_in" else a


def _place_own_transposed(name, srcs, dev, out_dtype, tc):
    n = len(srcs)
    kdim, cdim = srcs[0][0].shape[-2:]

    def body(dev_ref, *refs):
        for a in range(n):
            refs[n + a][...] = refs[a][...].T.astype(out_dtype)

    return pl.pallas_call(
        body, name=name,
        grid_spec=pltpu.PrefetchScalarGridSpec(
            num_scalar_prefetch=1, grid=(kdim // tc,),
            in_specs=[pl.BlockSpec((None, tc, cdim), lambda i, d, l=l: (l, i, 0)) for _, l in srcs],
            out_specs=[pl.BlockSpec((cdim, tc), lambda i, d: (d[0], i))] * n),
        out_shape=[jax.ShapeDtypeStruct((N_DEV * cdim, kdim), out_dtype)] * n, compiler_params=_cparams(),
    )(dev, *[a for a, _ in srcs])


def _pack(arrays, rows):
    flat = jnp.concatenate([a.reshape(-1) for a in arrays])
    return jnp.pad(flat, (0, rows * D - flat.shape[0])).reshape(rows, D)


def _rows_for(shapes, mult=8):
    total = 0
    for shp in shapes:
        size = 1
        for dim in shp:
            size *= dim
        total += size
    return -(-total // (mult * D)) * mult


def kernel(x, mem, norm_mix_pre, norm_mix_post, w_in, w_out, gmlp_v_gain, w_spatial, b_spatial, w_pool, s_pool, w_dw, b_dw, conv_ln_g, conv_ln_b, norm_xattn_pre, norm_mem, norm_xattn_post, w_q, w_k, w_v, w_o, norm_ffn_pre, norm_ffn_post, w_up, w_down, loss_target, m_norm_mix_pre, m_norm_mix_post, m_w_in, m_w_out, m_gmlp_v_gain, m_w_spatial, m_b_spatial, m_w_pool, m_s_pool, m_w_dw, m_b_dw, m_conv_ln_g, m_conv_ln_b, m_norm_xattn_pre, m_norm_mem, m_norm_xattn_post, m_w_q, m_w_k, m_w_v, m_w_o, m_norm_ffn_pre, m_norm_ffn_post, m_w_up, m_w_down, v_norm_mix_pre, v_norm_mix_post, v_w_in, v_w_out, v_gmlp_v_gain, v_w_spatial, v_b_spatial, v_w_pool, v_s_pool, v_w_dw, v_b_dw, v_conv_ln_g, v_conv_ln_b, v_norm_xattn_pre, v_norm_mem, v_norm_xattn_post, v_w_q, v_w_k, v_w_v, v_w_o, v_norm_ffn_pre, v_norm_ffn_post, v_w_up, v_w_down):
    args = dict(locals())
    wts = {n: args[n] for n in WEIGHTS}
    mom_m = {n: args["m_" + n] for n in WEIGHTS}
    mom_v = {n: args["v_" + n] for n in WEIGHTS}
    xi, yi, ci = _position()
    me = 4 * xi + 2 * yi + ci

    dev = jnp.reshape(me, (1,)).astype(jnp.int32)
    lands = {}
    for call, names, tr in (("place_in", ("w_in",), 256), ("place_att", ("w_out", "w_q", "w_k", "w_v", "w_o"), 64),
                            ("place_up", ("w_up",), 256), ("place_down", ("w_down",), 256)):
        srcs = [(_to_rows(n, wts[n]), l) for l in range(DEPTH) for n in names]
        placed = (_place_own_transposed if names == ("w_up",) else _place_own)(call, srcs, dev, BF16, tr)
        lands.update(zip([(l, n) for l in range(DEPTH) for n in names], placed))
    (lands[(0, "taps")],) = _place_own("place_taps", [(_pack([w_dw], _rows_for([w_dw.shape])), None)], dev, F32, 8)
    groups = []
    for l in range(DEPTH):
        for group, names in GATHER_GROUPS:
            if (l, group) == (0, "in"):
                names = names + ("taps",)
            groups.append(((l, group), names, [lands[(l, n)] for n in names]))
    gather = _WeightGather(groups)

    def fetch(layer, group, marker):
        w = gather.fetch(layer, group, marker)
        if "taps" in w:
            blocks = w["taps"].reshape(N_DEV, -1)[:, :w_dw.size].reshape((N_DEV,) + w_dw.shape)
            w["taps"] = jnp.moveaxis(blocks, 0, 2).reshape(DEPTH, CONV_K, CW)
        return w

    reduce = _GradReduce(jnp.reshape(ci, (1,)).astype(jnp.int32), jnp.reshape(2 * xi + yi, (1,)).astype(jnp.int32))
    small = {n: wts[n] for n in SMALL if n != "w_dw"}
    _, dx = _local_step(x[0], mem[0], loss_target[0], fetch, small, reduce)
    reduce.advance((dx,))

    grad_w, delta, new_m, new_v = {}, {}, {}, {}
    marker = (dx,) + tuple(reduce.after())
    for n in UPDATE_ORDER:
        reduced = [reduce.collect((l, n), marker) for l in range(DEPTH)]
        outs = _adamw_layers("adamw_" + n, _to_rows(n, wts[n]), reduced, _to_rows(n, mom_m[n]), _to_rows(n, mom_v[n]),
                             reduce.chip, 256, transposed=n == "w_up", after=marker)
        grad_w[n], delta[n], new_m[n], new_v[n] = (_to_rows(n, o) for o in outs)
        marker = (outs[1],)

    own, slots = [None] * DEPTH, [None] * DEPTH
    for l in reversed(range(DEPTH)):
        own[l], slots[l] = reduce.small_finish(l, marker)
        if l == 0:
            loss_own, loss_slots = own[l].pop("loss"), slots[l].pop("loss")
    shard_cols = CW // N_DEV
    for l in range(DEPTH):
        own[l]["w_dw"] = lax.dynamic_slice_in_dim(own[l]["w_dw"], me * shard_cols, shard_cols, axis=1)
        slots[l]["w_dw"] = lax.dynamic_slice_in_dim(slots[l]["w_dw"], me * shard_cols, shard_cols, axis=2)
    *small_out, loss_tile = _adamw_small(wts, mom_m, mom_v, own, slots, loss_own, loss_slots, dev)
    for dst, src in zip((grad_w, delta, new_m, new_v), small_out):
        dst.update(src)

    return (loss_tile[0, 0], dx[None], *[grad_w[n] for n in WEIGHTS], *[delta[n] for n in WEIGHTS],
            *[new_m[n] for n in WEIGHTS], *[new_v[n] for n in WEIGHTS])
```

```python
import functools

import jax
import jax.numpy as jnp
from jax import lax
from jax.experimental import pallas as pl
from jax.experimental.pallas import tpu as pltpu

F32 = jnp.float32
BF16 = jnp.bfloat16

D = 2048
GW = 1024
PW = 512
CW = 512
HD = 128
NH = 8
NG = 4
POOL_WINDOWS = (2, 4, 8, 16)
CONV_K = 31
IN_COLS = 2 * GW + PW + 2 * CW
XH = 4
XHD = D // XH
ATT_SCALE = XHD ** -0.5
RMS_EPS = 1e-6
LN_EPS = 1e-5
DEPTH = 2
N_DEV = 8

ADAM_LR = 0.001
ADAM_B1 = 0.9
ADAM_B2 = 0.999
ADAM_EPS = 1e-08
ADAM_WD = 0.01
ADAM_STEP = 10

LANES = 128
CONV_HALO = 32
POOL_HALO = 16
ROW_TILE = 128
VMEM_LIMIT = 60 * 1024 * 1024

MESH = pl.DeviceIdType.MESH
NT = (((1,), (1,)), ((), ()))
NN = (((1,), (0,)), ((), ()))
TN = (((0,), (0,)), ((), ()))

UPDATE_ORDER = ("w_down", "w_up", "w_o", "w_q", "w_k", "w_v", "w_out", "w_in")
GATHER_GROUPS = (("in", ("w_in",)), ("out", ("w_out",)), ("att", ("w_q", "w_k", "w_v", "w_o")), ("up", ("w_up",)),
                 ("down", ("w_down",)))
SMALL = ("norm_mix_pre", "norm_mix_post", "gmlp_v_gain", "w_spatial", "b_spatial", "w_pool", "s_pool",
         "w_dw", "b_dw", "conv_ln_g", "conv_ln_b", "norm_xattn_pre", "norm_mem", "norm_xattn_post",
         "norm_ffn_pre", "norm_ffn_post")
WEIGHTS = ("norm_mix_pre", "norm_mix_post", "w_in", "w_out", "gmlp_v_gain", "w_spatial", "b_spatial", "w_pool",
           "s_pool", "w_dw", "b_dw", "conv_ln_g", "conv_ln_b", "norm_xattn_pre", "norm_mem", "norm_xattn_post",
           "w_q", "w_k", "w_v", "w_o", "norm_ffn_pre", "norm_ffn_post", "w_up", "w_down")


def _cparams():
    return pltpu.CompilerParams(vmem_limit_bytes=VMEM_LIMIT)


def _dot(a, b, dims):
    return lax.dot_general(a, b, dims, preferred_element_type=F32)


def _rms(x, g):
    y = x * lax.rsqrt(jnp.mean(x * x, axis=-1, keepdims=True) + RMS_EPS)
    return y * g


def _rms_bwd(x, g, dy):
    r = lax.rsqrt(jnp.mean(x * x, axis=-1, keepdims=True) + RMS_EPS)
    xh = x * r
    t = dy * g
    dx = r * (t - xh * jnp.mean(t * xh, axis=-1, keepdims=True))
    return dx, jnp.sum(dy * xh, axis=0, keepdims=True)


def _gelu(x):
    cdf = 0.5 * (1.0 + jnp.tanh(0.7978845608028654 * (x + 0.044715 * (x * x * x))))
    return x * cdf


def _layer_norm(x, g, b=None):
    mu = jnp.mean(x, axis=-1, keepdims=True)
    xc = x - mu
    var = jnp.mean(xc * xc, axis=-1, keepdims=True)
    y = xc * lax.rsqrt(var + LN_EPS) * g
    return y if b is None else y + b


def _sigmoid(x):
    return 1.0 / (1.0 + jnp.exp(-x))


def _gmlp_rows(zu, zv, gv):
    return _gelu(zu), _layer_norm(_gelu(zv), gv)


def _glu(cv, cg):
    return cv * _sigmoid(cg)


def _ln_silu(h, g, b):
    y = _layer_norm(h, g, b)
    return y * _sigmoid(y)


ANY = pl.BlockSpec(memory_space=pl.ANY)


ROWS_TILE = 256
COLS_TILE = 512
DW_TILE = 512
RESIDENT_K = 2048
STREAM_K_TILE = 1024
STREAM_ROWS = 512


def _k_tiles(kdim):
    if kdim <= RESIDENT_K:
        return ROWS_TILE, kdim
    return STREAM_ROWS, max(t for t in range(LANES, STREAM_K_TILE + 1, LANES) if kdim % t == 0)


def _rowop_mm(name, kind, rows, g, w, dims, out_dtype, u=None, after=()):
    s = rows[0].shape[0]
    n = w.shape[0] if dims == NT else w.shape[1]
    tm, tn = min(ROWS_TILE, s), min(COLS_TILE, n)
    ni, nj = s // tm, n // tn
    bwd = kind == "rms_bwd"
    out_shape = [jax.ShapeDtypeStruct((s, n), out_dtype), jax.ShapeDtypeStruct((s, D), BF16)]
    if bwd:
        out_shape.append(jax.ShapeDtypeStruct((ni, 1, D), F32))

    if n <= RESIDENT_K and u is None:
        def row_body(*refs):
            refs = list(refs)
            row_refs = [refs.pop(0) for _ in rows]
            g_ref, w_ref = refs.pop(0), refs.pop(0)
            del refs[:len(after)]
            if bwd:
                a, dg = _rms_bwd(row_refs[0][...], g_ref[...], row_refs[1][...])
                refs[2][0] = dg
            else:
                a = _rms(row_refs[0][...], g_ref[...])
            a = a.astype(BF16)
            refs[1][...] = a
            refs[0][...] = _dot(a, w_ref[...], dims).astype(out_dtype)

        blk = pl.BlockSpec((tm, D), lambda i: (i, 0))
        return pl.pallas_call(
            row_body, name=name, grid=(ni,),
            in_specs=[blk] * len(rows) + [pl.BlockSpec((1, D), lambda i: (0, 0)), pl.BlockSpec(w.shape, lambda i: (0, 0))]
            + [ANY] * len(after),
            out_specs=[pl.BlockSpec((tm, n), lambda i: (i, 0)), blk]
            + ([pl.BlockSpec((1, 1, D), lambda i: (i, 0, 0))] if bwd else []),
            out_shape=out_shape, compiler_params=_cparams(),
        )(*rows, g, w, *after)

    def body(*refs):
        refs = list(refs)
        row_refs = [refs.pop(0) for _ in rows]
        g_ref, w_ref = refs.pop(0), refs.pop(0)
        u_ref = refs.pop(0) if u is not None else None
        del refs[:len(after)]
        out_ref, a_ref = refs.pop(0), refs.pop(0)
        dg_ref = refs.pop(0) if bwd else None
        a_all = refs.pop(0)
        t = pl.program_id(0)

        @pl.when(t < ni)
        def _():
            if bwd:
                a, dg = _rms_bwd(row_refs[0][...], g_ref[...], row_refs[1][...])
                dg_ref[0] = dg
            else:
                a = _rms(row_refs[0][...], g_ref[...])
            a_ref[...] = a.astype(BF16)
            a_all[pl.ds(pl.multiple_of(t * tm, tm), tm), :] = a.astype(BF16)

        @pl.when(t >= ni)
        def _():
            acc = _dot(a_all[...], w_ref[...], dims)
            if u_ref is not None:
                acc = acc * (2.0 * jnp.maximum(u_ref[...], 0.0))
            out_ref[...] = acc.astype(out_dtype)

    rows_at = lambda t: jnp.minimum(t, ni - 1)
    cols_at = lambda t: jnp.maximum(t - ni, 0)
    row_spec = pl.BlockSpec((tm, D), lambda t: (rows_at(t), 0))
    w_spec = (pl.BlockSpec((tn, D), lambda t: (cols_at(t), 0)) if dims == NT
              else pl.BlockSpec((D, tn), lambda t: (0, cols_at(t))))
    tile = pl.BlockSpec((s, tn), lambda t: (0, cols_at(t)))
    in_specs = [row_spec] * len(rows) + [pl.BlockSpec((1, D), lambda t: (0, 0)), w_spec]
    in_specs += ([tile] if u is not None else []) + [ANY] * len(after)
    out_specs = [tile, row_spec]
    if bwd:
        out_specs.append(pl.BlockSpec((1, 1, D), lambda t: (rows_at(t), 0, 0)))
    return pl.pallas_call(
        body, name=name, grid=(ni + nj,), in_specs=in_specs, out_specs=out_specs, out_shape=out_shape,
        scratch_shapes=[pltpu.VMEM((s, D), BF16)], compiler_params=_cparams(),
    )(*rows, g, w, *([u] if u is not None else []), *after)


def _mm_rowop(name, kind, pairs, rows, g, relu2=False, after=()):
    s, kdim = pairs[0][0].shape
    tm, tk = _k_tiles(kdim)
    tm = min(tm, s)
    ni, nk = s // tm, kdim // tk
    npair = len(pairs)

    def body(*refs):
        refs = list(refs)
        a_refs = [refs.pop(0) for _ in range(npair)]
        w_refs = [refs.pop(0) for _ in range(npair)]
        row_refs = [refs.pop(0) for _ in rows]
        g_ref = refs.pop(0)
        del refs[:len(after)]
        acc = refs.pop()
        outs = refs
        k = pl.program_id(1)

        def product():
            total = None
            for a_ref, w_ref, (_, _, dims) in zip(a_refs, w_refs, pairs):
                a = a_ref[...]
                if relu2:
                    a = jnp.square(jnp.maximum(a, 0.0))
                term = _dot(a.astype(BF16), w_ref[...], dims)
                total = term if total is None else total + term
            return total

        def finish(h):
            if kind == "rms_res":
                outs[0][...] = row_refs[0][...] + _rms(h, g_ref[...])
                outs[1][...] = h
            else:
                dx, dg = _rms_bwd(row_refs[0][...], g_ref[...], h)
                if kind == "rms_bwd_res":
                    outs[0][...] = row_refs[1][...] + dx
                    outs[1][0] = dg
                else:
                    outs[0][0] = dg

        if nk == 1:
            finish(product())
            return

        @pl.when(k == 0)
        def _():
            acc[...] = jnp.zeros_like(acc)

        acc[...] += product()

        @pl.when(k == nk - 1)
        def _():
            finish(acc[...])

    row_spec = pl.BlockSpec((tm, D), lambda i, k: (i, 0))
    dg_shape = jax.ShapeDtypeStruct((ni, 1, D), F32)
    dg_spec = pl.BlockSpec((1, 1, D), lambda i, k: (i, 0, 0))
    in_specs = [pl.BlockSpec((tm, tk), lambda i, k: (i, k))] * npair
    for _, _, dims in pairs:
        in_specs.append(pl.BlockSpec((tk, D), lambda i, k: (k, 0)) if dims == NN
                        else pl.BlockSpec((D, tk), lambda i, k: (0, k)))
    in_specs += [row_spec] * len(rows) + [pl.BlockSpec((1, D), lambda i, k: (0, 0))] + [ANY] * len(after)
    if kind == "rms_res":
        out_shape = [jax.ShapeDtypeStruct((s, D), F32)] * 2
        out_specs = [row_spec, row_spec]
    elif kind == "rms_bwd_res":
        out_shape = [jax.ShapeDtypeStruct((s, D), F32), dg_shape]
        out_specs = [row_spec, dg_spec]
    else:
        out_shape = [dg_shape]
        out_specs = [dg_spec]
    return pl.pallas_call(
        body, name=name, grid=(ni, nk), in_specs=in_specs, out_specs=out_specs, out_shape=out_shape,
        scratch_shapes=[pltpu.VMEM((tm, D), F32)], compiler_params=_cparams(),
    )(*[p[0] for p in pairs], *[p[1] for p in pairs], *rows, g, *after)


def _mm_tn(name, a, gmat, relu2=False, after=()):
    s, m = a.shape
    tm = min(DW_TILE, m)
    ni = m // tm

    def body(a_ref, g_ref, *rest):
        av = a_ref[...]
        if relu2:
            av = jnp.square(jnp.maximum(av, 0.0))
        rest[len(after)][...] = _dot(av.astype(BF16), g_ref[...], TN).astype(BF16)

    return pl.pallas_call(
        body, name=name, grid=(ni,),
        in_specs=[pl.BlockSpec((s, tm), lambda i: (0, i)), pl.BlockSpec((s, D), lambda i: (0, 0))] + [ANY] * len(after),
        out_specs=pl.BlockSpec((tm, D), lambda i: (i, 0)),
        out_shape=jax.ShapeDtypeStruct((m, D), BF16), compiler_params=_cparams(),
    )(a, gmat, *after)


def _tril():
    r = lax.broadcasted_iota(jnp.int32, (HD, HD), 0)
    c = lax.broadcasted_iota(jnp.int32, (HD, HD), 1)
    return (c <= r).astype(F32)


def _gmlp_fwd(z, gv, ws, bst, tb):
    s = z.shape[0]
    tb = min(tb, s)

    def body(zu_ref, zv_ref, gv_ref, ws_ref, bst_ref, y_ref):
        tril = _tril()
        for h in range(NH):
            cols = slice(h * HD, (h + 1) * HD)
            u, vln = _gmlp_rows(zu_ref[:, cols], zv_ref[:, cols], gv_ref[h:h + 1, :])
            wm = (ws_ref[h] * tril).astype(BF16)
            vb = vln.astype(BF16)
            for c in range(tb // HD):
                rws = slice(c * HD, (c + 1) * HD)
                mixed = _dot(wm, vb[rws], NN) + bst_ref[:, h:h + 1]
                y_ref[rws, cols] = (u[rws] * mixed).astype(BF16)

    return pl.pallas_call(
        body, name="gmlp_fwd", grid=(s // tb,),
        in_specs=[pl.BlockSpec((tb, GW), lambda i: (i, 0)), pl.BlockSpec((tb, GW), lambda i: (i, 1)),
                  pl.BlockSpec((NH, HD), lambda i: (0, 0)), pl.BlockSpec((NH, HD, HD), lambda i: (0, 0, 0)),
                  pl.BlockSpec((HD, NH), lambda i: (0, 0))],
        out_specs=pl.BlockSpec((tb, GW), lambda i: (i, 0)),
        out_shape=jax.ShapeDtypeStruct((s, D), BF16), compiler_params=_cparams(),
    )(z, z, gv, ws, bst)


def _gmlp_bwd(z, dy, gv, ws, bst, tb, after=()):
    s = z.shape[0]
    tb = min(tb, s)
    nb = s // tb

    def body(zu_ref, zv_ref, dy_ref, gv_ref, ws_ref, bst_ref, *rest):
        dz_ref, dgv_ref, dws_ref, db_ref = rest[len(after):]
        tril = _tril()
        for h in range(NH):
            cols = slice(h * HD, (h + 1) * HD)
            (u, vln), vjp = jax.vjp(_gmlp_rows, zu_ref[:, cols], zv_ref[:, cols], gv_ref[h:h + 1, :])
            wmf = ws_ref[h] * tril
            wm = wmf.astype(BF16)
            wmt = wmf.T.astype(BF16)
            vb = vln.astype(BF16)
            dws = jnp.zeros((HD, HD), F32)
            db = jnp.zeros((HD, 1), F32)
            du_parts, dvln_parts = [], []
            for c in range(tb // HD):
                rws = slice(c * HD, (c + 1) * HD)
                mixed = _dot(wm, vb[rws], NN) + bst_ref[:, h:h + 1]
                dyc = dy_ref[rws, cols]
                du_parts.append(dyc * mixed)
                dmixed = dyc * u[rws]
                dmb = dmixed.astype(BF16)
                dws = dws + _dot(dmb, vb[rws], NT)
                db = db + jnp.sum(dmixed, axis=1, keepdims=True)
                dvln_parts.append(_dot(wmt, dmb, NN))
            du = jnp.concatenate(du_parts, axis=0)
            dvln = jnp.concatenate(dvln_parts, axis=0)
            dzu, dzv, dgv = vjp((du, dvln))
            dz_ref[:, cols] = dzu.astype(BF16)
            dz_ref[:, slice(GW + h * HD, GW + (h + 1) * HD)] = dzv.astype(BF16)
            dgv_ref[0, h:h + 1, :] = dgv
            dws_ref[0, h] = dws * tril
            db_ref[0, h] = jnp.broadcast_to(db, (HD, LANES))

    blk = pl.BlockSpec((tb, GW), lambda i: (i, 0))
    return pl.pallas_call(
        body, name="gmlp_bwd", grid=(nb,),
        in_specs=[blk, pl.BlockSpec((tb, GW), lambda i: (i, 1)), blk,
                  pl.BlockSpec((NH, HD), lambda i: (0, 0)), pl.BlockSpec((NH, HD, HD), lambda i: (0, 0, 0)),
                  pl.BlockSpec((HD, NH), lambda i: (0, 0))] + [ANY] * len(after),
        out_specs=[pl.BlockSpec((tb, 2 * GW), lambda i: (i, 0)), pl.BlockSpec((1, NH, HD), lambda i: (i, 0, 0)),
                   pl.BlockSpec((1, NH, HD, HD), lambda i: (i, 0, 0, 0)),
                   pl.BlockSpec((1, NH, HD, LANES), lambda i: (i, 0, 0, 0))],
        out_shape=[jax.ShapeDtypeStruct((s, IN_COLS), BF16),
                   jax.ShapeDtypeStruct((nb, NH, HD), F32), jax.ShapeDtypeStruct((nb, NH, HD, HD), F32),
                   jax.ShapeDtypeStruct((nb, NH, HD, LANES), F32)],
        compiler_params=_cparams(),
    )(z, z, dy, gv, ws, bst, *after)


POOL_TILE = 1024


def _pool_count(t0, window):
    pos = (t0 + lax.broadcasted_iota(jnp.int32, (POOL_TILE, LANES), 0)).astype(F32)
    return jnp.minimum(pos + 1.0, float(window))


def _window_sum(win, levels, back):
    n = win.shape[0]
    for lv in range(levels):
        step = 1 << lv
        win = win + pltpu.roll(win, n - step if back else step, 0)
    return win


def _pool_pooled(ppad_ref, t0, g):
    win = ppad_ref[pl.ds(t0, POOL_TILE + POOL_HALO), :]
    wsum = _window_sum(win, g + 1, False)[POOL_HALO:]
    return wsum / _pool_count(t0, POOL_WINDOWS[g]) - win[POOL_HALO:]


def _pool_fwd(z, wp, sp, y):
    s = z.shape[0]
    nt = s // POOL_TILE

    def body(p_ref, wp_ref, sp_ref, _, y_ref, ppad):
        for g in range(NG):
            cols = slice(g * LANES, (g + 1) * LANES)
            ppad[pl.ds(0, POOL_HALO), :] = jnp.zeros((POOL_HALO, LANES), F32)
            ppad[pl.ds(POOL_HALO, s), :] = p_ref[:, cols]
            wpb = wp_ref[g].astype(BF16)
            scale = sp_ref[:, cols]

            def tile(t, carry):
                t0 = pl.multiple_of(t * POOL_TILE, POOL_TILE)
                pooled = _pool_pooled(ppad, t0, g)
                y_ref[pl.ds(t0, POOL_TILE), cols] = (_dot(pooled.astype(BF16), wpb, NN) * scale).astype(BF16)
                return carry

            lax.fori_loop(0, nt, tile, 0)

    return pl.pallas_call(
        body, name="pool_fwd", grid=(1,),
        in_specs=[pl.BlockSpec((s, PW), lambda i: (0, 2 * GW // PW)),
                  pl.BlockSpec((NG, LANES, LANES), lambda i: (0, 0, 0)), pl.BlockSpec((1, PW), lambda i: (0, 0)), ANY],
        out_specs=pl.BlockSpec((s, PW), lambda i: (0, GW // PW)),
        out_shape=jax.ShapeDtypeStruct((s, D), BF16), input_output_aliases={3: 0},
        scratch_shapes=[pltpu.VMEM((s + POOL_HALO, LANES), F32)], compiler_params=_cparams(),
    )(z, wp, sp, y)


def _pool_bwd(z, dy, wp, sp, dz):
    s = z.shape[0]
    nt = s // POOL_TILE

    def body(p_ref, dy_ref, wp_ref, sp_ref, _, dp_ref, dwp_ref, dsp_ref, ppad, rpad, dpool):
        for g in range(NG):
            cols = slice(g * LANES, (g + 1) * LANES)
            ppad[pl.ds(0, POOL_HALO), :] = jnp.zeros((POOL_HALO, LANES), F32)
            ppad[pl.ds(POOL_HALO, s), :] = p_ref[:, cols]
            rpad[pl.ds(s, POOL_HALO), :] = jnp.zeros((POOL_HALO, LANES), F32)
            wpb = wp_ref[g].astype(BF16)
            scale = sp_ref[:, cols]

            def tile(t, carry):
                dwp, dsp = carry
                t0 = pl.multiple_of(t * POOL_TILE, POOL_TILE)
                pooled = _pool_pooled(ppad, t0, g)
                pb = pooled.astype(BF16)
                dyt = dy_ref[pl.ds(t0, POOL_TILE), cols]
                dsp = dsp + jnp.sum(dyt * _dot(pb, wpb, NN), axis=0, keepdims=True)
                dmm = (dyt * scale).astype(BF16)
                dwp = dwp + _dot(pb, dmm, TN)
                dpooled = _dot(dmm, wpb, NT)
                rpad[pl.ds(t0, POOL_TILE), :] = dpooled / _pool_count(t0, POOL_WINDOWS[g])
                dpool[pl.ds(t0, POOL_TILE), :] = dpooled
                return dwp, dsp

            dwp, dsp = lax.fori_loop(0, nt, tile, (jnp.zeros((LANES, LANES), F32), jnp.zeros((1, LANES), F32)))
            dwp_ref[g] = dwp
            dsp_ref[:, cols] = dsp

            def tile2(t, carry):
                t0 = pl.multiple_of(t * POOL_TILE, POOL_TILE)
                win = rpad[pl.ds(t0, POOL_TILE + POOL_HALO), :]
                back = _window_sum(win, g + 1, True)[:POOL_TILE]
                rows = pl.ds(t0, POOL_TILE)
                dp_ref[rows, cols] = (back - dpool[rows, :]).astype(BF16)
                return carry

            lax.fori_loop(0, nt, tile2, 0)

    return pl.pallas_call(
        body, name="pool_bwd", grid=(1,),
        in_specs=[pl.BlockSpec((s, PW), lambda i: (0, 2 * GW // PW)), pl.BlockSpec((s, PW), lambda i: (0, GW // PW)),
                  pl.BlockSpec((NG, LANES, LANES), lambda i: (0, 0, 0)), pl.BlockSpec((1, PW), lambda i: (0, 0)), ANY],
        out_specs=[pl.BlockSpec((s, PW), lambda i: (0, 2 * GW // PW)),
                   pl.BlockSpec((NG, LANES, LANES), lambda i: (0, 0, 0)), pl.BlockSpec((1, PW), lambda i: (0, 0))],
        out_shape=[jax.ShapeDtypeStruct((s, IN_COLS), BF16), jax.ShapeDtypeStruct((NG, LANES, LANES), F32),
                   jax.ShapeDtypeStruct((1, PW), F32)],
        input_output_aliases={4: 0},
        scratch_shapes=[pltpu.VMEM((s + POOL_HALO, LANES), F32), pltpu.VMEM((s + POOL_HALO, LANES), F32),
                        pltpu.VMEM((s, LANES), F32)],
        compiler_params=_cparams(),
    )(z, dy, wp, sp, dz)


CONV_LEAD = CONV_HALO - (CONV_K - 1)


SUBLANES = 8


def _sublane_shifts(win):
    n = win.shape[0]
    return [win] + [pltpu.roll(win, n - b, 0) for b in range(1, SUBLANES)]


def _shifted(shifts, offset):
    a, b = divmod(offset, SUBLANES)
    return shifts[b][a * SUBLANES:a * SUBLANES + ROW_TILE]


def _conv_taps(shifts, wdw_ref, lead, reverse):
    acc = jnp.zeros((ROW_TILE, CW), F32)
    for j in range(CONV_K):
        tap = (CONV_K - 1 - j) if reverse else j
        acc = acc + wdw_ref[tap:tap + 1, :] * _shifted(shifts, lead + j)
    return acc


def _conv_fill_glu(cv_ref, cg_ref, xpad, s):
    xpad[pl.ds(0, CONV_HALO), :] = jnp.zeros((CONV_HALO, CW), F32)

    def fill(t, carry):
        t0 = pl.multiple_of(t * ROW_TILE, ROW_TILE)
        rows = pl.ds(t0, ROW_TILE)
        xpad[pl.ds(t0 + CONV_HALO, ROW_TILE), :] = _glu(cv_ref[rows, :], cg_ref[rows, :])
        return carry

    lax.fori_loop(0, s // ROW_TILE, fill, 0)


def _conv_fwd(z, wdw, bdw, lng, lnb, y):
    s = z.shape[0]

    def body(cv_ref, cg_ref, wdw_ref, bdw_ref, lng_ref, lnb_ref, _, y_ref, xpad):
        _conv_fill_glu(cv_ref, cg_ref, xpad, s)

        def tile(t, carry):
            t0 = pl.multiple_of(t * ROW_TILE, ROW_TILE)
            shifts = _sublane_shifts(xpad[pl.ds(t0, ROW_TILE + CONV_HALO), :])
            hc = _conv_taps(shifts, wdw_ref, CONV_LEAD, False) + bdw_ref[...]
            y_ref[pl.ds(t0, ROW_TILE), :] = _ln_silu(hc, lng_ref[...], lnb_ref[...]).astype(BF16)
            return carry

        lax.fori_loop(0, s // ROW_TILE, tile, 0)

    vec = pl.BlockSpec((1, CW), lambda i: (0, 0))
    return pl.pallas_call(
        body, name="conv_fwd", grid=(1,),
        in_specs=[pl.BlockSpec((s, CW), lambda i: (0, (2 * GW + PW) // CW)),
                  pl.BlockSpec((s, CW), lambda i: (0, (2 * GW + PW) // CW + 1)),
                  pl.BlockSpec((CONV_K + 1, CW), lambda i: (0, 0)), vec, vec, vec, ANY],
        out_specs=pl.BlockSpec((s, CW), lambda i: (0, (GW + PW) // CW)),
        out_shape=jax.ShapeDtypeStruct((s, D), BF16), input_output_aliases={6: 0},
        scratch_shapes=[pltpu.VMEM((s + CONV_HALO, CW), F32)], compiler_params=_cparams(),
    )(z, z, wdw, bdw, lng, lnb, y)


def _conv_bwd(z, dy, wdw, bdw, lng, lnb, dz):
    s = z.shape[0]

    def body(cv_ref, cg_ref, dy_ref, wdw_ref, bdw_ref, lng_ref, lnb_ref, _,
             dz_ref, dwdw_ref, dbdw_ref, dlng_ref, dlnb_ref, xpad, dpad, dcg_keep):
        @pl.when(pl.program_id(0) == 0)
        def _():
            compute(cv_ref, cg_ref, dy_ref, wdw_ref, bdw_ref, lng_ref, lnb_ref,
                    dz_ref, dcg_keep, dwdw_ref, dbdw_ref, dlng_ref, dlnb_ref, xpad, dpad)

        @pl.when(pl.program_id(0) == 1)
        def _():
            dz_ref[...] = dcg_keep[...]

    def compute(cv_ref, cg_ref, dy_ref, wdw_ref, bdw_ref, lng_ref, lnb_ref,
                dcv_ref, dcg_ref, dwdw_ref, dbdw_ref, dlng_ref, dlnb_ref, xpad, dpad):
        _conv_fill_glu(cv_ref, cg_ref, xpad, s)
        dpad[pl.ds(s, CONV_HALO), :] = jnp.zeros((CONV_HALO, CW), F32)
        dwdw_ref[...] = jnp.zeros((CONV_K + 1, CW), F32)

        def tile(t, carry):
            db, dg, dbeta = carry
            t0 = pl.multiple_of(t * ROW_TILE, ROW_TILE)
            shifts = _sublane_shifts(xpad[pl.ds(t0, ROW_TILE + CONV_HALO), :])
            hc = _conv_taps(shifts, wdw_ref, CONV_LEAD, False) + bdw_ref[...]
            _, vjp = jax.vjp(_ln_silu, hc, lng_ref[...], lnb_ref[...])
            dhc, dg_t, dbeta_t = vjp(dy_ref[pl.ds(t0, ROW_TILE), :])
            dpad[pl.ds(t0, ROW_TILE), :] = dhc
            for j in range(CONV_K):
                dwdw_ref[j:j + 1, :] += jnp.sum(dhc * _shifted(shifts, CONV_LEAD + j), axis=0, keepdims=True)
            return db + jnp.sum(dhc, axis=0, keepdims=True), dg + dg_t, dbeta + dbeta_t

        zero = jnp.zeros((1, CW), F32)
        db, dg, dbeta = lax.fori_loop(0, s // ROW_TILE, tile, (zero, zero, zero))
        dbdw_ref[...] = db
        dlng_ref[...] = dg
        dlnb_ref[...] = dbeta

        def tile2(t, carry):
            t0 = pl.multiple_of(t * ROW_TILE, ROW_TILE)
            rows = pl.ds(t0, ROW_TILE)
            dglu = _conv_taps(_sublane_shifts(dpad[pl.ds(t0, ROW_TILE + CONV_HALO), :]), wdw_ref, 0, True)
            _, vjp = jax.vjp(_glu, cv_ref[rows, :], cg_ref[rows, :])
            dcv, dcg = vjp(dglu)
            dcv_ref[rows, :] = dcv.astype(BF16)
            dcg_ref[rows, :] = dcg.astype(BF16)
            return carry

        lax.fori_loop(0, s // ROW_TILE, tile2, 0)

    vec = pl.BlockSpec((1, CW), lambda i: (0, 0))
    wspec = pl.BlockSpec((CONV_K + 1, CW), lambda i: (0, 0))
    vshape = jax.ShapeDtypeStruct((1, CW), F32)
    return pl.pallas_call(
        body, name="conv_bwd", grid=(2,),
        in_specs=[pl.BlockSpec((s, CW), lambda i: (0, (2 * GW + PW) // CW)),
                  pl.BlockSpec((s, CW), lambda i: (0, (2 * GW + PW) // CW + 1)),
                  pl.BlockSpec((s, CW), lambda i: (0, (GW + PW) // CW)), wspec, vec, vec, vec, ANY],
        out_specs=[pl.BlockSpec((s, CW), lambda i: (0, (2 * GW + PW) // CW + i)), wspec, vec, vec, vec],
        out_shape=[jax.ShapeDtypeStruct((s, IN_COLS), BF16), jax.ShapeDtypeStruct((CONV_K + 1, CW), F32),
                   vshape, vshape, vshape],
        input_output_aliases={7: 0},
        scratch_shapes=[pltpu.VMEM((s + CONV_HALO, CW), F32), pltpu.VMEM((s + CONV_HALO, CW), F32),
                        pltpu.VMEM((s, CW), BF16)],
        compiler_params=_cparams(),
    )(z, z, dy, wdw, bdw, lng, lnb, dz)


def _softmax_rows(sc):
    e = jnp.exp(sc - jnp.max(sc, axis=-1, keepdims=True))
    return e / jnp.sum(e, axis=-1, keepdims=True)


def _attn_fwd(q, k, v, tq):
    s, m = q.shape[0], k.shape[0]
    tq = min(tq, s)

    def body(q_ref, k_ref, v_ref, o_ref):
        for h in range(XH):
            cols = slice(h * XHD, (h + 1) * XHD)
            p = _softmax_rows(_dot(q_ref[:, cols], k_ref[:, cols], NT) * ATT_SCALE)
            o_ref[:, cols] = _dot(p.astype(BF16), v_ref[:, cols], NN).astype(BF16)

    kv = pl.BlockSpec((m, D), lambda i: (0, 0))
    return pl.pallas_call(
        body, name="attn_fwd", grid=(s // tq,),
        in_specs=[pl.BlockSpec((tq, D), lambda i: (i, 0)), kv, kv],
        out_specs=pl.BlockSpec((tq, D), lambda i: (i, 0)),
        out_shape=jax.ShapeDtypeStruct((s, D), BF16), compiler_params=_cparams(),
    )(q, k, v)


def _attn_bwd(q, k, v, do, tq, after=()):
    s, m = q.shape[0], k.shape[0]
    tq = min(tq, s)

    def body(q_ref, k_ref, v_ref, do_ref, *rest):
        dq_ref, dk_ref, dv_ref = rest[len(after):]

        @pl.when(pl.program_id(0) == 0)
        def _():
            dk_ref[...] = jnp.zeros_like(dk_ref)
            dv_ref[...] = jnp.zeros_like(dv_ref)

        for h in range(XH):
            cols = slice(h * XHD, (h + 1) * XHD)
            qh, kh, vh, doh = q_ref[:, cols], k_ref[:, cols], v_ref[:, cols], do_ref[:, cols]
            p = _softmax_rows(_dot(qh, kh, NT) * ATT_SCALE)
            dp = _dot(doh, vh, NT)
            dv_ref[:, cols] += _dot(p.astype(BF16), doh, TN)
            ds = (p * (dp - jnp.sum(p * dp, axis=-1, keepdims=True)) * ATT_SCALE).astype(BF16)
            dq_ref[:, cols] = _dot(ds, kh, NN).astype(BF16)
            dk_ref[:, cols] += _dot(ds, qh, TN)

    kv = pl.BlockSpec((m, D), lambda i: (0, 0))
    qs = pl.BlockSpec((tq, D), lambda i: (i, 0))
    return pl.pallas_call(
        body, name="attn_bwd", grid=(s // tq,),
        in_specs=[qs, kv, kv, qs] + [ANY] * len(after), out_specs=[qs, kv, kv],
        out_shape=[jax.ShapeDtypeStruct((s, D), BF16), jax.ShapeDtypeStruct((m, D), F32),
                   jax.ShapeDtypeStruct((m, D), F32)],
        compiler_params=_cparams(),
    )(q, k, v, do, *after)


def _loss_head(y, target, tm):
    s = y.shape[0]
    tm = min(tm, s)

    def body(y_ref, t_ref, dy_ref, part_ref):
        err = y_ref[...] - t_ref[...]
        dy_ref[...] = err * (1.0 / D)
        part_ref[...] = jnp.full((1, 8, LANES), 0.5 * jnp.sum(err * err) * (1.0 / D), F32)

    blk = pl.BlockSpec((tm, D), lambda i: (i, 0))
    return pl.pallas_call(
        body, name="loss_head", grid=(s // tm,), in_specs=[blk, blk],
        out_specs=[blk, pl.BlockSpec((1, 8, LANES), lambda i: (i, 0, 0))],
        out_shape=[jax.ShapeDtypeStruct((s, D), F32), jax.ShapeDtypeStruct((s // tm, 8, LANES), F32)],
        compiler_params=_cparams(),
    )(y, target)


def _layer_fwd(x0, mem, w, p, fetch):
    z, hn0 = _rowop_mm("mix_in", "rms", (x0,), p["norm_mix_pre"], w["w_in"], NT, F32)
    y = _gmlp_fwd(z, p["gmlp_v_gain"], p["w_spatial"], p["b_spatial_t"], 1024)
    y = _pool_fwd(z, p["w_pool"], p["s_pool"], y)
    y = _conv_fwd(z, p["w_dw"], p["b_dw"], p["conv_ln_g"], p["conv_ln_b"], y)
    w.update(fetch("out", (y,)))
    x1, h0 = _mm_rowop("mix_out", "rms_res", [(y, w["w_out"], NN)], (x0,), p["norm_mix_post"])
    w.update(fetch("att", (x1,)))
    q, hn1 = _rowop_mm("att_q", "rms", (x1,), p["norm_xattn_pre"], w["w_q"], NN, BF16)
    k, mn = _rowop_mm("att_k", "rms", (mem,), p["norm_mem"], w["w_k"], NN, BF16, after=(x1,))
    v, _ = _rowop_mm("att_v", "rms", (mem,), p["norm_mem"], w["w_v"], NN, BF16, after=(x1,))
    o = _attn_fwd(q, k, v, 1024)
    x2, h1 = _mm_rowop("att_o", "rms_res", [(o, w["w_o"], NN)], (x1,), p["norm_xattn_post"])
    w.update(fetch("up", (x2,)))
    u, hn2 = _rowop_mm("ffn_up", "rms", (x2,), p["norm_ffn_pre"], w["w_up"], NT, F32)
    w.update(fetch("down", (u,)))
    x3, h2 = _mm_rowop("ffn_down", "rms_res", [(u, w["w_down"], NN)], (x2,), p["norm_ffn_post"], relu2=True)
    saved = dict(x0=x0, z=z, hn0=hn0, y=y, h0=h0, x1=x1, q=q, hn1=hn1, k=k, v=v, mn=mn, o=o, h1=h1, x2=x2, u=u,
                 hn2=hn2, h2=h2)
    return x3, saved


def _layer_bwd(dx3, mem, w, p, sv, red):
    gs = {}
    du, dh2, dg = _rowop_mm("ffn_down_bwd", "rms_bwd", (sv["h2"], dx3), p["norm_ffn_post"], w["w_down"], NT, BF16,
                            u=sv["u"], after=red.after())
    gs["norm_ffn_post"] = jnp.sum(dg, axis=0)
    g_down = _mm_tn("ffn_down_dw", sv["u"], dh2, relu2=True)
    red.advance((g_down,))
    dx2, dg = _mm_rowop("ffn_up_bwd", "rms_bwd_res", [(du, w["w_up"], NN)], (sv["x2"], dx3), p["norm_ffn_pre"],
                        after=red.after())
    gs["norm_ffn_pre"] = jnp.sum(dg, axis=0)
    g_up = _mm_tn("ffn_up_dw", du, sv["hn2"])
    red.add("ffn", ("w_down", "w_up"), [g_down, g_up])
    do, dh1, dg = _rowop_mm("att_o_bwd", "rms_bwd", (sv["h1"], dx2), p["norm_xattn_post"], w["w_o"], NT, BF16,
                            after=red.after())
    gs["norm_xattn_post"] = jnp.sum(dg, axis=0)
    g_o = _mm_tn("att_o_dw", sv["o"], dh1)
    red.advance((g_o,))
    dq, dk, dv = _attn_bwd(sv["q"], sv["k"], sv["v"], do, 1024, after=red.after())
    dk, dv = dk.astype(BF16), dv.astype(BF16)
    dx1, dg = _mm_rowop("att_q_bwd", "rms_bwd_res", [(dq, w["w_q"], NT)], (sv["x1"], dx2), p["norm_xattn_pre"],
                        after=red.after())
    gs["norm_xattn_pre"] = jnp.sum(dg, axis=0)
    g_q = _mm_tn("att_q_dw", sv["hn1"], dq)
    g_k = _mm_tn("att_k_dw", sv["mn"], dk)
    g_v = _mm_tn("att_v_dw", sv["mn"], dv)
    (dg,) = _mm_rowop("att_kv_bwd", "rms_bwd_gain", [(dk, w["w_k"], NT), (dv, w["w_v"], NT)], (mem,), p["norm_mem"])
    gs["norm_mem"] = jnp.sum(dg, axis=0)
    red.add("att", ("w_o", "w_q", "w_k", "w_v"), [g_o, g_q, g_k, g_v])
    dy, dh0, dg = _rowop_mm("mix_out_bwd", "rms_bwd", (sv["h0"], dx1), p["norm_mix_post"], w["w_out"], NT, F32,
                            after=red.after())
    gs["norm_mix_post"] = jnp.sum(dg, axis=0)
    g_out = _mm_tn("mix_out_dw", sv["y"], dh0)
    red.advance((g_out,))
    red.add("out", ("w_out",), [g_out])
    z = sv["z"]
    dz, dgv, dws, dbs = _gmlp_bwd(z, dy, p["gmlp_v_gain"], p["w_spatial"], p["b_spatial_t"], 512, after=red.after())
    gs["gmlp_v_gain"] = jnp.sum(dgv, axis=0)
    gs["w_spatial"] = jnp.sum(dws, axis=0)
    gs["b_spatial"] = jnp.sum(dbs[..., 0], axis=0)
    dz, gs["w_pool"], gs["s_pool"] = _pool_bwd(z, dy, p["w_pool"], p["s_pool"], dz)
    dz, dwdw, gs["b_dw"], gs["conv_ln_g"], gs["conv_ln_b"] = _conv_bwd(
        z, dy, p["w_dw"], p["b_dw"], p["conv_ln_g"], p["conv_ln_b"], dz)
    red.advance((dz,))
    g_in = _mm_tn("mix_in_dw", dz, sv["hn0"], after=red.after())
    red.add("in", ("w_in",), [g_in])
    if red.layer == 0:
        red.advance(())
    red.small("mixer", _small_grad_arrays(gs, dwdw, norms=False))
    dx0, dg = _mm_rowop("mix_in_bwd", "rms_bwd_res", [(dz, w["w_in"], NN)], (sv["x0"], dx1), p["norm_mix_pre"],
                        after=red.after())
    gs["norm_mix_pre"] = jnp.sum(dg, axis=0)
    late = {"norms": jnp.concatenate([gs[n] for n in NORM_NAMES], axis=0)}
    if red.layer == 0:
        late["loss"] = red.extra[0]
    red.small("norms", late)
    return dx0


NORM_NAMES = ("norm_mix_pre", "norm_mix_post", "norm_xattn_pre", "norm_mem", "norm_xattn_post", "norm_ffn_pre",
              "norm_ffn_post")
VEC_NAMES = ("s_pool", "b_dw", "conv_ln_g", "conv_ln_b")
SMALL_ARRAYS = ("norms", "gain_bias", "w_spatial", "w_pool", "vecs", "w_dw")


def _small_grad_arrays(gs, dwdw, norms=True):
    out = {"norms": jnp.concatenate([gs[n] for n in NORM_NAMES], axis=0)} if norms else {}
    out.update({"gain_bias": jnp.concatenate([gs["gmlp_v_gain"], gs["b_spatial"]], axis=0),
                "w_spatial": gs["w_spatial"], "w_pool": gs["w_pool"],
                "vecs": jnp.concatenate([gs[n] for n in VEC_NAMES], axis=0), "w_dw": dwdw})
    return out


def _layer_params(small, l):
    p = {n: small[n][l].reshape(1, -1) for n in ("norm_mix_pre", "norm_mix_post", "s_pool", "b_dw", "conv_ln_g",
                                                   "conv_ln_b", "norm_xattn_pre", "norm_mem", "norm_xattn_post",
                                                   "norm_ffn_pre", "norm_ffn_post")}
    p["gmlp_v_gain"] = small["gmlp_v_gain"][l]
    p["w_spatial"] = small["w_spatial"][l]
    p["b_spatial_t"] = small["b_spatial"][l].T
    p["w_pool"] = small["w_pool"][l]
    p["w_dw"] = jnp.pad(small["w_dw"][l], ((0, 1), (0, 0)))
    return p


def _local_step(x, mem, target, fetch, small, red):
    small = dict(small)
    saved, weights, params = [], [], []
    h = x
    marker = ()
    for l in range(DEPTH):
        w = fetch(l, "in", marker)
        if "taps" in w:
            small["w_dw"] = w.pop("taps")
        p = _layer_params(small, l)
        h, sv = _layer_fwd(h, mem, w, p, functools.partial(fetch, l))
        marker = (h,)
        saved.append(sv)
        weights.append(w)
        params.append(p)
    dh, loss = _loss_head(h, target, 1024)
    red.extra = (loss,)
    for l in reversed(range(DEPTH)):
        red.layer = l
        dh = _layer_bwd(dh, mem, weights[l], params[l], saved[l], red)
    return loss, dh


HBM = pl.BlockSpec(memory_space=pltpu.HBM)


def _position():
    return lax.axis_index("x"), lax.axis_index("y"), lax.axis_index("c")


SEM = pl.BlockSpec(memory_space=pltpu.SEMAPHORE)
EFFECT = pltpu.SideEffectType.DATAFLOW_SIDE_EFFECTING
TOKEN = jax.ShapeDtypeStruct((8, LANES), F32)
TOKEN_SPEC = pl.BlockSpec(memory_space=pltpu.VMEM)


def _landing(shape, dtype):
    return pltpu.with_memory_space_constraint(lax.empty(shape, dtype), pltpu.HBM)


def _hbm_shapes(arrays):
    return [pltpu.HBM(a.shape, a.dtype) for a in arrays]


def _block(ref, r, dev):
    return ref.at[pl.ds((4 * dev[0] + 2 * dev[1] + dev[2]) * r, r), :]


def _split_call(name, body, thru, sems_in, after, sems_out, token):
    n = len(thru)
    out_shape = [pltpu.SemaphoreType.DMA(s) for s in sems_out] + _hbm_shapes(thru) + ([TOKEN] if token else [])
    out_specs = [SEM] * len(sems_out) + [HBM] * n + ([TOKEN_SPEC] if token else [])
    return pl.pallas_call(
        body, name=name, in_specs=[HBM] * n + [SEM] * len(sems_in) + [ANY] * len(after),
        out_specs=out_specs, out_shape=out_shape,
        input_output_aliases={i: len(sems_out) + i for i in range(n)},
        compiler_params=pltpu.CompilerParams(has_side_effects=EFFECT),
    )(*thru, *sems_in, *after)


def _place_own(name, srcs, dev, out_dtype, tr):
    n = len(srcs)
    r, cols = srcs[0][0].shape[-2:]
    tr = r if r < 16 else _row_tile(r, tr)
    nb = r // tr

    def body(dev_ref, *refs):
        for a in range(n):
            refs[n + a][...] = refs[a][...].astype(out_dtype)

    in_specs = [pl.BlockSpec((tr, cols), lambda i, d: (i, 0)) if l is None
                else pl.BlockSpec((None, tr, cols), lambda i, d, l=l: (l, i, 0)) for _, l in srcs]
    return pl.pallas_call(
        body, name=name,
        grid_spec=pltpu.PrefetchScalarGridSpec(
            num_scalar_prefetch=1, grid=(nb,), in_specs=in_specs,
            out_specs=[pl.BlockSpec((tr, cols), lambda i, d: (d[0] * nb + i, 0))] * n),
        out_shape=[jax.ShapeDtypeStruct((N_DEV * r, cols), out_dtype)] * n, compiler_params=_cparams(),
    )(dev, *[a for a, _ in srcs])


def _gather_peers(x, y, c):
    return [(1 - x, y, c), (x, 1 - y, c), (1 - x, 1 - y, c), (x, y, 1 - c)]


def _block_rows(land):
    return land.shape[0] // N_DEV


def _near_peers(x, y, c):
    return [(1 - x, y, c), (x, 1 - y, c), (x, y, 1 - c)]


def _relay_route(x, y, c):
    origin = (x + c * (1 - 2 * x), y + (1 - c) * (1 - 2 * y), c)
    target = (x + (1 - c) * (1 - 2 * x), y + c * (1 - 2 * y), c)
    return origin, target


def _same_block_copy(blk, send_sem, recv_sem, to):
    return pltpu.make_async_remote_copy(src_ref=blk, dst_ref=blk, send_sem=send_sem, recv_sem=recv_sem, device_id=to,
                                        device_id_type=MESH)


def _gather_start(name, lands, after):
    n = len(lands)

    def body(*refs):
        lz = refs[:n]
        send_sems, recv_sems = refs[n + len(after)], refs[n + len(after) + 1]
        token = refs[-1]
        x, y, c = _position()
        for a in range(n):
            own = _block(lz[a], _block_rows(lands[a]), (x, y, c))
            for k, to in enumerate(_near_peers(x, y, c)):
                _same_block_copy(own, send_sems.at[k], recv_sems.at[k], to).start()
        token[...] = jnp.zeros_like(token)

    out = _split_call(name, body, list(lands), [], after, [(3,), (3,)], True)
    return out[0], out[1], out[2:2 + n], out[-1]


def _gather_step(name, near, far, fresh, after):
    groups = [g for g in (near and near[0], far and far[0], fresh) if g]
    counts = [len(near[0]) if near else 0, len(far[0]) if far else 0, len(fresh) if fresh else 0]
    n = sum(counts)
    sems_in = ([near[1]] if near else []) + ([far[1]] if far else [])
    sems_out = ([(2,), (2,), (1,), (1,)] if near else []) + ([(1,), (1,)] if far else []) + ([(3,), (3,)] if fresh else [])

    def body(*refs):
        lz = list(refs[:n])
        ins = list(refs[n:n + len(sems_in)])
        outs = list(refs[n + len(sems_in) + len(after):n + len(sems_in) + len(after) + len(sems_out)])
        token = refs[-1]
        x, y, c = _position()
        me, sibling = (x, y, c), (x, y, 1 - c)
        near_lz, far_lz, fresh_lz = (lz[sum(counts[:i]):sum(counts[:i + 1])] for i in range(3))
        neighbours = _near_peers(x, y, c)[:2]
        origin, target = _relay_route(x, y, c)
        diagonal = (1 - x, 1 - y, c)
        if near:
            recv0 = ins.pop(0)
            fsend, frecv, rsend, rrecv = (outs.pop(0) for _ in range(4))
            for a, land in enumerate(near[0]):
                for j, chip in enumerate(neighbours):
                    _same_block_copy(_block(near_lz[a], _block_rows(land), chip), fsend.at[j], recv0.at[j], me).wait_recv()
        if far:
            rrecv_in = ins.pop(0)
            f2send, f2recv = outs.pop(0), outs.pop(0)
            for a, land in enumerate(far[0]):
                _same_block_copy(_block(far_lz[a], _block_rows(land), diagonal), f2send.at[0], rrecv_in.at[0], me).wait_recv()
            for a, land in enumerate(far[0]):
                _same_block_copy(_block(far_lz[a], _block_rows(land), diagonal), f2send.at[0], f2recv.at[0], sibling).start()
        if near:
            for a, land in enumerate(near[0]):
                r = _block_rows(land)
                _same_block_copy(_block(near_lz[a], r, origin), rsend.at[0], rrecv.at[0], target).start()
                for j, chip in enumerate(neighbours):
                    _same_block_copy(_block(near_lz[a], r, chip), fsend.at[j], frecv.at[j], sibling).start()
        if fresh:
            send_sems, recv_sems = outs.pop(0), outs.pop(0)
            for a, land in enumerate(fresh):
                own = _block(fresh_lz[a], _block_rows(land), me)
                for k, to in enumerate(_near_peers(x, y, c)):
                    _same_block_copy(own, send_sems.at[k], recv_sems.at[k], to).start()
        token[...] = jnp.zeros_like(token)

    out = list(_split_call(name, body, [l for g in groups for l in g], sems_in, after, sems_out, True))
    res = {"token": out.pop()}
    if near:
        res.update(fsend=out.pop(0), frecv=out.pop(0), rsend=out.pop(0), rrecv=out.pop(0))
    if far:
        res.update(f2send=out.pop(0), f2recv=out.pop(0))
    if fresh:
        res.update(send=out.pop(0), recv=out.pop(0))
    res["near"], res["far"], res["fresh"] = (out[sum(counts[:i]):sum(counts[:i + 1])] for i in range(3))
    return res


def _gather_finish(name, lands, send_sems, recv_sems, fsend, frecv, rsend, f2send, f2recv, after):
    n = len(lands)

    def body(*refs):
        lz = refs[:n]
        send0, recv0, fsend_ref, frecv_ref, rsend_ref, f2send_ref, f2recv_ref = refs[n:n + 7]
        x, y, c = _position()
        me = (x, y, c)
        near = _near_peers(x, y, c)[:2]
        origin, _ = _relay_route(x, y, c)
        for a in range(n):
            r = _block_rows(lands[a])
            sib = _block(lz[a], r, (x, y, 1 - c))
            _same_block_copy(sib, send0.at[2], recv0.at[2], me).wait_recv()
            for j, chip in enumerate(near):
                blk = _block(lz[a], r, (chip[0], chip[1], 1 - c))
                _same_block_copy(blk, fsend_ref.at[j], frecv_ref.at[j], me).wait_recv()
            far = _block(lz[a], r, (1 - x, 1 - y, 1 - c))
            _same_block_copy(far, f2send_ref.at[0], f2recv_ref.at[0], me).wait_recv()
            own = _block(lz[a], r, me)
            for k in range(3):
                _same_block_copy(own, send0.at[k], recv0.at[k], me).wait_send()
            for j, chip in enumerate(near):
                _same_block_copy(_block(lz[a], r, chip), fsend_ref.at[j], frecv_ref.at[j], me).wait_send()
            _same_block_copy(_block(lz[a], r, origin), rsend_ref.at[0], recv0.at[0], me).wait_send()
            _same_block_copy(_block(lz[a], r, (1 - x, 1 - y, c)), f2send_ref.at[0], f2recv_ref.at[0], me).wait_send()

    return _split_call(name, body, list(lands), [send_sems, recv_sems, fsend, frecv, rsend, f2send, f2recv], after, [],
                       False)


def _sibling_start(name, grads, after):
    n = len(grads)
    lands = [_landing((4, g.shape[0] // N_DEV, D), g.dtype) for g in grads]

    def body(*refs):
        ins, lz = refs[:n], refs[n:2 * n]
        send_sem, recv_sem = refs[2 * n + len(after)], refs[2 * n + len(after) + 1]
        token = refs[-1]
        x, y, c = _position()
        for a in range(n):
            r = grads[a].shape[0] // N_DEV
            for q in range(4):
                pltpu.make_async_remote_copy(
                    src_ref=ins[a].at[pl.ds((2 * q + 1 - c) * r, r), :], dst_ref=lz[a].at[q], send_sem=send_sem.at[0],
                    recv_sem=recv_sem.at[0], device_id=(x, y, 1 - c), device_id_type=MESH).start()
        token[...] = jnp.zeros_like(token)

    out = _split_call(name, body, list(grads) + lands, [], after, [(1,), (1,)], True)
    return out[0], out[1], out[2:2 + n], out[2 + n:2 + 2 * n], out[-1]


def _sibling_finish(name, grads, lands, send_sem, recv_sem, after):
    n = len(grads)

    def body(*refs):
        ins, lz = refs[:n], refs[n:2 * n]
        send_ref, recv_ref = refs[2 * n], refs[2 * n + 1]
        x, y, c = _position()
        for a in range(n):
            r = grads[a].shape[0] // N_DEV
            for q in range(4):
                cp = pltpu.make_async_remote_copy(
                    src_ref=ins[a].at[pl.ds((2 * q + 1 - c) * r, r), :], dst_ref=lz[a].at[q], send_sem=send_ref.at[0],
                    recv_sem=recv_ref.at[0], device_id=(x, y, c), device_id_type=MESH)
                cp.wait_send()
                cp.wait_recv()

    out = _split_call(name, body, list(grads) + list(lands), [send_sem, recv_sem], after, [], False)
    return out[:n], out[n:2 * n]


def _chip_start(name, parts, after):
    n = len(parts)
    lands = [_landing((3,) + p.shape[1:], p.dtype) for p in parts]

    def body(*refs):
        ins, lz = refs[:n], refs[n:2 * n]
        send_sems, recv_sems = refs[2 * n + len(after)], refs[2 * n + len(after) + 1]
        token = refs[-1]
        x, y, c = _position()
        for a in range(n):
            for j, chip in enumerate(_gather_peers(x, y, c)[:3]):
                pltpu.make_async_remote_copy(
                    src_ref=ins[a].at[2 * chip[0] + chip[1]], dst_ref=lz[a].at[j], send_sem=send_sems.at[j],
                    recv_sem=recv_sems.at[j], device_id=chip, device_id_type=MESH).start()
        token[...] = jnp.zeros_like(token)

    out = _split_call(name, body, list(parts) + lands, [], after, [(3,), (3,)], True)
    return out[0], out[1], out[2:2 + n], out[2 + n:2 + 2 * n], out[-1]


def _chip_finish(name, parts, lands, send_sems, recv_sems, after):
    n = len(parts)

    def body(*refs):
        ins, lz = refs[:n], refs[n:2 * n]
        send_ref, recv_ref = refs[2 * n], refs[2 * n + 1]
        me = _position()
        for a in range(n):
            for j in range(3):
                cp = pltpu.make_async_remote_copy(
                    src_ref=ins[a].at[j], dst_ref=lz[a].at[j], send_sem=send_ref.at[j], recv_sem=recv_ref.at[j],
                    device_id=me, device_id_type=MESH)
                cp.wait_send()
                cp.wait_recv()

    out = _split_call(name, body, list(parts) + list(lands), [send_sems, recv_sems], after, [], False)
    return out[:n], out[n:2 * n]


def _other_devices(x, y, c):
    return [(x + (k >> 2 & 1) * (1 - 2 * x), y + (k >> 1 & 1) * (1 - 2 * y), c + (k & 1) * (1 - 2 * c))
            for k in range(1, N_DEV)]


def _broadcast_start(name, arrays, after):
    n = len(arrays)
    lands = [_landing((N_DEV,) + a.shape, a.dtype) for a in arrays]

    def body(*refs):
        ins, lz = refs[:n], refs[n:2 * n]
        send_sems, recv_sems = refs[2 * n + len(after)], refs[2 * n + len(after) + 1]
        token = refs[-1]
        x, y, c = _position()
        for a in range(n):
            for k, peer in enumerate(_other_devices(x, y, c)):
                pltpu.make_async_remote_copy(
                    src_ref=ins[a], dst_ref=lz[a].at[4 * x + 2 * y + c], send_sem=send_sems.at[k],
                    recv_sem=recv_sems.at[k], device_id=peer, device_id_type=MESH).start()
        token[...] = jnp.zeros_like(token)

    out = _split_call(name, body, list(arrays) + lands, [], after, [(N_DEV - 1,), (N_DEV - 1,)], True)
    return out[0], out[1], out[2:2 + n], out[2 + n:2 + 2 * n], out[-1]


def _broadcast_finish(name, arrays, lands, send_sems, recv_sems, after):
    n = len(arrays)

    def body(*refs):
        ins, lz = refs[:n], refs[n:2 * n]
        send_ref, recv_ref = refs[2 * n], refs[2 * n + 1]
        x, y, c = _position()
        for a in range(n):
            for k, peer in enumerate(_other_devices(x, y, c)):
                cp = pltpu.make_async_remote_copy(
                    src_ref=ins[a], dst_ref=lz[a].at[4 * peer[0] + 2 * peer[1] + peer[2]], send_sem=send_ref.at[k],
                    recv_sem=recv_ref.at[k], device_id=(x, y, c), device_id_type=MESH)
                cp.wait_send()
                cp.wait_recv()

    out = _split_call(name, body, list(arrays) + list(lands), [send_sems, recv_sems], after, [], False)
    return out[:n], out[n:2 * n]


def _row_tile(r, target):
    return max(t for t in range(16, min(r, target) + 1, 16) if r % t == 0)


CHIP_PARTIAL_BYTES = 12 * 1024 * 1024


def _chip_partial(name, grads, gots, c):
    n = len(grads)
    r = grads[0].shape[0] // N_DEV
    tr = _row_tile(r, CHIP_PARTIAL_BYTES // (n * 3 * D * 2))

    def body(c_ref, *refs):
        for a in range(n):
            refs[2 * n + a][...] = (refs[a][...].astype(F32) + refs[n + a][...].astype(F32)).astype(BF16)

    blk = pl.BlockSpec((None, tr, D), lambda q, i, c_ref: (q, i, 0))
    return pl.pallas_call(
        body, name=name,
        grid_spec=pltpu.PrefetchScalarGridSpec(
            num_scalar_prefetch=1, grid=(4, r // tr),
            in_specs=[pl.BlockSpec((None, None, tr, D), lambda q, i, c_ref: (q, c_ref[0], i, 0))] * n + [blk] * n,
            out_specs=[blk] * n),
        out_shape=[jax.ShapeDtypeStruct((4, r, D), BF16)] * n, compiler_params=_cparams(),
    )(c, *[g.reshape(4, 2, r, D) for g in grads], *gots)


class _WeightGather:
    def __init__(self, groups):
        self.groups = list(groups)
        self.index = {key: i for i, (key, _, _) in enumerate(groups)}
        self.state = [None] * len(groups)
        self.token = ()
        for i in range(min(2, len(groups))):
            self._start(i)

    def _tag(self, i):
        return "%s_%d" % self.groups[i][0][::-1]

    def _start(self, i):
        send, recv, lz, tok = _gather_start("gather_start_" + self._tag(i), self.groups[i][2], self.token)
        self.state[i] = dict(send=send, recv=recv, lands=lz)
        self.token = (tok,)

    def _step(self, name, near, far, fresh, marker):
        exists = lambda i: i is not None and i < len(self.groups)
        near, far, fresh = (i if exists(i) else None for i in (near, far, fresh))
        res = _gather_step(
            name, None if near is None else (self.state[near]["lands"], self.state[near]["recv"]),
            None if far is None else (self.state[far]["lands"], self.state[far]["rrecv"]),
            None if fresh is None else self.groups[fresh][2], tuple(marker) + self.token)
        self.token = (res["token"],)
        if near is not None:
            self.state[near].update(lands=res["near"], fsend=res["fsend"], frecv=res["frecv"], rsend=res["rsend"],
                                    rrecv=res["rrecv"])
        if far is not None:
            self.state[far].update(lands=res["far"], f2send=res["f2send"], f2recv=res["f2recv"])
        if fresh is not None:
            self.state[fresh] = dict(send=res["send"], recv=res["recv"], lands=res["fresh"])

    def fetch(self, layer, group, marker):
        k = self.index[(layer, group)]
        if k == 0:
            self._step("gather_step_first", 0, None, None, marker)
        self._step("gather_step_" + self._tag(k), k + 1, k, k + 2, marker)
        st = self.state[k]
        lz = _gather_finish("gather_finish_" + self._tag(k), st["lands"], st["send"], st["recv"], st["fsend"],
                            st["frecv"], st["rsend"], st["f2send"], st["f2recv"], self.token)
        self.state[k] = None
        return dict(zip(self.groups[k][1], lz))


class _GradReduce:
    def __init__(self, core, chip):
        self.core, self.chip = core, chip
        self.layer = None
        self.token = ()
        self.at_sibling, self.at_chips = [], []
        self.extra, self.smalls = (), {}

    def after(self):
        return self.token

    def add(self, group, names, grads):
        tag = "%s_%d" % (group, self.layer)
        send, recv, grads, lands, tok = _sibling_start("grad_sibling_start_" + tag, grads, self.token)
        self.at_sibling.append((tag, [(self.layer, n) for n in names], send, recv, grads, lands))
        self.token = (tok,)

    def advance(self, marker):
        for tag, keys, send, recv, grads, lands in self.at_sibling:
            grads, lands = _sibling_finish("grad_sibling_finish_" + tag, grads, lands, send, recv, marker)
            parts = _chip_partial("chip_partial_" + tag, grads, lands, self.core)
            send, recv, parts, lands, tok = _chip_start("grad_chip_start_" + tag, parts, ())
            self.at_chips.append([tag, keys, send, recv, parts, lands])
            self.token = (tok,)
        self.at_sibling = []

    def small(self, part, arrays):
        keys = list(arrays)
        send, recv, own, slots, tok = _broadcast_start(
            "small_grads_start_%d_%s" % (self.layer, part), [arrays[k] for k in keys], self.token)
        self.smalls.setdefault(self.layer, []).append((part, keys, send, recv, own, slots))
        self.token = (tok,)

    def small_finish(self, layer, marker):
        mine, theirs = {}, {}
        for part, keys, send, recv, own, slots in self.smalls[layer]:
            own, slots = _broadcast_finish("small_grads_finish_%d_%s" % (layer, part), own, slots, send, recv, marker)
            mine.update(zip(keys, own))
            theirs.update(zip(keys, slots))
        return mine, theirs

    def collect(self, key, marker):
        for entry in self.at_chips:
            tag, keys, send, recv, parts, lands = entry
            if key in keys:
                if send is not None:
                    parts, lands = _chip_finish("grad_chip_finish_" + tag, parts, lands, send, recv, marker)
                    entry[2:] = [None, None, parts, lands]
                i = keys.index(key)
                return parts[i], lands[i]
        raise KeyError(key)


def _adamw_math(w, g, m, v):
    m = ADAM_B1 * m + (1.0 - ADAM_B1) * g
    v = ADAM_B2 * v + (1.0 - ADAM_B2) * jnp.square(g)
    m_hat = m / (1.0 - ADAM_B1 ** ADAM_STEP)
    v_hat = v / (1.0 - ADAM_B2 ** ADAM_STEP)
    delta = -ADAM_LR * (m_hat / (jnp.sqrt(v_hat) + ADAM_EPS) + ADAM_WD * w)
    return delta, m, v


def _adamw_small(wts, mom_m, mom_v, own, gathered, loss_own, loss_gathered, dev):
    names = SMALL
    nw = len(names)
    na = len(SMALL_ARRAYS)

    def body(dev_ref, *refs):
        w_refs, m_refs, v_refs = (dict(zip(names, refs[i * nw:(i + 1) * nw])) for i in range(3))
        own_refs = refs[3 * nw:3 * nw + DEPTH * na]
        g_refs = refs[3 * nw + DEPTH * na:3 * nw + 2 * DEPTH * na]
        loss_own_ref, loss_got_ref = refs[3 * nw + 2 * DEPTH * na:3 * nw + 2 * DEPTH * na + 2]
        outs = refs[3 * nw + 2 * DEPTH * na + 2:]
        g_out, d_out, m_out, v_out = (dict(zip(names, outs[i * nw:(i + 1) * nw])) for i in range(4))
        me = dev_ref[0]

        loss = None
        for d in range(N_DEV):
            for b in range(loss_own.shape[0]):
                term = jnp.where(me == d, loss_own_ref[b], loss_got_ref[d, b])
                loss = term if loss is None else loss + term
        outs[4 * nw][...] = loss

        def update(name, at, g):
            g_out[name][at] = g
            d_out[name][at], m_out[name][at], v_out[name][at] = _adamw_math(
                w_refs[name][at], g, m_refs[name][at], v_refs[name][at])

        for l in range(DEPTH):
            mine = dict(zip(SMALL_ARRAYS, own_refs[l * na:(l + 1) * na]))
            got = dict(zip(SMALL_ARRAYS, g_refs[l * na:(l + 1) * na]))

            def total(key, at):
                acc = None
                for d in range(N_DEV):
                    term = jnp.where(me == d, mine[key][at] if at else mine[key][...], got[key][(d,) + at])
                    acc = term if acc is None else acc + term
                return acc

            row = (slice(l, l + 1),)
            for k, name in enumerate(NORM_NAMES):
                update(name, row, total("norms", (slice(k, k + 1),)))
            for k, name in enumerate(VEC_NAMES):
                update(name, row, total("vecs", (slice(k, k + 1),)))
            update("gmlp_v_gain", (l,), total("gain_bias", (slice(0, NH),)))
            update("b_spatial", (l,), total("gain_bias", (slice(NH, 2 * NH),)))
            update("w_spatial", (l,), total("w_spatial", ()))
            update("w_pool", (l,), total("w_pool", ()))
            update("w_dw", (l,), total("w_dw", (slice(0, CONV_K),)))

    args = [src[n] for src in (wts, mom_m, mom_v) for n in names]
    args += [src[l][k] for src in (own, gathered) for l in range(DEPTH) for k in SMALL_ARRAYS]
    args += [loss_own, loss_gathered]
    outs = pl.pallas_call(
        body, name="adamw_small",
        in_specs=[pl.BlockSpec(memory_space=pltpu.SMEM)] + [pl.BlockSpec(memory_space=pltpu.VMEM)] * len(args),
        out_shape=[jax.ShapeDtypeStruct(wts[n].shape, F32) for _ in range(4) for n in names]
        + [jax.ShapeDtypeStruct((8, LANES), F32)],
        compiler_params=_cparams(),
    )(dev, *args)
    return tuple(dict(zip(names, outs[i * nw:(i + 1) * nw])) for i in range(4)) + (outs[4 * nw],)


def _adamw_layers(name, w, reduced, m, v, chip, tr, transposed=False, after=()):
    nl, r, cdim = w.shape
    tr = _row_tile(r, tr)
    nb = r // tr

    def body(q_ref, w_ref, p0_ref, g0_ref, p1_ref, g1_ref, m_ref, v_ref, *rest):
        g_ref, d_ref, nm_ref, nv_ref = rest[len(after):]

        def total(p_ref, got_ref):
            acc = p_ref[...].astype(F32)
            for j in range(3):
                acc = acc + got_ref[j].astype(F32)
            return acc

        g = jnp.where(pl.program_id(0) == 0, total(p0_ref, g0_ref), total(p1_ref, g1_ref))
        if transposed:
            g = g.T
        g_ref[...] = g
        d_ref[...], nm_ref[...], nv_ref[...] = _adamw_math(w_ref[...], g, m_ref[...], v_ref[...])

    blk = pl.BlockSpec((None, tr, cdim), lambda l, i, q: (l, i, 0))
    first = lambda l, i: i * (1 - l) + (nb - 1) * l
    second = lambda l, i: i * l
    if transposed:
        gshape = (cdim, tr)
        at = lambda lead, i: (lead, 0, i)
    else:
        gshape = (tr, cdim)
        at = lambda lead, i: (lead, i, 0)
    specs = [blk,
             pl.BlockSpec((None,) + gshape, lambda l, i, q: at(q[0], first(l, i))),
             pl.BlockSpec((3,) + gshape, lambda l, i, q: at(0, first(l, i))),
             pl.BlockSpec((None,) + gshape, lambda l, i, q: at(q[0], second(l, i))),
             pl.BlockSpec((3,) + gshape, lambda l, i, q: at(0, second(l, i))), blk, blk] + [ANY] * len(after)
    shape = jax.ShapeDtypeStruct((nl, r, cdim), F32)
    return pl.pallas_call(
        body, name=name,
        grid_spec=pltpu.PrefetchScalarGridSpec(num_scalar_prefetch=1, grid=(nl, nb), in_specs=specs, out_specs=[blk] * 4),
        out_shape=[shape] * 4, compiler_params=_cparams(),
    )(chip, w, *reduced[0], *reduced[1], m, v, *after)


def _to_rows(name, a):
    return jnp.swapaxes(a, 1, 2) if name == "w_in" else a


def _place_own_transposed(name, srcs, dev, out_dtype, tc):
    n = len(srcs)
    kdim, cdim = srcs[0][0].shape[-2:]

    def body(dev_ref, *refs):
        for a in range(n):
            refs[n + a][...] = refs[a][...].T.astype(out_dtype)

    return pl.pallas_call(
        body, name=name,
        grid_spec=pltpu.PrefetchScalarGridSpec(
            num_scalar_prefetch=1, grid=(kdim // tc,),
            in_specs=[pl.BlockSpec((None, tc, cdim), lambda i, d, l=l: (l, i, 0)) for _, l in srcs],
            out_specs=[pl.BlockSpec((cdim, tc), lambda i, d: (d[0], i))] * n),
        out_shape=[jax.ShapeDtypeStruct((N_DEV * cdim, kdim), out_dtype)] * n, compiler_params=_cparams(),
    )(dev, *[a for a, _ in srcs])


def _pack(arrays, rows):
    flat = jnp.concatenate([a.reshape(-1) for a in arrays])
    return jnp.pad(flat, (0, rows * D - flat.shape[0])).reshape(rows, D)


def _rows_for(shapes, mult=8):
    total = 0
    for shp in shapes:
        size = 1
        for dim in shp:
            size *= dim
        total += size
    return -(-total // (mult * D)) * mult


def kernel(x, mem, norm_mix_pre, norm_mix_post, w_in, w_out, gmlp_v_gain, w_spatial, b_spatial, w_pool, s_pool, w_dw, b_dw, conv_ln_g, conv_ln_b, norm_xattn_pre, norm_mem, norm_xattn_post, w_q, w_k, w_v, w_o, norm_ffn_pre, norm_ffn_post, w_up, w_down, loss_target, m_norm_mix_pre, m_norm_mix_post, m_w_in, m_w_out, m_gmlp_v_gain, m_w_spatial, m_b_spatial, m_w_pool, m_s_pool, m_w_dw, m_b_dw, m_conv_ln_g, m_conv_ln_b, m_norm_xattn_pre, m_norm_mem, m_norm_xattn_post, m_w_q, m_w_k, m_w_v, m_w_o, m_norm_ffn_pre, m_norm_ffn_post, m_w_up, m_w_down, v_norm_mix_pre, v_norm_mix_post, v_w_in, v_w_out, v_gmlp_v_gain, v_w_spatial, v_b_spatial, v_w_pool, v_s_pool, v_w_dw, v_b_dw, v_conv_ln_g, v_conv_ln_b, v_norm_xattn_pre, v_norm_mem, v_norm_xattn_post, v_w_q, v_w_k, v_w_v, v_w_o, v_norm_ffn_pre, v_norm_ffn_post, v_w_up, v_w_down):
    args = dict(locals())
    wts = {n: args[n] for n in WEIGHTS}
    mom_m = {n: args["m_" + n] for n in WEIGHTS}
    mom_v = {n: args["v_" + n] for n in WEIGHTS}
    xi, yi, ci = _position()
    me = 4 * xi + 2 * yi + ci

    dev = jnp.reshape(me, (1,)).astype(jnp.int32)
    lands = {}
    for call, names, tr in (("place_in", ("w_in",), 256), ("place_att", ("w_out", "w_q", "w_k", "w_v", "w_o"), 64),
                            ("place_up", ("w_up",), 256), ("place_down", ("w_down",), 256)):
        srcs = [(_to_rows(n, wts[n]), l) for l in range(DEPTH) for n in names]
        placed = (_place_own_transposed if names == ("w_up",) else _place_own)(call, srcs, dev, BF16, tr)
        lands.update(zip([(l, n) for l in range(DEPTH) for n in names], placed))
    (lands[(0, "taps")],) = _place_own("place_taps", [(_pack([w_dw], _rows_for([w_dw.shape])), None)], dev, F32, 8)
    groups = []
    for l in range(DEPTH):
        for group, names in GATHER_GROUPS:
            if (l, group) == (0, "in"):
                names = names + ("taps",)
            groups.append(((l, group), names, [lands[(l, n)] for n in names]))
    gather = _WeightGather(groups)

    def fetch(layer, group, marker):
        w = gather.fetch(layer, group, marker)
        if "taps" in w:
            blocks = w["taps"].reshape(N_DEV, -1)[:, :w_dw.size].reshape((N_DEV,) + w_dw.shape)
            w["taps"] = jnp.moveaxis(blocks, 0, 2).reshape(DEPTH, CONV_K, CW)
        return w

    reduce = _GradReduce(jnp.reshape(ci, (1,)).astype(jnp.int32), jnp.reshape(2 * xi + yi, (1,)).astype(jnp.int32))
    small = {n: wts[n] for n in SMALL if n != "w_dw"}
    _, dx = _local_step(x[0], mem[0], loss_target[0], fetch, small, reduce)
    reduce.advance((dx,))

    grad_w, delta, new_m, new_v = {}, {}, {}, {}
    marker = (dx,) + tuple(reduce.after())
    for n in UPDATE_ORDER:
        reduced = [reduce.collect((l, n), marker) for l in range(DEPTH)]
        outs = _adamw_layers("adamw_" + n, _to_rows(n, wts[n]), reduced, _to_rows(n, mom_m[n]), _to_rows(n, mom_v[n]),
                             reduce.chip, 256, transposed=n == "w_up", after=marker)
        grad_w[n], delta[n], new_m[n], new_v[n] = (_to_rows(n, o) for o in outs)
        marker = (outs[1],)

    own, slots = [None] * DEPTH, [None] * DEPTH
    for l in reversed(range(DEPTH)):
        own[l], slots[l] = reduce.small_finish(l, marker)
        if l == 0:
            loss_own, loss_slots = own[l].pop("loss"), slots[l].pop("loss")
    shard_cols = CW // N_DEV
    for l in range(DEPTH):
        own[l]["w_dw"] = lax.dynamic_slice_in_dim(own[l]["w_dw"], me * shard_cols, shard_cols, axis=1)
        slots[l]["w_dw"] = lax.dynamic_slice_in_dim(slots[l]["w_dw"], me * shard_cols, shard_cols, axis=2)
    *small_out, loss_tile = _adamw_small(wts, mom_m, mom_v, own, slots, loss_own, loss_slots, dev)
    for dst, src in zip((grad_w, delta, new_m, new_v), small_out):
        dst.update(src)

    return (loss_tile[0, 0], dx[None], *[grad_w[n] for n in WEIGHTS], *[delta[n] for n in WEIGHTS],
            *[new_m[n] for n in WEIGHTS], *[new_v[n] for n in WEIGHTS])
```

```python
import functools

import jax
import jax.numpy as jnp
from jax import lax
from jax.experimental import pallas as pl
from jax.experimental.pallas import tpu as pltpu

F32 = jnp.float32
BF16 = jnp.bfloat16

D = 2048
GW = 1024
PW = 512
CW = 512
HD = 128
NH = 8
NG = 4
POOL_WINDOWS = (2, 4, 8, 16)
CONV_K = 31
IN_COLS = 2 * GW + PW + 2 * CW
XH = 4
XHD = D // XH
ATT_SCALE = XHD ** -0.5
RMS_EPS = 1e-6
LN_EPS = 1e-5
DEPTH = 2
N_DEV = 8

ADAM_LR = 0.001
ADAM_B1 = 0.9
ADAM_B2 = 0.999
ADAM_EPS = 1e-08
ADAM_WD = 0.01
ADAM_STEP = 10

LANES = 128
CONV_HALO = 32
POOL_HALO = 16
ROW_TILE = 128
VMEM_LIMIT = 60 * 1024 * 1024

MESH = pl.DeviceIdType.MESH
NT = (((1,), (1,)), ((), ()))
NN = (((1,), (0,)), ((), ()))
TN = (((0,), (0,)), ((), ()))

UPDATE_ORDER = ("w_down", "w_up", "w_o", "w_q", "w_k", "w_v", "w_out", "w_in")
GATHER_GROUPS = (("in", ("w_in",)), ("out", ("w_out",)), ("att", ("w_q", "w_k", "w_v", "w_o")), ("up", ("w_up",)),
                 ("down", ("w_down",)))
SMALL = ("norm_mix_pre", "norm_mix_post", "gmlp_v_gain", "w_spatial", "b_spatial", "w_pool", "s_pool",
         "w_dw", "b_dw", "conv_ln_g", "conv_ln_b", "norm_xattn_pre", "norm_mem", "norm_xattn_post",
         "norm_ffn_pre", "norm_ffn_post")
WEIGHTS = ("norm_mix_pre", "norm_mix_post", "w_in", "w_out", "gmlp_v_gain", "w_spatial", "b_spatial", "w_pool",
           "s_pool", "w_dw", "b_dw", "conv_ln_g", "conv_ln_b", "norm_xattn_pre", "norm_mem", "norm_xattn_post",
           "w_q", "w_k", "w_v", "w_o", "norm_ffn_pre", "norm_ffn_post", "w_up", "w_down")


def _cparams():
    return pltpu.CompilerParams(vmem_limit_bytes=VMEM_LIMIT)


def _dot(a, b, dims):
    return lax.dot_general(a, b, dims, preferred_element_type=F32)


def _rms(x, g):
    y = x * lax.rsqrt(jnp.mean(x * x, axis=-1, keepdims=True) + RMS_EPS)
    return y * g


def _rms_bwd(x, g, dy):
    r = lax.rsqrt(jnp.mean(x * x, axis=-1, keepdims=True) + RMS_EPS)
    xh = x * r
    t = dy * g
    dx = r * (t - xh * jnp.mean(t * xh, axis=-1, keepdims=True))
    return dx, jnp.sum(dy * xh, axis=0, keepdims=True)


def _gelu(x):
    cdf = 0.5 * (1.0 + jnp.tanh(0.7978845608028654 * (x + 0.044715 * (x * x * x))))
    return x * cdf


def _layer_norm(x, g, b=None):
    mu = jnp.mean(x, axis=-1, keepdims=True)
    xc = x - mu
    var = jnp.mean(xc * xc, axis=-1, keepdims=True)
    y = xc * lax.rsqrt(var + LN_EPS) * g
    return y if b is None else y + b


def _sigmoid(x):
    return 1.0 / (1.0 + jnp.exp(-x))


def _gmlp_rows(zu, zv, gv):
    return _gelu(zu), _layer_norm(_gelu(zv), gv)


def _glu(cv, cg):
    return cv * _sigmoid(cg)


def _ln_silu(h, g, b):
    y = _layer_norm(h, g, b)
    return y * _sigmoid(y)


ANY = pl.BlockSpec(memory_space=pl.ANY)


ROWS_TILE = 256
COLS_TILE = 512
DW_TILE = 512
RESIDENT_K = 2048
RESIDENT_ROWS = 512
STREAM_K_TILE = 1024
STREAM_ROWS = 512


def _k_tiles(kdim):
    if kdim <= RESIDENT_K:
        return RESIDENT_ROWS, kdim
    return STREAM_ROWS, max(t for t in range(LANES, STREAM_K_TILE + 1, LANES) if kdim % t == 0)


def _rowop_mm(name, kind, rows, g, w, dims, out_dtype, u=None, after=()):
    s = rows[0].shape[0]
    n = w.shape[0] if dims == NT else w.shape[1]
    resident = n <= RESIDENT_K and u is None
    tm, tn = min(RESIDENT_ROWS if resident else ROWS_TILE, s), min(COLS_TILE, n)
    ni, nj = s // tm, n // tn
    bwd = kind == "rms_bwd"
    out_shape = [jax.ShapeDtypeStruct((s, n), out_dtype), jax.ShapeDtypeStruct((s, D), BF16)]
    if bwd:
        out_shape.append(jax.ShapeDtypeStruct((ni, 1, D), F32))

    if resident:
        def row_body(*refs):
            refs = list(refs)
            row_refs = [refs.pop(0) for _ in rows]
            g_ref, w_ref = refs.pop(0), refs.pop(0)
            del refs[:len(after)]
            if bwd:
                a, dg = _rms_bwd(row_refs[0][...], g_ref[...], row_refs[1][...])
                refs[2][0] = dg
            else:
                a = _rms(row_refs[0][...], g_ref[...])
            a = a.astype(BF16)
            refs[1][...] = a
            refs[0][...] = _dot(a, w_ref[...], dims).astype(out_dtype)

        blk = pl.BlockSpec((tm, D), lambda i: (i, 0))
        return pl.pallas_call(
            row_body, name=name, grid=(ni,),
            in_specs=[blk] * len(rows) + [pl.BlockSpec((1, D), lambda i: (0, 0)),
                                          pl.BlockSpec(w.shape, lambda i: (0, 0), pipeline_mode=pl.Buffered(1))]
            + [ANY] * len(after),
            out_specs=[pl.BlockSpec((tm, n), lambda i: (i, 0)), blk]
            + ([pl.BlockSpec((1, 1, D), lambda i: (i, 0, 0))] if bwd else []),
            out_shape=out_shape, compiler_params=_cparams(),
        )(*rows, g, w, *after)

    def body(*refs):
        refs = list(refs)
        row_refs = [refs.pop(0) for _ in rows]
        g_ref, w_ref = refs.pop(0), refs.pop(0)
        u_ref = refs.pop(0) if u is not None else None
        del refs[:len(after)]
        out_ref, a_ref = refs.pop(0), refs.pop(0)
        dg_ref = refs.pop(0) if bwd else None
        a_all = refs.pop(0)
        t = pl.program_id(0)

        @pl.when(t < ni)
        def _():
            if bwd:
                a, dg = _rms_bwd(row_refs[0][...], g_ref[...], row_refs[1][...])
                dg_ref[0] = dg
            else:
                a = _rms(row_refs[0][...], g_ref[...])
            a_ref[...] = a.astype(BF16)
            a_all[pl.ds(pl.multiple_of(t * tm, tm), tm), :] = a.astype(BF16)

        @pl.when(t >= ni)
        def _():
            acc = _dot(a_all[...], w_ref[...], dims)
            if u_ref is not None:
                acc = acc * (2.0 * jnp.maximum(u_ref[...], 0.0))
            out_ref[...] = acc.astype(out_dtype)

    rows_at = lambda t: jnp.minimum(t, ni - 1)
    cols_at = lambda t: jnp.maximum(t - ni, 0)
    row_spec = pl.BlockSpec((tm, D), lambda t: (rows_at(t), 0))
    w_spec = (pl.BlockSpec((tn, D), lambda t: (cols_at(t), 0)) if dims == NT
              else pl.BlockSpec((D, tn), lambda t: (0, cols_at(t))))
    tile = pl.BlockSpec((s, tn), lambda t: (0, cols_at(t)))
    in_specs = [row_spec] * len(rows) + [pl.BlockSpec((1, D), lambda t: (0, 0)), w_spec]
    in_specs += ([tile] if u is not None else []) + [ANY] * len(after)
    out_specs = [tile, row_spec]
    if bwd:
        out_specs.append(pl.BlockSpec((1, 1, D), lambda t: (rows_at(t), 0, 0)))
    return pl.pallas_call(
        body, name=name, grid=(ni + nj,), in_specs=in_specs, out_specs=out_specs, out_shape=out_shape,
        scratch_shapes=[pltpu.VMEM((s, D), BF16)], compiler_params=_cparams(),
    )(*rows, g, w, *([u] if u is not None else []), *after)


def _mm_rowop(name, kind, pairs, rows, g, relu2=False, after=()):
    s, kdim = pairs[0][0].shape
    tm, tk = _k_tiles(kdim)
    tm = min(tm, s)
    ni, nk = s // tm, kdim // tk
    npair = len(pairs)

    def body(*refs):
        refs = list(refs)
        a_refs = [refs.pop(0) for _ in range(npair)]
        w_refs = [refs.pop(0) for _ in range(npair)]
        row_refs = [refs.pop(0) for _ in rows]
        g_ref = refs.pop(0)
        del refs[:len(after)]
        acc = refs.pop() if nk > 1 else None
        outs = refs
        k = pl.program_id(1)

        def product():
            total = None
            for a_ref, w_ref, (_, _, dims) in zip(a_refs, w_refs, pairs):
                a = a_ref[...]
                if relu2:
                    a = jnp.square(jnp.maximum(a, 0.0))
                term = _dot(a.astype(BF16), w_ref[...], dims)
                total = term if total is None else total + term
            return total

        def finish(h):
            if kind == "rms_res":
                outs[0][...] = row_refs[0][...] + _rms(h, g_ref[...])
                outs[1][...] = h
            else:
                dx, dg = _rms_bwd(row_refs[0][...], g_ref[...], h)
                if kind == "rms_bwd_res":
                    outs[0][...] = row_refs[1][...] + dx
                    outs[1][0] = dg
                else:
                    outs[0][0] = dg

        if nk == 1:
            finish(product())
            return

        @pl.when(k == 0)
        def _():
            acc[...] = jnp.zeros_like(acc)

        acc[...] += product()

        @pl.when(k == nk - 1)
        def _():
            finish(acc[...])

    row_spec = pl.BlockSpec((tm, D), lambda i, k: (i, 0))
    dg_shape = jax.ShapeDtypeStruct((ni, 1, D), F32)
    dg_spec = pl.BlockSpec((1, 1, D), lambda i, k: (i, 0, 0))
    in_specs = [pl.BlockSpec((tm, tk), lambda i, k: (i, k))] * npair
    for _, _, dims in pairs:
        mode = dict(pipeline_mode=pl.Buffered(1)) if nk == 1 else {}
        in_specs.append(pl.BlockSpec((tk, D), lambda i, k: (k, 0), **mode) if dims == NN
                        else pl.BlockSpec((D, tk), lambda i, k: (0, k), **mode))
    in_specs += [row_spec] * len(rows) + [pl.BlockSpec((1, D), lambda i, k: (0, 0))] + [ANY] * len(after)
    if kind == "rms_res":
        out_shape = [jax.ShapeDtypeStruct((s, D), F32)] * 2
        out_specs = [row_spec, row_spec]
    elif kind == "rms_bwd_res":
        out_shape = [jax.ShapeDtypeStruct((s, D), F32), dg_shape]
        out_specs = [row_spec, dg_spec]
    else:
        out_shape = [dg_shape]
        out_specs = [dg_spec]
    return pl.pallas_call(
        body, name=name, grid=(ni, nk), in_specs=in_specs, out_specs=out_specs, out_shape=out_shape,
        scratch_shapes=[pltpu.VMEM((tm, D), F32)] if nk > 1 else [], compiler_params=_cparams(),
    )(*[p[0] for p in pairs], *[p[1] for p in pairs], *rows, g, *after)


def _mm_tn(name, a, gmat, relu2=False, after=()):
    s, m = a.shape
    tm = min(DW_TILE, m)
    ni = m // tm

    def body(a_ref, g_ref, *rest):
        av = a_ref[...]
        if relu2:
            av = jnp.square(jnp.maximum(av, 0.0))
        rest[len(after)][...] = _dot(av.astype(BF16), g_ref[...], TN).astype(BF16)

    return pl.pallas_call(
        body, name=name, grid=(ni,),
        in_specs=[pl.BlockSpec((s, tm), lambda i: (0, i)), pl.BlockSpec((s, D), lambda i: (0, 0))] + [ANY] * len(after),
        out_specs=pl.BlockSpec((tm, D), lambda i: (i, 0)),
        out_shape=jax.ShapeDtypeStruct((m, D), BF16), compiler_params=_cparams(),
    )(a, gmat, *after)


def _tril():
    r = lax.broadcasted_iota(jnp.int32, (HD, HD), 0)
    c = lax.broadcasted_iota(jnp.int32, (HD, HD), 1)
    return (c <= r).astype(F32)


def _gmlp_fwd(z, gv, ws, bst, tb):
    s = z.shape[0]
    tb = min(tb, s)

    def body(zu_ref, zv_ref, gv_ref, ws_ref, bst_ref, y_ref):
        tril = _tril()
        for h in range(NH):
            cols = slice(h * HD, (h + 1) * HD)
            u, vln = _gmlp_rows(zu_ref[:, cols], zv_ref[:, cols], gv_ref[h:h + 1, :])
            wm = (ws_ref[h] * tril).astype(BF16)
            vb = vln.astype(BF16)
            for c in range(tb // HD):
                rws = slice(c * HD, (c + 1) * HD)
                mixed = _dot(wm, vb[rws], NN) + bst_ref[:, h:h + 1]
                y_ref[rws, cols] = (u[rws] * mixed).astype(BF16)

    return pl.pallas_call(
        body, name="gmlp_fwd", grid=(s // tb,),
        in_specs=[pl.BlockSpec((tb, GW), lambda i: (i, 0)), pl.BlockSpec((tb, GW), lambda i: (i, 1)),
                  pl.BlockSpec((NH, HD), lambda i: (0, 0)), pl.BlockSpec((NH, HD, HD), lambda i: (0, 0, 0)),
                  pl.BlockSpec((HD, NH), lambda i: (0, 0))],
        out_specs=pl.BlockSpec((tb, GW), lambda i: (i, 0)),
        out_shape=jax.ShapeDtypeStruct((s, D), BF16), compiler_params=_cparams(),
    )(z, z, gv, ws, bst)


def _gmlp_bwd(z, dy, gv, ws, bst, tb, after=()):
    s = z.shape[0]
    tb = min(tb, s)
    nb = s // tb

    def body(zu_ref, zv_ref, dy_ref, gv_ref, ws_ref, bst_ref, *rest):
        dz_ref, dgv_ref, dws_ref, db_ref = rest[len(after):]
        tril = _tril()
        for h in range(NH):
            cols = slice(h * HD, (h + 1) * HD)
            (u, vln), vjp = jax.vjp(_gmlp_rows, zu_ref[:, cols], zv_ref[:, cols], gv_ref[h:h + 1, :])
            wmf = ws_ref[h] * tril
            wm = wmf.astype(BF16)
            wmt = wmf.T.astype(BF16)
            vb = vln.astype(BF16)
            dws = jnp.zeros((HD, HD), F32)
            db = jnp.zeros((HD, 1), F32)
            du_parts, dvln_parts = [], []
            for c in range(tb // HD):
                rws = slice(c * HD, (c + 1) * HD)
                mixed = _dot(wm, vb[rws], NN) + bst_ref[:, h:h + 1]
                dyc = dy_ref[rws, cols]
                du_parts.append(dyc * mixed)
                dmixed = dyc * u[rws]
                dmb = dmixed.astype(BF16)
                dws = dws + _dot(dmb, vb[rws], NT)
                db = db + jnp.sum(dmixed, axis=1, keepdims=True)
                dvln_parts.append(_dot(wmt, dmb, NN))
            du = jnp.concatenate(du_parts, axis=0)
            dvln = jnp.concatenate(dvln_parts, axis=0)
            dzu, dzv, dgv = vjp((du, dvln))
            dz_ref[:, cols] = dzu.astype(BF16)
            dz_ref[:, slice(GW + h * HD, GW + (h + 1) * HD)] = dzv.astype(BF16)
            dgv_ref[0, h:h + 1, :] = dgv
            dws_ref[0, h] = dws * tril
            db_ref[0, h] = jnp.broadcast_to(db, (HD, LANES))

    blk = pl.BlockSpec((tb, GW), lambda i: (i, 0))
    return pl.pallas_call(
        body, name="gmlp_bwd", grid=(nb,),
        in_specs=[blk, pl.BlockSpec((tb, GW), lambda i: (i, 1)), blk,
                  pl.BlockSpec((NH, HD), lambda i: (0, 0)), pl.BlockSpec((NH, HD, HD), lambda i: (0, 0, 0)),
                  pl.BlockSpec((HD, NH), lambda i: (0, 0))] + [ANY] * len(after),
        out_specs=[pl.BlockSpec((tb, 2 * GW), lambda i: (i, 0)), pl.BlockSpec((1, NH, HD), lambda i: (i, 0, 0)),
                   pl.BlockSpec((1, NH, HD, HD), lambda i: (i, 0, 0, 0)),
                   pl.BlockSpec((1, NH, HD, LANES), lambda i: (i, 0, 0, 0))],
        out_shape=[jax.ShapeDtypeStruct((s, IN_COLS), BF16),
                   jax.ShapeDtypeStruct((nb, NH, HD), F32), jax.ShapeDtypeStruct((nb, NH, HD, HD), F32),
                   jax.ShapeDtypeStruct((nb, NH, HD, LANES), F32)],
        compiler_params=_cparams(),
    )(z, z, dy, gv, ws, bst, *after)


POOL_TILE = 1024


def _pool_count(t0, window):
    pos = (t0 + lax.broadcasted_iota(jnp.int32, (POOL_TILE, LANES), 0)).astype(F32)
    return jnp.minimum(pos + 1.0, float(window))


def _window_sum(win, levels, back):
    n = win.shape[0]
    for lv in range(levels):
        step = 1 << lv
        win = win + pltpu.roll(win, n - step if back else step, 0)
    return win


def _pool_pooled(ppad_ref, t0, g):
    win = ppad_ref[pl.ds(t0, POOL_TILE + POOL_HALO), :]
    wsum = _window_sum(win, g + 1, False)[POOL_HALO:]
    return wsum / _pool_count(t0, POOL_WINDOWS[g]) - win[POOL_HALO:]


def _pool_fwd(z, wp, sp, y):
    s = z.shape[0]
    nt = s // POOL_TILE

    def body(p_ref, wp_ref, sp_ref, _, y_ref, ppad):
        for g in range(NG):
            cols = slice(g * LANES, (g + 1) * LANES)
            ppad[pl.ds(0, POOL_HALO), :] = jnp.zeros((POOL_HALO, LANES), F32)
            ppad[pl.ds(POOL_HALO, s), :] = p_ref[:, cols]
            wpb = wp_ref[g].astype(BF16)
            scale = sp_ref[:, cols]

            def tile(t, carry):
                t0 = pl.multiple_of(t * POOL_TILE, POOL_TILE)
                pooled = _pool_pooled(ppad, t0, g)
                y_ref[pl.ds(t0, POOL_TILE), cols] = (_dot(pooled.astype(BF16), wpb, NN) * scale).astype(BF16)
                return carry

            lax.fori_loop(0, nt, tile, 0)

    return pl.pallas_call(
        body, name="pool_fwd", grid=(1,),
        in_specs=[pl.BlockSpec((s, PW), lambda i: (0, 2 * GW // PW)),
                  pl.BlockSpec((NG, LANES, LANES), lambda i: (0, 0, 0)), pl.BlockSpec((1, PW), lambda i: (0, 0)), ANY],
        out_specs=pl.BlockSpec((s, PW), lambda i: (0, GW // PW)),
        out_shape=jax.ShapeDtypeStruct((s, D), BF16), input_output_aliases={3: 0},
        scratch_shapes=[pltpu.VMEM((s + POOL_HALO, LANES), F32)], compiler_params=_cparams(),
    )(z, wp, sp, y)


def _pool_bwd(z, dy, wp, sp, dz):
    s = z.shape[0]
    nt = s // POOL_TILE

    def body(p_ref, dy_ref, wp_ref, sp_ref, _, dp_ref, dwp_ref, dsp_ref, ppad, rpad, dpool):
        for g in range(NG):
            cols = slice(g * LANES, (g + 1) * LANES)
            ppad[pl.ds(0, POOL_HALO), :] = jnp.zeros((POOL_HALO, LANES), F32)
            ppad[pl.ds(POOL_HALO, s), :] = p_ref[:, cols]
            rpad[pl.ds(s, POOL_HALO), :] = jnp.zeros((POOL_HALO, LANES), F32)
            wpb = wp_ref[g].astype(BF16)
            scale = sp_ref[:, cols]

            def tile(t, carry):
                dwp, dsp = carry
                t0 = pl.multiple_of(t * POOL_TILE, POOL_TILE)
                pooled = _pool_pooled(ppad, t0, g)
                pb = pooled.astype(BF16)
                dyt = dy_ref[pl.ds(t0, POOL_TILE), cols]
                dsp = dsp + jnp.sum(dyt * _dot(pb, wpb, NN), axis=0, keepdims=True)
                dmm = (dyt * scale).astype(BF16)
                dwp = dwp + _dot(pb, dmm, TN)
                dpooled = _dot(dmm, wpb, NT)
                rpad[pl.ds(t0, POOL_TILE), :] = dpooled / _pool_count(t0, POOL_WINDOWS[g])
                dpool[pl.ds(t0, POOL_TILE), :] = dpooled
                return dwp, dsp

            dwp, dsp = lax.fori_loop(0, nt, tile, (jnp.zeros((LANES, LANES), F32), jnp.zeros((1, LANES), F32)))
            dwp_ref[g] = dwp
            dsp_ref[:, cols] = dsp

            def tile2(t, carry):
                t0 = pl.multiple_of(t * POOL_TILE, POOL_TILE)
                win = rpad[pl.ds(t0, POOL_TILE + POOL_HALO), :]
                back = _window_sum(win, g + 1, True)[:POOL_TILE]
                rows = pl.ds(t0, POOL_TILE)
                dp_ref[rows, cols] = (back - dpool[rows, :]).astype(BF16)
                return carry

            lax.fori_loop(0, nt, tile2, 0)

    return pl.pallas_call(
        body, name="pool_bwd", grid=(1,),
        in_specs=[pl.BlockSpec((s, PW), lambda i: (0, 2 * GW // PW)), pl.BlockSpec((s, PW), lambda i: (0, GW // PW)),
                  pl.BlockSpec((NG, LANES, LANES), lambda i: (0, 0, 0)), pl.BlockSpec((1, PW), lambda i: (0, 0)), ANY],
        out_specs=[pl.BlockSpec((s, PW), lambda i: (0, 2 * GW // PW)),
                   pl.BlockSpec((NG, LANES, LANES), lambda i: (0, 0, 0)), pl.BlockSpec((1, PW), lambda i: (0, 0))],
        out_shape=[jax.ShapeDtypeStruct((s, IN_COLS), BF16), jax.ShapeDtypeStruct((NG, LANES, LANES), F32),
                   jax.ShapeDtypeStruct((1, PW), F32)],
        input_output_aliases={4: 0},
        scratch_shapes=[pltpu.VMEM((s + POOL_HALO, LANES), F32), pltpu.VMEM((s + POOL_HALO, LANES), F32),
                        pltpu.VMEM((s, LANES), F32)],
        compiler_params=_cparams(),
    )(z, dy, wp, sp, dz)


CONV_LEAD = CONV_HALO - (CONV_K - 1)


SUBLANES = 8


def _sublane_shifts(win):
    n = win.shape[0]
    return [win] + [pltpu.roll(win, n - b, 0) for b in range(1, SUBLANES)]


def _shifted(shifts, offset):
    a, b = divmod(offset, SUBLANES)
    return shifts[b][a * SUBLANES:a * SUBLANES + ROW_TILE]


def _conv_taps(shifts, wdw_ref, lead, reverse):
    acc = jnp.zeros((ROW_TILE, CW), F32)
    for j in range(CONV_K):
        tap = (CONV_K - 1 - j) if reverse else j
        acc = acc + wdw_ref[tap:tap + 1, :] * _shifted(shifts, lead + j)
    return acc


def _conv_fill_glu(cv_ref, cg_ref, xpad, s):
    xpad[pl.ds(0, CONV_HALO), :] = jnp.zeros((CONV_HALO, CW), F32)

    def fill(t, carry):
        t0 = pl.multiple_of(t * ROW_TILE, ROW_TILE)
        rows = pl.ds(t0, ROW_TILE)
        xpad[pl.ds(t0 + CONV_HALO, ROW_TILE), :] = _glu(cv_ref[rows, :], cg_ref[rows, :])
        return carry

    lax.fori_loop(0, s // ROW_TILE, fill, 0)


def _conv_fwd(z, wdw, bdw, lng, lnb, y):
    s = z.shape[0]

    def body(cv_ref, cg_ref, wdw_ref, bdw_ref, lng_ref, lnb_ref, _, y_ref, xpad):
        _conv_fill_glu(cv_ref, cg_ref, xpad, s)

        def tile(t, carry):
            t0 = pl.multiple_of(t * ROW_TILE, ROW_TILE)
            shifts = _sublane_shifts(xpad[pl.ds(t0, ROW_TILE + CONV_HALO), :])
            hc = _conv_taps(shifts, wdw_ref, CONV_LEAD, False) + bdw_ref[...]
            y_ref[pl.ds(t0, ROW_TILE), :] = _ln_silu(hc, lng_ref[...], lnb_ref[...]).astype(BF16)
            return carry

        lax.fori_loop(0, s // ROW_TILE, tile, 0)

    vec = pl.BlockSpec((1, CW), lambda i: (0, 0))
    return pl.pallas_call(
        body, name="conv_fwd", grid=(1,),
        in_specs=[pl.BlockSpec((s, CW), lambda i: (0, (2 * GW + PW) // CW)),
                  pl.BlockSpec((s, CW), lambda i: (0, (2 * GW + PW) // CW + 1)),
                  pl.BlockSpec((CONV_K + 1, CW), lambda i: (0, 0)), vec, vec, vec, ANY],
        out_specs=pl.BlockSpec((s, CW), lambda i: (0, (GW + PW) // CW)),
        out_shape=jax.ShapeDtypeStruct((s, D), BF16), input_output_aliases={6: 0},
        scratch_shapes=[pltpu.VMEM((s + CONV_HALO, CW), F32)], compiler_params=_cparams(),
    )(z, z, wdw, bdw, lng, lnb, y)


def _conv_bwd(z, dy, wdw, bdw, lng, lnb, dz):
    s = z.shape[0]

    def body(cv_ref, cg_ref, dy_ref, wdw_ref, bdw_ref, lng_ref, lnb_ref, _,
             dz_ref, dwdw_ref, dbdw_ref, dlng_ref, dlnb_ref, xpad, dpad, dcg_keep):
        @pl.when(pl.program_id(0) == 0)
        def _():
            compute(cv_ref, cg_ref, dy_ref, wdw_ref, bdw_ref, lng_ref, lnb_ref,
                    dz_ref, dcg_keep, dwdw_ref, dbdw_ref, dlng_ref, dlnb_ref, xpad, dpad)

        @pl.when(pl.program_id(0) == 1)
        def _():
            dz_ref[...] = dcg_keep[...]

    def compute(cv_ref, cg_ref, dy_ref, wdw_ref, bdw_ref, lng_ref, lnb_ref,
                dcv_ref, dcg_ref, dwdw_ref, dbdw_ref, dlng_ref, dlnb_ref, xpad, dpad):
        _conv_fill_glu(cv_ref, cg_ref, xpad, s)
        dpad[pl.ds(s, CONV_HALO), :] = jnp.zeros((CONV_HALO, CW), F32)
        dwdw_ref[...] = jnp.zeros((CONV_K + 1, CW), F32)

        def tile(t, carry):
            db, dg, dbeta = carry
            t0 = pl.multiple_of(t * ROW_TILE, ROW_TILE)
            shifts = _sublane_shifts(xpad[pl.ds(t0, ROW_TILE + CONV_HALO), :])
            hc = _conv_taps(shifts, wdw_ref, CONV_LEAD, False) + bdw_ref[...]
            _, vjp = jax.vjp(_ln_silu, hc, lng_ref[...], lnb_ref[...])
            dhc, dg_t, dbeta_t = vjp(dy_ref[pl.ds(t0, ROW_TILE), :])
            dpad[pl.ds(t0, ROW_TILE), :] = dhc
            for j in range(CONV_K):
                dwdw_ref[j:j + 1, :] += jnp.sum(dhc * _shifted(shifts, CONV_LEAD + j), axis=0, keepdims=True)
            return db + jnp.sum(dhc, axis=0, keepdims=True), dg + dg_t, dbeta + dbeta_t

        zero = jnp.zeros((1, CW), F32)
        db, dg, dbeta = lax.fori_loop(0, s // ROW_TILE, tile, (zero, zero, zero))
        dbdw_ref[...] = db
        dlng_ref[...] = dg
        dlnb_ref[...] = dbeta

        def tile2(t, carry):
            t0 = pl.multiple_of(t * ROW_TILE, ROW_TILE)
            rows = pl.ds(t0, ROW_TILE)
            dglu = _conv_taps(_sublane_shifts(dpad[pl.ds(t0, ROW_TILE + CONV_HALO), :]), wdw_ref, 0, True)
            _, vjp = jax.vjp(_glu, cv_ref[rows, :], cg_ref[rows, :])
            dcv, dcg = vjp(dglu)
            dcv_ref[rows, :] = dcv.astype(BF16)
            dcg_ref[rows, :] = dcg.astype(BF16)
            return carry

        lax.fori_loop(0, s // ROW_TILE, tile2, 0)

    vec = pl.BlockSpec((1, CW), lambda i: (0, 0))
    wspec = pl.BlockSpec((CONV_K + 1, CW), lambda i: (0, 0))
    vshape = jax.ShapeDtypeStruct((1, CW), F32)
    return pl.pallas_call(
        body, name="conv_bwd", grid=(2,),
        in_specs=[pl.BlockSpec((s, CW), lambda i: (0, (2 * GW + PW) // CW)),
                  pl.BlockSpec((s, CW), lambda i: (0, (2 * GW + PW) // CW + 1)),
                  pl.BlockSpec((s, CW), lambda i: (0, (GW + PW) // CW)), wspec, vec, vec, vec, ANY],
        out_specs=[pl.BlockSpec((s, CW), lambda i: (0, (2 * GW + PW) // CW + i)), wspec, vec, vec, vec],
        out_shape=[jax.ShapeDtypeStruct((s, IN_COLS), BF16), jax.ShapeDtypeStruct((CONV_K + 1, CW), F32),
                   vshape, vshape, vshape],
        input_output_aliases={7: 0},
        scratch_shapes=[pltpu.VMEM((s + CONV_HALO, CW), F32), pltpu.VMEM((s + CONV_HALO, CW), F32),
                        pltpu.VMEM((s, CW), BF16)],
        compiler_params=_cparams(),
    )(z, z, dy, wdw, bdw, lng, lnb, dz)


def _softmax_rows(sc):
    e = jnp.exp(sc - jnp.max(sc, axis=-1, keepdims=True))
    return e / jnp.sum(e, axis=-1, keepdims=True)


def _attn_fwd(q, k, v, tq):
    s, m = q.shape[0], k.shape[0]
    tq = min(tq, s)

    def body(q_ref, k_ref, v_ref, o_ref):
        for h in range(XH):
            cols = slice(h * XHD, (h + 1) * XHD)
            p = _softmax_rows(_dot(q_ref[:, cols], k_ref[:, cols], NT) * ATT_SCALE)
            o_ref[:, cols] = _dot(p.astype(BF16), v_ref[:, cols], NN).astype(BF16)

    kv = pl.BlockSpec((m, D), lambda i: (0, 0))
    return pl.pallas_call(
        body, name="attn_fwd", grid=(s // tq,),
        in_specs=[pl.BlockSpec((tq, D), lambda i: (i, 0)), kv, kv],
        out_specs=pl.BlockSpec((tq, D), lambda i: (i, 0)),
        out_shape=jax.ShapeDtypeStruct((s, D), BF16), compiler_params=_cparams(),
    )(q, k, v)


def _attn_bwd(q, k, v, do, tq, after=()):
    s, m = q.shape[0], k.shape[0]
    tq = min(tq, s)

    def body(q_ref, k_ref, v_ref, do_ref, *rest):
        dq_ref, dk_ref, dv_ref = rest[len(after):]

        @pl.when(pl.program_id(0) == 0)
        def _():
            dk_ref[...] = jnp.zeros_like(dk_ref)
            dv_ref[...] = jnp.zeros_like(dv_ref)

        for h in range(XH):
            cols = slice(h * XHD, (h + 1) * XHD)
            qh, kh, vh, doh = q_ref[:, cols], k_ref[:, cols], v_ref[:, cols], do_ref[:, cols]
            p = _softmax_rows(_dot(qh, kh, NT) * ATT_SCALE)
            dp = _dot(doh, vh, NT)
            dv_ref[:, cols] += _dot(p.astype(BF16), doh, TN)
            ds = (p * (dp - jnp.sum(p * dp, axis=-1, keepdims=True)) * ATT_SCALE).astype(BF16)
            dq_ref[:, cols] = _dot(ds, kh, NN).astype(BF16)
            dk_ref[:, cols] += _dot(ds, qh, TN)

    kv = pl.BlockSpec((m, D), lambda i: (0, 0))
    qs = pl.BlockSpec((tq, D), lambda i: (i, 0))
    return pl.pallas_call(
        body, name="attn_bwd", grid=(s // tq,),
        in_specs=[qs, kv, kv, qs] + [ANY] * len(after), out_specs=[qs, kv, kv],
        out_shape=[jax.ShapeDtypeStruct((s, D), BF16), jax.ShapeDtypeStruct((m, D), F32),
                   jax.ShapeDtypeStruct((m, D), F32)],
        compiler_params=_cparams(),
    )(q, k, v, do, *after)


def _loss_head(y, target, tm):
    s = y.shape[0]
    tm = min(tm, s)

    def body(y_ref, t_ref, dy_ref, part_ref):
        err = y_ref[...] - t_ref[...]
        dy_ref[...] = err * (1.0 / D)
        part_ref[...] = jnp.full((1, 8, LANES), 0.5 * jnp.sum(err * err) * (1.0 / D), F32)

    blk = pl.BlockSpec((tm, D), lambda i: (i, 0))
    return pl.pallas_call(
        body, name="loss_head", grid=(s // tm,), in_specs=[blk, blk],
        out_specs=[blk, pl.BlockSpec((1, 8, LANES), lambda i: (i, 0, 0))],
        out_shape=[jax.ShapeDtypeStruct((s, D), F32), jax.ShapeDtypeStruct((s // tm, 8, LANES), F32)],
        compiler_params=_cparams(),
    )(y, target)


def _layer_fwd(x0, mem, w, p, fetch):
    z, hn0 = _rowop_mm("mix_in", "rms", (x0,), p["norm_mix_pre"], w["w_in"], NT, F32)
    y = _gmlp_fwd(z, p["gmlp_v_gain"], p["w_spatial"], p["b_spatial_t"], 1024)
    y = _pool_fwd(z, p["w_pool"], p["s_pool"], y)
    y = _conv_fwd(z, p["w_dw"], p["b_dw"], p["conv_ln_g"], p["conv_ln_b"], y)
    w.update(fetch("out", (y,)))
    x1, h0 = _mm_rowop("mix_out", "rms_res", [(y, w["w_out"], NN)], (x0,), p["norm_mix_post"])
    w.update(fetch("att", (x1,)))
    q, hn1 = _rowop_mm("att_q", "rms", (x1,), p["norm_xattn_pre"], w["w_q"], NN, BF16)
    k, mn = _rowop_mm("att_k", "rms", (mem,), p["norm_mem"], w["w_k"], NN, BF16, after=(x1,))
    v, _ = _rowop_mm("att_v", "rms", (mem,), p["norm_mem"], w["w_v"], NN, BF16, after=(x1,))
    o = _attn_fwd(q, k, v, 1024)
    x2, h1 = _mm_rowop("att_o", "rms_res", [(o, w["w_o"], NN)], (x1,), p["norm_xattn_post"])
    w.update(fetch("up", (x2,)))
    u, hn2 = _rowop_mm("ffn_up", "rms", (x2,), p["norm_ffn_pre"], w["w_up"], NT, F32)
    w.update(fetch("down", (u,)))
    x3, h2 = _mm_rowop("ffn_down", "rms_res", [(u, w["w_down"], NN)], (x2,), p["norm_ffn_post"], relu2=True)
    saved = dict(x0=x0, z=z, hn0=hn0, y=y, h0=h0, x1=x1, q=q, hn1=hn1, k=k, v=v, mn=mn, o=o, h1=h1, x2=x2, u=u,
                 hn2=hn2, h2=h2)
    return x3, saved


def _layer_bwd(dx3, mem, w, p, sv, red):
    gs = {}
    du, dh2, dg = _rowop_mm("ffn_down_bwd", "rms_bwd", (sv["h2"], dx3), p["norm_ffn_post"], w["w_down"], NT, BF16,
                            u=sv["u"], after=red.after())
    gs["norm_ffn_post"] = jnp.sum(dg, axis=0)
    g_down = _mm_tn("ffn_down_dw", sv["u"], dh2, relu2=True)
    red.advance((g_down,))
    dx2, dg = _mm_rowop("ffn_up_bwd", "rms_bwd_res", [(du, w["w_up"], NN)], (sv["x2"], dx3), p["norm_ffn_pre"],
                        after=red.after())
    gs["norm_ffn_pre"] = jnp.sum(dg, axis=0)
    g_up = _mm_tn("ffn_up_dw", du, sv["hn2"])
    red.add("ffn", ("w_down", "w_up"), [g_down, g_up])
    do, dh1, dg = _rowop_mm("att_o_bwd", "rms_bwd", (sv["h1"], dx2), p["norm_xattn_post"], w["w_o"], NT, BF16,
                            after=red.after())
    gs["norm_xattn_post"] = jnp.sum(dg, axis=0)
    g_o = _mm_tn("att_o_dw", sv["o"], dh1)
    red.advance((g_o,))
    dq, dk, dv = _attn_bwd(sv["q"], sv["k"], sv["v"], do, 1024, after=red.after())
    dk, dv = dk.astype(BF16), dv.astype(BF16)
    dx1, dg = _mm_rowop("att_q_bwd", "rms_bwd_res", [(dq, w["w_q"], NT)], (sv["x1"], dx2), p["norm_xattn_pre"],
                        after=red.after())
    gs["norm_xattn_pre"] = jnp.sum(dg, axis=0)
    g_q = _mm_tn("att_q_dw", sv["hn1"], dq)
    g_k = _mm_tn("att_k_dw", sv["mn"], dk)
    g_v = _mm_tn("att_v_dw", sv["mn"], dv)
    (dg,) = _mm_rowop("att_kv_bwd", "rms_bwd_gain", [(dk, w["w_k"], NT), (dv, w["w_v"], NT)], (mem,), p["norm_mem"])
    gs["norm_mem"] = jnp.sum(dg, axis=0)
    red.add("att", ("w_o", "w_q", "w_k", "w_v"), [g_o, g_q, g_k, g_v])
    dy, dh0, dg = _rowop_mm("mix_out_bwd", "rms_bwd", (sv["h0"], dx1), p["norm_mix_post"], w["w_out"], NT, F32,
                            after=red.after())
    gs["norm_mix_post"] = jnp.sum(dg, axis=0)
    g_out = _mm_tn("mix_out_dw", sv["y"], dh0)
    red.advance((g_out,))
    red.add("out", ("w_out",), [g_out])
    z = sv["z"]
    dz, dgv, dws, dbs = _gmlp_bwd(z, dy, p["gmlp_v_gain"], p["w_spatial"], p["b_spatial_t"], 512, after=red.after())
    gs["gmlp_v_gain"] = jnp.sum(dgv, axis=0)
    gs["w_spatial"] = jnp.sum(dws, axis=0)
    gs["b_spatial"] = jnp.sum(dbs[..., 0], axis=0)
    dz, gs["w_pool"], gs["s_pool"] = _pool_bwd(z, dy, p["w_pool"], p["s_pool"], dz)
    dz, dwdw, gs["b_dw"], gs["conv_ln_g"], gs["conv_ln_b"] = _conv_bwd(
        z, dy, p["w_dw"], p["b_dw"], p["conv_ln_g"], p["conv_ln_b"], dz)
    red.advance((dz,))
    g_in = _mm_tn("mix_in_dw", dz, sv["hn0"], after=red.after())
    red.add("in", ("w_in",), [g_in])
    if red.layer == 0:
        red.advance(())
    red.small("mixer", _small_grad_arrays(gs, dwdw, norms=False))
    dx0, dg = _mm_rowop("mix_in_bwd", "rms_bwd_res", [(dz, w["w_in"], NN)], (sv["x0"], dx1), p["norm_mix_pre"],
                        after=red.after())
    gs["norm_mix_pre"] = jnp.sum(dg, axis=0)
    late = {"norms": jnp.concatenate([gs[n] for n in NORM_NAMES], axis=0)}
    if red.layer == 0:
        late["loss"] = red.extra[0]
    red.small("norms", late)
    return dx0


NORM_NAMES = ("norm_mix_pre", "norm_mix_post", "norm_xattn_pre", "norm_mem", "norm_xattn_post", "norm_ffn_pre",
              "norm_ffn_post")
VEC_NAMES = ("s_pool", "b_dw", "conv_ln_g", "conv_ln_b")
SMALL_ARRAYS = ("norms", "gain_bias", "w_spatial", "w_pool", "vecs", "w_dw")


def _small_grad_arrays(gs, dwdw, norms=True):
    out = {"norms": jnp.concatenate([gs[n] for n in NORM_NAMES], axis=0)} if norms else {}
    out.update({"gain_bias": jnp.concatenate([gs["gmlp_v_gain"], gs["b_spatial"]], axis=0),
                "w_spatial": gs["w_spatial"], "w_pool": gs["w_pool"],
                "vecs": jnp.concatenate([gs[n] for n in VEC_NAMES], axis=0), "w_dw": dwdw})
    return out


def _layer_params(small, l):
    p = {n: small[n][l].reshape(1, -1) for n in ("norm_mix_pre", "norm_mix_post", "s_pool", "b_dw", "conv_ln_g",
                                                   "conv_ln_b", "norm_xattn_pre", "norm_mem", "norm_xattn_post",
                                                   "norm_ffn_pre", "norm_ffn_post")}
    p["gmlp_v_gain"] = small["gmlp_v_gain"][l]
    p["w_spatial"] = small["w_spatial"][l]
    p["b_spatial_t"] = small["b_spatial"][l].T
    p["w_pool"] = small["w_pool"][l]
    p["w_dw"] = jnp.pad(small["w_dw"][l], ((0, 1), (0, 0)))
    return p


def _local_step(x, mem, target, fetch, small, red):
    small = dict(small)
    saved, weights, params = [], [], []
    h = x
    marker = ()
    for l in range(DEPTH):
        w = fetch(l, "in", marker)
        if "taps" in w:
            small["w_dw"] = w.pop("taps")
        p = _layer_params(small, l)
        h, sv = _layer_fwd(h, mem, w, p, functools.partial(fetch, l))
        marker = (h,)
        saved.append(sv)
        weights.append(w)
        params.append(p)
    dh, loss = _loss_head(h, target, 1024)
    red.extra = (loss,)
    for l in reversed(range(DEPTH)):
        red.layer = l
        dh = _layer_bwd(dh, mem, weights[l], params[l], saved[l], red)
    return loss, dh


HBM = pl.BlockSpec(memory_space=pltpu.HBM)


def _position():
    return lax.axis_index("x"), lax.axis_index("y"), lax.axis_index("c")


SEM = pl.BlockSpec(memory_space=pltpu.SEMAPHORE)
EFFECT = pltpu.SideEffectType.DATAFLOW_SIDE_EFFECTING
TOKEN = jax.ShapeDtypeStruct((8, LANES), F32)
TOKEN_SPEC = pl.BlockSpec(memory_space=pltpu.VMEM)


def _landing(shape, dtype):
    return pltpu.with_memory_space_constraint(lax.empty(shape, dtype), pltpu.HBM)


def _hbm_shapes(arrays):
    return [pltpu.HBM(a.shape, a.dtype) for a in arrays]


def _block(ref, r, dev):
    return ref.at[pl.ds((4 * dev[0] + 2 * dev[1] + dev[2]) * r, r), :]


def _split_call(name, body, thru, sems_in, after, sems_out, token):
    n = len(thru)
    out_shape = [pltpu.SemaphoreType.DMA(s) for s in sems_out] + _hbm_shapes(thru) + ([TOKEN] if token else [])
    out_specs = [SEM] * len(sems_out) + [HBM] * n + ([TOKEN_SPEC] if token else [])
    return pl.pallas_call(
        body, name=name, in_specs=[HBM] * n + [SEM] * len(sems_in) + [ANY] * len(after),
        out_specs=out_specs, out_shape=out_shape,
        input_output_aliases={i: len(sems_out) + i for i in range(n)},
        compiler_params=pltpu.CompilerParams(has_side_effects=EFFECT),
    )(*thru, *sems_in, *after)


def _place_own(name, srcs, dev, out_dtype, tr):
    n = len(srcs)
    r, cols = srcs[0][0].shape[-2:]
    tr = r if r < 16 else _row_tile(r, tr)
    nb = r // tr

    def body(dev_ref, *refs):
        for a in range(n):
            refs[n + a][...] = refs[a][...].astype(out_dtype)

    in_specs = [pl.BlockSpec((tr, cols), lambda i, d: (i, 0)) if l is None
                else pl.BlockSpec((None, tr, cols), lambda i, d, l=l: (l, i, 0)) for _, l in srcs]
    return pl.pallas_call(
        body, name=name,
        grid_spec=pltpu.PrefetchScalarGridSpec(
            num_scalar_prefetch=1, grid=(nb,), in_specs=in_specs,
            out_specs=[pl.BlockSpec((tr, cols), lambda i, d: (d[0] * nb + i, 0))] * n),
        out_shape=[jax.ShapeDtypeStruct((N_DEV * r, cols), out_dtype)] * n, compiler_params=_cparams(),
    )(dev, *[a for a, _ in srcs])


def _gather_peers(x, y, c):
    return [(1 - x, y, c), (x, 1 - y, c), (1 - x, 1 - y, c), (x, y, 1 - c)]


def _block_rows(land):
    return land.shape[0] // N_DEV


def _near_peers(x, y, c):
    return [(1 - x, y, c), (x, 1 - y, c), (x, y, 1 - c)]


def _relay_route(x, y, c):
    origin = (x + c * (1 - 2 * x), y + (1 - c) * (1 - 2 * y), c)
    target = (x + (1 - c) * (1 - 2 * x), y + c * (1 - 2 * y), c)
    return origin, target


def _same_block_copy(blk, send_sem, recv_sem, to):
    return pltpu.make_async_remote_copy(src_ref=blk, dst_ref=blk, send_sem=send_sem, recv_sem=recv_sem, device_id=to,
                                        device_id_type=MESH)


def _gather_start(name, lands, after):
    n = len(lands)

    def body(*refs):
        lz = refs[:n]
        send_sems, recv_sems = refs[n + len(after)], refs[n + len(after) + 1]
        token = refs[-1]
        x, y, c = _position()
        for a in range(n):
            own = _block(lz[a], _block_rows(lands[a]), (x, y, c))
            for k, to in enumerate(_near_peers(x, y, c)):
                _same_block_copy(own, send_sems.at[k], recv_sems.at[k], to).start()
        token[...] = jnp.zeros_like(token)

    out = _split_call(name, body, list(lands), [], after, [(3,), (3,)], True)
    return out[0], out[1], out[2:2 + n], out[-1]


def _gather_step(name, near, far, fresh, after):
    groups = [g for g in (near and near[0], far and far[0], fresh) if g]
    counts = [len(near[0]) if near else 0, len(far[0]) if far else 0, len(fresh) if fresh else 0]
    n = sum(counts)
    sems_in = ([near[1]] if near else []) + ([far[1]] if far else [])
    sems_out = ([(2,), (2,), (1,), (1,)] if near else []) + ([(1,), (1,)] if far else []) + ([(3,), (3,)] if fresh else [])

    def body(*refs):
        lz = list(refs[:n])
        ins = list(refs[n:n + len(sems_in)])
        outs = list(refs[n + len(sems_in) + len(after):n + len(sems_in) + len(after) + len(sems_out)])
        token = refs[-1]
        x, y, c = _position()
        me, sibling = (x, y, c), (x, y, 1 - c)
        near_lz, far_lz, fresh_lz = (lz[sum(counts[:i]):sum(counts[:i + 1])] for i in range(3))
        neighbours = _near_peers(x, y, c)[:2]
        origin, target = _relay_route(x, y, c)
        diagonal = (1 - x, 1 - y, c)
        if near:
            recv0 = ins.pop(0)
            fsend, frecv, rsend, rrecv = (outs.pop(0) for _ in range(4))
            for a, land in enumerate(near[0]):
                for j, chip in enumerate(neighbours):
                    _same_block_copy(_block(near_lz[a], _block_rows(land), chip), fsend.at[j], recv0.at[j], me).wait_recv()
        if far:
            rrecv_in = ins.pop(0)
            f2send, f2recv = outs.pop(0), outs.pop(0)
            for a, land in enumerate(far[0]):
                _same_block_copy(_block(far_lz[a], _block_rows(land), diagonal), f2send.at[0], rrecv_in.at[0], me).wait_recv()
            for a, land in enumerate(far[0]):
                _same_block_copy(_block(far_lz[a], _block_rows(land), diagonal), f2send.at[0], f2recv.at[0], sibling).start()
        if near:
            for a, land in enumerate(near[0]):
                r = _block_rows(land)
                _same_block_copy(_block(near_lz[a], r, origin), rsend.at[0], rrecv.at[0], target).start()
                for j, chip in enumerate(neighbours):
                    _same_block_copy(_block(near_lz[a], r, chip), fsend.at[j], frecv.at[j], sibling).start()
        if fresh:
            send_sems, recv_sems = outs.pop(0), outs.pop(0)
            for a, land in enumerate(fresh):
                own = _block(fresh_lz[a], _block_rows(land), me)
                for k, to in enumerate(_near_peers(x, y, c)):
                    _same_block_copy(own, send_sems.at[k], recv_sems.at[k], to).start()
        token[...] = jnp.zeros_like(token)

    out = list(_split_call(name, body, [l for g in groups for l in g], sems_in, after, sems_out, True))
    res = {"token": out.pop()}
    if near:
        res.update(fsend=out.pop(0), frecv=out.pop(0), rsend=out.pop(0), rrecv=out.pop(0))
    if far:
        res.update(f2send=out.pop(0), f2recv=out.pop(0))
    if fresh:
        res.update(send=out.pop(0), recv=out.pop(0))
    res["near"], res["far"], res["fresh"] = (out[sum(counts[:i]):sum(counts[:i + 1])] for i in range(3))
    return res


def _gather_finish(name, lands, send_sems, recv_sems, fsend, frecv, rsend, f2send, f2recv, after):
    n = len(lands)

    def body(*refs):
        lz = refs[:n]
        send0, recv0, fsend_ref, frecv_ref, rsend_ref, f2send_ref, f2recv_ref = refs[n:n + 7]
        x, y, c = _position()
        me = (x, y, c)
        near = _near_peers(x, y, c)[:2]
        origin, _ = _relay_route(x, y, c)
        for a in range(n):
            r = _block_rows(lands[a])
            sib = _block(lz[a], r, (x, y, 1 - c))
            _same_block_copy(sib, send0.at[2], recv0.at[2], me).wait_recv()
            for j, chip in enumerate(near):
                blk = _block(lz[a], r, (chip[0], chip[1], 1 - c))
                _same_block_copy(blk, fsend_ref.at[j], frecv_ref.at[j], me).wait_recv()
            far = _block(lz[a], r, (1 - x, 1 - y, 1 - c))
            _same_block_copy(far, f2send_ref.at[0], f2recv_ref.at[0], me).wait_recv()
            own = _block(lz[a], r, me)
            for k in range(3):
                _same_block_copy(own, send0.at[k], recv0.at[k], me).wait_send()
            for j, chip in enumerate(near):
                _same_block_copy(_block(lz[a], r, chip), fsend_ref.at[j], frecv_ref.at[j], me).wait_send()
            _same_block_copy(_block(lz[a], r, origin), rsend_ref.at[0], recv0.at[0], me).wait_send()
            _same_block_copy(_block(lz[a], r, (1 - x, 1 - y, c)), f2send_ref.at[0], f2recv_ref.at[0], me).wait_send()

    return _split_call(name, body, list(lands), [send_sems, recv_sems, fsend, frecv, rsend, f2send, f2recv], after, [],
                       False)


def _sibling_start(name, grads, after):
    n = len(grads)
    lands = [_landing((4, g.shape[0] // N_DEV, D), g.dtype) for g in grads]

    def body(*refs):
        ins, lz = refs[:n], refs[n:2 * n]
        send_sem, recv_sem = refs[2 * n + len(after)], refs[2 * n + len(after) + 1]
        token = refs[-1]
        x, y, c = _position()
        for a in range(n):
            r = grads[a].shape[0] // N_DEV
            for q in range(4):
                pltpu.make_async_remote_copy(
                    src_ref=ins[a].at[pl.ds((2 * q + 1 - c) * r, r), :], dst_ref=lz[a].at[q], send_sem=send_sem.at[0],
                    recv_sem=recv_sem.at[0], device_id=(x, y, 1 - c), device_id_type=MESH).start()
        token[...] = jnp.zeros_like(token)

    out = _split_call(name, body, list(grads) + lands, [], after, [(1,), (1,)], True)
    return out[0], out[1], out[2:2 + n], out[2 + n:2 + 2 * n], out[-1]


def _sibling_finish(name, grads, lands, send_sem, recv_sem, after):
    n = len(grads)

    def body(*refs):
        ins, lz = refs[:n], refs[n:2 * n]
        send_ref, recv_ref = refs[2 * n], refs[2 * n + 1]
        x, y, c = _position()
        for a in range(n):
            r = grads[a].shape[0] // N_DEV
            for q in range(4):
                cp = pltpu.make_async_remote_copy(
                    src_ref=ins[a].at[pl.ds((2 * q + 1 - c) * r, r), :], dst_ref=lz[a].at[q], send_sem=send_ref.at[0],
                    recv_sem=recv_ref.at[0], device_id=(x, y, c), device_id_type=MESH)
                cp.wait_send()
                cp.wait_recv()

    out = _split_call(name, body, list(grads) + list(lands), [send_sem, recv_sem], after, [], False)
    return out[:n], out[n:2 * n]


def _chip_start(name, parts, after):
    n = len(parts)
    lands = [_landing((3,) + p.shape[1:], p.dtype) for p in parts]

    def body(*refs):
        ins, lz = refs[:n], refs[n:2 * n]
        send_sems, recv_sems = refs[2 * n + len(after)], refs[2 * n + len(after) + 1]
        token = refs[-1]
        x, y, c = _position()
        for a in range(n):
            for j, chip in enumerate(_gather_peers(x, y, c)[:3]):
                pltpu.make_async_remote_copy(
                    src_ref=ins[a].at[2 * chip[0] + chip[1]], dst_ref=lz[a].at[j], send_sem=send_sems.at[j],
                    recv_sem=recv_sems.at[j], device_id=chip, device_id_type=MESH).start()
        token[...] = jnp.zeros_like(token)

    out = _split_call(name, body, list(parts) + lands, [], after, [(3,), (3,)], True)
    return out[0], out[1], out[2:2 + n], out[2 + n:2 + 2 * n], out[-1]


def _chip_finish(name, parts, lands, send_sems, recv_sems, after):
    n = len(parts)

    def body(*refs):
        ins, lz = refs[:n], refs[n:2 * n]
        send_ref, recv_ref = refs[2 * n], refs[2 * n + 1]
        me = _position()
        for a in range(n):
            for j in range(3):
                cp = pltpu.make_async_remote_copy(
                    src_ref=ins[a].at[j], dst_ref=lz[a].at[j], send_sem=send_ref.at[j], recv_sem=recv_ref.at[j],
                    device_id=me, device_id_type=MESH)
                cp.wait_send()
                cp.wait_recv()

    out = _split_call(name, body, list(parts) + list(lands), [send_sems, recv_sems], after, [], False)
    return out[:n], out[n:2 * n]


def _other_devices(x, y, c):
    return [(x + (k >> 2 & 1) * (1 - 2 * x), y + (k >> 1 & 1) * (1 - 2 * y), c + (k & 1) * (1 - 2 * c))
            for k in range(1, N_DEV)]


def _broadcast_start(name, arrays, after):
    n = len(arrays)
    lands = [_landing((N_DEV,) + a.shape, a.dtype) for a in arrays]

    def body(*refs):
        ins, lz = refs[:n], refs[n:2 * n]
        send_sems, recv_sems = refs[2 * n + len(after)], refs[2 * n + len(after) + 1]
        token = refs[-1]
        x, y, c = _position()
        for a in range(n):
            for k, peer in enumerate(_other_devices(x, y, c)):
                pltpu.make_async_remote_copy(
                    src_ref=ins[a], dst_ref=lz[a].at[4 * x + 2 * y + c], send_sem=send_sems.at[k],
                    recv_sem=recv_sems.at[k], device_id=peer, device_id_type=MESH).start()
        token[...] = jnp.zeros_like(token)

    out = _split_call(name, body, list(arrays) + lands, [], after, [(N_DEV - 1,), (N_DEV - 1,)], True)
    return out[0], out[1], out[2:2 + n], out[2 + n:2 + 2 * n], out[-1]


def _broadcast_finish(name, arrays, lands, send_sems, recv_sems, after):
    n = len(arrays)

    def body(*refs):
        ins, lz = refs[:n], refs[n:2 * n]
        send_ref, recv_ref = refs[2 * n], refs[2 * n + 1]
        x, y, c = _position()
        for a in range(n):
            for k, peer in enumerate(_other_devices(x, y, c)):
                cp = pltpu.make_async_remote_copy(
                    src_ref=ins[a], dst_ref=lz[a].at[4 * peer[0] + 2 * peer[1] + peer[2]], send_sem=send_ref.at[k],
                    recv_sem=recv_ref.at[k], device_id=(x, y, c), device_id_type=MESH)
                cp.wait_send()
                cp.wait_recv()

    out = _split_call(name, body, list(arrays) + list(lands), [send_sems, recv_sems], after, [], False)
    return out[:n], out[n:2 * n]


def _row_tile(r, target):
    return max(t for t in range(16, min(r, target) + 1, 16) if r % t == 0)


CHIP_PARTIAL_BYTES = 12 * 1024 * 1024


def _chip_partial(name, grads, gots, c):
    n = len(grads)
    r = grads[0].shape[0] // N_DEV
    tr = _row_tile(r, CHIP_PARTIAL_BYTES // (n * 3 * D * 2))

    def body(c_ref, *refs):
        for a in range(n):
            refs[2 * n + a][...] = (refs[a][...].astype(F32) + refs[n + a][...].astype(F32)).astype(BF16)

    blk = pl.BlockSpec((None, tr, D), lambda q, i, c_ref: (q, i, 0))
    return pl.pallas_call(
        body, name=name,
        grid_spec=pltpu.PrefetchScalarGridSpec(
            num_scalar_prefetch=1, grid=(4, r // tr),
            in_specs=[pl.BlockSpec((None, None, tr, D), lambda q, i, c_ref: (q, c_ref[0], i, 0))] * n + [blk] * n,
            out_specs=[blk] * n),
        out_shape=[jax.ShapeDtypeStruct((4, r, D), BF16)] * n, compiler_params=_cparams(),
    )(c, *[g.reshape(4, 2, r, D) for g in grads], *gots)


class _WeightGather:
    def __init__(self, groups):
        self.groups = list(groups)
        self.index = {key: i for i, (key, _, _) in enumerate(groups)}
        self.state = [None] * len(groups)
        self.token = ()
        for i in range(min(2, len(groups))):
            self._start(i)

    def _tag(self, i):
        return "%s_%d" % self.groups[i][0][::-1]

    def _start(self, i):
        send, recv, lz, tok = _gather_start("gather_start_" + self._tag(i), self.groups[i][2], self.token)
        self.state[i] = dict(send=send, recv=recv, lands=lz)
        self.token = (tok,)

    def _step(self, name, near, far, fresh, marker):
        exists = lambda i: i is not None and i < len(self.groups)
        near, far, fresh = (i if exists(i) else None for i in (near, far, fresh))
        res = _gather_step(
            name, None if near is None else (self.state[near]["lands"], self.state[near]["recv"]),
            None if far is None else (self.state[far]["lands"], self.state[far]["rrecv"]),
            None if fresh is None else self.groups[fresh][2], tuple(marker) + self.token)
        self.token = (res["token"],)
        if near is not None:
            self.state[near].update(lands=res["near"], fsend=res["fsend"], frecv=res["frecv"], rsend=res["rsend"],
                                    rrecv=res["rrecv"])
        if far is not None:
            self.state[far].update(lands=res["far"], f2send=res["f2send"], f2recv=res["f2recv"])
        if fresh is not None:
            self.state[fresh] = dict(send=res["send"], recv=res["recv"], lands=res["fresh"])

    def fetch(self, layer, group, marker):
        k = self.index[(layer, group)]
        if k == 0:
            self._step("gather_step_first", 0, None, None, marker)
        self._step("gather_step_" + self._tag(k), k + 1, k, k + 2, marker)
        st = self.state[k]
        lz = _gather_finish("gather_finish_" + self._tag(k), st["lands"], st["send"], st["recv"], st["fsend"],
                            st["frecv"], st["rsend"], st["f2send"], st["f2recv"], self.token)
        self.state[k] = None
        return dict(zip(self.groups[k][1], lz))


class _GradReduce:
    def __init__(self, core, chip):
        self.core, self.chip = core, chip
        self.layer = None
        self.token = ()
        self.at_sibling, self.at_chips = [], []
        self.extra, self.smalls = (), {}

    def after(self):
        return self.token

    def add(self, group, names, grads):
        tag = "%s_%d" % (group, self.layer)
        send, recv, grads, lands, tok = _sibling_start("grad_sibling_start_" + tag, grads, self.token)
        self.at_sibling.append((tag, [(self.layer, n) for n in names], send, recv, grads, lands))
        self.token = (tok,)

    def advance(self, marker):
        for tag, keys, send, recv, grads, lands in self.at_sibling:
            grads, lands = _sibling_finish("grad_sibling_finish_" + tag, grads, lands, send, recv, marker)
            parts = _chip_partial("chip_partial_" + tag, grads, lands, self.core)
            send, recv, parts, lands, tok = _chip_start("grad_chip_start_" + tag, parts, ())
            self.at_chips.append([tag, keys, send, recv, parts, lands])
            self.token = (tok,)
        self.at_sibling = []

    def small(self, part, arrays):
        keys = list(arrays)
        send, recv, own, slots, tok = _broadcast_start(
            "small_grads_start_%d_%s" % (self.layer, part), [arrays[k] for k in keys], self.token)
        self.smalls.setdefault(self.layer, []).append((part, keys, send, recv, own, slots))
        self.token = (tok,)

    def small_finish(self, layer, marker):
        mine, theirs = {}, {}
        for part, keys, send, recv, own, slots in self.smalls[layer]:
            own, slots = _broadcast_finish("small_grads_finish_%d_%s" % (layer, part), own, slots, send, recv, marker)
            mine.update(zip(keys, own))
            theirs.update(zip(keys, slots))
        return mine, theirs

    def collect(self, key, marker):
        for entry in self.at_chips:
            tag, keys, send, recv, parts, lands = entry
            if key in keys:
                if send is not None:
                    parts, lands = _chip_finish("grad_chip_finish_" + tag, parts, lands, send, recv, marker)
                    entry[2:] = [None, None, parts, lands]
                i = keys.index(key)
                return parts[i], lands[i]
        raise KeyError(key)


def _adamw_math(w, g, m, v):
    m = ADAM_B1 * m + (1.0 - ADAM_B1) * g
    v = ADAM_B2 * v + (1.0 - ADAM_B2) * jnp.square(g)
    m_hat = m / (1.0 - ADAM_B1 ** ADAM_STEP)
    v_hat = v / (1.0 - ADAM_B2 ** ADAM_STEP)
    delta = -ADAM_LR * (m_hat / (jnp.sqrt(v_hat) + ADAM_EPS) + ADAM_WD * w)
    return delta, m, v


def _adamw_small(wts, mom_m, mom_v, own, gathered, loss_own, loss_gathered, dev):
    names = SMALL
    nw = len(names)
    na = len(SMALL_ARRAYS)

    def body(dev_ref, *refs):
        w_refs, m_refs, v_refs = (dict(zip(names, refs[i * nw:(i + 1) * nw])) for i in range(3))
        own_refs = refs[3 * nw:3 * nw + DEPTH * na]
        g_refs = refs[3 * nw + DEPTH * na:3 * nw + 2 * DEPTH * na]
        loss_own_ref, loss_got_ref = refs[3 * nw + 2 * DEPTH * na:3 * nw + 2 * DEPTH * na + 2]
        outs = refs[3 * nw + 2 * DEPTH * na + 2:]
        g_out, d_out, m_out, v_out = (dict(zip(names, outs[i * nw:(i + 1) * nw])) for i in range(4))
        me = dev_ref[0]

        loss = None
        for d in range(N_DEV):
            for b in range(loss_own.shape[0]):
                term = jnp.where(me == d, loss_own_ref[b], loss_got_ref[d, b])
                loss = term if loss is None else loss + term
        outs[4 * nw][...] = loss

        def update(name, at, g):
            g_out[name][at] = g
            d_out[name][at], m_out[name][at], v_out[name][at] = _adamw_math(
                w_refs[name][at], g, m_refs[name][at], v_refs[name][at])

        for l in range(DEPTH):
            mine = dict(zip(SMALL_ARRAYS, own_refs[l * na:(l + 1) * na]))
            got = dict(zip(SMALL_ARRAYS, g_refs[l * na:(l + 1) * na]))

            def total(key, at):
                acc = None
                for d in range(N_DEV):
                    term = jnp.where(me == d, mine[key][at] if at else mine[key][...], got[key][(d,) + at])
                    acc = term if acc is None else acc + term
                return acc

            row = (slice(l, l + 1),)
            for k, name in enumerate(NORM_NAMES):
                update(name, row, total("norms", (slice(k, k + 1),)))
            for k, name in enumerate(VEC_NAMES):
                update(name, row, total("vecs", (slice(k, k + 1),)))
            update("gmlp_v_gain", (l,), total("gain_bias", (slice(0, NH),)))
            update("b_spatial", (l,), total("gain_bias", (slice(NH, 2 * NH),)))
            update("w_spatial", (l,), total("w_spatial", ()))
            update("w_pool", (l,), total("w_pool", ()))
            update("w_dw", (l,), total("w_dw", (slice(0, CONV_K),)))

    args = [src[n] for src in (wts, mom_m, mom_v) for n in names]
    args += [src[l][k] for src in (own, gathered) for l in range(DEPTH) for k in SMALL_ARRAYS]
    args += [loss_own, loss_gathered]
    outs = pl.pallas_call(
        body, name="adamw_small",
        in_specs=[pl.BlockSpec(memory_space=pltpu.SMEM)] + [pl.BlockSpec(memory_space=pltpu.VMEM)] * len(args),
        out_shape=[jax.ShapeDtypeStruct(wts[n].shape, F32) for _ in range(4) for n in names]
        + [jax.ShapeDtypeStruct((8, LANES), F32)],
        compiler_params=_cparams(),
    )(dev, *args)
    return tuple(dict(zip(names, outs[i * nw:(i + 1) * nw])) for i in range(4)) + (outs[4 * nw],)


def _adamw_layers(name, w, reduced, m, v, chip, tr, transposed=False, after=()):
    nl, r, cdim = w.shape
    tr = _row_tile(r, tr)
    nb = r // tr

    def body(q_ref, w_ref, p0_ref, g0_ref, p1_ref, g1_ref, m_ref, v_ref, *rest):
        g_ref, d_ref, nm_ref, nv_ref = rest[len(after):]

        def total(p_ref, got_ref):
            acc = p_ref[...].astype(F32)
            for j in range(3):
                acc = acc + got_ref[j].astype(F32)
            return acc

        g = jnp.where(pl.program_id(0) == 0, total(p0_ref, g0_ref), total(p1_ref, g1_ref))
        if transposed:
            g = g.T
        g_ref[...] = g
        d_ref[...], nm_ref[...], nv_ref[...] = _adamw_math(w_ref[...], g, m_ref[...], v_ref[...])

    blk = pl.BlockSpec((None, tr, cdim), lambda l, i, q: (l, i, 0))
    first = lambda l, i: i * (1 - l) + (nb - 1) * l
    second = lambda l, i: i * l
    if transposed:
        gshape = (cdim, tr)
        at = lambda lead, i: (lead, 0, i)
    else:
        gshape = (tr, cdim)
        at = lambda lead, i: (lead, i, 0)
    specs = [blk,
             pl.BlockSpec((None,) + gshape, lambda l, i, q: at(q[0], first(l, i))),
             pl.BlockSpec((3,) + gshape, lambda l, i, q: at(0, first(l, i))),
             pl.BlockSpec((None,) + gshape, lambda l, i, q: at(q[0], second(l, i))),
             pl.BlockSpec((3,) + gshape, lambda l, i, q: at(0, second(l, i))), blk, blk] + [ANY] * len(after)
    shape = jax.ShapeDtypeStruct((nl, r, cdim), F32)
    return pl.pallas_call(
        body, name=name,
        grid_spec=pltpu.PrefetchScalarGridSpec(num_scalar_prefetch=1, grid=(nl, nb), in_specs=specs, out_specs=[blk] * 4),
        out_shape=[shape] * 4, compiler_params=_cparams(),
    )(chip, w, *reduced[0], *reduced[1], m, v, *after)


def _to_rows(name, a):
    return jnp.swapaxes(a, 1, 2) if name == "w_in" else a


def _place_own_transposed(name, srcs, dev, out_dtype, tc):
    n = len(srcs)
    kdim, cdim = srcs[0][0].shape[-2:]

    def body(dev_ref, *refs):
        for a in range(n):
            refs[n + a][...] = refs[a][...].T.astype(out_dtype)

    return pl.pallas_call(
        body, name=name,
        grid_spec=pltpu.PrefetchScalarGridSpec(
            num_scalar_prefetch=1, grid=(kdim // tc,),
            in_specs=[pl.BlockSpec((None, tc, cdim), lambda i, d, l=l: (l, i, 0)) for _, l in srcs],
            out_specs=[pl.BlockSpec((cdim, tc), lambda i, d: (d[0], i))] * n),
        out_shape=[jax.ShapeDtypeStruct((N_DEV * cdim, kdim), out_dtype)] * n, compiler_params=_cparams(),
    )(dev, *[a for a, _ in srcs])


def _pack(arrays, rows):
    flat = jnp.concatenate([a.reshape(-1) for a in arrays])
    return jnp.pad(flat, (0, rows * D - flat.shape[0])).reshape(rows, D)


def _rows_for(shapes, mult=8):
    total = 0
    for shp in shapes:
        size = 1
        for dim in shp:
            size *= dim
        total += size
    return -(-total // (mult * D)) * mult


def kernel(x, mem, norm_mix_pre, norm_mix_post, w_in, w_out, gmlp_v_gain, w_spatial, b_spatial, w_pool, s_pool, w_dw, b_dw, conv_ln_g, conv_ln_b, norm_xattn_pre, norm_mem, norm_xattn_post, w_q, w_k, w_v, w_o, norm_ffn_pre, norm_ffn_post, w_up, w_down, loss_target, m_norm_mix_pre, m_norm_mix_post, m_w_in, m_w_out, m_gmlp_v_gain, m_w_spatial, m_b_spatial, m_w_pool, m_s_pool, m_w_dw, m_b_dw, m_conv_ln_g, m_conv_ln_b, m_norm_xattn_pre, m_norm_mem, m_norm_xattn_post, m_w_q, m_w_k, m_w_v, m_w_o, m_norm_ffn_pre, m_norm_ffn_post, m_w_up, m_w_down, v_norm_mix_pre, v_norm_mix_post, v_w_in, v_w_out, v_gmlp_v_gain, v_w_spatial, v_b_spatial, v_w_pool, v_s_pool, v_w_dw, v_b_dw, v_conv_ln_g, v_conv_ln_b, v_norm_xattn_pre, v_norm_mem, v_norm_xattn_post, v_w_q, v_w_k, v_w_v, v_w_o, v_norm_ffn_pre, v_norm_ffn_post, v_w_up, v_w_down):
    args = dict(locals())
    wts = {n: args[n] for n in WEIGHTS}
    mom_m = {n: args["m_" + n] for n in WEIGHTS}
    mom_v = {n: args["v_" + n] for n in WEIGHTS}
    xi, yi, ci = _position()
    me = 4 * xi + 2 * yi + ci

    dev = jnp.reshape(me, (1,)).astype(jnp.int32)
    lands = {}
    for call, names, tr in (("place_in", ("w_in",), 256), ("place_att", ("w_out", "w_q", "w_k", "w_v", "w_o"), 64),
                            ("place_up", ("w_up",), 256), ("place_down", ("w_down",), 256)):
        srcs = [(_to_rows(n, wts[n]), l) for l in range(DEPTH) for n in names]
        placed = (_place_own_transposed if names == ("w_up",) else _place_own)(call, srcs, dev, BF16, tr)
        lands.update(zip([(l, n) for l in range(DEPTH) for n in names], placed))
    (lands[(0, "taps")],) = _place_own("place_taps", [(_pack([w_dw], _rows_for([w_dw.shape])), None)], dev, F32, 8)
    groups = []
    for l in range(DEPTH):
        for group, names in GATHER_GROUPS:
            if (l, group) == (0, "in"):
                names = names + ("taps",)
            groups.append(((l, group), names, [lands[(l, n)] for n in names]))
    gather = _WeightGather(groups)

    def fetch(layer, group, marker):
        w = gather.fetch(layer, group, marker)
        if "taps" in w:
            blocks = w["taps"].reshape(N_DEV, -1)[:, :w_dw.size].reshape((N_DEV,) + w_dw.shape)
            w["taps"] = jnp.moveaxis(blocks, 0, 2).reshape(DEPTH, CONV_K, CW)
        return w

    reduce = _GradReduce(jnp.reshape(ci, (1,)).astype(jnp.int32), jnp.reshape(2 * xi + yi, (1,)).astype(jnp.int32))
    small = {n: wts[n] for n in SMALL if n != "w_dw"}
    _, dx = _local_step(x[0], mem[0], loss_target[0], fetch, small, reduce)
    reduce.advance((dx,))

    grad_w, delta, new_m, new_v = {}, {}, {}, {}
    marker = (dx,) + tuple(reduce.after())
    for n in UPDATE_ORDER:
        reduced = [reduce.collect((l, n), marker) for l in range(DEPTH)]
        outs = _adamw_layers("adamw_" + n, _to_rows(n, wts[n]), reduced, _to_rows(n, mom_m[n]), _to_rows(n, mom_v[n]),
                             reduce.chip, 256, transposed=n == "w_up", after=marker)
        grad_w[n], delta[n], new_m[n], new_v[n] = (_to_rows(n, o) for o in outs)
        marker = (outs[1],)

    own, slots = [None] * DEPTH, [None] * DEPTH
    for l in reversed(range(DEPTH)):
        own[l], slots[l] = reduce.small_finish(l, marker)
        if l == 0:
            loss_own, loss_slots = own[l].pop("loss"), slots[l].pop("loss")
    shard_cols = CW // N_DEV
    for l in range(DEPTH):
        own[l]["w_dw"] = lax.dynamic_slice_in_dim(own[l]["w_dw"], me * shard_cols, shard_cols, axis=1)
        slots[l]["w_dw"] = lax.dynamic_slice_in_dim(slots[l]["w_dw"], me * shard_cols, shard_cols, axis=2)
    *small_out, loss_tile = _adamw_small(wts, mom_m, mom_v, own, slots, loss_own, loss_slots, dev)
    for dst, src in zip((grad_w, delta, new_m, new_v), small_out):
        dst.update(src)

    return (loss_tile[0, 0], dx[None], *[grad_w[n] for n in WEIGHTS], *[delta[n] for n in WEIGHTS],
            *[new_m[n] for n in WEIGHTS], *[new_v[n] for n in WEIGHTS])
```

```python
import functools

import jax
import jax.numpy as jnp
from jax import lax
from jax.experimental import pallas as pl
from jax.experimental.pallas import tpu as pltpu

F32 = jnp.float32
BF16 = jnp.bfloat16

D = 2048
GW = 1024
PW = 512
CW = 512
HD = 128
NH = 8
NG = 4
POOL_WINDOWS = (2, 4, 8, 16)
CONV_K = 31
IN_COLS = 2 * GW + PW + 2 * CW
XH = 4
XHD = D // XH
ATT_SCALE = XHD ** -0.5
RMS_EPS = 1e-6
LN_EPS = 1e-5
DEPTH = 2
N_DEV = 8

ADAM_LR = 0.001
ADAM_B1 = 0.9
ADAM_B2 = 0.999
ADAM_EPS = 1e-08
ADAM_WD = 0.01
ADAM_STEP = 10

LANES = 128
CONV_HALO = 32
POOL_HALO = 16
ROW_TILE = 128
VMEM_LIMIT = 60 * 1024 * 1024

MESH = pl.DeviceIdType.MESH
NT = (((1,), (1,)), ((), ()))
NN = (((1,), (0,)), ((), ()))
TN = (((0,), (0,)), ((), ()))

UPDATE_ORDER = ("w_down", "w_up", "w_o", "w_q", "w_k", "w_v", "w_out", "w_in")
GATHER_GROUPS = (("in", ("w_in",)), ("out", ("w_out",)), ("att", ("w_q", "w_k", "w_v", "w_o")), ("up", ("w_up",)),
                 ("down", ("w_down",)))
SMALL = ("norm_mix_pre", "norm_mix_post", "gmlp_v_gain", "w_spatial", "b_spatial", "w_pool", "s_pool",
         "w_dw", "b_dw", "conv_ln_g", "conv_ln_b", "norm_xattn_pre", "norm_mem", "norm_xattn_post",
         "norm_ffn_pre", "norm_ffn_post")
WEIGHTS = ("norm_mix_pre", "norm_mix_post", "w_in", "w_out", "gmlp_v_gain", "w_spatial", "b_spatial", "w_pool",
           "s_pool", "w_dw", "b_dw", "conv_ln_g", "conv_ln_b", "norm_xattn_pre", "norm_mem", "norm_xattn_post",
           "w_q", "w_k", "w_v", "w_o", "norm_ffn_pre", "norm_ffn_post", "w_up", "w_down")


def _cparams():
    return pltpu.CompilerParams(vmem_limit_bytes=VMEM_LIMIT)


def _dot(a, b, dims):
    return lax.dot_general(a, b, dims, preferred_element_type=F32)


def _rms(x, g):
    y = x * lax.rsqrt(jnp.mean(x * x, axis=-1, keepdims=True) + RMS_EPS)
    return y * g


def _rms_bwd(x, g, dy):
    r = lax.rsqrt(jnp.mean(x * x, axis=-1, keepdims=True) + RMS_EPS)
    xh = x * r
    t = dy * g
    dx = r * (t - xh * jnp.mean(t * xh, axis=-1, keepdims=True))
    return dx, jnp.sum(dy * xh, axis=0, keepdims=True)


def _gelu(x):
    cdf = 0.5 * (1.0 + jnp.tanh(0.7978845608028654 * (x + 0.044715 * (x * x * x))))
    return x * cdf


def _layer_norm(x, g, b=None):
    mu = jnp.mean(x, axis=-1, keepdims=True)
    xc = x - mu
    var = jnp.mean(xc * xc, axis=-1, keepdims=True)
    y = xc * lax.rsqrt(var + LN_EPS) * g
    return y if b is None else y + b


def _sigmoid(x):
    return 1.0 / (1.0 + jnp.exp(-x))


def _gmlp_rows(zu, zv, gv):
    return _gelu(zu), _layer_norm(_gelu(zv), gv)


def _glu(cv, cg):
    return cv * _sigmoid(cg)


def _ln_silu(h, g, b):
    y = _layer_norm(h, g, b)
    return y * _sigmoid(y)


ANY = pl.BlockSpec(memory_space=pl.ANY)


ROWS_TILE = 256
COLS_TILE = 512
DW_TILE = 512
RESIDENT_K = 2048
RESIDENT_ROWS = 512
STREAM_K_TILE = 1024
STREAM_ROWS = 512


def _k_tiles(kdim):
    if kdim <= RESIDENT_K:
        return RESIDENT_ROWS, kdim
    return STREAM_ROWS, max(t for t in range(LANES, STREAM_K_TILE + 1, LANES) if kdim % t == 0)


def _rowop_mm(name, kind, rows, g, w, dims, out_dtype, u=None, after=()):
    s = rows[0].shape[0]
    n = w.shape[0] if dims == NT else w.shape[1]
    resident = n <= IN_COLS and u is None
    tm, tn = min(RESIDENT_ROWS if resident and n <= RESIDENT_K else ROWS_TILE, s), min(COLS_TILE, n)
    ni, nj = s // tm, n // tn
    bwd = kind == "rms_bwd"
    out_shape = [jax.ShapeDtypeStruct((s, n), out_dtype), jax.ShapeDtypeStruct((s, D), BF16)]
    if bwd:
        out_shape.append(jax.ShapeDtypeStruct((ni, 1, D), F32))

    if resident:
        def row_body(*refs):
            refs = list(refs)
            row_refs = [refs.pop(0) for _ in rows]
            g_ref, w_ref = refs.pop(0), refs.pop(0)
            del refs[:len(after)]
            if bwd:
                a, dg = _rms_bwd(row_refs[0][...], g_ref[...], row_refs[1][...])
                refs[2][0] = dg
            else:
                a = _rms(row_refs[0][...], g_ref[...])
            a = a.astype(BF16)
            refs[1][...] = a
            refs[0][...] = _dot(a, w_ref[...], dims).astype(out_dtype)

        blk = pl.BlockSpec((tm, D), lambda i: (i, 0))
        return pl.pallas_call(
            row_body, name=name, grid=(ni,),
            in_specs=[blk] * len(rows) + [pl.BlockSpec((1, D), lambda i: (0, 0)),
                                          pl.BlockSpec(w.shape, lambda i: (0, 0), pipeline_mode=pl.Buffered(1))]
            + [ANY] * len(after),
            out_specs=[pl.BlockSpec((tm, n), lambda i: (i, 0)), blk]
            + ([pl.BlockSpec((1, 1, D), lambda i: (i, 0, 0))] if bwd else []),
            out_shape=out_shape, compiler_params=_cparams(),
        )(*rows, g, w, *after)

    def body(*refs):
        refs = list(refs)
        row_refs = [refs.pop(0) for _ in rows]
        g_ref, w_ref = refs.pop(0), refs.pop(0)
        u_ref = refs.pop(0) if u is not None else None
        del refs[:len(after)]
        out_ref, a_ref = refs.pop(0), refs.pop(0)
        dg_ref = refs.pop(0) if bwd else None
        a_all = refs.pop(0)
        t = pl.program_id(0)

        @pl.when(t < ni)
        def _():
            if bwd:
                a, dg = _rms_bwd(row_refs[0][...], g_ref[...], row_refs[1][...])
                dg_ref[0] = dg
            else:
                a = _rms(row_refs[0][...], g_ref[...])
            a_ref[...] = a.astype(BF16)
            a_all[pl.ds(pl.multiple_of(t * tm, tm), tm), :] = a.astype(BF16)

        @pl.when(t >= ni)
        def _():
            acc = _dot(a_all[...], w_ref[...], dims)
            if u_ref is not None:
                acc = acc * (2.0 * jnp.maximum(u_ref[...], 0.0))
            out_ref[...] = acc.astype(out_dtype)

    rows_at = lambda t: jnp.minimum(t, ni - 1)
    cols_at = lambda t: jnp.maximum(t - ni, 0)
    row_spec = pl.BlockSpec((tm, D), lambda t: (rows_at(t), 0))
    w_spec = (pl.BlockSpec((tn, D), lambda t: (cols_at(t), 0)) if dims == NT
              else pl.BlockSpec((D, tn), lambda t: (0, cols_at(t))))
    tile = pl.BlockSpec((s, tn), lambda t: (0, cols_at(t)))
    in_specs = [row_spec] * len(rows) + [pl.BlockSpec((1, D), lambda t: (0, 0)), w_spec]
    in_specs += ([tile] if u is not None else []) + [ANY] * len(after)
    out_specs = [tile, row_spec]
    if bwd:
        out_specs.append(pl.BlockSpec((1, 1, D), lambda t: (rows_at(t), 0, 0)))
    return pl.pallas_call(
        body, name=name, grid=(ni + nj,), in_specs=in_specs, out_specs=out_specs, out_shape=out_shape,
        scratch_shapes=[pltpu.VMEM((s, D), BF16)], compiler_params=_cparams(),
    )(*rows, g, w, *([u] if u is not None else []), *after)


def _mm_rowop(name, kind, pairs, rows, g, relu2=False, after=()):
    s, kdim = pairs[0][0].shape
    tm, tk = _k_tiles(kdim)
    tm = min(tm, s)
    ni, nk = s // tm, kdim // tk
    npair = len(pairs)

    def body(*refs):
        refs = list(refs)
        a_refs = [refs.pop(0) for _ in range(npair)]
        w_refs = [refs.pop(0) for _ in range(npair)]
        row_refs = [refs.pop(0) for _ in rows]
        g_ref = refs.pop(0)
        del refs[:len(after)]
        acc = refs.pop() if nk > 1 else None
        outs = refs
        k = pl.program_id(1)

        def product():
            total = None
            for a_ref, w_ref, (_, _, dims) in zip(a_refs, w_refs, pairs):
                a = a_ref[...]
                if relu2:
                    a = jnp.square(jnp.maximum(a, 0.0))
                term = _dot(a.astype(BF16), w_ref[...], dims)
                total = term if total is None else total + term
            return total

        def finish(h):
            if kind == "rms_res":
                outs[0][...] = row_refs[0][...] + _rms(h, g_ref[...])
                outs[1][...] = h
            else:
                dx, dg = _rms_bwd(row_refs[0][...], g_ref[...], h)
                if kind == "rms_bwd_res":
                    outs[0][...] = row_refs[1][...] + dx
                    outs[1][0] = dg
                else:
                    outs[0][0] = dg

        if nk == 1:
            finish(product())
            return

        @pl.when(k == 0)
        def _():
            acc[...] = jnp.zeros_like(acc)

        acc[...] += product()

        @pl.when(k == nk - 1)
        def _():
            finish(acc[...])

    row_spec = pl.BlockSpec((tm, D), lambda i, k: (i, 0))
    dg_shape = jax.ShapeDtypeStruct((ni, 1, D), F32)
    dg_spec = pl.BlockSpec((1, 1, D), lambda i, k: (i, 0, 0))
    in_specs = [pl.BlockSpec((tm, tk), lambda i, k: (i, k))] * npair
    for _, _, dims in pairs:
        mode = dict(pipeline_mode=pl.Buffered(1)) if nk == 1 else {}
        in_specs.append(pl.BlockSpec((tk, D), lambda i, k: (k, 0), **mode) if dims == NN
                        else pl.BlockSpec((D, tk), lambda i, k: (0, k), **mode))
    in_specs += [row_spec] * len(rows) + [pl.BlockSpec((1, D), lambda i, k: (0, 0))] + [ANY] * len(after)
    if kind == "rms_res":
        out_shape = [jax.ShapeDtypeStruct((s, D), F32)] * 2
        out_specs = [row_spec, row_spec]
    elif kind == "rms_bwd_res":
        out_shape = [jax.ShapeDtypeStruct((s, D), F32), dg_shape]
        out_specs = [row_spec, dg_spec]
    else:
        out_shape = [dg_shape]
        out_specs = [dg_spec]
    return pl.pallas_call(
        body, name=name, grid=(ni, nk), in_specs=in_specs, out_specs=out_specs, out_shape=out_shape,
        scratch_shapes=[pltpu.VMEM((tm, D), F32)] if nk > 1 else [], compiler_params=_cparams(),
    )(*[p[0] for p in pairs], *[p[1] for p in pairs], *rows, g, *after)


def _mm_tn(name, a, gmat, relu2=False, after=()):
    s, m = a.shape
    tm = min(DW_TILE, m)
    ni = m // tm

    def body(a_ref, g_ref, *rest):
        av = a_ref[...]
        if relu2:
            av = jnp.square(jnp.maximum(av, 0.0))
        rest[len(after)][...] = _dot(av.astype(BF16), g_ref[...], TN).astype(BF16)

    return pl.pallas_call(
        body, name=name, grid=(ni,),
        in_specs=[pl.BlockSpec((s, tm), lambda i: (0, i)), pl.BlockSpec((s, D), lambda i: (0, 0))] + [ANY] * len(after),
        out_specs=pl.BlockSpec((tm, D), lambda i: (i, 0)),
        out_shape=jax.ShapeDtypeStruct((m, D), BF16), compiler_params=_cparams(),
    )(a, gmat, *after)


def _tril():
    r = lax.broadcasted_iota(jnp.int32, (HD, HD), 0)
    c = lax.broadcasted_iota(jnp.int32, (HD, HD), 1)
    return (c <= r).astype(F32)


def _gmlp_fwd(z, gv, ws, bst, tb):
    s = z.shape[0]
    tb = min(tb, s)

    def body(zu_ref, zv_ref, gv_ref, ws_ref, bst_ref, y_ref):
        tril = _tril()
        for h in range(NH):
            cols = slice(h * HD, (h + 1) * HD)
            u, vln = _gmlp_rows(zu_ref[:, cols], zv_ref[:, cols], gv_ref[h:h + 1, :])
            wm = (ws_ref[h] * tril).astype(BF16)
            vb = vln.astype(BF16)
            for c in range(tb // HD):
                rws = slice(c * HD, (c + 1) * HD)
                mixed = _dot(wm, vb[rws], NN) + bst_ref[:, h:h + 1]
                y_ref[rws, cols] = (u[rws] * mixed).astype(BF16)

    return pl.pallas_call(
        body, name="gmlp_fwd", grid=(s // tb,),
        in_specs=[pl.BlockSpec((tb, GW), lambda i: (i, 0)), pl.BlockSpec((tb, GW), lambda i: (i, 1)),
                  pl.BlockSpec((NH, HD), lambda i: (0, 0)), pl.BlockSpec((NH, HD, HD), lambda i: (0, 0, 0)),
                  pl.BlockSpec((HD, NH), lambda i: (0, 0))],
        out_specs=pl.BlockSpec((tb, GW), lambda i: (i, 0)),
        out_shape=jax.ShapeDtypeStruct((s, D), BF16), compiler_params=_cparams(),
    )(z, z, gv, ws, bst)


def _gmlp_bwd(z, dy, gv, ws, bst, tb, after=()):
    s = z.shape[0]
    tb = min(tb, s)
    nb = s // tb

    def body(zu_ref, zv_ref, dy_ref, gv_ref, ws_ref, bst_ref, *rest):
        dz_ref, dgv_ref, dws_ref, db_ref = rest[len(after):]
        tril = _tril()
        for h in range(NH):
            cols = slice(h * HD, (h + 1) * HD)
            (u, vln), vjp = jax.vjp(_gmlp_rows, zu_ref[:, cols], zv_ref[:, cols], gv_ref[h:h + 1, :])
            wmf = ws_ref[h] * tril
            wm = wmf.astype(BF16)
            wmt = wmf.T.astype(BF16)
            vb = vln.astype(BF16)
            dws = jnp.zeros((HD, HD), F32)
            db = jnp.zeros((HD, 1), F32)
            du_parts, dvln_parts = [], []
            for c in range(tb // HD):
                rws = slice(c * HD, (c + 1) * HD)
                mixed = _dot(wm, vb[rws], NN) + bst_ref[:, h:h + 1]
                dyc = dy_ref[rws, cols]
                du_parts.append(dyc * mixed)
                dmixed = dyc * u[rws]
                dmb = dmixed.astype(BF16)
                dws = dws + _dot(dmb, vb[rws], NT)
                db = db + jnp.sum(dmixed, axis=1, keepdims=True)
                dvln_parts.append(_dot(wmt, dmb, NN))
            du = jnp.concatenate(du_parts, axis=0)
            dvln = jnp.concatenate(dvln_parts, axis=0)
            dzu, dzv, dgv = vjp((du, dvln))
            dz_ref[:, cols] = dzu.astype(BF16)
            dz_ref[:, slice(GW + h * HD, GW + (h + 1) * HD)] = dzv.astype(BF16)
            dgv_ref[0, h:h + 1, :] = dgv
            dws_ref[0, h] = dws * tril
            db_ref[0, h] = jnp.broadcast_to(db, (HD, LANES))

    blk = pl.BlockSpec((tb, GW), lambda i: (i, 0))
    return pl.pallas_call(
        body, name="gmlp_bwd", grid=(nb,),
        in_specs=[blk, pl.BlockSpec((tb, GW), lambda i: (i, 1)), blk,
                  pl.BlockSpec((NH, HD), lambda i: (0, 0)), pl.BlockSpec((NH, HD, HD), lambda i: (0, 0, 0)),
                  pl.BlockSpec((HD, NH), lambda i: (0, 0))] + [ANY] * len(after),
        out_specs=[pl.BlockSpec((tb, 2 * GW), lambda i: (i, 0)), pl.BlockSpec((1, NH, HD), lambda i: (i, 0, 0)),
                   pl.BlockSpec((1, NH, HD, HD), lambda i: (i, 0, 0, 0)),
                   pl.BlockSpec((1, NH, HD, LANES), lambda i: (i, 0, 0, 0))],
        out_shape=[jax.ShapeDtypeStruct((s, IN_COLS), BF16),
                   jax.ShapeDtypeStruct((nb, NH, HD), F32), jax.ShapeDtypeStruct((nb, NH, HD, HD), F32),
                   jax.ShapeDtypeStruct((nb, NH, HD, LANES), F32)],
        compiler_params=_cparams(),
    )(z, z, dy, gv, ws, bst, *after)


POOL_TILE = 1024


def _pool_count(t0, window):
    pos = (t0 + lax.broadcasted_iota(jnp.int32, (POOL_TILE, LANES), 0)).astype(F32)
    return jnp.minimum(pos + 1.0, float(window))


def _window_sum(win, levels, back):
    n = win.shape[0]
    for lv in range(levels):
        step = 1 << lv
        win = win + pltpu.roll(win, n - step if back else step, 0)
    return win


def _pool_pooled(ppad_ref, t0, g):
    win = ppad_ref[pl.ds(t0, POOL_TILE + POOL_HALO), :]
    wsum = _window_sum(win, g + 1, False)[POOL_HALO:]
    return wsum / _pool_count(t0, POOL_WINDOWS[g]) - win[POOL_HALO:]


def _pool_fwd(z, wp, sp, y):
    s = z.shape[0]
    nt = s // POOL_TILE

    def body(p_ref, wp_ref, sp_ref, _, y_ref, ppad):
        for g in range(NG):
            cols = slice(g * LANES, (g + 1) * LANES)
            ppad[pl.ds(0, POOL_HALO), :] = jnp.zeros((POOL_HALO, LANES), F32)
            ppad[pl.ds(POOL_HALO, s), :] = p_ref[:, cols]
            wpb = wp_ref[g].astype(BF16)
            scale = sp_ref[:, cols]

            def tile(t, carry):
                t0 = pl.multiple_of(t * POOL_TILE, POOL_TILE)
                pooled = _pool_pooled(ppad, t0, g)
                y_ref[pl.ds(t0, POOL_TILE), cols] = (_dot(pooled.astype(BF16), wpb, NN) * scale).astype(BF16)
                return carry

            lax.fori_loop(0, nt, tile, 0)

    return pl.pallas_call(
        body, name="pool_fwd", grid=(1,),
        in_specs=[pl.BlockSpec((s, PW), lambda i: (0, 2 * GW // PW)),
                  pl.BlockSpec((NG, LANES, LANES), lambda i: (0, 0, 0)), pl.BlockSpec((1, PW), lambda i: (0, 0)), ANY],
        out_specs=pl.BlockSpec((s, PW), lambda i: (0, GW // PW)),
        out_shape=jax.ShapeDtypeStruct((s, D), BF16), input_output_aliases={3: 0},
        scratch_shapes=[pltpu.VMEM((s + POOL_HALO, LANES), F32)], compiler_params=_cparams(),
    )(z, wp, sp, y)


def _pool_bwd(z, dy, wp, sp, dz):
    s = z.shape[0]
    nt = s // POOL_TILE

    def body(p_ref, dy_ref, wp_ref, sp_ref, _, dp_ref, dwp_ref, dsp_ref, ppad, rpad, dpool):
        for g in range(NG):
            cols = slice(g * LANES, (g + 1) * LANES)
            ppad[pl.ds(0, POOL_HALO), :] = jnp.zeros((POOL_HALO, LANES), F32)
            ppad[pl.ds(POOL_HALO, s), :] = p_ref[:, cols]
            rpad[pl.ds(s, POOL_HALO), :] = jnp.zeros((POOL_HALO, LANES), F32)
            wpb = wp_ref[g].astype(BF16)
            scale = sp_ref[:, cols]

            def tile(t, carry):
                dwp, dsp = carry
                t0 = pl.multiple_of(t * POOL_TILE, POOL_TILE)
                pooled = _pool_pooled(ppad, t0, g)
                pb = pooled.astype(BF16)
                dyt = dy_ref[pl.ds(t0, POOL_TILE), cols]
                dsp = dsp + jnp.sum(dyt * _dot(pb, wpb, NN), axis=0, keepdims=True)
                dmm = (dyt * scale).astype(BF16)
                dwp = dwp + _dot(pb, dmm, TN)
                dpooled = _dot(dmm, wpb, NT)
                rpad[pl.ds(t0, POOL_TILE), :] = dpooled / _pool_count(t0, POOL_WINDOWS[g])
                dpool[pl.ds(t0, POOL_TILE), :] = dpooled
                return dwp, dsp

            dwp, dsp = lax.fori_loop(0, nt, tile, (jnp.zeros((LANES, LANES), F32), jnp.zeros((1, LANES), F32)))
            dwp_ref[g] = dwp
            dsp_ref[:, cols] = dsp

            def tile2(t, carry):
                t0 = pl.multiple_of(t * POOL_TILE, POOL_TILE)
                win = rpad[pl.ds(t0, POOL_TILE + POOL_HALO), :]
                back = _window_sum(win, g + 1, True)[:POOL_TILE]
                rows = pl.ds(t0, POOL_TILE)
                dp_ref[rows, cols] = (back - dpool[rows, :]).astype(BF16)
                return carry

            lax.fori_loop(0, nt, tile2, 0)

    return pl.pallas_call(
        body, name="pool_bwd", grid=(1,),
        in_specs=[pl.BlockSpec((s, PW), lambda i: (0, 2 * GW // PW)), pl.BlockSpec((s, PW), lambda i: (0, GW // PW)),
                  pl.BlockSpec((NG, LANES, LANES), lambda i: (0, 0, 0)), pl.BlockSpec((1, PW), lambda i: (0, 0)), ANY],
        out_specs=[pl.BlockSpec((s, PW), lambda i: (0, 2 * GW // PW)),
                   pl.BlockSpec((NG, LANES, LANES), lambda i: (0, 0, 0)), pl.BlockSpec((1, PW), lambda i: (0, 0))],
        out_shape=[jax.ShapeDtypeStruct((s, IN_COLS), BF16), jax.ShapeDtypeStruct((NG, LANES, LANES), F32),
                   jax.ShapeDtypeStruct((1, PW), F32)],
        input_output_aliases={4: 0},
        scratch_shapes=[pltpu.VMEM((s + POOL_HALO, LANES), F32), pltpu.VMEM((s + POOL_HALO, LANES), F32),
                        pltpu.VMEM((s, LANES), F32)],
        compiler_params=_cparams(),
    )(z, dy, wp, sp, dz)


CONV_LEAD = CONV_HALO - (CONV_K - 1)


SUBLANES = 8


def _sublane_shifts(win):
    n = win.shape[0]
    return [win] + [pltpu.roll(win, n - b, 0) for b in range(1, SUBLANES)]


def _shifted(shifts, offset):
    a, b = divmod(offset, SUBLANES)
    return shifts[b][a * SUBLANES:a * SUBLANES + ROW_TILE]


def _conv_taps(shifts, wdw_ref, lead, reverse):
    acc = jnp.zeros((ROW_TILE, CW), F32)
    for j in range(CONV_K):
        tap = (CONV_K - 1 - j) if reverse else j
        acc = acc + wdw_ref[tap:tap + 1, :] * _shifted(shifts, lead + j)
    return acc


def _conv_fill_glu(cv_ref, cg_ref, xpad, s):
    xpad[pl.ds(0, CONV_HALO), :] = jnp.zeros((CONV_HALO, CW), F32)

    def fill(t, carry):
        t0 = pl.multiple_of(t * ROW_TILE, ROW_TILE)
        rows = pl.ds(t0, ROW_TILE)
        xpad[pl.ds(t0 + CONV_HALO, ROW_TILE), :] = _glu(cv_ref[rows, :], cg_ref[rows, :])
        return carry

    lax.fori_loop(0, s // ROW_TILE, fill, 0)


def _conv_fwd(z, wdw, bdw, lng, lnb, y):
    s = z.shape[0]

    def body(cv_ref, cg_ref, wdw_ref, bdw_ref, lng_ref, lnb_ref, _, y_ref, xpad):
        _conv_fill_glu(cv_ref, cg_ref, xpad, s)

        def tile(t, carry):
            t0 = pl.multiple_of(t * ROW_TILE, ROW_TILE)
            shifts = _sublane_shifts(xpad[pl.ds(t0, ROW_TILE + CONV_HALO), :])
            hc = _conv_taps(shifts, wdw_ref, CONV_LEAD, False) + bdw_ref[...]
            y_ref[pl.ds(t0, ROW_TILE), :] = _ln_silu(hc, lng_ref[...], lnb_ref[...]).astype(BF16)
            return carry

        lax.fori_loop(0, s // ROW_TILE, tile, 0)

    vec = pl.BlockSpec((1, CW), lambda i: (0, 0))
    return pl.pallas_call(
        body, name="conv_fwd", grid=(1,),
        in_specs=[pl.BlockSpec((s, CW), lambda i: (0, (2 * GW + PW) // CW)),
                  pl.BlockSpec((s, CW), lambda i: (0, (2 * GW + PW) // CW + 1)),
                  pl.BlockSpec((CONV_K + 1, CW), lambda i: (0, 0)), vec, vec, vec, ANY],
        out_specs=pl.BlockSpec((s, CW), lambda i: (0, (GW + PW) // CW)),
        out_shape=jax.ShapeDtypeStruct((s, D), BF16), input_output_aliases={6: 0},
        scratch_shapes=[pltpu.VMEM((s + CONV_HALO, CW), F32)], compiler_params=_cparams(),
    )(z, z, wdw, bdw, lng, lnb, y)


def _conv_bwd(z, dy, wdw, bdw, lng, lnb, dz):
    s = z.shape[0]

    def body(cv_ref, cg_ref, dy_ref, wdw_ref, bdw_ref, lng_ref, lnb_ref, _,
             dz_ref, dwdw_ref, dbdw_ref, dlng_ref, dlnb_ref, xpad, dpad, dcg_keep):
        @pl.when(pl.program_id(0) == 0)
        def _():
            compute(cv_ref, cg_ref, dy_ref, wdw_ref, bdw_ref, lng_ref, lnb_ref,
                    dz_ref, dcg_keep, dwdw_ref, dbdw_ref, dlng_ref, dlnb_ref, xpad, dpad)

        @pl.when(pl.program_id(0) == 1)
        def _():
            dz_ref[...] = dcg_keep[...]

    def compute(cv_ref, cg_ref, dy_ref, wdw_ref, bdw_ref, lng_ref, lnb_ref,
                dcv_ref, dcg_ref, dwdw_ref, dbdw_ref, dlng_ref, dlnb_ref, xpad, dpad):
        _conv_fill_glu(cv_ref, cg_ref, xpad, s)
        dpad[pl.ds(s, CONV_HALO), :] = jnp.zeros((CONV_HALO, CW), F32)
        dwdw_ref[...] = jnp.zeros((CONV_K + 1, CW), F32)

        def tile(t, carry):
            db, dg, dbeta = carry
            t0 = pl.multiple_of(t * ROW_TILE, ROW_TILE)
            shifts = _sublane_shifts(xpad[pl.ds(t0, ROW_TILE + CONV_HALO), :])
            hc = _conv_taps(shifts, wdw_ref, CONV_LEAD, False) + bdw_ref[...]
            _, vjp = jax.vjp(_ln_silu, hc, lng_ref[...], lnb_ref[...])
            dhc, dg_t, dbeta_t = vjp(dy_ref[pl.ds(t0, ROW_TILE), :])
            dpad[pl.ds(t0, ROW_TILE), :] = dhc
            for j in range(CONV_K):
                dwdw_ref[j:j + 1, :] += jnp.sum(dhc * _shifted(shifts, CONV_LEAD + j), axis=0, keepdims=True)
            return db + jnp.sum(dhc, axis=0, keepdims=True), dg + dg_t, dbeta + dbeta_t

        zero = jnp.zeros((1, CW), F32)
        db, dg, dbeta = lax.fori_loop(0, s // ROW_TILE, tile, (zero, zero, zero))
        dbdw_ref[...] = db
        dlng_ref[...] = dg
        dlnb_ref[...] = dbeta

        def tile2(t, carry):
            t0 = pl.multiple_of(t * ROW_TILE, ROW_TILE)
            rows = pl.ds(t0, ROW_TILE)
            dglu = _conv_taps(_sublane_shifts(dpad[pl.ds(t0, ROW_TILE + CONV_HALO), :]), wdw_ref, 0, True)
            _, vjp = jax.vjp(_glu, cv_ref[rows, :], cg_ref[rows, :])
            dcv, dcg = vjp(dglu)
            dcv_ref[rows, :] = dcv.astype(BF16)
            dcg_ref[rows, :] = dcg.astype(BF16)
            return carry

        lax.fori_loop(0, s // ROW_TILE, tile2, 0)

    vec = pl.BlockSpec((1, CW), lambda i: (0, 0))
    wspec = pl.BlockSpec((CONV_K + 1, CW), lambda i: (0, 0))
    vshape = jax.ShapeDtypeStruct((1, CW), F32)
    return pl.pallas_call(
        body, name="conv_bwd", grid=(2,),
        in_specs=[pl.BlockSpec((s, CW), lambda i: (0, (2 * GW + PW) // CW)),
                  pl.BlockSpec((s, CW), lambda i: (0, (2 * GW + PW) // CW + 1)),
                  pl.BlockSpec((s, CW), lambda i: (0, (GW + PW) // CW)), wspec, vec, vec, vec, ANY],
        out_specs=[pl.BlockSpec((s, CW), lambda i: (0, (2 * GW + PW) // CW + i)), wspec, vec, vec, vec],
        out_shape=[jax.ShapeDtypeStruct((s, IN_COLS), BF16), jax.ShapeDtypeStruct((CONV_K + 1, CW), F32),
                   vshape, vshape, vshape],
        input_output_aliases={7: 0},
        scratch_shapes=[pltpu.VMEM((s + CONV_HALO, CW), F32), pltpu.VMEM((s + CONV_HALO, CW), F32),
                        pltpu.VMEM((s, CW), BF16)],
        compiler_params=_cparams(),
    )(z, z, dy, wdw, bdw, lng, lnb, dz)


def _softmax_rows(sc):
    e = jnp.exp(sc - jnp.max(sc, axis=-1, keepdims=True))
    return e / jnp.sum(e, axis=-1, keepdims=True)


def _attn_fwd(q, k, v, tq):
    s, m = q.shape[0], k.shape[0]
    tq = min(tq, s)

    def body(q_ref, k_ref, v_ref, o_ref):
        for h in range(XH):
            cols = slice(h * XHD, (h + 1) * XHD)
            p = _softmax_rows(_dot(q_ref[:, cols], k_ref[:, cols], NT) * ATT_SCALE)
            o_ref[:, cols] = _dot(p.astype(BF16), v_ref[:, cols], NN).astype(BF16)

    kv = pl.BlockSpec((m, D), lambda i: (0, 0))
    return pl.pallas_call(
        body, name="attn_fwd", grid=(s // tq,),
        in_specs=[pl.BlockSpec((tq, D), lambda i: (i, 0)), kv, kv],
        out_specs=pl.BlockSpec((tq, D), lambda i: (i, 0)),
        out_shape=jax.ShapeDtypeStruct((s, D), BF16), compiler_params=_cparams(),
    )(q, k, v)


def _attn_bwd(q, k, v, do, tq, after=()):
    s, m = q.shape[0], k.shape[0]
    tq = min(tq, s)

    def body(q_ref, k_ref, v_ref, do_ref, *rest):
        dq_ref, dk_ref, dv_ref = rest[len(after):]

        @pl.when(pl.program_id(0) == 0)
        def _():
            dk_ref[...] = jnp.zeros_like(dk_ref)
            dv_ref[...] = jnp.zeros_like(dv_ref)

        for h in range(XH):
            cols = slice(h * XHD, (h + 1) * XHD)
            qh, kh, vh, doh = q_ref[:, cols], k_ref[:, cols], v_ref[:, cols], do_ref[:, cols]
            p = _softmax_rows(_dot(qh, kh, NT) * ATT_SCALE)
            dp = _dot(doh, vh, NT)
            dv_ref[:, cols] += _dot(p.astype(BF16), doh, TN)
            ds = (p * (dp - jnp.sum(p * dp, axis=-1, keepdims=True)) * ATT_SCALE).astype(BF16)
            dq_ref[:, cols] = _dot(ds, kh, NN).astype(BF16)
            dk_ref[:, cols] += _dot(ds, qh, TN)

    kv = pl.BlockSpec((m, D), lambda i: (0, 0))
    qs = pl.BlockSpec((tq, D), lambda i: (i, 0))
    return pl.pallas_call(
        body, name="attn_bwd", grid=(s // tq,),
        in_specs=[qs, kv, kv, qs] + [ANY] * len(after), out_specs=[qs, kv, kv],
        out_shape=[jax.ShapeDtypeStruct((s, D), BF16), jax.ShapeDtypeStruct((m, D), F32),
                   jax.ShapeDtypeStruct((m, D), F32)],
        compiler_params=_cparams(),
    )(q, k, v, do, *after)


def _loss_head(y, target, tm):
    s = y.shape[0]
    tm = min(tm, s)

    def body(y_ref, t_ref, dy_ref, part_ref):
        err = y_ref[...] - t_ref[...]
        dy_ref[...] = err * (1.0 / D)
        part_ref[...] = jnp.full((1, 8, LANES), 0.5 * jnp.sum(err * err) * (1.0 / D), F32)

    blk = pl.BlockSpec((tm, D), lambda i: (i, 0))
    return pl.pallas_call(
        body, name="loss_head", grid=(s // tm,), in_specs=[blk, blk],
        out_specs=[blk, pl.BlockSpec((1, 8, LANES), lambda i: (i, 0, 0))],
        out_shape=[jax.ShapeDtypeStruct((s, D), F32), jax.ShapeDtypeStruct((s // tm, 8, LANES), F32)],
        compiler_params=_cparams(),
    )(y, target)


def _layer_fwd(x0, mem, w, p, fetch):
    z, hn0 = _rowop_mm("mix_in", "rms", (x0,), p["norm_mix_pre"], w["w_in"], NT, F32)
    y = _gmlp_fwd(z, p["gmlp_v_gain"], p["w_spatial"], p["b_spatial_t"], 1024)
    y = _pool_fwd(z, p["w_pool"], p["s_pool"], y)
    y = _conv_fwd(z, p["w_dw"], p["b_dw"], p["conv_ln_g"], p["conv_ln_b"], y)
    w.update(fetch("out", (y,)))
    x1, h0 = _mm_rowop("mix_out", "rms_res", [(y, w["w_out"], NN)], (x0,), p["norm_mix_post"])
    w.update(fetch("att", (x1,)))
    q, hn1 = _rowop_mm("att_q", "rms", (x1,), p["norm_xattn_pre"], w["w_q"], NN, BF16)
    k, mn = _rowop_mm("att_k", "rms", (mem,), p["norm_mem"], w["w_k"], NN, BF16, after=(x1,))
    v, _ = _rowop_mm("att_v", "rms", (mem,), p["norm_mem"], w["w_v"], NN, BF16, after=(x1,))
    o = _attn_fwd(q, k, v, 1024)
    x2, h1 = _mm_rowop("att_o", "rms_res", [(o, w["w_o"], NN)], (x1,), p["norm_xattn_post"])
    w.update(fetch("up", (x2,)))
    u, hn2 = _rowop_mm("ffn_up", "rms", (x2,), p["norm_ffn_pre"], w["w_up"], NT, F32)
    w.update(fetch("down", (u,)))
    x3, h2 = _mm_rowop("ffn_down", "rms_res", [(u, w["w_down"], NN)], (x2,), p["norm_ffn_post"], relu2=True)
    saved = dict(x0=x0, z=z, hn0=hn0, y=y, h0=h0, x1=x1, q=q, hn1=hn1, k=k, v=v, mn=mn, o=o, h1=h1, x2=x2, u=u,
                 hn2=hn2, h2=h2)
    return x3, saved


def _layer_bwd(dx3, mem, w, p, sv, red):
    gs = {}
    du, dh2, dg = _rowop_mm("ffn_down_bwd", "rms_bwd", (sv["h2"], dx3), p["norm_ffn_post"], w["w_down"], NT, BF16,
                            u=sv["u"], after=red.after())
    gs["norm_ffn_post"] = jnp.sum(dg, axis=0)
    g_down = _mm_tn("ffn_down_dw", sv["u"], dh2, relu2=True)
    red.advance((g_down,))
    dx2, dg = _mm_rowop("ffn_up_bwd", "rms_bwd_res", [(du, w["w_up"], NN)], (sv["x2"], dx3), p["norm_ffn_pre"],
                        after=red.after())
    gs["norm_ffn_pre"] = jnp.sum(dg, axis=0)
    g_up = _mm_tn("ffn_up_dw", du, sv["hn2"])
    red.add("ffn", ("w_down", "w_up"), [g_down, g_up])
    do, dh1, dg = _rowop_mm("att_o_bwd", "rms_bwd", (sv["h1"], dx2), p["norm_xattn_post"], w["w_o"], NT, BF16,
                            after=red.after())
    gs["norm_xattn_post"] = jnp.sum(dg, axis=0)
    g_o = _mm_tn("att_o_dw", sv["o"], dh1)
    red.advance((g_o,))
    dq, dk, dv = _attn_bwd(sv["q"], sv["k"], sv["v"], do, 1024, after=red.after())
    dk, dv = dk.astype(BF16), dv.astype(BF16)
    dx1, dg = _mm_rowop("att_q_bwd", "rms_bwd_res", [(dq, w["w_q"], NT)], (sv["x1"], dx2), p["norm_xattn_pre"],
                        after=red.after())
    gs["norm_xattn_pre"] = jnp.sum(dg, axis=0)
    g_q = _mm_tn("att_q_dw", sv["hn1"], dq)
    g_k = _mm_tn("att_k_dw", sv["mn"], dk)
    g_v = _mm_tn("att_v_dw", sv["mn"], dv)
    (dg,) = _mm_rowop("att_kv_bwd", "rms_bwd_gain", [(dk, w["w_k"], NT), (dv, w["w_v"], NT)], (mem,), p["norm_mem"])
    gs["norm_mem"] = jnp.sum(dg, axis=0)
    red.add("att", ("w_o", "w_q", "w_k", "w_v"), [g_o, g_q, g_k, g_v])
    dy, dh0, dg = _rowop_mm("mix_out_bwd", "rms_bwd", (sv["h0"], dx1), p["norm_mix_post"], w["w_out"], NT, F32,
                            after=red.after())
    gs["norm_mix_post"] = jnp.sum(dg, axis=0)
    g_out = _mm_tn("mix_out_dw", sv["y"], dh0)
    red.advance((g_out,))
    red.add("out", ("w_out",), [g_out])
    z = sv["z"]
    dz, dgv, dws, dbs = _gmlp_bwd(z, dy, p["gmlp_v_gain"], p["w_spatial"], p["b_spatial_t"], 512, after=red.after())
    gs["gmlp_v_gain"] = jnp.sum(dgv, axis=0)
    gs["w_spatial"] = jnp.sum(dws, axis=0)
    gs["b_spatial"] = jnp.sum(dbs[..., 0], axis=0)
    dz, gs["w_pool"], gs["s_pool"] = _pool_bwd(z, dy, p["w_pool"], p["s_pool"], dz)
    dz, dwdw, gs["b_dw"], gs["conv_ln_g"], gs["conv_ln_b"] = _conv_bwd(
        z, dy, p["w_dw"], p["b_dw"], p["conv_ln_g"], p["conv_ln_b"], dz)
    red.advance((dz,))
    g_in = _mm_tn("mix_in_dw", dz, sv["hn0"], after=red.after())
    red.add("in", ("w_in",), [g_in])
    if red.layer == 0:
        red.advance(())
    red.small("mixer", _small_grad_arrays(gs, dwdw, norms=False))
    dx0, dg = _mm_rowop("mix_in_bwd", "rms_bwd_res", [(dz, w["w_in"], NN)], (sv["x0"], dx1), p["norm_mix_pre"],
                        after=red.after())
    gs["norm_mix_pre"] = jnp.sum(dg, axis=0)
    late = {"norms": jnp.concatenate([gs[n] for n in NORM_NAMES], axis=0)}
    if red.layer == 0:
        late["loss"] = red.extra[0]
    red.small("norms", late)
    return dx0


NORM_NAMES = ("norm_mix_pre", "norm_mix_post", "norm_xattn_pre", "norm_mem", "norm_xattn_post", "norm_ffn_pre",
              "norm_ffn_post")
VEC_NAMES = ("s_pool", "b_dw", "conv_ln_g", "conv_ln_b")
SMALL_ARRAYS = ("norms", "gain_bias", "w_spatial", "w_pool", "vecs", "w_dw")


def _small_grad_arrays(gs, dwdw, norms=True):
    out = {"norms": jnp.concatenate([gs[n] for n in NORM_NAMES], axis=0)} if norms else {}
    out.update({"gain_bias": jnp.concatenate([gs["gmlp_v_gain"], gs["b_spatial"]], axis=0),
                "w_spatial": gs["w_spatial"], "w_pool": gs["w_pool"],
                "vecs": jnp.concatenate([gs[n] for n in VEC_NAMES], axis=0), "w_dw": dwdw})
    return out


def _layer_params(small, l):
    p = {n: small[n][l].reshape(1, -1) for n in ("norm_mix_pre", "norm_mix_post", "s_pool", "b_dw", "conv_ln_g",
                                                   "conv_ln_b", "norm_xattn_pre", "norm_mem", "norm_xattn_post",
                                                   "norm_ffn_pre", "norm_ffn_post")}
    p["gmlp_v_gain"] = small["gmlp_v_gain"][l]
    p["w_spatial"] = small["w_spatial"][l]
    p["b_spatial_t"] = small["b_spatial"][l].T
    p["w_pool"] = small["w_pool"][l]
    p["w_dw"] = jnp.pad(small["w_dw"][l], ((0, 1), (0, 0)))
    return p


def _local_step(x, mem, target, fetch, small, red):
    small = dict(small)
    saved, weights, params = [], [], []
    h = x
    marker = ()
    for l in range(DEPTH):
        w = fetch(l, "in", marker)
        if "taps" in w:
            small["w_dw"] = w.pop("taps")
        p = _layer_params(small, l)
        h, sv = _layer_fwd(h, mem, w, p, functools.partial(fetch, l))
        marker = (h,)
        saved.append(sv)
        weights.append(w)
        params.append(p)
    dh, loss = _loss_head(h, target, 1024)
    red.extra = (loss,)
    for l in reversed(range(DEPTH)):
        red.layer = l
        dh = _layer_bwd(dh, mem, weights[l], params[l], saved[l], red)
    return loss, dh


HBM = pl.BlockSpec(memory_space=pltpu.HBM)


def _position():
    return lax.axis_index("x"), lax.axis_index("y"), lax.axis_index("c")


SEM = pl.BlockSpec(memory_space=pltpu.SEMAPHORE)
EFFECT = pltpu.SideEffectType.DATAFLOW_SIDE_EFFECTING
TOKEN = jax.ShapeDtypeStruct((8, LANES), F32)
TOKEN_SPEC = pl.BlockSpec(memory_space=pltpu.VMEM)


def _landing(shape, dtype):
    return pltpu.with_memory_space_constraint(lax.empty(shape, dtype), pltpu.HBM)


def _hbm_shapes(arrays):
    return [pltpu.HBM(a.shape, a.dtype) for a in arrays]


def _block(ref, r, dev):
    return ref.at[pl.ds((4 * dev[0] + 2 * dev[1] + dev[2]) * r, r), :]


def _split_call(name, body, thru, sems_in, after, sems_out, token):
    n = len(thru)
    out_shape = [pltpu.SemaphoreType.DMA(s) for s in sems_out] + _hbm_shapes(thru) + ([TOKEN] if token else [])
    out_specs = [SEM] * len(sems_out) + [HBM] * n + ([TOKEN_SPEC] if token else [])
    return pl.pallas_call(
        body, name=name, in_specs=[HBM] * n + [SEM] * len(sems_in) + [ANY] * len(after),
        out_specs=out_specs, out_shape=out_shape,
        input_output_aliases={i: len(sems_out) + i for i in range(n)},
        compiler_params=pltpu.CompilerParams(has_side_effects=EFFECT),
    )(*thru, *sems_in, *after)


def _place_own(name, srcs, dev, out_dtype, tr):
    n = len(srcs)
    r, cols = srcs[0][0].shape[-2:]
    tr = r if r < 16 else _row_tile(r, tr)
    nb = r // tr

    def body(dev_ref, *refs):
        for a in range(n):
            refs[n + a][...] = refs[a][...].astype(out_dtype)

    in_specs = [pl.BlockSpec((tr, cols), lambda i, d: (i, 0)) if l is None
                else pl.BlockSpec((None, tr, cols), lambda i, d, l=l: (l, i, 0)) for _, l in srcs]
    return pl.pallas_call(
        body, name=name,
        grid_spec=pltpu.PrefetchScalarGridSpec(
            num_scalar_prefetch=1, grid=(nb,), in_specs=in_specs,
            out_specs=[pl.BlockSpec((tr, cols), lambda i, d: (d[0] * nb + i, 0))] * n),
        out_shape=[jax.ShapeDtypeStruct((N_DEV * r, cols), out_dtype)] * n, compiler_params=_cparams(),
    )(dev, *[a for a, _ in srcs])


def _gather_peers(x, y, c):
    return [(1 - x, y, c), (x, 1 - y, c), (1 - x, 1 - y, c), (x, y, 1 - c)]


def _block_rows(land):
    return land.shape[0] // N_DEV


def _near_peers(x, y, c):
    return [(1 - x, y, c), (x, 1 - y, c), (x, y, 1 - c)]


def _relay_route(x, y, c):
    origin = (x + c * (1 - 2 * x), y + (1 - c) * (1 - 2 * y), c)
    target = (x + (1 - c) * (1 - 2 * x), y + c * (1 - 2 * y), c)
    return origin, target


def _same_block_copy(blk, send_sem, recv_sem, to):
    return pltpu.make_async_remote_copy(src_ref=blk, dst_ref=blk, send_sem=send_sem, recv_sem=recv_sem, device_id=to,
                                        device_id_type=MESH)


def _gather_start(name, lands, after):
    n = len(lands)

    def body(*refs):
        lz = refs[:n]
        send_sems, recv_sems = refs[n + len(after)], refs[n + len(after) + 1]
        token = refs[-1]
        x, y, c = _position()
        for a in range(n):
            own = _block(lz[a], _block_rows(lands[a]), (x, y, c))
            for k, to in enumerate(_near_peers(x, y, c)):
                _same_block_copy(own, send_sems.at[k], recv_sems.at[k], to).start()
        token[...] = jnp.zeros_like(token)

    out = _split_call(name, body, list(lands), [], after, [(3,), (3,)], True)
    return out[0], out[1], out[2:2 + n], out[-1]


def _gather_step(name, near, far, fresh, after):
    groups = [g for g in (near and near[0], far and far[0], fresh) if g]
    counts = [len(near[0]) if near else 0, len(far[0]) if far else 0, len(fresh) if fresh else 0]
    n = sum(counts)
    sems_in = ([near[1]] if near else []) + ([far[1]] if far else [])
    sems_out = ([(2,), (2,), (1,), (1,)] if near else []) + ([(1,), (1,)] if far else []) + ([(3,), (3,)] if fresh else [])

    def body(*refs):
        lz = list(refs[:n])
        ins = list(refs[n:n + len(sems_in)])
        outs = list(refs[n + len(sems_in) + len(after):n + len(sems_in) + len(after) + len(sems_out)])
        token = refs[-1]
        x, y, c = _position()
        me, sibling = (x, y, c), (x, y, 1 - c)
        near_lz, far_lz, fresh_lz = (lz[sum(counts[:i]):sum(counts[:i + 1])] for i in range(3))
        neighbours = _near_peers(x, y, c)[:2]
        origin, target = _relay_route(x, y, c)
        diagonal = (1 - x, 1 - y, c)
        if near:
            recv0 = ins.pop(0)
            fsend, frecv, rsend, rrecv = (outs.pop(0) for _ in range(4))
            for a, land in enumerate(near[0]):
                for j, chip in enumerate(neighbours):
                    _same_block_copy(_block(near_lz[a], _block_rows(land), chip), fsend.at[j], recv0.at[j], me).wait_recv()
        if far:
            rrecv_in = ins.pop(0)
            f2send, f2recv = outs.pop(0), outs.pop(0)
            for a, land in enumerate(far[0]):
                _same_block_copy(_block(far_lz[a], _block_rows(land), diagonal), f2send.at[0], rrecv_in.at[0], me).wait_recv()
            for a, land in enumerate(far[0]):
                _same_block_copy(_block(far_lz[a], _block_rows(land), diagonal), f2send.at[0], f2recv.at[0], sibling).start()
        if near:
            for a, land in enumerate(near[0]):
                r = _block_rows(land)
                _same_block_copy(_block(near_lz[a], r, origin), rsend.at[0], rrecv.at[0], target).start()
                for j, chip in enumerate(neighbours):
                    _same_block_copy(_block(near_lz[a], r, chip), fsend.at[j], frecv.at[j], sibling).start()
        if fresh:
            send_sems, recv_sems = outs.pop(0), outs.pop(0)
            for a, land in enumerate(fresh):
                own = _block(fresh_lz[a], _block_rows(land), me)
                for k, to in enumerate(_near_peers(x, y, c)):
                    _same_block_copy(own, send_sems.at[k], recv_sems.at[k], to).start()
        token[...] = jnp.zeros_like(token)

    out = list(_split_call(name, body, [l for g in groups for l in g], sems_in, after, sems_out, True))
    res = {"token": out.pop()}
    if near:
        res.update(fsend=out.pop(0), frecv=out.pop(0), rsend=out.pop(0), rrecv=out.pop(0))
    if far:
        res.update(f2send=out.pop(0), f2recv=out.pop(0))
    if fresh:
        res.update(send=out.pop(0), recv=out.pop(0))
    res["near"], res["far"], res["fresh"] = (out[sum(counts[:i]):sum(counts[:i + 1])] for i in range(3))
    return res


def _gather_finish(name, lands, send_sems, recv_sems, fsend, frecv, rsend, f2send, f2recv, after):
    n = len(lands)

    def body(*refs):
        lz = refs[:n]
        send0, recv0, fsend_ref, frecv_ref, rsend_ref, f2send_ref, f2recv_ref = refs[n:n + 7]
        x, y, c = _position()
        me = (x, y, c)
        near = _near_peers(x, y, c)[:2]
        origin, _ = _relay_route(x, y, c)
        for a in range(n):
            r = _block_rows(lands[a])
            sib = _block(lz[a], r, (x, y, 1 - c))
            _same_block_copy(sib, send0.at[2], recv0.at[2], me).wait_recv()
            for j, chip in enumerate(near):
                blk = _block(lz[a], r, (chip[0], chip[1], 1 - c))
                _same_block_copy(blk, fsend_ref.at[j], frecv_ref.at[j], me).wait_recv()
            far = _block(lz[a], r, (1 - x, 1 - y, 1 - c))
            _same_block_copy(far, f2send_ref.at[0], f2recv_ref.at[0], me).wait_recv()
            own = _block(lz[a], r, me)
            for k in range(3):
                _same_block_copy(own, send0.at[k], recv0.at[k], me).wait_send()
            for j, chip in enumerate(near):
                _same_block_copy(_block(lz[a], r, chip), fsend_ref.at[j], frecv_ref.at[j], me).wait_send()
            _same_block_copy(_block(lz[a], r, origin), rsend_ref.at[0], recv0.at[0], me).wait_send()
            _same_block_copy(_block(lz[a], r, (1 - x, 1 - y, c)), f2send_ref.at[0], f2recv_ref.at[0], me).wait_send()

    return _split_call(name, body, list(lands), [send_sems, recv_sems, fsend, frecv, rsend, f2send, f2recv], after, [],
                       False)


def _sibling_start(name, grads, after):
    n = len(grads)
    lands = [_landing((4, g.shape[0] // N_DEV, D), g.dtype) for g in grads]

    def body(*refs):
        ins, lz = refs[:n], refs[n:2 * n]
        send_sem, recv_sem = refs[2 * n + len(after)], refs[2 * n + len(after) + 1]
        token = refs[-1]
        x, y, c = _position()
        for a in range(n):
            r = grads[a].shape[0] // N_DEV
            for q in range(4):
                pltpu.make_async_remote_copy(
                    src_ref=ins[a].at[pl.ds((2 * q + 1 - c) * r, r), :], dst_ref=lz[a].at[q], send_sem=send_sem.at[0],
                    recv_sem=recv_sem.at[0], device_id=(x, y, 1 - c), device_id_type=MESH).start()
        token[...] = jnp.zeros_like(token)

    out = _split_call(name, body, list(grads) + lands, [], after, [(1,), (1,)], True)
    return out[0], out[1], out[2:2 + n], out[2 + n:2 + 2 * n], out[-1]


def _sibling_finish(name, grads, lands, send_sem, recv_sem, after):
    n = len(grads)

    def body(*refs):
        ins, lz = refs[:n], refs[n:2 * n]
        send_ref, recv_ref = refs[2 * n], refs[2 * n + 1]
        x, y, c = _position()
        for a in range(n):
            r = grads[a].shape[0] // N_DEV
            for q in range(4):
                cp = pltpu.make_async_remote_copy(
                    src_ref=ins[a].at[pl.ds((2 * q + 1 - c) * r, r), :], dst_ref=lz[a].at[q], send_sem=send_ref.at[0],
                    recv_sem=recv_ref.at[0], device_id=(x, y, c), device_id_type=MESH)
                cp.wait_send()
                cp.wait_recv()

    out = _split_call(name, body, list(grads) + list(lands), [send_sem, recv_sem], after, [], False)
    return out[:n], out[n:2 * n]


def _chip_start(name, parts, after):
    n = len(parts)
    lands = [_landing((3,) + p.shape[1:], p.dtype) for p in parts]

    def body(*refs):
        ins, lz = refs[:n], refs[n:2 * n]
        send_sems, recv_sems = refs[2 * n + len(after)], refs[2 * n + len(after) + 1]
        token = refs[-1]
        x, y, c = _position()
        for a in range(n):
            for j, chip in enumerate(_gather_peers(x, y, c)[:3]):
                pltpu.make_async_remote_copy(
                    src_ref=ins[a].at[2 * chip[0] + chip[1]], dst_ref=lz[a].at[j], send_sem=send_sems.at[j],
                    recv_sem=recv_sems.at[j], device_id=chip, device_id_type=MESH).start()
        token[...] = jnp.zeros_like(token)

    out = _split_call(name, body, list(parts) + lands, [], after, [(3,), (3,)], True)
    return out[0], out[1], out[2:2 + n], out[2 + n:2 + 2 * n], out[-1]


def _chip_finish(name, parts, lands, send_sems, recv_sems, after):
    n = len(parts)

    def body(*refs):
        ins, lz = refs[:n], refs[n:2 * n]
        send_ref, recv_ref = refs[2 * n], refs[2 * n + 1]
        me = _position()
        for a in range(n):
            for j in range(3):
                cp = pltpu.make_async_remote_copy(
                    src_ref=ins[a].at[j], dst_ref=lz[a].at[j], send_sem=send_ref.at[j], recv_sem=recv_ref.at[j],
                    device_id=me, device_id_type=MESH)
                cp.wait_send()
                cp.wait_recv()

    out = _split_call(name, body, list(parts) + list(lands), [send_sems, recv_sems], after, [], False)
    return out[:n], out[n:2 * n]


def _other_devices(x, y, c):
    return [(x + (k >> 2 & 1) * (1 - 2 * x), y + (k >> 1 & 1) * (1 - 2 * y), c + (k & 1) * (1 - 2 * c))
            for k in range(1, N_DEV)]


def _broadcast_start(name, arrays, after):
    n = len(arrays)
    lands = [_landing((N_DEV,) + a.shape, a.dtype) for a in arrays]

    def body(*refs):
        ins, lz = refs[:n], refs[n:2 * n]
        send_sems, recv_sems = refs[2 * n + len(after)], refs[2 * n + len(after) + 1]
        token = refs[-1]
        x, y, c = _position()
        for a in range(n):
            for k, peer in enumerate(_other_devices(x, y, c)):
                pltpu.make_async_remote_copy(
                    src_ref=ins[a], dst_ref=lz[a].at[4 * x + 2 * y + c], send_sem=send_sems.at[k],
                    recv_sem=recv_sems.at[k], device_id=peer, device_id_type=MESH).start()
        token[...] = jnp.zeros_like(token)

    out = _split_call(name, body, list(arrays) + lands, [], after, [(N_DEV - 1,), (N_DEV - 1,)], True)
    return out[0], out[1], out[2:2 + n], out[2 + n:2 + 2 * n], out[-1]


def _broadcast_finish(name, arrays, lands, send_sems, recv_sems, after):
    n = len(arrays)

    def body(*refs):
        ins, lz = refs[:n], refs[n:2 * n]
        send_ref, recv_ref = refs[2 * n], refs[2 * n + 1]
        x, y, c = _position()
        for a in range(n):
            for k, peer in enumerate(_other_devices(x, y, c)):
                cp = pltpu.make_async_remote_copy(
                    src_ref=ins[a], dst_ref=lz[a].at[4 * peer[0] + 2 * peer[1] + peer[2]], send_sem=send_ref.at[k],
                    recv_sem=recv_ref.at[k], device_id=(x, y, c), device_id_type=MESH)
                cp.wait_send()
                cp.wait_recv()

    out = _split_call(name, body, list(arrays) + list(lands), [send_sems, recv_sems], after, [], False)
    return out[:n], out[n:2 * n]


def _row_tile(r, target):
    return max(t for t in range(16, min(r, target) + 1, 16) if r % t == 0)


CHIP_PARTIAL_BYTES = 12 * 1024 * 1024


def _chip_partial(name, grads, gots, c):
    n = len(grads)
    r = grads[0].shape[0] // N_DEV
    tr = _row_tile(r, CHIP_PARTIAL_BYTES // (n * 3 * D * 2))

    def body(c_ref, *refs):
        for a in range(n):
            refs[2 * n + a][...] = (refs[a][...].astype(F32) + refs[n + a][...].astype(F32)).astype(BF16)

    blk = pl.BlockSpec((None, tr, D), lambda q, i, c_ref: (q, i, 0))
    return pl.pallas_call(
        body, name=name,
        grid_spec=pltpu.PrefetchScalarGridSpec(
            num_scalar_prefetch=1, grid=(4, r // tr),
            in_specs=[pl.BlockSpec((None, None, tr, D), lambda q, i, c_ref: (q, c_ref[0], i, 0))] * n + [blk] * n,
            out_specs=[blk] * n),
        out_shape=[jax.ShapeDtypeStruct((4, r, D), BF16)] * n, compiler_params=_cparams(),
    )(c, *[g.reshape(4, 2, r, D) for g in grads], *gots)


class _WeightGather:
    def __init__(self, groups):
        self.groups = list(groups)
        self.index = {key: i for i, (key, _, _) in enumerate(groups)}
        self.state = [None] * len(groups)
        self.token = ()
        for i in range(min(2, len(groups))):
            self._start(i)

    def _tag(self, i):
        return "%s_%d" % self.groups[i][0][::-1]

    def _start(self, i):
        send, recv, lz, tok = _gather_start("gather_start_" + self._tag(i), self.groups[i][2], self.token)
        self.state[i] = dict(send=send, recv=recv, lands=lz)
        self.token = (tok,)

    def _step(self, name, near, far, fresh, marker):
        exists = lambda i: i is not None and i < len(self.groups)
        near, far, fresh = (i if exists(i) else None for i in (near, far, fresh))
        res = _gather_step(
            name, None if near is None else (self.state[near]["lands"], self.state[near]["recv"]),
            None if far is None else (self.state[far]["lands"], self.state[far]["rrecv"]),
            None if fresh is None else self.groups[fresh][2], tuple(marker) + self.token)
        self.token = (res["token"],)
        if near is not None:
            self.state[near].update(lands=res["near"], fsend=res["fsend"], frecv=res["frecv"], rsend=res["rsend"],
                                    rrecv=res["rrecv"])
        if far is not None:
            self.state[far].update(lands=res["far"], f2send=res["f2send"], f2recv=res["f2recv"])
        if fresh is not None:
            self.state[fresh] = dict(send=res["send"], recv=res["recv"], lands=res["fresh"])

    def fetch(self, layer, group, marker):
        k = self.index[(layer, group)]
        if k == 0:
            self._step("gather_step_first", 0, None, None, marker)
        self._step("gather_step_" + self._tag(k), k + 1, k, k + 2, marker)
        st = self.state[k]
        lz = _gather_finish("gather_finish_" + self._tag(k), st["lands"], st["send"], st["recv"], st["fsend"],
                            st["frecv"], st["rsend"], st["f2send"], st["f2recv"], self.token)
        self.state[k] = None
        return dict(zip(self.groups[k][1], lz))


class _GradReduce:
    def __init__(self, core, chip):
        self.core, self.chip = core, chip
        self.layer = None
        self.token = ()
        self.at_sibling, self.at_chips = [], []
        self.extra, self.smalls = (), {}

    def after(self):
        return self.token

    def add(self, group, names, grads):
        tag = "%s_%d" % (group, self.layer)
        send, recv, grads, lands, tok = _sibling_start("grad_sibling_start_" + tag, grads, self.token)
        self.at_sibling.append((tag, [(self.layer, n) for n in names], send, recv, grads, lands))
        self.token = (tok,)

    def advance(self, marker):
        for tag, keys, send, recv, grads, lands in self.at_sibling:
            grads, lands = _sibling_finish("grad_sibling_finish_" + tag, grads, lands, send, recv, marker)
            parts = _chip_partial("chip_partial_" + tag, grads, lands, self.core)
            send, recv, parts, lands, tok = _chip_start("grad_chip_start_" + tag, parts, ())
            self.at_chips.append([tag, keys, send, recv, parts, lands])
            self.token = (tok,)
        self.at_sibling = []

    def small(self, part, arrays):
        keys = list(arrays)
        send, recv, own, slots, tok = _broadcast_start(
            "small_grads_start_%d_%s" % (self.layer, part), [arrays[k] for k in keys], self.token)
        self.smalls.setdefault(self.layer, []).append((part, keys, send, recv, own, slots))
        self.token = (tok,)

    def small_finish(self, layer, marker):
        mine, theirs = {}, {}
        for part, keys, send, recv, own, slots in self.smalls[layer]:
            own, slots = _broadcast_finish("small_grads_finish_%d_%s" % (layer, part), own, slots, send, recv, marker)
            mine.update(zip(keys, own))
            theirs.update(zip(keys, slots))
        return mine, theirs

    def collect(self, key, marker):
        for entry in self.at_chips:
            tag, keys, send, recv, parts, lands = entry
            if key in keys:
                if send is not None:
                    parts, lands = _chip_finish("grad_chip_finish_" + tag, parts, lands, send, recv, marker)
                    entry[2:] = [None, None, parts, lands]
                i = keys.index(key)
                return parts[i], lands[i]
        raise KeyError(key)


def _adamw_math(w, g, m, v):
    m = ADAM_B1 * m + (1.0 - ADAM_B1) * g
    v = ADAM_B2 * v + (1.0 - ADAM_B2) * jnp.square(g)
    m_hat = m / (1.0 - ADAM_B1 ** ADAM_STEP)
    v_hat = v / (1.0 - ADAM_B2 ** ADAM_STEP)
    delta = -ADAM_LR * (m_hat / (jnp.sqrt(v_hat) + ADAM_EPS) + ADAM_WD * w)
    return delta, m, v


def _adamw_small(wts, mom_m, mom_v, own, gathered, loss_own, loss_gathered, dev):
    names = SMALL
    nw = len(names)
    na = len(SMALL_ARRAYS)

    def body(dev_ref, *refs):
        w_refs, m_refs, v_refs = (dict(zip(names, refs[i * nw:(i + 1) * nw])) for i in range(3))
        own_refs = refs[3 * nw:3 * nw + DEPTH * na]
        g_refs = refs[3 * nw + DEPTH * na:3 * nw + 2 * DEPTH * na]
        loss_own_ref, loss_got_ref = refs[3 * nw + 2 * DEPTH * na:3 * nw + 2 * DEPTH * na + 2]
        outs = refs[3 * nw + 2 * DEPTH * na + 2:]
        g_out, d_out, m_out, v_out = (dict(zip(names, outs[i * nw:(i + 1) * nw])) for i in range(4))
        me = dev_ref[0]

        loss = None
        for d in range(N_DEV):
            for b in range(loss_own.shape[0]):
                term = jnp.where(me == d, loss_own_ref[b], loss_got_ref[d, b])
                loss = term if loss is None else loss + term
        outs[4 * nw][...] = loss

        def update(name, at, g):
            g_out[name][at] = g
            d_out[name][at], m_out[name][at], v_out[name][at] = _adamw_math(
                w_refs[name][at], g, m_refs[name][at], v_refs[name][at])

        for l in range(DEPTH):
            mine = dict(zip(SMALL_ARRAYS, own_refs[l * na:(l + 1) * na]))
            got = dict(zip(SMALL_ARRAYS, g_refs[l * na:(l + 1) * na]))

            def total(key, at):
                acc = None
                for d in range(N_DEV):
                    term = jnp.where(me == d, mine[key][at] if at else mine[key][...], got[key][(d,) + at])
                    acc = term if acc is None else acc + term
                return acc

            row = (slice(l, l + 1),)
            for k, name in enumerate(NORM_NAMES):
                update(name, row, total("norms", (slice(k, k + 1),)))
            for k, name in enumerate(VEC_NAMES):
                update(name, row, total("vecs", (slice(k, k + 1),)))
            update("gmlp_v_gain", (l,), total("gain_bias", (slice(0, NH),)))
            update("b_spatial", (l,), total("gain_bias", (slice(NH, 2 * NH),)))
            update("w_spatial", (l,), total("w_spatial", ()))
            update("w_pool", (l,), total("w_pool", ()))
            update("w_dw", (l,), total("w_dw", (slice(0, CONV_K),)))

    args = [src[n] for src in (wts, mom_m, mom_v) for n in names]
    args += [src[l][k] for src in (own, gathered) for l in range(DEPTH) for k in SMALL_ARRAYS]
    args += [loss_own, loss_gathered]
    outs = pl.pallas_call(
        body, name="adamw_small",
        in_specs=[pl.BlockSpec(memory_space=pltpu.SMEM)] + [pl.BlockSpec(memory_space=pltpu.VMEM)] * len(args),
        out_shape=[jax.ShapeDtypeStruct(wts[n].shape, F32) for _ in range(4) for n in names]
        + [jax.ShapeDtypeStruct((8, LANES), F32)],
        compiler_params=_cparams(),
    )(dev, *args)
    return tuple(dict(zip(names, outs[i * nw:(i + 1) * nw])) for i in range(4)) + (outs[4 * nw],)


def _adamw_layers(name, w, reduced, m, v, chip, tr, transposed=False, after=()):
    nl, r, cdim = w.shape
    tr = _row_tile(r, tr)
    nb = r // tr

    def body(q_ref, w_ref, p0_ref, g0_ref, p1_ref, g1_ref, m_ref, v_ref, *rest):
        g_ref, d_ref, nm_ref, nv_ref = rest[len(after):]

        def total(p_ref, got_ref):
            acc = p_ref[...].astype(F32)
            for j in range(3):
                acc = acc + got_ref[j].astype(F32)
            return acc

        g = jnp.where(pl.program_id(0) == 0, total(p0_ref, g0_ref), total(p1_ref, g1_ref))
        if transposed:
            g = g.T
        g_ref[...] = g
        d_ref[...], nm_ref[...], nv_ref[...] = _adamw_math(w_ref[...], g, m_ref[...], v_ref[...])

    blk = pl.BlockSpec((None, tr, cdim), lambda l, i, q: (l, i, 0))
    first = lambda l, i: i * (1 - l) + (nb - 1) * l
    second = lambda l, i: i * l
    if transposed:
        gshape = (cdim, tr)
        at = lambda lead, i: (lead, 0, i)
    else:
        gshape = (tr, cdim)
        at = lambda lead, i: (lead, i, 0)
    specs = [blk,
             pl.BlockSpec((None,) + gshape, lambda l, i, q: at(q[0], first(l, i))),
             pl.BlockSpec((3,) + gshape, lambda l, i, q: at(0, first(l, i))),
             pl.BlockSpec((None,) + gshape, lambda l, i, q: at(q[0], second(l, i))),
             pl.BlockSpec((3,) + gshape, lambda l, i, q: at(0, second(l, i))), blk, blk] + [ANY] * len(after)
    shape = jax.ShapeDtypeStruct((nl, r, cdim), F32)
    return pl.pallas_call(
        body, name=name,
        grid_spec=pltpu.PrefetchScalarGridSpec(num_scalar_prefetch=1, grid=(nl, nb), in_specs=specs, out_specs=[blk] * 4),
        out_shape=[shape] * 4, compiler_params=_cparams(),
    )(chip, w, *reduced[0], *reduced[1], m, v, *after)


def _to_rows(name, a):
    return jnp.swapaxes(a, 1, 2) if name == "w_in" else a


def _place_own_transposed(name, srcs, dev, out_dtype, tc):
    n = len(srcs)
    kdim, cdim = srcs[0][0].shape[-2:]

    def body(dev_ref, *refs):
        for a in range(n):
            refs[n + a][...] = refs[a][...].T.astype(out_dtype)

    return pl.pallas_call(
        body, name=name,
        grid_spec=pltpu.PrefetchScalarGridSpec(
            num_scalar_prefetch=1, grid=(kdim // tc,),
            in_specs=[pl.BlockSpec((None, tc, cdim), lambda i, d, l=l: (l, i, 0)) for _, l in srcs],
            out_specs=[pl.BlockSpec((cdim, tc), lambda i, d: (d[0], i))] * n),
        out_shape=[jax.ShapeDtypeStruct((N_DEV * cdim, kdim), out_dtype)] * n, compiler_params=_cparams(),
    )(dev, *[a for a, _ in srcs])


def _pack(arrays, rows):
    flat = jnp.concatenate([a.reshape(-1) for a in arrays])
    return jnp.pad(flat, (0, rows * D - flat.shape[0])).reshape(rows, D)


def _rows_for(shapes, mult=8):
    total = 0
    for shp in shapes:
        size = 1
        for dim in shp:
            size *= dim
        total += size
    return -(-total // (mult * D)) * mult


def kernel(x, mem, norm_mix_pre, norm_mix_post, w_in, w_out, gmlp_v_gain, w_spatial, b_spatial, w_pool, s_pool, w_dw, b_dw, conv_ln_g, conv_ln_b, norm_xattn_pre, norm_mem, norm_xattn_post, w_q, w_k, w_v, w_o, norm_ffn_pre, norm_ffn_post, w_up, w_down, loss_target, m_norm_mix_pre, m_norm_mix_post, m_w_in, m_w_out, m_gmlp_v_gain, m_w_spatial, m_b_spatial, m_w_pool, m_s_pool, m_w_dw, m_b_dw, m_conv_ln_g, m_conv_ln_b, m_norm_xattn_pre, m_norm_mem, m_norm_xattn_post, m_w_q, m_w_k, m_w_v, m_w_o, m_norm_ffn_pre, m_norm_ffn_post, m_w_up, m_w_down, v_norm_mix_pre, v_norm_mix_post, v_w_in, v_w_out, v_gmlp_v_gain, v_w_spatial, v_b_spatial, v_w_pool, v_s_pool, v_w_dw, v_b_dw, v_conv_ln_g, v_conv_ln_b, v_norm_xattn_pre, v_norm_mem, v_norm_xattn_post, v_w_q, v_w_k, v_w_v, v_w_o, v_norm_ffn_pre, v_norm_ffn_post, v_w_up, v_w_down):
    args = dict(locals())
    wts = {n: args[n] for n in WEIGHTS}
    mom_m = {n: args["m_" + n] for n in WEIGHTS}
    mom_v = {n: args["v_" + n] for n in WEIGHTS}
    xi, yi, ci = _position()
    me = 4 * xi + 2 * yi + ci

    dev = jnp.reshape(me, (1,)).astype(jnp.int32)
    lands = {}
    for call, names, tr in (("place_in", ("w_in",), 256), ("place_att", ("w_out", "w_q", "w_k", "w_v", "w_o"), 64),
                            ("place_up", ("w_up",), 256), ("place_down", ("w_down",), 256)):
        srcs = [(_to_rows(n, wts[n]), l) for l in range(DEPTH) for n in names]
        placed = (_place_own_transposed if names == ("w_up",) else _place_own)(call, srcs, dev, BF16, tr)
        lands.update(zip([(l, n) for l in range(DEPTH) for n in names], placed))
    (lands[(0, "taps")],) = _place_own("place_taps", [(_pack([w_dw], _rows_for([w_dw.shape])), None)], dev, F32, 8)
    groups = []
    for l in range(DEPTH):
        for group, names in GATHER_GROUPS:
            if (l, group) == (0, "in"):
                names = names + ("taps",)
            groups.append(((l, group), names, [lands[(l, n)] for n in names]))
    gather = _WeightGather(groups)

    def fetch(layer, group, marker):
        w = gather.fetch(layer, group, marker)
        if "taps" in w:
            blocks = w["taps"].reshape(N_DEV, -1)[:, :w_dw.size].reshape((N_DEV,) + w_dw.shape)
            w["taps"] = jnp.moveaxis(blocks, 0, 2).reshape(DEPTH, CONV_K, CW)
        return w

    reduce = _GradReduce(jnp.reshape(ci, (1,)).astype(jnp.int32), jnp.reshape(2 * xi + yi, (1,)).astype(jnp.int32))
    small = {n: wts[n] for n in SMALL if n != "w_dw"}
    _, dx = _local_step(x[0], mem[0], loss_target[0], fetch, small, reduce)
    reduce.advance((dx,))

    grad_w, delta, new_m, new_v = {}, {}, {}, {}
    marker = (dx,) + tuple(reduce.after())
    for n in UPDATE_ORDER:
        reduced = [reduce.collect((l, n), marker) for l in range(DEPTH)]
        outs = _adamw_layers("adamw_" + n, _to_rows(n, wts[n]), reduced, _to_rows(n, mom_m[n]), _to_rows(n, mom_v[n]),
                             reduce.chip, 256, transposed=n == "w_up", after=marker)
        grad_w[n], delta[n], new_m[n], new_v[n] = (_to_rows(n, o) for o in outs)
        marker = (outs[1],)

    own, slots = [None] * DEPTH, [None] * DEPTH
    for l in reversed(range(DEPTH)):
        own[l], slots[l] = reduce.small_finish(l, marker)
        if l == 0:
            loss_own, loss_slots = own[l].pop("loss"), slots[l].pop("loss")
    shard_cols = CW // N_DEV
    for l in range(DEPTH):
        own[l]["w_dw"] = lax.dynamic_slice_in_dim(own[l]["w_dw"], me * shard_cols, shard_cols, axis=1)
        slots[l]["w_dw"] = lax.dynamic_slice_in_dim(slots[l]["w_dw"], me * shard_cols, shard_cols, axis=2)
    *small_out, loss_tile = _adamw_small(wts, mom_m, mom_v, own, slots, loss_own, loss_slots, dev)
    for dst, src in zip((grad_w, delta, new_m, new_v), small_out):
        dst.update(src)

    return (loss_tile[0, 0], dx[None], *[grad_w[n] for n in WEIGHTS], *[delta[n] for n in WEIGHTS],
            *[new_m[n] for n in WEIGHTS], *[new_v[n] for n in WEIGHTS])
```

```python
import functools

import jax
import jax.numpy as jnp
from jax import lax
from jax.experimental import pallas as pl
from jax.experimental.pallas import tpu as pltpu

F32 = jnp.float32
BF16 = jnp.bfloat16

D = 2048
GW = 1024
PW = 512
CW = 512
HD = 128
NH = 8
NG = 4
POOL_WINDOWS = (2, 4, 8, 16)
CONV_K = 31
IN_COLS = 2 * GW + PW + 2 * CW
XH = 4
XHD = D // XH
ATT_SCALE = XHD ** -0.5
RMS_EPS = 1e-6
LN_EPS = 1e-5
DEPTH = 2
N_DEV = 8

ADAM_LR = 0.001
ADAM_B1 = 0.9
ADAM_B2 = 0.999
ADAM_EPS = 1e-08
ADAM_WD = 0.01
ADAM_STEP = 10

LANES = 128
CONV_HALO = 32
POOL_HALO = 16
ROW_TILE = 128
VMEM_LIMIT = 60 * 1024 * 1024

MESH = pl.DeviceIdType.MESH
NT = (((1,), (1,)), ((), ()))
NN = (((1,), (0,)), ((), ()))
TN = (((0,), (0,)), ((), ()))

UPDATE_ORDER = ("w_down", "w_up", "w_o", "w_q", "w_k", "w_v", "w_out", "w_in")
GATHER_GROUPS = (("in", ("w_in",)), ("out", ("w_out",)), ("att", ("w_q", "w_k", "w_v", "w_o")), ("up", ("w_up",)),
                 ("down", ("w_down",)))
SMALL = ("norm_mix_pre", "norm_mix_post", "gmlp_v_gain", "w_spatial", "b_spatial", "w_pool", "s_pool",
         "w_dw", "b_dw", "conv_ln_g", "conv_ln_b", "norm_xattn_pre", "norm_mem", "norm_xattn_post",
         "norm_ffn_pre", "norm_ffn_post")
WEIGHTS = ("norm_mix_pre", "norm_mix_post", "w_in", "w_out", "gmlp_v_gain", "w_spatial", "b_spatial", "w_pool",
           "s_pool", "w_dw", "b_dw", "conv_ln_g", "conv_ln_b", "norm_xattn_pre", "norm_mem", "norm_xattn_post",
           "w_q", "w_k", "w_v", "w_o", "norm_ffn_pre", "norm_ffn_post", "w_up", "w_down")


def _cparams():
    return pltpu.CompilerParams(vmem_limit_bytes=VMEM_LIMIT)


def _dot(a, b, dims):
    return lax.dot_general(a, b, dims, preferred_element_type=F32)


def _rms(x, g):
    y = x * lax.rsqrt(jnp.mean(x * x, axis=-1, keepdims=True) + RMS_EPS)
    return y * g


def _rms_bwd(x, g, dy):
    r = lax.rsqrt(jnp.mean(x * x, axis=-1, keepdims=True) + RMS_EPS)
    xh = x * r
    t = dy * g
    dx = r * (t - xh * jnp.mean(t * xh, axis=-1, keepdims=True))
    return dx, jnp.sum(dy * xh, axis=0, keepdims=True)


def _gelu(x):
    cdf = 0.5 * (1.0 + jnp.tanh(0.7978845608028654 * (x + 0.044715 * (x * x * x))))
    return x * cdf


def _layer_norm(x, g, b=None):
    mu = jnp.mean(x, axis=-1, keepdims=True)
    xc = x - mu
    var = jnp.mean(xc * xc, axis=-1, keepdims=True)
    y = xc * lax.rsqrt(var + LN_EPS) * g
    return y if b is None else y + b


def _sigmoid(x):
    return 1.0 / (1.0 + jnp.exp(-x))


def _gmlp_rows(zu, zv, gv):
    return _gelu(zu), _layer_norm(_gelu(zv), gv)


def _glu(cv, cg):
    return cv * _sigmoid(cg)


def _ln_silu(h, g, b):
    y = _layer_norm(h, g, b)
    return y * _sigmoid(y)


ANY = pl.BlockSpec(memory_space=pl.ANY)


ROWS_TILE = 256
COLS_TILE = 512
DW_TILE = 512
RESIDENT_K = 2048
RESIDENT_ROWS = 512
STREAM_K_TILE = 1024
STREAM_ROWS = 512


def _k_tiles(kdim):
    if kdim <= RESIDENT_K:
        return RESIDENT_ROWS, kdim
    if kdim <= IN_COLS:
        return ROWS_TILE, kdim
    return STREAM_ROWS, max(t for t in range(LANES, STREAM_K_TILE + 1, LANES) if kdim % t == 0)


def _rowop_mm(name, kind, rows, g, w, dims, out_dtype, u=None, after=()):
    s = rows[0].shape[0]
    n = w.shape[0] if dims == NT else w.shape[1]
    resident = n <= IN_COLS and u is None
    tm, tn = min(RESIDENT_ROWS if resident and n <= RESIDENT_K else ROWS_TILE, s), min(COLS_TILE, n)
    ni, nj = s // tm, n // tn
    bwd = kind == "rms_bwd"
    out_shape = [jax.ShapeDtypeStruct((s, n), out_dtype), jax.ShapeDtypeStruct((s, D), BF16)]
    if bwd:
        out_shape.append(jax.ShapeDtypeStruct((ni, 1, D), F32))

    if resident:
        def row_body(*refs):
            refs = list(refs)
            row_refs = [refs.pop(0) for _ in rows]
            g_ref, w_ref = refs.pop(0), refs.pop(0)
            del refs[:len(after)]
            if bwd:
                a, dg = _rms_bwd(row_refs[0][...], g_ref[...], row_refs[1][...])
                refs[2][0] = dg
            else:
                a = _rms(row_refs[0][...], g_ref[...])
            a = a.astype(BF16)
            refs[1][...] = a
            refs[0][...] = _dot(a, w_ref[...], dims).astype(out_dtype)

        blk = pl.BlockSpec((tm, D), lambda i: (i, 0))
        return pl.pallas_call(
            row_body, name=name, grid=(ni,),
            in_specs=[blk] * len(rows) + [pl.BlockSpec((1, D), lambda i: (0, 0)),
                                          pl.BlockSpec(w.shape, lambda i: (0, 0), pipeline_mode=pl.Buffered(1))]
            + [ANY] * len(after),
            out_specs=[pl.BlockSpec((tm, n), lambda i: (i, 0)), blk]
            + ([pl.BlockSpec((1, 1, D), lambda i: (i, 0, 0))] if bwd else []),
            out_shape=out_shape, compiler_params=_cparams(),
        )(*rows, g, w, *after)

    def body(*refs):
        refs = list(refs)
        row_refs = [refs.pop(0) for _ in rows]
        g_ref, w_ref = refs.pop(0), refs.pop(0)
        u_ref = refs.pop(0) if u is not None else None
        del refs[:len(after)]
        out_ref, a_ref = refs.pop(0), refs.pop(0)
        dg_ref = refs.pop(0) if bwd else None
        a_all = refs.pop(0)
        t = pl.program_id(0)

        @pl.when(t < ni)
        def _():
            if bwd:
                a, dg = _rms_bwd(row_refs[0][...], g_ref[...], row_refs[1][...])
                dg_ref[0] = dg
            else:
                a = _rms(row_refs[0][...], g_ref[...])
            a_ref[...] = a.astype(BF16)
            a_all[pl.ds(pl.multiple_of(t * tm, tm), tm), :] = a.astype(BF16)

        @pl.when(t >= ni)
        def _():
            acc = _dot(a_all[...], w_ref[...], dims)
            if u_ref is not None:
                acc = acc * (2.0 * jnp.maximum(u_ref[...], 0.0))
            out_ref[...] = acc.astype(out_dtype)

    rows_at = lambda t: jnp.minimum(t, ni - 1)
    cols_at = lambda t: jnp.maximum(t - ni, 0)
    row_spec = pl.BlockSpec((tm, D), lambda t: (rows_at(t), 0))
    w_spec = (pl.BlockSpec((tn, D), lambda t: (cols_at(t), 0)) if dims == NT
              else pl.BlockSpec((D, tn), lambda t: (0, cols_at(t))))
    tile = pl.BlockSpec((s, tn), lambda t: (0, cols_at(t)))
    in_specs = [row_spec] * len(rows) + [pl.BlockSpec((1, D), lambda t: (0, 0)), w_spec]
    in_specs += ([tile] if u is not None else []) + [ANY] * len(after)
    out_specs = [tile, row_spec]
    if bwd:
        out_specs.append(pl.BlockSpec((1, 1, D), lambda t: (rows_at(t), 0, 0)))
    return pl.pallas_call(
        body, name=name, grid=(ni + nj,), in_specs=in_specs, out_specs=out_specs, out_shape=out_shape,
        scratch_shapes=[pltpu.VMEM((s, D), BF16)], compiler_params=_cparams(),
    )(*rows, g, w, *([u] if u is not None else []), *after)


def _mm_rowop(name, kind, pairs, rows, g, relu2=False, after=()):
    s, kdim = pairs[0][0].shape
    tm, tk = _k_tiles(kdim)
    tm = min(tm, s)
    ni, nk = s // tm, kdim // tk
    npair = len(pairs)

    def body(*refs):
        refs = list(refs)
        a_refs = [refs.pop(0) for _ in range(npair)]
        w_refs = [refs.pop(0) for _ in range(npair)]
        row_refs = [refs.pop(0) for _ in rows]
        g_ref = refs.pop(0)
        del refs[:len(after)]
        acc = refs.pop() if nk > 1 else None
        outs = refs
        k = pl.program_id(1)

        def product():
            total = None
            for a_ref, w_ref, (_, _, dims) in zip(a_refs, w_refs, pairs):
                a = a_ref[...]
                if relu2:
                    a = jnp.square(jnp.maximum(a, 0.0))
                term = _dot(a.astype(BF16), w_ref[...], dims)
                total = term if total is None else total + term
            return total

        def finish(h):
            if kind == "rms_res":
                outs[0][...] = row_refs[0][...] + _rms(h, g_ref[...])
                outs[1][...] = h
            else:
                dx, dg = _rms_bwd(row_refs[0][...], g_ref[...], h)
                if kind == "rms_bwd_res":
                    outs[0][...] = row_refs[1][...] + dx
                    outs[1][0] = dg
                else:
                    outs[0][0] = dg

        if nk == 1:
            finish(product())
            return

        @pl.when(k == 0)
        def _():
            acc[...] = jnp.zeros_like(acc)

        acc[...] += product()

        @pl.when(k == nk - 1)
        def _():
            finish(acc[...])

    row_spec = pl.BlockSpec((tm, D), lambda i, k: (i, 0))
    dg_shape = jax.ShapeDtypeStruct((ni, 1, D), F32)
    dg_spec = pl.BlockSpec((1, 1, D), lambda i, k: (i, 0, 0))
    in_specs = [pl.BlockSpec((tm, tk), lambda i, k: (i, k))] * npair
    for _, _, dims in pairs:
        mode = dict(pipeline_mode=pl.Buffered(1)) if nk == 1 else {}
        in_specs.append(pl.BlockSpec((tk, D), lambda i, k: (k, 0), **mode) if dims == NN
                        else pl.BlockSpec((D, tk), lambda i, k: (0, k), **mode))
    in_specs += [row_spec] * len(rows) + [pl.BlockSpec((1, D), lambda i, k: (0, 0))] + [ANY] * len(after)
    if kind == "rms_res":
        out_shape = [jax.ShapeDtypeStruct((s, D), F32)] * 2
        out_specs = [row_spec, row_spec]
    elif kind == "rms_bwd_res":
        out_shape = [jax.ShapeDtypeStruct((s, D), F32), dg_shape]
        out_specs = [row_spec, dg_spec]
    else:
        out_shape = [dg_shape]
        out_specs = [dg_spec]
    return pl.pallas_call(
        body, name=name, grid=(ni, nk), in_specs=in_specs, out_specs=out_specs, out_shape=out_shape,
        scratch_shapes=[pltpu.VMEM((tm, D), F32)] if nk > 1 else [], compiler_params=_cparams(),
    )(*[p[0] for p in pairs], *[p[1] for p in pairs], *rows, g, *after)


def _mm_tn(name, a, gmat, relu2=False, after=()):
    s, m = a.shape
    tm = min(DW_TILE, m)
    ni = m // tm

    def body(a_ref, g_ref, *rest):
        av = a_ref[...]
        if relu2:
            av = jnp.square(jnp.maximum(av, 0.0))
        rest[len(after)][...] = _dot(av.astype(BF16), g_ref[...], TN).astype(BF16)

    return pl.pallas_call(
        body, name=name, grid=(ni,),
        in_specs=[pl.BlockSpec((s, tm), lambda i: (0, i)), pl.BlockSpec((s, D), lambda i: (0, 0))] + [ANY] * len(after),
        out_specs=pl.BlockSpec((tm, D), lambda i: (i, 0)),
        out_shape=jax.ShapeDtypeStruct((m, D), BF16), compiler_params=_cparams(),
    )(a, gmat, *after)


def _tril():
    r = lax.broadcasted_iota(jnp.int32, (HD, HD), 0)
    c = lax.broadcasted_iota(jnp.int32, (HD, HD), 1)
    return (c <= r).astype(F32)


def _gmlp_fwd(z, gv, ws, bst, tb):
    s = z.shape[0]
    tb = min(tb, s)

    def body(zu_ref, zv_ref, gv_ref, ws_ref, bst_ref, y_ref):
        tril = _tril()
        for h in range(NH):
            cols = slice(h * HD, (h + 1) * HD)
            u, vln = _gmlp_rows(zu_ref[:, cols], zv_ref[:, cols], gv_ref[h:h + 1, :])
            wm = (ws_ref[h] * tril).astype(BF16)
            vb = vln.astype(BF16)
            for c in range(tb // HD):
                rws = slice(c * HD, (c + 1) * HD)
                mixed = _dot(wm, vb[rws], NN) + bst_ref[:, h:h + 1]
                y_ref[rws, cols] = (u[rws] * mixed).astype(BF16)

    return pl.pallas_call(
        body, name="gmlp_fwd", grid=(s // tb,),
        in_specs=[pl.BlockSpec((tb, GW), lambda i: (i, 0)), pl.BlockSpec((tb, GW), lambda i: (i, 1)),
                  pl.BlockSpec((NH, HD), lambda i: (0, 0)), pl.BlockSpec((NH, HD, HD), lambda i: (0, 0, 0)),
                  pl.BlockSpec((HD, NH), lambda i: (0, 0))],
        out_specs=pl.BlockSpec((tb, GW), lambda i: (i, 0)),
        out_shape=jax.ShapeDtypeStruct((s, D), BF16), compiler_params=_cparams(),
    )(z, z, gv, ws, bst)


def _gmlp_bwd(z, dy, gv, ws, bst, tb, after=()):
    s = z.shape[0]
    tb = min(tb, s)
    nb = s // tb

    def body(zu_ref, zv_ref, dy_ref, gv_ref, ws_ref, bst_ref, *rest):
        dz_ref, dgv_ref, dws_ref, db_ref = rest[len(after):]
        tril = _tril()
        for h in range(NH):
            cols = slice(h * HD, (h + 1) * HD)
            (u, vln), vjp = jax.vjp(_gmlp_rows, zu_ref[:, cols], zv_ref[:, cols], gv_ref[h:h + 1, :])
            wmf = ws_ref[h] * tril
            wm = wmf.astype(BF16)
            wmt = wmf.T.astype(BF16)
            vb = vln.astype(BF16)
            dws = jnp.zeros((HD, HD), F32)
            db = jnp.zeros((HD, 1), F32)
            du_parts, dvln_parts = [], []
            for c in range(tb // HD):
                rws = slice(c * HD, (c + 1) * HD)
                mixed = _dot(wm, vb[rws], NN) + bst_ref[:, h:h + 1]
                dyc = dy_ref[rws, cols]
                du_parts.append(dyc * mixed)
                dmixed = dyc * u[rws]
                dmb = dmixed.astype(BF16)
                dws = dws + _dot(dmb, vb[rws], NT)
                db = db + jnp.sum(dmixed, axis=1, keepdims=True)
                dvln_parts.append(_dot(wmt, dmb, NN))
            du = jnp.concatenate(du_parts, axis=0)
            dvln = jnp.concatenate(dvln_parts, axis=0)
            dzu, dzv, dgv = vjp((du, dvln))
            dz_ref[:, cols] = dzu.astype(BF16)
            dz_ref[:, slice(GW + h * HD, GW + (h + 1) * HD)] = dzv.astype(BF16)
            dgv_ref[0, h:h + 1, :] = dgv
            dws_ref[0, h] = dws * tril
            db_ref[0, h] = jnp.broadcast_to(db, (HD, LANES))

    blk = pl.BlockSpec((tb, GW), lambda i: (i, 0))
    return pl.pallas_call(
        body, name="gmlp_bwd", grid=(nb,),
        in_specs=[blk, pl.BlockSpec((tb, GW), lambda i: (i, 1)), blk,
                  pl.BlockSpec((NH, HD), lambda i: (0, 0)), pl.BlockSpec((NH, HD, HD), lambda i: (0, 0, 0)),
                  pl.BlockSpec((HD, NH), lambda i: (0, 0))] + [ANY] * len(after),
        out_specs=[pl.BlockSpec((tb, 2 * GW), lambda i: (i, 0)), pl.BlockSpec((1, NH, HD), lambda i: (i, 0, 0)),
                   pl.BlockSpec((1, NH, HD, HD), lambda i: (i, 0, 0, 0)),
                   pl.BlockSpec((1, NH, HD, LANES), lambda i: (i, 0, 0, 0))],
        out_shape=[jax.ShapeDtypeStruct((s, IN_COLS), BF16),
                   jax.ShapeDtypeStruct((nb, NH, HD), F32), jax.ShapeDtypeStruct((nb, NH, HD, HD), F32),
                   jax.ShapeDtypeStruct((nb, NH, HD, LANES), F32)],
        compiler_params=_cparams(),
    )(z, z, dy, gv, ws, bst, *after)


POOL_TILE = 1024


def _pool_count(t0, window):
    pos = (t0 + lax.broadcasted_iota(jnp.int32, (POOL_TILE, LANES), 0)).astype(F32)
    return jnp.minimum(pos + 1.0, float(window))


def _window_sum(win, levels, back):
    n = win.shape[0]
    for lv in range(levels):
        step = 1 << lv
        win = win + pltpu.roll(win, n - step if back else step, 0)
    return win


def _pool_pooled(ppad_ref, t0, g):
    win = ppad_ref[pl.ds(t0, POOL_TILE + POOL_HALO), :]
    wsum = _window_sum(win, g + 1, False)[POOL_HALO:]
    return wsum / _pool_count(t0, POOL_WINDOWS[g]) - win[POOL_HALO:]


def _pool_fwd(z, wp, sp, y):
    s = z.shape[0]
    nt = s // POOL_TILE

    def body(p_ref, wp_ref, sp_ref, _, y_ref, ppad):
        for g in range(NG):
            cols = slice(g * LANES, (g + 1) * LANES)
            ppad[pl.ds(0, POOL_HALO), :] = jnp.zeros((POOL_HALO, LANES), F32)
            ppad[pl.ds(POOL_HALO, s), :] = p_ref[:, cols]
            wpb = wp_ref[g].astype(BF16)
            scale = sp_ref[:, cols]

            def tile(t, carry):
                t0 = pl.multiple_of(t * POOL_TILE, POOL_TILE)
                pooled = _pool_pooled(ppad, t0, g)
                y_ref[pl.ds(t0, POOL_TILE), cols] = (_dot(pooled.astype(BF16), wpb, NN) * scale).astype(BF16)
                return carry

            lax.fori_loop(0, nt, tile, 0)

    return pl.pallas_call(
        body, name="pool_fwd", grid=(1,),
        in_specs=[pl.BlockSpec((s, PW), lambda i: (0, 2 * GW // PW)),
                  pl.BlockSpec((NG, LANES, LANES), lambda i: (0, 0, 0)), pl.BlockSpec((1, PW), lambda i: (0, 0)), ANY],
        out_specs=pl.BlockSpec((s, PW), lambda i: (0, GW // PW)),
        out_shape=jax.ShapeDtypeStruct((s, D), BF16), input_output_aliases={3: 0},
        scratch_shapes=[pltpu.VMEM((s + POOL_HALO, LANES), F32)], compiler_params=_cparams(),
    )(z, wp, sp, y)


def _pool_bwd(z, dy, wp, sp, dz):
    s = z.shape[0]
    nt = s // POOL_TILE

    def body(p_ref, dy_ref, wp_ref, sp_ref, _, dp_ref, dwp_ref, dsp_ref, ppad, rpad, dpool):
        for g in range(NG):
            cols = slice(g * LANES, (g + 1) * LANES)
            ppad[pl.ds(0, POOL_HALO), :] = jnp.zeros((POOL_HALO, LANES), F32)
            ppad[pl.ds(POOL_HALO, s), :] = p_ref[:, cols]
            rpad[pl.ds(s, POOL_HALO), :] = jnp.zeros((POOL_HALO, LANES), F32)
            wpb = wp_ref[g].astype(BF16)
            scale = sp_ref[:, cols]

            def tile(t, carry):
                dwp, dsp = carry
                t0 = pl.multiple_of(t * POOL_TILE, POOL_TILE)
                pooled = _pool_pooled(ppad, t0, g)
                pb = pooled.astype(BF16)
                dyt = dy_ref[pl.ds(t0, POOL_TILE), cols]
                dsp = dsp + jnp.sum(dyt * _dot(pb, wpb, NN), axis=0, keepdims=True)
                dmm = (dyt * scale).astype(BF16)
                dwp = dwp + _dot(pb, dmm, TN)
                dpooled = _dot(dmm, wpb, NT)
                rpad[pl.ds(t0, POOL_TILE), :] = dpooled / _pool_count(t0, POOL_WINDOWS[g])
                dpool[pl.ds(t0, POOL_TILE), :] = dpooled
                return dwp, dsp

            dwp, dsp = lax.fori_loop(0, nt, tile, (jnp.zeros((LANES, LANES), F32), jnp.zeros((1, LANES), F32)))
            dwp_ref[g] = dwp
            dsp_ref[:, cols] = dsp

            def tile2(t, carry):
                t0 = pl.multiple_of(t * POOL_TILE, POOL_TILE)
                win = rpad[pl.ds(t0, POOL_TILE + POOL_HALO), :]
                back = _window_sum(win, g + 1, True)[:POOL_TILE]
                rows = pl.ds(t0, POOL_TILE)
                dp_ref[rows, cols] = (back - dpool[rows, :]).astype(BF16)
                return carry

            lax.fori_loop(0, nt, tile2, 0)

    return pl.pallas_call(
        body, name="pool_bwd", grid=(1,),
        in_specs=[pl.BlockSpec((s, PW), lambda i: (0, 2 * GW // PW)), pl.BlockSpec((s, PW), lambda i: (0, GW // PW)),
                  pl.BlockSpec((NG, LANES, LANES), lambda i: (0, 0, 0)), pl.BlockSpec((1, PW), lambda i: (0, 0)), ANY],
        out_specs=[pl.BlockSpec((s, PW), lambda i: (0, 2 * GW // PW)),
                   pl.BlockSpec((NG, LANES, LANES), lambda i: (0, 0, 0)), pl.BlockSpec((1, PW), lambda i: (0, 0))],
        out_shape=[jax.ShapeDtypeStruct((s, IN_COLS), BF16), jax.ShapeDtypeStruct((NG, LANES, LANES), F32),
                   jax.ShapeDtypeStruct((1, PW), F32)],
        input_output_aliases={4: 0},
        scratch_shapes=[pltpu.VMEM((s + POOL_HALO, LANES), F32), pltpu.VMEM((s + POOL_HALO, LANES), F32),
                        pltpu.VMEM((s, LANES), F32)],
        compiler_params=_cparams(),
    )(z, dy, wp, sp, dz)


CONV_LEAD = CONV_HALO - (CONV_K - 1)


SUBLANES = 8


def _sublane_shifts(win):
    n = win.shape[0]
    return [win] + [pltpu.roll(win, n - b, 0) for b in range(1, SUBLANES)]


def _shifted(shifts, offset):
    a, b = divmod(offset, SUBLANES)
    return shifts[b][a * SUBLANES:a * SUBLANES + ROW_TILE]


def _conv_taps(shifts, wdw_ref, lead, reverse):
    acc = jnp.zeros((ROW_TILE, CW), F32)
    for j in range(CONV_K):
        tap = (CONV_K - 1 - j) if reverse else j
        acc = acc + wdw_ref[tap:tap + 1, :] * _shifted(shifts, lead + j)
    return acc


def _conv_fill_glu(cv_ref, cg_ref, xpad, s):
    xpad[pl.ds(0, CONV_HALO), :] = jnp.zeros((CONV_HALO, CW), F32)

    def fill(t, carry):
        t0 = pl.multiple_of(t * ROW_TILE, ROW_TILE)
        rows = pl.ds(t0, ROW_TILE)
        xpad[pl.ds(t0 + CONV_HALO, ROW_TILE), :] = _glu(cv_ref[rows, :], cg_ref[rows, :])
        return carry

    lax.fori_loop(0, s // ROW_TILE, fill, 0)


def _conv_fwd(z, wdw, bdw, lng, lnb, y):
    s = z.shape[0]

    def body(cv_ref, cg_ref, wdw_ref, bdw_ref, lng_ref, lnb_ref, _, y_ref, xpad):
        _conv_fill_glu(cv_ref, cg_ref, xpad, s)

        def tile(t, carry):
            t0 = pl.multiple_of(t * ROW_TILE, ROW_TILE)
            shifts = _sublane_shifts(xpad[pl.ds(t0, ROW_TILE + CONV_HALO), :])
            hc = _conv_taps(shifts, wdw_ref, CONV_LEAD, False) + bdw_ref[...]
            y_ref[pl.ds(t0, ROW_TILE), :] = _ln_silu(hc, lng_ref[...], lnb_ref[...]).astype(BF16)
            return carry

        lax.fori_loop(0, s // ROW_TILE, tile, 0)

    vec = pl.BlockSpec((1, CW), lambda i: (0, 0))
    return pl.pallas_call(
        body, name="conv_fwd", grid=(1,),
        in_specs=[pl.BlockSpec((s, CW), lambda i: (0, (2 * GW + PW) // CW)),
                  pl.BlockSpec((s, CW), lambda i: (0, (2 * GW + PW) // CW + 1)),
                  pl.BlockSpec((CONV_K + 1, CW), lambda i: (0, 0)), vec, vec, vec, ANY],
        out_specs=pl.BlockSpec((s, CW), lambda i: (0, (GW + PW) // CW)),
        out_shape=jax.ShapeDtypeStruct((s, D), BF16), input_output_aliases={6: 0},
        scratch_shapes=[pltpu.VMEM((s + CONV_HALO, CW), F32)], compiler_params=_cparams(),
    )(z, z, wdw, bdw, lng, lnb, y)


def _conv_bwd(z, dy, wdw, bdw, lng, lnb, dz):
    s = z.shape[0]

    def body(cv_ref, cg_ref, dy_ref, wdw_ref, bdw_ref, lng_ref, lnb_ref, _,
             dz_ref, dwdw_ref, dbdw_ref, dlng_ref, dlnb_ref, xpad, dpad, dcg_keep):
        @pl.when(pl.program_id(0) == 0)
        def _():
            compute(cv_ref, cg_ref, dy_ref, wdw_ref, bdw_ref, lng_ref, lnb_ref,
                    dz_ref, dcg_keep, dwdw_ref, dbdw_ref, dlng_ref, dlnb_ref, xpad, dpad)

        @pl.when(pl.program_id(0) == 1)
        def _():
            dz_ref[...] = dcg_keep[...]

    def compute(cv_ref, cg_ref, dy_ref, wdw_ref, bdw_ref, lng_ref, lnb_ref,
                dcv_ref, dcg_ref, dwdw_ref, dbdw_ref, dlng_ref, dlnb_ref, xpad, dpad):
        _conv_fill_glu(cv_ref, cg_ref, xpad, s)
        dpad[pl.ds(s, CONV_HALO), :] = jnp.zeros((CONV_HALO, CW), F32)
        dwdw_ref[...] = jnp.zeros((CONV_K + 1, CW), F32)

        def tile(t, carry):
            db, dg, dbeta = carry
            t0 = pl.multiple_of(t * ROW_TILE, ROW_TILE)
            shifts = _sublane_shifts(xpad[pl.ds(t0, ROW_TILE + CONV_HALO), :])
            hc = _conv_taps(shifts, wdw_ref, CONV_LEAD, False) + bdw_ref[...]
            _, vjp = jax.vjp(_ln_silu, hc, lng_ref[...], lnb_ref[...])
            dhc, dg_t, dbeta_t = vjp(dy_ref[pl.ds(t0, ROW_TILE), :])
            dpad[pl.ds(t0, ROW_TILE), :] = dhc
            for j in range(CONV_K):
                dwdw_ref[j:j + 1, :] += jnp.sum(dhc * _shifted(shifts, CONV_LEAD + j), axis=0, keepdims=True)
            return db + jnp.sum(dhc, axis=0, keepdims=True), dg + dg_t, dbeta + dbeta_t

        zero = jnp.zeros((1, CW), F32)
        db, dg, dbeta = lax.fori_loop(0, s // ROW_TILE, tile, (zero, zero, zero))
        dbdw_ref[...] = db
        dlng_ref[...] = dg
        dlnb_ref[...] = dbeta

        def tile2(t, carry):
            t0 = pl.multiple_of(t * ROW_TILE, ROW_TILE)
            rows = pl.ds(t0, ROW_TILE)
            dglu = _conv_taps(_sublane_shifts(dpad[pl.ds(t0, ROW_TILE + CONV_HALO), :]), wdw_ref, 0, True)
            _, vjp = jax.vjp(_glu, cv_ref[rows, :], cg_ref[rows, :])
            dcv, dcg = vjp(dglu)
            dcv_ref[rows, :] = dcv.astype(BF16)
            dcg_ref[rows, :] = dcg.astype(BF16)
            return carry

        lax.fori_loop(0, s // ROW_TILE, tile2, 0)

    vec = pl.BlockSpec((1, CW), lambda i: (0, 0))
    wspec = pl.BlockSpec((CONV_K + 1, CW), lambda i: (0, 0))
    vshape = jax.ShapeDtypeStruct((1, CW), F32)
    return pl.pallas_call(
        body, name="conv_bwd", grid=(2,),
        in_specs=[pl.BlockSpec((s, CW), lambda i: (0, (2 * GW + PW) // CW)),
                  pl.BlockSpec((s, CW), lambda i: (0, (2 * GW + PW) // CW + 1)),
                  pl.BlockSpec((s, CW), lambda i: (0, (GW + PW) // CW)), wspec, vec, vec, vec, ANY],
        out_specs=[pl.BlockSpec((s, CW), lambda i: (0, (2 * GW + PW) // CW + i)), wspec, vec, vec, vec],
        out_shape=[jax.ShapeDtypeStruct((s, IN_COLS), BF16), jax.ShapeDtypeStruct((CONV_K + 1, CW), F32),
                   vshape, vshape, vshape],
        input_output_aliases={7: 0},
        scratch_shapes=[pltpu.VMEM((s + CONV_HALO, CW), F32), pltpu.VMEM((s + CONV_HALO, CW), F32),
                        pltpu.VMEM((s, CW), BF16)],
        compiler_params=_cparams(),
    )(z, z, dy, wdw, bdw, lng, lnb, dz)


def _softmax_rows(sc):
    e = jnp.exp(sc - jnp.max(sc, axis=-1, keepdims=True))
    return e / jnp.sum(e, axis=-1, keepdims=True)


def _attn_fwd(q, k, v, tq):
    s, m = q.shape[0], k.shape[0]
    tq = min(tq, s)

    def body(q_ref, k_ref, v_ref, o_ref):
        for h in range(XH):
            cols = slice(h * XHD, (h + 1) * XHD)
            p = _softmax_rows(_dot(q_ref[:, cols], k_ref[:, cols], NT) * ATT_SCALE)
            o_ref[:, cols] = _dot(p.astype(BF16), v_ref[:, cols], NN).astype(BF16)

    kv = pl.BlockSpec((m, D), lambda i: (0, 0))
    return pl.pallas_call(
        body, name="attn_fwd", grid=(s // tq,),
        in_specs=[pl.BlockSpec((tq, D), lambda i: (i, 0)), kv, kv],
        out_specs=pl.BlockSpec((tq, D), lambda i: (i, 0)),
        out_shape=jax.ShapeDtypeStruct((s, D), BF16), compiler_params=_cparams(),
    )(q, k, v)


def _attn_bwd(q, k, v, do, tq, after=()):
    s, m = q.shape[0], k.shape[0]
    tq = min(tq, s)

    def body(q_ref, k_ref, v_ref, do_ref, *rest):
        dq_ref, dk_ref, dv_ref = rest[len(after):]

        @pl.when(pl.program_id(0) == 0)
        def _():
            dk_ref[...] = jnp.zeros_like(dk_ref)
            dv_ref[...] = jnp.zeros_like(dv_ref)

        for h in range(XH):
            cols = slice(h * XHD, (h + 1) * XHD)
            qh, kh, vh, doh = q_ref[:, cols], k_ref[:, cols], v_ref[:, cols], do_ref[:, cols]
            p = _softmax_rows(_dot(qh, kh, NT) * ATT_SCALE)
            dp = _dot(doh, vh, NT)
            dv_ref[:, cols] += _dot(p.astype(BF16), doh, TN)
            ds = (p * (dp - jnp.sum(p * dp, axis=-1, keepdims=True)) * ATT_SCALE).astype(BF16)
            dq_ref[:, cols] = _dot(ds, kh, NN).astype(BF16)
            dk_ref[:, cols] += _dot(ds, qh, TN)

    kv = pl.BlockSpec((m, D), lambda i: (0, 0))
    qs = pl.BlockSpec((tq, D), lambda i: (i, 0))
    return pl.pallas_call(
        body, name="attn_bwd", grid=(s // tq,),
        in_specs=[qs, kv, kv, qs] + [ANY] * len(after), out_specs=[qs, kv, kv],
        out_shape=[jax.ShapeDtypeStruct((s, D), BF16), jax.ShapeDtypeStruct((m, D), F32),
                   jax.ShapeDtypeStruct((m, D), F32)],
        compiler_params=_cparams(),
    )(q, k, v, do, *after)


def _loss_head(y, target, tm):
    s = y.shape[0]
    tm = min(tm, s)

    def body(y_ref, t_ref, dy_ref, part_ref):
        err = y_ref[...] - t_ref[...]
        dy_ref[...] = err * (1.0 / D)
        part_ref[...] = jnp.full((1, 8, LANES), 0.5 * jnp.sum(err * err) * (1.0 / D), F32)

    blk = pl.BlockSpec((tm, D), lambda i: (i, 0))
    return pl.pallas_call(
        body, name="loss_head", grid=(s // tm,), in_specs=[blk, blk],
        out_specs=[blk, pl.BlockSpec((1, 8, LANES), lambda i: (i, 0, 0))],
        out_shape=[jax.ShapeDtypeStruct((s, D), F32), jax.ShapeDtypeStruct((s // tm, 8, LANES), F32)],
        compiler_params=_cparams(),
    )(y, target)


def _layer_fwd(x0, mem, w, p, fetch):
    z, hn0 = _rowop_mm("mix_in", "rms", (x0,), p["norm_mix_pre"], w["w_in"], NT, F32)
    y = _gmlp_fwd(z, p["gmlp_v_gain"], p["w_spatial"], p["b_spatial_t"], 1024)
    y = _pool_fwd(z, p["w_pool"], p["s_pool"], y)
    y = _conv_fwd(z, p["w_dw"], p["b_dw"], p["conv_ln_g"], p["conv_ln_b"], y)
    w.update(fetch("out", (y,)))
    x1, h0 = _mm_rowop("mix_out", "rms_res", [(y, w["w_out"], NN)], (x0,), p["norm_mix_post"])
    w.update(fetch("att", (x1,)))
    q, hn1 = _rowop_mm("att_q", "rms", (x1,), p["norm_xattn_pre"], w["w_q"], NN, BF16)
    k, mn = _rowop_mm("att_k", "rms", (mem,), p["norm_mem"], w["w_k"], NN, BF16, after=(x1,))
    v, _ = _rowop_mm("att_v", "rms", (mem,), p["norm_mem"], w["w_v"], NN, BF16, after=(x1,))
    o = _attn_fwd(q, k, v, 1024)
    x2, h1 = _mm_rowop("att_o", "rms_res", [(o, w["w_o"], NN)], (x1,), p["norm_xattn_post"])
    w.update(fetch("up", (x2,)))
    u, hn2 = _rowop_mm("ffn_up", "rms", (x2,), p["norm_ffn_pre"], w["w_up"], NT, F32)
    w.update(fetch("down", (u,)))
    x3, h2 = _mm_rowop("ffn_down", "rms_res", [(u, w["w_down"], NN)], (x2,), p["norm_ffn_post"], relu2=True)
    saved = dict(x0=x0, z=z, hn0=hn0, y=y, h0=h0, x1=x1, q=q, hn1=hn1, k=k, v=v, mn=mn, o=o, h1=h1, x2=x2, u=u,
                 hn2=hn2, h2=h2)
    return x3, saved


def _layer_bwd(dx3, mem, w, p, sv, red):
    gs = {}
    du, dh2, dg = _rowop_mm("ffn_down_bwd", "rms_bwd", (sv["h2"], dx3), p["norm_ffn_post"], w["w_down"], NT, BF16,
                            u=sv["u"], after=red.after())
    gs["norm_ffn_post"] = jnp.sum(dg, axis=0)
    g_down = _mm_tn("ffn_down_dw", sv["u"], dh2, relu2=True)
    red.advance((g_down,))
    dx2, dg = _mm_rowop("ffn_up_bwd", "rms_bwd_res", [(du, w["w_up"], NN)], (sv["x2"], dx3), p["norm_ffn_pre"],
                        after=red.after())
    gs["norm_ffn_pre"] = jnp.sum(dg, axis=0)
    g_up = _mm_tn("ffn_up_dw", du, sv["hn2"])
    red.add("ffn", ("w_down", "w_up"), [g_down, g_up])
    do, dh1, dg = _rowop_mm("att_o_bwd", "rms_bwd", (sv["h1"], dx2), p["norm_xattn_post"], w["w_o"], NT, BF16,
                            after=red.after())
    gs["norm_xattn_post"] = jnp.sum(dg, axis=0)
    g_o = _mm_tn("att_o_dw", sv["o"], dh1)
    red.advance((g_o,))
    dq, dk, dv = _attn_bwd(sv["q"], sv["k"], sv["v"], do, 1024, after=red.after())
    dk, dv = dk.astype(BF16), dv.astype(BF16)
    dx1, dg = _mm_rowop("att_q_bwd", "rms_bwd_res", [(dq, w["w_q"], NT)], (sv["x1"], dx2), p["norm_xattn_pre"],
                        after=red.after())
    gs["norm_xattn_pre"] = jnp.sum(dg, axis=0)
    g_q = _mm_tn("att_q_dw", sv["hn1"], dq)
    g_k = _mm_tn("att_k_dw", sv["mn"], dk)
    g_v = _mm_tn("att_v_dw", sv["mn"], dv)
    (dg,) = _mm_rowop("att_kv_bwd", "rms_bwd_gain", [(dk, w["w_k"], NT), (dv, w["w_v"], NT)], (mem,), p["norm_mem"])
    gs["norm_mem"] = jnp.sum(dg, axis=0)
    red.add("att", ("w_o", "w_q", "w_k", "w_v"), [g_o, g_q, g_k, g_v])
    dy, dh0, dg = _rowop_mm("mix_out_bwd", "rms_bwd", (sv["h0"], dx1), p["norm_mix_post"], w["w_out"], NT, F32,
                            after=red.after())
    gs["norm_mix_post"] = jnp.sum(dg, axis=0)
    g_out = _mm_tn("mix_out_dw", sv["y"], dh0)
    red.advance((g_out,))
    red.add("out", ("w_out",), [g_out])
    z = sv["z"]
    dz, dgv, dws, dbs = _gmlp_bwd(z, dy, p["gmlp_v_gain"], p["w_spatial"], p["b_spatial_t"], 512, after=red.after())
    gs["gmlp_v_gain"] = jnp.sum(dgv, axis=0)
    gs["w_spatial"] = jnp.sum(dws, axis=0)
    gs["b_spatial"] = jnp.sum(dbs[..., 0], axis=0)
    dz, gs["w_pool"], gs["s_pool"] = _pool_bwd(z, dy, p["w_pool"], p["s_pool"], dz)
    dz, dwdw, gs["b_dw"], gs["conv_ln_g"], gs["conv_ln_b"] = _conv_bwd(
        z, dy, p["w_dw"], p["b_dw"], p["conv_ln_g"], p["conv_ln_b"], dz)
    red.advance((dz,))
    g_in = _mm_tn("mix_in_dw", dz, sv["hn0"], after=red.after())
    red.add("in", ("w_in",), [g_in])
    if red.layer == 0:
        red.advance(())
    red.small("mixer", _small_grad_arrays(gs, dwdw, norms=False))
    dx0, dg = _mm_rowop("mix_in_bwd", "rms_bwd_res", [(dz, w["w_in"], NN)], (sv["x0"], dx1), p["norm_mix_pre"],
                        after=red.after())
    gs["norm_mix_pre"] = jnp.sum(dg, axis=0)
    late = {"norms": jnp.concatenate([gs[n] for n in NORM_NAMES], axis=0)}
    if red.layer == 0:
        late["loss"] = red.extra[0]
    red.small("norms", late)
    return dx0


NORM_NAMES = ("norm_mix_pre", "norm_mix_post", "norm_xattn_pre", "norm_mem", "norm_xattn_post", "norm_ffn_pre",
              "norm_ffn_post")
VEC_NAMES = ("s_pool", "b_dw", "conv_ln_g", "conv_ln_b")
SMALL_ARRAYS = ("norms", "gain_bias", "w_spatial", "w_pool", "vecs", "w_dw")


def _small_grad_arrays(gs, dwdw, norms=True):
    out = {"norms": jnp.concatenate([gs[n] for n in NORM_NAMES], axis=0)} if norms else {}
    out.update({"gain_bias": jnp.concatenate([gs["gmlp_v_gain"], gs["b_spatial"]], axis=0),
                "w_spatial": gs["w_spatial"], "w_pool": gs["w_pool"],
                "vecs": jnp.concatenate([gs[n] for n in VEC_NAMES], axis=0), "w_dw": dwdw})
    return out


def _layer_params(small, l):
    p = {n: small[n][l].reshape(1, -1) for n in ("norm_mix_pre", "norm_mix_post", "s_pool", "b_dw", "conv_ln_g",
                                                   "conv_ln_b", "norm_xattn_pre", "norm_mem", "norm_xattn_post",
                                                   "norm_ffn_pre", "norm_ffn_post")}
    p["gmlp_v_gain"] = small["gmlp_v_gain"][l]
    p["w_spatial"] = small["w_spatial"][l]
    p["b_spatial_t"] = small["b_spatial"][l].T
    p["w_pool"] = small["w_pool"][l]
    p["w_dw"] = jnp.pad(small["w_dw"][l], ((0, 1), (0, 0)))
    return p


def _local_step(x, mem, target, fetch, small, red):
    small = dict(small)
    saved, weights, params = [], [], []
    h = x
    marker = ()
    for l in range(DEPTH):
        w = fetch(l, "in", marker)
        if "taps" in w:
            small["w_dw"] = w.pop("taps")
        p = _layer_params(small, l)
        h, sv = _layer_fwd(h, mem, w, p, functools.partial(fetch, l))
        marker = (h,)
        saved.append(sv)
        weights.append(w)
        params.append(p)
    dh, loss = _loss_head(h, target, 1024)
    red.extra = (loss,)
    for l in reversed(range(DEPTH)):
        red.layer = l
        dh = _layer_bwd(dh, mem, weights[l], params[l], saved[l], red)
    return loss, dh


HBM = pl.BlockSpec(memory_space=pltpu.HBM)


def _position():
    return lax.axis_index("x"), lax.axis_index("y"), lax.axis_index("c")


SEM = pl.BlockSpec(memory_space=pltpu.SEMAPHORE)
EFFECT = pltpu.SideEffectType.DATAFLOW_SIDE_EFFECTING
TOKEN = jax.ShapeDtypeStruct((8, LANES), F32)
TOKEN_SPEC = pl.BlockSpec(memory_space=pltpu.VMEM)


def _landing(shape, dtype):
    return pltpu.with_memory_space_constraint(lax.empty(shape, dtype), pltpu.HBM)


def _hbm_shapes(arrays):
    return [pltpu.HBM(a.shape, a.dtype) for a in arrays]


def _block(ref, r, dev):
    return ref.at[pl.ds((4 * dev[0] + 2 * dev[1] + dev[2]) * r, r), :]


def _split_call(name, body, thru, sems_in, after, sems_out, token):
    n = len(thru)
    out_shape = [pltpu.SemaphoreType.DMA(s) for s in sems_out] + _hbm_shapes(thru) + ([TOKEN] if token else [])
    out_specs = [SEM] * len(sems_out) + [HBM] * n + ([TOKEN_SPEC] if token else [])
    return pl.pallas_call(
        body, name=name, in_specs=[HBM] * n + [SEM] * len(sems_in) + [ANY] * len(after),
        out_specs=out_specs, out_shape=out_shape,
        input_output_aliases={i: len(sems_out) + i for i in range(n)},
        compiler_params=pltpu.CompilerParams(has_side_effects=EFFECT),
    )(*thru, *sems_in, *after)


def _place_own(name, srcs, dev, out_dtype, tr):
    n = len(srcs)
    r, cols = srcs[0][0].shape[-2:]
    tr = r if r < 16 else _row_tile(r, tr)
    nb = r // tr

    def body(dev_ref, *refs):
        for a in range(n):
            refs[n + a][...] = refs[a][...].astype(out_dtype)

    in_specs = [pl.BlockSpec((tr, cols), lambda i, d: (i, 0)) if l is None
                else pl.BlockSpec((None, tr, cols), lambda i, d, l=l: (l, i, 0)) for _, l in srcs]
    return pl.pallas_call(
        body, name=name,
        grid_spec=pltpu.PrefetchScalarGridSpec(
            num_scalar_prefetch=1, grid=(nb,), in_specs=in_specs,
            out_specs=[pl.BlockSpec((tr, cols), lambda i, d: (d[0] * nb + i, 0))] * n),
        out_shape=[jax.ShapeDtypeStruct((N_DEV * r, cols), out_dtype)] * n, compiler_params=_cparams(),
    )(dev, *[a for a, _ in srcs])


def _gather_peers(x, y, c):
    return [(1 - x, y, c), (x, 1 - y, c), (1 - x, 1 - y, c), (x, y, 1 - c)]


def _block_rows(land):
    return land.shape[0] // N_DEV


def _near_peers(x, y, c):
    return [(1 - x, y, c), (x, 1 - y, c), (x, y, 1 - c)]


def _relay_route(x, y, c):
    origin = (x + c * (1 - 2 * x), y + (1 - c) * (1 - 2 * y), c)
    target = (x + (1 - c) * (1 - 2 * x), y + c * (1 - 2 * y), c)
    return origin, target


def _same_block_copy(blk, send_sem, recv_sem, to):
    return pltpu.make_async_remote_copy(src_ref=blk, dst_ref=blk, send_sem=send_sem, recv_sem=recv_sem, device_id=to,
                                        device_id_type=MESH)


def _gather_start(name, lands, after):
    n = len(lands)

    def body(*refs):
        lz = refs[:n]
        send_sems, recv_sems = refs[n + len(after)], refs[n + len(after) + 1]
        token = refs[-1]
        x, y, c = _position()
        for a in range(n):
            own = _block(lz[a], _block_rows(lands[a]), (x, y, c))
            for k, to in enumerate(_near_peers(x, y, c)):
                _same_block_copy(own, send_sems.at[k], recv_sems.at[k], to).start()
        token[...] = jnp.zeros_like(token)

    out = _split_call(name, body, list(lands), [], after, [(3,), (3,)], True)
    return out[0], out[1], out[2:2 + n], out[-1]


def _gather_step(name, near, far, fresh, after):
    groups = [g for g in (near and near[0], far and far[0], fresh) if g]
    counts = [len(near[0]) if near else 0, len(far[0]) if far else 0, len(fresh) if fresh else 0]
    n = sum(counts)
    sems_in = ([near[1]] if near else []) + ([far[1]] if far else [])
    sems_out = ([(2,), (2,), (1,), (1,)] if near else []) + ([(1,), (1,)] if far else []) + ([(3,), (3,)] if fresh else [])

    def body(*refs):
        lz = list(refs[:n])
        ins = list(refs[n:n + len(sems_in)])
        outs = list(refs[n + len(sems_in) + len(after):n + len(sems_in) + len(after) + len(sems_out)])
        token = refs[-1]
        x, y, c = _position()
        me, sibling = (x, y, c), (x, y, 1 - c)
        near_lz, far_lz, fresh_lz = (lz[sum(counts[:i]):sum(counts[:i + 1])] for i in range(3))
        neighbours = _near_peers(x, y, c)[:2]
        origin, target = _relay_route(x, y, c)
        diagonal = (1 - x, 1 - y, c)
        if near:
            recv0 = ins.pop(0)
            fsend, frecv, rsend, rrecv = (outs.pop(0) for _ in range(4))
            for a, land in enumerate(near[0]):
                for j, chip in enumerate(neighbours):
                    _same_block_copy(_block(near_lz[a], _block_rows(land), chip), fsend.at[j], recv0.at[j], me).wait_recv()
        if far:
            rrecv_in = ins.pop(0)
            f2send, f2recv = outs.pop(0), outs.pop(0)
            for a, land in enumerate(far[0]):
                _same_block_copy(_block(far_lz[a], _block_rows(land), diagonal), f2send.at[0], rrecv_in.at[0], me).wait_recv()
            for a, land in enumerate(far[0]):
                _same_block_copy(_block(far_lz[a], _block_rows(land), diagonal), f2send.at[0], f2recv.at[0], sibling).start()
        if near:
            for a, land in enumerate(near[0]):
                r = _block_rows(land)
                _same_block_copy(_block(near_lz[a], r, origin), rsend.at[0], rrecv.at[0], target).start()
                for j, chip in enumerate(neighbours):
                    _same_block_copy(_block(near_lz[a], r, chip), fsend.at[j], frecv.at[j], sibling).start()
        if fresh:
            send_sems, recv_sems = outs.pop(0), outs.pop(0)
            for a, land in enumerate(fresh):
                own = _block(fresh_lz[a], _block_rows(land), me)
                for k, to in enumerate(_near_peers(x, y, c)):
                    _same_block_copy(own, send_sems.at[k], recv_sems.at[k], to).start()
        token[...] = jnp.zeros_like(token)

    out = list(_split_call(name, body, [l for g in groups for l in g], sems_in, after, sems_out, True))
    res = {"token": out.pop()}
    if near:
        res.update(fsend=out.pop(0), frecv=out.pop(0), rsend=out.pop(0), rrecv=out.pop(0))
    if far:
        res.update(f2send=out.pop(0), f2recv=out.pop(0))
    if fresh:
        res.update(send=out.pop(0), recv=out.pop(0))
    res["near"], res["far"], res["fresh"] = (out[sum(counts[:i]):sum(counts[:i + 1])] for i in range(3))
    return res


def _gather_finish(name, lands, send_sems, recv_sems, fsend, frecv, rsend, f2send, f2recv, after):
    n = len(lands)

    def body(*refs):
        lz = refs[:n]
        send0, recv0, fsend_ref, frecv_ref, rsend_ref, f2send_ref, f2recv_ref = refs[n:n + 7]
        x, y, c = _position()
        me = (x, y, c)
        near = _near_peers(x, y, c)[:2]
        origin, _ = _relay_route(x, y, c)
        for a in range(n):
            r = _block_rows(lands[a])
            sib = _block(lz[a], r, (x, y, 1 - c))
            _same_block_copy(sib, send0.at[2], recv0.at[2], me).wait_recv()
            for j, chip in enumerate(near):
                blk = _block(lz[a], r, (chip[0], chip[1], 1 - c))
                _same_block_copy(blk, fsend_ref.at[j], frecv_ref.at[j], me).wait_recv()
            far = _block(lz[a], r, (1 - x, 1 - y, 1 - c))
            _same_block_copy(far, f2send_ref.at[0], f2recv_ref.at[0], me).wait_recv()
            own = _block(lz[a], r, me)
            for k in range(3):
                _same_block_copy(own, send0.at[k], recv0.at[k], me).wait_send()
            for j, chip in enumerate(near):
                _same_block_copy(_block(lz[a], r, chip), fsend_ref.at[j], frecv_ref.at[j], me).wait_send()
            _same_block_copy(_block(lz[a], r, origin), rsend_ref.at[0], recv0.at[0], me).wait_send()
            _same_block_copy(_block(lz[a], r, (1 - x, 1 - y, c)), f2send_ref.at[0], f2recv_ref.at[0], me).wait_send()

    return _split_call(name, body, list(lands), [send_sems, recv_sems, fsend, frecv, rsend, f2send, f2recv], after, [],
                       False)


def _sibling_start(name, grads, after):
    n = len(grads)
    lands = [_landing((4, g.shape[0] // N_DEV, D), g.dtype) for g in grads]

    def body(*refs):
        ins, lz = refs[:n], refs[n:2 * n]
        send_sem, recv_sem = refs[2 * n + len(after)], refs[2 * n + len(after) + 1]
        token = refs[-1]
        x, y, c = _position()
        for a in range(n):
            r = grads[a].shape[0] // N_DEV
            for q in range(4):
                pltpu.make_async_remote_copy(
                    src_ref=ins[a].at[pl.ds((2 * q + 1 - c) * r, r), :], dst_ref=lz[a].at[q], send_sem=send_sem.at[0],
                    recv_sem=recv_sem.at[0], device_id=(x, y, 1 - c), device_id_type=MESH).start()
        token[...] = jnp.zeros_like(token)

    out = _split_call(name, body, list(grads) + lands, [], after, [(1,), (1,)], True)
    return out[0], out[1], out[2:2 + n], out[2 + n:2 + 2 * n], out[-1]


def _sibling_finish(name, grads, lands, send_sem, recv_sem, after):
    n = len(grads)

    def body(*refs):
        ins, lz = refs[:n], refs[n:2 * n]
        send_ref, recv_ref = refs[2 * n], refs[2 * n + 1]
        x, y, c = _position()
        for a in range(n):
            r = grads[a].shape[0] // N_DEV
            for q in range(4):
                cp = pltpu.make_async_remote_copy(
                    src_ref=ins[a].at[pl.ds((2 * q + 1 - c) * r, r), :], dst_ref=lz[a].at[q], send_sem=send_ref.at[0],
                    recv_sem=recv_ref.at[0], device_id=(x, y, c), device_id_type=MESH)
                cp.wait_send()
                cp.wait_recv()

    out = _split_call(name, body, list(grads) + list(lands), [send_sem, recv_sem], after, [], False)
    return out[:n], out[n:2 * n]


def _chip_start(name, parts, after):
    n = len(parts)
    lands = [_landing((3,) + p.shape[1:], p.dtype) for p in parts]

    def body(*refs):
        ins, lz = refs[:n], refs[n:2 * n]
        send_sems, recv_sems = refs[2 * n + len(after)], refs[2 * n + len(after) + 1]
        token = refs[-1]
        x, y, c = _position()
        for a in range(n):
            for j, chip in enumerate(_gather_peers(x, y, c)[:3]):
                pltpu.make_async_remote_copy(
                    src_ref=ins[a].at[2 * chip[0] + chip[1]], dst_ref=lz[a].at[j], send_sem=send_sems.at[j],
                    recv_sem=recv_sems.at[j], device_id=chip, device_id_type=MESH).start()
        token[...] = jnp.zeros_like(token)

    out = _split_call(name, body, list(parts) + lands, [], after, [(3,), (3,)], True)
    return out[0], out[1], out[2:2 + n], out[2 + n:2 + 2 * n], out[-1]


def _chip_finish(name, parts, lands, send_sems, recv_sems, after):
    n = len(parts)

    def body(*refs):
        ins, lz = refs[:n], refs[n:2 * n]
        send_ref, recv_ref = refs[2 * n], refs[2 * n + 1]
        me = _position()
        for a in range(n):
            for j in range(3):
                cp = pltpu.make_async_remote_copy(
                    src_ref=ins[a].at[j], dst_ref=lz[a].at[j], send_sem=send_ref.at[j], recv_sem=recv_ref.at[j],
                    device_id=me, device_id_type=MESH)
                cp.wait_send()
                cp.wait_recv()

    out = _split_call(name, body, list(parts) + list(lands), [send_sems, recv_sems], after, [], False)
    return out[:n], out[n:2 * n]


def _other_devices(x, y, c):
    return [(x + (k >> 2 & 1) * (1 - 2 * x), y + (k >> 1 & 1) * (1 - 2 * y), c + (k & 1) * (1 - 2 * c))
            for k in range(1, N_DEV)]


def _broadcast_start(name, arrays, after):
    n = len(arrays)
    lands = [_landing((N_DEV,) + a.shape, a.dtype) for a in arrays]

    def body(*refs):
        ins, lz = refs[:n], refs[n:2 * n]
        send_sems, recv_sems = refs[2 * n + len(after)], refs[2 * n + len(after) + 1]
        token = refs[-1]
        x, y, c = _position()
        for a in range(n):
            for k, peer in enumerate(_other_devices(x, y, c)):
                pltpu.make_async_remote_copy(
                    src_ref=ins[a], dst_ref=lz[a].at[4 * x + 2 * y + c], send_sem=send_sems.at[k],
                    recv_sem=recv_sems.at[k], device_id=peer, device_id_type=MESH).start()
        token[...] = jnp.zeros_like(token)

    out = _split_call(name, body, list(arrays) + lands, [], after, [(N_DEV - 1,), (N_DEV - 1,)], True)
    return out[0], out[1], out[2:2 + n], out[2 + n:2 + 2 * n], out[-1]


def _broadcast_finish(name, arrays, lands, send_sems, recv_sems, after):
    n = len(arrays)

    def body(*refs):
        ins, lz = refs[:n], refs[n:2 * n]
        send_ref, recv_ref = refs[2 * n], refs[2 * n + 1]
        x, y, c = _position()
        for a in range(n):
            for k, peer in enumerate(_other_devices(x, y, c)):
                cp = pltpu.make_async_remote_copy(
                    src_ref=ins[a], dst_ref=lz[a].at[4 * peer[0] + 2 * peer[1] + peer[2]], send_sem=send_ref.at[k],
                    recv_sem=recv_ref.at[k], device_id=(x, y, c), device_id_type=MESH)
                cp.wait_send()
                cp.wait_recv()

    out = _split_call(name, body, list(arrays) + list(lands), [send_sems, recv_sems], after, [], False)
    return out[:n], out[n:2 * n]


def _row_tile(r, target):
    return max(t for t in range(16, min(r, target) + 1, 16) if r % t == 0)


CHIP_PARTIAL_BYTES = 12 * 1024 * 1024


def _chip_partial(name, grads, gots, c):
    n = len(grads)
    r = grads[0].shape[0] // N_DEV
    tr = _row_tile(r, CHIP_PARTIAL_BYTES // (n * 3 * D * 2))

    def body(c_ref, *refs):
        for a in range(n):
            refs[2 * n + a][...] = (refs[a][...].astype(F32) + refs[n + a][...].astype(F32)).astype(BF16)

    blk = pl.BlockSpec((None, tr, D), lambda q, i, c_ref: (q, i, 0))
    return pl.pallas_call(
        body, name=name,
        grid_spec=pltpu.PrefetchScalarGridSpec(
            num_scalar_prefetch=1, grid=(4, r // tr),
            in_specs=[pl.BlockSpec((None, None, tr, D), lambda q, i, c_ref: (q, c_ref[0], i, 0))] * n + [blk] * n,
            out_specs=[blk] * n),
        out_shape=[jax.ShapeDtypeStruct((4, r, D), BF16)] * n, compiler_params=_cparams(),
    )(c, *[g.reshape(4, 2, r, D) for g in grads], *gots)


class _WeightGather:
    def __init__(self, groups):
        self.groups = list(groups)
        self.index = {key: i for i, (key, _, _) in enumerate(groups)}
        self.state = [None] * len(groups)
        self.token = ()
        for i in range(min(2, len(groups))):
            self._start(i)

    def _tag(self, i):
        return "%s_%d" % self.groups[i][0][::-1]

    def _start(self, i):
        send, recv, lz, tok = _gather_start("gather_start_" + self._tag(i), self.groups[i][2], self.token)
        self.state[i] = dict(send=send, recv=recv, lands=lz)
        self.token = (tok,)

    def _step(self, name, near, far, fresh, marker):
        exists = lambda i: i is not None and i < len(self.groups)
        near, far, fresh = (i if exists(i) else None for i in (near, far, fresh))
        res = _gather_step(
            name, None if near is None else (self.state[near]["lands"], self.state[near]["recv"]),
            None if far is None else (self.state[far]["lands"], self.state[far]["rrecv"]),
            None if fresh is None else self.groups[fresh][2], tuple(marker) + self.token)
        self.token = (res["token"],)
        if near is not None:
            self.state[near].update(lands=res["near"], fsend=res["fsend"], frecv=res["frecv"], rsend=res["rsend"],
                                    rrecv=res["rrecv"])
        if far is not None:
            self.state[far].update(lands=res["far"], f2send=res["f2send"], f2recv=res["f2recv"])
        if fresh is not None:
            self.state[fresh] = dict(send=res["send"], recv=res["recv"], lands=res["fresh"])

    def fetch(self, layer, group, marker):
        k = self.index[(layer, group)]
        if k == 0:
            self._step("gather_step_first", 0, None, None, marker)
        self._step("gather_step_" + self._tag(k), k + 1, k, k + 2, marker)
        st = self.state[k]
        lz = _gather_finish("gather_finish_" + self._tag(k), st["lands"], st["send"], st["recv"], st["fsend"],
                            st["frecv"], st["rsend"], st["f2send"], st["f2recv"], self.token)
        self.state[k] = None
        return dict(zip(self.groups[k][1], lz))


class _GradReduce:
    def __init__(self, core, chip):
        self.core, self.chip = core, chip
        self.layer = None
        self.token = ()
        self.at_sibling, self.at_chips = [], []
        self.extra, self.smalls = (), {}

    def after(self):
        return self.token

    def add(self, group, names, grads):
        tag = "%s_%d" % (group, self.layer)
        send, recv, grads, lands, tok = _sibling_start("grad_sibling_start_" + tag, grads, self.token)
        self.at_sibling.append((tag, [(self.layer, n) for n in names], send, recv, grads, lands))
        self.token = (tok,)

    def advance(self, marker):
        for tag, keys, send, recv, grads, lands in self.at_sibling:
            grads, lands = _sibling_finish("grad_sibling_finish_" + tag, grads, lands, send, recv, marker)
            parts = _chip_partial("chip_partial_" + tag, grads, lands, self.core)
            send, recv, parts, lands, tok = _chip_start("grad_chip_start_" + tag, parts, ())
            self.at_chips.append([tag, keys, send, recv, parts, lands])
            self.token = (tok,)
        self.at_sibling = []

    def small(self, part, arrays):
        keys = list(arrays)
        send, recv, own, slots, tok = _broadcast_start(
            "small_grads_start_%d_%s" % (self.layer, part), [arrays[k] for k in keys], self.token)
        self.smalls.setdefault(self.layer, []).append((part, keys, send, recv, own, slots))
        self.token = (tok,)

    def small_finish(self, layer, marker):
        mine, theirs = {}, {}
        for part, keys, send, recv, own, slots in self.smalls[layer]:
            own, slots = _broadcast_finish("small_grads_finish_%d_%s" % (layer, part), own, slots, send, recv, marker)
            mine.update(zip(keys, own))
            theirs.update(zip(keys, slots))
        return mine, theirs

    def collect(self, key, marker):
        for entry in self.at_chips:
            tag, keys, send, recv, parts, lands = entry
            if key in keys:
                if send is not None:
                    parts, lands = _chip_finish("grad_chip_finish_" + tag, parts, lands, send, recv, marker)
                    entry[2:] = [None, None, parts, lands]
                i = keys.index(key)
                return parts[i], lands[i]
        raise KeyError(key)


def _adamw_math(w, g, m, v):
    m = ADAM_B1 * m + (1.0 - ADAM_B1) * g
    v = ADAM_B2 * v + (1.0 - ADAM_B2) * jnp.square(g)
    m_hat = m / (1.0 - ADAM_B1 ** ADAM_STEP)
    v_hat = v / (1.0 - ADAM_B2 ** ADAM_STEP)
    delta = -ADAM_LR * (m_hat / (jnp.sqrt(v_hat) + ADAM_EPS) + ADAM_WD * w)
    return delta, m, v


def _adamw_small(wts, mom_m, mom_v, own, gathered, loss_own, loss_gathered, dev):
    names = SMALL
    nw = len(names)
    na = len(SMALL_ARRAYS)

    def body(dev_ref, *refs):
        w_refs, m_refs, v_refs = (dict(zip(names, refs[i * nw:(i + 1) * nw])) for i in range(3))
        own_refs = refs[3 * nw:3 * nw + DEPTH * na]
        g_refs = refs[3 * nw + DEPTH * na:3 * nw + 2 * DEPTH * na]
        loss_own_ref, loss_got_ref = refs[3 * nw + 2 * DEPTH * na:3 * nw + 2 * DEPTH * na + 2]
        outs = refs[3 * nw + 2 * DEPTH * na + 2:]
        g_out, d_out, m_out, v_out = (dict(zip(names, outs[i * nw:(i + 1) * nw])) for i in range(4))
        me = dev_ref[0]

        loss = None
        for d in range(N_DEV):
            for b in range(loss_own.shape[0]):
                term = jnp.where(me == d, loss_own_ref[b], loss_got_ref[d, b])
                loss = term if loss is None else loss + term
        outs[4 * nw][...] = loss

        def update(name, at, g):
            g_out[name][at] = g
            d_out[name][at], m_out[name][at], v_out[name][at] = _adamw_math(
                w_refs[name][at], g, m_refs[name][at], v_refs[name][at])

        for l in range(DEPTH):
            mine = dict(zip(SMALL_ARRAYS, own_refs[l * na:(l + 1) * na]))
            got = dict(zip(SMALL_ARRAYS, g_refs[l * na:(l + 1) * na]))

            def total(key, at):
                acc = None
                for d in range(N_DEV):
                    term = jnp.where(me == d, mine[key][at] if at else mine[key][...], got[key][(d,) + at])
                    acc = term if acc is None else acc + term
                return acc

            row = (slice(l, l + 1),)
            for k, name in enumerate(NORM_NAMES):
                update(name, row, total("norms", (slice(k, k + 1),)))
            for k, name in enumerate(VEC_NAMES):
                update(name, row, total("vecs", (slice(k, k + 1),)))
            update("gmlp_v_gain", (l,), total("gain_bias", (slice(0, NH),)))
            update("b_spatial", (l,), total("gain_bias", (slice(NH, 2 * NH),)))
            update("w_spatial", (l,), total("w_spatial", ()))
            update("w_pool", (l,), total("w_pool", ()))
            update("w_dw", (l,), total("w_dw", (slice(0, CONV_K),)))

    args = [src[n] for src in (wts, mom_m, mom_v) for n in names]
    args += [src[l][k] for src in (own, gathered) for l in range(DEPTH) for k in SMALL_ARRAYS]
    args += [loss_own, loss_gathered]
    outs = pl.pallas_call(
        body, name="adamw_small",
        in_specs=[pl.BlockSpec(memory_space=pltpu.SMEM)] + [pl.BlockSpec(memory_space=pltpu.VMEM)] * len(args),
        out_shape=[jax.ShapeDtypeStruct(wts[n].shape, F32) for _ in range(4) for n in names]
        + [jax.ShapeDtypeStruct((8, LANES), F32)],
        compiler_params=_cparams(),
    )(dev, *args)
    return tuple(dict(zip(names, outs[i * nw:(i + 1) * nw])) for i in range(4)) + (outs[4 * nw],)


def _adamw_layers(name, w, reduced, m, v, chip, tr, transposed=False, after=()):
    nl, r, cdim = w.shape
    tr = _row_tile(r, tr)
    nb = r // tr

    def body(q_ref, w_ref, p0_ref, g0_ref, p1_ref, g1_ref, m_ref, v_ref, *rest):
        g_ref, d_ref, nm_ref, nv_ref = rest[len(after):]

        def total(p_ref, got_ref):
            acc = p_ref[...].astype(F32)
            for j in range(3):
                acc = acc + got_ref[j].astype(F32)
            return acc

        g = jnp.where(pl.program_id(0) == 0, total(p0_ref, g0_ref), total(p1_ref, g1_ref))
        if transposed:
            g = g.T
        g_ref[...] = g
        d_ref[...], nm_ref[...], nv_ref[...] = _adamw_math(w_ref[...], g, m_ref[...], v_ref[...])

    blk = pl.BlockSpec((None, tr, cdim), lambda l, i, q: (l, i, 0))
    first = lambda l, i: i * (1 - l) + (nb - 1) * l
    second = lambda l, i: i * l
    if transposed:
        gshape = (cdim, tr)
        at = lambda lead, i: (lead, 0, i)
    else:
        gshape = (tr, cdim)
        at = lambda lead, i: (lead, i, 0)
    specs = [blk,
             pl.BlockSpec((None,) + gshape, lambda l, i, q: at(q[0], first(l, i))),
             pl.BlockSpec((3,) + gshape, lambda l, i, q: at(0, first(l, i))),
             pl.BlockSpec((None,) + gshape, lambda l, i, q: at(q[0], second(l, i))),
             pl.BlockSpec((3,) + gshape, lambda l, i, q: at(0, second(l, i))), blk, blk] + [ANY] * len(after)
    shape = jax.ShapeDtypeStruct((nl, r, cdim), F32)
    return pl.pallas_call(
        body, name=name,
        grid_spec=pltpu.PrefetchScalarGridSpec(num_scalar_prefetch=1, grid=(nl, nb), in_specs=specs, out_specs=[blk] * 4),
        out_shape=[shape] * 4, compiler_params=_cparams(),
    )(chip, w, *reduced[0], *reduced[1], m, v, *after)


def _to_rows(name, a):
    return jnp.swapaxes(a, 1, 2) if name == "w_in" else a


def _place_own_transposed(name, srcs, dev, out_dtype, tc):
    n = len(srcs)
    kdim, cdim = srcs[0][0].shape[-2:]

    def body(dev_ref, *refs):
        for a in range(n):
            refs[n + a][...] = refs[a][...].T.astype(out_dtype)

    return pl.pallas_call(
        body, name=name,
        grid_spec=pltpu.PrefetchScalarGridSpec(
            num_scalar_prefetch=1, grid=(kdim // tc,),
            in_specs=[pl.BlockSpec((None, tc, cdim), lambda i, d, l=l: (l, i, 0)) for _, l in srcs],
            out_specs=[pl.BlockSpec((cdim, tc), lambda i, d: (d[0], i))] * n),
        out_shape=[jax.ShapeDtypeStruct((N_DEV * cdim, kdim), out_dtype)] * n, compiler_params=_cparams(),
    )(dev, *[a for a, _ in srcs])


def _pack(arrays, rows):
    flat = jnp.concatenate([a.reshape(-1) for a in arrays])
    return jnp.pad(flat, (0, rows * D - flat.shape[0])).reshape(rows, D)


def _rows_for(shapes, mult=8):
    total = 0
    for shp in shapes:
        size = 1
        for dim in shp:
            size *= dim
        total += size
    return -(-total // (mult * D)) * mult


def kernel(x, mem, norm_mix_pre, norm_mix_post, w_in, w_out, gmlp_v_gain, w_spatial, b_spatial, w_pool, s_pool, w_dw, b_dw, conv_ln_g, conv_ln_b, norm_xattn_pre, norm_mem, norm_xattn_post, w_q, w_k, w_v, w_o, norm_ffn_pre, norm_ffn_post, w_up, w_down, loss_target, m_norm_mix_pre, m_norm_mix_post, m_w_in, m_w_out, m_gmlp_v_gain, m_w_spatial, m_b_spatial, m_w_pool, m_s_pool, m_w_dw, m_b_dw, m_conv_ln_g, m_conv_ln_b, m_norm_xattn_pre, m_norm_mem, m_norm_xattn_post, m_w_q, m_w_k, m_w_v, m_w_o, m_norm_ffn_pre, m_norm_ffn_post, m_w_up, m_w_down, v_norm_mix_pre, v_norm_mix_post, v_w_in, v_w_out, v_gmlp_v_gain, v_w_spatial, v_b_spatial, v_w_pool, v_s_pool, v_w_dw, v_b_dw, v_conv_ln_g, v_conv_ln_b, v_norm_xattn_pre, v_norm_mem, v_norm_xattn_post, v_w_q, v_w_k, v_w_v, v_w_o, v_norm_ffn_pre, v_norm_ffn_post, v_w_up, v_w_down):
    args = dict(locals())
    wts = {n: args[n] for n in WEIGHTS}
    mom_m = {n: args["m_" + n] for n in WEIGHTS}
    mom_v = {n: args["v_" + n] for n in WEIGHTS}
    xi, yi, ci = _position()
    me = 4 * xi + 2 * yi + ci

    dev = jnp.reshape(me, (1,)).astype(jnp.int32)
    lands = {}
    for call, names, tr in (("place_in", ("w_in",), 256), ("place_att", ("w_out", "w_q", "w_k", "w_v", "w_o"), 64),
                            ("place_up", ("w_up",), 256), ("place_down", ("w_down",), 256)):
        srcs = [(_to_rows(n, wts[n]), l) for l in range(DEPTH) for n in names]
        placed = (_place_own_transposed if names == ("w_up",) else _place_own)(call, srcs, dev, BF16, tr)
        lands.update(zip([(l, n) for l in range(DEPTH) for n in names], placed))
    (lands[(0, "taps")],) = _place_own("place_taps", [(_pack([w_dw], _rows_for([w_dw.shape])), None)], dev, F32, 8)
    groups = []
    for l in range(DEPTH):
        for group, names in GATHER_GROUPS:
            if (l, group) == (0, "in"):
                names = names + ("taps",)
            groups.append(((l, group), names, [lands[(l, n)] for n in names]))
    gather = _WeightGather(groups)

    def fetch(layer, group, marker):
        w = gather.fetch(layer, group, marker)
        if "taps" in w:
            blocks = w["taps"].reshape(N_DEV, -1)[:, :w_dw.size].reshape((N_DEV,) + w_dw.shape)
            w["taps"] = jnp.moveaxis(blocks, 0, 2).reshape(DEPTH, CONV_K, CW)
        return w

    reduce = _GradReduce(jnp.reshape(ci, (1,)).astype(jnp.int32), jnp.reshape(2 * xi + yi, (1,)).astype(jnp.int32))
    small = {n: wts[n] for n in SMALL if n != "w_dw"}
    _, dx = _local_step(x[0], mem[0], loss_target[0], fetch, small, reduce)
    reduce.advance((dx,))

    grad_w, delta, new_m, new_v = {}, {}, {}, {}
    marker = (dx,) + tuple(reduce.after())
    for n in UPDATE_ORDER:
        reduced = [reduce.collect((l, n), marker) for l in range(DEPTH)]
        outs = _adamw_layers("adamw_" + n, _to_rows(n, wts[n]), reduced, _to_rows(n, mom_m[n]), _to_rows(n, mom_v[n]),
                             reduce.chip, 256, transposed=n == "w_up", after=marker)
        grad_w[n], delta[n], new_m[n], new_v[n] = (_to_rows(n, o) for o in outs)
        marker = (outs[1],)

    own, slots = [None] * DEPTH, [None] * DEPTH
    for l in reversed(range(DEPTH)):
        own[l], slots[l] = reduce.small_finish(l, marker)
        if l == 0:
            loss_own, loss_slots = own[l].pop("loss"), slots[l].pop("loss")
    shard_cols = CW // N_DEV
    for l in range(DEPTH):
        own[l]["w_dw"] = lax.dynamic_slice_in_dim(own[l]["w_dw"], me * shard_cols, shard_cols, axis=1)
        slots[l]["w_dw"] = lax.dynamic_slice_in_dim(slots[l]["w_dw"], me * shard_cols, shard_cols, axis=2)
    *small_out, loss_tile = _adamw_small(wts, mom_m, mom_v, own, slots, loss_own, loss_slots, dev)
    for dst, src in zip((grad_w, delta, new_m, new_v), small_out):
        dst.update(src)

    return (loss_tile[0, 0], dx[None], *[grad_w[n] for n in WEIGHTS], *[delta[n] for n in WEIGHTS],
            *[new_m[n] for n in WEIGHTS], *[new_v[n] for n in WEIGHTS])
```

```python
import functools

import jax
import jax.numpy as jnp
from jax import lax
from jax.experimental import pallas as pl
from jax.experimental.pallas import tpu as pltpu

F32 = jnp.float32
BF16 = jnp.bfloat16

D = 2048
GW = 1024
PW = 512
CW = 512
HD = 128
NH = 8
NG = 4
POOL_WINDOWS = (2, 4, 8, 16)
CONV_K = 31
IN_COLS = 2 * GW + PW + 2 * CW
XH = 4
XHD = D // XH
ATT_SCALE = XHD ** -0.5
RMS_EPS = 1e-6
LN_EPS = 1e-5
DEPTH = 2
N_DEV = 8

ADAM_LR = 0.001
ADAM_B1 = 0.9
ADAM_B2 = 0.999
ADAM_EPS = 1e-08
ADAM_WD = 0.01
ADAM_STEP = 10

LANES = 128
CONV_HALO = 32
POOL_HALO = 16
ROW_TILE = 128
VMEM_LIMIT = 60 * 1024 * 1024

MESH = pl.DeviceIdType.MESH
NT = (((1,), (1,)), ((), ()))
NN = (((1,), (0,)), ((), ()))
TN = (((0,), (0,)), ((), ()))

UPDATE_ORDER = ("w_down", "w_up", "w_o", "w_q", "w_k", "w_v", "w_out", "w_in")
GATHER_GROUPS = (("in", ("w_in",)), ("out", ("w_out",)), ("att", ("w_q", "w_k", "w_v", "w_o")), ("up", ("w_up",)),
                 ("down", ("w_down",)))
SMALL = ("norm_mix_pre", "norm_mix_post", "gmlp_v_gain", "w_spatial", "b_spatial", "w_pool", "s_pool",
         "w_dw", "b_dw", "conv_ln_g", "conv_ln_b", "norm_xattn_pre", "norm_mem", "norm_xattn_post",
         "norm_ffn_pre", "norm_ffn_post")
WEIGHTS = ("norm_mix_pre", "norm_mix_post", "w_in", "w_out", "gmlp_v_gain", "w_spatial", "b_spatial", "w_pool",
           "s_pool", "w_dw", "b_dw", "conv_ln_g", "conv_ln_b", "norm_xattn_pre", "norm_mem", "norm_xattn_post",
           "w_q", "w_k", "w_v", "w_o", "norm_ffn_pre", "norm_ffn_post", "w_up", "w_down")


def _cparams():
    return pltpu.CompilerParams(vmem_limit_bytes=VMEM_LIMIT)


def _dot(a, b, dims):
    return lax.dot_general(a, b, dims, preferred_element_type=F32)


def _rms(x, g):
    y = x * lax.rsqrt(jnp.mean(x * x, axis=-1, keepdims=True) + RMS_EPS)
    return y * g


def _rms_bwd(x, g, dy):
    r = lax.rsqrt(jnp.mean(x * x, axis=-1, keepdims=True) + RMS_EPS)
    xh = x * r
    t = dy * g
    dx = r * (t - xh * jnp.mean(t * xh, axis=-1, keepdims=True))
    return dx, jnp.sum(dy * xh, axis=0, keepdims=True)


def _gelu(x):
    cdf = 0.5 * (1.0 + jnp.tanh(0.7978845608028654 * (x + 0.044715 * (x * x * x))))
    return x * cdf


def _layer_norm(x, g, b=None):
    mu = jnp.mean(x, axis=-1, keepdims=True)
    xc = x - mu
    var = jnp.mean(xc * xc, axis=-1, keepdims=True)
    y = xc * lax.rsqrt(var + LN_EPS) * g
    return y if b is None else y + b


def _sigmoid(x):
    return 1.0 / (1.0 + jnp.exp(-x))


def _gmlp_rows(zu, zv, gv):
    return _gelu(zu), _layer_norm(_gelu(zv), gv)


def _glu(cv, cg):
    return cv * _sigmoid(cg)


def _ln_silu(h, g, b):
    y = _layer_norm(h, g, b)
    return y * _sigmoid(y)


ANY = pl.BlockSpec(memory_space=pl.ANY)


ROWS_TILE = 256
COLS_TILE = 512
DW_TILE = 512
RESIDENT_K = 2048
RESIDENT_ROWS = 512
STREAM_K_TILE = 1024
STREAM_ROWS = 512


def _k_tiles(kdim):
    if kdim <= RESIDENT_K:
        return RESIDENT_ROWS, kdim
    if kdim <= IN_COLS:
        return ROWS_TILE, kdim
    return STREAM_ROWS, max(t for t in range(LANES, STREAM_K_TILE + 1, LANES) if kdim % t == 0)


def _rowop_mm(name, kind, rows, g, w, dims, out_dtype, u=None, after=()):
    s = rows[0].shape[0]
    n = w.shape[0] if dims == NT else w.shape[1]
    resident = n <= IN_COLS and u is None
    tm, tn = min(RESIDENT_ROWS if resident and n <= RESIDENT_K else ROWS_TILE, s), min(COLS_TILE, n)
    ni, nj = s // tm, n // tn
    bwd = kind == "rms_bwd"
    out_shape = [jax.ShapeDtypeStruct((s, n), out_dtype), jax.ShapeDtypeStruct((s, D), BF16)]
    if bwd:
        out_shape.append(jax.ShapeDtypeStruct((ni, 1, D), F32))

    if resident:
        def row_body(*refs):
            refs = list(refs)
            row_refs = [refs.pop(0) for _ in rows]
            g_ref, w_ref = refs.pop(0), refs.pop(0)
            del refs[:len(after)]
            if bwd:
                a, dg = _rms_bwd(row_refs[0][...], g_ref[...], row_refs[1][...])
                refs[2][0] = dg
            else:
                a = _rms(row_refs[0][...], g_ref[...])
            a = a.astype(BF16)
            refs[1][...] = a
            refs[0][...] = _dot(a, w_ref[...], dims).astype(out_dtype)

        blk = pl.BlockSpec((tm, D), lambda i: (i, 0))
        return pl.pallas_call(
            row_body, name=name, grid=(ni,),
            in_specs=[blk] * len(rows) + [pl.BlockSpec((1, D), lambda i: (0, 0)),
                                          pl.BlockSpec(w.shape, lambda i: (0, 0), pipeline_mode=pl.Buffered(1))]
            + [ANY] * len(after),
            out_specs=[pl.BlockSpec((tm, n), lambda i: (i, 0)), blk]
            + ([pl.BlockSpec((1, 1, D), lambda i: (i, 0, 0))] if bwd else []),
            out_shape=out_shape, compiler_params=_cparams(),
        )(*rows, g, w, *after)

    def body(*refs):
        refs = list(refs)
        row_refs = [refs.pop(0) for _ in rows]
        g_ref, w_ref = refs.pop(0), refs.pop(0)
        u_ref = refs.pop(0) if u is not None else None
        del refs[:len(after)]
        out_ref, a_ref = refs.pop(0), refs.pop(0)
        dg_ref = refs.pop(0) if bwd else None
        a_all = refs.pop(0)
        t = pl.program_id(0)

        @pl.when(t < ni)
        def _():
            if bwd:
                a, dg = _rms_bwd(row_refs[0][...], g_ref[...], row_refs[1][...])
                dg_ref[0] = dg
            else:
                a = _rms(row_refs[0][...], g_ref[...])
            a_ref[...] = a.astype(BF16)
            a_all[pl.ds(pl.multiple_of(t * tm, tm), tm), :] = a.astype(BF16)

        @pl.when(t >= ni)
        def _():
            acc = _dot(a_all[...], w_ref[...], dims)
            if u_ref is not None:
                acc = acc * (2.0 * jnp.maximum(u_ref[...], 0.0))
            out_ref[...] = acc.astype(out_dtype)

    rows_at = lambda t: jnp.minimum(t, ni - 1)
    cols_at = lambda t: jnp.maximum(t - ni, 0)
    row_spec = pl.BlockSpec((tm, D), lambda t: (rows_at(t), 0))
    w_spec = (pl.BlockSpec((tn, D), lambda t: (cols_at(t), 0)) if dims == NT
              else pl.BlockSpec((D, tn), lambda t: (0, cols_at(t))))
    tile = pl.BlockSpec((s, tn), lambda t: (0, cols_at(t)))
    in_specs = [row_spec] * len(rows) + [pl.BlockSpec((1, D), lambda t: (0, 0)), w_spec]
    in_specs += ([tile] if u is not None else []) + [ANY] * len(after)
    out_specs = [tile, row_spec]
    if bwd:
        out_specs.append(pl.BlockSpec((1, 1, D), lambda t: (rows_at(t), 0, 0)))
    return pl.pallas_call(
        body, name=name, grid=(ni + nj,), in_specs=in_specs, out_specs=out_specs, out_shape=out_shape,
        scratch_shapes=[pltpu.VMEM((s, D), BF16)], compiler_params=_cparams(),
    )(*rows, g, w, *([u] if u is not None else []), *after)


def _mm_rowop(name, kind, pairs, rows, g, relu2=False, after=()):
    s, kdim = pairs[0][0].shape
    tm, tk = _k_tiles(kdim)
    tm = min(tm, s)
    ni, nk = s // tm, kdim // tk
    npair = len(pairs)

    def body(*refs):
        refs = list(refs)
        a_refs = [refs.pop(0) for _ in range(npair)]
        w_refs = [refs.pop(0) for _ in range(npair)]
        row_refs = [refs.pop(0) for _ in rows]
        g_ref = refs.pop(0)
        del refs[:len(after)]
        acc = refs.pop() if nk > 1 else None
        outs = refs
        k = pl.program_id(1)

        def product():
            total = None
            for a_ref, w_ref, (_, _, dims) in zip(a_refs, w_refs, pairs):
                a = a_ref[...]
                if relu2:
                    a = jnp.square(jnp.maximum(a, 0.0))
                term = _dot(a.astype(BF16), w_ref[...], dims)
                total = term if total is None else total + term
            return total

        def finish(h):
            if kind == "rms_res":
                outs[0][...] = row_refs[0][...] + _rms(h, g_ref[...])
                outs[1][...] = h
            else:
                dx, dg = _rms_bwd(row_refs[0][...], g_ref[...], h)
                if kind == "rms_bwd_res":
                    outs[0][...] = row_refs[1][...] + dx
                    outs[1][0] = dg
                else:
                    outs[0][0] = dg

        if nk == 1:
            finish(product())
            return

        @pl.when(k == 0)
        def _():
            acc[...] = jnp.zeros_like(acc)

        acc[...] += product()

        @pl.when(k == nk - 1)
        def _():
            finish(acc[...])

    row_spec = pl.BlockSpec((tm, D), lambda i, k: (i, 0))
    dg_shape = jax.ShapeDtypeStruct((ni, 1, D), F32)
    dg_spec = pl.BlockSpec((1, 1, D), lambda i, k: (i, 0, 0))
    in_specs = [pl.BlockSpec((tm, tk), lambda i, k: (i, k))] * npair
    for _, _, dims in pairs:
        mode = dict(pipeline_mode=pl.Buffered(1)) if nk == 1 else {}
        in_specs.append(pl.BlockSpec((tk, D), lambda i, k: (k, 0), **mode) if dims == NN
                        else pl.BlockSpec((D, tk), lambda i, k: (0, k), **mode))
    in_specs += [row_spec] * len(rows) + [pl.BlockSpec((1, D), lambda i, k: (0, 0))] + [ANY] * len(after)
    if kind == "rms_res":
        out_shape = [jax.ShapeDtypeStruct((s, D), F32)] * 2
        out_specs = [row_spec, row_spec]
    elif kind == "rms_bwd_res":
        out_shape = [jax.ShapeDtypeStruct((s, D), F32), dg_shape]
        out_specs = [row_spec, dg_spec]
    else:
        out_shape = [dg_shape]
        out_specs = [dg_spec]
    return pl.pallas_call(
        body, name=name, grid=(ni, nk), in_specs=in_specs, out_specs=out_specs, out_shape=out_shape,
        scratch_shapes=[pltpu.VMEM((tm, D), F32)] if nk > 1 else [], compiler_params=_cparams(),
    )(*[p[0] for p in pairs], *[p[1] for p in pairs], *rows, g, *after)


def _mm_tn(name, a, gmat, relu2=False, after=()):
    s, m = a.shape
    tm = min(DW_TILE, m)
    if a.dtype == BF16 and m % (2 * DW_TILE) == 0:
        tm = 2 * DW_TILE
    ni = m // tm

    def body(a_ref, g_ref, *rest):
        av = a_ref[...]
        if relu2:
            av = jnp.square(jnp.maximum(av, 0.0))
        rest[len(after)][...] = _dot(av.astype(BF16), g_ref[...], TN).astype(BF16)

    return pl.pallas_call(
        body, name=name, grid=(ni,),
        in_specs=[pl.BlockSpec((s, tm), lambda i: (0, i)),
                  pl.BlockSpec((s, D), lambda i: (0, 0), pipeline_mode=pl.Buffered(1))] + [ANY] * len(after),
        out_specs=pl.BlockSpec((tm, D), lambda i: (i, 0)),
        out_shape=jax.ShapeDtypeStruct((m, D), BF16), compiler_params=_cparams(),
    )(a, gmat, *after)


def _tril():
    r = lax.broadcasted_iota(jnp.int32, (HD, HD), 0)
    c = lax.broadcasted_iota(jnp.int32, (HD, HD), 1)
    return (c <= r).astype(F32)


def _gmlp_fwd(z, gv, ws, bst, tb):
    s = z.shape[0]
    tb = min(tb, s)

    def body(zu_ref, zv_ref, gv_ref, ws_ref, bst_ref, y_ref):
        tril = _tril()
        for h in range(NH):
            cols = slice(h * HD, (h + 1) * HD)
            u, vln = _gmlp_rows(zu_ref[:, cols], zv_ref[:, cols], gv_ref[h:h + 1, :])
            wm = (ws_ref[h] * tril).astype(BF16)
            vb = vln.astype(BF16)
            for c in range(tb // HD):
                rws = slice(c * HD, (c + 1) * HD)
                mixed = _dot(wm, vb[rws], NN) + bst_ref[:, h:h + 1]
                y_ref[rws, cols] = (u[rws] * mixed).astype(BF16)

    return pl.pallas_call(
        body, name="gmlp_fwd", grid=(s // tb,),
        in_specs=[pl.BlockSpec((tb, GW), lambda i: (i, 0)), pl.BlockSpec((tb, GW), lambda i: (i, 1)),
                  pl.BlockSpec((NH, HD), lambda i: (0, 0)), pl.BlockSpec((NH, HD, HD), lambda i: (0, 0, 0)),
                  pl.BlockSpec((HD, NH), lambda i: (0, 0))],
        out_specs=pl.BlockSpec((tb, GW), lambda i: (i, 0)),
        out_shape=jax.ShapeDtypeStruct((s, D), BF16), compiler_params=_cparams(),
    )(z, z, gv, ws, bst)


def _gmlp_bwd(z, dy, gv, ws, bst, tb, after=()):
    s = z.shape[0]
    tb = min(tb, s)
    nb = s // tb

    def body(zu_ref, zv_ref, dy_ref, gv_ref, ws_ref, bst_ref, *rest):
        dz_ref, dgv_ref, dws_ref, db_ref = rest[len(after):]
        tril = _tril()
        for h in range(NH):
            cols = slice(h * HD, (h + 1) * HD)
            (u, vln), vjp = jax.vjp(_gmlp_rows, zu_ref[:, cols], zv_ref[:, cols], gv_ref[h:h + 1, :])
            wmf = ws_ref[h] * tril
            wm = wmf.astype(BF16)
            wmt = wmf.T.astype(BF16)
            vb = vln.astype(BF16)
            dws = jnp.zeros((HD, HD), F32)
            db = jnp.zeros((HD, 1), F32)
            du_parts, dvln_parts = [], []
            for c in range(tb // HD):
                rws = slice(c * HD, (c + 1) * HD)
                mixed = _dot(wm, vb[rws], NN) + bst_ref[:, h:h + 1]
                dyc = dy_ref[rws, cols]
                du_parts.append(dyc * mixed)
                dmixed = dyc * u[rws]
                dmb = dmixed.astype(BF16)
                dws = dws + _dot(dmb, vb[rws], NT)
                db = db + jnp.sum(dmixed, axis=1, keepdims=True)
                dvln_parts.append(_dot(wmt, dmb, NN))
            du = jnp.concatenate(du_parts, axis=0)
            dvln = jnp.concatenate(dvln_parts, axis=0)
            dzu, dzv, dgv = vjp((du, dvln))
            dz_ref[:, cols] = dzu.astype(BF16)
            dz_ref[:, slice(GW + h * HD, GW + (h + 1) * HD)] = dzv.astype(BF16)
            dgv_ref[0, h:h + 1, :] = dgv
            dws_ref[0, h] = dws * tril
            db_ref[0, h] = jnp.broadcast_to(db, (HD, LANES))

    blk = pl.BlockSpec((tb, GW), lambda i: (i, 0))
    return pl.pallas_call(
        body, name="gmlp_bwd", grid=(nb,),
        in_specs=[blk, pl.BlockSpec((tb, GW), lambda i: (i, 1)), blk,
                  pl.BlockSpec((NH, HD), lambda i: (0, 0)), pl.BlockSpec((NH, HD, HD), lambda i: (0, 0, 0)),
                  pl.BlockSpec((HD, NH), lambda i: (0, 0))] + [ANY] * len(after),
        out_specs=[pl.BlockSpec((tb, 2 * GW), lambda i: (i, 0)), pl.BlockSpec((1, NH, HD), lambda i: (i, 0, 0)),
                   pl.BlockSpec((1, NH, HD, HD), lambda i: (i, 0, 0, 0)),
                   pl.BlockSpec((1, NH, HD, LANES), lambda i: (i, 0, 0, 0))],
        out_shape=[jax.ShapeDtypeStruct((s, IN_COLS), BF16),
                   jax.ShapeDtypeStruct((nb, NH, HD), F32), jax.ShapeDtypeStruct((nb, NH, HD, HD), F32),
                   jax.ShapeDtypeStruct((nb, NH, HD, LANES), F32)],
        compiler_params=_cparams(),
    )(z, z, dy, gv, ws, bst, *after)


POOL_TILE = 1024


def _pool_count(t0, window):
    pos = (t0 + lax.broadcasted_iota(jnp.int32, (POOL_TILE, LANES), 0)).astype(F32)
    return jnp.minimum(pos + 1.0, float(window))


def _window_sum(win, levels, back):
    n = win.shape[0]
    for lv in range(levels):
        step = 1 << lv
        win = win + pltpu.roll(win, n - step if back else step, 0)
    return win


def _pool_pooled(ppad_ref, t0, g):
    win = ppad_ref[pl.ds(t0, POOL_TILE + POOL_HALO), :]
    wsum = _window_sum(win, g + 1, False)[POOL_HALO:]
    return wsum / _pool_count(t0, POOL_WINDOWS[g]) - win[POOL_HALO:]


def _pool_fwd(z, wp, sp, y):
    s = z.shape[0]
    nt = s // POOL_TILE

    def body(p_ref, wp_ref, sp_ref, _, y_ref, ppad):
        for g in range(NG):
            cols = slice(g * LANES, (g + 1) * LANES)
            ppad[pl.ds(0, POOL_HALO), :] = jnp.zeros((POOL_HALO, LANES), F32)
            ppad[pl.ds(POOL_HALO, s), :] = p_ref[:, cols]
            wpb = wp_ref[g].astype(BF16)
            scale = sp_ref[:, cols]

            def tile(t, carry):
                t0 = pl.multiple_of(t * POOL_TILE, POOL_TILE)
                pooled = _pool_pooled(ppad, t0, g)
                y_ref[pl.ds(t0, POOL_TILE), cols] = (_dot(pooled.astype(BF16), wpb, NN) * scale).astype(BF16)
                return carry

            lax.fori_loop(0, nt, tile, 0)

    return pl.pallas_call(
        body, name="pool_fwd", grid=(1,),
        in_specs=[pl.BlockSpec((s, PW), lambda i: (0, 2 * GW // PW)),
                  pl.BlockSpec((NG, LANES, LANES), lambda i: (0, 0, 0)), pl.BlockSpec((1, PW), lambda i: (0, 0)), ANY],
        out_specs=pl.BlockSpec((s, PW), lambda i: (0, GW // PW)),
        out_shape=jax.ShapeDtypeStruct((s, D), BF16), input_output_aliases={3: 0},
        scratch_shapes=[pltpu.VMEM((s + POOL_HALO, LANES), F32)], compiler_params=_cparams(),
    )(z, wp, sp, y)


def _pool_bwd(z, dy, wp, sp, dz):
    s = z.shape[0]
    nt = s // POOL_TILE

    def body(p_ref, dy_ref, wp_ref, sp_ref, _, dp_ref, dwp_ref, dsp_ref, ppad, rpad, dpool):
        for g in range(NG):
            cols = slice(g * LANES, (g + 1) * LANES)
            ppad[pl.ds(0, POOL_HALO), :] = jnp.zeros((POOL_HALO, LANES), F32)
            ppad[pl.ds(POOL_HALO, s), :] = p_ref[:, cols]
            rpad[pl.ds(s, POOL_HALO), :] = jnp.zeros((POOL_HALO, LANES), F32)
            wpb = wp_ref[g].astype(BF16)
            scale = sp_ref[:, cols]

            def tile(t, carry):
                dwp, dsp = carry
                t0 = pl.multiple_of(t * POOL_TILE, POOL_TILE)
                pooled = _pool_pooled(ppad, t0, g)
                pb = pooled.astype(BF16)
                dyt = dy_ref[pl.ds(t0, POOL_TILE), cols]
                dsp = dsp + jnp.sum(dyt * _dot(pb, wpb, NN), axis=0, keepdims=True)
                dmm = (dyt * scale).astype(BF16)
                dwp = dwp + _dot(pb, dmm, TN)
                dpooled = _dot(dmm, wpb, NT)
                rpad[pl.ds(t0, POOL_TILE), :] = dpooled / _pool_count(t0, POOL_WINDOWS[g])
                dpool[pl.ds(t0, POOL_TILE), :] = dpooled
                return dwp, dsp

            dwp, dsp = lax.fori_loop(0, nt, tile, (jnp.zeros((LANES, LANES), F32), jnp.zeros((1, LANES), F32)))
            dwp_ref[g] = dwp
            dsp_ref[:, cols] = dsp

            def tile2(t, carry):
                t0 = pl.multiple_of(t * POOL_TILE, POOL_TILE)
                win = rpad[pl.ds(t0, POOL_TILE + POOL_HALO), :]
                back = _window_sum(win, g + 1, True)[:POOL_TILE]
                rows = pl.ds(t0, POOL_TILE)
                dp_ref[rows, cols] = (back - dpool[rows, :]).astype(BF16)
                return carry

            lax.fori_loop(0, nt, tile2, 0)

    return pl.pallas_call(
        body, name="pool_bwd", grid=(1,),
        in_specs=[pl.BlockSpec((s, PW), lambda i: (0, 2 * GW // PW)), pl.BlockSpec((s, PW), lambda i: (0, GW // PW)),
                  pl.BlockSpec((NG, LANES, LANES), lambda i: (0, 0, 0)), pl.BlockSpec((1, PW), lambda i: (0, 0)), ANY],
        out_specs=[pl.BlockSpec((s, PW), lambda i: (0, 2 * GW // PW)),
                   pl.BlockSpec((NG, LANES, LANES), lambda i: (0, 0, 0)), pl.BlockSpec((1, PW), lambda i: (0, 0))],
        out_shape=[jax.ShapeDtypeStruct((s, IN_COLS), BF16), jax.ShapeDtypeStruct((NG, LANES, LANES), F32),
                   jax.ShapeDtypeStruct((1, PW), F32)],
        input_output_aliases={4: 0},
        scratch_shapes=[pltpu.VMEM((s + POOL_HALO, LANES), F32), pltpu.VMEM((s + POOL_HALO, LANES), F32),
                        pltpu.VMEM((s, LANES), F32)],
        compiler_params=_cparams(),
    )(z, dy, wp, sp, dz)


CONV_LEAD = CONV_HALO - (CONV_K - 1)


SUBLANES = 8


def _sublane_shifts(win):
    n = win.shape[0]
    return [win] + [pltpu.roll(win, n - b, 0) for b in range(1, SUBLANES)]


def _shifted(shifts, offset):
    a, b = divmod(offset, SUBLANES)
    return shifts[b][a * SUBLANES:a * SUBLANES + ROW_TILE]


def _conv_taps(shifts, wdw_ref, lead, reverse):
    acc = jnp.zeros((ROW_TILE, CW), F32)
    for j in range(CONV_K):
        tap = (CONV_K - 1 - j) if reverse else j
        acc = acc + wdw_ref[tap:tap + 1, :] * _shifted(shifts, lead + j)
    return acc


def _conv_fill_glu(cv_ref, cg_ref, xpad, s):
    xpad[pl.ds(0, CONV_HALO), :] = jnp.zeros((CONV_HALO, CW), F32)

    def fill(t, carry):
        t0 = pl.multiple_of(t * ROW_TILE, ROW_TILE)
        rows = pl.ds(t0, ROW_TILE)
        xpad[pl.ds(t0 + CONV_HALO, ROW_TILE), :] = _glu(cv_ref[rows, :], cg_ref[rows, :])
        return carry

    lax.fori_loop(0, s // ROW_TILE, fill, 0)


def _conv_fwd(z, wdw, bdw, lng, lnb, y):
    s = z.shape[0]

    def body(cv_ref, cg_ref, wdw_ref, bdw_ref, lng_ref, lnb_ref, _, y_ref, xpad):
        _conv_fill_glu(cv_ref, cg_ref, xpad, s)

        def tile(t, carry):
            t0 = pl.multiple_of(t * ROW_TILE, ROW_TILE)
            shifts = _sublane_shifts(xpad[pl.ds(t0, ROW_TILE + CONV_HALO), :])
            hc = _conv_taps(shifts, wdw_ref, CONV_LEAD, False) + bdw_ref[...]
            y_ref[pl.ds(t0, ROW_TILE), :] = _ln_silu(hc, lng_ref[...], lnb_ref[...]).astype(BF16)
            return carry

        lax.fori_loop(0, s // ROW_TILE, tile, 0)

    vec = pl.BlockSpec((1, CW), lambda i: (0, 0))
    return pl.pallas_call(
        body, name="conv_fwd", grid=(1,),
        in_specs=[pl.BlockSpec((s, CW), lambda i: (0, (2 * GW + PW) // CW)),
                  pl.BlockSpec((s, CW), lambda i: (0, (2 * GW + PW) // CW + 1)),
                  pl.BlockSpec((CONV_K + 1, CW), lambda i: (0, 0)), vec, vec, vec, ANY],
        out_specs=pl.BlockSpec((s, CW), lambda i: (0, (GW + PW) // CW)),
        out_shape=jax.ShapeDtypeStruct((s, D), BF16), input_output_aliases={6: 0},
        scratch_shapes=[pltpu.VMEM((s + CONV_HALO, CW), F32)], compiler_params=_cparams(),
    )(z, z, wdw, bdw, lng, lnb, y)


def _conv_bwd(z, dy, wdw, bdw, lng, lnb, dz):
    s = z.shape[0]

    def body(cv_ref, cg_ref, dy_ref, wdw_ref, bdw_ref, lng_ref, lnb_ref, _,
             dz_ref, dwdw_ref, dbdw_ref, dlng_ref, dlnb_ref, xpad, dpad, dcg_keep):
        @pl.when(pl.program_id(0) == 0)
        def _():
            compute(cv_ref, cg_ref, dy_ref, wdw_ref, bdw_ref, lng_ref, lnb_ref,
                    dz_ref, dcg_keep, dwdw_ref, dbdw_ref, dlng_ref, dlnb_ref, xpad, dpad)

        @pl.when(pl.program_id(0) == 1)
        def _():
            dz_ref[...] = dcg_keep[...]

    def compute(cv_ref, cg_ref, dy_ref, wdw_ref, bdw_ref, lng_ref, lnb_ref,
                dcv_ref, dcg_ref, dwdw_ref, dbdw_ref, dlng_ref, dlnb_ref, xpad, dpad):
        _conv_fill_glu(cv_ref, cg_ref, xpad, s)
        dpad[pl.ds(s, CONV_HALO), :] = jnp.zeros((CONV_HALO, CW), F32)
        dwdw_ref[...] = jnp.zeros((CONV_K + 1, CW), F32)

        def tile(t, carry):
            db, dg, dbeta = carry
            t0 = pl.multiple_of(t * ROW_TILE, ROW_TILE)
            shifts = _sublane_shifts(xpad[pl.ds(t0, ROW_TILE + CONV_HALO), :])
            hc = _conv_taps(shifts, wdw_ref, CONV_LEAD, False) + bdw_ref[...]
            _, vjp = jax.vjp(_ln_silu, hc, lng_ref[...], lnb_ref[...])
            dhc, dg_t, dbeta_t = vjp(dy_ref[pl.ds(t0, ROW_TILE), :])
            dpad[pl.ds(t0, ROW_TILE), :] = dhc
            for j in range(CONV_K):
                dwdw_ref[j:j + 1, :] += jnp.sum(dhc * _shifted(shifts, CONV_LEAD + j), axis=0, keepdims=True)
            return db + jnp.sum(dhc, axis=0, keepdims=True), dg + dg_t, dbeta + dbeta_t

        zero = jnp.zeros((1, CW), F32)
        db, dg, dbeta = lax.fori_loop(0, s // ROW_TILE, tile, (zero, zero, zero))
        dbdw_ref[...] = db
        dlng_ref[...] = dg
        dlnb_ref[...] = dbeta

        def tile2(t, carry):
            t0 = pl.multiple_of(t * ROW_TILE, ROW_TILE)
            rows = pl.ds(t0, ROW_TILE)
            dglu = _conv_taps(_sublane_shifts(dpad[pl.ds(t0, ROW_TILE + CONV_HALO), :]), wdw_ref, 0, True)
            _, vjp = jax.vjp(_glu, cv_ref[rows, :], cg_ref[rows, :])
            dcv, dcg = vjp(dglu)
            dcv_ref[rows, :] = dcv.astype(BF16)
            dcg_ref[rows, :] = dcg.astype(BF16)
            return carry

        lax.fori_loop(0, s // ROW_TILE, tile2, 0)

    vec = pl.BlockSpec((1, CW), lambda i: (0, 0))
    wspec = pl.BlockSpec((CONV_K + 1, CW), lambda i: (0, 0))
    vshape = jax.ShapeDtypeStruct((1, CW), F32)
    return pl.pallas_call(
        body, name="conv_bwd", grid=(2,),
        in_specs=[pl.BlockSpec((s, CW), lambda i: (0, (2 * GW + PW) // CW)),
                  pl.BlockSpec((s, CW), lambda i: (0, (2 * GW + PW) // CW + 1)),
                  pl.BlockSpec((s, CW), lambda i: (0, (GW + PW) // CW)), wspec, vec, vec, vec, ANY],
        out_specs=[pl.BlockSpec((s, CW), lambda i: (0, (2 * GW + PW) // CW + i)), wspec, vec, vec, vec],
        out_shape=[jax.ShapeDtypeStruct((s, IN_COLS), BF16), jax.ShapeDtypeStruct((CONV_K + 1, CW), F32),
                   vshape, vshape, vshape],
        input_output_aliases={7: 0},
        scratch_shapes=[pltpu.VMEM((s + CONV_HALO, CW), F32), pltpu.VMEM((s + CONV_HALO, CW), F32),
                        pltpu.VMEM((s, CW), BF16)],
        compiler_params=_cparams(),
    )(z, z, dy, wdw, bdw, lng, lnb, dz)


def _softmax_rows(sc):
    e = jnp.exp(sc - jnp.max(sc, axis=-1, keepdims=True))
    return e / jnp.sum(e, axis=-1, keepdims=True)


def _attn_fwd(q, k, v, tq):
    s, m = q.shape[0], k.shape[0]
    tq = min(tq, s)

    def body(q_ref, k_ref, v_ref, o_ref):
        for h in range(XH):
            cols = slice(h * XHD, (h + 1) * XHD)
            p = _softmax_rows(_dot(q_ref[:, cols], k_ref[:, cols], NT) * ATT_SCALE)
            o_ref[:, cols] = _dot(p.astype(BF16), v_ref[:, cols], NN).astype(BF16)

    kv = pl.BlockSpec((m, D), lambda i: (0, 0))
    return pl.pallas_call(
        body, name="attn_fwd", grid=(s // tq,),
        in_specs=[pl.BlockSpec((tq, D), lambda i: (i, 0)), kv, kv],
        out_specs=pl.BlockSpec((tq, D), lambda i: (i, 0)),
        out_shape=jax.ShapeDtypeStruct((s, D), BF16), compiler_params=_cparams(),
    )(q, k, v)


def _attn_bwd(q, k, v, do, tq, after=()):
    s, m = q.shape[0], k.shape[0]
    tq = min(tq, s)

    def body(q_ref, k_ref, v_ref, do_ref, *rest):
        dq_ref, dk_ref, dv_ref = rest[len(after):]

        @pl.when(pl.program_id(0) == 0)
        def _():
            dk_ref[...] = jnp.zeros_like(dk_ref)
            dv_ref[...] = jnp.zeros_like(dv_ref)

        for h in range(XH):
            cols = slice(h * XHD, (h + 1) * XHD)
            qh, kh, vh, doh = q_ref[:, cols], k_ref[:, cols], v_ref[:, cols], do_ref[:, cols]
            p = _softmax_rows(_dot(qh, kh, NT) * ATT_SCALE)
            dp = _dot(doh, vh, NT)
            dv_ref[:, cols] += _dot(p.astype(BF16), doh, TN)
            ds = (p * (dp - jnp.sum(p * dp, axis=-1, keepdims=True)) * ATT_SCALE).astype(BF16)
            dq_ref[:, cols] = _dot(ds, kh, NN).astype(BF16)
            dk_ref[:, cols] += _dot(ds, qh, TN)

    kv = pl.BlockSpec((m, D), lambda i: (0, 0))
    qs = pl.BlockSpec((tq, D), lambda i: (i, 0))
    return pl.pallas_call(
        body, name="attn_bwd", grid=(s // tq,),
        in_specs=[qs, kv, kv, qs] + [ANY] * len(after), out_specs=[qs, kv, kv],
        out_shape=[jax.ShapeDtypeStruct((s, D), BF16), jax.ShapeDtypeStruct((m, D), F32),
                   jax.ShapeDtypeStruct((m, D), F32)],
        compiler_params=_cparams(),
    )(q, k, v, do, *after)


def _loss_head(y, target, tm):
    s = y.shape[0]
    tm = min(tm, s)

    def body(y_ref, t_ref, dy_ref, part_ref):
        err = y_ref[...] - t_ref[...]
        dy_ref[...] = err * (1.0 / D)
        part_ref[...] = jnp.full((1, 8, LANES), 0.5 * jnp.sum(err * err) * (1.0 / D), F32)

    blk = pl.BlockSpec((tm, D), lambda i: (i, 0))
    return pl.pallas_call(
        body, name="loss_head", grid=(s // tm,), in_specs=[blk, blk],
        out_specs=[blk, pl.BlockSpec((1, 8, LANES), lambda i: (i, 0, 0))],
        out_shape=[jax.ShapeDtypeStruct((s, D), F32), jax.ShapeDtypeStruct((s // tm, 8, LANES), F32)],
        compiler_params=_cparams(),
    )(y, target)


def _layer_fwd(x0, mem, w, p, fetch):
    z, hn0 = _rowop_mm("mix_in", "rms", (x0,), p["norm_mix_pre"], w["w_in"], NT, F32)
    y = _gmlp_fwd(z, p["gmlp_v_gain"], p["w_spatial"], p["b_spatial_t"], 1024)
    y = _pool_fwd(z, p["w_pool"], p["s_pool"], y)
    y = _conv_fwd(z, p["w_dw"], p["b_dw"], p["conv_ln_g"], p["conv_ln_b"], y)
    w.update(fetch("out", (y,)))
    x1, h0 = _mm_rowop("mix_out", "rms_res", [(y, w["w_out"], NN)], (x0,), p["norm_mix_post"])
    w.update(fetch("att", (x1,)))
    q, hn1 = _rowop_mm("att_q", "rms", (x1,), p["norm_xattn_pre"], w["w_q"], NN, BF16)
    k, mn = _rowop_mm("att_k", "rms", (mem,), p["norm_mem"], w["w_k"], NN, BF16, after=(x1,))
    v, _ = _rowop_mm("att_v", "rms", (mem,), p["norm_mem"], w["w_v"], NN, BF16, after=(x1,))
    o = _attn_fwd(q, k, v, 1024)
    x2, h1 = _mm_rowop("att_o", "rms_res", [(o, w["w_o"], NN)], (x1,), p["norm_xattn_post"])
    w.update(fetch("up", (x2,)))
    u, hn2 = _rowop_mm("ffn_up", "rms", (x2,), p["norm_ffn_pre"], w["w_up"], NT, F32)
    w.update(fetch("down", (u,)))
    x3, h2 = _mm_rowop("ffn_down", "rms_res", [(u, w["w_down"], NN)], (x2,), p["norm_ffn_post"], relu2=True)
    saved = dict(x0=x0, z=z, hn0=hn0, y=y, h0=h0, x1=x1, q=q, hn1=hn1, k=k, v=v, mn=mn, o=o, h1=h1, x2=x2, u=u,
                 hn2=hn2, h2=h2)
    return x3, saved


def _layer_bwd(dx3, mem, w, p, sv, red):
    gs = {}
    du, dh2, dg = _rowop_mm("ffn_down_bwd", "rms_bwd", (sv["h2"], dx3), p["norm_ffn_post"], w["w_down"], NT, BF16,
                            u=sv["u"], after=red.after())
    gs["norm_ffn_post"] = jnp.sum(dg, axis=0)
    g_down = _mm_tn("ffn_down_dw", sv["u"], dh2, relu2=True)
    red.advance((g_down,))
    dx2, dg = _mm_rowop("ffn_up_bwd", "rms_bwd_res", [(du, w["w_up"], NN)], (sv["x2"], dx3), p["norm_ffn_pre"],
                        after=red.after())
    gs["norm_ffn_pre"] = jnp.sum(dg, axis=0)
    g_up = _mm_tn("ffn_up_dw", du, sv["hn2"])
    red.add("ffn", ("w_down", "w_up"), [g_down, g_up])
    do, dh1, dg = _rowop_mm("att_o_bwd", "rms_bwd", (sv["h1"], dx2), p["norm_xattn_post"], w["w_o"], NT, BF16,
                            after=red.after())
    gs["norm_xattn_post"] = jnp.sum(dg, axis=0)
    g_o = _mm_tn("att_o_dw", sv["o"], dh1)
    red.advance((g_o,))
    dq, dk, dv = _attn_bwd(sv["q"], sv["k"], sv["v"], do, 1024, after=red.after())
    dk, dv = dk.astype(BF16), dv.astype(BF16)
    dx1, dg = _mm_rowop("att_q_bwd", "rms_bwd_res", [(dq, w["w_q"], NT)], (sv["x1"], dx2), p["norm_xattn_pre"],
                        after=red.after())
    gs["norm_xattn_pre"] = jnp.sum(dg, axis=0)
    g_q = _mm_tn("att_q_dw", sv["hn1"], dq)
    g_k = _mm_tn("att_k_dw", sv["mn"], dk)
    g_v = _mm_tn("att_v_dw", sv["mn"], dv)
    (dg,) = _mm_rowop("att_kv_bwd", "rms_bwd_gain", [(dk, w["w_k"], NT), (dv, w["w_v"], NT)], (mem,), p["norm_mem"])
    gs["norm_mem"] = jnp.sum(dg, axis=0)
    red.add("att", ("w_o", "w_q", "w_k", "w_v"), [g_o, g_q, g_k, g_v])
    dy, dh0, dg = _rowop_mm("mix_out_bwd", "rms_bwd", (sv["h0"], dx1), p["norm_mix_post"], w["w_out"], NT, F32,
                            after=red.after())
    gs["norm_mix_post"] = jnp.sum(dg, axis=0)
    g_out = _mm_tn("mix_out_dw", sv["y"], dh0)
    red.advance((g_out,))
    red.add("out", ("w_out",), [g_out])
    z = sv["z"]
    dz, dgv, dws, dbs = _gmlp_bwd(z, dy, p["gmlp_v_gain"], p["w_spatial"], p["b_spatial_t"], 512, after=red.after())
    gs["gmlp_v_gain"] = jnp.sum(dgv, axis=0)
    gs["w_spatial"] = jnp.sum(dws, axis=0)
    gs["b_spatial"] = jnp.sum(dbs[..., 0], axis=0)
    dz, gs["w_pool"], gs["s_pool"] = _pool_bwd(z, dy, p["w_pool"], p["s_pool"], dz)
    dz, dwdw, gs["b_dw"], gs["conv_ln_g"], gs["conv_ln_b"] = _conv_bwd(
        z, dy, p["w_dw"], p["b_dw"], p["conv_ln_g"], p["conv_ln_b"], dz)
    red.advance((dz,))
    g_in = _mm_tn("mix_in_dw", dz, sv["hn0"], after=red.after())
    red.add("in", ("w_in",), [g_in])
    if red.layer == 0:
        red.advance(())
    red.small("mixer", _small_grad_arrays(gs, dwdw, norms=False))
    dx0, dg = _mm_rowop("mix_in_bwd", "rms_bwd_res", [(dz, w["w_in"], NN)], (sv["x0"], dx1), p["norm_mix_pre"],
                        after=red.after())
    gs["norm_mix_pre"] = jnp.sum(dg, axis=0)
    late = {"norms": jnp.concatenate([gs[n] for n in NORM_NAMES], axis=0)}
    if red.layer == 0:
        late["loss"] = red.extra[0]
    red.small("norms", late)
    return dx0


NORM_NAMES = ("norm_mix_pre", "norm_mix_post", "norm_xattn_pre", "norm_mem", "norm_xattn_post", "norm_ffn_pre",
              "norm_ffn_post")
VEC_NAMES = ("s_pool", "b_dw", "conv_ln_g", "conv_ln_b")
SMALL_ARRAYS = ("norms", "gain_bias", "w_spatial", "w_pool", "vecs", "w_dw")


def _small_grad_arrays(gs, dwdw, norms=True):
    out = {"norms": jnp.concatenate([gs[n] for n in NORM_NAMES], axis=0)} if norms else {}
    out.update({"gain_bias": jnp.concatenate([gs["gmlp_v_gain"], gs["b_spatial"]], axis=0),
                "w_spatial": gs["w_spatial"], "w_pool": gs["w_pool"],
                "vecs": jnp.concatenate([gs[n] for n in VEC_NAMES], axis=0), "w_dw": dwdw})
    return out


def _layer_params(small, l):
    p = {n: small[n][l].reshape(1, -1) for n in ("norm_mix_pre", "norm_mix_post", "s_pool", "b_dw", "conv_ln_g",
                                                   "conv_ln_b", "norm_xattn_pre", "norm_mem", "norm_xattn_post",
                                                   "norm_ffn_pre", "norm_ffn_post")}
    p["gmlp_v_gain"] = small["gmlp_v_gain"][l]
    p["w_spatial"] = small["w_spatial"][l]
    p["b_spatial_t"] = small["b_spatial"][l].T
    p["w_pool"] = small["w_pool"][l]
    p["w_dw"] = jnp.pad(small["w_dw"][l], ((0, 1), (0, 0)))
    return p


def _local_step(x, mem, target, fetch, small, red):
    small = dict(small)
    saved, weights, params = [], [], []
    h = x
    marker = ()
    for l in range(DEPTH):
        w = fetch(l, "in", marker)
        if "taps" in w:
            small["w_dw"] = w.pop("taps")
        p = _layer_params(small, l)
        h, sv = _layer_fwd(h, mem, w, p, functools.partial(fetch, l))
        marker = (h,)
        saved.append(sv)
        weights.append(w)
        params.append(p)
    dh, loss = _loss_head(h, target, 1024)
    red.extra = (loss,)
    for l in reversed(range(DEPTH)):
        red.layer = l
        dh = _layer_bwd(dh, mem, weights[l], params[l], saved[l], red)
    return loss, dh


HBM = pl.BlockSpec(memory_space=pltpu.HBM)


def _position():
    return lax.axis_index("x"), lax.axis_index("y"), lax.axis_index("c")


SEM = pl.BlockSpec(memory_space=pltpu.SEMAPHORE)
EFFECT = pltpu.SideEffectType.DATAFLOW_SIDE_EFFECTING
TOKEN = jax.ShapeDtypeStruct((8, LANES), F32)
TOKEN_SPEC = pl.BlockSpec(memory_space=pltpu.VMEM)


def _landing(shape, dtype):
    return pltpu.with_memory_space_constraint(lax.empty(shape, dtype), pltpu.HBM)


def _hbm_shapes(arrays):
    return [pltpu.HBM(a.shape, a.dtype) for a in arrays]


def _block(ref, r, dev):
    return ref.at[pl.ds((4 * dev[0] + 2 * dev[1] + dev[2]) * r, r), :]


def _split_call(name, body, thru, sems_in, after, sems_out, token):
    n = len(thru)
    out_shape = [pltpu.SemaphoreType.DMA(s) for s in sems_out] + _hbm_shapes(thru) + ([TOKEN] if token else [])
    out_specs = [SEM] * len(sems_out) + [HBM] * n + ([TOKEN_SPEC] if token else [])
    return pl.pallas_call(
        body, name=name, in_specs=[HBM] * n + [SEM] * len(sems_in) + [ANY] * len(after),
        out_specs=out_specs, out_shape=out_shape,
        input_output_aliases={i: len(sems_out) + i for i in range(n)},
        compiler_params=pltpu.CompilerParams(has_side_effects=EFFECT),
    )(*thru, *sems_in, *after)


def _place_own(name, srcs, dev, out_dtype, tr):
    n = len(srcs)
    r, cols = srcs[0][0].shape[-2:]
    tr = r if r < 16 else _row_tile(r, tr)
    nb = r // tr

    def body(dev_ref, *refs):
        for a in range(n):
            refs[n + a][...] = refs[a][...].astype(out_dtype)

    in_specs = [pl.BlockSpec((tr, cols), lambda i, d: (i, 0)) if l is None
                else pl.BlockSpec((None, tr, cols), lambda i, d, l=l: (l, i, 0)) for _, l in srcs]
    return pl.pallas_call(
        body, name=name,
        grid_spec=pltpu.PrefetchScalarGridSpec(
            num_scalar_prefetch=1, grid=(nb,), in_specs=in_specs,
            out_specs=[pl.BlockSpec((tr, cols), lambda i, d: (d[0] * nb + i, 0))] * n),
        out_shape=[jax.ShapeDtypeStruct((N_DEV * r, cols), out_dtype)] * n, compiler_params=_cparams(),
    )(dev, *[a for a, _ in srcs])


def _gather_peers(x, y, c):
    return [(1 - x, y, c), (x, 1 - y, c), (1 - x, 1 - y, c), (x, y, 1 - c)]


def _block_rows(land):
    return land.shape[0] // N_DEV


def _near_peers(x, y, c):
    return [(1 - x, y, c), (x, 1 - y, c), (x, y, 1 - c)]


def _relay_route(x, y, c):
    origin = (x + c * (1 - 2 * x), y + (1 - c) * (1 - 2 * y), c)
    target = (x + (1 - c) * (1 - 2 * x), y + c * (1 - 2 * y), c)
    return origin, target


def _same_block_copy(blk, send_sem, recv_sem, to):
    return pltpu.make_async_remote_copy(src_ref=blk, dst_ref=blk, send_sem=send_sem, recv_sem=recv_sem, device_id=to,
                                        device_id_type=MESH)


def _gather_start(name, lands, after):
    n = len(lands)

    def body(*refs):
        lz = refs[:n]
        send_sems, recv_sems = refs[n + len(after)], refs[n + len(after) + 1]
        token = refs[-1]
        x, y, c = _position()
        for a in range(n):
            own = _block(lz[a], _block_rows(lands[a]), (x, y, c))
            for k, to in enumerate(_near_peers(x, y, c)):
                _same_block_copy(own, send_sems.at[k], recv_sems.at[k], to).start()
        token[...] = jnp.zeros_like(token)

    out = _split_call(name, body, list(lands), [], after, [(3,), (3,)], True)
    return out[0], out[1], out[2:2 + n], out[-1]


def _gather_step(name, near, far, fresh, after):
    groups = [g for g in (near and near[0], far and far[0], fresh) if g]
    counts = [len(near[0]) if near else 0, len(far[0]) if far else 0, len(fresh) if fresh else 0]
    n = sum(counts)
    sems_in = ([near[1]] if near else []) + ([far[1]] if far else [])
    sems_out = ([(2,), (2,), (1,), (1,)] if near else []) + ([(1,), (1,)] if far else []) + ([(3,), (3,)] if fresh else [])

    def body(*refs):
        lz = list(refs[:n])
        ins = list(refs[n:n + len(sems_in)])
        outs = list(refs[n + len(sems_in) + len(after):n + len(sems_in) + len(after) + len(sems_out)])
        token = refs[-1]
        x, y, c = _position()
        me, sibling = (x, y, c), (x, y, 1 - c)
        near_lz, far_lz, fresh_lz = (lz[sum(counts[:i]):sum(counts[:i + 1])] for i in range(3))
        neighbours = _near_peers(x, y, c)[:2]
        origin, target = _relay_route(x, y, c)
        diagonal = (1 - x, 1 - y, c)
        if near:
            recv0 = ins.pop(0)
            fsend, frecv, rsend, rrecv = (outs.pop(0) for _ in range(4))
            for a, land in enumerate(near[0]):
                for j, chip in enumerate(neighbours):
                    _same_block_copy(_block(near_lz[a], _block_rows(land), chip), fsend.at[j], recv0.at[j], me).wait_recv()
        if far:
            rrecv_in = ins.pop(0)
            f2send, f2recv = outs.pop(0), outs.pop(0)
            for a, land in enumerate(far[0]):
                _same_block_copy(_block(far_lz[a], _block_rows(land), diagonal), f2send.at[0], rrecv_in.at[0], me).wait_recv()
            for a, land in enumerate(far[0]):
                _same_block_copy(_block(far_lz[a], _block_rows(land), diagonal), f2send.at[0], f2recv.at[0], sibling).start()
        if near:
            for a, land in enumerate(near[0]):
                r = _block_rows(land)
                _same_block_copy(_block(near_lz[a], r, origin), rsend.at[0], rrecv.at[0], target).start()
                for j, chip in enumerate(neighbours):
                    _same_block_copy(_block(near_lz[a], r, chip), fsend.at[j], frecv.at[j], sibling).start()
        if fresh:
            send_sems, recv_sems = outs.pop(0), outs.pop(0)
            for a, land in enumerate(fresh):
                own = _block(fresh_lz[a], _block_rows(land), me)
                for k, to in enumerate(_near_peers(x, y, c)):
                    _same_block_copy(own, send_sems.at[k], recv_sems.at[k], to).start()
        token[...] = jnp.zeros_like(token)

    out = list(_split_call(name, body, [l for g in groups for l in g], sems_in, after, sems_out, True))
    res = {"token": out.pop()}
    if near:
        res.update(fsend=out.pop(0), frecv=out.pop(0), rsend=out.pop(0), rrecv=out.pop(0))
    if far:
        res.update(f2send=out.pop(0), f2recv=out.pop(0))
    if fresh:
        res.update(send=out.pop(0), recv=out.pop(0))
    res["near"], res["far"], res["fresh"] = (out[sum(counts[:i]):sum(counts[:i + 1])] for i in range(3))
    return res


def _gather_finish(name, lands, send_sems, recv_sems, fsend, frecv, rsend, f2send, f2recv, after):
    n = len(lands)

    def body(*refs):
        lz = refs[:n]
        send0, recv0, fsend_ref, frecv_ref, rsend_ref, f2send_ref, f2recv_ref = refs[n:n + 7]
        x, y, c = _position()
        me = (x, y, c)
        near = _near_peers(x, y, c)[:2]
        origin, _ = _relay_route(x, y, c)
        for a in range(n):
            r = _block_rows(lands[a])
            sib = _block(lz[a], r, (x, y, 1 - c))
            _same_block_copy(sib, send0.at[2], recv0.at[2], me).wait_recv()
            for j, chip in enumerate(near):
                blk = _block(lz[a], r, (chip[0], chip[1], 1 - c))
                _same_block_copy(blk, fsend_ref.at[j], frecv_ref.at[j], me).wait_recv()
            far = _block(lz[a], r, (1 - x, 1 - y, 1 - c))
            _same_block_copy(far, f2send_ref.at[0], f2recv_ref.at[0], me).wait_recv()
            own = _block(lz[a], r, me)
            for k in range(3):
                _same_block_copy(own, send0.at[k], recv0.at[k], me).wait_send()
            for j, chip in enumerate(near):
                _same_block_copy(_block(lz[a], r, chip), fsend_ref.at[j], frecv_ref.at[j], me).wait_send()
            _same_block_copy(_block(lz[a], r, origin), rsend_ref.at[0], recv0.at[0], me).wait_send()
            _same_block_copy(_block(lz[a], r, (1 - x, 1 - y, c)), f2send_ref.at[0], f2recv_ref.at[0], me).wait_send()

    return _split_call(name, body, list(lands), [send_sems, recv_sems, fsend, frecv, rsend, f2send, f2recv], after, [],
                       False)


def _sibling_start(name, grads, after):
    n = len(grads)
    lands = [_landing((4, g.shape[0] // N_DEV, D), g.dtype) for g in grads]

    def body(*refs):
        ins, lz = refs[:n], refs[n:2 * n]
        send_sem, recv_sem = refs[2 * n + len(after)], refs[2 * n + len(after) + 1]
        token = refs[-1]
        x, y, c = _position()
        for a in range(n):
            r = grads[a].shape[0] // N_DEV
            for q in range(4):
                pltpu.make_async_remote_copy(
                    src_ref=ins[a].at[pl.ds((2 * q + 1 - c) * r, r), :], dst_ref=lz[a].at[q], send_sem=send_sem.at[0],
                    recv_sem=recv_sem.at[0], device_id=(x, y, 1 - c), device_id_type=MESH).start()
        token[...] = jnp.zeros_like(token)

    out = _split_call(name, body, list(grads) + lands, [], after, [(1,), (1,)], True)
    return out[0], out[1], out[2:2 + n], out[2 + n:2 + 2 * n], out[-1]


def _sibling_finish(name, grads, lands, send_sem, recv_sem, after):
    n = len(grads)

    def body(*refs):
        ins, lz = refs[:n], refs[n:2 * n]
        send_ref, recv_ref = refs[2 * n], refs[2 * n + 1]
        x, y, c = _position()
        for a in range(n):
            r = grads[a].shape[0] // N_DEV
            for q in range(4):
                cp = pltpu.make_async_remote_copy(
                    src_ref=ins[a].at[pl.ds((2 * q + 1 - c) * r, r), :], dst_ref=lz[a].at[q], send_sem=send_ref.at[0],
                    recv_sem=recv_ref.at[0], device_id=(x, y, c), device_id_type=MESH)
                cp.wait_send()
                cp.wait_recv()

    out = _split_call(name, body, list(grads) + list(lands), [send_sem, recv_sem], after, [], False)
    return out[:n], out[n:2 * n]


def _chip_start(name, parts, after):
    n = len(parts)
    lands = [_landing((3,) + p.shape[1:], p.dtype) for p in parts]

    def body(*refs):
        ins, lz = refs[:n], refs[n:2 * n]
        send_sems, recv_sems = refs[2 * n + len(after)], refs[2 * n + len(after) + 1]
        token = refs[-1]
        x, y, c = _position()
        for a in range(n):
            for j, chip in enumerate(_gather_peers(x, y, c)[:3]):
                pltpu.make_async_remote_copy(
                    src_ref=ins[a].at[2 * chip[0] + chip[1]], dst_ref=lz[a].at[j], send_sem=send_sems.at[j],
                    recv_sem=recv_sems.at[j], device_id=chip, device_id_type=MESH).start()
        token[...] = jnp.zeros_like(token)

    out = _split_call(name, body, list(parts) + lands, [], after, [(3,), (3,)], True)
    return out[0], out[1], out[2:2 + n], out[2 + n:2 + 2 * n], out[-1]


def _chip_finish(name, parts, lands, send_sems, recv_sems, after):
    n = len(parts)

    def body(*refs):
        ins, lz = refs[:n], refs[n:2 * n]
        send_ref, recv_ref = refs[2 * n], refs[2 * n + 1]
        me = _position()
        for a in range(n):
            for j in range(3):
                cp = pltpu.make_async_remote_copy(
                    src_ref=ins[a].at[j], dst_ref=lz[a].at[j], send_sem=send_ref.at[j], recv_sem=recv_ref.at[j],
                    device_id=me, device_id_type=MESH)
                cp.wait_send()
                cp.wait_recv()

    out = _split_call(name, body, list(parts) + list(lands), [send_sems, recv_sems], after, [], False)
    return out[:n], out[n:2 * n]


def _other_devices(x, y, c):
    return [(x + (k >> 2 & 1) * (1 - 2 * x), y + (k >> 1 & 1) * (1 - 2 * y), c + (k & 1) * (1 - 2 * c))
            for k in range(1, N_DEV)]


def _broadcast_start(name, arrays, after):
    n = len(arrays)
    lands = [_landing((N_DEV,) + a.shape, a.dtype) for a in arrays]

    def body(*refs):
        ins, lz = refs[:n], refs[n:2 * n]
        send_sems, recv_sems = refs[2 * n + len(after)], refs[2 * n + len(after) + 1]
        token = refs[-1]
        x, y, c = _position()
        for a in range(n):
            for k, peer in enumerate(_other_devices(x, y, c)):
                pltpu.make_async_remote_copy(
                    src_ref=ins[a], dst_ref=lz[a].at[4 * x + 2 * y + c], send_sem=send_sems.at[k],
                    recv_sem=recv_sems.at[k], device_id=peer, device_id_type=MESH).start()
        token[...] = jnp.zeros_like(token)

    out = _split_call(name, body, list(arrays) + lands, [], after, [(N_DEV - 1,), (N_DEV - 1,)], True)
    return out[0], out[1], out[2:2 + n], out[2 + n:2 + 2 * n], out[-1]


def _broadcast_finish(name, arrays, lands, send_sems, recv_sems, after):
    n = len(arrays)

    def body(*refs):
        ins, lz = refs[:n], refs[n:2 * n]
        send_ref, recv_ref = refs[2 * n], refs[2 * n + 1]
        x, y, c = _position()
        for a in range(n):
            for k, peer in enumerate(_other_devices(x, y, c)):
                cp = pltpu.make_async_remote_copy(
                    src_ref=ins[a], dst_ref=lz[a].at[4 * peer[0] + 2 * peer[1] + peer[2]], send_sem=send_ref.at[k],
                    recv_sem=recv_ref.at[k], device_id=(x, y, c), device_id_type=MESH)
                cp.wait_send()
                cp.wait_recv()

    out = _split_call(name, body, list(arrays) + list(lands), [send_sems, recv_sems], after, [], False)
    return out[:n], out[n:2 * n]


def _row_tile(r, target):
    return max(t for t in range(16, min(r, target) + 1, 16) if r % t == 0)


CHIP_PARTIAL_BYTES = 12 * 1024 * 1024


def _chip_partial(name, grads, gots, c):
    n = len(grads)
    r = grads[0].shape[0] // N_DEV
    tr = _row_tile(r, CHIP_PARTIAL_BYTES // (n * 3 * D * 2))

    def body(c_ref, *refs):
        for a in range(n):
            refs[2 * n + a][...] = (refs[a][...].astype(F32) + refs[n + a][...].astype(F32)).astype(BF16)

    blk = pl.BlockSpec((None, tr, D), lambda q, i, c_ref: (q, i, 0))
    return pl.pallas_call(
        body, name=name,
        grid_spec=pltpu.PrefetchScalarGridSpec(
            num_scalar_prefetch=1, grid=(4, r // tr),
            in_specs=[pl.BlockSpec((None, None, tr, D), lambda q, i, c_ref: (q, c_ref[0], i, 0))] * n + [blk] * n,
            out_specs=[blk] * n),
        out_shape=[jax.ShapeDtypeStruct((4, r, D), BF16)] * n, compiler_params=_cparams(),
    )(c, *[g.reshape(4, 2, r, D) for g in grads], *gots)


class _WeightGather:
    def __init__(self, groups):
        self.groups = list(groups)
        self.index = {key: i for i, (key, _, _) in enumerate(groups)}
        self.state = [None] * len(groups)
        self.token = ()
        for i in range(min(2, len(groups))):
            self._start(i)

    def _tag(self, i):
        return "%s_%d" % self.groups[i][0][::-1]

    def _start(self, i):
        send, recv, lz, tok = _gather_start("gather_start_" + self._tag(i), self.groups[i][2], self.token)
        self.state[i] = dict(send=send, recv=recv, lands=lz)
        self.token = (tok,)

    def _step(self, name, near, far, fresh, marker):
        exists = lambda i: i is not None and i < len(self.groups)
        near, far, fresh = (i if exists(i) else None for i in (near, far, fresh))
        res = _gather_step(
            name, None if near is None else (self.state[near]["lands"], self.state[near]["recv"]),
            None if far is None else (self.state[far]["lands"], self.state[far]["rrecv"]),
            None if fresh is None else self.groups[fresh][2], tuple(marker) + self.token)
        self.token = (res["token"],)
        if near is not None:
            self.state[near].update(lands=res["near"], fsend=res["fsend"], frecv=res["frecv"], rsend=res["rsend"],
                                    rrecv=res["rrecv"])
        if far is not None:
            self.state[far].update(lands=res["far"], f2send=res["f2send"], f2recv=res["f2recv"])
        if fresh is not None:
            self.state[fresh] = dict(send=res["send"], recv=res["recv"], lands=res["fresh"])

    def fetch(self, layer, group, marker):
        k = self.index[(layer, group)]
        if k == 0:
            self._step("gather_step_first", 0, None, None, marker)
        self._step("gather_step_" + self._tag(k), k + 1, k, k + 2, marker)
        st = self.state[k]
        lz = _gather_finish("gather_finish_" + self._tag(k), st["lands"], st["send"], st["recv"], st["fsend"],
                            st["frecv"], st["rsend"], st["f2send"], st["f2recv"], self.token)
        self.state[k] = None
        return dict(zip(self.groups[k][1], lz))


class _GradReduce:
    def __init__(self, core, chip):
        self.core, self.chip = core, chip
        self.layer = None
        self.token = ()
        self.at_sibling, self.at_chips = [], []
        self.extra, self.smalls = (), {}

    def after(self):
        return self.token

    def add(self, group, names, grads):
        tag = "%s_%d" % (group, self.layer)
        send, recv, grads, lands, tok = _sibling_start("grad_sibling_start_" + tag, grads, self.token)
        self.at_sibling.append((tag, [(self.layer, n) for n in names], send, recv, grads, lands))
        self.token = (tok,)

    def advance(self, marker):
        for tag, keys, send, recv, grads, lands in self.at_sibling:
            grads, lands = _sibling_finish("grad_sibling_finish_" + tag, grads, lands, send, recv, marker)
            parts = _chip_partial("chip_partial_" + tag, grads, lands, self.core)
            send, recv, parts, lands, tok = _chip_start("grad_chip_start_" + tag, parts, ())
            self.at_chips.append([tag, keys, send, recv, parts, lands])
            self.token = (tok,)
        self.at_sibling = []

    def small(self, part, arrays):
        keys = list(arrays)
        send, recv, own, slots, tok = _broadcast_start(
            "small_grads_start_%d_%s" % (self.layer, part), [arrays[k] for k in keys], self.token)
        self.smalls.setdefault(self.layer, []).append((part, keys, send, recv, own, slots))
        self.token = (tok,)

    def small_finish(self, layer, marker):
        mine, theirs = {}, {}
        for part, keys, send, recv, own, slots in self.smalls[layer]:
            own, slots = _broadcast_finish("small_grads_finish_%d_%s" % (layer, part), own, slots, send, recv, marker)
            mine.update(zip(keys, own))
            theirs.update(zip(keys, slots))
        return mine, theirs

    def collect(self, key, marker):
        for entry in self.at_chips:
            tag, keys, send, recv, parts, lands = entry
            if key in keys:
                if send is not None:
                    parts, lands = _chip_finish("grad_chip_finish_" + tag, parts, lands, send, recv, marker)
                    entry[2:] = [None, None, parts, lands]
                i = keys.index(key)
                return parts[i], lands[i]
        raise KeyError(key)


def _adamw_math(w, g, m, v):
    m = ADAM_B1 * m + (1.0 - ADAM_B1) * g
    v = ADAM_B2 * v + (1.0 - ADAM_B2) * jnp.square(g)
    m_hat = m / (1.0 - ADAM_B1 ** ADAM_STEP)
    v_hat = v / (1.0 - ADAM_B2 ** ADAM_STEP)
    delta = -ADAM_LR * (m_hat / (jnp.sqrt(v_hat) + ADAM_EPS) + ADAM_WD * w)
    return delta, m, v


def _adamw_small(wts, mom_m, mom_v, own, gathered, loss_own, loss_gathered, dev):
    names = SMALL
    nw = len(names)
    na = len(SMALL_ARRAYS)

    def body(dev_ref, *refs):
        w_refs, m_refs, v_refs = (dict(zip(names, refs[i * nw:(i + 1) * nw])) for i in range(3))
        own_refs = refs[3 * nw:3 * nw + DEPTH * na]
        g_refs = refs[3 * nw + DEPTH * na:3 * nw + 2 * DEPTH * na]
        loss_own_ref, loss_got_ref = refs[3 * nw + 2 * DEPTH * na:3 * nw + 2 * DEPTH * na + 2]
        outs = refs[3 * nw + 2 * DEPTH * na + 2:]
        g_out, d_out, m_out, v_out = (dict(zip(names, outs[i * nw:(i + 1) * nw])) for i in range(4))
        me = dev_ref[0]

        loss = None
        for d in range(N_DEV):
            for b in range(loss_own.shape[0]):
                term = jnp.where(me == d, loss_own_ref[b], loss_got_ref[d, b])
                loss = term if loss is None else loss + term
        outs[4 * nw][...] = loss

        def update(name, at, g):
            g_out[name][at] = g
            d_out[name][at], m_out[name][at], v_out[name][at] = _adamw_math(
                w_refs[name][at], g, m_refs[name][at], v_refs[name][at])

        for l in range(DEPTH):
            mine = dict(zip(SMALL_ARRAYS, own_refs[l * na:(l + 1) * na]))
            got = dict(zip(SMALL_ARRAYS, g_refs[l * na:(l + 1) * na]))

            def total(key, at):
                acc = None
                for d in range(N_DEV):
                    term = jnp.where(me == d, mine[key][at] if at else mine[key][...], got[key][(d,) + at])
                    acc = term if acc is None else acc + term
                return acc

            row = (slice(l, l + 1),)
            for k, name in enumerate(NORM_NAMES):
                update(name, row, total("norms", (slice(k, k + 1),)))
            for k, name in enumerate(VEC_NAMES):
                update(name, row, total("vecs", (slice(k, k + 1),)))
            update("gmlp_v_gain", (l,), total("gain_bias", (slice(0, NH),)))
            update("b_spatial", (l,), total("gain_bias", (slice(NH, 2 * NH),)))
            update("w_spatial", (l,), total("w_spatial", ()))
            update("w_pool", (l,), total("w_pool", ()))
            update("w_dw", (l,), total("w_dw", (slice(0, CONV_K),)))

    args = [src[n] for src in (wts, mom_m, mom_v) for n in names]
    args += [src[l][k] for src in (own, gathered) for l in range(DEPTH) for k in SMALL_ARRAYS]
    args += [loss_own, loss_gathered]
    outs = pl.pallas_call(
        body, name="adamw_small",
        in_specs=[pl.BlockSpec(memory_space=pltpu.SMEM)] + [pl.BlockSpec(memory_space=pltpu.VMEM)] * len(args),
        out_shape=[jax.ShapeDtypeStruct(wts[n].shape, F32) for _ in range(4) for n in names]
        + [jax.ShapeDtypeStruct((8, LANES), F32)],
        compiler_params=_cparams(),
    )(dev, *args)
    return tuple(dict(zip(names, outs[i * nw:(i + 1) * nw])) for i in range(4)) + (outs[4 * nw],)


def _adamw_layers(name, w, reduced, m, v, chip, tr, transposed=False, after=()):
    nl, r, cdim = w.shape
    tr = _row_tile(r, tr)
    nb = r // tr

    def body(q_ref, w_ref, p0_ref, g0_ref, p1_ref, g1_ref, m_ref, v_ref, *rest):
        g_ref, d_ref, nm_ref, nv_ref = rest[len(after):]

        def total(p_ref, got_ref):
            acc = p_ref[...].astype(F32)
            for j in range(3):
                acc = acc + got_ref[j].astype(F32)
            return acc

        g = jnp.where(pl.program_id(0) == 0, total(p0_ref, g0_ref), total(p1_ref, g1_ref))
        if transposed:
            g = g.T
        g_ref[...] = g
        d_ref[...], nm_ref[...], nv_ref[...] = _adamw_math(w_ref[...], g, m_ref[...], v_ref[...])

    blk = pl.BlockSpec((None, tr, cdim), lambda l, i, q: (l, i, 0))
    first = lambda l, i: i * (1 - l) + (nb - 1) * l
    second = lambda l, i: i * l
    if transposed:
        gshape = (cdim, tr)
        at = lambda lead, i: (lead, 0, i)
    else:
        gshape = (tr, cdim)
        at = lambda lead, i: (lead, i, 0)
    specs = [blk,
             pl.BlockSpec((None,) + gshape, lambda l, i, q: at(q[0], first(l, i))),
             pl.BlockSpec((3,) + gshape, lambda l, i, q: at(0, first(l, i))),
             pl.BlockSpec((None,) + gshape, lambda l, i, q: at(q[0], second(l, i))),
             pl.BlockSpec((3,) + gshape, lambda l, i, q: at(0, second(l, i))), blk, blk] + [ANY] * len(after)
    shape = jax.ShapeDtypeStruct((nl, r, cdim), F32)
    return pl.pallas_call(
        body, name=name,
        grid_spec=pltpu.PrefetchScalarGridSpec(num_scalar_prefetch=1, grid=(nl, nb), in_specs=specs, out_specs=[blk] * 4),
        out_shape=[shape] * 4, compiler_params=_cparams(),
    )(chip, w, *reduced[0], *reduced[1], m, v, *after)


def _to_rows(name, a):
    return jnp.swapaxes(a, 1, 2) if name == "w_in" else a


def _place_own_transposed(name, srcs, dev, out_dtype, tc):
    n = len(srcs)
    kdim, cdim = srcs[0][0].shape[-2:]

    def body(dev_ref, *refs):
        for a in range(n):
            refs[n + a][...] = refs[a][...].T.astype(out_dtype)

    return pl.pallas_call(
        body, name=name,
        grid_spec=pltpu.PrefetchScalarGridSpec(
            num_scalar_prefetch=1, grid=(kdim // tc,),
            in_specs=[pl.BlockSpec((None, tc, cdim), lambda i, d, l=l: (l, i, 0)) for _, l in srcs],
            out_specs=[pl.BlockSpec((cdim, tc), lambda i, d: (d[0], i))] * n),
        out_shape=[jax.ShapeDtypeStruct((N_DEV * cdim, kdim), out_dtype)] * n, compiler_params=_cparams(),
    )(dev, *[a for a, _ in srcs])


def _pack(arrays, rows):
    flat = jnp.concatenate([a.reshape(-1) for a in arrays])
    return jnp.pad(flat, (0, rows * D - flat.shape[0])).reshape(rows, D)


def _rows_for(shapes, mult=8):
    total = 0
    for shp in shapes:
        size = 1
        for dim in shp:
            size *= dim
        total += size
    return -(-total // (mult * D)) * mult


def kernel(x, mem, norm_mix_pre, norm_mix_post, w_in, w_out, gmlp_v_gain, w_spatial, b_spatial, w_pool, s_pool, w_dw, b_dw, conv_ln_g, conv_ln_b, norm_xattn_pre, norm_mem, norm_xattn_post, w_q, w_k, w_v, w_o, norm_ffn_pre, norm_ffn_post, w_up, w_down, loss_target, m_norm_mix_pre, m_norm_mix_post, m_w_in, m_w_out, m_gmlp_v_gain, m_w_spatial, m_b_spatial, m_w_pool, m_s_pool, m_w_dw, m_b_dw, m_conv_ln_g, m_conv_ln_b, m_norm_xattn_pre, m_norm_mem, m_norm_xattn_post, m_w_q, m_w_k, m_w_v, m_w_o, m_norm_ffn_pre, m_norm_ffn_post, m_w_up, m_w_down, v_norm_mix_pre, v_norm_mix_post, v_w_in, v_w_out, v_gmlp_v_gain, v_w_spatial, v_b_spatial, v_w_pool, v_s_pool, v_w_dw, v_b_dw, v_conv_ln_g, v_conv_ln_b, v_norm_xattn_pre, v_norm_mem, v_norm_xattn_post, v_w_q, v_w_k, v_w_v, v_w_o, v_norm_ffn_pre, v_norm_ffn_post, v_w_up, v_w_down):
    args = dict(locals())
    wts = {n: args[n] for n in WEIGHTS}
    mom_m = {n: args["m_" + n] for n in WEIGHTS}
    mom_v = {n: args["v_" + n] for n in WEIGHTS}
    xi, yi, ci = _position()
    me = 4 * xi + 2 * yi + ci

    dev = jnp.reshape(me, (1,)).astype(jnp.int32)
    lands = {}
    for call, names, tr in (("place_in", ("w_in",), 256), ("place_att", ("w_out", "w_q", "w_k", "w_v", "w_o"), 64),
                            ("place_up", ("w_up",), 256), ("place_down", ("w_down",), 256)):
        srcs = [(_to_rows(n, wts[n]), l) for l in range(DEPTH) for n in names]
        placed = (_place_own_transposed if names == ("w_up",) else _place_own)(call, srcs, dev, BF16, tr)
        lands.update(zip([(l, n) for l in range(DEPTH) for n in names], placed))
    (lands[(0, "taps")],) = _place_own("place_taps", [(_pack([w_dw], _rows_for([w_dw.shape])), None)], dev, F32, 8)
    groups = []
    for l in range(DEPTH):
        for group, names in GATHER_GROUPS:
            if (l, group) == (0, "in"):
                names = names + ("taps",)
            groups.append(((l, group), names, [lands[(l, n)] for n in names]))
    gather = _WeightGather(groups)

    def fetch(layer, group, marker):
        w = gather.fetch(layer, group, marker)
        if "taps" in w:
            blocks = w["taps"].reshape(N_DEV, -1)[:, :w_dw.size].reshape((N_DEV,) + w_dw.shape)
            w["taps"] = jnp.moveaxis(blocks, 0, 2).reshape(DEPTH, CONV_K, CW)
        return w

    reduce = _GradReduce(jnp.reshape(ci, (1,)).astype(jnp.int32), jnp.reshape(2 * xi + yi, (1,)).astype(jnp.int32))
    small = {n: wts[n] for n in SMALL if n != "w_dw"}
    _, dx = _local_step(x[0], mem[0], loss_target[0], fetch, small, reduce)
    reduce.advance((dx,))

    grad_w, delta, new_m, new_v = {}, {}, {}, {}
    marker = (dx,) + tuple(reduce.after())
    for n in UPDATE_ORDER:
        reduced = [reduce.collect((l, n), marker) for l in range(DEPTH)]
        outs = _adamw_layers("adamw_" + n, _to_rows(n, wts[n]), reduced, _to_rows(n, mom_m[n]), _to_rows(n, mom_v[n]),
                             reduce.chip, 256, transposed=n == "w_up", after=marker)
        grad_w[n], delta[n], new_m[n], new_v[n] = (_to_rows(n, o) for o in outs)
        marker = (outs[1],)

    own, slots = [None] * DEPTH, [None] * DEPTH
    for l in reversed(range(DEPTH)):
        own[l], slots[l] = reduce.small_finish(l, marker)
        if l == 0:
            loss_own, loss_slots = own[l].pop("loss"), slots[l].pop("loss")
    shard_cols = CW // N_DEV
    for l in range(DEPTH):
        own[l]["w_dw"] = lax.dynamic_slice_in_dim(own[l]["w_dw"], me * shard_cols, shard_cols, axis=1)
        slots[l]["w_dw"] = lax.dynamic_slice_in_dim(slots[l]["w_dw"], me * shard_cols, shard_cols, axis=2)
    *small_out, loss_tile = _adamw_small(wts, mom_m, mom_v, own, slots, loss_own, loss_slots, dev)
    for dst, src in zip((grad_w, delta, new_m, new_v), small_out):
        dst.update(src)

    return (loss_tile[0, 0], dx[None], *[grad_w[n] for n in WEIGHTS], *[delta[n] for n in WEIGHTS],
            *[new_m[n] for n in WEIGHTS], *[new_v[n] for n in WEIGHTS])
```

```python
import functools

import jax
import jax.numpy as jnp
from jax import lax
from jax.experimental import pallas as pl
from jax.experimental.pallas import tpu as pltpu

F32 = jnp.float32
BF16 = jnp.bfloat16

D = 2048
GW = 1024
PW = 512
CW = 512
HD = 128
NH = 8
NG = 4
POOL_WINDOWS = (2, 4, 8, 16)
CONV_K = 31
IN_COLS = 2 * GW + PW + 2 * CW
XH = 4
XHD = D // XH
ATT_SCALE = XHD ** -0.5
RMS_EPS = 1e-6
LN_EPS = 1e-5
DEPTH = 2
N_DEV = 8

ADAM_LR = 0.001
ADAM_B1 = 0.9
ADAM_B2 = 0.999
ADAM_EPS = 1e-08
ADAM_WD = 0.01
ADAM_STEP = 10

LANES = 128
CONV_HALO = 32
POOL_HALO = 16
ROW_TILE = 128
VMEM_LIMIT = 60 * 1024 * 1024

MESH = pl.DeviceIdType.MESH
NT = (((1,), (1,)), ((), ()))
NN = (((1,), (0,)), ((), ()))
TN = (((0,), (0,)), ((), ()))

UPDATE_ORDER = ("w_down", "w_up", "w_o", "w_q", "w_k", "w_v", "w_out", "w_in")
GATHER_GROUPS = (("in", ("w_in",)), ("out", ("w_out",)), ("att", ("w_q", "w_k", "w_v", "w_o")), ("up", ("w_up",)),
                 ("down", ("w_down",)))
SMALL = ("norm_mix_pre", "norm_mix_post", "gmlp_v_gain", "w_spatial", "b_spatial", "w_pool", "s_pool",
         "w_dw", "b_dw", "conv_ln_g", "conv_ln_b", "norm_xattn_pre", "norm_mem", "norm_xattn_post",
         "norm_ffn_pre", "norm_ffn_post")
WEIGHTS = ("norm_mix_pre", "norm_mix_post", "w_in", "w_out", "gmlp_v_gain", "w_spatial", "b_spatial", "w_pool",
           "s_pool", "w_dw", "b_dw", "conv_ln_g", "conv_ln_b", "norm_xattn_pre", "norm_mem", "norm_xattn_post",
           "w_q", "w_k", "w_v", "w_o", "norm_ffn_pre", "norm_ffn_post", "w_up", "w_down")


def _cparams():
    return pltpu.CompilerParams(vmem_limit_bytes=VMEM_LIMIT)


def _dot(a, b, dims):
    return lax.dot_general(a, b, dims, preferred_element_type=F32)


def _rms(x, g):
    y = x * lax.rsqrt(jnp.mean(x * x, axis=-1, keepdims=True) + RMS_EPS)
    return y * g


def _rms_bwd(x, g, dy):
    r = lax.rsqrt(jnp.mean(x * x, axis=-1, keepdims=True) + RMS_EPS)
    xh = x * r
    t = dy * g
    dx = r * (t - xh * jnp.mean(t * xh, axis=-1, keepdims=True))
    return dx, jnp.sum(dy * xh, axis=0, keepdims=True)


def _gelu(x):
    cdf = 0.5 * (1.0 + jnp.tanh(0.7978845608028654 * (x + 0.044715 * (x * x * x))))
    return x * cdf


def _layer_norm(x, g, b=None):
    mu = jnp.mean(x, axis=-1, keepdims=True)
    xc = x - mu
    var = jnp.mean(xc * xc, axis=-1, keepdims=True)
    y = xc * lax.rsqrt(var + LN_EPS) * g
    return y if b is None else y + b


def _sigmoid(x):
    return 1.0 / (1.0 + jnp.exp(-x))


def _gmlp_rows(zu, zv, gv):
    return _gelu(zu), _layer_norm(_gelu(zv), gv)


def _glu(cv, cg):
    return cv * _sigmoid(cg)


def _ln_silu(h, g, b):
    y = _layer_norm(h, g, b)
    return y * _sigmoid(y)


ANY = pl.BlockSpec(memory_space=pl.ANY)


ROWS_TILE = 256
COLS_TILE = 512
DW_TILE = 512
RESIDENT_K = 2048
RESIDENT_ROWS = 512
STREAM_K_TILE = 1024
STREAM_ROWS = 512


def _k_tiles(kdim):
    if kdim <= RESIDENT_K:
        return RESIDENT_ROWS, kdim
    if kdim <= IN_COLS:
        return ROWS_TILE, kdim
    return STREAM_ROWS, max(t for t in range(LANES, STREAM_K_TILE + 1, LANES) if kdim % t == 0)


def _rowop_mm(name, kind, rows, g, w, dims, out_dtype, u=None, after=()):
    s = rows[0].shape[0]
    n = w.shape[0] if dims == NT else w.shape[1]
    resident = n <= IN_COLS and u is None
    tm, tn = min(RESIDENT_ROWS if resident and n <= RESIDENT_K else ROWS_TILE, s), min(COLS_TILE, n)
    ni, nj = s // tm, n // tn
    bwd = kind == "rms_bwd"
    out_shape = [jax.ShapeDtypeStruct((s, n), out_dtype), jax.ShapeDtypeStruct((s, D), BF16)]
    if bwd:
        out_shape.append(jax.ShapeDtypeStruct((ni, 1, D), F32))

    if resident:
        def row_body(*refs):
            refs = list(refs)
            row_refs = [refs.pop(0) for _ in rows]
            g_ref, w_ref = refs.pop(0), refs.pop(0)
            del refs[:len(after)]
            if bwd:
                a, dg = _rms_bwd(row_refs[0][...], g_ref[...], row_refs[1][...])
                refs[2][0] = dg
            else:
                a = _rms(row_refs[0][...], g_ref[...])
            a = a.astype(BF16)
            refs[1][...] = a
            refs[0][...] = _dot(a, w_ref[...], dims).astype(out_dtype)

        blk = pl.BlockSpec((tm, D), lambda i: (i, 0))
        return pl.pallas_call(
            row_body, name=name, grid=(ni,),
            in_specs=[blk] * len(rows) + [pl.BlockSpec((1, D), lambda i: (0, 0)),
                                          pl.BlockSpec(w.shape, lambda i: (0, 0), pipeline_mode=pl.Buffered(1))]
            + [ANY] * len(after),
            out_specs=[pl.BlockSpec((tm, n), lambda i: (i, 0)), blk]
            + ([pl.BlockSpec((1, 1, D), lambda i: (i, 0, 0))] if bwd else []),
            out_shape=out_shape, compiler_params=_cparams(),
        )(*rows, g, w, *after)

    def body(*refs):
        refs = list(refs)
        row_refs = [refs.pop(0) for _ in rows]
        g_ref, w_ref = refs.pop(0), refs.pop(0)
        u_ref = refs.pop(0) if u is not None else None
        del refs[:len(after)]
        out_ref, a_ref = refs.pop(0), refs.pop(0)
        dg_ref = refs.pop(0) if bwd else None
        a_all = refs.pop(0)
        t = pl.program_id(0)

        @pl.when(t < ni)
        def _():
            if bwd:
                a, dg = _rms_bwd(row_refs[0][...], g_ref[...], row_refs[1][...])
                dg_ref[0] = dg
            else:
                a = _rms(row_refs[0][...], g_ref[...])
            a_ref[...] = a.astype(BF16)
            a_all[pl.ds(pl.multiple_of(t * tm, tm), tm), :] = a.astype(BF16)

        @pl.when(t >= ni)
        def _():
            acc = _dot(a_all[...], w_ref[...], dims)
            if u_ref is not None:
                acc = acc * (2.0 * jnp.maximum(u_ref[...], 0.0))
            out_ref[...] = acc.astype(out_dtype)

    rows_at = lambda t: jnp.minimum(t, ni - 1)
    cols_at = lambda t: jnp.maximum(t - ni, 0)
    row_spec = pl.BlockSpec((tm, D), lambda t: (rows_at(t), 0))
    w_spec = (pl.BlockSpec((tn, D), lambda t: (cols_at(t), 0)) if dims == NT
              else pl.BlockSpec((D, tn), lambda t: (0, cols_at(t))))
    tile = pl.BlockSpec((s, tn), lambda t: (0, cols_at(t)))
    in_specs = [row_spec] * len(rows) + [pl.BlockSpec((1, D), lambda t: (0, 0)), w_spec]
    in_specs += ([tile] if u is not None else []) + [ANY] * len(after)
    out_specs = [tile, row_spec]
    if bwd:
        out_specs.append(pl.BlockSpec((1, 1, D), lambda t: (rows_at(t), 0, 0)))
    return pl.pallas_call(
        body, name=name, grid=(ni + nj,), in_specs=in_specs, out_specs=out_specs, out_shape=out_shape,
        scratch_shapes=[pltpu.VMEM((s, D), BF16)], compiler_params=_cparams(),
    )(*rows, g, w, *([u] if u is not None else []), *after)


def _mm_rowop(name, kind, pairs, rows, g, relu2=False, after=()):
    s, kdim = pairs[0][0].shape
    tm, tk = _k_tiles(kdim)
    tm = min(tm, s)
    ni, nk = s // tm, kdim // tk
    npair = len(pairs)

    def body(*refs):
        refs = list(refs)
        a_refs = [refs.pop(0) for _ in range(npair)]
        w_refs = [refs.pop(0) for _ in range(npair)]
        row_refs = [refs.pop(0) for _ in rows]
        g_ref = refs.pop(0)
        del refs[:len(after)]
        acc = refs.pop() if nk > 1 else None
        outs = refs
        k = pl.program_id(1)

        def product():
            total = None
            for a_ref, w_ref, (_, _, dims) in zip(a_refs, w_refs, pairs):
                a = a_ref[...]
                if relu2:
                    a = jnp.square(jnp.maximum(a, 0.0))
                term = _dot(a.astype(BF16), w_ref[...], dims)
                total = term if total is None else total + term
            return total

        def finish(h):
            if kind == "rms_res":
                outs[0][...] = row_refs[0][...] + _rms(h, g_ref[...])
                outs[1][...] = h
            else:
                dx, dg = _rms_bwd(row_refs[0][...], g_ref[...], h)
                if kind == "rms_bwd_res":
                    outs[0][...] = row_refs[1][...] + dx
                    outs[1][0] = dg
                else:
                    outs[0][0] = dg

        if nk == 1:
            finish(product())
            return

        @pl.when(k == 0)
        def _():
            acc[...] = jnp.zeros_like(acc)

        acc[...] += product()

        @pl.when(k == nk - 1)
        def _():
            finish(acc[...])

    row_spec = pl.BlockSpec((tm, D), lambda i, k: (i, 0))
    dg_shape = jax.ShapeDtypeStruct((ni, 1, D), F32)
    dg_spec = pl.BlockSpec((1, 1, D), lambda i, k: (i, 0, 0))
    in_specs = [pl.BlockSpec((tm, tk), lambda i, k: (i, k))] * npair
    for _, _, dims in pairs:
        mode = dict(pipeline_mode=pl.Buffered(1)) if nk == 1 else {}
        in_specs.append(pl.BlockSpec((tk, D), lambda i, k: (k, 0), **mode) if dims == NN
                        else pl.BlockSpec((D, tk), lambda i, k: (0, k), **mode))
    in_specs += [row_spec] * len(rows) + [pl.BlockSpec((1, D), lambda i, k: (0, 0))] + [ANY] * len(after)
    if kind == "rms_res":
        out_shape = [jax.ShapeDtypeStruct((s, D), F32)] * 2
        out_specs = [row_spec, row_spec]
    elif kind == "rms_bwd_res":
        out_shape = [jax.ShapeDtypeStruct((s, D), F32), dg_shape]
        out_specs = [row_spec, dg_spec]
    else:
        out_shape = [dg_shape]
        out_specs = [dg_spec]
    return pl.pallas_call(
        body, name=name, grid=(ni, nk), in_specs=in_specs, out_specs=out_specs, out_shape=out_shape,
        scratch_shapes=[pltpu.VMEM((tm, D), F32)] if nk > 1 else [], compiler_params=_cparams(),
    )(*[p[0] for p in pairs], *[p[1] for p in pairs], *rows, g, *after)


def _mm_tn(name, a, gmat, relu2=False, after=()):
    s, m = a.shape
    tm = min(DW_TILE, m)
    ni = m // tm

    def body(a_ref, g_ref, *rest):
        av = a_ref[...]
        if relu2:
            av = jnp.square(jnp.maximum(av, 0.0))
        rest[len(after)][...] = _dot(av.astype(BF16), g_ref[...], TN).astype(BF16)

    return pl.pallas_call(
        body, name=name, grid=(ni,),
        in_specs=[pl.BlockSpec((s, tm), lambda i: (0, i)), pl.BlockSpec((s, D), lambda i: (0, 0))] + [ANY] * len(after),
        out_specs=pl.BlockSpec((tm, D), lambda i: (i, 0)),
        out_shape=jax.ShapeDtypeStruct((m, D), BF16), compiler_params=_cparams(),
    )(a, gmat, *after)


def _tril():
    r = lax.broadcasted_iota(jnp.int32, (HD, HD), 0)
    c = lax.broadcasted_iota(jnp.int32, (HD, HD), 1)
    return (c <= r).astype(F32)


def _gmlp_fwd(z, gv, ws, bst, tb):
    s = z.shape[0]
    tb = min(tb, s)

    def body(zu_ref, zv_ref, gv_ref, ws_ref, bst_ref, y_ref):
        tril = _tril()
        for h in range(NH):
            cols = slice(h * HD, (h + 1) * HD)
            u, vln = _gmlp_rows(zu_ref[:, cols], zv_ref[:, cols], gv_ref[h:h + 1, :])
            wm = (ws_ref[h] * tril).astype(BF16)
            vb = vln.astype(BF16)
            for c in range(tb // HD):
                rws = slice(c * HD, (c + 1) * HD)
                mixed = _dot(wm, vb[rws], NN) + bst_ref[:, h:h + 1]
                y_ref[rws, cols] = (u[rws] * mixed).astype(BF16)

    return pl.pallas_call(
        body, name="gmlp_fwd", grid=(s // tb,),
        in_specs=[pl.BlockSpec((tb, GW), lambda i: (i, 0)), pl.BlockSpec((tb, GW), lambda i: (i, 1)),
                  pl.BlockSpec((NH, HD), lambda i: (0, 0)), pl.BlockSpec((NH, HD, HD), lambda i: (0, 0, 0)),
                  pl.BlockSpec((HD, NH), lambda i: (0, 0))],
        out_specs=pl.BlockSpec((tb, GW), lambda i: (i, 0)),
        out_shape=jax.ShapeDtypeStruct((s, D), BF16), compiler_params=_cparams(),
    )(z, z, gv, ws, bst)


def _gmlp_bwd(z, dy, gv, ws, bst, tb, after=()):
    s = z.shape[0]
    tb = min(tb, s)
    nb = s // tb

    def body(zu_ref, zv_ref, dy_ref, gv_ref, ws_ref, bst_ref, *rest):
        dz_ref, dgv_ref, dws_ref, db_ref = rest[len(after):]
        tril = _tril()
        for h in range(NH):
            cols = slice(h * HD, (h + 1) * HD)
            (u, vln), vjp = jax.vjp(_gmlp_rows, zu_ref[:, cols], zv_ref[:, cols], gv_ref[h:h + 1, :])
            wmf = ws_ref[h] * tril
            wm = wmf.astype(BF16)
            wmt = wmf.T.astype(BF16)
            vb = vln.astype(BF16)
            dws = jnp.zeros((HD, HD), F32)
            db = jnp.zeros((HD, 1), F32)
            du_parts, dvln_parts = [], []
            for c in range(tb // HD):
                rws = slice(c * HD, (c + 1) * HD)
                mixed = _dot(wm, vb[rws], NN) + bst_ref[:, h:h + 1]
                dyc = dy_ref[rws, cols]
                du_parts.append(dyc * mixed)
                dmixed = dyc * u[rws]
                dmb = dmixed.astype(BF16)
                dws = dws + _dot(dmb, vb[rws], NT)
                db = db + jnp.sum(dmixed, axis=1, keepdims=True)
                dvln_parts.append(_dot(wmt, dmb, NN))
            du = jnp.concatenate(du_parts, axis=0)
            dvln = jnp.concatenate(dvln_parts, axis=0)
            dzu, dzv, dgv = vjp((du, dvln))
            dz_ref[:, cols] = dzu.astype(BF16)
            dz_ref[:, slice(GW + h * HD, GW + (h + 1) * HD)] = dzv.astype(BF16)
            dgv_ref[0, h:h + 1, :] = dgv
            dws_ref[0, h] = dws * tril
            db_ref[0, h] = jnp.broadcast_to(db, (HD, LANES))

    blk = pl.BlockSpec((tb, GW), lambda i: (i, 0))
    return pl.pallas_call(
        body, name="gmlp_bwd", grid=(nb,),
        in_specs=[blk, pl.BlockSpec((tb, GW), lambda i: (i, 1)), blk,
                  pl.BlockSpec((NH, HD), lambda i: (0, 0)), pl.BlockSpec((NH, HD, HD), lambda i: (0, 0, 0)),
                  pl.BlockSpec((HD, NH), lambda i: (0, 0))] + [ANY] * len(after),
        out_specs=[pl.BlockSpec((tb, 2 * GW), lambda i: (i, 0)), pl.BlockSpec((1, NH, HD), lambda i: (i, 0, 0)),
                   pl.BlockSpec((1, NH, HD, HD), lambda i: (i, 0, 0, 0)),
                   pl.BlockSpec((1, NH, HD, LANES), lambda i: (i, 0, 0, 0))],
        out_shape=[jax.ShapeDtypeStruct((s, IN_COLS), BF16),
                   jax.ShapeDtypeStruct((nb, NH, HD), F32), jax.ShapeDtypeStruct((nb, NH, HD, HD), F32),
                   jax.ShapeDtypeStruct((nb, NH, HD, LANES), F32)],
        compiler_params=_cparams(),
    )(z, z, dy, gv, ws, bst, *after)


POOL_TILE = 1024


def _pool_count(t0, window):
    pos = (t0 + lax.broadcasted_iota(jnp.int32, (POOL_TILE, LANES), 0)).astype(F32)
    return jnp.minimum(pos + 1.0, float(window))


def _window_sum(win, levels, back):
    n = win.shape[0]
    for lv in range(levels):
        step = 1 << lv
        win = win + pltpu.roll(win, n - step if back else step, 0)
    return win


def _pool_pooled(ppad_ref, t0, g):
    win = ppad_ref[pl.ds(t0, POOL_TILE + POOL_HALO), :]
    wsum = _window_sum(win, g + 1, False)[POOL_HALO:]
    return wsum / _pool_count(t0, POOL_WINDOWS[g]) - win[POOL_HALO:]


def _pool_fwd(z, wp, sp, y):
    s = z.shape[0]
    nt = s // POOL_TILE

    def body(p_ref, wp_ref, sp_ref, _, y_ref, ppad):
        for g in range(NG):
            cols = slice(g * LANES, (g + 1) * LANES)
            ppad[pl.ds(0, POOL_HALO), :] = jnp.zeros((POOL_HALO, LANES), F32)
            ppad[pl.ds(POOL_HALO, s), :] = p_ref[:, cols]
            wpb = wp_ref[g].astype(BF16)
            scale = sp_ref[:, cols]

            def tile(t, carry):
                t0 = pl.multiple_of(t * POOL_TILE, POOL_TILE)
                pooled = _pool_pooled(ppad, t0, g)
                y_ref[pl.ds(t0, POOL_TILE), cols] = (_dot(pooled.astype(BF16), wpb, NN) * scale).astype(BF16)
                return carry

            lax.fori_loop(0, nt, tile, 0)

    return pl.pallas_call(
        body, name="pool_fwd", grid=(1,),
        in_specs=[pl.BlockSpec((s, PW), lambda i: (0, 2 * GW // PW)),
                  pl.BlockSpec((NG, LANES, LANES), lambda i: (0, 0, 0)), pl.BlockSpec((1, PW), lambda i: (0, 0)), ANY],
        out_specs=pl.BlockSpec((s, PW), lambda i: (0, GW // PW)),
        out_shape=jax.ShapeDtypeStruct((s, D), BF16), input_output_aliases={3: 0},
        scratch_shapes=[pltpu.VMEM((s + POOL_HALO, LANES), F32)], compiler_params=_cparams(),
    )(z, wp, sp, y)


def _pool_bwd(z, dy, wp, sp, dz):
    s = z.shape[0]
    nt = s // POOL_TILE

    def body(p_ref, dy_ref, wp_ref, sp_ref, _, dp_ref, dwp_ref, dsp_ref, ppad, rpad, dpool):
        for g in range(NG):
            cols = slice(g * LANES, (g + 1) * LANES)
            ppad[pl.ds(0, POOL_HALO), :] = jnp.zeros((POOL_HALO, LANES), F32)
            ppad[pl.ds(POOL_HALO, s), :] = p_ref[:, cols]
            rpad[pl.ds(s, POOL_HALO), :] = jnp.zeros((POOL_HALO, LANES), F32)
            wpb = wp_ref[g].astype(BF16)
            scale = sp_ref[:, cols]

            def tile(t, carry):
                dwp, dsp = carry
                t0 = pl.multiple_of(t * POOL_TILE, POOL_TILE)
                pooled = _pool_pooled(ppad, t0, g)
                pb = pooled.astype(BF16)
                dyt = dy_ref[pl.ds(t0, POOL_TILE), cols]
                dsp = dsp + jnp.sum(dyt * _dot(pb, wpb, NN), axis=0, keepdims=True)
                dmm = (dyt * scale).astype(BF16)
                dwp = dwp + _dot(pb, dmm, TN)
                dpooled = _dot(dmm, wpb, NT)
                rpad[pl.ds(t0, POOL_TILE), :] = dpooled / _pool_count(t0, POOL_WINDOWS[g])
                dpool[pl.ds(t0, POOL_TILE), :] = dpooled
                return dwp, dsp

            dwp, dsp = lax.fori_loop(0, nt, tile, (jnp.zeros((LANES, LANES), F32), jnp.zeros((1, LANES), F32)))
            dwp_ref[g] = dwp
            dsp_ref[:, cols] = dsp

            def tile2(t, carry):
                t0 = pl.multiple_of(t * POOL_TILE, POOL_TILE)
                win = rpad[pl.ds(t0, POOL_TILE + POOL_HALO), :]
                back = _window_sum(win, g + 1, True)[:POOL_TILE]
                rows = pl.ds(t0, POOL_TILE)
                dp_ref[rows, cols] = (back - dpool[rows, :]).astype(BF16)
                return carry

            lax.fori_loop(0, nt, tile2, 0)

    return pl.pallas_call(
        body, name="pool_bwd", grid=(1,),
        in_specs=[pl.BlockSpec((s, PW), lambda i: (0, 2 * GW // PW)), pl.BlockSpec((s, PW), lambda i: (0, GW // PW)),
                  pl.BlockSpec((NG, LANES, LANES), lambda i: (0, 0, 0)), pl.BlockSpec((1, PW), lambda i: (0, 0)), ANY],
        out_specs=[pl.BlockSpec((s, PW), lambda i: (0, 2 * GW // PW)),
                   pl.BlockSpec((NG, LANES, LANES), lambda i: (0, 0, 0)), pl.BlockSpec((1, PW), lambda i: (0, 0))],
        out_shape=[jax.ShapeDtypeStruct((s, IN_COLS), BF16), jax.ShapeDtypeStruct((NG, LANES, LANES), F32),
                   jax.ShapeDtypeStruct((1, PW), F32)],
        input_output_aliases={4: 0},
        scratch_shapes=[pltpu.VMEM((s + POOL_HALO, LANES), F32), pltpu.VMEM((s + POOL_HALO, LANES), F32),
                        pltpu.VMEM((s, LANES), F32)],
        compiler_params=_cparams(),
    )(z, dy, wp, sp, dz)


CONV_LEAD = CONV_HALO - (CONV_K - 1)


SUBLANES = 8


def _sublane_shifts(win):
    n = win.shape[0]
    return [win] + [pltpu.roll(win, n - b, 0) for b in range(1, SUBLANES)]


def _shifted(shifts, offset):
    a, b = divmod(offset, SUBLANES)
    return shifts[b][a * SUBLANES:a * SUBLANES + ROW_TILE]


def _conv_taps(shifts, wdw_ref, lead, reverse):
    acc = jnp.zeros((ROW_TILE, CW), F32)
    for j in range(CONV_K):
        tap = (CONV_K - 1 - j) if reverse else j
        acc = acc + wdw_ref[tap:tap + 1, :] * _shifted(shifts, lead + j)
    return acc


def _conv_fill_glu(cv_ref, cg_ref, xpad, s):
    xpad[pl.ds(0, CONV_HALO), :] = jnp.zeros((CONV_HALO, CW), F32)

    def fill(t, carry):
        t0 = pl.multiple_of(t * ROW_TILE, ROW_TILE)
        rows = pl.ds(t0, ROW_TILE)
        xpad[pl.ds(t0 + CONV_HALO, ROW_TILE), :] = _glu(cv_ref[rows, :], cg_ref[rows, :])
        return carry

    lax.fori_loop(0, s // ROW_TILE, fill, 0)


def _conv_fwd(z, wdw, bdw, lng, lnb, y):
    s = z.shape[0]

    def body(cv_ref, cg_ref, wdw_ref, bdw_ref, lng_ref, lnb_ref, _, y_ref, xpad):
        _conv_fill_glu(cv_ref, cg_ref, xpad, s)

        def tile(t, carry):
            t0 = pl.multiple_of(t * ROW_TILE, ROW_TILE)
            shifts = _sublane_shifts(xpad[pl.ds(t0, ROW_TILE + CONV_HALO), :])
            hc = _conv_taps(shifts, wdw_ref, CONV_LEAD, False) + bdw_ref[...]
            y_ref[pl.ds(t0, ROW_TILE), :] = _ln_silu(hc, lng_ref[...], lnb_ref[...]).astype(BF16)
            return carry

        lax.fori_loop(0, s // ROW_TILE, tile, 0)

    vec = pl.BlockSpec((1, CW), lambda i: (0, 0))
    return pl.pallas_call(
        body, name="conv_fwd", grid=(1,),
        in_specs=[pl.BlockSpec((s, CW), lambda i: (0, (2 * GW + PW) // CW)),
                  pl.BlockSpec((s, CW), lambda i: (0, (2 * GW + PW) // CW + 1)),
                  pl.BlockSpec((CONV_K + 1, CW), lambda i: (0, 0)), vec, vec, vec, ANY],
        out_specs=pl.BlockSpec((s, CW), lambda i: (0, (GW + PW) // CW)),
        out_shape=jax.ShapeDtypeStruct((s, D), BF16), input_output_aliases={6: 0},
        scratch_shapes=[pltpu.VMEM((s + CONV_HALO, CW), F32)], compiler_params=_cparams(),
    )(z, z, wdw, bdw, lng, lnb, y)


def _conv_bwd(z, dy, wdw, bdw, lng, lnb, dz):
    s = z.shape[0]

    def body(cv_ref, cg_ref, dy_ref, wdw_ref, bdw_ref, lng_ref, lnb_ref, _,
             dz_ref, dwdw_ref, dbdw_ref, dlng_ref, dlnb_ref, xpad, dpad, dcg_keep):
        @pl.when(pl.program_id(0) == 0)
        def _():
            compute(cv_ref, cg_ref, dy_ref, wdw_ref, bdw_ref, lng_ref, lnb_ref,
                    dz_ref, dcg_keep, dwdw_ref, dbdw_ref, dlng_ref, dlnb_ref, xpad, dpad)

        @pl.when(pl.program_id(0) == 1)
        def _():
            dz_ref[...] = dcg_keep[...]

    def compute(cv_ref, cg_ref, dy_ref, wdw_ref, bdw_ref, lng_ref, lnb_ref,
                dcv_ref, dcg_ref, dwdw_ref, dbdw_ref, dlng_ref, dlnb_ref, xpad, dpad):
        _conv_fill_glu(cv_ref, cg_ref, xpad, s)
        dpad[pl.ds(s, CONV_HALO), :] = jnp.zeros((CONV_HALO, CW), F32)
        dwdw_ref[...] = jnp.zeros((CONV_K + 1, CW), F32)

        def tile(t, carry):
            db, dg, dbeta = carry
            t0 = pl.multiple_of(t * ROW_TILE, ROW_TILE)
            shifts = _sublane_shifts(xpad[pl.ds(t0, ROW_TILE + CONV_HALO), :])
            hc = _conv_taps(shifts, wdw_ref, CONV_LEAD, False) + bdw_ref[...]
            _, vjp = jax.vjp(_ln_silu, hc, lng_ref[...], lnb_ref[...])
            dhc, dg_t, dbeta_t = vjp(dy_ref[pl.ds(t0, ROW_TILE), :])
            dpad[pl.ds(t0, ROW_TILE), :] = dhc
            for j in range(CONV_K):
                dwdw_ref[j:j + 1, :] += jnp.sum(dhc * _shifted(shifts, CONV_LEAD + j), axis=0, keepdims=True)
            return db + jnp.sum(dhc, axis=0, keepdims=True), dg + dg_t, dbeta + dbeta_t

        zero = jnp.zeros((1, CW), F32)
        db, dg, dbeta = lax.fori_loop(0, s // ROW_TILE, tile, (zero, zero, zero))
        dbdw_ref[...] = db
        dlng_ref[...] = dg
        dlnb_ref[...] = dbeta

        def tile2(t, carry):
            t0 = pl.multiple_of(t * ROW_TILE, ROW_TILE)
            rows = pl.ds(t0, ROW_TILE)
            dglu = _conv_taps(_sublane_shifts(dpad[pl.ds(t0, ROW_TILE + CONV_HALO), :]), wdw_ref, 0, True)
            _, vjp = jax.vjp(_glu, cv_ref[rows, :], cg_ref[rows, :])
            dcv, dcg = vjp(dglu)
            dcv_ref[rows, :] = dcv.astype(BF16)
            dcg_ref[rows, :] = dcg.astype(BF16)
            return carry

        lax.fori_loop(0, s // ROW_TILE, tile2, 0)

    vec = pl.BlockSpec((1, CW), lambda i: (0, 0))
    wspec = pl.BlockSpec((CONV_K + 1, CW), lambda i: (0, 0))
    vshape = jax.ShapeDtypeStruct((1, CW), F32)
    return pl.pallas_call(
        body, name="conv_bwd", grid=(2,),
        in_specs=[pl.BlockSpec((s, CW), lambda i: (0, (2 * GW + PW) // CW)),
                  pl.BlockSpec((s, CW), lambda i: (0, (2 * GW + PW) // CW + 1)),
                  pl.BlockSpec((s, CW), lambda i: (0, (GW + PW) // CW)), wspec, vec, vec, vec, ANY],
        out_specs=[pl.BlockSpec((s, CW), lambda i: (0, (2 * GW + PW) // CW + i)), wspec, vec, vec, vec],
        out_shape=[jax.ShapeDtypeStruct((s, IN_COLS), BF16), jax.ShapeDtypeStruct((CONV_K + 1, CW), F32),
                   vshape, vshape, vshape],
        input_output_aliases={7: 0},
        scratch_shapes=[pltpu.VMEM((s + CONV_HALO, CW), F32), pltpu.VMEM((s + CONV_HALO, CW), F32),
                        pltpu.VMEM((s, CW), BF16)],
        compiler_params=_cparams(),
    )(z, z, dy, wdw, bdw, lng, lnb, dz)


def _softmax_rows(sc):
    e = jnp.exp(sc - jnp.max(sc, axis=-1, keepdims=True))
    return e / jnp.sum(e, axis=-1, keepdims=True)


def _attn_fwd(q, k, v, tq):
    s, m = q.shape[0], k.shape[0]
    tq = min(tq, s)

    def body(q_ref, k_ref, v_ref, o_ref):
        for h in range(XH):
            cols = slice(h * XHD, (h + 1) * XHD)
            p = _softmax_rows(_dot(q_ref[:, cols], k_ref[:, cols], NT) * ATT_SCALE)
            o_ref[:, cols] = _dot(p.astype(BF16), v_ref[:, cols], NN).astype(BF16)

    kv = pl.BlockSpec((m, D), lambda i: (0, 0))
    return pl.pallas_call(
        body, name="attn_fwd", grid=(s // tq,),
        in_specs=[pl.BlockSpec((tq, D), lambda i: (i, 0)), kv, kv],
        out_specs=pl.BlockSpec((tq, D), lambda i: (i, 0)),
        out_shape=jax.ShapeDtypeStruct((s, D), BF16), compiler_params=_cparams(),
    )(q, k, v)


def _attn_bwd(q, k, v, do, tq, after=()):
    s, m = q.shape[0], k.shape[0]
    tq = min(tq, s)

    def body(q_ref, k_ref, v_ref, do_ref, *rest):
        dq_ref, dk_ref, dv_ref = rest[len(after):]

        @pl.when(pl.program_id(0) == 0)
        def _():
            dk_ref[...] = jnp.zeros_like(dk_ref)
            dv_ref[...] = jnp.zeros_like(dv_ref)

        for h in range(XH):
            cols = slice(h * XHD, (h + 1) * XHD)
            qh, kh, vh, doh = q_ref[:, cols], k_ref[:, cols], v_ref[:, cols], do_ref[:, cols]
            p = _softmax_rows(_dot(qh, kh, NT) * ATT_SCALE)
            dp = _dot(doh, vh, NT)
            dv_ref[:, cols] += _dot(p.astype(BF16), doh, TN)
            ds = (p * (dp - jnp.sum(p * dp, axis=-1, keepdims=True)) * ATT_SCALE).astype(BF16)
            dq_ref[:, cols] = _dot(ds, kh, NN).astype(BF16)
            dk_ref[:, cols] += _dot(ds, qh, TN)

    kv = pl.BlockSpec((m, D), lambda i: (0, 0))
    qs = pl.BlockSpec((tq, D), lambda i: (i, 0))
    return pl.pallas_call(
        body, name="attn_bwd", grid=(s // tq,),
        in_specs=[qs, kv, kv, qs] + [ANY] * len(after), out_specs=[qs, kv, kv],
        out_shape=[jax.ShapeDtypeStruct((s, D), BF16), jax.ShapeDtypeStruct((m, D), F32),
                   jax.ShapeDtypeStruct((m, D), F32)],
        compiler_params=_cparams(),
    )(q, k, v, do, *after)


def _loss_head(y, target, tm):
    s = y.shape[0]
    tm = min(tm, s)

    def body(y_ref, t_ref, dy_ref, part_ref):
        err = y_ref[...] - t_ref[...]
        dy_ref[...] = err * (1.0 / D)
        part_ref[...] = jnp.full((1, 8, LANES), 0.5 * jnp.sum(err * err) * (1.0 / D), F32)

    blk = pl.BlockSpec((tm, D), lambda i: (i, 0))
    return pl.pallas_call(
        body, name="loss_head", grid=(s // tm,), in_specs=[blk, blk],
        out_specs=[blk, pl.BlockSpec((1, 8, LANES), lambda i: (i, 0, 0))],
        out_shape=[jax.ShapeDtypeStruct((s, D), F32), jax.ShapeDtypeStruct((s // tm, 8, LANES), F32)],
        compiler_params=_cparams(),
    )(y, target)


def _layer_fwd(x0, mem, w, p, fetch):
    z, hn0 = _rowop_mm("mix_in", "rms", (x0,), p["norm_mix_pre"], w["w_in"], NT, F32)
    y = _gmlp_fwd(z, p["gmlp_v_gain"], p["w_spatial"], p["b_spatial_t"], 1024)
    y = _pool_fwd(z, p["w_pool"], p["s_pool"], y)
    y = _conv_fwd(z, p["w_dw"], p["b_dw"], p["conv_ln_g"], p["conv_ln_b"], y)
    w.update(fetch("out", (y,)))
    x1, h0 = _mm_rowop("mix_out", "rms_res", [(y, w["w_out"], NN)], (x0,), p["norm_mix_post"])
    w.update(fetch("att", (x1,)))
    q, hn1 = _rowop_mm("att_q", "rms", (x1,), p["norm_xattn_pre"], w["w_q"], NN, BF16)
    k, mn = _rowop_mm("att_k", "rms", (mem,), p["norm_mem"], w["w_k"], NN, BF16, after=(x1,))
    v, _ = _rowop_mm("att_v", "rms", (mem,), p["norm_mem"], w["w_v"], NN, BF16, after=(x1,))
    o = _attn_fwd(q, k, v, 1024)
    x2, h1 = _mm_rowop("att_o", "rms_res", [(o, w["w_o"], NN)], (x1,), p["norm_xattn_post"])
    w.update(fetch("up", (x2,)))
    u, hn2 = _rowop_mm("ffn_up", "rms", (x2,), p["norm_ffn_pre"], w["w_up"], NT, F32)
    w.update(fetch("down", (u,)))
    x3, h2 = _mm_rowop("ffn_down", "rms_res", [(u, w["w_down"], NN)], (x2,), p["norm_ffn_post"], relu2=True)
    saved = dict(x0=x0, z=z, hn0=hn0, y=y, h0=h0, x1=x1, q=q, hn1=hn1, k=k, v=v, mn=mn, o=o, h1=h1, x2=x2, u=u,
                 hn2=hn2, h2=h2)
    return x3, saved


def _layer_bwd(dx3, mem, w, p, sv, red):
    gs = {}
    du, dh2, dg = _rowop_mm("ffn_down_bwd", "rms_bwd", (sv["h2"], dx3), p["norm_ffn_post"], w["w_down"], NT, BF16,
                            u=sv["u"], after=red.after())
    gs["norm_ffn_post"] = jnp.sum(dg, axis=0)
    g_down = _mm_tn("ffn_down_dw", sv["u"], dh2, relu2=True)
    red.advance((g_down,))
    red.add("down", ("w_down",), [g_down])
    dx2, dg = _mm_rowop("ffn_up_bwd", "rms_bwd_res", [(du, w["w_up"], NN)], (sv["x2"], dx3), p["norm_ffn_pre"],
                        after=red.after())
    gs["norm_ffn_pre"] = jnp.sum(dg, axis=0)
    red.advance((dx2,))
    g_up = _mm_tn("ffn_up_dw", du, sv["hn2"], after=red.after())
    red.add("up", ("w_up",), [g_up])
    do, dh1, dg = _rowop_mm("att_o_bwd", "rms_bwd", (sv["h1"], dx2), p["norm_xattn_post"], w["w_o"], NT, BF16,
                            after=red.after())
    gs["norm_xattn_post"] = jnp.sum(dg, axis=0)
    g_o = _mm_tn("att_o_dw", sv["o"], dh1)
    red.advance((g_o,))
    dq, dk, dv = _attn_bwd(sv["q"], sv["k"], sv["v"], do, 1024, after=red.after())
    dk, dv = dk.astype(BF16), dv.astype(BF16)
    dx1, dg = _mm_rowop("att_q_bwd", "rms_bwd_res", [(dq, w["w_q"], NT)], (sv["x1"], dx2), p["norm_xattn_pre"],
                        after=red.after())
    gs["norm_xattn_pre"] = jnp.sum(dg, axis=0)
    g_q = _mm_tn("att_q_dw", sv["hn1"], dq)
    g_k = _mm_tn("att_k_dw", sv["mn"], dk)
    g_v = _mm_tn("att_v_dw", sv["mn"], dv)
    (dg,) = _mm_rowop("att_kv_bwd", "rms_bwd_gain", [(dk, w["w_k"], NT), (dv, w["w_v"], NT)], (mem,), p["norm_mem"])
    gs["norm_mem"] = jnp.sum(dg, axis=0)
    red.add("att", ("w_o", "w_q", "w_k", "w_v"), [g_o, g_q, g_k, g_v])
    dy, dh0, dg = _rowop_mm("mix_out_bwd", "rms_bwd", (sv["h0"], dx1), p["norm_mix_post"], w["w_out"], NT, F32,
                            after=red.after())
    gs["norm_mix_post"] = jnp.sum(dg, axis=0)
    g_out = _mm_tn("mix_out_dw", sv["y"], dh0)
    red.advance((g_out,))
    red.add("out", ("w_out",), [g_out])
    z = sv["z"]
    dz, dgv, dws, dbs = _gmlp_bwd(z, dy, p["gmlp_v_gain"], p["w_spatial"], p["b_spatial_t"], 512, after=red.after())
    gs["gmlp_v_gain"] = jnp.sum(dgv, axis=0)
    gs["w_spatial"] = jnp.sum(dws, axis=0)
    gs["b_spatial"] = jnp.sum(dbs[..., 0], axis=0)
    dz, gs["w_pool"], gs["s_pool"] = _pool_bwd(z, dy, p["w_pool"], p["s_pool"], dz)
    dz, dwdw, gs["b_dw"], gs["conv_ln_g"], gs["conv_ln_b"] = _conv_bwd(
        z, dy, p["w_dw"], p["b_dw"], p["conv_ln_g"], p["conv_ln_b"], dz)
    red.advance((dz,))
    g_in = _mm_tn("mix_in_dw", dz, sv["hn0"], after=red.after())
    red.add("in", ("w_in",), [g_in])
    if red.layer == 0:
        red.advance(())
    red.small("mixer", _small_grad_arrays(gs, dwdw, norms=False))
    dx0, dg = _mm_rowop("mix_in_bwd", "rms_bwd_res", [(dz, w["w_in"], NN)], (sv["x0"], dx1), p["norm_mix_pre"],
                        after=red.after())
    gs["norm_mix_pre"] = jnp.sum(dg, axis=0)
    late = {"norms": jnp.concatenate([gs[n] for n in NORM_NAMES], axis=0)}
    if red.layer == 0:
        late["loss"] = red.extra[0]
    red.small("norms", late)
    return dx0


NORM_NAMES = ("norm_mix_pre", "norm_mix_post", "norm_xattn_pre", "norm_mem", "norm_xattn_post", "norm_ffn_pre",
              "norm_ffn_post")
VEC_NAMES = ("s_pool", "b_dw", "conv_ln_g", "conv_ln_b")
SMALL_ARRAYS = ("norms", "gain_bias", "w_spatial", "w_pool", "vecs", "w_dw")


def _small_grad_arrays(gs, dwdw, norms=True):
    out = {"norms": jnp.concatenate([gs[n] for n in NORM_NAMES], axis=0)} if norms else {}
    out.update({"gain_bias": jnp.concatenate([gs["gmlp_v_gain"], gs["b_spatial"]], axis=0),
                "w_spatial": gs["w_spatial"], "w_pool": gs["w_pool"],
                "vecs": jnp.concatenate([gs[n] for n in VEC_NAMES], axis=0), "w_dw": dwdw})
    return out


def _layer_params(small, l):
    p = {n: small[n][l].reshape(1, -1) for n in ("norm_mix_pre", "norm_mix_post", "s_pool", "b_dw", "conv_ln_g",
                                                   "conv_ln_b", "norm_xattn_pre", "norm_mem", "norm_xattn_post",
                                                   "norm_ffn_pre", "norm_ffn_post")}
    p["gmlp_v_gain"] = small["gmlp_v_gain"][l]
    p["w_spatial"] = small["w_spatial"][l]
    p["b_spatial_t"] = small["b_spatial"][l].T
    p["w_pool"] = small["w_pool"][l]
    p["w_dw"] = jnp.pad(small["w_dw"][l], ((0, 1), (0, 0)))
    return p


def _local_step(x, mem, target, fetch, small, red):
    small = dict(small)
    saved, weights, params = [], [], []
    h = x
    marker = ()
    for l in range(DEPTH):
        w = fetch(l, "in", marker)
        if "taps" in w:
            small["w_dw"] = w.pop("taps")
        p = _layer_params(small, l)
        h, sv = _layer_fwd(h, mem, w, p, functools.partial(fetch, l))
        marker = (h,)
        saved.append(sv)
        weights.append(w)
        params.append(p)
    dh, loss = _loss_head(h, target, 1024)
    red.extra = (loss,)
    for l in reversed(range(DEPTH)):
        red.layer = l
        dh = _layer_bwd(dh, mem, weights[l], params[l], saved[l], red)
    return loss, dh


HBM = pl.BlockSpec(memory_space=pltpu.HBM)


def _position():
    return lax.axis_index("x"), lax.axis_index("y"), lax.axis_index("c")


SEM = pl.BlockSpec(memory_space=pltpu.SEMAPHORE)
EFFECT = pltpu.SideEffectType.DATAFLOW_SIDE_EFFECTING
TOKEN = jax.ShapeDtypeStruct((8, LANES), F32)
TOKEN_SPEC = pl.BlockSpec(memory_space=pltpu.VMEM)


def _landing(shape, dtype):
    return pltpu.with_memory_space_constraint(lax.empty(shape, dtype), pltpu.HBM)


def _hbm_shapes(arrays):
    return [pltpu.HBM(a.shape, a.dtype) for a in arrays]


def _block(ref, r, dev):
    return ref.at[pl.ds((4 * dev[0] + 2 * dev[1] + dev[2]) * r, r), :]


def _split_call(name, body, thru, sems_in, after, sems_out, token):
    n = len(thru)
    out_shape = [pltpu.SemaphoreType.DMA(s) for s in sems_out] + _hbm_shapes(thru) + ([TOKEN] if token else [])
    out_specs = [SEM] * len(sems_out) + [HBM] * n + ([TOKEN_SPEC] if token else [])
    return pl.pallas_call(
        body, name=name, in_specs=[HBM] * n + [SEM] * len(sems_in) + [ANY] * len(after),
        out_specs=out_specs, out_shape=out_shape,
        input_output_aliases={i: len(sems_out) + i for i in range(n)},
        compiler_params=pltpu.CompilerParams(has_side_effects=EFFECT),
    )(*thru, *sems_in, *after)


def _place_own(name, srcs, dev, out_dtype, tr):
    n = len(srcs)
    r, cols = srcs[0][0].shape[-2:]
    tr = r if r < 16 else _row_tile(r, tr)
    nb = r // tr

    def body(dev_ref, *refs):
        for a in range(n):
            refs[n + a][...] = refs[a][...].astype(out_dtype)

    in_specs = [pl.BlockSpec((tr, cols), lambda i, d: (i, 0)) if l is None
                else pl.BlockSpec((None, tr, cols), lambda i, d, l=l: (l, i, 0)) for _, l in srcs]
    return pl.pallas_call(
        body, name=name,
        grid_spec=pltpu.PrefetchScalarGridSpec(
            num_scalar_prefetch=1, grid=(nb,), in_specs=in_specs,
            out_specs=[pl.BlockSpec((tr, cols), lambda i, d: (d[0] * nb + i, 0))] * n),
        out_shape=[jax.ShapeDtypeStruct((N_DEV * r, cols), out_dtype)] * n, compiler_params=_cparams(),
    )(dev, *[a for a, _ in srcs])


def _gather_peers(x, y, c):
    return [(1 - x, y, c), (x, 1 - y, c), (1 - x, 1 - y, c), (x, y, 1 - c)]


def _block_rows(land):
    return land.shape[0] // N_DEV


def _near_peers(x, y, c):
    return [(1 - x, y, c), (x, 1 - y, c), (x, y, 1 - c)]


def _relay_route(x, y, c):
    origin = (x + c * (1 - 2 * x), y + (1 - c) * (1 - 2 * y), c)
    target = (x + (1 - c) * (1 - 2 * x), y + c * (1 - 2 * y), c)
    return origin, target


def _same_block_copy(blk, send_sem, recv_sem, to):
    return pltpu.make_async_remote_copy(src_ref=blk, dst_ref=blk, send_sem=send_sem, recv_sem=recv_sem, device_id=to,
                                        device_id_type=MESH)


def _gather_start(name, lands, after):
    n = len(lands)

    def body(*refs):
        lz = refs[:n]
        send_sems, recv_sems = refs[n + len(after)], refs[n + len(after) + 1]
        token = refs[-1]
        x, y, c = _position()
        for a in range(n):
            own = _block(lz[a], _block_rows(lands[a]), (x, y, c))
            for k, to in enumerate(_near_peers(x, y, c)):
                _same_block_copy(own, send_sems.at[k], recv_sems.at[k], to).start()
        token[...] = jnp.zeros_like(token)

    out = _split_call(name, body, list(lands), [], after, [(3,), (3,)], True)
    return out[0], out[1], out[2:2 + n], out[-1]


def _gather_step(name, near, far, fresh, after):
    groups = [g for g in (near and near[0], far and far[0], fresh) if g]
    counts = [len(near[0]) if near else 0, len(far[0]) if far else 0, len(fresh) if fresh else 0]
    n = sum(counts)
    sems_in = ([near[1]] if near else []) + ([far[1]] if far else [])
    sems_out = ([(2,), (2,), (1,), (1,)] if near else []) + ([(1,), (1,)] if far else []) + ([(3,), (3,)] if fresh else [])

    def body(*refs):
        lz = list(refs[:n])
        ins = list(refs[n:n + len(sems_in)])
        outs = list(refs[n + len(sems_in) + len(after):n + len(sems_in) + len(after) + len(sems_out)])
        token = refs[-1]
        x, y, c = _position()
        me, sibling = (x, y, c), (x, y, 1 - c)
        near_lz, far_lz, fresh_lz = (lz[sum(counts[:i]):sum(counts[:i + 1])] for i in range(3))
        neighbours = _near_peers(x, y, c)[:2]
        origin, target = _relay_route(x, y, c)
        diagonal = (1 - x, 1 - y, c)
        if near:
            recv0 = ins.pop(0)
            fsend, frecv, rsend, rrecv = (outs.pop(0) for _ in range(4))
            for a, land in enumerate(near[0]):
                for j, chip in enumerate(neighbours):
                    _same_block_copy(_block(near_lz[a], _block_rows(land), chip), fsend.at[j], recv0.at[j], me).wait_recv()
        if far:
            rrecv_in = ins.pop(0)
            f2send, f2recv = outs.pop(0), outs.pop(0)
            for a, land in enumerate(far[0]):
                _same_block_copy(_block(far_lz[a], _block_rows(land), diagonal), f2send.at[0], rrecv_in.at[0], me).wait_recv()
            for a, land in enumerate(far[0]):
                _same_block_copy(_block(far_lz[a], _block_rows(land), diagonal), f2send.at[0], f2recv.at[0], sibling).start()
        if near:
            for a, land in enumerate(near[0]):
                r = _block_rows(land)
                _same_block_copy(_block(near_lz[a], r, origin), rsend.at[0], rrecv.at[0], target).start()
                for j, chip in enumerate(neighbours):
                    _same_block_copy(_block(near_lz[a], r, chip), fsend.at[j], frecv.at[j], sibling).start()
        if fresh:
            send_sems, recv_sems = outs.pop(0), outs.pop(0)
            for a, land in enumerate(fresh):
                own = _block(fresh_lz[a], _block_rows(land), me)
                for k, to in enumerate(_near_peers(x, y, c)):
                    _same_block_copy(own, send_sems.at[k], recv_sems.at[k], to).start()
        token[...] = jnp.zeros_like(token)

    out = list(_split_call(name, body, [l for g in groups for l in g], sems_in, after, sems_out, True))
    res = {"token": out.pop()}
    if near:
        res.update(fsend=out.pop(0), frecv=out.pop(0), rsend=out.pop(0), rrecv=out.pop(0))
    if far:
        res.update(f2send=out.pop(0), f2recv=out.pop(0))
    if fresh:
        res.update(send=out.pop(0), recv=out.pop(0))
    res["near"], res["far"], res["fresh"] = (out[sum(counts[:i]):sum(counts[:i + 1])] for i in range(3))
    return res


def _gather_finish(name, lands, send_sems, recv_sems, fsend, frecv, rsend, f2send, f2recv, after):
    n = len(lands)

    def body(*refs):
        lz = refs[:n]
        send0, recv0, fsend_ref, frecv_ref, rsend_ref, f2send_ref, f2recv_ref = refs[n:n + 7]
        x, y, c = _position()
        me = (x, y, c)
        near = _near_peers(x, y, c)[:2]
        origin, _ = _relay_route(x, y, c)
        for a in range(n):
            r = _block_rows(lands[a])
            sib = _block(lz[a], r, (x, y, 1 - c))
            _same_block_copy(sib, send0.at[2], recv0.at[2], me).wait_recv()
            for j, chip in enumerate(near):
                blk = _block(lz[a], r, (chip[0], chip[1], 1 - c))
                _same_block_copy(blk, fsend_ref.at[j], frecv_ref.at[j], me).wait_recv()
            far = _block(lz[a], r, (1 - x, 1 - y, 1 - c))
            _same_block_copy(far, f2send_ref.at[0], f2recv_ref.at[0], me).wait_recv()
            own = _block(lz[a], r, me)
            for k in range(3):
                _same_block_copy(own, send0.at[k], recv0.at[k], me).wait_send()
            for j, chip in enumerate(near):
                _same_block_copy(_block(lz[a], r, chip), fsend_ref.at[j], frecv_ref.at[j], me).wait_send()
            _same_block_copy(_block(lz[a], r, origin), rsend_ref.at[0], recv0.at[0], me).wait_send()
            _same_block_copy(_block(lz[a], r, (1 - x, 1 - y, c)), f2send_ref.at[0], f2recv_ref.at[0], me).wait_send()

    return _split_call(name, body, list(lands), [send_sems, recv_sems, fsend, frecv, rsend, f2send, f2recv], after, [],
                       False)


def _sibling_start(name, grads, after):
    n = len(grads)
    lands = [_landing((4, g.shape[0] // N_DEV, D), g.dtype) for g in grads]

    def body(*refs):
        ins, lz = refs[:n], refs[n:2 * n]
        send_sem, recv_sem = refs[2 * n + len(after)], refs[2 * n + len(after) + 1]
        token = refs[-1]
        x, y, c = _position()
        for a in range(n):
            r = grads[a].shape[0] // N_DEV
            for q in range(4):
                pltpu.make_async_remote_copy(
                    src_ref=ins[a].at[pl.ds((2 * q + 1 - c) * r, r), :], dst_ref=lz[a].at[q], send_sem=send_sem.at[0],
                    recv_sem=recv_sem.at[0], device_id=(x, y, 1 - c), device_id_type=MESH).start()
        token[...] = jnp.zeros_like(token)

    out = _split_call(name, body, list(grads) + lands, [], after, [(1,), (1,)], True)
    return out[0], out[1], out[2:2 + n], out[2 + n:2 + 2 * n], out[-1]


def _sibling_finish(name, grads, lands, send_sem, recv_sem, after):
    n = len(grads)

    def body(*refs):
        ins, lz = refs[:n], refs[n:2 * n]
        send_ref, recv_ref = refs[2 * n], refs[2 * n + 1]
        x, y, c = _position()
        for a in range(n):
            r = grads[a].shape[0] // N_DEV
            for q in range(4):
                cp = pltpu.make_async_remote_copy(
                    src_ref=ins[a].at[pl.ds((2 * q + 1 - c) * r, r), :], dst_ref=lz[a].at[q], send_sem=send_ref.at[0],
                    recv_sem=recv_ref.at[0], device_id=(x, y, c), device_id_type=MESH)
                cp.wait_send()
                cp.wait_recv()

    out = _split_call(name, body, list(grads) + list(lands), [send_sem, recv_sem], after, [], False)
    return out[:n], out[n:2 * n]


def _chip_start(name, parts, after):
    n = len(parts)
    lands = [_landing((3,) + p.shape[1:], p.dtype) for p in parts]

    def body(*refs):
        ins, lz = refs[:n], refs[n:2 * n]
        send_sems, recv_sems = refs[2 * n + len(after)], refs[2 * n + len(after) + 1]
        token = refs[-1]
        x, y, c = _position()
        for a in range(n):
            for j, chip in enumerate(_gather_peers(x, y, c)[:3]):
                pltpu.make_async_remote_copy(
                    src_ref=ins[a].at[2 * chip[0] + chip[1]], dst_ref=lz[a].at[j], send_sem=send_sems.at[j],
                    recv_sem=recv_sems.at[j], device_id=chip, device_id_type=MESH).start()
        token[...] = jnp.zeros_like(token)

    out = _split_call(name, body, list(parts) + lands, [], after, [(3,), (3,)], True)
    return out[0], out[1], out[2:2 + n], out[2 + n:2 + 2 * n], out[-1]


def _chip_finish(name, parts, lands, send_sems, recv_sems, after):
    n = len(parts)

    def body(*refs):
        ins, lz = refs[:n], refs[n:2 * n]
        send_ref, recv_ref = refs[2 * n], refs[2 * n + 1]
        me = _position()
        for a in range(n):
            for j in range(3):
                cp = pltpu.make_async_remote_copy(
                    src_ref=ins[a].at[j], dst_ref=lz[a].at[j], send_sem=send_ref.at[j], recv_sem=recv_ref.at[j],
                    device_id=me, device_id_type=MESH)
                cp.wait_send()
                cp.wait_recv()

    out = _split_call(name, body, list(parts) + list(lands), [send_sems, recv_sems], after, [], False)
    return out[:n], out[n:2 * n]


def _other_devices(x, y, c):
    return [(x + (k >> 2 & 1) * (1 - 2 * x), y + (k >> 1 & 1) * (1 - 2 * y), c + (k & 1) * (1 - 2 * c))
            for k in range(1, N_DEV)]


def _broadcast_start(name, arrays, after):
    n = len(arrays)
    lands = [_landing((N_DEV,) + a.shape, a.dtype) for a in arrays]

    def body(*refs):
        ins, lz = refs[:n], refs[n:2 * n]
        send_sems, recv_sems = refs[2 * n + len(after)], refs[2 * n + len(after) + 1]
        token = refs[-1]
        x, y, c = _position()
        for a in range(n):
            for k, peer in enumerate(_other_devices(x, y, c)):
                pltpu.make_async_remote_copy(
                    src_ref=ins[a], dst_ref=lz[a].at[4 * x + 2 * y + c], send_sem=send_sems.at[k],
                    recv_sem=recv_sems.at[k], device_id=peer, device_id_type=MESH).start()
        token[...] = jnp.zeros_like(token)

    out = _split_call(name, body, list(arrays) + lands, [], after, [(N_DEV - 1,), (N_DEV - 1,)], True)
    return out[0], out[1], out[2:2 + n], out[2 + n:2 + 2 * n], out[-1]


def _broadcast_finish(name, arrays, lands, send_sems, recv_sems, after):
    n = len(arrays)

    def body(*refs):
        ins, lz = refs[:n], refs[n:2 * n]
        send_ref, recv_ref = refs[2 * n], refs[2 * n + 1]
        x, y, c = _position()
        for a in range(n):
            for k, peer in enumerate(_other_devices(x, y, c)):
                cp = pltpu.make_async_remote_copy(
                    src_ref=ins[a], dst_ref=lz[a].at[4 * peer[0] + 2 * peer[1] + peer[2]], send_sem=send_ref.at[k],
                    recv_sem=recv_ref.at[k], device_id=(x, y, c), device_id_type=MESH)
                cp.wait_send()
                cp.wait_recv()

    out = _split_call(name, body, list(arrays) + list(lands), [send_sems, recv_sems], after, [], False)
    return out[:n], out[n:2 * n]


def _row_tile(r, target):
    return max(t for t in range(16, min(r, target) + 1, 16) if r % t == 0)


CHIP_PARTIAL_BYTES = 12 * 1024 * 1024


def _chip_partial(name, grads, gots, c):
    n = len(grads)
    r = grads[0].shape[0] // N_DEV
    tr = _row_tile(r, CHIP_PARTIAL_BYTES // (n * 3 * D * 2))

    def body(c_ref, *refs):
        for a in range(n):
            refs[2 * n + a][...] = (refs[a][...].astype(F32) + refs[n + a][...].astype(F32)).astype(BF16)

    blk = pl.BlockSpec((None, tr, D), lambda q, i, c_ref: (q, i, 0))
    return pl.pallas_call(
        body, name=name,
        grid_spec=pltpu.PrefetchScalarGridSpec(
            num_scalar_prefetch=1, grid=(4, r // tr),
            in_specs=[pl.BlockSpec((None, None, tr, D), lambda q, i, c_ref: (q, c_ref[0], i, 0))] * n + [blk] * n,
            out_specs=[blk] * n),
        out_shape=[jax.ShapeDtypeStruct((4, r, D), BF16)] * n, compiler_params=_cparams(),
    )(c, *[g.reshape(4, 2, r, D) for g in grads], *gots)


class _WeightGather:
    def __init__(self, groups):
        self.groups = list(groups)
        self.index = {key: i for i, (key, _, _) in enumerate(groups)}
        self.state = [None] * len(groups)
        self.token = ()
        for i in range(min(2, len(groups))):
            self._start(i)

    def _tag(self, i):
        return "%s_%d" % self.groups[i][0][::-1]

    def _start(self, i):
        send, recv, lz, tok = _gather_start("gather_start_" + self._tag(i), self.groups[i][2], self.token)
        self.state[i] = dict(send=send, recv=recv, lands=lz)
        self.token = (tok,)

    def _step(self, name, near, far, fresh, marker):
        exists = lambda i: i is not None and i < len(self.groups)
        near, far, fresh = (i if exists(i) else None for i in (near, far, fresh))
        res = _gather_step(
            name, None if near is None else (self.state[near]["lands"], self.state[near]["recv"]),
            None if far is None else (self.state[far]["lands"], self.state[far]["rrecv"]),
            None if fresh is None else self.groups[fresh][2], tuple(marker) + self.token)
        self.token = (res["token"],)
        if near is not None:
            self.state[near].update(lands=res["near"], fsend=res["fsend"], frecv=res["frecv"], rsend=res["rsend"],
                                    rrecv=res["rrecv"])
        if far is not None:
            self.state[far].update(lands=res["far"], f2send=res["f2send"], f2recv=res["f2recv"])
        if fresh is not None:
            self.state[fresh] = dict(send=res["send"], recv=res["recv"], lands=res["fresh"])

    def fetch(self, layer, group, marker):
        k = self.index[(layer, group)]
        if k == 0:
            self._step("gather_step_first", 0, None, None, marker)
        self._step("gather_step_" + self._tag(k), k + 1, k, k + 2, marker)
        st = self.state[k]
        lz = _gather_finish("gather_finish_" + self._tag(k), st["lands"], st["send"], st["recv"], st["fsend"],
                            st["frecv"], st["rsend"], st["f2send"], st["f2recv"], self.token)
        self.state[k] = None
        return dict(zip(self.groups[k][1], lz))


class _GradReduce:
    def __init__(self, core, chip):
        self.core, self.chip = core, chip
        self.layer = None
        self.token = ()
        self.at_sibling, self.at_chips = [], []
        self.extra, self.smalls = (), {}

    def after(self):
        return self.token

    def add(self, group, names, grads):
        tag = "%s_%d" % (group, self.layer)
        send, recv, grads, lands, tok = _sibling_start("grad_sibling_start_" + tag, grads, self.token)
        self.at_sibling.append((tag, [(self.layer, n) for n in names], send, recv, grads, lands))
        self.token = (tok,)

    def advance(self, marker):
        for tag, keys, send, recv, grads, lands in self.at_sibling:
            grads, lands = _sibling_finish("grad_sibling_finish_" + tag, grads, lands, send, recv, marker)
            parts = _chip_partial("chip_partial_" + tag, grads, lands, self.core)
            send, recv, parts, lands, tok = _chip_start("grad_chip_start_" + tag, parts, ())
            self.at_chips.append([tag, keys, send, recv, parts, lands])
            self.token = (tok,)
        self.at_sibling = []

    def small(self, part, arrays):
        keys = list(arrays)
        send, recv, own, slots, tok = _broadcast_start(
            "small_grads_start_%d_%s" % (self.layer, part), [arrays[k] for k in keys], self.token)
        self.smalls.setdefault(self.layer, []).append((part, keys, send, recv, own, slots))
        self.token = (tok,)

    def small_finish(self, layer, marker):
        mine, theirs = {}, {}
        for part, keys, send, recv, own, slots in self.smalls[layer]:
            own, slots = _broadcast_finish("small_grads_finish_%d_%s" % (layer, part), own, slots, send, recv, marker)
            mine.update(zip(keys, own))
            theirs.update(zip(keys, slots))
        return mine, theirs

    def collect(self, key, marker):
        for entry in self.at_chips:
            tag, keys, send, recv, parts, lands = entry
            if key in keys:
                if send is not None:
                    parts, lands = _chip_finish("grad_chip_finish_" + tag, parts, lands, send, recv, marker)
                    entry[2:] = [None, None, parts, lands]
                i = keys.index(key)
                return parts[i], lands[i]
        raise KeyError(key)


def _adamw_math(w, g, m, v):
    m = ADAM_B1 * m + (1.0 - ADAM_B1) * g
    v = ADAM_B2 * v + (1.0 - ADAM_B2) * jnp.square(g)
    m_hat = m / (1.0 - ADAM_B1 ** ADAM_STEP)
    v_hat = v / (1.0 - ADAM_B2 ** ADAM_STEP)
    delta = -ADAM_LR * (m_hat / (jnp.sqrt(v_hat) + ADAM_EPS) + ADAM_WD * w)
    return delta, m, v


def _adamw_small(wts, mom_m, mom_v, own, gathered, loss_own, loss_gathered, dev):
    names = SMALL
    nw = len(names)
    na = len(SMALL_ARRAYS)

    def body(dev_ref, *refs):
        w_refs, m_refs, v_refs = (dict(zip(names, refs[i * nw:(i + 1) * nw])) for i in range(3))
        own_refs = refs[3 * nw:3 * nw + DEPTH * na]
        g_refs = refs[3 * nw + DEPTH * na:3 * nw + 2 * DEPTH * na]
        loss_own_ref, loss_got_ref = refs[3 * nw + 2 * DEPTH * na:3 * nw + 2 * DEPTH * na + 2]
        outs = refs[3 * nw + 2 * DEPTH * na + 2:]
        g_out, d_out, m_out, v_out = (dict(zip(names, outs[i * nw:(i + 1) * nw])) for i in range(4))
        me = dev_ref[0]

        loss = None
        for d in range(N_DEV):
            for b in range(loss_own.shape[0]):
                term = jnp.where(me == d, loss_own_ref[b], loss_got_ref[d, b])
                loss = term if loss is None else loss + term
        outs[4 * nw][...] = loss

        def update(name, at, g):
            g_out[name][at] = g
            d_out[name][at], m_out[name][at], v_out[name][at] = _adamw_math(
                w_refs[name][at], g, m_refs[name][at], v_refs[name][at])

        for l in range(DEPTH):
            mine = dict(zip(SMALL_ARRAYS, own_refs[l * na:(l + 1) * na]))
            got = dict(zip(SMALL_ARRAYS, g_refs[l * na:(l + 1) * na]))

            def total(key, at):
                acc = None
                for d in range(N_DEV):
                    term = jnp.where(me == d, mine[key][at] if at else mine[key][...], got[key][(d,) + at])
                    acc = term if acc is None else acc + term
                return acc

            row = (slice(l, l + 1),)
            for k, name in enumerate(NORM_NAMES):
                update(name, row, total("norms", (slice(k, k + 1),)))
            for k, name in enumerate(VEC_NAMES):
                update(name, row, total("vecs", (slice(k, k + 1),)))
            update("gmlp_v_gain", (l,), total("gain_bias", (slice(0, NH),)))
            update("b_spatial", (l,), total("gain_bias", (slice(NH, 2 * NH),)))
            update("w_spatial", (l,), total("w_spatial", ()))
            update("w_pool", (l,), total("w_pool", ()))
            update("w_dw", (l,), total("w_dw", (slice(0, CONV_K),)))

    args = [src[n] for src in (wts, mom_m, mom_v) for n in names]
    args += [src[l][k] for src in (own, gathered) for l in range(DEPTH) for k in SMALL_ARRAYS]
    args += [loss_own, loss_gathered]
    outs = pl.pallas_call(
        body, name="adamw_small",
        in_specs=[pl.BlockSpec(memory_space=pltpu.SMEM)] + [pl.BlockSpec(memory_space=pltpu.VMEM)] * len(args),
        out_shape=[jax.ShapeDtypeStruct(wts[n].shape, F32) for _ in range(4) for n in names]
        + [jax.ShapeDtypeStruct((8, LANES), F32)],
        compiler_params=_cparams(),
    )(dev, *args)
    return tuple(dict(zip(names, outs[i * nw:(i + 1) * nw])) for i in range(4)) + (outs[4 * nw],)


def _adamw_layers(name, w, reduced, m, v, chip, tr, transposed=False, after=()):
    nl, r, cdim = w.shape
    tr = _row_tile(r, tr)
    nb = r // tr

    def body(q_ref, w_ref, p0_ref, g0_ref, p1_ref, g1_ref, m_ref, v_ref, *rest):
        g_ref, d_ref, nm_ref, nv_ref = rest[len(after):]

        def total(p_ref, got_ref):
            acc = p_ref[...].astype(F32)
            for j in range(3):
                acc = acc + got_ref[j].astype(F32)
            return acc

        g = jnp.where(pl.program_id(0) == 0, total(p0_ref, g0_ref), total(p1_ref, g1_ref))
        if transposed:
            g = g.T
        g_ref[...] = g
        d_ref[...], nm_ref[...], nv_ref[...] = _adamw_math(w_ref[...], g, m_ref[...], v_ref[...])

    blk = pl.BlockSpec((None, tr, cdim), lambda l, i, q: (l, i, 0))
    first = lambda l, i: i * (1 - l) + (nb - 1) * l
    second = lambda l, i: i * l
    if transposed:
        gshape = (cdim, tr)
        at = lambda lead, i: (lead, 0, i)
    else:
        gshape = (tr, cdim)
        at = lambda lead, i: (lead, i, 0)
    specs = [blk,
             pl.BlockSpec((None,) + gshape, lambda l, i, q: at(q[0], first(l, i))),
             pl.BlockSpec((3,) + gshape, lambda l, i, q: at(0, first(l, i))),
             pl.BlockSpec((None,) + gshape, lambda l, i, q: at(q[0], second(l, i))),
             pl.BlockSpec((3,) + gshape, lambda l, i, q: at(0, second(l, i))), blk, blk] + [ANY] * len(after)
    shape = jax.ShapeDtypeStruct((nl, r, cdim), F32)
    return pl.pallas_call(
        body, name=name,
        grid_spec=pltpu.PrefetchScalarGridSpec(num_scalar_prefetch=1, grid=(nl, nb), in_specs=specs, out_specs=[blk] * 4),
        out_shape=[shape] * 4, compiler_params=_cparams(),
    )(chip, w, *reduced[0], *reduced[1], m, v, *after)


def _to_rows(name, a):
    return jnp.swapaxes(a, 1, 2) if name == "w_in" else a


def _place_own_transposed(name, srcs, dev, out_dtype, tc):
    n = len(srcs)
    kdim, cdim = srcs[0][0].shape[-2:]

    def body(dev_ref, *refs):
        for a in range(n):
            refs[n + a][...] = refs[a][...].T.astype(out_dtype)

    return pl.pallas_call(
        body, name=name,
        grid_spec=pltpu.PrefetchScalarGridSpec(
            num_scalar_prefetch=1, grid=(kdim // tc,),
            in_specs=[pl.BlockSpec((None, tc, cdim), lambda i, d, l=l: (l, i, 0)) for _, l in srcs],
            out_specs=[pl.BlockSpec((cdim, tc), lambda i, d: (d[0], i))] * n),
        out_shape=[jax.ShapeDtypeStruct((N_DEV * cdim, kdim), out_dtype)] * n, compiler_params=_cparams(),
    )(dev, *[a for a, _ in srcs])


def _pack(arrays, rows):
    flat = jnp.concatenate([a.reshape(-1) for a in arrays])
    return jnp.pad(flat, (0, rows * D - flat.shape[0])).reshape(rows, D)


def _rows_for(shapes, mult=8):
    total = 0
    for shp in shapes:
        size = 1
        for dim in shp:
            size *= dim
        total += size
    return -(-total // (mult * D)) * mult


def kernel(x, mem, norm_mix_pre, norm_mix_post, w_in, w_out, gmlp_v_gain, w_spatial, b_spatial, w_pool, s_pool, w_dw, b_dw, conv_ln_g, conv_ln_b, norm_xattn_pre, norm_mem, norm_xattn_post, w_q, w_k, w_v, w_o, norm_ffn_pre, norm_ffn_post, w_up, w_down, loss_target, m_norm_mix_pre, m_norm_mix_post, m_w_in, m_w_out, m_gmlp_v_gain, m_w_spatial, m_b_spatial, m_w_pool, m_s_pool, m_w_dw, m_b_dw, m_conv_ln_g, m_conv_ln_b, m_norm_xattn_pre, m_norm_mem, m_norm_xattn_post, m_w_q, m_w_k, m_w_v, m_w_o, m_norm_ffn_pre, m_norm_ffn_post, m_w_up, m_w_down, v_norm_mix_pre, v_norm_mix_post, v_w_in, v_w_out, v_gmlp_v_gain, v_w_spatial, v_b_spatial, v_w_pool, v_s_pool, v_w_dw, v_b_dw, v_conv_ln_g, v_conv_ln_b, v_norm_xattn_pre, v_norm_mem, v_norm_xattn_post, v_w_q, v_w_k, v_w_v, v_w_o, v_norm_ffn_pre, v_norm_ffn_post, v_w_up, v_w_down):
    args = dict(locals())
    wts = {n: args[n] for n in WEIGHTS}
    mom_m = {n: args["m_" + n] for n in WEIGHTS}
    mom_v = {n: args["v_" + n] for n in WEIGHTS}
    xi, yi, ci = _position()
    me = 4 * xi + 2 * yi + ci

    dev = jnp.reshape(me, (1,)).astype(jnp.int32)
    lands = {}
    for call, names, tr in (("place_in", ("w_in",), 256), ("place_att", ("w_out", "w_q", "w_k", "w_v", "w_o"), 64),
                            ("place_up", ("w_up",), 256), ("place_down", ("w_down",), 256)):
        srcs = [(_to_rows(n, wts[n]), l) for l in range(DEPTH) for n in names]
        placed = (_place_own_transposed if names == ("w_up",) else _place_own)(call, srcs, dev, BF16, tr)
        lands.update(zip([(l, n) for l in range(DEPTH) for n in names], placed))
    (lands[(0, "taps")],) = _place_own("place_taps", [(_pack([w_dw], _rows_for([w_dw.shape])), None)], dev, F32, 8)
    groups = []
    for l in range(DEPTH):
        for group, names in GATHER_GROUPS:
            if (l, group) == (0, "in"):
                names = names + ("taps",)
            groups.append(((l, group), names, [lands[(l, n)] for n in names]))
    gather = _WeightGather(groups)

    def fetch(layer, group, marker):
        w = gather.fetch(layer, group, marker)
        if "taps" in w:
            blocks = w["taps"].reshape(N_DEV, -1)[:, :w_dw.size].reshape((N_DEV,) + w_dw.shape)
            w["taps"] = jnp.moveaxis(blocks, 0, 2).reshape(DEPTH, CONV_K, CW)
        return w

    reduce = _GradReduce(jnp.reshape(ci, (1,)).astype(jnp.int32), jnp.reshape(2 * xi + yi, (1,)).astype(jnp.int32))
    small = {n: wts[n] for n in SMALL if n != "w_dw"}
    _, dx = _local_step(x[0], mem[0], loss_target[0], fetch, small, reduce)
    reduce.advance((dx,))

    grad_w, delta, new_m, new_v = {}, {}, {}, {}
    marker = (dx,) + tuple(reduce.after())
    for n in UPDATE_ORDER:
        reduced = [reduce.collect((l, n), marker) for l in range(DEPTH)]
        outs = _adamw_layers("adamw_" + n, _to_rows(n, wts[n]), reduced, _to_rows(n, mom_m[n]), _to_rows(n, mom_v[n]),
                             reduce.chip, 256, transposed=n == "w_up", after=marker)
        grad_w[n], delta[n], new_m[n], new_v[n] = (_to_rows(n, o) for o in outs)
        marker = (outs[1],)

    own, slots = [None] * DEPTH, [None] * DEPTH
    for l in reversed(range(DEPTH)):
        own[l], slots[l] = reduce.small_finish(l, marker)
        if l == 0:
            loss_own, loss_slots = own[l].pop("loss"), slots[l].pop("loss")
    shard_cols = CW // N_DEV
    for l in range(DEPTH):
        own[l]["w_dw"] = lax.dynamic_slice_in_dim(own[l]["w_dw"], me * shard_cols, shard_cols, axis=1)
        slots[l]["w_dw"] = lax.dynamic_slice_in_dim(slots[l]["w_dw"], me * shard_cols, shard_cols, axis=2)
    *small_out, loss_tile = _adamw_small(wts, mom_m, mom_v, own, slots, loss_own, loss_slots, dev)
    for dst, src in zip((grad_w, delta, new_m, new_v), small_out):
        dst.update(src)

    return (loss_tile[0, 0], dx[None], *[grad_w[n] for n in WEIGHTS], *[delta[n] for n in WEIGHTS],
            *[new_m[n] for n in WEIGHTS], *[new_v[n] for n in WEIGHTS])
```

```python
import functools

import jax
import jax.numpy as jnp
from jax import lax
from jax.experimental import pallas as pl
from jax.experimental.pallas import tpu as pltpu

F32 = jnp.float32
BF16 = jnp.bfloat16

D = 2048
GW = 1024
PW = 512
CW = 512
HD = 128
NH = 8
NG = 4
POOL_WINDOWS = (2, 4, 8, 16)
CONV_K = 31
IN_COLS = 2 * GW + PW + 2 * CW
XH = 4
XHD = D // XH
ATT_SCALE = XHD ** -0.5
RMS_EPS = 1e-6
LN_EPS = 1e-5
DEPTH = 2
N_DEV = 8

ADAM_LR = 0.001
ADAM_B1 = 0.9
ADAM_B2 = 0.999
ADAM_EPS = 1e-08
ADAM_WD = 0.01
ADAM_STEP = 10

LANES = 128
CONV_HALO = 32
POOL_HALO = 16
ROW_TILE = 128
VMEM_LIMIT = 60 * 1024 * 1024

MESH = pl.DeviceIdType.MESH
NT = (((1,), (1,)), ((), ()))
NN = (((1,), (0,)), ((), ()))
TN = (((0,), (0,)), ((), ()))

UPDATE_ORDER = ("w_down", "w_up", "w_o", "w_q", "w_k", "w_v", "w_out", "w_in")
GATHER_GROUPS = (("in", ("w_in",)), ("out", ("w_out",)), ("att", ("w_q", "w_k", "w_v", "w_o")), ("up", ("w_up",)),
                 ("down", ("w_down",)))
SMALL = ("norm_mix_pre", "norm_mix_post", "gmlp_v_gain", "w_spatial", "b_spatial", "w_pool", "s_pool",
         "w_dw", "b_dw", "conv_ln_g", "conv_ln_b", "norm_xattn_pre", "norm_mem", "norm_xattn_post",
         "norm_ffn_pre", "norm_ffn_post")
WEIGHTS = ("norm_mix_pre", "norm_mix_post", "w_in", "w_out", "gmlp_v_gain", "w_spatial", "b_spatial", "w_pool",
           "s_pool", "w_dw", "b_dw", "conv_ln_g", "conv_ln_b", "norm_xattn_pre", "norm_mem", "norm_xattn_post",
           "w_q", "w_k", "w_v", "w_o", "norm_ffn_pre", "norm_ffn_post", "w_up", "w_down")


def _cparams():
    return pltpu.CompilerParams(vmem_limit_bytes=VMEM_LIMIT)


def _dot(a, b, dims):
    return lax.dot_general(a, b, dims, preferred_element_type=F32)


def _rms(x, g):
    y = x * lax.rsqrt(jnp.mean(x * x, axis=-1, keepdims=True) + RMS_EPS)
    return y * g


def _rms_bwd(x, g, dy):
    r = lax.rsqrt(jnp.mean(x * x, axis=-1, keepdims=True) + RMS_EPS)
    xh = x * r
    t = dy * g
    dx = r * (t - xh * jnp.mean(t * xh, axis=-1, keepdims=True))
    return dx, jnp.sum(dy * xh, axis=0, keepdims=True)


def _gelu(x):
    cdf = 0.5 * (1.0 + jnp.tanh(0.7978845608028654 * (x + 0.044715 * (x * x * x))))
    return x * cdf


def _layer_norm(x, g, b=None):
    mu = jnp.mean(x, axis=-1, keepdims=True)
    xc = x - mu
    var = jnp.mean(xc * xc, axis=-1, keepdims=True)
    y = xc * lax.rsqrt(var + LN_EPS) * g
    return y if b is None else y + b


def _sigmoid(x):
    return 1.0 / (1.0 + jnp.exp(-x))


def _gmlp_rows(zu, zv, gv):
    return _gelu(zu), _layer_norm(_gelu(zv), gv)


def _glu(cv, cg):
    return cv * _sigmoid(cg)


def _ln_silu(h, g, b):
    y = _layer_norm(h, g, b)
    return y * _sigmoid(y)


ANY = pl.BlockSpec(memory_space=pl.ANY)


ROWS_TILE = 256
COLS_TILE = 512
DW_TILE = 512
RESIDENT_K = 2048
RESIDENT_ROWS = 512
STREAM_K_TILE = 1024
STREAM_ROWS = 512


def _k_tiles(kdim):
    if kdim <= RESIDENT_K:
        return RESIDENT_ROWS, kdim
    if kdim <= IN_COLS:
        return ROWS_TILE, kdim
    return STREAM_ROWS, max(t for t in range(LANES, STREAM_K_TILE + 1, LANES) if kdim % t == 0)


def _rowop_mm(name, kind, rows, g, w, dims, out_dtype, u=None, after=()):
    s = rows[0].shape[0]
    n = w.shape[0] if dims == NT else w.shape[1]
    resident = n <= IN_COLS and u is None
    tm, tn = min(RESIDENT_ROWS if resident and n <= RESIDENT_K else ROWS_TILE, s), min(COLS_TILE, n)
    ni, nj = s // tm, n // tn
    bwd = kind == "rms_bwd"
    out_shape = [jax.ShapeDtypeStruct((s, n), out_dtype), jax.ShapeDtypeStruct((s, D), BF16)]
    if bwd:
        out_shape.append(jax.ShapeDtypeStruct((ni, 1, D), F32))

    if resident:
        def row_body(*refs):
            refs = list(refs)
            row_refs = [refs.pop(0) for _ in rows]
            g_ref, w_ref = refs.pop(0), refs.pop(0)
            del refs[:len(after)]
            if bwd:
                a, dg = _rms_bwd(row_refs[0][...], g_ref[...], row_refs[1][...])
                refs[2][0] = dg
            else:
                a = _rms(row_refs[0][...], g_ref[...])
            a = a.astype(BF16)
            refs[1][...] = a
            refs[0][...] = _dot(a, w_ref[...], dims).astype(out_dtype)

        blk = pl.BlockSpec((tm, D), lambda i: (i, 0))
        return pl.pallas_call(
            row_body, name=name, grid=(ni,),
            in_specs=[blk] * len(rows) + [pl.BlockSpec((1, D), lambda i: (0, 0)),
                                          pl.BlockSpec(w.shape, lambda i: (0, 0), pipeline_mode=pl.Buffered(1))]
            + [ANY] * len(after),
            out_specs=[pl.BlockSpec((tm, n), lambda i: (i, 0)), blk]
            + ([pl.BlockSpec((1, 1, D), lambda i: (i, 0, 0))] if bwd else []),
            out_shape=out_shape, compiler_params=_cparams(),
        )(*rows, g, w, *after)

    def body(*refs):
        refs = list(refs)
        row_refs = [refs.pop(0) for _ in rows]
        g_ref, w_ref = refs.pop(0), refs.pop(0)
        u_ref = refs.pop(0) if u is not None else None
        del refs[:len(after)]
        out_ref, a_ref = refs.pop(0), refs.pop(0)
        dg_ref = refs.pop(0) if bwd else None
        a_all = refs.pop(0)
        t = pl.program_id(0)

        @pl.when(t < ni)
        def _():
            if bwd:
                a, dg = _rms_bwd(row_refs[0][...], g_ref[...], row_refs[1][...])
                dg_ref[0] = dg
            else:
                a = _rms(row_refs[0][...], g_ref[...])
            a_ref[...] = a.astype(BF16)
            a_all[pl.ds(pl.multiple_of(t * tm, tm), tm), :] = a.astype(BF16)

        @pl.when(t >= ni)
        def _():
            acc = _dot(a_all[...], w_ref[...], dims)
            if u_ref is not None:
                acc = acc * (2.0 * jnp.maximum(u_ref[...], 0.0))
            out_ref[...] = acc.astype(out_dtype)

    rows_at = lambda t: jnp.minimum(t, ni - 1)
    cols_at = lambda t: jnp.maximum(t - ni, 0)
    row_spec = pl.BlockSpec((tm, D), lambda t: (rows_at(t), 0))
    w_spec = (pl.BlockSpec((tn, D), lambda t: (cols_at(t), 0)) if dims == NT
              else pl.BlockSpec((D, tn), lambda t: (0, cols_at(t))))
    tile = pl.BlockSpec((s, tn), lambda t: (0, cols_at(t)))
    in_specs = [row_spec] * len(rows) + [pl.BlockSpec((1, D), lambda t: (0, 0)), w_spec]
    in_specs += ([tile] if u is not None else []) + [ANY] * len(after)
    out_specs = [tile, row_spec]
    if bwd:
        out_specs.append(pl.BlockSpec((1, 1, D), lambda t: (rows_at(t), 0, 0)))
    return pl.pallas_call(
        body, name=name, grid=(ni + nj,), in_specs=in_specs, out_specs=out_specs, out_shape=out_shape,
        scratch_shapes=[pltpu.VMEM((s, D), BF16)], compiler_params=_cparams(),
    )(*rows, g, w, *([u] if u is not None else []), *after)


def _mm_rowop(name, kind, pairs, rows, g, relu2=False, after=()):
    s, kdim = pairs[0][0].shape
    tm, tk = _k_tiles(kdim)
    tm = min(tm, s)
    ni, nk = s // tm, kdim // tk
    npair = len(pairs)

    def body(*refs):
        refs = list(refs)
        a_refs = [refs.pop(0) for _ in range(npair)]
        w_refs = [refs.pop(0) for _ in range(npair)]
        row_refs = [refs.pop(0) for _ in rows]
        g_ref = refs.pop(0)
        del refs[:len(after)]
        acc = refs.pop() if nk > 1 else None
        outs = refs
        k = pl.program_id(1)

        def product():
            total = None
            for a_ref, w_ref, (_, _, dims) in zip(a_refs, w_refs, pairs):
                a = a_ref[...]
                if relu2:
                    a = jnp.square(jnp.maximum(a, 0.0))
                term = _dot(a.astype(BF16), w_ref[...], dims)
                total = term if total is None else total + term
            return total

        def finish(h):
            if kind == "rms_res":
                outs[0][...] = row_refs[0][...] + _rms(h, g_ref[...])
                outs[1][...] = h
            else:
                dx, dg = _rms_bwd(row_refs[0][...], g_ref[...], h)
                if kind == "rms_bwd_res":
                    outs[0][...] = row_refs[1][...] + dx
                    outs[1][0] = dg
                else:
                    outs[0][0] = dg

        if nk == 1:
            finish(product())
            return

        @pl.when(k == 0)
        def _():
            acc[...] = jnp.zeros_like(acc)

        acc[...] += product()

        @pl.when(k == nk - 1)
        def _():
            finish(acc[...])

    row_spec = pl.BlockSpec((tm, D), lambda i, k: (i, 0))
    dg_shape = jax.ShapeDtypeStruct((ni, 1, D), F32)
    dg_spec = pl.BlockSpec((1, 1, D), lambda i, k: (i, 0, 0))
    in_specs = [pl.BlockSpec((tm, tk), lambda i, k: (i, k))] * npair
    for _, _, dims in pairs:
        mode = dict(pipeline_mode=pl.Buffered(1)) if nk == 1 else {}
        in_specs.append(pl.BlockSpec((tk, D), lambda i, k: (k, 0), **mode) if dims == NN
                        else pl.BlockSpec((D, tk), lambda i, k: (0, k), **mode))
    in_specs += [row_spec] * len(rows) + [pl.BlockSpec((1, D), lambda i, k: (0, 0))] + [ANY] * len(after)
    if kind == "rms_res":
        out_shape = [jax.ShapeDtypeStruct((s, D), F32)] * 2
        out_specs = [row_spec, row_spec]
    elif kind == "rms_bwd_res":
        out_shape = [jax.ShapeDtypeStruct((s, D), F32), dg_shape]
        out_specs = [row_spec, dg_spec]
    else:
        out_shape = [dg_shape]
        out_specs = [dg_spec]
    return pl.pallas_call(
        body, name=name, grid=(ni, nk), in_specs=in_specs, out_specs=out_specs, out_shape=out_shape,
        scratch_shapes=[pltpu.VMEM((tm, D), F32)] if nk > 1 else [], compiler_params=_cparams(),
    )(*[p[0] for p in pairs], *[p[1] for p in pairs], *rows, g, *after)


def _mm_tn(name, a, gmat, relu2=False, after=()):
    s, m = a.shape
    tm = min(DW_TILE, m)
    ni = m // tm

    def body(a_ref, g_ref, *rest):
        av = a_ref[...]
        if relu2:
            av = jnp.square(jnp.maximum(av, 0.0))
        rest[len(after)][...] = _dot(av.astype(BF16), g_ref[...], TN).astype(BF16)

    return pl.pallas_call(
        body, name=name, grid=(ni,),
        in_specs=[pl.BlockSpec((s, tm), lambda i: (0, i)), pl.BlockSpec((s, D), lambda i: (0, 0))] + [ANY] * len(after),
        out_specs=pl.BlockSpec((tm, D), lambda i: (i, 0)),
        out_shape=jax.ShapeDtypeStruct((m, D), BF16), compiler_params=_cparams(),
    )(a, gmat, *after)


def _tril():
    r = lax.broadcasted_iota(jnp.int32, (HD, HD), 0)
    c = lax.broadcasted_iota(jnp.int32, (HD, HD), 1)
    return (c <= r).astype(F32)


def _gmlp_fwd(z, gv, ws, bst, tb):
    s = z.shape[0]
    tb = min(tb, s)

    def body(zu_ref, zv_ref, gv_ref, ws_ref, bst_ref, y_ref):
        tril = _tril()
        for h in range(NH):
            cols = slice(h * HD, (h + 1) * HD)
            u, vln = _gmlp_rows(zu_ref[:, cols], zv_ref[:, cols], gv_ref[h:h + 1, :])
            wm = (ws_ref[h] * tril).astype(BF16)
            vb = vln.astype(BF16)
            for c in range(tb // HD):
                rws = slice(c * HD, (c + 1) * HD)
                mixed = _dot(wm, vb[rws], NN) + bst_ref[:, h:h + 1]
                y_ref[rws, cols] = (u[rws] * mixed).astype(BF16)

    return pl.pallas_call(
        body, name="gmlp_fwd", grid=(s // tb,),
        in_specs=[pl.BlockSpec((tb, GW), lambda i: (i, 0)), pl.BlockSpec((tb, GW), lambda i: (i, 1)),
                  pl.BlockSpec((NH, HD), lambda i: (0, 0)), pl.BlockSpec((NH, HD, HD), lambda i: (0, 0, 0)),
                  pl.BlockSpec((HD, NH), lambda i: (0, 0))],
        out_specs=pl.BlockSpec((tb, GW), lambda i: (i, 0)),
        out_shape=jax.ShapeDtypeStruct((s, D), BF16), compiler_params=_cparams(),
    )(z, z, gv, ws, bst)


def _gmlp_bwd(z, dy, gv, ws, bst, tb, after=()):
    s = z.shape[0]
    tb = min(tb, s)
    nb = s // tb

    def body(zu_ref, zv_ref, dy_ref, gv_ref, ws_ref, bst_ref, *rest):
        dz_ref, dgv_ref, dws_ref, db_ref = rest[len(after):]
        tril = _tril()
        for h in range(NH):
            cols = slice(h * HD, (h + 1) * HD)
            (u, vln), vjp = jax.vjp(_gmlp_rows, zu_ref[:, cols], zv_ref[:, cols], gv_ref[h:h + 1, :])
            wmf = ws_ref[h] * tril
            wm = wmf.astype(BF16)
            wmt = wmf.T.astype(BF16)
            vb = vln.astype(BF16)
            dws = jnp.zeros((HD, HD), F32)
            db = jnp.zeros((HD, 1), F32)
            du_parts, dvln_parts = [], []
            for c in range(tb // HD):
                rws = slice(c * HD, (c + 1) * HD)
                mixed = _dot(wm, vb[rws], NN) + bst_ref[:, h:h + 1]
                dyc = dy_ref[rws, cols]
                du_parts.append(dyc * mixed)
                dmixed = dyc * u[rws]
                dmb = dmixed.astype(BF16)
                dws = dws + _dot(dmb, vb[rws], NT)
                db = db + jnp.sum(dmixed, axis=1, keepdims=True)
                dvln_parts.append(_dot(wmt, dmb, NN))
            du = jnp.concatenate(du_parts, axis=0)
            dvln = jnp.concatenate(dvln_parts, axis=0)
            dzu, dzv, dgv = vjp((du, dvln))
            dz_ref[:, cols] = dzu.astype(BF16)
            dz_ref[:, slice(GW + h * HD, GW + (h + 1) * HD)] = dzv.astype(BF16)
            dgv_ref[0, h:h + 1, :] = dgv
            dws_ref[0, h] = dws * tril
            db_ref[0, h] = jnp.broadcast_to(db, (HD, LANES))

    blk = pl.BlockSpec((tb, GW), lambda i: (i, 0))
    return pl.pallas_call(
        body, name="gmlp_bwd", grid=(nb,),
        in_specs=[blk, pl.BlockSpec((tb, GW), lambda i: (i, 1)), blk,
                  pl.BlockSpec((NH, HD), lambda i: (0, 0)), pl.BlockSpec((NH, HD, HD), lambda i: (0, 0, 0)),
                  pl.BlockSpec((HD, NH), lambda i: (0, 0))] + [ANY] * len(after),
        out_specs=[pl.BlockSpec((tb, 2 * GW), lambda i: (i, 0)), pl.BlockSpec((1, NH, HD), lambda i: (i, 0, 0)),
                   pl.BlockSpec((1, NH, HD, HD), lambda i: (i, 0, 0, 0)),
                   pl.BlockSpec((1, NH, HD, LANES), lambda i: (i, 0, 0, 0))],
        out_shape=[jax.ShapeDtypeStruct((s, IN_COLS), BF16),
                   jax.ShapeDtypeStruct((nb, NH, HD), F32), jax.ShapeDtypeStruct((nb, NH, HD, HD), F32),
                   jax.ShapeDtypeStruct((nb, NH, HD, LANES), F32)],
        compiler_params=_cparams(),
    )(z, z, dy, gv, ws, bst, *after)


POOL_TILE = 1024


def _pool_count(t0, window):
    pos = (t0 + lax.broadcasted_iota(jnp.int32, (POOL_TILE, LANES), 0)).astype(F32)
    return jnp.minimum(pos + 1.0, float(window))


def _window_sum(win, levels, back):
    n = win.shape[0]
    for lv in range(levels):
        step = 1 << lv
        win = win + pltpu.roll(win, n - step if back else step, 0)
    return win


def _pool_pooled(ppad_ref, t0, g):
    win = ppad_ref[pl.ds(t0, POOL_TILE + POOL_HALO), :]
    wsum = _window_sum(win, g + 1, False)[POOL_HALO:]
    return wsum / _pool_count(t0, POOL_WINDOWS[g]) - win[POOL_HALO:]


def _pool_fwd(z, wp, sp, y):
    s = z.shape[0]
    nt = s // POOL_TILE

    def body(p_ref, wp_ref, sp_ref, _, y_ref, ppad):
        for g in range(NG):
            cols = slice(g * LANES, (g + 1) * LANES)
            ppad[pl.ds(0, POOL_HALO), :] = jnp.zeros((POOL_HALO, LANES), F32)
            ppad[pl.ds(POOL_HALO, s), :] = p_ref[:, cols]
            wpb = wp_ref[g].astype(BF16)
            scale = sp_ref[:, cols]

            def tile(t, carry):
                t0 = pl.multiple_of(t * POOL_TILE, POOL_TILE)
                pooled = _pool_pooled(ppad, t0, g)
                y_ref[pl.ds(t0, POOL_TILE), cols] = (_dot(pooled.astype(BF16), wpb, NN) * scale).astype(BF16)
                return carry

            lax.fori_loop(0, nt, tile, 0)

    return pl.pallas_call(
        body, name="pool_fwd", grid=(1,),
        in_specs=[pl.BlockSpec((s, PW), lambda i: (0, 2 * GW // PW)),
                  pl.BlockSpec((NG, LANES, LANES), lambda i: (0, 0, 0)), pl.BlockSpec((1, PW), lambda i: (0, 0)), ANY],
        out_specs=pl.BlockSpec((s, PW), lambda i: (0, GW // PW)),
        out_shape=jax.ShapeDtypeStruct((s, D), BF16), input_output_aliases={3: 0},
        scratch_shapes=[pltpu.VMEM((s + POOL_HALO, LANES), F32)], compiler_params=_cparams(),
    )(z, wp, sp, y)


def _pool_bwd(z, dy, wp, sp, dz):
    s = z.shape[0]
    nt = s // POOL_TILE

    def body(p_ref, dy_ref, wp_ref, sp_ref, _, dp_ref, dwp_ref, dsp_ref, ppad, rpad, dpool):
        for g in range(NG):
            cols = slice(g * LANES, (g + 1) * LANES)
            ppad[pl.ds(0, POOL_HALO), :] = jnp.zeros((POOL_HALO, LANES), F32)
            ppad[pl.ds(POOL_HALO, s), :] = p_ref[:, cols]
            rpad[pl.ds(s, POOL_HALO), :] = jnp.zeros((POOL_HALO, LANES), F32)
            wpb = wp_ref[g].astype(BF16)
            scale = sp_ref[:, cols]

            def tile(t, carry):
                dwp, dsp = carry
                t0 = pl.multiple_of(t * POOL_TILE, POOL_TILE)
                pooled = _pool_pooled(ppad, t0, g)
                pb = pooled.astype(BF16)
                dyt = dy_ref[pl.ds(t0, POOL_TILE), cols]
                dsp = dsp + jnp.sum(dyt * _dot(pb, wpb, NN), axis=0, keepdims=True)
                dmm = (dyt * scale).astype(BF16)
                dwp = dwp + _dot(pb, dmm, TN)
                dpooled = _dot(dmm, wpb, NT)
                rpad[pl.ds(t0, POOL_TILE), :] = dpooled / _pool_count(t0, POOL_WINDOWS[g])
                dpool[pl.ds(t0, POOL_TILE), :] = dpooled
                return dwp, dsp

            dwp, dsp = lax.fori_loop(0, nt, tile, (jnp.zeros((LANES, LANES), F32), jnp.zeros((1, LANES), F32)))
            dwp_ref[g] = dwp
            dsp_ref[:, cols] = dsp

            def tile2(t, carry):
                t0 = pl.multiple_of(t * POOL_TILE, POOL_TILE)
                win = rpad[pl.ds(t0, POOL_TILE + POOL_HALO), :]
                back = _window_sum(win, g + 1, True)[:POOL_TILE]
                rows = pl.ds(t0, POOL_TILE)
                dp_ref[rows, cols] = (back - dpool[rows, :]).astype(BF16)
                return carry

            lax.fori_loop(0, nt, tile2, 0)

    return pl.pallas_call(
        body, name="pool_bwd", grid=(1,),
        in_specs=[pl.BlockSpec((s, PW), lambda i: (0, 2 * GW // PW)), pl.BlockSpec((s, PW), lambda i: (0, GW // PW)),
                  pl.BlockSpec((NG, LANES, LANES), lambda i: (0, 0, 0)), pl.BlockSpec((1, PW), lambda i: (0, 0)), ANY],
        out_specs=[pl.BlockSpec((s, PW), lambda i: (0, 2 * GW // PW)),
                   pl.BlockSpec((NG, LANES, LANES), lambda i: (0, 0, 0)), pl.BlockSpec((1, PW), lambda i: (0, 0))],
        out_shape=[jax.ShapeDtypeStruct((s, IN_COLS), BF16), jax.ShapeDtypeStruct((NG, LANES, LANES), F32),
                   jax.ShapeDtypeStruct((1, PW), F32)],
        input_output_aliases={4: 0},
        scratch_shapes=[pltpu.VMEM((s + POOL_HALO, LANES), F32), pltpu.VMEM((s + POOL_HALO, LANES), F32),
                        pltpu.VMEM((s, LANES), F32)],
        compiler_params=_cparams(),
    )(z, dy, wp, sp, dz)


CONV_LEAD = CONV_HALO - (CONV_K - 1)


SUBLANES = 8


def _sublane_shifts(win):
    n = win.shape[0]
    return [win] + [pltpu.roll(win, n - b, 0) for b in range(1, SUBLANES)]


def _shifted(shifts, offset):
    a, b = divmod(offset, SUBLANES)
    return shifts[b][a * SUBLANES:a * SUBLANES + ROW_TILE]


def _conv_taps(shifts, wdw_ref, lead, reverse):
    acc = jnp.zeros((ROW_TILE, CW), F32)
    for j in range(CONV_K):
        tap = (CONV_K - 1 - j) if reverse else j
        acc = acc + wdw_ref[tap:tap + 1, :] * _shifted(shifts, lead + j)
    return acc


def _conv_fill_glu(cv_ref, cg_ref, xpad, s):
    xpad[pl.ds(0, CONV_HALO), :] = jnp.zeros((CONV_HALO, CW), F32)

    def fill(t, carry):
        t0 = pl.multiple_of(t * ROW_TILE, ROW_TILE)
        rows = pl.ds(t0, ROW_TILE)
        xpad[pl.ds(t0 + CONV_HALO, ROW_TILE), :] = _glu(cv_ref[rows, :], cg_ref[rows, :])
        return carry

    lax.fori_loop(0, s // ROW_TILE, fill, 0)


def _conv_fwd(z, wdw, bdw, lng, lnb, y):
    s = z.shape[0]

    def body(cv_ref, cg_ref, wdw_ref, bdw_ref, lng_ref, lnb_ref, _, y_ref, xpad):
        _conv_fill_glu(cv_ref, cg_ref, xpad, s)

        def tile(t, carry):
            t0 = pl.multiple_of(t * ROW_TILE, ROW_TILE)
            shifts = _sublane_shifts(xpad[pl.ds(t0, ROW_TILE + CONV_HALO), :])
            hc = _conv_taps(shifts, wdw_ref, CONV_LEAD, False) + bdw_ref[...]
            y_ref[pl.ds(t0, ROW_TILE), :] = _ln_silu(hc, lng_ref[...], lnb_ref[...]).astype(BF16)
            return carry

        lax.fori_loop(0, s // ROW_TILE, tile, 0)

    vec = pl.BlockSpec((1, CW), lambda i: (0, 0))
    return pl.pallas_call(
        body, name="conv_fwd", grid=(1,),
        in_specs=[pl.BlockSpec((s, CW), lambda i: (0, (2 * GW + PW) // CW)),
                  pl.BlockSpec((s, CW), lambda i: (0, (2 * GW + PW) // CW + 1)),
                  pl.BlockSpec((CONV_K + 1, CW), lambda i: (0, 0)), vec, vec, vec, ANY],
        out_specs=pl.BlockSpec((s, CW), lambda i: (0, (GW + PW) // CW)),
        out_shape=jax.ShapeDtypeStruct((s, D), BF16), input_output_aliases={6: 0},
        scratch_shapes=[pltpu.VMEM((s + CONV_HALO, CW), F32)], compiler_params=_cparams(),
    )(z, z, wdw, bdw, lng, lnb, y)


def _conv_bwd(z, dy, wdw, bdw, lng, lnb, dz):
    s = z.shape[0]

    def body(cv_ref, cg_ref, dy_ref, wdw_ref, bdw_ref, lng_ref, lnb_ref, _,
             dz_ref, dwdw_ref, dbdw_ref, dlng_ref, dlnb_ref, xpad, dpad, dcg_keep):
        @pl.when(pl.program_id(0) == 0)
        def _():
            compute(cv_ref, cg_ref, dy_ref, wdw_ref, bdw_ref, lng_ref, lnb_ref,
                    dz_ref, dcg_keep, dwdw_ref, dbdw_ref, dlng_ref, dlnb_ref, xpad, dpad)

        @pl.when(pl.program_id(0) == 1)
        def _():
            dz_ref[...] = dcg_keep[...]

    def compute(cv_ref, cg_ref, dy_ref, wdw_ref, bdw_ref, lng_ref, lnb_ref,
                dcv_ref, dcg_ref, dwdw_ref, dbdw_ref, dlng_ref, dlnb_ref, xpad, dpad):
        _conv_fill_glu(cv_ref, cg_ref, xpad, s)
        dpad[pl.ds(s, CONV_HALO), :] = jnp.zeros((CONV_HALO, CW), F32)
        dwdw_ref[...] = jnp.zeros((CONV_K + 1, CW), F32)

        def tile(t, carry):
            db, dg, dbeta = carry
            t0 = pl.multiple_of(t * ROW_TILE, ROW_TILE)
            shifts = _sublane_shifts(xpad[pl.ds(t0, ROW_TILE + CONV_HALO), :])
            hc = _conv_taps(shifts, wdw_ref, CONV_LEAD, False) + bdw_ref[...]
            _, vjp = jax.vjp(_ln_silu, hc, lng_ref[...], lnb_ref[...])
            dhc, dg_t, dbeta_t = vjp(dy_ref[pl.ds(t0, ROW_TILE), :])
            dpad[pl.ds(t0, ROW_TILE), :] = dhc
            for j in range(CONV_K):
                dwdw_ref[j:j + 1, :] += jnp.sum(dhc * _shifted(shifts, CONV_LEAD + j), axis=0, keepdims=True)
            return db + jnp.sum(dhc, axis=0, keepdims=True), dg + dg_t, dbeta + dbeta_t

        zero = jnp.zeros((1, CW), F32)
        db, dg, dbeta = lax.fori_loop(0, s // ROW_TILE, tile, (zero, zero, zero))
        dbdw_ref[...] = db
        dlng_ref[...] = dg
        dlnb_ref[...] = dbeta

        def tile2(t, carry):
            t0 = pl.multiple_of(t * ROW_TILE, ROW_TILE)
            rows = pl.ds(t0, ROW_TILE)
            dglu = _conv_taps(_sublane_shifts(dpad[pl.ds(t0, ROW_TILE + CONV_HALO), :]), wdw_ref, 0, True)
            _, vjp = jax.vjp(_glu, cv_ref[rows, :], cg_ref[rows, :])
            dcv, dcg = vjp(dglu)
            dcv_ref[rows, :] = dcv.astype(BF16)
            dcg_ref[rows, :] = dcg.astype(BF16)
            return carry

        lax.fori_loop(0, s // ROW_TILE, tile2, 0)

    vec = pl.BlockSpec((1, CW), lambda i: (0, 0))
    wspec = pl.BlockSpec((CONV_K + 1, CW), lambda i: (0, 0))
    vshape = jax.ShapeDtypeStruct((1, CW), F32)
    return pl.pallas_call(
        body, name="conv_bwd", grid=(2,),
        in_specs=[pl.BlockSpec((s, CW), lambda i: (0, (2 * GW + PW) // CW)),
                  pl.BlockSpec((s, CW), lambda i: (0, (2 * GW + PW) // CW + 1)),
                  pl.BlockSpec((s, CW), lambda i: (0, (GW + PW) // CW)), wspec, vec, vec, vec, ANY],
        out_specs=[pl.BlockSpec((s, CW), lambda i: (0, (2 * GW + PW) // CW + i)), wspec, vec, vec, vec],
        out_shape=[jax.ShapeDtypeStruct((s, IN_COLS), BF16), jax.ShapeDtypeStruct((CONV_K + 1, CW), F32),
                   vshape, vshape, vshape],
        input_output_aliases={7: 0},
        scratch_shapes=[pltpu.VMEM((s + CONV_HALO, CW), F32), pltpu.VMEM((s + CONV_HALO, CW), F32),
                        pltpu.VMEM((s, CW), BF16)],
        compiler_params=_cparams(),
    )(z, z, dy, wdw, bdw, lng, lnb, dz)


def _softmax_rows(sc):
    e = jnp.exp(sc - jnp.max(sc, axis=-1, keepdims=True))
    return e / jnp.sum(e, axis=-1, keepdims=True)


def _attn_fwd(q, k, v, tq):
    s, m = q.shape[0], k.shape[0]
    tq = min(tq, s)

    def body(q_ref, k_ref, v_ref, o_ref):
        for h in range(XH):
            cols = slice(h * XHD, (h + 1) * XHD)
            p = _softmax_rows(_dot(q_ref[:, cols], k_ref[:, cols], NT) * ATT_SCALE)
            o_ref[:, cols] = _dot(p.astype(BF16), v_ref[:, cols], NN).astype(BF16)

    kv = pl.BlockSpec((m, D), lambda i: (0, 0))
    return pl.pallas_call(
        body, name="attn_fwd", grid=(s // tq,),
        in_specs=[pl.BlockSpec((tq, D), lambda i: (i, 0)), kv, kv],
        out_specs=pl.BlockSpec((tq, D), lambda i: (i, 0)),
        out_shape=jax.ShapeDtypeStruct((s, D), BF16), compiler_params=_cparams(),
    )(q, k, v)


def _attn_bwd(q, k, v, do, tq, after=()):
    s, m = q.shape[0], k.shape[0]
    tq = min(tq, s)

    def body(q_ref, k_ref, v_ref, do_ref, *rest):
        dq_ref, dk_ref, dv_ref = rest[len(after):]

        @pl.when(pl.program_id(0) == 0)
        def _():
            dk_ref[...] = jnp.zeros_like(dk_ref)
            dv_ref[...] = jnp.zeros_like(dv_ref)

        for h in range(XH):
            cols = slice(h * XHD, (h + 1) * XHD)
            qh, kh, vh, doh = q_ref[:, cols], k_ref[:, cols], v_ref[:, cols], do_ref[:, cols]
            p = _softmax_rows(_dot(qh, kh, NT) * ATT_SCALE)
            dp = _dot(doh, vh, NT)
            dv_ref[:, cols] += _dot(p.astype(BF16), doh, TN)
            ds = (p * (dp - jnp.sum(p * dp, axis=-1, keepdims=True)) * ATT_SCALE).astype(BF16)
            dq_ref[:, cols] = _dot(ds, kh, NN).astype(BF16)
            dk_ref[:, cols] += _dot(ds, qh, TN)

    kv = pl.BlockSpec((m, D), lambda i: (0, 0))
    qs = pl.BlockSpec((tq, D), lambda i: (i, 0))
    return pl.pallas_call(
        body, name="attn_bwd", grid=(s // tq,),
        in_specs=[qs, kv, kv, qs] + [ANY] * len(after), out_specs=[qs, kv, kv],
        out_shape=[jax.ShapeDtypeStruct((s, D), BF16), jax.ShapeDtypeStruct((m, D), F32),
                   jax.ShapeDtypeStruct((m, D), F32)],
        compiler_params=_cparams(),
    )(q, k, v, do, *after)


def _loss_head(y, target, tm):
    s = y.shape[0]
    tm = min(tm, s)

    def body(y_ref, t_ref, dy_ref, part_ref):
        err = y_ref[...] - t_ref[...]
        dy_ref[...] = err * (1.0 / D)
        part_ref[...] = jnp.full((1, 8, LANES), 0.5 * jnp.sum(err * err) * (1.0 / D), F32)

    blk = pl.BlockSpec((tm, D), lambda i: (i, 0))
    return pl.pallas_call(
        body, name="loss_head", grid=(s // tm,), in_specs=[blk, blk],
        out_specs=[blk, pl.BlockSpec((1, 8, LANES), lambda i: (i, 0, 0))],
        out_shape=[jax.ShapeDtypeStruct((s, D), F32), jax.ShapeDtypeStruct((s // tm, 8, LANES), F32)],
        compiler_params=_cparams(),
    )(y, target)


def _layer_fwd(x0, mem, w, p, fetch):
    z, hn0 = _rowop_mm("mix_in", "rms", (x0,), p["norm_mix_pre"], w["w_in"], NT, F32)
    y = _gmlp_fwd(z, p["gmlp_v_gain"], p["w_spatial"], p["b_spatial_t"], 1024)
    y = _pool_fwd(z, p["w_pool"], p["s_pool"], y)
    y = _conv_fwd(z, p["w_dw"], p["b_dw"], p["conv_ln_g"], p["conv_ln_b"], y)
    w.update(fetch("out", (y,)))
    x1, h0 = _mm_rowop("mix_out", "rms_res", [(y, w["w_out"], NN)], (x0,), p["norm_mix_post"])
    w.update(fetch("att", (x1,)))
    q, hn1 = _rowop_mm("att_q", "rms", (x1,), p["norm_xattn_pre"], w["w_q"], NN, BF16)
    k, mn = _rowop_mm("att_k", "rms", (mem,), p["norm_mem"], w["w_k"], NN, BF16, after=(x1,))
    v, _ = _rowop_mm("att_v", "rms", (mem,), p["norm_mem"], w["w_v"], NN, BF16, after=(x1,))
    o = _attn_fwd(q, k, v, 1024)
    x2, h1 = _mm_rowop("att_o", "rms_res", [(o, w["w_o"], NN)], (x1,), p["norm_xattn_post"])
    w.update(fetch("up", (x2,)))
    u, hn2 = _rowop_mm("ffn_up", "rms", (x2,), p["norm_ffn_pre"], w["w_up"], NT, F32)
    w.update(fetch("down", (u,)))
    x3, h2 = _mm_rowop("ffn_down", "rms_res", [(u, w["w_down"], NN)], (x2,), p["norm_ffn_post"], relu2=True)
    saved = dict(x0=x0, z=z, hn0=hn0, y=y, h0=h0, x1=x1, q=q, hn1=hn1, k=k, v=v, mn=mn, o=o, h1=h1, x2=x2, u=u,
                 hn2=hn2, h2=h2)
    return x3, saved


def _layer_bwd(dx3, mem, w, p, sv, red):
    gs = {}
    du, dh2, dg = _rowop_mm("ffn_down_bwd", "rms_bwd", (sv["h2"], dx3), p["norm_ffn_post"], w["w_down"], NT, BF16,
                            u=sv["u"], after=red.after())
    gs["norm_ffn_post"] = jnp.sum(dg, axis=0)
    g_down = _mm_tn("ffn_down_dw", sv["u"], dh2, relu2=True)
    red.advance((g_down,))
    red.add("down", ("w_down",), [g_down])
    dx2, dg = _mm_rowop("ffn_up_bwd", "rms_bwd_res", [(du, w["w_up"], NN)], (sv["x2"], dx3), p["norm_ffn_pre"],
                        after=red.after())
    gs["norm_ffn_pre"] = jnp.sum(dg, axis=0)
    red.advance((dx2,))
    g_up = _mm_tn("ffn_up_dw", du, sv["hn2"], after=red.after())
    red.add("up", ("w_up",), [g_up])
    do, dh1, dg = _rowop_mm("att_o_bwd", "rms_bwd", (sv["h1"], dx2), p["norm_xattn_post"], w["w_o"], NT, BF16,
                            after=red.after())
    gs["norm_xattn_post"] = jnp.sum(dg, axis=0)
    g_o = _mm_tn("att_o_dw", sv["o"], dh1)
    red.advance((g_o,))
    dq, dk, dv = _attn_bwd(sv["q"], sv["k"], sv["v"], do, 1024, after=red.after())
    dk, dv = dk.astype(BF16), dv.astype(BF16)
    dx1, dg = _mm_rowop("att_q_bwd", "rms_bwd_res", [(dq, w["w_q"], NT)], (sv["x1"], dx2), p["norm_xattn_pre"],
                        after=red.after())
    gs["norm_xattn_pre"] = jnp.sum(dg, axis=0)
    g_q = _mm_tn("att_q_dw", sv["hn1"], dq)
    g_k = _mm_tn("att_k_dw", sv["mn"], dk)
    g_v = _mm_tn("att_v_dw", sv["mn"], dv)
    (dg,) = _mm_rowop("att_kv_bwd", "rms_bwd_gain", [(dk, w["w_k"], NT), (dv, w["w_v"], NT)], (mem,), p["norm_mem"])
    gs["norm_mem"] = jnp.sum(dg, axis=0)
    red.add("att", ("w_o", "w_q", "w_k", "w_v"), [g_o, g_q, g_k, g_v])
    dy, dh0, dg = _rowop_mm("mix_out_bwd", "rms_bwd", (sv["h0"], dx1), p["norm_mix_post"], w["w_out"], NT, F32,
                            after=red.after())
    gs["norm_mix_post"] = jnp.sum(dg, axis=0)
    g_out = _mm_tn("mix_out_dw", sv["y"], dh0)
    red.advance((g_out,))
    red.add("out", ("w_out",), [g_out])
    z = sv["z"]
    dz, dgv, dws, dbs = _gmlp_bwd(z, dy, p["gmlp_v_gain"], p["w_spatial"], p["b_spatial_t"], 512, after=red.after())
    gs["gmlp_v_gain"] = jnp.sum(dgv, axis=0)
    gs["w_spatial"] = jnp.sum(dws, axis=0)
    gs["b_spatial"] = jnp.sum(dbs[..., 0], axis=0)
    dz, gs["w_pool"], gs["s_pool"] = _pool_bwd(z, dy, p["w_pool"], p["s_pool"], dz)
    dz, dwdw, gs["b_dw"], gs["conv_ln_g"], gs["conv_ln_b"] = _conv_bwd(
        z, dy, p["w_dw"], p["b_dw"], p["conv_ln_g"], p["conv_ln_b"], dz)
    red.advance((dz,))
    g_in = _mm_tn("mix_in_dw", dz, sv["hn0"], after=red.after())
    red.add("in", ("w_in",), [g_in])
    red.small("mixer", _small_grad_arrays(gs, dwdw, norms=False))
    dx0, dg = _mm_rowop("mix_in_bwd", "rms_bwd_res", [(dz, w["w_in"], NN)], (sv["x0"], dx1), p["norm_mix_pre"],
                        after=red.after())
    if red.layer == 0:
        red.advance((dx0,))
    gs["norm_mix_pre"] = jnp.sum(dg, axis=0)
    late = {"norms": jnp.concatenate([gs[n] for n in NORM_NAMES], axis=0)}
    if red.layer == 0:
        late["loss"] = red.extra[0]
    red.small("norms", late)
    return dx0


NORM_NAMES = ("norm_mix_pre", "norm_mix_post", "norm_xattn_pre", "norm_mem", "norm_xattn_post", "norm_ffn_pre",
              "norm_ffn_post")
VEC_NAMES = ("s_pool", "b_dw", "conv_ln_g", "conv_ln_b")
SMALL_ARRAYS = ("norms", "gain_bias", "w_spatial", "w_pool", "vecs", "w_dw")


def _small_grad_arrays(gs, dwdw, norms=True):
    out = {"norms": jnp.concatenate([gs[n] for n in NORM_NAMES], axis=0)} if norms else {}
    out.update({"gain_bias": jnp.concatenate([gs["gmlp_v_gain"], gs["b_spatial"]], axis=0),
                "w_spatial": gs["w_spatial"], "w_pool": gs["w_pool"],
                "vecs": jnp.concatenate([gs[n] for n in VEC_NAMES], axis=0), "w_dw": dwdw})
    return out


def _layer_params(small, l):
    p = {n: small[n][l].reshape(1, -1) for n in ("norm_mix_pre", "norm_mix_post", "s_pool", "b_dw", "conv_ln_g",
                                                   "conv_ln_b", "norm_xattn_pre", "norm_mem", "norm_xattn_post",
                                                   "norm_ffn_pre", "norm_ffn_post")}
    p["gmlp_v_gain"] = small["gmlp_v_gain"][l]
    p["w_spatial"] = small["w_spatial"][l]
    p["b_spatial_t"] = small["b_spatial"][l].T
    p["w_pool"] = small["w_pool"][l]
    p["w_dw"] = jnp.pad(small["w_dw"][l], ((0, 1), (0, 0)))
    return p


def _local_step(x, mem, target, fetch, small, red):
    small = dict(small)
    saved, weights, params = [], [], []
    h = x
    marker = ()
    for l in range(DEPTH):
        w = fetch(l, "in", marker)
        if "taps" in w:
            small["w_dw"] = w.pop("taps")
        p = _layer_params(small, l)
        h, sv = _layer_fwd(h, mem, w, p, functools.partial(fetch, l))
        marker = (h,)
        saved.append(sv)
        weights.append(w)
        params.append(p)
    dh, loss = _loss_head(h, target, 1024)
    red.extra = (loss,)
    for l in reversed(range(DEPTH)):
        red.layer = l
        dh = _layer_bwd(dh, mem, weights[l], params[l], saved[l], red)
    return loss, dh


HBM = pl.BlockSpec(memory_space=pltpu.HBM)


def _position():
    return lax.axis_index("x"), lax.axis_index("y"), lax.axis_index("c")


SEM = pl.BlockSpec(memory_space=pltpu.SEMAPHORE)
EFFECT = pltpu.SideEffectType.DATAFLOW_SIDE_EFFECTING
TOKEN = jax.ShapeDtypeStruct((8, LANES), F32)
TOKEN_SPEC = pl.BlockSpec(memory_space=pltpu.VMEM)


def _landing(shape, dtype):
    return pltpu.with_memory_space_constraint(lax.empty(shape, dtype), pltpu.HBM)


def _hbm_shapes(arrays):
    return [pltpu.HBM(a.shape, a.dtype) for a in arrays]


def _block(ref, r, dev):
    return ref.at[pl.ds((4 * dev[0] + 2 * dev[1] + dev[2]) * r, r), :]


def _split_call(name, body, thru, sems_in, after, sems_out, token):
    n = len(thru)
    out_shape = [pltpu.SemaphoreType.DMA(s) for s in sems_out] + _hbm_shapes(thru) + ([TOKEN] if token else [])
    out_specs = [SEM] * len(sems_out) + [HBM] * n + ([TOKEN_SPEC] if token else [])
    return pl.pallas_call(
        body, name=name, in_specs=[HBM] * n + [SEM] * len(sems_in) + [ANY] * len(after),
        out_specs=out_specs, out_shape=out_shape,
        input_output_aliases={i: len(sems_out) + i for i in range(n)},
        compiler_params=pltpu.CompilerParams(has_side_effects=EFFECT),
    )(*thru, *sems_in, *after)


def _place_own(name, srcs, dev, out_dtype, tr):
    n = len(srcs)
    r, cols = srcs[0][0].shape[-2:]
    tr = r if r < 16 else _row_tile(r, tr)
    nb = r // tr

    def body(dev_ref, *refs):
        for a in range(n):
            refs[n + a][...] = refs[a][...].astype(out_dtype)

    in_specs = [pl.BlockSpec((tr, cols), lambda i, d: (i, 0)) if l is None
                else pl.BlockSpec((None, tr, cols), lambda i, d, l=l: (l, i, 0)) for _, l in srcs]
    return pl.pallas_call(
        body, name=name,
        grid_spec=pltpu.PrefetchScalarGridSpec(
            num_scalar_prefetch=1, grid=(nb,), in_specs=in_specs,
            out_specs=[pl.BlockSpec((tr, cols), lambda i, d: (d[0] * nb + i, 0))] * n),
        out_shape=[jax.ShapeDtypeStruct((N_DEV * r, cols), out_dtype)] * n, compiler_params=_cparams(),
    )(dev, *[a for a, _ in srcs])


def _gather_peers(x, y, c):
    return [(1 - x, y, c), (x, 1 - y, c), (1 - x, 1 - y, c), (x, y, 1 - c)]


def _block_rows(land):
    return land.shape[0] // N_DEV


def _near_peers(x, y, c):
    return [(1 - x, y, c), (x, 1 - y, c), (x, y, 1 - c)]


def _relay_route(x, y, c):
    origin = (x + c * (1 - 2 * x), y + (1 - c) * (1 - 2 * y), c)
    target = (x + (1 - c) * (1 - 2 * x), y + c * (1 - 2 * y), c)
    return origin, target


def _same_block_copy(blk, send_sem, recv_sem, to):
    return pltpu.make_async_remote_copy(src_ref=blk, dst_ref=blk, send_sem=send_sem, recv_sem=recv_sem, device_id=to,
                                        device_id_type=MESH)


def _gather_start(name, lands, after):
    n = len(lands)

    def body(*refs):
        lz = refs[:n]
        send_sems, recv_sems = refs[n + len(after)], refs[n + len(after) + 1]
        token = refs[-1]
        x, y, c = _position()
        for a in range(n):
            own = _block(lz[a], _block_rows(lands[a]), (x, y, c))
            for k, to in enumerate(_near_peers(x, y, c)):
                _same_block_copy(own, send_sems.at[k], recv_sems.at[k], to).start()
        token[...] = jnp.zeros_like(token)

    out = _split_call(name, body, list(lands), [], after, [(3,), (3,)], True)
    return out[0], out[1], out[2:2 + n], out[-1]


def _gather_step(name, near, far, fresh, after):
    groups = [g for g in (near and near[0], far and far[0], fresh) if g]
    counts = [len(near[0]) if near else 0, len(far[0]) if far else 0, len(fresh) if fresh else 0]
    n = sum(counts)
    sems_in = ([near[1]] if near else []) + ([far[1]] if far else [])
    sems_out = ([(2,), (2,), (1,), (1,)] if near else []) + ([(1,), (1,)] if far else []) + ([(3,), (3,)] if fresh else [])

    def body(*refs):
        lz = list(refs[:n])
        ins = list(refs[n:n + len(sems_in)])
        outs = list(refs[n + len(sems_in) + len(after):n + len(sems_in) + len(after) + len(sems_out)])
        token = refs[-1]
        x, y, c = _position()
        me, sibling = (x, y, c), (x, y, 1 - c)
        near_lz, far_lz, fresh_lz = (lz[sum(counts[:i]):sum(counts[:i + 1])] for i in range(3))
        neighbours = _near_peers(x, y, c)[:2]
        origin, target = _relay_route(x, y, c)
        diagonal = (1 - x, 1 - y, c)
        if near:
            recv0 = ins.pop(0)
            fsend, frecv, rsend, rrecv = (outs.pop(0) for _ in range(4))
            for a, land in enumerate(near[0]):
                for j, chip in enumerate(neighbours):
                    _same_block_copy(_block(near_lz[a], _block_rows(land), chip), fsend.at[j], recv0.at[j], me).wait_recv()
        if far:
            rrecv_in = ins.pop(0)
            f2send, f2recv = outs.pop(0), outs.pop(0)
            for a, land in enumerate(far[0]):
                _same_block_copy(_block(far_lz[a], _block_rows(land), diagonal), f2send.at[0], rrecv_in.at[0], me).wait_recv()
            for a, land in enumerate(far[0]):
                _same_block_copy(_block(far_lz[a], _block_rows(land), diagonal), f2send.at[0], f2recv.at[0], sibling).start()
        if near:
            for a, land in enumerate(near[0]):
                r = _block_rows(land)
                _same_block_copy(_block(near_lz[a], r, origin), rsend.at[0], rrecv.at[0], target).start()
                for j, chip in enumerate(neighbours):
                    _same_block_copy(_block(near_lz[a], r, chip), fsend.at[j], frecv.at[j], sibling).start()
        if fresh:
            send_sems, recv_sems = outs.pop(0), outs.pop(0)
            for a, land in enumerate(fresh):
                own = _block(fresh_lz[a], _block_rows(land), me)
                for k, to in enumerate(_near_peers(x, y, c)):
                    _same_block_copy(own, send_sems.at[k], recv_sems.at[k], to).start()
        token[...] = jnp.zeros_like(token)

    out = list(_split_call(name, body, [l for g in groups for l in g], sems_in, after, sems_out, True))
    res = {"token": out.pop()}
    if near:
        res.update(fsend=out.pop(0), frecv=out.pop(0), rsend=out.pop(0), rrecv=out.pop(0))
    if far:
        res.update(f2send=out.pop(0), f2recv=out.pop(0))
    if fresh:
        res.update(send=out.pop(0), recv=out.pop(0))
    res["near"], res["far"], res["fresh"] = (out[sum(counts[:i]):sum(counts[:i + 1])] for i in range(3))
    return res


def _gather_finish(name, lands, send_sems, recv_sems, fsend, frecv, rsend, f2send, f2recv, after):
    n = len(lands)

    def body(*refs):
        lz = refs[:n]
        send0, recv0, fsend_ref, frecv_ref, rsend_ref, f2send_ref, f2recv_ref = refs[n:n + 7]
        x, y, c = _position()
        me = (x, y, c)
        near = _near_peers(x, y, c)[:2]
        origin, _ = _relay_route(x, y, c)
        for a in range(n):
            r = _block_rows(lands[a])
            sib = _block(lz[a], r, (x, y, 1 - c))
            _same_block_copy(sib, send0.at[2], recv0.at[2], me).wait_recv()
            for j, chip in enumerate(near):
                blk = _block(lz[a], r, (chip[0], chip[1], 1 - c))
                _same_block_copy(blk, fsend_ref.at[j], frecv_ref.at[j], me).wait_recv()
            far = _block(lz[a], r, (1 - x, 1 - y, 1 - c))
            _same_block_copy(far, f2send_ref.at[0], f2recv_ref.at[0], me).wait_recv()
            own = _block(lz[a], r, me)
            for k in range(3):
                _same_block_copy(own, send0.at[k], recv0.at[k], me).wait_send()
            for j, chip in enumerate(near):
                _same_block_copy(_block(lz[a], r, chip), fsend_ref.at[j], frecv_ref.at[j], me).wait_send()
            _same_block_copy(_block(lz[a], r, origin), rsend_ref.at[0], recv0.at[0], me).wait_send()
            _same_block_copy(_block(lz[a], r, (1 - x, 1 - y, c)), f2send_ref.at[0], f2recv_ref.at[0], me).wait_send()

    return _split_call(name, body, list(lands), [send_sems, recv_sems, fsend, frecv, rsend, f2send, f2recv], after, [],
                       False)


def _sibling_start(name, grads, after):
    n = len(grads)
    lands = [_landing((4, g.shape[0] // N_DEV, D), g.dtype) for g in grads]

    def body(*refs):
        ins, lz = refs[:n], refs[n:2 * n]
        send_sem, recv_sem = refs[2 * n + len(after)], refs[2 * n + len(after) + 1]
        token = refs[-1]
        x, y, c = _position()
        for a in range(n):
            r = grads[a].shape[0] // N_DEV
            for q in range(4):
                pltpu.make_async_remote_copy(
                    src_ref=ins[a].at[pl.ds((2 * q + 1 - c) * r, r), :], dst_ref=lz[a].at[q], send_sem=send_sem.at[0],
                    recv_sem=recv_sem.at[0], device_id=(x, y, 1 - c), device_id_type=MESH).start()
        token[...] = jnp.zeros_like(token)

    out = _split_call(name, body, list(grads) + lands, [], after, [(1,), (1,)], True)
    return out[0], out[1], out[2:2 + n], out[2 + n:2 + 2 * n], out[-1]


def _sibling_finish(name, grads, lands, send_sem, recv_sem, after):
    n = len(grads)

    def body(*refs):
        ins, lz = refs[:n], refs[n:2 * n]
        send_ref, recv_ref = refs[2 * n], refs[2 * n + 1]
        x, y, c = _position()
        for a in range(n):
            r = grads[a].shape[0] // N_DEV
            for q in range(4):
                cp = pltpu.make_async_remote_copy(
                    src_ref=ins[a].at[pl.ds((2 * q + 1 - c) * r, r), :], dst_ref=lz[a].at[q], send_sem=send_ref.at[0],
                    recv_sem=recv_ref.at[0], device_id=(x, y, c), device_id_type=MESH)
                cp.wait_send()
                cp.wait_recv()

    out = _split_call(name, body, list(grads) + list(lands), [send_sem, recv_sem], after, [], False)
    return out[:n], out[n:2 * n]


def _chip_start(name, parts, after):
    n = len(parts)
    lands = [_landing((3,) + p.shape[1:], p.dtype) for p in parts]

    def body(*refs):
        ins, lz = refs[:n], refs[n:2 * n]
        send_sems, recv_sems = refs[2 * n + len(after)], refs[2 * n + len(after) + 1]
        token = refs[-1]
        x, y, c = _position()
        for a in range(n):
            for j, chip in enumerate(_gather_peers(x, y, c)[:3]):
                pltpu.make_async_remote_copy(
                    src_ref=ins[a].at[2 * chip[0] + chip[1]], dst_ref=lz[a].at[j], send_sem=send_sems.at[j],
                    recv_sem=recv_sems.at[j], device_id=chip, device_id_type=MESH).start()
        token[...] = jnp.zeros_like(token)

    out = _split_call(name, body, list(parts) + lands, [], after, [(3,), (3,)], True)
    return out[0], out[1], out[2:2 + n], out[2 + n:2 + 2 * n], out[-1]


def _chip_finish(name, parts, lands, send_sems, recv_sems, after):
    n = len(parts)

    def body(*refs):
        ins, lz = refs[:n], refs[n:2 * n]
        send_ref, recv_ref = refs[2 * n], refs[2 * n + 1]
        me = _position()
        for a in range(n):
            for j in range(3):
                cp = pltpu.make_async_remote_copy(
                    src_ref=ins[a].at[j], dst_ref=lz[a].at[j], send_sem=send_ref.at[j], recv_sem=recv_ref.at[j],
                    device_id=me, device_id_type=MESH)
                cp.wait_send()
                cp.wait_recv()

    out = _split_call(name, body, list(parts) + list(lands), [send_sems, recv_sems], after, [], False)
    return out[:n], out[n:2 * n]


def _other_devices(x, y, c):
    return [(x + (k >> 2 & 1) * (1 - 2 * x), y + (k >> 1 & 1) * (1 - 2 * y), c + (k & 1) * (1 - 2 * c))
            for k in range(1, N_DEV)]


def _broadcast_start(name, arrays, after):
    n = len(arrays)
    lands = [_landing((N_DEV,) + a.shape, a.dtype) for a in arrays]

    def body(*refs):
        ins, lz = refs[:n], refs[n:2 * n]
        send_sems, recv_sems = refs[2 * n + len(after)], refs[2 * n + len(after) + 1]
        token = refs[-1]
        x, y, c = _position()
        for a in range(n):
            for k, peer in enumerate(_other_devices(x, y, c)):
                pltpu.make_async_remote_copy(
                    src_ref=ins[a], dst_ref=lz[a].at[4 * x + 2 * y + c], send_sem=send_sems.at[k],
                    recv_sem=recv_sems.at[k], device_id=peer, device_id_type=MESH).start()
        token[...] = jnp.zeros_like(token)

    out = _split_call(name, body, list(arrays) + lands, [], after, [(N_DEV - 1,), (N_DEV - 1,)], True)
    return out[0], out[1], out[2:2 + n], out[2 + n:2 + 2 * n], out[-1]


def _broadcast_finish(name, arrays, lands, send_sems, recv_sems, after):
    n = len(arrays)

    def body(*refs):
        ins, lz = refs[:n], refs[n:2 * n]
        send_ref, recv_ref = refs[2 * n], refs[2 * n + 1]
        x, y, c = _position()
        for a in range(n):
            for k, peer in enumerate(_other_devices(x, y, c)):
                cp = pltpu.make_async_remote_copy(
                    src_ref=ins[a], dst_ref=lz[a].at[4 * peer[0] + 2 * peer[1] + peer[2]], send_sem=send_ref.at[k],
                    recv_sem=recv_ref.at[k], device_id=(x, y, c), device_id_type=MESH)
                cp.wait_send()
                cp.wait_recv()

    out = _split_call(name, body, list(arrays) + list(lands), [send_sems, recv_sems], after, [], False)
    return out[:n], out[n:2 * n]


def _row_tile(r, target):
    return max(t for t in range(16, min(r, target) + 1, 16) if r % t == 0)


CHIP_PARTIAL_BYTES = 12 * 1024 * 1024


def _chip_partial(name, grads, gots, c):
    n = len(grads)
    r = grads[0].shape[0] // N_DEV
    tr = _row_tile(r, CHIP_PARTIAL_BYTES // (n * 3 * D * 2))

    def body(c_ref, *refs):
        for a in range(n):
            refs[2 * n + a][...] = (refs[a][...].astype(F32) + refs[n + a][...].astype(F32)).astype(BF16)

    blk = pl.BlockSpec((None, tr, D), lambda q, i, c_ref: (q, i, 0))
    return pl.pallas_call(
        body, name=name,
        grid_spec=pltpu.PrefetchScalarGridSpec(
            num_scalar_prefetch=1, grid=(4, r // tr),
            in_specs=[pl.BlockSpec((None, None, tr, D), lambda q, i, c_ref: (q, c_ref[0], i, 0))] * n + [blk] * n,
            out_specs=[blk] * n),
        out_shape=[jax.ShapeDtypeStruct((4, r, D), BF16)] * n, compiler_params=_cparams(),
    )(c, *[g.reshape(4, 2, r, D) for g in grads], *gots)


class _WeightGather:
    def __init__(self, groups):
        self.groups = list(groups)
        self.index = {key: i for i, (key, _, _) in enumerate(groups)}
        self.state = [None] * len(groups)
        self.token = ()
        for i in range(min(2, len(groups))):
            self._start(i)

    def _tag(self, i):
        return "%s_%d" % self.groups[i][0][::-1]

    def _start(self, i):
        send, recv, lz, tok = _gather_start("gather_start_" + self._tag(i), self.groups[i][2], self.token)
        self.state[i] = dict(send=send, recv=recv, lands=lz)
        self.token = (tok,)

    def _step(self, name, near, far, fresh, marker):
        exists = lambda i: i is not None and i < len(self.groups)
        near, far, fresh = (i if exists(i) else None for i in (near, far, fresh))
        res = _gather_step(
            name, None if near is None else (self.state[near]["lands"], self.state[near]["recv"]),
            None if far is None else (self.state[far]["lands"], self.state[far]["rrecv"]),
            None if fresh is None else self.groups[fresh][2], tuple(marker) + self.token)
        self.token = (res["token"],)
        if near is not None:
            self.state[near].update(lands=res["near"], fsend=res["fsend"], frecv=res["frecv"], rsend=res["rsend"],
                                    rrecv=res["rrecv"])
        if far is not None:
            self.state[far].update(lands=res["far"], f2send=res["f2send"], f2recv=res["f2recv"])
        if fresh is not None:
            self.state[fresh] = dict(send=res["send"], recv=res["recv"], lands=res["fresh"])

    def fetch(self, layer, group, marker):
        k = self.index[(layer, group)]
        if k == 0:
            self._step("gather_step_first", 0, None, None, marker)
        self._step("gather_step_" + self._tag(k), k + 1, k, k + 2, marker)
        st = self.state[k]
        lz = _gather_finish("gather_finish_" + self._tag(k), st["lands"], st["send"], st["recv"], st["fsend"],
                            st["frecv"], st["rsend"], st["f2send"], st["f2recv"], self.token)
        self.state[k] = None
        return dict(zip(self.groups[k][1], lz))


class _GradReduce:
    def __init__(self, core, chip):
        self.core, self.chip = core, chip
        self.layer = None
        self.token = ()
        self.at_sibling, self.at_chips = [], []
        self.extra, self.smalls = (), {}

    def after(self):
        return self.token

    def add(self, group, names, grads):
        tag = "%s_%d" % (group, self.layer)
        send, recv, grads, lands, tok = _sibling_start("grad_sibling_start_" + tag, grads, self.token)
        self.at_sibling.append((tag, [(self.layer, n) for n in names], send, recv, grads, lands))
        self.token = (tok,)

    def advance(self, marker):
        for tag, keys, send, recv, grads, lands in self.at_sibling:
            grads, lands = _sibling_finish("grad_sibling_finish_" + tag, grads, lands, send, recv, marker)
            parts = _chip_partial("chip_partial_" + tag, grads, lands, self.core)
            send, recv, parts, lands, tok = _chip_start("grad_chip_start_" + tag, parts, ())
            self.at_chips.append([tag, keys, send, recv, parts, lands])
            self.token = (tok,)
        self.at_sibling = []

    def small(self, part, arrays):
        keys = list(arrays)
        send, recv, own, slots, tok = _broadcast_start(
            "small_grads_start_%d_%s" % (self.layer, part), [arrays[k] for k in keys], self.token)
        self.smalls.setdefault(self.layer, []).append((part, keys, send, recv, own, slots))
        self.token = (tok,)

    def small_finish(self, layer, marker):
        mine, theirs = {}, {}
        for part, keys, send, recv, own, slots in self.smalls[layer]:
            own, slots = _broadcast_finish("small_grads_finish_%d_%s" % (layer, part), own, slots, send, recv, marker)
            mine.update(zip(keys, own))
            theirs.update(zip(keys, slots))
        return mine, theirs

    def collect(self, key, marker):
        for entry in self.at_chips:
            tag, keys, send, recv, parts, lands = entry
            if key in keys:
                if send is not None:
                    parts, lands = _chip_finish("grad_chip_finish_" + tag, parts, lands, send, recv, marker)
                    entry[2:] = [None, None, parts, lands]
                i = keys.index(key)
                return parts[i], lands[i]
        raise KeyError(key)


def _adamw_math(w, g, m, v):
    m = ADAM_B1 * m + (1.0 - ADAM_B1) * g
    v = ADAM_B2 * v + (1.0 - ADAM_B2) * jnp.square(g)
    m_hat = m / (1.0 - ADAM_B1 ** ADAM_STEP)
    v_hat = v / (1.0 - ADAM_B2 ** ADAM_STEP)
    delta = -ADAM_LR * (m_hat / (jnp.sqrt(v_hat) + ADAM_EPS) + ADAM_WD * w)
    return delta, m, v


def _adamw_small(wts, mom_m, mom_v, own, gathered, loss_own, loss_gathered, dev):
    names = SMALL
    nw = len(names)
    na = len(SMALL_ARRAYS)

    def body(dev_ref, *refs):
        w_refs, m_refs, v_refs = (dict(zip(names, refs[i * nw:(i + 1) * nw])) for i in range(3))
        own_refs = refs[3 * nw:3 * nw + DEPTH * na]
        g_refs = refs[3 * nw + DEPTH * na:3 * nw + 2 * DEPTH * na]
        loss_own_ref, loss_got_ref = refs[3 * nw + 2 * DEPTH * na:3 * nw + 2 * DEPTH * na + 2]
        outs = refs[3 * nw + 2 * DEPTH * na + 2:]
        g_out, d_out, m_out, v_out = (dict(zip(names, outs[i * nw:(i + 1) * nw])) for i in range(4))
        me = dev_ref[0]

        loss = None
        for d in range(N_DEV):
            for b in range(loss_own.shape[0]):
                term = jnp.where(me == d, loss_own_ref[b], loss_got_ref[d, b])
                loss = term if loss is None else loss + term
        outs[4 * nw][...] = loss

        def update(name, at, g):
            g_out[name][at] = g
            d_out[name][at], m_out[name][at], v_out[name][at] = _adamw_math(
                w_refs[name][at], g, m_refs[name][at], v_refs[name][at])

        for l in range(DEPTH):
            mine = dict(zip(SMALL_ARRAYS, own_refs[l * na:(l + 1) * na]))
            got = dict(zip(SMALL_ARRAYS, g_refs[l * na:(l + 1) * na]))

            def total(key, at):
                acc = None
                for d in range(N_DEV):
                    term = jnp.where(me == d, mine[key][at] if at else mine[key][...], got[key][(d,) + at])
                    acc = term if acc is None else acc + term
                return acc

            row = (slice(l, l + 1),)
            for k, name in enumerate(NORM_NAMES):
                update(name, row, total("norms", (slice(k, k + 1),)))
            for k, name in enumerate(VEC_NAMES):
                update(name, row, total("vecs", (slice(k, k + 1),)))
            update("gmlp_v_gain", (l,), total("gain_bias", (slice(0, NH),)))
            update("b_spatial", (l,), total("gain_bias", (slice(NH, 2 * NH),)))
            update("w_spatial", (l,), total("w_spatial", ()))
            update("w_pool", (l,), total("w_pool", ()))
            update("w_dw", (l,), total("w_dw", (slice(0, CONV_K),)))

    args = [src[n] for src in (wts, mom_m, mom_v) for n in names]
    args += [src[l][k] for src in (own, gathered) for l in range(DEPTH) for k in SMALL_ARRAYS]
    args += [loss_own, loss_gathered]
    outs = pl.pallas_call(
        body, name="adamw_small",
        in_specs=[pl.BlockSpec(memory_space=pltpu.SMEM)] + [pl.BlockSpec(memory_space=pltpu.VMEM)] * len(args),
        out_shape=[jax.ShapeDtypeStruct(wts[n].shape, F32) for _ in range(4) for n in names]
        + [jax.ShapeDtypeStruct((8, LANES), F32)],
        compiler_params=_cparams(),
    )(dev, *args)
    return tuple(dict(zip(names, outs[i * nw:(i + 1) * nw])) for i in range(4)) + (outs[4 * nw],)


def _adamw_layers(name, w, reduced, m, v, chip, tr, transposed=False, after=()):
    nl, r, cdim = w.shape
    tr = _row_tile(r, tr)
    nb = r // tr

    def body(q_ref, w_ref, p0_ref, g0_ref, p1_ref, g1_ref, m_ref, v_ref, *rest):
        g_ref, d_ref, nm_ref, nv_ref = rest[len(after):]

        def total(p_ref, got_ref):
            acc = p_ref[...].astype(F32)
            for j in range(3):
                acc = acc + got_ref[j].astype(F32)
            return acc

        g = jnp.where(pl.program_id(0) == 0, total(p0_ref, g0_ref), total(p1_ref, g1_ref))
        if transposed:
            g = g.T
        g_ref[...] = g
        d_ref[...], nm_ref[...], nv_ref[...] = _adamw_math(w_ref[...], g, m_ref[...], v_ref[...])

    blk = pl.BlockSpec((None, tr, cdim), lambda l, i, q: (l, i, 0))
    first = lambda l, i: i * (1 - l) + (nb - 1) * l
    second = lambda l, i: i * l
    if transposed:
        gshape = (cdim, tr)
        at = lambda lead, i: (lead, 0, i)
    else:
        gshape = (tr, cdim)
        at = lambda lead, i: (lead, i, 0)
    specs = [blk,
             pl.BlockSpec((None,) + gshape, lambda l, i, q: at(q[0], first(l, i))),
             pl.BlockSpec((3,) + gshape, lambda l, i, q: at(0, first(l, i))),
             pl.BlockSpec((None,) + gshape, lambda l, i, q: at(q[0], second(l, i))),
             pl.BlockSpec((3,) + gshape, lambda l, i, q: at(0, second(l, i))), blk, blk] + [ANY] * len(after)
    shape = jax.ShapeDtypeStruct((nl, r, cdim), F32)
    return pl.pallas_call(
        body, name=name,
        grid_spec=pltpu.PrefetchScalarGridSpec(num_scalar_prefetch=1, grid=(nl, nb), in_specs=specs, out_specs=[blk] * 4),
        out_shape=[shape] * 4, compiler_params=_cparams(),
    )(chip, w, *reduced[0], *reduced[1], m, v, *after)


def _to_rows(name, a):
    return jnp.swapaxes(a, 1, 2) if name == "w_in" else a


def _place_own_transposed(name, srcs, dev, out_dtype, tc):
    n = len(srcs)
    kdim, cdim = srcs[0][0].shape[-2:]

    def body(dev_ref, *refs):
        for a in range(n):
            refs[n + a][...] = refs[a][...].T.astype(out_dtype)

    return pl.pallas_call(
        body, name=name,
        grid_spec=pltpu.PrefetchScalarGridSpec(
            num_scalar_prefetch=1, grid=(kdim // tc,),
            in_specs=[pl.BlockSpec((None, tc, cdim), lambda i, d, l=l: (l, i, 0)) for _, l in srcs],
            out_specs=[pl.BlockSpec((cdim, tc), lambda i, d: (d[0], i))] * n),
        out_shape=[jax.ShapeDtypeStruct((N_DEV * cdim, kdim), out_dtype)] * n, compiler_params=_cparams(),
    )(dev, *[a for a, _ in srcs])


def _pack(arrays, rows):
    flat = jnp.concatenate([a.reshape(-1) for a in arrays])
    return jnp.pad(flat, (0, rows * D - flat.shape[0])).reshape(rows, D)


def _rows_for(shapes, mult=8):
    total = 0
    for shp in shapes:
        size = 1
        for dim in shp:
            size *= dim
        total += size
    return -(-total // (mult * D)) * mult


def kernel(x, mem, norm_mix_pre, norm_mix_post, w_in, w_out, gmlp_v_gain, w_spatial, b_spatial, w_pool, s_pool, w_dw, b_dw, conv_ln_g, conv_ln_b, norm_xattn_pre, norm_mem, norm_xattn_post, w_q, w_k, w_v, w_o, norm_ffn_pre, norm_ffn_post, w_up, w_down, loss_target, m_norm_mix_pre, m_norm_mix_post, m_w_in, m_w_out, m_gmlp_v_gain, m_w_spatial, m_b_spatial, m_w_pool, m_s_pool, m_w_dw, m_b_dw, m_conv_ln_g, m_conv_ln_b, m_norm_xattn_pre, m_norm_mem, m_norm_xattn_post, m_w_q, m_w_k, m_w_v, m_w_o, m_norm_ffn_pre, m_norm_ffn_post, m_w_up, m_w_down, v_norm_mix_pre, v_norm_mix_post, v_w_in, v_w_out, v_gmlp_v_gain, v_w_spatial, v_b_spatial, v_w_pool, v_s_pool, v_w_dw, v_b_dw, v_conv_ln_g, v_conv_ln_b, v_norm_xattn_pre, v_norm_mem, v_norm_xattn_post, v_w_q, v_w_k, v_w_v, v_w_o, v_norm_ffn_pre, v_norm_ffn_post, v_w_up, v_w_down):
    args = dict(locals())
    wts = {n: args[n] for n in WEIGHTS}
    mom_m = {n: args["m_" + n] for n in WEIGHTS}
    mom_v = {n: args["v_" + n] for n in WEIGHTS}
    xi, yi, ci = _position()
    me = 4 * xi + 2 * yi + ci

    dev = jnp.reshape(me, (1,)).astype(jnp.int32)
    lands = {}
    for call, names, tr in (("place_in", ("w_in",), 256), ("place_att", ("w_out", "w_q", "w_k", "w_v", "w_o"), 64),
                            ("place_up", ("w_up",), 256), ("place_down", ("w_down",), 256)):
        srcs = [(_to_rows(n, wts[n]), l) for l in range(DEPTH) for n in names]
        placed = (_place_own_transposed if names == ("w_up",) else _place_own)(call, srcs, dev, BF16, tr)
        lands.update(zip([(l, n) for l in range(DEPTH) for n in names], placed))
    (lands[(0, "taps")],) = _place_own("place_taps", [(_pack([w_dw], _rows_for([w_dw.shape])), None)], dev, F32, 8)
    groups = []
    for l in range(DEPTH):
        for group, names in GATHER_GROUPS:
            if (l, group) == (0, "in"):
                names = names + ("taps",)
            groups.append(((l, group), names, [lands[(l, n)] for n in names]))
    gather = _WeightGather(groups)

    def fetch(layer, group, marker):
        w = gather.fetch(layer, group, marker)
        if "taps" in w:
            blocks = w["taps"].reshape(N_DEV, -1)[:, :w_dw.size].reshape((N_DEV,) + w_dw.shape)
            w["taps"] = jnp.moveaxis(blocks, 0, 2).reshape(DEPTH, CONV_K, CW)
        return w

    reduce = _GradReduce(jnp.reshape(ci, (1,)).astype(jnp.int32), jnp.reshape(2 * xi + yi, (1,)).astype(jnp.int32))
    small = {n: wts[n] for n in SMALL if n != "w_dw"}
    _, dx = _local_step(x[0], mem[0], loss_target[0], fetch, small, reduce)
    reduce.advance((dx,))

    grad_w, delta, new_m, new_v = {}, {}, {}, {}
    marker = (dx,) + tuple(reduce.after())
    for n in UPDATE_ORDER:
        reduced = [reduce.collect((l, n), marker) for l in range(DEPTH)]
        outs = _adamw_layers("adamw_" + n, _to_rows(n, wts[n]), reduced, _to_rows(n, mom_m[n]), _to_rows(n, mom_v[n]),
                             reduce.chip, 256, transposed=n == "w_up", after=marker)
        grad_w[n], delta[n], new_m[n], new_v[n] = (_to_rows(n, o) for o in outs)
        marker = (outs[1],)

    own, slots = [None] * DEPTH, [None] * DEPTH
    for l in reversed(range(DEPTH)):
        own[l], slots[l] = reduce.small_finish(l, marker)
        if l == 0:
            loss_own, loss_slots = own[l].pop("loss"), slots[l].pop("loss")
    shard_cols = CW // N_DEV
    for l in range(DEPTH):
        own[l]["w_dw"] = lax.dynamic_slice_in_dim(own[l]["w_dw"], me * shard_cols, shard_cols, axis=1)
        slots[l]["w_dw"] = lax.dynamic_slice_in_dim(slots[l]["w_dw"], me * shard_cols, shard_cols, axis=2)
    *small_out, loss_tile = _adamw_small(wts, mom_m, mom_v, own, slots, loss_own, loss_slots, dev)
    for dst, src in zip((grad_w, delta, new_m, new_v), small_out):
        dst.update(src)

    return (loss_tile[0, 0], dx[None], *[grad_w[n] for n in WEIGHTS], *[delta[n] for n in WEIGHTS],
            *[new_m[n] for n in WEIGHTS], *[new_v[n] for n in WEIGHTS])
```

```python
import functools

import jax
import jax.numpy as jnp
from jax import lax
from jax.experimental import pallas as pl
from jax.experimental.pallas import tpu as pltpu

F32 = jnp.float32
BF16 = jnp.bfloat16

D = 2048
GW = 1024
PW = 512
CW = 512
HD = 128
NH = 8
NG = 4
POOL_WINDOWS = (2, 4, 8, 16)
CONV_K = 31
IN_COLS = 2 * GW + PW + 2 * CW
XH = 4
XHD = D // XH
ATT_SCALE = XHD ** -0.5
RMS_EPS = 1e-6
LN_EPS = 1e-5
DEPTH = 2
N_DEV = 8

ADAM_LR = 0.001
ADAM_B1 = 0.9
ADAM_B2 = 0.999
ADAM_EPS = 1e-08
ADAM_WD = 0.01
ADAM_STEP = 10

LANES = 128
CONV_HALO = 32
POOL_HALO = 16
ROW_TILE = 128
VMEM_LIMIT = 60 * 1024 * 1024

MESH = pl.DeviceIdType.MESH
NT = (((1,), (1,)), ((), ()))
NN = (((1,), (0,)), ((), ()))
TN = (((0,), (0,)), ((), ()))

UPDATE_ORDER = ("w_down", "w_up", "w_o", "w_q", "w_k", "w_v", "w_out", "w_in")
GATHER_GROUPS = (("in", ("w_in",)), ("out", ("w_out",)), ("att", ("w_q", "w_k", "w_v", "w_o")), ("up", ("w_up",)),
                 ("down", ("w_down",)))
SMALL = ("norm_mix_pre", "norm_mix_post", "gmlp_v_gain", "w_spatial", "b_spatial", "w_pool", "s_pool",
         "w_dw", "b_dw", "conv_ln_g", "conv_ln_b", "norm_xattn_pre", "norm_mem", "norm_xattn_post",
         "norm_ffn_pre", "norm_ffn_post")
WEIGHTS = ("norm_mix_pre", "norm_mix_post", "w_in", "w_out", "gmlp_v_gain", "w_spatial", "b_spatial", "w_pool",
           "s_pool", "w_dw", "b_dw", "conv_ln_g", "conv_ln_b", "norm_xattn_pre", "norm_mem", "norm_xattn_post",
           "w_q", "w_k", "w_v", "w_o", "norm_ffn_pre", "norm_ffn_post", "w_up", "w_down")


def _cparams():
    return pltpu.CompilerParams(vmem_limit_bytes=VMEM_LIMIT)


def _dot(a, b, dims):
    return lax.dot_general(a, b, dims, preferred_element_type=F32)


def _rms(x, g):
    y = x * lax.rsqrt(jnp.mean(x * x, axis=-1, keepdims=True) + RMS_EPS)
    return y * g


def _rms_bwd(x, g, dy):
    r = lax.rsqrt(jnp.mean(x * x, axis=-1, keepdims=True) + RMS_EPS)
    xh = x * r
    t = dy * g
    dx = r * (t - xh * jnp.mean(t * xh, axis=-1, keepdims=True))
    return dx, jnp.sum(dy * xh, axis=0, keepdims=True)


def _gelu(x):
    cdf = 0.5 * (1.0 + jnp.tanh(0.7978845608028654 * (x + 0.044715 * (x * x * x))))
    return x * cdf


def _layer_norm(x, g, b=None):
    mu = jnp.mean(x, axis=-1, keepdims=True)
    xc = x - mu
    var = jnp.mean(xc * xc, axis=-1, keepdims=True)
    y = xc * lax.rsqrt(var + LN_EPS) * g
    return y if b is None else y + b


def _sigmoid(x):
    return 1.0 / (1.0 + jnp.exp(-x))


def _gmlp_rows(zu, zv, gv):
    return _gelu(zu), _layer_norm(_gelu(zv), gv)


def _glu(cv, cg):
    return cv * _sigmoid(cg)


def _ln_silu(h, g, b):
    y = _layer_norm(h, g, b)
    return y * _sigmoid(y)


ANY = pl.BlockSpec(memory_space=pl.ANY)


ROWS_TILE = 256
COLS_TILE = 512
DW_TILE = 512
RESIDENT_K = 2048
RESIDENT_ROWS = 512
STREAM_K_TILE = 2048
STREAM_ROWS = 512


def _k_tiles(kdim):
    if kdim <= RESIDENT_K:
        return RESIDENT_ROWS, kdim
    if kdim <= IN_COLS:
        return ROWS_TILE, kdim
    return STREAM_ROWS, max(t for t in range(LANES, STREAM_K_TILE + 1, LANES) if kdim % t == 0)


def _rowop_mm(name, kind, rows, g, w, dims, out_dtype, u=None, after=()):
    s = rows[0].shape[0]
    n = w.shape[0] if dims == NT else w.shape[1]
    resident = n <= IN_COLS and u is None
    tm, tn = min(RESIDENT_ROWS if resident and n <= RESIDENT_K else ROWS_TILE, s), min(COLS_TILE, n)
    ni, nj = s // tm, n // tn
    bwd = kind == "rms_bwd"
    out_shape = [jax.ShapeDtypeStruct((s, n), out_dtype), jax.ShapeDtypeStruct((s, D), BF16)]
    if bwd:
        out_shape.append(jax.ShapeDtypeStruct((ni, 1, D), F32))

    if resident:
        def row_body(*refs):
            refs = list(refs)
            row_refs = [refs.pop(0) for _ in rows]
            g_ref, w_ref = refs.pop(0), refs.pop(0)
            del refs[:len(after)]
            if bwd:
                a, dg = _rms_bwd(row_refs[0][...], g_ref[...], row_refs[1][...])
                refs[2][0] = dg
            else:
                a = _rms(row_refs[0][...], g_ref[...])
            a = a.astype(BF16)
            refs[1][...] = a
            refs[0][...] = _dot(a, w_ref[...], dims).astype(out_dtype)

        blk = pl.BlockSpec((tm, D), lambda i: (i, 0))
        return pl.pallas_call(
            row_body, name=name, grid=(ni,),
            in_specs=[blk] * len(rows) + [pl.BlockSpec((1, D), lambda i: (0, 0)),
                                          pl.BlockSpec(w.shape, lambda i: (0, 0), pipeline_mode=pl.Buffered(1))]
            + [ANY] * len(after),
            out_specs=[pl.BlockSpec((tm, n), lambda i: (i, 0)), blk]
            + ([pl.BlockSpec((1, 1, D), lambda i: (i, 0, 0))] if bwd else []),
            out_shape=out_shape, compiler_params=_cparams(),
        )(*rows, g, w, *after)

    def body(*refs):
        refs = list(refs)
        row_refs = [refs.pop(0) for _ in rows]
        g_ref, w_ref = refs.pop(0), refs.pop(0)
        u_ref = refs.pop(0) if u is not None else None
        del refs[:len(after)]
        out_ref, a_ref = refs.pop(0), refs.pop(0)
        dg_ref = refs.pop(0) if bwd else None
        a_all = refs.pop(0)
        t = pl.program_id(0)

        @pl.when(t < ni)
        def _():
            if bwd:
                a, dg = _rms_bwd(row_refs[0][...], g_ref[...], row_refs[1][...])
                dg_ref[0] = dg
            else:
                a = _rms(row_refs[0][...], g_ref[...])
            a_ref[...] = a.astype(BF16)
            a_all[pl.ds(pl.multiple_of(t * tm, tm), tm), :] = a.astype(BF16)

        @pl.when(t >= ni)
        def _():
            acc = _dot(a_all[...], w_ref[...], dims)
            if u_ref is not None:
                acc = acc * (2.0 * jnp.maximum(u_ref[...], 0.0))
            out_ref[...] = acc.astype(out_dtype)

    rows_at = lambda t: jnp.minimum(t, ni - 1)
    cols_at = lambda t: jnp.maximum(t - ni, 0)
    row_spec = pl.BlockSpec((tm, D), lambda t: (rows_at(t), 0))
    w_spec = (pl.BlockSpec((tn, D), lambda t: (cols_at(t), 0)) if dims == NT
              else pl.BlockSpec((D, tn), lambda t: (0, cols_at(t))))
    tile = pl.BlockSpec((s, tn), lambda t: (0, cols_at(t)))
    in_specs = [row_spec] * len(rows) + [pl.BlockSpec((1, D), lambda t: (0, 0)), w_spec]
    in_specs += ([tile] if u is not None else []) + [ANY] * len(after)
    out_specs = [tile, row_spec]
    if bwd:
        out_specs.append(pl.BlockSpec((1, 1, D), lambda t: (rows_at(t), 0, 0)))
    return pl.pallas_call(
        body, name=name, grid=(ni + nj,), in_specs=in_specs, out_specs=out_specs, out_shape=out_shape,
        scratch_shapes=[pltpu.VMEM((s, D), BF16)], compiler_params=_cparams(),
    )(*rows, g, w, *([u] if u is not None else []), *after)


def _mm_rowop(name, kind, pairs, rows, g, relu2=False, after=()):
    s, kdim = pairs[0][0].shape
    tm, tk = _k_tiles(kdim)
    tm = min(tm, s)
    ni, nk = s // tm, kdim // tk
    npair = len(pairs)
    by_hand = nk > 1
    own_acc = by_hand and kind != "rms_res"
    n_out = 1 if kind == "rms_bwd_gain" else 2

    def body(*refs):
        refs = list(refs)
        a_refs = [refs.pop(0) for _ in range(npair)]
        w_refs = [refs.pop(0) for _ in range(npair)]
        row_refs = [refs.pop(0) for _ in rows]
        g_ref = refs.pop(0)
        del refs[:len(after)]
        outs, scratch = refs[:n_out], refs[n_out:]
        i, k = pl.program_id(0), pl.program_id(1)
        if by_hand:
            acc = scratch[0] if own_acc else outs[1]
            row_bufs, sems = scratch[-len(rows) - 1:-1], scratch[-1]
            copies = [pltpu.make_async_copy(r.at[pl.ds(pl.multiple_of(i * tm, tm), tm)], b, sems.at[j])
                      for j, (r, b) in enumerate(zip(row_refs, row_bufs))]
            row_refs = row_bufs

        def product():
            total = None
            for a_ref, w_ref, (_, _, dims) in zip(a_refs, w_refs, pairs):
                a = a_ref[...]
                if relu2:
                    a = jnp.square(jnp.maximum(a, 0.0))
                term = _dot(a.astype(BF16), w_ref[...], dims)
                total = term if total is None else total + term
            return total

        def finish(h):
            if kind == "rms_res":
                outs[0][...] = row_refs[0][...] + _rms(h, g_ref[...])
                outs[1][...] = h
            else:
                dx, dg = _rms_bwd(row_refs[0][...], g_ref[...], h)
                if kind == "rms_bwd_res":
                    outs[0][...] = row_refs[1][...] + dx
                    outs[1][0] = dg
                else:
                    outs[0][0] = dg

        if nk == 1:
            finish(product())
            return

        @pl.when(k == 0)
        def _():
            acc[...] = jnp.zeros_like(acc)
            for cp in copies:
                cp.start()

        acc[...] += product()

        @pl.when(k == nk - 1)
        def _():
            for cp in copies:
                cp.wait()
            finish(acc[...])

    row_spec = pl.BlockSpec((tm, D), lambda i, k: (i, 0))
    dg_shape = jax.ShapeDtypeStruct((ni, 1, D), F32)
    dg_spec = pl.BlockSpec((1, 1, D), lambda i, k: (i, 0, 0))
    in_specs = [pl.BlockSpec((tm, tk), lambda i, k: (i, k))] * npair
    for _, _, dims in pairs:
        mode = dict(pipeline_mode=pl.Buffered(1)) if nk == 1 else {}
        in_specs.append(pl.BlockSpec((tk, D), lambda i, k: (k, 0), **mode) if dims == NN
                        else pl.BlockSpec((D, tk), lambda i, k: (0, k), **mode))
    in_specs += [ANY if by_hand else row_spec] * len(rows) + [pl.BlockSpec((1, D), lambda i, k: (0, 0))] + [ANY] * len(after)
    scratch_shapes = [pltpu.VMEM((tm, D), F32)] * (own_acc + len(rows) * by_hand)
    if by_hand:
        scratch_shapes.append(pltpu.SemaphoreType.DMA((len(rows),)))
    if kind == "rms_res":
        out_shape = [jax.ShapeDtypeStruct((s, D), F32)] * 2
        out_specs = [row_spec, row_spec]
    elif kind == "rms_bwd_res":
        out_shape = [jax.ShapeDtypeStruct((s, D), F32), dg_shape]
        out_specs = [row_spec, dg_spec]
    else:
        out_shape = [dg_shape]
        out_specs = [dg_spec]
    return pl.pallas_call(
        body, name=name, grid=(ni, nk), in_specs=in_specs, out_specs=out_specs, out_shape=out_shape,
        scratch_shapes=scratch_shapes, compiler_params=_cparams(),
    )(*[p[0] for p in pairs], *[p[1] for p in pairs], *rows, g, *after)


def _mm_tn(name, a, gmat, relu2=False, after=()):
    s, m = a.shape
    tm = min(DW_TILE, m)
    ni = m // tm

    def body(a_ref, g_ref, *rest):
        av = a_ref[...]
        if relu2:
            av = jnp.square(jnp.maximum(av, 0.0))
        rest[len(after)][...] = _dot(av.astype(BF16), g_ref[...], TN).astype(BF16)

    return pl.pallas_call(
        body, name=name, grid=(ni,),
        in_specs=[pl.BlockSpec((s, tm), lambda i: (0, i)), pl.BlockSpec((s, D), lambda i: (0, 0))] + [ANY] * len(after),
        out_specs=pl.BlockSpec((tm, D), lambda i: (i, 0)),
        out_shape=jax.ShapeDtypeStruct((m, D), BF16), compiler_params=_cparams(),
    )(a, gmat, *after)


def _tril():
    r = lax.broadcasted_iota(jnp.int32, (HD, HD), 0)
    c = lax.broadcasted_iota(jnp.int32, (HD, HD), 1)
    return (c <= r).astype(F32)


def _gmlp_fwd(z, gv, ws, bst, tb):
    s = z.shape[0]
    tb = min(tb, s)

    def body(zu_ref, zv_ref, gv_ref, ws_ref, bst_ref, y_ref):
        tril = _tril()
        for h in range(NH):
            cols = slice(h * HD, (h + 1) * HD)
            u, vln = _gmlp_rows(zu_ref[:, cols], zv_ref[:, cols], gv_ref[h:h + 1, :])
            wm = (ws_ref[h] * tril).astype(BF16)
            vb = vln.astype(BF16)
            for c in range(tb // HD):
                rws = slice(c * HD, (c + 1) * HD)
                mixed = _dot(wm, vb[rws], NN) + bst_ref[:, h:h + 1]
                y_ref[rws, cols] = (u[rws] * mixed).astype(BF16)

    return pl.pallas_call(
        body, name="gmlp_fwd", grid=(s // tb,),
        in_specs=[pl.BlockSpec((tb, GW), lambda i: (i, 0)), pl.BlockSpec((tb, GW), lambda i: (i, 1)),
                  pl.BlockSpec((NH, HD), lambda i: (0, 0)), pl.BlockSpec((NH, HD, HD), lambda i: (0, 0, 0)),
                  pl.BlockSpec((HD, NH), lambda i: (0, 0))],
        out_specs=pl.BlockSpec((tb, GW), lambda i: (i, 0)),
        out_shape=jax.ShapeDtypeStruct((s, D), BF16), compiler_params=_cparams(),
    )(z, z, gv, ws, bst)


def _gmlp_bwd(z, dy, gv, ws, bst, tb, after=()):
    s = z.shape[0]
    tb = min(tb, s)
    nb = s // tb

    def body(zu_ref, zv_ref, dy_ref, gv_ref, ws_ref, bst_ref, *rest):
        dz_ref, dgv_ref, dws_ref, db_ref = rest[len(after):]
        tril = _tril()
        for h in range(NH):
            cols = slice(h * HD, (h + 1) * HD)
            (u, vln), vjp = jax.vjp(_gmlp_rows, zu_ref[:, cols], zv_ref[:, cols], gv_ref[h:h + 1, :])
            wmf = ws_ref[h] * tril
            wm = wmf.astype(BF16)
            wmt = wmf.T.astype(BF16)
            vb = vln.astype(BF16)
            dws = jnp.zeros((HD, HD), F32)
            db = jnp.zeros((HD, 1), F32)
            du_parts, dvln_parts = [], []
            for c in range(tb // HD):
                rws = slice(c * HD, (c + 1) * HD)
                mixed = _dot(wm, vb[rws], NN) + bst_ref[:, h:h + 1]
                dyc = dy_ref[rws, cols]
                du_parts.append(dyc * mixed)
                dmixed = dyc * u[rws]
                dmb = dmixed.astype(BF16)
                dws = dws + _dot(dmb, vb[rws], NT)
                db = db + jnp.sum(dmixed, axis=1, keepdims=True)
                dvln_parts.append(_dot(wmt, dmb, NN))
            du = jnp.concatenate(du_parts, axis=0)
            dvln = jnp.concatenate(dvln_parts, axis=0)
            dzu, dzv, dgv = vjp((du, dvln))
            dz_ref[:, cols] = dzu.astype(BF16)
            dz_ref[:, slice(GW + h * HD, GW + (h + 1) * HD)] = dzv.astype(BF16)
            dgv_ref[0, h:h + 1, :] = dgv
            dws_ref[0, h] = dws * tril
            db_ref[0, h] = jnp.broadcast_to(db, (HD, LANES))

    blk = pl.BlockSpec((tb, GW), lambda i: (i, 0))
    return pl.pallas_call(
        body, name="gmlp_bwd", grid=(nb,),
        in_specs=[blk, pl.BlockSpec((tb, GW), lambda i: (i, 1)), blk,
                  pl.BlockSpec((NH, HD), lambda i: (0, 0)), pl.BlockSpec((NH, HD, HD), lambda i: (0, 0, 0)),
                  pl.BlockSpec((HD, NH), lambda i: (0, 0))] + [ANY] * len(after),
        out_specs=[pl.BlockSpec((tb, 2 * GW), lambda i: (i, 0)), pl.BlockSpec((1, NH, HD), lambda i: (i, 0, 0)),
                   pl.BlockSpec((1, NH, HD, HD), lambda i: (i, 0, 0, 0)),
                   pl.BlockSpec((1, NH, HD, LANES), lambda i: (i, 0, 0, 0))],
        out_shape=[jax.ShapeDtypeStruct((s, IN_COLS), BF16),
                   jax.ShapeDtypeStruct((nb, NH, HD), F32), jax.ShapeDtypeStruct((nb, NH, HD, HD), F32),
                   jax.ShapeDtypeStruct((nb, NH, HD, LANES), F32)],
        compiler_params=_cparams(),
    )(z, z, dy, gv, ws, bst, *after)


POOL_TILE = 1024


def _pool_count(t0, window):
    pos = (t0 + lax.broadcasted_iota(jnp.int32, (POOL_TILE, LANES), 0)).astype(F32)
    return jnp.minimum(pos + 1.0, float(window))


def _window_sum(win, levels, back):
    n = win.shape[0]
    for lv in range(levels):
        step = 1 << lv
        win = win + pltpu.roll(win, n - step if back else step, 0)
    return win


def _pool_pooled(ppad_ref, t0, g):
    win = ppad_ref[pl.ds(t0, POOL_TILE + POOL_HALO), :]
    wsum = _window_sum(win, g + 1, False)[POOL_HALO:]
    return wsum / _pool_count(t0, POOL_WINDOWS[g]) - win[POOL_HALO:]


def _pool_fwd(z, wp, sp, y):
    s = z.shape[0]
    nt = s // POOL_TILE

    def body(p_ref, wp_ref, sp_ref, _, y_ref, ppad):
        for g in range(NG):
            cols = slice(g * LANES, (g + 1) * LANES)
            ppad[pl.ds(0, POOL_HALO), :] = jnp.zeros((POOL_HALO, LANES), F32)
            ppad[pl.ds(POOL_HALO, s), :] = p_ref[:, cols]
            wpb = wp_ref[g].astype(BF16)
            scale = sp_ref[:, cols]

            def tile(t, carry):
                t0 = pl.multiple_of(t * POOL_TILE, POOL_TILE)
                pooled = _pool_pooled(ppad, t0, g)
                y_ref[pl.ds(t0, POOL_TILE), cols] = (_dot(pooled.astype(BF16), wpb, NN) * scale).astype(BF16)
                return carry

            lax.fori_loop(0, nt, tile, 0)

    return pl.pallas_call(
        body, name="pool_fwd", grid=(1,),
        in_specs=[pl.BlockSpec((s, PW), lambda i: (0, 2 * GW // PW)),
                  pl.BlockSpec((NG, LANES, LANES), lambda i: (0, 0, 0)), pl.BlockSpec((1, PW), lambda i: (0, 0)), ANY],
        out_specs=pl.BlockSpec((s, PW), lambda i: (0, GW // PW)),
        out_shape=jax.ShapeDtypeStruct((s, D), BF16), input_output_aliases={3: 0},
        scratch_shapes=[pltpu.VMEM((s + POOL_HALO, LANES), F32)], compiler_params=_cparams(),
    )(z, wp, sp, y)


def _pool_bwd(z, dy, wp, sp, dz):
    s = z.shape[0]
    nt = s // POOL_TILE

    def body(p_ref, dy_ref, wp_ref, sp_ref, _, dp_ref, dwp_ref, dsp_ref, ppad, rpad, dpool):
        for g in range(NG):
            cols = slice(g * LANES, (g + 1) * LANES)
            ppad[pl.ds(0, POOL_HALO), :] = jnp.zeros((POOL_HALO, LANES), F32)
            ppad[pl.ds(POOL_HALO, s), :] = p_ref[:, cols]
            rpad[pl.ds(s, POOL_HALO), :] = jnp.zeros((POOL_HALO, LANES), F32)
            wpb = wp_ref[g].astype(BF16)
            scale = sp_ref[:, cols]

            def tile(t, carry):
                dwp, dsp = carry
                t0 = pl.multiple_of(t * POOL_TILE, POOL_TILE)
                pooled = _pool_pooled(ppad, t0, g)
                pb = pooled.astype(BF16)
                dyt = dy_ref[pl.ds(t0, POOL_TILE), cols]
                dsp = dsp + jnp.sum(dyt * _dot(pb, wpb, NN), axis=0, keepdims=True)
                dmm = (dyt * scale).astype(BF16)
                dwp = dwp + _dot(pb, dmm, TN)
                dpooled = _dot(dmm, wpb, NT)
                rpad[pl.ds(t0, POOL_TILE), :] = dpooled / _pool_count(t0, POOL_WINDOWS[g])
                dpool[pl.ds(t0, POOL_TILE), :] = dpooled
                return dwp, dsp

            dwp, dsp = lax.fori_loop(0, nt, tile, (jnp.zeros((LANES, LANES), F32), jnp.zeros((1, LANES), F32)))
            dwp_ref[g] = dwp
            dsp_ref[:, cols] = dsp

            def tile2(t, carry):
                t0 = pl.multiple_of(t * POOL_TILE, POOL_TILE)
                win = rpad[pl.ds(t0, POOL_TILE + POOL_HALO), :]
                back = _window_sum(win, g + 1, True)[:POOL_TILE]
                rows = pl.ds(t0, POOL_TILE)
                dp_ref[rows, cols] = (back - dpool[rows, :]).astype(BF16)
                return carry

            lax.fori_loop(0, nt, tile2, 0)

    return pl.pallas_call(
        body, name="pool_bwd", grid=(1,),
        in_specs=[pl.BlockSpec((s, PW), lambda i: (0, 2 * GW // PW)), pl.BlockSpec((s, PW), lambda i: (0, GW // PW)),
                  pl.BlockSpec((NG, LANES, LANES), lambda i: (0, 0, 0)), pl.BlockSpec((1, PW), lambda i: (0, 0)), ANY],
        out_specs=[pl.BlockSpec((s, PW), lambda i: (0, 2 * GW // PW)),
                   pl.BlockSpec((NG, LANES, LANES), lambda i: (0, 0, 0)), pl.BlockSpec((1, PW), lambda i: (0, 0))],
        out_shape=[jax.ShapeDtypeStruct((s, IN_COLS), BF16), jax.ShapeDtypeStruct((NG, LANES, LANES), F32),
                   jax.ShapeDtypeStruct((1, PW), F32)],
        input_output_aliases={4: 0},
        scratch_shapes=[pltpu.VMEM((s + POOL_HALO, LANES), F32), pltpu.VMEM((s + POOL_HALO, LANES), F32),
                        pltpu.VMEM((s, LANES), F32)],
        compiler_params=_cparams(),
    )(z, dy, wp, sp, dz)


CONV_LEAD = CONV_HALO - (CONV_K - 1)


SUBLANES = 8


def _sublane_shifts(win):
    n = win.shape[0]
    return [win] + [pltpu.roll(win, n - b, 0) for b in range(1, SUBLANES)]


def _shifted(shifts, offset):
    a, b = divmod(offset, SUBLANES)
    return shifts[b][a * SUBLANES:a * SUBLANES + ROW_TILE]


def _conv_taps(shifts, wdw_ref, lead, reverse):
    acc = jnp.zeros((ROW_TILE, CW), F32)
    for j in range(CONV_K):
        tap = (CONV_K - 1 - j) if reverse else j
        acc = acc + wdw_ref[tap:tap + 1, :] * _shifted(shifts, lead + j)
    return acc


def _conv_fill_glu(cv_ref, cg_ref, xpad, s):
    xpad[pl.ds(0, CONV_HALO), :] = jnp.zeros((CONV_HALO, CW), F32)

    def fill(t, carry):
        t0 = pl.multiple_of(t * ROW_TILE, ROW_TILE)
        rows = pl.ds(t0, ROW_TILE)
        xpad[pl.ds(t0 + CONV_HALO, ROW_TILE), :] = _glu(cv_ref[rows, :], cg_ref[rows, :])
        return carry

    lax.fori_loop(0, s // ROW_TILE, fill, 0)


def _conv_fwd(z, wdw, bdw, lng, lnb, y):
    s = z.shape[0]

    def body(cv_ref, cg_ref, wdw_ref, bdw_ref, lng_ref, lnb_ref, _, y_ref, xpad):
        _conv_fill_glu(cv_ref, cg_ref, xpad, s)

        def tile(t, carry):
            t0 = pl.multiple_of(t * ROW_TILE, ROW_TILE)
            shifts = _sublane_shifts(xpad[pl.ds(t0, ROW_TILE + CONV_HALO), :])
            hc = _conv_taps(shifts, wdw_ref, CONV_LEAD, False) + bdw_ref[...]
            y_ref[pl.ds(t0, ROW_TILE), :] = _ln_silu(hc, lng_ref[...], lnb_ref[...]).astype(BF16)
            return carry

        lax.fori_loop(0, s // ROW_TILE, tile, 0)

    vec = pl.BlockSpec((1, CW), lambda i: (0, 0))
    return pl.pallas_call(
        body, name="conv_fwd", grid=(1,),
        in_specs=[pl.BlockSpec((s, CW), lambda i: (0, (2 * GW + PW) // CW)),
                  pl.BlockSpec((s, CW), lambda i: (0, (2 * GW + PW) // CW + 1)),
                  pl.BlockSpec((CONV_K + 1, CW), lambda i: (0, 0)), vec, vec, vec, ANY],
        out_specs=pl.BlockSpec((s, CW), lambda i: (0, (GW + PW) // CW)),
        out_shape=jax.ShapeDtypeStruct((s, D), BF16), input_output_aliases={6: 0},
        scratch_shapes=[pltpu.VMEM((s + CONV_HALO, CW), F32)], compiler_params=_cparams(),
    )(z, z, wdw, bdw, lng, lnb, y)


def _conv_bwd(z, dy, wdw, bdw, lng, lnb, dz):
    s = z.shape[0]

    def body(cv_ref, cg_ref, dy_ref, wdw_ref, bdw_ref, lng_ref, lnb_ref, _,
             dz_ref, dwdw_ref, dbdw_ref, dlng_ref, dlnb_ref, xpad, dpad, dcg_keep):
        @pl.when(pl.program_id(0) == 0)
        def _():
            compute(cv_ref, cg_ref, dy_ref, wdw_ref, bdw_ref, lng_ref, lnb_ref,
                    dz_ref, dcg_keep, dwdw_ref, dbdw_ref, dlng_ref, dlnb_ref, xpad, dpad)

        @pl.when(pl.program_id(0) == 1)
        def _():
            dz_ref[...] = dcg_keep[...]

    def compute(cv_ref, cg_ref, dy_ref, wdw_ref, bdw_ref, lng_ref, lnb_ref,
                dcv_ref, dcg_ref, dwdw_ref, dbdw_ref, dlng_ref, dlnb_ref, xpad, dpad):
        _conv_fill_glu(cv_ref, cg_ref, xpad, s)
        dpad[pl.ds(s, CONV_HALO), :] = jnp.zeros((CONV_HALO, CW), F32)
        dwdw_ref[...] = jnp.zeros((CONV_K + 1, CW), F32)

        def tile(t, carry):
            db, dg, dbeta = carry
            t0 = pl.multiple_of(t * ROW_TILE, ROW_TILE)
            shifts = _sublane_shifts(xpad[pl.ds(t0, ROW_TILE + CONV_HALO), :])
            hc = _conv_taps(shifts, wdw_ref, CONV_LEAD, False) + bdw_ref[...]
            _, vjp = jax.vjp(_ln_silu, hc, lng_ref[...], lnb_ref[...])
            dhc, dg_t, dbeta_t = vjp(dy_ref[pl.ds(t0, ROW_TILE), :])
            dpad[pl.ds(t0, ROW_TILE), :] = dhc
            for j in range(CONV_K):
                dwdw_ref[j:j + 1, :] += jnp.sum(dhc * _shifted(shifts, CONV_LEAD + j), axis=0, keepdims=True)
            return db + jnp.sum(dhc, axis=0, keepdims=True), dg + dg_t, dbeta + dbeta_t

        zero = jnp.zeros((1, CW), F32)
        db, dg, dbeta = lax.fori_loop(0, s // ROW_TILE, tile, (zero, zero, zero))
        dbdw_ref[...] = db
        dlng_ref[...] = dg
        dlnb_ref[...] = dbeta

        def tile2(t, carry):
            t0 = pl.multiple_of(t * ROW_TILE, ROW_TILE)
            rows = pl.ds(t0, ROW_TILE)
            dglu = _conv_taps(_sublane_shifts(dpad[pl.ds(t0, ROW_TILE + CONV_HALO), :]), wdw_ref, 0, True)
            _, vjp = jax.vjp(_glu, cv_ref[rows, :], cg_ref[rows, :])
            dcv, dcg = vjp(dglu)
            dcv_ref[rows, :] = dcv.astype(BF16)
            dcg_ref[rows, :] = dcg.astype(BF16)
            return carry

        lax.fori_loop(0, s // ROW_TILE, tile2, 0)

    vec = pl.BlockSpec((1, CW), lambda i: (0, 0))
    wspec = pl.BlockSpec((CONV_K + 1, CW), lambda i: (0, 0))
    vshape = jax.ShapeDtypeStruct((1, CW), F32)
    return pl.pallas_call(
        body, name="conv_bwd", grid=(2,),
        in_specs=[pl.BlockSpec((s, CW), lambda i: (0, (2 * GW + PW) // CW)),
                  pl.BlockSpec((s, CW), lambda i: (0, (2 * GW + PW) // CW + 1)),
                  pl.BlockSpec((s, CW), lambda i: (0, (GW + PW) // CW)), wspec, vec, vec, vec, ANY],
        out_specs=[pl.BlockSpec((s, CW), lambda i: (0, (2 * GW + PW) // CW + i)), wspec, vec, vec, vec],
        out_shape=[jax.ShapeDtypeStruct((s, IN_COLS), BF16), jax.ShapeDtypeStruct((CONV_K + 1, CW), F32),
                   vshape, vshape, vshape],
        input_output_aliases={7: 0},
        scratch_shapes=[pltpu.VMEM((s + CONV_HALO, CW), F32), pltpu.VMEM((s + CONV_HALO, CW), F32),
                        pltpu.VMEM((s, CW), BF16)],
        compiler_params=_cparams(),
    )(z, z, dy, wdw, bdw, lng, lnb, dz)


def _softmax_rows(sc):
    e = jnp.exp(sc - jnp.max(sc, axis=-1, keepdims=True))
    return e / jnp.sum(e, axis=-1, keepdims=True)


def _attn_fwd(q, k, v, tq):
    s, m = q.shape[0], k.shape[0]
    tq = min(tq, s)

    def body(q_ref, k_ref, v_ref, o_ref):
        for h in range(XH):
            cols = slice(h * XHD, (h + 1) * XHD)
            p = _softmax_rows(_dot(q_ref[:, cols], k_ref[:, cols], NT) * ATT_SCALE)
            o_ref[:, cols] = _dot(p.astype(BF16), v_ref[:, cols], NN).astype(BF16)

    kv = pl.BlockSpec((m, D), lambda i: (0, 0))
    return pl.pallas_call(
        body, name="attn_fwd", grid=(s // tq,),
        in_specs=[pl.BlockSpec((tq, D), lambda i: (i, 0)), kv, kv],
        out_specs=pl.BlockSpec((tq, D), lambda i: (i, 0)),
        out_shape=jax.ShapeDtypeStruct((s, D), BF16), compiler_params=_cparams(),
    )(q, k, v)


def _attn_bwd(q, k, v, do, tq, after=()):
    s, m = q.shape[0], k.shape[0]
    tq = min(tq, s)

    def body(q_ref, k_ref, v_ref, do_ref, *rest):
        dq_ref, dk_ref, dv_ref = rest[len(after):]

        @pl.when(pl.program_id(0) == 0)
        def _():
            dk_ref[...] = jnp.zeros_like(dk_ref)
            dv_ref[...] = jnp.zeros_like(dv_ref)

        for h in range(XH):
            cols = slice(h * XHD, (h + 1) * XHD)
            qh, kh, vh, doh = q_ref[:, cols], k_ref[:, cols], v_ref[:, cols], do_ref[:, cols]
            p = _softmax_rows(_dot(qh, kh, NT) * ATT_SCALE)
            dp = _dot(doh, vh, NT)
            dv_ref[:, cols] += _dot(p.astype(BF16), doh, TN)
            ds = (p * (dp - jnp.sum(p * dp, axis=-1, keepdims=True)) * ATT_SCALE).astype(BF16)
            dq_ref[:, cols] = _dot(ds, kh, NN).astype(BF16)
            dk_ref[:, cols] += _dot(ds, qh, TN)

    kv = pl.BlockSpec((m, D), lambda i: (0, 0))
    qs = pl.BlockSpec((tq, D), lambda i: (i, 0))
    return pl.pallas_call(
        body, name="attn_bwd", grid=(s // tq,),
        in_specs=[qs, kv, kv, qs] + [ANY] * len(after), out_specs=[qs, kv, kv],
        out_shape=[jax.ShapeDtypeStruct((s, D), BF16), jax.ShapeDtypeStruct((m, D), F32),
                   jax.ShapeDtypeStruct((m, D), F32)],
        compiler_params=_cparams(),
    )(q, k, v, do, *after)


def _loss_head(y, target, tm):
    s = y.shape[0]
    tm = min(tm, s)

    def body(y_ref, t_ref, dy_ref, part_ref):
        err = y_ref[...] - t_ref[...]
        dy_ref[...] = err * (1.0 / D)
        part_ref[...] = jnp.full((1, 8, LANES), 0.5 * jnp.sum(err * err) * (1.0 / D), F32)

    blk = pl.BlockSpec((tm, D), lambda i: (i, 0))
    return pl.pallas_call(
        body, name="loss_head", grid=(s // tm,), in_specs=[blk, blk],
        out_specs=[blk, pl.BlockSpec((1, 8, LANES), lambda i: (i, 0, 0))],
        out_shape=[jax.ShapeDtypeStruct((s, D), F32), jax.ShapeDtypeStruct((s // tm, 8, LANES), F32)],
        compiler_params=_cparams(),
    )(y, target)


def _layer_fwd(x0, mem, w, p, fetch):
    z, hn0 = _rowop_mm("mix_in", "rms", (x0,), p["norm_mix_pre"], w["w_in"], NT, F32)
    y = _gmlp_fwd(z, p["gmlp_v_gain"], p["w_spatial"], p["b_spatial_t"], 1024)
    y = _pool_fwd(z, p["w_pool"], p["s_pool"], y)
    y = _conv_fwd(z, p["w_dw"], p["b_dw"], p["conv_ln_g"], p["conv_ln_b"], y)
    w.update(fetch("out", (y,)))
    x1, h0 = _mm_rowop("mix_out", "rms_res", [(y, w["w_out"], NN)], (x0,), p["norm_mix_post"])
    w.update(fetch("att", (x1,)))
    q, hn1 = _rowop_mm("att_q", "rms", (x1,), p["norm_xattn_pre"], w["w_q"], NN, BF16)
    k, mn = _rowop_mm("att_k", "rms", (mem,), p["norm_mem"], w["w_k"], NN, BF16, after=(x1,))
    v, _ = _rowop_mm("att_v", "rms", (mem,), p["norm_mem"], w["w_v"], NN, BF16, after=(x1,))
    o = _attn_fwd(q, k, v, 1024)
    x2, h1 = _mm_rowop("att_o", "rms_res", [(o, w["w_o"], NN)], (x1,), p["norm_xattn_post"])
    w.update(fetch("up", (x2,)))
    u, hn2 = _rowop_mm("ffn_up", "rms", (x2,), p["norm_ffn_pre"], w["w_up"], NT, F32)
    w.update(fetch("down", (u,)))
    x3, h2 = _mm_rowop("ffn_down", "rms_res", [(u, w["w_down"], NN)], (x2,), p["norm_ffn_post"], relu2=True)
    saved = dict(x0=x0, z=z, hn0=hn0, y=y, h0=h0, x1=x1, q=q, hn1=hn1, k=k, v=v, mn=mn, o=o, h1=h1, x2=x2, u=u,
                 hn2=hn2, h2=h2)
    return x3, saved


def _layer_bwd(dx3, mem, w, p, sv, red):
    gs = {}
    du, dh2, dg = _rowop_mm("ffn_down_bwd", "rms_bwd", (sv["h2"], dx3), p["norm_ffn_post"], w["w_down"], NT, BF16,
                            u=sv["u"], after=red.after())
    gs["norm_ffn_post"] = jnp.sum(dg, axis=0)
    g_down = _mm_tn("ffn_down_dw", sv["u"], dh2, relu2=True)
    red.advance((g_down,))
    red.add("down", ("w_down",), [g_down])
    dx2, dg = _mm_rowop("ffn_up_bwd", "rms_bwd_res", [(du, w["w_up"], NN)], (sv["x2"], dx3), p["norm_ffn_pre"],
                        after=red.after())
    gs["norm_ffn_pre"] = jnp.sum(dg, axis=0)
    red.advance((dx2,))
    g_up = _mm_tn("ffn_up_dw", du, sv["hn2"], after=red.after())
    red.add("up", ("w_up",), [g_up])
    do, dh1, dg = _rowop_mm("att_o_bwd", "rms_bwd", (sv["h1"], dx2), p["norm_xattn_post"], w["w_o"], NT, BF16,
                            after=red.after())
    gs["norm_xattn_post"] = jnp.sum(dg, axis=0)
    g_o = _mm_tn("att_o_dw", sv["o"], dh1)
    red.advance((g_o,))
    dq, dk, dv = _attn_bwd(sv["q"], sv["k"], sv["v"], do, 1024, after=red.after())
    dk, dv = dk.astype(BF16), dv.astype(BF16)
    dx1, dg = _mm_rowop("att_q_bwd", "rms_bwd_res", [(dq, w["w_q"], NT)], (sv["x1"], dx2), p["norm_xattn_pre"],
                        after=red.after())
    gs["norm_xattn_pre"] = jnp.sum(dg, axis=0)
    g_q = _mm_tn("att_q_dw", sv["hn1"], dq)
    g_k = _mm_tn("att_k_dw", sv["mn"], dk)
    g_v = _mm_tn("att_v_dw", sv["mn"], dv)
    (dg,) = _mm_rowop("att_kv_bwd", "rms_bwd_gain", [(dk, w["w_k"], NT), (dv, w["w_v"], NT)], (mem,), p["norm_mem"])
    gs["norm_mem"] = jnp.sum(dg, axis=0)
    red.add("att", ("w_o", "w_q", "w_k", "w_v"), [g_o, g_q, g_k, g_v])
    dy, dh0, dg = _rowop_mm("mix_out_bwd", "rms_bwd", (sv["h0"], dx1), p["norm_mix_post"], w["w_out"], NT, F32,
                            after=red.after())
    gs["norm_mix_post"] = jnp.sum(dg, axis=0)
    g_out = _mm_tn("mix_out_dw", sv["y"], dh0)
    red.advance((g_out,))
    red.add("out", ("w_out",), [g_out])
    z = sv["z"]
    dz, dgv, dws, dbs = _gmlp_bwd(z, dy, p["gmlp_v_gain"], p["w_spatial"], p["b_spatial_t"], 512, after=red.after())
    gs["gmlp_v_gain"] = jnp.sum(dgv, axis=0)
    gs["w_spatial"] = jnp.sum(dws, axis=0)
    gs["b_spatial"] = jnp.sum(dbs[..., 0], axis=0)
    dz, gs["w_pool"], gs["s_pool"] = _pool_bwd(z, dy, p["w_pool"], p["s_pool"], dz)
    dz, dwdw, gs["b_dw"], gs["conv_ln_g"], gs["conv_ln_b"] = _conv_bwd(
        z, dy, p["w_dw"], p["b_dw"], p["conv_ln_g"], p["conv_ln_b"], dz)
    red.advance((dz,))
    g_in = _mm_tn("mix_in_dw", dz, sv["hn0"], after=red.after())
    red.add("in", ("w_in",), [g_in])
    red.small("mixer", _small_grad_arrays(gs, dwdw, norms=False))
    dx0, dg = _mm_rowop("mix_in_bwd", "rms_bwd_res", [(dz, w["w_in"], NN)], (sv["x0"], dx1), p["norm_mix_pre"],
                        after=red.after())
    if red.layer == 0:
        red.advance((dx0,))
    gs["norm_mix_pre"] = jnp.sum(dg, axis=0)
    late = {"norms": jnp.concatenate([gs[n] for n in NORM_NAMES], axis=0)}
    if red.layer == 0:
        late["loss"] = red.extra[0]
    red.small("norms", late)
    return dx0


NORM_NAMES = ("norm_mix_pre", "norm_mix_post", "norm_xattn_pre", "norm_mem", "norm_xattn_post", "norm_ffn_pre",
              "norm_ffn_post")
VEC_NAMES = ("s_pool", "b_dw", "conv_ln_g", "conv_ln_b")
SMALL_ARRAYS = ("norms", "gain_bias", "w_spatial", "w_pool", "vecs", "w_dw")


def _small_grad_arrays(gs, dwdw, norms=True):
    out = {"norms": jnp.concatenate([gs[n] for n in NORM_NAMES], axis=0)} if norms else {}
    out.update({"gain_bias": jnp.concatenate([gs["gmlp_v_gain"], gs["b_spatial"]], axis=0),
                "w_spatial": gs["w_spatial"], "w_pool": gs["w_pool"],
                "vecs": jnp.concatenate([gs[n] for n in VEC_NAMES], axis=0), "w_dw": dwdw})
    return out


def _layer_params(small, l):
    p = {n: small[n][l].reshape(1, -1) for n in ("norm_mix_pre", "norm_mix_post", "s_pool", "b_dw", "conv_ln_g",
                                                   "conv_ln_b", "norm_xattn_pre", "norm_mem", "norm_xattn_post",
                                                   "norm_ffn_pre", "norm_ffn_post")}
    p["gmlp_v_gain"] = small["gmlp_v_gain"][l]
    p["w_spatial"] = small["w_spatial"][l]
    p["b_spatial_t"] = small["b_spatial"][l].T
    p["w_pool"] = small["w_pool"][l]
    p["w_dw"] = jnp.pad(small["w_dw"][l], ((0, 1), (0, 0)))
    return p


def _local_step(x, mem, target, fetch, small, red):
    small = dict(small)
    saved, weights, params = [], [], []
    h = x
    marker = ()
    for l in range(DEPTH):
        w = fetch(l, "in", marker)
        if "taps" in w:
            small["w_dw"] = w.pop("taps")
        p = _layer_params(small, l)
        h, sv = _layer_fwd(h, mem, w, p, functools.partial(fetch, l))
        marker = (h,)
        saved.append(sv)
        weights.append(w)
        params.append(p)
    dh, loss = _loss_head(h, target, 1024)
    red.extra = (loss,)
    for l in reversed(range(DEPTH)):
        red.layer = l
        dh = _layer_bwd(dh, mem, weights[l], params[l], saved[l], red)
    return loss, dh


HBM = pl.BlockSpec(memory_space=pltpu.HBM)


def _position():
    return lax.axis_index("x"), lax.axis_index("y"), lax.axis_index("c")


SEM = pl.BlockSpec(memory_space=pltpu.SEMAPHORE)
EFFECT = pltpu.SideEffectType.DATAFLOW_SIDE_EFFECTING
TOKEN = jax.ShapeDtypeStruct((8, LANES), F32)
TOKEN_SPEC = pl.BlockSpec(memory_space=pltpu.VMEM)


def _landing(shape, dtype):
    return pltpu.with_memory_space_constraint(lax.empty(shape, dtype), pltpu.HBM)


def _hbm_shapes(arrays):
    return [pltpu.HBM(a.shape, a.dtype) for a in arrays]


def _block(ref, r, dev):
    return ref.at[pl.ds((4 * dev[0] + 2 * dev[1] + dev[2]) * r, r), :]


def _split_call(name, body, thru, sems_in, after, sems_out, token):
    n = len(thru)
    out_shape = [pltpu.SemaphoreType.DMA(s) for s in sems_out] + _hbm_shapes(thru) + ([TOKEN] if token else [])
    out_specs = [SEM] * len(sems_out) + [HBM] * n + ([TOKEN_SPEC] if token else [])
    return pl.pallas_call(
        body, name=name, in_specs=[HBM] * n + [SEM] * len(sems_in) + [ANY] * len(after),
        out_specs=out_specs, out_shape=out_shape,
        input_output_aliases={i: len(sems_out) + i for i in range(n)},
        compiler_params=pltpu.CompilerParams(has_side_effects=EFFECT),
    )(*thru, *sems_in, *after)


def _place_own(name, srcs, dev, out_dtype, tr):
    n = len(srcs)
    r, cols = srcs[0][0].shape[-2:]
    tr = r if r < 16 else _row_tile(r, tr)
    nb = r // tr

    def body(dev_ref, *refs):
        for a in range(n):
            refs[n + a][...] = refs[a][...].astype(out_dtype)

    in_specs = [pl.BlockSpec((tr, cols), lambda i, d: (i, 0)) if l is None
                else pl.BlockSpec((None, tr, cols), lambda i, d, l=l: (l, i, 0)) for _, l in srcs]
    return pl.pallas_call(
        body, name=name,
        grid_spec=pltpu.PrefetchScalarGridSpec(
            num_scalar_prefetch=1, grid=(nb,), in_specs=in_specs,
            out_specs=[pl.BlockSpec((tr, cols), lambda i, d: (d[0] * nb + i, 0))] * n),
        out_shape=[jax.ShapeDtypeStruct((N_DEV * r, cols), out_dtype)] * n, compiler_params=_cparams(),
    )(dev, *[a for a, _ in srcs])


def _gather_peers(x, y, c):
    return [(1 - x, y, c), (x, 1 - y, c), (1 - x, 1 - y, c), (x, y, 1 - c)]


def _block_rows(land):
    return land.shape[0] // N_DEV


def _near_peers(x, y, c):
    return [(1 - x, y, c), (x, 1 - y, c), (x, y, 1 - c)]


def _relay_route(x, y, c):
    origin = (x + c * (1 - 2 * x), y + (1 - c) * (1 - 2 * y), c)
    target = (x + (1 - c) * (1 - 2 * x), y + c * (1 - 2 * y), c)
    return origin, target


def _same_block_copy(blk, send_sem, recv_sem, to):
    return pltpu.make_async_remote_copy(src_ref=blk, dst_ref=blk, send_sem=send_sem, recv_sem=recv_sem, device_id=to,
                                        device_id_type=MESH)


def _gather_start(name, lands, after):
    n = len(lands)

    def body(*refs):
        lz = refs[:n]
        send_sems, recv_sems = refs[n + len(after)], refs[n + len(after) + 1]
        token = refs[-1]
        x, y, c = _position()
        for a in range(n):
            own = _block(lz[a], _block_rows(lands[a]), (x, y, c))
            for k, to in enumerate(_near_peers(x, y, c)):
                _same_block_copy(own, send_sems.at[k], recv_sems.at[k], to).start()
        token[...] = jnp.zeros_like(token)

    out = _split_call(name, body, list(lands), [], after, [(3,), (3,)], True)
    return out[0], out[1], out[2:2 + n], out[-1]


def _gather_step(name, near, far, fresh, after):
    groups = [g for g in (near and near[0], far and far[0], fresh) if g]
    counts = [len(near[0]) if near else 0, len(far[0]) if far else 0, len(fresh) if fresh else 0]
    n = sum(counts)
    sems_in = ([near[1]] if near else []) + ([far[1]] if far else [])
    sems_out = ([(2,), (2,), (1,), (1,)] if near else []) + ([(1,), (1,)] if far else []) + ([(3,), (3,)] if fresh else [])

    def body(*refs):
        lz = list(refs[:n])
        ins = list(refs[n:n + len(sems_in)])
        outs = list(refs[n + len(sems_in) + len(after):n + len(sems_in) + len(after) + len(sems_out)])
        token = refs[-1]
        x, y, c = _position()
        me, sibling = (x, y, c), (x, y, 1 - c)
        near_lz, far_lz, fresh_lz = (lz[sum(counts[:i]):sum(counts[:i + 1])] for i in range(3))
        neighbours = _near_peers(x, y, c)[:2]
        origin, target = _relay_route(x, y, c)
        diagonal = (1 - x, 1 - y, c)
        if near:
            recv0 = ins.pop(0)
            fsend, frecv, rsend, rrecv = (outs.pop(0) for _ in range(4))
            for a, land in enumerate(near[0]):
                for j, chip in enumerate(neighbours):
                    _same_block_copy(_block(near_lz[a], _block_rows(land), chip), fsend.at[j], recv0.at[j], me).wait_recv()
        if far:
            rrecv_in = ins.pop(0)
            f2send, f2recv = outs.pop(0), outs.pop(0)
            for a, land in enumerate(far[0]):
                _same_block_copy(_block(far_lz[a], _block_rows(land), diagonal), f2send.at[0], rrecv_in.at[0], me).wait_recv()
            for a, land in enumerate(far[0]):
                _same_block_copy(_block(far_lz[a], _block_rows(land), diagonal), f2send.at[0], f2recv.at[0], sibling).start()
        if near:
            for a, land in enumerate(near[0]):
                r = _block_rows(land)
                _same_block_copy(_block(near_lz[a], r, origin), rsend.at[0], rrecv.at[0], target).start()
                for j, chip in enumerate(neighbours):
                    _same_block_copy(_block(near_lz[a], r, chip), fsend.at[j], frecv.at[j], sibling).start()
        if fresh:
            send_sems, recv_sems = outs.pop(0), outs.pop(0)
            for a, land in enumerate(fresh):
                own = _block(fresh_lz[a], _block_rows(land), me)
                for k, to in enumerate(_near_peers(x, y, c)):
                    _same_block_copy(own, send_sems.at[k], recv_sems.at[k], to).start()
        token[...] = jnp.zeros_like(token)

    out = list(_split_call(name, body, [l for g in groups for l in g], sems_in, after, sems_out, True))
    res = {"token": out.pop()}
    if near:
        res.update(fsend=out.pop(0), frecv=out.pop(0), rsend=out.pop(0), rrecv=out.pop(0))
    if far:
        res.update(f2send=out.pop(0), f2recv=out.pop(0))
    if fresh:
        res.update(send=out.pop(0), recv=out.pop(0))
    res["near"], res["far"], res["fresh"] = (out[sum(counts[:i]):sum(counts[:i + 1])] for i in range(3))
    return res


def _gather_finish(name, lands, send_sems, recv_sems, fsend, frecv, rsend, f2send, f2recv, after):
    n = len(lands)

    def body(*refs):
        lz = refs[:n]
        send0, recv0, fsend_ref, frecv_ref, rsend_ref, f2send_ref, f2recv_ref = refs[n:n + 7]
        x, y, c = _position()
        me = (x, y, c)
        near = _near_peers(x, y, c)[:2]
        origin, _ = _relay_route(x, y, c)
        for a in range(n):
            r = _block_rows(lands[a])
            sib = _block(lz[a], r, (x, y, 1 - c))
            _same_block_copy(sib, send0.at[2], recv0.at[2], me).wait_recv()
            for j, chip in enumerate(near):
                blk = _block(lz[a], r, (chip[0], chip[1], 1 - c))
                _same_block_copy(blk, fsend_ref.at[j], frecv_ref.at[j], me).wait_recv()
            far = _block(lz[a], r, (1 - x, 1 - y, 1 - c))
            _same_block_copy(far, f2send_ref.at[0], f2recv_ref.at[0], me).wait_recv()
            own = _block(lz[a], r, me)
            for k in range(3):
                _same_block_copy(own, send0.at[k], recv0.at[k], me).wait_send()
            for j, chip in enumerate(near):
                _same_block_copy(_block(lz[a], r, chip), fsend_ref.at[j], frecv_ref.at[j], me).wait_send()
            _same_block_copy(_block(lz[a], r, origin), rsend_ref.at[0], recv0.at[0], me).wait_send()
            _same_block_copy(_block(lz[a], r, (1 - x, 1 - y, c)), f2send_ref.at[0], f2recv_ref.at[0], me).wait_send()

    return _split_call(name, body, list(lands), [send_sems, recv_sems, fsend, frecv, rsend, f2send, f2recv], after, [],
                       False)


def _sibling_start(name, grads, after):
    n = len(grads)
    lands = [_landing((4, g.shape[0] // N_DEV, D), g.dtype) for g in grads]

    def body(*refs):
        ins, lz = refs[:n], refs[n:2 * n]
        send_sem, recv_sem = refs[2 * n + len(after)], refs[2 * n + len(after) + 1]
        token = refs[-1]
        x, y, c = _position()
        for a in range(n):
            r = grads[a].shape[0] // N_DEV
            for q in range(4):
                pltpu.make_async_remote_copy(
                    src_ref=ins[a].at[pl.ds((2 * q + 1 - c) * r, r), :], dst_ref=lz[a].at[q], send_sem=send_sem.at[0],
                    recv_sem=recv_sem.at[0], device_id=(x, y, 1 - c), device_id_type=MESH).start()
        token[...] = jnp.zeros_like(token)

    out = _split_call(name, body, list(grads) + lands, [], after, [(1,), (1,)], True)
    return out[0], out[1], out[2:2 + n], out[2 + n:2 + 2 * n], out[-1]


def _sibling_finish(name, grads, lands, send_sem, recv_sem, after):
    n = len(grads)

    def body(*refs):
        ins, lz = refs[:n], refs[n:2 * n]
        send_ref, recv_ref = refs[2 * n], refs[2 * n + 1]
        x, y, c = _position()
        for a in range(n):
            r = grads[a].shape[0] // N_DEV
            for q in range(4):
                cp = pltpu.make_async_remote_copy(
                    src_ref=ins[a].at[pl.ds((2 * q + 1 - c) * r, r), :], dst_ref=lz[a].at[q], send_sem=send_ref.at[0],
                    recv_sem=recv_ref.at[0], device_id=(x, y, c), device_id_type=MESH)
                cp.wait_send()
                cp.wait_recv()

    out = _split_call(name, body, list(grads) + list(lands), [send_sem, recv_sem], after, [], False)
    return out[:n], out[n:2 * n]


def _chip_start(name, parts, after):
    n = len(parts)
    lands = [_landing((3,) + p.shape[1:], p.dtype) for p in parts]

    def body(*refs):
        ins, lz = refs[:n], refs[n:2 * n]
        send_sems, recv_sems = refs[2 * n + len(after)], refs[2 * n + len(after) + 1]
        token = refs[-1]
        x, y, c = _position()
        for a in range(n):
            for j, chip in enumerate(_gather_peers(x, y, c)[:3]):
                pltpu.make_async_remote_copy(
                    src_ref=ins[a].at[2 * chip[0] + chip[1]], dst_ref=lz[a].at[j], send_sem=send_sems.at[j],
                    recv_sem=recv_sems.at[j], device_id=chip, device_id_type=MESH).start()
        token[...] = jnp.zeros_like(token)

    out = _split_call(name, body, list(parts) + lands, [], after, [(3,), (3,)], True)
    return out[0], out[1], out[2:2 + n], out[2 + n:2 + 2 * n], out[-1]


def _chip_finish(name, parts, lands, send_sems, recv_sems, after):
    n = len(parts)

    def body(*refs):
        ins, lz = refs[:n], refs[n:2 * n]
        send_ref, recv_ref = refs[2 * n], refs[2 * n + 1]
        me = _position()
        for a in range(n):
            for j in range(3):
                cp = pltpu.make_async_remote_copy(
                    src_ref=ins[a].at[j], dst_ref=lz[a].at[j], send_sem=send_ref.at[j], recv_sem=recv_ref.at[j],
                    device_id=me, device_id_type=MESH)
                cp.wait_send()
                cp.wait_recv()

    out = _split_call(name, body, list(parts) + list(lands), [send_sems, recv_sems], after, [], False)
    return out[:n], out[n:2 * n]


def _other_devices(x, y, c):
    return [(x + (k >> 2 & 1) * (1 - 2 * x), y + (k >> 1 & 1) * (1 - 2 * y), c + (k & 1) * (1 - 2 * c))
            for k in range(1, N_DEV)]


def _broadcast_start(name, arrays, after):
    n = len(arrays)
    lands = [_landing((N_DEV,) + a.shape, a.dtype) for a in arrays]

    def body(*refs):
        ins, lz = refs[:n], refs[n:2 * n]
        send_sems, recv_sems = refs[2 * n + len(after)], refs[2 * n + len(after) + 1]
        token = refs[-1]
        x, y, c = _position()
        for a in range(n):
            for k, peer in enumerate(_other_devices(x, y, c)):
                pltpu.make_async_remote_copy(
                    src_ref=ins[a], dst_ref=lz[a].at[4 * x + 2 * y + c], send_sem=send_sems.at[k],
                    recv_sem=recv_sems.at[k], device_id=peer, device_id_type=MESH).start()
        token[...] = jnp.zeros_like(token)

    out = _split_call(name, body, list(arrays) + lands, [], after, [(N_DEV - 1,), (N_DEV - 1,)], True)
    return out[0], out[1], out[2:2 + n], out[2 + n:2 + 2 * n], out[-1]


def _broadcast_finish(name, arrays, lands, send_sems, recv_sems, after):
    n = len(arrays)

    def body(*refs):
        ins, lz = refs[:n], refs[n:2 * n]
        send_ref, recv_ref = refs[2 * n], refs[2 * n + 1]
        x, y, c = _position()
        for a in range(n):
            for k, peer in enumerate(_other_devices(x, y, c)):
                cp = pltpu.make_async_remote_copy(
                    src_ref=ins[a], dst_ref=lz[a].at[4 * peer[0] + 2 * peer[1] + peer[2]], send_sem=send_ref.at[k],
                    recv_sem=recv_ref.at[k], device_id=(x, y, c), device_id_type=MESH)
                cp.wait_send()
                cp.wait_recv()

    out = _split_call(name, body, list(arrays) + list(lands), [send_sems, recv_sems], after, [], False)
    return out[:n], out[n:2 * n]


def _row_tile(r, target):
    return max(t for t in range(16, min(r, target) + 1, 16) if r % t == 0)


CHIP_PARTIAL_BYTES = 12 * 1024 * 1024


def _chip_partial(name, grads, gots, c):
    n = len(grads)
    r = grads[0].shape[0] // N_DEV
    tr = _row_tile(r, CHIP_PARTIAL_BYTES // (n * 3 * D * 2))

    def body(c_ref, *refs):
        for a in range(n):
            refs[2 * n + a][...] = (refs[a][...].astype(F32) + refs[n + a][...].astype(F32)).astype(BF16)

    blk = pl.BlockSpec((None, tr, D), lambda q, i, c_ref: (q, i, 0))
    return pl.pallas_call(
        body, name=name,
        grid_spec=pltpu.PrefetchScalarGridSpec(
            num_scalar_prefetch=1, grid=(4, r // tr),
            in_specs=[pl.BlockSpec((None, None, tr, D), lambda q, i, c_ref: (q, c_ref[0], i, 0))] * n + [blk] * n,
            out_specs=[blk] * n),
        out_shape=[jax.ShapeDtypeStruct((4, r, D), BF16)] * n, compiler_params=_cparams(),
    )(c, *[g.reshape(4, 2, r, D) for g in grads], *gots)


class _WeightGather:
    def __init__(self, groups):
        self.groups = list(groups)
        self.index = {key: i for i, (key, _, _) in enumerate(groups)}
        self.state = [None] * len(groups)
        self.token = ()
        for i in range(min(2, len(groups))):
            self._start(i)

    def _tag(self, i):
        return "%s_%d" % self.groups[i][0][::-1]

    def _start(self, i):
        send, recv, lz, tok = _gather_start("gather_start_" + self._tag(i), self.groups[i][2], self.token)
        self.state[i] = dict(send=send, recv=recv, lands=lz)
        self.token = (tok,)

    def _step(self, name, near, far, fresh, marker):
        exists = lambda i: i is not None and i < len(self.groups)
        near, far, fresh = (i if exists(i) else None for i in (near, far, fresh))
        res = _gather_step(
            name, None if near is None else (self.state[near]["lands"], self.state[near]["recv"]),
            None if far is None else (self.state[far]["lands"], self.state[far]["rrecv"]),
            None if fresh is None else self.groups[fresh][2], tuple(marker) + self.token)
        self.token = (res["token"],)
        if near is not None:
            self.state[near].update(lands=res["near"], fsend=res["fsend"], frecv=res["frecv"], rsend=res["rsend"],
                                    rrecv=res["rrecv"])
        if far is not None:
            self.state[far].update(lands=res["far"], f2send=res["f2send"], f2recv=res["f2recv"])
        if fresh is not None:
            self.state[fresh] = dict(send=res["send"], recv=res["recv"], lands=res["fresh"])

    def fetch(self, layer, group, marker):
        k = self.index[(layer, group)]
        if k == 0:
            self._step("gather_step_first", 0, None, None, marker)
        self._step("gather_step_" + self._tag(k), k + 1, k, k + 2, marker)
        st = self.state[k]
        lz = _gather_finish("gather_finish_" + self._tag(k), st["lands"], st["send"], st["recv"], st["fsend"],
                            st["frecv"], st["rsend"], st["f2send"], st["f2recv"], self.token)
        self.state[k] = None
        return dict(zip(self.groups[k][1], lz))


class _GradReduce:
    def __init__(self, core, chip):
        self.core, self.chip = core, chip
        self.layer = None
        self.token = ()
        self.at_sibling, self.at_chips = [], []
        self.extra, self.smalls = (), {}

    def after(self):
        return self.token

    def add(self, group, names, grads):
        tag = "%s_%d" % (group, self.layer)
        send, recv, grads, lands, tok = _sibling_start("grad_sibling_start_" + tag, grads, self.token)
        self.at_sibling.append((tag, [(self.layer, n) for n in names], send, recv, grads, lands))
        self.token = (tok,)

    def advance(self, marker):
        for tag, keys, send, recv, grads, lands in self.at_sibling:
            grads, lands = _sibling_finish("grad_sibling_finish_" + tag, grads, lands, send, recv, marker)
            parts = _chip_partial("chip_partial_" + tag, grads, lands, self.core)
            send, recv, parts, lands, tok = _chip_start("grad_chip_start_" + tag, parts, ())
            self.at_chips.append([tag, keys, send, recv, parts, lands])
            self.token = (tok,)
        self.at_sibling = []

    def small(self, part, arrays):
        keys = list(arrays)
        send, recv, own, slots, tok = _broadcast_start(
            "small_grads_start_%d_%s" % (self.layer, part), [arrays[k] for k in keys], self.token)
        self.smalls.setdefault(self.layer, []).append((part, keys, send, recv, own, slots))
        self.token = (tok,)

    def small_finish(self, layer, marker):
        mine, theirs = {}, {}
        for part, keys, send, recv, own, slots in self.smalls[layer]:
            own, slots = _broadcast_finish("small_grads_finish_%d_%s" % (layer, part), own, slots, send, recv, marker)
            mine.update(zip(keys, own))
            theirs.update(zip(keys, slots))
        return mine, theirs

    def collect(self, key, marker):
        for entry in self.at_chips:
            tag, keys, send, recv, parts, lands = entry
            if key in keys:
                if send is not None:
                    parts, lands = _chip_finish("grad_chip_finish_" + tag, parts, lands, send, recv, marker)
                    entry[2:] = [None, None, parts, lands]
                i = keys.index(key)
                return parts[i], lands[i]
        raise KeyError(key)


def _adamw_math(w, g, m, v):
    m = ADAM_B1 * m + (1.0 - ADAM_B1) * g
    v = ADAM_B2 * v + (1.0 - ADAM_B2) * jnp.square(g)
    m_hat = m / (1.0 - ADAM_B1 ** ADAM_STEP)
    v_hat = v / (1.0 - ADAM_B2 ** ADAM_STEP)
    delta = -ADAM_LR * (m_hat / (jnp.sqrt(v_hat) + ADAM_EPS) + ADAM_WD * w)
    return delta, m, v


def _adamw_small(wts, mom_m, mom_v, own, gathered, loss_own, loss_gathered, dev):
    names = SMALL
    nw = len(names)
    na = len(SMALL_ARRAYS)

    def body(dev_ref, *refs):
        w_refs, m_refs, v_refs = (dict(zip(names, refs[i * nw:(i + 1) * nw])) for i in range(3))
        own_refs = refs[3 * nw:3 * nw + DEPTH * na]
        g_refs = refs[3 * nw + DEPTH * na:3 * nw + 2 * DEPTH * na]
        loss_own_ref, loss_got_ref = refs[3 * nw + 2 * DEPTH * na:3 * nw + 2 * DEPTH * na + 2]
        outs = refs[3 * nw + 2 * DEPTH * na + 2:]
        g_out, d_out, m_out, v_out = (dict(zip(names, outs[i * nw:(i + 1) * nw])) for i in range(4))
        me = dev_ref[0]

        loss = None
        for d in range(N_DEV):
            for b in range(loss_own.shape[0]):
                term = jnp.where(me == d, loss_own_ref[b], loss_got_ref[d, b])
                loss = term if loss is None else loss + term
        outs[4 * nw][...] = loss

        def update(name, at, g):
            g_out[name][at] = g
            d_out[name][at], m_out[name][at], v_out[name][at] = _adamw_math(
                w_refs[name][at], g, m_refs[name][at], v_refs[name][at])

        for l in range(DEPTH):
            mine = dict(zip(SMALL_ARRAYS, own_refs[l * na:(l + 1) * na]))
            got = dict(zip(SMALL_ARRAYS, g_refs[l * na:(l + 1) * na]))

            def total(key, at):
                acc = None
                for d in range(N_DEV):
                    term = jnp.where(me == d, mine[key][at] if at else mine[key][...], got[key][(d,) + at])
                    acc = term if acc is None else acc + term
                return acc

            row = (slice(l, l + 1),)
            for k, name in enumerate(NORM_NAMES):
                update(name, row, total("norms", (slice(k, k + 1),)))
            for k, name in enumerate(VEC_NAMES):
                update(name, row, total("vecs", (slice(k, k + 1),)))
            update("gmlp_v_gain", (l,), total("gain_bias", (slice(0, NH),)))
            update("b_spatial", (l,), total("gain_bias", (slice(NH, 2 * NH),)))
            update("w_spatial", (l,), total("w_spatial", ()))
            update("w_pool", (l,), total("w_pool", ()))
            update("w_dw", (l,), total("w_dw", (slice(0, CONV_K),)))

    args = [src[n] for src in (wts, mom_m, mom_v) for n in names]
    args += [src[l][k] for src in (own, gathered) for l in range(DEPTH) for k in SMALL_ARRAYS]
    args += [loss_own, loss_gathered]
    outs = pl.pallas_call(
        body, name="adamw_small",
        in_specs=[pl.BlockSpec(memory_space=pltpu.SMEM)] + [pl.BlockSpec(memory_space=pltpu.VMEM)] * len(args),
        out_shape=[jax.ShapeDtypeStruct(wts[n].shape, F32) for _ in range(4) for n in names]
        + [jax.ShapeDtypeStruct((8, LANES), F32)],
        compiler_params=_cparams(),
    )(dev, *args)
    return tuple(dict(zip(names, outs[i * nw:(i + 1) * nw])) for i in range(4)) + (outs[4 * nw],)


def _adamw_layers(name, w, reduced, m, v, chip, tr, transposed=False, after=()):
    nl, r, cdim = w.shape
    tr = _row_tile(r, tr)
    nb = r // tr

    def body(q_ref, w_ref, p0_ref, g0_ref, p1_ref, g1_ref, m_ref, v_ref, *rest):
        g_ref, d_ref, nm_ref, nv_ref = rest[len(after):]

        def total(p_ref, got_ref):
            acc = p_ref[...].astype(F32)
            for j in range(3):
                acc = acc + got_ref[j].astype(F32)
            return acc

        g = jnp.where(pl.program_id(0) == 0, total(p0_ref, g0_ref), total(p1_ref, g1_ref))
        if transposed:
            g = g.T
        g_ref[...] = g
        d_ref[...], nm_ref[...], nv_ref[...] = _adamw_math(w_ref[...], g, m_ref[...], v_ref[...])

    blk = pl.BlockSpec((None, tr, cdim), lambda l, i, q: (l, i, 0))
    first = lambda l, i: i * (1 - l) + (nb - 1) * l
    second = lambda l, i: i * l
    if transposed:
        gshape = (cdim, tr)
        at = lambda lead, i: (lead, 0, i)
    else:
        gshape = (tr, cdim)
        at = lambda lead, i: (lead, i, 0)
    specs = [blk,
             pl.BlockSpec((None,) + gshape, lambda l, i, q: at(q[0], first(l, i))),
             pl.BlockSpec((3,) + gshape, lambda l, i, q: at(0, first(l, i))),
             pl.BlockSpec((None,) + gshape, lambda l, i, q: at(q[0], second(l, i))),
             pl.BlockSpec((3,) + gshape, lambda l, i, q: at(0, second(l, i))), blk, blk] + [ANY] * len(after)
    shape = jax.ShapeDtypeStruct((nl, r, cdim), F32)
    return pl.pallas_call(
        body, name=name,
        grid_spec=pltpu.PrefetchScalarGridSpec(num_scalar_prefetch=1, grid=(nl, nb), in_specs=specs, out_specs=[blk] * 4),
        out_shape=[shape] * 4, compiler_params=_cparams(),
    )(chip, w, *reduced[0], *reduced[1], m, v, *after)


def _to_rows(name, a):
    return jnp.swapaxes(a, 1, 2) if name == "w_in" else a


def _place_own_transposed(name, srcs, dev, out_dtype, tc):
    n = len(srcs)
    kdim, cdim = srcs[0][0].shape[-2:]

    def body(dev_ref, *refs):
        for a in range(n):
            refs[n + a][...] = refs[a][...].T.astype(out_dtype)

    return pl.pallas_call(
        body, name=name,
        grid_spec=pltpu.PrefetchScalarGridSpec(
            num_scalar_prefetch=1, grid=(kdim // tc,),
            in_specs=[pl.BlockSpec((None, tc, cdim), lambda i, d, l=l: (l, i, 0)) for _, l in srcs],
            out_specs=[pl.BlockSpec((cdim, tc), lambda i, d: (d[0], i))] * n),
        out_shape=[jax.ShapeDtypeStruct((N_DEV * cdim, kdim), out_dtype)] * n, compiler_params=_cparams(),
    )(dev, *[a for a, _ in srcs])


def _pack(arrays, rows):
    flat = jnp.concatenate([a.reshape(-1) for a in arrays])
    return jnp.pad(flat, (0, rows * D - flat.shape[0])).reshape(rows, D)


def _rows_for(shapes, mult=8):
    total = 0
    for shp in shapes:
        size = 1
        for dim in shp:
            size *= dim
        total += size
    return -(-total // (mult * D)) * mult


def kernel(x, mem, norm_mix_pre, norm_mix_post, w_in, w_out, gmlp_v_gain, w_spatial, b_spatial, w_pool, s_pool, w_dw, b_dw, conv_ln_g, conv_ln_b, norm_xattn_pre, norm_mem, norm_xattn_post, w_q, w_k, w_v, w_o, norm_ffn_pre, norm_ffn_post, w_up, w_down, loss_target, m_norm_mix_pre, m_norm_mix_post, m_w_in, m_w_out, m_gmlp_v_gain, m_w_spatial, m_b_spatial, m_w_pool, m_s_pool, m_w_dw, m_b_dw, m_conv_ln_g, m_conv_ln_b, m_norm_xattn_pre, m_norm_mem, m_norm_xattn_post, m_w_q, m_w_k, m_w_v, m_w_o, m_norm_ffn_pre, m_norm_ffn_post, m_w_up, m_w_down, v_norm_mix_pre, v_norm_mix_post, v_w_in, v_w_out, v_gmlp_v_gain, v_w_spatial, v_b_spatial, v_w_pool, v_s_pool, v_w_dw, v_b_dw, v_conv_ln_g, v_conv_ln_b, v_norm_xattn_pre, v_norm_mem, v_norm_xattn_post, v_w_q, v_w_k, v_w_v, v_w_o, v_norm_ffn_pre, v_norm_ffn_post, v_w_up, v_w_down):
    args = dict(locals())
    wts = {n: args[n] for n in WEIGHTS}
    mom_m = {n: args["m_" + n] for n in WEIGHTS}
    mom_v = {n: args["v_" + n] for n in WEIGHTS}
    xi, yi, ci = _position()
    me = 4 * xi + 2 * yi + ci

    dev = jnp.reshape(me, (1,)).astype(jnp.int32)
    lands = {}
    for call, names, tr in (("place_in", ("w_in",), 256), ("place_att", ("w_out", "w_q", "w_k", "w_v", "w_o"), 64),
                            ("place_up", ("w_up",), 256), ("place_down", ("w_down",), 256)):
        srcs = [(_to_rows(n, wts[n]), l) for l in range(DEPTH) for n in names]
        placed = (_place_own_transposed if names == ("w_up",) else _place_own)(call, srcs, dev, BF16, tr)
        lands.update(zip([(l, n) for l in range(DEPTH) for n in names], placed))
    (lands[(0, "taps")],) = _place_own("place_taps", [(_pack([w_dw], _rows_for([w_dw.shape])), None)], dev, F32, 8)
    groups = []
    for l in range(DEPTH):
        for group, names in GATHER_GROUPS:
            if (l, group) == (0, "in"):
                names = names + ("taps",)
            groups.append(((l, group), names, [lands[(l, n)] for n in names]))
    gather = _WeightGather(groups)

    def fetch(layer, group, marker):
        w = gather.fetch(layer, group, marker)
        if "taps" in w:
            blocks = w["taps"].reshape(N_DEV, -1)[:, :w_dw.size].reshape((N_DEV,) + w_dw.shape)
            w["taps"] = jnp.moveaxis(blocks, 0, 2).reshape(DEPTH, CONV_K, CW)
        return w

    reduce = _GradReduce(jnp.reshape(ci, (1,)).astype(jnp.int32), jnp.reshape(2 * xi + yi, (1,)).astype(jnp.int32))
    small = {n: wts[n] for n in SMALL if n != "w_dw"}
    _, dx = _local_step(x[0], mem[0], loss_target[0], fetch, small, reduce)
    reduce.advance((dx,))

    grad_w, delta, new_m, new_v = {}, {}, {}, {}
    marker = (dx,) + tuple(reduce.after())
    for n in UPDATE_ORDER:
        reduced = [reduce.collect((l, n), marker) for l in range(DEPTH)]
        outs = _adamw_layers("adamw_" + n, _to_rows(n, wts[n]), reduced, _to_rows(n, mom_m[n]), _to_rows(n, mom_v[n]),
                             reduce.chip, 256, transposed=n == "w_up", after=marker)
        grad_w[n], delta[n], new_m[n], new_v[n] = (_to_rows(n, o) for o in outs)
        marker = (outs[1],)

    own, slots = [None] * DEPTH, [None] * DEPTH
    for l in reversed(range(DEPTH)):
        own[l], slots[l] = reduce.small_finish(l, marker)
        if l == 0:
            loss_own, loss_slots = own[l].pop("loss"), slots[l].pop("loss")
    shard_cols = CW // N_DEV
    for l in range(DEPTH):
        own[l]["w_dw"] = lax.dynamic_slice_in_dim(own[l]["w_dw"], me * shard_cols, shard_cols, axis=1)
        slots[l]["w_dw"] = lax.dynamic_slice_in_dim(slots[l]["w_dw"], me * shard_cols, shard_cols, axis=2)
    *small_out, loss_tile = _adamw_small(wts, mom_m, mom_v, own, slots, loss_own, loss_slots, dev)
    for dst, src in zip((grad_w, delta, new_m, new_v), small_out):
        dst.update(src)

    return (loss_tile[0, 0], dx[None], *[grad_w[n] for n in WEIGHTS], *[delta[n] for n in WEIGHTS],
            *[new_m[n] for n in WEIGHTS], *[new_v[n] for n in WEIGHTS])
```
